```python
import jax, jax.numpy as jnp
from jax import lax
import numpy as np

D_MODEL = 1024
BATCH = 16
SEQ = 4096
DEPTH = 2

EXPAND = 2
D_INNER = EXPAND * D_MODEL
EPS = 1e-6

CONV_WIDTH = D_INNER // 2
ATTN_WIDTH = D_INNER - CONV_WIDTH
SB_HEAD_DIM = 128
SB_HEADS = ATTN_WIDTH // SB_HEAD_DIM
CONF_KERNEL = 31
Q_BLOCK = 128
EVEN_SPLITS = [CONV_WIDTH, 2 * CONV_WIDTH, 3 * CONV_WIDTH,
               3 * CONV_WIDTH + ATTN_WIDTH, 3 * CONV_WIDTH + 2 * ATTN_WIDTH,
               3 * CONV_WIDTH + 3 * ATTN_WIDTH]
IN_EVEN = 3 * CONV_WIDTH + 4 * ATTN_WIDTH

SSM_HEAD_DIM = 64
SSM_HEADS = D_INNER // SSM_HEAD_DIM
SSM_GROUPS = 4
SSM_STATE = 128
SSM_CONV = 4
SSM_CHUNK = 128
XBC_WIDTH = D_INNER + 2 * SSM_GROUPS * SSM_STATE
IN_ODD = D_INNER + XBC_WIDTH + SSM_HEADS
DT_MIN = 0.001
DT_MAX = 0.1

N_EVEN = (DEPTH + 1) // 2
N_ODD = DEPTH // 2

kernel_name = "hybrid_conformer_stickbreak_ssd"


def rmsnorm(x, w):
    xf = x.astype(jnp.float32)
    y = xf * lax.rsqrt(jnp.mean(xf * xf, axis=-1, keepdims=True) + EPS)
    return (y * w.astype(jnp.float32)).astype(x.dtype)


def layernorm(x, w, b):
    xf = x.astype(jnp.float32)
    mu = jnp.mean(xf, axis=-1, keepdims=True)
    xc = xf - mu
    var = jnp.mean(xc * xc, axis=-1, keepdims=True)
    return xc * lax.rsqrt(var + EPS) * w.astype(jnp.float32) + b.astype(jnp.float32)


def causal_dwconv(x, w, b):
    k_width = w.shape[0]
    out = lax.conv_general_dilated(
        x, w[:, None, :].astype(x.dtype), window_strides=(1,),
        padding=((k_width - 1, 0),), dimension_numbers=('NWC', 'WIO', 'NWC'),
        feature_group_count=x.shape[-1])
    return out + b.astype(x.dtype)


def stick_breaking_attention(q, k, v):
    S = q.shape[1]
    dh = q.shape[-1]
    qf = q.astype(jnp.float32) * (dh ** -0.5)
    kf = k.astype(jnp.float32)
    vf = v.astype(jnp.float32)
    outs = []
    for i in range(S // Q_BLOCK):
        t0 = i * Q_BLOCK
        kl = t0 + Q_BLOCK
        z = jnp.einsum('bthd,bshd->bhts', qf[:, t0:kl], kf[:, :kl])
        t_idx = t0 + jnp.arange(Q_BLOCK)
        s_idx = jnp.arange(kl)
        mask = s_idx[None, :] < t_idx[:, None]
        log_keep = jnp.where(mask, jax.nn.log_sigmoid(-z), 0.0)
        later = lax.cumsum(log_keep, axis=3, reverse=True) - log_keep
        wts = jnp.where(mask, jnp.exp(jax.nn.log_sigmoid(z) + later), 0.0)
        outs.append(jnp.einsum('bhts,bshd->bthd', wts, vf[:, :kl]))
    return jnp.concatenate(outs, axis=1).astype(q.dtype)


def ssd_scan(x, dt, a, bm, cm):
    bsz, S, H, P = x.shape
    G, N = bm.shape[2], bm.shape[3]
    R = H // G
    L = SSM_CHUNK
    nc = S // L
    xs = (x * dt[..., None]).reshape(bsz, nc, L, G, R, P).transpose(1, 0, 2, 3, 4, 5)
    la = (dt * a).reshape(bsz, nc, L, G, R).transpose(1, 0, 3, 4, 2)
    bc = bm.reshape(bsz, nc, L, G, N).transpose(1, 0, 2, 3, 4)
    cc = cm.reshape(bsz, nc, L, G, N).transpose(1, 0, 2, 3, 4)
    causal = jnp.tril(jnp.ones((L, L), dtype=bool))

    def step(state, inp):
        xck, ack, bck, cck = inp
        cs = jnp.cumsum(ack, axis=-1)
        seg = cs[..., :, None] - cs[..., None, :]
        decay = jnp.exp(jnp.where(causal, seg, -jnp.inf))
        cb = jnp.einsum('blgn,bsgn->bgls', cck, bck)
        y = jnp.einsum('bgls,bgrls,bsgrp->blgrp', cb, decay, xck)
        y = y + jnp.einsum('blgn,bgrpn,bgrl->blgrp', cck, state, jnp.exp(cs))
        tail = jnp.exp(cs[..., -1:] - cs)
        new_state = (state * jnp.exp(cs[..., -1])[..., None, None]
                     + jnp.einsum('bsgn,bgrs,bsgrp->bgrpn', bck, tail, xck))
        return new_state, y

    init = jnp.zeros((bsz, G, R, P, N), jnp.float32)
    _, ys = lax.scan(step, init, (xs, la, bc, cc))
    return ys.transpose(1, 0, 2, 3, 4, 5).reshape(bsz, S, H, P)


def conv_attn_mixer(h, w_in, dw_w, dw_b, ln_w, ln_b, w_out):
    bsz, S, _ = h.shape
    proj = h @ w_in
    glu_a, glu_b, gate_c, q, k, v, gate_a = jnp.split(proj, EVEN_SPLITS, axis=-1)
    u = glu_a * jax.nn.sigmoid(glu_b)
    u = causal_dwconv(u, dw_w, dw_b)
    u = jax.nn.silu(layernorm(u, ln_w, ln_b))
    y_conv = (u * jax.nn.silu(gate_c.astype(jnp.float32))).astype(h.dtype)
    shp = (bsz, S, SB_HEADS, SB_HEAD_DIM)
    o = stick_breaking_attention(q.reshape(shp), k.reshape(shp), v.reshape(shp))
    y_attn = (o.reshape(bsz, S, ATTN_WIDTH).astype(jnp.float32)
              * jax.nn.silu(gate_a.astype(jnp.float32))).astype(h.dtype)
    return jnp.concatenate([y_conv, y_attn], axis=-1) @ w_out


def mamba2_mixer(h, w_in, conv_w, conv_b, dt_bias, a_log, d_skip, norm_w, w_out):
    bsz, S, _ = h.shape
    gn = SSM_GROUPS * SSM_STATE
    proj = h @ w_in
    z, xbc, dt = jnp.split(proj, [D_INNER, D_INNER + XBC_WIDTH], axis=-1)
    xbc = jax.nn.silu(causal_dwconv(xbc, conv_w, conv_b))
    xs, bm, cm = jnp.split(xbc, [D_INNER, D_INNER + gn], axis=-1)
    xs = xs.astype(jnp.float32).reshape(bsz, S, SSM_HEADS, SSM_HEAD_DIM)
    bm = bm.astype(jnp.float32).reshape(bsz, S, SSM_GROUPS, SSM_STATE)
    cm = cm.astype(jnp.float32).reshape(bsz, S, SSM_GROUPS, SSM_STATE)
    dt = jax.nn.softplus(dt.astype(jnp.float32) + dt_bias.astype(jnp.float32))
    a = -jnp.exp(a_log.astype(jnp.float32))
    y = ssd_scan(xs, dt, a, bm, cm)
    y = y + d_skip.astype(jnp.float32)[:, None] * xs
    y = y.reshape(bsz, S, D_INNER) * jax.nn.silu(z.astype(jnp.float32))
    yg = y.reshape(bsz, S, SSM_GROUPS, D_INNER // SSM_GROUPS)
    yg = yg * lax.rsqrt(jnp.mean(yg * yg, axis=-1, keepdims=True) + EPS)
    y = yg.reshape(bsz, S, D_INNER) * norm_w.astype(jnp.float32)
    return y.astype(h.dtype) @ w_out


def _fwd_setup_inputs(seed: int = 0) -> dict:
    key = jax.random.key(seed)
    ks = jax.random.split(key, 20)
    f32 = jnp.float32
    nrm = lambda k, shp, s: jax.random.normal(k, shp, f32) * s
    x = jax.random.normal(ks[0], (BATCH, SEQ, D_MODEL), f32)
    ev_norm_w = 1.0 + nrm(ks[1], (N_EVEN, D_MODEL), 0.02)
    ev_w_in = nrm(ks[2], (N_EVEN, D_MODEL, IN_EVEN), D_MODEL ** -0.5)
    ev_dw_w = nrm(ks[3], (N_EVEN, CONF_KERNEL, CONV_WIDTH), CONF_KERNEL ** -0.5)
    ev_dw_b = nrm(ks[4], (N_EVEN, CONV_WIDTH), 0.02)
    ev_ln_w = 1.0 + nrm(ks[5], (N_EVEN, CONV_WIDTH), 0.02)
    ev_ln_b = nrm(ks[6], (N_EVEN, CONV_WIDTH), 0.02)
    ev_w_out = nrm(ks[7], (N_EVEN, D_INNER, D_MODEL), D_INNER ** -0.5)
    od_norm_w = 1.0 + nrm(ks[8], (N_ODD, D_MODEL), 0.02)
    od_w_in = nrm(ks[9], (N_ODD, D_MODEL, IN_ODD), D_MODEL ** -0.5)
    od_conv_w = nrm(ks[10], (N_ODD, SSM_CONV, XBC_WIDTH), SSM_CONV ** -0.5)
    od_conv_b = nrm(ks[11], (N_ODD, XBC_WIDTH), 0.02)
    u = jax.random.uniform(ks[12], (N_ODD, SSM_HEADS), f32)
    dt0 = jnp.exp(u * (np.log(DT_MAX) - np.log(DT_MIN)) + np.log(DT_MIN))
    od_dt_bias = dt0 + jnp.log(-jnp.expm1(-dt0))
    od_a_log = jnp.log(jax.random.uniform(ks[13], (N_ODD, SSM_HEADS), f32, 1.0, 16.0))
    od_d = 1.0 + nrm(ks[14], (N_ODD, SSM_HEADS), 0.1)
    od_gnorm_w = 1.0 + nrm(ks[15], (N_ODD, D_INNER), 0.02)
    od_w_out = nrm(ks[16], (N_ODD, D_INNER, D_MODEL), D_INNER ** -0.5)
    final_norm_w = 1.0 + nrm(ks[17], (D_MODEL,), 0.02)
    return {"x": x, "ev_norm_w": ev_norm_w, "ev_w_in": ev_w_in, "ev_dw_w": ev_dw_w,
            "ev_dw_b": ev_dw_b, "ev_ln_w": ev_ln_w, "ev_ln_b": ev_ln_b, "ev_w_out": ev_w_out,
            "od_norm_w": od_norm_w, "od_w_in": od_w_in, "od_conv_w": od_conv_w,
            "od_conv_b": od_conv_b, "od_dt_bias": od_dt_bias, "od_a_log": od_a_log,
            "od_d": od_d, "od_gnorm_w": od_gnorm_w, "od_w_out": od_w_out,
            "final_norm_w": final_norm_w}


def _fwd_reference(x, ev_norm_w, ev_w_in, ev_dw_w, ev_dw_b, ev_ln_w, ev_ln_b, ev_w_out,
              od_norm_w, od_w_in, od_conv_w, od_conv_b, od_dt_bias, od_a_log, od_d,
              od_gnorm_w, od_w_out, final_norm_w):
    h = x
    for layer in range(DEPTH):
        i = layer // 2
        if layer % 2 == 0:
            h = h + conv_attn_mixer(rmsnorm(h, ev_norm_w[i]), ev_w_in[i], ev_dw_w[i],
                                    ev_dw_b[i], ev_ln_w[i], ev_ln_b[i], ev_w_out[i])
        else:
            h = h + mamba2_mixer(rmsnorm(h, od_norm_w[i]), od_w_in[i], od_conv_w[i],
                                 od_conv_b[i], od_dt_bias[i], od_a_log[i], od_d[i],
                                 od_gnorm_w[i], od_w_out[i])
    return rmsnorm(h, final_norm_w)


import jax as _jax
import jax.numpy as _jnp

TWIN_FORMAT = 'train_step'
FWD_PARAMS = ['x', 'ev_norm_w', 'ev_w_in', 'ev_dw_w', 'ev_dw_b', 'ev_ln_w', 'ev_ln_b', 'ev_w_out', 'od_norm_w', 'od_w_in', 'od_conv_w', 'od_conv_b', 'od_dt_bias', 'od_a_log', 'od_d', 'od_gnorm_w', 'od_w_out', 'final_norm_w']
TWIN_WEIGHTS = ['ev_norm_w', 'ev_w_in', 'ev_dw_w', 'ev_dw_b', 'ev_ln_w', 'ev_ln_b', 'ev_w_out', 'od_norm_w', 'od_w_in', 'od_conv_w', 'od_conv_b', 'od_dt_bias', 'od_a_log', 'od_d', 'od_gnorm_w', 'od_w_out', 'final_norm_w']
TWIN_DIFF_INPUT = 'x'
TWIN_INPUTS = ['x', 'ev_norm_w', 'ev_w_in', 'ev_dw_w', 'ev_dw_b', 'ev_ln_w', 'ev_ln_b', 'ev_w_out', 'od_norm_w', 'od_w_in', 'od_conv_w', 'od_conv_b', 'od_dt_bias', 'od_a_log', 'od_d', 'od_gnorm_w', 'od_w_out', 'final_norm_w', 'loss_target', 'm_ev_norm_w', 'm_ev_w_in', 'm_ev_dw_w', 'm_ev_dw_b', 'm_ev_ln_w', 'm_ev_ln_b', 'm_ev_w_out', 'm_od_norm_w', 'm_od_w_in', 'm_od_conv_w', 'm_od_conv_b', 'm_od_dt_bias', 'm_od_a_log', 'm_od_d', 'm_od_gnorm_w', 'm_od_w_out', 'm_final_norm_w', 'v_ev_norm_w', 'v_ev_w_in', 'v_ev_dw_w', 'v_ev_dw_b', 'v_ev_ln_w', 'v_ev_ln_b', 'v_ev_w_out', 'v_od_norm_w', 'v_od_w_in', 'v_od_conv_w', 'v_od_conv_b', 'v_od_dt_bias', 'v_od_a_log', 'v_od_d', 'v_od_gnorm_w', 'v_od_w_out', 'v_final_norm_w']
TWIN_OUTPUTS = ['loss', 'grad_x', 'grad_ev_norm_w', 'grad_ev_w_in', 'grad_ev_dw_w', 'grad_ev_dw_b', 'grad_ev_ln_w', 'grad_ev_ln_b', 'grad_ev_w_out', 'grad_od_norm_w', 'grad_od_w_in', 'grad_od_conv_w', 'grad_od_conv_b', 'grad_od_dt_bias', 'grad_od_a_log', 'grad_od_d', 'grad_od_gnorm_w', 'grad_od_w_out', 'grad_final_norm_w', 'delta_ev_norm_w', 'delta_ev_w_in', 'delta_ev_dw_w', 'delta_ev_dw_b', 'delta_ev_ln_w', 'delta_ev_ln_b', 'delta_ev_w_out', 'delta_od_norm_w', 'delta_od_w_in', 'delta_od_conv_w', 'delta_od_conv_b', 'delta_od_dt_bias', 'delta_od_a_log', 'delta_od_d', 'delta_od_gnorm_w', 'delta_od_w_out', 'delta_final_norm_w', 'new_m_ev_norm_w', 'new_m_ev_w_in', 'new_m_ev_dw_w', 'new_m_ev_dw_b', 'new_m_ev_ln_w', 'new_m_ev_ln_b', 'new_m_ev_w_out', 'new_m_od_norm_w', 'new_m_od_w_in', 'new_m_od_conv_w', 'new_m_od_conv_b', 'new_m_od_dt_bias', 'new_m_od_a_log', 'new_m_od_d', 'new_m_od_gnorm_w', 'new_m_od_w_out', 'new_m_final_norm_w', 'new_v_ev_norm_w', 'new_v_ev_w_in', 'new_v_ev_dw_w', 'new_v_ev_dw_b', 'new_v_ev_ln_w', 'new_v_ev_ln_b', 'new_v_ev_w_out', 'new_v_od_norm_w', 'new_v_od_w_in', 'new_v_od_conv_w', 'new_v_od_conv_b', 'new_v_od_dt_bias', 'new_v_od_a_log', 'new_v_od_d', 'new_v_od_gnorm_w', 'new_v_od_w_out', 'new_v_final_norm_w']
TWIN_LEAF_KINDS = {'loss': 'loss', 'grad_x': 'grad_x', 'grad_ev_norm_w': 'grad_w', 'grad_ev_w_in': 'grad_w', 'grad_ev_dw_w': 'grad_w', 'grad_ev_dw_b': 'grad_w', 'grad_ev_ln_w': 'grad_w', 'grad_ev_ln_b': 'grad_w', 'grad_ev_w_out': 'grad_w', 'grad_od_norm_w': 'grad_w', 'grad_od_w_in': 'grad_w', 'grad_od_conv_w': 'grad_w', 'grad_od_conv_b': 'grad_w', 'grad_od_dt_bias': 'grad_w', 'grad_od_a_log': 'grad_w', 'grad_od_d': 'grad_w', 'grad_od_gnorm_w': 'grad_w', 'grad_od_w_out': 'grad_w', 'grad_final_norm_w': 'grad_w', 'delta_ev_norm_w': 'delta_w', 'delta_ev_w_in': 'delta_w', 'delta_ev_dw_w': 'delta_w', 'delta_ev_dw_b': 'delta_w', 'delta_ev_ln_w': 'delta_w', 'delta_ev_ln_b': 'delta_w', 'delta_ev_w_out': 'delta_w', 'delta_od_norm_w': 'delta_w', 'delta_od_w_in': 'delta_w', 'delta_od_conv_w': 'delta_w', 'delta_od_conv_b': 'delta_w', 'delta_od_dt_bias': 'delta_w', 'delta_od_a_log': 'delta_w', 'delta_od_d': 'delta_w', 'delta_od_gnorm_w': 'delta_w', 'delta_od_w_out': 'delta_w', 'delta_final_norm_w': 'delta_w', 'new_m_ev_norm_w': 'new_m', 'new_m_ev_w_in': 'new_m', 'new_m_ev_dw_w': 'new_m', 'new_m_ev_dw_b': 'new_m', 'new_m_ev_ln_w': 'new_m', 'new_m_ev_ln_b': 'new_m', 'new_m_ev_w_out': 'new_m', 'new_m_od_norm_w': 'new_m', 'new_m_od_w_in': 'new_m', 'new_m_od_conv_w': 'new_m', 'new_m_od_conv_b': 'new_m', 'new_m_od_dt_bias': 'new_m', 'new_m_od_a_log': 'new_m', 'new_m_od_d': 'new_m', 'new_m_od_gnorm_w': 'new_m', 'new_m_od_w_out': 'new_m', 'new_m_final_norm_w': 'new_m', 'new_v_ev_norm_w': 'new_v', 'new_v_ev_w_in': 'new_v', 'new_v_ev_dw_w': 'new_v', 'new_v_ev_dw_b': 'new_v', 'new_v_ev_ln_w': 'new_v', 'new_v_ev_ln_b': 'new_v', 'new_v_ev_w_out': 'new_v', 'new_v_od_norm_w': 'new_v', 'new_v_od_w_in': 'new_v', 'new_v_od_conv_w': 'new_v', 'new_v_od_conv_b': 'new_v', 'new_v_od_dt_bias': 'new_v', 'new_v_od_a_log': 'new_v', 'new_v_od_d': 'new_v', 'new_v_od_gnorm_w': 'new_v', 'new_v_od_w_out': 'new_v', 'new_v_final_norm_w': 'new_v'}


def _forward(args):
    return _fwd_reference(*[args[k] for k in FWD_PARAMS])


def _output_shape():
    out = _jax.eval_shape(lambda: _forward(_fwd_setup_inputs(0)))
    return out.shape, out.dtype

N_MICROBATCH = 1
ADAM_LR = 0.001
ADAM_B1 = 0.9
ADAM_B2 = 0.999
ADAM_EPS = 1e-08
ADAM_WD = 0.01
ADAM_STEP = 10
PER_EXAMPLE_BATCH_AXIS = {'x': 0, 'loss_target': 0}
SHARED_INPUTS = []
_WEIGHT_DTYPES = {'ev_norm_w': _jnp.float32, 'ev_w_in': _jnp.float32, 'ev_dw_w': _jnp.float32, 'ev_dw_b': _jnp.float32, 'ev_ln_w': _jnp.float32, 'ev_ln_b': _jnp.float32, 'ev_w_out': _jnp.float32, 'od_norm_w': _jnp.float32, 'od_w_in': _jnp.float32, 'od_conv_w': _jnp.float32, 'od_conv_b': _jnp.float32, 'od_dt_bias': _jnp.float32, 'od_a_log': _jnp.float32, 'od_d': _jnp.float32, 'od_gnorm_w': _jnp.float32, 'od_w_out': _jnp.float32, 'final_norm_w': _jnp.float32}
MOMENT_SCALE = {'ev_norm_w': 1.771002e-01, 'ev_w_in': 6.418269e-02, 'ev_dw_w': 7.767465e-02, 'ev_dw_b': 1.655457e-01, 'ev_ln_w': 9.691046e-02, 'ev_ln_b': 8.169254e-02, 'ev_w_out': 1.098150e-01, 'od_norm_w': 2.575162e-01, 'od_w_in': 1.147242e-01, 'od_conv_w': 1.055670e-01, 'od_conv_b': 1.397038e-01, 'od_dt_bias': 2.623685e-01, 'od_a_log': 7.575561e-01, 'od_d': 8.126851e-01, 'od_gnorm_w': 1.254135e-01, 'od_w_out': 1.720091e-01, 'final_norm_w': 6.395402e+01}


def _to_microbatches(a, axis):
    t = _jnp.moveaxis(a, axis, 0)
    t = t.reshape((N_MICROBATCH, t.shape[0] // N_MICROBATCH) + t.shape[1:])
    return _jnp.moveaxis(t, 1, axis + 1)


def setup_inputs(seed: int = 0) -> dict:
    inp = _fwd_setup_inputs(seed)
    key = _jax.random.fold_in(_jax.random.key(seed), 7919)
    shape, _ = _output_shape()
    out = dict(inp)
    out["loss_target"] = _jax.random.normal(_jax.random.fold_in(key, 0), shape, _jnp.float32)
    for i, name in enumerate(TWIN_WEIGHTS):
        w = inp[name].astype(_jnp.float32)
        if MOMENT_SCALE is None:
            s = _jnp.sqrt(_jnp.mean(_jnp.square(w)) + 1e-30)
        else:
            s = MOMENT_SCALE[name]
        km, kv = _jax.random.split(_jax.random.fold_in(key, i + 1))
        out[name] = w
        out["m_" + name] = s * _jax.random.normal(km, w.shape, _jnp.float32)
        out["v_" + name] = (s * s) * _jax.random.uniform(kv, w.shape, _jnp.float32, 0.5, 1.5)
    if N_MICROBATCH > 1:
        for name, axis in PER_EXAMPLE_BATCH_AXIS.items():
            out[name] = _to_microbatches(out[name], axis)
    return {'x': out['x'], 'ev_norm_w': out['ev_norm_w'], 'ev_w_in': out['ev_w_in'], 'ev_dw_w': out['ev_dw_w'], 'ev_dw_b': out['ev_dw_b'], 'ev_ln_w': out['ev_ln_w'], 'ev_ln_b': out['ev_ln_b'], 'ev_w_out': out['ev_w_out'], 'od_norm_w': out['od_norm_w'], 'od_w_in': out['od_w_in'], 'od_conv_w': out['od_conv_w'], 'od_conv_b': out['od_conv_b'], 'od_dt_bias': out['od_dt_bias'], 'od_a_log': out['od_a_log'], 'od_d': out['od_d'], 'od_gnorm_w': out['od_gnorm_w'], 'od_w_out': out['od_w_out'], 'final_norm_w': out['final_norm_w'], 'loss_target': out['loss_target'], 'm_ev_norm_w': out['m_ev_norm_w'], 'm_ev_w_in': out['m_ev_w_in'], 'm_ev_dw_w': out['m_ev_dw_w'], 'm_ev_dw_b': out['m_ev_dw_b'], 'm_ev_ln_w': out['m_ev_ln_w'], 'm_ev_ln_b': out['m_ev_ln_b'], 'm_ev_w_out': out['m_ev_w_out'], 'm_od_norm_w': out['m_od_norm_w'], 'm_od_w_in': out['m_od_w_in'], 'm_od_conv_w': out['m_od_conv_w'], 'm_od_conv_b': out['m_od_conv_b'], 'm_od_dt_bias': out['m_od_dt_bias'], 'm_od_a_log': out['m_od_a_log'], 'm_od_d': out['m_od_d'], 'm_od_gnorm_w': out['m_od_gnorm_w'], 'm_od_w_out': out['m_od_w_out'], 'm_final_norm_w': out['m_final_norm_w'], 'v_ev_norm_w': out['v_ev_norm_w'], 'v_ev_w_in': out['v_ev_w_in'], 'v_ev_dw_w': out['v_ev_dw_w'], 'v_ev_dw_b': out['v_ev_dw_b'], 'v_ev_ln_w': out['v_ev_ln_w'], 'v_ev_ln_b': out['v_ev_ln_b'], 'v_ev_w_out': out['v_ev_w_out'], 'v_od_norm_w': out['v_od_norm_w'], 'v_od_w_in': out['v_od_w_in'], 'v_od_conv_w': out['v_od_conv_w'], 'v_od_conv_b': out['v_od_conv_b'], 'v_od_dt_bias': out['v_od_dt_bias'], 'v_od_a_log': out['v_od_a_log'], 'v_od_d': out['v_od_d'], 'v_od_gnorm_w': out['v_od_gnorm_w'], 'v_od_w_out': out['v_od_w_out'], 'v_final_norm_w': out['v_final_norm_w']}


def _loss(weights, diff, rest, loss_target):
    with _jax.named_scope("forward"):
        args = {**rest, TWIN_DIFF_INPUT: diff, **{k: w.astype(_WEIGHT_DTYPES[k]) for k, w in weights.items()}}
        y = _forward(args)
    with _jax.named_scope("loss_head"):
        err = _jnp.square(y.astype(_jnp.float32) - loss_target)
        return 0.5 * _jnp.sum(_jnp.mean(err, axis=-1)) if err.ndim else 0.5 * err


def _adamw(w, g, m, v):
    m = ADAM_B1 * m + (1.0 - ADAM_B1) * g
    v = ADAM_B2 * v + (1.0 - ADAM_B2) * _jnp.square(g)
    m_hat = m / (1.0 - ADAM_B1 ** ADAM_STEP)
    v_hat = v / (1.0 - ADAM_B2 ** ADAM_STEP)
    delta = -ADAM_LR * (m_hat / (_jnp.sqrt(v_hat) + ADAM_EPS) + ADAM_WD * w)
    return delta, m, v


def reference(x, ev_norm_w, ev_w_in, ev_dw_w, ev_dw_b, ev_ln_w, ev_ln_b, ev_w_out, od_norm_w, od_w_in, od_conv_w, od_conv_b, od_dt_bias, od_a_log, od_d, od_gnorm_w, od_w_out, final_norm_w, loss_target, m_ev_norm_w, m_ev_w_in, m_ev_dw_w, m_ev_dw_b, m_ev_ln_w, m_ev_ln_b, m_ev_w_out, m_od_norm_w, m_od_w_in, m_od_conv_w, m_od_conv_b, m_od_dt_bias, m_od_a_log, m_od_d, m_od_gnorm_w, m_od_w_out, m_final_norm_w, v_ev_norm_w, v_ev_w_in, v_ev_dw_w, v_ev_dw_b, v_ev_ln_w, v_ev_ln_b, v_ev_w_out, v_od_norm_w, v_od_w_in, v_od_conv_w, v_od_conv_b, v_od_dt_bias, v_od_a_log, v_od_d, v_od_gnorm_w, v_od_w_out, v_final_norm_w):
    given = dict(x=x, ev_norm_w=ev_norm_w, ev_w_in=ev_w_in, ev_dw_w=ev_dw_w, ev_dw_b=ev_dw_b, ev_ln_w=ev_ln_w, ev_ln_b=ev_ln_b, ev_w_out=ev_w_out, od_norm_w=od_norm_w, od_w_in=od_w_in, od_conv_w=od_conv_w, od_conv_b=od_conv_b, od_dt_bias=od_dt_bias, od_a_log=od_a_log, od_d=od_d, od_gnorm_w=od_gnorm_w, od_w_out=od_w_out, final_norm_w=final_norm_w, loss_target=loss_target, m_ev_norm_w=m_ev_norm_w, m_ev_w_in=m_ev_w_in, m_ev_dw_w=m_ev_dw_w, m_ev_dw_b=m_ev_dw_b, m_ev_ln_w=m_ev_ln_w, m_ev_ln_b=m_ev_ln_b, m_ev_w_out=m_ev_w_out, m_od_norm_w=m_od_norm_w, m_od_w_in=m_od_w_in, m_od_conv_w=m_od_conv_w, m_od_conv_b=m_od_conv_b, m_od_dt_bias=m_od_dt_bias, m_od_a_log=m_od_a_log, m_od_d=m_od_d, m_od_gnorm_w=m_od_gnorm_w, m_od_w_out=m_od_w_out, m_final_norm_w=m_final_norm_w, v_ev_norm_w=v_ev_norm_w, v_ev_w_in=v_ev_w_in, v_ev_dw_w=v_ev_dw_w, v_ev_dw_b=v_ev_dw_b, v_ev_ln_w=v_ev_ln_w, v_ev_ln_b=v_ev_ln_b, v_ev_w_out=v_ev_w_out, v_od_norm_w=v_od_norm_w, v_od_w_in=v_od_w_in, v_od_conv_w=v_od_conv_w, v_od_conv_b=v_od_conv_b, v_od_dt_bias=v_od_dt_bias, v_od_a_log=v_od_a_log, v_od_d=v_od_d, v_od_gnorm_w=v_od_gnorm_w, v_od_w_out=v_od_w_out, v_final_norm_w=v_final_norm_w)
    weights = {n: given[n] for n in TWIN_WEIGHTS}
    shared = {n: given[n] for n in SHARED_INPUTS}
    per_example = {n: given[n] for n in ['x']}
    grad_fn = _jax.value_and_grad(_loss, argnums=(0, 1))

    def one_microbatch(ex, loss_target):
        ex = dict(ex)
        diff = ex.pop(TWIN_DIFF_INPUT)
        return grad_fn(weights, diff, {**shared, **ex}, loss_target)

    if N_MICROBATCH == 1:
        loss, (grad_w, grad_x) = one_microbatch(per_example, given["loss_target"])
    else:
        def body(carry, xs):
            loss_sum, grad_sum = carry
            l_k, (gw_k, gx_k) = one_microbatch(xs[0], xs[1])
            with _jax.named_scope("update"):
                return (loss_sum + l_k, _jax.tree.map(_jnp.add, grad_sum, gw_k)), gx_k

        init = (_jnp.zeros((), _jnp.float32), _jax.tree.map(_jnp.zeros_like, weights))
        (loss, grad_w), grad_x = _jax.lax.scan(body, init, (per_example, given["loss_target"]))
    with _jax.named_scope("update"):
        delta_w, new_m, new_v = {}, {}, {}
        for n in TWIN_WEIGHTS:
            delta_w[n], new_m[n], new_v[n] = _adamw(weights[n], grad_w[n], given["m_" + n], given["v_" + n])
    return (loss, grad_x, *[grad_w[n] for n in TWIN_WEIGHTS], *[delta_w[n] for n in TWIN_WEIGHTS],
            *[new_m[n] for n in TWIN_WEIGHTS], *[new_v[n] for n in TWIN_WEIGHTS])
```

```python
import functools

import jax
import jax.numpy as jnp
from jax import lax
from jax.experimental import pallas as pl
from jax.experimental.pallas import tpu as pltpu

F32 = jnp.float32
BF16 = jnp.bfloat16

EPS = 1e-6
N_DEV = 8
LANE = 128
VMEM_LIMIT_BYTES = 48 * 1024 * 1024

SB_HEAD_DIM = 128
CONF_KERNEL = 31
SSM_CONV = 4
SSM_HEAD_DIM = 64
SSM_GROUPS = 4
SSM_STATE = 128
SSM_CHUNK = 128
HALO = 32
HEAD_ROWS = 8
NEG_BIG = -1e30

ADAM_LR = 0.001
ADAM_B1 = 0.9
ADAM_B2 = 0.999
ADAM_EPS = 1e-08
ADAM_WD = 0.01
ADAM_STEP = 10

NT = (((1,), (1,)), ((), ()))
TN = (((0,), (0,)), ((), ()))


def _cp(*sem):
    return pltpu.CompilerParams(dimension_semantics=sem, vmem_limit_bytes=VMEM_LIMIT_BYTES)


def _pick(n, cap, align):
    if n <= cap:
        return n
    t = (cap // align) * align
    while t >= align:
        if n % t == 0:
            return t
        t -= align
    raise ValueError(f"no tile for {n} (cap {cap}, align {align})")


def _sigmoid(x):
    return 1.0 / (1.0 + jnp.exp(-x))


def _silu(x):
    return x * _sigmoid(x)


def _silu_grad(x):
    s = _sigmoid(x)
    return s * (1.0 + x * (1.0 - s))


def _dot(a, b, dims=None):
    if dims is None:
        return jnp.dot(a, b, preferred_element_type=F32)
    return lax.dot_general(a, b, dims, preferred_element_type=F32)


def _split_dot(x, m):
    hi = x.astype(BF16)
    lo = (x - hi.astype(F32)).astype(BF16)
    return _dot(hi, m) + _dot(lo, m)


def _tri_dot3(tri, x):
    hi = x.astype(BF16)
    r1 = x - hi.astype(F32)
    mid = r1.astype(BF16)
    lo = (r1 - mid.astype(F32)).astype(BF16)
    return _dot(tri, hi) + _dot(tri, mid) + _dot(tri, lo)


def mm_nn(a, b, *, add=None, out_dtype, name):
    M, K = a.shape
    N = b.shape[1]
    tm = _pick(M, 1024, 16)
    tn = _pick(N, 1024, LANE)

    def body(*refs):
        if add is None:
            a_ref, b_ref, o_ref = refs
        else:
            a_ref, b_ref, add_ref, o_ref = refs
        acc = _dot(a_ref[...], b_ref[...])
        if add is not None:
            acc = acc + add_ref[...]
        o_ref[...] = acc.astype(out_dtype)

    in_specs = [pl.BlockSpec((tm, K), lambda i, j: (i, 0)), pl.BlockSpec((K, tn), lambda i, j: (0, j))]
    args = [a, b]
    if add is not None:
        in_specs.append(pl.BlockSpec((tm, tn), lambda i, j: (i, j)))
        args.append(add)
    return pl.pallas_call(
        body, name=name, grid=(M // tm, N // tn), in_specs=in_specs,
        out_specs=pl.BlockSpec((tm, tn), lambda i, j: (i, j)),
        out_shape=jax.ShapeDtypeStruct((M, N), out_dtype),
        compiler_params=_cp("parallel", "parallel"))(*args)


def mm_nt_terms(terms, b, *, out_dtype, name):
    M = terms[0][0].shape[0]
    N = b.shape[0]
    tm = _pick(M, 512, 16)
    tn = _pick(N, 512, LANE)
    n_terms = len(terms)

    def body(*refs):
        o_ref = refs[-1]
        acc = None
        for t in range(n_terms):
            part = _dot(refs[2 * t][...], refs[2 * t + 1][...], NT)
            acc = part if acc is None else acc + part
        o_ref[...] = acc.astype(out_dtype)

    in_specs, args = [], []
    for arr, cb, w, off in terms:
        assert off % w == 0
        in_specs.append(pl.BlockSpec((tm, w), lambda i, j, cb=cb: (i, cb)))
        in_specs.append(pl.BlockSpec((tn, w), lambda i, j, ob=off // w: (j, ob)))
        args += [arr, b]
    return pl.pallas_call(
        body, name=name, grid=(M // tm, N // tn), in_specs=in_specs,
        out_specs=pl.BlockSpec((tm, tn), lambda i, j: (i, j)),
        out_shape=jax.ShapeDtypeStruct((M, N), out_dtype),
        compiler_params=_cp("parallel", "parallel"))(*args)


def mm_tn(a, b, *, out_dtype, name):
    T, M = a.shape
    N = b.shape[1]
    tm = _pick(M, 1024, LANE)
    tn = _pick(N, 1024, LANE)
    tk = _pick(T, 1024, 16)
    nk = T // tk

    def body(a_ref, b_ref, o_ref, acc_ref):
        k = pl.program_id(2)

        @pl.when(k == 0)
        def _():
            acc_ref[...] = jnp.zeros_like(acc_ref)

        acc_ref[...] += _dot(a_ref[...], b_ref[...], TN)

        @pl.when(k == nk - 1)
        def _():
            o_ref[...] = acc_ref[...].astype(out_dtype)

    return pl.pallas_call(
        body, name=name, grid=(M // tm, N // tn, nk),
        in_specs=[pl.BlockSpec((tk, tm), lambda i, j, k: (k, i)), pl.BlockSpec((tk, tn), lambda i, j, k: (k, j))],
        out_specs=pl.BlockSpec((tm, tn), lambda i, j, k: (i, j)),
        out_shape=jax.ShapeDtypeStruct((M, N), out_dtype),
        scratch_shapes=[pltpu.VMEM((tm, tn), F32)],
        compiler_params=_cp("parallel", "parallel", "arbitrary"))(a, b)


def rmsnorm_fwd(h, w, *, name):
    T, D = h.shape
    tt = _pick(T, 512, 16)

    def body(h_ref, w_ref, n_ref):
        x = h_ref[...]
        r = lax.rsqrt(jnp.mean(x * x, axis=-1, keepdims=True) + EPS)
        n_ref[...] = (x * r * w_ref[...]).astype(BF16)

    return pl.pallas_call(
        body, name=name, grid=(T // tt,),
        in_specs=[pl.BlockSpec((tt, D), lambda i: (i, 0)), pl.BlockSpec((1, D), lambda i: (0, 0))],
        out_specs=pl.BlockSpec((tt, D), lambda i: (i, 0)),
        out_shape=jax.ShapeDtypeStruct((T, D), BF16),
        compiler_params=_cp("parallel"))(h, w)


def rmsnorm_bwd(h, w, dn, dres, *, name):
    T, D = h.shape
    tt = _pick(T, 512, 16)

    def body(h_ref, w_ref, dn_ref, dres_ref, dh_ref, gw_ref):
        @pl.when(pl.program_id(0) == 0)
        def _():
            gw_ref[...] = jnp.zeros_like(gw_ref)

        x = h_ref[...]
        r = lax.rsqrt(jnp.mean(x * x, axis=-1, keepdims=True) + EPS)
        xhat = x * r
        g = dn_ref[...].astype(F32)
        gw_ref[...] += jnp.sum(g * xhat, axis=0, keepdims=True)
        dxh = g * w_ref[...]
        dx = r * (dxh - xhat * jnp.mean(dxh * xhat, axis=-1, keepdims=True))
        dh_ref[...] = dres_ref[...] + dx

    row = pl.BlockSpec((tt, D), lambda i: (i, 0))
    vec = pl.BlockSpec((1, D), lambda i: (0, 0))
    return pl.pallas_call(
        body, name=name, grid=(T // tt,), in_specs=[row, vec, row, row], out_specs=[row, vec],
        out_shape=[jax.ShapeDtypeStruct((T, D), F32), jax.ShapeDtypeStruct((1, D), F32)],
        compiler_params=_cp("arbitrary"))(h, w, dn, dres)


def final_loss(h, w, target, *, name):
    T, D = h.shape
    tt = _pick(T, 512, 16)

    def body(h_ref, w_ref, t_ref, loss_ref, dh_ref, gw_ref):
        @pl.when(pl.program_id(0) == 0)
        def _():
            gw_ref[...] = jnp.zeros_like(gw_ref)
            loss_ref[...] = jnp.zeros_like(loss_ref)

        x = h_ref[...]
        r = lax.rsqrt(jnp.mean(x * x, axis=-1, keepdims=True) + EPS)
        xhat = x * r
        e = xhat * w_ref[...] - t_ref[...]
        loss_ref[...] += jnp.sum(e * e) * (0.5 / D)
        g = e * (1.0 / D)
        gw_ref[...] += jnp.sum(g * xhat, axis=0, keepdims=True)
        dxh = g * w_ref[...]
        dh_ref[...] = r * (dxh - xhat * jnp.mean(dxh * xhat, axis=-1, keepdims=True))

    row = pl.BlockSpec((tt, D), lambda i: (i, 0))
    vec = pl.BlockSpec((1, D), lambda i: (0, 0))
    one = pl.BlockSpec((1, LANE), lambda i: (0, 0))
    return pl.pallas_call(
        body, name=name, grid=(T // tt,), in_specs=[row, vec, row], out_specs=[one, row, vec],
        out_shape=[jax.ShapeDtypeStruct((1, LANE), F32), jax.ShapeDtypeStruct((T, D), F32),
                   jax.ShapeDtypeStruct((1, D), F32)],
        compiler_params=_cp("arbitrary"))(h, w, target)


def _conv_tiles(T, C):
    return _pick(T, 256, HALO), _pick(C, 512, LANE)


def dwconv_fwd(src, offs, w, b, *, C, seq, glu, silu_out, name):
    T = src.shape[0]
    K = w.shape[0]
    tt, tc = _conv_tiles(T, C)
    n_in = 2 if glu else 1
    per = tt // HALO

    def body(*refs):
        cur = refs[0:2 * n_in:2]
        halo = refs[1:2 * n_in:2]
        w_ref, b_ref = refs[2 * n_in], refs[2 * n_in + 1]
        outs = refs[2 * n_in + 2:-1]
        buf = refs[-1]
        i = pl.program_id(1)
        first = (i * tt) % seq == 0

        def pre(rs):
            v = rs[0][...].astype(F32)
            return v * _sigmoid(rs[1][...].astype(F32)) if glu else v

        buf[0:HALO, :] = jnp.where(first, 0.0, pre(halo))
        buf[HALO:HALO + tt, :] = pre(cur)
        acc = jnp.broadcast_to(b_ref[...], (tt, tc))
        for k in range(K):
            acc = acc + w_ref[k:k + 1, :] * buf[pl.ds(HALO - (K - 1) + k, tt), :]
        outs[0][...] = acc.astype(BF16)
        if silu_out:
            outs[1][...] = _silu(acc).astype(BF16)

    in_specs, args = [], []
    for off in offs:
        assert off % tc == 0
        in_specs.append(pl.BlockSpec((tt, tc), lambda j, i, ob=off // tc: (i, ob + j)))
        in_specs.append(pl.BlockSpec((HALO, tc), lambda j, i, ob=off // tc: (jnp.maximum(i * per - 1, 0), ob + j)))
        args += [src, src]
    in_specs += [pl.BlockSpec((K, tc), lambda j, i: (0, j)), pl.BlockSpec((1, tc), lambda j, i: (0, j))]
    args += [w, b]
    n_out = 2 if silu_out else 1
    out = pl.pallas_call(
        body, name=name, grid=(C // tc, T // tt), in_specs=in_specs,
        out_specs=[pl.BlockSpec((tt, tc), lambda j, i: (i, j))] * n_out,
        out_shape=[jax.ShapeDtypeStruct((T, C), BF16)] * n_out,
        scratch_shapes=[pltpu.VMEM((HALO + tt, tc), F32)],
        compiler_params=_cp("parallel", "arbitrary"))(*args)
    return out


def dwconv_bwd(du, u, src, offs, w, *, C, seq, glu, silu_out, name):
    T = src.shape[0]
    K = w.shape[0]
    tt, tc = _conv_tiles(T, C)
    n_in = 2 if glu else 1
    per = tt // HALO
    last_blk = T // HALO - 1

    def body(*refs):
        pos = 0
        du_cur, du_nxt = refs[0], refs[1]
        pos = 2
        if silu_out:
            u_cur, u_nxt = refs[2], refs[3]
            pos = 4
        cur = refs[pos:pos + 2 * n_in:2]
        halo = refs[pos + 1:pos + 2 * n_in:2]
        pos += 2 * n_in
        w_ref = refs[pos]
        outs = refs[pos + 1:pos + 1 + n_in]
        dw_ref, db_ref = refs[pos + 1 + n_in], refs[pos + 2 + n_in]
        gbuf, xbuf = refs[-2], refs[-1]
        i = pl.program_id(1)
        first = (i * tt) % seq == 0
        last = ((i + 1) * tt) % seq == 0

        @pl.when(i == 0)
        def _():
            dw_ref[...] = jnp.zeros_like(dw_ref)
            db_ref[...] = jnp.zeros_like(db_ref)

        g_cur = du_cur[...].astype(F32)
        g_nxt = du_nxt[...].astype(F32)
        if silu_out:
            g_cur = g_cur * _silu_grad(u_cur[...].astype(F32))
            g_nxt = g_nxt * _silu_grad(u_nxt[...].astype(F32))
        gbuf[0:tt, :] = g_cur
        gbuf[tt:tt + HALO, :] = jnp.where(last, 0.0, g_nxt)

        def pre(rs):
            v = rs[0][...].astype(F32)
            return v * _sigmoid(rs[1][...].astype(F32)) if glu else v

        xbuf[0:HALO, :] = jnp.where(first, 0.0, pre(halo))
        xbuf[HALO:HALO + tt, :] = pre(cur)

        dx = jnp.zeros((tt, tc), F32)
        for k in range(K):
            dx = dx + w_ref[k:k + 1, :] * gbuf[pl.ds(K - 1 - k, tt), :]
            dw_ref[k:k + 1, :] += jnp.sum(g_cur * xbuf[pl.ds(HALO - (K - 1) + k, tt), :], axis=0, keepdims=True)
        db_ref[...] += jnp.sum(g_cur, axis=0, keepdims=True)
        if glu:
            a = cur[0][...].astype(F32)
            s = _sigmoid(cur[1][...].astype(F32))
            outs[0][...] = (dx * s).astype(BF16)
            outs[1][...] = (dx * a * s * (1.0 - s)).astype(BF16)
        else:
            outs[0][...] = dx.astype(BF16)

    def cur_spec(ob):
        return pl.BlockSpec((tt, tc), lambda j, i: (i, ob + j))

    def nxt_spec(ob):
        return pl.BlockSpec((HALO, tc), lambda j, i: (jnp.minimum((i + 1) * per, last_blk), ob + j))

    def prv_spec(ob):
        return pl.BlockSpec((HALO, tc), lambda j, i: (jnp.maximum(i * per - 1, 0), ob + j))

    in_specs = [cur_spec(0), nxt_spec(0)]
    args = [du, du]
    if silu_out:
        in_specs += [cur_spec(0), nxt_spec(0)]
        args += [u, u]
    for off in offs:
        assert off % tc == 0
        in_specs += [cur_spec(off // tc), prv_spec(off // tc)]
        args += [src, src]
    in_specs.append(pl.BlockSpec((K, tc), lambda j, i: (0, j)))
    args.append(w)
    out_specs = [pl.BlockSpec((tt, tc), lambda j, i: (i, j))] * n_in
    out_specs += [pl.BlockSpec((K, tc), lambda j, i: (0, j)), pl.BlockSpec((1, tc), lambda j, i: (0, j))]
    out_shape = [jax.ShapeDtypeStruct((T, C), BF16)] * n_in
    out_shape += [jax.ShapeDtypeStruct((K, C), F32), jax.ShapeDtypeStruct((1, C), F32)]
    return pl.pallas_call(
        body, name=name, grid=(C // tc, T // tt), in_specs=in_specs, out_specs=out_specs, out_shape=out_shape,
        scratch_shapes=[pltpu.VMEM((tt + HALO, tc), F32), pltpu.VMEM((HALO + tt, tc), F32)],
        compiler_params=_cp("parallel", "arbitrary"))(*args)


def mix0_post_fwd(u2, proj, o, ln_w, ln_b, *, CW, gc_off, ga_off, name):
    T = u2.shape[0]
    tt = _pick(T, 256, 16)

    def body(u_ref, gc_ref, ga_ref, o_ref, lw_ref, lb_ref, y_ref):
        u = u_ref[...].astype(F32)
        mu = jnp.mean(u, axis=-1, keepdims=True)
        xc = u - mu
        r = lax.rsqrt(jnp.mean(xc * xc, axis=-1, keepdims=True) + EPS)
        u3 = xc * r * lw_ref[...] + lb_ref[...]
        y_ref[:, 0:CW] = (_silu(u3) * _silu(gc_ref[...].astype(F32))).astype(BF16)
        y_ref[:, CW:2 * CW] = (o_ref[...].astype(F32) * _silu(ga_ref[...].astype(F32))).astype(BF16)

    row = pl.BlockSpec((tt, CW), lambda i: (i, 0))
    vec = pl.BlockSpec((1, CW), lambda i: (0, 0))
    return pl.pallas_call(
        body, name=name, grid=(T // tt,),
        in_specs=[row, pl.BlockSpec((tt, CW), lambda i: (i, gc_off // CW)),
                  pl.BlockSpec((tt, CW), lambda i: (i, ga_off // CW)), row, vec, vec],
        out_specs=pl.BlockSpec((tt, 2 * CW), lambda i: (i, 0)),
        out_shape=jax.ShapeDtypeStruct((T, 2 * CW), BF16),
        compiler_params=_cp("parallel"))(u2, proj, proj, o, ln_w, ln_b)


def mix0_post_bwd(dy, u2, proj, o, ln_w, ln_b, *, CW, gc_off, ga_off, name):
    T = u2.shape[0]
    tt = _pick(T, 256, 16)

    def body(dy_ref, u_ref, gc_ref, ga_ref, o_ref, lw_ref, lb_ref, du_ref, dgc_ref, dga_ref, do_ref, dlw_ref, dlb_ref):
        @pl.when(pl.program_id(0) == 0)
        def _():
            dlw_ref[...] = jnp.zeros_like(dlw_ref)
            dlb_ref[...] = jnp.zeros_like(dlb_ref)

        dyc = dy_ref[:, 0:CW].astype(F32)
        dya = dy_ref[:, CW:2 * CW].astype(F32)
        u = u_ref[...].astype(F32)
        mu = jnp.mean(u, axis=-1, keepdims=True)
        xc = u - mu
        r = lax.rsqrt(jnp.mean(xc * xc, axis=-1, keepdims=True) + EPS)
        xhat = xc * r
        u3 = xhat * lw_ref[...] + lb_ref[...]
        gc = gc_ref[...].astype(F32)
        dgc_ref[...] = (dyc * _silu(u3) * _silu_grad(gc)).astype(BF16)
        du3 = dyc * _silu(gc) * _silu_grad(u3)
        dlw_ref[...] += jnp.sum(du3 * xhat, axis=0, keepdims=True)
        dlb_ref[...] += jnp.sum(du3, axis=0, keepdims=True)
        dxh = du3 * lw_ref[...]
        du = r * (dxh - jnp.mean(dxh, axis=-1, keepdims=True) - xhat * jnp.mean(dxh * xhat, axis=-1, keepdims=True))
        du_ref[...] = du.astype(BF16)
        ga = ga_ref[...].astype(F32)
        ov = o_ref[...].astype(F32)
        do_ref[...] = (dya * _silu(ga)).astype(BF16)
        dga_ref[...] = (dya * ov * _silu_grad(ga)).astype(BF16)

    row = pl.BlockSpec((tt, CW), lambda i: (i, 0))
    vec = pl.BlockSpec((1, CW), lambda i: (0, 0))
    big = jax.ShapeDtypeStruct((T, CW), BF16)
    small = jax.ShapeDtypeStruct((1, CW), F32)
    return pl.pallas_call(
        body, name=name, grid=(T // tt,),
        in_specs=[pl.BlockSpec((tt, 2 * CW), lambda i: (i, 0)), row,
                  pl.BlockSpec((tt, CW), lambda i: (i, gc_off // CW)),
                  pl.BlockSpec((tt, CW), lambda i: (i, ga_off // CW)), row, vec, vec],
        out_specs=[row, row, row, row, vec, vec],
        out_shape=[big, big, big, big, small, small],
        compiler_params=_cp("arbitrary"))(dy, u2, proj, proj, o, ln_w, ln_b)


def _sb_tiles(seq):
    return _pick(seq, 256, SSM_CHUNK), 128


def _sb_scores(q, k_blk, row0, col0):
    tq, tk = q.shape[0], k_blk.shape[0]
    z = _dot(q, k_blk, NT)
    rows = row0 + lax.broadcasted_iota(jnp.int32, (tq, tk), 0)
    cols = col0 + lax.broadcasted_iota(jnp.int32, (tq, tk), 1)
    mask = cols < rows
    e = jnp.exp(-jnp.abs(z))
    sp = jnp.where(mask, jnp.maximum(z, 0.0) + jnp.log(1.0 + e), 0.0)
    return z, mask, e, sp


def _tri01(n, lower):
    i = lax.broadcasted_iota(jnp.int32, (n, n), 0)
    j = lax.broadcasted_iota(jnp.int32, (n, n), 1)
    return ((i >= j) if lower else (i <= j)).astype(BF16)


def sba_fwd(proj, *, B, seq, heads, q_off, k_off, v_off, name):
    dh = SB_HEAD_DIM
    tq, tk = _sb_tiles(seq)
    nq = seq // tq
    scale = dh ** -0.5

    def body(q_ref, k_ref, v_ref, o_ref, ct_ref, acc_ref):
        qi = pl.program_id(1)
        q = (q_ref[...].astype(F32) * scale).astype(BF16)
        tri = _tri01(tk, True)
        acc_ref[...] = jnp.zeros_like(acc_ref)
        nkb = (qi + 1) * (tq // tk)

        def step(j, r):
            start = pl.multiple_of((nkb - 1 - j) * tk, tk)
            k_blk = k_ref[pl.ds(start, tk), :]
            v_blk = v_ref[pl.ds(start, tk), :]
            z, mask, _, sp = _sb_scores(q, k_blk, qi * tq, start)
            wts = jnp.where(mask, jnp.exp(z - (_split_dot(sp, tri) + r)), 0.0)
            acc_ref[...] += _dot(wts.astype(BF16), v_blk)
            return r + jnp.sum(sp, axis=-1, keepdims=True)

        total = lax.fori_loop(0, nkb, step, jnp.zeros((tq, 1), F32))
        o_ref[...] = acc_ref[...].astype(BF16)
        ct_ref[0] = jnp.broadcast_to(total, (tq, LANE)).T[0:8, :]

    qb, kb, vb = q_off // dh, k_off // dh, v_off // dh
    return pl.pallas_call(
        body, name=name, grid=(B * heads, nq),
        in_specs=[pl.BlockSpec((tq, dh), lambda g, i: ((g // heads) * nq + i, qb + g % heads)),
                  pl.BlockSpec((seq, dh), lambda g, i: (g // heads, kb + g % heads)),
                  pl.BlockSpec((seq, dh), lambda g, i: (g // heads, vb + g % heads))],
        out_specs=[pl.BlockSpec((tq, dh), lambda g, i: ((g // heads) * nq + i, g % heads)),
                   pl.BlockSpec((1, 8, tq), lambda g, i: (g * nq + i, 0, 0))],
        out_shape=[jax.ShapeDtypeStruct((B * seq, heads * dh), BF16),
                   jax.ShapeDtypeStruct((B * heads * nq, 8, tq), F32)],
        scratch_shapes=[pltpu.VMEM((tq, dh), F32)],
        compiler_params=_cp("parallel", "arbitrary"))(proj, proj, proj)


def sba_bwd(proj, ctot, do, *, B, seq, heads, q_off, k_off, v_off, name):
    dh = SB_HEAD_DIM
    tq, tk = _sb_tiles(seq)
    assert tk == LANE
    nq = seq // tq
    scale = dh ** -0.5

    def body(q_ref, k_ref, v_ref, ct_ref, do_ref, dq_ref, dk_ref, dv_ref, dq_acc, dk_acc, dv_acc):
        qi = pl.program_id(1)

        @pl.when(qi == 0)
        def _():
            dk_acc[...] = jnp.zeros_like(dk_acc)
            dv_acc[...] = jnp.zeros_like(dv_acc)

        q = (q_ref[...].astype(F32) * scale).astype(BF16)
        dov = do_ref[...]
        total = jnp.broadcast_to(ct_ref[0, 0:1, :], (LANE, tq)).T
        tri = _tri01(tk, False)
        dq_acc[...] = jnp.zeros_like(dq_acc)
        nkb = (qi + 1) * (tq // tk)

        def step(j, carry):
            pc, pg = carry
            start = pl.multiple_of(j * tk, tk)
            k_blk = k_ref[pl.ds(start, tk), :]
            v_blk = v_ref[pl.ds(start, tk), :]
            z, mask, e, sp = _sb_scores(q, k_blk, qi * tq, start)
            suffix = (total - pc) - _split_dot(sp, tri) + sp
            wts = jnp.where(mask, jnp.exp(z - suffix), 0.0)
            sig = jnp.where(z >= 0.0, 1.0, e) / (1.0 + e)
            g = _dot(dov, v_blk, NT) * wts
            dz = jnp.where(mask, g - sig * (pg + _split_dot(g, tri)), 0.0).astype(BF16)
            dq_acc[...] += _dot(dz, k_blk)
            dk_acc[pl.ds(start, tk), :] += _dot(dz, q, TN)
            dv_acc[pl.ds(start, tk), :] += _dot(wts.astype(BF16), dov, TN)
            return pc + jnp.sum(sp, axis=-1, keepdims=True), pg + jnp.sum(g, axis=-1, keepdims=True)

        zero = jnp.zeros((tq, 1), F32)
        lax.fori_loop(0, nkb, step, (zero, zero))
        dq_ref[...] = (dq_acc[...] * scale).astype(BF16)

        @pl.when(qi == nq - 1)
        def _():
            dk_ref[...] = dk_acc[...].astype(BF16)
            dv_ref[...] = dv_acc[...].astype(BF16)

    qb, kb, vb = q_off // dh, k_off // dh, v_off // dh
    q_spec = pl.BlockSpec((tq, dh), lambda g, i: ((g // heads) * nq + i, qb + g % heads))
    o_spec = pl.BlockSpec((tq, dh), lambda g, i: ((g // heads) * nq + i, g % heads))
    kv_out = pl.BlockSpec((seq, dh), lambda g, i: (g // heads, g % heads))
    shp = jax.ShapeDtypeStruct((B * seq, heads * dh), BF16)
    return pl.pallas_call(
        body, name=name, grid=(B * heads, nq),
        in_specs=[q_spec,
                  pl.BlockSpec((seq, dh), lambda g, i: (g // heads, kb + g % heads)),
                  pl.BlockSpec((seq, dh), lambda g, i: (g // heads, vb + g % heads)),
                  pl.BlockSpec((1, 8, tq), lambda g, i: (g * nq + i, 0, 0)), o_spec],
        out_specs=[o_spec, kv_out, kv_out], out_shape=[shp, shp, shp],
        scratch_shapes=[pltpu.VMEM((tq, dh), F32), pltpu.VMEM((seq, dh), F32), pltpu.VMEM((seq, dh), F32)],
        compiler_params=_cp("parallel", "arbitrary"))(proj, proj, proj, ctot, do)


def _chunk_tri(lower):
    i = lax.broadcasted_iota(jnp.int32, (SSM_CHUNK, SSM_CHUNK), 0)
    j = lax.broadcasted_iota(jnp.int32, (SSM_CHUNK, SSM_CHUNK), 1)
    return ((i >= j) if lower else (i <= j)).astype(BF16)


def ssm_dt_fwd(proj, bias, a_log, *, dt_off, name):
    T = proj.shape[0]
    L = SSM_CHUNK
    tt = _pick(T, 512, L)

    def body(raw_ref, bias_ref, al_ref, dt_ref, cs_ref):
        x = raw_ref[...].astype(F32) + bias_ref[...]
        dt = jnp.maximum(x, 0.0) + jnp.log(1.0 + jnp.exp(-jnp.abs(x)))
        dt_ref[...] = dt
        la = dt * (-jnp.exp(al_ref[...]))
        tri = _chunk_tri(True)
        for c in range(tt // L):
            cs_ref[c * L:(c + 1) * L, :] = _tri_dot3(tri, la[c * L:(c + 1) * L, :])

    row = pl.BlockSpec((tt, LANE), lambda i: (i, 0))
    vec = pl.BlockSpec((1, LANE), lambda i: (0, 0))
    shp = jax.ShapeDtypeStruct((T, LANE), F32)
    return pl.pallas_call(
        body, name=name, grid=(T // tt,),
        in_specs=[pl.BlockSpec((tt, LANE), lambda i: (i, dt_off // LANE)), vec, vec],
        out_specs=[row, row], out_shape=[shp, shp], compiler_params=_cp("parallel"))(proj, bias, a_log)


def ssm_dt_bwd(ddt, dcs, proj, dt, bias, a_log, *, dt_off, n_heads, name):
    T = proj.shape[0]
    L = SSM_CHUNK
    tt = _pick(T, 512, L)

    def body(ddt_ref, dcs_ref, raw_ref, dt_ref, bias_ref, al_ref, draw_ref, dbias_ref, dal_ref, dla_buf):
        @pl.when(pl.program_id(0) == 0)
        def _():
            dbias_ref[...] = jnp.zeros_like(dbias_ref)
            dal_ref[...] = jnp.zeros_like(dal_ref)

        triu = _chunk_tri(False)
        dcs = dcs_ref[...]
        for c in range(tt // L):
            dla_buf[c * L:(c + 1) * L, :] = _tri_dot3(triu, dcs[c * L:(c + 1) * L, :])
        dla = dla_buf[...]
        a = -jnp.exp(al_ref[...])
        dtv = dt_ref[...]
        valid = lax.broadcasted_iota(jnp.int32, (tt, LANE), 1) < n_heads
        dal_ref[...] += jnp.sum(jnp.where(valid, dla * dtv, 0.0), axis=0, keepdims=True) * a
        x = raw_ref[...].astype(F32) + bias_ref[...]
        draw = jnp.where(valid, (ddt_ref[...] + dla * a) * _sigmoid(x), 0.0)
        dbias_ref[...] += jnp.sum(draw, axis=0, keepdims=True)
        draw_ref[...] = draw.astype(BF16)

    row = pl.BlockSpec((tt, LANE), lambda i: (i, 0))
    vec = pl.BlockSpec((1, LANE), lambda i: (0, 0))
    return pl.pallas_call(
        body, name=name, grid=(T // tt,),
        in_specs=[row, row, pl.BlockSpec((tt, LANE), lambda i: (i, dt_off // LANE)), row, vec, vec],
        out_specs=[row, vec, vec],
        out_shape=[jax.ShapeDtypeStruct((T, LANE), BF16), jax.ShapeDtypeStruct((1, LANE), F32),
                   jax.ShapeDtypeStruct((1, LANE), F32)],
        scratch_shapes=[pltpu.VMEM((tt, LANE), F32)],
        compiler_params=_cp("arbitrary"))(ddt, dcs, proj, dt, bias, a_log)


def _colb(row):
    return jnp.broadcast_to(row, (LANE, SSM_CHUNK)).T


def _ssd_pair_common(x_ref, dt_ref, cs_ref, pair, ppg, lo_half, causal, lane_row):
    L = SSM_CHUNK
    g, pp = divmod(pair, ppg)
    ra = g * HEAD_ROWS + 2 * pp
    X = x_ref[:, pair * LANE:(pair + 1) * LANE].astype(F32)
    dta, dtb = dt_ref[0, ra:ra + 1, :], dt_ref[0, ra + 1:ra + 2, :]
    csa, csb = cs_ref[0, ra:ra + 1, :], cs_ref[0, ra + 1:ra + 2, :]
    csa_c, csb_c = _colb(csa), _colb(csb)
    dt_p = jnp.where(lo_half, _colb(dta), _colb(dtb))
    La = jnp.exp(jnp.where(causal, csa_c - csa, NEG_BIG))
    Lb = jnp.exp(jnp.where(causal, csb_c - csb, NEG_BIG))
    last_a = jnp.sum(jnp.where(lane_row == L - 1, csa, 0.0), axis=1, keepdims=True)
    last_b = jnp.sum(jnp.where(lane_row == L - 1, csb, 0.0), axis=1, keepdims=True)
    ecs = jnp.exp(jnp.where(lo_half, csa_c, csb_c))
    tail = jnp.exp(jnp.where(lo_half, last_a - csa_c, last_b - csb_c))
    return g, ra, X, dt_p, La, Lb, last_a, last_b, ecs, tail


def ssd_fwd(xbc, dt_row, cs_row, d_full, *, B, seq, DI, name):
    L, N, G = SSM_CHUNK, SSM_STATE, SSM_GROUPS
    nc = seq // L
    XW = xbc.shape[1]
    n_pairs = DI // LANE
    ppg = n_pairs // G

    def body(x_ref, dt_ref, cs_ref, d_ref, y_ref, st_ref, state):
        c = pl.program_id(1)

        @pl.when(c == 0)
        def _():
            state[...] = jnp.zeros_like(state)

        causal = lax.broadcasted_iota(jnp.int32, (L, L), 0) >= lax.broadcasted_iota(jnp.int32, (L, L), 1)
        lo_half = lax.broadcasted_iota(jnp.int32, (L, LANE), 1) < SSM_HEAD_DIM
        rows_lo = lax.broadcasted_iota(jnp.int32, (LANE, N), 0) < SSM_HEAD_DIM
        lane_row = lax.broadcasted_iota(jnp.int32, (1, L), 1)
        cbs = []
        for g in range(G):
            Bc = x_ref[:, DI + g * N:DI + (g + 1) * N]
            Cc = x_ref[:, DI + G * N + g * N:DI + G * N + (g + 1) * N]
            cbs.append((Bc, Cc, _dot(Cc, Bc, NT)))
        for pair in range(n_pairs):
            g, ra, X, dt_p, La, Lb, last_a, last_b, ecs, tail = _ssd_pair_common(
                x_ref, dt_ref, cs_ref, pair, ppg, lo_half, causal, lane_row)
            Bc, Cc, CB = cbs[g]
            xs = X * dt_p
            xsb = xs.astype(BF16)
            y = jnp.where(lo_half, _dot((CB * La).astype(BF16), xsb), _dot((CB * Lb).astype(BF16), xsb))
            S = state[pair]
            st_ref[0, 0, pair] = S
            y = y + ecs * _dot(Cc, S.astype(BF16), NT)
            y = y + d_ref[:, pair * LANE:(pair + 1) * LANE] * X
            y_ref[:, pair * LANE:(pair + 1) * LANE] = y.astype(BF16)
            e_rows = jnp.where(rows_lo, jnp.exp(last_a), jnp.exp(last_b))
            state[pair] = e_rows * S + _dot((xs * tail).astype(BF16), Bc, TN)

    return pl.pallas_call(
        body, name=name, grid=(B, nc),
        in_specs=[pl.BlockSpec((L, XW), lambda b, c: (b * nc + c, 0)),
                  pl.BlockSpec((1, G * HEAD_ROWS, L), lambda b, c: (b, 0, c)),
                  pl.BlockSpec((1, G * HEAD_ROWS, L), lambda b, c: (b, 0, c)),
                  pl.BlockSpec((1, DI), lambda b, c: (0, 0))],
        out_specs=[pl.BlockSpec((L, DI), lambda b, c: (b * nc + c, 0)),
                   pl.BlockSpec((1, 1, n_pairs, LANE, N), lambda b, c: (b, c, 0, 0, 0))],
        out_shape=[jax.ShapeDtypeStruct((B * seq, DI), BF16),
                   jax.ShapeDtypeStruct((B, nc, n_pairs, LANE, N), F32)],
        scratch_shapes=[pltpu.VMEM((n_pairs, LANE, N), F32)],
        compiler_params=_cp("parallel", "arbitrary"))(xbc, dt_row, cs_row, d_full)


def ssd_bwd(xbc, dt_row, cs_row, d_full, states, dy, *, B, seq, DI, name):
    L, N, G = SSM_CHUNK, SSM_STATE, SSM_GROUPS
    nc = seq // L
    XW = xbc.shape[1]
    n_pairs = DI // LANE
    ppg = n_pairs // G
    HR = G * HEAD_ROWS

    def body(x_ref, dt_ref, cs_ref, d_ref, st_ref, dy_ref, dx_ref, ddt_ref, dcs_ref, dd_ref, dH):
        c = pl.program_id(1)

        @pl.when(c == 0)
        def _():
            dH[...] = jnp.zeros_like(dH)
            dd_ref[...] = jnp.zeros_like(dd_ref)

        causal = lax.broadcasted_iota(jnp.int32, (L, L), 0) >= lax.broadcasted_iota(jnp.int32, (L, L), 1)
        lo_half = lax.broadcasted_iota(jnp.int32, (L, LANE), 1) < SSM_HEAD_DIM
        rows_lo = lax.broadcasted_iota(jnp.int32, (LANE, N), 0) < SSM_HEAD_DIM
        lane_row = lax.broadcasted_iota(jnp.int32, (1, L), 1)
        head_row = lax.broadcasted_iota(jnp.int32, (HR, 1), 0)
        ddt_all = jnp.zeros((HR, L), F32)
        dcs_all = jnp.zeros((HR, L), F32)

        def place(row, r):
            return jnp.where(head_row == r, row, 0.0)

        def as_rows(col):
            return jnp.broadcast_to(col, (L, LANE)).T[0:HR, :]

        def head_sums(t, ra):
            sa = as_rows(jnp.sum(jnp.where(lo_half, t, 0.0), axis=1, keepdims=True))
            sb = as_rows(jnp.sum(jnp.where(lo_half, 0.0, t), axis=1, keepdims=True))
            return place(sa, ra) + place(sb, ra + 1)

        for g in range(G):
            Bc = x_ref[:, DI + g * N:DI + (g + 1) * N]
            Cc = x_ref[:, DI + G * N + g * N:DI + G * N + (g + 1) * N]
            CB = _dot(Cc, Bc, NT)
            dCB = jnp.zeros((L, L), F32)
            dC = jnp.zeros((L, N), F32)
            dB = jnp.zeros((L, N), F32)
            for pp in range(ppg):
                pair = g * ppg + pp
                _, ra, X, dt_p, La, Lb, last_a, last_b, ecs, tail = _ssd_pair_common(
                    x_ref, dt_ref, cs_ref, pair, ppg, lo_half, causal, lane_row)
                rb = ra + 1
                xs = X * dt_p
                xsb = xs.astype(BF16)
                Ma, Mb = CB * La, CB * Lb
                dY = dy_ref[:, pair * LANE:(pair + 1) * LANE].astype(F32)
                dYb = dY.astype(BF16)
                dMa = _dot(jnp.where(lo_half, dY, 0.0).astype(BF16), xsb, NT)
                dMb = _dot(jnp.where(lo_half, 0.0, dY).astype(BF16), xsb, NT)
                dSa, dSb = dMa * Ma, dMb * Mb
                dCB = dCB + dMa * La + dMb * Lb
                dcs_all = dcs_all + place(as_rows(jnp.sum(dSa, axis=1, keepdims=True)) - jnp.sum(dSa, axis=0, keepdims=True), ra)
                dcs_all = dcs_all + place(as_rows(jnp.sum(dSb, axis=1, keepdims=True)) - jnp.sum(dSb, axis=0, keepdims=True), rb)
                dxs = jnp.where(lo_half, _dot(Ma.astype(BF16), dYb, TN), _dot(Mb.astype(BF16), dYb, TN))
                S = st_ref[0, 0, pair]
                Sb = S.astype(BF16)
                y_inter = ecs * _dot(Cc, Sb, NT)
                dYe = (dY * ecs).astype(BF16)
                dC = dC + _dot(dYe, Sb)
                dHp = _dot(dYe, Cc, TN)
                dcs_all = dcs_all + head_sums(dY * y_inter, ra)
                dHn = dH[pair]
                dHnb = dHn.astype(BF16)
                ea, eb = jnp.exp(last_a), jnp.exp(last_b)
                dHp = dHp + jnp.where(rows_lo, ea, eb) * dHn
                prod = dHn * S
                dlast_a = ea * jnp.sum(jnp.where(rows_lo, prod, 0.0), keepdims=True)
                dlast_b = eb * jnp.sum(jnp.where(rows_lo, 0.0, prod), keepdims=True)
                XBt = _dot(Bc, dHnb, NT)
                dxs = dxs + tail * XBt
                t2 = xs * XBt * tail
                dlast_a = dlast_a + jnp.sum(jnp.where(lo_half, t2, 0.0), keepdims=True)
                dlast_b = dlast_b + jnp.sum(jnp.where(lo_half, 0.0, t2), keepdims=True)
                dcs_all = dcs_all - head_sums(t2, ra)
                dcs_all = dcs_all + place(jnp.where(lane_row == L - 1, dlast_a, 0.0), ra)
                dcs_all = dcs_all + place(jnp.where(lane_row == L - 1, dlast_b, 0.0), rb)
                dB = dB + _dot((xs * tail).astype(BF16), dHnb)
                dfull = d_ref[:, pair * LANE:(pair + 1) * LANE]
                dx_ref[:, pair * LANE:(pair + 1) * LANE] = (dxs * dt_p + dfull * dY).astype(BF16)
                ddt_all = ddt_all + head_sums(dxs * X, ra)
                dd_ref[0, :, pair * LANE:(pair + 1) * LANE] += jnp.sum(dY * X, axis=0, keepdims=True)
                dH[pair] = dHp
            dCBb = dCB.astype(BF16)
            dx_ref[:, DI + g * N:DI + (g + 1) * N] = (dB + _dot(dCBb, Cc, TN)).astype(BF16)
            dx_ref[:, DI + G * N + g * N:DI + G * N + (g + 1) * N] = (dC + _dot(dCBb, Bc)).astype(BF16)
        ddt_ref[0] = ddt_all
        dcs_ref[0] = dcs_all

    rev = lambda b, c: (b * nc + (nc - 1 - c), 0)
    hrow = pl.BlockSpec((1, HR, L), lambda b, c: (b, 0, nc - 1 - c))
    return pl.pallas_call(
        body, name=name, grid=(B, nc),
        in_specs=[pl.BlockSpec((L, XW), rev), hrow, hrow,
                  pl.BlockSpec((1, DI), lambda b, c: (0, 0)),
                  pl.BlockSpec((1, 1, n_pairs, LANE, N), lambda b, c: (b, nc - 1 - c, 0, 0, 0)),
                  pl.BlockSpec((L, DI), rev)],
        out_specs=[pl.BlockSpec((L, XW), rev), hrow, hrow, pl.BlockSpec((1, 1, DI), lambda b, c: (b, 0, 0))],
        out_shape=[jax.ShapeDtypeStruct((B * seq, XW), BF16),
                   jax.ShapeDtypeStruct((B, HR, seq), F32), jax.ShapeDtypeStruct((B, HR, seq), F32),
                   jax.ShapeDtypeStruct((B, 1, DI), F32)],
        scratch_shapes=[pltpu.VMEM((n_pairs, LANE, N), F32)],
        compiler_params=_cp("parallel", "arbitrary"))(xbc, dt_row, cs_row, d_full, states, dy)


def gnorm_fwd(y, proj, w, *, DI, name):
    T = y.shape[0]
    tt = _pick(T, 256, 16)
    gw = DI // SSM_GROUPS

    def body(y_ref, z_ref, w_ref, o_ref):
        for g in range(SSM_GROUPS):
            sl = slice(g * gw, (g + 1) * gw)
            y2 = y_ref[:, sl].astype(F32) * _silu(z_ref[:, sl].astype(F32))
            r = lax.rsqrt(jnp.mean(y2 * y2, axis=-1, keepdims=True) + EPS)
            o_ref[:, sl] = (y2 * r * w_ref[:, sl]).astype(BF16)

    row = pl.BlockSpec((tt, DI), lambda i: (i, 0))
    return pl.pallas_call(
        body, name=name, grid=(T // tt,),
        in_specs=[row, row, pl.BlockSpec((1, DI), lambda i: (0, 0))], out_specs=row,
        out_shape=jax.ShapeDtypeStruct((T, DI), BF16), compiler_params=_cp("parallel"))(y, proj, w)


def gnorm_bwd(dyn, y, proj, w, *, DI, name):
    T = y.shape[0]
    tt = _pick(T, 256, 16)
    gw = DI // SSM_GROUPS

    def body(dyn_ref, y_ref, z_ref, w_ref, dy_ref, dz_ref, dw_ref):
        @pl.when(pl.program_id(0) == 0)
        def _():
            dw_ref[...] = jnp.zeros_like(dw_ref)

        for g in range(SSM_GROUPS):
            sl = slice(g * gw, (g + 1) * gw)
            yv = y_ref[:, sl].astype(F32)
            z = z_ref[:, sl].astype(F32)
            sz = _silu(z)
            y2 = yv * sz
            r = lax.rsqrt(jnp.mean(y2 * y2, axis=-1, keepdims=True) + EPS)
            xhat = y2 * r
            d = dyn_ref[:, sl].astype(F32)
            dw_ref[:, sl] += jnp.sum(d * xhat, axis=0, keepdims=True)
            dxh = d * w_ref[:, sl]
            dy2 = r * (dxh - xhat * jnp.mean(dxh * xhat, axis=-1, keepdims=True))
            dy_ref[:, sl] = (dy2 * sz).astype(BF16)
            dz_ref[:, sl] = (dy2 * yv * _silu_grad(z)).astype(BF16)

    row = pl.BlockSpec((tt, DI), lambda i: (i, 0))
    vec = pl.BlockSpec((1, DI), lambda i: (0, 0))
    shp = jax.ShapeDtypeStruct((T, DI), BF16)
    return pl.pallas_call(
        body, name=name, grid=(T // tt,), in_specs=[row, row, row, vec], out_specs=[row, row, vec],
        out_shape=[shp, shp, jax.ShapeDtypeStruct((1, DI), F32)],
        compiler_params=_cp("arbitrary"))(dyn, y, proj, w)


def exchange(srcs, gather, *, name):
    n = len(srcs)
    out_shape = [jax.ShapeDtypeStruct((N_DEV,) + (s.shape if gt else s.shape[1:]), s.dtype)
                 for s, gt in zip(srcs, gather)]

    def body(*refs):
        src_refs, out_refs = refs[:n], refs[n:2 * n]
        send_sems, recv_sems, local_sems = refs[2 * n:]
        x, y, c = lax.axis_index("x"), lax.axis_index("y"), lax.axis_index("c")
        me = 4 * x + 2 * y + c
        started = []
        for a in range(n):
            def block(p, a=a):
                return src_refs[a] if gather[a] else src_refs[a].at[p]

            local = pltpu.make_async_copy(block(me), out_refs[a].at[me], local_sems.at[a])
            local.start()
            started.append(local)
            for k in range(1, N_DEV):
                px = 1 - x if k & 4 else x
                py = 1 - y if k & 2 else y
                pc = 1 - c if k & 1 else c
                cp = pltpu.make_async_remote_copy(
                    src_ref=block(4 * px + 2 * py + pc), dst_ref=out_refs[a].at[me],
                    send_sem=send_sems.at[a, k - 1], recv_sem=recv_sems.at[a, k - 1],
                    device_id=(px, py, pc), device_id_type=pl.DeviceIdType.MESH)
                cp.start()
                started.append(cp)
        for cp in started:
            cp.wait()

    any_spec = pl.BlockSpec(memory_space=pl.ANY)
    return pl.pallas_call(
        body, name=name, in_specs=[any_spec] * n, out_specs=[any_spec] * n, out_shape=out_shape,
        scratch_shapes=[pltpu.SemaphoreType.DMA((n, N_DEV - 1)), pltpu.SemaphoreType.DMA((n, N_DEV - 1)),
                        pltpu.SemaphoreType.DMA((n,))],
        compiler_params=pltpu.CompilerParams(has_side_effects=True))(*srcs)


def sum_slots(recv, *, name):
    _, R, C = recv.shape
    tr = _pick(R, 512, 8)

    def body(r_ref, o_ref):
        acc = r_ref[0].astype(F32)
        for p in range(1, N_DEV):
            acc = acc + r_ref[p].astype(F32)
        o_ref[...] = acc

    return pl.pallas_call(
        body, name=name, grid=(R // tr,),
        in_specs=[pl.BlockSpec((N_DEV, tr, C), lambda i: (0, i, 0))],
        out_specs=pl.BlockSpec((tr, C), lambda i: (i, 0)),
        out_shape=jax.ShapeDtypeStruct((R, C), F32), compiler_params=_cp("parallel"))(recv)


def adamw(gsrc, w, m, v, *, name):
    slots, R, C = gsrc.shape
    tr = _pick(R, 256, 8)
    c1 = 1.0 / (1.0 - ADAM_B1 ** ADAM_STEP)
    c2 = 1.0 / (1.0 - ADAM_B2 ** ADAM_STEP)

    def body(g_ref, w_ref, m_ref, v_ref, go_ref, d_ref, mo_ref, vo_ref):
        g = g_ref[0].astype(F32)
        for p in range(1, slots):
            g = g + g_ref[p].astype(F32)
        m2 = ADAM_B1 * m_ref[...] + (1.0 - ADAM_B1) * g
        v2 = ADAM_B2 * v_ref[...] + (1.0 - ADAM_B2) * (g * g)
        go_ref[...] = g
        mo_ref[...] = m2
        vo_ref[...] = v2
        d_ref[...] = -ADAM_LR * ((m2 * c1) / (jnp.sqrt(v2 * c2) + ADAM_EPS) + ADAM_WD * w_ref[...])

    blk = pl.BlockSpec((tr, C), lambda i: (i, 0))
    shp = jax.ShapeDtypeStruct((R, C), F32)
    return pl.pallas_call(
        body, name=name, grid=(R // tr,),
        in_specs=[pl.BlockSpec((slots, tr, C), lambda i: (0, i, 0)), blk, blk, blk],
        out_specs=[blk] * 4, out_shape=[shp] * 4, compiler_params=_cp("parallel"))(gsrc, w, m, v)


def _pad_cols(a, n):
    return jnp.pad(a, ((0, 0), (0, n - a.shape[1])))


def _to_rows(a, B, seq, H):
    G = SSM_GROUPS
    R = H // G
    t = a[:, :H].reshape(B, seq, G, R).transpose(0, 2, 3, 1)
    t = jnp.pad(t, ((0, 0), (0, 0), (0, HEAD_ROWS - R), (0, 0)))
    return t.reshape(B, G * HEAD_ROWS, seq)


def _from_rows(a, B, seq, H):
    G = SSM_GROUPS
    R = H // G
    t = a.reshape(B, G, HEAD_ROWS, seq)[:, :, :R].transpose(0, 3, 1, 2).reshape(B * seq, H)
    return _pad_cols(t, LANE)


def local_step(x, target, p, *, B, seq):
    T, D = x.shape
    CW = D
    heads = CW // SB_HEAD_DIM
    DI = 2 * D
    H = DI // SSM_HEAD_DIM
    XW = DI + 2 * SSM_GROUPS * SSM_STATE
    in_odd = DI + XW + H
    n1p = p["od_w_in"].shape[1]
    q_off, k_off, v_off, gc_off, ga_off = 3 * CW, 4 * CW, 5 * CW, 2 * CW, 6 * CW
    dt_off = DI + XW

    n0 = rmsnorm_fwd(x, p["ev_norm_w"], name="l0_norm")
    proj0 = mm_nn(n0, p["ev_w_in"], out_dtype=BF16, name="l0_in_proj")
    (u2,) = dwconv_fwd(proj0, (0, CW), p["ev_dw_w"], p["ev_dw_b"], C=CW, seq=seq, glu=True, silu_out=False,
                       name="l0_conv")
    o, ctot = sba_fwd(proj0, B=B, seq=seq, heads=heads, q_off=q_off, k_off=k_off, v_off=v_off, name="l0_attn")
    ycat = mix0_post_fwd(u2, proj0, o, p["ev_ln_w"], p["ev_ln_b"], CW=CW, gc_off=gc_off, ga_off=ga_off,
                         name="l0_post")
    h1 = mm_nn(ycat, p["ev_w_out"], add=x, out_dtype=F32, name="l0_out_proj")

    n1 = rmsnorm_fwd(h1, p["od_norm_w"], name="l1_norm")
    proj1 = mm_nn(n1, p["od_w_in"], out_dtype=BF16, name="l1_in_proj")
    u_pre, xbc = dwconv_fwd(proj1, (DI,), p["od_conv_w"], p["od_conv_b"], C=XW, seq=seq, glu=False, silu_out=True,
                            name="l1_conv")
    bias_p, alog_p = _pad_cols(p["od_dt_bias"], LANE), _pad_cols(p["od_a_log"], LANE)
    dt, cs = ssm_dt_fwd(proj1, bias_p, alog_p, dt_off=dt_off, name="l1_dt")
    dt_row, cs_row = _to_rows(dt, B, seq, H), _to_rows(cs, B, seq, H)
    d_full = jnp.repeat(p["od_d"], SSM_HEAD_DIM, axis=1)
    y_ssd, states = ssd_fwd(xbc, dt_row, cs_row, d_full, B=B, seq=seq, DI=DI, name="l1_ssd")
    yn = gnorm_fwd(y_ssd, proj1, p["od_gnorm_w"], DI=DI, name="l1_gnorm")
    h2 = mm_nn(yn, p["od_w_out"], add=h1, out_dtype=F32, name="l1_out_proj")

    loss, dh2, g_final = final_loss(h2, p["final_norm_w"], target, name="loss_head")

    dh2b = dh2.astype(BF16)
    g_od_w_out = mm_tn(yn, dh2b, out_dtype=BF16, name="l1_dw_out")
    dyn = mm_nt_terms([(dh2b, 0, D, 0)], p["od_w_out"], out_dtype=BF16, name="l1_d_out_proj")
    dy_ssd, dz, g_gnorm = gnorm_bwd(dyn, y_ssd, proj1, p["od_gnorm_w"], DI=DI, name="l1_gnorm_bwd")
    dxbc_c, ddt_row, dcs_row, dd_part = ssd_bwd(xbc, dt_row, cs_row, d_full, states, dy_ssd, B=B, seq=seq, DI=DI,
                                                name="l1_ssd_bwd")
    g_d = dd_part.sum(axis=(0, 1)).reshape(H, SSM_HEAD_DIM).sum(axis=1)[None, :]
    draw, g_bias, g_alog = ssm_dt_bwd(_from_rows(ddt_row, B, seq, H), _from_rows(dcs_row, B, seq, H), proj1, dt,
                                      bias_p, alog_p, dt_off=dt_off, n_heads=H, name="l1_dt_bwd")
    dxbc, g_conv_w, g_conv_b = dwconv_bwd(dxbc_c, u_pre, proj1, (DI,), p["od_conv_w"], C=XW, seq=seq, glu=False,
                                          silu_out=True, name="l1_conv_bwd")
    tw = 512 if DI % 512 == 0 else LANE
    terms = [(dz, j, tw, j * tw) for j in range(DI // tw)]
    terms += [(dxbc, j, tw, DI + j * tw) for j in range(XW // tw)]
    terms += [(draw, 0, LANE, dt_off)]
    dn1 = mm_nt_terms(terms, p["od_w_in"], out_dtype=F32, name="l1_d_in_proj")
    g_od_w_in = jnp.concatenate([mm_tn(n1, dz, out_dtype=BF16, name="l1_dw_in_z"),
                                 mm_tn(n1, dxbc, out_dtype=BF16, name="l1_dw_in_xbc"),
                                 mm_tn(n1, draw, out_dtype=BF16, name="l1_dw_in_dt")], axis=1)[:, :in_odd]
    dh1, g_od_norm = rmsnorm_bwd(h1, p["od_norm_w"], dn1, dh2, name="l1_norm_bwd")

    dh1b = dh1.astype(BF16)
    g_ev_w_out = mm_tn(ycat, dh1b, out_dtype=BF16, name="l0_dw_out")
    dycat = mm_nt_terms([(dh1b, 0, D, 0)], p["ev_w_out"], out_dtype=BF16, name="l0_d_out_proj")
    du2, dgc, dga, do, g_ln_w, g_ln_b = mix0_post_bwd(dycat, u2, proj0, o, p["ev_ln_w"], p["ev_ln_b"], CW=CW,
                                                      gc_off=gc_off, ga_off=ga_off, name="l0_post_bwd")
    dq, dk, dv = sba_bwd(proj0, ctot, do, B=B, seq=seq, heads=heads, q_off=q_off, k_off=k_off, v_off=v_off,
                         name="l0_attn_bwd")
    dga_a, dga_b, g_dw_w, g_dw_b = dwconv_bwd(du2, None, proj0, (0, CW), p["ev_dw_w"], C=CW, seq=seq, glu=True,
                                              silu_out=False, name="l0_conv_bwd")
    pieces = [dga_a, dga_b, dgc, dq, dk, dv, dga]
    dn0 = mm_nt_terms([(pc, 0, CW, j * CW) for j, pc in enumerate(pieces)], p["ev_w_in"], out_dtype=F32,
                      name="l0_d_in_proj")
    g_ev_w_in = jnp.concatenate([mm_tn(n0, pc, out_dtype=BF16, name=f"l0_dw_in_{j}") for j, pc in enumerate(pieces)],
                                axis=1)
    dx, g_ev_norm = rmsnorm_bwd(x, p["ev_norm_w"], dn0, dh1, name="l0_norm_bwd")

    grads = dict(ev_norm_w=g_ev_norm, ev_w_in=g_ev_w_in, ev_dw_w=g_dw_w, ev_dw_b=g_dw_b, ev_ln_w=g_ln_w,
                 ev_ln_b=g_ln_b, ev_w_out=g_ev_w_out, od_norm_w=g_od_norm, od_w_in=g_od_w_in, od_conv_w=g_conv_w,
                 od_conv_b=g_conv_b, od_dt_bias=g_bias[:, :H], od_a_log=g_alog[:, :H], od_d=g_d, od_gnorm_w=g_gnorm,
                 od_w_out=g_od_w_out, final_norm_w=g_final)
    return loss, dx, grads


BIG = ("ev_w_in", "ev_w_out", "od_w_in", "od_w_out")
SMALL = ("ev_norm_w", "ev_dw_w", "ev_dw_b", "ev_ln_w", "ev_ln_b", "od_norm_w", "od_conv_w", "od_conv_b",
         "od_dt_bias", "od_a_log", "od_d", "od_gnorm_w", "final_norm_w")
SMALL_SHARDED = ("ev_dw_w", "od_norm_w", "od_conv_w", "od_conv_b", "od_gnorm_w")
ORDER = ("ev_norm_w", "ev_w_in", "ev_dw_w", "ev_dw_b", "ev_ln_w", "ev_ln_b", "ev_w_out", "od_norm_w", "od_w_in",
         "od_conv_w", "od_conv_b", "od_dt_bias", "od_a_log", "od_d", "od_gnorm_w", "od_w_out", "final_norm_w")


def _pack_rows(arrs, width, row_align):
    parts, spans, r0 = [], [], 0
    for a in arrs:
        flat = a.reshape(-1)
        rows = -(-flat.shape[0] // (width * row_align)) * row_align
        parts.append(jnp.pad(flat, (0, rows * width - flat.shape[0])).reshape(rows, width))
        spans.append((r0, a.size, a.shape))
        r0 += rows
    return jnp.concatenate(parts, axis=0), spans


def _unpack_rows(packed, spans):
    lead = packed.shape[:-2]
    width = packed.shape[-1]
    out = []
    for r0, size, shape in spans:
        rows = -(-size // width)
        blk = packed[..., r0:r0 + rows, :].reshape(lead + (rows * width,))[..., :size]
        out.append(blk.reshape(lead + tuple(shape)))
    return out


def _col_shards(a):
    R, C8 = a.shape
    return a.reshape(R, N_DEV, C8 // N_DEV).transpose(1, 0, 2)


def _col_unshards(a):
    n, R, C = a.shape
    return a.transpose(1, 0, 2).reshape(R, n * C)


def kernel(x, ev_norm_w, ev_w_in, ev_dw_w, ev_dw_b, ev_ln_w, ev_ln_b, ev_w_out, od_norm_w, od_w_in, od_conv_w, od_conv_b, od_dt_bias, od_a_log, od_d, od_gnorm_w, od_w_out, final_norm_w, loss_target, m_ev_norm_w, m_ev_w_in, m_ev_dw_w, m_ev_dw_b, m_ev_ln_w, m_ev_ln_b, m_ev_w_out, m_od_norm_w, m_od_w_in, m_od_conv_w, m_od_conv_b, m_od_dt_bias, m_od_a_log, m_od_d, m_od_gnorm_w, m_od_w_out, m_final_norm_w, v_ev_norm_w, v_ev_w_in, v_ev_dw_w, v_ev_dw_b, v_ev_ln_w, v_ev_ln_b, v_ev_w_out, v_od_norm_w, v_od_w_in, v_od_conv_w, v_od_conv_b, v_od_dt_bias, v_od_a_log, v_od_d, v_od_gnorm_w, v_od_w_out, v_final_norm_w):
    loc = dict(ev_norm_w=ev_norm_w, ev_w_in=ev_w_in, ev_dw_w=ev_dw_w, ev_dw_b=ev_dw_b, ev_ln_w=ev_ln_w,
               ev_ln_b=ev_ln_b, ev_w_out=ev_w_out, od_norm_w=od_norm_w, od_w_in=od_w_in, od_conv_w=od_conv_w,
               od_conv_b=od_conv_b, od_dt_bias=od_dt_bias, od_a_log=od_a_log, od_d=od_d, od_gnorm_w=od_gnorm_w,
               od_w_out=od_w_out, final_norm_w=final_norm_w)
    mom = dict(ev_norm_w=m_ev_norm_w, ev_w_in=m_ev_w_in, ev_dw_w=m_ev_dw_w, ev_dw_b=m_ev_dw_b, ev_ln_w=m_ev_ln_w,
               ev_ln_b=m_ev_ln_b, ev_w_out=m_ev_w_out, od_norm_w=m_od_norm_w, od_w_in=m_od_w_in,
               od_conv_w=m_od_conv_w, od_conv_b=m_od_conv_b, od_dt_bias=m_od_dt_bias, od_a_log=m_od_a_log,
               od_d=m_od_d, od_gnorm_w=m_od_gnorm_w, od_w_out=m_od_w_out, final_norm_w=m_final_norm_w)
    var = dict(ev_norm_w=v_ev_norm_w, ev_w_in=v_ev_w_in, ev_dw_w=v_ev_dw_w, ev_dw_b=v_ev_dw_b, ev_ln_w=v_ev_ln_w,
               ev_ln_b=v_ev_ln_b, ev_w_out=v_ev_w_out, od_norm_w=v_od_norm_w, od_w_in=v_od_w_in,
               od_conv_w=v_od_conv_w, od_conv_b=v_od_conv_b, od_dt_bias=v_od_dt_bias, od_a_log=v_od_a_log,
               od_d=v_od_d, od_gnorm_w=v_od_gnorm_w, od_w_out=v_od_w_out, final_norm_w=v_final_norm_w)
    shapes = {n: loc[n].shape for n in ORDER}
    loc = {n: (a.reshape(1, -1) if a.ndim == 1 else a.reshape(a.shape[-2:]) if a.ndim == 3 else a)
           for n, a in loc.items()}
    mom = {n: a.reshape(loc[n].shape) for n, a in mom.items()}
    var = {n: a.reshape(loc[n].shape) for n, a in var.items()}

    B, seq, D = x.shape
    me = 4 * lax.axis_index("x") + 2 * lax.axis_index("y") + lax.axis_index("c")

    big_packed, big_spans = _pack_rows([loc[n].astype(BF16) for n in BIG], D, 16)
    small_packed, small_spans = _pack_rows([loc[n] for n in SMALL_SHARDED], LANE, 8)
    big_all, small_all = exchange([big_packed, small_packed], [True, True], name="gather_weights")
    full = dict(loc)
    g_big = dict(zip(BIG, _unpack_rows(big_all, big_spans)))
    full["ev_w_in"] = _col_unshards(g_big["ev_w_in"])
    full["ev_w_out"] = g_big["ev_w_out"].reshape(-1, D)
    full["od_w_out"] = g_big["od_w_out"].reshape(-1, D)
    w1 = _col_unshards(g_big["od_w_in"])
    full["od_w_in"] = _pad_cols(w1, -(-(w1.shape[1] + LANE) // 256) * 256)
    for n, a in zip(SMALL_SHARDED, _unpack_rows(small_all, small_spans)):
        full[n] = _col_unshards(a)

    loss, dx, grads = local_step(x.reshape(B * seq, D), loss_target.reshape(B * seq, D), full, B=B, seq=seq)

    gb = dict(ev_w_in=_col_shards(grads["ev_w_in"]), ev_w_out=grads["ev_w_out"].reshape(N_DEV, -1, D),
              od_w_in=_col_shards(grads["od_w_in"]), od_w_out=grads["od_w_out"].reshape(N_DEV, -1, D))
    gbig_packed = jnp.concatenate(
        [jnp.pad(gb[n].reshape(N_DEV, -1), ((0, 0), (0, (-gb[n][0].size) % (16 * D)))).reshape(N_DEV, -1, D)
         for n in BIG], axis=1)
    gsmall_packed, gsmall_spans = _pack_rows([grads[n] for n in SMALL], LANE, 8)
    gbig_recv, gsmall_recv = exchange([gbig_packed, gsmall_packed], [False, True], name="exchange_grads")

    wb, _ = _pack_rows([loc[n] for n in BIG], D, 16)
    mb, _ = _pack_rows([mom[n] for n in BIG], D, 16)
    vb, _ = _pack_rows([var[n] for n in BIG], D, 16)
    big_out = [dict(zip(BIG, _unpack_rows(a, big_spans))) for a in adamw(gbig_recv, wb, mb, vb, name="adamw_big")]

    gsmall = dict(zip(SMALL, _unpack_rows(sum_slots(gsmall_recv, name="sum_small_grads"), gsmall_spans)))
    for n in SMALL_SHARDED:
        width = loc[n].shape[1]
        gsmall[n] = lax.dynamic_slice_in_dim(gsmall[n], me * width, width, axis=1)
    gs, sspans = _pack_rows([gsmall[n] for n in SMALL], LANE, 8)
    ws, _ = _pack_rows([loc[n] for n in SMALL], LANE, 8)
    ms, _ = _pack_rows([mom[n] for n in SMALL], LANE, 8)
    vs, _ = _pack_rows([var[n] for n in SMALL], LANE, 8)
    small_out = [dict(zip(SMALL, _unpack_rows(a, sspans))) for a in adamw(gs[None], ws, ms, vs, name="adamw_small")]

    outs = [lax.psum(loss[0, 0], ("x", "y", "c")), dx.reshape(B, seq, D)]
    for kind in range(4):
        for n in ORDER:
            src = big_out[kind] if n in BIG else small_out[kind]
            outs.append(src[n].reshape(shapes[n]))
    return tuple(outs)
```

```python
import functools

import jax
import jax.numpy as jnp
from jax import lax
from jax.experimental import pallas as pl
from jax.experimental.pallas import tpu as pltpu

F32 = jnp.float32
BF16 = jnp.bfloat16

EPS = 1e-6
N_DEV = 8
LANE = 128
VMEM_LIMIT_BYTES = 48 * 1024 * 1024

SB_HEAD_DIM = 128
CONF_KERNEL = 31
SSM_CONV = 4
SSM_HEAD_DIM = 64
SSM_GROUPS = 4
SSM_STATE = 128
SSM_CHUNK = 128
HALO = 32
HEAD_ROWS = 8
NEG_BIG = -1e30

ADAM_LR = 0.001
ADAM_B1 = 0.9
ADAM_B2 = 0.999
ADAM_EPS = 1e-08
ADAM_WD = 0.01
ADAM_STEP = 10

NT = (((1,), (1,)), ((), ()))
TN = (((0,), (0,)), ((), ()))


def _cp(*sem):
    return pltpu.CompilerParams(dimension_semantics=sem, vmem_limit_bytes=VMEM_LIMIT_BYTES)


def _pick(n, cap, align):
    if n <= cap:
        return n
    t = (cap // align) * align
    while t >= align:
        if n % t == 0:
            return t
        t -= align
    raise ValueError(f"no tile for {n} (cap {cap}, align {align})")


def _sigmoid(x):
    return 1.0 / (1.0 + jnp.exp(-x))


def _silu(x):
    return x * _sigmoid(x)


def _silu_grad(x):
    s = _sigmoid(x)
    return s * (1.0 + x * (1.0 - s))


def _dot(a, b, dims=None):
    if dims is None:
        return jnp.dot(a, b, preferred_element_type=F32)
    return lax.dot_general(a, b, dims, preferred_element_type=F32)


def _tri_dot3(tri, x):
    hi = x.astype(BF16)
    r1 = x - hi.astype(F32)
    mid = r1.astype(BF16)
    lo = (r1 - mid.astype(F32)).astype(BF16)
    return _dot(tri, hi) + _dot(tri, mid) + _dot(tri, lo)


def mm_nn(a, b, *, add=None, out_dtype, name):
    M, K = a.shape
    N = b.shape[1]
    tm = _pick(M, 1024, 16)
    tn = _pick(N, 1024, LANE)

    def body(*refs):
        if add is None:
            a_ref, b_ref, o_ref = refs
        else:
            a_ref, b_ref, add_ref, o_ref = refs
        acc = _dot(a_ref[...], b_ref[...])
        if add is not None:
            acc = acc + add_ref[...]
        o_ref[...] = acc.astype(out_dtype)

    in_specs = [pl.BlockSpec((tm, K), lambda i, j: (i, 0)), pl.BlockSpec((K, tn), lambda i, j: (0, j))]
    args = [a, b]
    if add is not None:
        in_specs.append(pl.BlockSpec((tm, tn), lambda i, j: (i, j)))
        args.append(add)
    return pl.pallas_call(
        body, name=name, grid=(M // tm, N // tn), in_specs=in_specs,
        out_specs=pl.BlockSpec((tm, tn), lambda i, j: (i, j)),
        out_shape=jax.ShapeDtypeStruct((M, N), out_dtype),
        compiler_params=_cp("parallel", "parallel"))(*args)


def mm_nt_terms(terms, b, *, out_dtype, name):
    M = terms[0][0].shape[0]
    N = b.shape[0]
    tm = _pick(M, 512, 16)
    tn = _pick(N, 512, LANE)
    n_terms = len(terms)

    def body(*refs):
        o_ref = refs[-1]
        acc = None
        for t in range(n_terms):
            part = _dot(refs[2 * t][...], refs[2 * t + 1][...], NT)
            acc = part if acc is None else acc + part
        o_ref[...] = acc.astype(out_dtype)

    in_specs, args = [], []
    for arr, cb, w, off in terms:
        assert off % w == 0
        in_specs.append(pl.BlockSpec((tm, w), lambda i, j, cb=cb: (i, cb)))
        in_specs.append(pl.BlockSpec((tn, w), lambda i, j, ob=off // w: (j, ob)))
        args += [arr, b]
    return pl.pallas_call(
        body, name=name, grid=(M // tm, N // tn), in_specs=in_specs,
        out_specs=pl.BlockSpec((tm, tn), lambda i, j: (i, j)),
        out_shape=jax.ShapeDtypeStruct((M, N), out_dtype),
        compiler_params=_cp("parallel", "parallel"))(*args)


def mm_nn_terms(terms, b, *, out_dtype, name):
    M = terms[0][0].shape[0]
    N = b.shape[1]
    tm = _pick(M, 512, 16)
    tn = _pick(N, 512, LANE)
    n_terms = len(terms)

    def body(*refs):
        o_ref = refs[-1]
        acc = None
        for t in range(n_terms):
            part = _dot(refs[2 * t][...], refs[2 * t + 1][...])
            acc = part if acc is None else acc + part
        o_ref[...] = acc.astype(out_dtype)

    in_specs, args = [], []
    for arr, cb, w, off in terms:
        assert off % w == 0
        in_specs.append(pl.BlockSpec((tm, w), lambda i, j, cb=cb: (i, cb)))
        in_specs.append(pl.BlockSpec((w, tn), lambda i, j, ob=off // w: (ob, j)))
        args += [arr, b]
    return pl.pallas_call(
        body, name=name, grid=(M // tm, N // tn), in_specs=in_specs,
        out_specs=pl.BlockSpec((tm, tn), lambda i, j: (i, j)),
        out_shape=jax.ShapeDtypeStruct((M, N), out_dtype),
        compiler_params=_cp("parallel", "parallel"))(*args)


def mm_tn(a, b, *, out_dtype, name):
    T, M = a.shape
    N = b.shape[1]
    tm = _pick(M, 1024, LANE)
    tn = _pick(N, 1024, LANE)
    tk = _pick(T, 1024, 16)
    nk = T // tk

    def body(a_ref, b_ref, o_ref, acc_ref):
        k = pl.program_id(2)

        @pl.when(k == 0)
        def _():
            acc_ref[...] = jnp.zeros_like(acc_ref)

        acc_ref[...] += _dot(a_ref[...], b_ref[...], TN)

        @pl.when(k == nk - 1)
        def _():
            o_ref[...] = acc_ref[...].astype(out_dtype)

    return pl.pallas_call(
        body, name=name, grid=(M // tm, N // tn, nk),
        in_specs=[pl.BlockSpec((tk, tm), lambda i, j, k: (k, i)), pl.BlockSpec((tk, tn), lambda i, j, k: (k, j))],
        out_specs=pl.BlockSpec((tm, tn), lambda i, j, k: (i, j)),
        out_shape=jax.ShapeDtypeStruct((M, N), out_dtype),
        scratch_shapes=[pltpu.VMEM((tm, tn), F32)],
        compiler_params=_cp("parallel", "parallel", "arbitrary"))(a, b)


def rmsnorm_fwd(h, w, *, name):
    T, D = h.shape
    tt = _pick(T, 512, 16)

    def body(h_ref, w_ref, n_ref):
        x = h_ref[...]
        r = lax.rsqrt(jnp.mean(x * x, axis=-1, keepdims=True) + EPS)
        n_ref[...] = (x * r * w_ref[...]).astype(BF16)

    return pl.pallas_call(
        body, name=name, grid=(T // tt,),
        in_specs=[pl.BlockSpec((tt, D), lambda i: (i, 0)), pl.BlockSpec((1, D), lambda i: (0, 0))],
        out_specs=pl.BlockSpec((tt, D), lambda i: (i, 0)),
        out_shape=jax.ShapeDtypeStruct((T, D), BF16),
        compiler_params=_cp("parallel"))(h, w)


def rmsnorm_bwd(h, w, dn, dres, *, name):
    T, D = h.shape
    tt = _pick(T, 512, 16)

    def body(h_ref, w_ref, dn_ref, dres_ref, dh_ref, gw_ref):
        @pl.when(pl.program_id(0) == 0)
        def _():
            gw_ref[...] = jnp.zeros_like(gw_ref)

        x = h_ref[...]
        r = lax.rsqrt(jnp.mean(x * x, axis=-1, keepdims=True) + EPS)
        xhat = x * r
        g = dn_ref[...].astype(F32)
        gw_ref[...] += jnp.sum(g * xhat, axis=0, keepdims=True)
        dxh = g * w_ref[...]
        dx = r * (dxh - xhat * jnp.mean(dxh * xhat, axis=-1, keepdims=True))
        dh_ref[...] = dres_ref[...] + dx

    row = pl.BlockSpec((tt, D), lambda i: (i, 0))
    vec = pl.BlockSpec((1, D), lambda i: (0, 0))
    return pl.pallas_call(
        body, name=name, grid=(T // tt,), in_specs=[row, vec, row, row], out_specs=[row, vec],
        out_shape=[jax.ShapeDtypeStruct((T, D), F32), jax.ShapeDtypeStruct((1, D), F32)],
        compiler_params=_cp("arbitrary"))(h, w, dn, dres)


def final_loss(h, w, target, *, name):
    T, D = h.shape
    tt = _pick(T, 512, 16)

    def body(h_ref, w_ref, t_ref, loss_ref, dh_ref, gw_ref):
        @pl.when(pl.program_id(0) == 0)
        def _():
            gw_ref[...] = jnp.zeros_like(gw_ref)
            loss_ref[...] = jnp.zeros_like(loss_ref)

        x = h_ref[...]
        r = lax.rsqrt(jnp.mean(x * x, axis=-1, keepdims=True) + EPS)
        xhat = x * r
        e = xhat * w_ref[...] - t_ref[...]
        loss_ref[...] += jnp.sum(e * e) * (0.5 / D)
        g = e * (1.0 / D)
        gw_ref[...] += jnp.sum(g * xhat, axis=0, keepdims=True)
        dxh = g * w_ref[...]
        dh_ref[...] = r * (dxh - xhat * jnp.mean(dxh * xhat, axis=-1, keepdims=True))

    row = pl.BlockSpec((tt, D), lambda i: (i, 0))
    vec = pl.BlockSpec((1, D), lambda i: (0, 0))
    one = pl.BlockSpec((1, LANE), lambda i: (0, 0))
    return pl.pallas_call(
        body, name=name, grid=(T // tt,), in_specs=[row, vec, row], out_specs=[one, row, vec],
        out_shape=[jax.ShapeDtypeStruct((1, LANE), F32), jax.ShapeDtypeStruct((T, D), F32),
                   jax.ShapeDtypeStruct((1, D), F32)],
        compiler_params=_cp("arbitrary"))(h, w, target)


def _conv_tiles(T, C):
    return _pick(T, 256, HALO), _pick(C, 512, LANE)


def dwconv_fwd(src, offs, w, b, *, C, seq, glu, silu_out, name):
    T = src.shape[0]
    K = w.shape[0]
    tt, tc = _conv_tiles(T, C)
    n_in = 2 if glu else 1
    per = tt // HALO

    def body(*refs):
        cur = refs[0:2 * n_in:2]
        halo = refs[1:2 * n_in:2]
        w_ref, b_ref = refs[2 * n_in], refs[2 * n_in + 1]
        outs = refs[2 * n_in + 2:-1]
        buf = refs[-1]
        i = pl.program_id(1)
        first = (i * tt) % seq == 0

        def pre(rs):
            v = rs[0][...].astype(F32)
            return v * _sigmoid(rs[1][...].astype(F32)) if glu else v

        buf[0:HALO, :] = jnp.where(first, 0.0, pre(halo))
        buf[HALO:HALO + tt, :] = pre(cur)
        acc = jnp.broadcast_to(b_ref[...], (tt, tc))
        for k in range(K):
            acc = acc + w_ref[k:k + 1, :] * buf[pl.ds(HALO - (K - 1) + k, tt), :]
        outs[0][...] = acc.astype(BF16)
        if silu_out:
            outs[1][...] = _silu(acc).astype(BF16)

    in_specs, args = [], []
    for off in offs:
        assert off % tc == 0
        in_specs.append(pl.BlockSpec((tt, tc), lambda j, i, ob=off // tc: (i, ob + j)))
        in_specs.append(pl.BlockSpec((HALO, tc), lambda j, i, ob=off // tc: (jnp.maximum(i * per - 1, 0), ob + j)))
        args += [src, src]
    in_specs += [pl.BlockSpec((K, tc), lambda j, i: (0, j)), pl.BlockSpec((1, tc), lambda j, i: (0, j))]
    args += [w, b]
    n_out = 2 if silu_out else 1
    out = pl.pallas_call(
        body, name=name, grid=(C // tc, T // tt), in_specs=in_specs,
        out_specs=[pl.BlockSpec((tt, tc), lambda j, i: (i, j))] * n_out,
        out_shape=[jax.ShapeDtypeStruct((T, C), BF16)] * n_out,
        scratch_shapes=[pltpu.VMEM((HALO + tt, tc), F32)],
        compiler_params=_cp("parallel", "arbitrary"))(*args)
    return out


def dwconv_bwd(du, u, src, offs, w, *, C, seq, glu, silu_out, name):
    T = src.shape[0]
    K = w.shape[0]
    tt, tc = _conv_tiles(T, C)
    n_in = 2 if glu else 1
    per = tt // HALO
    last_blk = T // HALO - 1

    def body(*refs):
        pos = 0
        du_cur, du_nxt = refs[0], refs[1]
        pos = 2
        if silu_out:
            u_cur, u_nxt = refs[2], refs[3]
            pos = 4
        cur = refs[pos:pos + 2 * n_in:2]
        halo = refs[pos + 1:pos + 2 * n_in:2]
        pos += 2 * n_in
        w_ref = refs[pos]
        outs = refs[pos + 1:pos + 1 + n_in]
        dw_ref, db_ref = refs[pos + 1 + n_in], refs[pos + 2 + n_in]
        gbuf, xbuf = refs[-2], refs[-1]
        i = pl.program_id(1)
        first = (i * tt) % seq == 0
        last = ((i + 1) * tt) % seq == 0

        @pl.when(i == 0)
        def _():
            dw_ref[...] = jnp.zeros_like(dw_ref)
            db_ref[...] = jnp.zeros_like(db_ref)

        g_cur = du_cur[...].astype(F32)
        g_nxt = du_nxt[...].astype(F32)
        if silu_out:
            g_cur = g_cur * _silu_grad(u_cur[...].astype(F32))
            g_nxt = g_nxt * _silu_grad(u_nxt[...].astype(F32))
        gbuf[0:tt, :] = g_cur
        gbuf[tt:tt + HALO, :] = jnp.where(last, 0.0, g_nxt)

        def pre(rs):
            v = rs[0][...].astype(F32)
            return v * _sigmoid(rs[1][...].astype(F32)) if glu else v

        xbuf[0:HALO, :] = jnp.where(first, 0.0, pre(halo))
        xbuf[HALO:HALO + tt, :] = pre(cur)

        dx = jnp.zeros((tt, tc), F32)
        for k in range(K):
            dx = dx + w_ref[k:k + 1, :] * gbuf[pl.ds(K - 1 - k, tt), :]
            dw_ref[k:k + 1, :] += jnp.sum(g_cur * xbuf[pl.ds(HALO - (K - 1) + k, tt), :], axis=0, keepdims=True)
        db_ref[...] += jnp.sum(g_cur, axis=0, keepdims=True)
        if glu:
            a = cur[0][...].astype(F32)
            s = _sigmoid(cur[1][...].astype(F32))
            outs[0][...] = (dx * s).astype(BF16)
            outs[1][...] = (dx * a * s * (1.0 - s)).astype(BF16)
        else:
            outs[0][...] = dx.astype(BF16)

    def cur_spec(ob):
        return pl.BlockSpec((tt, tc), lambda j, i: (i, ob + j))

    def nxt_spec(ob):
        return pl.BlockSpec((HALO, tc), lambda j, i: (jnp.minimum((i + 1) * per, last_blk), ob + j))

    def prv_spec(ob):
        return pl.BlockSpec((HALO, tc), lambda j, i: (jnp.maximum(i * per - 1, 0), ob + j))

    in_specs = [cur_spec(0), nxt_spec(0)]
    args = [du, du]
    if silu_out:
        in_specs += [cur_spec(0), nxt_spec(0)]
        args += [u, u]
    for off in offs:
        assert off % tc == 0
        in_specs += [cur_spec(off // tc), prv_spec(off // tc)]
        args += [src, src]
    in_specs.append(pl.BlockSpec((K, tc), lambda j, i: (0, j)))
    args.append(w)
    out_specs = [pl.BlockSpec((tt, tc), lambda j, i: (i, j))] * n_in
    out_specs += [pl.BlockSpec((K, tc), lambda j, i: (0, j)), pl.BlockSpec((1, tc), lambda j, i: (0, j))]
    out_shape = [jax.ShapeDtypeStruct((T, C), BF16)] * n_in
    out_shape += [jax.ShapeDtypeStruct((K, C), F32), jax.ShapeDtypeStruct((1, C), F32)]
    return pl.pallas_call(
        body, name=name, grid=(C // tc, T // tt), in_specs=in_specs, out_specs=out_specs, out_shape=out_shape,
        scratch_shapes=[pltpu.VMEM((tt + HALO, tc), F32), pltpu.VMEM((HALO + tt, tc), F32)],
        compiler_params=_cp("parallel", "arbitrary"))(*args)


def mix0_post_fwd(u2, proj, o, ln_w, ln_b, *, CW, gc_off, ga_off, name):
    T = u2.shape[0]
    tt = _pick(T, 256, 16)

    def body(u_ref, gc_ref, ga_ref, o_ref, lw_ref, lb_ref, y_ref):
        u = u_ref[...].astype(F32)
        mu = jnp.mean(u, axis=-1, keepdims=True)
        xc = u - mu
        r = lax.rsqrt(jnp.mean(xc * xc, axis=-1, keepdims=True) + EPS)
        u3 = xc * r * lw_ref[...] + lb_ref[...]
        y_ref[:, 0:CW] = (_silu(u3) * _silu(gc_ref[...].astype(F32))).astype(BF16)
        y_ref[:, CW:2 * CW] = (o_ref[...].astype(F32) * _silu(ga_ref[...].astype(F32))).astype(BF16)

    row = pl.BlockSpec((tt, CW), lambda i: (i, 0))
    vec = pl.BlockSpec((1, CW), lambda i: (0, 0))
    return pl.pallas_call(
        body, name=name, grid=(T // tt,),
        in_specs=[row, pl.BlockSpec((tt, CW), lambda i: (i, gc_off // CW)),
                  pl.BlockSpec((tt, CW), lambda i: (i, ga_off // CW)), row, vec, vec],
        out_specs=pl.BlockSpec((tt, 2 * CW), lambda i: (i, 0)),
        out_shape=jax.ShapeDtypeStruct((T, 2 * CW), BF16),
        compiler_params=_cp("parallel"))(u2, proj, proj, o, ln_w, ln_b)


def mix0_post_bwd(dy, u2, proj, o, ln_w, ln_b, *, CW, gc_off, ga_off, name):
    T = u2.shape[0]
    tt = _pick(T, 256, 16)

    def body(dy_ref, u_ref, gc_ref, ga_ref, o_ref, lw_ref, lb_ref, du_ref, dgc_ref, dga_ref, do_ref, dlw_ref, dlb_ref):
        @pl.when(pl.program_id(0) == 0)
        def _():
            dlw_ref[...] = jnp.zeros_like(dlw_ref)
            dlb_ref[...] = jnp.zeros_like(dlb_ref)

        dyc = dy_ref[:, 0:CW].astype(F32)
        dya = dy_ref[:, CW:2 * CW].astype(F32)
        u = u_ref[...].astype(F32)
        mu = jnp.mean(u, axis=-1, keepdims=True)
        xc = u - mu
        r = lax.rsqrt(jnp.mean(xc * xc, axis=-1, keepdims=True) + EPS)
        xhat = xc * r
        u3 = xhat * lw_ref[...] + lb_ref[...]
        gc = gc_ref[...].astype(F32)
        dgc_ref[...] = (dyc * _silu(u3) * _silu_grad(gc)).astype(BF16)
        du3 = dyc * _silu(gc) * _silu_grad(u3)
        dlw_ref[...] += jnp.sum(du3 * xhat, axis=0, keepdims=True)
        dlb_ref[...] += jnp.sum(du3, axis=0, keepdims=True)
        dxh = du3 * lw_ref[...]
        du = r * (dxh - jnp.mean(dxh, axis=-1, keepdims=True) - xhat * jnp.mean(dxh * xhat, axis=-1, keepdims=True))
        du_ref[...] = du.astype(BF16)
        ga = ga_ref[...].astype(F32)
        ov = o_ref[...].astype(F32)
        do_ref[...] = (dya * _silu(ga)).astype(BF16)
        dga_ref[...] = (dya * ov * _silu_grad(ga)).astype(BF16)

    row = pl.BlockSpec((tt, CW), lambda i: (i, 0))
    vec = pl.BlockSpec((1, CW), lambda i: (0, 0))
    big = jax.ShapeDtypeStruct((T, CW), BF16)
    small = jax.ShapeDtypeStruct((1, CW), F32)
    return pl.pallas_call(
        body, name=name, grid=(T // tt,),
        in_specs=[pl.BlockSpec((tt, 2 * CW), lambda i: (i, 0)), row,
                  pl.BlockSpec((tt, CW), lambda i: (i, gc_off // CW)),
                  pl.BlockSpec((tt, CW), lambda i: (i, ga_off // CW)), row, vec, vec],
        out_specs=[row, row, row, row, vec, vec],
        out_shape=[big, big, big, big, small, small],
        compiler_params=_cp("arbitrary"))(dy, u2, proj, proj, o, ln_w, ln_b)


SB_UNDERFLOW = 110.0
SB_BOUND_MARGIN = 1.02


def _sb_tile(seq):
    return _pick(seq, 256, LANE)


def _softplus(z):
    return jnp.maximum(z, 0.0) + jnp.log(1.0 + jnp.exp(-jnp.abs(z)))


def _tri01(n, lower):
    i = lax.broadcasted_iota(jnp.int32, (n, n), 0)
    j = lax.broadcasted_iota(jnp.int32, (n, n), 1)
    return ((i >= j) if lower else (i <= j)).astype(BF16)


def _sb_heads_per_step(heads):
    return 2 if heads % 2 == 0 else 1


def sba_fwd(proj, *, B, seq, heads, q_off, k_off, v_off, name):
    dh = SB_HEAD_DIM
    tq = _sb_tile(seq)
    nq = seq // tq
    hps = _sb_heads_per_step(heads)
    hw = hps * dh
    scale = dh ** -0.5

    def body(q_ref, k_ref, v_ref, o_ref, ct_ref, acc_ref, kmax_ref):
        qi = pl.program_id(1)
        tri = _tri01(tq, True)
        below = lax.broadcasted_iota(jnp.int32, (tq, tq), 1) < lax.broadcasted_iota(jnp.int32, (tq, tq), 0)
        qs = [(q_ref[:, h * dh:(h + 1) * dh].astype(F32) * scale).astype(BF16) for h in range(hps)]

        @pl.when(qi == 0)
        def _():
            def chunk(i, best):
                rows = k_ref[pl.ds(pl.multiple_of(i * tq, tq), tq), :].astype(F32)
                sq = rows * rows
                return tuple(jnp.maximum(best[h], jnp.max(jnp.sum(sq[:, h * dh:(h + 1) * dh], axis=1, keepdims=True),
                                                          axis=0, keepdims=True)) for h in range(hps))

            best = lax.fori_loop(0, nq, chunk, (jnp.zeros((1, 1), F32),) * hps)
            for h in range(hps):
                kmax_ref[h] = jnp.broadcast_to(jnp.sqrt(best[h]), (8, LANE))

        z_bound = [jnp.sqrt(jnp.sum(qs[h].astype(F32) ** 2, axis=1, keepdims=True))
                   * (SB_BOUND_MARGIN * jnp.max(kmax_ref[h], keepdims=True)) for h in range(hps)]

        def block(start, rs, diag):
            pvs, out = [], []
            for h in range(hps):
                k_blk = k_ref[pl.ds(start, tq), h * dh:(h + 1) * dh]
                v_blk = v_ref[pl.ds(start, tq), h * dh:(h + 1) * dh]
                z = _dot(qs[h], k_blk, NT)
                sp = _softplus(z)
                if diag:
                    sp = jnp.where(below, sp, 0.0)
                wts = jnp.exp(z - (_dot(sp.astype(BF16), tri) + rs[h]))
                if diag:
                    wts = jnp.where(below, wts, 0.0)
                pvs.append(_dot(wts.astype(BF16), v_blk))
                out.append(rs[h] + jnp.sum(sp, axis=-1, keepdims=True))
            return pvs, tuple(out)

        zero = jnp.zeros((tq, 1), F32)
        pvs, rs = block(pl.multiple_of(qi * tq, tq), (zero,) * hps, True)
        for h in range(hps):
            acc_ref[:, h * dh:(h + 1) * dh] = pvs[h]

        def more(c):
            j, rs = c
            slack = rs[0] - z_bound[0]
            for h in range(1, hps):
                slack = jnp.minimum(slack, rs[h] - z_bound[h])
            return jnp.logical_and(j < qi, jnp.min(slack) <= SB_UNDERFLOW)

        def step(c):
            j, rs = c
            pvs, rs = block(pl.multiple_of((qi - 1 - j) * tq, tq), rs, False)
            for h in range(hps):
                acc_ref[:, h * dh:(h + 1) * dh] += pvs[h]
            return j + 1, rs

        n_left, totals = lax.while_loop(more, step, (jnp.int32(0), rs))
        o_ref[...] = acc_ref[...].astype(BF16)
        for h in range(hps):
            ct_ref[0, 8 * h:8 * h + 8, :] = jnp.broadcast_to(totals[h], (tq, LANE)).T[0:8, :]
        ct_ref[0, 8 * hps:8 * hps + 8, :] = jnp.full((8, tq), n_left, F32)

    qb, kb, vb = q_off // hw, k_off // hw, v_off // hw
    G = heads // hps
    return pl.pallas_call(
        body, name=name, grid=(B * G, nq),
        in_specs=[pl.BlockSpec((tq, hw), lambda g, i: ((g // G) * nq + i, qb + g % G)),
                  pl.BlockSpec((seq, hw), lambda g, i: (g // G, kb + g % G)),
                  pl.BlockSpec((seq, hw), lambda g, i: (g // G, vb + g % G))],
        out_specs=[pl.BlockSpec((tq, hw), lambda g, i: ((g // G) * nq + i, g % G)),
                   pl.BlockSpec((1, 8 * hps + 8, tq), lambda g, i: (g * nq + i, 0, 0))],
        out_shape=[jax.ShapeDtypeStruct((B * seq, heads * dh), BF16),
                   jax.ShapeDtypeStruct((B * G * nq, 8 * hps + 8, tq), F32)],
        scratch_shapes=[pltpu.VMEM((tq, hw), F32), pltpu.VMEM((hps, 8, LANE), F32)],
        compiler_params=_cp("parallel", "arbitrary"))(proj, proj, proj)


def sba_bwd(proj, ctot, do, *, B, seq, heads, q_off, k_off, v_off, name):
    dh = SB_HEAD_DIM
    tq = _sb_tile(seq)
    nq = seq // tq
    hps = _sb_heads_per_step(heads)
    hw = hps * dh
    scale = dh ** -0.5

    def body(q_ref, k_ref, v_ref, ct_ref, do_ref, dq_ref, dk_ref, dv_ref, dq_acc, dk_acc, dv_acc):
        qi = pl.program_id(1)

        @pl.when(qi == 0)
        def _():
            dk_acc[...] = jnp.zeros_like(dk_acc)
            dv_acc[...] = jnp.zeros_like(dv_acc)

        tri_sfx = _tri01(tq, True)
        tri_pre = _tri01(tq, False)
        below = lax.broadcasted_iota(jnp.int32, (tq, tq), 1) < lax.broadcasted_iota(jnp.int32, (tq, tq), 0)
        qs = [(q_ref[:, h * dh:(h + 1) * dh].astype(F32) * scale).astype(BF16) for h in range(hps)]
        dos = [do_ref[:, h * dh:(h + 1) * dh] for h in range(hps)]
        totals = [jnp.max(jnp.broadcast_to(ct_ref[0, 8 * h:8 * h + 1, :], (LANE, tq)).T, axis=1, keepdims=True)
                  for h in range(hps)]
        dq_acc[...] = jnp.zeros_like(dq_acc)

        def block(start, carry, diag):
            out = []
            for h in range(hps):
                pc, pg = carry[h]
                cols = slice(h * dh, (h + 1) * dh)
                k_blk = k_ref[pl.ds(start, tq), cols]
                v_blk = v_ref[pl.ds(start, tq), cols]
                z = _dot(qs[h], k_blk, NT)
                sp = _softplus(z)
                sig = jnp.exp(z - sp)
                if diag:
                    sp = jnp.where(below, sp, 0.0)
                pc_next = pc + jnp.sum(sp, axis=-1, keepdims=True)
                wts = jnp.exp(z - (_dot(sp.astype(BF16), tri_sfx) + (totals[h] - pc_next)))
                if diag:
                    wts = jnp.where(below, wts, 0.0)
                g = _dot(dos[h], v_blk, NT) * wts
                dz = g - sig * (_dot(g.astype(BF16), tri_pre) + pg)
                if diag:
                    dz = jnp.where(below, dz, 0.0)
                dz = dz.astype(BF16)
                dq_acc[:, cols] += _dot(dz, k_blk)
                dk_acc[pl.ds(start, tq), cols] += _dot(dz, qs[h], TN)
                dv_acc[pl.ds(start, tq), cols] += _dot(wts.astype(BF16), dos[h], TN)
                out.append((pc_next, pg + jnp.sum(g, axis=-1, keepdims=True)))
            return tuple(out)

        zero = jnp.zeros((tq, 1), F32)
        n_left = jnp.max(ct_ref[0, 8 * hps:8 * hps + 8, :]).astype(jnp.int32)
        carry = lax.fori_loop(qi - n_left, qi, lambda j, c: block(pl.multiple_of(j * tq, tq), c, False),
                              ((zero, zero),) * hps)
        block(pl.multiple_of(qi * tq, tq), carry, True)
        dq_ref[...] = (dq_acc[...] * scale).astype(BF16)

        @pl.when(qi == nq - 1)
        def _():
            dk_ref[...] = dk_acc[...].astype(BF16)
            dv_ref[...] = dv_acc[...].astype(BF16)

    qb, kb, vb = q_off // hw, k_off // hw, v_off // hw
    G = heads // hps
    q_spec = pl.BlockSpec((tq, hw), lambda g, i: ((g // G) * nq + i, qb + g % G))
    o_spec = pl.BlockSpec((tq, hw), lambda g, i: ((g // G) * nq + i, g % G))
    kv_out = pl.BlockSpec((seq, hw), lambda g, i: (g // G, g % G))
    shp = jax.ShapeDtypeStruct((B * seq, heads * dh), BF16)
    return pl.pallas_call(
        body, name=name, grid=(B * G, nq),
        in_specs=[q_spec,
                  pl.BlockSpec((seq, hw), lambda g, i: (g // G, kb + g % G)),
                  pl.BlockSpec((seq, hw), lambda g, i: (g // G, vb + g % G)),
                  pl.BlockSpec((1, 8 * hps + 8, tq), lambda g, i: (g * nq + i, 0, 0)), o_spec],
        out_specs=[o_spec, kv_out, kv_out], out_shape=[shp, shp, shp],
        scratch_shapes=[pltpu.VMEM((tq, hw), F32), pltpu.VMEM((seq, hw), F32), pltpu.VMEM((seq, hw), F32)],
        compiler_params=_cp("parallel", "arbitrary"))(proj, proj, proj, ctot, do)


def _chunk_tri(lower):
    i = lax.broadcasted_iota(jnp.int32, (SSM_CHUNK, SSM_CHUNK), 0)
    j = lax.broadcasted_iota(jnp.int32, (SSM_CHUNK, SSM_CHUNK), 1)
    return ((i >= j) if lower else (i <= j)).astype(BF16)


def ssm_dt_fwd(proj, bias, a_log, *, dt_off, name):
    T = proj.shape[0]
    L = SSM_CHUNK
    tt = _pick(T, 512, L)

    def body(raw_ref, bias_ref, al_ref, dt_ref, cs_ref):
        x = raw_ref[...].astype(F32) + bias_ref[...]
        dt = jnp.maximum(x, 0.0) + jnp.log(1.0 + jnp.exp(-jnp.abs(x)))
        dt_ref[...] = dt
        la = dt * (-jnp.exp(al_ref[...]))
        tri = _chunk_tri(True)
        for c in range(tt // L):
            cs_ref[c * L:(c + 1) * L, :] = _tri_dot3(tri, la[c * L:(c + 1) * L, :])

    row = pl.BlockSpec((tt, LANE), lambda i: (i, 0))
    vec = pl.BlockSpec((1, LANE), lambda i: (0, 0))
    shp = jax.ShapeDtypeStruct((T, LANE), F32)
    return pl.pallas_call(
        body, name=name, grid=(T // tt,),
        in_specs=[pl.BlockSpec((tt, LANE), lambda i: (i, dt_off // LANE)), vec, vec],
        out_specs=[row, row], out_shape=[shp, shp], compiler_params=_cp("parallel"))(proj, bias, a_log)


def ssm_dt_bwd(ddt, dcs, proj, dt, bias, a_log, *, dt_off, n_heads, name):
    T = proj.shape[0]
    L = SSM_CHUNK
    tt = _pick(T, 512, L)

    def body(ddt_ref, dcs_ref, raw_ref, dt_ref, bias_ref, al_ref, draw_ref, dbias_ref, dal_ref, dla_buf):
        @pl.when(pl.program_id(0) == 0)
        def _():
            dbias_ref[...] = jnp.zeros_like(dbias_ref)
            dal_ref[...] = jnp.zeros_like(dal_ref)

        triu = _chunk_tri(False)
        dcs = dcs_ref[...]
        for c in range(tt // L):
            dla_buf[c * L:(c + 1) * L, :] = _tri_dot3(triu, dcs[c * L:(c + 1) * L, :])
        dla = dla_buf[...]
        a = -jnp.exp(al_ref[...])
        dtv = dt_ref[...]
        valid = lax.broadcasted_iota(jnp.int32, (tt, LANE), 1) < n_heads
        dal_ref[...] += jnp.sum(jnp.where(valid, dla * dtv, 0.0), axis=0, keepdims=True) * a
        x = raw_ref[...].astype(F32) + bias_ref[...]
        draw = jnp.where(valid, (ddt_ref[...] + dla * a) * _sigmoid(x), 0.0)
        dbias_ref[...] += jnp.sum(draw, axis=0, keepdims=True)
        draw_ref[...] = draw.astype(BF16)

    row = pl.BlockSpec((tt, LANE), lambda i: (i, 0))
    vec = pl.BlockSpec((1, LANE), lambda i: (0, 0))
    return pl.pallas_call(
        body, name=name, grid=(T // tt,),
        in_specs=[row, row, pl.BlockSpec((tt, LANE), lambda i: (i, dt_off // LANE)), row, vec, vec],
        out_specs=[row, vec, vec],
        out_shape=[jax.ShapeDtypeStruct((T, LANE), BF16), jax.ShapeDtypeStruct((1, LANE), F32),
                   jax.ShapeDtypeStruct((1, LANE), F32)],
        scratch_shapes=[pltpu.VMEM((tt, LANE), F32)],
        compiler_params=_cp("arbitrary"))(ddt, dcs, proj, dt, bias, a_log)


def _colb(row):
    return jnp.broadcast_to(row, (LANE, SSM_CHUNK)).T


def _ssd_pair_common(x_ref, dt_ref, cs_ref, pair, ppg, lo_half, causal, lane_row):
    L = SSM_CHUNK
    g, pp = divmod(pair, ppg)
    ra = g * HEAD_ROWS + 2 * pp
    X = x_ref[:, pair * LANE:(pair + 1) * LANE].astype(F32)
    dta, dtb = dt_ref[0, ra:ra + 1, :], dt_ref[0, ra + 1:ra + 2, :]
    csa, csb = cs_ref[0, ra:ra + 1, :], cs_ref[0, ra + 1:ra + 2, :]
    csa_c, csb_c = _colb(csa), _colb(csb)
    dt_p = jnp.where(lo_half, _colb(dta), _colb(dtb))
    La = jnp.exp(jnp.where(causal, csa_c - csa, NEG_BIG))
    Lb = jnp.exp(jnp.where(causal, csb_c - csb, NEG_BIG))
    last_a = jnp.sum(jnp.where(lane_row == L - 1, csa, 0.0), axis=1, keepdims=True)
    last_b = jnp.sum(jnp.where(lane_row == L - 1, csb, 0.0), axis=1, keepdims=True)
    ecs = jnp.exp(jnp.where(lo_half, csa_c, csb_c))
    tail = jnp.exp(jnp.where(lo_half, last_a - csa_c, last_b - csb_c))
    return g, ra, X, dt_p, La, Lb, last_a, last_b, ecs, tail


def ssd_fwd(xbc, dt_row, cs_row, d_full, *, B, seq, DI, name):
    L, N, G = SSM_CHUNK, SSM_STATE, SSM_GROUPS
    nc = seq // L
    XW = xbc.shape[1]
    n_pairs = DI // LANE
    ppg = n_pairs // G

    def body(x_ref, dt_ref, cs_ref, d_ref, y_ref, st_ref, state):
        c = pl.program_id(1)

        @pl.when(c == 0)
        def _():
            state[...] = jnp.zeros_like(state)

        causal = lax.broadcasted_iota(jnp.int32, (L, L), 0) >= lax.broadcasted_iota(jnp.int32, (L, L), 1)
        lo_half = lax.broadcasted_iota(jnp.int32, (L, LANE), 1) < SSM_HEAD_DIM
        rows_lo = lax.broadcasted_iota(jnp.int32, (LANE, N), 0) < SSM_HEAD_DIM
        lane_row = lax.broadcasted_iota(jnp.int32, (1, L), 1)
        cbs = []
        for g in range(G):
            Bc = x_ref[:, DI + g * N:DI + (g + 1) * N]
            Cc = x_ref[:, DI + G * N + g * N:DI + G * N + (g + 1) * N]
            cbs.append((Bc, Cc, _dot(Cc, Bc, NT)))
        for pair in range(n_pairs):
            g, ra, X, dt_p, La, Lb, last_a, last_b, ecs, tail = _ssd_pair_common(
                x_ref, dt_ref, cs_ref, pair, ppg, lo_half, causal, lane_row)
            Bc, Cc, CB = cbs[g]
            xs = X * dt_p
            xsb = xs.astype(BF16)
            y = jnp.where(lo_half, _dot((CB * La).astype(BF16), xsb), _dot((CB * Lb).astype(BF16), xsb))
            S = state[pair]
            st_ref[0, 0, pair] = S
            y = y + ecs * _dot(Cc, S.astype(BF16), NT)
            y = y + d_ref[:, pair * LANE:(pair + 1) * LANE] * X
            y_ref[:, pair * LANE:(pair + 1) * LANE] = y.astype(BF16)
            e_rows = jnp.where(rows_lo, jnp.exp(last_a), jnp.exp(last_b))
            state[pair] = e_rows * S + _dot((xs * tail).astype(BF16), Bc, TN)

    return pl.pallas_call(
        body, name=name, grid=(B, nc),
        in_specs=[pl.BlockSpec((L, XW), lambda b, c: (b * nc + c, 0)),
                  pl.BlockSpec((1, G * HEAD_ROWS, L), lambda b, c: (b, 0, c)),
                  pl.BlockSpec((1, G * HEAD_ROWS, L), lambda b, c: (b, 0, c)),
                  pl.BlockSpec((1, DI), lambda b, c: (0, 0))],
        out_specs=[pl.BlockSpec((L, DI), lambda b, c: (b * nc + c, 0)),
                   pl.BlockSpec((1, 1, n_pairs, LANE, N), lambda b, c: (b, c, 0, 0, 0))],
        out_shape=[jax.ShapeDtypeStruct((B * seq, DI), BF16),
                   jax.ShapeDtypeStruct((B, nc, n_pairs, LANE, N), F32)],
        scratch_shapes=[pltpu.VMEM((n_pairs, LANE, N), F32)],
        compiler_params=_cp("parallel", "arbitrary"))(xbc, dt_row, cs_row, d_full)


def ssd_bwd(xbc, dt_row, cs_row, d_full, states, dy, *, B, seq, DI, name):
    L, N, G = SSM_CHUNK, SSM_STATE, SSM_GROUPS
    nc = seq // L
    XW = xbc.shape[1]
    n_pairs = DI // LANE
    ppg = n_pairs // G
    HR = G * HEAD_ROWS

    def body(x_ref, dt_ref, cs_ref, d_ref, st_ref, dy_ref, dx_ref, ddt_ref, dcs_ref, dd_ref, dH):
        c = pl.program_id(1)

        @pl.when(c == 0)
        def _():
            dH[...] = jnp.zeros_like(dH)
            dd_ref[...] = jnp.zeros_like(dd_ref)

        causal = lax.broadcasted_iota(jnp.int32, (L, L), 0) >= lax.broadcasted_iota(jnp.int32, (L, L), 1)
        lo_half = lax.broadcasted_iota(jnp.int32, (L, LANE), 1) < SSM_HEAD_DIM
        rows_lo = lax.broadcasted_iota(jnp.int32, (LANE, N), 0) < SSM_HEAD_DIM
        lane_row = lax.broadcasted_iota(jnp.int32, (1, L), 1)
        head_row = lax.broadcasted_iota(jnp.int32, (HR, 1), 0)
        ddt_all = jnp.zeros((HR, L), F32)
        dcs_all = jnp.zeros((HR, L), F32)

        def place(row, r):
            return jnp.where(head_row == r, row, 0.0)

        def as_rows(col):
            return jnp.broadcast_to(col, (L, LANE)).T[0:HR, :]

        def head_sums(t, ra):
            sa = as_rows(jnp.sum(jnp.where(lo_half, t, 0.0), axis=1, keepdims=True))
            sb = as_rows(jnp.sum(jnp.where(lo_half, 0.0, t), axis=1, keepdims=True))
            return place(sa, ra) + place(sb, ra + 1)

        for g in range(G):
            Bc = x_ref[:, DI + g * N:DI + (g + 1) * N]
            Cc = x_ref[:, DI + G * N + g * N:DI + G * N + (g + 1) * N]
            CB = _dot(Cc, Bc, NT)
            dCB = jnp.zeros((L, L), F32)
            dC = jnp.zeros((L, N), F32)
            dB = jnp.zeros((L, N), F32)
            for pp in range(ppg):
                pair = g * ppg + pp
                _, ra, X, dt_p, La, Lb, last_a, last_b, ecs, tail = _ssd_pair_common(
                    x_ref, dt_ref, cs_ref, pair, ppg, lo_half, causal, lane_row)
                rb = ra + 1
                xs = X * dt_p
                xsb = xs.astype(BF16)
                Ma, Mb = CB * La, CB * Lb
                dY = dy_ref[:, pair * LANE:(pair + 1) * LANE].astype(F32)
                dYb = dY.astype(BF16)
                dMa = _dot(jnp.where(lo_half, dY, 0.0).astype(BF16), xsb, NT)
                dMb = _dot(jnp.where(lo_half, 0.0, dY).astype(BF16), xsb, NT)
                dSa, dSb = dMa * Ma, dMb * Mb
                dCB = dCB + dMa * La + dMb * Lb
                dcs_all = dcs_all + place(as_rows(jnp.sum(dSa, axis=1, keepdims=True)) - jnp.sum(dSa, axis=0, keepdims=True), ra)
                dcs_all = dcs_all + place(as_rows(jnp.sum(dSb, axis=1, keepdims=True)) - jnp.sum(dSb, axis=0, keepdims=True), rb)
                dxs = jnp.where(lo_half, _dot(Ma.astype(BF16), dYb, TN), _dot(Mb.astype(BF16), dYb, TN))
                S = st_ref[0, 0, pair]
                Sb = S.astype(BF16)
                y_inter = ecs * _dot(Cc, Sb, NT)
                dYe = (dY * ecs).astype(BF16)
                dC = dC + _dot(dYe, Sb)
                dHp = _dot(dYe, Cc, TN)
                dcs_all = dcs_all + head_sums(dY * y_inter, ra)
                dHn = dH[pair]
                dHnb = dHn.astype(BF16)
                ea, eb = jnp.exp(last_a), jnp.exp(last_b)
                dHp = dHp + jnp.where(rows_lo, ea, eb) * dHn
                prod = dHn * S
                dlast_a = ea * jnp.sum(jnp.where(rows_lo, prod, 0.0), keepdims=True)
                dlast_b = eb * jnp.sum(jnp.where(rows_lo, 0.0, prod), keepdims=True)
                XBt = _dot(Bc, dHnb, NT)
                dxs = dxs + tail * XBt
                t2 = xs * XBt * tail
                dlast_a = dlast_a + jnp.sum(jnp.where(lo_half, t2, 0.0), keepdims=True)
                dlast_b = dlast_b + jnp.sum(jnp.where(lo_half, 0.0, t2), keepdims=True)
                dcs_all = dcs_all - head_sums(t2, ra)
                dcs_all = dcs_all + place(jnp.where(lane_row == L - 1, dlast_a, 0.0), ra)
                dcs_all = dcs_all + place(jnp.where(lane_row == L - 1, dlast_b, 0.0), rb)
                dB = dB + _dot((xs * tail).astype(BF16), dHnb)
                dfull = d_ref[:, pair * LANE:(pair + 1) * LANE]
                dx_ref[:, pair * LANE:(pair + 1) * LANE] = (dxs * dt_p + dfull * dY).astype(BF16)
                ddt_all = ddt_all + head_sums(dxs * X, ra)
                dd_ref[0, :, pair * LANE:(pair + 1) * LANE] += jnp.sum(dY * X, axis=0, keepdims=True)
                dH[pair] = dHp
            dCBb = dCB.astype(BF16)
            dx_ref[:, DI + g * N:DI + (g + 1) * N] = (dB + _dot(dCBb, Cc, TN)).astype(BF16)
            dx_ref[:, DI + G * N + g * N:DI + G * N + (g + 1) * N] = (dC + _dot(dCBb, Bc)).astype(BF16)
        ddt_ref[0] = ddt_all
        dcs_ref[0] = dcs_all

    rev = lambda b, c: (b * nc + (nc - 1 - c), 0)
    hrow = pl.BlockSpec((1, HR, L), lambda b, c: (b, 0, nc - 1 - c))
    return pl.pallas_call(
        body, name=name, grid=(B, nc),
        in_specs=[pl.BlockSpec((L, XW), rev), hrow, hrow,
                  pl.BlockSpec((1, DI), lambda b, c: (0, 0)),
                  pl.BlockSpec((1, 1, n_pairs, LANE, N), lambda b, c: (b, nc - 1 - c, 0, 0, 0)),
                  pl.BlockSpec((L, DI), rev)],
        out_specs=[pl.BlockSpec((L, XW), rev), hrow, hrow, pl.BlockSpec((1, 1, DI), lambda b, c: (b, 0, 0))],
        out_shape=[jax.ShapeDtypeStruct((B * seq, XW), BF16),
                   jax.ShapeDtypeStruct((B, HR, seq), F32), jax.ShapeDtypeStruct((B, HR, seq), F32),
                   jax.ShapeDtypeStruct((B, 1, DI), F32)],
        scratch_shapes=[pltpu.VMEM((n_pairs, LANE, N), F32)],
        compiler_params=_cp("parallel", "arbitrary"))(xbc, dt_row, cs_row, d_full, states, dy)


def gnorm_fwd(y, proj, w, *, DI, name):
    T = y.shape[0]
    tt = _pick(T, 256, 16)
    gw = DI // SSM_GROUPS

    def body(y_ref, z_ref, w_ref, o_ref):
        for g in range(SSM_GROUPS):
            sl = slice(g * gw, (g + 1) * gw)
            y2 = y_ref[:, sl].astype(F32) * _silu(z_ref[:, sl].astype(F32))
            r = lax.rsqrt(jnp.mean(y2 * y2, axis=-1, keepdims=True) + EPS)
            o_ref[:, sl] = (y2 * r * w_ref[:, sl]).astype(BF16)

    row = pl.BlockSpec((tt, DI), lambda i: (i, 0))
    return pl.pallas_call(
        body, name=name, grid=(T // tt,),
        in_specs=[row, row, pl.BlockSpec((1, DI), lambda i: (0, 0))], out_specs=row,
        out_shape=jax.ShapeDtypeStruct((T, DI), BF16), compiler_params=_cp("parallel"))(y, proj, w)


def gnorm_bwd(dyn, y, proj, w, *, DI, name):
    T = y.shape[0]
    tt = _pick(T, 256, 16)
    gw = DI // SSM_GROUPS

    def body(dyn_ref, y_ref, z_ref, w_ref, dy_ref, dz_ref, dw_ref):
        @pl.when(pl.program_id(0) == 0)
        def _():
            dw_ref[...] = jnp.zeros_like(dw_ref)

        for g in range(SSM_GROUPS):
            sl = slice(g * gw, (g + 1) * gw)
            yv = y_ref[:, sl].astype(F32)
            z = z_ref[:, sl].astype(F32)
            sz = _silu(z)
            y2 = yv * sz
            r = lax.rsqrt(jnp.mean(y2 * y2, axis=-1, keepdims=True) + EPS)
            xhat = y2 * r
            d = dyn_ref[:, sl].astype(F32)
            dw_ref[:, sl] += jnp.sum(d * xhat, axis=0, keepdims=True)
            dxh = d * w_ref[:, sl]
            dy2 = r * (dxh - xhat * jnp.mean(dxh * xhat, axis=-1, keepdims=True))
            dy_ref[:, sl] = (dy2 * sz).astype(BF16)
            dz_ref[:, sl] = (dy2 * yv * _silu_grad(z)).astype(BF16)

    row = pl.BlockSpec((tt, DI), lambda i: (i, 0))
    vec = pl.BlockSpec((1, DI), lambda i: (0, 0))
    shp = jax.ShapeDtypeStruct((T, DI), BF16)
    return pl.pallas_call(
        body, name=name, grid=(T // tt,), in_specs=[row, row, row, vec], out_specs=[row, row, vec],
        out_shape=[shp, shp, jax.ShapeDtypeStruct((1, DI), F32)],
        compiler_params=_cp("arbitrary"))(dyn, y, proj, w)


def exchange(srcs, gather, *, name):
    n = len(srcs)
    out_shape = [jax.ShapeDtypeStruct((N_DEV,) + (s.shape if gt else s.shape[1:]), s.dtype)
                 for s, gt in zip(srcs, gather)]

    def body(*refs):
        src_refs, out_refs = refs[:n], refs[n:2 * n]
        send_sems, recv_sems, local_sems = refs[2 * n:]
        x, y, c = lax.axis_index("x"), lax.axis_index("y"), lax.axis_index("c")
        me = 4 * x + 2 * y + c
        started = []
        for a in range(n):
            def block(p, a=a):
                return src_refs[a] if gather[a] else src_refs[a].at[p]

            local = pltpu.make_async_copy(block(me), out_refs[a].at[me], local_sems.at[a])
            local.start()
            started.append(local)
            for k in range(1, N_DEV):
                px = 1 - x if k & 4 else x
                py = 1 - y if k & 2 else y
                pc = 1 - c if k & 1 else c
                cp = pltpu.make_async_remote_copy(
                    src_ref=block(4 * px + 2 * py + pc), dst_ref=out_refs[a].at[me],
                    send_sem=send_sems.at[a, k - 1], recv_sem=recv_sems.at[a, k - 1],
                    device_id=(px, py, pc), device_id_type=pl.DeviceIdType.MESH)
                cp.start()
                started.append(cp)
        for cp in started:
            cp.wait()

    any_spec = pl.BlockSpec(memory_space=pl.ANY)
    return pl.pallas_call(
        body, name=name, in_specs=[any_spec] * n, out_specs=[any_spec] * n, out_shape=out_shape,
        scratch_shapes=[pltpu.SemaphoreType.DMA((n, N_DEV - 1)), pltpu.SemaphoreType.DMA((n, N_DEV - 1)),
                        pltpu.SemaphoreType.DMA((n,))],
        compiler_params=pltpu.CompilerParams(has_side_effects=True))(*srcs)


def sum_slots(recv, *, name):
    _, R, C = recv.shape
    tr = _pick(R, 512, 8)

    def body(r_ref, o_ref):
        acc = r_ref[0].astype(F32)
        for p in range(1, N_DEV):
            acc = acc + r_ref[p].astype(F32)
        o_ref[...] = acc

    return pl.pallas_call(
        body, name=name, grid=(R // tr,),
        in_specs=[pl.BlockSpec((N_DEV, tr, C), lambda i: (0, i, 0))],
        out_specs=pl.BlockSpec((tr, C), lambda i: (i, 0)),
        out_shape=jax.ShapeDtypeStruct((R, C), F32), compiler_params=_cp("parallel"))(recv)


def adamw(gsrc, w, m, v, *, name):
    slots, R, C = gsrc.shape
    tr = _pick(R, 256, 16 if gsrc.dtype == BF16 else 8)
    c1 = 1.0 / (1.0 - ADAM_B1 ** ADAM_STEP)
    c2 = 1.0 / (1.0 - ADAM_B2 ** ADAM_STEP)

    def body(g_ref, w_ref, m_ref, v_ref, go_ref, d_ref, mo_ref, vo_ref):
        g = g_ref[0].astype(F32)
        for p in range(1, slots):
            g = g + g_ref[p].astype(F32)
        m2 = ADAM_B1 * m_ref[...] + (1.0 - ADAM_B1) * g
        v2 = ADAM_B2 * v_ref[...] + (1.0 - ADAM_B2) * (g * g)
        go_ref[...] = g
        mo_ref[...] = m2
        vo_ref[...] = v2
        d_ref[...] = -ADAM_LR * ((m2 * c1) / (jnp.sqrt(v2 * c2) + ADAM_EPS) + ADAM_WD * w_ref[...])

    blk = pl.BlockSpec((tr, C), lambda i: (i, 0))
    shp = jax.ShapeDtypeStruct((R, C), F32)
    return pl.pallas_call(
        body, name=name, grid=(R // tr,),
        in_specs=[pl.BlockSpec((slots, tr, C), lambda i: (0, i, 0)), blk, blk, blk],
        out_specs=[blk] * 4, out_shape=[shp] * 4, compiler_params=_cp("parallel"))(gsrc, w, m, v)


def _pad_cols(a, n):
    return jnp.pad(a, ((0, 0), (0, n - a.shape[1])))


def _to_rows(a, B, seq, H):
    G = SSM_GROUPS
    R = H // G
    t = a[:, :H].reshape(B, seq, G, R).transpose(0, 2, 3, 1)
    t = jnp.pad(t, ((0, 0), (0, 0), (0, HEAD_ROWS - R), (0, 0)))
    return t.reshape(B, G * HEAD_ROWS, seq)


def _from_rows(a, B, seq, H):
    G = SSM_GROUPS
    R = H // G
    t = a.reshape(B, G, HEAD_ROWS, seq)[:, :, :R].transpose(0, 3, 1, 2).reshape(B * seq, H)
    return _pad_cols(t, LANE)


def local_step(x, target, p, *, B, seq):
    T, D = x.shape
    CW = D
    heads = CW // SB_HEAD_DIM
    DI = 2 * D
    H = DI // SSM_HEAD_DIM
    XW = DI + 2 * SSM_GROUPS * SSM_STATE
    in_odd = DI + XW + H
    w1t = p["od_w_in_t"]
    q_off, k_off, v_off, gc_off, ga_off = 3 * CW, 4 * CW, 5 * CW, 2 * CW, 6 * CW
    dt_off = DI + XW

    n0 = rmsnorm_fwd(x, p["ev_norm_w"], name="l0_norm")
    proj0 = mm_nn(n0, p["ev_w_in"], out_dtype=BF16, name="l0_in_proj")
    (u2,) = dwconv_fwd(proj0, (0, CW), p["ev_dw_w"], p["ev_dw_b"], C=CW, seq=seq, glu=True, silu_out=False,
                       name="l0_conv")
    o, ctot = sba_fwd(proj0, B=B, seq=seq, heads=heads, q_off=q_off, k_off=k_off, v_off=v_off, name="l0_attn")
    ycat = mix0_post_fwd(u2, proj0, o, p["ev_ln_w"], p["ev_ln_b"], CW=CW, gc_off=gc_off, ga_off=ga_off,
                         name="l0_post")
    h1 = mm_nn(ycat, p["ev_w_out"], add=x, out_dtype=F32, name="l0_out_proj")

    n1 = rmsnorm_fwd(h1, p["od_norm_w"], name="l1_norm")
    proj1 = mm_nt_terms([(n1, 0, D, 0)], w1t, out_dtype=BF16, name="l1_in_proj")
    u_pre, xbc = dwconv_fwd(proj1, (DI,), p["od_conv_w"], p["od_conv_b"], C=XW, seq=seq, glu=False, silu_out=True,
                            name="l1_conv")
    bias_p, alog_p = _pad_cols(p["od_dt_bias"], LANE), _pad_cols(p["od_a_log"], LANE)
    dt, cs = ssm_dt_fwd(proj1, bias_p, alog_p, dt_off=dt_off, name="l1_dt")
    dt_row, cs_row = _to_rows(dt, B, seq, H), _to_rows(cs, B, seq, H)
    d_full = jnp.repeat(p["od_d"], SSM_HEAD_DIM, axis=1)
    y_ssd, states = ssd_fwd(xbc, dt_row, cs_row, d_full, B=B, seq=seq, DI=DI, name="l1_ssd")
    yn = gnorm_fwd(y_ssd, proj1, p["od_gnorm_w"], DI=DI, name="l1_gnorm")
    h2 = mm_nn(yn, p["od_w_out"], add=h1, out_dtype=F32, name="l1_out_proj")

    loss, dh2, g_final = final_loss(h2, p["final_norm_w"], target, name="loss_head")

    dh2b = dh2.astype(BF16)
    g_od_w_out = mm_tn(yn, dh2b, out_dtype=BF16, name="l1_dw_out")
    dyn = mm_nt_terms([(dh2b, 0, D, 0)], p["od_w_out"], out_dtype=BF16, name="l1_d_out_proj")
    dy_ssd, dz, g_gnorm = gnorm_bwd(dyn, y_ssd, proj1, p["od_gnorm_w"], DI=DI, name="l1_gnorm_bwd")
    dxbc_c, ddt_row, dcs_row, dd_part = ssd_bwd(xbc, dt_row, cs_row, d_full, states, dy_ssd, B=B, seq=seq, DI=DI,
                                                name="l1_ssd_bwd")
    g_d = dd_part.sum(axis=(0, 1)).reshape(H, SSM_HEAD_DIM).sum(axis=1)[None, :]
    draw, g_bias, g_alog = ssm_dt_bwd(_from_rows(ddt_row, B, seq, H), _from_rows(dcs_row, B, seq, H), proj1, dt,
                                      bias_p, alog_p, dt_off=dt_off, n_heads=H, name="l1_dt_bwd")
    dxbc, g_conv_w, g_conv_b = dwconv_bwd(dxbc_c, u_pre, proj1, (DI,), p["od_conv_w"], C=XW, seq=seq, glu=False,
                                          silu_out=True, name="l1_conv_bwd")
    tw = 512 if DI % 512 == 0 else LANE
    terms = [(dz, j, tw, j * tw) for j in range(DI // tw)]
    terms += [(dxbc, j, tw, DI + j * tw) for j in range(XW // tw)]
    terms += [(draw, 0, LANE, dt_off)]
    dn1 = mm_nn_terms(terms, w1t, out_dtype=F32, name="l1_d_in_proj")
    g_od_w_in_t = jnp.concatenate([mm_tn(dz, n1, out_dtype=BF16, name="l1_dw_in_z"),
                                   mm_tn(dxbc, n1, out_dtype=BF16, name="l1_dw_in_xbc"),
                                   mm_tn(draw, n1, out_dtype=BF16, name="l1_dw_in_dt")], axis=0)[:in_odd]
    dh1, g_od_norm = rmsnorm_bwd(h1, p["od_norm_w"], dn1, dh2, name="l1_norm_bwd")

    dh1b = dh1.astype(BF16)
    g_ev_w_out = mm_tn(ycat, dh1b, out_dtype=BF16, name="l0_dw_out")
    dycat = mm_nt_terms([(dh1b, 0, D, 0)], p["ev_w_out"], out_dtype=BF16, name="l0_d_out_proj")
    du2, dgc, dga, do, g_ln_w, g_ln_b = mix0_post_bwd(dycat, u2, proj0, o, p["ev_ln_w"], p["ev_ln_b"], CW=CW,
                                                      gc_off=gc_off, ga_off=ga_off, name="l0_post_bwd")
    dq, dk, dv = sba_bwd(proj0, ctot, do, B=B, seq=seq, heads=heads, q_off=q_off, k_off=k_off, v_off=v_off,
                         name="l0_attn_bwd")
    dga_a, dga_b, g_dw_w, g_dw_b = dwconv_bwd(du2, None, proj0, (0, CW), p["ev_dw_w"], C=CW, seq=seq, glu=True,
                                              silu_out=False, name="l0_conv_bwd")
    pieces = [dga_a, dga_b, dgc, dq, dk, dv, dga]
    dn0 = mm_nt_terms([(pc, 0, CW, j * CW) for j, pc in enumerate(pieces)], p["ev_w_in"], out_dtype=F32,
                      name="l0_d_in_proj")
    g_ev_w_in = jnp.concatenate([mm_tn(n0, pc, out_dtype=BF16, name=f"l0_dw_in_{j}") for j, pc in enumerate(pieces)],
                                axis=1)
    dx, g_ev_norm = rmsnorm_bwd(x, p["ev_norm_w"], dn0, dh1, name="l0_norm_bwd")

    grads = dict(ev_norm_w=g_ev_norm, ev_w_in=g_ev_w_in, ev_dw_w=g_dw_w, ev_dw_b=g_dw_b, ev_ln_w=g_ln_w,
                 ev_ln_b=g_ln_b, ev_w_out=g_ev_w_out, od_norm_w=g_od_norm, od_w_in_t=g_od_w_in_t, od_conv_w=g_conv_w,
                 od_conv_b=g_conv_b, od_dt_bias=g_bias[:, :H], od_a_log=g_alog[:, :H], od_d=g_d, od_gnorm_w=g_gnorm,
                 od_w_out=g_od_w_out, final_norm_w=g_final)
    return loss, dx, grads


BIG = ("ev_w_in", "ev_w_out", "od_w_in", "od_w_out")
SMALL = ("ev_norm_w", "ev_dw_w", "ev_dw_b", "ev_ln_w", "ev_ln_b", "od_norm_w", "od_conv_w", "od_conv_b",
         "od_dt_bias", "od_a_log", "od_d", "od_gnorm_w", "final_norm_w")
SMALL_SHARDED = ("ev_dw_w", "od_norm_w", "od_conv_w", "od_conv_b", "od_gnorm_w")
ORDER = ("ev_norm_w", "ev_w_in", "ev_dw_w", "ev_dw_b", "ev_ln_w", "ev_ln_b", "ev_w_out", "od_norm_w", "od_w_in",
         "od_conv_w", "od_conv_b", "od_dt_bias", "od_a_log", "od_d", "od_gnorm_w", "od_w_out", "final_norm_w")


def _pack_rows(arrs, width, row_align):
    parts, spans, r0 = [], [], 0
    for a in arrs:
        flat = a.reshape(-1)
        rows = -(-flat.shape[0] // (width * row_align)) * row_align
        parts.append(jnp.pad(flat, (0, rows * width - flat.shape[0])).reshape(rows, width))
        spans.append((r0, a.size, a.shape))
        r0 += rows
    return jnp.concatenate(parts, axis=0), spans


def _unpack_rows(packed, spans):
    lead = packed.shape[:-2]
    width = packed.shape[-1]
    out = []
    for r0, size, shape in spans:
        rows = -(-size // width)
        blk = packed[..., r0:r0 + rows, :].reshape(lead + (rows * width,))[..., :size]
        out.append(blk.reshape(lead + tuple(shape)))
    return out


def _col_shards(a):
    R, C8 = a.shape
    return a.reshape(R, N_DEV, C8 // N_DEV).transpose(1, 0, 2)


def _col_unshards(a):
    n, R, C = a.shape
    return a.transpose(1, 0, 2).reshape(R, n * C)


def kernel(x, ev_norm_w, ev_w_in, ev_dw_w, ev_dw_b, ev_ln_w, ev_ln_b, ev_w_out, od_norm_w, od_w_in, od_conv_w, od_conv_b, od_dt_bias, od_a_log, od_d, od_gnorm_w, od_w_out, final_norm_w, loss_target, m_ev_norm_w, m_ev_w_in, m_ev_dw_w, m_ev_dw_b, m_ev_ln_w, m_ev_ln_b, m_ev_w_out, m_od_norm_w, m_od_w_in, m_od_conv_w, m_od_conv_b, m_od_dt_bias, m_od_a_log, m_od_d, m_od_gnorm_w, m_od_w_out, m_final_norm_w, v_ev_norm_w, v_ev_w_in, v_ev_dw_w, v_ev_dw_b, v_ev_ln_w, v_ev_ln_b, v_ev_w_out, v_od_norm_w, v_od_w_in, v_od_conv_w, v_od_conv_b, v_od_dt_bias, v_od_a_log, v_od_d, v_od_gnorm_w, v_od_w_out, v_final_norm_w):
    loc = dict(ev_norm_w=ev_norm_w, ev_w_in=ev_w_in, ev_dw_w=ev_dw_w, ev_dw_b=ev_dw_b, ev_ln_w=ev_ln_w,
               ev_ln_b=ev_ln_b, ev_w_out=ev_w_out, od_norm_w=od_norm_w, od_w_in=od_w_in, od_conv_w=od_conv_w,
               od_conv_b=od_conv_b, od_dt_bias=od_dt_bias, od_a_log=od_a_log, od_d=od_d, od_gnorm_w=od_gnorm_w,
               od_w_out=od_w_out, final_norm_w=final_norm_w)
    mom = dict(ev_norm_w=m_ev_norm_w, ev_w_in=m_ev_w_in, ev_dw_w=m_ev_dw_w, ev_dw_b=m_ev_dw_b, ev_ln_w=m_ev_ln_w,
               ev_ln_b=m_ev_ln_b, ev_w_out=m_ev_w_out, od_norm_w=m_od_norm_w, od_w_in=m_od_w_in,
               od_conv_w=m_od_conv_w, od_conv_b=m_od_conv_b, od_dt_bias=m_od_dt_bias, od_a_log=m_od_a_log,
               od_d=m_od_d, od_gnorm_w=m_od_gnorm_w, od_w_out=m_od_w_out, final_norm_w=m_final_norm_w)
    var = dict(ev_norm_w=v_ev_norm_w, ev_w_in=v_ev_w_in, ev_dw_w=v_ev_dw_w, ev_dw_b=v_ev_dw_b, ev_ln_w=v_ev_ln_w,
               ev_ln_b=v_ev_ln_b, ev_w_out=v_ev_w_out, od_norm_w=v_od_norm_w, od_w_in=v_od_w_in,
               od_conv_w=v_od_conv_w, od_conv_b=v_od_conv_b, od_dt_bias=v_od_dt_bias, od_a_log=v_od_a_log,
               od_d=v_od_d, od_gnorm_w=v_od_gnorm_w, od_w_out=v_od_w_out, final_norm_w=v_final_norm_w)
    shapes = {n: loc[n].shape for n in ORDER}
    loc = {n: (a.reshape(1, -1) if a.ndim == 1 else a.reshape(a.shape[-2:]) if a.ndim == 3 else a)
           for n, a in loc.items()}
    mom = {n: a.reshape(loc[n].shape) for n, a in mom.items()}
    var = {n: a.reshape(loc[n].shape) for n, a in var.items()}

    B, seq, D = x.shape
    me = 4 * lax.axis_index("x") + 2 * lax.axis_index("y") + lax.axis_index("c")

    w1_rows = loc["od_w_in"].shape[1]
    w1_pad = (-w1_rows) % 16

    def to_t(a):
        return jnp.pad(a.T, ((0, w1_pad), (0, 0)))

    small_packed, small_spans = _pack_rows([loc[n] for n in SMALL_SHARDED], LANE, 8)
    g_ev_in, g_ev_out, g_od_in_t, g_od_out, small_all = exchange(
        [loc["ev_w_in"].astype(BF16), loc["ev_w_out"].astype(BF16), to_t(loc["od_w_in"]).astype(BF16),
         loc["od_w_out"].astype(BF16), small_packed], [True] * 5, name="gather_weights")
    full = dict(loc)
    del full["od_w_in"]
    full["ev_w_in"] = _col_unshards(g_ev_in)
    full["ev_w_out"] = g_ev_out.reshape(-1, D)
    full["od_w_out"] = g_od_out.reshape(-1, D)
    w1t = g_od_in_t[:, :w1_rows].reshape(N_DEV * w1_rows, D)
    full["od_w_in_t"] = jnp.pad(w1t, ((0, -(-(w1t.shape[0] + LANE) // 256) * 256 - w1t.shape[0]), (0, 0)))
    for n, a in zip(SMALL_SHARDED, _unpack_rows(small_all, small_spans)):
        full[n] = _col_unshards(a)

    loss, dx, grads = local_step(x.reshape(B * seq, D), loss_target.reshape(B * seq, D), full, B=B, seq=seq)

    g1t = jnp.pad(grads["od_w_in_t"].reshape(N_DEV, w1_rows, D), ((0, 0), (0, w1_pad), (0, 0)))
    gsmall_packed, gsmall_spans = _pack_rows([grads[n] for n in SMALL], LANE, 8)
    r_ev_in, r_ev_out, r_od_in_t, r_od_out, gsmall_recv = exchange(
        [_col_shards(grads["ev_w_in"]), grads["ev_w_out"].reshape(N_DEV, -1, D), g1t,
         grads["od_w_out"].reshape(N_DEV, -1, D), gsmall_packed], [False] * 4 + [True], name="exchange_grads")

    big_out = [{} for _ in range(4)]
    for n, recv in (("ev_w_in", r_ev_in), ("ev_w_out", r_ev_out), ("od_w_out", r_od_out)):
        for kind, a in enumerate(adamw(recv, loc[n], mom[n], var[n], name="adamw_" + n)):
            big_out[kind][n] = a
    for kind, a in enumerate(adamw(r_od_in_t, to_t(loc["od_w_in"]), to_t(mom["od_w_in"]), to_t(var["od_w_in"]),
                                   name="adamw_od_w_in")):
        big_out[kind]["od_w_in"] = a[:w1_rows].T

    gsmall = dict(zip(SMALL, _unpack_rows(sum_slots(gsmall_recv, name="sum_small_grads"), gsmall_spans)))
    for n in SMALL_SHARDED:
        width = loc[n].shape[1]
        gsmall[n] = lax.dynamic_slice_in_dim(gsmall[n], me * width, width, axis=1)
    gs, sspans = _pack_rows([gsmall[n] for n in SMALL], LANE, 8)
    ws, _ = _pack_rows([loc[n] for n in SMALL], LANE, 8)
    ms, _ = _pack_rows([mom[n] for n in SMALL], LANE, 8)
    vs, _ = _pack_rows([var[n] for n in SMALL], LANE, 8)
    small_out = [dict(zip(SMALL, _unpack_rows(a, sspans))) for a in adamw(gs[None], ws, ms, vs, name="adamw_small")]

    outs = [lax.psum(loss[0, 0], ("x", "y", "c")), dx.reshape(B, seq, D)]
    for kind in range(4):
        for n in ORDER:
            src = big_out[kind] if n in BIG else small_out[kind]
            outs.append(src[n].reshape(shapes[n]))
    return tuple(outs)
```

```python
import functools

import jax
import jax.numpy as jnp
from jax import lax
from jax.experimental import pallas as pl
from jax.experimental.pallas import tpu as pltpu

F32 = jnp.float32
BF16 = jnp.bfloat16

EPS = 1e-6
N_DEV = 8
LANE = 128
VMEM_LIMIT_BYTES = 48 * 1024 * 1024

SB_HEAD_DIM = 128
CONF_KERNEL = 31
SSM_CONV = 4
SSM_HEAD_DIM = 64
SSM_GROUPS = 4
SSM_STATE = 128
SSM_CHUNK = 128
HALO = 32
HEAD_ROWS = 8
NEG_BIG = -1e30

ADAM_LR = 0.001
ADAM_B1 = 0.9
ADAM_B2 = 0.999
ADAM_EPS = 1e-08
ADAM_WD = 0.01
ADAM_STEP = 10

NT = (((1,), (1,)), ((), ()))
TN = (((0,), (0,)), ((), ()))


def _cp(*sem):
    return pltpu.CompilerParams(dimension_semantics=sem, vmem_limit_bytes=VMEM_LIMIT_BYTES)


def _pick(n, cap, align):
    if n <= cap:
        return n
    t = (cap // align) * align
    while t >= align:
        if n % t == 0:
            return t
        t -= align
    raise ValueError(f"no tile for {n} (cap {cap}, align {align})")


def _sigmoid(x):
    return 1.0 / (1.0 + jnp.exp(-x))


def _silu(x):
    return x * _sigmoid(x)


def _silu_grad(x):
    s = _sigmoid(x)
    return s * (1.0 + x * (1.0 - s))


def _dot(a, b, dims=None):
    if dims is None:
        return jnp.dot(a, b, preferred_element_type=F32)
    return lax.dot_general(a, b, dims, preferred_element_type=F32)


def _tri_dot3(tri, x):
    hi = x.astype(BF16)
    r1 = x - hi.astype(F32)
    mid = r1.astype(BF16)
    lo = (r1 - mid.astype(F32)).astype(BF16)
    return _dot(tri, hi) + _dot(tri, mid) + _dot(tri, lo)


def mm_nn(a, b, *, add=None, out_dtype, name):
    M, K = a.shape
    N = b.shape[1]
    tm = _pick(M, 1024, 16)
    tn = _pick(N, 1024, LANE)

    def body(*refs):
        if add is None:
            a_ref, b_ref, o_ref = refs
        else:
            a_ref, b_ref, add_ref, o_ref = refs
        acc = _dot(a_ref[...], b_ref[...])
        if add is not None:
            acc = acc + add_ref[...]
        o_ref[...] = acc.astype(out_dtype)

    in_specs = [pl.BlockSpec((tm, K), lambda i, j: (i, 0)), pl.BlockSpec((K, tn), lambda i, j: (0, j))]
    args = [a, b]
    if add is not None:
        in_specs.append(pl.BlockSpec((tm, tn), lambda i, j: (i, j)))
        args.append(add)
    return pl.pallas_call(
        body, name=name, grid=(M // tm, N // tn), in_specs=in_specs,
        out_specs=pl.BlockSpec((tm, tn), lambda i, j: (i, j)),
        out_shape=jax.ShapeDtypeStruct((M, N), out_dtype),
        compiler_params=_cp("parallel", "parallel"))(*args)


def mm_nt_terms(terms, b, *, out_dtype, name):
    M = terms[0][0].shape[0]
    N = b.shape[0]
    n_terms = len(terms)
    cap = 1024 if n_terms == 1 else 512
    tm = _pick(M, cap, 16)
    tn = _pick(N, cap, LANE)

    def body(*refs):
        o_ref = refs[-1]
        acc = None
        for t in range(n_terms):
            part = _dot(refs[2 * t][...], refs[2 * t + 1][...], NT)
            acc = part if acc is None else acc + part
        o_ref[...] = acc.astype(out_dtype)

    in_specs, args = [], []
    for arr, cb, w, off in terms:
        assert off % w == 0
        in_specs.append(pl.BlockSpec((tm, w), lambda i, j, cb=cb: (i, cb)))
        in_specs.append(pl.BlockSpec((tn, w), lambda i, j, ob=off // w: (j, ob)))
        args += [arr, b]
    return pl.pallas_call(
        body, name=name, grid=(M // tm, N // tn), in_specs=in_specs,
        out_specs=pl.BlockSpec((tm, tn), lambda i, j: (i, j)),
        out_shape=jax.ShapeDtypeStruct((M, N), out_dtype),
        compiler_params=_cp("parallel", "parallel"))(*args)


def mm_nn_terms(terms, b, *, out_dtype, name):
    M = terms[0][0].shape[0]
    N = b.shape[1]
    tm = _pick(M, 512, 16)
    tn = _pick(N, 512, LANE)
    n_terms = len(terms)

    def body(*refs):
        o_ref = refs[-1]
        acc = None
        for t in range(n_terms):
            part = _dot(refs[2 * t][...], refs[2 * t + 1][...])
            acc = part if acc is None else acc + part
        o_ref[...] = acc.astype(out_dtype)

    in_specs, args = [], []
    for arr, cb, w, off in terms:
        assert off % w == 0
        in_specs.append(pl.BlockSpec((tm, w), lambda i, j, cb=cb: (i, cb)))
        in_specs.append(pl.BlockSpec((w, tn), lambda i, j, ob=off // w: (ob, j)))
        args += [arr, b]
    return pl.pallas_call(
        body, name=name, grid=(M // tm, N // tn), in_specs=in_specs,
        out_specs=pl.BlockSpec((tm, tn), lambda i, j: (i, j)),
        out_shape=jax.ShapeDtypeStruct((M, N), out_dtype),
        compiler_params=_cp("parallel", "parallel"))(*args)


def mm_tn(a, b, *, out_dtype, name):
    T, M = a.shape
    N = b.shape[1]
    tm = _pick(M, 1024, LANE)
    tn = _pick(N, 1024, LANE)
    tk = _pick(T, 1024, 16)
    nk = T // tk

    def body(a_ref, b_ref, o_ref, acc_ref):
        k = pl.program_id(2)

        @pl.when(k == 0)
        def _():
            acc_ref[...] = jnp.zeros_like(acc_ref)

        acc_ref[...] += _dot(a_ref[...], b_ref[...], TN)

        @pl.when(k == nk - 1)
        def _():
            o_ref[...] = acc_ref[...].astype(out_dtype)

    return pl.pallas_call(
        body, name=name, grid=(M // tm, N // tn, nk),
        in_specs=[pl.BlockSpec((tk, tm), lambda i, j, k: (k, i)), pl.BlockSpec((tk, tn), lambda i, j, k: (k, j))],
        out_specs=pl.BlockSpec((tm, tn), lambda i, j, k: (i, j)),
        out_shape=jax.ShapeDtypeStruct((M, N), out_dtype),
        scratch_shapes=[pltpu.VMEM((tm, tn), F32)],
        compiler_params=_cp("parallel", "parallel", "arbitrary"))(a, b)


def rmsnorm_fwd(h, w, *, name):
    T, D = h.shape
    tt = _pick(T, 512, 16)

    def body(h_ref, w_ref, n_ref):
        x = h_ref[...]
        r = lax.rsqrt(jnp.mean(x * x, axis=-1, keepdims=True) + EPS)
        n_ref[...] = (x * r * w_ref[...]).astype(BF16)

    return pl.pallas_call(
        body, name=name, grid=(T // tt,),
        in_specs=[pl.BlockSpec((tt, D), lambda i: (i, 0)), pl.BlockSpec((1, D), lambda i: (0, 0))],
        out_specs=pl.BlockSpec((tt, D), lambda i: (i, 0)),
        out_shape=jax.ShapeDtypeStruct((T, D), BF16),
        compiler_params=_cp("parallel"))(h, w)


def rmsnorm_bwd(h, w, dn, dres, *, name):
    T, D = h.shape
    tt = _pick(T, 512, 16)

    def body(h_ref, w_ref, dn_ref, dres_ref, dh_ref, gw_ref):
        @pl.when(pl.program_id(0) == 0)
        def _():
            gw_ref[...] = jnp.zeros_like(gw_ref)

        x = h_ref[...]
        r = lax.rsqrt(jnp.mean(x * x, axis=-1, keepdims=True) + EPS)
        xhat = x * r
        g = dn_ref[...].astype(F32)
        gw_ref[...] += jnp.sum(g * xhat, axis=0, keepdims=True)
        dxh = g * w_ref[...]
        dx = r * (dxh - xhat * jnp.mean(dxh * xhat, axis=-1, keepdims=True))
        dh_ref[...] = dres_ref[...] + dx

    row = pl.BlockSpec((tt, D), lambda i: (i, 0))
    vec = pl.BlockSpec((1, D), lambda i: (0, 0))
    return pl.pallas_call(
        body, name=name, grid=(T // tt,), in_specs=[row, vec, row, row], out_specs=[row, vec],
        out_shape=[jax.ShapeDtypeStruct((T, D), F32), jax.ShapeDtypeStruct((1, D), F32)],
        compiler_params=_cp("arbitrary"))(h, w, dn, dres)


def final_loss(h, w, target, *, name):
    T, D = h.shape
    tt = _pick(T, 512, 16)

    def body(h_ref, w_ref, t_ref, loss_ref, dh_ref, gw_ref):
        @pl.when(pl.program_id(0) == 0)
        def _():
            gw_ref[...] = jnp.zeros_like(gw_ref)
            loss_ref[...] = jnp.zeros_like(loss_ref)

        x = h_ref[...]
        r = lax.rsqrt(jnp.mean(x * x, axis=-1, keepdims=True) + EPS)
        xhat = x * r
        e = xhat * w_ref[...] - t_ref[...]
        loss_ref[...] += jnp.sum(e * e) * (0.5 / D)
        g = e * (1.0 / D)
        gw_ref[...] += jnp.sum(g * xhat, axis=0, keepdims=True)
        dxh = g * w_ref[...]
        dh_ref[...] = r * (dxh - xhat * jnp.mean(dxh * xhat, axis=-1, keepdims=True))

    row = pl.BlockSpec((tt, D), lambda i: (i, 0))
    vec = pl.BlockSpec((1, D), lambda i: (0, 0))
    one = pl.BlockSpec((1, LANE), lambda i: (0, 0))
    return pl.pallas_call(
        body, name=name, grid=(T // tt,), in_specs=[row, vec, row], out_specs=[one, row, vec],
        out_shape=[jax.ShapeDtypeStruct((1, LANE), F32), jax.ShapeDtypeStruct((T, D), F32),
                   jax.ShapeDtypeStruct((1, D), F32)],
        compiler_params=_cp("arbitrary"))(h, w, target)


def _conv_tiles(T, C):
    return _pick(T, 256, HALO), _pick(C, 512, LANE)


def dwconv_fwd(src, offs, w, b, *, C, seq, glu, silu_out, name):
    T = src.shape[0]
    K = w.shape[0]
    tt, tc = _conv_tiles(T, C)
    n_in = 2 if glu else 1
    per = tt // HALO

    def body(*refs):
        cur = refs[0:2 * n_in:2]
        halo = refs[1:2 * n_in:2]
        w_ref, b_ref = refs[2 * n_in], refs[2 * n_in + 1]
        outs = refs[2 * n_in + 2:-1]
        buf = refs[-1]
        i = pl.program_id(1)
        first = (i * tt) % seq == 0

        def pre(rs):
            v = rs[0][...].astype(F32)
            return v * _sigmoid(rs[1][...].astype(F32)) if glu else v

        buf[0:HALO, :] = jnp.where(first, 0.0, pre(halo))
        buf[HALO:HALO + tt, :] = pre(cur)
        acc = jnp.broadcast_to(b_ref[...], (tt, tc))
        for k in range(K):
            acc = acc + w_ref[k:k + 1, :] * buf[pl.ds(HALO - (K - 1) + k, tt), :]
        outs[0][...] = acc.astype(BF16)
        if silu_out:
            outs[1][...] = _silu(acc).astype(BF16)

    in_specs, args = [], []
    for off in offs:
        assert off % tc == 0
        in_specs.append(pl.BlockSpec((tt, tc), lambda j, i, ob=off // tc: (i, ob + j)))
        in_specs.append(pl.BlockSpec((HALO, tc), lambda j, i, ob=off // tc: (jnp.maximum(i * per - 1, 0), ob + j)))
        args += [src, src]
    in_specs += [pl.BlockSpec((K, tc), lambda j, i: (0, j)), pl.BlockSpec((1, tc), lambda j, i: (0, j))]
    args += [w, b]
    n_out = 2 if silu_out else 1
    out = pl.pallas_call(
        body, name=name, grid=(C // tc, T // tt), in_specs=in_specs,
        out_specs=[pl.BlockSpec((tt, tc), lambda j, i: (i, j))] * n_out,
        out_shape=[jax.ShapeDtypeStruct((T, C), BF16)] * n_out,
        scratch_shapes=[pltpu.VMEM((HALO + tt, tc), F32)],
        compiler_params=_cp("parallel", "arbitrary"))(*args)
    return out


def dwconv_bwd(du, u, src, offs, w, *, C, seq, glu, silu_out, name):
    T = src.shape[0]
    K = w.shape[0]
    tt, tc = _conv_tiles(T, C)
    n_in = 2 if glu else 1
    per = tt // HALO
    last_blk = T // HALO - 1

    def body(*refs):
        pos = 0
        du_cur, du_nxt = refs[0], refs[1]
        pos = 2
        if silu_out:
            u_cur, u_nxt = refs[2], refs[3]
            pos = 4
        cur = refs[pos:pos + 2 * n_in:2]
        halo = refs[pos + 1:pos + 2 * n_in:2]
        pos += 2 * n_in
        w_ref = refs[pos]
        outs = refs[pos + 1:pos + 1 + n_in]
        dw_ref, db_ref = refs[pos + 1 + n_in], refs[pos + 2 + n_in]
        gbuf, xbuf = refs[-2], refs[-1]
        i = pl.program_id(1)
        first = (i * tt) % seq == 0
        last = ((i + 1) * tt) % seq == 0

        @pl.when(i == 0)
        def _():
            dw_ref[...] = jnp.zeros_like(dw_ref)
            db_ref[...] = jnp.zeros_like(db_ref)

        g_cur = du_cur[...].astype(F32)
        g_nxt = du_nxt[...].astype(F32)
        if silu_out:
            g_cur = g_cur * _silu_grad(u_cur[...].astype(F32))
            g_nxt = g_nxt * _silu_grad(u_nxt[...].astype(F32))
        gbuf[0:tt, :] = g_cur
        gbuf[tt:tt + HALO, :] = jnp.where(last, 0.0, g_nxt)

        def pre(rs):
            v = rs[0][...].astype(F32)
            return v * _sigmoid(rs[1][...].astype(F32)) if glu else v

        xbuf[0:HALO, :] = jnp.where(first, 0.0, pre(halo))
        xbuf[HALO:HALO + tt, :] = pre(cur)

        dx = jnp.zeros((tt, tc), F32)
        for k in range(K):
            dx = dx + w_ref[k:k + 1, :] * gbuf[pl.ds(K - 1 - k, tt), :]
            dw_ref[k:k + 1, :] += jnp.sum(g_cur * xbuf[pl.ds(HALO - (K - 1) + k, tt), :], axis=0, keepdims=True)
        db_ref[...] += jnp.sum(g_cur, axis=0, keepdims=True)
        if glu:
            a = cur[0][...].astype(F32)
            s = _sigmoid(cur[1][...].astype(F32))
            outs[0][...] = (dx * s).astype(BF16)
            outs[1][...] = (dx * a * s * (1.0 - s)).astype(BF16)
        else:
            outs[0][...] = dx.astype(BF16)

    def cur_spec(ob):
        return pl.BlockSpec((tt, tc), lambda j, i: (i, ob + j))

    def nxt_spec(ob):
        return pl.BlockSpec((HALO, tc), lambda j, i: (jnp.minimum((i + 1) * per, last_blk), ob + j))

    def prv_spec(ob):
        return pl.BlockSpec((HALO, tc), lambda j, i: (jnp.maximum(i * per - 1, 0), ob + j))

    in_specs = [cur_spec(0), nxt_spec(0)]
    args = [du, du]
    if silu_out:
        in_specs += [cur_spec(0), nxt_spec(0)]
        args += [u, u]
    for off in offs:
        assert off % tc == 0
        in_specs += [cur_spec(off // tc), prv_spec(off // tc)]
        args += [src, src]
    in_specs.append(pl.BlockSpec((K, tc), lambda j, i: (0, j)))
    args.append(w)
    out_specs = [pl.BlockSpec((tt, tc), lambda j, i: (i, j))] * n_in
    out_specs += [pl.BlockSpec((K, tc), lambda j, i: (0, j)), pl.BlockSpec((1, tc), lambda j, i: (0, j))]
    out_shape = [jax.ShapeDtypeStruct((T, C), BF16)] * n_in
    out_shape += [jax.ShapeDtypeStruct((K, C), F32), jax.ShapeDtypeStruct((1, C), F32)]
    return pl.pallas_call(
        body, name=name, grid=(C // tc, T // tt), in_specs=in_specs, out_specs=out_specs, out_shape=out_shape,
        scratch_shapes=[pltpu.VMEM((tt + HALO, tc), F32), pltpu.VMEM((HALO + tt, tc), F32)],
        compiler_params=_cp("parallel", "arbitrary"))(*args)


def mix0_post_fwd(u2, proj, o, ln_w, ln_b, *, CW, gc_off, ga_off, name):
    T = u2.shape[0]
    tt = _pick(T, 256, 16)

    def body(u_ref, gc_ref, ga_ref, o_ref, lw_ref, lb_ref, y_ref):
        u = u_ref[...].astype(F32)
        mu = jnp.mean(u, axis=-1, keepdims=True)
        xc = u - mu
        r = lax.rsqrt(jnp.mean(xc * xc, axis=-1, keepdims=True) + EPS)
        u3 = xc * r * lw_ref[...] + lb_ref[...]
        y_ref[:, 0:CW] = (_silu(u3) * _silu(gc_ref[...].astype(F32))).astype(BF16)
        y_ref[:, CW:2 * CW] = (o_ref[...].astype(F32) * _silu(ga_ref[...].astype(F32))).astype(BF16)

    row = pl.BlockSpec((tt, CW), lambda i: (i, 0))
    vec = pl.BlockSpec((1, CW), lambda i: (0, 0))
    return pl.pallas_call(
        body, name=name, grid=(T // tt,),
        in_specs=[row, pl.BlockSpec((tt, CW), lambda i: (i, gc_off // CW)),
                  pl.BlockSpec((tt, CW), lambda i: (i, ga_off // CW)), row, vec, vec],
        out_specs=pl.BlockSpec((tt, 2 * CW), lambda i: (i, 0)),
        out_shape=jax.ShapeDtypeStruct((T, 2 * CW), BF16),
        compiler_params=_cp("parallel"))(u2, proj, proj, o, ln_w, ln_b)


def mix0_post_bwd(dy, u2, proj, o, ln_w, ln_b, *, CW, gc_off, ga_off, name):
    T = u2.shape[0]
    tt = _pick(T, 256, 16)

    def body(dy_ref, u_ref, gc_ref, ga_ref, o_ref, lw_ref, lb_ref, du_ref, dgc_ref, dga_ref, do_ref, dlw_ref, dlb_ref):
        @pl.when(pl.program_id(0) == 0)
        def _():
            dlw_ref[...] = jnp.zeros_like(dlw_ref)
            dlb_ref[...] = jnp.zeros_like(dlb_ref)

        dyc = dy_ref[:, 0:CW].astype(F32)
        dya = dy_ref[:, CW:2 * CW].astype(F32)
        u = u_ref[...].astype(F32)
        mu = jnp.mean(u, axis=-1, keepdims=True)
        xc = u - mu
        r = lax.rsqrt(jnp.mean(xc * xc, axis=-1, keepdims=True) + EPS)
        xhat = xc * r
        u3 = xhat * lw_ref[...] + lb_ref[...]
        gc = gc_ref[...].astype(F32)
        dgc_ref[...] = (dyc * _silu(u3) * _silu_grad(gc)).astype(BF16)
        du3 = dyc * _silu(gc) * _silu_grad(u3)
        dlw_ref[...] += jnp.sum(du3 * xhat, axis=0, keepdims=True)
        dlb_ref[...] += jnp.sum(du3, axis=0, keepdims=True)
        dxh = du3 * lw_ref[...]
        du = r * (dxh - jnp.mean(dxh, axis=-1, keepdims=True) - xhat * jnp.mean(dxh * xhat, axis=-1, keepdims=True))
        du_ref[...] = du.astype(BF16)
        ga = ga_ref[...].astype(F32)
        ov = o_ref[...].astype(F32)
        do_ref[...] = (dya * _silu(ga)).astype(BF16)
        dga_ref[...] = (dya * ov * _silu_grad(ga)).astype(BF16)

    row = pl.BlockSpec((tt, CW), lambda i: (i, 0))
    vec = pl.BlockSpec((1, CW), lambda i: (0, 0))
    big = jax.ShapeDtypeStruct((T, CW), BF16)
    small = jax.ShapeDtypeStruct((1, CW), F32)
    return pl.pallas_call(
        body, name=name, grid=(T // tt,),
        in_specs=[pl.BlockSpec((tt, 2 * CW), lambda i: (i, 0)), row,
                  pl.BlockSpec((tt, CW), lambda i: (i, gc_off // CW)),
                  pl.BlockSpec((tt, CW), lambda i: (i, ga_off // CW)), row, vec, vec],
        out_specs=[row, row, row, row, vec, vec],
        out_shape=[big, big, big, big, small, small],
        compiler_params=_cp("arbitrary"))(dy, u2, proj, proj, o, ln_w, ln_b)


SB_UNDERFLOW = 110.0
SB_BOUND_MARGIN = 1.02


def _sb_tile(seq):
    return _pick(seq, 256, LANE)


def _softplus(z):
    return jnp.maximum(z, 0.0) + jnp.log(1.0 + jnp.exp(-jnp.abs(z)))


def _tri01(n, lower):
    i = lax.broadcasted_iota(jnp.int32, (n, n), 0)
    j = lax.broadcasted_iota(jnp.int32, (n, n), 1)
    return ((i >= j) if lower else (i <= j)).astype(BF16)


def _sb_heads_per_step(heads):
    return 2 if heads % 2 == 0 else 1


def sba_fwd(proj, *, B, seq, heads, q_off, k_off, v_off, name):
    dh = SB_HEAD_DIM
    tq = _sb_tile(seq)
    nq = seq // tq
    hps = _sb_heads_per_step(heads)
    hw = hps * dh
    scale = dh ** -0.5

    def body(q_ref, k_ref, v_ref, o_ref, ct_ref, acc_ref, kmax_ref):
        qi = pl.program_id(1)
        tri = _tri01(tq, True)
        below = lax.broadcasted_iota(jnp.int32, (tq, tq), 1) < lax.broadcasted_iota(jnp.int32, (tq, tq), 0)
        qs = [(q_ref[:, h * dh:(h + 1) * dh].astype(F32) * scale).astype(BF16) for h in range(hps)]

        @pl.when(qi == 0)
        def _():
            def chunk(i, best):
                rows = k_ref[pl.ds(pl.multiple_of(i * tq, tq), tq), :].astype(F32)
                sq = rows * rows
                return tuple(jnp.maximum(best[h], jnp.max(jnp.sum(sq[:, h * dh:(h + 1) * dh], axis=1, keepdims=True),
                                                          axis=0, keepdims=True)) for h in range(hps))

            best = lax.fori_loop(0, nq, chunk, (jnp.zeros((1, 1), F32),) * hps)
            for h in range(hps):
                kmax_ref[h] = jnp.broadcast_to(jnp.sqrt(best[h]), (8, LANE))

        z_bound = [jnp.sqrt(jnp.sum(qs[h].astype(F32) ** 2, axis=1, keepdims=True))
                   * (SB_BOUND_MARGIN * jnp.max(kmax_ref[h], keepdims=True)) for h in range(hps)]

        def block(start, rs, diag):
            pvs, out = [], []
            for h in range(hps):
                k_blk = k_ref[pl.ds(start, tq), h * dh:(h + 1) * dh]
                v_blk = v_ref[pl.ds(start, tq), h * dh:(h + 1) * dh]
                z = _dot(qs[h], k_blk, NT)
                sp = _softplus(z)
                if diag:
                    sp = jnp.where(below, sp, 0.0)
                wts = jnp.exp(z - (_dot(sp.astype(BF16), tri) + rs[h]))
                if diag:
                    wts = jnp.where(below, wts, 0.0)
                pvs.append(_dot(wts.astype(BF16), v_blk))
                out.append(rs[h] + jnp.sum(sp, axis=-1, keepdims=True))
            return pvs, tuple(out)

        zero = jnp.zeros((tq, 1), F32)
        pvs, rs = block(pl.multiple_of(qi * tq, tq), (zero,) * hps, True)
        for h in range(hps):
            acc_ref[:, h * dh:(h + 1) * dh] = pvs[h]

        def more(c):
            j, rs = c
            slack = rs[0] - z_bound[0]
            for h in range(1, hps):
                slack = jnp.minimum(slack, rs[h] - z_bound[h])
            return jnp.logical_and(j < qi, jnp.min(slack) <= SB_UNDERFLOW)

        def step(c):
            j, rs = c
            pvs, rs = block(pl.multiple_of((qi - 1 - j) * tq, tq), rs, False)
            for h in range(hps):
                acc_ref[:, h * dh:(h + 1) * dh] += pvs[h]
            return j + 1, rs

        n_left, totals = lax.while_loop(more, step, (jnp.int32(0), rs))
        o_ref[...] = acc_ref[...].astype(BF16)
        for h in range(hps):
            ct_ref[0, 8 * h:8 * h + 8, :] = jnp.broadcast_to(totals[h], (tq, LANE)).T[0:8, :]
        ct_ref[0, 8 * hps:8 * hps + 8, :] = jnp.full((8, tq), n_left, F32)

    qb, kb, vb = q_off // hw, k_off // hw, v_off // hw
    G = heads // hps
    return pl.pallas_call(
        body, name=name, grid=(B * G, nq),
        in_specs=[pl.BlockSpec((tq, hw), lambda g, i: ((g // G) * nq + i, qb + g % G)),
                  pl.BlockSpec((seq, hw), lambda g, i: (g // G, kb + g % G)),
                  pl.BlockSpec((seq, hw), lambda g, i: (g // G, vb + g % G))],
        out_specs=[pl.BlockSpec((tq, hw), lambda g, i: ((g // G) * nq + i, g % G)),
                   pl.BlockSpec((1, 8 * hps + 8, tq), lambda g, i: (g * nq + i, 0, 0))],
        out_shape=[jax.ShapeDtypeStruct((B * seq, heads * dh), BF16),
                   jax.ShapeDtypeStruct((B * G * nq, 8 * hps + 8, tq), F32)],
        scratch_shapes=[pltpu.VMEM((tq, hw), F32), pltpu.VMEM((hps, 8, LANE), F32)],
        compiler_params=_cp("parallel", "arbitrary"))(proj, proj, proj)


def sba_bwd(proj, ctot, do, *, B, seq, heads, q_off, k_off, v_off, name):
    dh = SB_HEAD_DIM
    tq = _sb_tile(seq)
    nq = seq // tq
    hps = _sb_heads_per_step(heads)
    hw = hps * dh
    scale = dh ** -0.5

    def body(q_ref, k_ref, v_ref, ct_ref, do_ref, dq_ref, dk_ref, dv_ref, dq_acc, dk_acc, dv_acc):
        qi = pl.program_id(1)

        @pl.when(qi == 0)
        def _():
            dk_acc[...] = jnp.zeros_like(dk_acc)
            dv_acc[...] = jnp.zeros_like(dv_acc)

        tri_sfx = _tri01(tq, True)
        tri_pre = _tri01(tq, False)
        below = lax.broadcasted_iota(jnp.int32, (tq, tq), 1) < lax.broadcasted_iota(jnp.int32, (tq, tq), 0)
        qs = [(q_ref[:, h * dh:(h + 1) * dh].astype(F32) * scale).astype(BF16) for h in range(hps)]
        dos = [do_ref[:, h * dh:(h + 1) * dh] for h in range(hps)]
        totals = [jnp.max(jnp.broadcast_to(ct_ref[0, 8 * h:8 * h + 1, :], (LANE, tq)).T, axis=1, keepdims=True)
                  for h in range(hps)]
        dq_acc[...] = jnp.zeros_like(dq_acc)

        def block(start, carry, diag):
            out = []
            for h in range(hps):
                pc, pg = carry[h]
                cols = slice(h * dh, (h + 1) * dh)
                k_blk = k_ref[pl.ds(start, tq), cols]
                v_blk = v_ref[pl.ds(start, tq), cols]
                z = _dot(qs[h], k_blk, NT)
                sp = _softplus(z)
                sig = jnp.exp(z - sp)
                if diag:
                    sp = jnp.where(below, sp, 0.0)
                pc_next = pc + jnp.sum(sp, axis=-1, keepdims=True)
                wts = jnp.exp(z - (_dot(sp.astype(BF16), tri_sfx) + (totals[h] - pc_next)))
                if diag:
                    wts = jnp.where(below, wts, 0.0)
                g = _dot(dos[h], v_blk, NT) * wts
                dz = g - sig * (_dot(g.astype(BF16), tri_pre) + pg)
                if diag:
                    dz = jnp.where(below, dz, 0.0)
                dz = dz.astype(BF16)
                dq_acc[:, cols] += _dot(dz, k_blk)
                dk_acc[pl.ds(start, tq), cols] += _dot(dz, qs[h], TN)
                dv_acc[pl.ds(start, tq), cols] += _dot(wts.astype(BF16), dos[h], TN)
                out.append((pc_next, pg + jnp.sum(g, axis=-1, keepdims=True)))
            return tuple(out)

        zero = jnp.zeros((tq, 1), F32)
        n_left = jnp.max(ct_ref[0, 8 * hps:8 * hps + 8, :]).astype(jnp.int32)
        carry = lax.fori_loop(qi - n_left, qi, lambda j, c: block(pl.multiple_of(j * tq, tq), c, False),
                              ((zero, zero),) * hps)
        block(pl.multiple_of(qi * tq, tq), carry, True)
        dq_ref[...] = (dq_acc[...] * scale).astype(BF16)

        @pl.when(qi == nq - 1)
        def _():
            dk_ref[...] = dk_acc[...].astype(BF16)
            dv_ref[...] = dv_acc[...].astype(BF16)

    qb, kb, vb = q_off // hw, k_off // hw, v_off // hw
    G = heads // hps
    q_spec = pl.BlockSpec((tq, hw), lambda g, i: ((g // G) * nq + i, qb + g % G))
    o_spec = pl.BlockSpec((tq, hw), lambda g, i: ((g // G) * nq + i, g % G))
    kv_out = pl.BlockSpec((seq, hw), lambda g, i: (g // G, g % G))
    shp = jax.ShapeDtypeStruct((B * seq, heads * dh), BF16)
    return pl.pallas_call(
        body, name=name, grid=(B * G, nq),
        in_specs=[q_spec,
                  pl.BlockSpec((seq, hw), lambda g, i: (g // G, kb + g % G)),
                  pl.BlockSpec((seq, hw), lambda g, i: (g // G, vb + g % G)),
                  pl.BlockSpec((1, 8 * hps + 8, tq), lambda g, i: (g * nq + i, 0, 0)), o_spec],
        out_specs=[o_spec, kv_out, kv_out], out_shape=[shp, shp, shp],
        scratch_shapes=[pltpu.VMEM((tq, hw), F32), pltpu.VMEM((seq, hw), F32), pltpu.VMEM((seq, hw), F32)],
        compiler_params=_cp("parallel", "arbitrary"))(proj, proj, proj, ctot, do)


def _chunk_tri(lower):
    i = lax.broadcasted_iota(jnp.int32, (SSM_CHUNK, SSM_CHUNK), 0)
    j = lax.broadcasted_iota(jnp.int32, (SSM_CHUNK, SSM_CHUNK), 1)
    return ((i >= j) if lower else (i <= j)).astype(BF16)


def ssm_dt_fwd(proj, bias, a_log, *, dt_off, name):
    T = proj.shape[0]
    L = SSM_CHUNK
    tt = _pick(T, 512, L)

    def body(raw_ref, bias_ref, al_ref, dt_ref, cs_ref):
        x = raw_ref[...].astype(F32) + bias_ref[...]
        dt = jnp.maximum(x, 0.0) + jnp.log(1.0 + jnp.exp(-jnp.abs(x)))
        dt_ref[...] = dt
        la = dt * (-jnp.exp(al_ref[...]))
        tri = _chunk_tri(True)
        for c in range(tt // L):
            cs_ref[c * L:(c + 1) * L, :] = _tri_dot3(tri, la[c * L:(c + 1) * L, :])

    row = pl.BlockSpec((tt, LANE), lambda i: (i, 0))
    vec = pl.BlockSpec((1, LANE), lambda i: (0, 0))
    shp = jax.ShapeDtypeStruct((T, LANE), F32)
    return pl.pallas_call(
        body, name=name, grid=(T // tt,),
        in_specs=[pl.BlockSpec((tt, LANE), lambda i: (i, dt_off // LANE)), vec, vec],
        out_specs=[row, row], out_shape=[shp, shp], compiler_params=_cp("parallel"))(proj, bias, a_log)


def ssm_dt_bwd(ddt, dcs, proj, dt, bias, a_log, *, dt_off, n_heads, name):
    T = proj.shape[0]
    L = SSM_CHUNK
    tt = _pick(T, 512, L)

    def body(ddt_ref, dcs_ref, raw_ref, dt_ref, bias_ref, al_ref, draw_ref, dbias_ref, dal_ref, dla_buf):
        @pl.when(pl.program_id(0) == 0)
        def _():
            dbias_ref[...] = jnp.zeros_like(dbias_ref)
            dal_ref[...] = jnp.zeros_like(dal_ref)

        triu = _chunk_tri(False)
        dcs = dcs_ref[...]
        for c in range(tt // L):
            dla_buf[c * L:(c + 1) * L, :] = _tri_dot3(triu, dcs[c * L:(c + 1) * L, :])
        dla = dla_buf[...]
        a = -jnp.exp(al_ref[...])
        dtv = dt_ref[...]
        valid = lax.broadcasted_iota(jnp.int32, (tt, LANE), 1) < n_heads
        dal_ref[...] += jnp.sum(jnp.where(valid, dla * dtv, 0.0), axis=0, keepdims=True) * a
        x = raw_ref[...].astype(F32) + bias_ref[...]
        draw = jnp.where(valid, (ddt_ref[...] + dla * a) * _sigmoid(x), 0.0)
        dbias_ref[...] += jnp.sum(draw, axis=0, keepdims=True)
        draw_ref[...] = draw.astype(BF16)

    row = pl.BlockSpec((tt, LANE), lambda i: (i, 0))
    vec = pl.BlockSpec((1, LANE), lambda i: (0, 0))
    return pl.pallas_call(
        body, name=name, grid=(T // tt,),
        in_specs=[row, row, pl.BlockSpec((tt, LANE), lambda i: (i, dt_off // LANE)), row, vec, vec],
        out_specs=[row, vec, vec],
        out_shape=[jax.ShapeDtypeStruct((T, LANE), BF16), jax.ShapeDtypeStruct((1, LANE), F32),
                   jax.ShapeDtypeStruct((1, LANE), F32)],
        scratch_shapes=[pltpu.VMEM((tt, LANE), F32)],
        compiler_params=_cp("arbitrary"))(ddt, dcs, proj, dt, bias, a_log)


def _colb(row):
    return jnp.broadcast_to(row, (LANE, SSM_CHUNK)).T


def _ssd_pair_common(x_ref, dt_ref, cs_ref, pair, ppg, lo_half, causal, lane_row):
    L = SSM_CHUNK
    g, pp = divmod(pair, ppg)
    ra = g * HEAD_ROWS + 2 * pp
    X = x_ref[:, pair * LANE:(pair + 1) * LANE].astype(F32)
    dta, dtb = dt_ref[0, ra:ra + 1, :], dt_ref[0, ra + 1:ra + 2, :]
    csa, csb = cs_ref[0, ra:ra + 1, :], cs_ref[0, ra + 1:ra + 2, :]
    csa_c, csb_c = _colb(csa), _colb(csb)
    dt_p = jnp.where(lo_half, _colb(dta), _colb(dtb))
    La = jnp.exp(jnp.where(causal, csa_c - csa, NEG_BIG))
    Lb = jnp.exp(jnp.where(causal, csb_c - csb, NEG_BIG))
    last_a = jnp.sum(jnp.where(lane_row == L - 1, csa, 0.0), axis=1, keepdims=True)
    last_b = jnp.sum(jnp.where(lane_row == L - 1, csb, 0.0), axis=1, keepdims=True)
    ecs = jnp.exp(jnp.where(lo_half, csa_c, csb_c))
    tail = jnp.exp(jnp.where(lo_half, last_a - csa_c, last_b - csb_c))
    return g, ra, X, dt_p, La, Lb, last_a, last_b, ecs, tail


def ssd_fwd(xbc, dt_row, cs_row, d_full, *, B, seq, DI, name):
    L, N, G = SSM_CHUNK, SSM_STATE, SSM_GROUPS
    nc = seq // L
    XW = xbc.shape[1]
    n_pairs = DI // LANE
    ppg = n_pairs // G

    def body(x_ref, dt_ref, cs_ref, d_ref, y_ref, st_ref, state):
        c = pl.program_id(1)

        @pl.when(c == 0)
        def _():
            state[...] = jnp.zeros_like(state)

        causal = lax.broadcasted_iota(jnp.int32, (L, L), 0) >= lax.broadcasted_iota(jnp.int32, (L, L), 1)
        lo_half = lax.broadcasted_iota(jnp.int32, (L, LANE), 1) < SSM_HEAD_DIM
        rows_lo = lax.broadcasted_iota(jnp.int32, (LANE, N), 0) < SSM_HEAD_DIM
        lane_row = lax.broadcasted_iota(jnp.int32, (1, L), 1)
        cbs = []
        for g in range(G):
            Bc = x_ref[:, DI + g * N:DI + (g + 1) * N]
            Cc = x_ref[:, DI + G * N + g * N:DI + G * N + (g + 1) * N]
            cbs.append((Bc, Cc, _dot(Cc, Bc, NT)))
        for pair in range(n_pairs):
            g, ra, X, dt_p, La, Lb, last_a, last_b, ecs, tail = _ssd_pair_common(
                x_ref, dt_ref, cs_ref, pair, ppg, lo_half, causal, lane_row)
            Bc, Cc, CB = cbs[g]
            xs = X * dt_p
            xsb = xs.astype(BF16)
            y = jnp.where(lo_half, _dot((CB * La).astype(BF16), xsb), _dot((CB * Lb).astype(BF16), xsb))
            S = state[pair]
            st_ref[0, 0, pair] = S
            y = y + ecs * _dot(Cc, S.astype(BF16), NT)
            y = y + d_ref[:, pair * LANE:(pair + 1) * LANE] * X
            y_ref[:, pair * LANE:(pair + 1) * LANE] = y.astype(BF16)
            e_rows = jnp.where(rows_lo, jnp.exp(last_a), jnp.exp(last_b))
            state[pair] = e_rows * S + _dot((xs * tail).astype(BF16), Bc, TN)

    return pl.pallas_call(
        body, name=name, grid=(B, nc),
        in_specs=[pl.BlockSpec((L, XW), lambda b, c: (b * nc + c, 0)),
                  pl.BlockSpec((1, G * HEAD_ROWS, L), lambda b, c: (b, 0, c)),
                  pl.BlockSpec((1, G * HEAD_ROWS, L), lambda b, c: (b, 0, c)),
                  pl.BlockSpec((1, DI), lambda b, c: (0, 0))],
        out_specs=[pl.BlockSpec((L, DI), lambda b, c: (b * nc + c, 0)),
                   pl.BlockSpec((1, 1, n_pairs, LANE, N), lambda b, c: (b, c, 0, 0, 0))],
        out_shape=[jax.ShapeDtypeStruct((B * seq, DI), BF16),
                   jax.ShapeDtypeStruct((B, nc, n_pairs, LANE, N), F32)],
        scratch_shapes=[pltpu.VMEM((n_pairs, LANE, N), F32)],
        compiler_params=_cp("parallel", "arbitrary"))(xbc, dt_row, cs_row, d_full)


def ssd_bwd(xbc, dt_row, cs_row, d_full, states, dy, *, B, seq, DI, name):
    L, N, G = SSM_CHUNK, SSM_STATE, SSM_GROUPS
    nc = seq // L
    XW = xbc.shape[1]
    n_pairs = DI // LANE
    ppg = n_pairs // G
    HR = G * HEAD_ROWS

    def body(x_ref, dt_ref, cs_ref, d_ref, st_ref, dy_ref, dx_ref, ddt_ref, dcs_ref, dd_ref, dH):
        c = pl.program_id(1)

        @pl.when(c == 0)
        def _():
            dH[...] = jnp.zeros_like(dH)
            dd_ref[...] = jnp.zeros_like(dd_ref)

        causal = lax.broadcasted_iota(jnp.int32, (L, L), 0) >= lax.broadcasted_iota(jnp.int32, (L, L), 1)
        lo_half = lax.broadcasted_iota(jnp.int32, (L, LANE), 1) < SSM_HEAD_DIM
        rows_lo = lax.broadcasted_iota(jnp.int32, (LANE, N), 0) < SSM_HEAD_DIM
        lane_row = lax.broadcasted_iota(jnp.int32, (1, L), 1)
        head_row = lax.broadcasted_iota(jnp.int32, (HR, 1), 0)
        ddt_all = jnp.zeros((HR, L), F32)
        dcs_all = jnp.zeros((HR, L), F32)

        def place(row, r):
            return jnp.where(head_row == r, row, 0.0)

        def as_rows(col):
            return jnp.broadcast_to(col, (L, LANE)).T[0:HR, :]

        def head_sums(t, ra):
            sa = as_rows(jnp.sum(jnp.where(lo_half, t, 0.0), axis=1, keepdims=True))
            sb = as_rows(jnp.sum(jnp.where(lo_half, 0.0, t), axis=1, keepdims=True))
            return place(sa, ra) + place(sb, ra + 1)

        for g in range(G):
            Bc = x_ref[:, DI + g * N:DI + (g + 1) * N]
            Cc = x_ref[:, DI + G * N + g * N:DI + G * N + (g + 1) * N]
            CB = _dot(Cc, Bc, NT)
            dCB = jnp.zeros((L, L), F32)
            dC = jnp.zeros((L, N), F32)
            dB = jnp.zeros((L, N), F32)
            for pp in range(ppg):
                pair = g * ppg + pp
                _, ra, X, dt_p, La, Lb, last_a, last_b, ecs, tail = _ssd_pair_common(
                    x_ref, dt_ref, cs_ref, pair, ppg, lo_half, causal, lane_row)
                rb = ra + 1
                xs = X * dt_p
                xsb = xs.astype(BF16)
                Ma, Mb = CB * La, CB * Lb
                dY = dy_ref[:, pair * LANE:(pair + 1) * LANE].astype(F32)
                dYb = dY.astype(BF16)
                dMa = _dot(jnp.where(lo_half, dY, 0.0).astype(BF16), xsb, NT)
                dMb = _dot(jnp.where(lo_half, 0.0, dY).astype(BF16), xsb, NT)
                dSa, dSb = dMa * Ma, dMb * Mb
                dCB = dCB + dMa * La + dMb * Lb
                dcs_all = dcs_all + place(as_rows(jnp.sum(dSa, axis=1, keepdims=True)) - jnp.sum(dSa, axis=0, keepdims=True), ra)
                dcs_all = dcs_all + place(as_rows(jnp.sum(dSb, axis=1, keepdims=True)) - jnp.sum(dSb, axis=0, keepdims=True), rb)
                dxs = jnp.where(lo_half, _dot(Ma.astype(BF16), dYb, TN), _dot(Mb.astype(BF16), dYb, TN))
                S = st_ref[0, 0, pair]
                Sb = S.astype(BF16)
                y_inter = ecs * _dot(Cc, Sb, NT)
                dYe = (dY * ecs).astype(BF16)
                dC = dC + _dot(dYe, Sb)
                dHp = _dot(dYe, Cc, TN)
                dcs_all = dcs_all + head_sums(dY * y_inter, ra)
                dHn = dH[pair]
                dHnb = dHn.astype(BF16)
                ea, eb = jnp.exp(last_a), jnp.exp(last_b)
                dHp = dHp + jnp.where(rows_lo, ea, eb) * dHn
                prod = dHn * S
                dlast_a = ea * jnp.sum(jnp.where(rows_lo, prod, 0.0), keepdims=True)
                dlast_b = eb * jnp.sum(jnp.where(rows_lo, 0.0, prod), keepdims=True)
                XBt = _dot(Bc, dHnb, NT)
                dxs = dxs + tail * XBt
                t2 = xs * XBt * tail
                dlast_a = dlast_a + jnp.sum(jnp.where(lo_half, t2, 0.0), keepdims=True)
                dlast_b = dlast_b + jnp.sum(jnp.where(lo_half, 0.0, t2), keepdims=True)
                dcs_all = dcs_all - head_sums(t2, ra)
                dcs_all = dcs_all + place(jnp.where(lane_row == L - 1, dlast_a, 0.0), ra)
                dcs_all = dcs_all + place(jnp.where(lane_row == L - 1, dlast_b, 0.0), rb)
                dB = dB + _dot((xs * tail).astype(BF16), dHnb)
                dfull = d_ref[:, pair * LANE:(pair + 1) * LANE]
                dx_ref[:, pair * LANE:(pair + 1) * LANE] = (dxs * dt_p + dfull * dY).astype(BF16)
                ddt_all = ddt_all + head_sums(dxs * X, ra)
                dd_ref[0, :, pair * LANE:(pair + 1) * LANE] += jnp.sum(dY * X, axis=0, keepdims=True)
                dH[pair] = dHp
            dCBb = dCB.astype(BF16)
            dx_ref[:, DI + g * N:DI + (g + 1) * N] = (dB + _dot(dCBb, Cc, TN)).astype(BF16)
            dx_ref[:, DI + G * N + g * N:DI + G * N + (g + 1) * N] = (dC + _dot(dCBb, Bc)).astype(BF16)
        ddt_ref[0] = ddt_all
        dcs_ref[0] = dcs_all

    rev = lambda b, c: (b * nc + (nc - 1 - c), 0)
    hrow = pl.BlockSpec((1, HR, L), lambda b, c: (b, 0, nc - 1 - c))
    return pl.pallas_call(
        body, name=name, grid=(B, nc),
        in_specs=[pl.BlockSpec((L, XW), rev), hrow, hrow,
                  pl.BlockSpec((1, DI), lambda b, c: (0, 0)),
                  pl.BlockSpec((1, 1, n_pairs, LANE, N), lambda b, c: (b, nc - 1 - c, 0, 0, 0)),
                  pl.BlockSpec((L, DI), rev)],
        out_specs=[pl.BlockSpec((L, XW), rev), hrow, hrow, pl.BlockSpec((1, 1, DI), lambda b, c: (b, 0, 0))],
        out_shape=[jax.ShapeDtypeStruct((B * seq, XW), BF16),
                   jax.ShapeDtypeStruct((B, HR, seq), F32), jax.ShapeDtypeStruct((B, HR, seq), F32),
                   jax.ShapeDtypeStruct((B, 1, DI), F32)],
        scratch_shapes=[pltpu.VMEM((n_pairs, LANE, N), F32)],
        compiler_params=_cp("parallel", "arbitrary"))(xbc, dt_row, cs_row, d_full, states, dy)


def _head_expand(n_heads, DI):
    j = jnp.arange(LANE, dtype=jnp.int32)[:, None]
    c = jnp.arange(DI, dtype=jnp.int32)[None, :] // SSM_HEAD_DIM
    return ((j == c) & (j < n_heads)).astype(BF16)


def _split3(x):
    hi = x.astype(BF16)
    r1 = x - hi.astype(F32)
    mid = r1.astype(BF16)
    return hi, mid, (r1 - mid.astype(F32)).astype(BF16)


def dt_fwd(proj, bias, a_log, expand, *, dt_off, name):
    T = proj.shape[0]
    DI = expand.shape[1]
    L = SSM_CHUNK
    tt = _pick(T, 512, L)

    def body(raw_ref, bias_ref, al_ref, e_ref, dt_ref, cs_ref, dtx_ref, csx_ref):
        x = raw_ref[...].astype(F32) + bias_ref[...]
        dt = _softplus(x)
        dt_ref[...] = dt
        la = dt * (-jnp.exp(al_ref[...]))
        tri = _tri01(L, True)
        for c in range(tt // L):
            cs_ref[c * L:(c + 1) * L, :] = _tri_dot3(tri, la[c * L:(c + 1) * L, :])
        e = e_ref[...]
        dtx_ref[...] = _dot(dt.astype(BF16), e).astype(BF16)
        hi, mid, lo = _split3(cs_ref[...])
        csx_ref[...] = _dot(hi, e) + _dot(mid, e) + _dot(lo, e)

    row = pl.BlockSpec((tt, LANE), lambda i: (i, 0))
    wide = pl.BlockSpec((tt, DI), lambda i: (i, 0))
    vec = pl.BlockSpec((1, LANE), lambda i: (0, 0))
    return pl.pallas_call(
        body, name=name, grid=(T // tt,),
        in_specs=[pl.BlockSpec((tt, LANE), lambda i: (i, dt_off // LANE)), vec, vec,
                  pl.BlockSpec((LANE, DI), lambda i: (0, 0))],
        out_specs=[row, row, wide, wide],
        out_shape=[jax.ShapeDtypeStruct((T, LANE), F32), jax.ShapeDtypeStruct((T, LANE), F32),
                   jax.ShapeDtypeStruct((T, DI), BF16), jax.ShapeDtypeStruct((T, DI), F32)],
        compiler_params=_cp("parallel"))(proj, bias, a_log, expand)


def dt_bwd(ddt_x, dcs_x, dcs_cols, proj, dt, bias, a_log, reduce_t, *, dt_off, n_heads, name):
    T = proj.shape[0]
    DI = reduce_t.shape[0]
    L = SSM_CHUNK
    tt = _pick(T, 512, L)

    def body(ddtx_ref, dcsx_ref, dcsc_ref, raw_ref, dt_ref, bias_ref, al_ref, r_ref, draw_ref, dbias_ref, dal_ref,
             dla_buf):
        @pl.when(pl.program_id(0) == 0)
        def _():
            dbias_ref[...] = jnp.zeros_like(dbias_ref)
            dal_ref[...] = jnp.zeros_like(dal_ref)

        r = r_ref[...]
        ddt = _dot(ddtx_ref[...], r)
        dx = dcsx_ref[...]
        hi = dx.astype(BF16)
        dcs = _dot(hi, r) + _dot((dx - hi.astype(F32)).astype(BF16), r) + dcsc_ref[...]
        triu = _tri01(L, False)
        for c in range(tt // L):
            dla_buf[c * L:(c + 1) * L, :] = _tri_dot3(triu, dcs[c * L:(c + 1) * L, :])
        dla = dla_buf[...]
        a = -jnp.exp(al_ref[...])
        valid = lax.broadcasted_iota(jnp.int32, (tt, LANE), 1) < n_heads
        dal_ref[...] += jnp.sum(jnp.where(valid, dla * dt_ref[...], 0.0), axis=0, keepdims=True) * a
        x = raw_ref[...].astype(F32) + bias_ref[...]
        draw = jnp.where(valid, (ddt + dla * a) * _sigmoid(x), 0.0)
        dbias_ref[...] += jnp.sum(draw, axis=0, keepdims=True)
        draw_ref[...] = draw.astype(BF16)

    row = pl.BlockSpec((tt, LANE), lambda i: (i, 0))
    wide = pl.BlockSpec((tt, DI), lambda i: (i, 0))
    vec = pl.BlockSpec((1, LANE), lambda i: (0, 0))
    return pl.pallas_call(
        body, name=name, grid=(T // tt,),
        in_specs=[wide, wide, row, pl.BlockSpec((tt, LANE), lambda i: (i, dt_off // LANE)), row, vec, vec,
                  pl.BlockSpec((DI, LANE), lambda i: (0, 0))],
        out_specs=[row, vec, vec],
        out_shape=[jax.ShapeDtypeStruct((T, LANE), BF16), jax.ShapeDtypeStruct((1, LANE), F32),
                   jax.ShapeDtypeStruct((1, LANE), F32)],
        scratch_shapes=[pltpu.VMEM((tt, LANE), F32)],
        compiler_params=_cp("arbitrary"))(ddt_x, dcs_x, dcs_cols, proj, dt, bias, a_log, reduce_t)


def _pair_terms(x_ref, dtx_ref, csx_ref, csr_ref, pair, ppg, lo_half, causal):
    L = SSM_CHUNK
    g, pp = divmod(pair, ppg)
    ra = g * HEAD_ROWS + 2 * pp
    cols = slice(pair * LANE, (pair + 1) * LANE)
    X = x_ref[:, cols].astype(F32)
    dt_p = dtx_ref[:, cols].astype(F32)
    own = csx_ref[:, cols]
    other = pltpu.roll(own, SSM_HEAD_DIM, 1)
    csa_c = jnp.where(lo_half, own, other)
    csb_c = jnp.where(lo_half, other, own)
    La = jnp.exp(jnp.where(causal, csa_c - csr_ref[0, ra:ra + 1, :], NEG_BIG))
    Lb = jnp.exp(jnp.where(causal, csb_c - csr_ref[0, ra + 1:ra + 2, :], NEG_BIG))
    last = csx_ref[L - 1:L, cols]
    return g, ra, cols, X, dt_p, La, Lb, jnp.exp(own), jnp.exp(last - own), jnp.exp(last)


def scan_fwd(xbc, dt_x, cs_x, cs_row, d_full, *, B, seq, DI, name):
    L, N, G = SSM_CHUNK, SSM_STATE, SSM_GROUPS
    nc = seq // L
    XW = xbc.shape[1]
    n_pairs = DI // LANE
    ppg = n_pairs // G

    def body(x_ref, dtx_ref, csx_ref, csr_ref, d_ref, y_ref, st_ref, state):
        c = pl.program_id(1)

        @pl.when(c == 0)
        def _():
            state[...] = jnp.zeros_like(state)

        causal = lax.broadcasted_iota(jnp.int32, (L, L), 0) >= lax.broadcasted_iota(jnp.int32, (L, L), 1)
        lo_half = lax.broadcasted_iota(jnp.int32, (L, LANE), 1) < SSM_HEAD_DIM
        cbs = []
        for g in range(G):
            Bc = x_ref[:, DI + g * N:DI + (g + 1) * N]
            Cc = x_ref[:, DI + G * N + g * N:DI + G * N + (g + 1) * N]
            cbs.append((Bc, Cc, _dot(Cc, Bc, NT)))
        for pair in range(n_pairs):
            g, _, cols, X, dt_p, La, Lb, ecs, tail, e_last = _pair_terms(
                x_ref, dtx_ref, csx_ref, csr_ref, pair, ppg, lo_half, causal)
            Bc, Cc, CB = cbs[g]
            xs = X * dt_p
            xsb = xs.astype(BF16)
            y = jnp.where(lo_half, _dot((CB * La).astype(BF16), xsb), _dot((CB * Lb).astype(BF16), xsb))
            ST = state[pair]
            st_ref[0, 0, pair] = ST
            y = y + ecs * _dot(Cc, ST.astype(BF16)) + d_ref[:, cols] * X
            y_ref[:, cols] = y.astype(BF16)
            state[pair] = e_last * ST + _dot(Bc, (xs * tail).astype(BF16), TN)

    wide = pl.BlockSpec((L, DI), lambda b, c: (b * nc + c, 0))
    return pl.pallas_call(
        body, name=name, grid=(B, nc),
        in_specs=[pl.BlockSpec((L, XW), lambda b, c: (b * nc + c, 0)), wide, wide,
                  pl.BlockSpec((1, G * HEAD_ROWS, L), lambda b, c: (b, 0, c)),
                  pl.BlockSpec((1, DI), lambda b, c: (0, 0))],
        out_specs=[wide, pl.BlockSpec((1, 1, n_pairs, N, LANE), lambda b, c: (b, c, 0, 0, 0))],
        out_shape=[jax.ShapeDtypeStruct((B * seq, DI), BF16),
                   jax.ShapeDtypeStruct((B, nc, n_pairs, N, LANE), F32)],
        scratch_shapes=[pltpu.VMEM((n_pairs, N, LANE), F32)],
        compiler_params=_cp("parallel", "arbitrary"))(xbc, dt_x, cs_x, cs_row, d_full)


def scan_bwd(xbc, dt_x, cs_x, cs_row, d_full, states, dy, *, B, seq, DI, name):
    L, N, G = SSM_CHUNK, SSM_STATE, SSM_GROUPS
    nc = seq // L
    XW = xbc.shape[1]
    n_pairs = DI // LANE
    ppg = n_pairs // G
    HR = G * HEAD_ROWS
    inv_p = 1.0 / SSM_HEAD_DIM

    def body(x_ref, dtx_ref, csx_ref, csr_ref, d_ref, st_ref, dy_ref, dx_ref, ddtx_ref, dcsx_ref, dcsr_ref, dd_ref,
             dH):
        c = pl.program_id(1)

        @pl.when(c == 0)
        def _():
            dH[...] = jnp.zeros_like(dH)
            dd_ref[...] = jnp.zeros_like(dd_ref)

        causal = lax.broadcasted_iota(jnp.int32, (L, L), 0) >= lax.broadcasted_iota(jnp.int32, (L, L), 1)
        lo_half = lax.broadcasted_iota(jnp.int32, (L, LANE), 1) < SSM_HEAD_DIM
        last_row = lax.broadcasted_iota(jnp.int32, (L, LANE), 0) == L - 1
        head_row = lax.broadcasted_iota(jnp.int32, (HR, 1), 0)
        dcs_rows = jnp.zeros((HR, L), F32)

        for g in range(G):
            Bc = x_ref[:, DI + g * N:DI + (g + 1) * N]
            Cc = x_ref[:, DI + G * N + g * N:DI + G * N + (g + 1) * N]
            CB = _dot(Cc, Bc, NT)
            dCB = jnp.zeros((L, L), F32)
            dC = jnp.zeros((L, N), F32)
            dB = jnp.zeros((L, N), F32)
            for pp in range(ppg):
                pair = g * ppg + pp
                _, ra, cols, X, dt_p, La, Lb, ecs, tail, e_last = _pair_terms(
                    x_ref, dtx_ref, csx_ref, csr_ref, pair, ppg, lo_half, causal)
                xs = X * dt_p
                xsb = xs.astype(BF16)
                Ma, Mb = CB * La, CB * Lb
                dY = dy_ref[:, cols].astype(F32)
                dYb = dY.astype(BF16)
                dMa = _dot(jnp.where(lo_half, dY, 0.0).astype(BF16), xsb, NT)
                dMb = _dot(jnp.where(lo_half, 0.0, dY).astype(BF16), xsb, NT)
                dSa, dSb = dMa * Ma, dMb * Mb
                dCB = dCB + dMa * La + dMb * Lb
                dcs = jnp.where(lo_half, jnp.sum(dSa, axis=1, keepdims=True), jnp.sum(dSb, axis=1, keepdims=True)) * inv_p
                dcs_rows = dcs_rows - jnp.where(head_row == ra, jnp.sum(dSa, axis=0, keepdims=True), 0.0)
                dcs_rows = dcs_rows - jnp.where(head_row == ra + 1, jnp.sum(dSb, axis=0, keepdims=True), 0.0)
                dxs = jnp.where(lo_half, _dot(Ma.astype(BF16), dYb, TN), _dot(Mb.astype(BF16), dYb, TN))
                ST = st_ref[0, 0, pair]
                STb = ST.astype(BF16)
                dYe = (dY * ecs).astype(BF16)
                dC = dC + _dot(dYe, STb, NT)
                dSTp = _dot(Cc, dYe, TN)
                dcs = dcs + dY * (ecs * _dot(Cc, STb))
                dSTn = dH[pair]
                dSTnb = dSTn.astype(BF16)
                dSTp = dSTp + e_last * dSTn
                XBt = _dot(Bc, dSTnb)
                dxs = dxs + tail * XBt
                t2 = xs * XBt * tail
                at_end = e_last * jnp.sum(dSTn * ST, axis=0, keepdims=True) + jnp.sum(t2, axis=0, keepdims=True)
                dcs = dcs - t2 + jnp.where(last_row, at_end, 0.0)
                dB = dB + _dot((xs * tail).astype(BF16), dSTnb, NT)
                dx_ref[:, cols] = (dxs * dt_p + d_ref[:, cols] * dY).astype(BF16)
                ddtx_ref[:, cols] = (dxs * X).astype(BF16)
                dcsx_ref[:, cols] = dcs
                dd_ref[0, :, cols] += jnp.sum(dY * X, axis=0, keepdims=True)
                dH[pair] = dSTp
            dCBb = dCB.astype(BF16)
            dx_ref[:, DI + g * N:DI + (g + 1) * N] = (dB + _dot(dCBb, Cc, TN)).astype(BF16)
            dx_ref[:, DI + G * N + g * N:DI + G * N + (g + 1) * N] = (dC + _dot(dCBb, Bc)).astype(BF16)
        dcsr_ref[0] = dcs_rows

    rev = lambda b, c: (b * nc + (nc - 1 - c), 0)
    wide = pl.BlockSpec((L, DI), rev)
    hrow = pl.BlockSpec((1, HR, L), lambda b, c: (b, 0, nc - 1 - c))
    return pl.pallas_call(
        body, name=name, grid=(B, nc),
        in_specs=[pl.BlockSpec((L, XW), rev), wide, wide, hrow,
                  pl.BlockSpec((1, DI), lambda b, c: (0, 0)),
                  pl.BlockSpec((1, 1, n_pairs, N, LANE), lambda b, c: (b, nc - 1 - c, 0, 0, 0)), wide],
        out_specs=[pl.BlockSpec((L, XW), rev), wide, wide, hrow, pl.BlockSpec((1, 1, DI), lambda b, c: (b, 0, 0))],
        out_shape=[jax.ShapeDtypeStruct((B * seq, XW), BF16), jax.ShapeDtypeStruct((B * seq, DI), BF16),
                   jax.ShapeDtypeStruct((B * seq, DI), F32), jax.ShapeDtypeStruct((B, HR, seq), F32),
                   jax.ShapeDtypeStruct((B, 1, DI), F32)],
        scratch_shapes=[pltpu.VMEM((n_pairs, N, LANE), F32)],
        compiler_params=_cp("parallel", "arbitrary"))(xbc, dt_x, cs_x, cs_row, d_full, states, dy)


def gnorm_fwd(y, proj, w, *, DI, name):
    T = y.shape[0]
    tt = _pick(T, 256, 16)
    gw = DI // SSM_GROUPS

    def body(y_ref, z_ref, w_ref, o_ref):
        for g in range(SSM_GROUPS):
            sl = slice(g * gw, (g + 1) * gw)
            y2 = y_ref[:, sl].astype(F32) * _silu(z_ref[:, sl].astype(F32))
            r = lax.rsqrt(jnp.mean(y2 * y2, axis=-1, keepdims=True) + EPS)
            o_ref[:, sl] = (y2 * r * w_ref[:, sl]).astype(BF16)

    row = pl.BlockSpec((tt, DI), lambda i: (i, 0))
    return pl.pallas_call(
        body, name=name, grid=(T // tt,),
        in_specs=[row, row, pl.BlockSpec((1, DI), lambda i: (0, 0))], out_specs=row,
        out_shape=jax.ShapeDtypeStruct((T, DI), BF16), compiler_params=_cp("parallel"))(y, proj, w)


def gnorm_bwd(dyn, y, proj, w, *, DI, name):
    T = y.shape[0]
    tt = _pick(T, 256, 16)
    gw = DI // SSM_GROUPS

    def body(dyn_ref, y_ref, z_ref, w_ref, dy_ref, dz_ref, dw_ref):
        @pl.when(pl.program_id(0) == 0)
        def _():
            dw_ref[...] = jnp.zeros_like(dw_ref)

        for g in range(SSM_GROUPS):
            sl = slice(g * gw, (g + 1) * gw)
            yv = y_ref[:, sl].astype(F32)
            z = z_ref[:, sl].astype(F32)
            sz = _silu(z)
            y2 = yv * sz
            r = lax.rsqrt(jnp.mean(y2 * y2, axis=-1, keepdims=True) + EPS)
            xhat = y2 * r
            d = dyn_ref[:, sl].astype(F32)
            dw_ref[:, sl] += jnp.sum(d * xhat, axis=0, keepdims=True)
            dxh = d * w_ref[:, sl]
            dy2 = r * (dxh - xhat * jnp.mean(dxh * xhat, axis=-1, keepdims=True))
            dy_ref[:, sl] = (dy2 * sz).astype(BF16)
            dz_ref[:, sl] = (dy2 * yv * _silu_grad(z)).astype(BF16)

    row = pl.BlockSpec((tt, DI), lambda i: (i, 0))
    vec = pl.BlockSpec((1, DI), lambda i: (0, 0))
    shp = jax.ShapeDtypeStruct((T, DI), BF16)
    return pl.pallas_call(
        body, name=name, grid=(T // tt,), in_specs=[row, row, row, vec], out_specs=[row, row, vec],
        out_shape=[shp, shp, jax.ShapeDtypeStruct((1, DI), F32)],
        compiler_params=_cp("arbitrary"))(dyn, y, proj, w)


N_CHIP = 4


def exchange(srcs, modes, *, name):
    n = len(srcs)
    out_shape = []
    for s, mode in zip(srcs, modes):
        lead = (N_DEV,) if mode in ("gather", "gather_direct") else ()
        out_shape.append(jax.ShapeDtypeStruct(lead + s.shape, s.dtype))

    def body(*refs):
        src_refs, out_refs = refs[:n], refs[n:2 * n]
        send_sems, recv_sems, local_sems = refs[2 * n:]
        x, y, c = lax.axis_index("x"), lax.axis_index("y"), lax.axis_index("c")
        me, sibling = (x, y, c), (x, y, 1 - c)
        chips = [(1 - x, y), (x, 1 - y), (1 - x, 1 - y)]

        def slot(p):
            return 4 * p[0] + 2 * p[1] + p[2]

        def remote(a, k, src, dst, to):
            return pltpu.make_async_remote_copy(src_ref=src, dst_ref=dst, send_sem=send_sems.at[a, k],
                                                recv_sem=recv_sems.at[a, k], device_id=to,
                                                device_id_type=pl.DeviceIdType.MESH)

        local, two_way, send_only, relays = [], [], [], []
        for a, mode in enumerate(modes):
            src, out = src_refs[a], out_refs[a]
            if mode == "sibling":
                two_way.append(remote(a, 0, src, out, sibling))
            elif mode == "chips":
                mine = 2 * x + y
                local.append(pltpu.make_async_copy(src.at[mine], out.at[mine], local_sems.at[a]))
                for j, chip in enumerate(chips):
                    two_way.append(remote(a, 1 + j, src.at[2 * chip[0] + chip[1]], out.at[mine], (*chip, c)))
            elif mode == "gather_direct":
                local.append(pltpu.make_async_copy(src, out.at[slot(me)], local_sems.at[a]))
                for k in range(1, N_DEV):
                    peer = (1 - x if k & 4 else x, 1 - y if k & 2 else y, 1 - c if k & 1 else c)
                    two_way.append(remote(a, k - 1, src, out.at[slot(me)], peer))
            else:
                assert mode == "gather"
                local.append(pltpu.make_async_copy(src, out.at[slot(me)], local_sems.at[a]))
                send_only.append(remote(a, 0, src, out.at[slot(me)], sibling))
                for j, chip in enumerate(chips):
                    send_only.append(remote(a, 1 + j, src, out.at[slot(me)], (*chip, c)))
                relays.append(a)
        for cp in local + two_way + send_only:
            cp.start()
        for a in relays:
            out = out_refs[a]
            for j, chip in enumerate(chips):
                landed = out.at[slot((*chip, c))]
                remote(a, 1 + j, landed, landed, me).wait_recv()
                fwd = remote(a, 4 + j, landed, landed, sibling)
                fwd.start()
                send_only.append(fwd)
        for a in relays:
            out = out_refs[a]
            blk = out.at[slot(sibling)]
            remote(a, 0, blk, blk, me).wait_recv()
            for j, chip in enumerate(chips):
                blk = out.at[slot((*chip, 1 - c))]
                remote(a, 4 + j, blk, blk, me).wait_recv()
        for cp in send_only:
            cp.wait_send()
        for cp in two_way + local:
            cp.wait()

    any_spec = pl.BlockSpec(memory_space=pl.ANY)
    return pl.pallas_call(
        body, name=name, in_specs=[any_spec] * n, out_specs=[any_spec] * n, out_shape=out_shape,
        scratch_shapes=[pltpu.SemaphoreType.DMA((n, N_DEV - 1)), pltpu.SemaphoreType.DMA((n, N_DEV - 1)),
                        pltpu.SemaphoreType.DMA((n,))],
        compiler_params=pltpu.CompilerParams(has_side_effects=True))(*srcs)


def pair_sum(a, b, *, name):
    n, R, C = a.shape
    tr = _pick(n * R, 512, 16)

    def body(a_ref, b_ref, o_ref):
        o_ref[...] = (a_ref[...].astype(F32) + b_ref[...].astype(F32)).astype(BF16)

    blk = pl.BlockSpec((tr, C), lambda i: (i, 0))
    out = pl.pallas_call(
        body, name=name, grid=(n * R // tr,), in_specs=[blk, blk], out_specs=blk,
        out_shape=jax.ShapeDtypeStruct((n * R, C), BF16),
        compiler_params=_cp("parallel"))(a.reshape(n * R, C), b.reshape(n * R, C))
    return out.reshape(n, R, C)


def sum_slots(recv, *, name):
    _, R, C = recv.shape
    tr = _pick(R, 512, 8)

    def body(r_ref, o_ref):
        acc = r_ref[0].astype(F32)
        for p in range(1, N_DEV):
            acc = acc + r_ref[p].astype(F32)
        o_ref[...] = acc

    return pl.pallas_call(
        body, name=name, grid=(R // tr,),
        in_specs=[pl.BlockSpec((N_DEV, tr, C), lambda i: (0, i, 0))],
        out_specs=pl.BlockSpec((tr, C), lambda i: (i, 0)),
        out_shape=jax.ShapeDtypeStruct((R, C), F32), compiler_params=_cp("parallel"))(recv)


def adamw(gsrc, w, m, v, *, name):
    slots, R, C = gsrc.shape
    tr = _pick(R, 256, 16 if gsrc.dtype == BF16 else 8)
    c1 = 1.0 / (1.0 - ADAM_B1 ** ADAM_STEP)
    c2 = 1.0 / (1.0 - ADAM_B2 ** ADAM_STEP)

    def body(g_ref, w_ref, m_ref, v_ref, go_ref, d_ref, mo_ref, vo_ref):
        g = g_ref[0].astype(F32)
        for p in range(1, slots):
            g = g + g_ref[p].astype(F32)
        m2 = ADAM_B1 * m_ref[...] + (1.0 - ADAM_B1) * g
        v2 = ADAM_B2 * v_ref[...] + (1.0 - ADAM_B2) * (g * g)
        go_ref[...] = g
        mo_ref[...] = m2
        vo_ref[...] = v2
        d_ref[...] = -ADAM_LR * ((m2 * c1) / (jnp.sqrt(v2 * c2) + ADAM_EPS) + ADAM_WD * w_ref[...])

    blk = pl.BlockSpec((tr, C), lambda i: (i, 0))
    shp = jax.ShapeDtypeStruct((R, C), F32)
    return pl.pallas_call(
        body, name=name, grid=(R // tr,),
        in_specs=[pl.BlockSpec((slots, tr, C), lambda i: (0, i, 0)), blk, blk, blk],
        out_specs=[blk] * 4, out_shape=[shp] * 4, compiler_params=_cp("parallel"))(gsrc, w, m, v)


def _pad_cols(a, n):
    return jnp.pad(a, ((0, 0), (0, n - a.shape[1])))


def _to_rows(a, B, seq, H):
    G = SSM_GROUPS
    R = H // G
    t = a[:, :H].reshape(B, seq, G, R).transpose(0, 2, 3, 1)
    t = jnp.pad(t, ((0, 0), (0, 0), (0, HEAD_ROWS - R), (0, 0)))
    return t.reshape(B, G * HEAD_ROWS, seq)


def _from_rows(a, B, seq, H):
    G = SSM_GROUPS
    R = H // G
    t = a.reshape(B, G, HEAD_ROWS, seq)[:, :, :R].transpose(0, 3, 1, 2).reshape(B * seq, H)
    return _pad_cols(t, LANE)


def local_step(x, target, p, *, B, seq):
    T, D = x.shape
    CW = D
    heads = CW // SB_HEAD_DIM
    DI = 2 * D
    H = DI // SSM_HEAD_DIM
    XW = DI + 2 * SSM_GROUPS * SSM_STATE
    in_odd = DI + XW + H
    w1t = p["od_w_in_t"]
    q_off, k_off, v_off, gc_off, ga_off = 3 * CW, 4 * CW, 5 * CW, 2 * CW, 6 * CW
    dt_off = DI + XW

    n0 = rmsnorm_fwd(x, p["ev_norm_w"], name="l0_norm")
    proj0 = mm_nn(n0, p["ev_w_in"], out_dtype=BF16, name="l0_in_proj")
    (u2,) = dwconv_fwd(proj0, (0, CW), p["ev_dw_w"], p["ev_dw_b"], C=CW, seq=seq, glu=True, silu_out=False,
                       name="l0_conv")
    o, ctot = sba_fwd(proj0, B=B, seq=seq, heads=heads, q_off=q_off, k_off=k_off, v_off=v_off, name="l0_attn")
    ycat = mix0_post_fwd(u2, proj0, o, p["ev_ln_w"], p["ev_ln_b"], CW=CW, gc_off=gc_off, ga_off=ga_off,
                         name="l0_post")
    h1 = mm_nn(ycat, p["ev_w_out"], add=x, out_dtype=F32, name="l0_out_proj")

    n1 = rmsnorm_fwd(h1, p["od_norm_w"], name="l1_norm")
    proj1 = mm_nt_terms([(n1, 0, D, 0)], w1t, out_dtype=BF16, name="l1_in_proj")
    u_pre, xbc = dwconv_fwd(proj1, (DI,), p["od_conv_w"], p["od_conv_b"], C=XW, seq=seq, glu=False, silu_out=True,
                            name="l1_conv")
    bias_p, alog_p = _pad_cols(p["od_dt_bias"], LANE), _pad_cols(p["od_a_log"], LANE)
    expand = _head_expand(H, DI)
    dt, cs, dt_x, cs_x = dt_fwd(proj1, bias_p, alog_p, expand, dt_off=dt_off, name="l1_dt")
    cs_row = _to_rows(cs, B, seq, H)
    d_full = jnp.repeat(p["od_d"], SSM_HEAD_DIM, axis=1)
    y_ssd, states = scan_fwd(xbc, dt_x, cs_x, cs_row, d_full, B=B, seq=seq, DI=DI, name="l1_ssd")
    yn = gnorm_fwd(y_ssd, proj1, p["od_gnorm_w"], DI=DI, name="l1_gnorm")
    h2 = mm_nn(yn, p["od_w_out"], add=h1, out_dtype=F32, name="l1_out_proj")

    loss, dh2, g_final = final_loss(h2, p["final_norm_w"], target, name="loss_head")

    dh2b = dh2.astype(BF16)
    g_od_w_out = mm_tn(yn, dh2b, out_dtype=BF16, name="l1_dw_out")
    dyn = mm_nt_terms([(dh2b, 0, D, 0)], p["od_w_out"], out_dtype=BF16, name="l1_d_out_proj")
    dy_ssd, dz, g_gnorm = gnorm_bwd(dyn, y_ssd, proj1, p["od_gnorm_w"], DI=DI, name="l1_gnorm_bwd")
    dxbc_c, ddt_x, dcs_x, dcs_row, dd_part = scan_bwd(xbc, dt_x, cs_x, cs_row, d_full, states, dy_ssd, B=B, seq=seq,
                                                      DI=DI, name="l1_ssd_bwd")
    g_d = dd_part.sum(axis=(0, 1)).reshape(H, SSM_HEAD_DIM).sum(axis=1)[None, :]
    draw, g_bias, g_alog = dt_bwd(ddt_x, dcs_x, _from_rows(dcs_row, B, seq, H), proj1, dt, bias_p, alog_p, expand.T,
                                  dt_off=dt_off, n_heads=H, name="l1_dt_bwd")
    dxbc, g_conv_w, g_conv_b = dwconv_bwd(dxbc_c, u_pre, proj1, (DI,), p["od_conv_w"], C=XW, seq=seq, glu=False,
                                          silu_out=True, name="l1_conv_bwd")
    tw = 512 if DI % 512 == 0 else LANE
    terms = [(dz, j, tw, j * tw) for j in range(DI // tw)]
    terms += [(dxbc, j, tw, DI + j * tw) for j in range(XW // tw)]
    terms += [(draw, 0, LANE, dt_off)]
    dn1 = mm_nn_terms(terms, w1t, out_dtype=F32, name="l1_d_in_proj")
    g_od_w_in_t = jnp.concatenate([mm_tn(dz, n1, out_dtype=BF16, name="l1_dw_in_z"),
                                   mm_tn(dxbc, n1, out_dtype=BF16, name="l1_dw_in_xbc"),
                                   mm_tn(draw, n1, out_dtype=BF16, name="l1_dw_in_dt")], axis=0)[:in_odd]
    dh1, g_od_norm = rmsnorm_bwd(h1, p["od_norm_w"], dn1, dh2, name="l1_norm_bwd")

    dh1b = dh1.astype(BF16)
    g_ev_w_out = mm_tn(ycat, dh1b, out_dtype=BF16, name="l0_dw_out")
    dycat = mm_nt_terms([(dh1b, 0, D, 0)], p["ev_w_out"], out_dtype=BF16, name="l0_d_out_proj")
    du2, dgc, dga, do, g_ln_w, g_ln_b = mix0_post_bwd(dycat, u2, proj0, o, p["ev_ln_w"], p["ev_ln_b"], CW=CW,
                                                      gc_off=gc_off, ga_off=ga_off, name="l0_post_bwd")
    dq, dk, dv = sba_bwd(proj0, ctot, do, B=B, seq=seq, heads=heads, q_off=q_off, k_off=k_off, v_off=v_off,
                         name="l0_attn_bwd")
    dga_a, dga_b, g_dw_w, g_dw_b = dwconv_bwd(du2, None, proj0, (0, CW), p["ev_dw_w"], C=CW, seq=seq, glu=True,
                                              silu_out=False, name="l0_conv_bwd")
    pieces = [dga_a, dga_b, dgc, dq, dk, dv, dga]
    dn0 = mm_nt_terms([(pc, 0, CW, j * CW) for j, pc in enumerate(pieces)], p["ev_w_in"], out_dtype=F32,
                      name="l0_d_in_proj")
    g_ev_w_in = jnp.concatenate([mm_tn(n0, pc, out_dtype=BF16, name=f"l0_dw_in_{j}") for j, pc in enumerate(pieces)],
                                axis=1)
    dx, g_ev_norm = rmsnorm_bwd(x, p["ev_norm_w"], dn0, dh1, name="l0_norm_bwd")

    grads = dict(ev_norm_w=g_ev_norm, ev_w_in=g_ev_w_in, ev_dw_w=g_dw_w, ev_dw_b=g_dw_b, ev_ln_w=g_ln_w,
                 ev_ln_b=g_ln_b, ev_w_out=g_ev_w_out, od_norm_w=g_od_norm, od_w_in_t=g_od_w_in_t, od_conv_w=g_conv_w,
                 od_conv_b=g_conv_b, od_dt_bias=g_bias[:, :H], od_a_log=g_alog[:, :H], od_d=g_d, od_gnorm_w=g_gnorm,
                 od_w_out=g_od_w_out, final_norm_w=g_final)
    return loss, dx, grads


BIG = ("ev_w_in", "ev_w_out", "od_w_in", "od_w_out")
SMALL = ("ev_norm_w", "ev_dw_w", "ev_dw_b", "ev_ln_w", "ev_ln_b", "od_norm_w", "od_conv_w", "od_conv_b",
         "od_dt_bias", "od_a_log", "od_d", "od_gnorm_w", "final_norm_w")
SMALL_SHARDED = ("ev_dw_w", "od_norm_w", "od_conv_w", "od_conv_b", "od_gnorm_w")
ORDER = ("ev_norm_w", "ev_w_in", "ev_dw_w", "ev_dw_b", "ev_ln_w", "ev_ln_b", "ev_w_out", "od_norm_w", "od_w_in",
         "od_conv_w", "od_conv_b", "od_dt_bias", "od_a_log", "od_d", "od_gnorm_w", "od_w_out", "final_norm_w")


def _pack_rows(arrs, width, row_align):
    parts, spans, r0 = [], [], 0
    for a in arrs:
        flat = a.reshape(-1)
        rows = -(-flat.shape[0] // (width * row_align)) * row_align
        parts.append(jnp.pad(flat, (0, rows * width - flat.shape[0])).reshape(rows, width))
        spans.append((r0, a.size, a.shape))
        r0 += rows
    return jnp.concatenate(parts, axis=0), spans


def _unpack_rows(packed, spans):
    lead = packed.shape[:-2]
    width = packed.shape[-1]
    out = []
    for r0, size, shape in spans:
        rows = -(-size // width)
        blk = packed[..., r0:r0 + rows, :].reshape(lead + (rows * width,))[..., :size]
        out.append(blk.reshape(lead + tuple(shape)))
    return out


def _col_shards(a):
    R, C8 = a.shape
    return a.reshape(R, N_DEV, C8 // N_DEV).transpose(1, 0, 2)


def _col_unshards(a):
    n, R, C = a.shape
    return a.transpose(1, 0, 2).reshape(R, n * C)


def kernel(x, ev_norm_w, ev_w_in, ev_dw_w, ev_dw_b, ev_ln_w, ev_ln_b, ev_w_out, od_norm_w, od_w_in, od_conv_w, od_conv_b, od_dt_bias, od_a_log, od_d, od_gnorm_w, od_w_out, final_norm_w, loss_target, m_ev_norm_w, m_ev_w_in, m_ev_dw_w, m_ev_dw_b, m_ev_ln_w, m_ev_ln_b, m_ev_w_out, m_od_norm_w, m_od_w_in, m_od_conv_w, m_od_conv_b, m_od_dt_bias, m_od_a_log, m_od_d, m_od_gnorm_w, m_od_w_out, m_final_norm_w, v_ev_norm_w, v_ev_w_in, v_ev_dw_w, v_ev_dw_b, v_ev_ln_w, v_ev_ln_b, v_ev_w_out, v_od_norm_w, v_od_w_in, v_od_conv_w, v_od_conv_b, v_od_dt_bias, v_od_a_log, v_od_d, v_od_gnorm_w, v_od_w_out, v_final_norm_w):
    loc = dict(ev_norm_w=ev_norm_w, ev_w_in=ev_w_in, ev_dw_w=ev_dw_w, ev_dw_b=ev_dw_b, ev_ln_w=ev_ln_w,
               ev_ln_b=ev_ln_b, ev_w_out=ev_w_out, od_norm_w=od_norm_w, od_w_in=od_w_in, od_conv_w=od_conv_w,
               od_conv_b=od_conv_b, od_dt_bias=od_dt_bias, od_a_log=od_a_log, od_d=od_d, od_gnorm_w=od_gnorm_w,
               od_w_out=od_w_out, final_norm_w=final_norm_w)
    mom = dict(ev_norm_w=m_ev_norm_w, ev_w_in=m_ev_w_in, ev_dw_w=m_ev_dw_w, ev_dw_b=m_ev_dw_b, ev_ln_w=m_ev_ln_w,
               ev_ln_b=m_ev_ln_b, ev_w_out=m_ev_w_out, od_norm_w=m_od_norm_w, od_w_in=m_od_w_in,
               od_conv_w=m_od_conv_w, od_conv_b=m_od_conv_b, od_dt_bias=m_od_dt_bias, od_a_log=m_od_a_log,
               od_d=m_od_d, od_gnorm_w=m_od_gnorm_w, od_w_out=m_od_w_out, final_norm_w=m_final_norm_w)
    var = dict(ev_norm_w=v_ev_norm_w, ev_w_in=v_ev_w_in, ev_dw_w=v_ev_dw_w, ev_dw_b=v_ev_dw_b, ev_ln_w=v_ev_ln_w,
               ev_ln_b=v_ev_ln_b, ev_w_out=v_ev_w_out, od_norm_w=v_od_norm_w, od_w_in=v_od_w_in,
               od_conv_w=v_od_conv_w, od_conv_b=v_od_conv_b, od_dt_bias=v_od_dt_bias, od_a_log=v_od_a_log,
               od_d=v_od_d, od_gnorm_w=v_od_gnorm_w, od_w_out=v_od_w_out, final_norm_w=v_final_norm_w)
    shapes = {n: loc[n].shape for n in ORDER}
    loc = {n: (a.reshape(1, -1) if a.ndim == 1 else a.reshape(a.shape[-2:]) if a.ndim == 3 else a)
           for n, a in loc.items()}
    mom = {n: a.reshape(loc[n].shape) for n, a in mom.items()}
    var = {n: a.reshape(loc[n].shape) for n, a in var.items()}

    B, seq, D = x.shape
    me = 4 * lax.axis_index("x") + 2 * lax.axis_index("y") + lax.axis_index("c")

    w1_rows = loc["od_w_in"].shape[1]
    w1_pad = (-w1_rows) % 16

    def to_t(a):
        return jnp.pad(a.T, ((0, w1_pad), (0, 0)))

    small_packed, small_spans = _pack_rows([loc[n] for n in SMALL_SHARDED], LANE, 8)
    g_ev_in, g_ev_out, g_od_in_t, g_od_out, small_all = exchange(
        [loc["ev_w_in"].astype(BF16), loc["ev_w_out"].astype(BF16), to_t(loc["od_w_in"]).astype(BF16),
         loc["od_w_out"].astype(BF16), small_packed], ["gather"] * 4 + ["gather_direct"], name="gather_weights")
    full = dict(loc)
    del full["od_w_in"]
    full["ev_w_in"] = _col_unshards(g_ev_in)
    full["ev_w_out"] = g_ev_out.reshape(-1, D)
    full["od_w_out"] = g_od_out.reshape(-1, D)
    w1t = g_od_in_t[:, :w1_rows].reshape(N_DEV * w1_rows, D)
    full["od_w_in_t"] = jnp.pad(w1t, ((0, -(-(w1t.shape[0] + LANE) // 256) * 256 - w1t.shape[0]), (0, 0)))
    for n, a in zip(SMALL_SHARDED, _unpack_rows(small_all, small_spans)):
        full[n] = _col_unshards(a)

    loss, dx, grads = local_step(x.reshape(B * seq, D), loss_target.reshape(B * seq, D), full, B=B, seq=seq)

    g1t = jnp.pad(grads["od_w_in_t"].reshape(N_DEV, w1_rows, D), ((0, 0), (0, w1_pad), (0, 0)))
    gsmall_packed, gsmall_spans = _pack_rows([grads[n] for n in SMALL], LANE, 8)
    c_idx = lax.axis_index("c")
    keep, give = [], []
    for g in (_col_shards(grads["ev_w_in"]), grads["ev_w_out"].reshape(N_DEV, -1, D), g1t,
              grads["od_w_out"].reshape(N_DEV, -1, D)):
        by_chip = g.reshape((N_CHIP, 2) + g.shape[1:])
        keep.append(lax.dynamic_index_in_dim(by_chip, c_idx, axis=1, keepdims=False))
        give.append(lax.dynamic_index_in_dim(by_chip, 1 - c_idx, axis=1, keepdims=False))
    swapped = exchange(give, ["sibling"] * 4, name="swap_grads")
    chip_sums = [pair_sum(k, s, name=f"chip_sum_{i}") for i, (k, s) in enumerate(zip(keep, swapped))]
    r_ev_in, r_ev_out, r_od_in_t, r_od_out, gsmall_recv = exchange(
        chip_sums + [gsmall_packed], ["chips"] * 4 + ["gather_direct"], name="exchange_grads")

    big_out = [{} for _ in range(4)]
    for n, recv in (("ev_w_in", r_ev_in), ("ev_w_out", r_ev_out), ("od_w_out", r_od_out)):
        for kind, a in enumerate(adamw(recv, loc[n], mom[n], var[n], name="adamw_" + n)):
            big_out[kind][n] = a
    for kind, a in enumerate(adamw(r_od_in_t, to_t(loc["od_w_in"]), to_t(mom["od_w_in"]), to_t(var["od_w_in"]),
                                   name="adamw_od_w_in")):
        big_out[kind]["od_w_in"] = a[:w1_rows].T

    gsmall = dict(zip(SMALL, _unpack_rows(sum_slots(gsmall_recv, name="sum_small_grads"), gsmall_spans)))
    for n in SMALL_SHARDED:
        width = loc[n].shape[1]
        gsmall[n] = lax.dynamic_slice_in_dim(gsmall[n], me * width, width, axis=1)
    gs, sspans = _pack_rows([gsmall[n] for n in SMALL], LANE, 8)
    ws, _ = _pack_rows([loc[n] for n in SMALL], LANE, 8)
    ms, _ = _pack_rows([mom[n] for n in SMALL], LANE, 8)
    vs, _ = _pack_rows([var[n] for n in SMALL], LANE, 8)
    small_out = [dict(zip(SMALL, _unpack_rows(a, sspans))) for a in adamw(gs[None], ws, ms, vs, name="adamw_small")]

    outs = [lax.psum(loss[0, 0], ("x", "y", "c")), dx.reshape(B, seq, D)]
    for kind in range(4):
        for n in ORDER:
            src = big_out[kind] if n in BIG else small_out[kind]
            outs.append(src[n].reshape(shapes[n]))
    return tuple(outs)
```

```python
import functools

import jax
import jax.numpy as jnp
from jax import lax
from jax.experimental import pallas as pl
from jax.experimental.pallas import tpu as pltpu

F32 = jnp.float32
BF16 = jnp.bfloat16

EPS = 1e-6
N_DEV = 8
LANE = 128
VMEM_LIMIT_BYTES = 48 * 1024 * 1024

SB_HEAD_DIM = 128
CONF_KERNEL = 31
SSM_CONV = 4
SSM_HEAD_DIM = 64
SSM_GROUPS = 4
SSM_STATE = 128
SSM_CHUNK = 128
HALO = 32
HEAD_ROWS = 8
NEG_BIG = -1e30

ADAM_LR = 0.001
ADAM_B1 = 0.9
ADAM_B2 = 0.999
ADAM_EPS = 1e-08
ADAM_WD = 0.01
ADAM_STEP = 10

NT = (((1,), (1,)), ((), ()))
TN = (((0,), (0,)), ((), ()))


def _cp(*sem):
    return pltpu.CompilerParams(dimension_semantics=sem, vmem_limit_bytes=VMEM_LIMIT_BYTES)


def _pick(n, cap, align):
    if n <= cap:
        return n
    t = (cap // align) * align
    while t >= align:
        if n % t == 0:
            return t
        t -= align
    raise ValueError(f"no tile for {n} (cap {cap}, align {align})")


def _sigmoid(x):
    return 1.0 / (1.0 + jnp.exp(-x))


def _silu(x):
    return x * _sigmoid(x)


def _silu_grad(x):
    s = _sigmoid(x)
    return s * (1.0 + x * (1.0 - s))


def _dot(a, b, dims=None):
    if dims is None:
        return jnp.dot(a, b, preferred_element_type=F32)
    return lax.dot_general(a, b, dims, preferred_element_type=F32)


def _tri_dot3(tri, x):
    hi = x.astype(BF16)
    r1 = x - hi.astype(F32)
    mid = r1.astype(BF16)
    lo = (r1 - mid.astype(F32)).astype(BF16)
    return _dot(tri, hi) + _dot(tri, mid) + _dot(tri, lo)


def mm_nn(a, b, *, add=None, out_dtype, name):
    M, K = a.shape
    N = b.shape[1]
    tm = _pick(M, 1024, 16)
    tn = _pick(N, 1024, LANE)

    def body(*refs):
        if add is None:
            a_ref, b_ref, o_ref = refs
        else:
            a_ref, b_ref, add_ref, o_ref = refs
        acc = _dot(a_ref[...], b_ref[...])
        if add is not None:
            acc = acc + add_ref[...]
        o_ref[...] = acc.astype(out_dtype)

    in_specs = [pl.BlockSpec((tm, K), lambda i, j: (i, 0)), pl.BlockSpec((K, tn), lambda i, j: (0, j))]
    args = [a, b]
    if add is not None:
        in_specs.append(pl.BlockSpec((tm, tn), lambda i, j: (i, j)))
        args.append(add)
    return pl.pallas_call(
        body, name=name, grid=(M // tm, N // tn), in_specs=in_specs,
        out_specs=pl.BlockSpec((tm, tn), lambda i, j: (i, j)),
        out_shape=jax.ShapeDtypeStruct((M, N), out_dtype),
        compiler_params=_cp("parallel", "parallel"))(*args)


def mm_nt_terms(terms, b, *, out_dtype, name):
    M = terms[0][0].shape[0]
    N = b.shape[0]
    n_terms = len(terms)
    cap = 1024 if n_terms == 1 else 512
    tm = _pick(M, cap, 16)
    tn = _pick(N, cap, LANE)

    def body(*refs):
        o_ref = refs[-1]
        acc = None
        for t in range(n_terms):
            part = _dot(refs[2 * t][...], refs[2 * t + 1][...], NT)
            acc = part if acc is None else acc + part
        o_ref[...] = acc.astype(out_dtype)

    in_specs, args = [], []
    for arr, cb, w, off in terms:
        assert off % w == 0
        in_specs.append(pl.BlockSpec((tm, w), lambda i, j, cb=cb: (i, cb)))
        in_specs.append(pl.BlockSpec((tn, w), lambda i, j, ob=off // w: (j, ob)))
        args += [arr, b]
    return pl.pallas_call(
        body, name=name, grid=(M // tm, N // tn), in_specs=in_specs,
        out_specs=pl.BlockSpec((tm, tn), lambda i, j: (i, j)),
        out_shape=jax.ShapeDtypeStruct((M, N), out_dtype),
        compiler_params=_cp("parallel", "parallel"))(*args)


def mm_nn_terms(terms, b, *, out_dtype, name):
    M = terms[0][0].shape[0]
    N = b.shape[1]
    tm = _pick(M, 512, 16)
    tn = _pick(N, 512, LANE)
    n_terms = len(terms)

    def body(*refs):
        o_ref = refs[-1]
        acc = None
        for t in range(n_terms):
            part = _dot(refs[2 * t][...], refs[2 * t + 1][...])
            acc = part if acc is None else acc + part
        o_ref[...] = acc.astype(out_dtype)

    in_specs, args = [], []
    for arr, cb, w, off in terms:
        assert off % w == 0
        in_specs.append(pl.BlockSpec((tm, w), lambda i, j, cb=cb: (i, cb)))
        in_specs.append(pl.BlockSpec((w, tn), lambda i, j, ob=off // w: (ob, j)))
        args += [arr, b]
    return pl.pallas_call(
        body, name=name, grid=(M // tm, N // tn), in_specs=in_specs,
        out_specs=pl.BlockSpec((tm, tn), lambda i, j: (i, j)),
        out_shape=jax.ShapeDtypeStruct((M, N), out_dtype),
        compiler_params=_cp("parallel", "parallel"))(*args)


def mm_tn(a, b, *, out_dtype, name):
    T, M = a.shape
    N = b.shape[1]
    tm = _pick(M, 1024, LANE)
    tn = _pick(N, 1024, LANE)
    tk = _pick(T, 1024, 16)
    nk = T // tk

    def body(a_ref, b_ref, o_ref, acc_ref):
        k = pl.program_id(2)

        @pl.when(k == 0)
        def _():
            acc_ref[...] = jnp.zeros_like(acc_ref)

        acc_ref[...] += _dot(a_ref[...], b_ref[...], TN)

        @pl.when(k == nk - 1)
        def _():
            o_ref[...] = acc_ref[...].astype(out_dtype)

    return pl.pallas_call(
        body, name=name, grid=(M // tm, N // tn, nk),
        in_specs=[pl.BlockSpec((tk, tm), lambda i, j, k: (k, i)), pl.BlockSpec((tk, tn), lambda i, j, k: (k, j))],
        out_specs=pl.BlockSpec((tm, tn), lambda i, j, k: (i, j)),
        out_shape=jax.ShapeDtypeStruct((M, N), out_dtype),
        scratch_shapes=[pltpu.VMEM((tm, tn), F32)],
        compiler_params=_cp("parallel", "parallel", "arbitrary"))(a, b)


def rmsnorm_fwd(h, w, *, name):
    T, D = h.shape
    tt = _pick(T, 512, 16)

    def body(h_ref, w_ref, n_ref):
        x = h_ref[...]
        r = lax.rsqrt(jnp.mean(x * x, axis=-1, keepdims=True) + EPS)
        n_ref[...] = (x * r * w_ref[...]).astype(BF16)

    return pl.pallas_call(
        body, name=name, grid=(T // tt,),
        in_specs=[pl.BlockSpec((tt, D), lambda i: (i, 0)), pl.BlockSpec((1, D), lambda i: (0, 0))],
        out_specs=pl.BlockSpec((tt, D), lambda i: (i, 0)),
        out_shape=jax.ShapeDtypeStruct((T, D), BF16),
        compiler_params=_cp("parallel"))(h, w)


def rmsnorm_bwd(h, w, dn, dres, *, name):
    T, D = h.shape
    tt = _pick(T, 512, 16)

    def body(h_ref, w_ref, dn_ref, dres_ref, dh_ref, gw_ref):
        @pl.when(pl.program_id(0) == 0)
        def _():
            gw_ref[...] = jnp.zeros_like(gw_ref)

        x = h_ref[...]
        r = lax.rsqrt(jnp.mean(x * x, axis=-1, keepdims=True) + EPS)
        xhat = x * r
        g = dn_ref[...].astype(F32)
        gw_ref[...] += jnp.sum(g * xhat, axis=0, keepdims=True)
        dxh = g * w_ref[...]
        dx = r * (dxh - xhat * jnp.mean(dxh * xhat, axis=-1, keepdims=True))
        dh_ref[...] = dres_ref[...] + dx

    row = pl.BlockSpec((tt, D), lambda i: (i, 0))
    vec = pl.BlockSpec((1, D), lambda i: (0, 0))
    return pl.pallas_call(
        body, name=name, grid=(T // tt,), in_specs=[row, vec, row, row], out_specs=[row, vec],
        out_shape=[jax.ShapeDtypeStruct((T, D), F32), jax.ShapeDtypeStruct((1, D), F32)],
        compiler_params=_cp("arbitrary"))(h, w, dn, dres)


def final_loss(h, w, target, *, name):
    T, D = h.shape
    tt = _pick(T, 512, 16)

    def body(h_ref, w_ref, t_ref, loss_ref, dh_ref, gw_ref):
        @pl.when(pl.program_id(0) == 0)
        def _():
            gw_ref[...] = jnp.zeros_like(gw_ref)
            loss_ref[...] = jnp.zeros_like(loss_ref)

        x = h_ref[...]
        r = lax.rsqrt(jnp.mean(x * x, axis=-1, keepdims=True) + EPS)
        xhat = x * r
        e = xhat * w_ref[...] - t_ref[...]
        loss_ref[...] += jnp.sum(e * e) * (0.5 / D)
        g = e * (1.0 / D)
        gw_ref[...] += jnp.sum(g * xhat, axis=0, keepdims=True)
        dxh = g * w_ref[...]
        dh_ref[...] = r * (dxh - xhat * jnp.mean(dxh * xhat, axis=-1, keepdims=True))

    row = pl.BlockSpec((tt, D), lambda i: (i, 0))
    vec = pl.BlockSpec((1, D), lambda i: (0, 0))
    one = pl.BlockSpec((1, LANE), lambda i: (0, 0))
    return pl.pallas_call(
        body, name=name, grid=(T // tt,), in_specs=[row, vec, row], out_specs=[one, row, vec],
        out_shape=[jax.ShapeDtypeStruct((1, LANE), F32), jax.ShapeDtypeStruct((T, D), F32),
                   jax.ShapeDtypeStruct((1, D), F32)],
        compiler_params=_cp("arbitrary"))(h, w, target)


CONV_CHUNK = 32
SUBLANES = 8


def _conv_tiles(seq, C):
    return _pick(seq, 512, HALO), _pick(C, 512, LANE)


def _residues(offsets):
    return sorted({s % SUBLANES for s in offsets} - {0})


def _fill_shifted(buf, shifted, residues):
    n = buf.shape[0] - SUBLANES
    for i, r in enumerate(residues):
        shifted[i, 0:n, :] = buf[r:r + n, :]


def _tap(buf, shifted, residues, offset, start, rows):
    r = offset % SUBLANES
    base = offset - r
    ref = buf if r == 0 else shifted.at[residues.index(r)]
    return ref[pl.ds(start + base, rows), :]


def dwconv_fwd(src, offs, w, b, *, C, seq, glu, silu_out, name):
    T = src.shape[0]
    K = w.shape[0]
    tt, tc = _conv_tiles(seq, C)
    n_in = 2 if glu else 1
    per = tt // HALO
    offsets = [HALO - (K - 1) + k for k in range(K)]
    residues = _residues(offsets)

    def body(*refs):
        cur = refs[0:2 * n_in:2]
        halo = refs[1:2 * n_in:2]
        w_ref, b_ref = refs[2 * n_in], refs[2 * n_in + 1]
        outs = refs[2 * n_in + 2:-2]
        buf, shifted = refs[-2], refs[-1]
        i = pl.program_id(1)
        first = (i * tt) % seq == 0

        def pre(rs):
            v = rs[0][...].astype(F32)
            return v * _sigmoid(rs[1][...].astype(F32)) if glu else v

        buf[0:HALO, :] = jnp.where(first, 0.0, pre(halo))
        buf[HALO:HALO + tt, :] = pre(cur)
        _fill_shifted(buf, shifted, residues)

        def chunk(ci, carry):
            start = pl.multiple_of(ci * CONV_CHUNK, CONV_CHUNK)
            acc = jnp.broadcast_to(b_ref[...], (CONV_CHUNK, tc))
            for k in range(K):
                acc = acc + w_ref[k:k + 1, :] * _tap(buf, shifted, residues, offsets[k], start, CONV_CHUNK)
            outs[0][pl.ds(start, CONV_CHUNK), :] = acc.astype(BF16)
            if silu_out:
                outs[1][pl.ds(start, CONV_CHUNK), :] = _silu(acc).astype(BF16)
            return carry

        lax.fori_loop(0, tt // CONV_CHUNK, chunk, 0)

    in_specs, args = [], []
    for off in offs:
        assert off % tc == 0
        in_specs.append(pl.BlockSpec((tt, tc), lambda j, i, ob=off // tc: (i, ob + j)))
        in_specs.append(pl.BlockSpec((HALO, tc), lambda j, i, ob=off // tc: (jnp.maximum(i * per - 1, 0), ob + j)))
        args += [src, src]
    in_specs += [pl.BlockSpec((K, tc), lambda j, i: (0, j)), pl.BlockSpec((1, tc), lambda j, i: (0, j))]
    args += [w, b]
    n_out = 2 if silu_out else 1
    out = pl.pallas_call(
        body, name=name, grid=(C // tc, T // tt), in_specs=in_specs,
        out_specs=[pl.BlockSpec((tt, tc), lambda j, i: (i, j))] * n_out,
        out_shape=[jax.ShapeDtypeStruct((T, C), BF16)] * n_out,
        scratch_shapes=[pltpu.VMEM((HALO + tt, tc), F32), pltpu.VMEM((max(len(residues), 1), HALO + tt, tc), F32)],
        compiler_params=_cp("parallel", "arbitrary"))(*args)
    return out


def dwconv_bwd(du, u, src, offs, w, *, C, seq, glu, silu_out, name):
    T = src.shape[0]
    K = w.shape[0]
    tt, tc = _conv_tiles(seq, C)
    n_in = 2 if glu else 1
    per = tt // HALO
    last_blk = T // HALO - 1
    g_offsets = [K - 1 - k for k in range(K)]
    x_offsets = [HALO - (K - 1) + k for k in range(K)]
    g_res, x_res = _residues(g_offsets), _residues(x_offsets)

    def body(*refs):
        pos = 0
        du_cur, du_nxt = refs[0], refs[1]
        pos = 2
        if silu_out:
            u_cur, u_nxt = refs[2], refs[3]
            pos = 4
        cur = refs[pos:pos + 2 * n_in:2]
        halo = refs[pos + 1:pos + 2 * n_in:2]
        pos += 2 * n_in
        w_ref = refs[pos]
        outs = refs[pos + 1:pos + 1 + n_in]
        dw_ref, db_ref = refs[pos + 1 + n_in], refs[pos + 2 + n_in]
        gbuf, gshift, xbuf, xshift, dw_acc, db_acc = refs[-6:]
        i = pl.program_id(1)
        first = (i * tt) % seq == 0
        last = ((i + 1) * tt) % seq == 0

        @pl.when(i == 0)
        def _():
            dw_acc[...] = jnp.zeros_like(dw_acc)
            db_acc[...] = jnp.zeros_like(db_acc)

        g_cur = du_cur[...].astype(F32)
        g_nxt = du_nxt[...].astype(F32)
        if silu_out:
            g_cur = g_cur * _silu_grad(u_cur[...].astype(F32))
            g_nxt = g_nxt * _silu_grad(u_nxt[...].astype(F32))
        gbuf[0:tt, :] = g_cur
        gbuf[tt:tt + HALO, :] = jnp.where(last, 0.0, g_nxt)

        def pre(rs):
            v = rs[0][...].astype(F32)
            return v * _sigmoid(rs[1][...].astype(F32)) if glu else v

        xbuf[0:HALO, :] = jnp.where(first, 0.0, pre(halo))
        xbuf[HALO:HALO + tt, :] = pre(cur)
        _fill_shifted(gbuf, gshift, g_res)
        _fill_shifted(xbuf, xshift, x_res)

        def fold(v):
            out = v[0:SUBLANES]
            for s in range(SUBLANES, CONV_CHUNK, SUBLANES):
                out = out + v[s:s + SUBLANES]
            return out

        def chunk(ci, carry):
            start = pl.multiple_of(ci * CONV_CHUNK, CONV_CHUNK)
            g = gbuf[pl.ds(start, CONV_CHUNK), :]
            dx = jnp.zeros((CONV_CHUNK, tc), F32)
            for k in range(K):
                dx = dx + w_ref[k:k + 1, :] * _tap(gbuf, gshift, g_res, g_offsets[k], start, CONV_CHUNK)
                dw_acc[k * SUBLANES:(k + 1) * SUBLANES, :] += fold(
                    g * _tap(xbuf, xshift, x_res, x_offsets[k], start, CONV_CHUNK))
            db_acc[...] += fold(g)
            rows = pl.ds(start, CONV_CHUNK)
            if glu:
                a = cur[0][rows, :].astype(F32)
                s = _sigmoid(cur[1][rows, :].astype(F32))
                outs[0][rows, :] = (dx * s).astype(BF16)
                outs[1][rows, :] = (dx * a * s * (1.0 - s)).astype(BF16)
            else:
                outs[0][rows, :] = dx.astype(BF16)
            return carry

        lax.fori_loop(0, tt // CONV_CHUNK, chunk, 0)

        @pl.when(i == T // tt - 1)
        def _():
            for k in range(K):
                dw_ref[k:k + 1, :] = jnp.sum(dw_acc[k * SUBLANES:(k + 1) * SUBLANES, :], axis=0, keepdims=True)
            db_ref[...] = jnp.sum(db_acc[...], axis=0, keepdims=True)

    def cur_spec(ob):
        return pl.BlockSpec((tt, tc), lambda j, i: (i, ob + j))

    def nxt_spec(ob):
        return pl.BlockSpec((HALO, tc), lambda j, i: (jnp.minimum((i + 1) * per, last_blk), ob + j))

    def prv_spec(ob):
        return pl.BlockSpec((HALO, tc), lambda j, i: (jnp.maximum(i * per - 1, 0), ob + j))

    in_specs = [cur_spec(0), nxt_spec(0)]
    args = [du, du]
    if silu_out:
        in_specs += [cur_spec(0), nxt_spec(0)]
        args += [u, u]
    for off in offs:
        assert off % tc == 0
        in_specs += [cur_spec(off // tc), prv_spec(off // tc)]
        args += [src, src]
    in_specs.append(pl.BlockSpec((K, tc), lambda j, i: (0, j)))
    args.append(w)
    out_specs = [pl.BlockSpec((tt, tc), lambda j, i: (i, j))] * n_in
    out_specs += [pl.BlockSpec((K, tc), lambda j, i: (0, j)), pl.BlockSpec((1, tc), lambda j, i: (0, j))]
    out_shape = [jax.ShapeDtypeStruct((T, C), BF16)] * n_in
    out_shape += [jax.ShapeDtypeStruct((K, C), F32), jax.ShapeDtypeStruct((1, C), F32)]
    return pl.pallas_call(
        body, name=name, grid=(C // tc, T // tt), in_specs=in_specs, out_specs=out_specs, out_shape=out_shape,
        scratch_shapes=[pltpu.VMEM((tt + HALO, tc), F32), pltpu.VMEM((max(len(g_res), 1), tt + HALO, tc), F32),
                        pltpu.VMEM((HALO + tt, tc), F32), pltpu.VMEM((max(len(x_res), 1), HALO + tt, tc), F32),
                        pltpu.VMEM((K * SUBLANES, tc), F32), pltpu.VMEM((SUBLANES, tc), F32)],
        compiler_params=_cp("parallel", "arbitrary"))(*args)


def mix0_post_fwd(u2, proj, o, ln_w, ln_b, *, CW, gc_off, ga_off, name):
    T = u2.shape[0]
    tt = _pick(T, 256, 16)

    def body(u_ref, gc_ref, ga_ref, o_ref, lw_ref, lb_ref, y_ref):
        u = u_ref[...].astype(F32)
        mu = jnp.mean(u, axis=-1, keepdims=True)
        xc = u - mu
        r = lax.rsqrt(jnp.mean(xc * xc, axis=-1, keepdims=True) + EPS)
        u3 = xc * r * lw_ref[...] + lb_ref[...]
        y_ref[:, 0:CW] = (_silu(u3) * _silu(gc_ref[...].astype(F32))).astype(BF16)
        y_ref[:, CW:2 * CW] = (o_ref[...].astype(F32) * _silu(ga_ref[...].astype(F32))).astype(BF16)

    row = pl.BlockSpec((tt, CW), lambda i: (i, 0))
    vec = pl.BlockSpec((1, CW), lambda i: (0, 0))
    return pl.pallas_call(
        body, name=name, grid=(T // tt,),
        in_specs=[row, pl.BlockSpec((tt, CW), lambda i: (i, gc_off // CW)),
                  pl.BlockSpec((tt, CW), lambda i: (i, ga_off // CW)), row, vec, vec],
        out_specs=pl.BlockSpec((tt, 2 * CW), lambda i: (i, 0)),
        out_shape=jax.ShapeDtypeStruct((T, 2 * CW), BF16),
        compiler_params=_cp("parallel"))(u2, proj, proj, o, ln_w, ln_b)


def mix0_post_bwd(dy, u2, proj, o, ln_w, ln_b, *, CW, gc_off, ga_off, name):
    T = u2.shape[0]
    tt = _pick(T, 256, 16)

    def body(dy_ref, u_ref, gc_ref, ga_ref, o_ref, lw_ref, lb_ref, du_ref, dgc_ref, dga_ref, do_ref, dlw_ref, dlb_ref):
        @pl.when(pl.program_id(0) == 0)
        def _():
            dlw_ref[...] = jnp.zeros_like(dlw_ref)
            dlb_ref[...] = jnp.zeros_like(dlb_ref)

        dyc = dy_ref[:, 0:CW].astype(F32)
        dya = dy_ref[:, CW:2 * CW].astype(F32)
        u = u_ref[...].astype(F32)
        mu = jnp.mean(u, axis=-1, keepdims=True)
        xc = u - mu
        r = lax.rsqrt(jnp.mean(xc * xc, axis=-1, keepdims=True) + EPS)
        xhat = xc * r
        u3 = xhat * lw_ref[...] + lb_ref[...]
        gc = gc_ref[...].astype(F32)
        dgc_ref[...] = (dyc * _silu(u3) * _silu_grad(gc)).astype(BF16)
        du3 = dyc * _silu(gc) * _silu_grad(u3)
        dlw_ref[...] += jnp.sum(du3 * xhat, axis=0, keepdims=True)
        dlb_ref[...] += jnp.sum(du3, axis=0, keepdims=True)
        dxh = du3 * lw_ref[...]
        du = r * (dxh - jnp.mean(dxh, axis=-1, keepdims=True) - xhat * jnp.mean(dxh * xhat, axis=-1, keepdims=True))
        du_ref[...] = du.astype(BF16)
        ga = ga_ref[...].astype(F32)
        ov = o_ref[...].astype(F32)
        do_ref[...] = (dya * _silu(ga)).astype(BF16)
        dga_ref[...] = (dya * ov * _silu_grad(ga)).astype(BF16)

    row = pl.BlockSpec((tt, CW), lambda i: (i, 0))
    vec = pl.BlockSpec((1, CW), lambda i: (0, 0))
    big = jax.ShapeDtypeStruct((T, CW), BF16)
    small = jax.ShapeDtypeStruct((1, CW), F32)
    return pl.pallas_call(
        body, name=name, grid=(T // tt,),
        in_specs=[pl.BlockSpec((tt, 2 * CW), lambda i: (i, 0)), row,
                  pl.BlockSpec((tt, CW), lambda i: (i, gc_off // CW)),
                  pl.BlockSpec((tt, CW), lambda i: (i, ga_off // CW)), row, vec, vec],
        out_specs=[row, row, row, row, vec, vec],
        out_shape=[big, big, big, big, small, small],
        compiler_params=_cp("arbitrary"))(dy, u2, proj, proj, o, ln_w, ln_b)


SB_UNDERFLOW = 110.0
SB_BOUND_MARGIN = 1.02


def _sb_tile(seq):
    return _pick(seq, 256, LANE)


def _softplus(z):
    return jnp.maximum(z, 0.0) + jnp.log(1.0 + jnp.exp(-jnp.abs(z)))


def _tri01(n, lower):
    i = lax.broadcasted_iota(jnp.int32, (n, n), 0)
    j = lax.broadcasted_iota(jnp.int32, (n, n), 1)
    return ((i >= j) if lower else (i <= j)).astype(BF16)


def _sb_heads_per_step(heads):
    return 2 if heads % 2 == 0 else 1


def sba_fwd(proj, *, B, seq, heads, q_off, k_off, v_off, name):
    dh = SB_HEAD_DIM
    tq = _sb_tile(seq)
    nq = seq // tq
    hps = _sb_heads_per_step(heads)
    hw = hps * dh
    scale = dh ** -0.5

    def body(q_ref, k_ref, v_ref, o_ref, ct_ref, acc_ref, kmax_ref):
        qi = pl.program_id(1)
        tri = _tri01(tq, True)
        below = lax.broadcasted_iota(jnp.int32, (tq, tq), 1) < lax.broadcasted_iota(jnp.int32, (tq, tq), 0)
        qs = [(q_ref[:, h * dh:(h + 1) * dh].astype(F32) * scale).astype(BF16) for h in range(hps)]

        @pl.when(qi == 0)
        def _():
            def chunk(i, best):
                rows = k_ref[pl.ds(pl.multiple_of(i * tq, tq), tq), :].astype(F32)
                sq = rows * rows
                return tuple(jnp.maximum(best[h], jnp.max(jnp.sum(sq[:, h * dh:(h + 1) * dh], axis=1, keepdims=True),
                                                          axis=0, keepdims=True)) for h in range(hps))

            best = lax.fori_loop(0, nq, chunk, (jnp.zeros((1, 1), F32),) * hps)
            for h in range(hps):
                kmax_ref[h] = jnp.broadcast_to(jnp.sqrt(best[h]), (8, LANE))

        z_bound = [jnp.sqrt(jnp.sum(qs[h].astype(F32) ** 2, axis=1, keepdims=True))
                   * (SB_BOUND_MARGIN * jnp.max(kmax_ref[h], keepdims=True)) for h in range(hps)]

        def block(start, rs, diag):
            pvs, out = [], []
            for h in range(hps):
                k_blk = k_ref[pl.ds(start, tq), h * dh:(h + 1) * dh]
                v_blk = v_ref[pl.ds(start, tq), h * dh:(h + 1) * dh]
                z = _dot(qs[h], k_blk, NT)
                sp = _softplus(z)
                if diag:
                    sp = jnp.where(below, sp, 0.0)
                wts = jnp.exp(z - (_dot(sp.astype(BF16), tri) + rs[h]))
                if diag:
                    wts = jnp.where(below, wts, 0.0)
                pvs.append(_dot(wts.astype(BF16), v_blk))
                out.append(rs[h] + jnp.sum(sp, axis=-1, keepdims=True))
            return pvs, tuple(out)

        zero = jnp.zeros((tq, 1), F32)
        pvs, rs = block(pl.multiple_of(qi * tq, tq), (zero,) * hps, True)
        for h in range(hps):
            acc_ref[:, h * dh:(h + 1) * dh] = pvs[h]

        def more(c):
            j, rs = c
            slack = rs[0] - z_bound[0]
            for h in range(1, hps):
                slack = jnp.minimum(slack, rs[h] - z_bound[h])
            return jnp.logical_and(j < qi, jnp.min(slack) <= SB_UNDERFLOW)

        def step(c):
            j, rs = c
            pvs, rs = block(pl.multiple_of((qi - 1 - j) * tq, tq), rs, False)
            for h in range(hps):
                acc_ref[:, h * dh:(h + 1) * dh] += pvs[h]
            return j + 1, rs

        n_left, totals = lax.while_loop(more, step, (jnp.int32(0), rs))
        o_ref[...] = acc_ref[...].astype(BF16)
        for h in range(hps):
            ct_ref[0, 8 * h:8 * h + 8, :] = jnp.broadcast_to(totals[h], (tq, LANE)).T[0:8, :]
        ct_ref[0, 8 * hps:8 * hps + 8, :] = jnp.full((8, tq), n_left, F32)

    qb, kb, vb = q_off // hw, k_off // hw, v_off // hw
    G = heads // hps
    return pl.pallas_call(
        body, name=name, grid=(B * G, nq),
        in_specs=[pl.BlockSpec((tq, hw), lambda g, i: ((g // G) * nq + i, qb + g % G)),
                  pl.BlockSpec((seq, hw), lambda g, i: (g // G, kb + g % G)),
                  pl.BlockSpec((seq, hw), lambda g, i: (g // G, vb + g % G))],
        out_specs=[pl.BlockSpec((tq, hw), lambda g, i: ((g // G) * nq + i, g % G)),
                   pl.BlockSpec((1, 8 * hps + 8, tq), lambda g, i: (g * nq + i, 0, 0))],
        out_shape=[jax.ShapeDtypeStruct((B * seq, heads * dh), BF16),
                   jax.ShapeDtypeStruct((B * G * nq, 8 * hps + 8, tq), F32)],
        scratch_shapes=[pltpu.VMEM((tq, hw), F32), pltpu.VMEM((hps, 8, LANE), F32)],
        compiler_params=_cp("parallel", "arbitrary"))(proj, proj, proj)


def sba_bwd(proj, ctot, do, *, B, seq, heads, q_off, k_off, v_off, name):
    dh = SB_HEAD_DIM
    tq = _sb_tile(seq)
    nq = seq // tq
    hps = _sb_heads_per_step(heads)
    hw = hps * dh
    scale = dh ** -0.5

    def body(q_ref, k_ref, v_ref, ct_ref, do_ref, dq_ref, dk_ref, dv_ref, dq_acc, dk_acc, dv_acc):
        qi = pl.program_id(1)

        @pl.when(qi == 0)
        def _():
            dk_acc[...] = jnp.zeros_like(dk_acc)
            dv_acc[...] = jnp.zeros_like(dv_acc)

        tri_sfx = _tri01(tq, True)
        tri_pre = _tri01(tq, False)
        below = lax.broadcasted_iota(jnp.int32, (tq, tq), 1) < lax.broadcasted_iota(jnp.int32, (tq, tq), 0)
        qs = [(q_ref[:, h * dh:(h + 1) * dh].astype(F32) * scale).astype(BF16) for h in range(hps)]
        dos = [do_ref[:, h * dh:(h + 1) * dh] for h in range(hps)]
        totals = [jnp.max(jnp.broadcast_to(ct_ref[0, 8 * h:8 * h + 1, :], (LANE, tq)).T, axis=1, keepdims=True)
                  for h in range(hps)]
        dq_acc[...] = jnp.zeros_like(dq_acc)

        def block(start, carry, diag):
            out = []
            for h in range(hps):
                pc, pg = carry[h]
                cols = slice(h * dh, (h + 1) * dh)
                k_blk = k_ref[pl.ds(start, tq), cols]
                v_blk = v_ref[pl.ds(start, tq), cols]
                z = _dot(qs[h], k_blk, NT)
                sp = _softplus(z)
                sig = jnp.exp(z - sp)
                if diag:
                    sp = jnp.where(below, sp, 0.0)
                pc_next = pc + jnp.sum(sp, axis=-1, keepdims=True)
                wts = jnp.exp(z - (_dot(sp.astype(BF16), tri_sfx) + (totals[h] - pc_next)))
                if diag:
                    wts = jnp.where(below, wts, 0.0)
                g = _dot(dos[h], v_blk, NT) * wts
                dz = g - sig * (_dot(g.astype(BF16), tri_pre) + pg)
                if diag:
                    dz = jnp.where(below, dz, 0.0)
                dz = dz.astype(BF16)
                dq_acc[:, cols] += _dot(dz, k_blk)
                dk_acc[pl.ds(start, tq), cols] += _dot(dz, qs[h], TN)
                dv_acc[pl.ds(start, tq), cols] += _dot(wts.astype(BF16), dos[h], TN)
                out.append((pc_next, pg + jnp.sum(g, axis=-1, keepdims=True)))
            return tuple(out)

        zero = jnp.zeros((tq, 1), F32)
        n_left = jnp.max(ct_ref[0, 8 * hps:8 * hps + 8, :]).astype(jnp.int32)
        carry = lax.fori_loop(qi - n_left, qi, lambda j, c: block(pl.multiple_of(j * tq, tq), c, False),
                              ((zero, zero),) * hps)
        block(pl.multiple_of(qi * tq, tq), carry, True)
        dq_ref[...] = (dq_acc[...] * scale).astype(BF16)

        @pl.when(qi == nq - 1)
        def _():
            dk_ref[...] = dk_acc[...].astype(BF16)
            dv_ref[...] = dv_acc[...].astype(BF16)

    qb, kb, vb = q_off // hw, k_off // hw, v_off // hw
    G = heads // hps
    q_spec = pl.BlockSpec((tq, hw), lambda g, i: ((g // G) * nq + i, qb + g % G))
    o_spec = pl.BlockSpec((tq, hw), lambda g, i: ((g // G) * nq + i, g % G))
    kv_out = pl.BlockSpec((seq, hw), lambda g, i: (g // G, g % G))
    shp = jax.ShapeDtypeStruct((B * seq, heads * dh), BF16)
    return pl.pallas_call(
        body, name=name, grid=(B * G, nq),
        in_specs=[q_spec,
                  pl.BlockSpec((seq, hw), lambda g, i: (g // G, kb + g % G)),
                  pl.BlockSpec((seq, hw), lambda g, i: (g // G, vb + g % G)),
                  pl.BlockSpec((1, 8 * hps + 8, tq), lambda g, i: (g * nq + i, 0, 0)), o_spec],
        out_specs=[o_spec, kv_out, kv_out], out_shape=[shp, shp, shp],
        scratch_shapes=[pltpu.VMEM((tq, hw), F32), pltpu.VMEM((seq, hw), F32), pltpu.VMEM((seq, hw), F32)],
        compiler_params=_cp("parallel", "arbitrary"))(proj, proj, proj, ctot, do)


def _chunk_tri(lower):
    i = lax.broadcasted_iota(jnp.int32, (SSM_CHUNK, SSM_CHUNK), 0)
    j = lax.broadcasted_iota(jnp.int32, (SSM_CHUNK, SSM_CHUNK), 1)
    return ((i >= j) if lower else (i <= j)).astype(BF16)


def ssm_dt_fwd(proj, bias, a_log, *, dt_off, name):
    T = proj.shape[0]
    L = SSM_CHUNK
    tt = _pick(T, 512, L)

    def body(raw_ref, bias_ref, al_ref, dt_ref, cs_ref):
        x = raw_ref[...].astype(F32) + bias_ref[...]
        dt = jnp.maximum(x, 0.0) + jnp.log(1.0 + jnp.exp(-jnp.abs(x)))
        dt_ref[...] = dt
        la = dt * (-jnp.exp(al_ref[...]))
        tri = _chunk_tri(True)
        for c in range(tt // L):
            cs_ref[c * L:(c + 1) * L, :] = _tri_dot3(tri, la[c * L:(c + 1) * L, :])

    row = pl.BlockSpec((tt, LANE), lambda i: (i, 0))
    vec = pl.BlockSpec((1, LANE), lambda i: (0, 0))
    shp = jax.ShapeDtypeStruct((T, LANE), F32)
    return pl.pallas_call(
        body, name=name, grid=(T // tt,),
        in_specs=[pl.BlockSpec((tt, LANE), lambda i: (i, dt_off // LANE)), vec, vec],
        out_specs=[row, row], out_shape=[shp, shp], compiler_params=_cp("parallel"))(proj, bias, a_log)


def ssm_dt_bwd(ddt, dcs, proj, dt, bias, a_log, *, dt_off, n_heads, name):
    T = proj.shape[0]
    L = SSM_CHUNK
    tt = _pick(T, 512, L)

    def body(ddt_ref, dcs_ref, raw_ref, dt_ref, bias_ref, al_ref, draw_ref, dbias_ref, dal_ref, dla_buf):
        @pl.when(pl.program_id(0) == 0)
        def _():
            dbias_ref[...] = jnp.zeros_like(dbias_ref)
            dal_ref[...] = jnp.zeros_like(dal_ref)

        triu = _chunk_tri(False)
        dcs = dcs_ref[...]
        for c in range(tt // L):
            dla_buf[c * L:(c + 1) * L, :] = _tri_dot3(triu, dcs[c * L:(c + 1) * L, :])
        dla = dla_buf[...]
        a = -jnp.exp(al_ref[...])
        dtv = dt_ref[...]
        valid = lax.broadcasted_iota(jnp.int32, (tt, LANE), 1) < n_heads
        dal_ref[...] += jnp.sum(jnp.where(valid, dla * dtv, 0.0), axis=0, keepdims=True) * a
        x = raw_ref[...].astype(F32) + bias_ref[...]
        draw = jnp.where(valid, (ddt_ref[...] + dla * a) * _sigmoid(x), 0.0)
        dbias_ref[...] += jnp.sum(draw, axis=0, keepdims=True)
        draw_ref[...] = draw.astype(BF16)

    row = pl.BlockSpec((tt, LANE), lambda i: (i, 0))
    vec = pl.BlockSpec((1, LANE), lambda i: (0, 0))
    return pl.pallas_call(
        body, name=name, grid=(T // tt,),
        in_specs=[row, row, pl.BlockSpec((tt, LANE), lambda i: (i, dt_off // LANE)), row, vec, vec],
        out_specs=[row, vec, vec],
        out_shape=[jax.ShapeDtypeStruct((T, LANE), BF16), jax.ShapeDtypeStruct((1, LANE), F32),
                   jax.ShapeDtypeStruct((1, LANE), F32)],
        scratch_shapes=[pltpu.VMEM((tt, LANE), F32)],
        compiler_params=_cp("arbitrary"))(ddt, dcs, proj, dt, bias, a_log)


def _colb(row):
    return jnp.broadcast_to(row, (LANE, SSM_CHUNK)).T


def _ssd_pair_common(x_ref, dt_ref, cs_ref, pair, ppg, lo_half, causal, lane_row):
    L = SSM_CHUNK
    g, pp = divmod(pair, ppg)
    ra = g * HEAD_ROWS + 2 * pp
    X = x_ref[:, pair * LANE:(pair + 1) * LANE].astype(F32)
    dta, dtb = dt_ref[0, ra:ra + 1, :], dt_ref[0, ra + 1:ra + 2, :]
    csa, csb = cs_ref[0, ra:ra + 1, :], cs_ref[0, ra + 1:ra + 2, :]
    csa_c, csb_c = _colb(csa), _colb(csb)
    dt_p = jnp.where(lo_half, _colb(dta), _colb(dtb))
    La = jnp.exp(jnp.where(causal, csa_c - csa, NEG_BIG))
    Lb = jnp.exp(jnp.where(causal, csb_c - csb, NEG_BIG))
    last_a = jnp.sum(jnp.where(lane_row == L - 1, csa, 0.0), axis=1, keepdims=True)
    last_b = jnp.sum(jnp.where(lane_row == L - 1, csb, 0.0), axis=1, keepdims=True)
    ecs = jnp.exp(jnp.where(lo_half, csa_c, csb_c))
    tail = jnp.exp(jnp.where(lo_half, last_a - csa_c, last_b - csb_c))
    return g, ra, X, dt_p, La, Lb, last_a, last_b, ecs, tail


def ssd_fwd(xbc, dt_row, cs_row, d_full, *, B, seq, DI, name):
    L, N, G = SSM_CHUNK, SSM_STATE, SSM_GROUPS
    nc = seq // L
    XW = xbc.shape[1]
    n_pairs = DI // LANE
    ppg = n_pairs // G

    def body(x_ref, dt_ref, cs_ref, d_ref, y_ref, st_ref, state):
        c = pl.program_id(1)

        @pl.when(c == 0)
        def _():
            state[...] = jnp.zeros_like(state)

        causal = lax.broadcasted_iota(jnp.int32, (L, L), 0) >= lax.broadcasted_iota(jnp.int32, (L, L), 1)
        lo_half = lax.broadcasted_iota(jnp.int32, (L, LANE), 1) < SSM_HEAD_DIM
        rows_lo = lax.broadcasted_iota(jnp.int32, (LANE, N), 0) < SSM_HEAD_DIM
        lane_row = lax.broadcasted_iota(jnp.int32, (1, L), 1)
        cbs = []
        for g in range(G):
            Bc = x_ref[:, DI + g * N:DI + (g + 1) * N]
            Cc = x_ref[:, DI + G * N + g * N:DI + G * N + (g + 1) * N]
            cbs.append((Bc, Cc, _dot(Cc, Bc, NT)))
        for pair in range(n_pairs):
            g, ra, X, dt_p, La, Lb, last_a, last_b, ecs, tail = _ssd_pair_common(
                x_ref, dt_ref, cs_ref, pair, ppg, lo_half, causal, lane_row)
            Bc, Cc, CB = cbs[g]
            xs = X * dt_p
            xsb = xs.astype(BF16)
            y = jnp.where(lo_half, _dot((CB * La).astype(BF16), xsb), _dot((CB * Lb).astype(BF16), xsb))
            S = state[pair]
            st_ref[0, 0, pair] = S
            y = y + ecs * _dot(Cc, S.astype(BF16), NT)
            y = y + d_ref[:, pair * LANE:(pair + 1) * LANE] * X
            y_ref[:, pair * LANE:(pair + 1) * LANE] = y.astype(BF16)
            e_rows = jnp.where(rows_lo, jnp.exp(last_a), jnp.exp(last_b))
            state[pair] = e_rows * S + _dot((xs * tail).astype(BF16), Bc, TN)

    return pl.pallas_call(
        body, name=name, grid=(B, nc),
        in_specs=[pl.BlockSpec((L, XW), lambda b, c: (b * nc + c, 0)),
                  pl.BlockSpec((1, G * HEAD_ROWS, L), lambda b, c: (b, 0, c)),
                  pl.BlockSpec((1, G * HEAD_ROWS, L), lambda b, c: (b, 0, c)),
                  pl.BlockSpec((1, DI), lambda b, c: (0, 0))],
        out_specs=[pl.BlockSpec((L, DI), lambda b, c: (b * nc + c, 0)),
                   pl.BlockSpec((1, 1, n_pairs, LANE, N), lambda b, c: (b, c, 0, 0, 0))],
        out_shape=[jax.ShapeDtypeStruct((B * seq, DI), BF16),
                   jax.ShapeDtypeStruct((B, nc, n_pairs, LANE, N), F32)],
        scratch_shapes=[pltpu.VMEM((n_pairs, LANE, N), F32)],
        compiler_params=_cp("parallel", "arbitrary"))(xbc, dt_row, cs_row, d_full)


def ssd_bwd(xbc, dt_row, cs_row, d_full, states, dy, *, B, seq, DI, name):
    L, N, G = SSM_CHUNK, SSM_STATE, SSM_GROUPS
    nc = seq // L
    XW = xbc.shape[1]
    n_pairs = DI // LANE
    ppg = n_pairs // G
    HR = G * HEAD_ROWS

    def body(x_ref, dt_ref, cs_ref, d_ref, st_ref, dy_ref, dx_ref, ddt_ref, dcs_ref, dd_ref, dH):
        c = pl.program_id(1)

        @pl.when(c == 0)
        def _():
            dH[...] = jnp.zeros_like(dH)
            dd_ref[...] = jnp.zeros_like(dd_ref)

        causal = lax.broadcasted_iota(jnp.int32, (L, L), 0) >= lax.broadcasted_iota(jnp.int32, (L, L), 1)
        lo_half = lax.broadcasted_iota(jnp.int32, (L, LANE), 1) < SSM_HEAD_DIM
        rows_lo = lax.broadcasted_iota(jnp.int32, (LANE, N), 0) < SSM_HEAD_DIM
        lane_row = lax.broadcasted_iota(jnp.int32, (1, L), 1)
        head_row = lax.broadcasted_iota(jnp.int32, (HR, 1), 0)
        ddt_all = jnp.zeros((HR, L), F32)
        dcs_all = jnp.zeros((HR, L), F32)

        def place(row, r):
            return jnp.where(head_row == r, row, 0.0)

        def as_rows(col):
            return jnp.broadcast_to(col, (L, LANE)).T[0:HR, :]

        def head_sums(t, ra):
            sa = as_rows(jnp.sum(jnp.where(lo_half, t, 0.0), axis=1, keepdims=True))
            sb = as_rows(jnp.sum(jnp.where(lo_half, 0.0, t), axis=1, keepdims=True))
            return place(sa, ra) + place(sb, ra + 1)

        for g in range(G):
            Bc = x_ref[:, DI + g * N:DI + (g + 1) * N]
            Cc = x_ref[:, DI + G * N + g * N:DI + G * N + (g + 1) * N]
            CB = _dot(Cc, Bc, NT)
            dCB = jnp.zeros((L, L), F32)
            dC = jnp.zeros((L, N), F32)
            dB = jnp.zeros((L, N), F32)
            for pp in range(ppg):
                pair = g * ppg + pp
                _, ra, X, dt_p, La, Lb, last_a, last_b, ecs, tail = _ssd_pair_common(
                    x_ref, dt_ref, cs_ref, pair, ppg, lo_half, causal, lane_row)
                rb = ra + 1
                xs = X * dt_p
                xsb = xs.astype(BF16)
                Ma, Mb = CB * La, CB * Lb
                dY = dy_ref[:, pair * LANE:(pair + 1) * LANE].astype(F32)
                dYb = dY.astype(BF16)
                dMa = _dot(jnp.where(lo_half, dY, 0.0).astype(BF16), xsb, NT)
                dMb = _dot(jnp.where(lo_half, 0.0, dY).astype(BF16), xsb, NT)
                dSa, dSb = dMa * Ma, dMb * Mb
                dCB = dCB + dMa * La + dMb * Lb
                dcs_all = dcs_all + place(as_rows(jnp.sum(dSa, axis=1, keepdims=True)) - jnp.sum(dSa, axis=0, keepdims=True), ra)
                dcs_all = dcs_all + place(as_rows(jnp.sum(dSb, axis=1, keepdims=True)) - jnp.sum(dSb, axis=0, keepdims=True), rb)
                dxs = jnp.where(lo_half, _dot(Ma.astype(BF16), dYb, TN), _dot(Mb.astype(BF16), dYb, TN))
                S = st_ref[0, 0, pair]
                Sb = S.astype(BF16)
                y_inter = ecs * _dot(Cc, Sb, NT)
                dYe = (dY * ecs).astype(BF16)
                dC = dC + _dot(dYe, Sb)
                dHp = _dot(dYe, Cc, TN)
                dcs_all = dcs_all + head_sums(dY * y_inter, ra)
                dHn = dH[pair]
                dHnb = dHn.astype(BF16)
                ea, eb = jnp.exp(last_a), jnp.exp(last_b)
                dHp = dHp + jnp.where(rows_lo, ea, eb) * dHn
                prod = dHn * S
                dlast_a = ea * jnp.sum(jnp.where(rows_lo, prod, 0.0), keepdims=True)
                dlast_b = eb * jnp.sum(jnp.where(rows_lo, 0.0, prod), keepdims=True)
                XBt = _dot(Bc, dHnb, NT)
                dxs = dxs + tail * XBt
                t2 = xs * XBt * tail
                dlast_a = dlast_a + jnp.sum(jnp.where(lo_half, t2, 0.0), keepdims=True)
                dlast_b = dlast_b + jnp.sum(jnp.where(lo_half, 0.0, t2), keepdims=True)
                dcs_all = dcs_all - head_sums(t2, ra)
                dcs_all = dcs_all + place(jnp.where(lane_row == L - 1, dlast_a, 0.0), ra)
                dcs_all = dcs_all + place(jnp.where(lane_row == L - 1, dlast_b, 0.0), rb)
                dB = dB + _dot((xs * tail).astype(BF16), dHnb)
                dfull = d_ref[:, pair * LANE:(pair + 1) * LANE]
                dx_ref[:, pair * LANE:(pair + 1) * LANE] = (dxs * dt_p + dfull * dY).astype(BF16)
                ddt_all = ddt_all + head_sums(dxs * X, ra)
                dd_ref[0, :, pair * LANE:(pair + 1) * LANE] += jnp.sum(dY * X, axis=0, keepdims=True)
                dH[pair] = dHp
            dCBb = dCB.astype(BF16)
            dx_ref[:, DI + g * N:DI + (g + 1) * N] = (dB + _dot(dCBb, Cc, TN)).astype(BF16)
            dx_ref[:, DI + G * N + g * N:DI + G * N + (g + 1) * N] = (dC + _dot(dCBb, Bc)).astype(BF16)
        ddt_ref[0] = ddt_all
        dcs_ref[0] = dcs_all

    rev = lambda b, c: (b * nc + (nc - 1 - c), 0)
    hrow = pl.BlockSpec((1, HR, L), lambda b, c: (b, 0, nc - 1 - c))
    return pl.pallas_call(
        body, name=name, grid=(B, nc),
        in_specs=[pl.BlockSpec((L, XW), rev), hrow, hrow,
                  pl.BlockSpec((1, DI), lambda b, c: (0, 0)),
                  pl.BlockSpec((1, 1, n_pairs, LANE, N), lambda b, c: (b, nc - 1 - c, 0, 0, 0)),
                  pl.BlockSpec((L, DI), rev)],
        out_specs=[pl.BlockSpec((L, XW), rev), hrow, hrow, pl.BlockSpec((1, 1, DI), lambda b, c: (b, 0, 0))],
        out_shape=[jax.ShapeDtypeStruct((B * seq, XW), BF16),
                   jax.ShapeDtypeStruct((B, HR, seq), F32), jax.ShapeDtypeStruct((B, HR, seq), F32),
                   jax.ShapeDtypeStruct((B, 1, DI), F32)],
        scratch_shapes=[pltpu.VMEM((n_pairs, LANE, N), F32)],
        compiler_params=_cp("parallel", "arbitrary"))(xbc, dt_row, cs_row, d_full, states, dy)


def _head_expand(n_heads, DI):
    j = jnp.arange(LANE, dtype=jnp.int32)[:, None]
    c = jnp.arange(DI, dtype=jnp.int32)[None, :] // SSM_HEAD_DIM
    return ((j == c) & (j < n_heads)).astype(BF16)


def _split3(x):
    hi = x.astype(BF16)
    r1 = x - hi.astype(F32)
    mid = r1.astype(BF16)
    return hi, mid, (r1 - mid.astype(F32)).astype(BF16)


def dt_fwd(proj, bias, a_log, expand, *, dt_off, name):
    T = proj.shape[0]
    DI = expand.shape[1]
    L = SSM_CHUNK
    tt = _pick(T, 512, L)

    def body(raw_ref, bias_ref, al_ref, e_ref, dt_ref, cs_ref, dtx_ref, csx_ref):
        x = raw_ref[...].astype(F32) + bias_ref[...]
        dt = _softplus(x)
        dt_ref[...] = dt
        la = dt * (-jnp.exp(al_ref[...]))
        tri = _tri01(L, True)
        for c in range(tt // L):
            cs_ref[c * L:(c + 1) * L, :] = _tri_dot3(tri, la[c * L:(c + 1) * L, :])
        e = e_ref[...]
        dtx_ref[...] = _dot(dt.astype(BF16), e).astype(BF16)
        hi, mid, lo = _split3(cs_ref[...])
        csx_ref[...] = _dot(hi, e) + _dot(mid, e) + _dot(lo, e)

    row = pl.BlockSpec((tt, LANE), lambda i: (i, 0))
    wide = pl.BlockSpec((tt, DI), lambda i: (i, 0))
    vec = pl.BlockSpec((1, LANE), lambda i: (0, 0))
    return pl.pallas_call(
        body, name=name, grid=(T // tt,),
        in_specs=[pl.BlockSpec((tt, LANE), lambda i: (i, dt_off // LANE)), vec, vec,
                  pl.BlockSpec((LANE, DI), lambda i: (0, 0))],
        out_specs=[row, row, wide, wide],
        out_shape=[jax.ShapeDtypeStruct((T, LANE), F32), jax.ShapeDtypeStruct((T, LANE), F32),
                   jax.ShapeDtypeStruct((T, DI), BF16), jax.ShapeDtypeStruct((T, DI), F32)],
        compiler_params=_cp("parallel"))(proj, bias, a_log, expand)


def dt_bwd(ddt_x, dcs_x, dcs_cols, proj, dt, bias, a_log, reduce_t, *, dt_off, n_heads, name):
    T = proj.shape[0]
    DI = reduce_t.shape[0]
    L = SSM_CHUNK
    tt = _pick(T, 512, L)

    def body(ddtx_ref, dcsx_ref, dcsc_ref, raw_ref, dt_ref, bias_ref, al_ref, r_ref, draw_ref, dbias_ref, dal_ref,
             dla_buf):
        @pl.when(pl.program_id(0) == 0)
        def _():
            dbias_ref[...] = jnp.zeros_like(dbias_ref)
            dal_ref[...] = jnp.zeros_like(dal_ref)

        r = r_ref[...]
        ddt = _dot(ddtx_ref[...], r)
        dx = dcsx_ref[...]
        hi = dx.astype(BF16)
        dcs = _dot(hi, r) + _dot((dx - hi.astype(F32)).astype(BF16), r) + dcsc_ref[...]
        triu = _tri01(L, False)
        for c in range(tt // L):
            dla_buf[c * L:(c + 1) * L, :] = _tri_dot3(triu, dcs[c * L:(c + 1) * L, :])
        dla = dla_buf[...]
        a = -jnp.exp(al_ref[...])
        valid = lax.broadcasted_iota(jnp.int32, (tt, LANE), 1) < n_heads
        dal_ref[...] += jnp.sum(jnp.where(valid, dla * dt_ref[...], 0.0), axis=0, keepdims=True) * a
        x = raw_ref[...].astype(F32) + bias_ref[...]
        draw = jnp.where(valid, (ddt + dla * a) * _sigmoid(x), 0.0)
        dbias_ref[...] += jnp.sum(draw, axis=0, keepdims=True)
        draw_ref[...] = draw.astype(BF16)

    row = pl.BlockSpec((tt, LANE), lambda i: (i, 0))
    wide = pl.BlockSpec((tt, DI), lambda i: (i, 0))
    vec = pl.BlockSpec((1, LANE), lambda i: (0, 0))
    return pl.pallas_call(
        body, name=name, grid=(T // tt,),
        in_specs=[wide, wide, row, pl.BlockSpec((tt, LANE), lambda i: (i, dt_off // LANE)), row, vec, vec,
                  pl.BlockSpec((DI, LANE), lambda i: (0, 0))],
        out_specs=[row, vec, vec],
        out_shape=[jax.ShapeDtypeStruct((T, LANE), BF16), jax.ShapeDtypeStruct((1, LANE), F32),
                   jax.ShapeDtypeStruct((1, LANE), F32)],
        scratch_shapes=[pltpu.VMEM((tt, LANE), F32)],
        compiler_params=_cp("arbitrary"))(ddt_x, dcs_x, dcs_cols, proj, dt, bias, a_log, reduce_t)


def _pair_terms(x_ref, dtx_ref, csx_ref, csr_ref, pair, ppg, lo_half, causal):
    L = SSM_CHUNK
    g, pp = divmod(pair, ppg)
    ra = g * HEAD_ROWS + 2 * pp
    cols = slice(pair * LANE, (pair + 1) * LANE)
    X = x_ref[:, cols].astype(F32)
    dt_p = dtx_ref[:, cols].astype(F32)
    own = csx_ref[:, cols]
    other = pltpu.roll(own, SSM_HEAD_DIM, 1)
    csa_c = jnp.where(lo_half, own, other)
    csb_c = jnp.where(lo_half, other, own)
    La = jnp.exp(jnp.where(causal, csa_c - csr_ref[0, ra:ra + 1, :], NEG_BIG))
    Lb = jnp.exp(jnp.where(causal, csb_c - csr_ref[0, ra + 1:ra + 2, :], NEG_BIG))
    last = csx_ref[L - 1:L, cols]
    return g, ra, cols, X, dt_p, La, Lb, jnp.exp(own), jnp.exp(last - own), jnp.exp(last)


def scan_fwd(xbc, dt_x, cs_x, cs_row, d_full, *, B, seq, DI, name):
    L, N, G = SSM_CHUNK, SSM_STATE, SSM_GROUPS
    nc = seq // L
    XW = xbc.shape[1]
    n_pairs = DI // LANE
    ppg = n_pairs // G

    def body(x_ref, dtx_ref, csx_ref, csr_ref, d_ref, y_ref, st_ref, state):
        c = pl.program_id(1)

        @pl.when(c == 0)
        def _():
            state[...] = jnp.zeros_like(state)

        causal = lax.broadcasted_iota(jnp.int32, (L, L), 0) >= lax.broadcasted_iota(jnp.int32, (L, L), 1)
        lo_half = lax.broadcasted_iota(jnp.int32, (L, LANE), 1) < SSM_HEAD_DIM
        cbs = []
        for g in range(G):
            Bc = x_ref[:, DI + g * N:DI + (g + 1) * N]
            Cc = x_ref[:, DI + G * N + g * N:DI + G * N + (g + 1) * N]
            cbs.append((Bc, Cc, _dot(Cc, Bc, NT)))
        for pair in range(n_pairs):
            g, _, cols, X, dt_p, La, Lb, ecs, tail, e_last = _pair_terms(
                x_ref, dtx_ref, csx_ref, csr_ref, pair, ppg, lo_half, causal)
            Bc, Cc, CB = cbs[g]
            xs = X * dt_p
            xsb = xs.astype(BF16)
            y = jnp.where(lo_half, _dot((CB * La).astype(BF16), xsb), _dot((CB * Lb).astype(BF16), xsb))
            ST = state[pair]
            st_ref[0, 0, pair] = ST
            y = y + ecs * _dot(Cc, ST.astype(BF16)) + d_ref[:, cols] * X
            y_ref[:, cols] = y.astype(BF16)
            state[pair] = e_last * ST + _dot(Bc, (xs * tail).astype(BF16), TN)

    wide = pl.BlockSpec((L, DI), lambda b, c: (b * nc + c, 0))
    return pl.pallas_call(
        body, name=name, grid=(B, nc),
        in_specs=[pl.BlockSpec((L, XW), lambda b, c: (b * nc + c, 0)), wide, wide,
                  pl.BlockSpec((1, G * HEAD_ROWS, L), lambda b, c: (b, 0, c)),
                  pl.BlockSpec((1, DI), lambda b, c: (0, 0))],
        out_specs=[wide, pl.BlockSpec((1, 1, n_pairs, N, LANE), lambda b, c: (b, c, 0, 0, 0))],
        out_shape=[jax.ShapeDtypeStruct((B * seq, DI), BF16),
                   jax.ShapeDtypeStruct((B, nc, n_pairs, N, LANE), F32)],
        scratch_shapes=[pltpu.VMEM((n_pairs, N, LANE), F32)],
        compiler_params=_cp("parallel", "arbitrary"))(xbc, dt_x, cs_x, cs_row, d_full)


def scan_bwd(xbc, dt_x, cs_x, cs_row, d_full, states, dy, *, B, seq, DI, name):
    L, N, G = SSM_CHUNK, SSM_STATE, SSM_GROUPS
    nc = seq // L
    XW = xbc.shape[1]
    n_pairs = DI // LANE
    ppg = n_pairs // G
    HR = G * HEAD_ROWS
    inv_p = 1.0 / SSM_HEAD_DIM

    def body(x_ref, dtx_ref, csx_ref, csr_ref, d_ref, st_ref, dy_ref, dx_ref, ddtx_ref, dcsx_ref, dcsr_ref, dd_ref,
             dH):
        c = pl.program_id(1)

        @pl.when(c == 0)
        def _():
            dH[...] = jnp.zeros_like(dH)
            dd_ref[...] = jnp.zeros_like(dd_ref)

        causal = lax.broadcasted_iota(jnp.int32, (L, L), 0) >= lax.broadcasted_iota(jnp.int32, (L, L), 1)
        lo_half = lax.broadcasted_iota(jnp.int32, (L, LANE), 1) < SSM_HEAD_DIM
        last_row = lax.broadcasted_iota(jnp.int32, (L, LANE), 0) == L - 1
        head_row = lax.broadcasted_iota(jnp.int32, (HR, 1), 0)
        dcs_rows = jnp.zeros((HR, L), F32)

        for g in range(G):
            Bc = x_ref[:, DI + g * N:DI + (g + 1) * N]
            Cc = x_ref[:, DI + G * N + g * N:DI + G * N + (g + 1) * N]
            CB = _dot(Cc, Bc, NT)
            dCB = jnp.zeros((L, L), F32)
            dC = jnp.zeros((L, N), F32)
            dB = jnp.zeros((L, N), F32)
            for pp in range(ppg):
                pair = g * ppg + pp
                _, ra, cols, X, dt_p, La, Lb, ecs, tail, e_last = _pair_terms(
                    x_ref, dtx_ref, csx_ref, csr_ref, pair, ppg, lo_half, causal)
                xs = X * dt_p
                xsb = xs.astype(BF16)
                Ma, Mb = CB * La, CB * Lb
                dY = dy_ref[:, cols].astype(F32)
                dYb = dY.astype(BF16)
                dMa = _dot(jnp.where(lo_half, dY, 0.0).astype(BF16), xsb, NT)
                dMb = _dot(jnp.where(lo_half, 0.0, dY).astype(BF16), xsb, NT)
                dSa, dSb = dMa * Ma, dMb * Mb
                dCB = dCB + dMa * La + dMb * Lb
                dcs = jnp.where(lo_half, jnp.sum(dSa, axis=1, keepdims=True), jnp.sum(dSb, axis=1, keepdims=True)) * inv_p
                dcs_rows = dcs_rows - jnp.where(head_row == ra, jnp.sum(dSa, axis=0, keepdims=True), 0.0)
                dcs_rows = dcs_rows - jnp.where(head_row == ra + 1, jnp.sum(dSb, axis=0, keepdims=True), 0.0)
                dxs = jnp.where(lo_half, _dot(Ma.astype(BF16), dYb, TN), _dot(Mb.astype(BF16), dYb, TN))
                ST = st_ref[0, 0, pair]
                STb = ST.astype(BF16)
                dYe = (dY * ecs).astype(BF16)
                dC = dC + _dot(dYe, STb, NT)
                dSTp = _dot(Cc, dYe, TN)
                dcs = dcs + dY * (ecs * _dot(Cc, STb))
                dSTn = dH[pair]
                dSTnb = dSTn.astype(BF16)
                dSTp = dSTp + e_last * dSTn
                XBt = _dot(Bc, dSTnb)
                dxs = dxs + tail * XBt
                t2 = xs * XBt * tail
                at_end = e_last * jnp.sum(dSTn * ST, axis=0, keepdims=True) + jnp.sum(t2, axis=0, keepdims=True)
                dcs = dcs - t2 + jnp.where(last_row, at_end, 0.0)
                dB = dB + _dot((xs * tail).astype(BF16), dSTnb, NT)
                dx_ref[:, cols] = (dxs * dt_p + d_ref[:, cols] * dY).astype(BF16)
                ddtx_ref[:, cols] = (dxs * X).astype(BF16)
                dcsx_ref[:, cols] = dcs
                dd_ref[0, :, cols] += jnp.sum(dY * X, axis=0, keepdims=True)
                dH[pair] = dSTp
            dCBb = dCB.astype(BF16)
            dx_ref[:, DI + g * N:DI + (g + 1) * N] = (dB + _dot(dCBb, Cc, TN)).astype(BF16)
            dx_ref[:, DI + G * N + g * N:DI + G * N + (g + 1) * N] = (dC + _dot(dCBb, Bc)).astype(BF16)
        dcsr_ref[0] = dcs_rows

    rev = lambda b, c: (b * nc + (nc - 1 - c), 0)
    wide = pl.BlockSpec((L, DI), rev)
    hrow = pl.BlockSpec((1, HR, L), lambda b, c: (b, 0, nc - 1 - c))
    return pl.pallas_call(
        body, name=name, grid=(B, nc),
        in_specs=[pl.BlockSpec((L, XW), rev), wide, wide, hrow,
                  pl.BlockSpec((1, DI), lambda b, c: (0, 0)),
                  pl.BlockSpec((1, 1, n_pairs, N, LANE), lambda b, c: (b, nc - 1 - c, 0, 0, 0)), wide],
        out_specs=[pl.BlockSpec((L, XW), rev), wide, wide, hrow, pl.BlockSpec((1, 1, DI), lambda b, c: (b, 0, 0))],
        out_shape=[jax.ShapeDtypeStruct((B * seq, XW), BF16), jax.ShapeDtypeStruct((B * seq, DI), BF16),
                   jax.ShapeDtypeStruct((B * seq, DI), F32), jax.ShapeDtypeStruct((B, HR, seq), F32),
                   jax.ShapeDtypeStruct((B, 1, DI), F32)],
        scratch_shapes=[pltpu.VMEM((n_pairs, N, LANE), F32)],
        compiler_params=_cp("parallel", "arbitrary"))(xbc, dt_x, cs_x, cs_row, d_full, states, dy)


def gnorm_fwd(y, proj, w, *, DI, name):
    T = y.shape[0]
    tt = _pick(T, 256, 16)
    gw = DI // SSM_GROUPS

    def body(y_ref, z_ref, w_ref, o_ref):
        for g in range(SSM_GROUPS):
            sl = slice(g * gw, (g + 1) * gw)
            y2 = y_ref[:, sl].astype(F32) * _silu(z_ref[:, sl].astype(F32))
            r = lax.rsqrt(jnp.mean(y2 * y2, axis=-1, keepdims=True) + EPS)
            o_ref[:, sl] = (y2 * r * w_ref[:, sl]).astype(BF16)

    row = pl.BlockSpec((tt, DI), lambda i: (i, 0))
    return pl.pallas_call(
        body, name=name, grid=(T // tt,),
        in_specs=[row, row, pl.BlockSpec((1, DI), lambda i: (0, 0))], out_specs=row,
        out_shape=jax.ShapeDtypeStruct((T, DI), BF16), compiler_params=_cp("parallel"))(y, proj, w)


def gnorm_bwd(dyn, y, proj, w, *, DI, name):
    T = y.shape[0]
    tt = _pick(T, 256, 16)
    gw = DI // SSM_GROUPS

    def body(dyn_ref, y_ref, z_ref, w_ref, dy_ref, dz_ref, dw_ref):
        @pl.when(pl.program_id(0) == 0)
        def _():
            dw_ref[...] = jnp.zeros_like(dw_ref)

        for g in range(SSM_GROUPS):
            sl = slice(g * gw, (g + 1) * gw)
            yv = y_ref[:, sl].astype(F32)
            z = z_ref[:, sl].astype(F32)
            sz = _silu(z)
            y2 = yv * sz
            r = lax.rsqrt(jnp.mean(y2 * y2, axis=-1, keepdims=True) + EPS)
            xhat = y2 * r
            d = dyn_ref[:, sl].astype(F32)
            dw_ref[:, sl] += jnp.sum(d * xhat, axis=0, keepdims=True)
            dxh = d * w_ref[:, sl]
            dy2 = r * (dxh - xhat * jnp.mean(dxh * xhat, axis=-1, keepdims=True))
            dy_ref[:, sl] = (dy2 * sz).astype(BF16)
            dz_ref[:, sl] = (dy2 * yv * _silu_grad(z)).astype(BF16)

    row = pl.BlockSpec((tt, DI), lambda i: (i, 0))
    vec = pl.BlockSpec((1, DI), lambda i: (0, 0))
    shp = jax.ShapeDtypeStruct((T, DI), BF16)
    return pl.pallas_call(
        body, name=name, grid=(T // tt,), in_specs=[row, row, row, vec], out_specs=[row, row, vec],
        out_shape=[shp, shp, jax.ShapeDtypeStruct((1, DI), F32)],
        compiler_params=_cp("arbitrary"))(dyn, y, proj, w)


N_CHIP = 4


def exchange(srcs, modes, *, name):
    n = len(srcs)
    out_shape = []
    for s, mode in zip(srcs, modes):
        lead = (N_DEV,) if mode in ("gather", "gather_direct") else ()
        out_shape.append(jax.ShapeDtypeStruct(lead + s.shape, s.dtype))

    def body(*refs):
        src_refs, out_refs = refs[:n], refs[n:2 * n]
        send_sems, recv_sems, local_sems = refs[2 * n:]
        x, y, c = lax.axis_index("x"), lax.axis_index("y"), lax.axis_index("c")
        me, sibling = (x, y, c), (x, y, 1 - c)
        chips = [(1 - x, y), (x, 1 - y), (1 - x, 1 - y)]

        def slot(p):
            return 4 * p[0] + 2 * p[1] + p[2]

        def remote(a, k, src, dst, to):
            return pltpu.make_async_remote_copy(src_ref=src, dst_ref=dst, send_sem=send_sems.at[a, k],
                                                recv_sem=recv_sems.at[a, k], device_id=to,
                                                device_id_type=pl.DeviceIdType.MESH)

        local, two_way, send_only, relays = [], [], [], []
        for a, mode in enumerate(modes):
            src, out = src_refs[a], out_refs[a]
            if mode == "sibling":
                two_way.append(remote(a, 0, src, out, sibling))
            elif mode == "chips":
                mine = 2 * x + y
                local.append(pltpu.make_async_copy(src.at[mine], out.at[mine], local_sems.at[a]))
                for j, chip in enumerate(chips):
                    two_way.append(remote(a, 1 + j, src.at[2 * chip[0] + chip[1]], out.at[mine], (*chip, c)))
            elif mode == "gather_direct":
                local.append(pltpu.make_async_copy(src, out.at[slot(me)], local_sems.at[a]))
                for k in range(1, N_DEV):
                    peer = (1 - x if k & 4 else x, 1 - y if k & 2 else y, 1 - c if k & 1 else c)
                    two_way.append(remote(a, k - 1, src, out.at[slot(me)], peer))
            else:
                assert mode == "gather"
                local.append(pltpu.make_async_copy(src, out.at[slot(me)], local_sems.at[a]))
                send_only.append(remote(a, 0, src, out.at[slot(me)], sibling))
                for j, chip in enumerate(chips):
                    send_only.append(remote(a, 1 + j, src, out.at[slot(me)], (*chip, c)))
                relays.append(a)
        for cp in local + two_way + send_only:
            cp.start()
        for a in relays:
            out = out_refs[a]
            for j, chip in enumerate(chips):
                landed = out.at[slot((*chip, c))]
                remote(a, 1 + j, landed, landed, me).wait_recv()
                fwd = remote(a, 4 + j, landed, landed, sibling)
                fwd.start()
                send_only.append(fwd)
        for a in relays:
            out = out_refs[a]
            blk = out.at[slot(sibling)]
            remote(a, 0, blk, blk, me).wait_recv()
            for j, chip in enumerate(chips):
                blk = out.at[slot((*chip, 1 - c))]
                remote(a, 4 + j, blk, blk, me).wait_recv()
        for cp in send_only:
            cp.wait_send()
        for cp in two_way + local:
            cp.wait()

    any_spec = pl.BlockSpec(memory_space=pl.ANY)
    return pl.pallas_call(
        body, name=name, in_specs=[any_spec] * n, out_specs=[any_spec] * n, out_shape=out_shape,
        scratch_shapes=[pltpu.SemaphoreType.DMA((n, N_DEV - 1)), pltpu.SemaphoreType.DMA((n, N_DEV - 1)),
                        pltpu.SemaphoreType.DMA((n,))],
        compiler_params=pltpu.CompilerParams(has_side_effects=True))(*srcs)


def pair_sum(a, b, *, name):
    n, R, C = a.shape
    tr = _pick(n * R, 512, 16)

    def body(a_ref, b_ref, o_ref):
        o_ref[...] = (a_ref[...].astype(F32) + b_ref[...].astype(F32)).astype(BF16)

    blk = pl.BlockSpec((tr, C), lambda i: (i, 0))
    out = pl.pallas_call(
        body, name=name, grid=(n * R // tr,), in_specs=[blk, blk], out_specs=blk,
        out_shape=jax.ShapeDtypeStruct((n * R, C), BF16),
        compiler_params=_cp("parallel"))(a.reshape(n * R, C), b.reshape(n * R, C))
    return out.reshape(n, R, C)


def sum_slots(recv, *, name):
    _, R, C = recv.shape
    tr = _pick(R, 512, 8)

    def body(r_ref, o_ref):
        acc = r_ref[0].astype(F32)
        for p in range(1, N_DEV):
            acc = acc + r_ref[p].astype(F32)
        o_ref[...] = acc

    return pl.pallas_call(
        body, name=name, grid=(R // tr,),
        in_specs=[pl.BlockSpec((N_DEV, tr, C), lambda i: (0, i, 0))],
        out_specs=pl.BlockSpec((tr, C), lambda i: (i, 0)),
        out_shape=jax.ShapeDtypeStruct((R, C), F32), compiler_params=_cp("parallel"))(recv)


def adamw(gsrc, w, m, v, *, name):
    slots, R, C = gsrc.shape
    tr = _pick(R, 256, 16 if gsrc.dtype == BF16 else 8)
    c1 = 1.0 / (1.0 - ADAM_B1 ** ADAM_STEP)
    c2 = 1.0 / (1.0 - ADAM_B2 ** ADAM_STEP)

    def body(g_ref, w_ref, m_ref, v_ref, go_ref, d_ref, mo_ref, vo_ref):
        g = g_ref[0].astype(F32)
        for p in range(1, slots):
            g = g + g_ref[p].astype(F32)
        m2 = ADAM_B1 * m_ref[...] + (1.0 - ADAM_B1) * g
        v2 = ADAM_B2 * v_ref[...] + (1.0 - ADAM_B2) * (g * g)
        go_ref[...] = g
        mo_ref[...] = m2
        vo_ref[...] = v2
        d_ref[...] = -ADAM_LR * ((m2 * c1) / (jnp.sqrt(v2 * c2) + ADAM_EPS) + ADAM_WD * w_ref[...])

    blk = pl.BlockSpec((tr, C), lambda i: (i, 0))
    shp = jax.ShapeDtypeStruct((R, C), F32)
    return pl.pallas_call(
        body, name=name, grid=(R // tr,),
        in_specs=[pl.BlockSpec((slots, tr, C), lambda i: (0, i, 0)), blk, blk, blk],
        out_specs=[blk] * 4, out_shape=[shp] * 4, compiler_params=_cp("parallel"))(gsrc, w, m, v)


def _pad_cols(a, n):
    return jnp.pad(a, ((0, 0), (0, n - a.shape[1])))


def _to_rows(a, B, seq, H):
    G = SSM_GROUPS
    R = H // G
    t = a[:, :H].reshape(B, seq, G, R).transpose(0, 2, 3, 1)
    t = jnp.pad(t, ((0, 0), (0, 0), (0, HEAD_ROWS - R), (0, 0)))
    return t.reshape(B, G * HEAD_ROWS, seq)


def _from_rows(a, B, seq, H):
    G = SSM_GROUPS
    R = H // G
    t = a.reshape(B, G, HEAD_ROWS, seq)[:, :, :R].transpose(0, 3, 1, 2).reshape(B * seq, H)
    return _pad_cols(t, LANE)


def local_step(x, target, p, *, B, seq):
    T, D = x.shape
    CW = D
    heads = CW // SB_HEAD_DIM
    DI = 2 * D
    H = DI // SSM_HEAD_DIM
    XW = DI + 2 * SSM_GROUPS * SSM_STATE
    in_odd = DI + XW + H
    w1t = p["od_w_in_t"]
    q_off, k_off, v_off, gc_off, ga_off = 3 * CW, 4 * CW, 5 * CW, 2 * CW, 6 * CW
    dt_off = DI + XW

    n0 = rmsnorm_fwd(x, p["ev_norm_w"], name="l0_norm")
    proj0 = mm_nn(n0, p["ev_w_in"], out_dtype=BF16, name="l0_in_proj")
    (u2,) = dwconv_fwd(proj0, (0, CW), p["ev_dw_w"], p["ev_dw_b"], C=CW, seq=seq, glu=True, silu_out=False,
                       name="l0_conv")
    o, ctot = sba_fwd(proj0, B=B, seq=seq, heads=heads, q_off=q_off, k_off=k_off, v_off=v_off, name="l0_attn")
    ycat = mix0_post_fwd(u2, proj0, o, p["ev_ln_w"], p["ev_ln_b"], CW=CW, gc_off=gc_off, ga_off=ga_off,
                         name="l0_post")
    h1 = mm_nn(ycat, p["ev_w_out"], add=x, out_dtype=F32, name="l0_out_proj")

    n1 = rmsnorm_fwd(h1, p["od_norm_w"], name="l1_norm")
    proj1 = mm_nt_terms([(n1, 0, D, 0)], w1t, out_dtype=BF16, name="l1_in_proj")
    u_pre, xbc = dwconv_fwd(proj1, (DI,), p["od_conv_w"], p["od_conv_b"], C=XW, seq=seq, glu=False, silu_out=True,
                            name="l1_conv")
    bias_p, alog_p = _pad_cols(p["od_dt_bias"], LANE), _pad_cols(p["od_a_log"], LANE)
    expand = _head_expand(H, DI)
    dt, cs, dt_x, cs_x = dt_fwd(proj1, bias_p, alog_p, expand, dt_off=dt_off, name="l1_dt")
    cs_row = _to_rows(cs, B, seq, H)
    d_full = jnp.repeat(p["od_d"], SSM_HEAD_DIM, axis=1)
    y_ssd, states = scan_fwd(xbc, dt_x, cs_x, cs_row, d_full, B=B, seq=seq, DI=DI, name="l1_ssd")
    yn = gnorm_fwd(y_ssd, proj1, p["od_gnorm_w"], DI=DI, name="l1_gnorm")
    h2 = mm_nn(yn, p["od_w_out"], add=h1, out_dtype=F32, name="l1_out_proj")

    loss, dh2, g_final = final_loss(h2, p["final_norm_w"], target, name="loss_head")

    dh2b = dh2.astype(BF16)
    g_od_w_out = mm_tn(yn, dh2b, out_dtype=BF16, name="l1_dw_out")
    dyn = mm_nt_terms([(dh2b, 0, D, 0)], p["od_w_out"], out_dtype=BF16, name="l1_d_out_proj")
    dy_ssd, dz, g_gnorm = gnorm_bwd(dyn, y_ssd, proj1, p["od_gnorm_w"], DI=DI, name="l1_gnorm_bwd")
    dxbc_c, ddt_x, dcs_x, dcs_row, dd_part = scan_bwd(xbc, dt_x, cs_x, cs_row, d_full, states, dy_ssd, B=B, seq=seq,
                                                      DI=DI, name="l1_ssd_bwd")
    g_d = dd_part.sum(axis=(0, 1)).reshape(H, SSM_HEAD_DIM).sum(axis=1)[None, :]
    draw, g_bias, g_alog = dt_bwd(ddt_x, dcs_x, _from_rows(dcs_row, B, seq, H), proj1, dt, bias_p, alog_p, expand.T,
                                  dt_off=dt_off, n_heads=H, name="l1_dt_bwd")
    dxbc, g_conv_w, g_conv_b = dwconv_bwd(dxbc_c, u_pre, proj1, (DI,), p["od_conv_w"], C=XW, seq=seq, glu=False,
                                          silu_out=True, name="l1_conv_bwd")
    tw = 512 if DI % 512 == 0 else LANE
    terms = [(dz, j, tw, j * tw) for j in range(DI // tw)]
    terms += [(dxbc, j, tw, DI + j * tw) for j in range(XW // tw)]
    terms += [(draw, 0, LANE, dt_off)]
    dn1 = mm_nn_terms(terms, w1t, out_dtype=F32, name="l1_d_in_proj")
    g_od_w_in_t = jnp.concatenate([mm_tn(dz, n1, out_dtype=BF16, name="l1_dw_in_z"),
                                   mm_tn(dxbc, n1, out_dtype=BF16, name="l1_dw_in_xbc"),
                                   mm_tn(draw, n1, out_dtype=BF16, name="l1_dw_in_dt")], axis=0)[:in_odd]
    dh1, g_od_norm = rmsnorm_bwd(h1, p["od_norm_w"], dn1, dh2, name="l1_norm_bwd")

    dh1b = dh1.astype(BF16)
    g_ev_w_out = mm_tn(ycat, dh1b, out_dtype=BF16, name="l0_dw_out")
    dycat = mm_nt_terms([(dh1b, 0, D, 0)], p["ev_w_out"], out_dtype=BF16, name="l0_d_out_proj")
    du2, dgc, dga, do, g_ln_w, g_ln_b = mix0_post_bwd(dycat, u2, proj0, o, p["ev_ln_w"], p["ev_ln_b"], CW=CW,
                                                      gc_off=gc_off, ga_off=ga_off, name="l0_post_bwd")
    dq, dk, dv = sba_bwd(proj0, ctot, do, B=B, seq=seq, heads=heads, q_off=q_off, k_off=k_off, v_off=v_off,
                         name="l0_attn_bwd")
    dga_a, dga_b, g_dw_w, g_dw_b = dwconv_bwd(du2, None, proj0, (0, CW), p["ev_dw_w"], C=CW, seq=seq, glu=True,
                                              silu_out=False, name="l0_conv_bwd")
    pieces = [dga_a, dga_b, dgc, dq, dk, dv, dga]
    dn0 = mm_nt_terms([(pc, 0, CW, j * CW) for j, pc in enumerate(pieces)], p["ev_w_in"], out_dtype=F32,
                      name="l0_d_in_proj")
    g_ev_w_in = jnp.concatenate([mm_tn(n0, pc, out_dtype=BF16, name=f"l0_dw_in_{j}") for j, pc in enumerate(pieces)],
                                axis=1)
    dx, g_ev_norm = rmsnorm_bwd(x, p["ev_norm_w"], dn0, dh1, name="l0_norm_bwd")

    grads = dict(ev_norm_w=g_ev_norm, ev_w_in=g_ev_w_in, ev_dw_w=g_dw_w, ev_dw_b=g_dw_b, ev_ln_w=g_ln_w,
                 ev_ln_b=g_ln_b, ev_w_out=g_ev_w_out, od_norm_w=g_od_norm, od_w_in_t=g_od_w_in_t, od_conv_w=g_conv_w,
                 od_conv_b=g_conv_b, od_dt_bias=g_bias[:, :H], od_a_log=g_alog[:, :H], od_d=g_d, od_gnorm_w=g_gnorm,
                 od_w_out=g_od_w_out, final_norm_w=g_final)
    return loss, dx, grads


BIG = ("ev_w_in", "ev_w_out", "od_w_in", "od_w_out")
SMALL = ("ev_norm_w", "ev_dw_w", "ev_dw_b", "ev_ln_w", "ev_ln_b", "od_norm_w", "od_conv_w", "od_conv_b",
         "od_dt_bias", "od_a_log", "od_d", "od_gnorm_w", "final_norm_w")
SMALL_SHARDED = ("ev_dw_w", "od_norm_w", "od_conv_w", "od_conv_b", "od_gnorm_w")
ORDER = ("ev_norm_w", "ev_w_in", "ev_dw_w", "ev_dw_b", "ev_ln_w", "ev_ln_b", "ev_w_out", "od_norm_w", "od_w_in",
         "od_conv_w", "od_conv_b", "od_dt_bias", "od_a_log", "od_d", "od_gnorm_w", "od_w_out", "final_norm_w")


def _pack_rows(arrs, width, row_align):
    parts, spans, r0 = [], [], 0
    for a in arrs:
        flat = a.reshape(-1)
        rows = -(-flat.shape[0] // (width * row_align)) * row_align
        parts.append(jnp.pad(flat, (0, rows * width - flat.shape[0])).reshape(rows, width))
        spans.append((r0, a.size, a.shape))
        r0 += rows
    return jnp.concatenate(parts, axis=0), spans


def _unpack_rows(packed, spans):
    lead = packed.shape[:-2]
    width = packed.shape[-1]
    out = []
    for r0, size, shape in spans:
        rows = -(-size // width)
        blk = packed[..., r0:r0 + rows, :].reshape(lead + (rows * width,))[..., :size]
        out.append(blk.reshape(lead + tuple(shape)))
    return out


def _col_shards(a):
    R, C8 = a.shape
    return a.reshape(R, N_DEV, C8 // N_DEV).transpose(1, 0, 2)


def _col_unshards(a):
    n, R, C = a.shape
    return a.transpose(1, 0, 2).reshape(R, n * C)


def kernel(x, ev_norm_w, ev_w_in, ev_dw_w, ev_dw_b, ev_ln_w, ev_ln_b, ev_w_out, od_norm_w, od_w_in, od_conv_w, od_conv_b, od_dt_bias, od_a_log, od_d, od_gnorm_w, od_w_out, final_norm_w, loss_target, m_ev_norm_w, m_ev_w_in, m_ev_dw_w, m_ev_dw_b, m_ev_ln_w, m_ev_ln_b, m_ev_w_out, m_od_norm_w, m_od_w_in, m_od_conv_w, m_od_conv_b, m_od_dt_bias, m_od_a_log, m_od_d, m_od_gnorm_w, m_od_w_out, m_final_norm_w, v_ev_norm_w, v_ev_w_in, v_ev_dw_w, v_ev_dw_b, v_ev_ln_w, v_ev_ln_b, v_ev_w_out, v_od_norm_w, v_od_w_in, v_od_conv_w, v_od_conv_b, v_od_dt_bias, v_od_a_log, v_od_d, v_od_gnorm_w, v_od_w_out, v_final_norm_w):
    loc = dict(ev_norm_w=ev_norm_w, ev_w_in=ev_w_in, ev_dw_w=ev_dw_w, ev_dw_b=ev_dw_b, ev_ln_w=ev_ln_w,
               ev_ln_b=ev_ln_b, ev_w_out=ev_w_out, od_norm_w=od_norm_w, od_w_in=od_w_in, od_conv_w=od_conv_w,
               od_conv_b=od_conv_b, od_dt_bias=od_dt_bias, od_a_log=od_a_log, od_d=od_d, od_gnorm_w=od_gnorm_w,
               od_w_out=od_w_out, final_norm_w=final_norm_w)
    mom = dict(ev_norm_w=m_ev_norm_w, ev_w_in=m_ev_w_in, ev_dw_w=m_ev_dw_w, ev_dw_b=m_ev_dw_b, ev_ln_w=m_ev_ln_w,
               ev_ln_b=m_ev_ln_b, ev_w_out=m_ev_w_out, od_norm_w=m_od_norm_w, od_w_in=m_od_w_in,
               od_conv_w=m_od_conv_w, od_conv_b=m_od_conv_b, od_dt_bias=m_od_dt_bias, od_a_log=m_od_a_log,
               od_d=m_od_d, od_gnorm_w=m_od_gnorm_w, od_w_out=m_od_w_out, final_norm_w=m_final_norm_w)
    var = dict(ev_norm_w=v_ev_norm_w, ev_w_in=v_ev_w_in, ev_dw_w=v_ev_dw_w, ev_dw_b=v_ev_dw_b, ev_ln_w=v_ev_ln_w,
               ev_ln_b=v_ev_ln_b, ev_w_out=v_ev_w_out, od_norm_w=v_od_norm_w, od_w_in=v_od_w_in,
               od_conv_w=v_od_conv_w, od_conv_b=v_od_conv_b, od_dt_bias=v_od_dt_bias, od_a_log=v_od_a_log,
               od_d=v_od_d, od_gnorm_w=v_od_gnorm_w, od_w_out=v_od_w_out, final_norm_w=v_final_norm_w)
    shapes = {n: loc[n].shape for n in ORDER}
    loc = {n: (a.reshape(1, -1) if a.ndim == 1 else a.reshape(a.shape[-2:]) if a.ndim == 3 else a)
           for n, a in loc.items()}
    mom = {n: a.reshape(loc[n].shape) for n, a in mom.items()}
    var = {n: a.reshape(loc[n].shape) for n, a in var.items()}

    B, seq, D = x.shape
    me = 4 * lax.axis_index("x") + 2 * lax.axis_index("y") + lax.axis_index("c")

    w1_rows = loc["od_w_in"].shape[1]
    w1_pad = (-w1_rows) % 16

    def to_t(a):
        return jnp.pad(a.T, ((0, w1_pad), (0, 0)))

    small_packed, small_spans = _pack_rows([loc[n] for n in SMALL_SHARDED], LANE, 8)
    g_ev_in, g_ev_out, g_od_in_t, g_od_out, small_all = exchange(
        [loc["ev_w_in"].astype(BF16), loc["ev_w_out"].astype(BF16), to_t(loc["od_w_in"]).astype(BF16),
         loc["od_w_out"].astype(BF16), small_packed], ["gather"] * 4 + ["gather_direct"], name="gather_weights")
    full = dict(loc)
    del full["od_w_in"]
    full["ev_w_in"] = _col_unshards(g_ev_in)
    full["ev_w_out"] = g_ev_out.reshape(-1, D)
    full["od_w_out"] = g_od_out.reshape(-1, D)
    w1t = g_od_in_t[:, :w1_rows].reshape(N_DEV * w1_rows, D)
    full["od_w_in_t"] = jnp.pad(w1t, ((0, -(-(w1t.shape[0] + LANE) // 256) * 256 - w1t.shape[0]), (0, 0)))
    for n, a in zip(SMALL_SHARDED, _unpack_rows(small_all, small_spans)):
        full[n] = _col_unshards(a)

    loss, dx, grads = local_step(x.reshape(B * seq, D), loss_target.reshape(B * seq, D), full, B=B, seq=seq)

    g1t = jnp.pad(grads["od_w_in_t"].reshape(N_DEV, w1_rows, D), ((0, 0), (0, w1_pad), (0, 0)))
    gsmall_packed, gsmall_spans = _pack_rows([grads[n] for n in SMALL], LANE, 8)
    c_idx = lax.axis_index("c")
    keep, give = [], []
    for g in (_col_shards(grads["ev_w_in"]), grads["ev_w_out"].reshape(N_DEV, -1, D), g1t,
              grads["od_w_out"].reshape(N_DEV, -1, D)):
        by_chip = g.reshape((N_CHIP, 2) + g.shape[1:])
        keep.append(lax.dynamic_index_in_dim(by_chip, c_idx, axis=1, keepdims=False))
        give.append(lax.dynamic_index_in_dim(by_chip, 1 - c_idx, axis=1, keepdims=False))
    swapped = exchange(give, ["sibling"] * 4, name="swap_grads")
    chip_sums = [pair_sum(k, s, name=f"chip_sum_{i}") for i, (k, s) in enumerate(zip(keep, swapped))]
    r_ev_in, r_ev_out, r_od_in_t, r_od_out, gsmall_recv = exchange(
        chip_sums + [gsmall_packed], ["chips"] * 4 + ["gather_direct"], name="exchange_grads")

    big_out = [{} for _ in range(4)]
    for n, recv in (("ev_w_in", r_ev_in), ("ev_w_out", r_ev_out), ("od_w_out", r_od_out)):
        for kind, a in enumerate(adamw(recv, loc[n], mom[n], var[n], name="adamw_" + n)):
            big_out[kind][n] = a
    for kind, a in enumerate(adamw(r_od_in_t, to_t(loc["od_w_in"]), to_t(mom["od_w_in"]), to_t(var["od_w_in"]),
                                   name="adamw_od_w_in")):
        big_out[kind]["od_w_in"] = a[:w1_rows].T

    gsmall = dict(zip(SMALL, _unpack_rows(sum_slots(gsmall_recv, name="sum_small_grads"), gsmall_spans)))
    for n in SMALL_SHARDED:
        width = loc[n].shape[1]
        gsmall[n] = lax.dynamic_slice_in_dim(gsmall[n], me * width, width, axis=1)
    gs, sspans = _pack_rows([gsmall[n] for n in SMALL], LANE, 8)
    ws, _ = _pack_rows([loc[n] for n in SMALL], LANE, 8)
    ms, _ = _pack_rows([mom[n] for n in SMALL], LANE, 8)
    vs, _ = _pack_rows([var[n] for n in SMALL], LANE, 8)
    small_out = [dict(zip(SMALL, _unpack_rows(a, sspans))) for a in adamw(gs[None], ws, ms, vs, name="adamw_small")]

    outs = [lax.psum(loss[0, 0], ("x", "y", "c")), dx.reshape(B, seq, D)]
    for kind in range(4):
        for n in ORDER:
            src = big_out[kind] if n in BIG else small_out[kind]
            outs.append(src[n].reshape(shapes[n]))
    return tuple(outs)
```

```python
import functools

import jax
import jax.numpy as jnp
from jax import lax
from jax.experimental import pallas as pl
from jax.experimental.pallas import tpu as pltpu

F32 = jnp.float32
BF16 = jnp.bfloat16

EPS = 1e-6
N_DEV = 8
LANE = 128
VMEM_LIMIT_BYTES = 48 * 1024 * 1024

SB_HEAD_DIM = 128
CONF_KERNEL = 31
SSM_CONV = 4
SSM_HEAD_DIM = 64
SSM_GROUPS = 4
SSM_STATE = 128
SSM_CHUNK = 128
HALO = 32
HEAD_ROWS = 8
NEG_BIG = -1e30

ADAM_LR = 0.001
ADAM_B1 = 0.9
ADAM_B2 = 0.999
ADAM_EPS = 1e-08
ADAM_WD = 0.01
ADAM_STEP = 10

NT = (((1,), (1,)), ((), ()))
TN = (((0,), (0,)), ((), ()))


def _cp(*sem):
    return pltpu.CompilerParams(dimension_semantics=sem, vmem_limit_bytes=VMEM_LIMIT_BYTES)


def _pick(n, cap, align):
    if n <= cap:
        return n
    t = (cap // align) * align
    while t >= align:
        if n % t == 0:
            return t
        t -= align
    raise ValueError(f"no tile for {n} (cap {cap}, align {align})")


def _sigmoid(x):
    return 1.0 / (1.0 + jnp.exp(-x))


def _silu(x):
    return x * _sigmoid(x)


def _silu_grad(x):
    s = _sigmoid(x)
    return s * (1.0 + x * (1.0 - s))


def _dot(a, b, dims=None):
    if dims is None:
        return jnp.dot(a, b, preferred_element_type=F32)
    return lax.dot_general(a, b, dims, preferred_element_type=F32)


def _tri_dot3(tri, x):
    hi = x.astype(BF16)
    r1 = x - hi.astype(F32)
    mid = r1.astype(BF16)
    lo = (r1 - mid.astype(F32)).astype(BF16)
    return _dot(tri, hi) + _dot(tri, mid) + _dot(tri, lo)


def _grid_step(n_inner):
    return lambda: pl.program_id(0) * n_inner + pl.program_id(1)


def mm_nn(a, b, *, add=None, out_dtype, name, comm=None):
    M, K = a.shape
    N = b.shape[1]
    tm = _pick(M, 1024, 16)
    tn = _pick(N, 1024, LANE)

    def body(*refs):
        if add is None:
            a_ref, b_ref, o_ref = refs
        else:
            a_ref, b_ref, add_ref, o_ref = refs
        acc = _dot(a_ref[...], b_ref[...])
        if add is not None:
            acc = acc + add_ref[...]
        o_ref[...] = acc.astype(out_dtype)

    in_specs = [pl.BlockSpec((tm, K), lambda i, j: (i, 0)), pl.BlockSpec((K, tn), lambda i, j: (0, j))]
    args = [a, b]
    if add is not None:
        in_specs.append(pl.BlockSpec((tm, tn), lambda i, j: (i, j)))
        args.append(add)
    grid = (M // tm, N // tn)
    body, c_in, c_args, c_out, c_shape, c_scratch = _hosted(body, len(args), 1, comm, _grid_step(grid[1]),
                                                            grid[0] * grid[1])
    out = pl.pallas_call(
        body, name=name, grid=grid, in_specs=in_specs + c_in,
        out_specs=[pl.BlockSpec((tm, tn), lambda i, j: (i, j))] + c_out,
        out_shape=[jax.ShapeDtypeStruct((M, N), out_dtype)] + c_shape, scratch_shapes=c_scratch,
        compiler_params=_cp(*(("arbitrary",) * 2 if comm else ("parallel",) * 2)))(*args, *c_args)
    return (out[0], out[1:]) if comm else out[0]


def mm_nt_terms(terms, b, *, out_dtype, name, comm=None):
    M = terms[0][0].shape[0]
    N = b.shape[0]
    n_terms = len(terms)
    cap = 1024 if n_terms == 1 else 512
    tm = _pick(M, cap, 16)
    tn = _pick(N, cap, LANE)

    def body(*refs):
        o_ref = refs[-1]
        acc = None
        for t in range(n_terms):
            part = _dot(refs[2 * t][...], refs[2 * t + 1][...], NT)
            acc = part if acc is None else acc + part
        o_ref[...] = acc.astype(out_dtype)

    in_specs, args = [], []
    for arr, cb, w, off in terms:
        assert off % w == 0
        in_specs.append(pl.BlockSpec((tm, w), lambda i, j, cb=cb: (i, cb)))
        in_specs.append(pl.BlockSpec((tn, w), lambda i, j, ob=off // w: (j, ob)))
        args += [arr, b]
    grid = (M // tm, N // tn)
    body, c_in, c_args, c_out, c_shape, c_scratch = _hosted(body, len(args), 1, comm, _grid_step(grid[1]),
                                                            grid[0] * grid[1])
    out = pl.pallas_call(
        body, name=name, grid=grid, in_specs=in_specs + c_in,
        out_specs=[pl.BlockSpec((tm, tn), lambda i, j: (i, j))] + c_out,
        out_shape=[jax.ShapeDtypeStruct((M, N), out_dtype)] + c_shape, scratch_shapes=c_scratch,
        compiler_params=_cp(*(("arbitrary",) * 2 if comm else ("parallel",) * 2)))(*args, *c_args)
    return (out[0], out[1:]) if comm else out[0]


def mm_nn_terms(terms, b, *, out_dtype, name):
    M = terms[0][0].shape[0]
    N = b.shape[1]
    tm = _pick(M, 512, 16)
    tn = _pick(N, 512, LANE)
    n_terms = len(terms)

    def body(*refs):
        o_ref = refs[-1]
        acc = None
        for t in range(n_terms):
            part = _dot(refs[2 * t][...], refs[2 * t + 1][...])
            acc = part if acc is None else acc + part
        o_ref[...] = acc.astype(out_dtype)

    in_specs, args = [], []
    for arr, cb, w, off in terms:
        assert off % w == 0
        in_specs.append(pl.BlockSpec((tm, w), lambda i, j, cb=cb: (i, cb)))
        in_specs.append(pl.BlockSpec((w, tn), lambda i, j, ob=off // w: (ob, j)))
        args += [arr, b]
    return pl.pallas_call(
        body, name=name, grid=(M // tm, N // tn), in_specs=in_specs,
        out_specs=pl.BlockSpec((tm, tn), lambda i, j: (i, j)),
        out_shape=jax.ShapeDtypeStruct((M, N), out_dtype),
        compiler_params=_cp("parallel", "parallel"))(*args)


def mm_tn(a, b, *, out_dtype, name):
    T, M = a.shape
    N = b.shape[1]
    tm = _pick(M, 1024, LANE)
    tn = _pick(N, 1024, LANE)
    tk = _pick(T, 1024, 16)
    nk = T // tk

    def body(a_ref, b_ref, o_ref, acc_ref):
        k = pl.program_id(2)

        @pl.when(k == 0)
        def _():
            acc_ref[...] = jnp.zeros_like(acc_ref)

        acc_ref[...] += _dot(a_ref[...], b_ref[...], TN)

        @pl.when(k == nk - 1)
        def _():
            o_ref[...] = acc_ref[...].astype(out_dtype)

    return pl.pallas_call(
        body, name=name, grid=(M // tm, N // tn, nk),
        in_specs=[pl.BlockSpec((tk, tm), lambda i, j, k: (k, i)), pl.BlockSpec((tk, tn), lambda i, j, k: (k, j))],
        out_specs=pl.BlockSpec((tm, tn), lambda i, j, k: (i, j)),
        out_shape=jax.ShapeDtypeStruct((M, N), out_dtype),
        scratch_shapes=[pltpu.VMEM((tm, tn), F32)],
        compiler_params=_cp("parallel", "parallel", "arbitrary"))(a, b)


def rmsnorm_fwd(h, w, *, name):
    T, D = h.shape
    tt = _pick(T, 512, 16)

    def body(h_ref, w_ref, n_ref):
        x = h_ref[...]
        r = lax.rsqrt(jnp.mean(x * x, axis=-1, keepdims=True) + EPS)
        n_ref[...] = (x * r * w_ref[...]).astype(BF16)

    return pl.pallas_call(
        body, name=name, grid=(T // tt,),
        in_specs=[pl.BlockSpec((tt, D), lambda i: (i, 0)), pl.BlockSpec((1, D), lambda i: (0, 0))],
        out_specs=pl.BlockSpec((tt, D), lambda i: (i, 0)),
        out_shape=jax.ShapeDtypeStruct((T, D), BF16),
        compiler_params=_cp("parallel"))(h, w)


def rmsnorm_bwd(h, w, dn, dres, *, name):
    T, D = h.shape
    tt = _pick(T, 512, 16)

    def body(h_ref, w_ref, dn_ref, dres_ref, dh_ref, gw_ref):
        @pl.when(pl.program_id(0) == 0)
        def _():
            gw_ref[...] = jnp.zeros_like(gw_ref)

        x = h_ref[...]
        r = lax.rsqrt(jnp.mean(x * x, axis=-1, keepdims=True) + EPS)
        xhat = x * r
        g = dn_ref[...].astype(F32)
        gw_ref[...] += jnp.sum(g * xhat, axis=0, keepdims=True)
        dxh = g * w_ref[...]
        dx = r * (dxh - xhat * jnp.mean(dxh * xhat, axis=-1, keepdims=True))
        dh_ref[...] = dres_ref[...] + dx

    row = pl.BlockSpec((tt, D), lambda i: (i, 0))
    vec = pl.BlockSpec((1, D), lambda i: (0, 0))
    return pl.pallas_call(
        body, name=name, grid=(T // tt,), in_specs=[row, vec, row, row], out_specs=[row, vec],
        out_shape=[jax.ShapeDtypeStruct((T, D), F32), jax.ShapeDtypeStruct((1, D), F32)],
        compiler_params=_cp("arbitrary"))(h, w, dn, dres)


def final_loss(h, w, target, *, name):
    T, D = h.shape
    tt = _pick(T, 512, 16)

    def body(h_ref, w_ref, t_ref, loss_ref, dh_ref, gw_ref):
        @pl.when(pl.program_id(0) == 0)
        def _():
            gw_ref[...] = jnp.zeros_like(gw_ref)
            loss_ref[...] = jnp.zeros_like(loss_ref)

        x = h_ref[...]
        r = lax.rsqrt(jnp.mean(x * x, axis=-1, keepdims=True) + EPS)
        xhat = x * r
        e = xhat * w_ref[...] - t_ref[...]
        loss_ref[...] += jnp.sum(e * e) * (0.5 / D)
        g = e * (1.0 / D)
        gw_ref[...] += jnp.sum(g * xhat, axis=0, keepdims=True)
        dxh = g * w_ref[...]
        dh_ref[...] = r * (dxh - xhat * jnp.mean(dxh * xhat, axis=-1, keepdims=True))

    row = pl.BlockSpec((tt, D), lambda i: (i, 0))
    vec = pl.BlockSpec((1, D), lambda i: (0, 0))
    one = pl.BlockSpec((1, LANE), lambda i: (0, 0))
    return pl.pallas_call(
        body, name=name, grid=(T // tt,), in_specs=[row, vec, row], out_specs=[one, row, vec],
        out_shape=[jax.ShapeDtypeStruct((1, LANE), F32), jax.ShapeDtypeStruct((T, D), F32),
                   jax.ShapeDtypeStruct((1, D), F32)],
        compiler_params=_cp("arbitrary"))(h, w, target)


CONV_CHUNK = 32
SUBLANES = 8


def _conv_tiles(seq, C):
    return _pick(seq, 512, HALO), _pick(C, 512, LANE)


def _residues(offsets):
    return sorted({s % SUBLANES for s in offsets} - {0})


def _fill_shifted(buf, shifted, residues):
    n = buf.shape[0] - SUBLANES
    for i, r in enumerate(residues):
        shifted[i, 0:n, :] = buf[r:r + n, :]


def _tap(buf, shifted, residues, offset, start, rows):
    r = offset % SUBLANES
    base = offset - r
    ref = buf if r == 0 else shifted.at[residues.index(r)]
    return ref[pl.ds(start + base, rows), :]


def dwconv_fwd(src, offs, w, b, *, C, seq, glu, silu_out, name):
    T = src.shape[0]
    K = w.shape[0]
    tt, tc = _conv_tiles(seq, C)
    n_in = 2 if glu else 1
    per = tt // HALO
    offsets = [HALO - (K - 1) + k for k in range(K)]
    residues = _residues(offsets)

    def body(*refs):
        cur = refs[0:2 * n_in:2]
        halo = refs[1:2 * n_in:2]
        w_ref, b_ref = refs[2 * n_in], refs[2 * n_in + 1]
        outs = refs[2 * n_in + 2:-2]
        buf, shifted = refs[-2], refs[-1]
        i = pl.program_id(1)
        first = (i * tt) % seq == 0

        def pre(rs):
            v = rs[0][...].astype(F32)
            return v * _sigmoid(rs[1][...].astype(F32)) if glu else v

        buf[0:HALO, :] = jnp.where(first, 0.0, pre(halo))
        buf[HALO:HALO + tt, :] = pre(cur)
        _fill_shifted(buf, shifted, residues)

        def chunk(ci, carry):
            start = pl.multiple_of(ci * CONV_CHUNK, CONV_CHUNK)
            acc = jnp.broadcast_to(b_ref[...], (CONV_CHUNK, tc))
            for k in range(K):
                acc = acc + w_ref[k:k + 1, :] * _tap(buf, shifted, residues, offsets[k], start, CONV_CHUNK)
            outs[0][pl.ds(start, CONV_CHUNK), :] = acc.astype(BF16)
            if silu_out:
                outs[1][pl.ds(start, CONV_CHUNK), :] = _silu(acc).astype(BF16)
            return carry

        lax.fori_loop(0, tt // CONV_CHUNK, chunk, 0)

    in_specs, args = [], []
    for off in offs:
        assert off % tc == 0
        in_specs.append(pl.BlockSpec((tt, tc), lambda j, i, ob=off // tc: (i, ob + j)))
        in_specs.append(pl.BlockSpec((HALO, tc), lambda j, i, ob=off // tc: (jnp.maximum(i * per - 1, 0), ob + j)))
        args += [src, src]
    in_specs += [pl.BlockSpec((K, tc), lambda j, i: (0, j)), pl.BlockSpec((1, tc), lambda j, i: (0, j))]
    args += [w, b]
    n_out = 2 if silu_out else 1
    out = pl.pallas_call(
        body, name=name, grid=(C // tc, T // tt), in_specs=in_specs,
        out_specs=[pl.BlockSpec((tt, tc), lambda j, i: (i, j))] * n_out,
        out_shape=[jax.ShapeDtypeStruct((T, C), BF16)] * n_out,
        scratch_shapes=[pltpu.VMEM((HALO + tt, tc), F32), pltpu.VMEM((max(len(residues), 1), HALO + tt, tc), F32)],
        compiler_params=_cp("parallel", "arbitrary"))(*args)
    return out


def dwconv_bwd(du, u, src, offs, w, *, C, seq, glu, silu_out, name):
    T = src.shape[0]
    K = w.shape[0]
    tt, tc = _conv_tiles(seq, C)
    n_in = 2 if glu else 1
    per = tt // HALO
    last_blk = T // HALO - 1
    g_offsets = [K - 1 - k for k in range(K)]
    x_offsets = [HALO - (K - 1) + k for k in range(K)]
    g_res, x_res = _residues(g_offsets), _residues(x_offsets)

    def body(*refs):
        pos = 0
        du_cur, du_nxt = refs[0], refs[1]
        pos = 2
        if silu_out:
            u_cur, u_nxt = refs[2], refs[3]
            pos = 4
        cur = refs[pos:pos + 2 * n_in:2]
        halo = refs[pos + 1:pos + 2 * n_in:2]
        pos += 2 * n_in
        w_ref = refs[pos]
        outs = refs[pos + 1:pos + 1 + n_in]
        dw_ref, db_ref = refs[pos + 1 + n_in], refs[pos + 2 + n_in]
        gbuf, gshift, xbuf, xshift, dw_acc, db_acc = refs[-6:]
        i = pl.program_id(1)
        first = (i * tt) % seq == 0
        last = ((i + 1) * tt) % seq == 0

        @pl.when(i == 0)
        def _():
            dw_acc[...] = jnp.zeros_like(dw_acc)
            db_acc[...] = jnp.zeros_like(db_acc)

        g_cur = du_cur[...].astype(F32)
        g_nxt = du_nxt[...].astype(F32)
        if silu_out:
            g_cur = g_cur * _silu_grad(u_cur[...].astype(F32))
            g_nxt = g_nxt * _silu_grad(u_nxt[...].astype(F32))
        gbuf[0:tt, :] = g_cur
        gbuf[tt:tt + HALO, :] = jnp.where(last, 0.0, g_nxt)

        def pre(rs):
            v = rs[0][...].astype(F32)
            return v * _sigmoid(rs[1][...].astype(F32)) if glu else v

        xbuf[0:HALO, :] = jnp.where(first, 0.0, pre(halo))
        xbuf[HALO:HALO + tt, :] = pre(cur)
        _fill_shifted(gbuf, gshift, g_res)
        _fill_shifted(xbuf, xshift, x_res)

        def fold(v):
            out = v[0:SUBLANES]
            for s in range(SUBLANES, CONV_CHUNK, SUBLANES):
                out = out + v[s:s + SUBLANES]
            return out

        def chunk(ci, carry):
            start = pl.multiple_of(ci * CONV_CHUNK, CONV_CHUNK)
            g = gbuf[pl.ds(start, CONV_CHUNK), :]
            dx = jnp.zeros((CONV_CHUNK, tc), F32)
            for k in range(K):
                dx = dx + w_ref[k:k + 1, :] * _tap(gbuf, gshift, g_res, g_offsets[k], start, CONV_CHUNK)
                dw_acc[k * SUBLANES:(k + 1) * SUBLANES, :] += fold(
                    g * _tap(xbuf, xshift, x_res, x_offsets[k], start, CONV_CHUNK))
            db_acc[...] += fold(g)
            rows = pl.ds(start, CONV_CHUNK)
            if glu:
                a = cur[0][rows, :].astype(F32)
                s = _sigmoid(cur[1][rows, :].astype(F32))
                outs[0][rows, :] = (dx * s).astype(BF16)
                outs[1][rows, :] = (dx * a * s * (1.0 - s)).astype(BF16)
            else:
                outs[0][rows, :] = dx.astype(BF16)
            return carry

        lax.fori_loop(0, tt // CONV_CHUNK, chunk, 0)

        @pl.when(i == T // tt - 1)
        def _():
            for k in range(K):
                dw_ref[k:k + 1, :] = jnp.sum(dw_acc[k * SUBLANES:(k + 1) * SUBLANES, :], axis=0, keepdims=True)
            db_ref[...] = jnp.sum(db_acc[...], axis=0, keepdims=True)

    def cur_spec(ob):
        return pl.BlockSpec((tt, tc), lambda j, i: (i, ob + j))

    def nxt_spec(ob):
        return pl.BlockSpec((HALO, tc), lambda j, i: (jnp.minimum((i + 1) * per, last_blk), ob + j))

    def prv_spec(ob):
        return pl.BlockSpec((HALO, tc), lambda j, i: (jnp.maximum(i * per - 1, 0), ob + j))

    in_specs = [cur_spec(0), nxt_spec(0)]
    args = [du, du]
    if silu_out:
        in_specs += [cur_spec(0), nxt_spec(0)]
        args += [u, u]
    for off in offs:
        assert off % tc == 0
        in_specs += [cur_spec(off // tc), prv_spec(off // tc)]
        args += [src, src]
    in_specs.append(pl.BlockSpec((K, tc), lambda j, i: (0, j)))
    args.append(w)
    out_specs = [pl.BlockSpec((tt, tc), lambda j, i: (i, j))] * n_in
    out_specs += [pl.BlockSpec((K, tc), lambda j, i: (0, j)), pl.BlockSpec((1, tc), lambda j, i: (0, j))]
    out_shape = [jax.ShapeDtypeStruct((T, C), BF16)] * n_in
    out_shape += [jax.ShapeDtypeStruct((K, C), F32), jax.ShapeDtypeStruct((1, C), F32)]
    return pl.pallas_call(
        body, name=name, grid=(C // tc, T // tt), in_specs=in_specs, out_specs=out_specs, out_shape=out_shape,
        scratch_shapes=[pltpu.VMEM((tt + HALO, tc), F32), pltpu.VMEM((max(len(g_res), 1), tt + HALO, tc), F32),
                        pltpu.VMEM((HALO + tt, tc), F32), pltpu.VMEM((max(len(x_res), 1), HALO + tt, tc), F32),
                        pltpu.VMEM((K * SUBLANES, tc), F32), pltpu.VMEM((SUBLANES, tc), F32)],
        compiler_params=_cp("parallel", "arbitrary"))(*args)


def mix0_post_fwd(u2, proj, o, ln_w, ln_b, *, CW, gc_off, ga_off, name):
    T = u2.shape[0]
    tt = _pick(T, 256, 16)

    def body(u_ref, gc_ref, ga_ref, o_ref, lw_ref, lb_ref, y_ref):
        u = u_ref[...].astype(F32)
        mu = jnp.mean(u, axis=-1, keepdims=True)
        xc = u - mu
        r = lax.rsqrt(jnp.mean(xc * xc, axis=-1, keepdims=True) + EPS)
        u3 = xc * r * lw_ref[...] + lb_ref[...]
        y_ref[:, 0:CW] = (_silu(u3) * _silu(gc_ref[...].astype(F32))).astype(BF16)
        y_ref[:, CW:2 * CW] = (o_ref[...].astype(F32) * _silu(ga_ref[...].astype(F32))).astype(BF16)

    row = pl.BlockSpec((tt, CW), lambda i: (i, 0))
    vec = pl.BlockSpec((1, CW), lambda i: (0, 0))
    return pl.pallas_call(
        body, name=name, grid=(T // tt,),
        in_specs=[row, pl.BlockSpec((tt, CW), lambda i: (i, gc_off // CW)),
                  pl.BlockSpec((tt, CW), lambda i: (i, ga_off // CW)), row, vec, vec],
        out_specs=pl.BlockSpec((tt, 2 * CW), lambda i: (i, 0)),
        out_shape=jax.ShapeDtypeStruct((T, 2 * CW), BF16),
        compiler_params=_cp("parallel"))(u2, proj, proj, o, ln_w, ln_b)


def mix0_post_bwd(dy, u2, proj, o, ln_w, ln_b, *, CW, gc_off, ga_off, name):
    T = u2.shape[0]
    tt = _pick(T, 256, 16)

    def body(dy_ref, u_ref, gc_ref, ga_ref, o_ref, lw_ref, lb_ref, du_ref, dgc_ref, dga_ref, do_ref, dlw_ref, dlb_ref):
        @pl.when(pl.program_id(0) == 0)
        def _():
            dlw_ref[...] = jnp.zeros_like(dlw_ref)
            dlb_ref[...] = jnp.zeros_like(dlb_ref)

        dyc = dy_ref[:, 0:CW].astype(F32)
        dya = dy_ref[:, CW:2 * CW].astype(F32)
        u = u_ref[...].astype(F32)
        mu = jnp.mean(u, axis=-1, keepdims=True)
        xc = u - mu
        r = lax.rsqrt(jnp.mean(xc * xc, axis=-1, keepdims=True) + EPS)
        xhat = xc * r
        u3 = xhat * lw_ref[...] + lb_ref[...]
        gc = gc_ref[...].astype(F32)
        dgc_ref[...] = (dyc * _silu(u3) * _silu_grad(gc)).astype(BF16)
        du3 = dyc * _silu(gc) * _silu_grad(u3)
        dlw_ref[...] += jnp.sum(du3 * xhat, axis=0, keepdims=True)
        dlb_ref[...] += jnp.sum(du3, axis=0, keepdims=True)
        dxh = du3 * lw_ref[...]
        du = r * (dxh - jnp.mean(dxh, axis=-1, keepdims=True) - xhat * jnp.mean(dxh * xhat, axis=-1, keepdims=True))
        du_ref[...] = du.astype(BF16)
        ga = ga_ref[...].astype(F32)
        ov = o_ref[...].astype(F32)
        do_ref[...] = (dya * _silu(ga)).astype(BF16)
        dga_ref[...] = (dya * ov * _silu_grad(ga)).astype(BF16)

    row = pl.BlockSpec((tt, CW), lambda i: (i, 0))
    vec = pl.BlockSpec((1, CW), lambda i: (0, 0))
    big = jax.ShapeDtypeStruct((T, CW), BF16)
    small = jax.ShapeDtypeStruct((1, CW), F32)
    return pl.pallas_call(
        body, name=name, grid=(T // tt,),
        in_specs=[pl.BlockSpec((tt, 2 * CW), lambda i: (i, 0)), row,
                  pl.BlockSpec((tt, CW), lambda i: (i, gc_off // CW)),
                  pl.BlockSpec((tt, CW), lambda i: (i, ga_off // CW)), row, vec, vec],
        out_specs=[row, row, row, row, vec, vec],
        out_shape=[big, big, big, big, small, small],
        compiler_params=_cp("arbitrary"))(dy, u2, proj, proj, o, ln_w, ln_b)


SB_UNDERFLOW = 110.0
SB_BOUND_MARGIN = 1.02


def _sb_tile(seq):
    return _pick(seq, 256, LANE)


def _softplus(z):
    return jnp.maximum(z, 0.0) + jnp.log(1.0 + jnp.exp(-jnp.abs(z)))


def _tri01(n, lower):
    i = lax.broadcasted_iota(jnp.int32, (n, n), 0)
    j = lax.broadcasted_iota(jnp.int32, (n, n), 1)
    return ((i >= j) if lower else (i <= j)).astype(BF16)


def _sb_heads_per_step(heads):
    return 2 if heads % 2 == 0 else 1


def sba_fwd(proj, *, B, seq, heads, q_off, k_off, v_off, name):
    dh = SB_HEAD_DIM
    tq = _sb_tile(seq)
    nq = seq // tq
    hps = _sb_heads_per_step(heads)
    hw = hps * dh
    scale = dh ** -0.5

    def body(q_ref, k_ref, v_ref, o_ref, ct_ref, acc_ref, kmax_ref):
        qi = pl.program_id(1)
        tri = _tri01(tq, True)
        below = lax.broadcasted_iota(jnp.int32, (tq, tq), 1) < lax.broadcasted_iota(jnp.int32, (tq, tq), 0)
        qs = [(q_ref[:, h * dh:(h + 1) * dh].astype(F32) * scale).astype(BF16) for h in range(hps)]

        @pl.when(qi == 0)
        def _():
            def chunk(i, best):
                rows = k_ref[pl.ds(pl.multiple_of(i * tq, tq), tq), :].astype(F32)
                sq = rows * rows
                return tuple(jnp.maximum(best[h], jnp.max(jnp.sum(sq[:, h * dh:(h + 1) * dh], axis=1, keepdims=True),
                                                          axis=0, keepdims=True)) for h in range(hps))

            best = lax.fori_loop(0, nq, chunk, (jnp.zeros((1, 1), F32),) * hps)
            for h in range(hps):
                kmax_ref[h] = jnp.broadcast_to(jnp.sqrt(best[h]), (8, LANE))

        z_bound = [jnp.sqrt(jnp.sum(qs[h].astype(F32) ** 2, axis=1, keepdims=True))
                   * (SB_BOUND_MARGIN * jnp.max(kmax_ref[h], keepdims=True)) for h in range(hps)]

        def block(start, rs, diag):
            pvs, out = [], []
            for h in range(hps):
                k_blk = k_ref[pl.ds(start, tq), h * dh:(h + 1) * dh]
                v_blk = v_ref[pl.ds(start, tq), h * dh:(h + 1) * dh]
                z = _dot(qs[h], k_blk, NT)
                sp = _softplus(z)
                if diag:
                    sp = jnp.where(below, sp, 0.0)
                wts = jnp.exp(z - (_dot(sp.astype(BF16), tri) + rs[h]))
                if diag:
                    wts = jnp.where(below, wts, 0.0)
                pvs.append(_dot(wts.astype(BF16), v_blk))
                out.append(rs[h] + jnp.sum(sp, axis=-1, keepdims=True))
            return pvs, tuple(out)

        zero = jnp.zeros((tq, 1), F32)
        pvs, rs = block(pl.multiple_of(qi * tq, tq), (zero,) * hps, True)
        for h in range(hps):
            acc_ref[:, h * dh:(h + 1) * dh] = pvs[h]

        def more(c):
            j, rs = c
            slack = rs[0] - z_bound[0]
            for h in range(1, hps):
                slack = jnp.minimum(slack, rs[h] - z_bound[h])
            return jnp.logical_and(j < qi, jnp.min(slack) <= SB_UNDERFLOW)

        def step(c):
            j, rs = c
            pvs, rs = block(pl.multiple_of((qi - 1 - j) * tq, tq), rs, False)
            for h in range(hps):
                acc_ref[:, h * dh:(h + 1) * dh] += pvs[h]
            return j + 1, rs

        n_left, totals = lax.while_loop(more, step, (jnp.int32(0), rs))
        o_ref[...] = acc_ref[...].astype(BF16)
        for h in range(hps):
            ct_ref[0, 8 * h:8 * h + 8, :] = jnp.broadcast_to(totals[h], (tq, LANE)).T[0:8, :]
        ct_ref[0, 8 * hps:8 * hps + 8, :] = jnp.full((8, tq), n_left, F32)

    qb, kb, vb = q_off // hw, k_off // hw, v_off // hw
    G = heads // hps
    return pl.pallas_call(
        body, name=name, grid=(B * G, nq),
        in_specs=[pl.BlockSpec((tq, hw), lambda g, i: ((g // G) * nq + i, qb + g % G)),
                  pl.BlockSpec((seq, hw), lambda g, i: (g // G, kb + g % G)),
                  pl.BlockSpec((seq, hw), lambda g, i: (g // G, vb + g % G))],
        out_specs=[pl.BlockSpec((tq, hw), lambda g, i: ((g // G) * nq + i, g % G)),
                   pl.BlockSpec((1, 8 * hps + 8, tq), lambda g, i: (g * nq + i, 0, 0))],
        out_shape=[jax.ShapeDtypeStruct((B * seq, heads * dh), BF16),
                   jax.ShapeDtypeStruct((B * G * nq, 8 * hps + 8, tq), F32)],
        scratch_shapes=[pltpu.VMEM((tq, hw), F32), pltpu.VMEM((hps, 8, LANE), F32)],
        compiler_params=_cp("parallel", "arbitrary"))(proj, proj, proj)


def sba_bwd(proj, ctot, do, *, B, seq, heads, q_off, k_off, v_off, name, comm=None):
    dh = SB_HEAD_DIM
    tq = _sb_tile(seq)
    nq = seq // tq
    hps = _sb_heads_per_step(heads)
    hw = hps * dh
    scale = dh ** -0.5

    def body(q_ref, k_ref, v_ref, ct_ref, do_ref, dq_ref, dk_ref, dv_ref, dq_acc, dk_acc, dv_acc):
        qi = pl.program_id(1)

        @pl.when(qi == 0)
        def _():
            dk_acc[...] = jnp.zeros_like(dk_acc)
            dv_acc[...] = jnp.zeros_like(dv_acc)

        tri_sfx = _tri01(tq, True)
        tri_pre = _tri01(tq, False)
        below = lax.broadcasted_iota(jnp.int32, (tq, tq), 1) < lax.broadcasted_iota(jnp.int32, (tq, tq), 0)
        qs = [(q_ref[:, h * dh:(h + 1) * dh].astype(F32) * scale).astype(BF16) for h in range(hps)]
        dos = [do_ref[:, h * dh:(h + 1) * dh] for h in range(hps)]
        totals = [jnp.max(jnp.broadcast_to(ct_ref[0, 8 * h:8 * h + 1, :], (LANE, tq)).T, axis=1, keepdims=True)
                  for h in range(hps)]
        dq_acc[...] = jnp.zeros_like(dq_acc)

        def block(start, carry, diag):
            out = []
            for h in range(hps):
                pc, pg = carry[h]
                cols = slice(h * dh, (h + 1) * dh)
                k_blk = k_ref[pl.ds(start, tq), cols]
                v_blk = v_ref[pl.ds(start, tq), cols]
                z = _dot(qs[h], k_blk, NT)
                sp = _softplus(z)
                sig = jnp.exp(z - sp)
                if diag:
                    sp = jnp.where(below, sp, 0.0)
                pc_next = pc + jnp.sum(sp, axis=-1, keepdims=True)
                wts = jnp.exp(z - (_dot(sp.astype(BF16), tri_sfx) + (totals[h] - pc_next)))
                if diag:
                    wts = jnp.where(below, wts, 0.0)
                g = _dot(dos[h], v_blk, NT) * wts
                dz = g - sig * (_dot(g.astype(BF16), tri_pre) + pg)
                if diag:
                    dz = jnp.where(below, dz, 0.0)
                dz = dz.astype(BF16)
                dq_acc[:, cols] += _dot(dz, k_blk)
                dk_acc[pl.ds(start, tq), cols] += _dot(dz, qs[h], TN)
                dv_acc[pl.ds(start, tq), cols] += _dot(wts.astype(BF16), dos[h], TN)
                out.append((pc_next, pg + jnp.sum(g, axis=-1, keepdims=True)))
            return tuple(out)

        zero = jnp.zeros((tq, 1), F32)
        n_left = jnp.max(ct_ref[0, 8 * hps:8 * hps + 8, :]).astype(jnp.int32)
        carry = lax.fori_loop(qi - n_left, qi, lambda j, c: block(pl.multiple_of(j * tq, tq), c, False),
                              ((zero, zero),) * hps)
        block(pl.multiple_of(qi * tq, tq), carry, True)
        dq_ref[...] = (dq_acc[...] * scale).astype(BF16)

        @pl.when(qi == nq - 1)
        def _():
            dk_ref[...] = dk_acc[...].astype(BF16)
            dv_ref[...] = dv_acc[...].astype(BF16)

    qb, kb, vb = q_off // hw, k_off // hw, v_off // hw
    G = heads // hps
    q_spec = pl.BlockSpec((tq, hw), lambda g, i: ((g // G) * nq + i, qb + g % G))
    o_spec = pl.BlockSpec((tq, hw), lambda g, i: ((g // G) * nq + i, g % G))
    kv_out = pl.BlockSpec((seq, hw), lambda g, i: (g // G, g % G))
    shp = jax.ShapeDtypeStruct((B * seq, heads * dh), BF16)
    body, c_in, c_args, c_out, c_shape, c_scratch = _hosted(body, 5, 3, comm, _grid_step(nq), B * G * nq)
    out = pl.pallas_call(
        body, name=name, grid=(B * G, nq),
        in_specs=[q_spec,
                  pl.BlockSpec((seq, hw), lambda g, i: (g // G, kb + g % G)),
                  pl.BlockSpec((seq, hw), lambda g, i: (g // G, vb + g % G)),
                  pl.BlockSpec((1, 8 * hps + 8, tq), lambda g, i: (g * nq + i, 0, 0)), o_spec] + c_in,
        out_specs=[o_spec, kv_out, kv_out] + c_out, out_shape=[shp, shp, shp] + c_shape,
        scratch_shapes=[pltpu.VMEM((tq, hw), F32), pltpu.VMEM((seq, hw), F32), pltpu.VMEM((seq, hw), F32)]
        + c_scratch,
        compiler_params=_cp("arbitrary" if comm else "parallel", "arbitrary"))(proj, proj, proj, ctot, do, *c_args)
    return (out[0], out[1], out[2], out[3:]) if comm else out


def _chunk_tri(lower):
    i = lax.broadcasted_iota(jnp.int32, (SSM_CHUNK, SSM_CHUNK), 0)
    j = lax.broadcasted_iota(jnp.int32, (SSM_CHUNK, SSM_CHUNK), 1)
    return ((i >= j) if lower else (i <= j)).astype(BF16)


def ssm_dt_fwd(proj, bias, a_log, *, dt_off, name):
    T = proj.shape[0]
    L = SSM_CHUNK
    tt = _pick(T, 512, L)

    def body(raw_ref, bias_ref, al_ref, dt_ref, cs_ref):
        x = raw_ref[...].astype(F32) + bias_ref[...]
        dt = jnp.maximum(x, 0.0) + jnp.log(1.0 + jnp.exp(-jnp.abs(x)))
        dt_ref[...] = dt
        la = dt * (-jnp.exp(al_ref[...]))
        tri = _chunk_tri(True)
        for c in range(tt // L):
            cs_ref[c * L:(c + 1) * L, :] = _tri_dot3(tri, la[c * L:(c + 1) * L, :])

    row = pl.BlockSpec((tt, LANE), lambda i: (i, 0))
    vec = pl.BlockSpec((1, LANE), lambda i: (0, 0))
    shp = jax.ShapeDtypeStruct((T, LANE), F32)
    return pl.pallas_call(
        body, name=name, grid=(T // tt,),
        in_specs=[pl.BlockSpec((tt, LANE), lambda i: (i, dt_off // LANE)), vec, vec],
        out_specs=[row, row], out_shape=[shp, shp], compiler_params=_cp("parallel"))(proj, bias, a_log)


def ssm_dt_bwd(ddt, dcs, proj, dt, bias, a_log, *, dt_off, n_heads, name):
    T = proj.shape[0]
    L = SSM_CHUNK
    tt = _pick(T, 512, L)

    def body(ddt_ref, dcs_ref, raw_ref, dt_ref, bias_ref, al_ref, draw_ref, dbias_ref, dal_ref, dla_buf):
        @pl.when(pl.program_id(0) == 0)
        def _():
            dbias_ref[...] = jnp.zeros_like(dbias_ref)
            dal_ref[...] = jnp.zeros_like(dal_ref)

        triu = _chunk_tri(False)
        dcs = dcs_ref[...]
        for c in range(tt // L):
            dla_buf[c * L:(c + 1) * L, :] = _tri_dot3(triu, dcs[c * L:(c + 1) * L, :])
        dla = dla_buf[...]
        a = -jnp.exp(al_ref[...])
        dtv = dt_ref[...]
        valid = lax.broadcasted_iota(jnp.int32, (tt, LANE), 1) < n_heads
        dal_ref[...] += jnp.sum(jnp.where(valid, dla * dtv, 0.0), axis=0, keepdims=True) * a
        x = raw_ref[...].astype(F32) + bias_ref[...]
        draw = jnp.where(valid, (ddt_ref[...] + dla * a) * _sigmoid(x), 0.0)
        dbias_ref[...] += jnp.sum(draw, axis=0, keepdims=True)
        draw_ref[...] = draw.astype(BF16)

    row = pl.BlockSpec((tt, LANE), lambda i: (i, 0))
    vec = pl.BlockSpec((1, LANE), lambda i: (0, 0))
    return pl.pallas_call(
        body, name=name, grid=(T // tt,),
        in_specs=[row, row, pl.BlockSpec((tt, LANE), lambda i: (i, dt_off // LANE)), row, vec, vec],
        out_specs=[row, vec, vec],
        out_shape=[jax.ShapeDtypeStruct((T, LANE), BF16), jax.ShapeDtypeStruct((1, LANE), F32),
                   jax.ShapeDtypeStruct((1, LANE), F32)],
        scratch_shapes=[pltpu.VMEM((tt, LANE), F32)],
        compiler_params=_cp("arbitrary"))(ddt, dcs, proj, dt, bias, a_log)


def _colb(row):
    return jnp.broadcast_to(row, (LANE, SSM_CHUNK)).T


def _ssd_pair_common(x_ref, dt_ref, cs_ref, pair, ppg, lo_half, causal, lane_row):
    L = SSM_CHUNK
    g, pp = divmod(pair, ppg)
    ra = g * HEAD_ROWS + 2 * pp
    X = x_ref[:, pair * LANE:(pair + 1) * LANE].astype(F32)
    dta, dtb = dt_ref[0, ra:ra + 1, :], dt_ref[0, ra + 1:ra + 2, :]
    csa, csb = cs_ref[0, ra:ra + 1, :], cs_ref[0, ra + 1:ra + 2, :]
    csa_c, csb_c = _colb(csa), _colb(csb)
    dt_p = jnp.where(lo_half, _colb(dta), _colb(dtb))
    La = jnp.exp(jnp.where(causal, csa_c - csa, NEG_BIG))
    Lb = jnp.exp(jnp.where(causal, csb_c - csb, NEG_BIG))
    last_a = jnp.sum(jnp.where(lane_row == L - 1, csa, 0.0), axis=1, keepdims=True)
    last_b = jnp.sum(jnp.where(lane_row == L - 1, csb, 0.0), axis=1, keepdims=True)
    ecs = jnp.exp(jnp.where(lo_half, csa_c, csb_c))
    tail = jnp.exp(jnp.where(lo_half, last_a - csa_c, last_b - csb_c))
    return g, ra, X, dt_p, La, Lb, last_a, last_b, ecs, tail


def ssd_fwd(xbc, dt_row, cs_row, d_full, *, B, seq, DI, name):
    L, N, G = SSM_CHUNK, SSM_STATE, SSM_GROUPS
    nc = seq // L
    XW = xbc.shape[1]
    n_pairs = DI // LANE
    ppg = n_pairs // G

    def body(x_ref, dt_ref, cs_ref, d_ref, y_ref, st_ref, state):
        c = pl.program_id(1)

        @pl.when(c == 0)
        def _():
            state[...] = jnp.zeros_like(state)

        causal = lax.broadcasted_iota(jnp.int32, (L, L), 0) >= lax.broadcasted_iota(jnp.int32, (L, L), 1)
        lo_half = lax.broadcasted_iota(jnp.int32, (L, LANE), 1) < SSM_HEAD_DIM
        rows_lo = lax.broadcasted_iota(jnp.int32, (LANE, N), 0) < SSM_HEAD_DIM
        lane_row = lax.broadcasted_iota(jnp.int32, (1, L), 1)
        cbs = []
        for g in range(G):
            Bc = x_ref[:, DI + g * N:DI + (g + 1) * N]
            Cc = x_ref[:, DI + G * N + g * N:DI + G * N + (g + 1) * N]
            cbs.append((Bc, Cc, _dot(Cc, Bc, NT)))
        for pair in range(n_pairs):
            g, ra, X, dt_p, La, Lb, last_a, last_b, ecs, tail = _ssd_pair_common(
                x_ref, dt_ref, cs_ref, pair, ppg, lo_half, causal, lane_row)
            Bc, Cc, CB = cbs[g]
            xs = X * dt_p
            xsb = xs.astype(BF16)
            y = jnp.where(lo_half, _dot((CB * La).astype(BF16), xsb), _dot((CB * Lb).astype(BF16), xsb))
            S = state[pair]
            st_ref[0, 0, pair] = S
            y = y + ecs * _dot(Cc, S.astype(BF16), NT)
            y = y + d_ref[:, pair * LANE:(pair + 1) * LANE] * X
            y_ref[:, pair * LANE:(pair + 1) * LANE] = y.astype(BF16)
            e_rows = jnp.where(rows_lo, jnp.exp(last_a), jnp.exp(last_b))
            state[pair] = e_rows * S + _dot((xs * tail).astype(BF16), Bc, TN)

    return pl.pallas_call(
        body, name=name, grid=(B, nc),
        in_specs=[pl.BlockSpec((L, XW), lambda b, c: (b * nc + c, 0)),
                  pl.BlockSpec((1, G * HEAD_ROWS, L), lambda b, c: (b, 0, c)),
                  pl.BlockSpec((1, G * HEAD_ROWS, L), lambda b, c: (b, 0, c)),
                  pl.BlockSpec((1, DI), lambda b, c: (0, 0))],
        out_specs=[pl.BlockSpec((L, DI), lambda b, c: (b * nc + c, 0)),
                   pl.BlockSpec((1, 1, n_pairs, LANE, N), lambda b, c: (b, c, 0, 0, 0))],
        out_shape=[jax.ShapeDtypeStruct((B * seq, DI), BF16),
                   jax.ShapeDtypeStruct((B, nc, n_pairs, LANE, N), F32)],
        scratch_shapes=[pltpu.VMEM((n_pairs, LANE, N), F32)],
        compiler_params=_cp("parallel", "arbitrary"))(xbc, dt_row, cs_row, d_full)


def ssd_bwd(xbc, dt_row, cs_row, d_full, states, dy, *, B, seq, DI, name):
    L, N, G = SSM_CHUNK, SSM_STATE, SSM_GROUPS
    nc = seq // L
    XW = xbc.shape[1]
    n_pairs = DI // LANE
    ppg = n_pairs // G
    HR = G * HEAD_ROWS

    def body(x_ref, dt_ref, cs_ref, d_ref, st_ref, dy_ref, dx_ref, ddt_ref, dcs_ref, dd_ref, dH):
        c = pl.program_id(1)

        @pl.when(c == 0)
        def _():
            dH[...] = jnp.zeros_like(dH)
            dd_ref[...] = jnp.zeros_like(dd_ref)

        causal = lax.broadcasted_iota(jnp.int32, (L, L), 0) >= lax.broadcasted_iota(jnp.int32, (L, L), 1)
        lo_half = lax.broadcasted_iota(jnp.int32, (L, LANE), 1) < SSM_HEAD_DIM
        rows_lo = lax.broadcasted_iota(jnp.int32, (LANE, N), 0) < SSM_HEAD_DIM
        lane_row = lax.broadcasted_iota(jnp.int32, (1, L), 1)
        head_row = lax.broadcasted_iota(jnp.int32, (HR, 1), 0)
        ddt_all = jnp.zeros((HR, L), F32)
        dcs_all = jnp.zeros((HR, L), F32)

        def place(row, r):
            return jnp.where(head_row == r, row, 0.0)

        def as_rows(col):
            return jnp.broadcast_to(col, (L, LANE)).T[0:HR, :]

        def head_sums(t, ra):
            sa = as_rows(jnp.sum(jnp.where(lo_half, t, 0.0), axis=1, keepdims=True))
            sb = as_rows(jnp.sum(jnp.where(lo_half, 0.0, t), axis=1, keepdims=True))
            return place(sa, ra) + place(sb, ra + 1)

        for g in range(G):
            Bc = x_ref[:, DI + g * N:DI + (g + 1) * N]
            Cc = x_ref[:, DI + G * N + g * N:DI + G * N + (g + 1) * N]
            CB = _dot(Cc, Bc, NT)
            dCB = jnp.zeros((L, L), F32)
            dC = jnp.zeros((L, N), F32)
            dB = jnp.zeros((L, N), F32)
            for pp in range(ppg):
                pair = g * ppg + pp
                _, ra, X, dt_p, La, Lb, last_a, last_b, ecs, tail = _ssd_pair_common(
                    x_ref, dt_ref, cs_ref, pair, ppg, lo_half, causal, lane_row)
                rb = ra + 1
                xs = X * dt_p
                xsb = xs.astype(BF16)
                Ma, Mb = CB * La, CB * Lb
                dY = dy_ref[:, pair * LANE:(pair + 1) * LANE].astype(F32)
                dYb = dY.astype(BF16)
                dMa = _dot(jnp.where(lo_half, dY, 0.0).astype(BF16), xsb, NT)
                dMb = _dot(jnp.where(lo_half, 0.0, dY).astype(BF16), xsb, NT)
                dSa, dSb = dMa * Ma, dMb * Mb
                dCB = dCB + dMa * La + dMb * Lb
                dcs_all = dcs_all + place(as_rows(jnp.sum(dSa, axis=1, keepdims=True)) - jnp.sum(dSa, axis=0, keepdims=True), ra)
                dcs_all = dcs_all + place(as_rows(jnp.sum(dSb, axis=1, keepdims=True)) - jnp.sum(dSb, axis=0, keepdims=True), rb)
                dxs = jnp.where(lo_half, _dot(Ma.astype(BF16), dYb, TN), _dot(Mb.astype(BF16), dYb, TN))
                S = st_ref[0, 0, pair]
                Sb = S.astype(BF16)
                y_inter = ecs * _dot(Cc, Sb, NT)
                dYe = (dY * ecs).astype(BF16)
                dC = dC + _dot(dYe, Sb)
                dHp = _dot(dYe, Cc, TN)
                dcs_all = dcs_all + head_sums(dY * y_inter, ra)
                dHn = dH[pair]
                dHnb = dHn.astype(BF16)
                ea, eb = jnp.exp(last_a), jnp.exp(last_b)
                dHp = dHp + jnp.where(rows_lo, ea, eb) * dHn
                prod = dHn * S
                dlast_a = ea * jnp.sum(jnp.where(rows_lo, prod, 0.0), keepdims=True)
                dlast_b = eb * jnp.sum(jnp.where(rows_lo, 0.0, prod), keepdims=True)
                XBt = _dot(Bc, dHnb, NT)
                dxs = dxs + tail * XBt
                t2 = xs * XBt * tail
                dlast_a = dlast_a + jnp.sum(jnp.where(lo_half, t2, 0.0), keepdims=True)
                dlast_b = dlast_b + jnp.sum(jnp.where(lo_half, 0.0, t2), keepdims=True)
                dcs_all = dcs_all - head_sums(t2, ra)
                dcs_all = dcs_all + place(jnp.where(lane_row == L - 1, dlast_a, 0.0), ra)
                dcs_all = dcs_all + place(jnp.where(lane_row == L - 1, dlast_b, 0.0), rb)
                dB = dB + _dot((xs * tail).astype(BF16), dHnb)
                dfull = d_ref[:, pair * LANE:(pair + 1) * LANE]
                dx_ref[:, pair * LANE:(pair + 1) * LANE] = (dxs * dt_p + dfull * dY).astype(BF16)
                ddt_all = ddt_all + head_sums(dxs * X, ra)
                dd_ref[0, :, pair * LANE:(pair + 1) * LANE] += jnp.sum(dY * X, axis=0, keepdims=True)
                dH[pair] = dHp
            dCBb = dCB.astype(BF16)
            dx_ref[:, DI + g * N:DI + (g + 1) * N] = (dB + _dot(dCBb, Cc, TN)).astype(BF16)
            dx_ref[:, DI + G * N + g * N:DI + G * N + (g + 1) * N] = (dC + _dot(dCBb, Bc)).astype(BF16)
        ddt_ref[0] = ddt_all
        dcs_ref[0] = dcs_all

    rev = lambda b, c: (b * nc + (nc - 1 - c), 0)
    hrow = pl.BlockSpec((1, HR, L), lambda b, c: (b, 0, nc - 1 - c))
    return pl.pallas_call(
        body, name=name, grid=(B, nc),
        in_specs=[pl.BlockSpec((L, XW), rev), hrow, hrow,
                  pl.BlockSpec((1, DI), lambda b, c: (0, 0)),
                  pl.BlockSpec((1, 1, n_pairs, LANE, N), lambda b, c: (b, nc - 1 - c, 0, 0, 0)),
                  pl.BlockSpec((L, DI), rev)],
        out_specs=[pl.BlockSpec((L, XW), rev), hrow, hrow, pl.BlockSpec((1, 1, DI), lambda b, c: (b, 0, 0))],
        out_shape=[jax.ShapeDtypeStruct((B * seq, XW), BF16),
                   jax.ShapeDtypeStruct((B, HR, seq), F32), jax.ShapeDtypeStruct((B, HR, seq), F32),
                   jax.ShapeDtypeStruct((B, 1, DI), F32)],
        scratch_shapes=[pltpu.VMEM((n_pairs, LANE, N), F32)],
        compiler_params=_cp("parallel", "arbitrary"))(xbc, dt_row, cs_row, d_full, states, dy)


def _head_expand(n_heads, DI):
    j = jnp.arange(LANE, dtype=jnp.int32)[:, None]
    c = jnp.arange(DI, dtype=jnp.int32)[None, :] // SSM_HEAD_DIM
    return ((j == c) & (j < n_heads)).astype(BF16)


def _split3(x):
    hi = x.astype(BF16)
    r1 = x - hi.astype(F32)
    mid = r1.astype(BF16)
    return hi, mid, (r1 - mid.astype(F32)).astype(BF16)


def dt_fwd(proj, bias, a_log, expand, *, dt_off, name):
    T = proj.shape[0]
    DI = expand.shape[1]
    L = SSM_CHUNK
    tt = _pick(T, 512, L)

    def body(raw_ref, bias_ref, al_ref, e_ref, dt_ref, cs_ref, dtx_ref, csx_ref):
        x = raw_ref[...].astype(F32) + bias_ref[...]
        dt = _softplus(x)
        dt_ref[...] = dt
        la = dt * (-jnp.exp(al_ref[...]))
        tri = _tri01(L, True)
        for c in range(tt // L):
            cs_ref[c * L:(c + 1) * L, :] = _tri_dot3(tri, la[c * L:(c + 1) * L, :])
        e = e_ref[...]
        dtx_ref[...] = _dot(dt.astype(BF16), e).astype(BF16)
        hi, mid, lo = _split3(cs_ref[...])
        csx_ref[...] = _dot(hi, e) + _dot(mid, e) + _dot(lo, e)

    row = pl.BlockSpec((tt, LANE), lambda i: (i, 0))
    wide = pl.BlockSpec((tt, DI), lambda i: (i, 0))
    vec = pl.BlockSpec((1, LANE), lambda i: (0, 0))
    return pl.pallas_call(
        body, name=name, grid=(T // tt,),
        in_specs=[pl.BlockSpec((tt, LANE), lambda i: (i, dt_off // LANE)), vec, vec,
                  pl.BlockSpec((LANE, DI), lambda i: (0, 0))],
        out_specs=[row, row, wide, wide],
        out_shape=[jax.ShapeDtypeStruct((T, LANE), F32), jax.ShapeDtypeStruct((T, LANE), F32),
                   jax.ShapeDtypeStruct((T, DI), BF16), jax.ShapeDtypeStruct((T, DI), F32)],
        compiler_params=_cp("parallel"))(proj, bias, a_log, expand)


def dt_bwd(ddt_x, dcs_x, dcs_cols, proj, dt, bias, a_log, reduce_t, *, dt_off, n_heads, name):
    T = proj.shape[0]
    DI = reduce_t.shape[0]
    L = SSM_CHUNK
    tt = _pick(T, 512, L)

    def body(ddtx_ref, dcsx_ref, dcsc_ref, raw_ref, dt_ref, bias_ref, al_ref, r_ref, draw_ref, dbias_ref, dal_ref,
             dla_buf):
        @pl.when(pl.program_id(0) == 0)
        def _():
            dbias_ref[...] = jnp.zeros_like(dbias_ref)
            dal_ref[...] = jnp.zeros_like(dal_ref)

        r = r_ref[...]
        ddt = _dot(ddtx_ref[...], r)
        dx = dcsx_ref[...]
        hi = dx.astype(BF16)
        dcs = _dot(hi, r) + _dot((dx - hi.astype(F32)).astype(BF16), r) + dcsc_ref[...]
        triu = _tri01(L, False)
        for c in range(tt // L):
            dla_buf[c * L:(c + 1) * L, :] = _tri_dot3(triu, dcs[c * L:(c + 1) * L, :])
        dla = dla_buf[...]
        a = -jnp.exp(al_ref[...])
        valid = lax.broadcasted_iota(jnp.int32, (tt, LANE), 1) < n_heads
        dal_ref[...] += jnp.sum(jnp.where(valid, dla * dt_ref[...], 0.0), axis=0, keepdims=True) * a
        x = raw_ref[...].astype(F32) + bias_ref[...]
        draw = jnp.where(valid, (ddt + dla * a) * _sigmoid(x), 0.0)
        dbias_ref[...] += jnp.sum(draw, axis=0, keepdims=True)
        draw_ref[...] = draw.astype(BF16)

    row = pl.BlockSpec((tt, LANE), lambda i: (i, 0))
    wide = pl.BlockSpec((tt, DI), lambda i: (i, 0))
    vec = pl.BlockSpec((1, LANE), lambda i: (0, 0))
    return pl.pallas_call(
        body, name=name, grid=(T // tt,),
        in_specs=[wide, wide, row, pl.BlockSpec((tt, LANE), lambda i: (i, dt_off // LANE)), row, vec, vec,
                  pl.BlockSpec((DI, LANE), lambda i: (0, 0))],
        out_specs=[row, vec, vec],
        out_shape=[jax.ShapeDtypeStruct((T, LANE), BF16), jax.ShapeDtypeStruct((1, LANE), F32),
                   jax.ShapeDtypeStruct((1, LANE), F32)],
        scratch_shapes=[pltpu.VMEM((tt, LANE), F32)],
        compiler_params=_cp("arbitrary"))(ddt_x, dcs_x, dcs_cols, proj, dt, bias, a_log, reduce_t)


def _pair_terms(x_ref, dtx_ref, csx_ref, csr_ref, pair, ppg, lo_half, causal):
    L = SSM_CHUNK
    g, pp = divmod(pair, ppg)
    ra = g * HEAD_ROWS + 2 * pp
    cols = slice(pair * LANE, (pair + 1) * LANE)
    X = x_ref[:, cols].astype(F32)
    dt_p = dtx_ref[:, cols].astype(F32)
    own = csx_ref[:, cols]
    other = pltpu.roll(own, SSM_HEAD_DIM, 1)
    csa_c = jnp.where(lo_half, own, other)
    csb_c = jnp.where(lo_half, other, own)
    La = jnp.exp(jnp.where(causal, csa_c - csr_ref[0, ra:ra + 1, :], NEG_BIG))
    Lb = jnp.exp(jnp.where(causal, csb_c - csr_ref[0, ra + 1:ra + 2, :], NEG_BIG))
    last = csx_ref[L - 1:L, cols]
    return g, ra, cols, X, dt_p, La, Lb, jnp.exp(own), jnp.exp(last - own), jnp.exp(last)


def scan_fwd(xbc, dt_x, cs_x, cs_row, d_full, *, B, seq, DI, name):
    L, N, G = SSM_CHUNK, SSM_STATE, SSM_GROUPS
    nc = seq // L
    XW = xbc.shape[1]
    n_pairs = DI // LANE
    ppg = n_pairs // G

    def body(x_ref, dtx_ref, csx_ref, csr_ref, d_ref, y_ref, st_ref, state):
        c = pl.program_id(1)

        @pl.when(c == 0)
        def _():
            state[...] = jnp.zeros_like(state)

        causal = lax.broadcasted_iota(jnp.int32, (L, L), 0) >= lax.broadcasted_iota(jnp.int32, (L, L), 1)
        lo_half = lax.broadcasted_iota(jnp.int32, (L, LANE), 1) < SSM_HEAD_DIM
        cbs = []
        for g in range(G):
            Bc = x_ref[:, DI + g * N:DI + (g + 1) * N]
            Cc = x_ref[:, DI + G * N + g * N:DI + G * N + (g + 1) * N]
            cbs.append((Bc, Cc, _dot(Cc, Bc, NT)))
        for pair in range(n_pairs):
            g, _, cols, X, dt_p, La, Lb, ecs, tail, e_last = _pair_terms(
                x_ref, dtx_ref, csx_ref, csr_ref, pair, ppg, lo_half, causal)
            Bc, Cc, CB = cbs[g]
            xs = X * dt_p
            xsb = xs.astype(BF16)
            y = jnp.where(lo_half, _dot((CB * La).astype(BF16), xsb), _dot((CB * Lb).astype(BF16), xsb))
            ST = state[pair]
            st_ref[0, 0, pair] = ST
            y = y + ecs * _dot(Cc, ST.astype(BF16)) + d_ref[:, cols] * X
            y_ref[:, cols] = y.astype(BF16)
            state[pair] = e_last * ST + _dot(Bc, (xs * tail).astype(BF16), TN)

    wide = pl.BlockSpec((L, DI), lambda b, c: (b * nc + c, 0))
    return pl.pallas_call(
        body, name=name, grid=(B, nc),
        in_specs=[pl.BlockSpec((L, XW), lambda b, c: (b * nc + c, 0)), wide, wide,
                  pl.BlockSpec((1, G * HEAD_ROWS, L), lambda b, c: (b, 0, c)),
                  pl.BlockSpec((1, DI), lambda b, c: (0, 0))],
        out_specs=[wide, pl.BlockSpec((1, 1, n_pairs, N, LANE), lambda b, c: (b, c, 0, 0, 0))],
        out_shape=[jax.ShapeDtypeStruct((B * seq, DI), BF16),
                   jax.ShapeDtypeStruct((B, nc, n_pairs, N, LANE), F32)],
        scratch_shapes=[pltpu.VMEM((n_pairs, N, LANE), F32)],
        compiler_params=_cp("parallel", "arbitrary"))(xbc, dt_x, cs_x, cs_row, d_full)


def scan_bwd(xbc, dt_x, cs_x, cs_row, d_full, states, dy, *, B, seq, DI, name):
    L, N, G = SSM_CHUNK, SSM_STATE, SSM_GROUPS
    nc = seq // L
    XW = xbc.shape[1]
    n_pairs = DI // LANE
    ppg = n_pairs // G
    HR = G * HEAD_ROWS
    inv_p = 1.0 / SSM_HEAD_DIM

    def body(x_ref, dtx_ref, csx_ref, csr_ref, d_ref, st_ref, dy_ref, dx_ref, ddtx_ref, dcsx_ref, dcsr_ref, dd_ref,
             dH):
        c = pl.program_id(1)

        @pl.when(c == 0)
        def _():
            dH[...] = jnp.zeros_like(dH)
            dd_ref[...] = jnp.zeros_like(dd_ref)

        causal = lax.broadcasted_iota(jnp.int32, (L, L), 0) >= lax.broadcasted_iota(jnp.int32, (L, L), 1)
        lo_half = lax.broadcasted_iota(jnp.int32, (L, LANE), 1) < SSM_HEAD_DIM
        last_row = lax.broadcasted_iota(jnp.int32, (L, LANE), 0) == L - 1
        head_row = lax.broadcasted_iota(jnp.int32, (HR, 1), 0)
        dcs_rows = jnp.zeros((HR, L), F32)

        for g in range(G):
            Bc = x_ref[:, DI + g * N:DI + (g + 1) * N]
            Cc = x_ref[:, DI + G * N + g * N:DI + G * N + (g + 1) * N]
            CB = _dot(Cc, Bc, NT)
            dCB = jnp.zeros((L, L), F32)
            dC = jnp.zeros((L, N), F32)
            dB = jnp.zeros((L, N), F32)
            for pp in range(ppg):
                pair = g * ppg + pp
                _, ra, cols, X, dt_p, La, Lb, ecs, tail, e_last = _pair_terms(
                    x_ref, dtx_ref, csx_ref, csr_ref, pair, ppg, lo_half, causal)
                xs = X * dt_p
                xsb = xs.astype(BF16)
                Ma, Mb = CB * La, CB * Lb
                dY = dy_ref[:, cols].astype(F32)
                dYb = dY.astype(BF16)
                dMa = _dot(jnp.where(lo_half, dY, 0.0).astype(BF16), xsb, NT)
                dMb = _dot(jnp.where(lo_half, 0.0, dY).astype(BF16), xsb, NT)
                dSa, dSb = dMa * Ma, dMb * Mb
                dCB = dCB + dMa * La + dMb * Lb
                dcs = jnp.where(lo_half, jnp.sum(dSa, axis=1, keepdims=True), jnp.sum(dSb, axis=1, keepdims=True)) * inv_p
                dcs_rows = dcs_rows - jnp.where(head_row == ra, jnp.sum(dSa, axis=0, keepdims=True), 0.0)
                dcs_rows = dcs_rows - jnp.where(head_row == ra + 1, jnp.sum(dSb, axis=0, keepdims=True), 0.0)
                dxs = jnp.where(lo_half, _dot(Ma.astype(BF16), dYb, TN), _dot(Mb.astype(BF16), dYb, TN))
                ST = st_ref[0, 0, pair]
                STb = ST.astype(BF16)
                dYe = (dY * ecs).astype(BF16)
                dC = dC + _dot(dYe, STb, NT)
                dSTp = _dot(Cc, dYe, TN)
                dcs = dcs + dY * (ecs * _dot(Cc, STb))
                dSTn = dH[pair]
                dSTnb = dSTn.astype(BF16)
                dSTp = dSTp + e_last * dSTn
                XBt = _dot(Bc, dSTnb)
                dxs = dxs + tail * XBt
                t2 = xs * XBt * tail
                at_end = e_last * jnp.sum(dSTn * ST, axis=0, keepdims=True) + jnp.sum(t2, axis=0, keepdims=True)
                dcs = dcs - t2 + jnp.where(last_row, at_end, 0.0)
                dB = dB + _dot((xs * tail).astype(BF16), dSTnb, NT)
                dx_ref[:, cols] = (dxs * dt_p + d_ref[:, cols] * dY).astype(BF16)
                ddtx_ref[:, cols] = (dxs * X).astype(BF16)
                dcsx_ref[:, cols] = dcs
                dd_ref[0, :, cols] += jnp.sum(dY * X, axis=0, keepdims=True)
                dH[pair] = dSTp
            dCBb = dCB.astype(BF16)
            dx_ref[:, DI + g * N:DI + (g + 1) * N] = (dB + _dot(dCBb, Cc, TN)).astype(BF16)
            dx_ref[:, DI + G * N + g * N:DI + G * N + (g + 1) * N] = (dC + _dot(dCBb, Bc)).astype(BF16)
        dcsr_ref[0] = dcs_rows

    rev = lambda b, c: (b * nc + (nc - 1 - c), 0)
    wide = pl.BlockSpec((L, DI), rev)
    hrow = pl.BlockSpec((1, HR, L), lambda b, c: (b, 0, nc - 1 - c))
    return pl.pallas_call(
        body, name=name, grid=(B, nc),
        in_specs=[pl.BlockSpec((L, XW), rev), wide, wide, hrow,
                  pl.BlockSpec((1, DI), lambda b, c: (0, 0)),
                  pl.BlockSpec((1, 1, n_pairs, N, LANE), lambda b, c: (b, nc - 1 - c, 0, 0, 0)), wide],
        out_specs=[pl.BlockSpec((L, XW), rev), wide, wide, hrow, pl.BlockSpec((1, 1, DI), lambda b, c: (b, 0, 0))],
        out_shape=[jax.ShapeDtypeStruct((B * seq, XW), BF16), jax.ShapeDtypeStruct((B * seq, DI), BF16),
                   jax.ShapeDtypeStruct((B * seq, DI), F32), jax.ShapeDtypeStruct((B, HR, seq), F32),
                   jax.ShapeDtypeStruct((B, 1, DI), F32)],
        scratch_shapes=[pltpu.VMEM((n_pairs, N, LANE), F32)],
        compiler_params=_cp("parallel", "arbitrary"))(xbc, dt_x, cs_x, cs_row, d_full, states, dy)


def gnorm_fwd(y, proj, w, *, DI, name):
    T = y.shape[0]
    tt = _pick(T, 256, 16)
    gw = DI // SSM_GROUPS

    def body(y_ref, z_ref, w_ref, o_ref):
        for g in range(SSM_GROUPS):
            sl = slice(g * gw, (g + 1) * gw)
            y2 = y_ref[:, sl].astype(F32) * _silu(z_ref[:, sl].astype(F32))
            r = lax.rsqrt(jnp.mean(y2 * y2, axis=-1, keepdims=True) + EPS)
            o_ref[:, sl] = (y2 * r * w_ref[:, sl]).astype(BF16)

    row = pl.BlockSpec((tt, DI), lambda i: (i, 0))
    return pl.pallas_call(
        body, name=name, grid=(T // tt,),
        in_specs=[row, row, pl.BlockSpec((1, DI), lambda i: (0, 0))], out_specs=row,
        out_shape=jax.ShapeDtypeStruct((T, DI), BF16), compiler_params=_cp("parallel"))(y, proj, w)


def gnorm_bwd(dyn, y, proj, w, *, DI, name):
    T = y.shape[0]
    tt = _pick(T, 256, 16)
    gw = DI // SSM_GROUPS

    def body(dyn_ref, y_ref, z_ref, w_ref, dy_ref, dz_ref, dw_ref):
        @pl.when(pl.program_id(0) == 0)
        def _():
            dw_ref[...] = jnp.zeros_like(dw_ref)

        for g in range(SSM_GROUPS):
            sl = slice(g * gw, (g + 1) * gw)
            yv = y_ref[:, sl].astype(F32)
            z = z_ref[:, sl].astype(F32)
            sz = _silu(z)
            y2 = yv * sz
            r = lax.rsqrt(jnp.mean(y2 * y2, axis=-1, keepdims=True) + EPS)
            xhat = y2 * r
            d = dyn_ref[:, sl].astype(F32)
            dw_ref[:, sl] += jnp.sum(d * xhat, axis=0, keepdims=True)
            dxh = d * w_ref[:, sl]
            dy2 = r * (dxh - xhat * jnp.mean(dxh * xhat, axis=-1, keepdims=True))
            dy_ref[:, sl] = (dy2 * sz).astype(BF16)
            dz_ref[:, sl] = (dy2 * yv * _silu_grad(z)).astype(BF16)

    row = pl.BlockSpec((tt, DI), lambda i: (i, 0))
    vec = pl.BlockSpec((1, DI), lambda i: (0, 0))
    shp = jax.ShapeDtypeStruct((T, DI), BF16)
    return pl.pallas_call(
        body, name=name, grid=(T // tt,), in_specs=[row, row, row, vec], out_specs=[row, row, vec],
        out_shape=[shp, shp, jax.ShapeDtypeStruct((1, DI), F32)],
        compiler_params=_cp("arbitrary"))(dyn, y, proj, w)


N_CHIP = 4


def _comm_out_shapes(srcs, modes):
    return [jax.ShapeDtypeStruct(((N_DEV,) if mode in ("gather", "gather_direct") else ()) + s.shape, s.dtype)
            for s, mode in zip(srcs, modes)]


def _comm_scratch(n):
    return [pltpu.SemaphoreType.DMA((n, N_DEV - 1)), pltpu.SemaphoreType.DMA((n, N_DEV - 1)),
            pltpu.SemaphoreType.DMA((n,))]


def _comm_phases(modes, src_refs, out_refs, send_sems, recv_sems, local_sems):
    x, y, c = lax.axis_index("x"), lax.axis_index("y"), lax.axis_index("c")
    me, sibling = (x, y, c), (x, y, 1 - c)
    chips = [(1 - x, y), (x, 1 - y), (1 - x, 1 - y)]
    relays = [a for a, mode in enumerate(modes) if mode == "gather"]

    def slot(p):
        return 4 * p[0] + 2 * p[1] + p[2]

    def remote(a, k, src, dst, to):
        return pltpu.make_async_remote_copy(src_ref=src, dst_ref=dst, send_sem=send_sems.at[a, k],
                                            recv_sem=recv_sems.at[a, k], device_id=to,
                                            device_id_type=pl.DeviceIdType.MESH)

    def first_copies():
        local, two_way, send_only = [], [], []
        for a, mode in enumerate(modes):
            src, out = src_refs[a], out_refs[a]
            if mode == "sibling":
                two_way.append(remote(a, 0, src, out, sibling))
            elif mode == "chips":
                mine = 2 * x + y
                local.append(pltpu.make_async_copy(src.at[mine], out.at[mine], local_sems.at[a]))
                for j, chip in enumerate(chips):
                    two_way.append(remote(a, 1 + j, src.at[2 * chip[0] + chip[1]], out.at[mine], (*chip, c)))
            elif mode == "gather_direct":
                local.append(pltpu.make_async_copy(src, out.at[slot(me)], local_sems.at[a]))
                for k in range(1, N_DEV):
                    peer = (1 - x if k & 4 else x, 1 - y if k & 2 else y, 1 - c if k & 1 else c)
                    two_way.append(remote(a, k - 1, src, out.at[slot(me)], peer))
            else:
                assert mode == "gather"
                local.append(pltpu.make_async_copy(src, out.at[slot(me)], local_sems.at[a]))
                send_only.append(remote(a, 0, src, out.at[slot(me)], sibling))
                for j, chip in enumerate(chips):
                    send_only.append(remote(a, 1 + j, src, out.at[slot(me)], (*chip, c)))
        return local, two_way, send_only

    def forwards():
        out = []
        for a in relays:
            for j, chip in enumerate(chips):
                landed = out_refs[a].at[slot((*chip, c))]
                out.append((remote(a, 1 + j, landed, landed, me), remote(a, 4 + j, landed, landed, sibling)))
        return out

    def start():
        local, two_way, send_only = first_copies()
        for cp in local + two_way + send_only:
            cp.start()

    def relay():
        for arrival, fwd in forwards():
            arrival.wait_recv()
            fwd.start()

    def finish():
        local, two_way, send_only = first_copies()
        for a in relays:
            blk = out_refs[a].at[slot(sibling)]
            remote(a, 0, blk, blk, me).wait_recv()
            for j, chip in enumerate(chips):
                blk = out_refs[a].at[slot((*chip, 1 - c))]
                remote(a, 4 + j, blk, blk, me).wait_recv()
        for cp in send_only + [fwd for _, fwd in forwards()]:
            cp.wait_send()
        for cp in two_way + local:
            cp.wait()

    return start, relay, finish, bool(relays)


def _hosted(body, n_in, n_out, comm, step, n_steps):
    if comm is None:
        return body, [], [], [], [], []
    srcs, modes = comm
    nc = len(srcs)

    def wrapped(*refs):
        ins, csrc = refs[:n_in], refs[n_in:n_in + nc]
        outs = refs[n_in + nc:n_in + nc + n_out]
        cout = refs[n_in + nc + n_out:n_in + 2 * nc + n_out]
        scratch = refs[n_in + 2 * nc + n_out:len(refs) - 3]
        start, relay, finish, has_relay = _comm_phases(modes, csrc, cout, *refs[len(refs) - 3:])
        s = step()
        pl.when(s == 0)(start)
        body(*ins, *outs, *scratch)
        if has_relay:
            pl.when(s == (2 * n_steps) // 3)(relay)
        pl.when(s == n_steps - 1)(finish)

    any_spec = pl.BlockSpec(memory_space=pl.ANY)
    return wrapped, [any_spec] * nc, list(srcs), [any_spec] * nc, _comm_out_shapes(srcs, modes), _comm_scratch(nc)


def exchange(srcs, modes, *, name):
    n = len(srcs)

    def body(*refs):
        start, relay, finish, has_relay = _comm_phases(modes, refs[:n], refs[n:2 * n], *refs[2 * n:])
        start()
        if has_relay:
            relay()
        finish()

    any_spec = pl.BlockSpec(memory_space=pl.ANY)
    return pl.pallas_call(
        body, name=name, in_specs=[any_spec] * n, out_specs=[any_spec] * n, out_shape=_comm_out_shapes(srcs, modes),
        scratch_shapes=_comm_scratch(n), compiler_params=pltpu.CompilerParams(has_side_effects=True))(*srcs)


def exchange_old(srcs, modes, *, name):
    n = len(srcs)
    out_shape = []
    for s, mode in zip(srcs, modes):
        lead = (N_DEV,) if mode in ("gather", "gather_direct") else ()
        out_shape.append(jax.ShapeDtypeStruct(lead + s.shape, s.dtype))

    def body(*refs):
        src_refs, out_refs = refs[:n], refs[n:2 * n]
        send_sems, recv_sems, local_sems = refs[2 * n:]
        x, y, c = lax.axis_index("x"), lax.axis_index("y"), lax.axis_index("c")
        me, sibling = (x, y, c), (x, y, 1 - c)
        chips = [(1 - x, y), (x, 1 - y), (1 - x, 1 - y)]

        def slot(p):
            return 4 * p[0] + 2 * p[1] + p[2]

        def remote(a, k, src, dst, to):
            return pltpu.make_async_remote_copy(src_ref=src, dst_ref=dst, send_sem=send_sems.at[a, k],
                                                recv_sem=recv_sems.at[a, k], device_id=to,
                                                device_id_type=pl.DeviceIdType.MESH)

        local, two_way, send_only, relays = [], [], [], []
        for a, mode in enumerate(modes):
            src, out = src_refs[a], out_refs[a]
            if mode == "sibling":
                two_way.append(remote(a, 0, src, out, sibling))
            elif mode == "chips":
                mine = 2 * x + y
                local.append(pltpu.make_async_copy(src.at[mine], out.at[mine], local_sems.at[a]))
                for j, chip in enumerate(chips):
                    two_way.append(remote(a, 1 + j, src.at[2 * chip[0] + chip[1]], out.at[mine], (*chip, c)))
            elif mode == "gather_direct":
                local.append(pltpu.make_async_copy(src, out.at[slot(me)], local_sems.at[a]))
                for k in range(1, N_DEV):
                    peer = (1 - x if k & 4 else x, 1 - y if k & 2 else y, 1 - c if k & 1 else c)
                    two_way.append(remote(a, k - 1, src, out.at[slot(me)], peer))
            else:
                assert mode == "gather"
                local.append(pltpu.make_async_copy(src, out.at[slot(me)], local_sems.at[a]))
                send_only.append(remote(a, 0, src, out.at[slot(me)], sibling))
                for j, chip in enumerate(chips):
                    send_only.append(remote(a, 1 + j, src, out.at[slot(me)], (*chip, c)))
                relays.append(a)
        for cp in local + two_way + send_only:
            cp.start()
        for a in relays:
            out = out_refs[a]
            for j, chip in enumerate(chips):
                landed = out.at[slot((*chip, c))]
                remote(a, 1 + j, landed, landed, me).wait_recv()
                fwd = remote(a, 4 + j, landed, landed, sibling)
                fwd.start()
                send_only.append(fwd)
        for a in relays:
            out = out_refs[a]
            blk = out.at[slot(sibling)]
            remote(a, 0, blk, blk, me).wait_recv()
            for j, chip in enumerate(chips):
                blk = out.at[slot((*chip, 1 - c))]
                remote(a, 4 + j, blk, blk, me).wait_recv()
        for cp in send_only:
            cp.wait_send()
        for cp in two_way + local:
            cp.wait()

    any_spec = pl.BlockSpec(memory_space=pl.ANY)
    return pl.pallas_call(
        body, name=name, in_specs=[any_spec] * n, out_specs=[any_spec] * n, out_shape=out_shape,
        scratch_shapes=[pltpu.SemaphoreType.DMA((n, N_DEV - 1)), pltpu.SemaphoreType.DMA((n, N_DEV - 1)),
                        pltpu.SemaphoreType.DMA((n,))],
        compiler_params=pltpu.CompilerParams(has_side_effects=True))(*srcs)


def pair_sum(a, b, *, name):
    n, R, C = a.shape
    tr = _pick(n * R, 512, 16)

    def body(a_ref, b_ref, o_ref):
        o_ref[...] = (a_ref[...].astype(F32) + b_ref[...].astype(F32)).astype(BF16)

    blk = pl.BlockSpec((tr, C), lambda i: (i, 0))
    out = pl.pallas_call(
        body, name=name, grid=(n * R // tr,), in_specs=[blk, blk], out_specs=blk,
        out_shape=jax.ShapeDtypeStruct((n * R, C), BF16),
        compiler_params=_cp("parallel"))(a.reshape(n * R, C), b.reshape(n * R, C))
    return out.reshape(n, R, C)


def sum_slots(recv, *, name):
    _, R, C = recv.shape
    tr = _pick(R, 512, 8)

    def body(r_ref, o_ref):
        acc = r_ref[0].astype(F32)
        for p in range(1, N_DEV):
            acc = acc + r_ref[p].astype(F32)
        o_ref[...] = acc

    return pl.pallas_call(
        body, name=name, grid=(R // tr,),
        in_specs=[pl.BlockSpec((N_DEV, tr, C), lambda i: (0, i, 0))],
        out_specs=pl.BlockSpec((tr, C), lambda i: (i, 0)),
        out_shape=jax.ShapeDtypeStruct((R, C), F32), compiler_params=_cp("parallel"))(recv)


def adamw(gsrc, w, m, v, *, name):
    slots, R, C = gsrc.shape
    tr = _pick(R, 256, 16 if gsrc.dtype == BF16 else 8)
    c1 = 1.0 / (1.0 - ADAM_B1 ** ADAM_STEP)
    c2 = 1.0 / (1.0 - ADAM_B2 ** ADAM_STEP)

    def body(g_ref, w_ref, m_ref, v_ref, go_ref, d_ref, mo_ref, vo_ref):
        g = g_ref[0].astype(F32)
        for p in range(1, slots):
            g = g + g_ref[p].astype(F32)
        m2 = ADAM_B1 * m_ref[...] + (1.0 - ADAM_B1) * g
        v2 = ADAM_B2 * v_ref[...] + (1.0 - ADAM_B2) * (g * g)
        go_ref[...] = g
        mo_ref[...] = m2
        vo_ref[...] = v2
        d_ref[...] = -ADAM_LR * ((m2 * c1) / (jnp.sqrt(v2 * c2) + ADAM_EPS) + ADAM_WD * w_ref[...])

    blk = pl.BlockSpec((tr, C), lambda i: (i, 0))
    shp = jax.ShapeDtypeStruct((R, C), F32)
    return pl.pallas_call(
        body, name=name, grid=(R // tr,),
        in_specs=[pl.BlockSpec((slots, tr, C), lambda i: (0, i, 0)), blk, blk, blk],
        out_specs=[blk] * 4, out_shape=[shp] * 4, compiler_params=_cp("parallel"))(gsrc, w, m, v)


def _pad_cols(a, n):
    return jnp.pad(a, ((0, 0), (0, n - a.shape[1])))


def _to_rows(a, B, seq, H):
    G = SSM_GROUPS
    R = H // G
    t = a[:, :H].reshape(B, seq, G, R).transpose(0, 2, 3, 1)
    t = jnp.pad(t, ((0, 0), (0, 0), (0, HEAD_ROWS - R), (0, 0)))
    return t.reshape(B, G * HEAD_ROWS, seq)


def _from_rows(a, B, seq, H):
    G = SSM_GROUPS
    R = H // G
    t = a.reshape(B, G, HEAD_ROWS, seq)[:, :, :R].transpose(0, 3, 1, 2).reshape(B * seq, H)
    return _pad_cols(t, LANE)


def _chip_sums(grads, name):
    c_idx = lax.axis_index("c")
    keep, give = [], []
    for g in grads:
        by_chip = g.reshape((N_CHIP, 2) + g.shape[1:])
        keep.append(lax.dynamic_index_in_dim(by_chip, c_idx, axis=1, keepdims=False))
        give.append(lax.dynamic_index_in_dim(by_chip, 1 - c_idx, axis=1, keepdims=False))
    swapped = exchange(give, ["sibling"] * len(give), name="swap_" + name)
    return [pair_sum(k, s, name=f"chip_sum_{name}_{i}") for i, (k, s) in enumerate(zip(keep, swapped))]


def local_step(x, target, p, l1_shards, *, B, seq):
    T, D = x.shape
    CW = D
    heads = CW // SB_HEAD_DIM
    DI = 2 * D
    H = DI // SSM_HEAD_DIM
    XW = DI + 2 * SSM_GROUPS * SSM_STATE
    in_odd = DI + XW + H
    w1_rows = in_odd // N_DEV
    q_off, k_off, v_off, gc_off, ga_off = 3 * CW, 4 * CW, 5 * CW, 2 * CW, 6 * CW
    dt_off = DI + XW

    n0 = rmsnorm_fwd(x, p["ev_norm_w"], name="l0_norm")
    proj0, (g_od_in_t, g_od_out) = mm_nn(n0, p["ev_w_in"], out_dtype=BF16, name="l0_in_proj",
                                         comm=(l1_shards, ["gather", "gather"]))
    w1t = g_od_in_t[:, :w1_rows].reshape(in_odd, D)
    w1t = jnp.pad(w1t, ((0, -(-(in_odd + LANE) // 256) * 256 - in_odd), (0, 0)))
    od_w_out = g_od_out.reshape(-1, D)
    (u2,) = dwconv_fwd(proj0, (0, CW), p["ev_dw_w"], p["ev_dw_b"], C=CW, seq=seq, glu=True, silu_out=False,
                       name="l0_conv")
    o, ctot = sba_fwd(proj0, B=B, seq=seq, heads=heads, q_off=q_off, k_off=k_off, v_off=v_off, name="l0_attn")
    ycat = mix0_post_fwd(u2, proj0, o, p["ev_ln_w"], p["ev_ln_b"], CW=CW, gc_off=gc_off, ga_off=ga_off,
                         name="l0_post")
    h1 = mm_nn(ycat, p["ev_w_out"], add=x, out_dtype=F32, name="l0_out_proj")

    n1 = rmsnorm_fwd(h1, p["od_norm_w"], name="l1_norm")
    proj1 = mm_nt_terms([(n1, 0, D, 0)], w1t, out_dtype=BF16, name="l1_in_proj")
    u_pre, xbc = dwconv_fwd(proj1, (DI,), p["od_conv_w"], p["od_conv_b"], C=XW, seq=seq, glu=False, silu_out=True,
                            name="l1_conv")
    bias_p, alog_p = _pad_cols(p["od_dt_bias"], LANE), _pad_cols(p["od_a_log"], LANE)
    expand = _head_expand(H, DI)
    dt, cs, dt_x, cs_x = dt_fwd(proj1, bias_p, alog_p, expand, dt_off=dt_off, name="l1_dt")
    cs_row = _to_rows(cs, B, seq, H)
    d_full = jnp.repeat(p["od_d"], SSM_HEAD_DIM, axis=1)
    y_ssd, states = scan_fwd(xbc, dt_x, cs_x, cs_row, d_full, B=B, seq=seq, DI=DI, name="l1_ssd")
    yn = gnorm_fwd(y_ssd, proj1, p["od_gnorm_w"], DI=DI, name="l1_gnorm")
    h2 = mm_nn(yn, od_w_out, add=h1, out_dtype=F32, name="l1_out_proj")

    loss, dh2, g_final = final_loss(h2, p["final_norm_w"], target, name="loss_head")

    dh2b = dh2.astype(BF16)
    g_od_w_out = mm_tn(yn, dh2b, out_dtype=BF16, name="l1_dw_out")
    dyn = mm_nt_terms([(dh2b, 0, D, 0)], od_w_out, out_dtype=BF16, name="l1_d_out_proj")
    dy_ssd, dz, g_gnorm = gnorm_bwd(dyn, y_ssd, proj1, p["od_gnorm_w"], DI=DI, name="l1_gnorm_bwd")
    dxbc_c, ddt_x, dcs_x, dcs_row, dd_part = scan_bwd(xbc, dt_x, cs_x, cs_row, d_full, states, dy_ssd, B=B, seq=seq,
                                                      DI=DI, name="l1_ssd_bwd")
    g_d = dd_part.sum(axis=(0, 1)).reshape(H, SSM_HEAD_DIM).sum(axis=1)[None, :]
    draw, g_bias, g_alog = dt_bwd(ddt_x, dcs_x, _from_rows(dcs_row, B, seq, H), proj1, dt, bias_p, alog_p, expand.T,
                                  dt_off=dt_off, n_heads=H, name="l1_dt_bwd")
    dxbc, g_conv_w, g_conv_b = dwconv_bwd(dxbc_c, u_pre, proj1, (DI,), p["od_conv_w"], C=XW, seq=seq, glu=False,
                                          silu_out=True, name="l1_conv_bwd")
    tw = 512 if DI % 512 == 0 else LANE
    terms = [(dz, j, tw, j * tw) for j in range(DI // tw)]
    terms += [(dxbc, j, tw, DI + j * tw) for j in range(XW // tw)]
    terms += [(draw, 0, LANE, dt_off)]
    dn1 = mm_nn_terms(terms, w1t, out_dtype=F32, name="l1_d_in_proj")
    g_od_w_in_t = jnp.concatenate([mm_tn(dz, n1, out_dtype=BF16, name="l1_dw_in_z"),
                                   mm_tn(dxbc, n1, out_dtype=BF16, name="l1_dw_in_xbc"),
                                   mm_tn(draw, n1, out_dtype=BF16, name="l1_dw_in_dt")], axis=0)[:in_odd]
    dh1, g_od_norm = rmsnorm_bwd(h1, p["od_norm_w"], dn1, dh2, name="l1_norm_bwd")
    w1_pad = (-w1_rows) % 16
    l1_chip = _chip_sums([jnp.pad(g_od_w_in_t.reshape(N_DEV, w1_rows, D), ((0, 0), (0, w1_pad), (0, 0))),
                          g_od_w_out.reshape(N_DEV, -1, D)], "l1")

    dh1b = dh1.astype(BF16)
    g_ev_w_out = mm_tn(ycat, dh1b, out_dtype=BF16, name="l0_dw_out")
    dycat = mm_nt_terms([(dh1b, 0, D, 0)], p["ev_w_out"], out_dtype=BF16, name="l0_d_out_proj")
    du2, dgc, dga, do, g_ln_w, g_ln_b = mix0_post_bwd(dycat, u2, proj0, o, p["ev_ln_w"], p["ev_ln_b"], CW=CW,
                                                      gc_off=gc_off, ga_off=ga_off, name="l0_post_bwd")
    dq, dk, dv, (r_od_in_t, r_od_out) = sba_bwd(proj0, ctot, do, B=B, seq=seq, heads=heads, q_off=q_off, k_off=k_off,
                                                v_off=v_off, name="l0_attn_bwd", comm=(l1_chip, ["chips", "chips"]))
    dga_a, dga_b, g_dw_w, g_dw_b = dwconv_bwd(du2, None, proj0, (0, CW), p["ev_dw_w"], C=CW, seq=seq, glu=True,
                                              silu_out=False, name="l0_conv_bwd")
    pieces = [dga_a, dga_b, dgc, dq, dk, dv, dga]
    g_ev_w_in = jnp.concatenate([mm_tn(n0, pc, out_dtype=BF16, name=f"l0_dw_in_{j}") for j, pc in enumerate(pieces)],
                                axis=1)
    l0_chip = _chip_sums([_col_shards(g_ev_w_in), g_ev_w_out.reshape(N_DEV, -1, D)], "l0")
    dn0, (r_ev_in, r_ev_out) = mm_nt_terms([(pc, 0, CW, j * CW) for j, pc in enumerate(pieces)], p["ev_w_in"],
                                           out_dtype=F32, name="l0_d_in_proj", comm=(l0_chip, ["chips", "chips"]))
    dx, g_ev_norm = rmsnorm_bwd(x, p["ev_norm_w"], dn0, dh1, name="l0_norm_bwd")

    small = dict(ev_norm_w=g_ev_norm, ev_dw_w=g_dw_w, ev_dw_b=g_dw_b, ev_ln_w=g_ln_w, ev_ln_b=g_ln_b,
                 od_norm_w=g_od_norm, od_conv_w=g_conv_w, od_conv_b=g_conv_b, od_dt_bias=g_bias[:, :H],
                 od_a_log=g_alog[:, :H], od_d=g_d, od_gnorm_w=g_gnorm, final_norm_w=g_final)
    received = dict(ev_w_in=r_ev_in, ev_w_out=r_ev_out, od_w_in=r_od_in_t, od_w_out=r_od_out)
    return loss, dx, small, received


BIG = ("ev_w_in", "ev_w_out", "od_w_in", "od_w_out")
SMALL = ("ev_norm_w", "ev_dw_w", "ev_dw_b", "ev_ln_w", "ev_ln_b", "od_norm_w", "od_conv_w", "od_conv_b",
         "od_dt_bias", "od_a_log", "od_d", "od_gnorm_w", "final_norm_w")
SMALL_SHARDED = ("ev_dw_w", "od_norm_w", "od_conv_w", "od_conv_b", "od_gnorm_w")
ORDER = ("ev_norm_w", "ev_w_in", "ev_dw_w", "ev_dw_b", "ev_ln_w", "ev_ln_b", "ev_w_out", "od_norm_w", "od_w_in",
         "od_conv_w", "od_conv_b", "od_dt_bias", "od_a_log", "od_d", "od_gnorm_w", "od_w_out", "final_norm_w")


def _pack_rows(arrs, width, row_align):
    parts, spans, r0 = [], [], 0
    for a in arrs:
        flat = a.reshape(-1)
        rows = -(-flat.shape[0] // (width * row_align)) * row_align
        parts.append(jnp.pad(flat, (0, rows * width - flat.shape[0])).reshape(rows, width))
        spans.append((r0, a.size, a.shape))
        r0 += rows
    return jnp.concatenate(parts, axis=0), spans


def _unpack_rows(packed, spans):
    lead = packed.shape[:-2]
    width = packed.shape[-1]
    out = []
    for r0, size, shape in spans:
        rows = -(-size // width)
        blk = packed[..., r0:r0 + rows, :].reshape(lead + (rows * width,))[..., :size]
        out.append(blk.reshape(lead + tuple(shape)))
    return out


def _col_shards(a):
    R, C8 = a.shape
    return a.reshape(R, N_DEV, C8 // N_DEV).transpose(1, 0, 2)


def _col_unshards(a):
    n, R, C = a.shape
    return a.transpose(1, 0, 2).reshape(R, n * C)


def kernel(x, ev_norm_w, ev_w_in, ev_dw_w, ev_dw_b, ev_ln_w, ev_ln_b, ev_w_out, od_norm_w, od_w_in, od_conv_w, od_conv_b, od_dt_bias, od_a_log, od_d, od_gnorm_w, od_w_out, final_norm_w, loss_target, m_ev_norm_w, m_ev_w_in, m_ev_dw_w, m_ev_dw_b, m_ev_ln_w, m_ev_ln_b, m_ev_w_out, m_od_norm_w, m_od_w_in, m_od_conv_w, m_od_conv_b, m_od_dt_bias, m_od_a_log, m_od_d, m_od_gnorm_w, m_od_w_out, m_final_norm_w, v_ev_norm_w, v_ev_w_in, v_ev_dw_w, v_ev_dw_b, v_ev_ln_w, v_ev_ln_b, v_ev_w_out, v_od_norm_w, v_od_w_in, v_od_conv_w, v_od_conv_b, v_od_dt_bias, v_od_a_log, v_od_d, v_od_gnorm_w, v_od_w_out, v_final_norm_w):
    loc = dict(ev_norm_w=ev_norm_w, ev_w_in=ev_w_in, ev_dw_w=ev_dw_w, ev_dw_b=ev_dw_b, ev_ln_w=ev_ln_w,
               ev_ln_b=ev_ln_b, ev_w_out=ev_w_out, od_norm_w=od_norm_w, od_w_in=od_w_in, od_conv_w=od_conv_w,
               od_conv_b=od_conv_b, od_dt_bias=od_dt_bias, od_a_log=od_a_log, od_d=od_d, od_gnorm_w=od_gnorm_w,
               od_w_out=od_w_out, final_norm_w=final_norm_w)
    mom = dict(ev_norm_w=m_ev_norm_w, ev_w_in=m_ev_w_in, ev_dw_w=m_ev_dw_w, ev_dw_b=m_ev_dw_b, ev_ln_w=m_ev_ln_w,
               ev_ln_b=m_ev_ln_b, ev_w_out=m_ev_w_out, od_norm_w=m_od_norm_w, od_w_in=m_od_w_in,
               od_conv_w=m_od_conv_w, od_conv_b=m_od_conv_b, od_dt_bias=m_od_dt_bias, od_a_log=m_od_a_log,
               od_d=m_od_d, od_gnorm_w=m_od_gnorm_w, od_w_out=m_od_w_out, final_norm_w=m_final_norm_w)
    var = dict(ev_norm_w=v_ev_norm_w, ev_w_in=v_ev_w_in, ev_dw_w=v_ev_dw_w, ev_dw_b=v_ev_dw_b, ev_ln_w=v_ev_ln_w,
               ev_ln_b=v_ev_ln_b, ev_w_out=v_ev_w_out, od_norm_w=v_od_norm_w, od_w_in=v_od_w_in,
               od_conv_w=v_od_conv_w, od_conv_b=v_od_conv_b, od_dt_bias=v_od_dt_bias, od_a_log=v_od_a_log,
               od_d=v_od_d, od_gnorm_w=v_od_gnorm_w, od_w_out=v_od_w_out, final_norm_w=v_final_norm_w)
    shapes = {n: loc[n].shape for n in ORDER}
    loc = {n: (a.reshape(1, -1) if a.ndim == 1 else a.reshape(a.shape[-2:]) if a.ndim == 3 else a)
           for n, a in loc.items()}
    mom = {n: a.reshape(loc[n].shape) for n, a in mom.items()}
    var = {n: a.reshape(loc[n].shape) for n, a in var.items()}

    B, seq, D = x.shape
    me = 4 * lax.axis_index("x") + 2 * lax.axis_index("y") + lax.axis_index("c")

    w1_rows = loc["od_w_in"].shape[1]
    w1_pad = (-w1_rows) % 16

    def to_t(a):
        return jnp.pad(a.T, ((0, w1_pad), (0, 0)))

    small_packed, small_spans = _pack_rows([loc[n] for n in SMALL_SHARDED], LANE, 8)
    g_ev_in, g_ev_out, small_all = exchange(
        [loc["ev_w_in"].astype(BF16), loc["ev_w_out"].astype(BF16), small_packed],
        ["gather", "gather", "gather_direct"], name="gather_weights")
    full = {n: loc[n] for n in SMALL}
    full["ev_w_in"] = _col_unshards(g_ev_in)
    full["ev_w_out"] = g_ev_out.reshape(-1, D)
    for n, a in zip(SMALL_SHARDED, _unpack_rows(small_all, small_spans)):
        full[n] = _col_unshards(a)

    loss, dx, grads, received = local_step(
        x.reshape(B * seq, D), loss_target.reshape(B * seq, D), full,
        [to_t(loc["od_w_in"]).astype(BF16), loc["od_w_out"].astype(BF16)], B=B, seq=seq)

    gsmall_packed, gsmall_spans = _pack_rows([grads[n] for n in SMALL], LANE, 8)
    (gsmall_recv,) = exchange([gsmall_packed], ["gather_direct"], name="gather_small_grads")

    big_out = [{} for _ in range(4)]
    for n in ("ev_w_in", "ev_w_out", "od_w_out"):
        for kind, a in enumerate(adamw(received[n], loc[n], mom[n], var[n], name="adamw_" + n)):
            big_out[kind][n] = a
    for kind, a in enumerate(adamw(received["od_w_in"], to_t(loc["od_w_in"]), to_t(mom["od_w_in"]),
                                   to_t(var["od_w_in"]), name="adamw_od_w_in")):
        big_out[kind]["od_w_in"] = a[:w1_rows].T

    gsmall = dict(zip(SMALL, _unpack_rows(sum_slots(gsmall_recv, name="sum_small_grads"), gsmall_spans)))
    for n in SMALL_SHARDED:
        width = loc[n].shape[1]
        gsmall[n] = lax.dynamic_slice_in_dim(gsmall[n], me * width, width, axis=1)
    gs, sspans = _pack_rows([gsmall[n] for n in SMALL], LANE, 8)
    ws, _ = _pack_rows([loc[n] for n in SMALL], LANE, 8)
    ms, _ = _pack_rows([mom[n] for n in SMALL], LANE, 8)
    vs, _ = _pack_rows([var[n] for n in SMALL], LANE, 8)
    small_out = [dict(zip(SMALL, _unpack_rows(a, sspans))) for a in adamw(gs[None], ws, ms, vs, name="adamw_small")]

    outs = [lax.psum(loss[0, 0], ("x", "y", "c")), dx.reshape(B, seq, D)]
    for kind in range(4):
        for n in ORDER:
            src = big_out[kind] if n in BIG else small_out[kind]
            outs.append(src[n].reshape(shapes[n]))
    return tuple(outs)
```

```python
import functools

import jax
import jax.numpy as jnp
from jax import lax
from jax.experimental import pallas as pl
from jax.experimental.pallas import tpu as pltpu

F32 = jnp.float32
BF16 = jnp.bfloat16

EPS = 1e-6
N_DEV = 8
LANE = 128
VMEM_LIMIT_BYTES = 48 * 1024 * 1024

SB_HEAD_DIM = 128
CONF_KERNEL = 31
SSM_CONV = 4
SSM_HEAD_DIM = 64
SSM_GROUPS = 4
SSM_STATE = 128
SSM_CHUNK = 128
HALO = 32
HEAD_ROWS = 8
NEG_BIG = -1e30

ADAM_LR = 0.001
ADAM_B1 = 0.9
ADAM_B2 = 0.999
ADAM_EPS = 1e-08
ADAM_WD = 0.01
ADAM_STEP = 10

NT = (((1,), (1,)), ((), ()))
TN = (((0,), (0,)), ((), ()))


def _cp(*sem):
    return pltpu.CompilerParams(dimension_semantics=sem, vmem_limit_bytes=VMEM_LIMIT_BYTES)


def _pick(n, cap, align):
    if n <= cap:
        return n
    t = (cap // align) * align
    while t >= align:
        if n % t == 0:
            return t
        t -= align
    raise ValueError(f"no tile for {n} (cap {cap}, align {align})")


def _sigmoid(x):
    return 0.5 * jnp.tanh(0.5 * x) + 0.5


def _silu(x):
    return x * _sigmoid(x)


def _silu_grad(x):
    s = _sigmoid(x)
    return s * (1.0 + x * (1.0 - s))


def _dot(a, b, dims=None):
    if dims is None:
        return jnp.dot(a, b, preferred_element_type=F32)
    return lax.dot_general(a, b, dims, preferred_element_type=F32)


def _tri_dot3(tri, x):
    hi = x.astype(BF16)
    r1 = x - hi.astype(F32)
    mid = r1.astype(BF16)
    lo = (r1 - mid.astype(F32)).astype(BF16)
    return _dot(tri, hi) + _dot(tri, mid) + _dot(tri, lo)


def _grid_step(n_inner):
    return lambda: pl.program_id(0) * n_inner + pl.program_id(1)


def mm_nn(a, b, *, add=None, out_dtype, name, comm=None):
    M, K = a.shape
    N = b.shape[1]
    tm = _pick(M, 1024, 16)
    tn = _pick(N, 1024, LANE)

    def body(*refs):
        if add is None:
            a_ref, b_ref, o_ref = refs
        else:
            a_ref, b_ref, add_ref, o_ref = refs
        acc = _dot(a_ref[...], b_ref[...])
        if add is not None:
            acc = acc + add_ref[...]
        o_ref[...] = acc.astype(out_dtype)

    in_specs = [pl.BlockSpec((tm, K), lambda i, j: (i, 0)), pl.BlockSpec((K, tn), lambda i, j: (0, j))]
    args = [a, b]
    if add is not None:
        in_specs.append(pl.BlockSpec((tm, tn), lambda i, j: (i, j)))
        args.append(add)
    grid = (M // tm, N // tn)
    body, c_in, c_args, c_out, c_shape, c_scratch = _hosted(body, len(args), 1, comm, _grid_step(grid[1]),
                                                            grid[0] * grid[1])
    out = pl.pallas_call(
        body, name=name, grid=grid, in_specs=in_specs + c_in,
        out_specs=[pl.BlockSpec((tm, tn), lambda i, j: (i, j))] + c_out,
        out_shape=[jax.ShapeDtypeStruct((M, N), out_dtype)] + c_shape, scratch_shapes=c_scratch,
        compiler_params=_cp(*(("arbitrary",) * 2 if comm else ("parallel",) * 2)))(*args, *c_args)
    return (out[0], out[1:]) if comm else out[0]


def mm_nt_terms(terms, b, *, out_dtype, name, comm=None):
    M = terms[0][0].shape[0]
    N = b.shape[0]
    n_terms = len(terms)
    cap = 1024 if n_terms == 1 else 512
    tm = _pick(M, cap, 16)
    tn = _pick(N, cap, LANE)

    def body(*refs):
        o_ref = refs[-1]
        acc = None
        for t in range(n_terms):
            part = _dot(refs[2 * t][...], refs[2 * t + 1][...], NT)
            acc = part if acc is None else acc + part
        o_ref[...] = acc.astype(out_dtype)

    in_specs, args = [], []
    for arr, cb, w, off in terms:
        assert off % w == 0
        in_specs.append(pl.BlockSpec((tm, w), lambda i, j, cb=cb: (i, cb)))
        in_specs.append(pl.BlockSpec((tn, w), lambda i, j, ob=off // w: (j, ob)))
        args += [arr, b]
    grid = (M // tm, N // tn)
    body, c_in, c_args, c_out, c_shape, c_scratch = _hosted(body, len(args), 1, comm, _grid_step(grid[1]),
                                                            grid[0] * grid[1])
    out = pl.pallas_call(
        body, name=name, grid=grid, in_specs=in_specs + c_in,
        out_specs=[pl.BlockSpec((tm, tn), lambda i, j: (i, j))] + c_out,
        out_shape=[jax.ShapeDtypeStruct((M, N), out_dtype)] + c_shape, scratch_shapes=c_scratch,
        compiler_params=_cp(*(("arbitrary",) * 2 if comm else ("parallel",) * 2)))(*args, *c_args)
    return (out[0], out[1:]) if comm else out[0]


def mm_nn_terms(terms, b, *, out_dtype, name):
    M = terms[0][0].shape[0]
    N = b.shape[1]
    tm = _pick(M, 512, 16)
    tn = _pick(N, 512, LANE)
    n_terms = len(terms)

    def body(*refs):
        o_ref = refs[-1]
        acc = None
        for t in range(n_terms):
            part = _dot(refs[2 * t][...], refs[2 * t + 1][...])
            acc = part if acc is None else acc + part
        o_ref[...] = acc.astype(out_dtype)

    in_specs, args = [], []
    for arr, cb, w, off in terms:
        assert off % w == 0
        in_specs.append(pl.BlockSpec((tm, w), lambda i, j, cb=cb: (i, cb)))
        in_specs.append(pl.BlockSpec((w, tn), lambda i, j, ob=off // w: (ob, j)))
        args += [arr, b]
    return pl.pallas_call(
        body, name=name, grid=(M // tm, N // tn), in_specs=in_specs,
        out_specs=pl.BlockSpec((tm, tn), lambda i, j: (i, j)),
        out_shape=jax.ShapeDtypeStruct((M, N), out_dtype),
        compiler_params=_cp("parallel", "parallel"))(*args)


def mm_tn(a, b, *, out_dtype, name):
    T, M = a.shape
    N = b.shape[1]
    tm = _pick(M, 1024, LANE)
    tn = _pick(N, 1024, LANE)
    tk = _pick(T, 1024, 16)
    nk = T // tk

    def body(a_ref, b_ref, o_ref, acc_ref):
        k = pl.program_id(2)

        @pl.when(k == 0)
        def _():
            acc_ref[...] = jnp.zeros_like(acc_ref)

        acc_ref[...] += _dot(a_ref[...], b_ref[...], TN)

        @pl.when(k == nk - 1)
        def _():
            o_ref[...] = acc_ref[...].astype(out_dtype)

    return pl.pallas_call(
        body, name=name, grid=(M // tm, N // tn, nk),
        in_specs=[pl.BlockSpec((tk, tm), lambda i, j, k: (k, i)), pl.BlockSpec((tk, tn), lambda i, j, k: (k, j))],
        out_specs=pl.BlockSpec((tm, tn), lambda i, j, k: (i, j)),
        out_shape=jax.ShapeDtypeStruct((M, N), out_dtype),
        scratch_shapes=[pltpu.VMEM((tm, tn), F32)],
        compiler_params=_cp("parallel", "parallel", "arbitrary"))(a, b)


def rmsnorm_fwd(h, w, *, name):
    T, D = h.shape
    tt = _pick(T, 512, 16)

    def body(h_ref, w_ref, n_ref):
        x = h_ref[...]
        r = lax.rsqrt(jnp.mean(x * x, axis=-1, keepdims=True) + EPS)
        n_ref[...] = (x * r * w_ref[...]).astype(BF16)

    return pl.pallas_call(
        body, name=name, grid=(T // tt,),
        in_specs=[pl.BlockSpec((tt, D), lambda i: (i, 0)), pl.BlockSpec((1, D), lambda i: (0, 0))],
        out_specs=pl.BlockSpec((tt, D), lambda i: (i, 0)),
        out_shape=jax.ShapeDtypeStruct((T, D), BF16),
        compiler_params=_cp("parallel"))(h, w)


def rmsnorm_bwd(h, w, dn, dres, *, name):
    T, D = h.shape
    tt = _pick(T, 512, 16)

    def body(h_ref, w_ref, dn_ref, dres_ref, dh_ref, dhb_ref, gw_ref):
        @pl.when(pl.program_id(0) == 0)
        def _():
            gw_ref[...] = jnp.zeros_like(gw_ref)

        x = h_ref[...]
        r = lax.rsqrt(jnp.mean(x * x, axis=-1, keepdims=True) + EPS)
        xhat = x * r
        g = dn_ref[...].astype(F32)
        gw_ref[...] += jnp.sum(g * xhat, axis=0, keepdims=True)
        dxh = g * w_ref[...]
        dx = r * (dxh - xhat * jnp.mean(dxh * xhat, axis=-1, keepdims=True))
        dh = dres_ref[...] + dx
        dh_ref[...] = dh
        dhb_ref[...] = dh.astype(BF16)

    row = pl.BlockSpec((tt, D), lambda i: (i, 0))
    vec = pl.BlockSpec((1, D), lambda i: (0, 0))
    return pl.pallas_call(
        body, name=name, grid=(T // tt,), in_specs=[row, vec, row, row], out_specs=[row, row, vec],
        out_shape=[jax.ShapeDtypeStruct((T, D), F32), jax.ShapeDtypeStruct((T, D), BF16),
                   jax.ShapeDtypeStruct((1, D), F32)],
        compiler_params=_cp("arbitrary"))(h, w, dn, dres)


def final_loss(h, w, target, *, name):
    T, D = h.shape
    tt = _pick(T, 512, 16)

    def body(h_ref, w_ref, t_ref, loss_ref, dh_ref, dhb_ref, gw_ref):
        @pl.when(pl.program_id(0) == 0)
        def _():
            gw_ref[...] = jnp.zeros_like(gw_ref)
            loss_ref[...] = jnp.zeros_like(loss_ref)

        x = h_ref[...]
        r = lax.rsqrt(jnp.mean(x * x, axis=-1, keepdims=True) + EPS)
        xhat = x * r
        e = xhat * w_ref[...] - t_ref[...]
        loss_ref[...] += jnp.sum(e * e) * (0.5 / D)
        g = e * (1.0 / D)
        gw_ref[...] += jnp.sum(g * xhat, axis=0, keepdims=True)
        dxh = g * w_ref[...]
        dh = r * (dxh - xhat * jnp.mean(dxh * xhat, axis=-1, keepdims=True))
        dh_ref[...] = dh
        dhb_ref[...] = dh.astype(BF16)

    row = pl.BlockSpec((tt, D), lambda i: (i, 0))
    vec = pl.BlockSpec((1, D), lambda i: (0, 0))
    one = pl.BlockSpec((1, LANE), lambda i: (0, 0))
    return pl.pallas_call(
        body, name=name, grid=(T // tt,), in_specs=[row, vec, row], out_specs=[one, row, row, vec],
        out_shape=[jax.ShapeDtypeStruct((1, LANE), F32), jax.ShapeDtypeStruct((T, D), F32),
                   jax.ShapeDtypeStruct((T, D), BF16), jax.ShapeDtypeStruct((1, D), F32)],
        compiler_params=_cp("arbitrary"))(h, w, target)


CONV_CHUNK = 32
SUBLANES = 8


def _conv_tiles(seq, C):
    return _pick(seq, 512, HALO), _pick(C, 512, LANE)


def _residues(offsets):
    return sorted({s % SUBLANES for s in offsets} - {0})


def _fill_shifted(buf, shifted, residues):
    n = buf.shape[0] - SUBLANES
    for i, r in enumerate(residues):
        shifted[i, 0:n, :] = buf[r:r + n, :]


def _tap(buf, shifted, residues, offset, start, rows):
    r = offset % SUBLANES
    base = offset - r
    ref = buf if r == 0 else shifted.at[residues.index(r)]
    return ref[pl.ds(start + base, rows), :]


def dwconv_fwd(src, offs, w, b, *, C, seq, glu, silu_out, name):
    T = src.shape[0]
    K = w.shape[0]
    tt, tc = _conv_tiles(seq, C)
    n_in = 2 if glu else 1
    per = tt // HALO
    offsets = [HALO - (K - 1) + k for k in range(K)]
    residues = _residues(offsets)

    def body(*refs):
        cur = refs[0:2 * n_in:2]
        halo = refs[1:2 * n_in:2]
        w_ref, b_ref = refs[2 * n_in], refs[2 * n_in + 1]
        outs = refs[2 * n_in + 2:-2]
        buf, shifted = refs[-2], refs[-1]
        i = pl.program_id(1)
        first = (i * tt) % seq == 0

        def pre(rs):
            v = rs[0][...].astype(F32)
            return v * _sigmoid(rs[1][...].astype(F32)) if glu else v

        buf[0:HALO, :] = jnp.where(first, 0.0, pre(halo))
        buf[HALO:HALO + tt, :] = pre(cur)
        _fill_shifted(buf, shifted, residues)

        def chunk(ci, carry):
            start = pl.multiple_of(ci * CONV_CHUNK, CONV_CHUNK)
            acc = jnp.broadcast_to(b_ref[...], (CONV_CHUNK, tc))
            for k in range(K):
                acc = acc + w_ref[k:k + 1, :] * _tap(buf, shifted, residues, offsets[k], start, CONV_CHUNK)
            outs[0][pl.ds(start, CONV_CHUNK), :] = acc.astype(BF16)
            if silu_out:
                outs[1][pl.ds(start, CONV_CHUNK), :] = _silu(acc).astype(BF16)
            return carry

        lax.fori_loop(0, tt // CONV_CHUNK, chunk, 0)

    in_specs, args = [], []
    for off in offs:
        assert off % tc == 0
        in_specs.append(pl.BlockSpec((tt, tc), lambda j, i, ob=off // tc: (i, ob + j)))
        in_specs.append(pl.BlockSpec((HALO, tc), lambda j, i, ob=off // tc: (jnp.maximum(i * per - 1, 0), ob + j)))
        args += [src, src]
    in_specs += [pl.BlockSpec((K, tc), lambda j, i: (0, j)), pl.BlockSpec((1, tc), lambda j, i: (0, j))]
    args += [w, b]
    n_out = 2 if silu_out else 1
    out = pl.pallas_call(
        body, name=name, grid=(C // tc, T // tt), in_specs=in_specs,
        out_specs=[pl.BlockSpec((tt, tc), lambda j, i: (i, j))] * n_out,
        out_shape=[jax.ShapeDtypeStruct((T, C), BF16)] * n_out,
        scratch_shapes=[pltpu.VMEM((HALO + tt, tc), F32), pltpu.VMEM((max(len(residues), 1), HALO + tt, tc), F32)],
        compiler_params=_cp("parallel", "arbitrary"))(*args)
    return out


def dwconv_bwd(du, u, src, offs, w, *, C, seq, glu, silu_out, name):
    T = src.shape[0]
    K = w.shape[0]
    tt, tc = _conv_tiles(seq, C)
    n_in = 2 if glu else 1
    per = tt // HALO
    last_blk = T // HALO - 1
    g_offsets = [K - 1 - k for k in range(K)]
    x_offsets = [HALO - (K - 1) + k for k in range(K)]
    g_res, x_res = _residues(g_offsets), _residues(x_offsets)

    def body(*refs):
        pos = 0
        du_cur, du_nxt = refs[0], refs[1]
        pos = 2
        if silu_out:
            u_cur, u_nxt = refs[2], refs[3]
            pos = 4
        cur = refs[pos:pos + 2 * n_in:2]
        halo = refs[pos + 1:pos + 2 * n_in:2]
        pos += 2 * n_in
        w_ref = refs[pos]
        outs = refs[pos + 1:pos + 1 + n_in]
        dw_ref, db_ref = refs[pos + 1 + n_in], refs[pos + 2 + n_in]
        gbuf, gshift, xbuf, xshift, dw_acc, db_acc = refs[-6:]
        i = pl.program_id(1)
        first = (i * tt) % seq == 0
        last = ((i + 1) * tt) % seq == 0

        @pl.when(i == 0)
        def _():
            dw_acc[...] = jnp.zeros_like(dw_acc)
            db_acc[...] = jnp.zeros_like(db_acc)

        g_cur = du_cur[...].astype(F32)
        g_nxt = du_nxt[...].astype(F32)
        if silu_out:
            g_cur = g_cur * _silu_grad(u_cur[...].astype(F32))
            g_nxt = g_nxt * _silu_grad(u_nxt[...].astype(F32))
        gbuf[0:tt, :] = g_cur
        gbuf[tt:tt + HALO, :] = jnp.where(last, 0.0, g_nxt)

        def pre(rs):
            v = rs[0][...].astype(F32)
            return v * _sigmoid(rs[1][...].astype(F32)) if glu else v

        xbuf[0:HALO, :] = jnp.where(first, 0.0, pre(halo))
        xbuf[HALO:HALO + tt, :] = pre(cur)
        _fill_shifted(gbuf, gshift, g_res)
        _fill_shifted(xbuf, xshift, x_res)

        def fold(v):
            out = v[0:SUBLANES]
            for s in range(SUBLANES, CONV_CHUNK, SUBLANES):
                out = out + v[s:s + SUBLANES]
            return out

        def chunk(ci, carry):
            start = pl.multiple_of(ci * CONV_CHUNK, CONV_CHUNK)
            g = gbuf[pl.ds(start, CONV_CHUNK), :]
            dx = jnp.zeros((CONV_CHUNK, tc), F32)
            for k in range(K):
                dx = dx + w_ref[k:k + 1, :] * _tap(gbuf, gshift, g_res, g_offsets[k], start, CONV_CHUNK)
                dw_acc[k * SUBLANES:(k + 1) * SUBLANES, :] += fold(
                    g * _tap(xbuf, xshift, x_res, x_offsets[k], start, CONV_CHUNK))
            db_acc[...] += fold(g)
            rows = pl.ds(start, CONV_CHUNK)
            if glu:
                a = cur[0][rows, :].astype(F32)
                s = _sigmoid(cur[1][rows, :].astype(F32))
                outs[0][rows, :] = (dx * s).astype(BF16)
                outs[1][rows, :] = (dx * a * s * (1.0 - s)).astype(BF16)
            else:
                outs[0][rows, :] = dx.astype(BF16)
            return carry

        lax.fori_loop(0, tt // CONV_CHUNK, chunk, 0)

        @pl.when(i == T // tt - 1)
        def _():
            for k in range(K):
                dw_ref[k:k + 1, :] = jnp.sum(dw_acc[k * SUBLANES:(k + 1) * SUBLANES, :], axis=0, keepdims=True)
            db_ref[...] = jnp.sum(db_acc[...], axis=0, keepdims=True)

    def cur_spec(ob):
        return pl.BlockSpec((tt, tc), lambda j, i: (i, ob + j))

    def nxt_spec(ob):
        return pl.BlockSpec((HALO, tc), lambda j, i: (jnp.minimum((i + 1) * per, last_blk), ob + j))

    def prv_spec(ob):
        return pl.BlockSpec((HALO, tc), lambda j, i: (jnp.maximum(i * per - 1, 0), ob + j))

    in_specs = [cur_spec(0), nxt_spec(0)]
    args = [du, du]
    if silu_out:
        in_specs += [cur_spec(0), nxt_spec(0)]
        args += [u, u]
    for off in offs:
        assert off % tc == 0
        in_specs += [cur_spec(off // tc), prv_spec(off // tc)]
        args += [src, src]
    in_specs.append(pl.BlockSpec((K, tc), lambda j, i: (0, j)))
    args.append(w)
    out_specs = [pl.BlockSpec((tt, tc), lambda j, i: (i, j))] * n_in
    out_specs += [pl.BlockSpec((K, tc), lambda j, i: (0, j)), pl.BlockSpec((1, tc), lambda j, i: (0, j))]
    out_shape = [jax.ShapeDtypeStruct((T, C), BF16)] * n_in
    out_shape += [jax.ShapeDtypeStruct((K, C), F32), jax.ShapeDtypeStruct((1, C), F32)]
    return pl.pallas_call(
        body, name=name, grid=(C // tc, T // tt), in_specs=in_specs, out_specs=out_specs, out_shape=out_shape,
        scratch_shapes=[pltpu.VMEM((tt + HALO, tc), F32), pltpu.VMEM((max(len(g_res), 1), tt + HALO, tc), F32),
                        pltpu.VMEM((HALO + tt, tc), F32), pltpu.VMEM((max(len(x_res), 1), HALO + tt, tc), F32),
                        pltpu.VMEM((K * SUBLANES, tc), F32), pltpu.VMEM((SUBLANES, tc), F32)],
        compiler_params=_cp("parallel", "arbitrary"))(*args)


def mix0_post_fwd(u2, proj, o, ln_w, ln_b, *, CW, gc_off, ga_off, name):
    T = u2.shape[0]
    tt = _pick(T, 256, 16)

    def body(u_ref, gc_ref, ga_ref, o_ref, lw_ref, lb_ref, y_ref):
        u = u_ref[...].astype(F32)
        mu = jnp.mean(u, axis=-1, keepdims=True)
        xc = u - mu
        r = lax.rsqrt(jnp.mean(xc * xc, axis=-1, keepdims=True) + EPS)
        u3 = xc * r * lw_ref[...] + lb_ref[...]
        y_ref[:, 0:CW] = (_silu(u3) * _silu(gc_ref[...].astype(F32))).astype(BF16)
        y_ref[:, CW:2 * CW] = (o_ref[...].astype(F32) * _silu(ga_ref[...].astype(F32))).astype(BF16)

    row = pl.BlockSpec((tt, CW), lambda i: (i, 0))
    vec = pl.BlockSpec((1, CW), lambda i: (0, 0))
    return pl.pallas_call(
        body, name=name, grid=(T // tt,),
        in_specs=[row, pl.BlockSpec((tt, CW), lambda i: (i, gc_off // CW)),
                  pl.BlockSpec((tt, CW), lambda i: (i, ga_off // CW)), row, vec, vec],
        out_specs=pl.BlockSpec((tt, 2 * CW), lambda i: (i, 0)),
        out_shape=jax.ShapeDtypeStruct((T, 2 * CW), BF16),
        compiler_params=_cp("parallel"))(u2, proj, proj, o, ln_w, ln_b)


def mix0_post_bwd(dy, u2, proj, o, ln_w, ln_b, *, CW, gc_off, ga_off, name):
    T = u2.shape[0]
    tt = _pick(T, 256, 16)

    def body(dy_ref, u_ref, gc_ref, ga_ref, o_ref, lw_ref, lb_ref, du_ref, dgc_ref, dga_ref, do_ref, dlw_ref, dlb_ref):
        @pl.when(pl.program_id(0) == 0)
        def _():
            dlw_ref[...] = jnp.zeros_like(dlw_ref)
            dlb_ref[...] = jnp.zeros_like(dlb_ref)

        dyc = dy_ref[:, 0:CW].astype(F32)
        dya = dy_ref[:, CW:2 * CW].astype(F32)
        u = u_ref[...].astype(F32)
        mu = jnp.mean(u, axis=-1, keepdims=True)
        xc = u - mu
        r = lax.rsqrt(jnp.mean(xc * xc, axis=-1, keepdims=True) + EPS)
        xhat = xc * r
        u3 = xhat * lw_ref[...] + lb_ref[...]
        gc = gc_ref[...].astype(F32)
        dgc_ref[...] = (dyc * _silu(u3) * _silu_grad(gc)).astype(BF16)
        du3 = dyc * _silu(gc) * _silu_grad(u3)
        dlw_ref[...] += jnp.sum(du3 * xhat, axis=0, keepdims=True)
        dlb_ref[...] += jnp.sum(du3, axis=0, keepdims=True)
        dxh = du3 * lw_ref[...]
        du = r * (dxh - jnp.mean(dxh, axis=-1, keepdims=True) - xhat * jnp.mean(dxh * xhat, axis=-1, keepdims=True))
        du_ref[...] = du.astype(BF16)
        ga = ga_ref[...].astype(F32)
        ov = o_ref[...].astype(F32)
        do_ref[...] = (dya * _silu(ga)).astype(BF16)
        dga_ref[...] = (dya * ov * _silu_grad(ga)).astype(BF16)

    row = pl.BlockSpec((tt, CW), lambda i: (i, 0))
    vec = pl.BlockSpec((1, CW), lambda i: (0, 0))
    big = jax.ShapeDtypeStruct((T, CW), BF16)
    small = jax.ShapeDtypeStruct((1, CW), F32)
    return pl.pallas_call(
        body, name=name, grid=(T // tt,),
        in_specs=[pl.BlockSpec((tt, 2 * CW), lambda i: (i, 0)), row,
                  pl.BlockSpec((tt, CW), lambda i: (i, gc_off // CW)),
                  pl.BlockSpec((tt, CW), lambda i: (i, ga_off // CW)), row, vec, vec],
        out_specs=[row, row, row, row, vec, vec],
        out_shape=[big, big, big, big, small, small],
        compiler_params=_cp("arbitrary"))(dy, u2, proj, proj, o, ln_w, ln_b)


SB_UNDERFLOW = 110.0
SB_BOUND_MARGIN = 1.02


def _sb_tile(seq):
    return _pick(seq, 256, LANE)


def _softplus(z):
    return jnp.maximum(z, 0.0) + jnp.log(1.0 + jnp.exp(-jnp.abs(z)))


def _tri01(n, lower):
    i = lax.broadcasted_iota(jnp.int32, (n, n), 0)
    j = lax.broadcasted_iota(jnp.int32, (n, n), 1)
    return ((i >= j) if lower else (i <= j)).astype(BF16)


SB_HEADS_FWD = 4
SB_HEADS_BWD = 2


def _sb_heads_per_step(heads, want):
    while heads % want:
        want //= 2
    return want


def sba_fwd(proj, *, B, seq, heads, q_off, k_off, v_off, name):
    dh = SB_HEAD_DIM
    tq = _sb_tile(seq)
    nq = seq // tq
    hps = _sb_heads_per_step(heads, SB_HEADS_FWD)
    hw = hps * dh
    scale = dh ** -0.5

    def body(q_ref, k_ref, v_ref, o_ref, ct_ref, acc_ref, kmax_ref):
        qi = pl.program_id(1)
        tri = _tri01(tq, True)
        below = lax.broadcasted_iota(jnp.int32, (tq, tq), 1) < lax.broadcasted_iota(jnp.int32, (tq, tq), 0)
        qs = [(q_ref[:, h * dh:(h + 1) * dh].astype(F32) * scale).astype(BF16) for h in range(hps)]

        @pl.when(qi == 0)
        def _():
            def chunk(i, best):
                rows = k_ref[pl.ds(pl.multiple_of(i * tq, tq), tq), :].astype(F32)
                sq = rows * rows
                return tuple(jnp.maximum(best[h], jnp.max(jnp.sum(sq[:, h * dh:(h + 1) * dh], axis=1, keepdims=True),
                                                          axis=0, keepdims=True)) for h in range(hps))

            best = lax.fori_loop(0, nq, chunk, (jnp.zeros((1, 1), F32),) * hps)
            for h in range(hps):
                kmax_ref[h] = jnp.broadcast_to(jnp.sqrt(best[h]), (8, LANE))

        z_bound = [jnp.sqrt(jnp.sum(qs[h].astype(F32) ** 2, axis=1, keepdims=True))
                   * (SB_BOUND_MARGIN * jnp.max(kmax_ref[h], keepdims=True)) for h in range(hps)]

        def block(start, rs, diag):
            pvs, out = [], []
            for h in range(hps):
                k_blk = k_ref[pl.ds(start, tq), h * dh:(h + 1) * dh]
                v_blk = v_ref[pl.ds(start, tq), h * dh:(h + 1) * dh]
                z = _dot(qs[h], k_blk, NT)
                sp = _softplus(z)
                if diag:
                    sp = jnp.where(below, sp, 0.0)
                wts = jnp.exp(z - (_dot(sp.astype(BF16), tri) + rs[h]))
                if diag:
                    wts = jnp.where(below, wts, 0.0)
                pvs.append(_dot(wts.astype(BF16), v_blk))
                out.append(rs[h] + jnp.sum(sp, axis=-1, keepdims=True))
            return pvs, tuple(out)

        zero = jnp.zeros((tq, 1), F32)
        pvs, rs = block(pl.multiple_of(qi * tq, tq), (zero,) * hps, True)
        for h in range(hps):
            acc_ref[:, h * dh:(h + 1) * dh] = pvs[h]

        def more(c):
            j, rs = c
            slack = rs[0] - z_bound[0]
            for h in range(1, hps):
                slack = jnp.minimum(slack, rs[h] - z_bound[h])
            return jnp.logical_and(j < qi, jnp.min(slack) <= SB_UNDERFLOW)

        def step(c):
            j, rs = c
            pvs, rs = block(pl.multiple_of((qi - 1 - j) * tq, tq), rs, False)
            for h in range(hps):
                acc_ref[:, h * dh:(h + 1) * dh] += pvs[h]
            return j + 1, rs

        n_left, totals = lax.while_loop(more, step, (jnp.int32(0), rs))
        o_ref[...] = acc_ref[...].astype(BF16)
        for h in range(hps):
            ct_ref[0, 0, h, 0:8, :] = jnp.broadcast_to(totals[h], (tq, LANE)).T[0:8, :]
            ct_ref[0, 0, h, 8:16, :] = jnp.full((8, tq), n_left, F32)

    qb, kb, vb = q_off // hw, k_off // hw, v_off // hw
    G = heads // hps
    return pl.pallas_call(
        body, name=name, grid=(B * G, nq),
        in_specs=[pl.BlockSpec((tq, hw), lambda g, i: ((g // G) * nq + i, qb + g % G)),
                  pl.BlockSpec((seq, hw), lambda g, i: (g // G, kb + g % G)),
                  pl.BlockSpec((seq, hw), lambda g, i: (g // G, vb + g % G))],
        out_specs=[pl.BlockSpec((tq, hw), lambda g, i: ((g // G) * nq + i, g % G)),
                   pl.BlockSpec((1, 1, hps, 16, tq), lambda g, i: (g // G, i, g % G, 0, 0))],
        out_shape=[jax.ShapeDtypeStruct((B * seq, heads * dh), BF16),
                   jax.ShapeDtypeStruct((B, nq, heads, 16, tq), F32)],
        scratch_shapes=[pltpu.VMEM((tq, hw), F32), pltpu.VMEM((hps, 8, LANE), F32)],
        compiler_params=_cp("parallel", "arbitrary"))(proj, proj, proj)


def sba_bwd(proj, ctot, do, *, B, seq, heads, q_off, k_off, v_off, name, comm=None):
    dh = SB_HEAD_DIM
    tq = _sb_tile(seq)
    nq = seq // tq
    hps = _sb_heads_per_step(heads, SB_HEADS_BWD)
    hw = hps * dh
    scale = dh ** -0.5

    def body(q_ref, k_ref, v_ref, ct_ref, do_ref, dq_ref, dk_ref, dv_ref, dq_acc, dk_acc, dv_acc):
        qi = pl.program_id(1)

        @pl.when(qi == 0)
        def _():
            dk_acc[...] = jnp.zeros_like(dk_acc)
            dv_acc[...] = jnp.zeros_like(dv_acc)

        tri_sfx = _tri01(tq, True)
        tri_pre = _tri01(tq, False)
        below = lax.broadcasted_iota(jnp.int32, (tq, tq), 1) < lax.broadcasted_iota(jnp.int32, (tq, tq), 0)
        qs = [(q_ref[:, h * dh:(h + 1) * dh].astype(F32) * scale).astype(BF16) for h in range(hps)]
        dos = [do_ref[:, h * dh:(h + 1) * dh] for h in range(hps)]
        totals = [jnp.max(jnp.broadcast_to(ct_ref[0, 0, h, 0:1, :], (LANE, tq)).T, axis=1, keepdims=True)
                  for h in range(hps)]
        dq_acc[...] = jnp.zeros_like(dq_acc)

        def block(start, carry, diag):
            out = []
            for h in range(hps):
                pc, pg = carry[h]
                cols = slice(h * dh, (h + 1) * dh)
                k_blk = k_ref[pl.ds(start, tq), cols]
                v_blk = v_ref[pl.ds(start, tq), cols]
                z = _dot(qs[h], k_blk, NT)
                sp = _softplus(z)
                sig = jnp.exp(z - sp)
                if diag:
                    sp = jnp.where(below, sp, 0.0)
                pc_next = pc + jnp.sum(sp, axis=-1, keepdims=True)
                wts = jnp.exp(z - (_dot(sp.astype(BF16), tri_sfx) + (totals[h] - pc_next)))
                if diag:
                    wts = jnp.where(below, wts, 0.0)
                g = _dot(dos[h], v_blk, NT) * wts
                dz = g - sig * (_dot(g.astype(BF16), tri_pre) + pg)
                if diag:
                    dz = jnp.where(below, dz, 0.0)
                dz = dz.astype(BF16)
                dq_acc[:, cols] += _dot(dz, k_blk)
                dk_acc[pl.ds(start, tq), cols] += _dot(dz, qs[h], TN)
                dv_acc[pl.ds(start, tq), cols] += _dot(wts.astype(BF16), dos[h], TN)
                out.append((pc_next, pg + jnp.sum(g, axis=-1, keepdims=True)))
            return tuple(out)

        zero = jnp.zeros((tq, 1), F32)
        n_left = jnp.max(ct_ref[0, 0, 0, 8:16, :]).astype(jnp.int32)
        carry = lax.fori_loop(qi - n_left, qi, lambda j, c: block(pl.multiple_of(j * tq, tq), c, False),
                              ((zero, zero),) * hps)
        block(pl.multiple_of(qi * tq, tq), carry, True)
        dq_ref[...] = (dq_acc[...] * scale).astype(BF16)

        @pl.when(qi == nq - 1)
        def _():
            dk_ref[...] = dk_acc[...].astype(BF16)
            dv_ref[...] = dv_acc[...].astype(BF16)

    qb, kb, vb = q_off // hw, k_off // hw, v_off // hw
    G = heads // hps
    q_spec = pl.BlockSpec((tq, hw), lambda g, i: ((g // G) * nq + i, qb + g % G))
    o_spec = pl.BlockSpec((tq, hw), lambda g, i: ((g // G) * nq + i, g % G))
    kv_out = pl.BlockSpec((seq, hw), lambda g, i: (g // G, g % G))
    shp = jax.ShapeDtypeStruct((B * seq, heads * dh), BF16)
    body, c_in, c_args, c_out, c_shape, c_scratch = _hosted(body, 5, 3, comm, _grid_step(nq), B * G * nq)
    out = pl.pallas_call(
        body, name=name, grid=(B * G, nq),
        in_specs=[q_spec,
                  pl.BlockSpec((seq, hw), lambda g, i: (g // G, kb + g % G)),
                  pl.BlockSpec((seq, hw), lambda g, i: (g // G, vb + g % G)),
                  pl.BlockSpec((1, 1, hps, 16, tq), lambda g, i: (g // G, i, g % G, 0, 0)), o_spec] + c_in,
        out_specs=[o_spec, kv_out, kv_out] + c_out, out_shape=[shp, shp, shp] + c_shape,
        scratch_shapes=[pltpu.VMEM((tq, hw), F32), pltpu.VMEM((seq, hw), F32), pltpu.VMEM((seq, hw), F32)]
        + c_scratch,
        compiler_params=_cp("arbitrary" if comm else "parallel", "arbitrary"))(proj, proj, proj, ctot, do, *c_args)
    return (out[0], out[1], out[2], out[3:]) if comm else out


def _head_expand(n_heads, DI):
    j = jnp.arange(LANE, dtype=jnp.int32)[:, None]
    c = jnp.arange(DI, dtype=jnp.int32)[None, :] // SSM_HEAD_DIM
    return ((j == c) & (j < n_heads)).astype(BF16)


def _split3(x):
    hi = x.astype(BF16)
    r1 = x - hi.astype(F32)
    mid = r1.astype(BF16)
    return hi, mid, (r1 - mid.astype(F32)).astype(BF16)


def dt_fwd(proj, bias, a_log, expand, *, dt_off, name):
    T = proj.shape[0]
    DI = expand.shape[1]
    L = SSM_CHUNK
    tt = _pick(T, 512, L)

    def body(raw_ref, bias_ref, al_ref, e_ref, dt_ref, cs_ref, dtx_ref, csx_ref):
        x = raw_ref[...].astype(F32) + bias_ref[...]
        dt = _softplus(x)
        dt_ref[...] = dt
        la = dt * (-jnp.exp(al_ref[...]))
        tri = _tri01(L, True)
        for c in range(tt // L):
            cs_ref[c * L:(c + 1) * L, :] = _tri_dot3(tri, la[c * L:(c + 1) * L, :])
        e = e_ref[...]
        dtx_ref[...] = _dot(dt.astype(BF16), e).astype(BF16)
        hi, mid, lo = _split3(cs_ref[...])
        csx_ref[...] = _dot(hi, e) + _dot(mid, e) + _dot(lo, e)

    row = pl.BlockSpec((tt, LANE), lambda i: (i, 0))
    wide = pl.BlockSpec((tt, DI), lambda i: (i, 0))
    vec = pl.BlockSpec((1, LANE), lambda i: (0, 0))
    return pl.pallas_call(
        body, name=name, grid=(T // tt,),
        in_specs=[pl.BlockSpec((tt, LANE), lambda i: (i, dt_off // LANE)), vec, vec,
                  pl.BlockSpec((LANE, DI), lambda i: (0, 0))],
        out_specs=[row, row, wide, wide],
        out_shape=[jax.ShapeDtypeStruct((T, LANE), F32), jax.ShapeDtypeStruct((T, LANE), F32),
                   jax.ShapeDtypeStruct((T, DI), BF16), jax.ShapeDtypeStruct((T, DI), F32)],
        compiler_params=_cp("parallel"))(proj, bias, a_log, expand)


def dt_bwd(ddt_x, dcs_x, dcs_cols, proj, dt, bias, a_log, reduce_t, *, dt_off, n_heads, name):
    T = proj.shape[0]
    DI = reduce_t.shape[0]
    L = SSM_CHUNK
    tt = _pick(T, 512, L)

    def body(ddtx_ref, dcsx_ref, dcsc_ref, raw_ref, dt_ref, bias_ref, al_ref, r_ref, draw_ref, dbias_ref, dal_ref,
             dla_buf):
        @pl.when(pl.program_id(0) == 0)
        def _():
            dbias_ref[...] = jnp.zeros_like(dbias_ref)
            dal_ref[...] = jnp.zeros_like(dal_ref)

        r = r_ref[...]
        ddt = _dot(ddtx_ref[...], r)
        dx = dcsx_ref[...]
        hi = dx.astype(BF16)
        dcs = _dot(hi, r) + _dot((dx - hi.astype(F32)).astype(BF16), r) + dcsc_ref[...]
        triu = _tri01(L, False)
        for c in range(tt // L):
            dla_buf[c * L:(c + 1) * L, :] = _tri_dot3(triu, dcs[c * L:(c + 1) * L, :])
        dla = dla_buf[...]
        a = -jnp.exp(al_ref[...])
        valid = lax.broadcasted_iota(jnp.int32, (tt, LANE), 1) < n_heads
        dal_ref[...] += jnp.sum(jnp.where(valid, dla * dt_ref[...], 0.0), axis=0, keepdims=True) * a
        x = raw_ref[...].astype(F32) + bias_ref[...]
        draw = jnp.where(valid, (ddt + dla * a) * _sigmoid(x), 0.0)
        dbias_ref[...] += jnp.sum(draw, axis=0, keepdims=True)
        draw_ref[...] = draw.astype(BF16)

    row = pl.BlockSpec((tt, LANE), lambda i: (i, 0))
    wide = pl.BlockSpec((tt, DI), lambda i: (i, 0))
    vec = pl.BlockSpec((1, LANE), lambda i: (0, 0))
    return pl.pallas_call(
        body, name=name, grid=(T // tt,),
        in_specs=[wide, wide, row, pl.BlockSpec((tt, LANE), lambda i: (i, dt_off // LANE)), row, vec, vec,
                  pl.BlockSpec((DI, LANE), lambda i: (0, 0))],
        out_specs=[row, vec, vec],
        out_shape=[jax.ShapeDtypeStruct((T, LANE), BF16), jax.ShapeDtypeStruct((1, LANE), F32),
                   jax.ShapeDtypeStruct((1, LANE), F32)],
        scratch_shapes=[pltpu.VMEM((tt, LANE), F32)],
        compiler_params=_cp("arbitrary"))(ddt_x, dcs_x, dcs_cols, proj, dt, bias, a_log, reduce_t)


def _pair_terms(x_ref, dtx_ref, csx_ref, csr_ref, pair, ppg, lo_half, causal):
    L = SSM_CHUNK
    g, pp = divmod(pair, ppg)
    ra = g * HEAD_ROWS + 2 * pp
    cols = slice(pair * LANE, (pair + 1) * LANE)
    X = x_ref[:, cols].astype(F32)
    dt_p = dtx_ref[:, cols].astype(F32)
    own = csx_ref[:, cols]
    other = pltpu.roll(own, SSM_HEAD_DIM, 1)
    csa_c = jnp.where(lo_half, own, other)
    csb_c = jnp.where(lo_half, other, own)
    La = jnp.exp(jnp.where(causal, csa_c - csr_ref[0, ra:ra + 1, :], NEG_BIG))
    Lb = jnp.exp(jnp.where(causal, csb_c - csr_ref[0, ra + 1:ra + 2, :], NEG_BIG))
    last = csx_ref[L - 1:L, cols]
    return g, ra, cols, X, dt_p, La, Lb, jnp.exp(own), jnp.exp(last - own), jnp.exp(last)


def scan_fwd(xbc, dt_x, cs_x, cs_row, d_full, *, B, seq, DI, name):
    L, N, G = SSM_CHUNK, SSM_STATE, SSM_GROUPS
    nc = seq // L
    XW = xbc.shape[1]
    n_pairs = DI // LANE
    ppg = n_pairs // G

    def body(x_ref, dtx_ref, csx_ref, csr_ref, d_ref, y_ref, st_ref, state):
        c = pl.program_id(1)

        @pl.when(c == 0)
        def _():
            state[...] = jnp.zeros_like(state)

        causal = lax.broadcasted_iota(jnp.int32, (L, L), 0) >= lax.broadcasted_iota(jnp.int32, (L, L), 1)
        lo_half = lax.broadcasted_iota(jnp.int32, (L, LANE), 1) < SSM_HEAD_DIM
        cbs = []
        for g in range(G):
            Bc = x_ref[:, DI + g * N:DI + (g + 1) * N]
            Cc = x_ref[:, DI + G * N + g * N:DI + G * N + (g + 1) * N]
            cbs.append((Bc, Cc, _dot(Cc, Bc, NT)))
        for pair in range(n_pairs):
            g, _, cols, X, dt_p, La, Lb, ecs, tail, e_last = _pair_terms(
                x_ref, dtx_ref, csx_ref, csr_ref, pair, ppg, lo_half, causal)
            Bc, Cc, CB = cbs[g]
            xs = X * dt_p
            xsb = xs.astype(BF16)
            y = jnp.where(lo_half, _dot((CB * La).astype(BF16), xsb), _dot((CB * Lb).astype(BF16), xsb))
            ST = state[pair]
            st_ref[0, 0, pair] = ST
            y = y + ecs * _dot(Cc, ST.astype(BF16)) + d_ref[:, cols] * X
            y_ref[:, cols] = y.astype(BF16)
            state[pair] = e_last * ST + _dot(Bc, (xs * tail).astype(BF16), TN)

    wide = pl.BlockSpec((L, DI), lambda b, c: (b * nc + c, 0))
    return pl.pallas_call(
        body, name=name, grid=(B, nc),
        in_specs=[pl.BlockSpec((L, XW), lambda b, c: (b * nc + c, 0)), wide, wide,
                  pl.BlockSpec((1, G * HEAD_ROWS, L), lambda b, c: (b, 0, c)),
                  pl.BlockSpec((1, DI), lambda b, c: (0, 0))],
        out_specs=[wide, pl.BlockSpec((1, 1, n_pairs, N, LANE), lambda b, c: (b, c, 0, 0, 0))],
        out_shape=[jax.ShapeDtypeStruct((B * seq, DI), BF16),
                   jax.ShapeDtypeStruct((B, nc, n_pairs, N, LANE), F32)],
        scratch_shapes=[pltpu.VMEM((n_pairs, N, LANE), F32)],
        compiler_params=_cp("parallel", "arbitrary"))(xbc, dt_x, cs_x, cs_row, d_full)


def scan_bwd(xbc, dt_x, cs_x, cs_row, d_full, states, dy, *, B, seq, DI, name):
    L, N, G = SSM_CHUNK, SSM_STATE, SSM_GROUPS
    nc = seq // L
    XW = xbc.shape[1]
    n_pairs = DI // LANE
    ppg = n_pairs // G
    HR = G * HEAD_ROWS
    inv_p = 1.0 / SSM_HEAD_DIM

    def body(x_ref, dtx_ref, csx_ref, csr_ref, d_ref, st_ref, dy_ref, dx_ref, ddtx_ref, dcsx_ref, dcsr_ref, dd_ref,
             dH):
        c = pl.program_id(1)

        @pl.when(c == 0)
        def _():
            dH[...] = jnp.zeros_like(dH)
            dd_ref[...] = jnp.zeros_like(dd_ref)

        causal = lax.broadcasted_iota(jnp.int32, (L, L), 0) >= lax.broadcasted_iota(jnp.int32, (L, L), 1)
        lo_half = lax.broadcasted_iota(jnp.int32, (L, LANE), 1) < SSM_HEAD_DIM
        last_row = lax.broadcasted_iota(jnp.int32, (L, LANE), 0) == L - 1
        head_row = lax.broadcasted_iota(jnp.int32, (HR, 1), 0)
        dcs_rows = jnp.zeros((HR, L), F32)

        for g in range(G):
            Bc = x_ref[:, DI + g * N:DI + (g + 1) * N]
            Cc = x_ref[:, DI + G * N + g * N:DI + G * N + (g + 1) * N]
            CB = _dot(Cc, Bc, NT)
            dCB = jnp.zeros((L, L), F32)
            dC = jnp.zeros((L, N), F32)
            dB = jnp.zeros((L, N), F32)
            for pp in range(ppg):
                pair = g * ppg + pp
                _, ra, cols, X, dt_p, La, Lb, ecs, tail, e_last = _pair_terms(
                    x_ref, dtx_ref, csx_ref, csr_ref, pair, ppg, lo_half, causal)
                xs = X * dt_p
                xsb = xs.astype(BF16)
                Ma, Mb = CB * La, CB * Lb
                dY = dy_ref[:, cols].astype(F32)
                dYb = dY.astype(BF16)
                dMa = _dot(jnp.where(lo_half, dY, 0.0).astype(BF16), xsb, NT)
                dMb = _dot(jnp.where(lo_half, 0.0, dY).astype(BF16), xsb, NT)
                dSa, dSb = dMa * Ma, dMb * Mb
                dCB = dCB + dMa * La + dMb * Lb
                dcs = jnp.where(lo_half, jnp.sum(dSa, axis=1, keepdims=True), jnp.sum(dSb, axis=1, keepdims=True)) * inv_p
                dcs_rows = dcs_rows - jnp.where(head_row == ra, jnp.sum(dSa, axis=0, keepdims=True), 0.0)
                dcs_rows = dcs_rows - jnp.where(head_row == ra + 1, jnp.sum(dSb, axis=0, keepdims=True), 0.0)
                dxs = jnp.where(lo_half, _dot(Ma.astype(BF16), dYb, TN), _dot(Mb.astype(BF16), dYb, TN))
                ST = st_ref[0, 0, pair]
                STb = ST.astype(BF16)
                dYe = (dY * ecs).astype(BF16)
                dC = dC + _dot(dYe, STb, NT)
                dSTp = _dot(Cc, dYe, TN)
                dcs = dcs + dY * (ecs * _dot(Cc, STb))
                dSTn = dH[pair]
                dSTnb = dSTn.astype(BF16)
                dSTp = dSTp + e_last * dSTn
                XBt = _dot(Bc, dSTnb)
                dxs = dxs + tail * XBt
                t2 = xs * XBt * tail
                at_end = e_last * jnp.sum(dSTn * ST, axis=0, keepdims=True) + jnp.sum(t2, axis=0, keepdims=True)
                dcs = dcs - t2 + jnp.where(last_row, at_end, 0.0)
                dB = dB + _dot((xs * tail).astype(BF16), dSTnb, NT)
                dx_ref[:, cols] = (dxs * dt_p + d_ref[:, cols] * dY).astype(BF16)
                ddtx_ref[:, cols] = (dxs * X).astype(BF16)
                dcsx_ref[:, cols] = dcs
                dd_ref[0, :, cols] += jnp.sum(dY * X, axis=0, keepdims=True)
                dH[pair] = dSTp
            dCBb = dCB.astype(BF16)
            dx_ref[:, DI + g * N:DI + (g + 1) * N] = (dB + _dot(dCBb, Cc, TN)).astype(BF16)
            dx_ref[:, DI + G * N + g * N:DI + G * N + (g + 1) * N] = (dC + _dot(dCBb, Bc)).astype(BF16)
        dcsr_ref[0] = dcs_rows

    rev = lambda b, c: (b * nc + (nc - 1 - c), 0)
    wide = pl.BlockSpec((L, DI), rev)
    hrow = pl.BlockSpec((1, HR, L), lambda b, c: (b, 0, nc - 1 - c))
    return pl.pallas_call(
        body, name=name, grid=(B, nc),
        in_specs=[pl.BlockSpec((L, XW), rev), wide, wide, hrow,
                  pl.BlockSpec((1, DI), lambda b, c: (0, 0)),
                  pl.BlockSpec((1, 1, n_pairs, N, LANE), lambda b, c: (b, nc - 1 - c, 0, 0, 0)), wide],
        out_specs=[pl.BlockSpec((L, XW), rev), wide, wide, hrow, pl.BlockSpec((1, 1, DI), lambda b, c: (b, 0, 0))],
        out_shape=[jax.ShapeDtypeStruct((B * seq, XW), BF16), jax.ShapeDtypeStruct((B * seq, DI), BF16),
                   jax.ShapeDtypeStruct((B * seq, DI), F32), jax.ShapeDtypeStruct((B, HR, seq), F32),
                   jax.ShapeDtypeStruct((B, 1, DI), F32)],
        scratch_shapes=[pltpu.VMEM((n_pairs, N, LANE), F32)],
        compiler_params=_cp("parallel", "arbitrary"))(xbc, dt_x, cs_x, cs_row, d_full, states, dy)


def gnorm_fwd(y, proj, w, *, DI, name):
    T = y.shape[0]
    tt = _pick(T, 256, 16)
    gw = DI // SSM_GROUPS

    def body(y_ref, z_ref, w_ref, o_ref):
        for g in range(SSM_GROUPS):
            sl = slice(g * gw, (g + 1) * gw)
            y2 = y_ref[:, sl].astype(F32) * _silu(z_ref[:, sl].astype(F32))
            r = lax.rsqrt(jnp.mean(y2 * y2, axis=-1, keepdims=True) + EPS)
            o_ref[:, sl] = (y2 * r * w_ref[:, sl]).astype(BF16)

    row = pl.BlockSpec((tt, DI), lambda i: (i, 0))
    return pl.pallas_call(
        body, name=name, grid=(T // tt,),
        in_specs=[row, row, pl.BlockSpec((1, DI), lambda i: (0, 0))], out_specs=row,
        out_shape=jax.ShapeDtypeStruct((T, DI), BF16), compiler_params=_cp("parallel"))(y, proj, w)


def gnorm_bwd(dyn, y, proj, w, *, DI, name):
    T = y.shape[0]
    tt = _pick(T, 256, 16)
    gw = DI // SSM_GROUPS

    def body(dyn_ref, y_ref, z_ref, w_ref, dy_ref, dz_ref, dw_ref):
        @pl.when(pl.program_id(0) == 0)
        def _():
            dw_ref[...] = jnp.zeros_like(dw_ref)

        for g in range(SSM_GROUPS):
            sl = slice(g * gw, (g + 1) * gw)
            yv = y_ref[:, sl].astype(F32)
            z = z_ref[:, sl].astype(F32)
            sz = _silu(z)
            y2 = yv * sz
            r = lax.rsqrt(jnp.mean(y2 * y2, axis=-1, keepdims=True) + EPS)
            xhat = y2 * r
            d = dyn_ref[:, sl].astype(F32)
            dw_ref[:, sl] += jnp.sum(d * xhat, axis=0, keepdims=True)
            dxh = d * w_ref[:, sl]
            dy2 = r * (dxh - xhat * jnp.mean(dxh * xhat, axis=-1, keepdims=True))
            dy_ref[:, sl] = (dy2 * sz).astype(BF16)
            dz_ref[:, sl] = (dy2 * yv * _silu_grad(z)).astype(BF16)

    row = pl.BlockSpec((tt, DI), lambda i: (i, 0))
    vec = pl.BlockSpec((1, DI), lambda i: (0, 0))
    shp = jax.ShapeDtypeStruct((T, DI), BF16)
    return pl.pallas_call(
        body, name=name, grid=(T // tt,), in_specs=[row, row, row, vec], out_specs=[row, row, vec],
        out_shape=[shp, shp, jax.ShapeDtypeStruct((1, DI), F32)],
        compiler_params=_cp("arbitrary"))(dyn, y, proj, w)


N_CHIP = 4


def _comm_out_shapes(srcs, modes):
    return [jax.ShapeDtypeStruct(((N_DEV,) if mode in ("gather", "gather_direct") else ()) + s.shape, s.dtype)
            for s, mode in zip(srcs, modes)]


def _comm_scratch(n):
    return [pltpu.SemaphoreType.DMA((n, N_DEV - 1)), pltpu.SemaphoreType.DMA((n, N_DEV - 1)),
            pltpu.SemaphoreType.DMA((n,))]


def _comm_phases(modes, src_refs, out_refs, send_sems, recv_sems, local_sems):
    x, y, c = lax.axis_index("x"), lax.axis_index("y"), lax.axis_index("c")
    me, sibling = (x, y, c), (x, y, 1 - c)
    chips = [(1 - x, y), (x, 1 - y), (1 - x, 1 - y)]
    relays = [a for a, mode in enumerate(modes) if mode == "gather"]

    def slot(p):
        return 4 * p[0] + 2 * p[1] + p[2]

    def remote(a, k, src, dst, to):
        return pltpu.make_async_remote_copy(src_ref=src, dst_ref=dst, send_sem=send_sems.at[a, k],
                                            recv_sem=recv_sems.at[a, k], device_id=to,
                                            device_id_type=pl.DeviceIdType.MESH)

    def first_copies():
        local, two_way, send_only = [], [], []
        for a, mode in enumerate(modes):
            src, out = src_refs[a], out_refs[a]
            if mode == "sibling":
                two_way.append(remote(a, 0, src, out, sibling))
            elif mode == "chips":
                mine = 2 * x + y
                local.append(pltpu.make_async_copy(src.at[mine], out.at[mine], local_sems.at[a]))
                for j, chip in enumerate(chips):
                    two_way.append(remote(a, 1 + j, src.at[2 * chip[0] + chip[1]], out.at[mine], (*chip, c)))
            elif mode == "gather_direct":
                local.append(pltpu.make_async_copy(src, out.at[slot(me)], local_sems.at[a]))
                for k in range(1, N_DEV):
                    peer = (1 - x if k & 4 else x, 1 - y if k & 2 else y, 1 - c if k & 1 else c)
                    two_way.append(remote(a, k - 1, src, out.at[slot(me)], peer))
            else:
                assert mode == "gather"
                local.append(pltpu.make_async_copy(src, out.at[slot(me)], local_sems.at[a]))
                send_only.append(remote(a, 0, src, out.at[slot(me)], sibling))
                for j, chip in enumerate(chips):
                    send_only.append(remote(a, 1 + j, src, out.at[slot(me)], (*chip, c)))
        return local, two_way, send_only

    def forwards():
        out = []
        for a in relays:
            for j, chip in enumerate(chips):
                landed = out_refs[a].at[slot((*chip, c))]
                out.append((remote(a, 1 + j, landed, landed, me), remote(a, 4 + j, landed, landed, sibling)))
        return out

    def start():
        local, two_way, send_only = first_copies()
        for cp in local + two_way + send_only:
            cp.start()

    def relay():
        for arrival, fwd in forwards():
            arrival.wait_recv()
            fwd.start()

    def finish():
        local, two_way, send_only = first_copies()
        for a in relays:
            blk = out_refs[a].at[slot(sibling)]
            remote(a, 0, blk, blk, me).wait_recv()
            for j, chip in enumerate(chips):
                blk = out_refs[a].at[slot((*chip, 1 - c))]
                remote(a, 4 + j, blk, blk, me).wait_recv()
        for cp in send_only + [fwd for _, fwd in forwards()]:
            cp.wait_send()
        for cp in two_way + local:
            cp.wait()

    return start, relay, finish, bool(relays)


def _hosted(body, n_in, n_out, comm, step, n_steps):
    if comm is None:
        return body, [], [], [], [], []
    srcs, modes = comm
    nc = len(srcs)

    def wrapped(*refs):
        ins, csrc = refs[:n_in], refs[n_in:n_in + nc]
        outs = refs[n_in + nc:n_in + nc + n_out]
        cout = refs[n_in + nc + n_out:n_in + 2 * nc + n_out]
        scratch = refs[n_in + 2 * nc + n_out:len(refs) - 3]
        start, relay, finish, has_relay = _comm_phases(modes, csrc, cout, *refs[len(refs) - 3:])
        s = step()
        pl.when(s == 0)(start)
        body(*ins, *outs, *scratch)
        if has_relay:
            pl.when(s == (2 * n_steps) // 3)(relay)
        pl.when(s == n_steps - 1)(finish)

    any_spec = pl.BlockSpec(memory_space=pl.ANY)
    return wrapped, [any_spec] * nc, list(srcs), [any_spec] * nc, _comm_out_shapes(srcs, modes), _comm_scratch(nc)


def exchange(srcs, modes, *, name):
    n = len(srcs)

    def body(*refs):
        start, relay, finish, has_relay = _comm_phases(modes, refs[:n], refs[n:2 * n], *refs[2 * n:])
        start()
        if has_relay:
            relay()
        finish()

    any_spec = pl.BlockSpec(memory_space=pl.ANY)
    return pl.pallas_call(
        body, name=name, in_specs=[any_spec] * n, out_specs=[any_spec] * n, out_shape=_comm_out_shapes(srcs, modes),
        scratch_shapes=_comm_scratch(n), compiler_params=pltpu.CompilerParams(has_side_effects=True))(*srcs)


def pair_sum(a, b, *, name):
    n, R, C = a.shape
    tr = _pick(n * R, 512, 16)

    def body(a_ref, b_ref, o_ref):
        o_ref[...] = (a_ref[...].astype(F32) + b_ref[...].astype(F32)).astype(BF16)

    blk = pl.BlockSpec((tr, C), lambda i: (i, 0))
    out = pl.pallas_call(
        body, name=name, grid=(n * R // tr,), in_specs=[blk, blk], out_specs=blk,
        out_shape=jax.ShapeDtypeStruct((n * R, C), BF16),
        compiler_params=_cp("parallel"))(a.reshape(n * R, C), b.reshape(n * R, C))
    return out.reshape(n, R, C)


def sum_slots(recv, *, name):
    _, R, C = recv.shape
    tr = _pick(R, 512, 8)

    def body(r_ref, o_ref):
        acc = r_ref[0].astype(F32)
        for p in range(1, N_DEV):
            acc = acc + r_ref[p].astype(F32)
        o_ref[...] = acc

    return pl.pallas_call(
        body, name=name, grid=(R // tr,),
        in_specs=[pl.BlockSpec((N_DEV, tr, C), lambda i: (0, i, 0))],
        out_specs=pl.BlockSpec((tr, C), lambda i: (i, 0)),
        out_shape=jax.ShapeDtypeStruct((R, C), F32), compiler_params=_cp("parallel"))(recv)


def adamw(gsrc, w, m, v, *, name):
    slots, R, C = gsrc.shape
    tr = _pick(R, 256, 16 if gsrc.dtype == BF16 else 8)
    c1 = 1.0 / (1.0 - ADAM_B1 ** ADAM_STEP)
    c2 = 1.0 / (1.0 - ADAM_B2 ** ADAM_STEP)

    def body(g_ref, w_ref, m_ref, v_ref, go_ref, d_ref, mo_ref, vo_ref):
        g = g_ref[0].astype(F32)
        for p in range(1, slots):
            g = g + g_ref[p].astype(F32)
        m2 = ADAM_B1 * m_ref[...] + (1.0 - ADAM_B1) * g
        v2 = ADAM_B2 * v_ref[...] + (1.0 - ADAM_B2) * (g * g)
        go_ref[...] = g
        mo_ref[...] = m2
        vo_ref[...] = v2
        d_ref[...] = -ADAM_LR * ((m2 * c1) / (jnp.sqrt(v2 * c2) + ADAM_EPS) + ADAM_WD * w_ref[...])

    blk = pl.BlockSpec((tr, C), lambda i: (i, 0))
    shp = jax.ShapeDtypeStruct((R, C), F32)
    return pl.pallas_call(
        body, name=name, grid=(R // tr,),
        in_specs=[pl.BlockSpec((slots, tr, C), lambda i: (0, i, 0)), blk, blk, blk],
        out_specs=[blk] * 4, out_shape=[shp] * 4, compiler_params=_cp("parallel"))(gsrc, w, m, v)


def _pad_cols(a, n):
    return jnp.pad(a, ((0, 0), (0, n - a.shape[1])))


def _to_rows(a, B, seq, H):
    G = SSM_GROUPS
    R = H // G
    t = a[:, :H].reshape(B, seq, G, R).transpose(0, 2, 3, 1)
    t = jnp.pad(t, ((0, 0), (0, 0), (0, HEAD_ROWS - R), (0, 0)))
    return t.reshape(B, G * HEAD_ROWS, seq)


def _from_rows(a, B, seq, H):
    G = SSM_GROUPS
    R = H // G
    t = a.reshape(B, G, HEAD_ROWS, seq)[:, :, :R].transpose(0, 3, 1, 2).reshape(B * seq, H)
    return _pad_cols(t, LANE)


def _chip_sums(grads, name):
    c_idx = lax.axis_index("c")
    keep, give = [], []
    for g in grads:
        by_chip = g.reshape((N_CHIP, 2) + g.shape[1:])
        keep.append(lax.dynamic_index_in_dim(by_chip, c_idx, axis=1, keepdims=False))
        give.append(lax.dynamic_index_in_dim(by_chip, 1 - c_idx, axis=1, keepdims=False))
    swapped = exchange(give, ["sibling"] * len(give), name="swap_" + name)
    return [pair_sum(k, s, name=f"chip_sum_{name}_{i}") for i, (k, s) in enumerate(zip(keep, swapped))]


def local_step(x, target, p, l1_shards, *, B, seq):
    T, D = x.shape
    CW = D
    heads = CW // SB_HEAD_DIM
    DI = 2 * D
    H = DI // SSM_HEAD_DIM
    XW = DI + 2 * SSM_GROUPS * SSM_STATE
    in_odd = DI + XW + H
    w1_rows = in_odd // N_DEV
    q_off, k_off, v_off, gc_off, ga_off = 3 * CW, 4 * CW, 5 * CW, 2 * CW, 6 * CW
    dt_off = DI + XW

    n0 = rmsnorm_fwd(x, p["ev_norm_w"], name="l0_norm")
    proj0, (g_od_in_t, g_od_out) = mm_nn(n0, p["ev_w_in"], out_dtype=BF16, name="l0_in_proj",
                                         comm=(l1_shards, ["gather", "gather"]))
    w1t = g_od_in_t[:, :w1_rows].reshape(in_odd, D)
    w1t = jnp.pad(w1t, ((0, -(-(in_odd + LANE) // 256) * 256 - in_odd), (0, 0)))
    od_w_out = g_od_out.reshape(-1, D)
    (u2,) = dwconv_fwd(proj0, (0, CW), p["ev_dw_w"], p["ev_dw_b"], C=CW, seq=seq, glu=True, silu_out=False,
                       name="l0_conv")
    o, ctot = sba_fwd(proj0, B=B, seq=seq, heads=heads, q_off=q_off, k_off=k_off, v_off=v_off, name="l0_attn")
    ycat = mix0_post_fwd(u2, proj0, o, p["ev_ln_w"], p["ev_ln_b"], CW=CW, gc_off=gc_off, ga_off=ga_off,
                         name="l0_post")
    h1 = mm_nn(ycat, p["ev_w_out"], add=x, out_dtype=F32, name="l0_out_proj")

    n1 = rmsnorm_fwd(h1, p["od_norm_w"], name="l1_norm")
    proj1 = mm_nt_terms([(n1, 0, D, 0)], w1t, out_dtype=BF16, name="l1_in_proj")
    u_pre, xbc = dwconv_fwd(proj1, (DI,), p["od_conv_w"], p["od_conv_b"], C=XW, seq=seq, glu=False, silu_out=True,
                            name="l1_conv")
    bias_p, alog_p = _pad_cols(p["od_dt_bias"], LANE), _pad_cols(p["od_a_log"], LANE)
    expand = _head_expand(H, DI)
    dt, cs, dt_x, cs_x = dt_fwd(proj1, bias_p, alog_p, expand, dt_off=dt_off, name="l1_dt")
    cs_row = _to_rows(cs, B, seq, H)
    d_full = jnp.repeat(p["od_d"], SSM_HEAD_DIM, axis=1)
    y_ssd, states = scan_fwd(xbc, dt_x, cs_x, cs_row, d_full, B=B, seq=seq, DI=DI, name="l1_ssd")
    yn = gnorm_fwd(y_ssd, proj1, p["od_gnorm_w"], DI=DI, name="l1_gnorm")
    h2 = mm_nn(yn, od_w_out, add=h1, out_dtype=F32, name="l1_out_proj")

    loss, dh2, dh2b, g_final = final_loss(h2, p["final_norm_w"], target, name="loss_head")

    g_od_w_out = mm_tn(yn, dh2b, out_dtype=BF16, name="l1_dw_out")
    dyn = mm_nt_terms([(dh2b, 0, D, 0)], od_w_out, out_dtype=BF16, name="l1_d_out_proj")
    dy_ssd, dz, g_gnorm = gnorm_bwd(dyn, y_ssd, proj1, p["od_gnorm_w"], DI=DI, name="l1_gnorm_bwd")
    dxbc_c, ddt_x, dcs_x, dcs_row, dd_part = scan_bwd(xbc, dt_x, cs_x, cs_row, d_full, states, dy_ssd, B=B, seq=seq,
                                                      DI=DI, name="l1_ssd_bwd")
    g_d = dd_part.sum(axis=(0, 1)).reshape(H, SSM_HEAD_DIM).sum(axis=1)[None, :]
    draw, g_bias, g_alog = dt_bwd(ddt_x, dcs_x, _from_rows(dcs_row, B, seq, H), proj1, dt, bias_p, alog_p, expand.T,
                                  dt_off=dt_off, n_heads=H, name="l1_dt_bwd")
    dxbc, g_conv_w, g_conv_b = dwconv_bwd(dxbc_c, u_pre, proj1, (DI,), p["od_conv_w"], C=XW, seq=seq, glu=False,
                                          silu_out=True, name="l1_conv_bwd")
    tw = 512 if DI % 512 == 0 else LANE
    terms = [(dz, j, tw, j * tw) for j in range(DI // tw)]
    terms += [(dxbc, j, tw, DI + j * tw) for j in range(XW // tw)]
    terms += [(draw, 0, LANE, dt_off)]
    dn1 = mm_nn_terms(terms, w1t, out_dtype=F32, name="l1_d_in_proj")
    g_od_w_in_t = jnp.concatenate([mm_tn(dz, n1, out_dtype=BF16, name="l1_dw_in_z"),
                                   mm_tn(dxbc, n1, out_dtype=BF16, name="l1_dw_in_xbc"),
                                   mm_tn(draw, n1, out_dtype=BF16, name="l1_dw_in_dt")], axis=0)[:in_odd]
    dh1, dh1b, g_od_norm = rmsnorm_bwd(h1, p["od_norm_w"], dn1, dh2, name="l1_norm_bwd")
    w1_pad = (-w1_rows) % 16
    l1_chip = _chip_sums([jnp.pad(g_od_w_in_t.reshape(N_DEV, w1_rows, D), ((0, 0), (0, w1_pad), (0, 0))),
                          g_od_w_out.reshape(N_DEV, -1, D)], "l1")

    g_ev_w_out = mm_tn(ycat, dh1b, out_dtype=BF16, name="l0_dw_out")
    dycat = mm_nt_terms([(dh1b, 0, D, 0)], p["ev_w_out"], out_dtype=BF16, name="l0_d_out_proj")
    du2, dgc, dga, do, g_ln_w, g_ln_b = mix0_post_bwd(dycat, u2, proj0, o, p["ev_ln_w"], p["ev_ln_b"], CW=CW,
                                                      gc_off=gc_off, ga_off=ga_off, name="l0_post_bwd")
    dq, dk, dv, (r_od_in_t, r_od_out) = sba_bwd(proj0, ctot, do, B=B, seq=seq, heads=heads, q_off=q_off, k_off=k_off,
                                                v_off=v_off, name="l0_attn_bwd", comm=(l1_chip, ["chips", "chips"]))
    dga_a, dga_b, g_dw_w, g_dw_b = dwconv_bwd(du2, None, proj0, (0, CW), p["ev_dw_w"], C=CW, seq=seq, glu=True,
                                              silu_out=False, name="l0_conv_bwd")
    pieces = [dga_a, dga_b, dgc, dq, dk, dv, dga]
    g_ev_w_in = jnp.concatenate([mm_tn(n0, pc, out_dtype=BF16, name=f"l0_dw_in_{j}") for j, pc in enumerate(pieces)],
                                axis=1)
    l0_chip = _chip_sums([_col_shards(g_ev_w_in), g_ev_w_out.reshape(N_DEV, -1, D)], "l0")
    dn0, (r_ev_in, r_ev_out) = mm_nt_terms([(pc, 0, CW, j * CW) for j, pc in enumerate(pieces)], p["ev_w_in"],
                                           out_dtype=F32, name="l0_d_in_proj", comm=(l0_chip, ["chips", "chips"]))
    dx, _, g_ev_norm = rmsnorm_bwd(x, p["ev_norm_w"], dn0, dh1, name="l0_norm_bwd")

    small = dict(ev_norm_w=g_ev_norm, ev_dw_w=g_dw_w, ev_dw_b=g_dw_b, ev_ln_w=g_ln_w, ev_ln_b=g_ln_b,
                 od_norm_w=g_od_norm, od_conv_w=g_conv_w, od_conv_b=g_conv_b, od_dt_bias=g_bias[:, :H],
                 od_a_log=g_alog[:, :H], od_d=g_d, od_gnorm_w=g_gnorm, final_norm_w=g_final)
    received = dict(ev_w_in=r_ev_in, ev_w_out=r_ev_out, od_w_in=r_od_in_t, od_w_out=r_od_out)
    return loss, dx, small, received


BIG = ("ev_w_in", "ev_w_out", "od_w_in", "od_w_out")
SMALL = ("ev_norm_w", "ev_dw_w", "ev_dw_b", "ev_ln_w", "ev_ln_b", "od_norm_w", "od_conv_w", "od_conv_b",
         "od_dt_bias", "od_a_log", "od_d", "od_gnorm_w", "final_norm_w")
SMALL_SHARDED = ("ev_dw_w", "od_norm_w", "od_conv_w", "od_conv_b", "od_gnorm_w")
ORDER = ("ev_norm_w", "ev_w_in", "ev_dw_w", "ev_dw_b", "ev_ln_w", "ev_ln_b", "ev_w_out", "od_norm_w", "od_w_in",
         "od_conv_w", "od_conv_b", "od_dt_bias", "od_a_log", "od_d", "od_gnorm_w", "od_w_out", "final_norm_w")


def _pack_rows(arrs, width, row_align):
    parts, spans, r0 = [], [], 0
    for a in arrs:
        flat = a.reshape(-1)
        rows = -(-flat.shape[0] // (width * row_align)) * row_align
        parts.append(jnp.pad(flat, (0, rows * width - flat.shape[0])).reshape(rows, width))
        spans.append((r0, a.size, a.shape))
        r0 += rows
    return jnp.concatenate(parts, axis=0), spans


def _unpack_rows(packed, spans):
    lead = packed.shape[:-2]
    width = packed.shape[-1]
    out = []
    for r0, size, shape in spans:
        rows = -(-size // width)
        blk = packed[..., r0:r0 + rows, :].reshape(lead + (rows * width,))[..., :size]
        out.append(blk.reshape(lead + tuple(shape)))
    return out


def _col_shards(a):
    R, C8 = a.shape
    return a.reshape(R, N_DEV, C8 // N_DEV).transpose(1, 0, 2)


def _col_unshards(a):
    n, R, C = a.shape
    return a.transpose(1, 0, 2).reshape(R, n * C)


def kernel(x, ev_norm_w, ev_w_in, ev_dw_w, ev_dw_b, ev_ln_w, ev_ln_b, ev_w_out, od_norm_w, od_w_in, od_conv_w, od_conv_b, od_dt_bias, od_a_log, od_d, od_gnorm_w, od_w_out, final_norm_w, loss_target, m_ev_norm_w, m_ev_w_in, m_ev_dw_w, m_ev_dw_b, m_ev_ln_w, m_ev_ln_b, m_ev_w_out, m_od_norm_w, m_od_w_in, m_od_conv_w, m_od_conv_b, m_od_dt_bias, m_od_a_log, m_od_d, m_od_gnorm_w, m_od_w_out, m_final_norm_w, v_ev_norm_w, v_ev_w_in, v_ev_dw_w, v_ev_dw_b, v_ev_ln_w, v_ev_ln_b, v_ev_w_out, v_od_norm_w, v_od_w_in, v_od_conv_w, v_od_conv_b, v_od_dt_bias, v_od_a_log, v_od_d, v_od_gnorm_w, v_od_w_out, v_final_norm_w):
    loc = dict(ev_norm_w=ev_norm_w, ev_w_in=ev_w_in, ev_dw_w=ev_dw_w, ev_dw_b=ev_dw_b, ev_ln_w=ev_ln_w,
               ev_ln_b=ev_ln_b, ev_w_out=ev_w_out, od_norm_w=od_norm_w, od_w_in=od_w_in, od_conv_w=od_conv_w,
               od_conv_b=od_conv_b, od_dt_bias=od_dt_bias, od_a_log=od_a_log, od_d=od_d, od_gnorm_w=od_gnorm_w,
               od_w_out=od_w_out, final_norm_w=final_norm_w)
    mom = dict(ev_norm_w=m_ev_norm_w, ev_w_in=m_ev_w_in, ev_dw_w=m_ev_dw_w, ev_dw_b=m_ev_dw_b, ev_ln_w=m_ev_ln_w,
               ev_ln_b=m_ev_ln_b, ev_w_out=m_ev_w_out, od_norm_w=m_od_norm_w, od_w_in=m_od_w_in,
               od_conv_w=m_od_conv_w, od_conv_b=m_od_conv_b, od_dt_bias=m_od_dt_bias, od_a_log=m_od_a_log,
               od_d=m_od_d, od_gnorm_w=m_od_gnorm_w, od_w_out=m_od_w_out, final_norm_w=m_final_norm_w)
    var = dict(ev_norm_w=v_ev_norm_w, ev_w_in=v_ev_w_in, ev_dw_w=v_ev_dw_w, ev_dw_b=v_ev_dw_b, ev_ln_w=v_ev_ln_w,
               ev_ln_b=v_ev_ln_b, ev_w_out=v_ev_w_out, od_norm_w=v_od_norm_w, od_w_in=v_od_w_in,
               od_conv_w=v_od_conv_w, od_conv_b=v_od_conv_b, od_dt_bias=v_od_dt_bias, od_a_log=v_od_a_log,
               od_d=v_od_d, od_gnorm_w=v_od_gnorm_w, od_w_out=v_od_w_out, final_norm_w=v_final_norm_w)
    shapes = {n: loc[n].shape for n in ORDER}
    loc = {n: (a.reshape(1, -1) if a.ndim == 1 else a.reshape(a.shape[-2:]) if a.ndim == 3 else a)
           for n, a in loc.items()}
    mom = {n: a.reshape(loc[n].shape) for n, a in mom.items()}
    var = {n: a.reshape(loc[n].shape) for n, a in var.items()}

    B, seq, D = x.shape
    me = 4 * lax.axis_index("x") + 2 * lax.axis_index("y") + lax.axis_index("c")

    w1_rows = loc["od_w_in"].shape[1]
    w1_pad = (-w1_rows) % 16

    def to_t(a):
        return jnp.pad(a.T, ((0, w1_pad), (0, 0)))

    small_packed, small_spans = _pack_rows([loc[n] for n in SMALL_SHARDED], LANE, 8)
    g_ev_in, g_ev_out, small_all = exchange(
        [loc["ev_w_in"].astype(BF16), loc["ev_w_out"].astype(BF16), small_packed],
        ["gather", "gather", "gather_direct"], name="gather_weights")
    full = {n: loc[n] for n in SMALL}
    full["ev_w_in"] = _col_unshards(g_ev_in)
    full["ev_w_out"] = g_ev_out.reshape(-1, D)
    for n, a in zip(SMALL_SHARDED, _unpack_rows(small_all, small_spans)):
        full[n] = _col_unshards(a)

    loss, dx, grads, received = local_step(
        x.reshape(B * seq, D), loss_target.reshape(B * seq, D), full,
        [to_t(loc["od_w_in"]).astype(BF16), loc["od_w_out"].astype(BF16)], B=B, seq=seq)

    gsmall_packed, gsmall_spans = _pack_rows([grads[n] for n in SMALL], LANE, 8)
    (gsmall_recv,) = exchange([gsmall_packed], ["gather_direct"], name="gather_small_grads")

    big_out = [{} for _ in range(4)]
    for n in ("ev_w_in", "ev_w_out", "od_w_out"):
        for kind, a in enumerate(adamw(received[n], loc[n], mom[n], var[n], name="adamw_" + n)):
            big_out[kind][n] = a
    for kind, a in enumerate(adamw(received["od_w_in"], to_t(loc["od_w_in"]), to_t(mom["od_w_in"]),
                                   to_t(var["od_w_in"]), name="adamw_od_w_in")):
        big_out[kind]["od_w_in"] = a[:w1_rows].T

    gsmall = dict(zip(SMALL, _unpack_rows(sum_slots(gsmall_recv, name="sum_small_grads"), gsmall_spans)))
    for n in SMALL_SHARDED:
        width = loc[n].shape[1]
        gsmall[n] = lax.dynamic_slice_in_dim(gsmall[n], me * width, width, axis=1)
    gs, sspans = _pack_rows([gsmall[n] for n in SMALL], LANE, 8)
    ws, _ = _pack_rows([loc[n] for n in SMALL], LANE, 8)
    ms, _ = _pack_rows([mom[n] for n in SMALL], LANE, 8)
    vs, _ = _pack_rows([var[n] for n in SMALL], LANE, 8)
    small_out = [dict(zip(SMALL, _unpack_rows(a, sspans))) for a in adamw(gs[None], ws, ms, vs, name="adamw_small")]

    outs = [lax.psum(loss[0, 0], ("x", "y", "c")), dx.reshape(B, seq, D)]
    for kind in range(4):
        for n in ORDER:
            src = big_out[kind] if n in BIG else small_out[kind]
            outs.append(src[n].reshape(shapes[n]))
    return tuple(outs)
```

```python
import functools

import jax
import jax.numpy as jnp
from jax import lax
from jax.experimental import pallas as pl
from jax.experimental.pallas import tpu as pltpu

F32 = jnp.float32
BF16 = jnp.bfloat16

EPS = 1e-6
N_DEV = 8
LANE = 128
VMEM_LIMIT_BYTES = 48 * 1024 * 1024

SB_HEAD_DIM = 128
CONF_KERNEL = 31
SSM_CONV = 4
SSM_HEAD_DIM = 64
SSM_GROUPS = 4
SSM_STATE = 128
SSM_CHUNK = 128
HALO = 32
HEAD_ROWS = 8
NEG_BIG = -1e30

ADAM_LR = 0.001
ADAM_B1 = 0.9
ADAM_B2 = 0.999
ADAM_EPS = 1e-08
ADAM_WD = 0.01
ADAM_STEP = 10

NT = (((1,), (1,)), ((), ()))
TN = (((0,), (0,)), ((), ()))


def _cp(*sem):
    return pltpu.CompilerParams(dimension_semantics=sem, vmem_limit_bytes=VMEM_LIMIT_BYTES)


def _pick(n, cap, align):
    if n <= cap:
        return n
    t = (cap // align) * align
    while t >= align:
        if n % t == 0:
            return t
        t -= align
    raise ValueError(f"no tile for {n} (cap {cap}, align {align})")


def _sigmoid(x):
    return 0.5 * jnp.tanh(0.5 * x) + 0.5


def _silu(x):
    return x * _sigmoid(x)


def _silu_grad(x):
    s = _sigmoid(x)
    return s * (1.0 + x * (1.0 - s))


def _dot(a, b, dims=None):
    if dims is None:
        return jnp.dot(a, b, preferred_element_type=F32)
    return lax.dot_general(a, b, dims, preferred_element_type=F32)


def _tri_dot3(tri, x):
    hi = x.astype(BF16)
    r1 = x - hi.astype(F32)
    mid = r1.astype(BF16)
    lo = (r1 - mid.astype(F32)).astype(BF16)
    return _dot(tri, hi) + _dot(tri, mid) + _dot(tri, lo)


def _grid_step(n_inner):
    return lambda: pl.program_id(0) * n_inner + pl.program_id(1)


def mm_nn(a, b, *, add=None, out_dtype, name, comm=None):
    M, K = a.shape
    N = b.shape[1]
    tm = _pick(M, 1024, 16)
    tn = _pick(N, 1024, LANE)

    def body(*refs):
        if add is None:
            a_ref, b_ref, o_ref = refs
        else:
            a_ref, b_ref, add_ref, o_ref = refs
        acc = _dot(a_ref[...], b_ref[...])
        if add is not None:
            acc = acc + add_ref[...]
        o_ref[...] = acc.astype(out_dtype)

    in_specs = [pl.BlockSpec((tm, K), lambda i, j: (i, 0)), pl.BlockSpec((K, tn), lambda i, j: (0, j))]
    args = [a, b]
    if add is not None:
        in_specs.append(pl.BlockSpec((tm, tn), lambda i, j: (i, j)))
        args.append(add)
    grid = (M // tm, N // tn)
    body, c_in, c_args, c_out, c_shape, c_scratch = _hosted(body, len(args), 1, comm, _grid_step(grid[1]),
                                                            grid[0] * grid[1])
    out = pl.pallas_call(
        body, name=name, grid=grid, in_specs=in_specs + c_in,
        out_specs=[pl.BlockSpec((tm, tn), lambda i, j: (i, j))] + c_out,
        out_shape=[jax.ShapeDtypeStruct((M, N), out_dtype)] + c_shape, scratch_shapes=c_scratch,
        compiler_params=_cp(*(("arbitrary",) * 2 if comm else ("parallel",) * 2)))(*args, *c_args)
    return (out[0], out[1:]) if comm else out[0]


def mm_nt_terms(terms, b, *, out_dtype, name, comm=None):
    M = terms[0][0].shape[0]
    N = b.shape[0]
    n_terms = len(terms)
    cap = 1024 if n_terms == 1 else 512
    tm = _pick(M, cap, 16)
    tn = _pick(N, cap, LANE)

    def body(*refs):
        o_ref = refs[-1]
        acc = None
        for t in range(n_terms):
            part = _dot(refs[2 * t][...], refs[2 * t + 1][...], NT)
            acc = part if acc is None else acc + part
        o_ref[...] = acc.astype(out_dtype)

    in_specs, args = [], []
    for arr, cb, w, off in terms:
        assert off % w == 0
        in_specs.append(pl.BlockSpec((tm, w), lambda i, j, cb=cb: (i, cb)))
        in_specs.append(pl.BlockSpec((tn, w), lambda i, j, ob=off // w: (j, ob)))
        args += [arr, b]
    grid = (M // tm, N // tn)
    body, c_in, c_args, c_out, c_shape, c_scratch = _hosted(body, len(args), 1, comm, _grid_step(grid[1]),
                                                            grid[0] * grid[1])
    out = pl.pallas_call(
        body, name=name, grid=grid, in_specs=in_specs + c_in,
        out_specs=[pl.BlockSpec((tm, tn), lambda i, j: (i, j))] + c_out,
        out_shape=[jax.ShapeDtypeStruct((M, N), out_dtype)] + c_shape, scratch_shapes=c_scratch,
        compiler_params=_cp(*(("arbitrary",) * 2 if comm else ("parallel",) * 2)))(*args, *c_args)
    return (out[0], out[1:]) if comm else out[0]


def mm_nn_terms(terms, b, *, out_dtype, name):
    M = terms[0][0].shape[0]
    N = b.shape[1]
    tm = _pick(M, 512, 16)
    tn = _pick(N, 512, LANE)
    n_terms = len(terms)

    def body(*refs):
        o_ref = refs[-1]
        acc = None
        for t in range(n_terms):
            part = _dot(refs[2 * t][...], refs[2 * t + 1][...])
            acc = part if acc is None else acc + part
        o_ref[...] = acc.astype(out_dtype)

    in_specs, args = [], []
    for arr, cb, w, off in terms:
        assert off % w == 0
        in_specs.append(pl.BlockSpec((tm, w), lambda i, j, cb=cb: (i, cb)))
        in_specs.append(pl.BlockSpec((w, tn), lambda i, j, ob=off // w: (ob, j)))
        args += [arr, b]
    return pl.pallas_call(
        body, name=name, grid=(M // tm, N // tn), in_specs=in_specs,
        out_specs=pl.BlockSpec((tm, tn), lambda i, j: (i, j)),
        out_shape=jax.ShapeDtypeStruct((M, N), out_dtype),
        compiler_params=_cp("parallel", "parallel"))(*args)


def mm_tn(a, b, *, out_dtype, name):
    T, M = a.shape
    N = b.shape[1]
    tm = _pick(M, 1024, LANE)
    tn = _pick(N, 1024, LANE)
    tk = _pick(T, 1024, 16)
    nk = T // tk

    def body(a_ref, b_ref, o_ref, acc_ref):
        k = pl.program_id(2)

        @pl.when(k == 0)
        def _():
            acc_ref[...] = jnp.zeros_like(acc_ref)

        acc_ref[...] += _dot(a_ref[...], b_ref[...], TN)

        @pl.when(k == nk - 1)
        def _():
            o_ref[...] = acc_ref[...].astype(out_dtype)

    return pl.pallas_call(
        body, name=name, grid=(M // tm, N // tn, nk),
        in_specs=[pl.BlockSpec((tk, tm), lambda i, j, k: (k, i)), pl.BlockSpec((tk, tn), lambda i, j, k: (k, j))],
        out_specs=pl.BlockSpec((tm, tn), lambda i, j, k: (i, j)),
        out_shape=jax.ShapeDtypeStruct((M, N), out_dtype),
        scratch_shapes=[pltpu.VMEM((tm, tn), F32)],
        compiler_params=_cp("parallel", "parallel", "arbitrary"))(a, b)


def rmsnorm_fwd(h, w, *, name, comm=None):
    T, D = h.shape
    tt = _pick(T, 512, 16)

    def body(h_ref, w_ref, n_ref):
        x = h_ref[...]
        r = lax.rsqrt(jnp.mean(x * x, axis=-1, keepdims=True) + EPS)
        n_ref[...] = (x * r * w_ref[...]).astype(BF16)

    body, c_in, c_args, c_out, c_shape, c_scratch = _hosted(body, 2, 1, comm, lambda: pl.program_id(0), T // tt)
    out = pl.pallas_call(
        body, name=name, grid=(T // tt,),
        in_specs=[pl.BlockSpec((tt, D), lambda i: (i, 0)), pl.BlockSpec((1, D), lambda i: (0, 0))] + c_in,
        out_specs=[pl.BlockSpec((tt, D), lambda i: (i, 0))] + c_out,
        out_shape=[jax.ShapeDtypeStruct((T, D), BF16)] + c_shape, scratch_shapes=c_scratch,
        compiler_params=_cp("arbitrary" if comm else "parallel"))(h, w, *c_args)
    return (out[0], out[1:]) if comm else out[0]


def rmsnorm_bwd(h, w, dn, dres, *, name):
    T, D = h.shape
    tt = _pick(T, 512, 16)

    def body(h_ref, w_ref, dn_ref, dres_ref, dh_ref, dhb_ref, gw_ref):
        @pl.when(pl.program_id(0) == 0)
        def _():
            gw_ref[...] = jnp.zeros_like(gw_ref)

        x = h_ref[...]
        r = lax.rsqrt(jnp.mean(x * x, axis=-1, keepdims=True) + EPS)
        xhat = x * r
        g = dn_ref[...].astype(F32)
        gw_ref[...] += jnp.sum(g * xhat, axis=0, keepdims=True)
        dxh = g * w_ref[...]
        dx = r * (dxh - xhat * jnp.mean(dxh * xhat, axis=-1, keepdims=True))
        dh = dres_ref[...] + dx
        dh_ref[...] = dh
        dhb_ref[...] = dh.astype(BF16)

    row = pl.BlockSpec((tt, D), lambda i: (i, 0))
    vec = pl.BlockSpec((1, D), lambda i: (0, 0))
    return pl.pallas_call(
        body, name=name, grid=(T // tt,), in_specs=[row, vec, row, row], out_specs=[row, row, vec],
        out_shape=[jax.ShapeDtypeStruct((T, D), F32), jax.ShapeDtypeStruct((T, D), BF16),
                   jax.ShapeDtypeStruct((1, D), F32)],
        compiler_params=_cp("arbitrary"))(h, w, dn, dres)


def final_loss(h, w, target, *, name):
    T, D = h.shape
    tt = _pick(T, 512, 16)

    def body(h_ref, w_ref, t_ref, loss_ref, dh_ref, dhb_ref, gw_ref):
        @pl.when(pl.program_id(0) == 0)
        def _():
            gw_ref[...] = jnp.zeros_like(gw_ref)
            loss_ref[...] = jnp.zeros_like(loss_ref)

        x = h_ref[...]
        r = lax.rsqrt(jnp.mean(x * x, axis=-1, keepdims=True) + EPS)
        xhat = x * r
        e = xhat * w_ref[...] - t_ref[...]
        loss_ref[...] += jnp.sum(e * e) * (0.5 / D)
        g = e * (1.0 / D)
        gw_ref[...] += jnp.sum(g * xhat, axis=0, keepdims=True)
        dxh = g * w_ref[...]
        dh = r * (dxh - xhat * jnp.mean(dxh * xhat, axis=-1, keepdims=True))
        dh_ref[...] = dh
        dhb_ref[...] = dh.astype(BF16)

    row = pl.BlockSpec((tt, D), lambda i: (i, 0))
    vec = pl.BlockSpec((1, D), lambda i: (0, 0))
    one = pl.BlockSpec((1, LANE), lambda i: (0, 0))
    return pl.pallas_call(
        body, name=name, grid=(T // tt,), in_specs=[row, vec, row], out_specs=[one, row, row, vec],
        out_shape=[jax.ShapeDtypeStruct((1, LANE), F32), jax.ShapeDtypeStruct((T, D), F32),
                   jax.ShapeDtypeStruct((T, D), BF16), jax.ShapeDtypeStruct((1, D), F32)],
        compiler_params=_cp("arbitrary"))(h, w, target)


CONV_CHUNK = 32
SUBLANES = 8


def _conv_tiles(seq, C):
    return _pick(seq, 512, HALO), _pick(C, 512, LANE)


def _residues(offsets):
    return sorted({s % SUBLANES for s in offsets} - {0})


def _fill_shifted(buf, shifted, residues):
    n = buf.shape[0] - SUBLANES
    for i, r in enumerate(residues):
        shifted[i, 0:n, :] = buf[r:r + n, :]


def _tap(buf, shifted, residues, offset, start, rows):
    r = offset % SUBLANES
    base = offset - r
    ref = buf if r == 0 else shifted.at[residues.index(r)]
    return ref[pl.ds(start + base, rows), :]


def dwconv_fwd(src, offs, w, b, *, C, seq, glu, silu_out, name):
    T = src.shape[0]
    K = w.shape[0]
    tt, tc = _conv_tiles(seq, C)
    n_in = 2 if glu else 1
    per = tt // HALO
    offsets = [HALO - (K - 1) + k for k in range(K)]
    residues = _residues(offsets)

    def body(*refs):
        cur = refs[0:2 * n_in:2]
        halo = refs[1:2 * n_in:2]
        w_ref, b_ref = refs[2 * n_in], refs[2 * n_in + 1]
        outs = refs[2 * n_in + 2:-2]
        buf, shifted = refs[-2], refs[-1]
        i = pl.program_id(1)
        first = (i * tt) % seq == 0

        def pre(rs):
            v = rs[0][...].astype(F32)
            return v * _sigmoid(rs[1][...].astype(F32)) if glu else v

        buf[0:HALO, :] = jnp.where(first, 0.0, pre(halo))
        buf[HALO:HALO + tt, :] = pre(cur)
        _fill_shifted(buf, shifted, residues)

        def chunk(ci, carry):
            start = pl.multiple_of(ci * CONV_CHUNK, CONV_CHUNK)
            acc = jnp.broadcast_to(b_ref[...], (CONV_CHUNK, tc))
            for k in range(K):
                acc = acc + w_ref[k:k + 1, :] * _tap(buf, shifted, residues, offsets[k], start, CONV_CHUNK)
            outs[0][pl.ds(start, CONV_CHUNK), :] = acc.astype(BF16)
            if silu_out:
                outs[1][pl.ds(start, CONV_CHUNK), :] = _silu(acc).astype(BF16)
            return carry

        lax.fori_loop(0, tt // CONV_CHUNK, chunk, 0)

    in_specs, args = [], []
    for off in offs:
        assert off % tc == 0
        in_specs.append(pl.BlockSpec((tt, tc), lambda j, i, ob=off // tc: (i, ob + j)))
        in_specs.append(pl.BlockSpec((HALO, tc), lambda j, i, ob=off // tc: (jnp.maximum(i * per - 1, 0), ob + j)))
        args += [src, src]
    in_specs += [pl.BlockSpec((K, tc), lambda j, i: (0, j)), pl.BlockSpec((1, tc), lambda j, i: (0, j))]
    args += [w, b]
    n_out = 2 if silu_out else 1
    out = pl.pallas_call(
        body, name=name, grid=(C // tc, T // tt), in_specs=in_specs,
        out_specs=[pl.BlockSpec((tt, tc), lambda j, i: (i, j))] * n_out,
        out_shape=[jax.ShapeDtypeStruct((T, C), BF16)] * n_out,
        scratch_shapes=[pltpu.VMEM((HALO + tt, tc), F32), pltpu.VMEM((max(len(residues), 1), HALO + tt, tc), F32)],
        compiler_params=_cp("parallel", "arbitrary"))(*args)
    return out


def dwconv_bwd(du, u, src, offs, w, *, C, seq, glu, silu_out, name):
    T = src.shape[0]
    K = w.shape[0]
    tt, tc = _conv_tiles(seq, C)
    n_in = 2 if glu else 1
    per = tt // HALO
    last_blk = T // HALO - 1
    g_offsets = [K - 1 - k for k in range(K)]
    x_offsets = [HALO - (K - 1) + k for k in range(K)]
    g_res, x_res = _residues(g_offsets), _residues(x_offsets)

    def body(*refs):
        pos = 0
        du_cur, du_nxt = refs[0], refs[1]
        pos = 2
        if silu_out:
            u_cur, u_nxt = refs[2], refs[3]
            pos = 4
        cur = refs[pos:pos + 2 * n_in:2]
        halo = refs[pos + 1:pos + 2 * n_in:2]
        pos += 2 * n_in
        w_ref = refs[pos]
        outs = refs[pos + 1:pos + 1 + n_in]
        dw_ref, db_ref = refs[pos + 1 + n_in], refs[pos + 2 + n_in]
        gbuf, gshift, xbuf, xshift, dw_acc, db_acc = refs[-6:]
        i = pl.program_id(1)
        first = (i * tt) % seq == 0
        last = ((i + 1) * tt) % seq == 0

        @pl.when(i == 0)
        def _():
            dw_acc[...] = jnp.zeros_like(dw_acc)
            db_acc[...] = jnp.zeros_like(db_acc)

        g_cur = du_cur[...].astype(F32)
        g_nxt = du_nxt[...].astype(F32)
        if silu_out:
            g_cur = g_cur * _silu_grad(u_cur[...].astype(F32))
            g_nxt = g_nxt * _silu_grad(u_nxt[...].astype(F32))
        gbuf[0:tt, :] = g_cur
        gbuf[tt:tt + HALO, :] = jnp.where(last, 0.0, g_nxt)

        def pre(rs):
            v = rs[0][...].astype(F32)
            return v * _sigmoid(rs[1][...].astype(F32)) if glu else v

        xbuf[0:HALO, :] = jnp.where(first, 0.0, pre(halo))
        xbuf[HALO:HALO + tt, :] = pre(cur)
        _fill_shifted(gbuf, gshift, g_res)
        _fill_shifted(xbuf, xshift, x_res)

        def fold(v):
            out = v[0:SUBLANES]
            for s in range(SUBLANES, CONV_CHUNK, SUBLANES):
                out = out + v[s:s + SUBLANES]
            return out

        def chunk(ci, carry):
            start = pl.multiple_of(ci * CONV_CHUNK, CONV_CHUNK)
            g = gbuf[pl.ds(start, CONV_CHUNK), :]
            dx = jnp.zeros((CONV_CHUNK, tc), F32)
            for k in range(K):
                dx = dx + w_ref[k:k + 1, :] * _tap(gbuf, gshift, g_res, g_offsets[k], start, CONV_CHUNK)
                dw_acc[k * SUBLANES:(k + 1) * SUBLANES, :] += fold(
                    g * _tap(xbuf, xshift, x_res, x_offsets[k], start, CONV_CHUNK))
            db_acc[...] += fold(g)
            rows = pl.ds(start, CONV_CHUNK)
            if glu:
                a = cur[0][rows, :].astype(F32)
                s = _sigmoid(cur[1][rows, :].astype(F32))
                outs[0][rows, :] = (dx * s).astype(BF16)
                outs[1][rows, :] = (dx * a * s * (1.0 - s)).astype(BF16)
            else:
                outs[0][rows, :] = dx.astype(BF16)
            return carry

        lax.fori_loop(0, tt // CONV_CHUNK, chunk, 0)

        @pl.when(i == T // tt - 1)
        def _():
            for k in range(K):
                dw_ref[k:k + 1, :] = jnp.sum(dw_acc[k * SUBLANES:(k + 1) * SUBLANES, :], axis=0, keepdims=True)
            db_ref[...] = jnp.sum(db_acc[...], axis=0, keepdims=True)

    def cur_spec(ob):
        return pl.BlockSpec((tt, tc), lambda j, i: (i, ob + j))

    def nxt_spec(ob):
        return pl.BlockSpec((HALO, tc), lambda j, i: (jnp.minimum((i + 1) * per, last_blk), ob + j))

    def prv_spec(ob):
        return pl.BlockSpec((HALO, tc), lambda j, i: (jnp.maximum(i * per - 1, 0), ob + j))

    in_specs = [cur_spec(0), nxt_spec(0)]
    args = [du, du]
    if silu_out:
        in_specs += [cur_spec(0), nxt_spec(0)]
        args += [u, u]
    for off in offs:
        assert off % tc == 0
        in_specs += [cur_spec(off // tc), prv_spec(off // tc)]
        args += [src, src]
    in_specs.append(pl.BlockSpec((K, tc), lambda j, i: (0, j)))
    args.append(w)
    out_specs = [pl.BlockSpec((tt, tc), lambda j, i: (i, j))] * n_in
    out_specs += [pl.BlockSpec((K, tc), lambda j, i: (0, j)), pl.BlockSpec((1, tc), lambda j, i: (0, j))]
    out_shape = [jax.ShapeDtypeStruct((T, C), BF16)] * n_in
    out_shape += [jax.ShapeDtypeStruct((K, C), F32), jax.ShapeDtypeStruct((1, C), F32)]
    return pl.pallas_call(
        body, name=name, grid=(C // tc, T // tt), in_specs=in_specs, out_specs=out_specs, out_shape=out_shape,
        scratch_shapes=[pltpu.VMEM((tt + HALO, tc), F32), pltpu.VMEM((max(len(g_res), 1), tt + HALO, tc), F32),
                        pltpu.VMEM((HALO + tt, tc), F32), pltpu.VMEM((max(len(x_res), 1), HALO + tt, tc), F32),
                        pltpu.VMEM((K * SUBLANES, tc), F32), pltpu.VMEM((SUBLANES, tc), F32)],
        compiler_params=_cp("parallel", "arbitrary"))(*args)


def mix0_post_fwd(u2, proj, o, ln_w, ln_b, *, CW, gc_off, ga_off, name):
    T = u2.shape[0]
    tt = _pick(T, 256, 16)

    def body(u_ref, gc_ref, ga_ref, o_ref, lw_ref, lb_ref, y_ref):
        u = u_ref[...].astype(F32)
        mu = jnp.mean(u, axis=-1, keepdims=True)
        xc = u - mu
        r = lax.rsqrt(jnp.mean(xc * xc, axis=-1, keepdims=True) + EPS)
        u3 = xc * r * lw_ref[...] + lb_ref[...]
        y_ref[:, 0:CW] = (_silu(u3) * _silu(gc_ref[...].astype(F32))).astype(BF16)
        y_ref[:, CW:2 * CW] = (o_ref[...].astype(F32) * _silu(ga_ref[...].astype(F32))).astype(BF16)

    row = pl.BlockSpec((tt, CW), lambda i: (i, 0))
    vec = pl.BlockSpec((1, CW), lambda i: (0, 0))
    return pl.pallas_call(
        body, name=name, grid=(T // tt,),
        in_specs=[row, pl.BlockSpec((tt, CW), lambda i: (i, gc_off // CW)),
                  pl.BlockSpec((tt, CW), lambda i: (i, ga_off // CW)), row, vec, vec],
        out_specs=pl.BlockSpec((tt, 2 * CW), lambda i: (i, 0)),
        out_shape=jax.ShapeDtypeStruct((T, 2 * CW), BF16),
        compiler_params=_cp("parallel"))(u2, proj, proj, o, ln_w, ln_b)


def mix0_post_bwd(dy, u2, proj, o, ln_w, ln_b, *, CW, gc_off, ga_off, name):
    T = u2.shape[0]
    tt = _pick(T, 256, 16)

    def body(dy_ref, u_ref, gc_ref, ga_ref, o_ref, lw_ref, lb_ref, du_ref, dgc_ref, dga_ref, do_ref, dlw_ref, dlb_ref):
        @pl.when(pl.program_id(0) == 0)
        def _():
            dlw_ref[...] = jnp.zeros_like(dlw_ref)
            dlb_ref[...] = jnp.zeros_like(dlb_ref)

        dyc = dy_ref[:, 0:CW].astype(F32)
        dya = dy_ref[:, CW:2 * CW].astype(F32)
        u = u_ref[...].astype(F32)
        mu = jnp.mean(u, axis=-1, keepdims=True)
        xc = u - mu
        r = lax.rsqrt(jnp.mean(xc * xc, axis=-1, keepdims=True) + EPS)
        xhat = xc * r
        u3 = xhat * lw_ref[...] + lb_ref[...]
        gc = gc_ref[...].astype(F32)
        dgc_ref[...] = (dyc * _silu(u3) * _silu_grad(gc)).astype(BF16)
        du3 = dyc * _silu(gc) * _silu_grad(u3)
        dlw_ref[...] += jnp.sum(du3 * xhat, axis=0, keepdims=True)
        dlb_ref[...] += jnp.sum(du3, axis=0, keepdims=True)
        dxh = du3 * lw_ref[...]
        du = r * (dxh - jnp.mean(dxh, axis=-1, keepdims=True) - xhat * jnp.mean(dxh * xhat, axis=-1, keepdims=True))
        du_ref[...] = du.astype(BF16)
        ga = ga_ref[...].astype(F32)
        ov = o_ref[...].astype(F32)
        do_ref[...] = (dya * _silu(ga)).astype(BF16)
        dga_ref[...] = (dya * ov * _silu_grad(ga)).astype(BF16)

    row = pl.BlockSpec((tt, CW), lambda i: (i, 0))
    vec = pl.BlockSpec((1, CW), lambda i: (0, 0))
    big = jax.ShapeDtypeStruct((T, CW), BF16)
    small = jax.ShapeDtypeStruct((1, CW), F32)
    return pl.pallas_call(
        body, name=name, grid=(T // tt,),
        in_specs=[pl.BlockSpec((tt, 2 * CW), lambda i: (i, 0)), row,
                  pl.BlockSpec((tt, CW), lambda i: (i, gc_off // CW)),
                  pl.BlockSpec((tt, CW), lambda i: (i, ga_off // CW)), row, vec, vec],
        out_specs=[row, row, row, row, vec, vec],
        out_shape=[big, big, big, big, small, small],
        compiler_params=_cp("arbitrary"))(dy, u2, proj, proj, o, ln_w, ln_b)


SB_UNDERFLOW = 110.0
SB_BOUND_MARGIN = 1.02


def _sb_tile(seq):
    return _pick(seq, 256, LANE)


def _softplus(z):
    return jnp.maximum(z, 0.0) + jnp.log(1.0 + jnp.exp(-jnp.abs(z)))


def _tri01(n, lower):
    i = lax.broadcasted_iota(jnp.int32, (n, n), 0)
    j = lax.broadcasted_iota(jnp.int32, (n, n), 1)
    return ((i >= j) if lower else (i <= j)).astype(BF16)


SB_HEADS_FWD = 4
SB_HEADS_BWD = 2


def _sb_heads_per_step(heads, want):
    while heads % want:
        want //= 2
    return want


def sba_fwd(proj, *, B, seq, heads, q_off, k_off, v_off, name):
    dh = SB_HEAD_DIM
    tq = _sb_tile(seq)
    nq = seq // tq
    hps = _sb_heads_per_step(heads, SB_HEADS_FWD)
    hw = hps * dh
    scale = dh ** -0.5

    def body(q_ref, k_ref, v_ref, tri_ref, o_ref, ct_ref, acc_ref, kmax_ref):
        qi = pl.program_id(1)
        tri = tri_ref[...]
        below = lax.broadcasted_iota(jnp.int32, (tq, tq), 1) < lax.broadcasted_iota(jnp.int32, (tq, tq), 0)
        qs = [(q_ref[:, h * dh:(h + 1) * dh].astype(F32) * scale).astype(BF16) for h in range(hps)]

        @pl.when(qi == 0)
        def _():
            def chunk(i, best):
                rows = k_ref[pl.ds(pl.multiple_of(i * tq, tq), tq), :].astype(F32)
                sq = rows * rows
                return tuple(jnp.maximum(best[h], jnp.max(jnp.sum(sq[:, h * dh:(h + 1) * dh], axis=1, keepdims=True),
                                                          axis=0, keepdims=True)) for h in range(hps))

            best = lax.fori_loop(0, nq, chunk, (jnp.zeros((1, 1), F32),) * hps)
            for h in range(hps):
                kmax_ref[h] = jnp.broadcast_to(jnp.sqrt(best[h]), (8, LANE))

        z_bound = [jnp.sqrt(jnp.sum(qs[h].astype(F32) ** 2, axis=1, keepdims=True))
                   * (SB_BOUND_MARGIN * jnp.max(kmax_ref[h], keepdims=True)) for h in range(hps)]

        def block(start, rs, diag):
            pvs, out = [], []
            for h in range(hps):
                k_blk = k_ref[pl.ds(start, tq), h * dh:(h + 1) * dh]
                v_blk = v_ref[pl.ds(start, tq), h * dh:(h + 1) * dh]
                z = _dot(qs[h], k_blk, NT)
                sp = _softplus(z)
                if diag:
                    sp = jnp.where(below, sp, 0.0)
                wts = jnp.exp(z - (_dot(sp.astype(BF16), tri) + rs[h]))
                if diag:
                    wts = jnp.where(below, wts, 0.0)
                pvs.append(_dot(wts.astype(BF16), v_blk))
                out.append(rs[h] + jnp.sum(sp, axis=-1, keepdims=True))
            return pvs, tuple(out)

        zero = jnp.zeros((tq, 1), F32)
        pvs, rs = block(pl.multiple_of(qi * tq, tq), (zero,) * hps, True)
        for h in range(hps):
            acc_ref[:, h * dh:(h + 1) * dh] = pvs[h]

        def more(c):
            j, rs = c
            slack = rs[0] - z_bound[0]
            for h in range(1, hps):
                slack = jnp.minimum(slack, rs[h] - z_bound[h])
            return jnp.logical_and(j < qi, jnp.min(slack) <= SB_UNDERFLOW)

        def step(c):
            j, rs = c
            pvs, rs = block(pl.multiple_of((qi - 1 - j) * tq, tq), rs, False)
            for h in range(hps):
                acc_ref[:, h * dh:(h + 1) * dh] += pvs[h]
            return j + 1, rs

        n_left, totals = lax.while_loop(more, step, (jnp.int32(0), rs))
        o_ref[...] = acc_ref[...].astype(BF16)
        for h in range(hps):
            ct_ref[0, 0, h, 0:8, :] = jnp.broadcast_to(totals[h], (tq, LANE)).T[0:8, :]
            ct_ref[0, 0, h, 8:16, :] = jnp.full((8, tq), n_left, F32)

    qb, kb, vb = q_off // hw, k_off // hw, v_off // hw
    G = heads // hps
    return pl.pallas_call(
        body, name=name, grid=(B * G, nq),
        in_specs=[pl.BlockSpec((tq, hw), lambda g, i: ((g // G) * nq + i, qb + g % G)),
                  pl.BlockSpec((seq, hw), lambda g, i: (g // G, kb + g % G)),
                  pl.BlockSpec((seq, hw), lambda g, i: (g // G, vb + g % G)),
                  pl.BlockSpec((tq, tq), lambda g, i: (0, 0))],
        out_specs=[pl.BlockSpec((tq, hw), lambda g, i: ((g // G) * nq + i, g % G)),
                   pl.BlockSpec((1, 1, hps, 16, tq), lambda g, i: (g // G, i, g % G, 0, 0))],
        out_shape=[jax.ShapeDtypeStruct((B * seq, heads * dh), BF16),
                   jax.ShapeDtypeStruct((B, nq, heads, 16, tq), F32)],
        scratch_shapes=[pltpu.VMEM((tq, hw), F32), pltpu.VMEM((hps, 8, LANE), F32)],
        compiler_params=_cp("parallel", "arbitrary"))(proj, proj, proj, jnp.tril(jnp.ones((tq, tq), BF16)))


def sba_bwd(proj, ctot, do, *, B, seq, heads, q_off, k_off, v_off, name, comm=None):
    dh = SB_HEAD_DIM
    tq = _sb_tile(seq)
    nq = seq // tq
    hps = _sb_heads_per_step(heads, SB_HEADS_BWD)
    hw = hps * dh
    scale = dh ** -0.5

    def body(q_ref, k_ref, v_ref, ct_ref, do_ref, sfx_ref, pre_ref, dq_ref, dk_ref, dv_ref, dq_acc, dk_acc, dv_acc):
        qi = pl.program_id(1)

        @pl.when(qi == 0)
        def _():
            dk_acc[...] = jnp.zeros_like(dk_acc)
            dv_acc[...] = jnp.zeros_like(dv_acc)

        tri_sfx = sfx_ref[...]
        tri_pre = pre_ref[...]
        below = lax.broadcasted_iota(jnp.int32, (tq, tq), 1) < lax.broadcasted_iota(jnp.int32, (tq, tq), 0)
        qs = [(q_ref[:, h * dh:(h + 1) * dh].astype(F32) * scale).astype(BF16) for h in range(hps)]
        dos = [do_ref[:, h * dh:(h + 1) * dh] for h in range(hps)]
        totals = [jnp.max(jnp.broadcast_to(ct_ref[0, 0, h, 0:1, :], (LANE, tq)).T, axis=1, keepdims=True)
                  for h in range(hps)]
        dq_acc[...] = jnp.zeros_like(dq_acc)

        def block(start, carry, diag):
            out = []
            for h in range(hps):
                pc, pg = carry[h]
                cols = slice(h * dh, (h + 1) * dh)
                k_blk = k_ref[pl.ds(start, tq), cols]
                v_blk = v_ref[pl.ds(start, tq), cols]
                z = _dot(qs[h], k_blk, NT)
                sp = _softplus(z)
                sig = jnp.exp(z - sp)
                if diag:
                    sp = jnp.where(below, sp, 0.0)
                pc_next = pc + jnp.sum(sp, axis=-1, keepdims=True)
                wts = jnp.exp(z - (_dot(sp.astype(BF16), tri_sfx) + (totals[h] - pc_next)))
                if diag:
                    wts = jnp.where(below, wts, 0.0)
                g = _dot(dos[h], v_blk, NT) * wts
                dz = g - sig * (_dot(g.astype(BF16), tri_pre) + pg)
                if diag:
                    dz = jnp.where(below, dz, 0.0)
                dz = dz.astype(BF16)
                dq_acc[:, cols] += _dot(dz, k_blk)
                dk_acc[pl.ds(start, tq), cols] += _dot(dz, qs[h], TN)
                dv_acc[pl.ds(start, tq), cols] += _dot(wts.astype(BF16), dos[h], TN)
                out.append((pc_next, pg + jnp.sum(g, axis=-1, keepdims=True)))
            return tuple(out)

        zero = jnp.zeros((tq, 1), F32)
        n_left = jnp.max(ct_ref[0, 0, 0, 8:16, :]).astype(jnp.int32)
        carry = lax.fori_loop(qi - n_left, qi, lambda j, c: block(pl.multiple_of(j * tq, tq), c, False),
                              ((zero, zero),) * hps)
        block(pl.multiple_of(qi * tq, tq), carry, True)
        dq_ref[...] = (dq_acc[...] * scale).astype(BF16)

        @pl.when(qi == nq - 1)
        def _():
            dk_ref[...] = dk_acc[...].astype(BF16)
            dv_ref[...] = dv_acc[...].astype(BF16)

    qb, kb, vb = q_off // hw, k_off // hw, v_off // hw
    G = heads // hps
    q_spec = pl.BlockSpec((tq, hw), lambda g, i: ((g // G) * nq + i, qb + g % G))
    o_spec = pl.BlockSpec((tq, hw), lambda g, i: ((g // G) * nq + i, g % G))
    kv_out = pl.BlockSpec((seq, hw), lambda g, i: (g // G, g % G))
    shp = jax.ShapeDtypeStruct((B * seq, heads * dh), BF16)
    body, c_in, c_args, c_out, c_shape, c_scratch = _hosted(body, 7, 3, comm, _grid_step(nq), B * G * nq)
    tri_spec = pl.BlockSpec((tq, tq), lambda g, i: (0, 0))
    ones = jnp.ones((tq, tq), BF16)
    out = pl.pallas_call(
        body, name=name, grid=(B * G, nq),
        in_specs=[q_spec,
                  pl.BlockSpec((seq, hw), lambda g, i: (g // G, kb + g % G)),
                  pl.BlockSpec((seq, hw), lambda g, i: (g // G, vb + g % G)),
                  pl.BlockSpec((1, 1, hps, 16, tq), lambda g, i: (g // G, i, g % G, 0, 0)), o_spec,
                  tri_spec, tri_spec] + c_in,
        out_specs=[o_spec, kv_out, kv_out] + c_out, out_shape=[shp, shp, shp] + c_shape,
        scratch_shapes=[pltpu.VMEM((tq, hw), F32), pltpu.VMEM((seq, hw), F32), pltpu.VMEM((seq, hw), F32)]
        + c_scratch,
        compiler_params=_cp("arbitrary" if comm else "parallel", "arbitrary"))(
            proj, proj, proj, ctot, do, jnp.tril(ones), jnp.triu(ones), *c_args)
    return (out[0], out[1], out[2], out[3:]) if comm else out


def _head_expand(n_heads, DI):
    j = jnp.arange(LANE, dtype=jnp.int32)[:, None]
    c = jnp.arange(DI, dtype=jnp.int32)[None, :] // SSM_HEAD_DIM
    return ((j == c) & (j < n_heads)).astype(BF16)


def _split3(x):
    hi = x.astype(BF16)
    r1 = x - hi.astype(F32)
    mid = r1.astype(BF16)
    return hi, mid, (r1 - mid.astype(F32)).astype(BF16)


def dt_fwd(proj, bias, a_log, expand, *, dt_off, name):
    T = proj.shape[0]
    DI = expand.shape[1]
    L = SSM_CHUNK
    tt = _pick(T, 512, L)

    def body(raw_ref, bias_ref, al_ref, e_ref, dt_ref, cs_ref, dtx_ref, csx_ref):
        x = raw_ref[...].astype(F32) + bias_ref[...]
        dt = _softplus(x)
        dt_ref[...] = dt
        la = dt * (-jnp.exp(al_ref[...]))
        tri = _tri01(L, True)
        for c in range(tt // L):
            cs_ref[c * L:(c + 1) * L, :] = _tri_dot3(tri, la[c * L:(c + 1) * L, :])
        e = e_ref[...]
        dtx_ref[...] = _dot(dt.astype(BF16), e).astype(BF16)
        hi, mid, lo = _split3(cs_ref[...])
        csx_ref[...] = _dot(hi, e) + _dot(mid, e) + _dot(lo, e)

    row = pl.BlockSpec((tt, LANE), lambda i: (i, 0))
    wide = pl.BlockSpec((tt, DI), lambda i: (i, 0))
    vec = pl.BlockSpec((1, LANE), lambda i: (0, 0))
    return pl.pallas_call(
        body, name=name, grid=(T // tt,),
        in_specs=[pl.BlockSpec((tt, LANE), lambda i: (i, dt_off // LANE)), vec, vec,
                  pl.BlockSpec((LANE, DI), lambda i: (0, 0))],
        out_specs=[row, row, wide, wide],
        out_shape=[jax.ShapeDtypeStruct((T, LANE), F32), jax.ShapeDtypeStruct((T, LANE), F32),
                   jax.ShapeDtypeStruct((T, DI), BF16), jax.ShapeDtypeStruct((T, DI), F32)],
        compiler_params=_cp("parallel"))(proj, bias, a_log, expand)


def dt_bwd(ddt_x, dcs_x, dcs_cols, proj, dt, bias, a_log, reduce_t, *, dt_off, n_heads, name):
    T = proj.shape[0]
    DI = reduce_t.shape[0]
    L = SSM_CHUNK
    tt = _pick(T, 512, L)

    def body(ddtx_ref, dcsx_ref, dcsc_ref, raw_ref, dt_ref, bias_ref, al_ref, r_ref, draw_ref, dbias_ref, dal_ref,
             dla_buf):
        @pl.when(pl.program_id(0) == 0)
        def _():
            dbias_ref[...] = jnp.zeros_like(dbias_ref)
            dal_ref[...] = jnp.zeros_like(dal_ref)

        r = r_ref[...]
        ddt = _dot(ddtx_ref[...], r)
        dx = dcsx_ref[...]
        hi = dx.astype(BF16)
        dcs = _dot(hi, r) + _dot((dx - hi.astype(F32)).astype(BF16), r) + dcsc_ref[...]
        triu = _tri01(L, False)
        for c in range(tt // L):
            dla_buf[c * L:(c + 1) * L, :] = _tri_dot3(triu, dcs[c * L:(c + 1) * L, :])
        dla = dla_buf[...]
        a = -jnp.exp(al_ref[...])
        valid = lax.broadcasted_iota(jnp.int32, (tt, LANE), 1) < n_heads
        dal_ref[...] += jnp.sum(jnp.where(valid, dla * dt_ref[...], 0.0), axis=0, keepdims=True) * a
        x = raw_ref[...].astype(F32) + bias_ref[...]
        draw = jnp.where(valid, (ddt + dla * a) * _sigmoid(x), 0.0)
        dbias_ref[...] += jnp.sum(draw, axis=0, keepdims=True)
        draw_ref[...] = draw.astype(BF16)

    row = pl.BlockSpec((tt, LANE), lambda i: (i, 0))
    wide = pl.BlockSpec((tt, DI), lambda i: (i, 0))
    vec = pl.BlockSpec((1, LANE), lambda i: (0, 0))
    return pl.pallas_call(
        body, name=name, grid=(T // tt,),
        in_specs=[wide, wide, row, pl.BlockSpec((tt, LANE), lambda i: (i, dt_off // LANE)), row, vec, vec,
                  pl.BlockSpec((DI, LANE), lambda i: (0, 0))],
        out_specs=[row, vec, vec],
        out_shape=[jax.ShapeDtypeStruct((T, LANE), BF16), jax.ShapeDtypeStruct((1, LANE), F32),
                   jax.ShapeDtypeStruct((1, LANE), F32)],
        scratch_shapes=[pltpu.VMEM((tt, LANE), F32)],
        compiler_params=_cp("arbitrary"))(ddt_x, dcs_x, dcs_cols, proj, dt, bias, a_log, reduce_t)


def _pair_terms(x_ref, dtx_ref, csx_ref, csr_ref, pair, ppg, lo_half, causal):
    L = SSM_CHUNK
    g, pp = divmod(pair, ppg)
    ra = g * HEAD_ROWS + 2 * pp
    cols = slice(pair * LANE, (pair + 1) * LANE)
    X = x_ref[:, cols].astype(F32)
    dt_p = dtx_ref[:, cols].astype(F32)
    own = csx_ref[:, cols]
    other = pltpu.roll(own, SSM_HEAD_DIM, 1)
    csa_c = jnp.where(lo_half, own, other)
    csb_c = jnp.where(lo_half, other, own)
    La = jnp.exp(jnp.where(causal, csa_c - csr_ref[0, ra:ra + 1, :], NEG_BIG))
    Lb = jnp.exp(jnp.where(causal, csb_c - csr_ref[0, ra + 1:ra + 2, :], NEG_BIG))
    last = csx_ref[L - 1:L, cols]
    return g, ra, cols, X, dt_p, La, Lb, jnp.exp(own), jnp.exp(last - own), jnp.exp(last)


def scan_fwd(xbc, dt_x, cs_x, cs_row, d_full, *, B, seq, DI, name):
    L, N, G = SSM_CHUNK, SSM_STATE, SSM_GROUPS
    nc = seq // L
    XW = xbc.shape[1]
    n_pairs = DI // LANE
    ppg = n_pairs // G

    def body(x_ref, dtx_ref, csx_ref, csr_ref, d_ref, y_ref, st_ref, state):
        c = pl.program_id(1)

        @pl.when(c == 0)
        def _():
            state[...] = jnp.zeros_like(state)

        causal = lax.broadcasted_iota(jnp.int32, (L, L), 0) >= lax.broadcasted_iota(jnp.int32, (L, L), 1)
        lo_half = lax.broadcasted_iota(jnp.int32, (L, LANE), 1) < SSM_HEAD_DIM
        cbs = []
        for g in range(G):
            Bc = x_ref[:, DI + g * N:DI + (g + 1) * N]
            Cc = x_ref[:, DI + G * N + g * N:DI + G * N + (g + 1) * N]
            cbs.append((Bc, Cc, _dot(Cc, Bc, NT)))
        for pair in range(n_pairs):
            g, _, cols, X, dt_p, La, Lb, ecs, tail, e_last = _pair_terms(
                x_ref, dtx_ref, csx_ref, csr_ref, pair, ppg, lo_half, causal)
            Bc, Cc, CB = cbs[g]
            xs = X * dt_p
            xsb = xs.astype(BF16)
            y = jnp.where(lo_half, _dot((CB * La).astype(BF16), xsb), _dot((CB * Lb).astype(BF16), xsb))
            ST = state[pair]
            st_ref[0, 0, pair] = ST
            y = y + ecs * _dot(Cc, ST.astype(BF16)) + d_ref[:, cols] * X
            y_ref[:, cols] = y.astype(BF16)
            state[pair] = e_last * ST + _dot(Bc, (xs * tail).astype(BF16), TN)

    wide = pl.BlockSpec((L, DI), lambda b, c: (b * nc + c, 0))
    return pl.pallas_call(
        body, name=name, grid=(B, nc),
        in_specs=[pl.BlockSpec((L, XW), lambda b, c: (b * nc + c, 0)), wide, wide,
                  pl.BlockSpec((1, G * HEAD_ROWS, L), lambda b, c: (b, 0, c)),
                  pl.BlockSpec((1, DI), lambda b, c: (0, 0))],
        out_specs=[wide, pl.BlockSpec((1, 1, n_pairs, N, LANE), lambda b, c: (b, c, 0, 0, 0))],
        out_shape=[jax.ShapeDtypeStruct((B * seq, DI), BF16),
                   jax.ShapeDtypeStruct((B, nc, n_pairs, N, LANE), F32)],
        scratch_shapes=[pltpu.VMEM((n_pairs, N, LANE), F32)],
        compiler_params=_cp("parallel", "arbitrary"))(xbc, dt_x, cs_x, cs_row, d_full)


def scan_bwd(xbc, dt_x, cs_x, cs_row, d_full, states, dy, *, B, seq, DI, name):
    L, N, G = SSM_CHUNK, SSM_STATE, SSM_GROUPS
    nc = seq // L
    XW = xbc.shape[1]
    n_pairs = DI // LANE
    ppg = n_pairs // G
    HR = G * HEAD_ROWS
    inv_p = 1.0 / SSM_HEAD_DIM

    def body(x_ref, dtx_ref, csx_ref, csr_ref, d_ref, st_ref, dy_ref, dx_ref, ddtx_ref, dcsx_ref, dcsr_ref, dd_ref,
             dH):
        c = pl.program_id(1)

        @pl.when(c == 0)
        def _():
            dH[...] = jnp.zeros_like(dH)
            dd_ref[...] = jnp.zeros_like(dd_ref)

        causal = lax.broadcasted_iota(jnp.int32, (L, L), 0) >= lax.broadcasted_iota(jnp.int32, (L, L), 1)
        lo_half = lax.broadcasted_iota(jnp.int32, (L, LANE), 1) < SSM_HEAD_DIM
        last_row = lax.broadcasted_iota(jnp.int32, (L, LANE), 0) == L - 1
        head_row = lax.broadcasted_iota(jnp.int32, (HR, 1), 0)
        dcs_rows = jnp.zeros((HR, L), F32)

        for g in range(G):
            Bc = x_ref[:, DI + g * N:DI + (g + 1) * N]
            Cc = x_ref[:, DI + G * N + g * N:DI + G * N + (g + 1) * N]
            CB = _dot(Cc, Bc, NT)
            dCB = jnp.zeros((L, L), F32)
            dC = jnp.zeros((L, N), F32)
            dB = jnp.zeros((L, N), F32)
            for pp in range(ppg):
                pair = g * ppg + pp
                _, ra, cols, X, dt_p, La, Lb, ecs, tail, e_last = _pair_terms(
                    x_ref, dtx_ref, csx_ref, csr_ref, pair, ppg, lo_half, causal)
                xs = X * dt_p
                xsb = xs.astype(BF16)
                Ma, Mb = CB * La, CB * Lb
                dY = dy_ref[:, cols].astype(F32)
                dYb = dY.astype(BF16)
                dMa = _dot(jnp.where(lo_half, dY, 0.0).astype(BF16), xsb, NT)
                dMb = _dot(jnp.where(lo_half, 0.0, dY).astype(BF16), xsb, NT)
                dSa, dSb = dMa * Ma, dMb * Mb
                dCB = dCB + dMa * La + dMb * Lb
                dcs = jnp.where(lo_half, jnp.sum(dSa, axis=1, keepdims=True), jnp.sum(dSb, axis=1, keepdims=True)) * inv_p
                dcs_rows = dcs_rows - jnp.where(head_row == ra, jnp.sum(dSa, axis=0, keepdims=True), 0.0)
                dcs_rows = dcs_rows - jnp.where(head_row == ra + 1, jnp.sum(dSb, axis=0, keepdims=True), 0.0)
                dxs = jnp.where(lo_half, _dot(Ma.astype(BF16), dYb, TN), _dot(Mb.astype(BF16), dYb, TN))
                ST = st_ref[0, 0, pair]
                STb = ST.astype(BF16)
                dYe = (dY * ecs).astype(BF16)
                dC = dC + _dot(dYe, STb, NT)
                dSTp = _dot(Cc, dYe, TN)
                dcs = dcs + dY * (ecs * _dot(Cc, STb))
                dSTn = dH[pair]
                dSTnb = dSTn.astype(BF16)
                dSTp = dSTp + e_last * dSTn
                XBt = _dot(Bc, dSTnb)
                dxs = dxs + tail * XBt
                t2 = xs * XBt * tail
                at_end = e_last * jnp.sum(dSTn * ST, axis=0, keepdims=True) + jnp.sum(t2, axis=0, keepdims=True)
                dcs = dcs - t2 + jnp.where(last_row, at_end, 0.0)
                dB = dB + _dot((xs * tail).astype(BF16), dSTnb, NT)
                dx_ref[:, cols] = (dxs * dt_p + d_ref[:, cols] * dY).astype(BF16)
                ddtx_ref[:, cols] = (dxs * X).astype(BF16)
                dcsx_ref[:, cols] = dcs
                dd_ref[0, :, cols] += jnp.sum(dY * X, axis=0, keepdims=True)
                dH[pair] = dSTp
            dCBb = dCB.astype(BF16)
            dx_ref[:, DI + g * N:DI + (g + 1) * N] = (dB + _dot(dCBb, Cc, TN)).astype(BF16)
            dx_ref[:, DI + G * N + g * N:DI + G * N + (g + 1) * N] = (dC + _dot(dCBb, Bc)).astype(BF16)
        dcsr_ref[0] = dcs_rows

    rev = lambda b, c: (b * nc + (nc - 1 - c), 0)
    wide = pl.BlockSpec((L, DI), rev)
    hrow = pl.BlockSpec((1, HR, L), lambda b, c: (b, 0, nc - 1 - c))
    return pl.pallas_call(
        body, name=name, grid=(B, nc),
        in_specs=[pl.BlockSpec((L, XW), rev), wide, wide, hrow,
                  pl.BlockSpec((1, DI), lambda b, c: (0, 0)),
                  pl.BlockSpec((1, 1, n_pairs, N, LANE), lambda b, c: (b, nc - 1 - c, 0, 0, 0)), wide],
        out_specs=[pl.BlockSpec((L, XW), rev), wide, wide, hrow, pl.BlockSpec((1, 1, DI), lambda b, c: (b, 0, 0))],
        out_shape=[jax.ShapeDtypeStruct((B * seq, XW), BF16), jax.ShapeDtypeStruct((B * seq, DI), BF16),
                   jax.ShapeDtypeStruct((B * seq, DI), F32), jax.ShapeDtypeStruct((B, HR, seq), F32),
                   jax.ShapeDtypeStruct((B, 1, DI), F32)],
        scratch_shapes=[pltpu.VMEM((n_pairs, N, LANE), F32)],
        compiler_params=_cp("parallel", "arbitrary"))(xbc, dt_x, cs_x, cs_row, d_full, states, dy)


def gnorm_fwd(y, proj, w, *, DI, name):
    T = y.shape[0]
    tt = _pick(T, 256, 16)
    gw = DI // SSM_GROUPS

    def body(y_ref, z_ref, w_ref, o_ref):
        for g in range(SSM_GROUPS):
            sl = slice(g * gw, (g + 1) * gw)
            y2 = y_ref[:, sl].astype(F32) * _silu(z_ref[:, sl].astype(F32))
            r = lax.rsqrt(jnp.mean(y2 * y2, axis=-1, keepdims=True) + EPS)
            o_ref[:, sl] = (y2 * r * w_ref[:, sl]).astype(BF16)

    row = pl.BlockSpec((tt, DI), lambda i: (i, 0))
    return pl.pallas_call(
        body, name=name, grid=(T // tt,),
        in_specs=[row, row, pl.BlockSpec((1, DI), lambda i: (0, 0))], out_specs=row,
        out_shape=jax.ShapeDtypeStruct((T, DI), BF16), compiler_params=_cp("parallel"))(y, proj, w)


def gnorm_bwd(dyn, y, proj, w, *, DI, name):
    T = y.shape[0]
    tt = _pick(T, 256, 16)
    gw = DI // SSM_GROUPS

    def body(dyn_ref, y_ref, z_ref, w_ref, dy_ref, dz_ref, dw_ref):
        @pl.when(pl.program_id(0) == 0)
        def _():
            dw_ref[...] = jnp.zeros_like(dw_ref)

        for g in range(SSM_GROUPS):
            sl = slice(g * gw, (g + 1) * gw)
            yv = y_ref[:, sl].astype(F32)
            z = z_ref[:, sl].astype(F32)
            sz = _silu(z)
            y2 = yv * sz
            r = lax.rsqrt(jnp.mean(y2 * y2, axis=-1, keepdims=True) + EPS)
            xhat = y2 * r
            d = dyn_ref[:, sl].astype(F32)
            dw_ref[:, sl] += jnp.sum(d * xhat, axis=0, keepdims=True)
            dxh = d * w_ref[:, sl]
            dy2 = r * (dxh - xhat * jnp.mean(dxh * xhat, axis=-1, keepdims=True))
            dy_ref[:, sl] = (dy2 * sz).astype(BF16)
            dz_ref[:, sl] = (dy2 * yv * _silu_grad(z)).astype(BF16)

    row = pl.BlockSpec((tt, DI), lambda i: (i, 0))
    vec = pl.BlockSpec((1, DI), lambda i: (0, 0))
    shp = jax.ShapeDtypeStruct((T, DI), BF16)
    return pl.pallas_call(
        body, name=name, grid=(T // tt,), in_specs=[row, row, row, vec], out_specs=[row, row, vec],
        out_shape=[shp, shp, jax.ShapeDtypeStruct((1, DI), F32)],
        compiler_params=_cp("arbitrary"))(dyn, y, proj, w)


N_CHIP = 4


def _comm_out_shapes(srcs, modes):
    return [jax.ShapeDtypeStruct(((N_DEV,) if mode in ("gather", "gather_direct") else ()) + s.shape, s.dtype)
            for s, mode in zip(srcs, modes)]


def _comm_scratch(n):
    return [pltpu.SemaphoreType.DMA((n, N_DEV - 1)), pltpu.SemaphoreType.DMA((n, N_DEV - 1)),
            pltpu.SemaphoreType.DMA((n,))]


def _comm_phases(modes, src_refs, out_refs, send_sems, recv_sems, local_sems):
    x, y, c = lax.axis_index("x"), lax.axis_index("y"), lax.axis_index("c")
    me, sibling = (x, y, c), (x, y, 1 - c)
    chips = [(1 - x, y), (x, 1 - y), (1 - x, 1 - y)]
    relays = [a for a, mode in enumerate(modes) if mode == "gather"]

    def slot(p):
        return 4 * p[0] + 2 * p[1] + p[2]

    def remote(a, k, src, dst, to):
        return pltpu.make_async_remote_copy(src_ref=src, dst_ref=dst, send_sem=send_sems.at[a, k],
                                            recv_sem=recv_sems.at[a, k], device_id=to,
                                            device_id_type=pl.DeviceIdType.MESH)

    def first_copies():
        local, two_way, send_only = [], [], []
        for a, mode in enumerate(modes):
            src, out = src_refs[a], out_refs[a]
            if mode == "sibling":
                two_way.append(remote(a, 0, src, out, sibling))
            elif mode == "chips":
                mine = 2 * x + y
                local.append(pltpu.make_async_copy(src.at[mine], out.at[mine], local_sems.at[a]))
                for j, chip in enumerate(chips):
                    two_way.append(remote(a, 1 + j, src.at[2 * chip[0] + chip[1]], out.at[mine], (*chip, c)))
            elif mode == "gather_direct":
                local.append(pltpu.make_async_copy(src, out.at[slot(me)], local_sems.at[a]))
                for k in range(1, N_DEV):
                    peer = (1 - x if k & 4 else x, 1 - y if k & 2 else y, 1 - c if k & 1 else c)
                    two_way.append(remote(a, k - 1, src, out.at[slot(me)], peer))
            else:
                assert mode == "gather"
                local.append(pltpu.make_async_copy(src, out.at[slot(me)], local_sems.at[a]))
                send_only.append(remote(a, 0, src, out.at[slot(me)], sibling))
                for j, chip in enumerate(chips):
                    send_only.append(remote(a, 1 + j, src, out.at[slot(me)], (*chip, c)))
        return local, two_way, send_only

    def forwards():
        out = []
        for a in relays:
            for j, chip in enumerate(chips):
                landed = out_refs[a].at[slot((*chip, c))]
                out.append((remote(a, 1 + j, landed, landed, me), remote(a, 4 + j, landed, landed, sibling)))
        return out

    def start():
        local, two_way, send_only = first_copies()
        for cp in local + two_way + send_only:
            cp.start()

    def relay():
        for arrival, fwd in forwards():
            arrival.wait_recv()
            fwd.start()

    def finish():
        local, two_way, send_only = first_copies()
        for a in relays:
            blk = out_refs[a].at[slot(sibling)]
            remote(a, 0, blk, blk, me).wait_recv()
            for j, chip in enumerate(chips):
                blk = out_refs[a].at[slot((*chip, 1 - c))]
                remote(a, 4 + j, blk, blk, me).wait_recv()
        for cp in send_only + [fwd for _, fwd in forwards()]:
            cp.wait_send()
        for cp in two_way + local:
            cp.wait()

    return start, relay, finish, bool(relays)


def _hosted(body, n_in, n_out, comm, step, n_steps):
    if comm is None:
        return body, [], [], [], [], []
    srcs, modes = comm
    nc = len(srcs)

    def wrapped(*refs):
        ins, csrc = refs[:n_in], refs[n_in:n_in + nc]
        outs = refs[n_in + nc:n_in + nc + n_out]
        cout = refs[n_in + nc + n_out:n_in + 2 * nc + n_out]
        scratch = refs[n_in + 2 * nc + n_out:len(refs) - 3]
        start, relay, finish, has_relay = _comm_phases(modes, csrc, cout, *refs[len(refs) - 3:])
        s = step()
        pl.when(s == 0)(start)
        body(*ins, *outs, *scratch)
        if has_relay:
            pl.when(s == (2 * n_steps) // 3)(relay)
        pl.when(s == n_steps - 1)(finish)

    any_spec = pl.BlockSpec(memory_space=pl.ANY)
    return wrapped, [any_spec] * nc, list(srcs), [any_spec] * nc, _comm_out_shapes(srcs, modes), _comm_scratch(nc)


def exchange(srcs, modes, *, name):
    n = len(srcs)

    def body(*refs):
        start, relay, finish, has_relay = _comm_phases(modes, refs[:n], refs[n:2 * n], *refs[2 * n:])
        start()
        if has_relay:
            relay()
        finish()

    any_spec = pl.BlockSpec(memory_space=pl.ANY)
    return pl.pallas_call(
        body, name=name, in_specs=[any_spec] * n, out_specs=[any_spec] * n, out_shape=_comm_out_shapes(srcs, modes),
        scratch_shapes=_comm_scratch(n), compiler_params=pltpu.CompilerParams(has_side_effects=True))(*srcs)


def pair_sum(a, b, *, name):
    n, R, C = a.shape
    tr = _pick(n * R, 512, 16)

    def body(a_ref, b_ref, o_ref):
        o_ref[...] = (a_ref[...].astype(F32) + b_ref[...].astype(F32)).astype(BF16)

    blk = pl.BlockSpec((tr, C), lambda i: (i, 0))
    out = pl.pallas_call(
        body, name=name, grid=(n * R // tr,), in_specs=[blk, blk], out_specs=blk,
        out_shape=jax.ShapeDtypeStruct((n * R, C), BF16),
        compiler_params=_cp("parallel"))(a.reshape(n * R, C), b.reshape(n * R, C))
    return out.reshape(n, R, C)


def sum_slots(recv, *, name):
    _, R, C = recv.shape
    tr = _pick(R, 512, 8)

    def body(r_ref, o_ref):
        acc = r_ref[0].astype(F32)
        for p in range(1, N_DEV):
            acc = acc + r_ref[p].astype(F32)
        o_ref[...] = acc

    return pl.pallas_call(
        body, name=name, grid=(R // tr,),
        in_specs=[pl.BlockSpec((N_DEV, tr, C), lambda i: (0, i, 0))],
        out_specs=pl.BlockSpec((tr, C), lambda i: (i, 0)),
        out_shape=jax.ShapeDtypeStruct((R, C), F32), compiler_params=_cp("parallel"))(recv)


def adamw(gsrc, w, m, v, *, name):
    slots, R, C = gsrc.shape
    tr = _pick(R, 256, 16 if gsrc.dtype == BF16 else 8)
    c1 = 1.0 / (1.0 - ADAM_B1 ** ADAM_STEP)
    c2 = 1.0 / (1.0 - ADAM_B2 ** ADAM_STEP)

    def body(g_ref, w_ref, m_ref, v_ref, go_ref, d_ref, mo_ref, vo_ref):
        g = g_ref[0].astype(F32)
        for p in range(1, slots):
            g = g + g_ref[p].astype(F32)
        m2 = ADAM_B1 * m_ref[...] + (1.0 - ADAM_B1) * g
        v2 = ADAM_B2 * v_ref[...] + (1.0 - ADAM_B2) * (g * g)
        go_ref[...] = g
        mo_ref[...] = m2
        vo_ref[...] = v2
        d_ref[...] = -ADAM_LR * ((m2 * c1) / (jnp.sqrt(v2 * c2) + ADAM_EPS) + ADAM_WD * w_ref[...])

    blk = pl.BlockSpec((tr, C), lambda i: (i, 0))
    shp = jax.ShapeDtypeStruct((R, C), F32)
    return pl.pallas_call(
        body, name=name, grid=(R // tr,),
        in_specs=[pl.BlockSpec((slots, tr, C), lambda i: (0, i, 0)), blk, blk, blk],
        out_specs=[blk] * 4, out_shape=[shp] * 4, compiler_params=_cp("parallel"))(gsrc, w, m, v)


def _pad_cols(a, n):
    return jnp.pad(a, ((0, 0), (0, n - a.shape[1])))


def _to_rows(a, B, seq, H):
    G = SSM_GROUPS
    R = H // G
    t = a[:, :H].reshape(B, seq, G, R).transpose(0, 2, 3, 1)
    t = jnp.pad(t, ((0, 0), (0, 0), (0, HEAD_ROWS - R), (0, 0)))
    return t.reshape(B, G * HEAD_ROWS, seq)


def _from_rows(a, B, seq, H):
    G = SSM_GROUPS
    R = H // G
    t = a.reshape(B, G, HEAD_ROWS, seq)[:, :, :R].transpose(0, 3, 1, 2).reshape(B * seq, H)
    return _pad_cols(t, LANE)


def _chip_sums(grads, name):
    c_idx = lax.axis_index("c")
    keep, give = [], []
    for g in grads:
        by_chip = g.reshape((N_CHIP, 2) + g.shape[1:])
        keep.append(lax.dynamic_index_in_dim(by_chip, c_idx, axis=1, keepdims=False))
        give.append(lax.dynamic_index_in_dim(by_chip, 1 - c_idx, axis=1, keepdims=False))
    swapped = exchange(give, ["sibling"] * len(give), name="swap_" + name)
    return [pair_sum(k, s, name=f"chip_sum_{name}_{i}") for i, (k, s) in enumerate(zip(keep, swapped))]


def local_step(x, target, loc, od_w_in_t, *, B, seq):
    T, D = x.shape
    CW = D
    heads = CW // SB_HEAD_DIM
    DI = 2 * D
    H = DI // SSM_HEAD_DIM
    XW = DI + 2 * SSM_GROUPS * SSM_STATE
    in_odd = DI + XW + H
    w1_rows = in_odd // N_DEV
    q_off, k_off, v_off, gc_off, ga_off = 3 * CW, 4 * CW, 5 * CW, 2 * CW, 6 * CW
    dt_off = DI + XW

    small_packed, small_spans = _pack_rows([loc[n] for n in SMALL_SHARDED], LANE, 8)
    n0, (g_ev_in, small_all) = rmsnorm_fwd(x, loc["ev_norm_w"], name="l0_norm",
                                           comm=([loc["ev_w_in"].astype(BF16), small_packed],
                                                 ["gather", "gather_direct"]))
    p = {n: loc[n] for n in SMALL}
    for n, a in zip(SMALL_SHARDED, _unpack_rows(small_all, small_spans)):
        p[n] = _col_unshards(a)
    p["ev_w_in"] = _col_unshards(g_ev_in)
    proj0, (g_od_in_t, g_od_out, g_ev_out) = mm_nn(
        n0, p["ev_w_in"], out_dtype=BF16, name="l0_in_proj",
        comm=([od_w_in_t.astype(BF16), loc["od_w_out"].astype(BF16), loc["ev_w_out"].astype(BF16)], ["gather"] * 3))
    p["ev_w_out"] = g_ev_out.reshape(-1, D)
    w1t = g_od_in_t[:, :w1_rows].reshape(in_odd, D)
    w1t = jnp.pad(w1t, ((0, -(-(in_odd + LANE) // 256) * 256 - in_odd), (0, 0)))
    od_w_out = g_od_out.reshape(-1, D)
    (u2,) = dwconv_fwd(proj0, (0, CW), p["ev_dw_w"], p["ev_dw_b"], C=CW, seq=seq, glu=True, silu_out=False,
                       name="l0_conv")
    o, ctot = sba_fwd(proj0, B=B, seq=seq, heads=heads, q_off=q_off, k_off=k_off, v_off=v_off, name="l0_attn")
    ycat = mix0_post_fwd(u2, proj0, o, p["ev_ln_w"], p["ev_ln_b"], CW=CW, gc_off=gc_off, ga_off=ga_off,
                         name="l0_post")
    h1 = mm_nn(ycat, p["ev_w_out"], add=x, out_dtype=F32, name="l0_out_proj")

    n1 = rmsnorm_fwd(h1, p["od_norm_w"], name="l1_norm")
    proj1 = mm_nt_terms([(n1, 0, D, 0)], w1t, out_dtype=BF16, name="l1_in_proj")
    u_pre, xbc = dwconv_fwd(proj1, (DI,), p["od_conv_w"], p["od_conv_b"], C=XW, seq=seq, glu=False, silu_out=True,
                            name="l1_conv")
    bias_p, alog_p = _pad_cols(p["od_dt_bias"], LANE), _pad_cols(p["od_a_log"], LANE)
    expand = _head_expand(H, DI)
    dt, cs, dt_x, cs_x = dt_fwd(proj1, bias_p, alog_p, expand, dt_off=dt_off, name="l1_dt")
    cs_row = _to_rows(cs, B, seq, H)
    d_full = jnp.repeat(p["od_d"], SSM_HEAD_DIM, axis=1)
    y_ssd, states = scan_fwd(xbc, dt_x, cs_x, cs_row, d_full, B=B, seq=seq, DI=DI, name="l1_ssd")
    yn = gnorm_fwd(y_ssd, proj1, p["od_gnorm_w"], DI=DI, name="l1_gnorm")
    h2 = mm_nn(yn, od_w_out, add=h1, out_dtype=F32, name="l1_out_proj")

    loss, dh2, dh2b, g_final = final_loss(h2, p["final_norm_w"], target, name="loss_head")

    g_od_w_out = mm_tn(yn, dh2b, out_dtype=BF16, name="l1_dw_out")
    dyn = mm_nt_terms([(dh2b, 0, D, 0)], od_w_out, out_dtype=BF16, name="l1_d_out_proj")
    dy_ssd, dz, g_gnorm = gnorm_bwd(dyn, y_ssd, proj1, p["od_gnorm_w"], DI=DI, name="l1_gnorm_bwd")
    dxbc_c, ddt_x, dcs_x, dcs_row, dd_part = scan_bwd(xbc, dt_x, cs_x, cs_row, d_full, states, dy_ssd, B=B, seq=seq,
                                                      DI=DI, name="l1_ssd_bwd")
    g_d = dd_part.sum(axis=(0, 1)).reshape(H, SSM_HEAD_DIM).sum(axis=1)[None, :]
    draw, g_bias, g_alog = dt_bwd(ddt_x, dcs_x, _from_rows(dcs_row, B, seq, H), proj1, dt, bias_p, alog_p, expand.T,
                                  dt_off=dt_off, n_heads=H, name="l1_dt_bwd")
    dxbc, g_conv_w, g_conv_b = dwconv_bwd(dxbc_c, u_pre, proj1, (DI,), p["od_conv_w"], C=XW, seq=seq, glu=False,
                                          silu_out=True, name="l1_conv_bwd")
    tw = 512 if DI % 512 == 0 else LANE
    terms = [(dz, j, tw, j * tw) for j in range(DI // tw)]
    terms += [(dxbc, j, tw, DI + j * tw) for j in range(XW // tw)]
    terms += [(draw, 0, LANE, dt_off)]
    dn1 = mm_nn_terms(terms, w1t, out_dtype=BF16, name="l1_d_in_proj")
    g_od_w_in_t = jnp.concatenate([mm_tn(dz, n1, out_dtype=BF16, name="l1_dw_in_z"),
                                   mm_tn(dxbc, n1, out_dtype=BF16, name="l1_dw_in_xbc"),
                                   mm_tn(draw, n1, out_dtype=BF16, name="l1_dw_in_dt")], axis=0)[:in_odd]
    dh1, dh1b, g_od_norm = rmsnorm_bwd(h1, p["od_norm_w"], dn1, dh2, name="l1_norm_bwd")
    w1_pad = (-w1_rows) % 16
    l1_chip = _chip_sums([jnp.pad(g_od_w_in_t.reshape(N_DEV, w1_rows, D), ((0, 0), (0, w1_pad), (0, 0))),
                          g_od_w_out.reshape(N_DEV, -1, D)], "l1")

    g_ev_w_out = mm_tn(ycat, dh1b, out_dtype=BF16, name="l0_dw_out")
    dycat = mm_nt_terms([(dh1b, 0, D, 0)], p["ev_w_out"], out_dtype=BF16, name="l0_d_out_proj")
    du2, dgc, dga, do, g_ln_w, g_ln_b = mix0_post_bwd(dycat, u2, proj0, o, p["ev_ln_w"], p["ev_ln_b"], CW=CW,
                                                      gc_off=gc_off, ga_off=ga_off, name="l0_post_bwd")
    dq, dk, dv, (r_od_in_t, r_od_out) = sba_bwd(proj0, ctot, do, B=B, seq=seq, heads=heads, q_off=q_off, k_off=k_off,
                                                v_off=v_off, name="l0_attn_bwd", comm=(l1_chip, ["chips", "chips"]))
    dga_a, dga_b, g_dw_w, g_dw_b = dwconv_bwd(du2, None, proj0, (0, CW), p["ev_dw_w"], C=CW, seq=seq, glu=True,
                                              silu_out=False, name="l0_conv_bwd")
    pieces = [dga_a, dga_b, dgc, dq, dk, dv, dga]
    g_ev_w_in = jnp.concatenate([mm_tn(n0, pc, out_dtype=BF16, name=f"l0_dw_in_{j}") for j, pc in enumerate(pieces)],
                                axis=1)
    l0_chip = _chip_sums([_col_shards(g_ev_w_in), g_ev_w_out.reshape(N_DEV, -1, D)], "l0")
    dn0, (r_ev_in, r_ev_out) = mm_nt_terms([(pc, 0, CW, j * CW) for j, pc in enumerate(pieces)], p["ev_w_in"],
                                           out_dtype=BF16, name="l0_d_in_proj", comm=(l0_chip, ["chips", "chips"]))
    dx, _, g_ev_norm = rmsnorm_bwd(x, p["ev_norm_w"], dn0, dh1, name="l0_norm_bwd")

    small = dict(ev_norm_w=g_ev_norm, ev_dw_w=g_dw_w, ev_dw_b=g_dw_b, ev_ln_w=g_ln_w, ev_ln_b=g_ln_b,
                 od_norm_w=g_od_norm, od_conv_w=g_conv_w, od_conv_b=g_conv_b, od_dt_bias=g_bias[:, :H],
                 od_a_log=g_alog[:, :H], od_d=g_d, od_gnorm_w=g_gnorm, final_norm_w=g_final)
    received = dict(ev_w_in=r_ev_in, ev_w_out=r_ev_out, od_w_in=r_od_in_t, od_w_out=r_od_out)
    return loss, dx, small, received


BIG = ("ev_w_in", "ev_w_out", "od_w_in", "od_w_out")
SMALL = ("ev_norm_w", "ev_dw_w", "ev_dw_b", "ev_ln_w", "ev_ln_b", "od_norm_w", "od_conv_w", "od_conv_b",
         "od_dt_bias", "od_a_log", "od_d", "od_gnorm_w", "final_norm_w")
SMALL_SHARDED = ("ev_dw_w", "od_norm_w", "od_conv_w", "od_conv_b", "od_gnorm_w")
ORDER = ("ev_norm_w", "ev_w_in", "ev_dw_w", "ev_dw_b", "ev_ln_w", "ev_ln_b", "ev_w_out", "od_norm_w", "od_w_in",
         "od_conv_w", "od_conv_b", "od_dt_bias", "od_a_log", "od_d", "od_gnorm_w", "od_w_out", "final_norm_w")


def _pack_rows(arrs, width, row_align):
    parts, spans, r0 = [], [], 0
    for a in arrs:
        flat = a.reshape(-1)
        rows = -(-flat.shape[0] // (width * row_align)) * row_align
        parts.append(jnp.pad(flat, (0, rows * width - flat.shape[0])).reshape(rows, width))
        spans.append((r0, a.size, a.shape))
        r0 += rows
    return jnp.concatenate(parts, axis=0), spans


def _unpack_rows(packed, spans):
    lead = packed.shape[:-2]
    width = packed.shape[-1]
    out = []
    for r0, size, shape in spans:
        rows = -(-size // width)
        blk = packed[..., r0:r0 + rows, :].reshape(lead + (rows * width,))[..., :size]
        out.append(blk.reshape(lead + tuple(shape)))
    return out


def _col_shards(a):
    R, C8 = a.shape
    return a.reshape(R, N_DEV, C8 // N_DEV).transpose(1, 0, 2)


def _col_unshards(a):
    n, R, C = a.shape
    return a.transpose(1, 0, 2).reshape(R, n * C)


def kernel(x, ev_norm_w, ev_w_in, ev_dw_w, ev_dw_b, ev_ln_w, ev_ln_b, ev_w_out, od_norm_w, od_w_in, od_conv_w, od_conv_b, od_dt_bias, od_a_log, od_d, od_gnorm_w, od_w_out, final_norm_w, loss_target, m_ev_norm_w, m_ev_w_in, m_ev_dw_w, m_ev_dw_b, m_ev_ln_w, m_ev_ln_b, m_ev_w_out, m_od_norm_w, m_od_w_in, m_od_conv_w, m_od_conv_b, m_od_dt_bias, m_od_a_log, m_od_d, m_od_gnorm_w, m_od_w_out, m_final_norm_w, v_ev_norm_w, v_ev_w_in, v_ev_dw_w, v_ev_dw_b, v_ev_ln_w, v_ev_ln_b, v_ev_w_out, v_od_norm_w, v_od_w_in, v_od_conv_w, v_od_conv_b, v_od_dt_bias, v_od_a_log, v_od_d, v_od_gnorm_w, v_od_w_out, v_final_norm_w):
    loc = dict(ev_norm_w=ev_norm_w, ev_w_in=ev_w_in, ev_dw_w=ev_dw_w, ev_dw_b=ev_dw_b, ev_ln_w=ev_ln_w,
               ev_ln_b=ev_ln_b, ev_w_out=ev_w_out, od_norm_w=od_norm_w, od_w_in=od_w_in, od_conv_w=od_conv_w,
               od_conv_b=od_conv_b, od_dt_bias=od_dt_bias, od_a_log=od_a_log, od_d=od_d, od_gnorm_w=od_gnorm_w,
               od_w_out=od_w_out, final_norm_w=final_norm_w)
    mom = dict(ev_norm_w=m_ev_norm_w, ev_w_in=m_ev_w_in, ev_dw_w=m_ev_dw_w, ev_dw_b=m_ev_dw_b, ev_ln_w=m_ev_ln_w,
               ev_ln_b=m_ev_ln_b, ev_w_out=m_ev_w_out, od_norm_w=m_od_norm_w, od_w_in=m_od_w_in,
               od_conv_w=m_od_conv_w, od_conv_b=m_od_conv_b, od_dt_bias=m_od_dt_bias, od_a_log=m_od_a_log,
               od_d=m_od_d, od_gnorm_w=m_od_gnorm_w, od_w_out=m_od_w_out, final_norm_w=m_final_norm_w)
    var = dict(ev_norm_w=v_ev_norm_w, ev_w_in=v_ev_w_in, ev_dw_w=v_ev_dw_w, ev_dw_b=v_ev_dw_b, ev_ln_w=v_ev_ln_w,
               ev_ln_b=v_ev_ln_b, ev_w_out=v_ev_w_out, od_norm_w=v_od_norm_w, od_w_in=v_od_w_in,
               od_conv_w=v_od_conv_w, od_conv_b=v_od_conv_b, od_dt_bias=v_od_dt_bias, od_a_log=v_od_a_log,
               od_d=v_od_d, od_gnorm_w=v_od_gnorm_w, od_w_out=v_od_w_out, final_norm_w=v_final_norm_w)
    shapes = {n: loc[n].shape for n in ORDER}
    loc = {n: (a.reshape(1, -1) if a.ndim == 1 else a.reshape(a.shape[-2:]) if a.ndim == 3 else a)
           for n, a in loc.items()}
    mom = {n: a.reshape(loc[n].shape) for n, a in mom.items()}
    var = {n: a.reshape(loc[n].shape) for n, a in var.items()}

    B, seq, D = x.shape
    me = 4 * lax.axis_index("x") + 2 * lax.axis_index("y") + lax.axis_index("c")

    w1_rows = loc["od_w_in"].shape[1]
    w1_pad = (-w1_rows) % 16

    def to_t(a):
        return jnp.pad(a.T, ((0, w1_pad), (0, 0)))

    loss, dx, grads, received = local_step(x.reshape(B * seq, D), loss_target.reshape(B * seq, D), loc,
                                           to_t(loc["od_w_in"]), B=B, seq=seq)

    gsmall_packed, gsmall_spans = _pack_rows([grads[n] for n in SMALL], LANE, 8)
    (gsmall_recv,) = exchange([gsmall_packed], ["gather_direct"], name="gather_small_grads")

    big_out = [{} for _ in range(4)]
    for n in ("ev_w_in", "ev_w_out", "od_w_out"):
        for kind, a in enumerate(adamw(received[n], loc[n], mom[n], var[n], name="adamw_" + n)):
            big_out[kind][n] = a
    for kind, a in enumerate(adamw(received["od_w_in"], to_t(loc["od_w_in"]), to_t(mom["od_w_in"]),
                                   to_t(var["od_w_in"]), name="adamw_od_w_in")):
        big_out[kind]["od_w_in"] = a[:w1_rows].T

    gsmall = dict(zip(SMALL, _unpack_rows(sum_slots(gsmall_recv, name="sum_small_grads"), gsmall_spans)))
    for n in SMALL_SHARDED:
        width = loc[n].shape[1]
        gsmall[n] = lax.dynamic_slice_in_dim(gsmall[n], me * width, width, axis=1)
    gs, sspans = _pack_rows([gsmall[n] for n in SMALL], LANE, 8)
    ws, _ = _pack_rows([loc[n] for n in SMALL], LANE, 8)
    ms, _ = _pack_rows([mom[n] for n in SMALL], LANE, 8)
    vs, _ = _pack_rows([var[n] for n in SMALL], LANE, 8)
    small_out = [dict(zip(SMALL, _unpack_rows(a, sspans))) for a in adamw(gs[None], ws, ms, vs, name="adamw_small")]

    outs = [lax.psum(loss[0, 0], ("x", "y", "c")), dx.reshape(B, seq, D)]
    for kind in range(4):
        for n in ORDER:
            src = big_out[kind] if n in BIG else small_out[kind]
            outs.append(src[n].reshape(shapes[n]))
    return tuple(outs)
```

```python
import functools

import jax
import jax.numpy as jnp
from jax import lax
from jax.experimental import pallas as pl
from jax.experimental.pallas import tpu as pltpu

F32 = jnp.float32
BF16 = jnp.bfloat16

EPS = 1e-6
N_DEV = 8
LANE = 128
VMEM_LIMIT_BYTES = 48 * 1024 * 1024

SB_HEAD_DIM = 128
CONF_KERNEL = 31
SSM_CONV = 4
SSM_HEAD_DIM = 64
SSM_GROUPS = 4
SSM_STATE = 128
SSM_CHUNK = 128
HALO = 32
HEAD_ROWS = 8
NEG_BIG = -1e30

ADAM_LR = 0.001
ADAM_B1 = 0.9
ADAM_B2 = 0.999
ADAM_EPS = 1e-08
ADAM_WD = 0.01
ADAM_STEP = 10

NT = (((1,), (1,)), ((), ()))
TN = (((0,), (0,)), ((), ()))


def _cp(*sem):
    return pltpu.CompilerParams(dimension_semantics=sem, vmem_limit_bytes=VMEM_LIMIT_BYTES)


def _pick(n, cap, align):
    if n <= cap:
        return n
    t = (cap // align) * align
    while t >= align:
        if n % t == 0:
            return t
        t -= align
    raise ValueError(f"no tile for {n} (cap {cap}, align {align})")


def _sigmoid(x):
    return 0.5 * jnp.tanh(0.5 * x) + 0.5


def _silu(x):
    return x * _sigmoid(x)


def _silu_grad(x):
    s = _sigmoid(x)
    return s * (1.0 + x * (1.0 - s))


def _dot(a, b, dims=None):
    if dims is None:
        return jnp.dot(a, b, preferred_element_type=F32)
    return lax.dot_general(a, b, dims, preferred_element_type=F32)


def _tri_dot3(tri, x):
    hi = x.astype(BF16)
    r1 = x - hi.astype(F32)
    mid = r1.astype(BF16)
    lo = (r1 - mid.astype(F32)).astype(BF16)
    return _dot(tri, hi) + _dot(tri, mid) + _dot(tri, lo)


def _grid_step(n_inner):
    return lambda: pl.program_id(0) * n_inner + pl.program_id(1)


def mm_nn(a, b, *, add=None, out_dtype, name, comm=None):
    M, K = a.shape
    N = b.shape[1]
    tm = _pick(M, 2048 if K <= 1024 and add is None else 1024, 16)
    tn = _pick(N, 1024, LANE)

    def body(*refs):
        if add is None:
            a_ref, b_ref, o_ref = refs
        else:
            a_ref, b_ref, add_ref, o_ref = refs
        acc = _dot(a_ref[...], b_ref[...])
        if add is not None:
            acc = acc + add_ref[...]
        o_ref[...] = acc.astype(out_dtype)

    in_specs = [pl.BlockSpec((tm, K), lambda i, j: (i, 0)), pl.BlockSpec((K, tn), lambda i, j: (0, j))]
    args = [a, b]
    if add is not None:
        in_specs.append(pl.BlockSpec((tm, tn), lambda i, j: (i, j)))
        args.append(add)
    grid = (M // tm, N // tn)
    body, c_in, c_args, c_out, c_shape, c_scratch = _hosted(body, len(args), 1, comm, _grid_step(grid[1]),
                                                            grid[0] * grid[1])
    out = pl.pallas_call(
        body, name=name, grid=grid, in_specs=in_specs + c_in,
        out_specs=[pl.BlockSpec((tm, tn), lambda i, j: (i, j))] + c_out,
        out_shape=[jax.ShapeDtypeStruct((M, N), out_dtype)] + c_shape, scratch_shapes=c_scratch,
        compiler_params=_cp(*(("arbitrary",) * 2 if comm else ("parallel",) * 2)))(*args, *c_args)
    return (out[0], out[1:]) if comm else out[0]


def mm_nt_terms(terms, b, *, out_dtype, name, comm=None):
    M = terms[0][0].shape[0]
    N = b.shape[0]
    n_terms = len(terms)
    cap = 1024 if n_terms == 1 else 512
    tm = _pick(M, 2 * cap if n_terms == 1 else cap, 16)
    tn = _pick(N, cap, LANE)

    def body(*refs):
        o_ref = refs[-1]
        acc = None
        for t in range(n_terms):
            part = _dot(refs[2 * t][...], refs[2 * t + 1][...], NT)
            acc = part if acc is None else acc + part
        o_ref[...] = acc.astype(out_dtype)

    in_specs, args = [], []
    for arr, cb, w, off in terms:
        assert off % w == 0
        in_specs.append(pl.BlockSpec((tm, w), lambda i, j, cb=cb: (i, cb)))
        in_specs.append(pl.BlockSpec((tn, w), lambda i, j, ob=off // w: (j, ob)))
        args += [arr, b]
    grid = (M // tm, N // tn)
    body, c_in, c_args, c_out, c_shape, c_scratch = _hosted(body, len(args), 1, comm, _grid_step(grid[1]),
                                                            grid[0] * grid[1])
    out = pl.pallas_call(
        body, name=name, grid=grid, in_specs=in_specs + c_in,
        out_specs=[pl.BlockSpec((tm, tn), lambda i, j: (i, j))] + c_out,
        out_shape=[jax.ShapeDtypeStruct((M, N), out_dtype)] + c_shape, scratch_shapes=c_scratch,
        compiler_params=_cp(*(("arbitrary",) * 2 if comm else ("parallel",) * 2)))(*args, *c_args)
    return (out[0], out[1:]) if comm else out[0]


def mm_nn_terms(terms, b, *, out_dtype, name):
    M = terms[0][0].shape[0]
    N = b.shape[1]
    tm = _pick(M, 512, 16)
    tn = _pick(N, 512, LANE)
    n_terms = len(terms)

    def body(*refs):
        o_ref = refs[-1]
        acc = None
        for t in range(n_terms):
            part = _dot(refs[2 * t][...], refs[2 * t + 1][...])
            acc = part if acc is None else acc + part
        o_ref[...] = acc.astype(out_dtype)

    in_specs, args = [], []
    for arr, cb, w, off in terms:
        assert off % w == 0
        in_specs.append(pl.BlockSpec((tm, w), lambda i, j, cb=cb: (i, cb)))
        in_specs.append(pl.BlockSpec((w, tn), lambda i, j, ob=off // w: (ob, j)))
        args += [arr, b]
    return pl.pallas_call(
        body, name=name, grid=(M // tm, N // tn), in_specs=in_specs,
        out_specs=pl.BlockSpec((tm, tn), lambda i, j: (i, j)),
        out_shape=jax.ShapeDtypeStruct((M, N), out_dtype),
        compiler_params=_cp("parallel", "parallel"))(*args)


def mm_tn(a, b, *, out_dtype, name):
    T, M = a.shape
    N = b.shape[1]
    tm = _pick(M, 1024, LANE)
    tn = _pick(N, 1024, LANE)
    tk = _pick(T, 2048, 16)
    nk = T // tk

    def body(a_ref, b_ref, o_ref, acc_ref):
        k = pl.program_id(2)

        @pl.when(k == 0)
        def _():
            acc_ref[...] = jnp.zeros_like(acc_ref)

        acc_ref[...] += _dot(a_ref[...], b_ref[...], TN)

        @pl.when(k == nk - 1)
        def _():
            o_ref[...] = acc_ref[...].astype(out_dtype)

    return pl.pallas_call(
        body, name=name, grid=(M // tm, N // tn, nk),
        in_specs=[pl.BlockSpec((tk, tm), lambda i, j, k: (k, i)), pl.BlockSpec((tk, tn), lambda i, j, k: (k, j))],
        out_specs=pl.BlockSpec((tm, tn), lambda i, j, k: (i, j)),
        out_shape=jax.ShapeDtypeStruct((M, N), out_dtype),
        scratch_shapes=[pltpu.VMEM((tm, tn), F32)],
        compiler_params=_cp("parallel", "parallel", "arbitrary"))(a, b)


def rmsnorm_fwd(h, w, *, name, comm=None):
    T, D = h.shape
    tt = _pick(T, 512, 16)

    def body(h_ref, w_ref, n_ref):
        x = h_ref[...]
        r = lax.rsqrt(jnp.mean(x * x, axis=-1, keepdims=True) + EPS)
        n_ref[...] = (x * r * w_ref[...]).astype(BF16)

    body, c_in, c_args, c_out, c_shape, c_scratch = _hosted(body, 2, 1, comm, lambda: pl.program_id(0), T // tt)
    out = pl.pallas_call(
        body, name=name, grid=(T // tt,),
        in_specs=[pl.BlockSpec((tt, D), lambda i: (i, 0)), pl.BlockSpec((1, D), lambda i: (0, 0))] + c_in,
        out_specs=[pl.BlockSpec((tt, D), lambda i: (i, 0))] + c_out,
        out_shape=[jax.ShapeDtypeStruct((T, D), BF16)] + c_shape, scratch_shapes=c_scratch,
        compiler_params=_cp("arbitrary" if comm else "parallel"))(h, w, *c_args)
    return (out[0], out[1:]) if comm else out[0]


def rmsnorm_bwd(h, w, dn, dres, *, name):
    T, D = h.shape
    tt = _pick(T, 512, 16)

    def body(h_ref, w_ref, dn_ref, dres_ref, dh_ref, dhb_ref, gw_ref):
        @pl.when(pl.program_id(0) == 0)
        def _():
            gw_ref[...] = jnp.zeros_like(gw_ref)

        x = h_ref[...]
        r = lax.rsqrt(jnp.mean(x * x, axis=-1, keepdims=True) + EPS)
        xhat = x * r
        g = dn_ref[...].astype(F32)
        gw_ref[...] += jnp.sum(g * xhat, axis=0, keepdims=True)
        dxh = g * w_ref[...]
        dx = r * (dxh - xhat * jnp.mean(dxh * xhat, axis=-1, keepdims=True))
        dh = dres_ref[...] + dx
        dh_ref[...] = dh
        dhb_ref[...] = dh.astype(BF16)

    row = pl.BlockSpec((tt, D), lambda i: (i, 0))
    vec = pl.BlockSpec((1, D), lambda i: (0, 0))
    return pl.pallas_call(
        body, name=name, grid=(T // tt,), in_specs=[row, vec, row, row], out_specs=[row, row, vec],
        out_shape=[jax.ShapeDtypeStruct((T, D), F32), jax.ShapeDtypeStruct((T, D), BF16),
                   jax.ShapeDtypeStruct((1, D), F32)],
        compiler_params=_cp("arbitrary"))(h, w, dn, dres)


def final_loss(h, w, target, *, name):
    T, D = h.shape
    tt = _pick(T, 512, 16)

    def body(h_ref, w_ref, t_ref, loss_ref, dh_ref, dhb_ref, gw_ref):
        @pl.when(pl.program_id(0) == 0)
        def _():
            gw_ref[...] = jnp.zeros_like(gw_ref)
            loss_ref[...] = jnp.zeros_like(loss_ref)

        x = h_ref[...]
        r = lax.rsqrt(jnp.mean(x * x, axis=-1, keepdims=True) + EPS)
        xhat = x * r
        e = xhat * w_ref[...] - t_ref[...]
        loss_ref[...] += jnp.sum(e * e) * (0.5 / D)
        g = e * (1.0 / D)
        gw_ref[...] += jnp.sum(g * xhat, axis=0, keepdims=True)
        dxh = g * w_ref[...]
        dh = r * (dxh - xhat * jnp.mean(dxh * xhat, axis=-1, keepdims=True))
        dh_ref[...] = dh
        dhb_ref[...] = dh.astype(BF16)

    row = pl.BlockSpec((tt, D), lambda i: (i, 0))
    vec = pl.BlockSpec((1, D), lambda i: (0, 0))
    one = pl.BlockSpec((1, LANE), lambda i: (0, 0))
    return pl.pallas_call(
        body, name=name, grid=(T // tt,), in_specs=[row, vec, row], out_specs=[one, row, row, vec],
        out_shape=[jax.ShapeDtypeStruct((1, LANE), F32), jax.ShapeDtypeStruct((T, D), F32),
                   jax.ShapeDtypeStruct((T, D), BF16), jax.ShapeDtypeStruct((1, D), F32)],
        compiler_params=_cp("arbitrary"))(h, w, target)


CONV_CHUNK = 32
SUBLANES = 8


def _conv_tiles(seq, C):
    return _pick(seq, 512, HALO), _pick(C, 512, LANE)


def _residues(offsets):
    return sorted({s % SUBLANES for s in offsets} - {0})


def _fill_shifted(buf, shifted, residues):
    n = buf.shape[0] - SUBLANES
    for i, r in enumerate(residues):
        shifted[i, 0:n, :] = buf[r:r + n, :]


def _tap(buf, shifted, residues, offset, start, rows):
    r = offset % SUBLANES
    base = offset - r
    ref = buf if r == 0 else shifted.at[residues.index(r)]
    return ref[pl.ds(start + base, rows), :]


def dwconv_fwd(src, offs, w, b, *, C, seq, glu, silu_out, name):
    T = src.shape[0]
    K = w.shape[0]
    tt, tc = _conv_tiles(seq, C)
    n_in = 2 if glu else 1
    per = tt // HALO
    offsets = [HALO - (K - 1) + k for k in range(K)]
    residues = _residues(offsets)

    def body(*refs):
        cur = refs[0:2 * n_in:2]
        halo = refs[1:2 * n_in:2]
        w_ref, b_ref = refs[2 * n_in], refs[2 * n_in + 1]
        outs = refs[2 * n_in + 2:-2]
        buf, shifted = refs[-2], refs[-1]
        i = pl.program_id(1)
        first = (i * tt) % seq == 0

        def pre(rs):
            v = rs[0][...].astype(F32)
            return v * _sigmoid(rs[1][...].astype(F32)) if glu else v

        buf[0:HALO, :] = jnp.where(first, 0.0, pre(halo))
        buf[HALO:HALO + tt, :] = pre(cur)
        _fill_shifted(buf, shifted, residues)

        def chunk(ci, carry):
            start = pl.multiple_of(ci * CONV_CHUNK, CONV_CHUNK)
            acc = jnp.broadcast_to(b_ref[...], (CONV_CHUNK, tc))
            for k in range(K):
                acc = acc + w_ref[k:k + 1, :] * _tap(buf, shifted, residues, offsets[k], start, CONV_CHUNK)
            outs[0][pl.ds(start, CONV_CHUNK), :] = acc.astype(BF16)
            if silu_out:
                outs[1][pl.ds(start, CONV_CHUNK), :] = _silu(acc).astype(BF16)
            return carry

        lax.fori_loop(0, tt // CONV_CHUNK, chunk, 0)

    in_specs, args = [], []
    for off in offs:
        assert off % tc == 0
        in_specs.append(pl.BlockSpec((tt, tc), lambda j, i, ob=off // tc: (i, ob + j)))
        in_specs.append(pl.BlockSpec((HALO, tc), lambda j, i, ob=off // tc: (jnp.maximum(i * per - 1, 0), ob + j)))
        args += [src, src]
    in_specs += [pl.BlockSpec((K, tc), lambda j, i: (0, j)), pl.BlockSpec((1, tc), lambda j, i: (0, j))]
    args += [w, b]
    n_out = 2 if silu_out else 1
    out = pl.pallas_call(
        body, name=name, grid=(C // tc, T // tt), in_specs=in_specs,
        out_specs=[pl.BlockSpec((tt, tc), lambda j, i: (i, j))] * n_out,
        out_shape=[jax.ShapeDtypeStruct((T, C), BF16)] * n_out,
        scratch_shapes=[pltpu.VMEM((HALO + tt, tc), F32), pltpu.VMEM((max(len(residues), 1), HALO + tt, tc), F32)],
        compiler_params=_cp("parallel", "arbitrary"))(*args)
    return out


def dwconv_bwd(du, u, src, offs, w, *, C, seq, glu, silu_out, name):
    T = src.shape[0]
    K = w.shape[0]
    tt, tc = _conv_tiles(seq, C)
    n_in = 2 if glu else 1
    per = tt // HALO
    last_blk = T // HALO - 1
    g_offsets = [K - 1 - k for k in range(K)]
    x_offsets = [HALO - (K - 1) + k for k in range(K)]
    g_res, x_res = _residues(g_offsets), _residues(x_offsets)

    def body(*refs):
        pos = 0
        du_cur, du_nxt = refs[0], refs[1]
        pos = 2
        if silu_out:
            u_cur, u_nxt = refs[2], refs[3]
            pos = 4
        cur = refs[pos:pos + 2 * n_in:2]
        halo = refs[pos + 1:pos + 2 * n_in:2]
        pos += 2 * n_in
        w_ref = refs[pos]
        outs = refs[pos + 1:pos + 1 + n_in]
        dw_ref, db_ref = refs[pos + 1 + n_in], refs[pos + 2 + n_in]
        gbuf, gshift, xbuf, xshift, dw_acc, db_acc = refs[-6:]
        i = pl.program_id(1)
        first = (i * tt) % seq == 0
        last = ((i + 1) * tt) % seq == 0

        @pl.when(i == 0)
        def _():
            dw_acc[...] = jnp.zeros_like(dw_acc)
            db_acc[...] = jnp.zeros_like(db_acc)

        g_cur = du_cur[...].astype(F32)
        g_nxt = du_nxt[...].astype(F32)
        if silu_out:
            g_cur = g_cur * _silu_grad(u_cur[...].astype(F32))
            g_nxt = g_nxt * _silu_grad(u_nxt[...].astype(F32))
        gbuf[0:tt, :] = g_cur
        gbuf[tt:tt + HALO, :] = jnp.where(last, 0.0, g_nxt)

        def pre(rs):
            v = rs[0][...].astype(F32)
            return v * _sigmoid(rs[1][...].astype(F32)) if glu else v

        xbuf[0:HALO, :] = jnp.where(first, 0.0, pre(halo))
        xbuf[HALO:HALO + tt, :] = pre(cur)
        _fill_shifted(gbuf, gshift, g_res)
        _fill_shifted(xbuf, xshift, x_res)

        def fold(v):
            out = v[0:SUBLANES]
            for s in range(SUBLANES, CONV_CHUNK, SUBLANES):
                out = out + v[s:s + SUBLANES]
            return out

        def chunk(ci, carry):
            start = pl.multiple_of(ci * CONV_CHUNK, CONV_CHUNK)
            g = gbuf[pl.ds(start, CONV_CHUNK), :]
            dx = jnp.zeros((CONV_CHUNK, tc), F32)
            for k in range(K):
                dx = dx + w_ref[k:k + 1, :] * _tap(gbuf, gshift, g_res, g_offsets[k], start, CONV_CHUNK)
                dw_acc[k * SUBLANES:(k + 1) * SUBLANES, :] += fold(
                    g * _tap(xbuf, xshift, x_res, x_offsets[k], start, CONV_CHUNK))
            db_acc[...] += fold(g)
            rows = pl.ds(start, CONV_CHUNK)
            if glu:
                a = cur[0][rows, :].astype(F32)
                s = _sigmoid(cur[1][rows, :].astype(F32))
                outs[0][rows, :] = (dx * s).astype(BF16)
                outs[1][rows, :] = (dx * a * s * (1.0 - s)).astype(BF16)
            else:
                outs[0][rows, :] = dx.astype(BF16)
            return carry

        lax.fori_loop(0, tt // CONV_CHUNK, chunk, 0)

        @pl.when(i == T // tt - 1)
        def _():
            for k in range(K):
                dw_ref[k:k + 1, :] = jnp.sum(dw_acc[k * SUBLANES:(k + 1) * SUBLANES, :], axis=0, keepdims=True)
            db_ref[...] = jnp.sum(db_acc[...], axis=0, keepdims=True)

    def cur_spec(ob):
        return pl.BlockSpec((tt, tc), lambda j, i: (i, ob + j))

    def nxt_spec(ob):
        return pl.BlockSpec((HALO, tc), lambda j, i: (jnp.minimum((i + 1) * per, last_blk), ob + j))

    def prv_spec(ob):
        return pl.BlockSpec((HALO, tc), lambda j, i: (jnp.maximum(i * per - 1, 0), ob + j))

    in_specs = [cur_spec(0), nxt_spec(0)]
    args = [du, du]
    if silu_out:
        in_specs += [cur_spec(0), nxt_spec(0)]
        args += [u, u]
    for off in offs:
        assert off % tc == 0
        in_specs += [cur_spec(off // tc), prv_spec(off // tc)]
        args += [src, src]
    in_specs.append(pl.BlockSpec((K, tc), lambda j, i: (0, j)))
    args.append(w)
    out_specs = [pl.BlockSpec((tt, tc), lambda j, i: (i, j))] * n_in
    out_specs += [pl.BlockSpec((K, tc), lambda j, i: (0, j)), pl.BlockSpec((1, tc), lambda j, i: (0, j))]
    out_shape = [jax.ShapeDtypeStruct((T, C), BF16)] * n_in
    out_shape += [jax.ShapeDtypeStruct((K, C), F32), jax.ShapeDtypeStruct((1, C), F32)]
    return pl.pallas_call(
        body, name=name, grid=(C // tc, T // tt), in_specs=in_specs, out_specs=out_specs, out_shape=out_shape,
        scratch_shapes=[pltpu.VMEM((tt + HALO, tc), F32), pltpu.VMEM((max(len(g_res), 1), tt + HALO, tc), F32),
                        pltpu.VMEM((HALO + tt, tc), F32), pltpu.VMEM((max(len(x_res), 1), HALO + tt, tc), F32),
                        pltpu.VMEM((K * SUBLANES, tc), F32), pltpu.VMEM((SUBLANES, tc), F32)],
        compiler_params=_cp("parallel", "arbitrary"))(*args)


def mix0_post_fwd(u2, proj, o, ln_w, ln_b, *, CW, gc_off, ga_off, name):
    T = u2.shape[0]
    tt = _pick(T, 256, 16)

    def body(u_ref, gc_ref, ga_ref, o_ref, lw_ref, lb_ref, y_ref):
        u = u_ref[...].astype(F32)
        mu = jnp.mean(u, axis=-1, keepdims=True)
        xc = u - mu
        r = lax.rsqrt(jnp.mean(xc * xc, axis=-1, keepdims=True) + EPS)
        u3 = xc * r * lw_ref[...] + lb_ref[...]
        y_ref[:, 0:CW] = (_silu(u3) * _silu(gc_ref[...].astype(F32))).astype(BF16)
        y_ref[:, CW:2 * CW] = (o_ref[...].astype(F32) * _silu(ga_ref[...].astype(F32))).astype(BF16)

    row = pl.BlockSpec((tt, CW), lambda i: (i, 0))
    vec = pl.BlockSpec((1, CW), lambda i: (0, 0))
    return pl.pallas_call(
        body, name=name, grid=(T // tt,),
        in_specs=[row, pl.BlockSpec((tt, CW), lambda i: (i, gc_off // CW)),
                  pl.BlockSpec((tt, CW), lambda i: (i, ga_off // CW)), row, vec, vec],
        out_specs=pl.BlockSpec((tt, 2 * CW), lambda i: (i, 0)),
        out_shape=jax.ShapeDtypeStruct((T, 2 * CW), BF16),
        compiler_params=_cp("parallel"))(u2, proj, proj, o, ln_w, ln_b)


def mix0_post_bwd(dy, u2, proj, o, ln_w, ln_b, *, CW, gc_off, ga_off, name):
    T = u2.shape[0]
    tt = _pick(T, 256, 16)

    def body(dy_ref, u_ref, gc_ref, ga_ref, o_ref, lw_ref, lb_ref, du_ref, dgc_ref, dga_ref, do_ref, dlw_ref, dlb_ref):
        @pl.when(pl.program_id(0) == 0)
        def _():
            dlw_ref[...] = jnp.zeros_like(dlw_ref)
            dlb_ref[...] = jnp.zeros_like(dlb_ref)

        dyc = dy_ref[:, 0:CW].astype(F32)
        dya = dy_ref[:, CW:2 * CW].astype(F32)
        u = u_ref[...].astype(F32)
        mu = jnp.mean(u, axis=-1, keepdims=True)
        xc = u - mu
        r = lax.rsqrt(jnp.mean(xc * xc, axis=-1, keepdims=True) + EPS)
        xhat = xc * r
        u3 = xhat * lw_ref[...] + lb_ref[...]
        gc = gc_ref[...].astype(F32)
        dgc_ref[...] = (dyc * _silu(u3) * _silu_grad(gc)).astype(BF16)
        du3 = dyc * _silu(gc) * _silu_grad(u3)
        dlw_ref[...] += jnp.sum(du3 * xhat, axis=0, keepdims=True)
        dlb_ref[...] += jnp.sum(du3, axis=0, keepdims=True)
        dxh = du3 * lw_ref[...]
        du = r * (dxh - jnp.mean(dxh, axis=-1, keepdims=True) - xhat * jnp.mean(dxh * xhat, axis=-1, keepdims=True))
        du_ref[...] = du.astype(BF16)
        ga = ga_ref[...].astype(F32)
        ov = o_ref[...].astype(F32)
        do_ref[...] = (dya * _silu(ga)).astype(BF16)
        dga_ref[...] = (dya * ov * _silu_grad(ga)).astype(BF16)

    row = pl.BlockSpec((tt, CW), lambda i: (i, 0))
    vec = pl.BlockSpec((1, CW), lambda i: (0, 0))
    big = jax.ShapeDtypeStruct((T, CW), BF16)
    small = jax.ShapeDtypeStruct((1, CW), F32)
    return pl.pallas_call(
        body, name=name, grid=(T // tt,),
        in_specs=[pl.BlockSpec((tt, 2 * CW), lambda i: (i, 0)), row,
                  pl.BlockSpec((tt, CW), lambda i: (i, gc_off // CW)),
                  pl.BlockSpec((tt, CW), lambda i: (i, ga_off // CW)), row, vec, vec],
        out_specs=[row, row, row, row, vec, vec],
        out_shape=[big, big, big, big, small, small],
        compiler_params=_cp("arbitrary"))(dy, u2, proj, proj, o, ln_w, ln_b)


SB_UNDERFLOW = 110.0
SB_BOUND_MARGIN = 1.02


def _sb_tile(seq):
    return _pick(seq, 256, LANE)


def _softplus(z):
    return jnp.maximum(z, 0.0) + jnp.log(1.0 + jnp.exp(-jnp.abs(z)))


def _tri01(n, lower):
    i = lax.broadcasted_iota(jnp.int32, (n, n), 0)
    j = lax.broadcasted_iota(jnp.int32, (n, n), 1)
    return ((i >= j) if lower else (i <= j)).astype(BF16)


SB_HEADS_FWD = 4
SB_HEADS_BWD = 2


def _sb_heads_per_step(heads, want):
    while heads % want:
        want //= 2
    return want


def sba_fwd(proj, *, B, seq, heads, q_off, k_off, v_off, name):
    dh = SB_HEAD_DIM
    tq = _sb_tile(seq)
    assert tq % (2 * LANE) == 0
    nq = seq // tq
    hps = _sb_heads_per_step(heads, SB_HEADS_FWD)
    hw = hps * dh
    scale = dh ** -0.5

    def body(q_ref, k_ref, v_ref, tri_ref, o_ref, ct_ref, acc_ref, kmax_ref):
        qi = pl.program_id(1)
        tri = tri_ref[...]
        qs = [(q_ref[:, h * dh:(h + 1) * dh].astype(F32) * scale).astype(BF16) for h in range(hps)]

        @pl.when(qi == 0)
        def _():
            def chunk(i, best):
                rows = k_ref[pl.ds(pl.multiple_of(i * tq, tq), tq), :].astype(F32)
                sq = rows * rows
                return tuple(jnp.maximum(best[h], jnp.max(jnp.sum(sq[:, h * dh:(h + 1) * dh], axis=1, keepdims=True),
                                                          axis=0, keepdims=True)) for h in range(hps))

            best = lax.fori_loop(0, nq, chunk, (jnp.zeros((1, 1), F32),) * hps)
            for h in range(hps):
                kmax_ref[h] = jnp.broadcast_to(jnp.sqrt(best[h]), (8, LANE))

        z_bound = [jnp.sqrt(jnp.sum(qs[h].astype(F32) ** 2, axis=1, keepdims=True))
                   * (SB_BOUND_MARGIN * jnp.max(kmax_ref[h], keepdims=True)) for h in range(hps)]

        def part(h, q_rows, start, n_keys, r, mask):
            k_blk = k_ref[pl.ds(start, n_keys), h * dh:(h + 1) * dh]
            v_blk = v_ref[pl.ds(start, n_keys), h * dh:(h + 1) * dh]
            z = _dot(q_rows, k_blk, NT)
            sp = _softplus(z)
            if mask is not None:
                sp = jnp.where(mask, sp, 0.0)
            wts = jnp.exp(z - (_dot(sp.astype(BF16), tri[0:n_keys, 0:n_keys]) + r))
            if mask is not None:
                wts = jnp.where(mask, wts, 0.0)
            return _dot(wts.astype(BF16), v_blk), r + jnp.sum(sp, axis=-1, keepdims=True)

        below = lax.broadcasted_iota(jnp.int32, (tq, tq), 1) < lax.broadcasted_iota(jnp.int32, (tq, tq), 0)
        has_left = qi > 0
        left = pl.multiple_of(jnp.maximum(qi - 1, 0) * tq, tq)
        rs = []
        for h in range(hps):
            pv_d, r = part(h, qs[h], pl.multiple_of(qi * tq, tq), tq, jnp.zeros((tq, 1), F32), below)
            pv_l, r = part(h, qs[h], left, tq, r, has_left)
            acc_ref[:, h * dh:(h + 1) * dh] = pv_d + pv_l
            rs.append(r)
        rs = tuple(rs)

        def block(start, rs):
            pvs, out = [], []
            for h in range(hps):
                pv, r = part(h, qs[h], start, tq, rs[h], None)
                pvs.append(pv)
                out.append(r)
            return pvs, tuple(out)

        def more(c):
            j, rs = c
            slack = rs[0] - z_bound[0]
            for h in range(1, hps):
                slack = jnp.minimum(slack, rs[h] - z_bound[h])
            return jnp.logical_and(j < qi, jnp.min(slack) <= SB_UNDERFLOW)

        def step(c):
            j, rs = c
            pvs, rs = block(pl.multiple_of((qi - 1 - j) * tq, tq), rs)
            for h in range(hps):
                acc_ref[:, h * dh:(h + 1) * dh] += pvs[h]
            return j + 1, rs

        n_left, totals = lax.while_loop(more, step, (has_left.astype(jnp.int32), rs))
        o_ref[...] = acc_ref[...].astype(BF16)
        for h in range(hps):
            ct_ref[0, 0, h, 0:8, :] = jnp.broadcast_to(totals[h], (tq, LANE)).T[0:8, :]
            ct_ref[0, 0, h, 8:16, :] = jnp.full((8, tq), n_left, F32)

    qb, kb, vb = q_off // hw, k_off // hw, v_off // hw
    G = heads // hps
    return pl.pallas_call(
        body, name=name, grid=(B * G, nq),
        in_specs=[pl.BlockSpec((tq, hw), lambda g, i: ((g // G) * nq + i, qb + g % G)),
                  pl.BlockSpec((seq, hw), lambda g, i: (g // G, kb + g % G)),
                  pl.BlockSpec((seq, hw), lambda g, i: (g // G, vb + g % G)),
                  pl.BlockSpec((tq, tq), lambda g, i: (0, 0))],
        out_specs=[pl.BlockSpec((tq, hw), lambda g, i: ((g // G) * nq + i, g % G)),
                   pl.BlockSpec((1, 1, hps, 16, tq), lambda g, i: (g // G, i, g % G, 0, 0))],
        out_shape=[jax.ShapeDtypeStruct((B * seq, heads * dh), BF16),
                   jax.ShapeDtypeStruct((B, nq, heads, 16, tq), F32)],
        scratch_shapes=[pltpu.VMEM((tq, hw), F32), pltpu.VMEM((hps, 8, LANE), F32)],
        compiler_params=_cp("parallel", "arbitrary"))(proj, proj, proj, jnp.tril(jnp.ones((tq, tq), BF16)))


def sba_bwd(proj, ctot, do, *, B, seq, heads, q_off, k_off, v_off, name, comm=None):
    dh = SB_HEAD_DIM
    tq = _sb_tile(seq)
    nq = seq // tq
    hps = _sb_heads_per_step(heads, SB_HEADS_BWD)
    hw = hps * dh
    scale = dh ** -0.5

    def body(q_ref, k_ref, v_ref, ct_ref, do_ref, sfx_ref, pre_ref, dq_ref, dk_ref, dv_ref, dq_acc, dk_acc, dv_acc):
        qi = pl.program_id(1)

        @pl.when(qi == 0)
        def _():
            dk_acc[...] = jnp.zeros_like(dk_acc)
            dv_acc[...] = jnp.zeros_like(dv_acc)

        tri_sfx = sfx_ref[...]
        tri_pre = pre_ref[...]
        qs = [(q_ref[:, h * dh:(h + 1) * dh].astype(F32) * scale).astype(BF16) for h in range(hps)]
        dos = [do_ref[:, h * dh:(h + 1) * dh] for h in range(hps)]
        totals = [jnp.max(jnp.broadcast_to(ct_ref[0, 0, h, 0:1, :], (LANE, tq)).T, axis=1, keepdims=True)
                  for h in range(hps)]
        dq_acc[...] = jnp.zeros_like(dq_acc)

        def part(h, rows, start, n_keys, pc, pg, mask):
            cols = slice(h * dh, (h + 1) * dh)
            q_rows, do_rows = qs[h][rows], dos[h][rows]
            k_blk = k_ref[pl.ds(start, n_keys), cols]
            v_blk = v_ref[pl.ds(start, n_keys), cols]
            z = _dot(q_rows, k_blk, NT)
            sp = _softplus(z)
            sig = jnp.exp(z - sp)
            if mask is not None:
                sp = jnp.where(mask, sp, 0.0)
            pc_next = pc + jnp.sum(sp, axis=-1, keepdims=True)
            wts = jnp.exp(z - (_dot(sp.astype(BF16), tri_sfx[0:n_keys, 0:n_keys]) + (totals[h][rows] - pc_next)))
            if mask is not None:
                wts = jnp.where(mask, wts, 0.0)
            g = _dot(do_rows, v_blk, NT) * wts
            dz = g - sig * (_dot(g.astype(BF16), tri_pre[0:n_keys, 0:n_keys]) + pg)
            if mask is not None:
                dz = jnp.where(mask, dz, 0.0)
            dz = dz.astype(BF16)
            dq_acc[rows, cols] += _dot(dz, k_blk)
            dk_acc[pl.ds(start, n_keys), cols] += _dot(dz, q_rows, TN)
            dv_acc[pl.ds(start, n_keys), cols] += _dot(wts.astype(BF16), do_rows, TN)
            return pc_next, pg + jnp.sum(g, axis=-1, keepdims=True)

        def block(start, carry):
            return tuple(part(h, slice(0, tq), start, tq, carry[h][0], carry[h][1], None) for h in range(hps))

        zero = jnp.zeros((tq, 1), F32)
        n_left = jnp.max(ct_ref[0, 0, 0, 8:16, :]).astype(jnp.int32)
        carry = lax.fori_loop(qi - n_left, qi - 1, lambda j, c: block(pl.multiple_of(j * tq, tq), c),
                              ((zero, zero),) * hps)
        below = lax.broadcasted_iota(jnp.int32, (tq, tq), 1) < lax.broadcasted_iota(jnp.int32, (tq, tq), 0)
        has_left = n_left > 0
        left = pl.multiple_of(jnp.maximum(qi - 1, 0) * tq, tq)
        for h in range(hps):
            pc, pg = part(h, slice(0, tq), left, tq, carry[h][0], carry[h][1], has_left)
            part(h, slice(0, tq), pl.multiple_of(qi * tq, tq), tq, pc, pg, below)
        dq_ref[...] = (dq_acc[...] * scale).astype(BF16)

        @pl.when(qi == nq - 1)
        def _():
            dk_ref[...] = dk_acc[...].astype(BF16)
            dv_ref[...] = dv_acc[...].astype(BF16)

    qb, kb, vb = q_off // hw, k_off // hw, v_off // hw
    G = heads // hps
    q_spec = pl.BlockSpec((tq, hw), lambda g, i: ((g // G) * nq + i, qb + g % G))
    o_spec = pl.BlockSpec((tq, hw), lambda g, i: ((g // G) * nq + i, g % G))
    kv_out = pl.BlockSpec((seq, hw), lambda g, i: (g // G, g % G))
    shp = jax.ShapeDtypeStruct((B * seq, heads * dh), BF16)
    body, c_in, c_args, c_out, c_shape, c_scratch = _hosted(body, 7, 3, comm, _grid_step(nq), B * G * nq)
    tri_spec = pl.BlockSpec((tq, tq), lambda g, i: (0, 0))
    ones = jnp.ones((tq, tq), BF16)
    out = pl.pallas_call(
        body, name=name, grid=(B * G, nq),
        in_specs=[q_spec,
                  pl.BlockSpec((seq, hw), lambda g, i: (g // G, kb + g % G)),
                  pl.BlockSpec((seq, hw), lambda g, i: (g // G, vb + g % G)),
                  pl.BlockSpec((1, 1, hps, 16, tq), lambda g, i: (g // G, i, g % G, 0, 0)), o_spec,
                  tri_spec, tri_spec] + c_in,
        out_specs=[o_spec, kv_out, kv_out] + c_out, out_shape=[shp, shp, shp] + c_shape,
        scratch_shapes=[pltpu.VMEM((tq, hw), F32), pltpu.VMEM((seq, hw), F32), pltpu.VMEM((seq, hw), F32)]
        + c_scratch,
        compiler_params=_cp("arbitrary" if comm else "parallel", "arbitrary"))(
            proj, proj, proj, ctot, do, jnp.tril(ones), jnp.triu(ones), *c_args)
    return (out[0], out[1], out[2], out[3:]) if comm else out


def _head_expand(n_heads, DI):
    j = jnp.arange(LANE, dtype=jnp.int32)[:, None]
    c = jnp.arange(DI, dtype=jnp.int32)[None, :] // SSM_HEAD_DIM
    return ((j == c) & (j < n_heads)).astype(BF16)


def _split3(x):
    hi = x.astype(BF16)
    r1 = x - hi.astype(F32)
    mid = r1.astype(BF16)
    return hi, mid, (r1 - mid.astype(F32)).astype(BF16)


def dt_fwd(proj, bias, a_log, expand, *, dt_off, name):
    T = proj.shape[0]
    DI = expand.shape[1]
    L = SSM_CHUNK
    tt = _pick(T, 512, L)

    def body(raw_ref, bias_ref, al_ref, e_ref, dt_ref, cs_ref, dtx_ref, csx_ref):
        x = raw_ref[...].astype(F32) + bias_ref[...]
        dt = _softplus(x)
        dt_ref[...] = dt
        la = dt * (-jnp.exp(al_ref[...]))
        tri = _tri01(L, True)
        for c in range(tt // L):
            cs_ref[c * L:(c + 1) * L, :] = _tri_dot3(tri, la[c * L:(c + 1) * L, :])
        e = e_ref[...]
        dtx_ref[...] = _dot(dt.astype(BF16), e).astype(BF16)
        hi, mid, lo = _split3(cs_ref[...])
        csx_ref[...] = _dot(hi, e) + _dot(mid, e) + _dot(lo, e)

    row = pl.BlockSpec((tt, LANE), lambda i: (i, 0))
    wide = pl.BlockSpec((tt, DI), lambda i: (i, 0))
    vec = pl.BlockSpec((1, LANE), lambda i: (0, 0))
    return pl.pallas_call(
        body, name=name, grid=(T // tt,),
        in_specs=[pl.BlockSpec((tt, LANE), lambda i: (i, dt_off // LANE)), vec, vec,
                  pl.BlockSpec((LANE, DI), lambda i: (0, 0))],
        out_specs=[row, row, wide, wide],
        out_shape=[jax.ShapeDtypeStruct((T, LANE), F32), jax.ShapeDtypeStruct((T, LANE), F32),
                   jax.ShapeDtypeStruct((T, DI), BF16), jax.ShapeDtypeStruct((T, DI), F32)],
        compiler_params=_cp("parallel"))(proj, bias, a_log, expand)


def dt_bwd(ddt_x, dcs_x, dcs_cols, proj, dt, bias, a_log, reduce_t, *, dt_off, n_heads, name):
    T = proj.shape[0]
    DI = reduce_t.shape[0]
    L = SSM_CHUNK
    tt = _pick(T, 512, L)

    def body(ddtx_ref, dcsx_ref, dcsc_ref, raw_ref, dt_ref, bias_ref, al_ref, r_ref, draw_ref, dbias_ref, dal_ref,
             dla_buf):
        @pl.when(pl.program_id(0) == 0)
        def _():
            dbias_ref[...] = jnp.zeros_like(dbias_ref)
            dal_ref[...] = jnp.zeros_like(dal_ref)

        r = r_ref[...]
        ddt = _dot(ddtx_ref[...], r)
        dx = dcsx_ref[...]
        hi = dx.astype(BF16)
        dcs = _dot(hi, r) + _dot((dx - hi.astype(F32)).astype(BF16), r) + dcsc_ref[...]
        triu = _tri01(L, False)
        for c in range(tt // L):
            dla_buf[c * L:(c + 1) * L, :] = _tri_dot3(triu, dcs[c * L:(c + 1) * L, :])
        dla = dla_buf[...]
        a = -jnp.exp(al_ref[...])
        valid = lax.broadcasted_iota(jnp.int32, (tt, LANE), 1) < n_heads
        dal_ref[...] += jnp.sum(jnp.where(valid, dla * dt_ref[...], 0.0), axis=0, keepdims=True) * a
        x = raw_ref[...].astype(F32) + bias_ref[...]
        draw = jnp.where(valid, (ddt + dla * a) * _sigmoid(x), 0.0)
        dbias_ref[...] += jnp.sum(draw, axis=0, keepdims=True)
        draw_ref[...] = draw.astype(BF16)

    row = pl.BlockSpec((tt, LANE), lambda i: (i, 0))
    wide = pl.BlockSpec((tt, DI), lambda i: (i, 0))
    vec = pl.BlockSpec((1, LANE), lambda i: (0, 0))
    return pl.pallas_call(
        body, name=name, grid=(T // tt,),
        in_specs=[wide, wide, row, pl.BlockSpec((tt, LANE), lambda i: (i, dt_off // LANE)), row, vec, vec,
                  pl.BlockSpec((DI, LANE), lambda i: (0, 0))],
        out_specs=[row, vec, vec],
        out_shape=[jax.ShapeDtypeStruct((T, LANE), BF16), jax.ShapeDtypeStruct((1, LANE), F32),
                   jax.ShapeDtypeStruct((1, LANE), F32)],
        scratch_shapes=[pltpu.VMEM((tt, LANE), F32)],
        compiler_params=_cp("arbitrary"))(ddt_x, dcs_x, dcs_cols, proj, dt, bias, a_log, reduce_t)


def _pair_terms(x_ref, dtx_ref, csx_ref, csr_ref, pair, ppg, lo_half, causal):
    L = SSM_CHUNK
    g, pp = divmod(pair, ppg)
    ra = g * HEAD_ROWS + 2 * pp
    cols = slice(pair * LANE, (pair + 1) * LANE)
    X = x_ref[:, cols].astype(F32)
    dt_p = dtx_ref[:, cols].astype(F32)
    own = csx_ref[:, cols]
    other = pltpu.roll(own, SSM_HEAD_DIM, 1)
    csa_c = jnp.where(lo_half, own, other)
    csb_c = jnp.where(lo_half, other, own)
    La = jnp.exp(jnp.where(causal, csa_c - csr_ref[0, ra:ra + 1, :], NEG_BIG))
    Lb = jnp.exp(jnp.where(causal, csb_c - csr_ref[0, ra + 1:ra + 2, :], NEG_BIG))
    last = csx_ref[L - 1:L, cols]
    return g, ra, cols, X, dt_p, La, Lb, jnp.exp(own), jnp.exp(last - own), jnp.exp(last)


def scan_fwd(xbc, dt_x, cs_x, cs_row, d_full, *, B, seq, DI, name):
    L, N, G = SSM_CHUNK, SSM_STATE, SSM_GROUPS
    nc = seq // L
    XW = xbc.shape[1]
    n_pairs = DI // LANE
    ppg = n_pairs // G

    def body(x_ref, dtx_ref, csx_ref, csr_ref, d_ref, y_ref, st_ref, state):
        c = pl.program_id(1)

        @pl.when(c == 0)
        def _():
            state[...] = jnp.zeros_like(state)

        causal = lax.broadcasted_iota(jnp.int32, (L, L), 0) >= lax.broadcasted_iota(jnp.int32, (L, L), 1)
        lo_half = lax.broadcasted_iota(jnp.int32, (L, LANE), 1) < SSM_HEAD_DIM
        cbs = []
        for g in range(G):
            Bc = x_ref[:, DI + g * N:DI + (g + 1) * N]
            Cc = x_ref[:, DI + G * N + g * N:DI + G * N + (g + 1) * N]
            cbs.append((Bc, Cc, _dot(Cc, Bc, NT)))
        for pair in range(n_pairs):
            g, _, cols, X, dt_p, La, Lb, ecs, tail, e_last = _pair_terms(
                x_ref, dtx_ref, csx_ref, csr_ref, pair, ppg, lo_half, causal)
            Bc, Cc, CB = cbs[g]
            xs = X * dt_p
            xsb = xs.astype(BF16)
            y = jnp.where(lo_half, _dot((CB * La).astype(BF16), xsb), _dot((CB * Lb).astype(BF16), xsb))
            ST = state[pair]
            st_ref[0, 0, pair] = ST
            y = y + ecs * _dot(Cc, ST.astype(BF16)) + d_ref[:, cols] * X
            y_ref[:, cols] = y.astype(BF16)
            state[pair] = e_last * ST + _dot(Bc, (xs * tail).astype(BF16), TN)

    wide = pl.BlockSpec((L, DI), lambda b, c: (b * nc + c, 0))
    return pl.pallas_call(
        body, name=name, grid=(B, nc),
        in_specs=[pl.BlockSpec((L, XW), lambda b, c: (b * nc + c, 0)), wide, wide,
                  pl.BlockSpec((1, G * HEAD_ROWS, L), lambda b, c: (b, 0, c)),
                  pl.BlockSpec((1, DI), lambda b, c: (0, 0))],
        out_specs=[wide, pl.BlockSpec((1, 1, n_pairs, N, LANE), lambda b, c: (b, c, 0, 0, 0))],
        out_shape=[jax.ShapeDtypeStruct((B * seq, DI), BF16),
                   jax.ShapeDtypeStruct((B, nc, n_pairs, N, LANE), F32)],
        scratch_shapes=[pltpu.VMEM((n_pairs, N, LANE), F32)],
        compiler_params=_cp("parallel", "arbitrary"))(xbc, dt_x, cs_x, cs_row, d_full)


def scan_bwd(xbc, dt_x, cs_x, cs_row, d_full, states, dy, *, B, seq, DI, name):
    L, N, G = SSM_CHUNK, SSM_STATE, SSM_GROUPS
    nc = seq // L
    XW = xbc.shape[1]
    n_pairs = DI // LANE
    ppg = n_pairs // G
    HR = G * HEAD_ROWS
    inv_p = 1.0 / SSM_HEAD_DIM

    def body(x_ref, dtx_ref, csx_ref, csr_ref, d_ref, st_ref, dy_ref, dx_ref, ddtx_ref, dcsx_ref, dcsr_ref, dd_ref,
             dH):
        c = pl.program_id(1)

        @pl.when(c == 0)
        def _():
            dH[...] = jnp.zeros_like(dH)
            dd_ref[...] = jnp.zeros_like(dd_ref)

        causal = lax.broadcasted_iota(jnp.int32, (L, L), 0) >= lax.broadcasted_iota(jnp.int32, (L, L), 1)
        lo_half = lax.broadcasted_iota(jnp.int32, (L, LANE), 1) < SSM_HEAD_DIM
        last_row = lax.broadcasted_iota(jnp.int32, (L, LANE), 0) == L - 1
        head_row = lax.broadcasted_iota(jnp.int32, (HR, 1), 0)
        dcs_rows = jnp.zeros((HR, L), F32)

        for g in range(G):
            Bc = x_ref[:, DI + g * N:DI + (g + 1) * N]
            Cc = x_ref[:, DI + G * N + g * N:DI + G * N + (g + 1) * N]
            CB = _dot(Cc, Bc, NT)
            dCB = jnp.zeros((L, L), F32)
            dC = jnp.zeros((L, N), F32)
            dB = jnp.zeros((L, N), F32)
            for pp in range(ppg):
                pair = g * ppg + pp
                _, ra, cols, X, dt_p, La, Lb, ecs, tail, e_last = _pair_terms(
                    x_ref, dtx_ref, csx_ref, csr_ref, pair, ppg, lo_half, causal)
                xs = X * dt_p
                xsb = xs.astype(BF16)
                Ma, Mb = CB * La, CB * Lb
                dY = dy_ref[:, cols].astype(F32)
                dYb = dY.astype(BF16)
                dMa = _dot(jnp.where(lo_half, dY, 0.0).astype(BF16), xsb, NT)
                dMb = _dot(jnp.where(lo_half, 0.0, dY).astype(BF16), xsb, NT)
                dSa, dSb = dMa * Ma, dMb * Mb
                dCB = dCB + dMa * La + dMb * Lb
                dcs = jnp.where(lo_half, jnp.sum(dSa, axis=1, keepdims=True), jnp.sum(dSb, axis=1, keepdims=True)) * inv_p
                dcs_rows = dcs_rows - jnp.where(head_row == ra, jnp.sum(dSa, axis=0, keepdims=True), 0.0)
                dcs_rows = dcs_rows - jnp.where(head_row == ra + 1, jnp.sum(dSb, axis=0, keepdims=True), 0.0)
                dxs = jnp.where(lo_half, _dot(Ma.astype(BF16), dYb, TN), _dot(Mb.astype(BF16), dYb, TN))
                ST = st_ref[0, 0, pair]
                STb = ST.astype(BF16)
                dYe = (dY * ecs).astype(BF16)
                dC = dC + _dot(dYe, STb, NT)
                dSTp = _dot(Cc, dYe, TN)
                dcs = dcs + dY * (ecs * _dot(Cc, STb))
                dSTn = dH[pair]
                dSTnb = dSTn.astype(BF16)
                dSTp = dSTp + e_last * dSTn
                XBt = _dot(Bc, dSTnb)
                dxs = dxs + tail * XBt
                t2 = xs * XBt * tail
                at_end = e_last * jnp.sum(dSTn * ST, axis=0, keepdims=True) + jnp.sum(t2, axis=0, keepdims=True)
                dcs = dcs - t2 + jnp.where(last_row, at_end, 0.0)
                dB = dB + _dot((xs * tail).astype(BF16), dSTnb, NT)
                dx_ref[:, cols] = (dxs * dt_p + d_ref[:, cols] * dY).astype(BF16)
                ddtx_ref[:, cols] = (dxs * X).astype(BF16)
                dcsx_ref[:, cols] = dcs
                dd_ref[0, :, cols] += jnp.sum(dY * X, axis=0, keepdims=True)
                dH[pair] = dSTp
            dCBb = dCB.astype(BF16)
            dx_ref[:, DI + g * N:DI + (g + 1) * N] = (dB + _dot(dCBb, Cc, TN)).astype(BF16)
            dx_ref[:, DI + G * N + g * N:DI + G * N + (g + 1) * N] = (dC + _dot(dCBb, Bc)).astype(BF16)
        dcsr_ref[0] = dcs_rows

    rev = lambda b, c: (b * nc + (nc - 1 - c), 0)
    wide = pl.BlockSpec((L, DI), rev)
    hrow = pl.BlockSpec((1, HR, L), lambda b, c: (b, 0, nc - 1 - c))
    return pl.pallas_call(
        body, name=name, grid=(B, nc),
        in_specs=[pl.BlockSpec((L, XW), rev), wide, wide, hrow,
                  pl.BlockSpec((1, DI), lambda b, c: (0, 0)),
                  pl.BlockSpec((1, 1, n_pairs, N, LANE), lambda b, c: (b, nc - 1 - c, 0, 0, 0)), wide],
        out_specs=[pl.BlockSpec((L, XW), rev), wide, wide, hrow, pl.BlockSpec((1, 1, DI), lambda b, c: (b, 0, 0))],
        out_shape=[jax.ShapeDtypeStruct((B * seq, XW), BF16), jax.ShapeDtypeStruct((B * seq, DI), BF16),
                   jax.ShapeDtypeStruct((B * seq, DI), F32), jax.ShapeDtypeStruct((B, HR, seq), F32),
                   jax.ShapeDtypeStruct((B, 1, DI), F32)],
        scratch_shapes=[pltpu.VMEM((n_pairs, N, LANE), F32)],
        compiler_params=_cp("parallel", "arbitrary"))(xbc, dt_x, cs_x, cs_row, d_full, states, dy)


def gnorm_fwd(y, proj, w, *, DI, name):
    T = y.shape[0]
    tt = _pick(T, 256, 16)
    gw = DI // SSM_GROUPS

    def body(y_ref, z_ref, w_ref, o_ref):
        for g in range(SSM_GROUPS):
            sl = slice(g * gw, (g + 1) * gw)
            y2 = y_ref[:, sl].astype(F32) * _silu(z_ref[:, sl].astype(F32))
            r = lax.rsqrt(jnp.mean(y2 * y2, axis=-1, keepdims=True) + EPS)
            o_ref[:, sl] = (y2 * r * w_ref[:, sl]).astype(BF16)

    row = pl.BlockSpec((tt, DI), lambda i: (i, 0))
    return pl.pallas_call(
        body, name=name, grid=(T // tt,),
        in_specs=[row, row, pl.BlockSpec((1, DI), lambda i: (0, 0))], out_specs=row,
        out_shape=jax.ShapeDtypeStruct((T, DI), BF16), compiler_params=_cp("parallel"))(y, proj, w)


def gnorm_bwd(dyn, y, proj, w, *, DI, name):
    T = y.shape[0]
    tt = _pick(T, 256, 16)
    gw = DI // SSM_GROUPS

    def body(dyn_ref, y_ref, z_ref, w_ref, dy_ref, dz_ref, dw_ref):
        @pl.when(pl.program_id(0) == 0)
        def _():
            dw_ref[...] = jnp.zeros_like(dw_ref)

        for g in range(SSM_GROUPS):
            sl = slice(g * gw, (g + 1) * gw)
            yv = y_ref[:, sl].astype(F32)
            z = z_ref[:, sl].astype(F32)
            sz = _silu(z)
            y2 = yv * sz
            r = lax.rsqrt(jnp.mean(y2 * y2, axis=-1, keepdims=True) + EPS)
            xhat = y2 * r
            d = dyn_ref[:, sl].astype(F32)
            dw_ref[:, sl] += jnp.sum(d * xhat, axis=0, keepdims=True)
            dxh = d * w_ref[:, sl]
            dy2 = r * (dxh - xhat * jnp.mean(dxh * xhat, axis=-1, keepdims=True))
            dy_ref[:, sl] = (dy2 * sz).astype(BF16)
            dz_ref[:, sl] = (dy2 * yv * _silu_grad(z)).astype(BF16)

    row = pl.BlockSpec((tt, DI), lambda i: (i, 0))
    vec = pl.BlockSpec((1, DI), lambda i: (0, 0))
    shp = jax.ShapeDtypeStruct((T, DI), BF16)
    return pl.pallas_call(
        body, name=name, grid=(T // tt,), in_specs=[row, row, row, vec], out_specs=[row, row, vec],
        out_shape=[shp, shp, jax.ShapeDtypeStruct((1, DI), F32)],
        compiler_params=_cp("arbitrary"))(dyn, y, proj, w)


N_CHIP = 4


def _comm_out_shapes(srcs, modes):
    return [jax.ShapeDtypeStruct(((N_DEV,) if mode in ("gather", "gather_direct") else ()) + s.shape, s.dtype)
            for s, mode in zip(srcs, modes)]


def _comm_scratch(n):
    return [pltpu.SemaphoreType.DMA((n, N_DEV - 1)), pltpu.SemaphoreType.DMA((n, N_DEV - 1)),
            pltpu.SemaphoreType.DMA((n,))]


def _comm_phases(modes, src_refs, out_refs, send_sems, recv_sems, local_sems):
    x, y, c = lax.axis_index("x"), lax.axis_index("y"), lax.axis_index("c")
    me, sibling = (x, y, c), (x, y, 1 - c)
    chips = [(1 - x, y), (x, 1 - y), (1 - x, 1 - y)]
    relays = [a for a, mode in enumerate(modes) if mode == "gather"]

    def slot(p):
        return 4 * p[0] + 2 * p[1] + p[2]

    def remote(a, k, src, dst, to):
        return pltpu.make_async_remote_copy(src_ref=src, dst_ref=dst, send_sem=send_sems.at[a, k],
                                            recv_sem=recv_sems.at[a, k], device_id=to,
                                            device_id_type=pl.DeviceIdType.MESH)

    def first_copies():
        local, two_way, send_only = [], [], []
        for a, mode in enumerate(modes):
            src, out = src_refs[a], out_refs[a]
            if mode == "sibling":
                two_way.append(remote(a, 0, src, out, sibling))
            elif mode == "chips":
                mine = 2 * x + y
                local.append(pltpu.make_async_copy(src.at[mine], out.at[mine], local_sems.at[a]))
                for j, chip in enumerate(chips):
                    two_way.append(remote(a, 1 + j, src.at[2 * chip[0] + chip[1]], out.at[mine], (*chip, c)))
            elif mode == "gather_direct":
                local.append(pltpu.make_async_copy(src, out.at[slot(me)], local_sems.at[a]))
                for k in range(1, N_DEV):
                    peer = (1 - x if k & 4 else x, 1 - y if k & 2 else y, 1 - c if k & 1 else c)
                    two_way.append(remote(a, k - 1, src, out.at[slot(me)], peer))
            else:
                assert mode == "gather"
                local.append(pltpu.make_async_copy(src, out.at[slot(me)], local_sems.at[a]))
                send_only.append(remote(a, 0, src, out.at[slot(me)], sibling))
                for j, chip in enumerate(chips):
                    send_only.append(remote(a, 1 + j, src, out.at[slot(me)], (*chip, c)))
        return local, two_way, send_only

    def forwards():
        out = []
        for a in relays:
            for j, chip in enumerate(chips):
                landed = out_refs[a].at[slot((*chip, c))]
                out.append((remote(a, 1 + j, landed, landed, me), remote(a, 4 + j, landed, landed, sibling)))
        return out

    def start():
        local, two_way, send_only = first_copies()
        for cp in local + two_way + send_only:
            cp.start()

    def relay():
        for arrival, fwd in forwards():
            arrival.wait_recv()
            fwd.start()

    def finish():
        local, two_way, send_only = first_copies()
        for a in relays:
            blk = out_refs[a].at[slot(sibling)]
            remote(a, 0, blk, blk, me).wait_recv()
            for j, chip in enumerate(chips):
                blk = out_refs[a].at[slot((*chip, 1 - c))]
                remote(a, 4 + j, blk, blk, me).wait_recv()
        for cp in send_only + [fwd for _, fwd in forwards()]:
            cp.wait_send()
        for cp in two_way + local:
            cp.wait()

    return start, relay, finish, bool(relays)


def _hosted(body, n_in, n_out, comm, step, n_steps):
    if comm is None:
        return body, [], [], [], [], []
    srcs, modes = comm
    nc = len(srcs)

    def wrapped(*refs):
        ins, csrc = refs[:n_in], refs[n_in:n_in + nc]
        outs = refs[n_in + nc:n_in + nc + n_out]
        cout = refs[n_in + nc + n_out:n_in + 2 * nc + n_out]
        scratch = refs[n_in + 2 * nc + n_out:len(refs) - 3]
        start, relay, finish, has_relay = _comm_phases(modes, csrc, cout, *refs[len(refs) - 3:])
        s = step()
        pl.when(s == 0)(start)
        body(*ins, *outs, *scratch)
        if has_relay:
            pl.when(s == (2 * n_steps) // 3)(relay)
        pl.when(s == n_steps - 1)(finish)

    any_spec = pl.BlockSpec(memory_space=pl.ANY)
    return wrapped, [any_spec] * nc, list(srcs), [any_spec] * nc, _comm_out_shapes(srcs, modes), _comm_scratch(nc)


def exchange(srcs, modes, *, name):
    n = len(srcs)

    def body(*refs):
        start, relay, finish, has_relay = _comm_phases(modes, refs[:n], refs[n:2 * n], *refs[2 * n:])
        start()
        if has_relay:
            relay()
        finish()

    any_spec = pl.BlockSpec(memory_space=pl.ANY)
    return pl.pallas_call(
        body, name=name, in_specs=[any_spec] * n, out_specs=[any_spec] * n, out_shape=_comm_out_shapes(srcs, modes),
        scratch_shapes=_comm_scratch(n), compiler_params=pltpu.CompilerParams(has_side_effects=True))(*srcs)


def pair_sum(a, b, *, name):
    n, R, C = a.shape
    tr = _pick(n * R, 512, 16)

    def body(a_ref, b_ref, o_ref):
        o_ref[...] = (a_ref[...].astype(F32) + b_ref[...].astype(F32)).astype(BF16)

    blk = pl.BlockSpec((tr, C), lambda i: (i, 0))
    out = pl.pallas_call(
        body, name=name, grid=(n * R // tr,), in_specs=[blk, blk], out_specs=blk,
        out_shape=jax.ShapeDtypeStruct((n * R, C), BF16),
        compiler_params=_cp("parallel"))(a.reshape(n * R, C), b.reshape(n * R, C))
    return out.reshape(n, R, C)


def sum_slots(recv, *, name):
    _, R, C = recv.shape
    tr = _pick(R, 512, 8)

    def body(r_ref, o_ref):
        acc = r_ref[0].astype(F32)
        for p in range(1, N_DEV):
            acc = acc + r_ref[p].astype(F32)
        o_ref[...] = acc

    return pl.pallas_call(
        body, name=name, grid=(R // tr,),
        in_specs=[pl.BlockSpec((N_DEV, tr, C), lambda i: (0, i, 0))],
        out_specs=pl.BlockSpec((tr, C), lambda i: (i, 0)),
        out_shape=jax.ShapeDtypeStruct((R, C), F32), compiler_params=_cp("parallel"))(recv)


def adamw(gsrc, w, m, v, *, name):
    slots, R, C = gsrc.shape
    tr = _pick(R, 256, 16 if gsrc.dtype == BF16 else 8)
    c1 = 1.0 / (1.0 - ADAM_B1 ** ADAM_STEP)
    c2 = 1.0 / (1.0 - ADAM_B2 ** ADAM_STEP)

    def body(g_ref, w_ref, m_ref, v_ref, go_ref, d_ref, mo_ref, vo_ref):
        g = g_ref[0].astype(F32)
        for p in range(1, slots):
            g = g + g_ref[p].astype(F32)
        m2 = ADAM_B1 * m_ref[...] + (1.0 - ADAM_B1) * g
        v2 = ADAM_B2 * v_ref[...] + (1.0 - ADAM_B2) * (g * g)
        go_ref[...] = g
        mo_ref[...] = m2
        vo_ref[...] = v2
        d_ref[...] = -ADAM_LR * ((m2 * c1) / (jnp.sqrt(v2 * c2) + ADAM_EPS) + ADAM_WD * w_ref[...])

    blk = pl.BlockSpec((tr, C), lambda i: (i, 0))
    shp = jax.ShapeDtypeStruct((R, C), F32)
    return pl.pallas_call(
        body, name=name, grid=(R // tr,),
        in_specs=[pl.BlockSpec((slots, tr, C), lambda i: (0, i, 0)), blk, blk, blk],
        out_specs=[blk] * 4, out_shape=[shp] * 4, compiler_params=_cp("parallel"))(gsrc, w, m, v)


def _pad_cols(a, n):
    return jnp.pad(a, ((0, 0), (0, n - a.shape[1])))


def _to_rows(a, B, seq, H):
    G = SSM_GROUPS
    R = H // G
    t = a[:, :H].reshape(B, seq, G, R).transpose(0, 2, 3, 1)
    t = jnp.pad(t, ((0, 0), (0, 0), (0, HEAD_ROWS - R), (0, 0)))
    return t.reshape(B, G * HEAD_ROWS, seq)


def _from_rows(a, B, seq, H):
    G = SSM_GROUPS
    R = H // G
    t = a.reshape(B, G, HEAD_ROWS, seq)[:, :, :R].transpose(0, 3, 1, 2).reshape(B * seq, H)
    return _pad_cols(t, LANE)


def _chip_sums(grads, name):
    c_idx = lax.axis_index("c")
    keep, give = [], []
    for g in grads:
        by_chip = g.reshape((N_CHIP, 2) + g.shape[1:])
        keep.append(lax.dynamic_index_in_dim(by_chip, c_idx, axis=1, keepdims=False))
        give.append(lax.dynamic_index_in_dim(by_chip, 1 - c_idx, axis=1, keepdims=False))
    swapped = exchange(give, ["sibling"] * len(give), name="swap_" + name)
    return [pair_sum(k, s, name=f"chip_sum_{name}_{i}") for i, (k, s) in enumerate(zip(keep, swapped))]


def local_step(x, target, loc, od_w_in_t, *, B, seq):
    T, D = x.shape
    CW = D
    heads = CW // SB_HEAD_DIM
    DI = 2 * D
    H = DI // SSM_HEAD_DIM
    XW = DI + 2 * SSM_GROUPS * SSM_STATE
    in_odd = DI + XW + H
    w1_rows = in_odd // N_DEV
    q_off, k_off, v_off, gc_off, ga_off = 3 * CW, 4 * CW, 5 * CW, 2 * CW, 6 * CW
    dt_off = DI + XW

    small_packed, small_spans = _pack_rows([loc[n] for n in SMALL_SHARDED], LANE, 8)
    n0, (g_ev_in, small_all) = rmsnorm_fwd(x, loc["ev_norm_w"], name="l0_norm",
                                           comm=([loc["ev_w_in"].astype(BF16), small_packed],
                                                 ["gather", "gather_direct"]))
    p = {n: loc[n] for n in SMALL}
    for n, a in zip(SMALL_SHARDED, _unpack_rows(small_all, small_spans)):
        p[n] = _col_unshards(a)
    p["ev_w_in"] = _col_unshards(g_ev_in)
    proj0, (g_od_in_t, g_od_out, g_ev_out) = mm_nn(
        n0, p["ev_w_in"], out_dtype=BF16, name="l0_in_proj",
        comm=([od_w_in_t.astype(BF16), loc["od_w_out"].astype(BF16), loc["ev_w_out"].astype(BF16)], ["gather"] * 3))
    p["ev_w_out"] = g_ev_out.reshape(-1, D)
    w1t = g_od_in_t[:, :w1_rows].reshape(in_odd, D)
    w1t = jnp.pad(w1t, ((0, -(-(in_odd + LANE) // 256) * 256 - in_odd), (0, 0)))
    od_w_out = g_od_out.reshape(-1, D)
    (u2,) = dwconv_fwd(proj0, (0, CW), p["ev_dw_w"], p["ev_dw_b"], C=CW, seq=seq, glu=True, silu_out=False,
                       name="l0_conv")
    o, ctot = sba_fwd(proj0, B=B, seq=seq, heads=heads, q_off=q_off, k_off=k_off, v_off=v_off, name="l0_attn")
    ycat = mix0_post_fwd(u2, proj0, o, p["ev_ln_w"], p["ev_ln_b"], CW=CW, gc_off=gc_off, ga_off=ga_off,
                         name="l0_post")
    h1 = mm_nn(ycat, p["ev_w_out"], add=x, out_dtype=F32, name="l0_out_proj")

    n1 = rmsnorm_fwd(h1, p["od_norm_w"], name="l1_norm")
    proj1 = mm_nt_terms([(n1, 0, D, 0)], w1t, out_dtype=BF16, name="l1_in_proj")
    u_pre, xbc = dwconv_fwd(proj1, (DI,), p["od_conv_w"], p["od_conv_b"], C=XW, seq=seq, glu=False, silu_out=True,
                            name="l1_conv")
    bias_p, alog_p = _pad_cols(p["od_dt_bias"], LANE), _pad_cols(p["od_a_log"], LANE)
    expand = _head_expand(H, DI)
    dt, cs, dt_x, cs_x = dt_fwd(proj1, bias_p, alog_p, expand, dt_off=dt_off, name="l1_dt")
    cs_row = _to_rows(cs, B, seq, H)
    d_full = jnp.repeat(p["od_d"], SSM_HEAD_DIM, axis=1)
    y_ssd, states = scan_fwd(xbc, dt_x, cs_x, cs_row, d_full, B=B, seq=seq, DI=DI, name="l1_ssd")
    yn = gnorm_fwd(y_ssd, proj1, p["od_gnorm_w"], DI=DI, name="l1_gnorm")
    h2 = mm_nn(yn, od_w_out, add=h1, out_dtype=F32, name="l1_out_proj")

    loss, dh2, dh2b, g_final = final_loss(h2, p["final_norm_w"], target, name="loss_head")

    g_od_w_out = mm_tn(yn, dh2b, out_dtype=BF16, name="l1_dw_out")
    dyn = mm_nt_terms([(dh2b, 0, D, 0)], od_w_out, out_dtype=BF16, name="l1_d_out_proj")
    dy_ssd, dz, g_gnorm = gnorm_bwd(dyn, y_ssd, proj1, p["od_gnorm_w"], DI=DI, name="l1_gnorm_bwd")
    dxbc_c, ddt_x, dcs_x, dcs_row, dd_part = scan_bwd(xbc, dt_x, cs_x, cs_row, d_full, states, dy_ssd, B=B, seq=seq,
                                                      DI=DI, name="l1_ssd_bwd")
    g_d = dd_part.sum(axis=(0, 1)).reshape(H, SSM_HEAD_DIM).sum(axis=1)[None, :]
    draw, g_bias, g_alog = dt_bwd(ddt_x, dcs_x, _from_rows(dcs_row, B, seq, H), proj1, dt, bias_p, alog_p, expand.T,
                                  dt_off=dt_off, n_heads=H, name="l1_dt_bwd")
    dxbc, g_conv_w, g_conv_b = dwconv_bwd(dxbc_c, u_pre, proj1, (DI,), p["od_conv_w"], C=XW, seq=seq, glu=False,
                                          silu_out=True, name="l1_conv_bwd")
    tw = 512 if DI % 512 == 0 else LANE
    terms = [(dz, j, tw, j * tw) for j in range(DI // tw)]
    terms += [(dxbc, j, tw, DI + j * tw) for j in range(XW // tw)]
    terms += [(draw, 0, LANE, dt_off)]
    dn1 = mm_nn_terms(terms, w1t, out_dtype=BF16, name="l1_d_in_proj")
    g_od_w_in_t = jnp.concatenate([mm_tn(dz, n1, out_dtype=BF16, name="l1_dw_in_z"),
                                   mm_tn(dxbc, n1, out_dtype=BF16, name="l1_dw_in_xbc"),
                                   mm_tn(draw, n1, out_dtype=BF16, name="l1_dw_in_dt")], axis=0)[:in_odd]
    dh1, dh1b, g_od_norm = rmsnorm_bwd(h1, p["od_norm_w"], dn1, dh2, name="l1_norm_bwd")
    w1_pad = (-w1_rows) % 16
    l1_chip = _chip_sums([jnp.pad(g_od_w_in_t.reshape(N_DEV, w1_rows, D), ((0, 0), (0, w1_pad), (0, 0))),
                          g_od_w_out.reshape(N_DEV, -1, D)], "l1")

    g_ev_w_out = mm_tn(ycat, dh1b, out_dtype=BF16, name="l0_dw_out")
    dycat = mm_nt_terms([(dh1b, 0, D, 0)], p["ev_w_out"], out_dtype=BF16, name="l0_d_out_proj")
    du2, dgc, dga, do, g_ln_w, g_ln_b = mix0_post_bwd(dycat, u2, proj0, o, p["ev_ln_w"], p["ev_ln_b"], CW=CW,
                                                      gc_off=gc_off, ga_off=ga_off, name="l0_post_bwd")
    dq, dk, dv, (r_od_in_t, r_od_out) = sba_bwd(proj0, ctot, do, B=B, seq=seq, heads=heads, q_off=q_off, k_off=k_off,
                                                v_off=v_off, name="l0_attn_bwd", comm=(l1_chip, ["chips", "chips"]))
    dga_a, dga_b, g_dw_w, g_dw_b = dwconv_bwd(du2, None, proj0, (0, CW), p["ev_dw_w"], C=CW, seq=seq, glu=True,
                                              silu_out=False, name="l0_conv_bwd")
    pieces = [dga_a, dga_b, dgc, dq, dk, dv, dga]
    g_ev_w_in = jnp.concatenate([mm_tn(n0, pc, out_dtype=BF16, name=f"l0_dw_in_{j}") for j, pc in enumerate(pieces)],
                                axis=1)
    l0_chip = _chip_sums([_col_shards(g_ev_w_in), g_ev_w_out.reshape(N_DEV, -1, D)], "l0")
    dn0, (r_ev_in, r_ev_out) = mm_nt_terms([(pc, 0, CW, j * CW) for j, pc in enumerate(pieces)], p["ev_w_in"],
                                           out_dtype=BF16, name="l0_d_in_proj", comm=(l0_chip, ["chips", "chips"]))
    dx, _, g_ev_norm = rmsnorm_bwd(x, p["ev_norm_w"], dn0, dh1, name="l0_norm_bwd")

    small = dict(ev_norm_w=g_ev_norm, ev_dw_w=g_dw_w, ev_dw_b=g_dw_b, ev_ln_w=g_ln_w, ev_ln_b=g_ln_b,
                 od_norm_w=g_od_norm, od_conv_w=g_conv_w, od_conv_b=g_conv_b, od_dt_bias=g_bias[:, :H],
                 od_a_log=g_alog[:, :H], od_d=g_d, od_gnorm_w=g_gnorm, final_norm_w=g_final)
    received = dict(ev_w_in=r_ev_in, ev_w_out=r_ev_out, od_w_in=r_od_in_t, od_w_out=r_od_out)
    return loss, dx, small, received


BIG = ("ev_w_in", "ev_w_out", "od_w_in", "od_w_out")
SMALL = ("ev_norm_w", "ev_dw_w", "ev_dw_b", "ev_ln_w", "ev_ln_b", "od_norm_w", "od_conv_w", "od_conv_b",
         "od_dt_bias", "od_a_log", "od_d", "od_gnorm_w", "final_norm_w")
SMALL_SHARDED = ("ev_dw_w", "od_norm_w", "od_conv_w", "od_conv_b", "od_gnorm_w")
ORDER = ("ev_norm_w", "ev_w_in", "ev_dw_w", "ev_dw_b", "ev_ln_w", "ev_ln_b", "ev_w_out", "od_norm_w", "od_w_in",
         "od_conv_w", "od_conv_b", "od_dt_bias", "od_a_log", "od_d", "od_gnorm_w", "od_w_out", "final_norm_w")


def _pack_rows(arrs, width, row_align):
    parts, spans, r0 = [], [], 0
    for a in arrs:
        flat = a.reshape(-1)
        rows = -(-flat.shape[0] // (width * row_align)) * row_align
        parts.append(jnp.pad(flat, (0, rows * width - flat.shape[0])).reshape(rows, width))
        spans.append((r0, a.size, a.shape))
        r0 += rows
    return jnp.concatenate(parts, axis=0), spans


def _unpack_rows(packed, spans):
    lead = packed.shape[:-2]
    width = packed.shape[-1]
    out = []
    for r0, size, shape in spans:
        rows = -(-size // width)
        blk = packed[..., r0:r0 + rows, :].reshape(lead + (rows * width,))[..., :size]
        out.append(blk.reshape(lead + tuple(shape)))
    return out


def _col_shards(a):
    R, C8 = a.shape
    return a.reshape(R, N_DEV, C8 // N_DEV).transpose(1, 0, 2)


def _col_unshards(a):
    n, R, C = a.shape
    return a.transpose(1, 0, 2).reshape(R, n * C)


def kernel(x, ev_norm_w, ev_w_in, ev_dw_w, ev_dw_b, ev_ln_w, ev_ln_b, ev_w_out, od_norm_w, od_w_in, od_conv_w, od_conv_b, od_dt_bias, od_a_log, od_d, od_gnorm_w, od_w_out, final_norm_w, loss_target, m_ev_norm_w, m_ev_w_in, m_ev_dw_w, m_ev_dw_b, m_ev_ln_w, m_ev_ln_b, m_ev_w_out, m_od_norm_w, m_od_w_in, m_od_conv_w, m_od_conv_b, m_od_dt_bias, m_od_a_log, m_od_d, m_od_gnorm_w, m_od_w_out, m_final_norm_w, v_ev_norm_w, v_ev_w_in, v_ev_dw_w, v_ev_dw_b, v_ev_ln_w, v_ev_ln_b, v_ev_w_out, v_od_norm_w, v_od_w_in, v_od_conv_w, v_od_conv_b, v_od_dt_bias, v_od_a_log, v_od_d, v_od_gnorm_w, v_od_w_out, v_final_norm_w):
    loc = dict(ev_norm_w=ev_norm_w, ev_w_in=ev_w_in, ev_dw_w=ev_dw_w, ev_dw_b=ev_dw_b, ev_ln_w=ev_ln_w,
               ev_ln_b=ev_ln_b, ev_w_out=ev_w_out, od_norm_w=od_norm_w, od_w_in=od_w_in, od_conv_w=od_conv_w,
               od_conv_b=od_conv_b, od_dt_bias=od_dt_bias, od_a_log=od_a_log, od_d=od_d, od_gnorm_w=od_gnorm_w,
               od_w_out=od_w_out, final_norm_w=final_norm_w)
    mom = dict(ev_norm_w=m_ev_norm_w, ev_w_in=m_ev_w_in, ev_dw_w=m_ev_dw_w, ev_dw_b=m_ev_dw_b, ev_ln_w=m_ev_ln_w,
               ev_ln_b=m_ev_ln_b, ev_w_out=m_ev_w_out, od_norm_w=m_od_norm_w, od_w_in=m_od_w_in,
               od_conv_w=m_od_conv_w, od_conv_b=m_od_conv_b, od_dt_bias=m_od_dt_bias, od_a_log=m_od_a_log,
               od_d=m_od_d, od_gnorm_w=m_od_gnorm_w, od_w_out=m_od_w_out, final_norm_w=m_final_norm_w)
    var = dict(ev_norm_w=v_ev_norm_w, ev_w_in=v_ev_w_in, ev_dw_w=v_ev_dw_w, ev_dw_b=v_ev_dw_b, ev_ln_w=v_ev_ln_w,
               ev_ln_b=v_ev_ln_b, ev_w_out=v_ev_w_out, od_norm_w=v_od_norm_w, od_w_in=v_od_w_in,
               od_conv_w=v_od_conv_w, od_conv_b=v_od_conv_b, od_dt_bias=v_od_dt_bias, od_a_log=v_od_a_log,
               od_d=v_od_d, od_gnorm_w=v_od_gnorm_w, od_w_out=v_od_w_out, final_norm_w=v_final_norm_w)
    shapes = {n: loc[n].shape for n in ORDER}
    loc = {n: (a.reshape(1, -1) if a.ndim == 1 else a.reshape(a.shape[-2:]) if a.ndim == 3 else a)
           for n, a in loc.items()}
    mom = {n: a.reshape(loc[n].shape) for n, a in mom.items()}
    var = {n: a.reshape(loc[n].shape) for n, a in var.items()}

    B, seq, D = x.shape
    me = 4 * lax.axis_index("x") + 2 * lax.axis_index("y") + lax.axis_index("c")

    w1_rows = loc["od_w_in"].shape[1]
    w1_pad = (-w1_rows) % 16

    def to_t(a):
        return jnp.pad(a.T, ((0, w1_pad), (0, 0)))

    loss, dx, grads, received = local_step(x.reshape(B * seq, D), loss_target.reshape(B * seq, D), loc,
                                           to_t(loc["od_w_in"]), B=B, seq=seq)

    gsmall_packed, gsmall_spans = _pack_rows([grads[n] for n in SMALL], LANE, 8)
    (gsmall_recv,) = exchange([gsmall_packed], ["gather_direct"], name="gather_small_grads")

    big_out = [{} for _ in range(4)]
    for n in ("ev_w_in", "ev_w_out", "od_w_out"):
        for kind, a in enumerate(adamw(received[n], loc[n], mom[n], var[n], name="adamw_" + n)):
            big_out[kind][n] = a
    for kind, a in enumerate(adamw(received["od_w_in"], to_t(loc["od_w_in"]), to_t(mom["od_w_in"]),
                                   to_t(var["od_w_in"]), name="adamw_od_w_in")):
        big_out[kind]["od_w_in"] = a[:w1_rows].T

    gsmall = dict(zip(SMALL, _unpack_rows(sum_slots(gsmall_recv, name="sum_small_grads"), gsmall_spans)))
    for n in SMALL_SHARDED:
        width = loc[n].shape[1]
        gsmall[n] = lax.dynamic_slice_in_dim(gsmall[n], me * width, width, axis=1)
    gs, sspans = _pack_rows([gsmall[n] for n in SMALL], LANE, 8)
    ws, _ = _pack_rows([loc[n] for n in SMALL], LANE, 8)
    ms, _ = _pack_rows([mom[n] for n in SMALL], LANE, 8)
    vs, _ = _pack_rows([var[n] for n in SMALL], LANE, 8)
    small_out = [dict(zip(SMALL, _unpack_rows(a, sspans))) for a in adamw(gs[None], ws, ms, vs, name="adamw_small")]

    outs = [lax.psum(loss[0, 0], ("x", "y", "c")), dx.reshape(B, seq, D)]
    for kind in range(4):
        for n in ORDER:
            src = big_out[kind] if n in BIG else small_out[kind]
            outs.append(src[n].reshape(shapes[n]))
    return tuple(outs)
```

```python
import functools

import jax
import jax.numpy as jnp
from jax import lax
from jax.experimental import pallas as pl
from jax.experimental.pallas import tpu as pltpu

F32 = jnp.float32
BF16 = jnp.bfloat16

EPS = 1e-6
N_DEV = 8
LANE = 128
VMEM_LIMIT_BYTES = 48 * 1024 * 1024

SB_HEAD_DIM = 128
CONF_KERNEL = 31
SSM_CONV = 4
SSM_HEAD_DIM = 64
SSM_GROUPS = 4
SSM_STATE = 128
SSM_CHUNK = 128
HALO = 32
HEAD_ROWS = 8
NEG_BIG = -1e30

ADAM_LR = 0.001
ADAM_B1 = 0.9
ADAM_B2 = 0.999
ADAM_EPS = 1e-08
ADAM_WD = 0.01
ADAM_STEP = 10

NT = (((1,), (1,)), ((), ()))
TN = (((0,), (0,)), ((), ()))


def _cp(*sem):
    return pltpu.CompilerParams(dimension_semantics=sem, vmem_limit_bytes=VMEM_LIMIT_BYTES)


def _pick(n, cap, align):
    if n <= cap:
        return n
    t = (cap // align) * align
    while t >= align:
        if n % t == 0:
            return t
        t -= align
    raise ValueError(f"no tile for {n} (cap {cap}, align {align})")


def _sigmoid(x):
    return 0.5 * jnp.tanh(0.5 * x) + 0.5


def _silu(x):
    return x * _sigmoid(x)


def _silu_grad(x):
    s = _sigmoid(x)
    return s * (1.0 + x * (1.0 - s))


def _dot(a, b, dims=None):
    if dims is None:
        return jnp.dot(a, b, preferred_element_type=F32)
    return lax.dot_general(a, b, dims, preferred_element_type=F32)


def _tri_dot3(tri, x):
    hi = x.astype(BF16)
    r1 = x - hi.astype(F32)
    mid = r1.astype(BF16)
    lo = (r1 - mid.astype(F32)).astype(BF16)
    return _dot(tri, hi) + _dot(tri, mid) + _dot(tri, lo)


def _grid_step(n_inner):
    return lambda: pl.program_id(0) * n_inner + pl.program_id(1)


def mm_nn(a, b, *, add=None, out_dtype, name, comm=None):
    M, K = a.shape
    N = b.shape[1]
    tm = _pick(M, 2048 if K <= 1024 and add is None else 1024, 16)
    tn = _pick(N, 1024, LANE)

    def body(*refs):
        if add is None:
            a_ref, b_ref, o_ref = refs
        else:
            a_ref, b_ref, add_ref, o_ref = refs
        acc = _dot(a_ref[...], b_ref[...])
        if add is not None:
            acc = acc + add_ref[...]
        o_ref[...] = acc.astype(out_dtype)

    in_specs = [pl.BlockSpec((tm, K), lambda i, j: (i, 0)), pl.BlockSpec((K, tn), lambda i, j: (0, j))]
    args = [a, b]
    if add is not None:
        in_specs.append(pl.BlockSpec((tm, tn), lambda i, j: (i, j)))
        args.append(add)
    grid = (M // tm, N // tn)
    body, c_in, c_args, c_out, c_shape, c_scratch = _hosted(body, len(args), 1, comm, _grid_step(grid[1]),
                                                            grid[0] * grid[1])
    out = pl.pallas_call(
        body, name=name, grid=grid, in_specs=in_specs + c_in,
        out_specs=[pl.BlockSpec((tm, tn), lambda i, j: (i, j))] + c_out,
        out_shape=[jax.ShapeDtypeStruct((M, N), out_dtype)] + c_shape, scratch_shapes=c_scratch,
        compiler_params=_cp(*(("arbitrary",) * 2 if comm else ("parallel",) * 2)))(*args, *c_args)
    return (out[0], out[1:]) if comm else out[0]


def mm_nt_terms(terms, b, *, out_dtype, name, comm=None):
    M = terms[0][0].shape[0]
    N = b.shape[0]
    n_terms = len(terms)
    cap = 1024 if n_terms == 1 else 512
    tm = _pick(M, 2 * cap if n_terms == 1 else cap, 16)
    tn = _pick(N, cap, LANE)

    def body(*refs):
        o_ref = refs[-1]
        acc = None
        for t in range(n_terms):
            part = _dot(refs[2 * t][...], refs[2 * t + 1][...], NT)
            acc = part if acc is None else acc + part
        o_ref[...] = acc.astype(out_dtype)

    in_specs, args = [], []
    for arr, cb, w, off in terms:
        assert off % w == 0
        in_specs.append(pl.BlockSpec((tm, w), lambda i, j, cb=cb: (i, cb)))
        in_specs.append(pl.BlockSpec((tn, w), lambda i, j, ob=off // w: (j, ob)))
        args += [arr, b]
    grid = (M // tm, N // tn)
    body, c_in, c_args, c_out, c_shape, c_scratch = _hosted(body, len(args), 1, comm, _grid_step(grid[1]),
                                                            grid[0] * grid[1])
    out = pl.pallas_call(
        body, name=name, grid=grid, in_specs=in_specs + c_in,
        out_specs=[pl.BlockSpec((tm, tn), lambda i, j: (i, j))] + c_out,
        out_shape=[jax.ShapeDtypeStruct((M, N), out_dtype)] + c_shape, scratch_shapes=c_scratch,
        compiler_params=_cp(*(("arbitrary",) * 2 if comm else ("parallel",) * 2)))(*args, *c_args)
    return (out[0], out[1:]) if comm else out[0]


def mm_nn_terms(terms, b, *, out_dtype, name):
    M = terms[0][0].shape[0]
    N = b.shape[1]
    tm = _pick(M, 512, 16)
    tn = _pick(N, 512, LANE)
    n_terms = len(terms)

    def body(*refs):
        o_ref = refs[-1]
        acc = None
        for t in range(n_terms):
            part = _dot(refs[2 * t][...], refs[2 * t + 1][...])
            acc = part if acc is None else acc + part
        o_ref[...] = acc.astype(out_dtype)

    in_specs, args = [], []
    for arr, cb, w, off in terms:
        assert off % w == 0
        in_specs.append(pl.BlockSpec((tm, w), lambda i, j, cb=cb: (i, cb)))
        in_specs.append(pl.BlockSpec((w, tn), lambda i, j, ob=off // w: (ob, j)))
        args += [arr, b]
    return pl.pallas_call(
        body, name=name, grid=(M // tm, N // tn), in_specs=in_specs,
        out_specs=pl.BlockSpec((tm, tn), lambda i, j: (i, j)),
        out_shape=jax.ShapeDtypeStruct((M, N), out_dtype),
        compiler_params=_cp("parallel", "parallel"))(*args)


def mm_tn(a, b, *, out_dtype, name):
    T, M = a.shape
    N = b.shape[1]
    tm = _pick(M, 1024, LANE)
    tn = _pick(N, 1024, LANE)
    tk = _pick(T, 2048, 16)
    nk = T // tk

    def body(a_ref, b_ref, o_ref, acc_ref):
        k = pl.program_id(2)

        @pl.when(k == 0)
        def _():
            acc_ref[...] = jnp.zeros_like(acc_ref)

        acc_ref[...] += _dot(a_ref[...], b_ref[...], TN)

        @pl.when(k == nk - 1)
        def _():
            o_ref[...] = acc_ref[...].astype(out_dtype)

    return pl.pallas_call(
        body, name=name, grid=(M // tm, N // tn, nk),
        in_specs=[pl.BlockSpec((tk, tm), lambda i, j, k: (k, i)), pl.BlockSpec((tk, tn), lambda i, j, k: (k, j))],
        out_specs=pl.BlockSpec((tm, tn), lambda i, j, k: (i, j)),
        out_shape=jax.ShapeDtypeStruct((M, N), out_dtype),
        scratch_shapes=[pltpu.VMEM((tm, tn), F32)],
        compiler_params=_cp("parallel", "parallel", "arbitrary"))(a, b)


def rmsnorm_fwd(h, w, *, name, comm=None):
    T, D = h.shape
    tt = _pick(T, 512, 16)

    def body(h_ref, w_ref, n_ref):
        x = h_ref[...]
        r = lax.rsqrt(jnp.mean(x * x, axis=-1, keepdims=True) + EPS)
        n_ref[...] = (x * r * w_ref[...]).astype(BF16)

    body, c_in, c_args, c_out, c_shape, c_scratch = _hosted(body, 2, 1, comm, lambda: pl.program_id(0), T // tt)
    out = pl.pallas_call(
        body, name=name, grid=(T // tt,),
        in_specs=[pl.BlockSpec((tt, D), lambda i: (i, 0)), pl.BlockSpec((1, D), lambda i: (0, 0))] + c_in,
        out_specs=[pl.BlockSpec((tt, D), lambda i: (i, 0))] + c_out,
        out_shape=[jax.ShapeDtypeStruct((T, D), BF16)] + c_shape, scratch_shapes=c_scratch,
        compiler_params=_cp("arbitrary" if comm else "parallel"))(h, w, *c_args)
    return (out[0], out[1:]) if comm else out[0]


def rmsnorm_bwd(h, w, dn, dres, *, name):
    T, D = h.shape
    tt = _pick(T, 512, 16)

    def body(h_ref, w_ref, dn_ref, dres_ref, dh_ref, dhb_ref, gw_ref):
        @pl.when(pl.program_id(0) == 0)
        def _():
            gw_ref[...] = jnp.zeros_like(gw_ref)

        x = h_ref[...]
        r = lax.rsqrt(jnp.mean(x * x, axis=-1, keepdims=True) + EPS)
        xhat = x * r
        g = dn_ref[...].astype(F32)
        gw_ref[...] += jnp.sum(g * xhat, axis=0, keepdims=True)
        dxh = g * w_ref[...]
        dx = r * (dxh - xhat * jnp.mean(dxh * xhat, axis=-1, keepdims=True))
        dh = dres_ref[...] + dx
        dh_ref[...] = dh
        dhb_ref[...] = dh.astype(BF16)

    row = pl.BlockSpec((tt, D), lambda i: (i, 0))
    vec = pl.BlockSpec((1, D), lambda i: (0, 0))
    return pl.pallas_call(
        body, name=name, grid=(T // tt,), in_specs=[row, vec, row, row], out_specs=[row, row, vec],
        out_shape=[jax.ShapeDtypeStruct((T, D), F32), jax.ShapeDtypeStruct((T, D), BF16),
                   jax.ShapeDtypeStruct((1, D), F32)],
        compiler_params=_cp("arbitrary"))(h, w, dn, dres)


def final_loss(h, w, target, *, name):
    T, D = h.shape
    tt = _pick(T, 512, 16)

    def body(h_ref, w_ref, t_ref, loss_ref, dh_ref, dhb_ref, gw_ref):
        @pl.when(pl.program_id(0) == 0)
        def _():
            gw_ref[...] = jnp.zeros_like(gw_ref)
            loss_ref[...] = jnp.zeros_like(loss_ref)

        x = h_ref[...]
        r = lax.rsqrt(jnp.mean(x * x, axis=-1, keepdims=True) + EPS)
        xhat = x * r
        e = xhat * w_ref[...] - t_ref[...]
        loss_ref[...] += jnp.sum(e * e) * (0.5 / D)
        g = e * (1.0 / D)
        gw_ref[...] += jnp.sum(g * xhat, axis=0, keepdims=True)
        dxh = g * w_ref[...]
        dh = r * (dxh - xhat * jnp.mean(dxh * xhat, axis=-1, keepdims=True))
        dh_ref[...] = dh
        dhb_ref[...] = dh.astype(BF16)

    row = pl.BlockSpec((tt, D), lambda i: (i, 0))
    vec = pl.BlockSpec((1, D), lambda i: (0, 0))
    one = pl.BlockSpec((1, LANE), lambda i: (0, 0))
    return pl.pallas_call(
        body, name=name, grid=(T // tt,), in_specs=[row, vec, row], out_specs=[one, row, row, vec],
        out_shape=[jax.ShapeDtypeStruct((1, LANE), F32), jax.ShapeDtypeStruct((T, D), F32),
                   jax.ShapeDtypeStruct((T, D), BF16), jax.ShapeDtypeStruct((1, D), F32)],
        compiler_params=_cp("arbitrary"))(h, w, target)


CONV_CHUNK = 32
ROW_CHUNK = 16
SUBLANES = 8


def _conv_tiles(seq, C):
    return _pick(seq, 512, HALO), _pick(C, 512, LANE)


def _residues(offsets):
    return sorted({s % SUBLANES for s in offsets} - {0})


def _fill_shifted(buf, shifted, residues):
    n = buf.shape[0] - SUBLANES
    for i, r in enumerate(residues):
        shifted[i, 0:n, :] = buf[r:r + n, :]


def _tap(buf, shifted, residues, offset, start, rows):
    r = offset % SUBLANES
    base = offset - r
    ref = buf if r == 0 else shifted.at[residues.index(r)]
    return ref[pl.ds(start + base, rows), :]


def dwconv_fwd(src, offs, w, b, *, C, seq, glu, silu_out, name):
    T = src.shape[0]
    K = w.shape[0]
    tt, tc = _conv_tiles(seq, C)
    n_in = 2 if glu else 1
    per = tt // HALO
    offsets = [HALO - (K - 1) + k for k in range(K)]
    residues = _residues(offsets)

    def body(*refs):
        cur = refs[0:2 * n_in:2]
        halo = refs[1:2 * n_in:2]
        w_ref, b_ref = refs[2 * n_in], refs[2 * n_in + 1]
        outs = refs[2 * n_in + 2:-2]
        buf, shifted = refs[-2], refs[-1]
        i = pl.program_id(1)
        first = (i * tt) % seq == 0

        def pre(rs):
            v = rs[0][...].astype(F32)
            return v * _sigmoid(rs[1][...].astype(F32)) if glu else v

        buf[0:HALO, :] = jnp.where(first, 0.0, pre(halo))
        buf[HALO:HALO + tt, :] = pre(cur)
        _fill_shifted(buf, shifted, residues)

        def chunk(ci, carry):
            start = pl.multiple_of(ci * CONV_CHUNK, CONV_CHUNK)
            acc = jnp.broadcast_to(b_ref[...], (CONV_CHUNK, tc))
            for k in range(K):
                acc = acc + w_ref[k:k + 1, :] * _tap(buf, shifted, residues, offsets[k], start, CONV_CHUNK)
            outs[0][pl.ds(start, CONV_CHUNK), :] = acc.astype(BF16)
            if silu_out:
                outs[1][pl.ds(start, CONV_CHUNK), :] = _silu(acc).astype(BF16)
            return carry

        lax.fori_loop(0, tt // CONV_CHUNK, chunk, 0)

    in_specs, args = [], []
    for off in offs:
        assert off % tc == 0
        in_specs.append(pl.BlockSpec((tt, tc), lambda j, i, ob=off // tc: (i, ob + j)))
        in_specs.append(pl.BlockSpec((HALO, tc), lambda j, i, ob=off // tc: (jnp.maximum(i * per - 1, 0), ob + j)))
        args += [src, src]
    in_specs += [pl.BlockSpec((K, tc), lambda j, i: (0, j)), pl.BlockSpec((1, tc), lambda j, i: (0, j))]
    args += [w, b]
    n_out = 2 if silu_out else 1
    out = pl.pallas_call(
        body, name=name, grid=(C // tc, T // tt), in_specs=in_specs,
        out_specs=[pl.BlockSpec((tt, tc), lambda j, i: (i, j))] * n_out,
        out_shape=[jax.ShapeDtypeStruct((T, C), BF16)] * n_out,
        scratch_shapes=[pltpu.VMEM((HALO + tt, tc), F32), pltpu.VMEM((max(len(residues), 1), HALO + tt, tc), F32)],
        compiler_params=_cp("parallel", "arbitrary"))(*args)
    return out


def dwconv_bwd(du, u, src, offs, w, *, C, seq, glu, silu_out, name):
    T = src.shape[0]
    K = w.shape[0]
    tt, tc = _conv_tiles(seq, C)
    n_in = 2 if glu else 1
    per = tt // HALO
    last_blk = T // HALO - 1
    g_offsets = [K - 1 - k for k in range(K)]
    x_offsets = [HALO - (K - 1) + k for k in range(K)]
    g_res, x_res = _residues(g_offsets), _residues(x_offsets)

    def body(*refs):
        pos = 0
        du_cur, du_nxt = refs[0], refs[1]
        pos = 2
        if silu_out:
            u_cur, u_nxt = refs[2], refs[3]
            pos = 4
        cur = refs[pos:pos + 2 * n_in:2]
        halo = refs[pos + 1:pos + 2 * n_in:2]
        pos += 2 * n_in
        w_ref = refs[pos]
        outs = refs[pos + 1:pos + 1 + n_in]
        dw_ref, db_ref = refs[pos + 1 + n_in], refs[pos + 2 + n_in]
        gbuf, gshift, xbuf, xshift, dw_acc, db_acc = refs[-6:]
        i = pl.program_id(1)
        first = (i * tt) % seq == 0
        last = ((i + 1) * tt) % seq == 0

        @pl.when(i == 0)
        def _():
            dw_acc[...] = jnp.zeros_like(dw_acc)
            db_acc[...] = jnp.zeros_like(db_acc)

        g_cur = du_cur[...].astype(F32)
        g_nxt = du_nxt[...].astype(F32)
        if silu_out:
            g_cur = g_cur * _silu_grad(u_cur[...].astype(F32))
            g_nxt = g_nxt * _silu_grad(u_nxt[...].astype(F32))
        gbuf[0:tt, :] = g_cur
        gbuf[tt:tt + HALO, :] = jnp.where(last, 0.0, g_nxt)

        def pre(rs):
            v = rs[0][...].astype(F32)
            return v * _sigmoid(rs[1][...].astype(F32)) if glu else v

        xbuf[0:HALO, :] = jnp.where(first, 0.0, pre(halo))
        xbuf[HALO:HALO + tt, :] = pre(cur)
        _fill_shifted(gbuf, gshift, g_res)
        _fill_shifted(xbuf, xshift, x_res)

        def fold(v):
            out = v[0:SUBLANES]
            for s in range(SUBLANES, CONV_CHUNK, SUBLANES):
                out = out + v[s:s + SUBLANES]
            return out

        def chunk(ci, carry):
            start = pl.multiple_of(ci * CONV_CHUNK, CONV_CHUNK)
            g = gbuf[pl.ds(start, CONV_CHUNK), :]
            dx = jnp.zeros((CONV_CHUNK, tc), F32)
            for k in range(K):
                dx = dx + w_ref[k:k + 1, :] * _tap(gbuf, gshift, g_res, g_offsets[k], start, CONV_CHUNK)
                dw_acc[k * SUBLANES:(k + 1) * SUBLANES, :] += fold(
                    g * _tap(xbuf, xshift, x_res, x_offsets[k], start, CONV_CHUNK))
            db_acc[...] += fold(g)
            rows = pl.ds(start, CONV_CHUNK)
            if glu:
                a = cur[0][rows, :].astype(F32)
                s = _sigmoid(cur[1][rows, :].astype(F32))
                outs[0][rows, :] = (dx * s).astype(BF16)
                outs[1][rows, :] = (dx * a * s * (1.0 - s)).astype(BF16)
            else:
                outs[0][rows, :] = dx.astype(BF16)
            return carry

        lax.fori_loop(0, tt // CONV_CHUNK, chunk, 0)

        @pl.when(i == T // tt - 1)
        def _():
            for k in range(K):
                dw_ref[k:k + 1, :] = jnp.sum(dw_acc[k * SUBLANES:(k + 1) * SUBLANES, :], axis=0, keepdims=True)
            db_ref[...] = jnp.sum(db_acc[...], axis=0, keepdims=True)

    def cur_spec(ob):
        return pl.BlockSpec((tt, tc), lambda j, i: (i, ob + j))

    def nxt_spec(ob):
        return pl.BlockSpec((HALO, tc), lambda j, i: (jnp.minimum((i + 1) * per, last_blk), ob + j))

    def prv_spec(ob):
        return pl.BlockSpec((HALO, tc), lambda j, i: (jnp.maximum(i * per - 1, 0), ob + j))

    in_specs = [cur_spec(0), nxt_spec(0)]
    args = [du, du]
    if silu_out:
        in_specs += [cur_spec(0), nxt_spec(0)]
        args += [u, u]
    for off in offs:
        assert off % tc == 0
        in_specs += [cur_spec(off // tc), prv_spec(off // tc)]
        args += [src, src]
    in_specs.append(pl.BlockSpec((K, tc), lambda j, i: (0, j)))
    args.append(w)
    out_specs = [pl.BlockSpec((tt, tc), lambda j, i: (i, j))] * n_in
    out_specs += [pl.BlockSpec((K, tc), lambda j, i: (0, j)), pl.BlockSpec((1, tc), lambda j, i: (0, j))]
    out_shape = [jax.ShapeDtypeStruct((T, C), BF16)] * n_in
    out_shape += [jax.ShapeDtypeStruct((K, C), F32), jax.ShapeDtypeStruct((1, C), F32)]
    return pl.pallas_call(
        body, name=name, grid=(C // tc, T // tt), in_specs=in_specs, out_specs=out_specs, out_shape=out_shape,
        scratch_shapes=[pltpu.VMEM((tt + HALO, tc), F32), pltpu.VMEM((max(len(g_res), 1), tt + HALO, tc), F32),
                        pltpu.VMEM((HALO + tt, tc), F32), pltpu.VMEM((max(len(x_res), 1), HALO + tt, tc), F32),
                        pltpu.VMEM((K * SUBLANES, tc), F32), pltpu.VMEM((SUBLANES, tc), F32)],
        compiler_params=_cp("parallel", "arbitrary"))(*args)


def mix0_post_fwd(u2, proj, o, ln_w, ln_b, *, CW, gc_off, ga_off, name):
    T = u2.shape[0]
    tt = _pick(T, 256, 16)

    def body(u_ref, gc_ref, ga_ref, o_ref, lw_ref, lb_ref, y_ref):
        def chunk(ci, carry):
            rows = pl.ds(pl.multiple_of(ci * ROW_CHUNK, ROW_CHUNK), ROW_CHUNK)
            u = u_ref[rows, :].astype(F32)
            mu = jnp.mean(u, axis=-1, keepdims=True)
            xc = u - mu
            r = lax.rsqrt(jnp.mean(xc * xc, axis=-1, keepdims=True) + EPS)
            u3 = xc * r * lw_ref[...] + lb_ref[...]
            y_ref[rows, 0:CW] = (_silu(u3) * _silu(gc_ref[rows, :].astype(F32))).astype(BF16)
            y_ref[rows, CW:2 * CW] = (o_ref[rows, :].astype(F32) * _silu(ga_ref[rows, :].astype(F32))).astype(BF16)
            return carry

        lax.fori_loop(0, tt // ROW_CHUNK, chunk, 0, unroll=4)

    row = pl.BlockSpec((tt, CW), lambda i: (i, 0))
    vec = pl.BlockSpec((1, CW), lambda i: (0, 0))
    return pl.pallas_call(
        body, name=name, grid=(T // tt,),
        in_specs=[row, pl.BlockSpec((tt, CW), lambda i: (i, gc_off // CW)),
                  pl.BlockSpec((tt, CW), lambda i: (i, ga_off // CW)), row, vec, vec],
        out_specs=pl.BlockSpec((tt, 2 * CW), lambda i: (i, 0)),
        out_shape=jax.ShapeDtypeStruct((T, 2 * CW), BF16),
        compiler_params=_cp("parallel"))(u2, proj, proj, o, ln_w, ln_b)


def mix0_post_bwd(dy, u2, proj, o, ln_w, ln_b, *, CW, gc_off, ga_off, name):
    T = u2.shape[0]
    tt = _pick(T, 256, 16)

    def body(dy_ref, u_ref, gc_ref, ga_ref, o_ref, lw_ref, lb_ref, du_ref, dgc_ref, dga_ref, do_ref, dlw_ref, dlb_ref,
             lw_acc, lb_acc):
        i = pl.program_id(0)

        @pl.when(i == 0)
        def _():
            lw_acc[...] = jnp.zeros_like(lw_acc)
            lb_acc[...] = jnp.zeros_like(lb_acc)

        def fold(v):
            out = v[0:SUBLANES]
            for s in range(SUBLANES, ROW_CHUNK, SUBLANES):
                out = out + v[s:s + SUBLANES]
            return out

        def chunk(ci, carry):
            rows = pl.ds(pl.multiple_of(ci * ROW_CHUNK, ROW_CHUNK), ROW_CHUNK)
            dyc = dy_ref[rows, 0:CW].astype(F32)
            dya = dy_ref[rows, CW:2 * CW].astype(F32)
            u = u_ref[rows, :].astype(F32)
            mu = jnp.mean(u, axis=-1, keepdims=True)
            xc = u - mu
            r = lax.rsqrt(jnp.mean(xc * xc, axis=-1, keepdims=True) + EPS)
            xhat = xc * r
            u3 = xhat * lw_ref[...] + lb_ref[...]
            gc = gc_ref[rows, :].astype(F32)
            dgc_ref[rows, :] = (dyc * _silu(u3) * _silu_grad(gc)).astype(BF16)
            du3 = dyc * _silu(gc) * _silu_grad(u3)
            lw_acc[...] += fold(du3 * xhat)
            lb_acc[...] += fold(du3)
            dxh = du3 * lw_ref[...]
            du = r * (dxh - jnp.mean(dxh, axis=-1, keepdims=True)
                      - xhat * jnp.mean(dxh * xhat, axis=-1, keepdims=True))
            du_ref[rows, :] = du.astype(BF16)
            ga = ga_ref[rows, :].astype(F32)
            ov = o_ref[rows, :].astype(F32)
            do_ref[rows, :] = (dya * _silu(ga)).astype(BF16)
            dga_ref[rows, :] = (dya * ov * _silu_grad(ga)).astype(BF16)
            return carry

        lax.fori_loop(0, tt // ROW_CHUNK, chunk, 0, unroll=4)

        @pl.when(i == T // tt - 1)
        def _():
            dlw_ref[...] = jnp.sum(lw_acc[...], axis=0, keepdims=True)
            dlb_ref[...] = jnp.sum(lb_acc[...], axis=0, keepdims=True)

    row = pl.BlockSpec((tt, CW), lambda i: (i, 0))
    vec = pl.BlockSpec((1, CW), lambda i: (0, 0))
    big = jax.ShapeDtypeStruct((T, CW), BF16)
    small = jax.ShapeDtypeStruct((1, CW), F32)
    return pl.pallas_call(
        body, name=name, grid=(T // tt,),
        in_specs=[pl.BlockSpec((tt, 2 * CW), lambda i: (i, 0)), row,
                  pl.BlockSpec((tt, CW), lambda i: (i, gc_off // CW)),
                  pl.BlockSpec((tt, CW), lambda i: (i, ga_off // CW)), row, vec, vec],
        out_specs=[row, row, row, row, vec, vec],
        out_shape=[big, big, big, big, small, small],
        scratch_shapes=[pltpu.VMEM((SUBLANES, CW), F32), pltpu.VMEM((SUBLANES, CW), F32)],
        compiler_params=_cp("arbitrary"))(dy, u2, proj, proj, o, ln_w, ln_b)


SB_UNDERFLOW = 110.0
SB_BOUND_MARGIN = 1.02


def _sb_tile(seq):
    return _pick(seq, 256, LANE)


def _softplus(z):
    return jnp.maximum(z, 0.0) + jnp.log(1.0 + jnp.exp(-jnp.abs(z)))


def _tri01(n, lower):
    i = lax.broadcasted_iota(jnp.int32, (n, n), 0)
    j = lax.broadcasted_iota(jnp.int32, (n, n), 1)
    return ((i >= j) if lower else (i <= j)).astype(BF16)


SB_HEADS_FWD = 4
SB_HEADS_BWD = 2


def _sb_heads_per_step(heads, want):
    while heads % want:
        want //= 2
    return want


def sba_fwd(proj, *, B, seq, heads, q_off, k_off, v_off, name):
    dh = SB_HEAD_DIM
    tq = _sb_tile(seq)
    assert tq % (2 * LANE) == 0
    nq = seq // tq
    hps = _sb_heads_per_step(heads, SB_HEADS_FWD)
    hw = hps * dh
    scale = dh ** -0.5

    def body(q_ref, k_ref, v_ref, tri_ref, o_ref, ct_ref, acc_ref, kmax_ref):
        qi = pl.program_id(1)
        tri = tri_ref[...]
        qs = [(q_ref[:, h * dh:(h + 1) * dh].astype(F32) * scale).astype(BF16) for h in range(hps)]

        @pl.when(qi == 0)
        def _():
            def chunk(i, best):
                rows = k_ref[pl.ds(pl.multiple_of(i * tq, tq), tq), :].astype(F32)
                sq = rows * rows
                return tuple(jnp.maximum(best[h], jnp.max(jnp.sum(sq[:, h * dh:(h + 1) * dh], axis=1, keepdims=True),
                                                          axis=0, keepdims=True)) for h in range(hps))

            best = lax.fori_loop(0, nq, chunk, (jnp.zeros((1, 1), F32),) * hps)
            for h in range(hps):
                kmax_ref[h] = jnp.broadcast_to(jnp.sqrt(best[h]), (8, LANE))

        z_bound = [jnp.sqrt(jnp.sum(qs[h].astype(F32) ** 2, axis=1, keepdims=True))
                   * (SB_BOUND_MARGIN * jnp.max(kmax_ref[h], keepdims=True)) for h in range(hps)]

        def part(h, q_rows, start, n_keys, r, mask):
            k_blk = k_ref[pl.ds(start, n_keys), h * dh:(h + 1) * dh]
            v_blk = v_ref[pl.ds(start, n_keys), h * dh:(h + 1) * dh]
            z = _dot(q_rows, k_blk, NT)
            sp = _softplus(z)
            if mask is not None:
                sp = jnp.where(mask, sp, 0.0)
            wts = jnp.exp(z - (_dot(sp.astype(BF16), tri[0:n_keys, 0:n_keys]) + r))
            if mask is not None:
                wts = jnp.where(mask, wts, 0.0)
            return _dot(wts.astype(BF16), v_blk), r + jnp.sum(sp, axis=-1, keepdims=True)

        below = lax.broadcasted_iota(jnp.int32, (tq, tq), 1) < lax.broadcasted_iota(jnp.int32, (tq, tq), 0)
        has_left = qi > 0
        left = pl.multiple_of(jnp.maximum(qi - 1, 0) * tq, tq)
        rs = []
        for h in range(hps):
            pv_d, r = part(h, qs[h], pl.multiple_of(qi * tq, tq), tq, jnp.zeros((tq, 1), F32), below)
            pv_l, r = part(h, qs[h], left, tq, r, has_left)
            acc_ref[:, h * dh:(h + 1) * dh] = pv_d + pv_l
            rs.append(r)
        rs = tuple(rs)

        def block(start, rs):
            pvs, out = [], []
            for h in range(hps):
                pv, r = part(h, qs[h], start, tq, rs[h], None)
                pvs.append(pv)
                out.append(r)
            return pvs, tuple(out)

        def more(c):
            j, rs = c
            slack = rs[0] - z_bound[0]
            for h in range(1, hps):
                slack = jnp.minimum(slack, rs[h] - z_bound[h])
            return jnp.logical_and(j < qi, jnp.min(slack) <= SB_UNDERFLOW)

        def step(c):
            j, rs = c
            pvs, rs = block(pl.multiple_of((qi - 1 - j) * tq, tq), rs)
            for h in range(hps):
                acc_ref[:, h * dh:(h + 1) * dh] += pvs[h]
            return j + 1, rs

        n_left, totals = lax.while_loop(more, step, (has_left.astype(jnp.int32), rs))
        o_ref[...] = acc_ref[...].astype(BF16)
        for h in range(hps):
            ct_ref[0, 0, h, 0:8, :] = jnp.broadcast_to(totals[h], (tq, LANE)).T[0:8, :]
            ct_ref[0, 0, h, 8:16, :] = jnp.full((8, tq), n_left, F32)

    qb, kb, vb = q_off // hw, k_off // hw, v_off // hw
    G = heads // hps
    return pl.pallas_call(
        body, name=name, grid=(B * G, nq),
        in_specs=[pl.BlockSpec((tq, hw), lambda g, i: ((g // G) * nq + i, qb + g % G)),
                  pl.BlockSpec((seq, hw), lambda g, i: (g // G, kb + g % G)),
                  pl.BlockSpec((seq, hw), lambda g, i: (g // G, vb + g % G)),
                  pl.BlockSpec((tq, tq), lambda g, i: (0, 0))],
        out_specs=[pl.BlockSpec((tq, hw), lambda g, i: ((g // G) * nq + i, g % G)),
                   pl.BlockSpec((1, 1, hps, 16, tq), lambda g, i: (g // G, i, g % G, 0, 0))],
        out_shape=[jax.ShapeDtypeStruct((B * seq, heads * dh), BF16),
                   jax.ShapeDtypeStruct((B, nq, heads, 16, tq), F32)],
        scratch_shapes=[pltpu.VMEM((tq, hw), F32), pltpu.VMEM((hps, 8, LANE), F32)],
        compiler_params=_cp("parallel", "arbitrary"))(proj, proj, proj, jnp.tril(jnp.ones((tq, tq), BF16)))


def sba_bwd(proj, ctot, do, *, B, seq, heads, q_off, k_off, v_off, name, comm=None):
    dh = SB_HEAD_DIM
    tq = _sb_tile(seq)
    nq = seq // tq
    hps = _sb_heads_per_step(heads, SB_HEADS_BWD)
    hw = hps * dh
    scale = dh ** -0.5

    def body(q_ref, k_ref, v_ref, ct_ref, do_ref, sfx_ref, pre_ref, dq_ref, dk_ref, dv_ref, dq_acc, dk_acc, dv_acc):
        qi = pl.program_id(1)

        @pl.when(qi == 0)
        def _():
            dk_acc[...] = jnp.zeros_like(dk_acc)
            dv_acc[...] = jnp.zeros_like(dv_acc)

        tri_sfx = sfx_ref[...]
        tri_pre = pre_ref[...]
        qs = [(q_ref[:, h * dh:(h + 1) * dh].astype(F32) * scale).astype(BF16) for h in range(hps)]
        dos = [do_ref[:, h * dh:(h + 1) * dh] for h in range(hps)]
        totals = [jnp.max(jnp.broadcast_to(ct_ref[0, 0, h, 0:1, :], (LANE, tq)).T, axis=1, keepdims=True)
                  for h in range(hps)]
        dq_acc[...] = jnp.zeros_like(dq_acc)

        def part(h, rows, start, n_keys, pc, pg, mask):
            cols = slice(h * dh, (h + 1) * dh)
            q_rows, do_rows = qs[h][rows], dos[h][rows]
            k_blk = k_ref[pl.ds(start, n_keys), cols]
            v_blk = v_ref[pl.ds(start, n_keys), cols]
            z = _dot(q_rows, k_blk, NT)
            sp = _softplus(z)
            sig = jnp.exp(z - sp)
            if mask is not None:
                sp = jnp.where(mask, sp, 0.0)
            pc_next = pc + jnp.sum(sp, axis=-1, keepdims=True)
            wts = jnp.exp(z - (_dot(sp.astype(BF16), tri_sfx[0:n_keys, 0:n_keys]) + (totals[h][rows] - pc_next)))
            if mask is not None:
                wts = jnp.where(mask, wts, 0.0)
            g = _dot(do_rows, v_blk, NT) * wts
            dz = g - sig * (_dot(g.astype(BF16), tri_pre[0:n_keys, 0:n_keys]) + pg)
            if mask is not None:
                dz = jnp.where(mask, dz, 0.0)
            dz = dz.astype(BF16)
            dq_acc[rows, cols] += _dot(dz, k_blk)
            dk_acc[pl.ds(start, n_keys), cols] += _dot(dz, q_rows, TN)
            dv_acc[pl.ds(start, n_keys), cols] += _dot(wts.astype(BF16), do_rows, TN)
            return pc_next, pg + jnp.sum(g, axis=-1, keepdims=True)

        def block(start, carry):
            return tuple(part(h, slice(0, tq), start, tq, carry[h][0], carry[h][1], None) for h in range(hps))

        zero = jnp.zeros((tq, 1), F32)
        n_left = jnp.max(ct_ref[0, 0, 0, 8:16, :]).astype(jnp.int32)
        carry = lax.fori_loop(qi - n_left, qi - 1, lambda j, c: block(pl.multiple_of(j * tq, tq), c),
                              ((zero, zero),) * hps)
        below = lax.broadcasted_iota(jnp.int32, (tq, tq), 1) < lax.broadcasted_iota(jnp.int32, (tq, tq), 0)
        has_left = n_left > 0
        left = pl.multiple_of(jnp.maximum(qi - 1, 0) * tq, tq)
        for h in range(hps):
            pc, pg = part(h, slice(0, tq), left, tq, carry[h][0], carry[h][1], has_left)
            part(h, slice(0, tq), pl.multiple_of(qi * tq, tq), tq, pc, pg, below)
        dq_ref[...] = (dq_acc[...] * scale).astype(BF16)

        @pl.when(qi == nq - 1)
        def _():
            dk_ref[...] = dk_acc[...].astype(BF16)
            dv_ref[...] = dv_acc[...].astype(BF16)

    qb, kb, vb = q_off // hw, k_off // hw, v_off // hw
    G = heads // hps
    q_spec = pl.BlockSpec((tq, hw), lambda g, i: ((g // G) * nq + i, qb + g % G))
    o_spec = pl.BlockSpec((tq, hw), lambda g, i: ((g // G) * nq + i, g % G))
    kv_out = pl.BlockSpec((seq, hw), lambda g, i: (g // G, g % G))
    shp = jax.ShapeDtypeStruct((B * seq, heads * dh), BF16)
    body, c_in, c_args, c_out, c_shape, c_scratch = _hosted(body, 7, 3, comm, _grid_step(nq), B * G * nq)
    tri_spec = pl.BlockSpec((tq, tq), lambda g, i: (0, 0))
    ones = jnp.ones((tq, tq), BF16)
    out = pl.pallas_call(
        body, name=name, grid=(B * G, nq),
        in_specs=[q_spec,
                  pl.BlockSpec((seq, hw), lambda g, i: (g // G, kb + g % G)),
                  pl.BlockSpec((seq, hw), lambda g, i: (g // G, vb + g % G)),
                  pl.BlockSpec((1, 1, hps, 16, tq), lambda g, i: (g // G, i, g % G, 0, 0)), o_spec,
                  tri_spec, tri_spec] + c_in,
        out_specs=[o_spec, kv_out, kv_out] + c_out, out_shape=[shp, shp, shp] + c_shape,
        scratch_shapes=[pltpu.VMEM((tq, hw), F32), pltpu.VMEM((seq, hw), F32), pltpu.VMEM((seq, hw), F32)]
        + c_scratch,
        compiler_params=_cp("arbitrary" if comm else "parallel", "arbitrary"))(
            proj, proj, proj, ctot, do, jnp.tril(ones), jnp.triu(ones), *c_args)
    return (out[0], out[1], out[2], out[3:]) if comm else out


def _head_expand(n_heads, DI):
    j = jnp.arange(LANE, dtype=jnp.int32)[:, None]
    c = jnp.arange(DI, dtype=jnp.int32)[None, :] // SSM_HEAD_DIM
    return ((j == c) & (j < n_heads)).astype(BF16)


def _split3(x):
    hi = x.astype(BF16)
    r1 = x - hi.astype(F32)
    mid = r1.astype(BF16)
    return hi, mid, (r1 - mid.astype(F32)).astype(BF16)


def dt_fwd(proj, bias, a_log, expand, *, dt_off, name):
    T = proj.shape[0]
    DI = expand.shape[1]
    L = SSM_CHUNK
    tt = _pick(T, 512, L)

    def body(raw_ref, bias_ref, al_ref, e_ref, dt_ref, cs_ref, dtx_ref, csx_ref):
        x = raw_ref[...].astype(F32) + bias_ref[...]
        dt = _softplus(x)
        dt_ref[...] = dt
        la = dt * (-jnp.exp(al_ref[...]))
        tri = _tri01(L, True)
        for c in range(tt // L):
            cs_ref[c * L:(c + 1) * L, :] = _tri_dot3(tri, la[c * L:(c + 1) * L, :])
        e = e_ref[...]
        dtx_ref[...] = _dot(dt.astype(BF16), e).astype(BF16)
        hi, mid, lo = _split3(cs_ref[...])
        csx_ref[...] = _dot(hi, e) + _dot(mid, e) + _dot(lo, e)

    row = pl.BlockSpec((tt, LANE), lambda i: (i, 0))
    wide = pl.BlockSpec((tt, DI), lambda i: (i, 0))
    vec = pl.BlockSpec((1, LANE), lambda i: (0, 0))
    return pl.pallas_call(
        body, name=name, grid=(T // tt,),
        in_specs=[pl.BlockSpec((tt, LANE), lambda i: (i, dt_off // LANE)), vec, vec,
                  pl.BlockSpec((LANE, DI), lambda i: (0, 0))],
        out_specs=[row, row, wide, wide],
        out_shape=[jax.ShapeDtypeStruct((T, LANE), F32), jax.ShapeDtypeStruct((T, LANE), F32),
                   jax.ShapeDtypeStruct((T, DI), BF16), jax.ShapeDtypeStruct((T, DI), F32)],
        compiler_params=_cp("parallel"))(proj, bias, a_log, expand)


def dt_bwd(ddt_x, dcs_x, dcs_cols, proj, dt, bias, a_log, reduce_t, *, dt_off, n_heads, name):
    T = proj.shape[0]
    DI = reduce_t.shape[0]
    L = SSM_CHUNK
    tt = _pick(T, 512, L)

    def body(ddtx_ref, dcsx_ref, dcsc_ref, raw_ref, dt_ref, bias_ref, al_ref, r_ref, draw_ref, dbias_ref, dal_ref,
             dla_buf):
        @pl.when(pl.program_id(0) == 0)
        def _():
            dbias_ref[...] = jnp.zeros_like(dbias_ref)
            dal_ref[...] = jnp.zeros_like(dal_ref)

        r = r_ref[...]
        ddt = _dot(ddtx_ref[...], r)
        dx = dcsx_ref[...]
        hi = dx.astype(BF16)
        dcs = _dot(hi, r) + _dot((dx - hi.astype(F32)).astype(BF16), r) + dcsc_ref[...]
        triu = _tri01(L, False)
        for c in range(tt // L):
            dla_buf[c * L:(c + 1) * L, :] = _tri_dot3(triu, dcs[c * L:(c + 1) * L, :])
        dla = dla_buf[...]
        a = -jnp.exp(al_ref[...])
        valid = lax.broadcasted_iota(jnp.int32, (tt, LANE), 1) < n_heads
        dal_ref[...] += jnp.sum(jnp.where(valid, dla * dt_ref[...], 0.0), axis=0, keepdims=True) * a
        x = raw_ref[...].astype(F32) + bias_ref[...]
        draw = jnp.where(valid, (ddt + dla * a) * _sigmoid(x), 0.0)
        dbias_ref[...] += jnp.sum(draw, axis=0, keepdims=True)
        draw_ref[...] = draw.astype(BF16)

    row = pl.BlockSpec((tt, LANE), lambda i: (i, 0))
    wide = pl.BlockSpec((tt, DI), lambda i: (i, 0))
    vec = pl.BlockSpec((1, LANE), lambda i: (0, 0))
    return pl.pallas_call(
        body, name=name, grid=(T // tt,),
        in_specs=[wide, wide, row, pl.BlockSpec((tt, LANE), lambda i: (i, dt_off // LANE)), row, vec, vec,
                  pl.BlockSpec((DI, LANE), lambda i: (0, 0))],
        out_specs=[row, vec, vec],
        out_shape=[jax.ShapeDtypeStruct((T, LANE), BF16), jax.ShapeDtypeStruct((1, LANE), F32),
                   jax.ShapeDtypeStruct((1, LANE), F32)],
        scratch_shapes=[pltpu.VMEM((tt, LANE), F32)],
        compiler_params=_cp("arbitrary"))(ddt_x, dcs_x, dcs_cols, proj, dt, bias, a_log, reduce_t)


def _pair_terms(x_ref, dtx_ref, csx_ref, csr_ref, pair, ppg, lo_half, causal):
    L = SSM_CHUNK
    g, pp = divmod(pair, ppg)
    ra = g * HEAD_ROWS + 2 * pp
    cols = slice(pair * LANE, (pair + 1) * LANE)
    X = x_ref[:, cols].astype(F32)
    dt_p = dtx_ref[:, cols].astype(F32)
    own = csx_ref[:, cols]
    other = pltpu.roll(own, SSM_HEAD_DIM, 1)
    csa_c = jnp.where(lo_half, own, other)
    csb_c = jnp.where(lo_half, other, own)
    La = jnp.exp(jnp.where(causal, csa_c - csr_ref[0, ra:ra + 1, :], NEG_BIG))
    Lb = jnp.exp(jnp.where(causal, csb_c - csr_ref[0, ra + 1:ra + 2, :], NEG_BIG))
    last = csx_ref[L - 1:L, cols]
    return g, ra, cols, X, dt_p, La, Lb, jnp.exp(own), jnp.exp(last - own), jnp.exp(last)


def scan_fwd(xbc, dt_x, cs_x, cs_row, d_full, *, B, seq, DI, name):
    L, N, G = SSM_CHUNK, SSM_STATE, SSM_GROUPS
    nc = seq // L
    XW = xbc.shape[1]
    n_pairs = DI // LANE
    ppg = n_pairs // G

    def body(x_ref, dtx_ref, csx_ref, csr_ref, d_ref, y_ref, st_ref, state):
        c = pl.program_id(1)

        @pl.when(c == 0)
        def _():
            state[...] = jnp.zeros_like(state)

        causal = lax.broadcasted_iota(jnp.int32, (L, L), 0) >= lax.broadcasted_iota(jnp.int32, (L, L), 1)
        lo_half = lax.broadcasted_iota(jnp.int32, (L, LANE), 1) < SSM_HEAD_DIM
        cbs = []
        for g in range(G):
            Bc = x_ref[:, DI + g * N:DI + (g + 1) * N]
            Cc = x_ref[:, DI + G * N + g * N:DI + G * N + (g + 1) * N]
            cbs.append((Bc, Cc, _dot(Cc, Bc, NT)))
        for pair in range(n_pairs):
            g, _, cols, X, dt_p, La, Lb, ecs, tail, e_last = _pair_terms(
                x_ref, dtx_ref, csx_ref, csr_ref, pair, ppg, lo_half, causal)
            Bc, Cc, CB = cbs[g]
            xs = X * dt_p
            xsb = xs.astype(BF16)
            y = jnp.where(lo_half, _dot((CB * La).astype(BF16), xsb), _dot((CB * Lb).astype(BF16), xsb))
            ST = state[pair]
            st_ref[0, 0, pair] = ST
            y = y + ecs * _dot(Cc, ST.astype(BF16)) + d_ref[:, cols] * X
            y_ref[:, cols] = y.astype(BF16)
            state[pair] = e_last * ST + _dot(Bc, (xs * tail).astype(BF16), TN)

    wide = pl.BlockSpec((L, DI), lambda b, c: (b * nc + c, 0))
    return pl.pallas_call(
        body, name=name, grid=(B, nc),
        in_specs=[pl.BlockSpec((L, XW), lambda b, c: (b * nc + c, 0)), wide, wide,
                  pl.BlockSpec((1, G * HEAD_ROWS, L), lambda b, c: (b, 0, c)),
                  pl.BlockSpec((1, DI), lambda b, c: (0, 0))],
        out_specs=[wide, pl.BlockSpec((1, 1, n_pairs, N, LANE), lambda b, c: (b, c, 0, 0, 0))],
        out_shape=[jax.ShapeDtypeStruct((B * seq, DI), BF16),
                   jax.ShapeDtypeStruct((B, nc, n_pairs, N, LANE), F32)],
        scratch_shapes=[pltpu.VMEM((n_pairs, N, LANE), F32)],
        compiler_params=_cp("parallel", "arbitrary"))(xbc, dt_x, cs_x, cs_row, d_full)


def scan_bwd(xbc, dt_x, cs_x, cs_row, d_full, states, dy, *, B, seq, DI, name):
    L, N, G = SSM_CHUNK, SSM_STATE, SSM_GROUPS
    nc = seq // L
    XW = xbc.shape[1]
    n_pairs = DI // LANE
    ppg = n_pairs // G
    HR = G * HEAD_ROWS
    inv_p = 1.0 / SSM_HEAD_DIM

    def body(x_ref, dtx_ref, csx_ref, csr_ref, d_ref, st_ref, dy_ref, dx_ref, ddtx_ref, dcsx_ref, dcsr_ref, dd_ref,
             dH):
        c = pl.program_id(1)

        @pl.when(c == 0)
        def _():
            dH[...] = jnp.zeros_like(dH)
            dd_ref[...] = jnp.zeros_like(dd_ref)

        causal = lax.broadcasted_iota(jnp.int32, (L, L), 0) >= lax.broadcasted_iota(jnp.int32, (L, L), 1)
        lo_half = lax.broadcasted_iota(jnp.int32, (L, LANE), 1) < SSM_HEAD_DIM
        last_row = lax.broadcasted_iota(jnp.int32, (L, LANE), 0) == L - 1
        head_row = lax.broadcasted_iota(jnp.int32, (HR, 1), 0)
        dcs_rows = jnp.zeros((HR, L), F32)

        for g in range(G):
            Bc = x_ref[:, DI + g * N:DI + (g + 1) * N]
            Cc = x_ref[:, DI + G * N + g * N:DI + G * N + (g + 1) * N]
            CB = _dot(Cc, Bc, NT)
            dCB = jnp.zeros((L, L), F32)
            dC = jnp.zeros((L, N), F32)
            dB = jnp.zeros((L, N), F32)
            for pp in range(ppg):
                pair = g * ppg + pp
                _, ra, cols, X, dt_p, La, Lb, ecs, tail, e_last = _pair_terms(
                    x_ref, dtx_ref, csx_ref, csr_ref, pair, ppg, lo_half, causal)
                xs = X * dt_p
                xsb = xs.astype(BF16)
                Ma, Mb = CB * La, CB * Lb
                dY = dy_ref[:, cols].astype(F32)
                dYb = dY.astype(BF16)
                dMa = _dot(jnp.where(lo_half, dY, 0.0).astype(BF16), xsb, NT)
                dMb = _dot(jnp.where(lo_half, 0.0, dY).astype(BF16), xsb, NT)
                dSa, dSb = dMa * Ma, dMb * Mb
                dCB = dCB + dMa * La + dMb * Lb
                dcs = jnp.where(lo_half, jnp.sum(dSa, axis=1, keepdims=True), jnp.sum(dSb, axis=1, keepdims=True)) * inv_p
                dcs_rows = dcs_rows - jnp.where(head_row == ra, jnp.sum(dSa, axis=0, keepdims=True), 0.0)
                dcs_rows = dcs_rows - jnp.where(head_row == ra + 1, jnp.sum(dSb, axis=0, keepdims=True), 0.0)
                dxs = jnp.where(lo_half, _dot(Ma.astype(BF16), dYb, TN), _dot(Mb.astype(BF16), dYb, TN))
                ST = st_ref[0, 0, pair]
                STb = ST.astype(BF16)
                dYe = (dY * ecs).astype(BF16)
                dC = dC + _dot(dYe, STb, NT)
                dSTp = _dot(Cc, dYe, TN)
                dcs = dcs + dY * (ecs * _dot(Cc, STb))
                dSTn = dH[pair]
                dSTnb = dSTn.astype(BF16)
                dSTp = dSTp + e_last * dSTn
                XBt = _dot(Bc, dSTnb)
                dxs = dxs + tail * XBt
                t2 = xs * XBt * tail
                at_end = e_last * jnp.sum(dSTn * ST, axis=0, keepdims=True) + jnp.sum(t2, axis=0, keepdims=True)
                dcs = dcs - t2 + jnp.where(last_row, at_end, 0.0)
                dB = dB + _dot((xs * tail).astype(BF16), dSTnb, NT)
                dx_ref[:, cols] = (dxs * dt_p + d_ref[:, cols] * dY).astype(BF16)
                ddtx_ref[:, cols] = (dxs * X).astype(BF16)
                dcsx_ref[:, cols] = dcs
                dd_ref[0, :, cols] += jnp.sum(dY * X, axis=0, keepdims=True)
                dH[pair] = dSTp
            dCBb = dCB.astype(BF16)
            dx_ref[:, DI + g * N:DI + (g + 1) * N] = (dB + _dot(dCBb, Cc, TN)).astype(BF16)
            dx_ref[:, DI + G * N + g * N:DI + G * N + (g + 1) * N] = (dC + _dot(dCBb, Bc)).astype(BF16)
        dcsr_ref[0] = dcs_rows

    rev = lambda b, c: (b * nc + (nc - 1 - c), 0)
    wide = pl.BlockSpec((L, DI), rev)
    hrow = pl.BlockSpec((1, HR, L), lambda b, c: (b, 0, nc - 1 - c))
    return pl.pallas_call(
        body, name=name, grid=(B, nc),
        in_specs=[pl.BlockSpec((L, XW), rev), wide, wide, hrow,
                  pl.BlockSpec((1, DI), lambda b, c: (0, 0)),
                  pl.BlockSpec((1, 1, n_pairs, N, LANE), lambda b, c: (b, nc - 1 - c, 0, 0, 0)), wide],
        out_specs=[pl.BlockSpec((L, XW), rev), wide, wide, hrow, pl.BlockSpec((1, 1, DI), lambda b, c: (b, 0, 0))],
        out_shape=[jax.ShapeDtypeStruct((B * seq, XW), BF16), jax.ShapeDtypeStruct((B * seq, DI), BF16),
                   jax.ShapeDtypeStruct((B * seq, DI), F32), jax.ShapeDtypeStruct((B, HR, seq), F32),
                   jax.ShapeDtypeStruct((B, 1, DI), F32)],
        scratch_shapes=[pltpu.VMEM((n_pairs, N, LANE), F32)],
        compiler_params=_cp("parallel", "arbitrary"))(xbc, dt_x, cs_x, cs_row, d_full, states, dy)


def gnorm_fwd(y, proj, w, *, DI, name):
    T = y.shape[0]
    tt = _pick(T, 256, 16)
    gw = DI // SSM_GROUPS

    def body(y_ref, z_ref, w_ref, o_ref):
        for g in range(SSM_GROUPS):
            sl = slice(g * gw, (g + 1) * gw)
            y2 = y_ref[:, sl].astype(F32) * _silu(z_ref[:, sl].astype(F32))
            r = lax.rsqrt(jnp.mean(y2 * y2, axis=-1, keepdims=True) + EPS)
            o_ref[:, sl] = (y2 * r * w_ref[:, sl]).astype(BF16)

    row = pl.BlockSpec((tt, DI), lambda i: (i, 0))
    return pl.pallas_call(
        body, name=name, grid=(T // tt,),
        in_specs=[row, row, pl.BlockSpec((1, DI), lambda i: (0, 0))], out_specs=row,
        out_shape=jax.ShapeDtypeStruct((T, DI), BF16), compiler_params=_cp("parallel"))(y, proj, w)


def gnorm_bwd(dyn, y, proj, w, *, DI, name):
    T = y.shape[0]
    tt = _pick(T, 256, 16)
    gw = DI // SSM_GROUPS

    def body(dyn_ref, y_ref, z_ref, w_ref, dy_ref, dz_ref, dw_ref):
        @pl.when(pl.program_id(0) == 0)
        def _():
            dw_ref[...] = jnp.zeros_like(dw_ref)

        for g in range(SSM_GROUPS):
            sl = slice(g * gw, (g + 1) * gw)
            yv = y_ref[:, sl].astype(F32)
            z = z_ref[:, sl].astype(F32)
            sz = _silu(z)
            y2 = yv * sz
            r = lax.rsqrt(jnp.mean(y2 * y2, axis=-1, keepdims=True) + EPS)
            xhat = y2 * r
            d = dyn_ref[:, sl].astype(F32)
            dw_ref[:, sl] += jnp.sum(d * xhat, axis=0, keepdims=True)
            dxh = d * w_ref[:, sl]
            dy2 = r * (dxh - xhat * jnp.mean(dxh * xhat, axis=-1, keepdims=True))
            dy_ref[:, sl] = (dy2 * sz).astype(BF16)
            dz_ref[:, sl] = (dy2 * yv * _silu_grad(z)).astype(BF16)

    row = pl.BlockSpec((tt, DI), lambda i: (i, 0))
    vec = pl.BlockSpec((1, DI), lambda i: (0, 0))
    shp = jax.ShapeDtypeStruct((T, DI), BF16)
    return pl.pallas_call(
        body, name=name, grid=(T // tt,), in_specs=[row, row, row, vec], out_specs=[row, row, vec],
        out_shape=[shp, shp, jax.ShapeDtypeStruct((1, DI), F32)],
        compiler_params=_cp("arbitrary"))(dyn, y, proj, w)


N_CHIP = 4


def _comm_out_shapes(srcs, modes):
    return [jax.ShapeDtypeStruct(((N_DEV,) if mode in ("gather", "gather_direct") else ()) + s.shape, s.dtype)
            for s, mode in zip(srcs, modes)]


def _comm_scratch(n):
    return [pltpu.SemaphoreType.DMA((n, N_DEV - 1)), pltpu.SemaphoreType.DMA((n, N_DEV - 1)),
            pltpu.SemaphoreType.DMA((n,))]


def _comm_phases(modes, src_refs, out_refs, send_sems, recv_sems, local_sems):
    x, y, c = lax.axis_index("x"), lax.axis_index("y"), lax.axis_index("c")
    me, sibling = (x, y, c), (x, y, 1 - c)
    chips = [(1 - x, y), (x, 1 - y), (1 - x, 1 - y)]
    relays = [a for a, mode in enumerate(modes) if mode == "gather"]

    def slot(p):
        return 4 * p[0] + 2 * p[1] + p[2]

    def remote(a, k, src, dst, to):
        return pltpu.make_async_remote_copy(src_ref=src, dst_ref=dst, send_sem=send_sems.at[a, k],
                                            recv_sem=recv_sems.at[a, k], device_id=to,
                                            device_id_type=pl.DeviceIdType.MESH)

    def first_copies():
        local, two_way, send_only = [], [], []
        for a, mode in enumerate(modes):
            src, out = src_refs[a], out_refs[a]
            if mode == "sibling":
                two_way.append(remote(a, 0, src, out, sibling))
            elif mode == "chips":
                mine = 2 * x + y
                local.append(pltpu.make_async_copy(src.at[mine], out.at[mine], local_sems.at[a]))
                for j, chip in enumerate(chips):
                    two_way.append(remote(a, 1 + j, src.at[2 * chip[0] + chip[1]], out.at[mine], (*chip, c)))
            elif mode == "gather_direct":
                local.append(pltpu.make_async_copy(src, out.at[slot(me)], local_sems.at[a]))
                for k in range(1, N_DEV):
                    peer = (1 - x if k & 4 else x, 1 - y if k & 2 else y, 1 - c if k & 1 else c)
                    two_way.append(remote(a, k - 1, src, out.at[slot(me)], peer))
            else:
                assert mode == "gather"
                local.append(pltpu.make_async_copy(src, out.at[slot(me)], local_sems.at[a]))
                send_only.append(remote(a, 0, src, out.at[slot(me)], sibling))
                for j, chip in enumerate(chips):
                    send_only.append(remote(a, 1 + j, src, out.at[slot(me)], (*chip, c)))
        return local, two_way, send_only

    def forwards():
        out = []
        for a in relays:
            for j, chip in enumerate(chips):
                landed = out_refs[a].at[slot((*chip, c))]
                out.append((remote(a, 1 + j, landed, landed, me), remote(a, 4 + j, landed, landed, sibling)))
        return out

    def start():
        local, two_way, send_only = first_copies()
        for cp in local + two_way + send_only:
            cp.start()

    def relay():
        for arrival, fwd in forwards():
            arrival.wait_recv()
            fwd.start()

    def finish():
        local, two_way, send_only = first_copies()
        for a in relays:
            blk = out_refs[a].at[slot(sibling)]
            remote(a, 0, blk, blk, me).wait_recv()
            for j, chip in enumerate(chips):
                blk = out_refs[a].at[slot((*chip, 1 - c))]
                remote(a, 4 + j, blk, blk, me).wait_recv()
        for cp in send_only + [fwd for _, fwd in forwards()]:
            cp.wait_send()
        for cp in two_way + local:
            cp.wait()

    return start, relay, finish, bool(relays)


def _hosted(body, n_in, n_out, comm, step, n_steps):
    if comm is None:
        return body, [], [], [], [], []
    srcs, modes = comm
    nc = len(srcs)

    def wrapped(*refs):
        ins, csrc = refs[:n_in], refs[n_in:n_in + nc]
        outs = refs[n_in + nc:n_in + nc + n_out]
        cout = refs[n_in + nc + n_out:n_in + 2 * nc + n_out]
        scratch = refs[n_in + 2 * nc + n_out:len(refs) - 3]
        start, relay, finish, has_relay = _comm_phases(modes, csrc, cout, *refs[len(refs) - 3:])
        s = step()
        pl.when(s == 0)(start)
        body(*ins, *outs, *scratch)
        if has_relay:
            pl.when(s == (2 * n_steps) // 3)(relay)
        pl.when(s == n_steps - 1)(finish)

    any_spec = pl.BlockSpec(memory_space=pl.ANY)
    return wrapped, [any_spec] * nc, list(srcs), [any_spec] * nc, _comm_out_shapes(srcs, modes), _comm_scratch(nc)


def exchange(srcs, modes, *, name):
    n = len(srcs)

    def body(*refs):
        start, relay, finish, has_relay = _comm_phases(modes, refs[:n], refs[n:2 * n], *refs[2 * n:])
        start()
        if has_relay:
            relay()
        finish()

    any_spec = pl.BlockSpec(memory_space=pl.ANY)
    return pl.pallas_call(
        body, name=name, in_specs=[any_spec] * n, out_specs=[any_spec] * n, out_shape=_comm_out_shapes(srcs, modes),
        scratch_shapes=_comm_scratch(n), compiler_params=pltpu.CompilerParams(has_side_effects=True))(*srcs)


def pair_sum(a, b, *, name):
    n, R, C = a.shape
    tr = _pick(n * R, 512, 16)

    def body(a_ref, b_ref, o_ref):
        o_ref[...] = (a_ref[...].astype(F32) + b_ref[...].astype(F32)).astype(BF16)

    blk = pl.BlockSpec((tr, C), lambda i: (i, 0))
    out = pl.pallas_call(
        body, name=name, grid=(n * R // tr,), in_specs=[blk, blk], out_specs=blk,
        out_shape=jax.ShapeDtypeStruct((n * R, C), BF16),
        compiler_params=_cp("parallel"))(a.reshape(n * R, C), b.reshape(n * R, C))
    return out.reshape(n, R, C)


def sum_slots(recv, *, name):
    _, R, C = recv.shape
    tr = _pick(R, 512, 8)

    def body(r_ref, o_ref):
        acc = r_ref[0].astype(F32)
        for p in range(1, N_DEV):
            acc = acc + r_ref[p].astype(F32)
        o_ref[...] = acc

    return pl.pallas_call(
        body, name=name, grid=(R // tr,),
        in_specs=[pl.BlockSpec((N_DEV, tr, C), lambda i: (0, i, 0))],
        out_specs=pl.BlockSpec((tr, C), lambda i: (i, 0)),
        out_shape=jax.ShapeDtypeStruct((R, C), F32), compiler_params=_cp("parallel"))(recv)


def adamw(gsrc, w, m, v, *, name):
    slots, R, C = gsrc.shape
    tr = _pick(R, 256, 16 if gsrc.dtype == BF16 else 8)
    c1 = 1.0 / (1.0 - ADAM_B1 ** ADAM_STEP)
    c2 = 1.0 / (1.0 - ADAM_B2 ** ADAM_STEP)

    def body(g_ref, w_ref, m_ref, v_ref, go_ref, d_ref, mo_ref, vo_ref):
        g = g_ref[0].astype(F32)
        for p in range(1, slots):
            g = g + g_ref[p].astype(F32)
        m2 = ADAM_B1 * m_ref[...] + (1.0 - ADAM_B1) * g
        v2 = ADAM_B2 * v_ref[...] + (1.0 - ADAM_B2) * (g * g)
        go_ref[...] = g
        mo_ref[...] = m2
        vo_ref[...] = v2
        d_ref[...] = -ADAM_LR * ((m2 * c1) / (jnp.sqrt(v2 * c2) + ADAM_EPS) + ADAM_WD * w_ref[...])

    blk = pl.BlockSpec((tr, C), lambda i: (i, 0))
    shp = jax.ShapeDtypeStruct((R, C), F32)
    return pl.pallas_call(
        body, name=name, grid=(R // tr,),
        in_specs=[pl.BlockSpec((slots, tr, C), lambda i: (0, i, 0)), blk, blk, blk],
        out_specs=[blk] * 4, out_shape=[shp] * 4, compiler_params=_cp("parallel"))(gsrc, w, m, v)


def _pad_cols(a, n):
    return jnp.pad(a, ((0, 0), (0, n - a.shape[1])))


def _to_rows(a, B, seq, H):
    G = SSM_GROUPS
    R = H // G
    t = a[:, :H].reshape(B, seq, G, R).transpose(0, 2, 3, 1)
    t = jnp.pad(t, ((0, 0), (0, 0), (0, HEAD_ROWS - R), (0, 0)))
    return t.reshape(B, G * HEAD_ROWS, seq)


def _from_rows(a, B, seq, H):
    G = SSM_GROUPS
    R = H // G
    t = a.reshape(B, G, HEAD_ROWS, seq)[:, :, :R].transpose(0, 3, 1, 2).reshape(B * seq, H)
    return _pad_cols(t, LANE)


def _chip_sums(grads, name):
    c_idx = lax.axis_index("c")
    keep, give = [], []
    for g in grads:
        by_chip = g.reshape((N_CHIP, 2) + g.shape[1:])
        keep.append(lax.dynamic_index_in_dim(by_chip, c_idx, axis=1, keepdims=False))
        give.append(lax.dynamic_index_in_dim(by_chip, 1 - c_idx, axis=1, keepdims=False))
    swapped = exchange(give, ["sibling"] * len(give), name="swap_" + name)
    return [pair_sum(k, s, name=f"chip_sum_{name}_{i}") for i, (k, s) in enumerate(zip(keep, swapped))]


def local_step(x, target, loc, od_w_in_t, *, B, seq):
    T, D = x.shape
    CW = D
    heads = CW // SB_HEAD_DIM
    DI = 2 * D
    H = DI // SSM_HEAD_DIM
    XW = DI + 2 * SSM_GROUPS * SSM_STATE
    in_odd = DI + XW + H
    w1_rows = in_odd // N_DEV
    q_off, k_off, v_off, gc_off, ga_off = 3 * CW, 4 * CW, 5 * CW, 2 * CW, 6 * CW
    dt_off = DI + XW

    small_packed, small_spans = _pack_rows([loc[n] for n in SMALL_SHARDED], LANE, 8)
    n0, (g_ev_in, small_all) = rmsnorm_fwd(x, loc["ev_norm_w"], name="l0_norm",
                                           comm=([loc["ev_w_in"].astype(BF16), small_packed],
                                                 ["gather", "gather_direct"]))
    p = {n: loc[n] for n in SMALL}
    for n, a in zip(SMALL_SHARDED, _unpack_rows(small_all, small_spans)):
        p[n] = _col_unshards(a)
    p["ev_w_in"] = _col_unshards(g_ev_in)
    proj0, (g_od_in_t, g_od_out, g_ev_out) = mm_nn(
        n0, p["ev_w_in"], out_dtype=BF16, name="l0_in_proj",
        comm=([od_w_in_t.astype(BF16), loc["od_w_out"].astype(BF16), loc["ev_w_out"].astype(BF16)], ["gather"] * 3))
    p["ev_w_out"] = g_ev_out.reshape(-1, D)
    w1t = g_od_in_t[:, :w1_rows].reshape(in_odd, D)
    w1t = jnp.pad(w1t, ((0, -(-(in_odd + LANE) // 256) * 256 - in_odd), (0, 0)))
    od_w_out = g_od_out.reshape(-1, D)
    (u2,) = dwconv_fwd(proj0, (0, CW), p["ev_dw_w"], p["ev_dw_b"], C=CW, seq=seq, glu=True, silu_out=False,
                       name="l0_conv")
    o, ctot = sba_fwd(proj0, B=B, seq=seq, heads=heads, q_off=q_off, k_off=k_off, v_off=v_off, name="l0_attn")
    ycat = mix0_post_fwd(u2, proj0, o, p["ev_ln_w"], p["ev_ln_b"], CW=CW, gc_off=gc_off, ga_off=ga_off,
                         name="l0_post")
    h1 = mm_nn(ycat, p["ev_w_out"], add=x, out_dtype=F32, name="l0_out_proj")

    n1 = rmsnorm_fwd(h1, p["od_norm_w"], name="l1_norm")
    proj1 = mm_nt_terms([(n1, 0, D, 0)], w1t, out_dtype=BF16, name="l1_in_proj")
    u_pre, xbc = dwconv_fwd(proj1, (DI,), p["od_conv_w"], p["od_conv_b"], C=XW, seq=seq, glu=False, silu_out=True,
                            name="l1_conv")
    bias_p, alog_p = _pad_cols(p["od_dt_bias"], LANE), _pad_cols(p["od_a_log"], LANE)
    expand = _head_expand(H, DI)
    dt, cs, dt_x, cs_x = dt_fwd(proj1, bias_p, alog_p, expand, dt_off=dt_off, name="l1_dt")
    cs_row = _to_rows(cs, B, seq, H)
    d_full = jnp.repeat(p["od_d"], SSM_HEAD_DIM, axis=1)
    y_ssd, states = scan_fwd(xbc, dt_x, cs_x, cs_row, d_full, B=B, seq=seq, DI=DI, name="l1_ssd")
    yn = gnorm_fwd(y_ssd, proj1, p["od_gnorm_w"], DI=DI, name="l1_gnorm")
    h2 = mm_nn(yn, od_w_out, add=h1, out_dtype=F32, name="l1_out_proj")

    loss, dh2, dh2b, g_final = final_loss(h2, p["final_norm_w"], target, name="loss_head")

    g_od_w_out = mm_tn(yn, dh2b, out_dtype=BF16, name="l1_dw_out")
    dyn = mm_nt_terms([(dh2b, 0, D, 0)], od_w_out, out_dtype=BF16, name="l1_d_out_proj")
    dy_ssd, dz, g_gnorm = gnorm_bwd(dyn, y_ssd, proj1, p["od_gnorm_w"], DI=DI, name="l1_gnorm_bwd")
    dxbc_c, ddt_x, dcs_x, dcs_row, dd_part = scan_bwd(xbc, dt_x, cs_x, cs_row, d_full, states, dy_ssd, B=B, seq=seq,
                                                      DI=DI, name="l1_ssd_bwd")
    g_d = dd_part.sum(axis=(0, 1)).reshape(H, SSM_HEAD_DIM).sum(axis=1)[None, :]
    draw, g_bias, g_alog = dt_bwd(ddt_x, dcs_x, _from_rows(dcs_row, B, seq, H), proj1, dt, bias_p, alog_p, expand.T,
                                  dt_off=dt_off, n_heads=H, name="l1_dt_bwd")
    dxbc, g_conv_w, g_conv_b = dwconv_bwd(dxbc_c, u_pre, proj1, (DI,), p["od_conv_w"], C=XW, seq=seq, glu=False,
                                          silu_out=True, name="l1_conv_bwd")
    tw = 512 if DI % 512 == 0 else LANE
    terms = [(dz, j, tw, j * tw) for j in range(DI // tw)]
    terms += [(dxbc, j, tw, DI + j * tw) for j in range(XW // tw)]
    terms += [(draw, 0, LANE, dt_off)]
    dn1 = mm_nn_terms(terms, w1t, out_dtype=BF16, name="l1_d_in_proj")
    g_od_w_in_t = jnp.concatenate([mm_tn(dz, n1, out_dtype=BF16, name="l1_dw_in_z"),
                                   mm_tn(dxbc, n1, out_dtype=BF16, name="l1_dw_in_xbc"),
                                   mm_tn(draw, n1, out_dtype=BF16, name="l1_dw_in_dt")], axis=0)[:in_odd]
    dh1, dh1b, g_od_norm = rmsnorm_bwd(h1, p["od_norm_w"], dn1, dh2, name="l1_norm_bwd")
    w1_pad = (-w1_rows) % 16
    l1_chip = _chip_sums([jnp.pad(g_od_w_in_t.reshape(N_DEV, w1_rows, D), ((0, 0), (0, w1_pad), (0, 0))),
                          g_od_w_out.reshape(N_DEV, -1, D)], "l1")

    g_ev_w_out = mm_tn(ycat, dh1b, out_dtype=BF16, name="l0_dw_out")
    dycat = mm_nt_terms([(dh1b, 0, D, 0)], p["ev_w_out"], out_dtype=BF16, name="l0_d_out_proj")
    du2, dgc, dga, do, g_ln_w, g_ln_b = mix0_post_bwd(dycat, u2, proj0, o, p["ev_ln_w"], p["ev_ln_b"], CW=CW,
                                                      gc_off=gc_off, ga_off=ga_off, name="l0_post_bwd")
    dq, dk, dv, (r_od_in_t, r_od_out) = sba_bwd(proj0, ctot, do, B=B, seq=seq, heads=heads, q_off=q_off, k_off=k_off,
                                                v_off=v_off, name="l0_attn_bwd", comm=(l1_chip, ["chips", "chips"]))
    dga_a, dga_b, g_dw_w, g_dw_b = dwconv_bwd(du2, None, proj0, (0, CW), p["ev_dw_w"], C=CW, seq=seq, glu=True,
                                              silu_out=False, name="l0_conv_bwd")
    pieces = [dga_a, dga_b, dgc, dq, dk, dv, dga]
    g_ev_w_in = jnp.concatenate([mm_tn(n0, pc, out_dtype=BF16, name=f"l0_dw_in_{j}") for j, pc in enumerate(pieces)],
                                axis=1)
    l0_chip = _chip_sums([_col_shards(g_ev_w_in), g_ev_w_out.reshape(N_DEV, -1, D)], "l0")
    dn0, (r_ev_in, r_ev_out) = mm_nt_terms([(pc, 0, CW, j * CW) for j, pc in enumerate(pieces)], p["ev_w_in"],
                                           out_dtype=BF16, name="l0_d_in_proj", comm=(l0_chip, ["chips", "chips"]))
    dx, _, g_ev_norm = rmsnorm_bwd(x, p["ev_norm_w"], dn0, dh1, name="l0_norm_bwd")

    small = dict(ev_norm_w=g_ev_norm, ev_dw_w=g_dw_w, ev_dw_b=g_dw_b, ev_ln_w=g_ln_w, ev_ln_b=g_ln_b,
                 od_norm_w=g_od_norm, od_conv_w=g_conv_w, od_conv_b=g_conv_b, od_dt_bias=g_bias[:, :H],
                 od_a_log=g_alog[:, :H], od_d=g_d, od_gnorm_w=g_gnorm, final_norm_w=g_final)
    received = dict(ev_w_in=r_ev_in, ev_w_out=r_ev_out, od_w_in=r_od_in_t, od_w_out=r_od_out)
    return loss, dx, small, received


BIG = ("ev_w_in", "ev_w_out", "od_w_in", "od_w_out")
SMALL = ("ev_norm_w", "ev_dw_w", "ev_dw_b", "ev_ln_w", "ev_ln_b", "od_norm_w", "od_conv_w", "od_conv_b",
         "od_dt_bias", "od_a_log", "od_d", "od_gnorm_w", "final_norm_w")
SMALL_SHARDED = ("ev_dw_w", "od_norm_w", "od_conv_w", "od_conv_b", "od_gnorm_w")
ORDER = ("ev_norm_w", "ev_w_in", "ev_dw_w", "ev_dw_b", "ev_ln_w", "ev_ln_b", "ev_w_out", "od_norm_w", "od_w_in",
         "od_conv_w", "od_conv_b", "od_dt_bias", "od_a_log", "od_d", "od_gnorm_w", "od_w_out", "final_norm_w")


def _pack_rows(arrs, width, row_align):
    parts, spans, r0 = [], [], 0
    for a in arrs:
        flat = a.reshape(-1)
        rows = -(-flat.shape[0] // (width * row_align)) * row_align
        parts.append(jnp.pad(flat, (0, rows * width - flat.shape[0])).reshape(rows, width))
        spans.append((r0, a.size, a.shape))
        r0 += rows
    return jnp.concatenate(parts, axis=0), spans


def _unpack_rows(packed, spans):
    lead = packed.shape[:-2]
    width = packed.shape[-1]
    out = []
    for r0, size, shape in spans:
        rows = -(-size // width)
        blk = packed[..., r0:r0 + rows, :].reshape(lead + (rows * width,))[..., :size]
        out.append(blk.reshape(lead + tuple(shape)))
    return out


def _col_shards(a):
    R, C8 = a.shape
    return a.reshape(R, N_DEV, C8 // N_DEV).transpose(1, 0, 2)


def _col_unshards(a):
    n, R, C = a.shape
    return a.transpose(1, 0, 2).reshape(R, n * C)


def kernel(x, ev_norm_w, ev_w_in, ev_dw_w, ev_dw_b, ev_ln_w, ev_ln_b, ev_w_out, od_norm_w, od_w_in, od_conv_w, od_conv_b, od_dt_bias, od_a_log, od_d, od_gnorm_w, od_w_out, final_norm_w, loss_target, m_ev_norm_w, m_ev_w_in, m_ev_dw_w, m_ev_dw_b, m_ev_ln_w, m_ev_ln_b, m_ev_w_out, m_od_norm_w, m_od_w_in, m_od_conv_w, m_od_conv_b, m_od_dt_bias, m_od_a_log, m_od_d, m_od_gnorm_w, m_od_w_out, m_final_norm_w, v_ev_norm_w, v_ev_w_in, v_ev_dw_w, v_ev_dw_b, v_ev_ln_w, v_ev_ln_b, v_ev_w_out, v_od_norm_w, v_od_w_in, v_od_conv_w, v_od_conv_b, v_od_dt_bias, v_od_a_log, v_od_d, v_od_gnorm_w, v_od_w_out, v_final_norm_w):
    loc = dict(ev_norm_w=ev_norm_w, ev_w_in=ev_w_in, ev_dw_w=ev_dw_w, ev_dw_b=ev_dw_b, ev_ln_w=ev_ln_w,
               ev_ln_b=ev_ln_b, ev_w_out=ev_w_out, od_norm_w=od_norm_w, od_w_in=od_w_in, od_conv_w=od_conv_w,
               od_conv_b=od_conv_b, od_dt_bias=od_dt_bias, od_a_log=od_a_log, od_d=od_d, od_gnorm_w=od_gnorm_w,
               od_w_out=od_w_out, final_norm_w=final_norm_w)
    mom = dict(ev_norm_w=m_ev_norm_w, ev_w_in=m_ev_w_in, ev_dw_w=m_ev_dw_w, ev_dw_b=m_ev_dw_b, ev_ln_w=m_ev_ln_w,
               ev_ln_b=m_ev_ln_b, ev_w_out=m_ev_w_out, od_norm_w=m_od_norm_w, od_w_in=m_od_w_in,
               od_conv_w=m_od_conv_w, od_conv_b=m_od_conv_b, od_dt_bias=m_od_dt_bias, od_a_log=m_od_a_log,
               od_d=m_od_d, od_gnorm_w=m_od_gnorm_w, od_w_out=m_od_w_out, final_norm_w=m_final_norm_w)
    var = dict(ev_norm_w=v_ev_norm_w, ev_w_in=v_ev_w_in, ev_dw_w=v_ev_dw_w, ev_dw_b=v_ev_dw_b, ev_ln_w=v_ev_ln_w,
               ev_ln_b=v_ev_ln_b, ev_w_out=v_ev_w_out, od_norm_w=v_od_norm_w, od_w_in=v_od_w_in,
               od_conv_w=v_od_conv_w, od_conv_b=v_od_conv_b, od_dt_bias=v_od_dt_bias, od_a_log=v_od_a_log,
               od_d=v_od_d, od_gnorm_w=v_od_gnorm_w, od_w_out=v_od_w_out, final_norm_w=v_final_norm_w)
    shapes = {n: loc[n].shape for n in ORDER}
    loc = {n: (a.reshape(1, -1) if a.ndim == 1 else a.reshape(a.shape[-2:]) if a.ndim == 3 else a)
           for n, a in loc.items()}
    mom = {n: a.reshape(loc[n].shape) for n, a in mom.items()}
    var = {n: a.reshape(loc[n].shape) for n, a in var.items()}

    B, seq, D = x.shape
    me = 4 * lax.axis_index("x") + 2 * lax.axis_index("y") + lax.axis_index("c")

    w1_rows = loc["od_w_in"].shape[1]
    w1_pad = (-w1_rows) % 16

    def to_t(a):
        return jnp.pad(a.T, ((0, w1_pad), (0, 0)))

    loss, dx, grads, received = local_step(x.reshape(B * seq, D), loss_target.reshape(B * seq, D), loc,
                                           to_t(loc["od_w_in"]), B=B, seq=seq)

    gsmall_packed, gsmall_spans = _pack_rows([grads[n] for n in SMALL], LANE, 8)
    (gsmall_recv,) = exchange([gsmall_packed], ["gather_direct"], name="gather_small_grads")

    big_out = [{} for _ in range(4)]
    for n in ("ev_w_in", "ev_w_out", "od_w_out"):
        for kind, a in enumerate(adamw(received[n], loc[n], mom[n], var[n], name="adamw_" + n)):
            big_out[kind][n] = a
    for kind, a in enumerate(adamw(received["od_w_in"], to_t(loc["od_w_in"]), to_t(mom["od_w_in"]),
                                   to_t(var["od_w_in"]), name="adamw_od_w_in")):
        big_out[kind]["od_w_in"] = a[:w1_rows].T

    gsmall = dict(zip(SMALL, _unpack_rows(sum_slots(gsmall_recv, name="sum_small_grads"), gsmall_spans)))
    for n in SMALL_SHARDED:
        width = loc[n].shape[1]
        gsmall[n] = lax.dynamic_slice_in_dim(gsmall[n], me * width, width, axis=1)
    gs, sspans = _pack_rows([gsmall[n] for n in SMALL], LANE, 8)
    ws, _ = _pack_rows([loc[n] for n in SMALL], LANE, 8)
    ms, _ = _pack_rows([mom[n] for n in SMALL], LANE, 8)
    vs, _ = _pack_rows([var[n] for n in SMALL], LANE, 8)
    small_out = [dict(zip(SMALL, _unpack_rows(a, sspans))) for a in adamw(gs[None], ws, ms, vs, name="adamw_small")]

    outs = [lax.psum(loss[0, 0], ("x", "y", "c")), dx.reshape(B, seq, D)]
    for kind in range(4):
        for n in ORDER:
            src = big_out[kind] if n in BIG else small_out[kind]
            outs.append(src[n].reshape(shapes[n]))
    return tuple(outs)
```

```python
import jax
import jax.numpy as jnp
from jax import lax
from jax.experimental import pallas as pl
from jax.experimental.pallas import tpu as pltpu

F32 = jnp.float32
BF16 = jnp.bfloat16

EPS = 1e-6
N_DEV = 8
LANE = 128
VMEM_LIMIT_BYTES = 48 * 1024 * 1024

SB_HEAD_DIM = 128
SSM_HEAD_DIM = 64
SSM_GROUPS = 4
SSM_STATE = 128
SSM_CHUNK = 128
HALO = 32
HEAD_ROWS = 8
NEG_BIG = -1e30

ADAM_LR = 0.001
ADAM_B1 = 0.9
ADAM_B2 = 0.999
ADAM_EPS = 1e-08
ADAM_WD = 0.01
ADAM_STEP = 10

NT = (((1,), (1,)), ((), ()))
TN = (((0,), (0,)), ((), ()))


def _cp(*sem):
    return pltpu.CompilerParams(dimension_semantics=sem, vmem_limit_bytes=VMEM_LIMIT_BYTES)


def _pick(n, cap, align):
    if n <= cap:
        return n
    t = (cap // align) * align
    while t >= align:
        if n % t == 0:
            return t
        t -= align
    raise ValueError(f"no tile for {n} (cap {cap}, align {align})")


def _sigmoid(x):
    return 0.5 * jnp.tanh(0.5 * x) + 0.5


def _silu(x):
    return x * _sigmoid(x)


def _silu_grad(x):
    s = _sigmoid(x)
    return s * (1.0 + x * (1.0 - s))


def _dot(a, b, dims=None):
    if dims is None:
        return jnp.dot(a, b, preferred_element_type=F32)
    return lax.dot_general(a, b, dims, preferred_element_type=F32)


def _tri_dot3(tri, x):
    hi = x.astype(BF16)
    r1 = x - hi.astype(F32)
    mid = r1.astype(BF16)
    lo = (r1 - mid.astype(F32)).astype(BF16)
    return _dot(tri, hi) + _dot(tri, mid) + _dot(tri, lo)


def _grid_step(n_inner):
    return lambda: pl.program_id(0) * n_inner + pl.program_id(1)


def mm_nn(a, b, *, add=None, out_dtype, name, comm=None):
    M, K = a.shape
    N = b.shape[1]
    tm = _pick(M, 2048 if K <= 1024 and add is None else 1024, 16)
    tn = _pick(N, 1024, LANE)

    def body(*refs):
        if add is None:
            a_ref, b_ref, o_ref = refs
        else:
            a_ref, b_ref, add_ref, o_ref = refs
        acc = _dot(a_ref[...], b_ref[...])
        if add is not None:
            acc = acc + add_ref[...]
        o_ref[...] = acc.astype(out_dtype)

    in_specs = [pl.BlockSpec((tm, K), lambda i, j: (i, 0)), pl.BlockSpec((K, tn), lambda i, j: (0, j))]
    args = [a, b]
    if add is not None:
        in_specs.append(pl.BlockSpec((tm, tn), lambda i, j: (i, j)))
        args.append(add)
    grid = (M // tm, N // tn)
    body, c_in, c_args, c_out, c_shape, c_scratch = _hosted(body, len(args), 1, comm, _grid_step(grid[1]),
                                                            grid[0] * grid[1])
    out = pl.pallas_call(
        body, name=name, grid=grid, in_specs=in_specs + c_in,
        out_specs=[pl.BlockSpec((tm, tn), lambda i, j: (i, j))] + c_out,
        out_shape=[jax.ShapeDtypeStruct((M, N), out_dtype)] + c_shape, scratch_shapes=c_scratch,
        compiler_params=_cp(*(("arbitrary",) * 2 if comm else ("parallel",) * 2)))(*args, *c_args)
    return (out[0], out[1:]) if comm else out[0]


def mm_nt_terms(terms, b, *, out_dtype, name, comm=None):
    M = terms[0][0].shape[0]
    N = b.shape[0]
    n_terms = len(terms)
    cap = 1024 if n_terms == 1 else 512
    tm = _pick(M, 2 * cap if n_terms == 1 else cap, 16)
    tn = _pick(N, cap, LANE)

    def body(*refs):
        o_ref = refs[-1]
        acc = None
        for t in range(n_terms):
            part = _dot(refs[2 * t][...], refs[2 * t + 1][...], NT)
            acc = part if acc is None else acc + part
        o_ref[...] = acc.astype(out_dtype)

    in_specs, args = [], []
    for arr, cb, w, off in terms:
        assert off % w == 0
        in_specs.append(pl.BlockSpec((tm, w), lambda i, j, cb=cb: (i, cb)))
        in_specs.append(pl.BlockSpec((tn, w), lambda i, j, ob=off // w: (j, ob)))
        args += [arr, b]
    grid = (M // tm, N // tn)
    body, c_in, c_args, c_out, c_shape, c_scratch = _hosted(body, len(args), 1, comm, _grid_step(grid[1]),
                                                            grid[0] * grid[1])
    out = pl.pallas_call(
        body, name=name, grid=grid, in_specs=in_specs + c_in,
        out_specs=[pl.BlockSpec((tm, tn), lambda i, j: (i, j))] + c_out,
        out_shape=[jax.ShapeDtypeStruct((M, N), out_dtype)] + c_shape, scratch_shapes=c_scratch,
        compiler_params=_cp(*(("arbitrary",) * 2 if comm else ("parallel",) * 2)))(*args, *c_args)
    return (out[0], out[1:]) if comm else out[0]


def mm_nn_terms(terms, b, *, out_dtype, name):
    M = terms[0][0].shape[0]
    N = b.shape[1]
    tm = _pick(M, 512, 16)
    tn = _pick(N, 512, LANE)
    n_terms = len(terms)

    def body(*refs):
        o_ref = refs[-1]
        acc = None
        for t in range(n_terms):
            part = _dot(refs[2 * t][...], refs[2 * t + 1][...])
            acc = part if acc is None else acc + part
        o_ref[...] = acc.astype(out_dtype)

    in_specs, args = [], []
    for arr, cb, w, off in terms:
        assert off % w == 0
        in_specs.append(pl.BlockSpec((tm, w), lambda i, j, cb=cb: (i, cb)))
        in_specs.append(pl.BlockSpec((w, tn), lambda i, j, ob=off // w: (ob, j)))
        args += [arr, b]
    return pl.pallas_call(
        body, name=name, grid=(M // tm, N // tn), in_specs=in_specs,
        out_specs=pl.BlockSpec((tm, tn), lambda i, j: (i, j)),
        out_shape=jax.ShapeDtypeStruct((M, N), out_dtype),
        compiler_params=_cp("parallel", "parallel"))(*args)


def mm_tn(a, b, *, out_dtype, name):
    T, M = a.shape
    N = b.shape[1]
    tm = _pick(M, 1024, LANE)
    tn = _pick(N, 1024, LANE)
    tk = _pick(T, 2048, 16)
    nk = T // tk

    def body(a_ref, b_ref, o_ref, acc_ref):
        k = pl.program_id(2)

        @pl.when(k == 0)
        def _():
            acc_ref[...] = jnp.zeros_like(acc_ref)

        acc_ref[...] += _dot(a_ref[...], b_ref[...], TN)

        @pl.when(k == nk - 1)
        def _():
            o_ref[...] = acc_ref[...].astype(out_dtype)

    return pl.pallas_call(
        body, name=name, grid=(M // tm, N // tn, nk),
        in_specs=[pl.BlockSpec((tk, tm), lambda i, j, k: (k, i)), pl.BlockSpec((tk, tn), lambda i, j, k: (k, j))],
        out_specs=pl.BlockSpec((tm, tn), lambda i, j, k: (i, j)),
        out_shape=jax.ShapeDtypeStruct((M, N), out_dtype),
        scratch_shapes=[pltpu.VMEM((tm, tn), F32)],
        compiler_params=_cp("parallel", "parallel", "arbitrary"))(a, b)


def rmsnorm_fwd(h, w, *, name, comm=None):
    T, D = h.shape
    tt = _pick(T, 512, 16)

    def body(h_ref, w_ref, n_ref):
        x = h_ref[...]
        r = lax.rsqrt(jnp.mean(x * x, axis=-1, keepdims=True) + EPS)
        n_ref[...] = (x * r * w_ref[...]).astype(BF16)

    body, c_in, c_args, c_out, c_shape, c_scratch = _hosted(body, 2, 1, comm, lambda: pl.program_id(0), T // tt)
    out = pl.pallas_call(
        body, name=name, grid=(T // tt,),
        in_specs=[pl.BlockSpec((tt, D), lambda i: (i, 0)), pl.BlockSpec((1, D), lambda i: (0, 0))] + c_in,
        out_specs=[pl.BlockSpec((tt, D), lambda i: (i, 0))] + c_out,
        out_shape=[jax.ShapeDtypeStruct((T, D), BF16)] + c_shape, scratch_shapes=c_scratch,
        compiler_params=_cp("arbitrary" if comm else "parallel"))(h, w, *c_args)
    return (out[0], out[1:]) if comm else out[0]


def rmsnorm_bwd(h, w, dn, dres, *, name):
    T, D = h.shape
    tt = _pick(T, 512, 16)

    def body(h_ref, w_ref, dn_ref, dres_ref, dh_ref, dhb_ref, gw_ref):
        @pl.when(pl.program_id(0) == 0)
        def _():
            gw_ref[...] = jnp.zeros_like(gw_ref)

        x = h_ref[...]
        r = lax.rsqrt(jnp.mean(x * x, axis=-1, keepdims=True) + EPS)
        xhat = x * r
        g = dn_ref[...].astype(F32)
        gw_ref[...] += jnp.sum(g * xhat, axis=0, keepdims=True)
        dxh = g * w_ref[...]
        dx = r * (dxh - xhat * jnp.mean(dxh * xhat, axis=-1, keepdims=True))
        dh = dres_ref[...] + dx
        dh_ref[...] = dh
        dhb_ref[...] = dh.astype(BF16)

    row = pl.BlockSpec((tt, D), lambda i: (i, 0))
    vec = pl.BlockSpec((1, D), lambda i: (0, 0))
    return pl.pallas_call(
        body, name=name, grid=(T // tt,), in_specs=[row, vec, row, row], out_specs=[row, row, vec],
        out_shape=[jax.ShapeDtypeStruct((T, D), F32), jax.ShapeDtypeStruct((T, D), BF16),
                   jax.ShapeDtypeStruct((1, D), F32)],
        compiler_params=_cp("arbitrary"))(h, w, dn, dres)


def final_loss(h, w, target, *, name):
    T, D = h.shape
    tt = _pick(T, 512, 16)

    def body(h_ref, w_ref, t_ref, loss_ref, dh_ref, dhb_ref, gw_ref):
        @pl.when(pl.program_id(0) == 0)
        def _():
            gw_ref[...] = jnp.zeros_like(gw_ref)
            loss_ref[...] = jnp.zeros_like(loss_ref)

        x = h_ref[...]
        r = lax.rsqrt(jnp.mean(x * x, axis=-1, keepdims=True) + EPS)
        xhat = x * r
        e = xhat * w_ref[...] - t_ref[...]
        loss_ref[...] += jnp.sum(e * e) * (0.5 / D)
        g = e * (1.0 / D)
        gw_ref[...] += jnp.sum(g * xhat, axis=0, keepdims=True)
        dxh = g * w_ref[...]
        dh = r * (dxh - xhat * jnp.mean(dxh * xhat, axis=-1, keepdims=True))
        dh_ref[...] = dh
        dhb_ref[...] = dh.astype(BF16)

    row = pl.BlockSpec((tt, D), lambda i: (i, 0))
    vec = pl.BlockSpec((1, D), lambda i: (0, 0))
    one = pl.BlockSpec((1, LANE), lambda i: (0, 0))
    return pl.pallas_call(
        body, name=name, grid=(T // tt,), in_specs=[row, vec, row], out_specs=[one, row, row, vec],
        out_shape=[jax.ShapeDtypeStruct((1, LANE), F32), jax.ShapeDtypeStruct((T, D), F32),
                   jax.ShapeDtypeStruct((T, D), BF16), jax.ShapeDtypeStruct((1, D), F32)],
        compiler_params=_cp("arbitrary"))(h, w, target)


CONV_CHUNK = 32
ROW_CHUNK = 16
SUBLANES = 8


def _conv_tiles(seq, C):
    return _pick(seq, 512, HALO), _pick(C, 512, LANE)


def _residues(offsets):
    return sorted({s % SUBLANES for s in offsets} - {0})


def _fill_shifted(buf, shifted, residues):
    n = buf.shape[0] - SUBLANES
    for i, r in enumerate(residues):
        shifted[i, 0:n, :] = buf[r:r + n, :]


def _tap(buf, shifted, residues, offset, start, rows):
    r = offset % SUBLANES
    base = offset - r
    ref = buf if r == 0 else shifted.at[residues.index(r)]
    return ref[pl.ds(start + base, rows), :]


def dwconv_fwd(src, offs, w, b, *, C, seq, glu, silu_out, name):
    T = src.shape[0]
    K = w.shape[0]
    assert K - 1 <= HALO
    tt, tc = _conv_tiles(seq, C)
    n_in = 2 if glu else 1
    per = tt // HALO
    offsets = [HALO - (K - 1) + k for k in range(K)]
    residues = _residues(offsets)

    def body(*refs):
        cur = refs[0:2 * n_in:2]
        halo = refs[1:2 * n_in:2]
        w_ref, b_ref = refs[2 * n_in], refs[2 * n_in + 1]
        outs = refs[2 * n_in + 2:-2]
        buf, shifted = refs[-2], refs[-1]
        i = pl.program_id(1)
        first = (i * tt) % seq == 0

        def pre(rs):
            v = rs[0][...].astype(F32)
            return v * _sigmoid(rs[1][...].astype(F32)) if glu else v

        buf[0:HALO, :] = jnp.where(first, 0.0, pre(halo))
        buf[HALO:HALO + tt, :] = pre(cur)
        _fill_shifted(buf, shifted, residues)

        def chunk(ci, carry):
            start = pl.multiple_of(ci * CONV_CHUNK, CONV_CHUNK)
            acc = jnp.broadcast_to(b_ref[...], (CONV_CHUNK, tc))
            for k in range(K):
                acc = acc + w_ref[k:k + 1, :] * _tap(buf, shifted, residues, offsets[k], start, CONV_CHUNK)
            outs[0][pl.ds(start, CONV_CHUNK), :] = acc.astype(BF16)
            if silu_out:
                outs[1][pl.ds(start, CONV_CHUNK), :] = _silu(acc).astype(BF16)
            return carry

        lax.fori_loop(0, tt // CONV_CHUNK, chunk, 0)

    in_specs, args = [], []
    for off in offs:
        assert off % tc == 0
        in_specs.append(pl.BlockSpec((tt, tc), lambda j, i, ob=off // tc: (i, ob + j)))
        in_specs.append(pl.BlockSpec((HALO, tc), lambda j, i, ob=off // tc: (jnp.maximum(i * per - 1, 0), ob + j)))
        args += [src, src]
    in_specs += [pl.BlockSpec((K, tc), lambda j, i: (0, j)), pl.BlockSpec((1, tc), lambda j, i: (0, j))]
    args += [w, b]
    n_out = 2 if silu_out else 1
    out = pl.pallas_call(
        body, name=name, grid=(C // tc, T // tt), in_specs=in_specs,
        out_specs=[pl.BlockSpec((tt, tc), lambda j, i: (i, j))] * n_out,
        out_shape=[jax.ShapeDtypeStruct((T, C), BF16)] * n_out,
        scratch_shapes=[pltpu.VMEM((HALO + tt, tc), F32), pltpu.VMEM((max(len(residues), 1), HALO + tt, tc), F32)],
        compiler_params=_cp("parallel", "arbitrary"))(*args)
    return out


def dwconv_bwd(du, u, src, offs, w, *, C, seq, glu, silu_out, name):
    T = src.shape[0]
    K = w.shape[0]
    assert K - 1 <= HALO
    tt, tc = _conv_tiles(seq, C)
    n_in = 2 if glu else 1
    per = tt // HALO
    last_blk = T // HALO - 1
    g_offsets = [K - 1 - k for k in range(K)]
    g_res = _residues(g_offsets)

    def body(*refs):
        pos = 0
        du_cur, du_nxt = refs[0], refs[1]
        pos = 2
        if silu_out:
            u_cur, u_nxt = refs[2], refs[3]
            pos = 4
        cur = refs[pos:pos + n_in]
        pos += n_in
        w_ref = refs[pos]
        outs = refs[pos + 1:pos + 1 + n_in]
        dw_ref, db_ref = refs[pos + 1 + n_in], refs[pos + 2 + n_in]
        gbuf, gshift, dw_acc, db_acc = refs[-4:]
        i = pl.program_id(1)
        last = ((i + 1) * tt) % seq == 0

        @pl.when(i == 0)
        def _():
            dw_acc[...] = jnp.zeros_like(dw_acc)
            db_acc[...] = jnp.zeros_like(db_acc)

        def build(ci, carry):
            rows = pl.ds(pl.multiple_of(ci * CONV_CHUNK, CONV_CHUNK), CONV_CHUNK)
            g = du_cur[rows, :].astype(F32)
            if silu_out:
                g = g * _silu_grad(u_cur[rows, :].astype(F32))
            gbuf[rows, :] = g
            return carry

        lax.fori_loop(0, tt // CONV_CHUNK, build, 0, unroll=2)
        g_nxt = du_nxt[...].astype(F32)
        if silu_out:
            g_nxt = g_nxt * _silu_grad(u_nxt[...].astype(F32))
        gbuf[tt:tt + HALO, :] = jnp.where(last, 0.0, g_nxt)
        _fill_shifted(gbuf, gshift, g_res)

        def fold(v):
            out = v[0:SUBLANES]
            for s in range(SUBLANES, CONV_CHUNK, SUBLANES):
                out = out + v[s:s + SUBLANES]
            return out

        def chunk(ci, carry):
            start = pl.multiple_of(ci * CONV_CHUNK, CONV_CHUNK)
            rows = pl.ds(start, CONV_CHUNK)
            a = cur[0][rows, :].astype(F32)
            if glu:
                s = _sigmoid(cur[1][rows, :].astype(F32))
                x_in = a * s
            else:
                x_in = a
            dx = jnp.zeros((CONV_CHUNK, tc), F32)
            for k in range(K):
                g_k = _tap(gbuf, gshift, g_res, g_offsets[k], start, CONV_CHUNK)
                dx = dx + w_ref[k:k + 1, :] * g_k
                dw_acc[k * SUBLANES:(k + 1) * SUBLANES, :] += fold(g_k * x_in)
            db_acc[...] += fold(gbuf[rows, :])
            if glu:
                outs[0][rows, :] = (dx * s).astype(BF16)
                outs[1][rows, :] = (dx * a * s * (1.0 - s)).astype(BF16)
            else:
                outs[0][rows, :] = dx.astype(BF16)
            return carry

        lax.fori_loop(0, tt // CONV_CHUNK, chunk, 0)

        @pl.when(i == T // tt - 1)
        def _():
            for k in range(K):
                dw_ref[k:k + 1, :] = jnp.sum(dw_acc[k * SUBLANES:(k + 1) * SUBLANES, :], axis=0, keepdims=True)
            db_ref[...] = jnp.sum(db_acc[...], axis=0, keepdims=True)

    def cur_spec(ob):
        return pl.BlockSpec((tt, tc), lambda j, i: (i, ob + j))

    def nxt_spec(ob):
        return pl.BlockSpec((HALO, tc), lambda j, i: (jnp.minimum((i + 1) * per, last_blk), ob + j))

    in_specs = [cur_spec(0), nxt_spec(0)]
    args = [du, du]
    if silu_out:
        in_specs += [cur_spec(0), nxt_spec(0)]
        args += [u, u]
    for off in offs:
        assert off % tc == 0
        in_specs.append(cur_spec(off // tc))
        args.append(src)
    in_specs.append(pl.BlockSpec((K, tc), lambda j, i: (0, j)))
    args.append(w)
    out_specs = [pl.BlockSpec((tt, tc), lambda j, i: (i, j))] * n_in
    out_specs += [pl.BlockSpec((K, tc), lambda j, i: (0, j)), pl.BlockSpec((1, tc), lambda j, i: (0, j))]
    out_shape = [jax.ShapeDtypeStruct((T, C), BF16)] * n_in
    out_shape += [jax.ShapeDtypeStruct((K, C), F32), jax.ShapeDtypeStruct((1, C), F32)]
    return pl.pallas_call(
        body, name=name, grid=(C // tc, T // tt), in_specs=in_specs, out_specs=out_specs, out_shape=out_shape,
        scratch_shapes=[pltpu.VMEM((tt + HALO, tc), F32), pltpu.VMEM((max(len(g_res), 1), tt + HALO, tc), F32),
                        pltpu.VMEM((K * SUBLANES, tc), F32), pltpu.VMEM((SUBLANES, tc), F32)],
        compiler_params=_cp("parallel", "arbitrary"))(*args)


def mix0_post_fwd(u2, proj, o, ln_w, ln_b, *, CW, gc_off, ga_off, name):
    T = u2.shape[0]
    tt = _pick(T, 256, 16)

    def body(u_ref, gc_ref, ga_ref, o_ref, lw_ref, lb_ref, y_ref):
        def chunk(ci, carry):
            rows = pl.ds(pl.multiple_of(ci * ROW_CHUNK, ROW_CHUNK), ROW_CHUNK)
            u = u_ref[rows, :].astype(F32)
            mu = jnp.mean(u, axis=-1, keepdims=True)
            xc = u - mu
            r = lax.rsqrt(jnp.mean(xc * xc, axis=-1, keepdims=True) + EPS)
            u3 = xc * r * lw_ref[...] + lb_ref[...]
            y_ref[rows, 0:CW] = (_silu(u3) * _silu(gc_ref[rows, :].astype(F32))).astype(BF16)
            y_ref[rows, CW:2 * CW] = (o_ref[rows, :].astype(F32) * _silu(ga_ref[rows, :].astype(F32))).astype(BF16)
            return carry

        lax.fori_loop(0, tt // ROW_CHUNK, chunk, 0, unroll=4)

    row = pl.BlockSpec((tt, CW), lambda i: (i, 0))
    vec = pl.BlockSpec((1, CW), lambda i: (0, 0))
    return pl.pallas_call(
        body, name=name, grid=(T // tt,),
        in_specs=[row, pl.BlockSpec((tt, CW), lambda i: (i, gc_off // CW)),
                  pl.BlockSpec((tt, CW), lambda i: (i, ga_off // CW)), row, vec, vec],
        out_specs=pl.BlockSpec((tt, 2 * CW), lambda i: (i, 0)),
        out_shape=jax.ShapeDtypeStruct((T, 2 * CW), BF16),
        compiler_params=_cp("parallel"))(u2, proj, proj, o, ln_w, ln_b)


def mix0_post_bwd(dy, u2, proj, o, ln_w, ln_b, *, CW, gc_off, ga_off, name):
    T = u2.shape[0]
    tt = _pick(T, 256, 16)

    def body(dy_ref, u_ref, gc_ref, ga_ref, o_ref, lw_ref, lb_ref, du_ref, dgc_ref, dga_ref, do_ref, dlw_ref, dlb_ref,
             lw_acc, lb_acc):
        i = pl.program_id(0)

        @pl.when(i == 0)
        def _():
            lw_acc[...] = jnp.zeros_like(lw_acc)
            lb_acc[...] = jnp.zeros_like(lb_acc)

        def fold(v):
            out = v[0:SUBLANES]
            for s in range(SUBLANES, ROW_CHUNK, SUBLANES):
                out = out + v[s:s + SUBLANES]
            return out

        def chunk(ci, carry):
            rows = pl.ds(pl.multiple_of(ci * ROW_CHUNK, ROW_CHUNK), ROW_CHUNK)
            dyc = dy_ref[rows, 0:CW].astype(F32)
            dya = dy_ref[rows, CW:2 * CW].astype(F32)
            u = u_ref[rows, :].astype(F32)
            mu = jnp.mean(u, axis=-1, keepdims=True)
            xc = u - mu
            r = lax.rsqrt(jnp.mean(xc * xc, axis=-1, keepdims=True) + EPS)
            xhat = xc * r
            u3 = xhat * lw_ref[...] + lb_ref[...]
            gc = gc_ref[rows, :].astype(F32)
            dgc_ref[rows, :] = (dyc * _silu(u3) * _silu_grad(gc)).astype(BF16)
            du3 = dyc * _silu(gc) * _silu_grad(u3)
            lw_acc[...] += fold(du3 * xhat)
            lb_acc[...] += fold(du3)
            dxh = du3 * lw_ref[...]
            du = r * (dxh - jnp.mean(dxh, axis=-1, keepdims=True)
                      - xhat * jnp.mean(dxh * xhat, axis=-1, keepdims=True))
            du_ref[rows, :] = du.astype(BF16)
            ga = ga_ref[rows, :].astype(F32)
            ov = o_ref[rows, :].astype(F32)
            do_ref[rows, :] = (dya * _silu(ga)).astype(BF16)
            dga_ref[rows, :] = (dya * ov * _silu_grad(ga)).astype(BF16)
            return carry

        lax.fori_loop(0, tt // ROW_CHUNK, chunk, 0, unroll=4)

        @pl.when(i == T // tt - 1)
        def _():
            dlw_ref[...] = jnp.sum(lw_acc[...], axis=0, keepdims=True)
            dlb_ref[...] = jnp.sum(lb_acc[...], axis=0, keepdims=True)

    row = pl.BlockSpec((tt, CW), lambda i: (i, 0))
    vec = pl.BlockSpec((1, CW), lambda i: (0, 0))
    big = jax.ShapeDtypeStruct((T, CW), BF16)
    small = jax.ShapeDtypeStruct((1, CW), F32)
    return pl.pallas_call(
        body, name=name, grid=(T // tt,),
        in_specs=[pl.BlockSpec((tt, 2 * CW), lambda i: (i, 0)), row,
                  pl.BlockSpec((tt, CW), lambda i: (i, gc_off // CW)),
                  pl.BlockSpec((tt, CW), lambda i: (i, ga_off // CW)), row, vec, vec],
        out_specs=[row, row, row, row, vec, vec],
        out_shape=[big, big, big, big, small, small],
        scratch_shapes=[pltpu.VMEM((SUBLANES, CW), F32), pltpu.VMEM((SUBLANES, CW), F32)],
        compiler_params=_cp("arbitrary"))(dy, u2, proj, proj, o, ln_w, ln_b)


SB_UNDERFLOW = 110.0
SB_BOUND_MARGIN = 1.02


def _sb_tile(seq):
    return _pick(seq, 256, LANE)


def _softplus(z):
    return jnp.maximum(z, 0.0) + jnp.log(1.0 + jnp.exp(-jnp.abs(z)))


def _tri01(n, lower):
    i = lax.broadcasted_iota(jnp.int32, (n, n), 0)
    j = lax.broadcasted_iota(jnp.int32, (n, n), 1)
    return ((i >= j) if lower else (i <= j)).astype(BF16)


SB_HEADS_FWD = 4
SB_HEADS_BWD = 2


def _sb_heads_per_step(heads, want):
    while heads % want:
        want //= 2
    return want


def sba_fwd(proj, *, B, seq, heads, q_off, k_off, v_off, name):
    dh = SB_HEAD_DIM
    tq = _sb_tile(seq)
    assert tq % (2 * LANE) == 0
    nq = seq // tq
    hps = _sb_heads_per_step(heads, SB_HEADS_FWD)
    hw = hps * dh
    scale = dh ** -0.5

    def body(q_ref, k_ref, v_ref, tri_ref, o_ref, ct_ref, acc_ref, kmax_ref):
        qi = pl.program_id(1)
        tri = tri_ref[...]
        qs = [(q_ref[:, h * dh:(h + 1) * dh].astype(F32) * scale).astype(BF16) for h in range(hps)]

        @pl.when(qi == 0)
        def _():
            def chunk(i, best):
                rows = k_ref[pl.ds(pl.multiple_of(i * tq, tq), tq), :].astype(F32)
                sq = rows * rows
                return tuple(jnp.maximum(best[h], jnp.max(jnp.sum(sq[:, h * dh:(h + 1) * dh], axis=1, keepdims=True),
                                                          axis=0, keepdims=True)) for h in range(hps))

            best = lax.fori_loop(0, nq, chunk, (jnp.zeros((1, 1), F32),) * hps)
            for h in range(hps):
                kmax_ref[h] = jnp.broadcast_to(jnp.sqrt(best[h]), (8, LANE))

        z_bound = [jnp.sqrt(jnp.sum(qs[h].astype(F32) ** 2, axis=1, keepdims=True))
                   * (SB_BOUND_MARGIN * jnp.max(kmax_ref[h], keepdims=True)) for h in range(hps)]

        def part(h, q_rows, start, n_keys, r, mask):
            k_blk = k_ref[pl.ds(start, n_keys), h * dh:(h + 1) * dh]
            v_blk = v_ref[pl.ds(start, n_keys), h * dh:(h + 1) * dh]
            z = _dot(q_rows, k_blk, NT)
            sp = _softplus(z)
            if mask is not None:
                sp = jnp.where(mask, sp, 0.0)
            wts = jnp.exp(z - (_dot(sp.astype(BF16), tri[0:n_keys, 0:n_keys]) + r))
            if mask is not None:
                wts = jnp.where(mask, wts, 0.0)
            return _dot(wts.astype(BF16), v_blk), r + jnp.sum(sp, axis=-1, keepdims=True)

        below = lax.broadcasted_iota(jnp.int32, (tq, tq), 1) < lax.broadcasted_iota(jnp.int32, (tq, tq), 0)
        has_left = qi > 0
        left = pl.multiple_of(jnp.maximum(qi - 1, 0) * tq, tq)
        rs = []
        for h in range(hps):
            pv_d, r = part(h, qs[h], pl.multiple_of(qi * tq, tq), tq, jnp.zeros((tq, 1), F32), below)
            pv_l, r = part(h, qs[h], left, tq, r, has_left)
            acc_ref[:, h * dh:(h + 1) * dh] = pv_d + pv_l
            rs.append(r)
        rs = tuple(rs)

        def block(start, rs):
            pvs, out = [], []
            for h in range(hps):
                pv, r = part(h, qs[h], start, tq, rs[h], None)
                pvs.append(pv)
                out.append(r)
            return pvs, tuple(out)

        def more(c):
            j, rs = c
            slack = rs[0] - z_bound[0]
            for h in range(1, hps):
                slack = jnp.minimum(slack, rs[h] - z_bound[h])
            return jnp.logical_and(j < qi, jnp.min(slack) <= SB_UNDERFLOW)

        def step(c):
            j, rs = c
            pvs, rs = block(pl.multiple_of((qi - 1 - j) * tq, tq), rs)
            for h in range(hps):
                acc_ref[:, h * dh:(h + 1) * dh] += pvs[h]
            return j + 1, rs

        n_left, totals = lax.while_loop(more, step, (has_left.astype(jnp.int32), rs))
        o_ref[...] = acc_ref[...].astype(BF16)
        for h in range(hps):
            ct_ref[0, 0, h, 0:8, :] = jnp.broadcast_to(totals[h], (tq, LANE)).T[0:8, :]
            ct_ref[0, 0, h, 8:16, :] = jnp.full((8, tq), n_left, F32)

    qb, kb, vb = q_off // hw, k_off // hw, v_off // hw
    G = heads // hps
    return pl.pallas_call(
        body, name=name, grid=(B * G, nq),
        in_specs=[pl.BlockSpec((tq, hw), lambda g, i: ((g // G) * nq + i, qb + g % G)),
                  pl.BlockSpec((seq, hw), lambda g, i: (g // G, kb + g % G)),
                  pl.BlockSpec((seq, hw), lambda g, i: (g // G, vb + g % G)),
                  pl.BlockSpec((tq, tq), lambda g, i: (0, 0))],
        out_specs=[pl.BlockSpec((tq, hw), lambda g, i: ((g // G) * nq + i, g % G)),
                   pl.BlockSpec((1, 1, hps, 16, tq), lambda g, i: (g // G, i, g % G, 0, 0))],
        out_shape=[jax.ShapeDtypeStruct((B * seq, heads * dh), BF16),
                   jax.ShapeDtypeStruct((B, nq, heads, 16, tq), F32)],
        scratch_shapes=[pltpu.VMEM((tq, hw), F32), pltpu.VMEM((hps, 8, LANE), F32)],
        compiler_params=_cp("parallel", "arbitrary"))(proj, proj, proj, jnp.tril(jnp.ones((tq, tq), BF16)))


def sba_bwd(proj, ctot, do, *, B, seq, heads, q_off, k_off, v_off, name, comm=None):
    dh = SB_HEAD_DIM
    tq = _sb_tile(seq)
    nq = seq // tq
    hps = _sb_heads_per_step(heads, SB_HEADS_BWD)
    hw = hps * dh
    scale = dh ** -0.5

    def body(q_ref, k_ref, v_ref, ct_ref, do_ref, sfx_ref, pre_ref, dq_ref, dk_ref, dv_ref, dq_acc, dk_acc, dv_acc):
        qi = pl.program_id(1)

        @pl.when(qi == 0)
        def _():
            dk_acc[...] = jnp.zeros_like(dk_acc)
            dv_acc[...] = jnp.zeros_like(dv_acc)

        tri_sfx = sfx_ref[...]
        tri_pre = pre_ref[...]
        qs = [(q_ref[:, h * dh:(h + 1) * dh].astype(F32) * scale).astype(BF16) for h in range(hps)]
        dos = [do_ref[:, h * dh:(h + 1) * dh] for h in range(hps)]
        totals = [jnp.max(jnp.broadcast_to(ct_ref[0, 0, h, 0:1, :], (LANE, tq)).T, axis=1, keepdims=True)
                  for h in range(hps)]
        dq_acc[...] = jnp.zeros_like(dq_acc)

        def part(h, rows, start, n_keys, pc, pg, mask):
            cols = slice(h * dh, (h + 1) * dh)
            q_rows, do_rows = qs[h][rows], dos[h][rows]
            k_blk = k_ref[pl.ds(start, n_keys), cols]
            v_blk = v_ref[pl.ds(start, n_keys), cols]
            z = _dot(q_rows, k_blk, NT)
            sp = _softplus(z)
            sig = jnp.exp(z - sp)
            if mask is not None:
                sp = jnp.where(mask, sp, 0.0)
            pc_next = pc + jnp.sum(sp, axis=-1, keepdims=True)
            wts = jnp.exp(z - (_dot(sp.astype(BF16), tri_sfx[0:n_keys, 0:n_keys]) + (totals[h][rows] - pc_next)))
            if mask is not None:
                wts = jnp.where(mask, wts, 0.0)
            g = _dot(do_rows, v_blk, NT) * wts
            dz = g - sig * (_dot(g.astype(BF16), tri_pre[0:n_keys, 0:n_keys]) + pg)
            if mask is not None:
                dz = jnp.where(mask, dz, 0.0)
            dz = dz.astype(BF16)
            dq_acc[rows, cols] += _dot(dz, k_blk)
            dk_acc[pl.ds(start, n_keys), cols] += _dot(dz, q_rows, TN)
            dv_acc[pl.ds(start, n_keys), cols] += _dot(wts.astype(BF16), do_rows, TN)
            return pc_next, pg + jnp.sum(g, axis=-1, keepdims=True)

        def block(start, carry):
            return tuple(part(h, slice(0, tq), start, tq, carry[h][0], carry[h][1], None) for h in range(hps))

        zero = jnp.zeros((tq, 1), F32)
        n_left = jnp.max(ct_ref[0, 0, 0, 8:16, :]).astype(jnp.int32)
        carry = lax.fori_loop(qi - n_left, qi - 1, lambda j, c: block(pl.multiple_of(j * tq, tq), c),
                              ((zero, zero),) * hps)
        below = lax.broadcasted_iota(jnp.int32, (tq, tq), 1) < lax.broadcasted_iota(jnp.int32, (tq, tq), 0)
        has_left = n_left > 0
        left = pl.multiple_of(jnp.maximum(qi - 1, 0) * tq, tq)
        for h in range(hps):
            pc, pg = part(h, slice(0, tq), left, tq, carry[h][0], carry[h][1], has_left)
            part(h, slice(0, tq), pl.multiple_of(qi * tq, tq), tq, pc, pg, below)
        dq_ref[...] = (dq_acc[...] * scale).astype(BF16)

        @pl.when(qi == nq - 1)
        def _():
            dk_ref[...] = dk_acc[...].astype(BF16)
            dv_ref[...] = dv_acc[...].astype(BF16)

    qb, kb, vb = q_off // hw, k_off // hw, v_off // hw
    G = heads // hps
    q_spec = pl.BlockSpec((tq, hw), lambda g, i: ((g // G) * nq + i, qb + g % G))
    o_spec = pl.BlockSpec((tq, hw), lambda g, i: ((g // G) * nq + i, g % G))
    kv_out = pl.BlockSpec((seq, hw), lambda g, i: (g // G, g % G))
    shp = jax.ShapeDtypeStruct((B * seq, heads * dh), BF16)
    body, c_in, c_args, c_out, c_shape, c_scratch = _hosted(body, 7, 3, comm, _grid_step(nq), B * G * nq)
    tri_spec = pl.BlockSpec((tq, tq), lambda g, i: (0, 0))
    ones = jnp.ones((tq, tq), BF16)
    out = pl.pallas_call(
        body, name=name, grid=(B * G, nq),
        in_specs=[q_spec,
                  pl.BlockSpec((seq, hw), lambda g, i: (g // G, kb + g % G)),
                  pl.BlockSpec((seq, hw), lambda g, i: (g // G, vb + g % G)),
                  pl.BlockSpec((1, 1, hps, 16, tq), lambda g, i: (g // G, i, g % G, 0, 0)), o_spec,
                  tri_spec, tri_spec] + c_in,
        out_specs=[o_spec, kv_out, kv_out] + c_out, out_shape=[shp, shp, shp] + c_shape,
        scratch_shapes=[pltpu.VMEM((tq, hw), F32), pltpu.VMEM((seq, hw), F32), pltpu.VMEM((seq, hw), F32)]
        + c_scratch,
        compiler_params=_cp("arbitrary" if comm else "parallel", "arbitrary"))(
            proj, proj, proj, ctot, do, jnp.tril(ones), jnp.triu(ones), *c_args)
    return (out[0], out[1], out[2], out[3:]) if comm else out


def _head_expand(n_heads, DI):
    j = jnp.arange(LANE, dtype=jnp.int32)[:, None]
    c = jnp.arange(DI, dtype=jnp.int32)[None, :] // SSM_HEAD_DIM
    return ((j == c) & (j < n_heads)).astype(BF16)


def _split3(x):
    hi = x.astype(BF16)
    r1 = x - hi.astype(F32)
    mid = r1.astype(BF16)
    return hi, mid, (r1 - mid.astype(F32)).astype(BF16)


def dt_fwd(proj, bias, a_log, expand, *, dt_off, name):
    T = proj.shape[0]
    DI = expand.shape[1]
    L = SSM_CHUNK
    tt = _pick(T, 512, L)

    def body(raw_ref, bias_ref, al_ref, e_ref, dt_ref, cs_ref, dtx_ref, csx_ref):
        x = raw_ref[...].astype(F32) + bias_ref[...]
        dt = _softplus(x)
        dt_ref[...] = dt
        la = dt * (-jnp.exp(al_ref[...]))
        tri = _tri01(L, True)
        for c in range(tt // L):
            cs_ref[c * L:(c + 1) * L, :] = _tri_dot3(tri, la[c * L:(c + 1) * L, :])
        e = e_ref[...]
        dtx_ref[...] = _dot(dt.astype(BF16), e).astype(BF16)
        hi, mid, lo = _split3(cs_ref[...])
        csx_ref[...] = _dot(hi, e) + _dot(mid, e) + _dot(lo, e)

    row = pl.BlockSpec((tt, LANE), lambda i: (i, 0))
    wide = pl.BlockSpec((tt, DI), lambda i: (i, 0))
    vec = pl.BlockSpec((1, LANE), lambda i: (0, 0))
    return pl.pallas_call(
        body, name=name, grid=(T // tt,),
        in_specs=[pl.BlockSpec((tt, LANE), lambda i: (i, dt_off // LANE)), vec, vec,
                  pl.BlockSpec((LANE, DI), lambda i: (0, 0))],
        out_specs=[row, row, wide, wide],
        out_shape=[jax.ShapeDtypeStruct((T, LANE), F32), jax.ShapeDtypeStruct((T, LANE), F32),
                   jax.ShapeDtypeStruct((T, DI), BF16), jax.ShapeDtypeStruct((T, DI), F32)],
        compiler_params=_cp("parallel"))(proj, bias, a_log, expand)


def dt_bwd(ddt_x, dcs_x, dcs_cols, proj, dt, bias, a_log, reduce_t, *, dt_off, n_heads, name):
    T = proj.shape[0]
    DI = reduce_t.shape[0]
    L = SSM_CHUNK
    tt = _pick(T, 512, L)

    def body(ddtx_ref, dcsx_ref, dcsc_ref, raw_ref, dt_ref, bias_ref, al_ref, r_ref, draw_ref, dbias_ref, dal_ref,
             dla_buf):
        @pl.when(pl.program_id(0) == 0)
        def _():
            dbias_ref[...] = jnp.zeros_like(dbias_ref)
            dal_ref[...] = jnp.zeros_like(dal_ref)

        r = r_ref[...]
        ddt = _dot(ddtx_ref[...], r)
        dx = dcsx_ref[...]
        hi = dx.astype(BF16)
        dcs = _dot(hi, r) + _dot((dx - hi.astype(F32)).astype(BF16), r) + dcsc_ref[...]
        triu = _tri01(L, False)
        for c in range(tt // L):
            dla_buf[c * L:(c + 1) * L, :] = _tri_dot3(triu, dcs[c * L:(c + 1) * L, :])
        dla = dla_buf[...]
        a = -jnp.exp(al_ref[...])
        valid = lax.broadcasted_iota(jnp.int32, (tt, LANE), 1) < n_heads
        dal_ref[...] += jnp.sum(jnp.where(valid, dla * dt_ref[...], 0.0), axis=0, keepdims=True) * a
        x = raw_ref[...].astype(F32) + bias_ref[...]
        draw = jnp.where(valid, (ddt + dla * a) * _sigmoid(x), 0.0)
        dbias_ref[...] += jnp.sum(draw, axis=0, keepdims=True)
        draw_ref[...] = draw.astype(BF16)

    row = pl.BlockSpec((tt, LANE), lambda i: (i, 0))
    wide = pl.BlockSpec((tt, DI), lambda i: (i, 0))
    vec = pl.BlockSpec((1, LANE), lambda i: (0, 0))
    return pl.pallas_call(
        body, name=name, grid=(T // tt,),
        in_specs=[wide, wide, row, pl.BlockSpec((tt, LANE), lambda i: (i, dt_off // LANE)), row, vec, vec,
                  pl.BlockSpec((DI, LANE), lambda i: (0, 0))],
        out_specs=[row, vec, vec],
        out_shape=[jax.ShapeDtypeStruct((T, LANE), BF16), jax.ShapeDtypeStruct((1, LANE), F32),
                   jax.ShapeDtypeStruct((1, LANE), F32)],
        scratch_shapes=[pltpu.VMEM((tt, LANE), F32)],
        compiler_params=_cp("arbitrary"))(ddt_x, dcs_x, dcs_cols, proj, dt, bias, a_log, reduce_t)


def _pair_terms(x_ref, dtx_ref, csx_ref, csr_ref, pair, ppg, lo_half, causal):
    L = SSM_CHUNK
    g, pp = divmod(pair, ppg)
    ra = g * HEAD_ROWS + 2 * pp
    cols = slice(pair * LANE, (pair + 1) * LANE)
    X = x_ref[:, cols].astype(F32)
    dt_p = dtx_ref[:, cols].astype(F32)
    own = csx_ref[:, cols]
    other = pltpu.roll(own, SSM_HEAD_DIM, 1)
    csa_c = jnp.where(lo_half, own, other)
    csb_c = jnp.where(lo_half, other, own)
    La = jnp.exp(jnp.where(causal, csa_c - csr_ref[0, ra:ra + 1, :], NEG_BIG))
    Lb = jnp.exp(jnp.where(causal, csb_c - csr_ref[0, ra + 1:ra + 2, :], NEG_BIG))
    last = csx_ref[L - 1:L, cols]
    return g, ra, cols, X, dt_p, La, Lb, jnp.exp(own), jnp.exp(last - own), jnp.exp(last)


def scan_fwd(xbc, dt_x, cs_x, cs_row, d_full, *, B, seq, DI, name):
    L, N, G = SSM_CHUNK, SSM_STATE, SSM_GROUPS
    nc = seq // L
    XW = xbc.shape[1]
    n_pairs = DI // LANE
    ppg = n_pairs // G

    def body(x_ref, dtx_ref, csx_ref, csr_ref, d_ref, y_ref, st_ref, state):
        c = pl.program_id(1)

        @pl.when(c == 0)
        def _():
            state[...] = jnp.zeros_like(state)

        causal = lax.broadcasted_iota(jnp.int32, (L, L), 0) >= lax.broadcasted_iota(jnp.int32, (L, L), 1)
        lo_half = lax.broadcasted_iota(jnp.int32, (L, LANE), 1) < SSM_HEAD_DIM
        cbs = []
        for g in range(G):
            Bc = x_ref[:, DI + g * N:DI + (g + 1) * N]
            Cc = x_ref[:, DI + G * N + g * N:DI + G * N + (g + 1) * N]
            cbs.append((Bc, Cc, _dot(Cc, Bc, NT)))
        for pair in range(n_pairs):
            g, _, cols, X, dt_p, La, Lb, ecs, tail, e_last = _pair_terms(
                x_ref, dtx_ref, csx_ref, csr_ref, pair, ppg, lo_half, causal)
            Bc, Cc, CB = cbs[g]
            xs = X * dt_p
            xsb = xs.astype(BF16)
            y = jnp.where(lo_half, _dot((CB * La).astype(BF16), xsb), _dot((CB * Lb).astype(BF16), xsb))
            ST = state[pair]
            st_ref[0, 0, pair] = ST
            y = y + ecs * _dot(Cc, ST.astype(BF16)) + d_ref[:, cols] * X
            y_ref[:, cols] = y.astype(BF16)
            state[pair] = e_last * ST + _dot(Bc, (xs * tail).astype(BF16), TN)

    wide = pl.BlockSpec((L, DI), lambda b, c: (b * nc + c, 0))
    return pl.pallas_call(
        body, name=name, grid=(B, nc),
        in_specs=[pl.BlockSpec((L, XW), lambda b, c: (b * nc + c, 0)), wide, wide,
                  pl.BlockSpec((1, G * HEAD_ROWS, L), lambda b, c: (b, 0, c)),
                  pl.BlockSpec((1, DI), lambda b, c: (0, 0))],
        out_specs=[wide, pl.BlockSpec((1, 1, n_pairs, N, LANE), lambda b, c: (b, c, 0, 0, 0))],
        out_shape=[jax.ShapeDtypeStruct((B * seq, DI), BF16),
                   jax.ShapeDtypeStruct((B, nc, n_pairs, N, LANE), F32)],
        scratch_shapes=[pltpu.VMEM((n_pairs, N, LANE), F32)],
        compiler_params=_cp("parallel", "arbitrary"))(xbc, dt_x, cs_x, cs_row, d_full)


def scan_bwd(xbc, dt_x, cs_x, cs_row, d_full, states, dy, *, B, seq, DI, name):
    L, N, G = SSM_CHUNK, SSM_STATE, SSM_GROUPS
    nc = seq // L
    XW = xbc.shape[1]
    n_pairs = DI // LANE
    ppg = n_pairs // G
    HR = G * HEAD_ROWS
    inv_p = 1.0 / SSM_HEAD_DIM

    def body(x_ref, dtx_ref, csx_ref, csr_ref, d_ref, st_ref, dy_ref, dx_ref, ddtx_ref, dcsx_ref, dcsr_ref, dd_ref,
             dH):
        c = pl.program_id(1)

        @pl.when(c == 0)
        def _():
            dH[...] = jnp.zeros_like(dH)
            dd_ref[...] = jnp.zeros_like(dd_ref)

        causal = lax.broadcasted_iota(jnp.int32, (L, L), 0) >= lax.broadcasted_iota(jnp.int32, (L, L), 1)
        lo_half = lax.broadcasted_iota(jnp.int32, (L, LANE), 1) < SSM_HEAD_DIM
        last_row = lax.broadcasted_iota(jnp.int32, (L, LANE), 0) == L - 1
        head_row = lax.broadcasted_iota(jnp.int32, (HR, 1), 0)
        dcs_rows = jnp.zeros((HR, L), F32)

        for g in range(G):
            Bc = x_ref[:, DI + g * N:DI + (g + 1) * N]
            Cc = x_ref[:, DI + G * N + g * N:DI + G * N + (g + 1) * N]
            CB = _dot(Cc, Bc, NT)
            dCB = jnp.zeros((L, L), F32)
            dC = jnp.zeros((L, N), F32)
            dB = jnp.zeros((L, N), F32)
            for pp in range(ppg):
                pair = g * ppg + pp
                _, ra, cols, X, dt_p, La, Lb, ecs, tail, e_last = _pair_terms(
                    x_ref, dtx_ref, csx_ref, csr_ref, pair, ppg, lo_half, causal)
                xs = X * dt_p
                xsb = xs.astype(BF16)
                Ma, Mb = CB * La, CB * Lb
                dY = dy_ref[:, cols].astype(F32)
                dYb = dY.astype(BF16)
                dMa = _dot(jnp.where(lo_half, dY, 0.0).astype(BF16), xsb, NT)
                dMb = _dot(jnp.where(lo_half, 0.0, dY).astype(BF16), xsb, NT)
                dSa, dSb = dMa * Ma, dMb * Mb
                dCB = dCB + dMa * La + dMb * Lb
                dcs = jnp.where(lo_half, jnp.sum(dSa, axis=1, keepdims=True), jnp.sum(dSb, axis=1, keepdims=True)) * inv_p
                dcs_rows = dcs_rows - jnp.where(head_row == ra, jnp.sum(dSa, axis=0, keepdims=True), 0.0)
                dcs_rows = dcs_rows - jnp.where(head_row == ra + 1, jnp.sum(dSb, axis=0, keepdims=True), 0.0)
                dxs = jnp.where(lo_half, _dot(Ma.astype(BF16), dYb, TN), _dot(Mb.astype(BF16), dYb, TN))
                ST = st_ref[0, 0, pair]
                STb = ST.astype(BF16)
                dYe = (dY * ecs).astype(BF16)
                dC = dC + _dot(dYe, STb, NT)
                dSTp = _dot(Cc, dYe, TN)
                dcs = dcs + dY * (ecs * _dot(Cc, STb))
                dSTn = dH[pair]
                dSTnb = dSTn.astype(BF16)
                dSTp = dSTp + e_last * dSTn
                XBt = _dot(Bc, dSTnb)
                dxs = dxs + tail * XBt
                t2 = xs * XBt * tail
                at_end = e_last * jnp.sum(dSTn * ST, axis=0, keepdims=True) + jnp.sum(t2, axis=0, keepdims=True)
                dcs = dcs - t2 + jnp.where(last_row, at_end, 0.0)
                dB = dB + _dot((xs * tail).astype(BF16), dSTnb, NT)
                dx_ref[:, cols] = (dxs * dt_p + d_ref[:, cols] * dY).astype(BF16)
                ddtx_ref[:, cols] = (dxs * X).astype(BF16)
                dcsx_ref[:, cols] = dcs
                dd_ref[0, :, cols] += jnp.sum(dY * X, axis=0, keepdims=True)
                dH[pair] = dSTp
            dCBb = dCB.astype(BF16)
            dx_ref[:, DI + g * N:DI + (g + 1) * N] = (dB + _dot(dCBb, Cc, TN)).astype(BF16)
            dx_ref[:, DI + G * N + g * N:DI + G * N + (g + 1) * N] = (dC + _dot(dCBb, Bc)).astype(BF16)
        dcsr_ref[0] = dcs_rows

    rev = lambda b, c: (b * nc + (nc - 1 - c), 0)
    wide = pl.BlockSpec((L, DI), rev)
    hrow = pl.BlockSpec((1, HR, L), lambda b, c: (b, 0, nc - 1 - c))
    return pl.pallas_call(
        body, name=name, grid=(B, nc),
        in_specs=[pl.BlockSpec((L, XW), rev), wide, wide, hrow,
                  pl.BlockSpec((1, DI), lambda b, c: (0, 0)),
                  pl.BlockSpec((1, 1, n_pairs, N, LANE), lambda b, c: (b, nc - 1 - c, 0, 0, 0)), wide],
        out_specs=[pl.BlockSpec((L, XW), rev), wide, wide, hrow, pl.BlockSpec((1, 1, DI), lambda b, c: (b, 0, 0))],
        out_shape=[jax.ShapeDtypeStruct((B * seq, XW), BF16), jax.ShapeDtypeStruct((B * seq, DI), BF16),
                   jax.ShapeDtypeStruct((B * seq, DI), F32), jax.ShapeDtypeStruct((B, HR, seq), F32),
                   jax.ShapeDtypeStruct((B, 1, DI), F32)],
        scratch_shapes=[pltpu.VMEM((n_pairs, N, LANE), F32)],
        compiler_params=_cp("parallel", "arbitrary"))(xbc, dt_x, cs_x, cs_row, d_full, states, dy)


def gnorm_fwd(y, proj, w, *, DI, name):
    T = y.shape[0]
    tt = _pick(T, 256, 16)
    gw = DI // SSM_GROUPS

    def body(y_ref, z_ref, w_ref, o_ref):
        for g in range(SSM_GROUPS):
            sl = slice(g * gw, (g + 1) * gw)
            y2 = y_ref[:, sl].astype(F32) * _silu(z_ref[:, sl].astype(F32))
            r = lax.rsqrt(jnp.mean(y2 * y2, axis=-1, keepdims=True) + EPS)
            o_ref[:, sl] = (y2 * r * w_ref[:, sl]).astype(BF16)

    row = pl.BlockSpec((tt, DI), lambda i: (i, 0))
    return pl.pallas_call(
        body, name=name, grid=(T // tt,),
        in_specs=[row, row, pl.BlockSpec((1, DI), lambda i: (0, 0))], out_specs=row,
        out_shape=jax.ShapeDtypeStruct((T, DI), BF16), compiler_params=_cp("parallel"))(y, proj, w)


def gnorm_bwd(dyn, y, proj, w, *, DI, name):
    T = y.shape[0]
    tt = _pick(T, 256, 16)
    gw = DI // SSM_GROUPS

    def body(dyn_ref, y_ref, z_ref, w_ref, dy_ref, dz_ref, dw_ref):
        @pl.when(pl.program_id(0) == 0)
        def _():
            dw_ref[...] = jnp.zeros_like(dw_ref)

        for g in range(SSM_GROUPS):
            sl = slice(g * gw, (g + 1) * gw)
            yv = y_ref[:, sl].astype(F32)
            z = z_ref[:, sl].astype(F32)
            sz = _silu(z)
            y2 = yv * sz
            r = lax.rsqrt(jnp.mean(y2 * y2, axis=-1, keepdims=True) + EPS)
            xhat = y2 * r
            d = dyn_ref[:, sl].astype(F32)
            dw_ref[:, sl] += jnp.sum(d * xhat, axis=0, keepdims=True)
            dxh = d * w_ref[:, sl]
            dy2 = r * (dxh - xhat * jnp.mean(dxh * xhat, axis=-1, keepdims=True))
            dy_ref[:, sl] = (dy2 * sz).astype(BF16)
            dz_ref[:, sl] = (dy2 * yv * _silu_grad(z)).astype(BF16)

    row = pl.BlockSpec((tt, DI), lambda i: (i, 0))
    vec = pl.BlockSpec((1, DI), lambda i: (0, 0))
    shp = jax.ShapeDtypeStruct((T, DI), BF16)
    return pl.pallas_call(
        body, name=name, grid=(T // tt,), in_specs=[row, row, row, vec], out_specs=[row, row, vec],
        out_shape=[shp, shp, jax.ShapeDtypeStruct((1, DI), F32)],
        compiler_params=_cp("arbitrary"))(dyn, y, proj, w)


N_CHIP = 4


def _comm_out_shapes(srcs, modes):
    return [jax.ShapeDtypeStruct(((N_DEV,) if mode in ("gather", "gather_direct") else ()) + s.shape, s.dtype)
            for s, mode in zip(srcs, modes)]


def _comm_scratch(n):
    return [pltpu.SemaphoreType.DMA((n, N_DEV - 1)), pltpu.SemaphoreType.DMA((n, N_DEV - 1)),
            pltpu.SemaphoreType.DMA((n,))]


def _comm_phases(modes, src_refs, out_refs, send_sems, recv_sems, local_sems):
    x, y, c = lax.axis_index("x"), lax.axis_index("y"), lax.axis_index("c")
    me, sibling = (x, y, c), (x, y, 1 - c)
    chips = [(1 - x, y), (x, 1 - y), (1 - x, 1 - y)]
    relays = [a for a, mode in enumerate(modes) if mode == "gather"]

    def slot(p):
        return 4 * p[0] + 2 * p[1] + p[2]

    def remote(a, k, src, dst, to):
        return pltpu.make_async_remote_copy(src_ref=src, dst_ref=dst, send_sem=send_sems.at[a, k],
                                            recv_sem=recv_sems.at[a, k], device_id=to,
                                            device_id_type=pl.DeviceIdType.MESH)

    def first_copies():
        local, two_way, send_only = [], [], []
        for a, mode in enumerate(modes):
            src, out = src_refs[a], out_refs[a]
            if mode == "sibling":
                two_way.append(remote(a, 0, src, out, sibling))
            elif mode == "chips":
                mine = 2 * x + y
                local.append(pltpu.make_async_copy(src.at[mine], out.at[mine], local_sems.at[a]))
                for j, chip in enumerate(chips):
                    two_way.append(remote(a, 1 + j, src.at[2 * chip[0] + chip[1]], out.at[mine], (*chip, c)))
            elif mode == "gather_direct":
                local.append(pltpu.make_async_copy(src, out.at[slot(me)], local_sems.at[a]))
                for k in range(1, N_DEV):
                    peer = (1 - x if k & 4 else x, 1 - y if k & 2 else y, 1 - c if k & 1 else c)
                    two_way.append(remote(a, k - 1, src, out.at[slot(me)], peer))
            else:
                assert mode == "gather"
                local.append(pltpu.make_async_copy(src, out.at[slot(me)], local_sems.at[a]))
                send_only.append(remote(a, 0, src, out.at[slot(me)], sibling))
                for j, chip in enumerate(chips):
                    send_only.append(remote(a, 1 + j, src, out.at[slot(me)], (*chip, c)))
        return local, two_way, send_only

    def forwards():
        out = []
        for a in relays:
            for j, chip in enumerate(chips):
                landed = out_refs[a].at[slot((*chip, c))]
                out.append((remote(a, 1 + j, landed, landed, me), remote(a, 4 + j, landed, landed, sibling)))
        return out

    def start():
        local, two_way, send_only = first_copies()
        for cp in local + two_way + send_only:
            cp.start()

    def relay():
        for arrival, fwd in forwards():
            arrival.wait_recv()
            fwd.start()

    def finish():
        local, two_way, send_only = first_copies()
        for a in relays:
            blk = out_refs[a].at[slot(sibling)]
            remote(a, 0, blk, blk, me).wait_recv()
            for j, chip in enumerate(chips):
                blk = out_refs[a].at[slot((*chip, 1 - c))]
                remote(a, 4 + j, blk, blk, me).wait_recv()
        for cp in send_only + [fwd for _, fwd in forwards()]:
            cp.wait_send()
        for cp in two_way + local:
            cp.wait()

    return start, relay, finish, bool(relays)


def _hosted(body, n_in, n_out, comm, step, n_steps):
    if comm is None:
        return body, [], [], [], [], []
    srcs, modes = comm
    nc = len(srcs)

    def wrapped(*refs):
        ins, csrc = refs[:n_in], refs[n_in:n_in + nc]
        outs = refs[n_in + nc:n_in + nc + n_out]
        cout = refs[n_in + nc + n_out:n_in + 2 * nc + n_out]
        scratch = refs[n_in + 2 * nc + n_out:len(refs) - 3]
        start, relay, finish, has_relay = _comm_phases(modes, csrc, cout, *refs[len(refs) - 3:])
        s = step()
        pl.when(s == 0)(start)
        body(*ins, *outs, *scratch)
        if has_relay:
            pl.when(s == (2 * n_steps) // 3)(relay)
        pl.when(s == n_steps - 1)(finish)

    any_spec = pl.BlockSpec(memory_space=pl.ANY)
    return wrapped, [any_spec] * nc, list(srcs), [any_spec] * nc, _comm_out_shapes(srcs, modes), _comm_scratch(nc)


def exchange(srcs, modes, *, name):
    n = len(srcs)

    def body(*refs):
        start, relay, finish, has_relay = _comm_phases(modes, refs[:n], refs[n:2 * n], *refs[2 * n:])
        start()
        if has_relay:
            relay()
        finish()

    any_spec = pl.BlockSpec(memory_space=pl.ANY)
    return pl.pallas_call(
        body, name=name, in_specs=[any_spec] * n, out_specs=[any_spec] * n, out_shape=_comm_out_shapes(srcs, modes),
        scratch_shapes=_comm_scratch(n), compiler_params=pltpu.CompilerParams(has_side_effects=True))(*srcs)


def pair_sum(a, b, *, name):
    n, R, C = a.shape
    tr = _pick(n * R, 512, 16)

    def body(a_ref, b_ref, o_ref):
        o_ref[...] = (a_ref[...].astype(F32) + b_ref[...].astype(F32)).astype(BF16)

    blk = pl.BlockSpec((tr, C), lambda i: (i, 0))
    out = pl.pallas_call(
        body, name=name, grid=(n * R // tr,), in_specs=[blk, blk], out_specs=blk,
        out_shape=jax.ShapeDtypeStruct((n * R, C), BF16),
        compiler_params=_cp("parallel"))(a.reshape(n * R, C), b.reshape(n * R, C))
    return out.reshape(n, R, C)


def sum_slots(recv, *, name):
    _, R, C = recv.shape
    tr = _pick(R, 512, 8)

    def body(r_ref, o_ref):
        acc = r_ref[0].astype(F32)
        for p in range(1, N_DEV):
            acc = acc + r_ref[p].astype(F32)
        o_ref[...] = acc

    return pl.pallas_call(
        body, name=name, grid=(R // tr,),
        in_specs=[pl.BlockSpec((N_DEV, tr, C), lambda i: (0, i, 0))],
        out_specs=pl.BlockSpec((tr, C), lambda i: (i, 0)),
        out_shape=jax.ShapeDtypeStruct((R, C), F32), compiler_params=_cp("parallel"))(recv)


def adamw(gsrc, w, m, v, *, name):
    slots, R, C = gsrc.shape
    tr = _pick(R, 256, 16 if gsrc.dtype == BF16 else 8)
    c1 = 1.0 / (1.0 - ADAM_B1 ** ADAM_STEP)
    c2 = 1.0 / (1.0 - ADAM_B2 ** ADAM_STEP)

    def body(g_ref, w_ref, m_ref, v_ref, go_ref, d_ref, mo_ref, vo_ref):
        g = g_ref[0].astype(F32)
        for p in range(1, slots):
            g = g + g_ref[p].astype(F32)
        m2 = ADAM_B1 * m_ref[...] + (1.0 - ADAM_B1) * g
        v2 = ADAM_B2 * v_ref[...] + (1.0 - ADAM_B2) * (g * g)
        go_ref[...] = g
        mo_ref[...] = m2
        vo_ref[...] = v2
        d_ref[...] = -ADAM_LR * ((m2 * c1) / (jnp.sqrt(v2 * c2) + ADAM_EPS) + ADAM_WD * w_ref[...])

    blk = pl.BlockSpec((tr, C), lambda i: (i, 0))
    shp = jax.ShapeDtypeStruct((R, C), F32)
    return pl.pallas_call(
        body, name=name, grid=(R // tr,),
        in_specs=[pl.BlockSpec((slots, tr, C), lambda i: (0, i, 0)), blk, blk, blk],
        out_specs=[blk] * 4, out_shape=[shp] * 4, compiler_params=_cp("parallel"))(gsrc, w, m, v)


def _pad_cols(a, n):
    return jnp.pad(a, ((0, 0), (0, n - a.shape[1])))


def _to_rows(a, B, seq, H):
    G = SSM_GROUPS
    R = H // G
    t = a[:, :H].reshape(B, seq, G, R).transpose(0, 2, 3, 1)
    t = jnp.pad(t, ((0, 0), (0, 0), (0, HEAD_ROWS - R), (0, 0)))
    return t.reshape(B, G * HEAD_ROWS, seq)


def _from_rows(a, B, seq, H):
    G = SSM_GROUPS
    R = H // G
    t = a.reshape(B, G, HEAD_ROWS, seq)[:, :, :R].transpose(0, 3, 1, 2).reshape(B * seq, H)
    return _pad_cols(t, LANE)


def _chip_sums(grads, name):
    c_idx = lax.axis_index("c")
    keep, give = [], []
    for g in grads:
        by_chip = g.reshape((N_CHIP, 2) + g.shape[1:])
        keep.append(lax.dynamic_index_in_dim(by_chip, c_idx, axis=1, keepdims=False))
        give.append(lax.dynamic_index_in_dim(by_chip, 1 - c_idx, axis=1, keepdims=False))
    swapped = exchange(give, ["sibling"] * len(give), name="swap_" + name)
    return [pair_sum(k, s, name=f"chip_sum_{name}_{i}") for i, (k, s) in enumerate(zip(keep, swapped))]


def local_step(x, target, loc, od_w_in_t, *, B, seq):
    T, D = x.shape
    CW = D
    heads = CW // SB_HEAD_DIM
    DI = 2 * D
    H = DI // SSM_HEAD_DIM
    XW = DI + 2 * SSM_GROUPS * SSM_STATE
    in_odd = DI + XW + H
    w1_rows = in_odd // N_DEV
    q_off, k_off, v_off, gc_off, ga_off = 3 * CW, 4 * CW, 5 * CW, 2 * CW, 6 * CW
    dt_off = DI + XW

    small_packed, small_spans = _pack_rows([loc[n] for n in SMALL_SHARDED], LANE, 8)
    n0, (g_ev_in, small_all) = rmsnorm_fwd(x, loc["ev_norm_w"], name="l0_norm",
                                           comm=([loc["ev_w_in"].astype(BF16), small_packed],
                                                 ["gather", "gather_direct"]))
    p = {n: loc[n] for n in SMALL}
    for n, a in zip(SMALL_SHARDED, _unpack_rows(small_all, small_spans)):
        p[n] = _col_unshards(a)
    p["ev_w_in"] = _col_unshards(g_ev_in)
    proj0, (g_od_in_t, g_od_out, g_ev_out) = mm_nn(
        n0, p["ev_w_in"], out_dtype=BF16, name="l0_in_proj",
        comm=([od_w_in_t.astype(BF16), loc["od_w_out"].astype(BF16), loc["ev_w_out"].astype(BF16)], ["gather"] * 3))
    p["ev_w_out"] = g_ev_out.reshape(-1, D)
    w1t = g_od_in_t[:, :w1_rows].reshape(in_odd, D)
    w1t = jnp.pad(w1t, ((0, -(-(in_odd + LANE) // 256) * 256 - in_odd), (0, 0)))
    od_w_out = g_od_out.reshape(-1, D)
    (u2,) = dwconv_fwd(proj0, (0, CW), p["ev_dw_w"], p["ev_dw_b"], C=CW, seq=seq, glu=True, silu_out=False,
                       name="l0_conv")
    o, ctot = sba_fwd(proj0, B=B, seq=seq, heads=heads, q_off=q_off, k_off=k_off, v_off=v_off, name="l0_attn")
    ycat = mix0_post_fwd(u2, proj0, o, p["ev_ln_w"], p["ev_ln_b"], CW=CW, gc_off=gc_off, ga_off=ga_off,
                         name="l0_post")
    h1 = mm_nn(ycat, p["ev_w_out"], add=x, out_dtype=F32, name="l0_out_proj")

    n1 = rmsnorm_fwd(h1, p["od_norm_w"], name="l1_norm")
    proj1 = mm_nt_terms([(n1, 0, D, 0)], w1t, out_dtype=BF16, name="l1_in_proj")
    u_pre, xbc = dwconv_fwd(proj1, (DI,), p["od_conv_w"], p["od_conv_b"], C=XW, seq=seq, glu=False, silu_out=True,
                            name="l1_conv")
    bias_p, alog_p = _pad_cols(p["od_dt_bias"], LANE), _pad_cols(p["od_a_log"], LANE)
    expand = _head_expand(H, DI)
    dt, cs, dt_x, cs_x = dt_fwd(proj1, bias_p, alog_p, expand, dt_off=dt_off, name="l1_dt")
    cs_row = _to_rows(cs, B, seq, H)
    d_full = jnp.repeat(p["od_d"], SSM_HEAD_DIM, axis=1)
    y_ssd, states = scan_fwd(xbc, dt_x, cs_x, cs_row, d_full, B=B, seq=seq, DI=DI, name="l1_ssd")
    yn = gnorm_fwd(y_ssd, proj1, p["od_gnorm_w"], DI=DI, name="l1_gnorm")
    h2 = mm_nn(yn, od_w_out, add=h1, out_dtype=F32, name="l1_out_proj")

    loss, dh2, dh2b, g_final = final_loss(h2, p["final_norm_w"], target, name="loss_head")

    g_od_w_out = mm_tn(yn, dh2b, out_dtype=BF16, name="l1_dw_out")
    dyn = mm_nt_terms([(dh2b, 0, D, 0)], od_w_out, out_dtype=BF16, name="l1_d_out_proj")
    dy_ssd, dz, g_gnorm = gnorm_bwd(dyn, y_ssd, proj1, p["od_gnorm_w"], DI=DI, name="l1_gnorm_bwd")
    dxbc_c, ddt_x, dcs_x, dcs_row, dd_part = scan_bwd(xbc, dt_x, cs_x, cs_row, d_full, states, dy_ssd, B=B, seq=seq,
                                                      DI=DI, name="l1_ssd_bwd")
    g_d = dd_part.sum(axis=(0, 1)).reshape(H, SSM_HEAD_DIM).sum(axis=1)[None, :]
    draw, g_bias, g_alog = dt_bwd(ddt_x, dcs_x, _from_rows(dcs_row, B, seq, H), proj1, dt, bias_p, alog_p, expand.T,
                                  dt_off=dt_off, n_heads=H, name="l1_dt_bwd")
    dxbc, g_conv_w, g_conv_b = dwconv_bwd(dxbc_c, u_pre, proj1, (DI,), p["od_conv_w"], C=XW, seq=seq, glu=False,
                                          silu_out=True, name="l1_conv_bwd")
    tw = 512 if DI % 512 == 0 else LANE
    terms = [(dz, j, tw, j * tw) for j in range(DI // tw)]
    terms += [(dxbc, j, tw, DI + j * tw) for j in range(XW // tw)]
    terms += [(draw, 0, LANE, dt_off)]
    dn1 = mm_nn_terms(terms, w1t, out_dtype=BF16, name="l1_d_in_proj")
    g_od_w_in_t = jnp.concatenate([mm_tn(dz, n1, out_dtype=BF16, name="l1_dw_in_z"),
                                   mm_tn(dxbc, n1, out_dtype=BF16, name="l1_dw_in_xbc"),
                                   mm_tn(draw, n1, out_dtype=BF16, name="l1_dw_in_dt")], axis=0)[:in_odd]
    dh1, dh1b, g_od_norm = rmsnorm_bwd(h1, p["od_norm_w"], dn1, dh2, name="l1_norm_bwd")
    w1_pad = (-w1_rows) % 16
    l1_chip = _chip_sums([jnp.pad(g_od_w_in_t.reshape(N_DEV, w1_rows, D), ((0, 0), (0, w1_pad), (0, 0))),
                          g_od_w_out.reshape(N_DEV, -1, D)], "l1")

    g_ev_w_out = mm_tn(ycat, dh1b, out_dtype=BF16, name="l0_dw_out")
    dycat = mm_nt_terms([(dh1b, 0, D, 0)], p["ev_w_out"], out_dtype=BF16, name="l0_d_out_proj")
    du2, dgc, dga, do, g_ln_w, g_ln_b = mix0_post_bwd(dycat, u2, proj0, o, p["ev_ln_w"], p["ev_ln_b"], CW=CW,
                                                      gc_off=gc_off, ga_off=ga_off, name="l0_post_bwd")
    dq, dk, dv, (r_od_in_t, r_od_out) = sba_bwd(proj0, ctot, do, B=B, seq=seq, heads=heads, q_off=q_off, k_off=k_off,
                                                v_off=v_off, name="l0_attn_bwd", comm=(l1_chip, ["chips", "chips"]))
    dga_a, dga_b, g_dw_w, g_dw_b = dwconv_bwd(du2, None, proj0, (0, CW), p["ev_dw_w"], C=CW, seq=seq, glu=True,
                                              silu_out=False, name="l0_conv_bwd")
    pieces = [dga_a, dga_b, dgc, dq, dk, dv, dga]
    g_ev_w_in = jnp.concatenate([mm_tn(n0, pc, out_dtype=BF16, name=f"l0_dw_in_{j}") for j, pc in enumerate(pieces)],
                                axis=1)
    l0_chip = _chip_sums([_col_shards(g_ev_w_in), g_ev_w_out.reshape(N_DEV, -1, D)], "l0")
    dn0, (r_ev_in, r_ev_out) = mm_nt_terms([(pc, 0, CW, j * CW) for j, pc in enumerate(pieces)], p["ev_w_in"],
                                           out_dtype=BF16, name="l0_d_in_proj", comm=(l0_chip, ["chips", "chips"]))
    dx, _, g_ev_norm = rmsnorm_bwd(x, p["ev_norm_w"], dn0, dh1, name="l0_norm_bwd")

    small = dict(ev_norm_w=g_ev_norm, ev_dw_w=g_dw_w, ev_dw_b=g_dw_b, ev_ln_w=g_ln_w, ev_ln_b=g_ln_b,
                 od_norm_w=g_od_norm, od_conv_w=g_conv_w, od_conv_b=g_conv_b, od_dt_bias=g_bias[:, :H],
                 od_a_log=g_alog[:, :H], od_d=g_d, od_gnorm_w=g_gnorm, final_norm_w=g_final)
    received = dict(ev_w_in=r_ev_in, ev_w_out=r_ev_out, od_w_in=r_od_in_t, od_w_out=r_od_out)
    return loss, dx, small, received


BIG = ("ev_w_in", "ev_w_out", "od_w_in", "od_w_out")
SMALL = ("ev_norm_w", "ev_dw_w", "ev_dw_b", "ev_ln_w", "ev_ln_b", "od_norm_w", "od_conv_w", "od_conv_b",
         "od_dt_bias", "od_a_log", "od_d", "od_gnorm_w", "final_norm_w")
SMALL_SHARDED = ("ev_dw_w", "od_norm_w", "od_conv_w", "od_conv_b", "od_gnorm_w")
ORDER = ("ev_norm_w", "ev_w_in", "ev_dw_w", "ev_dw_b", "ev_ln_w", "ev_ln_b", "ev_w_out", "od_norm_w", "od_w_in",
         "od_conv_w", "od_conv_b", "od_dt_bias", "od_a_log", "od_d", "od_gnorm_w", "od_w_out", "final_norm_w")


def _pack_rows(arrs, width, row_align):
    parts, spans, r0 = [], [], 0
    for a in arrs:
        flat = a.reshape(-1)
        rows = -(-flat.shape[0] // (width * row_align)) * row_align
        parts.append(jnp.pad(flat, (0, rows * width - flat.shape[0])).reshape(rows, width))
        spans.append((r0, a.size, a.shape))
        r0 += rows
    return jnp.concatenate(parts, axis=0), spans


def _unpack_rows(packed, spans):
    lead = packed.shape[:-2]
    width = packed.shape[-1]
    out = []
    for r0, size, shape in spans:
        rows = -(-size // width)
        blk = packed[..., r0:r0 + rows, :].reshape(lead + (rows * width,))[..., :size]
        out.append(blk.reshape(lead + tuple(shape)))
    return out


def _col_shards(a):
    R, C8 = a.shape
    return a.reshape(R, N_DEV, C8 // N_DEV).transpose(1, 0, 2)


def _col_unshards(a):
    n, R, C = a.shape
    return a.transpose(1, 0, 2).reshape(R, n * C)


def kernel(x, ev_norm_w, ev_w_in, ev_dw_w, ev_dw_b, ev_ln_w, ev_ln_b, ev_w_out, od_norm_w, od_w_in, od_conv_w, od_conv_b, od_dt_bias, od_a_log, od_d, od_gnorm_w, od_w_out, final_norm_w, loss_target, m_ev_norm_w, m_ev_w_in, m_ev_dw_w, m_ev_dw_b, m_ev_ln_w, m_ev_ln_b, m_ev_w_out, m_od_norm_w, m_od_w_in, m_od_conv_w, m_od_conv_b, m_od_dt_bias, m_od_a_log, m_od_d, m_od_gnorm_w, m_od_w_out, m_final_norm_w, v_ev_norm_w, v_ev_w_in, v_ev_dw_w, v_ev_dw_b, v_ev_ln_w, v_ev_ln_b, v_ev_w_out, v_od_norm_w, v_od_w_in, v_od_conv_w, v_od_conv_b, v_od_dt_bias, v_od_a_log, v_od_d, v_od_gnorm_w, v_od_w_out, v_final_norm_w):
    loc = dict(ev_norm_w=ev_norm_w, ev_w_in=ev_w_in, ev_dw_w=ev_dw_w, ev_dw_b=ev_dw_b, ev_ln_w=ev_ln_w,
               ev_ln_b=ev_ln_b, ev_w_out=ev_w_out, od_norm_w=od_norm_w, od_w_in=od_w_in, od_conv_w=od_conv_w,
               od_conv_b=od_conv_b, od_dt_bias=od_dt_bias, od_a_log=od_a_log, od_d=od_d, od_gnorm_w=od_gnorm_w,
               od_w_out=od_w_out, final_norm_w=final_norm_w)
    mom = dict(ev_norm_w=m_ev_norm_w, ev_w_in=m_ev_w_in, ev_dw_w=m_ev_dw_w, ev_dw_b=m_ev_dw_b, ev_ln_w=m_ev_ln_w,
               ev_ln_b=m_ev_ln_b, ev_w_out=m_ev_w_out, od_norm_w=m_od_norm_w, od_w_in=m_od_w_in,
               od_conv_w=m_od_conv_w, od_conv_b=m_od_conv_b, od_dt_bias=m_od_dt_bias, od_a_log=m_od_a_log,
               od_d=m_od_d, od_gnorm_w=m_od_gnorm_w, od_w_out=m_od_w_out, final_norm_w=m_final_norm_w)
    var = dict(ev_norm_w=v_ev_norm_w, ev_w_in=v_ev_w_in, ev_dw_w=v_ev_dw_w, ev_dw_b=v_ev_dw_b, ev_ln_w=v_ev_ln_w,
               ev_ln_b=v_ev_ln_b, ev_w_out=v_ev_w_out, od_norm_w=v_od_norm_w, od_w_in=v_od_w_in,
               od_conv_w=v_od_conv_w, od_conv_b=v_od_conv_b, od_dt_bias=v_od_dt_bias, od_a_log=v_od_a_log,
               od_d=v_od_d, od_gnorm_w=v_od_gnorm_w, od_w_out=v_od_w_out, final_norm_w=v_final_norm_w)
    shapes = {n: loc[n].shape for n in ORDER}
    loc = {n: (a.reshape(1, -1) if a.ndim == 1 else a.reshape(a.shape[-2:]) if a.ndim == 3 else a)
           for n, a in loc.items()}
    mom = {n: a.reshape(loc[n].shape) for n, a in mom.items()}
    var = {n: a.reshape(loc[n].shape) for n, a in var.items()}

    B, seq, D = x.shape
    me = 4 * lax.axis_index("x") + 2 * lax.axis_index("y") + lax.axis_index("c")

    w1_rows = loc["od_w_in"].shape[1]
    w1_pad = (-w1_rows) % 16

    def to_t(a):
        return jnp.pad(a.T, ((0, w1_pad), (0, 0)))

    loss, dx, grads, received = local_step(x.reshape(B * seq, D), loss_target.reshape(B * seq, D), loc,
                                           to_t(loc["od_w_in"]), B=B, seq=seq)

    gsmall_packed, gsmall_spans = _pack_rows([grads[n] for n in SMALL] + [loss], LANE, 8)
    (gsmall_recv,) = exchange([gsmall_packed], ["gather_direct"], name="gather_small_grads")

    big_out = [{} for _ in range(4)]
    for n in ("ev_w_in", "ev_w_out", "od_w_out"):
        for kind, a in enumerate(adamw(received[n], loc[n], mom[n], var[n], name="adamw_" + n)):
            big_out[kind][n] = a
    for kind, a in enumerate(adamw(received["od_w_in"], to_t(loc["od_w_in"]), to_t(mom["od_w_in"]),
                                   to_t(var["od_w_in"]), name="adamw_od_w_in")):
        big_out[kind]["od_w_in"] = a[:w1_rows].T

    summed = _unpack_rows(sum_slots(gsmall_recv, name="sum_small_grads"), gsmall_spans)
    loss_total = summed[-1][0, 0]
    gsmall = dict(zip(SMALL, summed[:-1]))
    for n in SMALL_SHARDED:
        width = loc[n].shape[1]
        gsmall[n] = lax.dynamic_slice_in_dim(gsmall[n], me * width, width, axis=1)
    gs, sspans = _pack_rows([gsmall[n] for n in SMALL], LANE, 8)
    ws, _ = _pack_rows([loc[n] for n in SMALL], LANE, 8)
    ms, _ = _pack_rows([mom[n] for n in SMALL], LANE, 8)
    vs, _ = _pack_rows([var[n] for n in SMALL], LANE, 8)
    small_out = [dict(zip(SMALL, _unpack_rows(a, sspans))) for a in adamw(gs[None], ws, ms, vs, name="adamw_small")]

    outs = [loss_total, dx.reshape(B, seq, D)]
    for kind in range(4):
        for n in ORDER:
            src = big_out[kind] if n in BIG else small_out[kind]
            outs.append(src[n].reshape(shapes[n]))
    return tuple(outs)
```

```python
import jax
import jax.numpy as jnp
from jax import lax
from jax.experimental import pallas as pl
from jax.experimental.pallas import tpu as pltpu

F32 = jnp.float32
BF16 = jnp.bfloat16

EPS = 1e-6
N_DEV = 8
LANE = 128
VMEM_LIMIT_BYTES = 48 * 1024 * 1024

SB_HEAD_DIM = 128
SSM_HEAD_DIM = 64
SSM_GROUPS = 4
SSM_STATE = 128
SSM_CHUNK = 128
HALO = 32
HEAD_ROWS = 8
NEG_BIG = -1e30

ADAM_LR = 0.001
ADAM_B1 = 0.9
ADAM_B2 = 0.999
ADAM_EPS = 1e-08
ADAM_WD = 0.01
ADAM_STEP = 10

NT = (((1,), (1,)), ((), ()))
TN = (((0,), (0,)), ((), ()))


def _cp(*sem):
    return pltpu.CompilerParams(dimension_semantics=sem, vmem_limit_bytes=VMEM_LIMIT_BYTES)


def _pick(n, cap, align):
    if n <= cap:
        return n
    t = (cap // align) * align
    while t >= align:
        if n % t == 0:
            return t
        t -= align
    raise ValueError(f"no tile for {n} (cap {cap}, align {align})")


def _sigmoid(x):
    return 0.5 * jnp.tanh(0.5 * x) + 0.5


def _silu(x):
    return x * _sigmoid(x)


def _silu_grad(x):
    s = _sigmoid(x)
    return s * (1.0 + x * (1.0 - s))


def _dot(a, b, dims=None):
    if dims is None:
        return jnp.dot(a, b, preferred_element_type=F32)
    return lax.dot_general(a, b, dims, preferred_element_type=F32)


def _tri_dot3(tri, x):
    hi = x.astype(BF16)
    r1 = x - hi.astype(F32)
    mid = r1.astype(BF16)
    lo = (r1 - mid.astype(F32)).astype(BF16)
    return _dot(tri, hi) + _dot(tri, mid) + _dot(tri, lo)


def _grid_step(n_inner):
    return lambda: pl.program_id(0) * n_inner + pl.program_id(1)


def mm_nn(a, b, *, add=None, out_dtype, name, comm=None):
    M, K = a.shape
    N = b.shape[1]
    tm = _pick(M, 2048 if K <= 1024 and add is None else 1024, 16)
    tn = _pick(N, 1024, LANE)

    def body(*refs):
        if add is None:
            a_ref, b_ref, o_ref = refs
        else:
            a_ref, b_ref, add_ref, o_ref = refs
        acc = _dot(a_ref[...], b_ref[...])
        if add is not None:
            acc = acc + add_ref[...]
        o_ref[...] = acc.astype(out_dtype)

    in_specs = [pl.BlockSpec((tm, K), lambda i, j: (i, 0)), pl.BlockSpec((K, tn), lambda i, j: (0, j))]
    args = [a, b]
    if add is not None:
        in_specs.append(pl.BlockSpec((tm, tn), lambda i, j: (i, j)))
        args.append(add)
    grid = (M // tm, N // tn)
    body, c_in, c_args, c_out, c_shape, c_scratch = _hosted(body, len(args), 1, comm, _grid_step(grid[1]),
                                                            grid[0] * grid[1])
    out = pl.pallas_call(
        body, name=name, grid=grid, in_specs=in_specs + c_in,
        out_specs=[pl.BlockSpec((tm, tn), lambda i, j: (i, j))] + c_out,
        out_shape=[jax.ShapeDtypeStruct((M, N), out_dtype)] + c_shape, scratch_shapes=c_scratch,
        compiler_params=_cp(*(("arbitrary",) * 2 if comm else ("parallel",) * 2)))(*args, *c_args)
    return (out[0], out[1:]) if comm else out[0]


def mm_nt_terms(terms, b, *, out_dtype, name, comm=None):
    M = terms[0][0].shape[0]
    N = b.shape[0]
    n_terms = len(terms)
    cap = 1024 if n_terms == 1 else 512
    tm = _pick(M, 2 * cap if n_terms == 1 else cap, 16)
    tn = _pick(N, cap, LANE)

    def body(*refs):
        o_ref = refs[-1]
        acc = None
        for t in range(n_terms):
            part = _dot(refs[2 * t][...], refs[2 * t + 1][...], NT)
            acc = part if acc is None else acc + part
        o_ref[...] = acc.astype(out_dtype)

    in_specs, args = [], []
    for arr, cb, w, off in terms:
        assert off % w == 0
        in_specs.append(pl.BlockSpec((tm, w), lambda i, j, cb=cb: (i, cb)))
        in_specs.append(pl.BlockSpec((tn, w), lambda i, j, ob=off // w: (j, ob)))
        args += [arr, b]
    grid = (M // tm, N // tn)
    body, c_in, c_args, c_out, c_shape, c_scratch = _hosted(body, len(args), 1, comm, _grid_step(grid[1]),
                                                            grid[0] * grid[1])
    out = pl.pallas_call(
        body, name=name, grid=grid, in_specs=in_specs + c_in,
        out_specs=[pl.BlockSpec((tm, tn), lambda i, j: (i, j))] + c_out,
        out_shape=[jax.ShapeDtypeStruct((M, N), out_dtype)] + c_shape, scratch_shapes=c_scratch,
        compiler_params=_cp(*(("arbitrary",) * 2 if comm else ("parallel",) * 2)))(*args, *c_args)
    return (out[0], out[1:]) if comm else out[0]


def mm_nn_terms(terms, b, *, out_dtype, name):
    M = terms[0][0].shape[0]
    N = b.shape[1]
    tm = _pick(M, 512, 16)
    tn = _pick(N, 512, LANE)
    n_terms = len(terms)

    def body(*refs):
        o_ref = refs[-1]
        acc = None
        for t in range(n_terms):
            part = _dot(refs[2 * t][...], refs[2 * t + 1][...])
            acc = part if acc is None else acc + part
        o_ref[...] = acc.astype(out_dtype)

    in_specs, args = [], []
    for arr, cb, w, off in terms:
        assert off % w == 0
        in_specs.append(pl.BlockSpec((tm, w), lambda i, j, cb=cb: (i, cb)))
        in_specs.append(pl.BlockSpec((w, tn), lambda i, j, ob=off // w: (ob, j)))
        args += [arr, b]
    return pl.pallas_call(
        body, name=name, grid=(M // tm, N // tn), in_specs=in_specs,
        out_specs=pl.BlockSpec((tm, tn), lambda i, j: (i, j)),
        out_shape=jax.ShapeDtypeStruct((M, N), out_dtype),
        compiler_params=_cp("parallel", "parallel"))(*args)


def mm_tn(a, b, *, out_dtype, name):
    T, M = a.shape
    N = b.shape[1]
    tm = _pick(M, 1024, LANE)
    tn = _pick(N, 1024, LANE)
    tk = _pick(T, 2048, 16)
    nk = T // tk

    def body(a_ref, b_ref, o_ref, acc_ref):
        k = pl.program_id(2)

        @pl.when(k == 0)
        def _():
            acc_ref[...] = jnp.zeros_like(acc_ref)

        acc_ref[...] += _dot(a_ref[...], b_ref[...], TN)

        @pl.when(k == nk - 1)
        def _():
            o_ref[...] = acc_ref[...].astype(out_dtype)

    return pl.pallas_call(
        body, name=name, grid=(M // tm, N // tn, nk),
        in_specs=[pl.BlockSpec((tk, tm), lambda i, j, k: (k, i)), pl.BlockSpec((tk, tn), lambda i, j, k: (k, j))],
        out_specs=pl.BlockSpec((tm, tn), lambda i, j, k: (i, j)),
        out_shape=jax.ShapeDtypeStruct((M, N), out_dtype),
        scratch_shapes=[pltpu.VMEM((tm, tn), F32)],
        compiler_params=_cp("parallel", "parallel", "arbitrary"))(a, b)


def rmsnorm_fwd(h, w, *, name, comm=None):
    T, D = h.shape
    tt = _pick(T, 512, 16)

    def body(h_ref, w_ref, n_ref):
        x = h_ref[...]
        r = lax.rsqrt(jnp.mean(x * x, axis=-1, keepdims=True) + EPS)
        n_ref[...] = (x * r * w_ref[...]).astype(BF16)

    body, c_in, c_args, c_out, c_shape, c_scratch = _hosted(body, 2, 1, comm, lambda: pl.program_id(0), T // tt)
    out = pl.pallas_call(
        body, name=name, grid=(T // tt,),
        in_specs=[pl.BlockSpec((tt, D), lambda i: (i, 0)), pl.BlockSpec((1, D), lambda i: (0, 0))] + c_in,
        out_specs=[pl.BlockSpec((tt, D), lambda i: (i, 0))] + c_out,
        out_shape=[jax.ShapeDtypeStruct((T, D), BF16)] + c_shape, scratch_shapes=c_scratch,
        compiler_params=_cp("arbitrary" if comm else "parallel"))(h, w, *c_args)
    return (out[0], out[1:]) if comm else out[0]


def rmsnorm_bwd(h, w, dn, dres, *, name):
    T, D = h.shape
    tt = _pick(T, 512, 16)

    def body(h_ref, w_ref, dn_ref, dres_ref, dh_ref, dhb_ref, gw_ref):
        @pl.when(pl.program_id(0) == 0)
        def _():
            gw_ref[...] = jnp.zeros_like(gw_ref)

        x = h_ref[...]
        r = lax.rsqrt(jnp.mean(x * x, axis=-1, keepdims=True) + EPS)
        xhat = x * r
        g = dn_ref[...].astype(F32)
        gw_ref[...] += jnp.sum(g * xhat, axis=0, keepdims=True)
        dxh = g * w_ref[...]
        dx = r * (dxh - xhat * jnp.mean(dxh * xhat, axis=-1, keepdims=True))
        dh = dres_ref[...] + dx
        dh_ref[...] = dh
        dhb_ref[...] = dh.astype(BF16)

    row = pl.BlockSpec((tt, D), lambda i: (i, 0))
    vec = pl.BlockSpec((1, D), lambda i: (0, 0))
    return pl.pallas_call(
        body, name=name, grid=(T // tt,), in_specs=[row, vec, row, row], out_specs=[row, row, vec],
        out_shape=[jax.ShapeDtypeStruct((T, D), F32), jax.ShapeDtypeStruct((T, D), BF16),
                   jax.ShapeDtypeStruct((1, D), F32)],
        compiler_params=_cp("arbitrary"))(h, w, dn, dres)


def final_loss(h, w, target, *, name):
    T, D = h.shape
    tt = _pick(T, 512, 16)

    def body(h_ref, w_ref, t_ref, loss_ref, dh_ref, dhb_ref, gw_ref):
        @pl.when(pl.program_id(0) == 0)
        def _():
            gw_ref[...] = jnp.zeros_like(gw_ref)
            loss_ref[...] = jnp.zeros_like(loss_ref)

        x = h_ref[...]
        r = lax.rsqrt(jnp.mean(x * x, axis=-1, keepdims=True) + EPS)
        xhat = x * r
        e = xhat * w_ref[...] - t_ref[...]
        loss_ref[...] += jnp.sum(e * e) * (0.5 / D)
        g = e * (1.0 / D)
        gw_ref[...] += jnp.sum(g * xhat, axis=0, keepdims=True)
        dxh = g * w_ref[...]
        dh = r * (dxh - xhat * jnp.mean(dxh * xhat, axis=-1, keepdims=True))
        dh_ref[...] = dh
        dhb_ref[...] = dh.astype(BF16)

    row = pl.BlockSpec((tt, D), lambda i: (i, 0))
    vec = pl.BlockSpec((1, D), lambda i: (0, 0))
    one = pl.BlockSpec((1, LANE), lambda i: (0, 0))
    return pl.pallas_call(
        body, name=name, grid=(T // tt,), in_specs=[row, vec, row], out_specs=[one, row, row, vec],
        out_shape=[jax.ShapeDtypeStruct((1, LANE), F32), jax.ShapeDtypeStruct((T, D), F32),
                   jax.ShapeDtypeStruct((T, D), BF16), jax.ShapeDtypeStruct((1, D), F32)],
        compiler_params=_cp("arbitrary"))(h, w, target)


CONV_CHUNK = 32
ROW_CHUNK = 16
SUBLANES = 8


def _conv_tiles(seq, C, K):
    return _pick(seq, 1024 if K <= SUBLANES else 512, HALO), _pick(C, 512, LANE)


def _residues(offsets):
    return sorted({s % SUBLANES for s in offsets} - {0})


def _fill_shifted(buf, shifted, residues):
    n = buf.shape[0] - SUBLANES
    for i, r in enumerate(residues):
        shifted[i, 0:n, :] = buf[r:r + n, :]


def _tap(buf, shifted, residues, offset, start, rows):
    r = offset % SUBLANES
    base = offset - r
    ref = buf if r == 0 else shifted.at[residues.index(r)]
    return ref[pl.ds(start + base, rows), :]


def dwconv_fwd(src, offs, w, b, *, C, seq, glu, silu_out, name):
    T = src.shape[0]
    K = w.shape[0]
    assert K - 1 <= HALO
    tt, tc = _conv_tiles(seq, C, K)
    n_in = 2 if glu else 1
    per = tt // HALO
    offsets = [HALO - (K - 1) + k for k in range(K)]
    residues = _residues(offsets)

    def body(*refs):
        cur = refs[0:2 * n_in:2]
        halo = refs[1:2 * n_in:2]
        w_ref, b_ref = refs[2 * n_in], refs[2 * n_in + 1]
        outs = refs[2 * n_in + 2:-2]
        buf, shifted = refs[-2], refs[-1]
        i = pl.program_id(1)
        first = (i * tt) % seq == 0

        def pre(rs, rows):
            v = rs[0][rows, :].astype(F32)
            return v * _sigmoid(rs[1][rows, :].astype(F32)) if glu else v

        def build(ci, carry):
            start = pl.multiple_of(ci * CONV_CHUNK, CONV_CHUNK)
            buf[pl.ds(HALO + start, CONV_CHUNK), :] = pre(cur, pl.ds(start, CONV_CHUNK))
            return carry

        buf[0:HALO, :] = jnp.where(first, 0.0, pre(halo, slice(None)))
        lax.fori_loop(0, tt // CONV_CHUNK, build, 0, unroll=2)
        _fill_shifted(buf, shifted, residues)

        def chunk(ci, carry):
            start = pl.multiple_of(ci * CONV_CHUNK, CONV_CHUNK)
            acc = jnp.broadcast_to(b_ref[...], (CONV_CHUNK, tc))
            for k in range(K):
                acc = acc + w_ref[k:k + 1, :] * _tap(buf, shifted, residues, offsets[k], start, CONV_CHUNK)
            outs[0][pl.ds(start, CONV_CHUNK), :] = acc.astype(BF16)
            if silu_out:
                outs[1][pl.ds(start, CONV_CHUNK), :] = _silu(acc).astype(BF16)
            return carry

        lax.fori_loop(0, tt // CONV_CHUNK, chunk, 0)

    in_specs, args = [], []
    for off in offs:
        assert off % tc == 0
        in_specs.append(pl.BlockSpec((tt, tc), lambda j, i, ob=off // tc: (i, ob + j)))
        in_specs.append(pl.BlockSpec((HALO, tc), lambda j, i, ob=off // tc: (jnp.maximum(i * per - 1, 0), ob + j)))
        args += [src, src]
    in_specs += [pl.BlockSpec((K, tc), lambda j, i: (0, j)), pl.BlockSpec((1, tc), lambda j, i: (0, j))]
    args += [w, b]
    n_out = 2 if silu_out else 1
    out = pl.pallas_call(
        body, name=name, grid=(C // tc, T // tt), in_specs=in_specs,
        out_specs=[pl.BlockSpec((tt, tc), lambda j, i: (i, j))] * n_out,
        out_shape=[jax.ShapeDtypeStruct((T, C), BF16)] * n_out,
        scratch_shapes=[pltpu.VMEM((HALO + tt, tc), F32), pltpu.VMEM((max(len(residues), 1), HALO + tt, tc), F32)],
        compiler_params=_cp("parallel", "arbitrary"))(*args)
    return out


def dwconv_bwd(du, u, src, offs, w, *, C, seq, glu, silu_out, name):
    T = src.shape[0]
    K = w.shape[0]
    assert K - 1 <= HALO
    tt, tc = _conv_tiles(seq, C, K)
    n_in = 2 if glu else 1
    per = tt // HALO
    last_blk = T // HALO - 1
    g_offsets = [K - 1 - k for k in range(K)]
    g_res = _residues(g_offsets)

    def body(*refs):
        pos = 0
        du_cur, du_nxt = refs[0], refs[1]
        pos = 2
        if silu_out:
            u_cur, u_nxt = refs[2], refs[3]
            pos = 4
        cur = refs[pos:pos + n_in]
        pos += n_in
        w_ref = refs[pos]
        outs = refs[pos + 1:pos + 1 + n_in]
        dw_ref, db_ref = refs[pos + 1 + n_in], refs[pos + 2 + n_in]
        gbuf, gshift, dw_acc, db_acc = refs[-4:]
        i = pl.program_id(1)
        last = ((i + 1) * tt) % seq == 0

        @pl.when(i == 0)
        def _():
            dw_acc[...] = jnp.zeros_like(dw_acc)
            db_acc[...] = jnp.zeros_like(db_acc)

        def build(ci, carry):
            rows = pl.ds(pl.multiple_of(ci * CONV_CHUNK, CONV_CHUNK), CONV_CHUNK)
            g = du_cur[rows, :].astype(F32)
            if silu_out:
                g = g * _silu_grad(u_cur[rows, :].astype(F32))
            gbuf[rows, :] = g
            return carry

        lax.fori_loop(0, tt // CONV_CHUNK, build, 0, unroll=2)
        g_nxt = du_nxt[...].astype(F32)
        if silu_out:
            g_nxt = g_nxt * _silu_grad(u_nxt[...].astype(F32))
        gbuf[tt:tt + HALO, :] = jnp.where(last, 0.0, g_nxt)
        _fill_shifted(gbuf, gshift, g_res)

        def fold(v):
            out = v[0:SUBLANES]
            for s in range(SUBLANES, CONV_CHUNK, SUBLANES):
                out = out + v[s:s + SUBLANES]
            return out

        def chunk(ci, carry):
            start = pl.multiple_of(ci * CONV_CHUNK, CONV_CHUNK)
            rows = pl.ds(start, CONV_CHUNK)
            a = cur[0][rows, :].astype(F32)
            if glu:
                s = _sigmoid(cur[1][rows, :].astype(F32))
                x_in = a * s
            else:
                x_in = a
            dx = jnp.zeros((CONV_CHUNK, tc), F32)
            for k in range(K):
                g_k = _tap(gbuf, gshift, g_res, g_offsets[k], start, CONV_CHUNK)
                dx = dx + w_ref[k:k + 1, :] * g_k
                dw_acc[k * SUBLANES:(k + 1) * SUBLANES, :] += fold(g_k * x_in)
            db_acc[...] += fold(gbuf[rows, :])
            if glu:
                outs[0][rows, :] = (dx * s).astype(BF16)
                outs[1][rows, :] = (dx * a * s * (1.0 - s)).astype(BF16)
            else:
                outs[0][rows, :] = dx.astype(BF16)
            return carry

        lax.fori_loop(0, tt // CONV_CHUNK, chunk, 0)

        @pl.when(i == T // tt - 1)
        def _():
            for k in range(K):
                dw_ref[k:k + 1, :] = jnp.sum(dw_acc[k * SUBLANES:(k + 1) * SUBLANES, :], axis=0, keepdims=True)
            db_ref[...] = jnp.sum(db_acc[...], axis=0, keepdims=True)

    def cur_spec(ob):
        return pl.BlockSpec((tt, tc), lambda j, i: (i, ob + j))

    def nxt_spec(ob):
        return pl.BlockSpec((HALO, tc), lambda j, i: (jnp.minimum((i + 1) * per, last_blk), ob + j))

    in_specs = [cur_spec(0), nxt_spec(0)]
    args = [du, du]
    if silu_out:
        in_specs += [cur_spec(0), nxt_spec(0)]
        args += [u, u]
    for off in offs:
        assert off % tc == 0
        in_specs.append(cur_spec(off // tc))
        args.append(src)
    in_specs.append(pl.BlockSpec((K, tc), lambda j, i: (0, j)))
    args.append(w)
    out_specs = [pl.BlockSpec((tt, tc), lambda j, i: (i, j))] * n_in
    out_specs += [pl.BlockSpec((K, tc), lambda j, i: (0, j)), pl.BlockSpec((1, tc), lambda j, i: (0, j))]
    out_shape = [jax.ShapeDtypeStruct((T, C), BF16)] * n_in
    out_shape += [jax.ShapeDtypeStruct((K, C), F32), jax.ShapeDtypeStruct((1, C), F32)]
    return pl.pallas_call(
        body, name=name, grid=(C // tc, T // tt), in_specs=in_specs, out_specs=out_specs, out_shape=out_shape,
        scratch_shapes=[pltpu.VMEM((tt + HALO, tc), F32), pltpu.VMEM((max(len(g_res), 1), tt + HALO, tc), F32),
                        pltpu.VMEM((K * SUBLANES, tc), F32), pltpu.VMEM((SUBLANES, tc), F32)],
        compiler_params=_cp("parallel", "arbitrary"))(*args)


def mix0_post_fwd(u2, proj, o, ln_w, ln_b, *, CW, gc_off, ga_off, name):
    T = u2.shape[0]
    tt = _pick(T, 256, 16)

    def body(u_ref, gc_ref, ga_ref, o_ref, lw_ref, lb_ref, y_ref):
        def chunk(ci, carry):
            rows = pl.ds(pl.multiple_of(ci * ROW_CHUNK, ROW_CHUNK), ROW_CHUNK)
            u = u_ref[rows, :].astype(F32)
            mu = jnp.mean(u, axis=-1, keepdims=True)
            xc = u - mu
            r = lax.rsqrt(jnp.mean(xc * xc, axis=-1, keepdims=True) + EPS)
            u3 = xc * r * lw_ref[...] + lb_ref[...]
            y_ref[rows, 0:CW] = (_silu(u3) * _silu(gc_ref[rows, :].astype(F32))).astype(BF16)
            y_ref[rows, CW:2 * CW] = (o_ref[rows, :].astype(F32) * _silu(ga_ref[rows, :].astype(F32))).astype(BF16)
            return carry

        lax.fori_loop(0, tt // ROW_CHUNK, chunk, 0, unroll=4)

    row = pl.BlockSpec((tt, CW), lambda i: (i, 0))
    vec = pl.BlockSpec((1, CW), lambda i: (0, 0))
    return pl.pallas_call(
        body, name=name, grid=(T // tt,),
        in_specs=[row, pl.BlockSpec((tt, CW), lambda i: (i, gc_off // CW)),
                  pl.BlockSpec((tt, CW), lambda i: (i, ga_off // CW)), row, vec, vec],
        out_specs=pl.BlockSpec((tt, 2 * CW), lambda i: (i, 0)),
        out_shape=jax.ShapeDtypeStruct((T, 2 * CW), BF16),
        compiler_params=_cp("parallel"))(u2, proj, proj, o, ln_w, ln_b)


def mix0_post_bwd(dy, u2, proj, o, ln_w, ln_b, *, CW, gc_off, ga_off, name):
    T = u2.shape[0]
    tt = _pick(T, 256, 16)

    def body(dy_ref, u_ref, gc_ref, ga_ref, o_ref, lw_ref, lb_ref, du_ref, dgc_ref, dga_ref, do_ref, dlw_ref, dlb_ref,
             lw_acc, lb_acc):
        i = pl.program_id(0)

        @pl.when(i == 0)
        def _():
            lw_acc[...] = jnp.zeros_like(lw_acc)
            lb_acc[...] = jnp.zeros_like(lb_acc)

        def fold(v):
            out = v[0:SUBLANES]
            for s in range(SUBLANES, ROW_CHUNK, SUBLANES):
                out = out + v[s:s + SUBLANES]
            return out

        def chunk(ci, carry):
            rows = pl.ds(pl.multiple_of(ci * ROW_CHUNK, ROW_CHUNK), ROW_CHUNK)
            dyc = dy_ref[rows, 0:CW].astype(F32)
            dya = dy_ref[rows, CW:2 * CW].astype(F32)
            u = u_ref[rows, :].astype(F32)
            mu = jnp.mean(u, axis=-1, keepdims=True)
            xc = u - mu
            r = lax.rsqrt(jnp.mean(xc * xc, axis=-1, keepdims=True) + EPS)
            xhat = xc * r
            u3 = xhat * lw_ref[...] + lb_ref[...]
            gc = gc_ref[rows, :].astype(F32)
            dgc_ref[rows, :] = (dyc * _silu(u3) * _silu_grad(gc)).astype(BF16)
            du3 = dyc * _silu(gc) * _silu_grad(u3)
            lw_acc[...] += fold(du3 * xhat)
            lb_acc[...] += fold(du3)
            dxh = du3 * lw_ref[...]
            du = r * (dxh - jnp.mean(dxh, axis=-1, keepdims=True)
                      - xhat * jnp.mean(dxh * xhat, axis=-1, keepdims=True))
            du_ref[rows, :] = du.astype(BF16)
            ga = ga_ref[rows, :].astype(F32)
            ov = o_ref[rows, :].astype(F32)
            do_ref[rows, :] = (dya * _silu(ga)).astype(BF16)
            dga_ref[rows, :] = (dya * ov * _silu_grad(ga)).astype(BF16)
            return carry

        lax.fori_loop(0, tt // ROW_CHUNK, chunk, 0, unroll=4)

        @pl.when(i == T // tt - 1)
        def _():
            dlw_ref[...] = jnp.sum(lw_acc[...], axis=0, keepdims=True)
            dlb_ref[...] = jnp.sum(lb_acc[...], axis=0, keepdims=True)

    row = pl.BlockSpec((tt, CW), lambda i: (i, 0))
    vec = pl.BlockSpec((1, CW), lambda i: (0, 0))
    big = jax.ShapeDtypeStruct((T, CW), BF16)
    small = jax.ShapeDtypeStruct((1, CW), F32)
    return pl.pallas_call(
        body, name=name, grid=(T // tt,),
        in_specs=[pl.BlockSpec((tt, 2 * CW), lambda i: (i, 0)), row,
                  pl.BlockSpec((tt, CW), lambda i: (i, gc_off // CW)),
                  pl.BlockSpec((tt, CW), lambda i: (i, ga_off // CW)), row, vec, vec],
        out_specs=[row, row, row, row, vec, vec],
        out_shape=[big, big, big, big, small, small],
        scratch_shapes=[pltpu.VMEM((SUBLANES, CW), F32), pltpu.VMEM((SUBLANES, CW), F32)],
        compiler_params=_cp("arbitrary"))(dy, u2, proj, proj, o, ln_w, ln_b)


SB_UNDERFLOW = 110.0
SB_BOUND_MARGIN = 1.02


def _sb_tile(seq):
    return _pick(seq, 256, LANE)


def _softplus(z):
    return jnp.maximum(z, 0.0) + jnp.log(1.0 + jnp.exp(-jnp.abs(z)))


def _tri01(n, lower):
    i = lax.broadcasted_iota(jnp.int32, (n, n), 0)
    j = lax.broadcasted_iota(jnp.int32, (n, n), 1)
    return ((i >= j) if lower else (i <= j)).astype(BF16)


SB_HEADS_FWD = 4
SB_HEADS_BWD = 2


def _sb_heads_per_step(heads, want):
    while heads % want:
        want //= 2
    return want


def sba_fwd(proj, *, B, seq, heads, q_off, k_off, v_off, name):
    dh = SB_HEAD_DIM
    tq = _sb_tile(seq)
    assert tq % (2 * LANE) == 0
    nq = seq // tq
    hps = _sb_heads_per_step(heads, SB_HEADS_FWD)
    hw = hps * dh
    scale = dh ** -0.5

    def body(q_ref, k_ref, v_ref, tri_ref, o_ref, ct_ref, acc_ref, kmax_ref):
        qi = pl.program_id(1)
        tri = tri_ref[...]
        qs = [(q_ref[:, h * dh:(h + 1) * dh].astype(F32) * scale).astype(BF16) for h in range(hps)]

        @pl.when(qi == 0)
        def _():
            def chunk(i, best):
                rows = k_ref[pl.ds(pl.multiple_of(i * tq, tq), tq), :].astype(F32)
                sq = rows * rows
                return tuple(jnp.maximum(best[h], jnp.max(jnp.sum(sq[:, h * dh:(h + 1) * dh], axis=1, keepdims=True),
                                                          axis=0, keepdims=True)) for h in range(hps))

            best = lax.fori_loop(0, nq, chunk, (jnp.zeros((1, 1), F32),) * hps)
            for h in range(hps):
                kmax_ref[h] = jnp.broadcast_to(jnp.sqrt(best[h]), (8, LANE))

        z_bound = [jnp.sqrt(jnp.sum(qs[h].astype(F32) ** 2, axis=1, keepdims=True))
                   * (SB_BOUND_MARGIN * jnp.max(kmax_ref[h], keepdims=True)) for h in range(hps)]

        def part(h, q_rows, start, n_keys, r, mask):
            k_blk = k_ref[pl.ds(start, n_keys), h * dh:(h + 1) * dh]
            v_blk = v_ref[pl.ds(start, n_keys), h * dh:(h + 1) * dh]
            z = _dot(q_rows, k_blk, NT)
            sp = _softplus(z)
            if mask is not None:
                sp = jnp.where(mask, sp, 0.0)
            wts = jnp.exp(z - (_dot(sp.astype(BF16), tri[0:n_keys, 0:n_keys]) + r))
            if mask is not None:
                wts = jnp.where(mask, wts, 0.0)
            return _dot(wts.astype(BF16), v_blk), r + jnp.sum(sp, axis=-1, keepdims=True)

        below = lax.broadcasted_iota(jnp.int32, (tq, tq), 1) < lax.broadcasted_iota(jnp.int32, (tq, tq), 0)
        has_left = qi > 0
        left = pl.multiple_of(jnp.maximum(qi - 1, 0) * tq, tq)
        rs = []
        for h in range(hps):
            pv_d, r = part(h, qs[h], pl.multiple_of(qi * tq, tq), tq, jnp.zeros((tq, 1), F32), below)
            pv_l, r = part(h, qs[h], left, tq, r, has_left)
            acc_ref[:, h * dh:(h + 1) * dh] = pv_d + pv_l
            rs.append(r)
        rs = tuple(rs)

        def block(start, rs):
            pvs, out = [], []
            for h in range(hps):
                pv, r = part(h, qs[h], start, tq, rs[h], None)
                pvs.append(pv)
                out.append(r)
            return pvs, tuple(out)

        def more(c):
            j, rs = c
            slack = rs[0] - z_bound[0]
            for h in range(1, hps):
                slack = jnp.minimum(slack, rs[h] - z_bound[h])
            return jnp.logical_and(j < qi, jnp.min(slack) <= SB_UNDERFLOW)

        def step(c):
            j, rs = c
            pvs, rs = block(pl.multiple_of((qi - 1 - j) * tq, tq), rs)
            for h in range(hps):
                acc_ref[:, h * dh:(h + 1) * dh] += pvs[h]
            return j + 1, rs

        n_left, totals = lax.while_loop(more, step, (has_left.astype(jnp.int32), rs))
        o_ref[...] = acc_ref[...].astype(BF16)
        for h in range(hps):
            ct_ref[0, 0, h, 0:8, :] = jnp.broadcast_to(totals[h], (tq, LANE)).T[0:8, :]
            ct_ref[0, 0, h, 8:16, :] = jnp.full((8, tq), n_left, F32)

    qb, kb, vb = q_off // hw, k_off // hw, v_off // hw
    G = heads // hps
    return pl.pallas_call(
        body, name=name, grid=(B * G, nq),
        in_specs=[pl.BlockSpec((tq, hw), lambda g, i: ((g // G) * nq + i, qb + g % G)),
                  pl.BlockSpec((seq, hw), lambda g, i: (g // G, kb + g % G)),
                  pl.BlockSpec((seq, hw), lambda g, i: (g // G, vb + g % G)),
                  pl.BlockSpec((tq, tq), lambda g, i: (0, 0))],
        out_specs=[pl.BlockSpec((tq, hw), lambda g, i: ((g // G) * nq + i, g % G)),
                   pl.BlockSpec((1, 1, hps, 16, tq), lambda g, i: (g // G, i, g % G, 0, 0))],
        out_shape=[jax.ShapeDtypeStruct((B * seq, heads * dh), BF16),
                   jax.ShapeDtypeStruct((B, nq, heads, 16, tq), F32)],
        scratch_shapes=[pltpu.VMEM((tq, hw), F32), pltpu.VMEM((hps, 8, LANE), F32)],
        compiler_params=_cp("parallel", "arbitrary"))(proj, proj, proj, jnp.tril(jnp.ones((tq, tq), BF16)))


def sba_bwd(proj, ctot, do, *, B, seq, heads, q_off, k_off, v_off, name, comm=None):
    dh = SB_HEAD_DIM
    tq = _sb_tile(seq)
    nq = seq // tq
    hps = _sb_heads_per_step(heads, SB_HEADS_BWD)
    hw = hps * dh
    scale = dh ** -0.5

    def body(q_ref, k_ref, v_ref, ct_ref, do_ref, sfx_ref, pre_ref, dq_ref, dk_ref, dv_ref, dq_acc, dk_acc, dv_acc):
        qi = pl.program_id(1)

        @pl.when(qi == 0)
        def _():
            dk_acc[...] = jnp.zeros_like(dk_acc)
            dv_acc[...] = jnp.zeros_like(dv_acc)

        tri_sfx = sfx_ref[...]
        tri_pre = pre_ref[...]
        qs = [(q_ref[:, h * dh:(h + 1) * dh].astype(F32) * scale).astype(BF16) for h in range(hps)]
        dos = [do_ref[:, h * dh:(h + 1) * dh] for h in range(hps)]
        totals = [jnp.max(jnp.broadcast_to(ct_ref[0, 0, h, 0:1, :], (LANE, tq)).T, axis=1, keepdims=True)
                  for h in range(hps)]
        dq_acc[...] = jnp.zeros_like(dq_acc)

        def part(h, rows, start, n_keys, pc, pg, mask):
            cols = slice(h * dh, (h + 1) * dh)
            q_rows, do_rows = qs[h][rows], dos[h][rows]
            k_blk = k_ref[pl.ds(start, n_keys), cols]
            v_blk = v_ref[pl.ds(start, n_keys), cols]
            z = _dot(q_rows, k_blk, NT)
            sp = _softplus(z)
            sig = jnp.exp(z - sp)
            if mask is not None:
                sp = jnp.where(mask, sp, 0.0)
            pc_next = pc + jnp.sum(sp, axis=-1, keepdims=True)
            wts = jnp.exp(z - (_dot(sp.astype(BF16), tri_sfx[0:n_keys, 0:n_keys]) + (totals[h][rows] - pc_next)))
            if mask is not None:
                wts = jnp.where(mask, wts, 0.0)
            g = _dot(do_rows, v_blk, NT) * wts
            dz = g - sig * (_dot(g.astype(BF16), tri_pre[0:n_keys, 0:n_keys]) + pg)
            if mask is not None:
                dz = jnp.where(mask, dz, 0.0)
            dz = dz.astype(BF16)
            dq_acc[rows, cols] += _dot(dz, k_blk)
            dk_acc[pl.ds(start, n_keys), cols] += _dot(dz, q_rows, TN)
            dv_acc[pl.ds(start, n_keys), cols] += _dot(wts.astype(BF16), do_rows, TN)
            return pc_next, pg + jnp.sum(g, axis=-1, keepdims=True)

        def block(start, carry):
            return tuple(part(h, slice(0, tq), start, tq, carry[h][0], carry[h][1], None) for h in range(hps))

        zero = jnp.zeros((tq, 1), F32)
        n_left = jnp.max(ct_ref[0, 0, 0, 8:16, :]).astype(jnp.int32)
        carry = lax.fori_loop(qi - n_left, qi - 1, lambda j, c: block(pl.multiple_of(j * tq, tq), c),
                              ((zero, zero),) * hps)
        below = lax.broadcasted_iota(jnp.int32, (tq, tq), 1) < lax.broadcasted_iota(jnp.int32, (tq, tq), 0)
        has_left = n_left > 0
        left = pl.multiple_of(jnp.maximum(qi - 1, 0) * tq, tq)
        for h in range(hps):
            pc, pg = part(h, slice(0, tq), left, tq, carry[h][0], carry[h][1], has_left)
            part(h, slice(0, tq), pl.multiple_of(qi * tq, tq), tq, pc, pg, below)
        dq_ref[...] = (dq_acc[...] * scale).astype(BF16)

        @pl.when(qi == nq - 1)
        def _():
            dk_ref[...] = dk_acc[...].astype(BF16)
            dv_ref[...] = dv_acc[...].astype(BF16)

    qb, kb, vb = q_off // hw, k_off // hw, v_off // hw
    G = heads // hps
    q_spec = pl.BlockSpec((tq, hw), lambda g, i: ((g // G) * nq + i, qb + g % G))
    o_spec = pl.BlockSpec((tq, hw), lambda g, i: ((g // G) * nq + i, g % G))
    kv_out = pl.BlockSpec((seq, hw), lambda g, i: (g // G, g % G))
    shp = jax.ShapeDtypeStruct((B * seq, heads * dh), BF16)
    body, c_in, c_args, c_out, c_shape, c_scratch = _hosted(body, 7, 3, comm, _grid_step(nq), B * G * nq)
    tri_spec = pl.BlockSpec((tq, tq), lambda g, i: (0, 0))
    ones = jnp.ones((tq, tq), BF16)
    out = pl.pallas_call(
        body, name=name, grid=(B * G, nq),
        in_specs=[q_spec,
                  pl.BlockSpec((seq, hw), lambda g, i: (g // G, kb + g % G)),
                  pl.BlockSpec((seq, hw), lambda g, i: (g // G, vb + g % G)),
                  pl.BlockSpec((1, 1, hps, 16, tq), lambda g, i: (g // G, i, g % G, 0, 0)), o_spec,
                  tri_spec, tri_spec] + c_in,
        out_specs=[o_spec, kv_out, kv_out] + c_out, out_shape=[shp, shp, shp] + c_shape,
        scratch_shapes=[pltpu.VMEM((tq, hw), F32), pltpu.VMEM((seq, hw), F32), pltpu.VMEM((seq, hw), F32)]
        + c_scratch,
        compiler_params=_cp("arbitrary" if comm else "parallel", "arbitrary"))(
            proj, proj, proj, ctot, do, jnp.tril(ones), jnp.triu(ones), *c_args)
    return (out[0], out[1], out[2], out[3:]) if comm else out


def _head_expand(n_heads, DI):
    j = jnp.arange(LANE, dtype=jnp.int32)[:, None]
    c = jnp.arange(DI, dtype=jnp.int32)[None, :] // SSM_HEAD_DIM
    return ((j == c) & (j < n_heads)).astype(BF16)


def _split3(x):
    hi = x.astype(BF16)
    r1 = x - hi.astype(F32)
    mid = r1.astype(BF16)
    return hi, mid, (r1 - mid.astype(F32)).astype(BF16)


def dt_fwd(proj, bias, a_log, expand, *, dt_off, name):
    T = proj.shape[0]
    DI = expand.shape[1]
    L = SSM_CHUNK
    tt = _pick(T, 512, L)

    def body(raw_ref, bias_ref, al_ref, e_ref, dt_ref, cs_ref, dtx_ref, csx_ref):
        x = raw_ref[...].astype(F32) + bias_ref[...]
        dt = _softplus(x)
        dt_ref[...] = dt
        la = dt * (-jnp.exp(al_ref[...]))
        tri = _tri01(L, True)
        for c in range(tt // L):
            cs_ref[c * L:(c + 1) * L, :] = _tri_dot3(tri, la[c * L:(c + 1) * L, :])
        e = e_ref[...]
        dtx_ref[...] = _dot(dt.astype(BF16), e).astype(BF16)
        hi, mid, lo = _split3(cs_ref[...])
        csx_ref[...] = _dot(hi, e) + _dot(mid, e) + _dot(lo, e)

    row = pl.BlockSpec((tt, LANE), lambda i: (i, 0))
    wide = pl.BlockSpec((tt, DI), lambda i: (i, 0))
    vec = pl.BlockSpec((1, LANE), lambda i: (0, 0))
    return pl.pallas_call(
        body, name=name, grid=(T // tt,),
        in_specs=[pl.BlockSpec((tt, LANE), lambda i: (i, dt_off // LANE)), vec, vec,
                  pl.BlockSpec((LANE, DI), lambda i: (0, 0))],
        out_specs=[row, row, wide, wide],
        out_shape=[jax.ShapeDtypeStruct((T, LANE), F32), jax.ShapeDtypeStruct((T, LANE), F32),
                   jax.ShapeDtypeStruct((T, DI), BF16), jax.ShapeDtypeStruct((T, DI), F32)],
        compiler_params=_cp("parallel"))(proj, bias, a_log, expand)


def dt_bwd(ddt_x, dcs_x, dcs_cols, proj, dt, bias, a_log, reduce_t, *, dt_off, n_heads, name):
    T = proj.shape[0]
    DI = reduce_t.shape[0]
    L = SSM_CHUNK
    tt = _pick(T, 512, L)

    def body(ddtx_ref, dcsx_ref, dcsc_ref, raw_ref, dt_ref, bias_ref, al_ref, r_ref, draw_ref, dbias_ref, dal_ref,
             dla_buf):
        @pl.when(pl.program_id(0) == 0)
        def _():
            dbias_ref[...] = jnp.zeros_like(dbias_ref)
            dal_ref[...] = jnp.zeros_like(dal_ref)

        r = r_ref[...]
        ddt = _dot(ddtx_ref[...], r)
        dx = dcsx_ref[...]
        hi = dx.astype(BF16)
        dcs = _dot(hi, r) + _dot((dx - hi.astype(F32)).astype(BF16), r) + dcsc_ref[...]
        triu = _tri01(L, False)
        for c in range(tt // L):
            dla_buf[c * L:(c + 1) * L, :] = _tri_dot3(triu, dcs[c * L:(c + 1) * L, :])
        dla = dla_buf[...]
        a = -jnp.exp(al_ref[...])
        valid = lax.broadcasted_iota(jnp.int32, (tt, LANE), 1) < n_heads
        dal_ref[...] += jnp.sum(jnp.where(valid, dla * dt_ref[...], 0.0), axis=0, keepdims=True) * a
        x = raw_ref[...].astype(F32) + bias_ref[...]
        draw = jnp.where(valid, (ddt + dla * a) * _sigmoid(x), 0.0)
        dbias_ref[...] += jnp.sum(draw, axis=0, keepdims=True)
        draw_ref[...] = draw.astype(BF16)

    row = pl.BlockSpec((tt, LANE), lambda i: (i, 0))
    wide = pl.BlockSpec((tt, DI), lambda i: (i, 0))
    vec = pl.BlockSpec((1, LANE), lambda i: (0, 0))
    return pl.pallas_call(
        body, name=name, grid=(T // tt,),
        in_specs=[wide, wide, row, pl.BlockSpec((tt, LANE), lambda i: (i, dt_off // LANE)), row, vec, vec,
                  pl.BlockSpec((DI, LANE), lambda i: (0, 0))],
        out_specs=[row, vec, vec],
        out_shape=[jax.ShapeDtypeStruct((T, LANE), BF16), jax.ShapeDtypeStruct((1, LANE), F32),
                   jax.ShapeDtypeStruct((1, LANE), F32)],
        scratch_shapes=[pltpu.VMEM((tt, LANE), F32)],
        compiler_params=_cp("arbitrary"))(ddt_x, dcs_x, dcs_cols, proj, dt, bias, a_log, reduce_t)


def _pair_terms(x_ref, dtx_ref, csx_ref, csr_ref, pair, ppg, lo_half, causal):
    L = SSM_CHUNK
    g, pp = divmod(pair, ppg)
    ra = g * HEAD_ROWS + 2 * pp
    cols = slice(pair * LANE, (pair + 1) * LANE)
    X = x_ref[:, cols].astype(F32)
    dt_p = dtx_ref[:, cols].astype(F32)
    own = csx_ref[:, cols]
    other = pltpu.roll(own, SSM_HEAD_DIM, 1)
    csa_c = jnp.where(lo_half, own, other)
    csb_c = jnp.where(lo_half, other, own)
    La = jnp.exp(jnp.where(causal, csa_c - csr_ref[0, ra:ra + 1, :], NEG_BIG))
    Lb = jnp.exp(jnp.where(causal, csb_c - csr_ref[0, ra + 1:ra + 2, :], NEG_BIG))
    last = csx_ref[L - 1:L, cols]
    return g, ra, cols, X, dt_p, La, Lb, jnp.exp(own), jnp.exp(last - own), jnp.exp(last)


def scan_fwd(xbc, dt_x, cs_x, cs_row, d_full, *, B, seq, DI, name):
    L, N, G = SSM_CHUNK, SSM_STATE, SSM_GROUPS
    nc = seq // L
    XW = xbc.shape[1]
    n_pairs = DI // LANE
    ppg = n_pairs // G

    def body(x_ref, dtx_ref, csx_ref, csr_ref, d_ref, y_ref, st_ref, state):
        c = pl.program_id(1)

        @pl.when(c == 0)
        def _():
            state[...] = jnp.zeros_like(state)

        causal = lax.broadcasted_iota(jnp.int32, (L, L), 0) >= lax.broadcasted_iota(jnp.int32, (L, L), 1)
        lo_half = lax.broadcasted_iota(jnp.int32, (L, LANE), 1) < SSM_HEAD_DIM
        cbs = []
        for g in range(G):
            Bc = x_ref[:, DI + g * N:DI + (g + 1) * N]
            Cc = x_ref[:, DI + G * N + g * N:DI + G * N + (g + 1) * N]
            cbs.append((Bc, Cc, _dot(Cc, Bc, NT)))
        for pair in range(n_pairs):
            g, _, cols, X, dt_p, La, Lb, ecs, tail, e_last = _pair_terms(
                x_ref, dtx_ref, csx_ref, csr_ref, pair, ppg, lo_half, causal)
            Bc, Cc, CB = cbs[g]
            xs = X * dt_p
            xsb = xs.astype(BF16)
            y = jnp.where(lo_half, _dot((CB * La).astype(BF16), xsb), _dot((CB * Lb).astype(BF16), xsb))
            ST = state[pair]
            st_ref[0, 0, pair] = ST
            y = y + ecs * _dot(Cc, ST.astype(BF16)) + d_ref[:, cols] * X
            y_ref[:, cols] = y.astype(BF16)
            state[pair] = e_last * ST + _dot(Bc, (xs * tail).astype(BF16), TN)

    wide = pl.BlockSpec((L, DI), lambda b, c: (b * nc + c, 0))
    return pl.pallas_call(
        body, name=name, grid=(B, nc),
        in_specs=[pl.BlockSpec((L, XW), lambda b, c: (b * nc + c, 0)), wide, wide,
                  pl.BlockSpec((1, G * HEAD_ROWS, L), lambda b, c: (b, 0, c)),
                  pl.BlockSpec((1, DI), lambda b, c: (0, 0))],
        out_specs=[wide, pl.BlockSpec((1, 1, n_pairs, N, LANE), lambda b, c: (b, c, 0, 0, 0))],
        out_shape=[jax.ShapeDtypeStruct((B * seq, DI), BF16),
                   jax.ShapeDtypeStruct((B, nc, n_pairs, N, LANE), F32)],
        scratch_shapes=[pltpu.VMEM((n_pairs, N, LANE), F32)],
        compiler_params=_cp("parallel", "arbitrary"))(xbc, dt_x, cs_x, cs_row, d_full)


def scan_bwd(xbc, dt_x, cs_x, cs_row, d_full, states, dy, *, B, seq, DI, name):
    L, N, G = SSM_CHUNK, SSM_STATE, SSM_GROUPS
    nc = seq // L
    XW = xbc.shape[1]
    n_pairs = DI // LANE
    ppg = n_pairs // G
    HR = G * HEAD_ROWS
    inv_p = 1.0 / SSM_HEAD_DIM

    def body(x_ref, dtx_ref, csx_ref, csr_ref, d_ref, st_ref, dy_ref, dx_ref, ddtx_ref, dcsx_ref, dcsr_ref, dd_ref,
             dH):
        c = pl.program_id(1)

        @pl.when(c == 0)
        def _():
            dH[...] = jnp.zeros_like(dH)
            dd_ref[...] = jnp.zeros_like(dd_ref)

        causal = lax.broadcasted_iota(jnp.int32, (L, L), 0) >= lax.broadcasted_iota(jnp.int32, (L, L), 1)
        lo_half = lax.broadcasted_iota(jnp.int32, (L, LANE), 1) < SSM_HEAD_DIM
        last_row = lax.broadcasted_iota(jnp.int32, (L, LANE), 0) == L - 1
        head_row = lax.broadcasted_iota(jnp.int32, (HR, 1), 0)
        dcs_rows = jnp.zeros((HR, L), F32)

        for g in range(G):
            Bc = x_ref[:, DI + g * N:DI + (g + 1) * N]
            Cc = x_ref[:, DI + G * N + g * N:DI + G * N + (g + 1) * N]
            CB = _dot(Cc, Bc, NT)
            dCB = jnp.zeros((L, L), F32)
            dC = jnp.zeros((L, N), F32)
            dB = jnp.zeros((L, N), F32)
            for pp in range(ppg):
                pair = g * ppg + pp
                _, ra, cols, X, dt_p, La, Lb, ecs, tail, e_last = _pair_terms(
                    x_ref, dtx_ref, csx_ref, csr_ref, pair, ppg, lo_half, causal)
                xs = X * dt_p
                xsb = xs.astype(BF16)
                Ma, Mb = CB * La, CB * Lb
                dY = dy_ref[:, cols].astype(F32)
                dYb = dY.astype(BF16)
                dMa = _dot(jnp.where(lo_half, dY, 0.0).astype(BF16), xsb, NT)
                dMb = _dot(jnp.where(lo_half, 0.0, dY).astype(BF16), xsb, NT)
                dSa, dSb = dMa * Ma, dMb * Mb
                dCB = dCB + dMa * La + dMb * Lb
                dcs = jnp.where(lo_half, jnp.sum(dSa, axis=1, keepdims=True), jnp.sum(dSb, axis=1, keepdims=True)) * inv_p
                dcs_rows = dcs_rows - jnp.where(head_row == ra, jnp.sum(dSa, axis=0, keepdims=True), 0.0)
                dcs_rows = dcs_rows - jnp.where(head_row == ra + 1, jnp.sum(dSb, axis=0, keepdims=True), 0.0)
                dxs = jnp.where(lo_half, _dot(Ma.astype(BF16), dYb, TN), _dot(Mb.astype(BF16), dYb, TN))
                ST = st_ref[0, 0, pair]
                STb = ST.astype(BF16)
                dYe = (dY * ecs).astype(BF16)
                dC = dC + _dot(dYe, STb, NT)
                dSTp = _dot(Cc, dYe, TN)
                dcs = dcs + dY * (ecs * _dot(Cc, STb))
                dSTn = dH[pair]
                dSTnb = dSTn.astype(BF16)
                dSTp = dSTp + e_last * dSTn
                XBt = _dot(Bc, dSTnb)
                dxs = dxs + tail * XBt
                t2 = xs * XBt * tail
                at_end = e_last * jnp.sum(dSTn * ST, axis=0, keepdims=True) + jnp.sum(t2, axis=0, keepdims=True)
                dcs = dcs - t2 + jnp.where(last_row, at_end, 0.0)
                dB = dB + _dot((xs * tail).astype(BF16), dSTnb, NT)
                dx_ref[:, cols] = (dxs * dt_p + d_ref[:, cols] * dY).astype(BF16)
                ddtx_ref[:, cols] = (dxs * X).astype(BF16)
                dcsx_ref[:, cols] = dcs
                dd_ref[0, :, cols] += jnp.sum(dY * X, axis=0, keepdims=True)
                dH[pair] = dSTp
            dCBb = dCB.astype(BF16)
            dx_ref[:, DI + g * N:DI + (g + 1) * N] = (dB + _dot(dCBb, Cc, TN)).astype(BF16)
            dx_ref[:, DI + G * N + g * N:DI + G * N + (g + 1) * N] = (dC + _dot(dCBb, Bc)).astype(BF16)
        dcsr_ref[0] = dcs_rows

    rev = lambda b, c: (b * nc + (nc - 1 - c), 0)
    wide = pl.BlockSpec((L, DI), rev)
    hrow = pl.BlockSpec((1, HR, L), lambda b, c: (b, 0, nc - 1 - c))
    return pl.pallas_call(
        body, name=name, grid=(B, nc),
        in_specs=[pl.BlockSpec((L, XW), rev), wide, wide, hrow,
                  pl.BlockSpec((1, DI), lambda b, c: (0, 0)),
                  pl.BlockSpec((1, 1, n_pairs, N, LANE), lambda b, c: (b, nc - 1 - c, 0, 0, 0)), wide],
        out_specs=[pl.BlockSpec((L, XW), rev), wide, wide, hrow, pl.BlockSpec((1, 1, DI), lambda b, c: (b, 0, 0))],
        out_shape=[jax.ShapeDtypeStruct((B * seq, XW), BF16), jax.ShapeDtypeStruct((B * seq, DI), BF16),
                   jax.ShapeDtypeStruct((B * seq, DI), F32), jax.ShapeDtypeStruct((B, HR, seq), F32),
                   jax.ShapeDtypeStruct((B, 1, DI), F32)],
        scratch_shapes=[pltpu.VMEM((n_pairs, N, LANE), F32)],
        compiler_params=_cp("parallel", "arbitrary"))(xbc, dt_x, cs_x, cs_row, d_full, states, dy)


def gnorm_fwd(y, proj, w, *, DI, name):
    T = y.shape[0]
    tt = _pick(T, 256, 16)
    gw = DI // SSM_GROUPS

    def body(y_ref, z_ref, w_ref, o_ref):
        for g in range(SSM_GROUPS):
            sl = slice(g * gw, (g + 1) * gw)
            y2 = y_ref[:, sl].astype(F32) * _silu(z_ref[:, sl].astype(F32))
            r = lax.rsqrt(jnp.mean(y2 * y2, axis=-1, keepdims=True) + EPS)
            o_ref[:, sl] = (y2 * r * w_ref[:, sl]).astype(BF16)

    row = pl.BlockSpec((tt, DI), lambda i: (i, 0))
    return pl.pallas_call(
        body, name=name, grid=(T // tt,),
        in_specs=[row, row, pl.BlockSpec((1, DI), lambda i: (0, 0))], out_specs=row,
        out_shape=jax.ShapeDtypeStruct((T, DI), BF16), compiler_params=_cp("parallel"))(y, proj, w)


def gnorm_bwd(dyn, y, proj, w, *, DI, name):
    T = y.shape[0]
    tt = _pick(T, 256, 16)
    gw = DI // SSM_GROUPS

    def body(dyn_ref, y_ref, z_ref, w_ref, dy_ref, dz_ref, dw_ref):
        @pl.when(pl.program_id(0) == 0)
        def _():
            dw_ref[...] = jnp.zeros_like(dw_ref)

        for g in range(SSM_GROUPS):
            sl = slice(g * gw, (g + 1) * gw)
            yv = y_ref[:, sl].astype(F32)
            z = z_ref[:, sl].astype(F32)
            sz = _silu(z)
            y2 = yv * sz
            r = lax.rsqrt(jnp.mean(y2 * y2, axis=-1, keepdims=True) + EPS)
            xhat = y2 * r
            d = dyn_ref[:, sl].astype(F32)
            dw_ref[:, sl] += jnp.sum(d * xhat, axis=0, keepdims=True)
            dxh = d * w_ref[:, sl]
            dy2 = r * (dxh - xhat * jnp.mean(dxh * xhat, axis=-1, keepdims=True))
            dy_ref[:, sl] = (dy2 * sz).astype(BF16)
            dz_ref[:, sl] = (dy2 * yv * _silu_grad(z)).astype(BF16)

    row = pl.BlockSpec((tt, DI), lambda i: (i, 0))
    vec = pl.BlockSpec((1, DI), lambda i: (0, 0))
    shp = jax.ShapeDtypeStruct((T, DI), BF16)
    return pl.pallas_call(
        body, name=name, grid=(T // tt,), in_specs=[row, row, row, vec], out_specs=[row, row, vec],
        out_shape=[shp, shp, jax.ShapeDtypeStruct((1, DI), F32)],
        compiler_params=_cp("arbitrary"))(dyn, y, proj, w)


N_CHIP = 4


def _comm_out_shapes(srcs, modes):
    return [jax.ShapeDtypeStruct(((N_DEV,) if mode in ("gather", "gather_direct") else ()) + s.shape, s.dtype)
            for s, mode in zip(srcs, modes)]


def _comm_scratch(n):
    return [pltpu.SemaphoreType.DMA((n, N_DEV - 1)), pltpu.SemaphoreType.DMA((n, N_DEV - 1)),
            pltpu.SemaphoreType.DMA((n,))]


def _comm_phases(modes, src_refs, out_refs, send_sems, recv_sems, local_sems):
    x, y, c = lax.axis_index("x"), lax.axis_index("y"), lax.axis_index("c")
    me, sibling = (x, y, c), (x, y, 1 - c)
    chips = [(1 - x, y), (x, 1 - y), (1 - x, 1 - y)]
    relays = [a for a, mode in enumerate(modes) if mode == "gather"]

    def slot(p):
        return 4 * p[0] + 2 * p[1] + p[2]

    def remote(a, k, src, dst, to):
        return pltpu.make_async_remote_copy(src_ref=src, dst_ref=dst, send_sem=send_sems.at[a, k],
                                            recv_sem=recv_sems.at[a, k], device_id=to,
                                            device_id_type=pl.DeviceIdType.MESH)

    def first_copies():
        local, two_way, send_only = [], [], []
        for a, mode in enumerate(modes):
            src, out = src_refs[a], out_refs[a]
            if mode == "sibling":
                two_way.append(remote(a, 0, src, out, sibling))
            elif mode == "chips":
                mine = 2 * x + y
                local.append(pltpu.make_async_copy(src.at[mine], out.at[mine], local_sems.at[a]))
                for j, chip in enumerate(chips):
                    two_way.append(remote(a, 1 + j, src.at[2 * chip[0] + chip[1]], out.at[mine], (*chip, c)))
            elif mode == "gather_direct":
                local.append(pltpu.make_async_copy(src, out.at[slot(me)], local_sems.at[a]))
                for k in range(1, N_DEV):
                    peer = (1 - x if k & 4 else x, 1 - y if k & 2 else y, 1 - c if k & 1 else c)
                    two_way.append(remote(a, k - 1, src, out.at[slot(me)], peer))
            else:
                assert mode == "gather"
                local.append(pltpu.make_async_copy(src, out.at[slot(me)], local_sems.at[a]))
                send_only.append(remote(a, 0, src, out.at[slot(me)], sibling))
                for j, chip in enumerate(chips):
                    send_only.append(remote(a, 1 + j, src, out.at[slot(me)], (*chip, c)))
        return local, two_way, send_only

    def forwards():
        out = []
        for a in relays:
            for j, chip in enumerate(chips):
                landed = out_refs[a].at[slot((*chip, c))]
                out.append((remote(a, 1 + j, landed, landed, me), remote(a, 4 + j, landed, landed, sibling)))
        return out

    def start():
        local, two_way, send_only = first_copies()
        for cp in local + two_way + send_only:
            cp.start()

    def relay():
        for arrival, fwd in forwards():
            arrival.wait_recv()
            fwd.start()

    def finish():
        local, two_way, send_only = first_copies()
        for a in relays:
            blk = out_refs[a].at[slot(sibling)]
            remote(a, 0, blk, blk, me).wait_recv()
            for j, chip in enumerate(chips):
                blk = out_refs[a].at[slot((*chip, 1 - c))]
                remote(a, 4 + j, blk, blk, me).wait_recv()
        for cp in send_only + [fwd for _, fwd in forwards()]:
            cp.wait_send()
        for cp in two_way + local:
            cp.wait()

    return start, relay, finish, bool(relays)


def _hosted(body, n_in, n_out, comm, step, n_steps):
    if comm is None:
        return body, [], [], [], [], []
    srcs, modes = comm
    nc = len(srcs)

    def wrapped(*refs):
        ins, csrc = refs[:n_in], refs[n_in:n_in + nc]
        outs = refs[n_in + nc:n_in + nc + n_out]
        cout = refs[n_in + nc + n_out:n_in + 2 * nc + n_out]
        scratch = refs[n_in + 2 * nc + n_out:len(refs) - 3]
        start, relay, finish, has_relay = _comm_phases(modes, csrc, cout, *refs[len(refs) - 3:])
        s = step()
        pl.when(s == 0)(start)
        body(*ins, *outs, *scratch)
        if has_relay:
            pl.when(s == (2 * n_steps) // 3)(relay)
        pl.when(s == n_steps - 1)(finish)

    any_spec = pl.BlockSpec(memory_space=pl.ANY)
    return wrapped, [any_spec] * nc, list(srcs), [any_spec] * nc, _comm_out_shapes(srcs, modes), _comm_scratch(nc)


def exchange(srcs, modes, *, name):
    n = len(srcs)

    def body(*refs):
        start, relay, finish, has_relay = _comm_phases(modes, refs[:n], refs[n:2 * n], *refs[2 * n:])
        start()
        if has_relay:
            relay()
        finish()

    any_spec = pl.BlockSpec(memory_space=pl.ANY)
    return pl.pallas_call(
        body, name=name, in_specs=[any_spec] * n, out_specs=[any_spec] * n, out_shape=_comm_out_shapes(srcs, modes),
        scratch_shapes=_comm_scratch(n), compiler_params=pltpu.CompilerParams(has_side_effects=True))(*srcs)


def pair_sum(a, b, *, name):
    n, R, C = a.shape
    tr = _pick(n * R, 512, 16)

    def body(a_ref, b_ref, o_ref):
        o_ref[...] = (a_ref[...].astype(F32) + b_ref[...].astype(F32)).astype(BF16)

    blk = pl.BlockSpec((tr, C), lambda i: (i, 0))
    out = pl.pallas_call(
        body, name=name, grid=(n * R // tr,), in_specs=[blk, blk], out_specs=blk,
        out_shape=jax.ShapeDtypeStruct((n * R, C), BF16),
        compiler_params=_cp("parallel"))(a.reshape(n * R, C), b.reshape(n * R, C))
    return out.reshape(n, R, C)


def sum_slots(recv, *, name):
    _, R, C = recv.shape
    tr = _pick(R, 512, 8)

    def body(r_ref, o_ref):
        acc = r_ref[0].astype(F32)
        for p in range(1, N_DEV):
            acc = acc + r_ref[p].astype(F32)
        o_ref[...] = acc

    return pl.pallas_call(
        body, name=name, grid=(R // tr,),
        in_specs=[pl.BlockSpec((N_DEV, tr, C), lambda i: (0, i, 0))],
        out_specs=pl.BlockSpec((tr, C), lambda i: (i, 0)),
        out_shape=jax.ShapeDtypeStruct((R, C), F32), compiler_params=_cp("parallel"))(recv)


def adamw(gsrc, w, m, v, *, name):
    slots, R, C = gsrc.shape
    tr = _pick(R, 256, 16 if gsrc.dtype == BF16 else 8)
    c1 = 1.0 / (1.0 - ADAM_B1 ** ADAM_STEP)
    c2 = 1.0 / (1.0 - ADAM_B2 ** ADAM_STEP)

    def body(g_ref, w_ref, m_ref, v_ref, go_ref, d_ref, mo_ref, vo_ref):
        g = g_ref[0].astype(F32)
        for p in range(1, slots):
            g = g + g_ref[p].astype(F32)
        m2 = ADAM_B1 * m_ref[...] + (1.0 - ADAM_B1) * g
        v2 = ADAM_B2 * v_ref[...] + (1.0 - ADAM_B2) * (g * g)
        go_ref[...] = g
        mo_ref[...] = m2
        vo_ref[...] = v2
        d_ref[...] = -ADAM_LR * ((m2 * c1) / (jnp.sqrt(v2 * c2) + ADAM_EPS) + ADAM_WD * w_ref[...])

    blk = pl.BlockSpec((tr, C), lambda i: (i, 0))
    shp = jax.ShapeDtypeStruct((R, C), F32)
    return pl.pallas_call(
        body, name=name, grid=(R // tr,),
        in_specs=[pl.BlockSpec((slots, tr, C), lambda i: (0, i, 0)), blk, blk, blk],
        out_specs=[blk] * 4, out_shape=[shp] * 4, compiler_params=_cp("parallel"))(gsrc, w, m, v)


def _pad_cols(a, n):
    return jnp.pad(a, ((0, 0), (0, n - a.shape[1])))


def _to_rows(a, B, seq, H):
    G = SSM_GROUPS
    R = H // G
    t = a[:, :H].reshape(B, seq, G, R).transpose(0, 2, 3, 1)
    t = jnp.pad(t, ((0, 0), (0, 0), (0, HEAD_ROWS - R), (0, 0)))
    return t.reshape(B, G * HEAD_ROWS, seq)


def _from_rows(a, B, seq, H):
    G = SSM_GROUPS
    R = H // G
    t = a.reshape(B, G, HEAD_ROWS, seq)[:, :, :R].transpose(0, 3, 1, 2).reshape(B * seq, H)
    return _pad_cols(t, LANE)


def _chip_sums(grads, name):
    c_idx = lax.axis_index("c")
    keep, give = [], []
    for g in grads:
        by_chip = g.reshape((N_CHIP, 2) + g.shape[1:])
        keep.append(lax.dynamic_index_in_dim(by_chip, c_idx, axis=1, keepdims=False))
        give.append(lax.dynamic_index_in_dim(by_chip, 1 - c_idx, axis=1, keepdims=False))
    swapped = exchange(give, ["sibling"] * len(give), name="swap_" + name)
    return [pair_sum(k, s, name=f"chip_sum_{name}_{i}") for i, (k, s) in enumerate(zip(keep, swapped))]


def local_step(x, target, loc, od_w_in_t, *, B, seq):
    T, D = x.shape
    CW = D
    heads = CW // SB_HEAD_DIM
    DI = 2 * D
    H = DI // SSM_HEAD_DIM
    XW = DI + 2 * SSM_GROUPS * SSM_STATE
    in_odd = DI + XW + H
    w1_rows = in_odd // N_DEV
    q_off, k_off, v_off, gc_off, ga_off = 3 * CW, 4 * CW, 5 * CW, 2 * CW, 6 * CW
    dt_off = DI + XW

    small_packed, small_spans = _pack_rows([loc[n] for n in SMALL_SHARDED], LANE, 8)
    n0, (g_ev_in, small_all) = rmsnorm_fwd(x, loc["ev_norm_w"], name="l0_norm",
                                           comm=([loc["ev_w_in"].astype(BF16), small_packed],
                                                 ["gather", "gather_direct"]))
    p = {n: loc[n] for n in SMALL}
    for n, a in zip(SMALL_SHARDED, _unpack_rows(small_all, small_spans)):
        p[n] = _col_unshards(a)
    p["ev_w_in"] = _col_unshards(g_ev_in)
    proj0, (g_od_in_t, g_od_out, g_ev_out) = mm_nn(
        n0, p["ev_w_in"], out_dtype=BF16, name="l0_in_proj",
        comm=([od_w_in_t.astype(BF16), loc["od_w_out"].astype(BF16), loc["ev_w_out"].astype(BF16)], ["gather"] * 3))
    p["ev_w_out"] = g_ev_out.reshape(-1, D)
    w1t = g_od_in_t[:, :w1_rows].reshape(in_odd, D)
    w1t = jnp.pad(w1t, ((0, -(-(in_odd + LANE) // 256) * 256 - in_odd), (0, 0)))
    od_w_out = g_od_out.reshape(-1, D)
    (u2,) = dwconv_fwd(proj0, (0, CW), p["ev_dw_w"], p["ev_dw_b"], C=CW, seq=seq, glu=True, silu_out=False,
                       name="l0_conv")
    o, ctot = sba_fwd(proj0, B=B, seq=seq, heads=heads, q_off=q_off, k_off=k_off, v_off=v_off, name="l0_attn")
    ycat = mix0_post_fwd(u2, proj0, o, p["ev_ln_w"], p["ev_ln_b"], CW=CW, gc_off=gc_off, ga_off=ga_off,
                         name="l0_post")
    h1 = mm_nn(ycat, p["ev_w_out"], add=x, out_dtype=F32, name="l0_out_proj")

    n1 = rmsnorm_fwd(h1, p["od_norm_w"], name="l1_norm")
    proj1 = mm_nt_terms([(n1, 0, D, 0)], w1t, out_dtype=BF16, name="l1_in_proj")
    u_pre, xbc = dwconv_fwd(proj1, (DI,), p["od_conv_w"], p["od_conv_b"], C=XW, seq=seq, glu=False, silu_out=True,
                            name="l1_conv")
    bias_p, alog_p = _pad_cols(p["od_dt_bias"], LANE), _pad_cols(p["od_a_log"], LANE)
    expand = _head_expand(H, DI)
    dt, cs, dt_x, cs_x = dt_fwd(proj1, bias_p, alog_p, expand, dt_off=dt_off, name="l1_dt")
    cs_row = _to_rows(cs, B, seq, H)
    d_full = jnp.repeat(p["od_d"], SSM_HEAD_DIM, axis=1)
    y_ssd, states = scan_fwd(xbc, dt_x, cs_x, cs_row, d_full, B=B, seq=seq, DI=DI, name="l1_ssd")
    yn = gnorm_fwd(y_ssd, proj1, p["od_gnorm_w"], DI=DI, name="l1_gnorm")
    h2 = mm_nn(yn, od_w_out, add=h1, out_dtype=F32, name="l1_out_proj")

    loss, dh2, dh2b, g_final = final_loss(h2, p["final_norm_w"], target, name="loss_head")

    g_od_w_out = mm_tn(yn, dh2b, out_dtype=BF16, name="l1_dw_out")
    dyn = mm_nt_terms([(dh2b, 0, D, 0)], od_w_out, out_dtype=BF16, name="l1_d_out_proj")
    dy_ssd, dz, g_gnorm = gnorm_bwd(dyn, y_ssd, proj1, p["od_gnorm_w"], DI=DI, name="l1_gnorm_bwd")
    dxbc_c, ddt_x, dcs_x, dcs_row, dd_part = scan_bwd(xbc, dt_x, cs_x, cs_row, d_full, states, dy_ssd, B=B, seq=seq,
                                                      DI=DI, name="l1_ssd_bwd")
    g_d = dd_part.sum(axis=(0, 1)).reshape(H, SSM_HEAD_DIM).sum(axis=1)[None, :]
    draw, g_bias, g_alog = dt_bwd(ddt_x, dcs_x, _from_rows(dcs_row, B, seq, H), proj1, dt, bias_p, alog_p, expand.T,
                                  dt_off=dt_off, n_heads=H, name="l1_dt_bwd")
    dxbc, g_conv_w, g_conv_b = dwconv_bwd(dxbc_c, u_pre, proj1, (DI,), p["od_conv_w"], C=XW, seq=seq, glu=False,
                                          silu_out=True, name="l1_conv_bwd")
    tw = 512 if DI % 512 == 0 else LANE
    terms = [(dz, j, tw, j * tw) for j in range(DI // tw)]
    terms += [(dxbc, j, tw, DI + j * tw) for j in range(XW // tw)]
    terms += [(draw, 0, LANE, dt_off)]
    dn1 = mm_nn_terms(terms, w1t, out_dtype=BF16, name="l1_d_in_proj")
    g_od_w_in_t = jnp.concatenate([mm_tn(dz, n1, out_dtype=BF16, name="l1_dw_in_z"),
                                   mm_tn(dxbc, n1, out_dtype=BF16, name="l1_dw_in_xbc"),
                                   mm_tn(draw, n1, out_dtype=BF16, name="l1_dw_in_dt")], axis=0)[:in_odd]
    dh1, dh1b, g_od_norm = rmsnorm_bwd(h1, p["od_norm_w"], dn1, dh2, name="l1_norm_bwd")
    w1_pad = (-w1_rows) % 16
    l1_chip = _chip_sums([jnp.pad(g_od_w_in_t.reshape(N_DEV, w1_rows, D), ((0, 0), (0, w1_pad), (0, 0))),
                          g_od_w_out.reshape(N_DEV, -1, D)], "l1")

    g_ev_w_out = mm_tn(ycat, dh1b, out_dtype=BF16, name="l0_dw_out")
    dycat = mm_nt_terms([(dh1b, 0, D, 0)], p["ev_w_out"], out_dtype=BF16, name="l0_d_out_proj")
    du2, dgc, dga, do, g_ln_w, g_ln_b = mix0_post_bwd(dycat, u2, proj0, o, p["ev_ln_w"], p["ev_ln_b"], CW=CW,
                                                      gc_off=gc_off, ga_off=ga_off, name="l0_post_bwd")
    dq, dk, dv, (r_od_in_t, r_od_out) = sba_bwd(proj0, ctot, do, B=B, seq=seq, heads=heads, q_off=q_off, k_off=k_off,
                                                v_off=v_off, name="l0_attn_bwd", comm=(l1_chip, ["chips", "chips"]))
    dga_a, dga_b, g_dw_w, g_dw_b = dwconv_bwd(du2, None, proj0, (0, CW), p["ev_dw_w"], C=CW, seq=seq, glu=True,
                                              silu_out=False, name="l0_conv_bwd")
    pieces = [dga_a, dga_b, dgc, dq, dk, dv, dga]
    g_ev_w_in = jnp.concatenate([mm_tn(n0, pc, out_dtype=BF16, name=f"l0_dw_in_{j}") for j, pc in enumerate(pieces)],
                                axis=1)
    l0_chip = _chip_sums([_col_shards(g_ev_w_in), g_ev_w_out.reshape(N_DEV, -1, D)], "l0")
    dn0, (r_ev_in, r_ev_out) = mm_nt_terms([(pc, 0, CW, j * CW) for j, pc in enumerate(pieces)], p["ev_w_in"],
                                           out_dtype=BF16, name="l0_d_in_proj", comm=(l0_chip, ["chips", "chips"]))
    dx, _, g_ev_norm = rmsnorm_bwd(x, p["ev_norm_w"], dn0, dh1, name="l0_norm_bwd")

    small = dict(ev_norm_w=g_ev_norm, ev_dw_w=g_dw_w, ev_dw_b=g_dw_b, ev_ln_w=g_ln_w, ev_ln_b=g_ln_b,
                 od_norm_w=g_od_norm, od_conv_w=g_conv_w, od_conv_b=g_conv_b, od_dt_bias=g_bias[:, :H],
                 od_a_log=g_alog[:, :H], od_d=g_d, od_gnorm_w=g_gnorm, final_norm_w=g_final)
    received = dict(ev_w_in=r_ev_in, ev_w_out=r_ev_out, od_w_in=r_od_in_t, od_w_out=r_od_out)
    return loss, dx, small, received


BIG = ("ev_w_in", "ev_w_out", "od_w_in", "od_w_out")
SMALL = ("ev_norm_w", "ev_dw_w", "ev_dw_b", "ev_ln_w", "ev_ln_b", "od_norm_w", "od_conv_w", "od_conv_b",
         "od_dt_bias", "od_a_log", "od_d", "od_gnorm_w", "final_norm_w")
SMALL_SHARDED = ("ev_dw_w", "od_norm_w", "od_conv_w", "od_conv_b", "od_gnorm_w")
ORDER = ("ev_norm_w", "ev_w_in", "ev_dw_w", "ev_dw_b", "ev_ln_w", "ev_ln_b", "ev_w_out", "od_norm_w", "od_w_in",
         "od_conv_w", "od_conv_b", "od_dt_bias", "od_a_log", "od_d", "od_gnorm_w", "od_w_out", "final_norm_w")


def _pack_rows(arrs, width, row_align):
    parts, spans, r0 = [], [], 0
    for a in arrs:
        flat = a.reshape(-1)
        rows = -(-flat.shape[0] // (width * row_align)) * row_align
        parts.append(jnp.pad(flat, (0, rows * width - flat.shape[0])).reshape(rows, width))
        spans.append((r0, a.size, a.shape))
        r0 += rows
    return jnp.concatenate(parts, axis=0), spans


def _unpack_rows(packed, spans):
    lead = packed.shape[:-2]
    width = packed.shape[-1]
    out = []
    for r0, size, shape in spans:
        rows = -(-size // width)
        blk = packed[..., r0:r0 + rows, :].reshape(lead + (rows * width,))[..., :size]
        out.append(blk.reshape(lead + tuple(shape)))
    return out


def _col_shards(a):
    R, C8 = a.shape
    return a.reshape(R, N_DEV, C8 // N_DEV).transpose(1, 0, 2)


def _col_unshards(a):
    n, R, C = a.shape
    return a.transpose(1, 0, 2).reshape(R, n * C)


def kernel(x, ev_norm_w, ev_w_in, ev_dw_w, ev_dw_b, ev_ln_w, ev_ln_b, ev_w_out, od_norm_w, od_w_in, od_conv_w, od_conv_b, od_dt_bias, od_a_log, od_d, od_gnorm_w, od_w_out, final_norm_w, loss_target, m_ev_norm_w, m_ev_w_in, m_ev_dw_w, m_ev_dw_b, m_ev_ln_w, m_ev_ln_b, m_ev_w_out, m_od_norm_w, m_od_w_in, m_od_conv_w, m_od_conv_b, m_od_dt_bias, m_od_a_log, m_od_d, m_od_gnorm_w, m_od_w_out, m_final_norm_w, v_ev_norm_w, v_ev_w_in, v_ev_dw_w, v_ev_dw_b, v_ev_ln_w, v_ev_ln_b, v_ev_w_out, v_od_norm_w, v_od_w_in, v_od_conv_w, v_od_conv_b, v_od_dt_bias, v_od_a_log, v_od_d, v_od_gnorm_w, v_od_w_out, v_final_norm_w):
    loc = dict(ev_norm_w=ev_norm_w, ev_w_in=ev_w_in, ev_dw_w=ev_dw_w, ev_dw_b=ev_dw_b, ev_ln_w=ev_ln_w,
               ev_ln_b=ev_ln_b, ev_w_out=ev_w_out, od_norm_w=od_norm_w, od_w_in=od_w_in, od_conv_w=od_conv_w,
               od_conv_b=od_conv_b, od_dt_bias=od_dt_bias, od_a_log=od_a_log, od_d=od_d, od_gnorm_w=od_gnorm_w,
               od_w_out=od_w_out, final_norm_w=final_norm_w)
    mom = dict(ev_norm_w=m_ev_norm_w, ev_w_in=m_ev_w_in, ev_dw_w=m_ev_dw_w, ev_dw_b=m_ev_dw_b, ev_ln_w=m_ev_ln_w,
               ev_ln_b=m_ev_ln_b, ev_w_out=m_ev_w_out, od_norm_w=m_od_norm_w, od_w_in=m_od_w_in,
               od_conv_w=m_od_conv_w, od_conv_b=m_od_conv_b, od_dt_bias=m_od_dt_bias, od_a_log=m_od_a_log,
               od_d=m_od_d, od_gnorm_w=m_od_gnorm_w, od_w_out=m_od_w_out, final_norm_w=m_final_norm_w)
    var = dict(ev_norm_w=v_ev_norm_w, ev_w_in=v_ev_w_in, ev_dw_w=v_ev_dw_w, ev_dw_b=v_ev_dw_b, ev_ln_w=v_ev_ln_w,
               ev_ln_b=v_ev_ln_b, ev_w_out=v_ev_w_out, od_norm_w=v_od_norm_w, od_w_in=v_od_w_in,
               od_conv_w=v_od_conv_w, od_conv_b=v_od_conv_b, od_dt_bias=v_od_dt_bias, od_a_log=v_od_a_log,
               od_d=v_od_d, od_gnorm_w=v_od_gnorm_w, od_w_out=v_od_w_out, final_norm_w=v_final_norm_w)
    shapes = {n: loc[n].shape for n in ORDER}
    loc = {n: (a.reshape(1, -1) if a.ndim == 1 else a.reshape(a.shape[-2:]) if a.ndim == 3 else a)
           for n, a in loc.items()}
    mom = {n: a.reshape(loc[n].shape) for n, a in mom.items()}
    var = {n: a.reshape(loc[n].shape) for n, a in var.items()}

    B, seq, D = x.shape
    me = 4 * lax.axis_index("x") + 2 * lax.axis_index("y") + lax.axis_index("c")

    w1_rows = loc["od_w_in"].shape[1]
    w1_pad = (-w1_rows) % 16

    def to_t(a):
        return jnp.pad(a.T, ((0, w1_pad), (0, 0)))

    loss, dx, grads, received = local_step(x.reshape(B * seq, D), loss_target.reshape(B * seq, D), loc,
                                           to_t(loc["od_w_in"]), B=B, seq=seq)

    gsmall_packed, gsmall_spans = _pack_rows([grads[n] for n in SMALL] + [loss], LANE, 8)
    (gsmall_recv,) = exchange([gsmall_packed], ["gather_direct"], name="gather_small_grads")

    big_out = [{} for _ in range(4)]
    for n in ("ev_w_in", "ev_w_out", "od_w_out"):
        for kind, a in enumerate(adamw(received[n], loc[n], mom[n], var[n], name="adamw_" + n)):
            big_out[kind][n] = a
    for kind, a in enumerate(adamw(received["od_w_in"], to_t(loc["od_w_in"]), to_t(mom["od_w_in"]),
                                   to_t(var["od_w_in"]), name="adamw_od_w_in")):
        big_out[kind]["od_w_in"] = a[:w1_rows].T

    summed = _unpack_rows(sum_slots(gsmall_recv, name="sum_small_grads"), gsmall_spans)
    loss_total = summed[-1][0, 0]
    gsmall = dict(zip(SMALL, summed[:-1]))
    for n in SMALL_SHARDED:
        width = loc[n].shape[1]
        gsmall[n] = lax.dynamic_slice_in_dim(gsmall[n], me * width, width, axis=1)
    gs, sspans = _pack_rows([gsmall[n] for n in SMALL], LANE, 8)
    ws, _ = _pack_rows([loc[n] for n in SMALL], LANE, 8)
    ms, _ = _pack_rows([mom[n] for n in SMALL], LANE, 8)
    vs, _ = _pack_rows([var[n] for n in SMALL], LANE, 8)
    small_out = [dict(zip(SMALL, _unpack_rows(a, sspans))) for a in adamw(gs[None], ws, ms, vs, name="adamw_small")]

    outs = [loss_total, dx.reshape(B, seq, D)]
    for kind in range(4):
        for n in ORDER:
            src = big_out[kind] if n in BIG else small_out[kind]
            outs.append(src[n].reshape(shapes[n]))
    return tuple(outs)
```

```python
import jax
import jax.numpy as jnp
from jax import lax
from jax.experimental import pallas as pl
from jax.experimental.pallas import tpu as pltpu

F32 = jnp.float32
BF16 = jnp.bfloat16

EPS = 1e-6
N_DEV = 8
LANE = 128
VMEM_LIMIT_BYTES = 48 * 1024 * 1024

SB_HEAD_DIM = 128
SSM_HEAD_DIM = 64
SSM_GROUPS = 4
SSM_STATE = 128
SSM_CHUNK = 128
HALO = 32
HEAD_ROWS = 8
NEG_BIG = -1e30

ADAM_LR = 0.001
ADAM_B1 = 0.9
ADAM_B2 = 0.999
ADAM_EPS = 1e-08
ADAM_WD = 0.01
ADAM_STEP = 10

NT = (((1,), (1,)), ((), ()))
TN = (((0,), (0,)), ((), ()))


def _cp(*sem):
    return pltpu.CompilerParams(dimension_semantics=sem, vmem_limit_bytes=VMEM_LIMIT_BYTES)


def _pick(n, cap, align):
    if n <= cap:
        return n
    t = (cap // align) * align
    while t >= align:
        if n % t == 0:
            return t
        t -= align
    raise ValueError(f"no tile for {n} (cap {cap}, align {align})")


def _sigmoid(x):
    return 0.5 * jnp.tanh(0.5 * x) + 0.5


def _silu(x):
    return x * _sigmoid(x)


def _silu_grad(x):
    s = _sigmoid(x)
    return s * (1.0 + x * (1.0 - s))


def _dot(a, b, dims=None):
    if dims is None:
        return jnp.dot(a, b, preferred_element_type=F32)
    return lax.dot_general(a, b, dims, preferred_element_type=F32)


def _tri_dot3(tri, x):
    hi = x.astype(BF16)
    r1 = x - hi.astype(F32)
    mid = r1.astype(BF16)
    lo = (r1 - mid.astype(F32)).astype(BF16)
    return _dot(tri, hi) + _dot(tri, mid) + _dot(tri, lo)


def _grid_step(n_inner):
    return lambda: pl.program_id(0) * n_inner + pl.program_id(1)


def mm_nn(a, b, *, add=None, out_dtype, name, comm=None):
    M, K = a.shape
    N = b.shape[1]
    tm = _pick(M, 2048 if K <= 1024 and add is None else 1024, 16)
    tn = _pick(N, 1024, LANE)

    def body(*refs):
        if add is None:
            a_ref, b_ref, o_ref = refs
        else:
            a_ref, b_ref, add_ref, o_ref = refs
        acc = _dot(a_ref[...], b_ref[...])
        if add is not None:
            acc = acc + add_ref[...]
        o_ref[...] = acc.astype(out_dtype)

    in_specs = [pl.BlockSpec((tm, K), lambda i, j: (i, 0)), pl.BlockSpec((K, tn), lambda i, j: (0, j))]
    args = [a, b]
    if add is not None:
        in_specs.append(pl.BlockSpec((tm, tn), lambda i, j: (i, j)))
        args.append(add)
    grid = (M // tm, N // tn)
    body, c_in, c_args, c_out, c_shape, c_scratch = _hosted(body, len(args), 1, comm, _grid_step(grid[1]),
                                                            grid[0] * grid[1])
    out = pl.pallas_call(
        body, name=name, grid=grid, in_specs=in_specs + c_in,
        out_specs=[pl.BlockSpec((tm, tn), lambda i, j: (i, j))] + c_out,
        out_shape=[jax.ShapeDtypeStruct((M, N), out_dtype)] + c_shape, scratch_shapes=c_scratch,
        compiler_params=_cp(*(("arbitrary",) * 2 if comm else ("parallel",) * 2)))(*args, *c_args)
    return (out[0], out[1:]) if comm else out[0]


def mm_nt_terms(terms, b, *, out_dtype, name, comm=None):
    M = terms[0][0].shape[0]
    N = b.shape[0]
    n_terms = len(terms)
    if n_terms == 1:
        tm, tn = _pick(M, 2048, 16), _pick(N, 1024, LANE)
    else:
        tm, tn = _pick(M, 256, 16), _pick(N, 1024, LANE)

    def body(*refs):
        o_ref = refs[-1]
        acc = None
        for t in range(n_terms):
            part = _dot(refs[2 * t][...], refs[2 * t + 1][...], NT)
            acc = part if acc is None else acc + part
        o_ref[...] = acc.astype(out_dtype)

    in_specs, args = [], []
    for arr, cb, w, off in terms:
        assert off % w == 0
        in_specs.append(pl.BlockSpec((tm, w), lambda i, j, cb=cb: (i, cb)))
        in_specs.append(pl.BlockSpec((tn, w), lambda i, j, ob=off // w: (j, ob)))
        args += [arr, b]
    grid = (M // tm, N // tn)
    body, c_in, c_args, c_out, c_shape, c_scratch = _hosted(body, len(args), 1, comm, _grid_step(grid[1]),
                                                            grid[0] * grid[1])
    out = pl.pallas_call(
        body, name=name, grid=grid, in_specs=in_specs + c_in,
        out_specs=[pl.BlockSpec((tm, tn), lambda i, j: (i, j))] + c_out,
        out_shape=[jax.ShapeDtypeStruct((M, N), out_dtype)] + c_shape, scratch_shapes=c_scratch,
        compiler_params=_cp(*(("arbitrary",) * 2 if comm else ("parallel",) * 2)))(*args, *c_args)
    return (out[0], out[1:]) if comm else out[0]


def mm_nn_terms(terms, b, *, out_dtype, name):
    M = terms[0][0].shape[0]
    N = b.shape[1]
    tm = _pick(M, 256, 16)
    tn = _pick(N, 1024, LANE)
    n_terms = len(terms)

    def body(*refs):
        o_ref = refs[-1]
        acc = None
        for t in range(n_terms):
            part = _dot(refs[2 * t][...], refs[2 * t + 1][...])
            acc = part if acc is None else acc + part
        o_ref[...] = acc.astype(out_dtype)

    in_specs, args = [], []
    for arr, cb, w, off in terms:
        assert off % w == 0
        in_specs.append(pl.BlockSpec((tm, w), lambda i, j, cb=cb: (i, cb)))
        in_specs.append(pl.BlockSpec((w, tn), lambda i, j, ob=off // w: (ob, j)))
        args += [arr, b]
    return pl.pallas_call(
        body, name=name, grid=(M // tm, N // tn), in_specs=in_specs,
        out_specs=pl.BlockSpec((tm, tn), lambda i, j: (i, j)),
        out_shape=jax.ShapeDtypeStruct((M, N), out_dtype),
        compiler_params=_cp("parallel", "parallel"))(*args)


def mm_tn(a, b, *, out_dtype, name):
    T, M = a.shape
    N = b.shape[1]
    tm = _pick(M, 1024, LANE)
    tn = _pick(N, 1024, LANE)
    tk = _pick(T, 2048, 16)
    nk = T // tk

    def body(a_ref, b_ref, o_ref, acc_ref):
        k = pl.program_id(2)

        @pl.when(k == 0)
        def _():
            acc_ref[...] = jnp.zeros_like(acc_ref)

        acc_ref[...] += _dot(a_ref[...], b_ref[...], TN)

        @pl.when(k == nk - 1)
        def _():
            o_ref[...] = acc_ref[...].astype(out_dtype)

    return pl.pallas_call(
        body, name=name, grid=(M // tm, N // tn, nk),
        in_specs=[pl.BlockSpec((tk, tm), lambda i, j, k: (k, i)), pl.BlockSpec((tk, tn), lambda i, j, k: (k, j))],
        out_specs=pl.BlockSpec((tm, tn), lambda i, j, k: (i, j)),
        out_shape=jax.ShapeDtypeStruct((M, N), out_dtype),
        scratch_shapes=[pltpu.VMEM((tm, tn), F32)],
        compiler_params=_cp("parallel", "parallel", "arbitrary"))(a, b)


def rmsnorm_fwd(h, w, *, name, comm=None):
    T, D = h.shape
    tt = _pick(T, 512, 16)

    def body(h_ref, w_ref, n_ref):
        x = h_ref[...]
        r = lax.rsqrt(jnp.mean(x * x, axis=-1, keepdims=True) + EPS)
        n_ref[...] = (x * r * w_ref[...]).astype(BF16)

    body, c_in, c_args, c_out, c_shape, c_scratch = _hosted(body, 2, 1, comm, lambda: pl.program_id(0), T // tt)
    out = pl.pallas_call(
        body, name=name, grid=(T // tt,),
        in_specs=[pl.BlockSpec((tt, D), lambda i: (i, 0)), pl.BlockSpec((1, D), lambda i: (0, 0))] + c_in,
        out_specs=[pl.BlockSpec((tt, D), lambda i: (i, 0))] + c_out,
        out_shape=[jax.ShapeDtypeStruct((T, D), BF16)] + c_shape, scratch_shapes=c_scratch,
        compiler_params=_cp("arbitrary" if comm else "parallel"))(h, w, *c_args)
    return (out[0], out[1:]) if comm else out[0]


def rmsnorm_bwd(h, w, dn, dres, *, name):
    T, D = h.shape
    tt = _pick(T, 512, 16)

    def body(h_ref, w_ref, dn_ref, dres_ref, dh_ref, dhb_ref, gw_ref):
        @pl.when(pl.program_id(0) == 0)
        def _():
            gw_ref[...] = jnp.zeros_like(gw_ref)

        x = h_ref[...]
        r = lax.rsqrt(jnp.mean(x * x, axis=-1, keepdims=True) + EPS)
        xhat = x * r
        g = dn_ref[...].astype(F32)
        gw_ref[...] += jnp.sum(g * xhat, axis=0, keepdims=True)
        dxh = g * w_ref[...]
        dx = r * (dxh - xhat * jnp.mean(dxh * xhat, axis=-1, keepdims=True))
        dh = dres_ref[...] + dx
        dh_ref[...] = dh
        dhb_ref[...] = dh.astype(BF16)

    row = pl.BlockSpec((tt, D), lambda i: (i, 0))
    vec = pl.BlockSpec((1, D), lambda i: (0, 0))
    return pl.pallas_call(
        body, name=name, grid=(T // tt,), in_specs=[row, vec, row, row], out_specs=[row, row, vec],
        out_shape=[jax.ShapeDtypeStruct((T, D), F32), jax.ShapeDtypeStruct((T, D), BF16),
                   jax.ShapeDtypeStruct((1, D), F32)],
        compiler_params=_cp("arbitrary"))(h, w, dn, dres)


def final_loss(h, w, target, *, name):
    T, D = h.shape
    tt = _pick(T, 512, 16)

    def body(h_ref, w_ref, t_ref, loss_ref, dh_ref, dhb_ref, gw_ref):
        @pl.when(pl.program_id(0) == 0)
        def _():
            gw_ref[...] = jnp.zeros_like(gw_ref)
            loss_ref[...] = jnp.zeros_like(loss_ref)

        x = h_ref[...]
        r = lax.rsqrt(jnp.mean(x * x, axis=-1, keepdims=True) + EPS)
        xhat = x * r
        e = xhat * w_ref[...] - t_ref[...]
        loss_ref[...] += jnp.sum(e * e) * (0.5 / D)
        g = e * (1.0 / D)
        gw_ref[...] += jnp.sum(g * xhat, axis=0, keepdims=True)
        dxh = g * w_ref[...]
        dh = r * (dxh - xhat * jnp.mean(dxh * xhat, axis=-1, keepdims=True))
        dh_ref[...] = dh
        dhb_ref[...] = dh.astype(BF16)

    row = pl.BlockSpec((tt, D), lambda i: (i, 0))
    vec = pl.BlockSpec((1, D), lambda i: (0, 0))
    one = pl.BlockSpec((1, LANE), lambda i: (0, 0))
    return pl.pallas_call(
        body, name=name, grid=(T // tt,), in_specs=[row, vec, row], out_specs=[one, row, row, vec],
        out_shape=[jax.ShapeDtypeStruct((1, LANE), F32), jax.ShapeDtypeStruct((T, D), F32),
                   jax.ShapeDtypeStruct((T, D), BF16), jax.ShapeDtypeStruct((1, D), F32)],
        compiler_params=_cp("arbitrary"))(h, w, target)


CONV_CHUNK = 32
ROW_CHUNK = 16
SUBLANES = 8


def _conv_tiles(seq, C, K):
    return _pick(seq, 1024 if K <= SUBLANES else 512, HALO), _pick(C, 512, LANE)


def _residues(offsets):
    return sorted({s % SUBLANES for s in offsets} - {0})


def _fill_shifted(buf, shifted, residues):
    n = buf.shape[0] - SUBLANES
    for i, r in enumerate(residues):
        shifted[i, 0:n, :] = buf[r:r + n, :]


def _tap(buf, shifted, residues, offset, start, rows):
    r = offset % SUBLANES
    base = offset - r
    ref = buf if r == 0 else shifted.at[residues.index(r)]
    return ref[pl.ds(start + base, rows), :]


def dwconv_fwd(src, offs, w, b, *, C, seq, glu, silu_out, name):
    T = src.shape[0]
    K = w.shape[0]
    assert K - 1 <= HALO
    tt, tc = _conv_tiles(seq, C, K)
    n_in = 2 if glu else 1
    per = tt // HALO
    offsets = [HALO - (K - 1) + k for k in range(K)]
    residues = _residues(offsets)

    def body(*refs):
        cur = refs[0:2 * n_in:2]
        halo = refs[1:2 * n_in:2]
        w_ref, b_ref = refs[2 * n_in], refs[2 * n_in + 1]
        outs = refs[2 * n_in + 2:-2]
        buf, shifted = refs[-2], refs[-1]
        i = pl.program_id(1)
        first = (i * tt) % seq == 0

        def pre(rs, rows):
            v = rs[0][rows, :].astype(F32)
            return v * _sigmoid(rs[1][rows, :].astype(F32)) if glu else v

        def build(ci, carry):
            start = pl.multiple_of(ci * CONV_CHUNK, CONV_CHUNK)
            buf[pl.ds(HALO + start, CONV_CHUNK), :] = pre(cur, pl.ds(start, CONV_CHUNK))
            return carry

        buf[0:HALO, :] = jnp.where(first, 0.0, pre(halo, slice(None)))
        lax.fori_loop(0, tt // CONV_CHUNK, build, 0, unroll=2)
        _fill_shifted(buf, shifted, residues)

        def chunk(ci, carry):
            start = pl.multiple_of(ci * CONV_CHUNK, CONV_CHUNK)
            acc = jnp.broadcast_to(b_ref[...], (CONV_CHUNK, tc))
            for k in range(K):
                acc = acc + w_ref[k:k + 1, :] * _tap(buf, shifted, residues, offsets[k], start, CONV_CHUNK)
            outs[0][pl.ds(start, CONV_CHUNK), :] = acc.astype(BF16)
            if silu_out:
                outs[1][pl.ds(start, CONV_CHUNK), :] = _silu(acc).astype(BF16)
            return carry

        lax.fori_loop(0, tt // CONV_CHUNK, chunk, 0)

    in_specs, args = [], []
    for off in offs:
        assert off % tc == 0
        in_specs.append(pl.BlockSpec((tt, tc), lambda j, i, ob=off // tc: (i, ob + j)))
        in_specs.append(pl.BlockSpec((HALO, tc), lambda j, i, ob=off // tc: (jnp.maximum(i * per - 1, 0), ob + j)))
        args += [src, src]
    in_specs += [pl.BlockSpec((K, tc), lambda j, i: (0, j)), pl.BlockSpec((1, tc), lambda j, i: (0, j))]
    args += [w, b]
    n_out = 2 if silu_out else 1
    out = pl.pallas_call(
        body, name=name, grid=(C // tc, T // tt), in_specs=in_specs,
        out_specs=[pl.BlockSpec((tt, tc), lambda j, i: (i, j))] * n_out,
        out_shape=[jax.ShapeDtypeStruct((T, C), BF16)] * n_out,
        scratch_shapes=[pltpu.VMEM((HALO + tt, tc), F32), pltpu.VMEM((max(len(residues), 1), HALO + tt, tc), F32)],
        compiler_params=_cp("parallel", "arbitrary"))(*args)
    return out


def dwconv_bwd(du, u, src, offs, w, *, C, seq, glu, silu_out, name):
    T = src.shape[0]
    K = w.shape[0]
    assert K - 1 <= HALO
    tt, tc = _conv_tiles(seq, C, K)
    n_in = 2 if glu else 1
    per = tt // HALO
    last_blk = T // HALO - 1
    g_offsets = [K - 1 - k for k in range(K)]
    g_res = _residues(g_offsets)

    def body(*refs):
        pos = 0
        du_cur, du_nxt = refs[0], refs[1]
        pos = 2
        if silu_out:
            u_cur, u_nxt = refs[2], refs[3]
            pos = 4
        cur = refs[pos:pos + n_in]
        pos += n_in
        w_ref = refs[pos]
        outs = refs[pos + 1:pos + 1 + n_in]
        dw_ref, db_ref = refs[pos + 1 + n_in], refs[pos + 2 + n_in]
        gbuf, gshift, dw_acc, db_acc = refs[-4:]
        i = pl.program_id(1)
        last = ((i + 1) * tt) % seq == 0

        @pl.when(i == 0)
        def _():
            dw_acc[...] = jnp.zeros_like(dw_acc)
            db_acc[...] = jnp.zeros_like(db_acc)

        def build(ci, carry):
            rows = pl.ds(pl.multiple_of(ci * CONV_CHUNK, CONV_CHUNK), CONV_CHUNK)
            g = du_cur[rows, :].astype(F32)
            if silu_out:
                g = g * _silu_grad(u_cur[rows, :].astype(F32))
            gbuf[rows, :] = g
            return carry

        lax.fori_loop(0, tt // CONV_CHUNK, build, 0, unroll=2)
        g_nxt = du_nxt[...].astype(F32)
        if silu_out:
            g_nxt = g_nxt * _silu_grad(u_nxt[...].astype(F32))
        gbuf[tt:tt + HALO, :] = jnp.where(last, 0.0, g_nxt)
        _fill_shifted(gbuf, gshift, g_res)

        def fold(v):
            out = v[0:SUBLANES]
            for s in range(SUBLANES, CONV_CHUNK, SUBLANES):
                out = out + v[s:s + SUBLANES]
            return out

        def chunk(ci, carry):
            start = pl.multiple_of(ci * CONV_CHUNK, CONV_CHUNK)
            rows = pl.ds(start, CONV_CHUNK)
            a = cur[0][rows, :].astype(F32)
            if glu:
                s = _sigmoid(cur[1][rows, :].astype(F32))
                x_in = a * s
            else:
                x_in = a
            dx = jnp.zeros((CONV_CHUNK, tc), F32)
            for k in range(K):
                g_k = _tap(gbuf, gshift, g_res, g_offsets[k], start, CONV_CHUNK)
                dx = dx + w_ref[k:k + 1, :] * g_k
                dw_acc[k * SUBLANES:(k + 1) * SUBLANES, :] += fold(g_k * x_in)
            db_acc[...] += fold(gbuf[rows, :])
            if glu:
                outs[0][rows, :] = (dx * s).astype(BF16)
                outs[1][rows, :] = (dx * a * s * (1.0 - s)).astype(BF16)
            else:
                outs[0][rows, :] = dx.astype(BF16)
            return carry

        lax.fori_loop(0, tt // CONV_CHUNK, chunk, 0)

        @pl.when(i == T // tt - 1)
        def _():
            for k in range(K):
                dw_ref[k:k + 1, :] = jnp.sum(dw_acc[k * SUBLANES:(k + 1) * SUBLANES, :], axis=0, keepdims=True)
            db_ref[...] = jnp.sum(db_acc[...], axis=0, keepdims=True)

    def cur_spec(ob):
        return pl.BlockSpec((tt, tc), lambda j, i: (i, ob + j))

    def nxt_spec(ob):
        return pl.BlockSpec((HALO, tc), lambda j, i: (jnp.minimum((i + 1) * per, last_blk), ob + j))

    in_specs = [cur_spec(0), nxt_spec(0)]
    args = [du, du]
    if silu_out:
        in_specs += [cur_spec(0), nxt_spec(0)]
        args += [u, u]
    for off in offs:
        assert off % tc == 0
        in_specs.append(cur_spec(off // tc))
        args.append(src)
    in_specs.append(pl.BlockSpec((K, tc), lambda j, i: (0, j)))
    args.append(w)
    out_specs = [pl.BlockSpec((tt, tc), lambda j, i: (i, j))] * n_in
    out_specs += [pl.BlockSpec((K, tc), lambda j, i: (0, j)), pl.BlockSpec((1, tc), lambda j, i: (0, j))]
    out_shape = [jax.ShapeDtypeStruct((T, C), BF16)] * n_in
    out_shape += [jax.ShapeDtypeStruct((K, C), F32), jax.ShapeDtypeStruct((1, C), F32)]
    return pl.pallas_call(
        body, name=name, grid=(C // tc, T // tt), in_specs=in_specs, out_specs=out_specs, out_shape=out_shape,
        scratch_shapes=[pltpu.VMEM((tt + HALO, tc), F32), pltpu.VMEM((max(len(g_res), 1), tt + HALO, tc), F32),
                        pltpu.VMEM((K * SUBLANES, tc), F32), pltpu.VMEM((SUBLANES, tc), F32)],
        compiler_params=_cp("parallel", "arbitrary"))(*args)


def mix0_post_fwd(u2, proj, o, ln_w, ln_b, *, CW, gc_off, ga_off, name):
    T = u2.shape[0]
    tt = _pick(T, 256, 16)

    def body(u_ref, gc_ref, ga_ref, o_ref, lw_ref, lb_ref, y_ref):
        def chunk(ci, carry):
            rows = pl.ds(pl.multiple_of(ci * ROW_CHUNK, ROW_CHUNK), ROW_CHUNK)
            u = u_ref[rows, :].astype(F32)
            mu = jnp.mean(u, axis=-1, keepdims=True)
            xc = u - mu
            r = lax.rsqrt(jnp.mean(xc * xc, axis=-1, keepdims=True) + EPS)
            u3 = xc * r * lw_ref[...] + lb_ref[...]
            y_ref[rows, 0:CW] = (_silu(u3) * _silu(gc_ref[rows, :].astype(F32))).astype(BF16)
            y_ref[rows, CW:2 * CW] = (o_ref[rows, :].astype(F32) * _silu(ga_ref[rows, :].astype(F32))).astype(BF16)
            return carry

        lax.fori_loop(0, tt // ROW_CHUNK, chunk, 0, unroll=4)

    row = pl.BlockSpec((tt, CW), lambda i: (i, 0))
    vec = pl.BlockSpec((1, CW), lambda i: (0, 0))
    return pl.pallas_call(
        body, name=name, grid=(T // tt,),
        in_specs=[row, pl.BlockSpec((tt, CW), lambda i: (i, gc_off // CW)),
                  pl.BlockSpec((tt, CW), lambda i: (i, ga_off // CW)), row, vec, vec],
        out_specs=pl.BlockSpec((tt, 2 * CW), lambda i: (i, 0)),
        out_shape=jax.ShapeDtypeStruct((T, 2 * CW), BF16),
        compiler_params=_cp("parallel"))(u2, proj, proj, o, ln_w, ln_b)


def mix0_post_bwd(dy, u2, proj, o, ln_w, ln_b, *, CW, gc_off, ga_off, name):
    T = u2.shape[0]
    tt = _pick(T, 256, 16)

    def body(dy_ref, u_ref, gc_ref, ga_ref, o_ref, lw_ref, lb_ref, du_ref, dgc_ref, dga_ref, do_ref, dlw_ref, dlb_ref,
             lw_acc, lb_acc):
        i = pl.program_id(0)

        @pl.when(i == 0)
        def _():
            lw_acc[...] = jnp.zeros_like(lw_acc)
            lb_acc[...] = jnp.zeros_like(lb_acc)

        def fold(v):
            out = v[0:SUBLANES]
            for s in range(SUBLANES, ROW_CHUNK, SUBLANES):
                out = out + v[s:s + SUBLANES]
            return out

        def chunk(ci, carry):
            rows = pl.ds(pl.multiple_of(ci * ROW_CHUNK, ROW_CHUNK), ROW_CHUNK)
            dyc = dy_ref[rows, 0:CW].astype(F32)
            dya = dy_ref[rows, CW:2 * CW].astype(F32)
            u = u_ref[rows, :].astype(F32)
            mu = jnp.mean(u, axis=-1, keepdims=True)
            xc = u - mu
            r = lax.rsqrt(jnp.mean(xc * xc, axis=-1, keepdims=True) + EPS)
            xhat = xc * r
            u3 = xhat * lw_ref[...] + lb_ref[...]
            gc = gc_ref[rows, :].astype(F32)
            dgc_ref[rows, :] = (dyc * _silu(u3) * _silu_grad(gc)).astype(BF16)
            du3 = dyc * _silu(gc) * _silu_grad(u3)
            lw_acc[...] += fold(du3 * xhat)
            lb_acc[...] += fold(du3)
            dxh = du3 * lw_ref[...]
            du = r * (dxh - jnp.mean(dxh, axis=-1, keepdims=True)
                      - xhat * jnp.mean(dxh * xhat, axis=-1, keepdims=True))
            du_ref[rows, :] = du.astype(BF16)
            ga = ga_ref[rows, :].astype(F32)
            ov = o_ref[rows, :].astype(F32)
            do_ref[rows, :] = (dya * _silu(ga)).astype(BF16)
            dga_ref[rows, :] = (dya * ov * _silu_grad(ga)).astype(BF16)
            return carry

        lax.fori_loop(0, tt // ROW_CHUNK, chunk, 0, unroll=4)

        @pl.when(i == T // tt - 1)
        def _():
            dlw_ref[...] = jnp.sum(lw_acc[...], axis=0, keepdims=True)
            dlb_ref[...] = jnp.sum(lb_acc[...], axis=0, keepdims=True)

    row = pl.BlockSpec((tt, CW), lambda i: (i, 0))
    vec = pl.BlockSpec((1, CW), lambda i: (0, 0))
    big = jax.ShapeDtypeStruct((T, CW), BF16)
    small = jax.ShapeDtypeStruct((1, CW), F32)
    return pl.pallas_call(
        body, name=name, grid=(T // tt,),
        in_specs=[pl.BlockSpec((tt, 2 * CW), lambda i: (i, 0)), row,
                  pl.BlockSpec((tt, CW), lambda i: (i, gc_off // CW)),
                  pl.BlockSpec((tt, CW), lambda i: (i, ga_off // CW)), row, vec, vec],
        out_specs=[row, row, row, row, vec, vec],
        out_shape=[big, big, big, big, small, small],
        scratch_shapes=[pltpu.VMEM((SUBLANES, CW), F32), pltpu.VMEM((SUBLANES, CW), F32)],
        compiler_params=_cp("arbitrary"))(dy, u2, proj, proj, o, ln_w, ln_b)


SB_UNDERFLOW = 110.0
SB_BOUND_MARGIN = 1.02


def _sb_tile(seq):
    return _pick(seq, 256, LANE)


def _softplus(z):
    return jnp.maximum(z, 0.0) + jnp.log(1.0 + jnp.exp(-jnp.abs(z)))


def _tri01(n, lower):
    i = lax.broadcasted_iota(jnp.int32, (n, n), 0)
    j = lax.broadcasted_iota(jnp.int32, (n, n), 1)
    return ((i >= j) if lower else (i <= j)).astype(BF16)


SB_HEADS_FWD = 4
SB_HEADS_BWD = 2


def _sb_heads_per_step(heads, want):
    while heads % want:
        want //= 2
    return want


def sba_fwd(proj, *, B, seq, heads, q_off, k_off, v_off, name):
    dh = SB_HEAD_DIM
    tq = _sb_tile(seq)
    assert tq % (2 * LANE) == 0
    nq = seq // tq
    hps = _sb_heads_per_step(heads, SB_HEADS_FWD)
    hw = hps * dh
    scale = dh ** -0.5

    def body(q_ref, k_ref, v_ref, tri_ref, o_ref, ct_ref, acc_ref, kmax_ref):
        qi = pl.program_id(1)
        tri = tri_ref[...]
        qs = [(q_ref[:, h * dh:(h + 1) * dh].astype(F32) * scale).astype(BF16) for h in range(hps)]

        @pl.when(qi == 0)
        def _():
            def chunk(i, best):
                rows = k_ref[pl.ds(pl.multiple_of(i * tq, tq), tq), :].astype(F32)
                sq = rows * rows
                return tuple(jnp.maximum(best[h], jnp.max(jnp.sum(sq[:, h * dh:(h + 1) * dh], axis=1, keepdims=True),
                                                          axis=0, keepdims=True)) for h in range(hps))

            best = lax.fori_loop(0, nq, chunk, (jnp.zeros((1, 1), F32),) * hps)
            for h in range(hps):
                kmax_ref[h] = jnp.broadcast_to(jnp.sqrt(best[h]), (8, LANE))

        z_bound = [jnp.sqrt(jnp.sum(qs[h].astype(F32) ** 2, axis=1, keepdims=True))
                   * (SB_BOUND_MARGIN * jnp.max(kmax_ref[h], keepdims=True)) for h in range(hps)]

        def part(h, q_rows, start, n_keys, r, mask):
            k_blk = k_ref[pl.ds(start, n_keys), h * dh:(h + 1) * dh]
            v_blk = v_ref[pl.ds(start, n_keys), h * dh:(h + 1) * dh]
            z = _dot(q_rows, k_blk, NT)
            sp = _softplus(z)
            if mask is not None:
                sp = jnp.where(mask, sp, 0.0)
            wts = jnp.exp(z - (_dot(sp.astype(BF16), tri[0:n_keys, 0:n_keys]) + r))
            if mask is not None:
                wts = jnp.where(mask, wts, 0.0)
            return _dot(wts.astype(BF16), v_blk), r + jnp.sum(sp, axis=-1, keepdims=True)

        below = lax.broadcasted_iota(jnp.int32, (tq, tq), 1) < lax.broadcasted_iota(jnp.int32, (tq, tq), 0)
        has_left = qi > 0
        left = pl.multiple_of(jnp.maximum(qi - 1, 0) * tq, tq)
        rs = []
        for h in range(hps):
            pv_d, r = part(h, qs[h], pl.multiple_of(qi * tq, tq), tq, jnp.zeros((tq, 1), F32), below)
            pv_l, r = part(h, qs[h], left, tq, r, has_left)
            acc_ref[:, h * dh:(h + 1) * dh] = pv_d + pv_l
            rs.append(r)
        rs = tuple(rs)

        def block(start, rs):
            pvs, out = [], []
            for h in range(hps):
                pv, r = part(h, qs[h], start, tq, rs[h], None)
                pvs.append(pv)
                out.append(r)
            return pvs, tuple(out)

        def more(c):
            j, rs = c
            slack = rs[0] - z_bound[0]
            for h in range(1, hps):
                slack = jnp.minimum(slack, rs[h] - z_bound[h])
            return jnp.logical_and(j < qi, jnp.min(slack) <= SB_UNDERFLOW)

        def step(c):
            j, rs = c
            pvs, rs = block(pl.multiple_of((qi - 1 - j) * tq, tq), rs)
            for h in range(hps):
                acc_ref[:, h * dh:(h + 1) * dh] += pvs[h]
            return j + 1, rs

        n_left, totals = lax.while_loop(more, step, (has_left.astype(jnp.int32), rs))
        o_ref[...] = acc_ref[...].astype(BF16)
        for h in range(hps):
            ct_ref[0, 0, h, 0:8, :] = jnp.broadcast_to(totals[h], (tq, LANE)).T[0:8, :]
            ct_ref[0, 0, h, 8:16, :] = jnp.full((8, tq), n_left, F32)

    qb, kb, vb = q_off // hw, k_off // hw, v_off // hw
    G = heads // hps
    return pl.pallas_call(
        body, name=name, grid=(B * G, nq),
        in_specs=[pl.BlockSpec((tq, hw), lambda g, i: ((g // G) * nq + i, qb + g % G)),
                  pl.BlockSpec((seq, hw), lambda g, i: (g // G, kb + g % G)),
                  pl.BlockSpec((seq, hw), lambda g, i: (g // G, vb + g % G)),
                  pl.BlockSpec((tq, tq), lambda g, i: (0, 0))],
        out_specs=[pl.BlockSpec((tq, hw), lambda g, i: ((g // G) * nq + i, g % G)),
                   pl.BlockSpec((1, 1, hps, 16, tq), lambda g, i: (g // G, i, g % G, 0, 0))],
        out_shape=[jax.ShapeDtypeStruct((B * seq, heads * dh), BF16),
                   jax.ShapeDtypeStruct((B, nq, heads, 16, tq), F32)],
        scratch_shapes=[pltpu.VMEM((tq, hw), F32), pltpu.VMEM((hps, 8, LANE), F32)],
        compiler_params=_cp("parallel", "arbitrary"))(proj, proj, proj, jnp.tril(jnp.ones((tq, tq), BF16)))


def sba_bwd(proj, ctot, do, *, B, seq, heads, q_off, k_off, v_off, name, comm=None):
    dh = SB_HEAD_DIM
    tq = _sb_tile(seq)
    nq = seq // tq
    hps = _sb_heads_per_step(heads, SB_HEADS_BWD)
    hw = hps * dh
    scale = dh ** -0.5

    def body(q_ref, k_ref, v_ref, ct_ref, do_ref, sfx_ref, pre_ref, dq_ref, dk_ref, dv_ref, dq_acc, dk_acc, dv_acc):
        qi = pl.program_id(1)

        @pl.when(qi == 0)
        def _():
            dk_acc[...] = jnp.zeros_like(dk_acc)
            dv_acc[...] = jnp.zeros_like(dv_acc)

        tri_sfx = sfx_ref[...]
        tri_pre = pre_ref[...]
        qs = [(q_ref[:, h * dh:(h + 1) * dh].astype(F32) * scale).astype(BF16) for h in range(hps)]
        dos = [do_ref[:, h * dh:(h + 1) * dh] for h in range(hps)]
        totals = [jnp.max(jnp.broadcast_to(ct_ref[0, 0, h, 0:1, :], (LANE, tq)).T, axis=1, keepdims=True)
                  for h in range(hps)]
        dq_acc[...] = jnp.zeros_like(dq_acc)

        def part(h, rows, start, n_keys, pc, pg, mask):
            cols = slice(h * dh, (h + 1) * dh)
            q_rows, do_rows = qs[h][rows], dos[h][rows]
            k_blk = k_ref[pl.ds(start, n_keys), cols]
            v_blk = v_ref[pl.ds(start, n_keys), cols]
            z = _dot(q_rows, k_blk, NT)
            sp = _softplus(z)
            sig = jnp.exp(z - sp)
            if mask is not None:
                sp = jnp.where(mask, sp, 0.0)
            pc_next = pc + jnp.sum(sp, axis=-1, keepdims=True)
            wts = jnp.exp(z - (_dot(sp.astype(BF16), tri_sfx[0:n_keys, 0:n_keys]) + (totals[h][rows] - pc_next)))
            if mask is not None:
                wts = jnp.where(mask, wts, 0.0)
            g = _dot(do_rows, v_blk, NT) * wts
            dz = g - sig * (_dot(g.astype(BF16), tri_pre[0:n_keys, 0:n_keys]) + pg)
            if mask is not None:
                dz = jnp.where(mask, dz, 0.0)
            dz = dz.astype(BF16)
            dq_acc[rows, cols] += _dot(dz, k_blk)
            dk_acc[pl.ds(start, n_keys), cols] += _dot(dz, q_rows, TN)
            dv_acc[pl.ds(start, n_keys), cols] += _dot(wts.astype(BF16), do_rows, TN)
            return pc_next, pg + jnp.sum(g, axis=-1, keepdims=True)

        def block(start, carry):
            return tuple(part(h, slice(0, tq), start, tq, carry[h][0], carry[h][1], None) for h in range(hps))

        zero = jnp.zeros((tq, 1), F32)
        n_left = jnp.max(ct_ref[0, 0, 0, 8:16, :]).astype(jnp.int32)
        carry = lax.fori_loop(qi - n_left, qi - 1, lambda j, c: block(pl.multiple_of(j * tq, tq), c),
                              ((zero, zero),) * hps)
        below = lax.broadcasted_iota(jnp.int32, (tq, tq), 1) < lax.broadcasted_iota(jnp.int32, (tq, tq), 0)
        has_left = n_left > 0
        left = pl.multiple_of(jnp.maximum(qi - 1, 0) * tq, tq)
        for h in range(hps):
            pc, pg = part(h, slice(0, tq), left, tq, carry[h][0], carry[h][1], has_left)
            part(h, slice(0, tq), pl.multiple_of(qi * tq, tq), tq, pc, pg, below)
        dq_ref[...] = (dq_acc[...] * scale).astype(BF16)

        @pl.when(qi == nq - 1)
        def _():
            dk_ref[...] = dk_acc[...].astype(BF16)
            dv_ref[...] = dv_acc[...].astype(BF16)

    qb, kb, vb = q_off // hw, k_off // hw, v_off // hw
    G = heads // hps
    q_spec = pl.BlockSpec((tq, hw), lambda g, i: ((g // G) * nq + i, qb + g % G))
    o_spec = pl.BlockSpec((tq, hw), lambda g, i: ((g // G) * nq + i, g % G))
    kv_out = pl.BlockSpec((seq, hw), lambda g, i: (g // G, g % G))
    shp = jax.ShapeDtypeStruct((B * seq, heads * dh), BF16)
    body, c_in, c_args, c_out, c_shape, c_scratch = _hosted(body, 7, 3, comm, _grid_step(nq), B * G * nq)
    tri_spec = pl.BlockSpec((tq, tq), lambda g, i: (0, 0))
    ones = jnp.ones((tq, tq), BF16)
    out = pl.pallas_call(
        body, name=name, grid=(B * G, nq),
        in_specs=[q_spec,
                  pl.BlockSpec((seq, hw), lambda g, i: (g // G, kb + g % G)),
                  pl.BlockSpec((seq, hw), lambda g, i: (g // G, vb + g % G)),
                  pl.BlockSpec((1, 1, hps, 16, tq), lambda g, i: (g // G, i, g % G, 0, 0)), o_spec,
                  tri_spec, tri_spec] + c_in,
        out_specs=[o_spec, kv_out, kv_out] + c_out, out_shape=[shp, shp, shp] + c_shape,
        scratch_shapes=[pltpu.VMEM((tq, hw), F32), pltpu.VMEM((seq, hw), F32), pltpu.VMEM((seq, hw), F32)]
        + c_scratch,
        compiler_params=_cp("arbitrary" if comm else "parallel", "arbitrary"))(
            proj, proj, proj, ctot, do, jnp.tril(ones), jnp.triu(ones), *c_args)
    return (out[0], out[1], out[2], out[3:]) if comm else out


def _head_expand(n_heads, DI):
    j = jnp.arange(LANE, dtype=jnp.int32)[:, None]
    c = jnp.arange(DI, dtype=jnp.int32)[None, :] // SSM_HEAD_DIM
    return ((j == c) & (j < n_heads)).astype(BF16)


def _split3(x):
    hi = x.astype(BF16)
    r1 = x - hi.astype(F32)
    mid = r1.astype(BF16)
    return hi, mid, (r1 - mid.astype(F32)).astype(BF16)


def dt_fwd(proj, bias, a_log, expand, *, dt_off, name):
    T = proj.shape[0]
    DI = expand.shape[1]
    L = SSM_CHUNK
    tt = _pick(T, 512, L)

    def body(raw_ref, bias_ref, al_ref, e_ref, dt_ref, cs_ref, dtx_ref, csx_ref):
        x = raw_ref[...].astype(F32) + bias_ref[...]
        dt = _softplus(x)
        dt_ref[...] = dt
        la = dt * (-jnp.exp(al_ref[...]))
        tri = _tri01(L, True)
        for c in range(tt // L):
            cs_ref[c * L:(c + 1) * L, :] = _tri_dot3(tri, la[c * L:(c + 1) * L, :])
        e = e_ref[...]
        dtx_ref[...] = _dot(dt.astype(BF16), e).astype(BF16)
        hi, mid, lo = _split3(cs_ref[...])
        csx_ref[...] = _dot(hi, e) + _dot(mid, e) + _dot(lo, e)

    row = pl.BlockSpec((tt, LANE), lambda i: (i, 0))
    wide = pl.BlockSpec((tt, DI), lambda i: (i, 0))
    vec = pl.BlockSpec((1, LANE), lambda i: (0, 0))
    return pl.pallas_call(
        body, name=name, grid=(T // tt,),
        in_specs=[pl.BlockSpec((tt, LANE), lambda i: (i, dt_off // LANE)), vec, vec,
                  pl.BlockSpec((LANE, DI), lambda i: (0, 0))],
        out_specs=[row, row, wide, wide],
        out_shape=[jax.ShapeDtypeStruct((T, LANE), F32), jax.ShapeDtypeStruct((T, LANE), F32),
                   jax.ShapeDtypeStruct((T, DI), BF16), jax.ShapeDtypeStruct((T, DI), F32)],
        compiler_params=_cp("parallel"))(proj, bias, a_log, expand)


def dt_bwd(ddt_x, dcs_x, dcs_cols, proj, dt, bias, a_log, reduce_t, *, dt_off, n_heads, name):
    T = proj.shape[0]
    DI = reduce_t.shape[0]
    L = SSM_CHUNK
    tt = _pick(T, 512, L)

    def body(ddtx_ref, dcsx_ref, dcsc_ref, raw_ref, dt_ref, bias_ref, al_ref, r_ref, draw_ref, dbias_ref, dal_ref,
             dla_buf):
        @pl.when(pl.program_id(0) == 0)
        def _():
            dbias_ref[...] = jnp.zeros_like(dbias_ref)
            dal_ref[...] = jnp.zeros_like(dal_ref)

        r = r_ref[...]
        ddt = _dot(ddtx_ref[...], r)
        dx = dcsx_ref[...]
        hi = dx.astype(BF16)
        dcs = _dot(hi, r) + _dot((dx - hi.astype(F32)).astype(BF16), r) + dcsc_ref[...]
        triu = _tri01(L, False)
        for c in range(tt // L):
            dla_buf[c * L:(c + 1) * L, :] = _tri_dot3(triu, dcs[c * L:(c + 1) * L, :])
        dla = dla_buf[...]
        a = -jnp.exp(al_ref[...])
        valid = lax.broadcasted_iota(jnp.int32, (tt, LANE), 1) < n_heads
        dal_ref[...] += jnp.sum(jnp.where(valid, dla * dt_ref[...], 0.0), axis=0, keepdims=True) * a
        x = raw_ref[...].astype(F32) + bias_ref[...]
        draw = jnp.where(valid, (ddt + dla * a) * _sigmoid(x), 0.0)
        dbias_ref[...] += jnp.sum(draw, axis=0, keepdims=True)
        draw_ref[...] = draw.astype(BF16)

    row = pl.BlockSpec((tt, LANE), lambda i: (i, 0))
    wide = pl.BlockSpec((tt, DI), lambda i: (i, 0))
    vec = pl.BlockSpec((1, LANE), lambda i: (0, 0))
    return pl.pallas_call(
        body, name=name, grid=(T // tt,),
        in_specs=[wide, wide, row, pl.BlockSpec((tt, LANE), lambda i: (i, dt_off // LANE)), row, vec, vec,
                  pl.BlockSpec((DI, LANE), lambda i: (0, 0))],
        out_specs=[row, vec, vec],
        out_shape=[jax.ShapeDtypeStruct((T, LANE), BF16), jax.ShapeDtypeStruct((1, LANE), F32),
                   jax.ShapeDtypeStruct((1, LANE), F32)],
        scratch_shapes=[pltpu.VMEM((tt, LANE), F32)],
        compiler_params=_cp("arbitrary"))(ddt_x, dcs_x, dcs_cols, proj, dt, bias, a_log, reduce_t)


def _pair_terms(x_ref, dtx_ref, csx_ref, csr_ref, pair, ppg, lo_half, causal):
    L = SSM_CHUNK
    g, pp = divmod(pair, ppg)
    ra = g * HEAD_ROWS + 2 * pp
    cols = slice(pair * LANE, (pair + 1) * LANE)
    X = x_ref[:, cols].astype(F32)
    dt_p = dtx_ref[:, cols].astype(F32)
    own = csx_ref[:, cols]
    other = pltpu.roll(own, SSM_HEAD_DIM, 1)
    csa_c = jnp.where(lo_half, own, other)
    csb_c = jnp.where(lo_half, other, own)
    La = jnp.exp(jnp.where(causal, csa_c - csr_ref[0, ra:ra + 1, :], NEG_BIG))
    Lb = jnp.exp(jnp.where(causal, csb_c - csr_ref[0, ra + 1:ra + 2, :], NEG_BIG))
    last = csx_ref[L - 1:L, cols]
    return g, ra, cols, X, dt_p, La, Lb, jnp.exp(own), jnp.exp(last - own), jnp.exp(last)


def scan_fwd(xbc, dt_x, cs_x, cs_row, d_full, *, B, seq, DI, name):
    L, N, G = SSM_CHUNK, SSM_STATE, SSM_GROUPS
    nc = seq // L
    XW = xbc.shape[1]
    n_pairs = DI // LANE
    ppg = n_pairs // G

    def body(x_ref, dtx_ref, csx_ref, csr_ref, d_ref, y_ref, st_ref, state):
        c = pl.program_id(1)

        @pl.when(c == 0)
        def _():
            state[...] = jnp.zeros_like(state)

        causal = lax.broadcasted_iota(jnp.int32, (L, L), 0) >= lax.broadcasted_iota(jnp.int32, (L, L), 1)
        lo_half = lax.broadcasted_iota(jnp.int32, (L, LANE), 1) < SSM_HEAD_DIM
        cbs = []
        for g in range(G):
            Bc = x_ref[:, DI + g * N:DI + (g + 1) * N]
            Cc = x_ref[:, DI + G * N + g * N:DI + G * N + (g + 1) * N]
            cbs.append((Bc, Cc, _dot(Cc, Bc, NT)))
        for pair in range(n_pairs):
            g, _, cols, X, dt_p, La, Lb, ecs, tail, e_last = _pair_terms(
                x_ref, dtx_ref, csx_ref, csr_ref, pair, ppg, lo_half, causal)
            Bc, Cc, CB = cbs[g]
            xs = X * dt_p
            xsb = xs.astype(BF16)
            y = jnp.where(lo_half, _dot((CB * La).astype(BF16), xsb), _dot((CB * Lb).astype(BF16), xsb))
            ST = state[pair]
            st_ref[0, 0, pair] = ST
            y = y + ecs * _dot(Cc, ST.astype(BF16)) + d_ref[:, cols] * X
            y_ref[:, cols] = y.astype(BF16)
            state[pair] = e_last * ST + _dot(Bc, (xs * tail).astype(BF16), TN)

    wide = pl.BlockSpec((L, DI), lambda b, c: (b * nc + c, 0))
    return pl.pallas_call(
        body, name=name, grid=(B, nc),
        in_specs=[pl.BlockSpec((L, XW), lambda b, c: (b * nc + c, 0)), wide, wide,
                  pl.BlockSpec((1, G * HEAD_ROWS, L), lambda b, c: (b, 0, c)),
                  pl.BlockSpec((1, DI), lambda b, c: (0, 0))],
        out_specs=[wide, pl.BlockSpec((1, 1, n_pairs, N, LANE), lambda b, c: (b, c, 0, 0, 0))],
        out_shape=[jax.ShapeDtypeStruct((B * seq, DI), BF16),
                   jax.ShapeDtypeStruct((B, nc, n_pairs, N, LANE), F32)],
        scratch_shapes=[pltpu.VMEM((n_pairs, N, LANE), F32)],
        compiler_params=_cp("parallel", "arbitrary"))(xbc, dt_x, cs_x, cs_row, d_full)


def scan_bwd(xbc, dt_x, cs_x, cs_row, d_full, states, dy, *, B, seq, DI, name):
    L, N, G = SSM_CHUNK, SSM_STATE, SSM_GROUPS
    nc = seq // L
    XW = xbc.shape[1]
    n_pairs = DI // LANE
    ppg = n_pairs // G
    HR = G * HEAD_ROWS
    inv_p = 1.0 / SSM_HEAD_DIM

    def body(x_ref, dtx_ref, csx_ref, csr_ref, d_ref, st_ref, dy_ref, dx_ref, ddtx_ref, dcsx_ref, dcsr_ref, dd_ref,
             dH):
        c = pl.program_id(1)

        @pl.when(c == 0)
        def _():
            dH[...] = jnp.zeros_like(dH)
            dd_ref[...] = jnp.zeros_like(dd_ref)

        causal = lax.broadcasted_iota(jnp.int32, (L, L), 0) >= lax.broadcasted_iota(jnp.int32, (L, L), 1)
        lo_half = lax.broadcasted_iota(jnp.int32, (L, LANE), 1) < SSM_HEAD_DIM
        last_row = lax.broadcasted_iota(jnp.int32, (L, LANE), 0) == L - 1
        head_row = lax.broadcasted_iota(jnp.int32, (HR, 1), 0)
        dcs_rows = jnp.zeros((HR, L), F32)

        for g in range(G):
            Bc = x_ref[:, DI + g * N:DI + (g + 1) * N]
            Cc = x_ref[:, DI + G * N + g * N:DI + G * N + (g + 1) * N]
            CB = _dot(Cc, Bc, NT)
            dCB = jnp.zeros((L, L), F32)
            dC = jnp.zeros((L, N), F32)
            dB = jnp.zeros((L, N), F32)
            for pp in range(ppg):
                pair = g * ppg + pp
                _, ra, cols, X, dt_p, La, Lb, ecs, tail, e_last = _pair_terms(
                    x_ref, dtx_ref, csx_ref, csr_ref, pair, ppg, lo_half, causal)
                xs = X * dt_p
                xsb = xs.astype(BF16)
                Ma, Mb = CB * La, CB * Lb
                dY = dy_ref[:, cols].astype(F32)
                dYb = dY.astype(BF16)
                dMa = _dot(jnp.where(lo_half, dY, 0.0).astype(BF16), xsb, NT)
                dMb = _dot(jnp.where(lo_half, 0.0, dY).astype(BF16), xsb, NT)
                dSa, dSb = dMa * Ma, dMb * Mb
                dCB = dCB + dMa * La + dMb * Lb
                dcs = jnp.where(lo_half, jnp.sum(dSa, axis=1, keepdims=True), jnp.sum(dSb, axis=1, keepdims=True)) * inv_p
                dcs_rows = dcs_rows - jnp.where(head_row == ra, jnp.sum(dSa, axis=0, keepdims=True), 0.0)
                dcs_rows = dcs_rows - jnp.where(head_row == ra + 1, jnp.sum(dSb, axis=0, keepdims=True), 0.0)
                dxs = jnp.where(lo_half, _dot(Ma.astype(BF16), dYb, TN), _dot(Mb.astype(BF16), dYb, TN))
                ST = st_ref[0, 0, pair]
                STb = ST.astype(BF16)
                dYe = (dY * ecs).astype(BF16)
                dC = dC + _dot(dYe, STb, NT)
                dSTp = _dot(Cc, dYe, TN)
                dcs = dcs + dY * (ecs * _dot(Cc, STb))
                dSTn = dH[pair]
                dSTnb = dSTn.astype(BF16)
                dSTp = dSTp + e_last * dSTn
                XBt = _dot(Bc, dSTnb)
                dxs = dxs + tail * XBt
                t2 = xs * XBt * tail
                at_end = e_last * jnp.sum(dSTn * ST, axis=0, keepdims=True) + jnp.sum(t2, axis=0, keepdims=True)
                dcs = dcs - t2 + jnp.where(last_row, at_end, 0.0)
                dB = dB + _dot((xs * tail).astype(BF16), dSTnb, NT)
                dx_ref[:, cols] = (dxs * dt_p + d_ref[:, cols] * dY).astype(BF16)
                ddtx_ref[:, cols] = (dxs * X).astype(BF16)
                dcsx_ref[:, cols] = dcs
                dd_ref[0, :, cols] += jnp.sum(dY * X, axis=0, keepdims=True)
                dH[pair] = dSTp
            dCBb = dCB.astype(BF16)
            dx_ref[:, DI + g * N:DI + (g + 1) * N] = (dB + _dot(dCBb, Cc, TN)).astype(BF16)
            dx_ref[:, DI + G * N + g * N:DI + G * N + (g + 1) * N] = (dC + _dot(dCBb, Bc)).astype(BF16)
        dcsr_ref[0] = dcs_rows

    rev = lambda b, c: (b * nc + (nc - 1 - c), 0)
    wide = pl.BlockSpec((L, DI), rev)
    hrow = pl.BlockSpec((1, HR, L), lambda b, c: (b, 0, nc - 1 - c))
    return pl.pallas_call(
        body, name=name, grid=(B, nc),
        in_specs=[pl.BlockSpec((L, XW), rev), wide, wide, hrow,
                  pl.BlockSpec((1, DI), lambda b, c: (0, 0)),
                  pl.BlockSpec((1, 1, n_pairs, N, LANE), lambda b, c: (b, nc - 1 - c, 0, 0, 0)), wide],
        out_specs=[pl.BlockSpec((L, XW), rev), wide, wide, hrow, pl.BlockSpec((1, 1, DI), lambda b, c: (b, 0, 0))],
        out_shape=[jax.ShapeDtypeStruct((B * seq, XW), BF16), jax.ShapeDtypeStruct((B * seq, DI), BF16),
                   jax.ShapeDtypeStruct((B * seq, DI), F32), jax.ShapeDtypeStruct((B, HR, seq), F32),
                   jax.ShapeDtypeStruct((B, 1, DI), F32)],
        scratch_shapes=[pltpu.VMEM((n_pairs, N, LANE), F32)],
        compiler_params=_cp("parallel", "arbitrary"))(xbc, dt_x, cs_x, cs_row, d_full, states, dy)


def gnorm_fwd(y, proj, w, *, DI, name):
    T = y.shape[0]
    tt = _pick(T, 256, 16)
    gw = DI // SSM_GROUPS

    def body(y_ref, z_ref, w_ref, o_ref):
        for g in range(SSM_GROUPS):
            sl = slice(g * gw, (g + 1) * gw)
            y2 = y_ref[:, sl].astype(F32) * _silu(z_ref[:, sl].astype(F32))
            r = lax.rsqrt(jnp.mean(y2 * y2, axis=-1, keepdims=True) + EPS)
            o_ref[:, sl] = (y2 * r * w_ref[:, sl]).astype(BF16)

    row = pl.BlockSpec((tt, DI), lambda i: (i, 0))
    return pl.pallas_call(
        body, name=name, grid=(T // tt,),
        in_specs=[row, row, pl.BlockSpec((1, DI), lambda i: (0, 0))], out_specs=row,
        out_shape=jax.ShapeDtypeStruct((T, DI), BF16), compiler_params=_cp("parallel"))(y, proj, w)


def gnorm_bwd(dyn, y, proj, w, *, DI, name):
    T = y.shape[0]
    tt = _pick(T, 256, 16)
    gw = DI // SSM_GROUPS

    def body(dyn_ref, y_ref, z_ref, w_ref, dy_ref, dz_ref, dw_ref):
        @pl.when(pl.program_id(0) == 0)
        def _():
            dw_ref[...] = jnp.zeros_like(dw_ref)

        for g in range(SSM_GROUPS):
            sl = slice(g * gw, (g + 1) * gw)
            yv = y_ref[:, sl].astype(F32)
            z = z_ref[:, sl].astype(F32)
            sz = _silu(z)
            y2 = yv * sz
            r = lax.rsqrt(jnp.mean(y2 * y2, axis=-1, keepdims=True) + EPS)
            xhat = y2 * r
            d = dyn_ref[:, sl].astype(F32)
            dw_ref[:, sl] += jnp.sum(d * xhat, axis=0, keepdims=True)
            dxh = d * w_ref[:, sl]
            dy2 = r * (dxh - xhat * jnp.mean(dxh * xhat, axis=-1, keepdims=True))
            dy_ref[:, sl] = (dy2 * sz).astype(BF16)
            dz_ref[:, sl] = (dy2 * yv * _silu_grad(z)).astype(BF16)

    row = pl.BlockSpec((tt, DI), lambda i: (i, 0))
    vec = pl.BlockSpec((1, DI), lambda i: (0, 0))
    shp = jax.ShapeDtypeStruct((T, DI), BF16)
    return pl.pallas_call(
        body, name=name, grid=(T // tt,), in_specs=[row, row, row, vec], out_specs=[row, row, vec],
        out_shape=[shp, shp, jax.ShapeDtypeStruct((1, DI), F32)],
        compiler_params=_cp("arbitrary"))(dyn, y, proj, w)


N_CHIP = 4


def _comm_out_shapes(srcs, modes):
    return [jax.ShapeDtypeStruct(((N_DEV,) if mode in ("gather", "gather_direct") else ()) + s.shape, s.dtype)
            for s, mode in zip(srcs, modes)]


def _comm_scratch(n):
    return [pltpu.SemaphoreType.DMA((n, N_DEV - 1)), pltpu.SemaphoreType.DMA((n, N_DEV - 1)),
            pltpu.SemaphoreType.DMA((n,))]


def _comm_phases(modes, src_refs, out_refs, send_sems, recv_sems, local_sems):
    x, y, c = lax.axis_index("x"), lax.axis_index("y"), lax.axis_index("c")
    me, sibling = (x, y, c), (x, y, 1 - c)
    chips = [(1 - x, y), (x, 1 - y), (1 - x, 1 - y)]
    relays = [a for a, mode in enumerate(modes) if mode == "gather"]

    def slot(p):
        return 4 * p[0] + 2 * p[1] + p[2]

    def remote(a, k, src, dst, to):
        return pltpu.make_async_remote_copy(src_ref=src, dst_ref=dst, send_sem=send_sems.at[a, k],
                                            recv_sem=recv_sems.at[a, k], device_id=to,
                                            device_id_type=pl.DeviceIdType.MESH)

    def first_copies():
        local, two_way, send_only = [], [], []
        for a, mode in enumerate(modes):
            src, out = src_refs[a], out_refs[a]
            if mode == "sibling":
                two_way.append(remote(a, 0, src, out, sibling))
            elif mode == "chips":
                mine = 2 * x + y
                local.append(pltpu.make_async_copy(src.at[mine], out.at[mine], local_sems.at[a]))
                for j, chip in enumerate(chips):
                    two_way.append(remote(a, 1 + j, src.at[2 * chip[0] + chip[1]], out.at[mine], (*chip, c)))
            elif mode == "gather_direct":
                local.append(pltpu.make_async_copy(src, out.at[slot(me)], local_sems.at[a]))
                for k in range(1, N_DEV):
                    peer = (1 - x if k & 4 else x, 1 - y if k & 2 else y, 1 - c if k & 1 else c)
                    two_way.append(remote(a, k - 1, src, out.at[slot(me)], peer))
            else:
                assert mode == "gather"
                local.append(pltpu.make_async_copy(src, out.at[slot(me)], local_sems.at[a]))
                send_only.append(remote(a, 0, src, out.at[slot(me)], sibling))
                for j, chip in enumerate(chips):
                    send_only.append(remote(a, 1 + j, src, out.at[slot(me)], (*chip, c)))
        return local, two_way, send_only

    def forwards():
        out = []
        for a in relays:
            for j, chip in enumerate(chips):
                landed = out_refs[a].at[slot((*chip, c))]
                out.append((remote(a, 1 + j, landed, landed, me), remote(a, 4 + j, landed, landed, sibling)))
        return out

    def start():
        local, two_way, send_only = first_copies()
        for cp in local + two_way + send_only:
            cp.start()

    def relay():
        for arrival, fwd in forwards():
            arrival.wait_recv()
            fwd.start()

    def finish():
        local, two_way, send_only = first_copies()
        for a in relays:
            blk = out_refs[a].at[slot(sibling)]
            remote(a, 0, blk, blk, me).wait_recv()
            for j, chip in enumerate(chips):
                blk = out_refs[a].at[slot((*chip, 1 - c))]
                remote(a, 4 + j, blk, blk, me).wait_recv()
        for cp in send_only + [fwd for _, fwd in forwards()]:
            cp.wait_send()
        for cp in two_way + local:
            cp.wait()

    return start, relay, finish, bool(relays)


def _hosted(body, n_in, n_out, comm, step, n_steps):
    if comm is None:
        return body, [], [], [], [], []
    srcs, modes = comm
    nc = len(srcs)

    def wrapped(*refs):
        ins, csrc = refs[:n_in], refs[n_in:n_in + nc]
        outs = refs[n_in + nc:n_in + nc + n_out]
        cout = refs[n_in + nc + n_out:n_in + 2 * nc + n_out]
        scratch = refs[n_in + 2 * nc + n_out:len(refs) - 3]
        start, relay, finish, has_relay = _comm_phases(modes, csrc, cout, *refs[len(refs) - 3:])
        s = step()
        pl.when(s == 0)(start)
        body(*ins, *outs, *scratch)
        if has_relay:
            pl.when(s == (2 * n_steps) // 3)(relay)
        pl.when(s == n_steps - 1)(finish)

    any_spec = pl.BlockSpec(memory_space=pl.ANY)
    return wrapped, [any_spec] * nc, list(srcs), [any_spec] * nc, _comm_out_shapes(srcs, modes), _comm_scratch(nc)


def exchange(srcs, modes, *, name):
    n = len(srcs)

    def body(*refs):
        start, relay, finish, has_relay = _comm_phases(modes, refs[:n], refs[n:2 * n], *refs[2 * n:])
        start()
        if has_relay:
            relay()
        finish()

    any_spec = pl.BlockSpec(memory_space=pl.ANY)
    return pl.pallas_call(
        body, name=name, in_specs=[any_spec] * n, out_specs=[any_spec] * n, out_shape=_comm_out_shapes(srcs, modes),
        scratch_shapes=_comm_scratch(n), compiler_params=pltpu.CompilerParams(has_side_effects=True))(*srcs)


def pair_sum(a, b, *, name):
    n, R, C = a.shape
    tr = _pick(n * R, 512, 16)

    def body(a_ref, b_ref, o_ref):
        o_ref[...] = (a_ref[...].astype(F32) + b_ref[...].astype(F32)).astype(BF16)

    blk = pl.BlockSpec((tr, C), lambda i: (i, 0))
    out = pl.pallas_call(
        body, name=name, grid=(n * R // tr,), in_specs=[blk, blk], out_specs=blk,
        out_shape=jax.ShapeDtypeStruct((n * R, C), BF16),
        compiler_params=_cp("parallel"))(a.reshape(n * R, C), b.reshape(n * R, C))
    return out.reshape(n, R, C)


def sum_slots(recv, *, name):
    _, R, C = recv.shape
    tr = _pick(R, 512, 8)

    def body(r_ref, o_ref):
        acc = r_ref[0].astype(F32)
        for p in range(1, N_DEV):
            acc = acc + r_ref[p].astype(F32)
        o_ref[...] = acc

    return pl.pallas_call(
        body, name=name, grid=(R // tr,),
        in_specs=[pl.BlockSpec((N_DEV, tr, C), lambda i: (0, i, 0))],
        out_specs=pl.BlockSpec((tr, C), lambda i: (i, 0)),
        out_shape=jax.ShapeDtypeStruct((R, C), F32), compiler_params=_cp("parallel"))(recv)


def adamw(gsrc, w, m, v, *, name):
    slots, R, C = gsrc.shape
    tr = _pick(R, 256, 16 if gsrc.dtype == BF16 else 8)
    c1 = 1.0 / (1.0 - ADAM_B1 ** ADAM_STEP)
    c2 = 1.0 / (1.0 - ADAM_B2 ** ADAM_STEP)

    def body(g_ref, w_ref, m_ref, v_ref, go_ref, d_ref, mo_ref, vo_ref):
        g = g_ref[0].astype(F32)
        for p in range(1, slots):
            g = g + g_ref[p].astype(F32)
        m2 = ADAM_B1 * m_ref[...] + (1.0 - ADAM_B1) * g
        v2 = ADAM_B2 * v_ref[...] + (1.0 - ADAM_B2) * (g * g)
        go_ref[...] = g
        mo_ref[...] = m2
        vo_ref[...] = v2
        d_ref[...] = -ADAM_LR * ((m2 * c1) / (jnp.sqrt(v2 * c2) + ADAM_EPS) + ADAM_WD * w_ref[...])

    blk = pl.BlockSpec((tr, C), lambda i: (i, 0))
    shp = jax.ShapeDtypeStruct((R, C), F32)
    return pl.pallas_call(
        body, name=name, grid=(R // tr,),
        in_specs=[pl.BlockSpec((slots, tr, C), lambda i: (0, i, 0)), blk, blk, blk],
        out_specs=[blk] * 4, out_shape=[shp] * 4, compiler_params=_cp("parallel"))(gsrc, w, m, v)


def _pad_cols(a, n):
    return jnp.pad(a, ((0, 0), (0, n - a.shape[1])))


def _to_rows(a, B, seq, H):
    G = SSM_GROUPS
    R = H // G
    t = a[:, :H].reshape(B, seq, G, R).transpose(0, 2, 3, 1)
    t = jnp.pad(t, ((0, 0), (0, 0), (0, HEAD_ROWS - R), (0, 0)))
    return t.reshape(B, G * HEAD_ROWS, seq)


def _from_rows(a, B, seq, H):
    G = SSM_GROUPS
    R = H // G
    t = a.reshape(B, G, HEAD_ROWS, seq)[:, :, :R].transpose(0, 3, 1, 2).reshape(B * seq, H)
    return _pad_cols(t, LANE)


def _chip_sums(grads, name):
    c_idx = lax.axis_index("c")
    keep, give = [], []
    for g in grads:
        by_chip = g.reshape((N_CHIP, 2) + g.shape[1:])
        keep.append(lax.dynamic_index_in_dim(by_chip, c_idx, axis=1, keepdims=False))
        give.append(lax.dynamic_index_in_dim(by_chip, 1 - c_idx, axis=1, keepdims=False))
    swapped = exchange(give, ["sibling"] * len(give), name="swap_" + name)
    return [pair_sum(k, s, name=f"chip_sum_{name}_{i}") for i, (k, s) in enumerate(zip(keep, swapped))]


def local_step(x, target, loc, od_w_in_t, *, B, seq):
    T, D = x.shape
    CW = D
    heads = CW // SB_HEAD_DIM
    DI = 2 * D
    H = DI // SSM_HEAD_DIM
    XW = DI + 2 * SSM_GROUPS * SSM_STATE
    in_odd = DI + XW + H
    w1_rows = in_odd // N_DEV
    q_off, k_off, v_off, gc_off, ga_off = 3 * CW, 4 * CW, 5 * CW, 2 * CW, 6 * CW
    dt_off = DI + XW

    small_packed, small_spans = _pack_rows([loc[n] for n in SMALL_SHARDED], LANE, 8)
    n0, (g_ev_in, small_all) = rmsnorm_fwd(x, loc["ev_norm_w"], name="l0_norm",
                                           comm=([loc["ev_w_in"].astype(BF16), small_packed],
                                                 ["gather", "gather_direct"]))
    p = {n: loc[n] for n in SMALL}
    for n, a in zip(SMALL_SHARDED, _unpack_rows(small_all, small_spans)):
        p[n] = _col_unshards(a)
    p["ev_w_in"] = _col_unshards(g_ev_in)
    proj0, (g_od_in_t, g_od_out, g_ev_out) = mm_nn(
        n0, p["ev_w_in"], out_dtype=BF16, name="l0_in_proj",
        comm=([od_w_in_t.astype(BF16), loc["od_w_out"].astype(BF16), loc["ev_w_out"].astype(BF16)], ["gather"] * 3))
    p["ev_w_out"] = g_ev_out.reshape(-1, D)
    w1t = g_od_in_t[:, :w1_rows].reshape(in_odd, D)
    w1t = jnp.pad(w1t, ((0, -(-(in_odd + LANE) // 256) * 256 - in_odd), (0, 0)))
    od_w_out = g_od_out.reshape(-1, D)
    (u2,) = dwconv_fwd(proj0, (0, CW), p["ev_dw_w"], p["ev_dw_b"], C=CW, seq=seq, glu=True, silu_out=False,
                       name="l0_conv")
    o, ctot = sba_fwd(proj0, B=B, seq=seq, heads=heads, q_off=q_off, k_off=k_off, v_off=v_off, name="l0_attn")
    ycat = mix0_post_fwd(u2, proj0, o, p["ev_ln_w"], p["ev_ln_b"], CW=CW, gc_off=gc_off, ga_off=ga_off,
                         name="l0_post")
    h1 = mm_nn(ycat, p["ev_w_out"], add=x, out_dtype=F32, name="l0_out_proj")

    n1 = rmsnorm_fwd(h1, p["od_norm_w"], name="l1_norm")
    proj1 = mm_nt_terms([(n1, 0, D, 0)], w1t, out_dtype=BF16, name="l1_in_proj")
    u_pre, xbc = dwconv_fwd(proj1, (DI,), p["od_conv_w"], p["od_conv_b"], C=XW, seq=seq, glu=False, silu_out=True,
                            name="l1_conv")
    bias_p, alog_p = _pad_cols(p["od_dt_bias"], LANE), _pad_cols(p["od_a_log"], LANE)
    expand = _head_expand(H, DI)
    dt, cs, dt_x, cs_x = dt_fwd(proj1, bias_p, alog_p, expand, dt_off=dt_off, name="l1_dt")
    cs_row = _to_rows(cs, B, seq, H)
    d_full = jnp.repeat(p["od_d"], SSM_HEAD_DIM, axis=1)
    y_ssd, states = scan_fwd(xbc, dt_x, cs_x, cs_row, d_full, B=B, seq=seq, DI=DI, name="l1_ssd")
    yn = gnorm_fwd(y_ssd, proj1, p["od_gnorm_w"], DI=DI, name="l1_gnorm")
    h2 = mm_nn(yn, od_w_out, add=h1, out_dtype=F32, name="l1_out_proj")

    loss, dh2, dh2b, g_final = final_loss(h2, p["final_norm_w"], target, name="loss_head")

    g_od_w_out = mm_tn(yn, dh2b, out_dtype=BF16, name="l1_dw_out")
    dyn = mm_nt_terms([(dh2b, 0, D, 0)], od_w_out, out_dtype=BF16, name="l1_d_out_proj")
    dy_ssd, dz, g_gnorm = gnorm_bwd(dyn, y_ssd, proj1, p["od_gnorm_w"], DI=DI, name="l1_gnorm_bwd")
    dxbc_c, ddt_x, dcs_x, dcs_row, dd_part = scan_bwd(xbc, dt_x, cs_x, cs_row, d_full, states, dy_ssd, B=B, seq=seq,
                                                      DI=DI, name="l1_ssd_bwd")
    g_d = dd_part.sum(axis=(0, 1)).reshape(H, SSM_HEAD_DIM).sum(axis=1)[None, :]
    draw, g_bias, g_alog = dt_bwd(ddt_x, dcs_x, _from_rows(dcs_row, B, seq, H), proj1, dt, bias_p, alog_p, expand.T,
                                  dt_off=dt_off, n_heads=H, name="l1_dt_bwd")
    dxbc, g_conv_w, g_conv_b = dwconv_bwd(dxbc_c, u_pre, proj1, (DI,), p["od_conv_w"], C=XW, seq=seq, glu=False,
                                          silu_out=True, name="l1_conv_bwd")
    tw = 512 if DI % 512 == 0 else LANE
    terms = [(dz, j, tw, j * tw) for j in range(DI // tw)]
    terms += [(dxbc, j, tw, DI + j * tw) for j in range(XW // tw)]
    terms += [(draw, 0, LANE, dt_off)]
    dn1 = mm_nn_terms(terms, w1t, out_dtype=BF16, name="l1_d_in_proj")
    g_od_w_in_t = jnp.concatenate([mm_tn(dz, n1, out_dtype=BF16, name="l1_dw_in_z"),
                                   mm_tn(dxbc, n1, out_dtype=BF16, name="l1_dw_in_xbc"),
                                   mm_tn(draw, n1, out_dtype=BF16, name="l1_dw_in_dt")], axis=0)[:in_odd]
    dh1, dh1b, g_od_norm = rmsnorm_bwd(h1, p["od_norm_w"], dn1, dh2, name="l1_norm_bwd")
    w1_pad = (-w1_rows) % 16
    l1_chip = _chip_sums([jnp.pad(g_od_w_in_t.reshape(N_DEV, w1_rows, D), ((0, 0), (0, w1_pad), (0, 0))),
                          g_od_w_out.reshape(N_DEV, -1, D)], "l1")

    g_ev_w_out = mm_tn(ycat, dh1b, out_dtype=BF16, name="l0_dw_out")
    dycat = mm_nt_terms([(dh1b, 0, D, 0)], p["ev_w_out"], out_dtype=BF16, name="l0_d_out_proj")
    du2, dgc, dga, do, g_ln_w, g_ln_b = mix0_post_bwd(dycat, u2, proj0, o, p["ev_ln_w"], p["ev_ln_b"], CW=CW,
                                                      gc_off=gc_off, ga_off=ga_off, name="l0_post_bwd")
    dq, dk, dv, (r_od_in_t, r_od_out) = sba_bwd(proj0, ctot, do, B=B, seq=seq, heads=heads, q_off=q_off, k_off=k_off,
                                                v_off=v_off, name="l0_attn_bwd", comm=(l1_chip, ["chips", "chips"]))
    dga_a, dga_b, g_dw_w, g_dw_b = dwconv_bwd(du2, None, proj0, (0, CW), p["ev_dw_w"], C=CW, seq=seq, glu=True,
                                              silu_out=False, name="l0_conv_bwd")
    pieces = [dga_a, dga_b, dgc, dq, dk, dv, dga]
    g_ev_w_in = jnp.concatenate([mm_tn(n0, pc, out_dtype=BF16, name=f"l0_dw_in_{j}") for j, pc in enumerate(pieces)],
                                axis=1)
    l0_chip = _chip_sums([_col_shards(g_ev_w_in), g_ev_w_out.reshape(N_DEV, -1, D)], "l0")
    dn0, (r_ev_in, r_ev_out) = mm_nt_terms([(pc, 0, CW, j * CW) for j, pc in enumerate(pieces)], p["ev_w_in"],
                                           out_dtype=BF16, name="l0_d_in_proj", comm=(l0_chip, ["chips", "chips"]))
    dx, _, g_ev_norm = rmsnorm_bwd(x, p["ev_norm_w"], dn0, dh1, name="l0_norm_bwd")

    small = dict(ev_norm_w=g_ev_norm, ev_dw_w=g_dw_w, ev_dw_b=g_dw_b, ev_ln_w=g_ln_w, ev_ln_b=g_ln_b,
                 od_norm_w=g_od_norm, od_conv_w=g_conv_w, od_conv_b=g_conv_b, od_dt_bias=g_bias[:, :H],
                 od_a_log=g_alog[:, :H], od_d=g_d, od_gnorm_w=g_gnorm, final_norm_w=g_final)
    received = dict(ev_w_in=r_ev_in, ev_w_out=r_ev_out, od_w_in=r_od_in_t, od_w_out=r_od_out)
    return loss, dx, small, received


BIG = ("ev_w_in", "ev_w_out", "od_w_in", "od_w_out")
SMALL = ("ev_norm_w", "ev_dw_w", "ev_dw_b", "ev_ln_w", "ev_ln_b", "od_norm_w", "od_conv_w", "od_conv_b",
         "od_dt_bias", "od_a_log", "od_d", "od_gnorm_w", "final_norm_w")
SMALL_SHARDED = ("ev_dw_w", "od_norm_w", "od_conv_w", "od_conv_b", "od_gnorm_w")
ORDER = ("ev_norm_w", "ev_w_in", "ev_dw_w", "ev_dw_b", "ev_ln_w", "ev_ln_b", "ev_w_out", "od_norm_w", "od_w_in",
         "od_conv_w", "od_conv_b", "od_dt_bias", "od_a_log", "od_d", "od_gnorm_w", "od_w_out", "final_norm_w")


def _pack_rows(arrs, width, row_align):
    parts, spans, r0 = [], [], 0
    for a in arrs:
        flat = a.reshape(-1)
        rows = -(-flat.shape[0] // (width * row_align)) * row_align
        parts.append(jnp.pad(flat, (0, rows * width - flat.shape[0])).reshape(rows, width))
        spans.append((r0, a.size, a.shape))
        r0 += rows
    return jnp.concatenate(parts, axis=0), spans


def _unpack_rows(packed, spans):
    lead = packed.shape[:-2]
    width = packed.shape[-1]
    out = []
    for r0, size, shape in spans:
        rows = -(-size // width)
        blk = packed[..., r0:r0 + rows, :].reshape(lead + (rows * width,))[..., :size]
        out.append(blk.reshape(lead + tuple(shape)))
    return out


def _col_shards(a):
    R, C8 = a.shape
    return a.reshape(R, N_DEV, C8 // N_DEV).transpose(1, 0, 2)


def _col_unshards(a):
    n, R, C = a.shape
    return a.transpose(1, 0, 2).reshape(R, n * C)


def kernel(x, ev_norm_w, ev_w_in, ev_dw_w, ev_dw_b, ev_ln_w, ev_ln_b, ev_w_out, od_norm_w, od_w_in, od_conv_w, od_conv_b, od_dt_bias, od_a_log, od_d, od_gnorm_w, od_w_out, final_norm_w, loss_target, m_ev_norm_w, m_ev_w_in, m_ev_dw_w, m_ev_dw_b, m_ev_ln_w, m_ev_ln_b, m_ev_w_out, m_od_norm_w, m_od_w_in, m_od_conv_w, m_od_conv_b, m_od_dt_bias, m_od_a_log, m_od_d, m_od_gnorm_w, m_od_w_out, m_final_norm_w, v_ev_norm_w, v_ev_w_in, v_ev_dw_w, v_ev_dw_b, v_ev_ln_w, v_ev_ln_b, v_ev_w_out, v_od_norm_w, v_od_w_in, v_od_conv_w, v_od_conv_b, v_od_dt_bias, v_od_a_log, v_od_d, v_od_gnorm_w, v_od_w_out, v_final_norm_w):
    loc = dict(ev_norm_w=ev_norm_w, ev_w_in=ev_w_in, ev_dw_w=ev_dw_w, ev_dw_b=ev_dw_b, ev_ln_w=ev_ln_w,
               ev_ln_b=ev_ln_b, ev_w_out=ev_w_out, od_norm_w=od_norm_w, od_w_in=od_w_in, od_conv_w=od_conv_w,
               od_conv_b=od_conv_b, od_dt_bias=od_dt_bias, od_a_log=od_a_log, od_d=od_d, od_gnorm_w=od_gnorm_w,
               od_w_out=od_w_out, final_norm_w=final_norm_w)
    mom = dict(ev_norm_w=m_ev_norm_w, ev_w_in=m_ev_w_in, ev_dw_w=m_ev_dw_w, ev_dw_b=m_ev_dw_b, ev_ln_w=m_ev_ln_w,
               ev_ln_b=m_ev_ln_b, ev_w_out=m_ev_w_out, od_norm_w=m_od_norm_w, od_w_in=m_od_w_in,
               od_conv_w=m_od_conv_w, od_conv_b=m_od_conv_b, od_dt_bias=m_od_dt_bias, od_a_log=m_od_a_log,
               od_d=m_od_d, od_gnorm_w=m_od_gnorm_w, od_w_out=m_od_w_out, final_norm_w=m_final_norm_w)
    var = dict(ev_norm_w=v_ev_norm_w, ev_w_in=v_ev_w_in, ev_dw_w=v_ev_dw_w, ev_dw_b=v_ev_dw_b, ev_ln_w=v_ev_ln_w,
               ev_ln_b=v_ev_ln_b, ev_w_out=v_ev_w_out, od_norm_w=v_od_norm_w, od_w_in=v_od_w_in,
               od_conv_w=v_od_conv_w, od_conv_b=v_od_conv_b, od_dt_bias=v_od_dt_bias, od_a_log=v_od_a_log,
               od_d=v_od_d, od_gnorm_w=v_od_gnorm_w, od_w_out=v_od_w_out, final_norm_w=v_final_norm_w)
    shapes = {n: loc[n].shape for n in ORDER}
    loc = {n: (a.reshape(1, -1) if a.ndim == 1 else a.reshape(a.shape[-2:]) if a.ndim == 3 else a)
           for n, a in loc.items()}
    mom = {n: a.reshape(loc[n].shape) for n, a in mom.items()}
    var = {n: a.reshape(loc[n].shape) for n, a in var.items()}

    B, seq, D = x.shape
    me = 4 * lax.axis_index("x") + 2 * lax.axis_index("y") + lax.axis_index("c")

    w1_rows = loc["od_w_in"].shape[1]
    w1_pad = (-w1_rows) % 16

    def to_t(a):
        return jnp.pad(a.T, ((0, w1_pad), (0, 0)))

    loss, dx, grads, received = local_step(x.reshape(B * seq, D), loss_target.reshape(B * seq, D), loc,
                                           to_t(loc["od_w_in"]), B=B, seq=seq)

    gsmall_packed, gsmall_spans = _pack_rows([grads[n] for n in SMALL] + [loss], LANE, 8)
    (gsmall_recv,) = exchange([gsmall_packed], ["gather_direct"], name="gather_small_grads")

    big_out = [{} for _ in range(4)]
    for n in ("ev_w_in", "ev_w_out", "od_w_out"):
        for kind, a in enumerate(adamw(received[n], loc[n], mom[n], var[n], name="adamw_" + n)):
            big_out[kind][n] = a
    for kind, a in enumerate(adamw(received["od_w_in"], to_t(loc["od_w_in"]), to_t(mom["od_w_in"]),
                                   to_t(var["od_w_in"]), name="adamw_od_w_in")):
        big_out[kind]["od_w_in"] = a[:w1_rows].T

    summed = _unpack_rows(sum_slots(gsmall_recv, name="sum_small_grads"), gsmall_spans)
    loss_total = summed[-1][0, 0]
    gsmall = dict(zip(SMALL, summed[:-1]))
    for n in SMALL_SHARDED:
        width = loc[n].shape[1]
        gsmall[n] = lax.dynamic_slice_in_dim(gsmall[n], me * width, width, axis=1)
    gs, sspans = _pack_rows([gsmall[n] for n in SMALL], LANE, 8)
    ws, _ = _pack_rows([loc[n] for n in SMALL], LANE, 8)
    ms, _ = _pack_rows([mom[n] for n in SMALL], LANE, 8)
    vs, _ = _pack_rows([var[n] for n in SMALL], LANE, 8)
    small_out = [dict(zip(SMALL, _unpack_rows(a, sspans))) for a in adamw(gs[None], ws, ms, vs, name="adamw_small")]

    outs = [loss_total, dx.reshape(B, seq, D)]
    for kind in range(4):
        for n in ORDER:
            src = big_out[kind] if n in BIG else small_out[kind]
            outs.append(src[n].reshape(shapes[n]))
    return tuple(outs)
```

```python
import jax
import jax.numpy as jnp
from jax import lax
from jax.experimental import pallas as pl
from jax.experimental.pallas import tpu as pltpu

F32 = jnp.float32
BF16 = jnp.bfloat16

EPS = 1e-6
N_DEV = 8
LANE = 128
VMEM_LIMIT_BYTES = 48 * 1024 * 1024

SB_HEAD_DIM = 128
SSM_HEAD_DIM = 64
SSM_GROUPS = 4
SSM_STATE = 128
SSM_CHUNK = 128
HALO = 32
HEAD_ROWS = 8
NEG_BIG = -1e30

ADAM_LR = 0.001
ADAM_B1 = 0.9
ADAM_B2 = 0.999
ADAM_EPS = 1e-08
ADAM_WD = 0.01
ADAM_STEP = 10

NT = (((1,), (1,)), ((), ()))
TN = (((0,), (0,)), ((), ()))


def _cp(*sem):
    return pltpu.CompilerParams(dimension_semantics=sem, vmem_limit_bytes=VMEM_LIMIT_BYTES)


def _pick(n, cap, align):
    if n <= cap:
        return n
    t = (cap // align) * align
    while t >= align:
        if n % t == 0:
            return t
        t -= align
    raise ValueError(f"no tile for {n} (cap {cap}, align {align})")


def _sigmoid(x):
    return 0.5 * jnp.tanh(0.5 * x) + 0.5


def _silu(x):
    return x * _sigmoid(x)


def _silu_grad(x):
    s = _sigmoid(x)
    return s * (1.0 + x * (1.0 - s))


def _dot(a, b, dims=None):
    if dims is None:
        return jnp.dot(a, b, preferred_element_type=F32)
    return lax.dot_general(a, b, dims, preferred_element_type=F32)


def _tri_dot3(tri, x):
    hi = x.astype(BF16)
    r1 = x - hi.astype(F32)
    mid = r1.astype(BF16)
    lo = (r1 - mid.astype(F32)).astype(BF16)
    return _dot(tri, hi) + _dot(tri, mid) + _dot(tri, lo)


def _grid_step(n_inner):
    return lambda: pl.program_id(0) * n_inner + pl.program_id(1)


def mm_nn(a, b, *, add=None, out_dtype, name, comm=None):
    M, K = a.shape
    N = b.shape[1]
    tm = _pick(M, 2048 if K <= 1024 and add is None else 1024, 16)
    tn = _pick(N, 1024, LANE)

    def body(*refs):
        if add is None:
            a_ref, b_ref, o_ref = refs
        else:
            a_ref, b_ref, add_ref, o_ref = refs
        acc = _dot(a_ref[...], b_ref[...])
        if add is not None:
            acc = acc + add_ref[...]
        o_ref[...] = acc.astype(out_dtype)

    in_specs = [pl.BlockSpec((tm, K), lambda i, j: (i, 0)), pl.BlockSpec((K, tn), lambda i, j: (0, j))]
    args = [a, b]
    if add is not None:
        in_specs.append(pl.BlockSpec((tm, tn), lambda i, j: (i, j)))
        args.append(add)
    grid = (M // tm, N // tn)
    body, c_in, c_args, c_out, c_shape, c_scratch = _hosted(body, len(args), 1, comm, _grid_step(grid[1]),
                                                            grid[0] * grid[1])
    out = pl.pallas_call(
        body, name=name, grid=grid, in_specs=in_specs + c_in,
        out_specs=[pl.BlockSpec((tm, tn), lambda i, j: (i, j))] + c_out,
        out_shape=[jax.ShapeDtypeStruct((M, N), out_dtype)] + c_shape, scratch_shapes=c_scratch,
        compiler_params=_cp(*(("arbitrary",) * 2 if comm else ("parallel",) * 2)))(*args, *c_args)
    return (out[0], out[1:]) if comm else out[0]


def mm_nt_terms(terms, b, *, out_dtype, name, comm=None):
    M = terms[0][0].shape[0]
    N = b.shape[0]
    n_terms = len(terms)
    if n_terms == 1:
        tm, tn = _pick(M, 2048, 16), _pick(N, 1024, LANE)
    else:
        tm, tn = _pick(M, 256, 16), _pick(N, 1024, LANE)

    def body(*refs):
        o_ref = refs[-1]
        acc = None
        for t in range(n_terms):
            part = _dot(refs[2 * t][...], refs[2 * t + 1][...], NT)
            acc = part if acc is None else acc + part
        o_ref[...] = acc.astype(out_dtype)

    in_specs, args = [], []
    for arr, cb, w, off in terms:
        assert off % w == 0
        in_specs.append(pl.BlockSpec((tm, w), lambda i, j, cb=cb: (i, cb)))
        in_specs.append(pl.BlockSpec((tn, w), lambda i, j, ob=off // w: (j, ob)))
        args += [arr, b]
    grid = (M // tm, N // tn)
    body, c_in, c_args, c_out, c_shape, c_scratch = _hosted(body, len(args), 1, comm, _grid_step(grid[1]),
                                                            grid[0] * grid[1])
    out = pl.pallas_call(
        body, name=name, grid=grid, in_specs=in_specs + c_in,
        out_specs=[pl.BlockSpec((tm, tn), lambda i, j: (i, j))] + c_out,
        out_shape=[jax.ShapeDtypeStruct((M, N), out_dtype)] + c_shape, scratch_shapes=c_scratch,
        compiler_params=_cp(*(("arbitrary",) * 2 if comm else ("parallel",) * 2)))(*args, *c_args)
    return (out[0], out[1:]) if comm else out[0]


def mm_nn_terms(terms, b, *, out_dtype, name):
    M = terms[0][0].shape[0]
    N = b.shape[1]
    tm = _pick(M, 256, 16)
    tn = _pick(N, 1024, LANE)
    n_terms = len(terms)

    def body(*refs):
        o_ref = refs[-1]
        acc = None
        for t in range(n_terms):
            part = _dot(refs[2 * t][...], refs[2 * t + 1][...])
            acc = part if acc is None else acc + part
        o_ref[...] = acc.astype(out_dtype)

    in_specs, args = [], []
    for arr, cb, w, off in terms:
        assert off % w == 0
        in_specs.append(pl.BlockSpec((tm, w), lambda i, j, cb=cb: (i, cb)))
        in_specs.append(pl.BlockSpec((w, tn), lambda i, j, ob=off // w: (ob, j)))
        args += [arr, b]
    return pl.pallas_call(
        body, name=name, grid=(M // tm, N // tn), in_specs=in_specs,
        out_specs=pl.BlockSpec((tm, tn), lambda i, j: (i, j)),
        out_shape=jax.ShapeDtypeStruct((M, N), out_dtype),
        compiler_params=_cp("parallel", "parallel"))(*args)


def mm_tn(a, b, *, out_dtype, name):
    T, M = a.shape
    N = b.shape[1]
    tm = _pick(M, 1024, LANE)
    tn = _pick(N, 1024, LANE)
    tk = _pick(T, 2048, 16)
    nk = T // tk

    def body(a_ref, b_ref, o_ref, acc_ref):
        k = pl.program_id(2)

        @pl.when(k == 0)
        def _():
            acc_ref[...] = jnp.zeros_like(acc_ref)

        acc_ref[...] += _dot(a_ref[...], b_ref[...], TN)

        @pl.when(k == nk - 1)
        def _():
            o_ref[...] = acc_ref[...].astype(out_dtype)

    return pl.pallas_call(
        body, name=name, grid=(M // tm, N // tn, nk),
        in_specs=[pl.BlockSpec((tk, tm), lambda i, j, k: (k, i)), pl.BlockSpec((tk, tn), lambda i, j, k: (k, j))],
        out_specs=pl.BlockSpec((tm, tn), lambda i, j, k: (i, j)),
        out_shape=jax.ShapeDtypeStruct((M, N), out_dtype),
        scratch_shapes=[pltpu.VMEM((tm, tn), F32)],
        compiler_params=_cp("parallel", "parallel", "arbitrary"))(a, b)


def rmsnorm_fwd(h, w, *, name, comm=None):
    T, D = h.shape
    tt = _pick(T, 512, 16)

    def body(h_ref, w_ref, n_ref):
        x = h_ref[...]
        r = lax.rsqrt(jnp.mean(x * x, axis=-1, keepdims=True) + EPS)
        n_ref[...] = (x * r * w_ref[...]).astype(BF16)

    body, c_in, c_args, c_out, c_shape, c_scratch = _hosted(body, 2, 1, comm, lambda: pl.program_id(0), T // tt)
    out = pl.pallas_call(
        body, name=name, grid=(T // tt,),
        in_specs=[pl.BlockSpec((tt, D), lambda i: (i, 0)), pl.BlockSpec((1, D), lambda i: (0, 0))] + c_in,
        out_specs=[pl.BlockSpec((tt, D), lambda i: (i, 0))] + c_out,
        out_shape=[jax.ShapeDtypeStruct((T, D), BF16)] + c_shape, scratch_shapes=c_scratch,
        compiler_params=_cp("arbitrary" if comm else "parallel"))(h, w, *c_args)
    return (out[0], out[1:]) if comm else out[0]


def rmsnorm_bwd(h, w, dn, dres, *, name, comm=None):
    T, D = h.shape
    tt = _pick(T, 512, 16)

    def body(h_ref, w_ref, dn_ref, dres_ref, dh_ref, dhb_ref, gw_ref):
        @pl.when(pl.program_id(0) == 0)
        def _():
            gw_ref[...] = jnp.zeros_like(gw_ref)

        x = h_ref[...]
        r = lax.rsqrt(jnp.mean(x * x, axis=-1, keepdims=True) + EPS)
        xhat = x * r
        g = dn_ref[...].astype(F32)
        gw_ref[...] += jnp.sum(g * xhat, axis=0, keepdims=True)
        dxh = g * w_ref[...]
        dx = r * (dxh - xhat * jnp.mean(dxh * xhat, axis=-1, keepdims=True))
        dh = dres_ref[...] + dx
        dh_ref[...] = dh
        dhb_ref[...] = dh.astype(BF16)

    row = pl.BlockSpec((tt, D), lambda i: (i, 0))
    vec = pl.BlockSpec((1, D), lambda i: (0, 0))
    body, c_in, c_args, c_out, c_shape, c_scratch = _hosted(body, 4, 3, comm, lambda: pl.program_id(0), T // tt)
    out = pl.pallas_call(
        body, name=name, grid=(T // tt,), in_specs=[row, vec, row, row] + c_in, out_specs=[row, row, vec] + c_out,
        out_shape=[jax.ShapeDtypeStruct((T, D), F32), jax.ShapeDtypeStruct((T, D), BF16),
                   jax.ShapeDtypeStruct((1, D), F32)] + c_shape,
        scratch_shapes=c_scratch, compiler_params=_cp("arbitrary"))(h, w, dn, dres, *c_args)
    return (out[0], out[1], out[2], out[3:]) if comm else out


def final_loss(h, w, target, *, name):
    T, D = h.shape
    tt = _pick(T, 512, 16)

    def body(h_ref, w_ref, t_ref, loss_ref, dh_ref, dhb_ref, gw_ref):
        @pl.when(pl.program_id(0) == 0)
        def _():
            gw_ref[...] = jnp.zeros_like(gw_ref)
            loss_ref[...] = jnp.zeros_like(loss_ref)

        x = h_ref[...]
        r = lax.rsqrt(jnp.mean(x * x, axis=-1, keepdims=True) + EPS)
        xhat = x * r
        e = xhat * w_ref[...] - t_ref[...]
        loss_ref[...] += jnp.sum(e * e) * (0.5 / D)
        g = e * (1.0 / D)
        gw_ref[...] += jnp.sum(g * xhat, axis=0, keepdims=True)
        dxh = g * w_ref[...]
        dh = r * (dxh - xhat * jnp.mean(dxh * xhat, axis=-1, keepdims=True))
        dh_ref[...] = dh
        dhb_ref[...] = dh.astype(BF16)

    row = pl.BlockSpec((tt, D), lambda i: (i, 0))
    vec = pl.BlockSpec((1, D), lambda i: (0, 0))
    one = pl.BlockSpec((1, LANE), lambda i: (0, 0))
    return pl.pallas_call(
        body, name=name, grid=(T // tt,), in_specs=[row, vec, row], out_specs=[one, row, row, vec],
        out_shape=[jax.ShapeDtypeStruct((1, LANE), F32), jax.ShapeDtypeStruct((T, D), F32),
                   jax.ShapeDtypeStruct((T, D), BF16), jax.ShapeDtypeStruct((1, D), F32)],
        compiler_params=_cp("arbitrary"))(h, w, target)


CONV_CHUNK = 32
ROW_CHUNK = 16
SUBLANES = 8


def _conv_tiles(seq, C, K):
    return _pick(seq, 1024 if K <= SUBLANES else 512, HALO), _pick(C, 512, LANE)


def _residues(offsets):
    return sorted({s % SUBLANES for s in offsets} - {0})


def _fill_shifted(buf, shifted, residues):
    n = buf.shape[0] - SUBLANES
    for i, r in enumerate(residues):
        shifted[i, 0:n, :] = buf[r:r + n, :]


def _tap(buf, shifted, residues, offset, start, rows):
    r = offset % SUBLANES
    base = offset - r
    ref = buf if r == 0 else shifted.at[residues.index(r)]
    return ref[pl.ds(start + base, rows), :]


def dwconv_fwd(src, offs, w, b, *, C, seq, glu, silu_out, name):
    T = src.shape[0]
    K = w.shape[0]
    assert K - 1 <= HALO
    tt, tc = _conv_tiles(seq, C, K)
    n_in = 2 if glu else 1
    per = tt // HALO
    offsets = [HALO - (K - 1) + k for k in range(K)]
    residues = _residues(offsets)

    def body(*refs):
        cur = refs[0:2 * n_in:2]
        halo = refs[1:2 * n_in:2]
        w_ref, b_ref = refs[2 * n_in], refs[2 * n_in + 1]
        outs = refs[2 * n_in + 2:-2]
        buf, shifted = refs[-2], refs[-1]
        i = pl.program_id(1)
        first = (i * tt) % seq == 0

        def pre(rs, rows):
            v = rs[0][rows, :].astype(F32)
            return v * _sigmoid(rs[1][rows, :].astype(F32)) if glu else v

        def build(ci, carry):
            start = pl.multiple_of(ci * CONV_CHUNK, CONV_CHUNK)
            buf[pl.ds(HALO + start, CONV_CHUNK), :] = pre(cur, pl.ds(start, CONV_CHUNK))
            return carry

        buf[0:HALO, :] = jnp.where(first, 0.0, pre(halo, slice(None)))
        lax.fori_loop(0, tt // CONV_CHUNK, build, 0, unroll=2)
        _fill_shifted(buf, shifted, residues)

        def chunk(ci, carry):
            start = pl.multiple_of(ci * CONV_CHUNK, CONV_CHUNK)
            acc = jnp.broadcast_to(b_ref[...], (CONV_CHUNK, tc))
            for k in range(K):
                acc = acc + w_ref[k:k + 1, :] * _tap(buf, shifted, residues, offsets[k], start, CONV_CHUNK)
            outs[0][pl.ds(start, CONV_CHUNK), :] = acc.astype(BF16)
            if silu_out:
                outs[1][pl.ds(start, CONV_CHUNK), :] = _silu(acc).astype(BF16)
            return carry

        lax.fori_loop(0, tt // CONV_CHUNK, chunk, 0)

    in_specs, args = [], []
    for off in offs:
        assert off % tc == 0
        in_specs.append(pl.BlockSpec((tt, tc), lambda j, i, ob=off // tc: (i, ob + j)))
        in_specs.append(pl.BlockSpec((HALO, tc), lambda j, i, ob=off // tc: (jnp.maximum(i * per - 1, 0), ob + j)))
        args += [src, src]
    in_specs += [pl.BlockSpec((K, tc), lambda j, i: (0, j)), pl.BlockSpec((1, tc), lambda j, i: (0, j))]
    args += [w, b]
    n_out = 2 if silu_out else 1
    out = pl.pallas_call(
        body, name=name, grid=(C // tc, T // tt), in_specs=in_specs,
        out_specs=[pl.BlockSpec((tt, tc), lambda j, i: (i, j))] * n_out,
        out_shape=[jax.ShapeDtypeStruct((T, C), BF16)] * n_out,
        scratch_shapes=[pltpu.VMEM((HALO + tt, tc), F32), pltpu.VMEM((max(len(residues), 1), HALO + tt, tc), F32)],
        compiler_params=_cp("parallel", "arbitrary"))(*args)
    return out


def dwconv_bwd(du, u, src, offs, w, *, C, seq, glu, silu_out, name):
    T = src.shape[0]
    K = w.shape[0]
    assert K - 1 <= HALO
    tt, tc = _conv_tiles(seq, C, K)
    n_in = 2 if glu else 1
    per = tt // HALO
    last_blk = T // HALO - 1
    g_offsets = [K - 1 - k for k in range(K)]
    g_res = _residues(g_offsets)

    def body(*refs):
        pos = 0
        du_cur, du_nxt = refs[0], refs[1]
        pos = 2
        if silu_out:
            u_cur, u_nxt = refs[2], refs[3]
            pos = 4
        cur = refs[pos:pos + n_in]
        pos += n_in
        w_ref = refs[pos]
        outs = refs[pos + 1:pos + 1 + n_in]
        dw_ref, db_ref = refs[pos + 1 + n_in], refs[pos + 2 + n_in]
        gbuf, gshift, dw_acc, db_acc = refs[-4:]
        i = pl.program_id(1)
        last = ((i + 1) * tt) % seq == 0

        @pl.when(i == 0)
        def _():
            dw_acc[...] = jnp.zeros_like(dw_acc)
            db_acc[...] = jnp.zeros_like(db_acc)

        def build(ci, carry):
            rows = pl.ds(pl.multiple_of(ci * CONV_CHUNK, CONV_CHUNK), CONV_CHUNK)
            g = du_cur[rows, :].astype(F32)
            if silu_out:
                g = g * _silu_grad(u_cur[rows, :].astype(F32))
            gbuf[rows, :] = g
            return carry

        lax.fori_loop(0, tt // CONV_CHUNK, build, 0, unroll=2)
        g_nxt = du_nxt[...].astype(F32)
        if silu_out:
            g_nxt = g_nxt * _silu_grad(u_nxt[...].astype(F32))
        gbuf[tt:tt + HALO, :] = jnp.where(last, 0.0, g_nxt)
        _fill_shifted(gbuf, gshift, g_res)

        def fold(v):
            out = v[0:SUBLANES]
            for s in range(SUBLANES, CONV_CHUNK, SUBLANES):
                out = out + v[s:s + SUBLANES]
            return out

        def chunk(ci, carry):
            start = pl.multiple_of(ci * CONV_CHUNK, CONV_CHUNK)
            rows = pl.ds(start, CONV_CHUNK)
            a = cur[0][rows, :].astype(F32)
            if glu:
                s = _sigmoid(cur[1][rows, :].astype(F32))
                x_in = a * s
            else:
                x_in = a
            dx = jnp.zeros((CONV_CHUNK, tc), F32)
            for k in range(K):
                g_k = _tap(gbuf, gshift, g_res, g_offsets[k], start, CONV_CHUNK)
                dx = dx + w_ref[k:k + 1, :] * g_k
                dw_acc[k * SUBLANES:(k + 1) * SUBLANES, :] += fold(g_k * x_in)
            db_acc[...] += fold(gbuf[rows, :])
            if glu:
                outs[0][rows, :] = (dx * s).astype(BF16)
                outs[1][rows, :] = (dx * a * s * (1.0 - s)).astype(BF16)
            else:
                outs[0][rows, :] = dx.astype(BF16)
            return carry

        lax.fori_loop(0, tt // CONV_CHUNK, chunk, 0)

        @pl.when(i == T // tt - 1)
        def _():
            for k in range(K):
                dw_ref[k:k + 1, :] = jnp.sum(dw_acc[k * SUBLANES:(k + 1) * SUBLANES, :], axis=0, keepdims=True)
            db_ref[...] = jnp.sum(db_acc[...], axis=0, keepdims=True)

    def cur_spec(ob):
        return pl.BlockSpec((tt, tc), lambda j, i: (i, ob + j))

    def nxt_spec(ob):
        return pl.BlockSpec((HALO, tc), lambda j, i: (jnp.minimum((i + 1) * per, last_blk), ob + j))

    in_specs = [cur_spec(0), nxt_spec(0)]
    args = [du, du]
    if silu_out:
        in_specs += [cur_spec(0), nxt_spec(0)]
        args += [u, u]
    for off in offs:
        assert off % tc == 0
        in_specs.append(cur_spec(off // tc))
        args.append(src)
    in_specs.append(pl.BlockSpec((K, tc), lambda j, i: (0, j)))
    args.append(w)
    out_specs = [pl.BlockSpec((tt, tc), lambda j, i: (i, j))] * n_in
    out_specs += [pl.BlockSpec((K, tc), lambda j, i: (0, j)), pl.BlockSpec((1, tc), lambda j, i: (0, j))]
    out_shape = [jax.ShapeDtypeStruct((T, C), BF16)] * n_in
    out_shape += [jax.ShapeDtypeStruct((K, C), F32), jax.ShapeDtypeStruct((1, C), F32)]
    return pl.pallas_call(
        body, name=name, grid=(C // tc, T // tt), in_specs=in_specs, out_specs=out_specs, out_shape=out_shape,
        scratch_shapes=[pltpu.VMEM((tt + HALO, tc), F32), pltpu.VMEM((max(len(g_res), 1), tt + HALO, tc), F32),
                        pltpu.VMEM((K * SUBLANES, tc), F32), pltpu.VMEM((SUBLANES, tc), F32)],
        compiler_params=_cp("parallel", "arbitrary"))(*args)


def mix0_post_fwd(u2, proj, o, ln_w, ln_b, *, CW, gc_off, ga_off, name):
    T = u2.shape[0]
    tt = _pick(T, 256, 16)

    def body(u_ref, gc_ref, ga_ref, o_ref, lw_ref, lb_ref, y_ref):
        def chunk(ci, carry):
            rows = pl.ds(pl.multiple_of(ci * ROW_CHUNK, ROW_CHUNK), ROW_CHUNK)
            u = u_ref[rows, :].astype(F32)
            mu = jnp.mean(u, axis=-1, keepdims=True)
            xc = u - mu
            r = lax.rsqrt(jnp.mean(xc * xc, axis=-1, keepdims=True) + EPS)
            u3 = xc * r * lw_ref[...] + lb_ref[...]
            y_ref[rows, 0:CW] = (_silu(u3) * _silu(gc_ref[rows, :].astype(F32))).astype(BF16)
            y_ref[rows, CW:2 * CW] = (o_ref[rows, :].astype(F32) * _silu(ga_ref[rows, :].astype(F32))).astype(BF16)
            return carry

        lax.fori_loop(0, tt // ROW_CHUNK, chunk, 0, unroll=4)

    row = pl.BlockSpec((tt, CW), lambda i: (i, 0))
    vec = pl.BlockSpec((1, CW), lambda i: (0, 0))
    return pl.pallas_call(
        body, name=name, grid=(T // tt,),
        in_specs=[row, pl.BlockSpec((tt, CW), lambda i: (i, gc_off // CW)),
                  pl.BlockSpec((tt, CW), lambda i: (i, ga_off // CW)), row, vec, vec],
        out_specs=pl.BlockSpec((tt, 2 * CW), lambda i: (i, 0)),
        out_shape=jax.ShapeDtypeStruct((T, 2 * CW), BF16),
        compiler_params=_cp("parallel"))(u2, proj, proj, o, ln_w, ln_b)


def mix0_post_bwd(dy, u2, proj, o, ln_w, ln_b, *, CW, gc_off, ga_off, name):
    T = u2.shape[0]
    tt = _pick(T, 256, 16)

    def body(dy_ref, u_ref, gc_ref, ga_ref, o_ref, lw_ref, lb_ref, du_ref, dgc_ref, dga_ref, do_ref, dlw_ref, dlb_ref,
             lw_acc, lb_acc):
        i = pl.program_id(0)

        @pl.when(i == 0)
        def _():
            lw_acc[...] = jnp.zeros_like(lw_acc)
            lb_acc[...] = jnp.zeros_like(lb_acc)

        def fold(v):
            out = v[0:SUBLANES]
            for s in range(SUBLANES, ROW_CHUNK, SUBLANES):
                out = out + v[s:s + SUBLANES]
            return out

        def chunk(ci, carry):
            rows = pl.ds(pl.multiple_of(ci * ROW_CHUNK, ROW_CHUNK), ROW_CHUNK)
            dyc = dy_ref[rows, 0:CW].astype(F32)
            dya = dy_ref[rows, CW:2 * CW].astype(F32)
            u = u_ref[rows, :].astype(F32)
            mu = jnp.mean(u, axis=-1, keepdims=True)
            xc = u - mu
            r = lax.rsqrt(jnp.mean(xc * xc, axis=-1, keepdims=True) + EPS)
            xhat = xc * r
            u3 = xhat * lw_ref[...] + lb_ref[...]
            gc = gc_ref[rows, :].astype(F32)
            dgc_ref[rows, :] = (dyc * _silu(u3) * _silu_grad(gc)).astype(BF16)
            du3 = dyc * _silu(gc) * _silu_grad(u3)
            lw_acc[...] += fold(du3 * xhat)
            lb_acc[...] += fold(du3)
            dxh = du3 * lw_ref[...]
            du = r * (dxh - jnp.mean(dxh, axis=-1, keepdims=True)
                      - xhat * jnp.mean(dxh * xhat, axis=-1, keepdims=True))
            du_ref[rows, :] = du.astype(BF16)
            ga = ga_ref[rows, :].astype(F32)
            ov = o_ref[rows, :].astype(F32)
            do_ref[rows, :] = (dya * _silu(ga)).astype(BF16)
            dga_ref[rows, :] = (dya * ov * _silu_grad(ga)).astype(BF16)
            return carry

        lax.fori_loop(0, tt // ROW_CHUNK, chunk, 0, unroll=4)

        @pl.when(i == T // tt - 1)
        def _():
            dlw_ref[...] = jnp.sum(lw_acc[...], axis=0, keepdims=True)
            dlb_ref[...] = jnp.sum(lb_acc[...], axis=0, keepdims=True)

    row = pl.BlockSpec((tt, CW), lambda i: (i, 0))
    vec = pl.BlockSpec((1, CW), lambda i: (0, 0))
    big = jax.ShapeDtypeStruct((T, CW), BF16)
    small = jax.ShapeDtypeStruct((1, CW), F32)
    return pl.pallas_call(
        body, name=name, grid=(T // tt,),
        in_specs=[pl.BlockSpec((tt, 2 * CW), lambda i: (i, 0)), row,
                  pl.BlockSpec((tt, CW), lambda i: (i, gc_off // CW)),
                  pl.BlockSpec((tt, CW), lambda i: (i, ga_off // CW)), row, vec, vec],
        out_specs=[row, row, row, row, vec, vec],
        out_shape=[big, big, big, big, small, small],
        scratch_shapes=[pltpu.VMEM((SUBLANES, CW), F32), pltpu.VMEM((SUBLANES, CW), F32)],
        compiler_params=_cp("arbitrary"))(dy, u2, proj, proj, o, ln_w, ln_b)


SB_UNDERFLOW = 110.0
SB_BOUND_MARGIN = 1.02


def _sb_tile(seq):
    return _pick(seq, 256, LANE)


def _softplus(z):
    return jnp.maximum(z, 0.0) + jnp.log(1.0 + jnp.exp(-jnp.abs(z)))


def _tri01(n, lower):
    i = lax.broadcasted_iota(jnp.int32, (n, n), 0)
    j = lax.broadcasted_iota(jnp.int32, (n, n), 1)
    return ((i >= j) if lower else (i <= j)).astype(BF16)


SB_HEADS_FWD = 4
SB_HEADS_BWD = 2


def _sb_heads_per_step(heads, want):
    while heads % want:
        want //= 2
    return want


def sba_fwd(proj, *, B, seq, heads, q_off, k_off, v_off, name):
    dh = SB_HEAD_DIM
    tq = _sb_tile(seq)
    assert tq % (2 * LANE) == 0
    nq = seq // tq
    hps = _sb_heads_per_step(heads, SB_HEADS_FWD)
    hw = hps * dh
    scale = dh ** -0.5

    def body(q_ref, k_ref, v_ref, tri_ref, o_ref, ct_ref, acc_ref, kmax_ref):
        qi = pl.program_id(1)
        tri = tri_ref[...]
        qs = [(q_ref[:, h * dh:(h + 1) * dh].astype(F32) * scale).astype(BF16) for h in range(hps)]

        @pl.when(qi == 0)
        def _():
            def chunk(i, best):
                rows = k_ref[pl.ds(pl.multiple_of(i * tq, tq), tq), :].astype(F32)
                sq = rows * rows
                return tuple(jnp.maximum(best[h], jnp.max(jnp.sum(sq[:, h * dh:(h + 1) * dh], axis=1, keepdims=True),
                                                          axis=0, keepdims=True)) for h in range(hps))

            best = lax.fori_loop(0, nq, chunk, (jnp.zeros((1, 1), F32),) * hps)
            for h in range(hps):
                kmax_ref[h] = jnp.broadcast_to(jnp.sqrt(best[h]), (8, LANE))

        z_bound = [jnp.sqrt(jnp.sum(qs[h].astype(F32) ** 2, axis=1, keepdims=True))
                   * (SB_BOUND_MARGIN * jnp.max(kmax_ref[h], keepdims=True)) for h in range(hps)]

        def part(h, q_rows, start, n_keys, r, mask):
            k_blk = k_ref[pl.ds(start, n_keys), h * dh:(h + 1) * dh]
            v_blk = v_ref[pl.ds(start, n_keys), h * dh:(h + 1) * dh]
            z = _dot(q_rows, k_blk, NT)
            sp = _softplus(z)
            if mask is not None:
                sp = jnp.where(mask, sp, 0.0)
            wts = jnp.exp(z - (_dot(sp.astype(BF16), tri[0:n_keys, 0:n_keys]) + r))
            if mask is not None:
                wts = jnp.where(mask, wts, 0.0)
            return _dot(wts.astype(BF16), v_blk), r + jnp.sum(sp, axis=-1, keepdims=True)

        below = lax.broadcasted_iota(jnp.int32, (tq, tq), 1) < lax.broadcasted_iota(jnp.int32, (tq, tq), 0)
        has_left = qi > 0
        left = pl.multiple_of(jnp.maximum(qi - 1, 0) * tq, tq)
        rs = []
        for h in range(hps):
            pv_d, r = part(h, qs[h], pl.multiple_of(qi * tq, tq), tq, jnp.zeros((tq, 1), F32), below)
            pv_l, r = part(h, qs[h], left, tq, r, has_left)
            acc_ref[:, h * dh:(h + 1) * dh] = pv_d + pv_l
            rs.append(r)
        rs = tuple(rs)

        def block(start, rs):
            pvs, out = [], []
            for h in range(hps):
                pv, r = part(h, qs[h], start, tq, rs[h], None)
                pvs.append(pv)
                out.append(r)
            return pvs, tuple(out)

        def more(c):
            j, rs = c
            slack = rs[0] - z_bound[0]
            for h in range(1, hps):
                slack = jnp.minimum(slack, rs[h] - z_bound[h])
            return jnp.logical_and(j < qi, jnp.min(slack) <= SB_UNDERFLOW)

        def step(c):
            j, rs = c
            pvs, rs = block(pl.multiple_of((qi - 1 - j) * tq, tq), rs)
            for h in range(hps):
                acc_ref[:, h * dh:(h + 1) * dh] += pvs[h]
            return j + 1, rs

        n_left, totals = lax.while_loop(more, step, (has_left.astype(jnp.int32), rs))
        o_ref[...] = acc_ref[...].astype(BF16)
        for h in range(hps):
            ct_ref[0, 0, h, 0:8, :] = jnp.broadcast_to(totals[h], (tq, LANE)).T[0:8, :]
            ct_ref[0, 0, h, 8:16, :] = jnp.full((8, tq), n_left, F32)

    qb, kb, vb = q_off // hw, k_off // hw, v_off // hw
    G = heads // hps
    return pl.pallas_call(
        body, name=name, grid=(B * G, nq),
        in_specs=[pl.BlockSpec((tq, hw), lambda g, i: ((g // G) * nq + i, qb + g % G)),
                  pl.BlockSpec((seq, hw), lambda g, i: (g // G, kb + g % G)),
                  pl.BlockSpec((seq, hw), lambda g, i: (g // G, vb + g % G)),
                  pl.BlockSpec((tq, tq), lambda g, i: (0, 0))],
        out_specs=[pl.BlockSpec((tq, hw), lambda g, i: ((g // G) * nq + i, g % G)),
                   pl.BlockSpec((1, 1, hps, 16, tq), lambda g, i: (g // G, i, g % G, 0, 0))],
        out_shape=[jax.ShapeDtypeStruct((B * seq, heads * dh), BF16),
                   jax.ShapeDtypeStruct((B, nq, heads, 16, tq), F32)],
        scratch_shapes=[pltpu.VMEM((tq, hw), F32), pltpu.VMEM((hps, 8, LANE), F32)],
        compiler_params=_cp("parallel", "arbitrary"))(proj, proj, proj, jnp.tril(jnp.ones((tq, tq), BF16)))


def sba_bwd(proj, ctot, do, *, B, seq, heads, q_off, k_off, v_off, name, comm=None):
    dh = SB_HEAD_DIM
    tq = _sb_tile(seq)
    nq = seq // tq
    hps = _sb_heads_per_step(heads, SB_HEADS_BWD)
    hw = hps * dh
    scale = dh ** -0.5

    def body(q_ref, k_ref, v_ref, ct_ref, do_ref, sfx_ref, pre_ref, dq_ref, dk_ref, dv_ref, dq_acc, dk_acc, dv_acc):
        qi = pl.program_id(1)

        @pl.when(qi == 0)
        def _():
            dk_acc[...] = jnp.zeros_like(dk_acc)
            dv_acc[...] = jnp.zeros_like(dv_acc)

        tri_sfx = sfx_ref[...]
        tri_pre = pre_ref[...]
        qs = [(q_ref[:, h * dh:(h + 1) * dh].astype(F32) * scale).astype(BF16) for h in range(hps)]
        dos = [do_ref[:, h * dh:(h + 1) * dh] for h in range(hps)]
        totals = [jnp.max(jnp.broadcast_to(ct_ref[0, 0, h, 0:1, :], (LANE, tq)).T, axis=1, keepdims=True)
                  for h in range(hps)]
        dq_acc[...] = jnp.zeros_like(dq_acc)

        def part(h, rows, start, n_keys, pc, pg, mask):
            cols = slice(h * dh, (h + 1) * dh)
            q_rows, do_rows = qs[h][rows], dos[h][rows]
            k_blk = k_ref[pl.ds(start, n_keys), cols]
            v_blk = v_ref[pl.ds(start, n_keys), cols]
            z = _dot(q_rows, k_blk, NT)
            sp = _softplus(z)
            sig = jnp.exp(z - sp)
            if mask is not None:
                sp = jnp.where(mask, sp, 0.0)
            pc_next = pc + jnp.sum(sp, axis=-1, keepdims=True)
            wts = jnp.exp(z - (_dot(sp.astype(BF16), tri_sfx[0:n_keys, 0:n_keys]) + (totals[h][rows] - pc_next)))
            if mask is not None:
                wts = jnp.where(mask, wts, 0.0)
            g = _dot(do_rows, v_blk, NT) * wts
            dz = g - sig * (_dot(g.astype(BF16), tri_pre[0:n_keys, 0:n_keys]) + pg)
            if mask is not None:
                dz = jnp.where(mask, dz, 0.0)
            dz = dz.astype(BF16)
            dq_acc[rows, cols] += _dot(dz, k_blk)
            dk_acc[pl.ds(start, n_keys), cols] += _dot(dz, q_rows, TN)
            dv_acc[pl.ds(start, n_keys), cols] += _dot(wts.astype(BF16), do_rows, TN)
            return pc_next, pg + jnp.sum(g, axis=-1, keepdims=True)

        def block(start, carry):
            return tuple(part(h, slice(0, tq), start, tq, carry[h][0], carry[h][1], None) for h in range(hps))

        zero = jnp.zeros((tq, 1), F32)
        n_left = jnp.max(ct_ref[0, 0, 0, 8:16, :]).astype(jnp.int32)
        carry = lax.fori_loop(qi - n_left, qi - 1, lambda j, c: block(pl.multiple_of(j * tq, tq), c),
                              ((zero, zero),) * hps)
        below = lax.broadcasted_iota(jnp.int32, (tq, tq), 1) < lax.broadcasted_iota(jnp.int32, (tq, tq), 0)
        has_left = n_left > 0
        left = pl.multiple_of(jnp.maximum(qi - 1, 0) * tq, tq)
        for h in range(hps):
            pc, pg = part(h, slice(0, tq), left, tq, carry[h][0], carry[h][1], has_left)
            part(h, slice(0, tq), pl.multiple_of(qi * tq, tq), tq, pc, pg, below)
        dq_ref[...] = (dq_acc[...] * scale).astype(BF16)

        @pl.when(qi == nq - 1)
        def _():
            dk_ref[...] = dk_acc[...].astype(BF16)
            dv_ref[...] = dv_acc[...].astype(BF16)

    qb, kb, vb = q_off // hw, k_off // hw, v_off // hw
    G = heads // hps
    q_spec = pl.BlockSpec((tq, hw), lambda g, i: ((g // G) * nq + i, qb + g % G))
    o_spec = pl.BlockSpec((tq, hw), lambda g, i: ((g // G) * nq + i, g % G))
    kv_out = pl.BlockSpec((seq, hw), lambda g, i: (g // G, g % G))
    shp = jax.ShapeDtypeStruct((B * seq, heads * dh), BF16)
    body, c_in, c_args, c_out, c_shape, c_scratch = _hosted(body, 7, 3, comm, _grid_step(nq), B * G * nq)
    tri_spec = pl.BlockSpec((tq, tq), lambda g, i: (0, 0))
    ones = jnp.ones((tq, tq), BF16)
    out = pl.pallas_call(
        body, name=name, grid=(B * G, nq),
        in_specs=[q_spec,
                  pl.BlockSpec((seq, hw), lambda g, i: (g // G, kb + g % G)),
                  pl.BlockSpec((seq, hw), lambda g, i: (g // G, vb + g % G)),
                  pl.BlockSpec((1, 1, hps, 16, tq), lambda g, i: (g // G, i, g % G, 0, 0)), o_spec,
                  tri_spec, tri_spec] + c_in,
        out_specs=[o_spec, kv_out, kv_out] + c_out, out_shape=[shp, shp, shp] + c_shape,
        scratch_shapes=[pltpu.VMEM((tq, hw), F32), pltpu.VMEM((seq, hw), F32), pltpu.VMEM((seq, hw), F32)]
        + c_scratch,
        compiler_params=_cp("arbitrary" if comm else "parallel", "arbitrary"))(
            proj, proj, proj, ctot, do, jnp.tril(ones), jnp.triu(ones), *c_args)
    return (out[0], out[1], out[2], out[3:]) if comm else out


def _head_expand(n_heads, DI):
    j = jnp.arange(LANE, dtype=jnp.int32)[:, None]
    c = jnp.arange(DI, dtype=jnp.int32)[None, :] // SSM_HEAD_DIM
    return ((j == c) & (j < n_heads)).astype(BF16)


def _split3(x):
    hi = x.astype(BF16)
    r1 = x - hi.astype(F32)
    mid = r1.astype(BF16)
    return hi, mid, (r1 - mid.astype(F32)).astype(BF16)


def dt_fwd(proj, bias, a_log, expand, *, dt_off, name):
    T = proj.shape[0]
    DI = expand.shape[1]
    L = SSM_CHUNK
    tt = _pick(T, 512, L)

    def body(raw_ref, bias_ref, al_ref, e_ref, dt_ref, cs_ref, dtx_ref, csx_ref):
        x = raw_ref[...].astype(F32) + bias_ref[...]
        dt = _softplus(x)
        dt_ref[...] = dt
        la = dt * (-jnp.exp(al_ref[...]))
        tri = _tri01(L, True)
        for c in range(tt // L):
            cs_ref[c * L:(c + 1) * L, :] = _tri_dot3(tri, la[c * L:(c + 1) * L, :])
        e = e_ref[...]
        dtx_ref[...] = _dot(dt.astype(BF16), e).astype(BF16)
        hi, mid, lo = _split3(cs_ref[...])
        csx_ref[...] = _dot(hi, e) + _dot(mid, e) + _dot(lo, e)

    row = pl.BlockSpec((tt, LANE), lambda i: (i, 0))
    wide = pl.BlockSpec((tt, DI), lambda i: (i, 0))
    vec = pl.BlockSpec((1, LANE), lambda i: (0, 0))
    return pl.pallas_call(
        body, name=name, grid=(T // tt,),
        in_specs=[pl.BlockSpec((tt, LANE), lambda i: (i, dt_off // LANE)), vec, vec,
                  pl.BlockSpec((LANE, DI), lambda i: (0, 0))],
        out_specs=[row, row, wide, wide],
        out_shape=[jax.ShapeDtypeStruct((T, LANE), F32), jax.ShapeDtypeStruct((T, LANE), F32),
                   jax.ShapeDtypeStruct((T, DI), BF16), jax.ShapeDtypeStruct((T, DI), F32)],
        compiler_params=_cp("parallel"))(proj, bias, a_log, expand)


def dt_bwd(ddt_x, dcs_x, dcs_cols, proj, dt, bias, a_log, reduce_t, *, dt_off, n_heads, name):
    T = proj.shape[0]
    DI = reduce_t.shape[0]
    L = SSM_CHUNK
    tt = _pick(T, 512, L)

    def body(ddtx_ref, dcsx_ref, dcsc_ref, raw_ref, dt_ref, bias_ref, al_ref, r_ref, draw_ref, dbias_ref, dal_ref,
             dla_buf):
        @pl.when(pl.program_id(0) == 0)
        def _():
            dbias_ref[...] = jnp.zeros_like(dbias_ref)
            dal_ref[...] = jnp.zeros_like(dal_ref)

        r = r_ref[...]
        ddt = _dot(ddtx_ref[...], r)
        dx = dcsx_ref[...]
        hi = dx.astype(BF16)
        dcs = _dot(hi, r) + _dot((dx - hi.astype(F32)).astype(BF16), r) + dcsc_ref[...]
        triu = _tri01(L, False)
        for c in range(tt // L):
            dla_buf[c * L:(c + 1) * L, :] = _tri_dot3(triu, dcs[c * L:(c + 1) * L, :])
        dla = dla_buf[...]
        a = -jnp.exp(al_ref[...])
        valid = lax.broadcasted_iota(jnp.int32, (tt, LANE), 1) < n_heads
        dal_ref[...] += jnp.sum(jnp.where(valid, dla * dt_ref[...], 0.0), axis=0, keepdims=True) * a
        x = raw_ref[...].astype(F32) + bias_ref[...]
        draw = jnp.where(valid, (ddt + dla * a) * _sigmoid(x), 0.0)
        dbias_ref[...] += jnp.sum(draw, axis=0, keepdims=True)
        draw_ref[...] = draw.astype(BF16)

    row = pl.BlockSpec((tt, LANE), lambda i: (i, 0))
    wide = pl.BlockSpec((tt, DI), lambda i: (i, 0))
    vec = pl.BlockSpec((1, LANE), lambda i: (0, 0))
    return pl.pallas_call(
        body, name=name, grid=(T // tt,),
        in_specs=[wide, wide, row, pl.BlockSpec((tt, LANE), lambda i: (i, dt_off // LANE)), row, vec, vec,
                  pl.BlockSpec((DI, LANE), lambda i: (0, 0))],
        out_specs=[row, vec, vec],
        out_shape=[jax.ShapeDtypeStruct((T, LANE), BF16), jax.ShapeDtypeStruct((1, LANE), F32),
                   jax.ShapeDtypeStruct((1, LANE), F32)],
        scratch_shapes=[pltpu.VMEM((tt, LANE), F32)],
        compiler_params=_cp("arbitrary"))(ddt_x, dcs_x, dcs_cols, proj, dt, bias, a_log, reduce_t)


def _pair_terms(x_ref, dtx_ref, csx_ref, csr_ref, pair, ppg, lo_half, causal):
    L = SSM_CHUNK
    g, pp = divmod(pair, ppg)
    ra = g * HEAD_ROWS + 2 * pp
    cols = slice(pair * LANE, (pair + 1) * LANE)
    X = x_ref[:, cols].astype(F32)
    dt_p = dtx_ref[:, cols].astype(F32)
    own = csx_ref[:, cols]
    other = pltpu.roll(own, SSM_HEAD_DIM, 1)
    csa_c = jnp.where(lo_half, own, other)
    csb_c = jnp.where(lo_half, other, own)
    La = jnp.exp(jnp.where(causal, csa_c - csr_ref[0, ra:ra + 1, :], NEG_BIG))
    Lb = jnp.exp(jnp.where(causal, csb_c - csr_ref[0, ra + 1:ra + 2, :], NEG_BIG))
    last = csx_ref[L - 1:L, cols]
    return g, ra, cols, X, dt_p, La, Lb, jnp.exp(own), jnp.exp(last - own), jnp.exp(last)


def scan_fwd(xbc, dt_x, cs_x, cs_row, d_full, *, B, seq, DI, name):
    L, N, G = SSM_CHUNK, SSM_STATE, SSM_GROUPS
    nc = seq // L
    XW = xbc.shape[1]
    n_pairs = DI // LANE
    ppg = n_pairs // G

    def body(x_ref, dtx_ref, csx_ref, csr_ref, d_ref, y_ref, st_ref, state):
        c = pl.program_id(1)

        @pl.when(c == 0)
        def _():
            state[...] = jnp.zeros_like(state)

        causal = lax.broadcasted_iota(jnp.int32, (L, L), 0) >= lax.broadcasted_iota(jnp.int32, (L, L), 1)
        lo_half = lax.broadcasted_iota(jnp.int32, (L, LANE), 1) < SSM_HEAD_DIM
        cbs = []
        for g in range(G):
            Bc = x_ref[:, DI + g * N:DI + (g + 1) * N]
            Cc = x_ref[:, DI + G * N + g * N:DI + G * N + (g + 1) * N]
            cbs.append((Bc, Cc, _dot(Cc, Bc, NT)))
        for pair in range(n_pairs):
            g, _, cols, X, dt_p, La, Lb, ecs, tail, e_last = _pair_terms(
                x_ref, dtx_ref, csx_ref, csr_ref, pair, ppg, lo_half, causal)
            Bc, Cc, CB = cbs[g]
            xs = X * dt_p
            xsb = xs.astype(BF16)
            y = jnp.where(lo_half, _dot((CB * La).astype(BF16), xsb), _dot((CB * Lb).astype(BF16), xsb))
            ST = state[pair]
            st_ref[0, 0, pair] = ST
            y = y + ecs * _dot(Cc, ST.astype(BF16)) + d_ref[:, cols] * X
            y_ref[:, cols] = y.astype(BF16)
            state[pair] = e_last * ST + _dot(Bc, (xs * tail).astype(BF16), TN)

    wide = pl.BlockSpec((L, DI), lambda b, c: (b * nc + c, 0))
    return pl.pallas_call(
        body, name=name, grid=(B, nc),
        in_specs=[pl.BlockSpec((L, XW), lambda b, c: (b * nc + c, 0)), wide, wide,
                  pl.BlockSpec((1, G * HEAD_ROWS, L), lambda b, c: (b, 0, c)),
                  pl.BlockSpec((1, DI), lambda b, c: (0, 0))],
        out_specs=[wide, pl.BlockSpec((1, 1, n_pairs, N, LANE), lambda b, c: (b, c, 0, 0, 0))],
        out_shape=[jax.ShapeDtypeStruct((B * seq, DI), BF16),
                   jax.ShapeDtypeStruct((B, nc, n_pairs, N, LANE), F32)],
        scratch_shapes=[pltpu.VMEM((n_pairs, N, LANE), F32)],
        compiler_params=_cp("parallel", "arbitrary"))(xbc, dt_x, cs_x, cs_row, d_full)


def scan_bwd(xbc, dt_x, cs_x, cs_row, d_full, states, dy, *, B, seq, DI, name):
    L, N, G = SSM_CHUNK, SSM_STATE, SSM_GROUPS
    nc = seq // L
    XW = xbc.shape[1]
    n_pairs = DI // LANE
    ppg = n_pairs // G
    HR = G * HEAD_ROWS
    inv_p = 1.0 / SSM_HEAD_DIM

    def body(x_ref, dtx_ref, csx_ref, csr_ref, d_ref, st_ref, dy_ref, dx_ref, ddtx_ref, dcsx_ref, dcsr_ref, dd_ref,
             dH):
        c = pl.program_id(1)

        @pl.when(c == 0)
        def _():
            dH[...] = jnp.zeros_like(dH)
            dd_ref[...] = jnp.zeros_like(dd_ref)

        causal = lax.broadcasted_iota(jnp.int32, (L, L), 0) >= lax.broadcasted_iota(jnp.int32, (L, L), 1)
        lo_half = lax.broadcasted_iota(jnp.int32, (L, LANE), 1) < SSM_HEAD_DIM
        last_row = lax.broadcasted_iota(jnp.int32, (L, LANE), 0) == L - 1
        head_row = lax.broadcasted_iota(jnp.int32, (HR, 1), 0)
        dcs_rows = jnp.zeros((HR, L), F32)

        for g in range(G):
            Bc = x_ref[:, DI + g * N:DI + (g + 1) * N]
            Cc = x_ref[:, DI + G * N + g * N:DI + G * N + (g + 1) * N]
            CB = _dot(Cc, Bc, NT)
            dCB = jnp.zeros((L, L), F32)
            dC = jnp.zeros((L, N), F32)
            dB = jnp.zeros((L, N), F32)
            for pp in range(ppg):
                pair = g * ppg + pp
                _, ra, cols, X, dt_p, La, Lb, ecs, tail, e_last = _pair_terms(
                    x_ref, dtx_ref, csx_ref, csr_ref, pair, ppg, lo_half, causal)
                xs = X * dt_p
                xsb = xs.astype(BF16)
                Ma, Mb = CB * La, CB * Lb
                dY = dy_ref[:, cols].astype(F32)
                dYb = dY.astype(BF16)
                dMa = _dot(jnp.where(lo_half, dY, 0.0).astype(BF16), xsb, NT)
                dMb = _dot(jnp.where(lo_half, 0.0, dY).astype(BF16), xsb, NT)
                dSa, dSb = dMa * Ma, dMb * Mb
                dCB = dCB + dMa * La + dMb * Lb
                dcs = jnp.where(lo_half, jnp.sum(dSa, axis=1, keepdims=True), jnp.sum(dSb, axis=1, keepdims=True)) * inv_p
                dcs_rows = dcs_rows - jnp.where(head_row == ra, jnp.sum(dSa, axis=0, keepdims=True), 0.0)
                dcs_rows = dcs_rows - jnp.where(head_row == ra + 1, jnp.sum(dSb, axis=0, keepdims=True), 0.0)
                dxs = jnp.where(lo_half, _dot(Ma.astype(BF16), dYb, TN), _dot(Mb.astype(BF16), dYb, TN))
                ST = st_ref[0, 0, pair]
                STb = ST.astype(BF16)
                dYe = (dY * ecs).astype(BF16)
                dC = dC + _dot(dYe, STb, NT)
                dSTp = _dot(Cc, dYe, TN)
                dcs = dcs + dY * (ecs * _dot(Cc, STb))
                dSTn = dH[pair]
                dSTnb = dSTn.astype(BF16)
                dSTp = dSTp + e_last * dSTn
                XBt = _dot(Bc, dSTnb)
                dxs = dxs + tail * XBt
                t2 = xs * XBt * tail
                at_end = e_last * jnp.sum(dSTn * ST, axis=0, keepdims=True) + jnp.sum(t2, axis=0, keepdims=True)
                dcs = dcs - t2 + jnp.where(last_row, at_end, 0.0)
                dB = dB + _dot((xs * tail).astype(BF16), dSTnb, NT)
                dx_ref[:, cols] = (dxs * dt_p + d_ref[:, cols] * dY).astype(BF16)
                ddtx_ref[:, cols] = (dxs * X).astype(BF16)
                dcsx_ref[:, cols] = dcs
                dd_ref[0, :, cols] += jnp.sum(dY * X, axis=0, keepdims=True)
                dH[pair] = dSTp
            dCBb = dCB.astype(BF16)
            dx_ref[:, DI + g * N:DI + (g + 1) * N] = (dB + _dot(dCBb, Cc, TN)).astype(BF16)
            dx_ref[:, DI + G * N + g * N:DI + G * N + (g + 1) * N] = (dC + _dot(dCBb, Bc)).astype(BF16)
        dcsr_ref[0] = dcs_rows

    rev = lambda b, c: (b * nc + (nc - 1 - c), 0)
    wide = pl.BlockSpec((L, DI), rev)
    hrow = pl.BlockSpec((1, HR, L), lambda b, c: (b, 0, nc - 1 - c))
    return pl.pallas_call(
        body, name=name, grid=(B, nc),
        in_specs=[pl.BlockSpec((L, XW), rev), wide, wide, hrow,
                  pl.BlockSpec((1, DI), lambda b, c: (0, 0)),
                  pl.BlockSpec((1, 1, n_pairs, N, LANE), lambda b, c: (b, nc - 1 - c, 0, 0, 0)), wide],
        out_specs=[pl.BlockSpec((L, XW), rev), wide, wide, hrow, pl.BlockSpec((1, 1, DI), lambda b, c: (b, 0, 0))],
        out_shape=[jax.ShapeDtypeStruct((B * seq, XW), BF16), jax.ShapeDtypeStruct((B * seq, DI), BF16),
                   jax.ShapeDtypeStruct((B * seq, DI), F32), jax.ShapeDtypeStruct((B, HR, seq), F32),
                   jax.ShapeDtypeStruct((B, 1, DI), F32)],
        scratch_shapes=[pltpu.VMEM((n_pairs, N, LANE), F32)],
        compiler_params=_cp("parallel", "arbitrary"))(xbc, dt_x, cs_x, cs_row, d_full, states, dy)


def gnorm_fwd(y, proj, w, *, DI, name):
    T = y.shape[0]
    tt = _pick(T, 256, 16)
    gw = DI // SSM_GROUPS

    def body(y_ref, z_ref, w_ref, o_ref):
        for g in range(SSM_GROUPS):
            sl = slice(g * gw, (g + 1) * gw)
            y2 = y_ref[:, sl].astype(F32) * _silu(z_ref[:, sl].astype(F32))
            r = lax.rsqrt(jnp.mean(y2 * y2, axis=-1, keepdims=True) + EPS)
            o_ref[:, sl] = (y2 * r * w_ref[:, sl]).astype(BF16)

    row = pl.BlockSpec((tt, DI), lambda i: (i, 0))
    return pl.pallas_call(
        body, name=name, grid=(T // tt,),
        in_specs=[row, row, pl.BlockSpec((1, DI), lambda i: (0, 0))], out_specs=row,
        out_shape=jax.ShapeDtypeStruct((T, DI), BF16), compiler_params=_cp("parallel"))(y, proj, w)


def gnorm_bwd(dyn, y, proj, w, *, DI, name):
    T = y.shape[0]
    tt = _pick(T, 256, 16)
    gw = DI // SSM_GROUPS

    def body(dyn_ref, y_ref, z_ref, w_ref, dy_ref, dz_ref, dw_ref):
        @pl.when(pl.program_id(0) == 0)
        def _():
            dw_ref[...] = jnp.zeros_like(dw_ref)

        for g in range(SSM_GROUPS):
            sl = slice(g * gw, (g + 1) * gw)
            yv = y_ref[:, sl].astype(F32)
            z = z_ref[:, sl].astype(F32)
            sz = _silu(z)
            y2 = yv * sz
            r = lax.rsqrt(jnp.mean(y2 * y2, axis=-1, keepdims=True) + EPS)
            xhat = y2 * r
            d = dyn_ref[:, sl].astype(F32)
            dw_ref[:, sl] += jnp.sum(d * xhat, axis=0, keepdims=True)
            dxh = d * w_ref[:, sl]
            dy2 = r * (dxh - xhat * jnp.mean(dxh * xhat, axis=-1, keepdims=True))
            dy_ref[:, sl] = (dy2 * sz).astype(BF16)
            dz_ref[:, sl] = (dy2 * yv * _silu_grad(z)).astype(BF16)

    row = pl.BlockSpec((tt, DI), lambda i: (i, 0))
    vec = pl.BlockSpec((1, DI), lambda i: (0, 0))
    shp = jax.ShapeDtypeStruct((T, DI), BF16)
    return pl.pallas_call(
        body, name=name, grid=(T // tt,), in_specs=[row, row, row, vec], out_specs=[row, row, vec],
        out_shape=[shp, shp, jax.ShapeDtypeStruct((1, DI), F32)],
        compiler_params=_cp("arbitrary"))(dyn, y, proj, w)


N_CHIP = 4
SHARD_ROW_ALIGN = 128


def _comm_out_shapes(srcs, modes):
    return [jax.ShapeDtypeStruct(((N_DEV,) if mode in ("gather", "gather_direct") else ()) + s.shape, s.dtype)
            for s, mode in zip(srcs, modes)]


def _comm_scratch(n):
    return [pltpu.SemaphoreType.DMA((n, N_DEV - 1)), pltpu.SemaphoreType.DMA((n, N_DEV - 1)),
            pltpu.SemaphoreType.DMA((n,))]


def _comm_phases(modes, src_refs, out_refs, send_sems, recv_sems, local_sems):
    x, y, c = lax.axis_index("x"), lax.axis_index("y"), lax.axis_index("c")
    me, sibling = (x, y, c), (x, y, 1 - c)
    chips = [(1 - x, y), (x, 1 - y), (1 - x, 1 - y)]
    relays = [a for a, mode in enumerate(modes) if mode == "gather"]

    def slot(p):
        return 4 * p[0] + 2 * p[1] + p[2]

    def remote(a, k, src, dst, to):
        return pltpu.make_async_remote_copy(src_ref=src, dst_ref=dst, send_sem=send_sems.at[a, k],
                                            recv_sem=recv_sems.at[a, k], device_id=to,
                                            device_id_type=pl.DeviceIdType.MESH)

    def first_copies():
        local, two_way, send_only = [], [], []
        for a, mode in enumerate(modes):
            src, out = src_refs[a], out_refs[a]
            if mode == "sibling":
                two_way.append(remote(a, 0, src, out, sibling))
            elif mode == "chips":
                mine = 2 * x + y
                local.append(pltpu.make_async_copy(src.at[mine], out.at[mine], local_sems.at[a]))
                for j, chip in enumerate(chips):
                    two_way.append(remote(a, 1 + j, src.at[2 * chip[0] + chip[1]], out.at[mine], (*chip, c)))
            elif mode == "gather_direct":
                local.append(pltpu.make_async_copy(src, out.at[slot(me)], local_sems.at[a]))
                for k in range(1, N_DEV):
                    peer = (1 - x if k & 4 else x, 1 - y if k & 2 else y, 1 - c if k & 1 else c)
                    two_way.append(remote(a, k - 1, src, out.at[slot(me)], peer))
            else:
                assert mode == "gather"
                local.append(pltpu.make_async_copy(src, out.at[slot(me)], local_sems.at[a]))
                send_only.append(remote(a, 0, src, out.at[slot(me)], sibling))
                for j, chip in enumerate(chips):
                    send_only.append(remote(a, 1 + j, src, out.at[slot(me)], (*chip, c)))
        return local, two_way, send_only

    def forwards():
        out = []
        for a in relays:
            for j, chip in enumerate(chips):
                landed = out_refs[a].at[slot((*chip, c))]
                out.append((remote(a, 1 + j, landed, landed, me), remote(a, 4 + j, landed, landed, sibling)))
        return out

    def start():
        local, two_way, send_only = first_copies()
        for cp in local + two_way + send_only:
            cp.start()

    def relay():
        for arrival, fwd in forwards():
            arrival.wait_recv()
            fwd.start()

    def finish():
        local, two_way, send_only = first_copies()
        for a in relays:
            blk = out_refs[a].at[slot(sibling)]
            remote(a, 0, blk, blk, me).wait_recv()
            for j, chip in enumerate(chips):
                blk = out_refs[a].at[slot((*chip, 1 - c))]
                remote(a, 4 + j, blk, blk, me).wait_recv()
        for cp in send_only + [fwd for _, fwd in forwards()]:
            cp.wait_send()
        for cp in two_way + local:
            cp.wait()

    return start, relay, finish, bool(relays)


def _hosted(body, n_in, n_out, comm, step, n_steps):
    if comm is None:
        return body, [], [], [], [], []
    srcs, modes = comm
    nc = len(srcs)

    def wrapped(*refs):
        ins, csrc = refs[:n_in], refs[n_in:n_in + nc]
        outs = refs[n_in + nc:n_in + nc + n_out]
        cout = refs[n_in + nc + n_out:n_in + 2 * nc + n_out]
        scratch = refs[n_in + 2 * nc + n_out:len(refs) - 3]
        start, relay, finish, has_relay = _comm_phases(modes, csrc, cout, *refs[len(refs) - 3:])
        s = step()
        pl.when(s == 0)(start)
        body(*ins, *outs, *scratch)
        if has_relay:
            pl.when(s == (2 * n_steps) // 3)(relay)
        pl.when(s == n_steps - 1)(finish)

    any_spec = pl.BlockSpec(memory_space=pl.ANY)
    return wrapped, [any_spec] * nc, list(srcs), [any_spec] * nc, _comm_out_shapes(srcs, modes), _comm_scratch(nc)


def exchange(srcs, modes, *, name):
    n = len(srcs)

    def body(*refs):
        start, relay, finish, has_relay = _comm_phases(modes, refs[:n], refs[n:2 * n], *refs[2 * n:])
        start()
        if has_relay:
            relay()
        finish()

    any_spec = pl.BlockSpec(memory_space=pl.ANY)
    return pl.pallas_call(
        body, name=name, in_specs=[any_spec] * n, out_specs=[any_spec] * n, out_shape=_comm_out_shapes(srcs, modes),
        scratch_shapes=_comm_scratch(n), compiler_params=pltpu.CompilerParams(has_side_effects=True))(*srcs)


def pair_sum(a, b, *, name):
    n, R, C = a.shape
    tr = _pick(n * R, 1024, 16)

    def body(a_ref, b_ref, o_ref):
        o_ref[...] = (a_ref[...].astype(F32) + b_ref[...].astype(F32)).astype(BF16)

    blk = pl.BlockSpec((tr, C), lambda i: (i, 0))
    out = pl.pallas_call(
        body, name=name, grid=(n * R // tr,), in_specs=[blk, blk], out_specs=blk,
        out_shape=jax.ShapeDtypeStruct((n * R, C), BF16),
        compiler_params=_cp("parallel"))(a.reshape(n * R, C), b.reshape(n * R, C))
    return out.reshape(n, R, C)


def sum_slots(recv, *, name):
    _, R, C = recv.shape
    tr = _pick(R, 512, 8)

    def body(r_ref, o_ref):
        acc = r_ref[0].astype(F32)
        for p in range(1, N_DEV):
            acc = acc + r_ref[p].astype(F32)
        o_ref[...] = acc

    return pl.pallas_call(
        body, name=name, grid=(R // tr,),
        in_specs=[pl.BlockSpec((N_DEV, tr, C), lambda i: (0, i, 0))],
        out_specs=pl.BlockSpec((tr, C), lambda i: (i, 0)),
        out_shape=jax.ShapeDtypeStruct((R, C), F32), compiler_params=_cp("parallel"))(recv)


def adamw(gsrc, w, m, v, *, name):
    slots, R, C = gsrc.shape
    tr = _pick(R, 256, 16 if gsrc.dtype == BF16 else 8)
    c1 = 1.0 / (1.0 - ADAM_B1 ** ADAM_STEP)
    c2 = 1.0 / (1.0 - ADAM_B2 ** ADAM_STEP)

    def body(g_ref, w_ref, m_ref, v_ref, go_ref, d_ref, mo_ref, vo_ref):
        g = g_ref[0].astype(F32)
        for p in range(1, slots):
            g = g + g_ref[p].astype(F32)
        m2 = ADAM_B1 * m_ref[...] + (1.0 - ADAM_B1) * g
        v2 = ADAM_B2 * v_ref[...] + (1.0 - ADAM_B2) * (g * g)
        go_ref[...] = g
        mo_ref[...] = m2
        vo_ref[...] = v2
        d_ref[...] = -ADAM_LR * ((m2 * c1) / (jnp.sqrt(v2 * c2) + ADAM_EPS) + ADAM_WD * w_ref[...])

    blk = pl.BlockSpec((tr, C), lambda i: (i, 0))
    shp = jax.ShapeDtypeStruct((R, C), F32)
    return pl.pallas_call(
        body, name=name, grid=(R // tr,),
        in_specs=[pl.BlockSpec((slots, tr, C), lambda i: (0, i, 0)), blk, blk, blk],
        out_specs=[blk] * 4, out_shape=[shp] * 4, compiler_params=_cp("parallel"))(gsrc, w, m, v)


def _pad_cols(a, n):
    return jnp.pad(a, ((0, 0), (0, n - a.shape[1])))


def _to_rows(a, B, seq, H):
    G = SSM_GROUPS
    R = H // G
    t = a[:, :H].reshape(B, seq, G, R).transpose(0, 2, 3, 1)
    t = jnp.pad(t, ((0, 0), (0, 0), (0, HEAD_ROWS - R), (0, 0)))
    return t.reshape(B, G * HEAD_ROWS, seq)


def _from_rows(a, B, seq, H):
    G = SSM_GROUPS
    R = H // G
    t = a.reshape(B, G, HEAD_ROWS, seq)[:, :, :R].transpose(0, 3, 1, 2).reshape(B * seq, H)
    return _pad_cols(t, LANE)


def _chip_split(grads):
    c_idx = lax.axis_index("c")
    keep, give = [], []
    for g in grads:
        by_chip = g.reshape((N_CHIP, 2) + g.shape[1:])
        keep.append(lax.dynamic_index_in_dim(by_chip, c_idx, axis=1, keepdims=False))
        give.append(lax.dynamic_index_in_dim(by_chip, 1 - c_idx, axis=1, keepdims=False))
    return keep, give


def _chip_sums(grads, name):
    keep, give = _chip_split(grads)
    swapped = exchange(give, ["sibling"] * len(give), name="swap_" + name)
    return [pair_sum(k, s, name=f"chip_sum_{name}_{i}") for i, (k, s) in enumerate(zip(keep, swapped))]


def local_step(x, target, loc, od_w_in_t, *, B, seq):
    T, D = x.shape
    CW = D
    heads = CW // SB_HEAD_DIM
    DI = 2 * D
    H = DI // SSM_HEAD_DIM
    XW = DI + 2 * SSM_GROUPS * SSM_STATE
    in_odd = DI + XW + H
    w1_rows = in_odd // N_DEV
    q_off, k_off, v_off, gc_off, ga_off = 3 * CW, 4 * CW, 5 * CW, 2 * CW, 6 * CW
    dt_off = DI + XW

    small_packed, small_spans = _pack_rows([loc[n] for n in SMALL_SHARDED], LANE, 8)
    n0, (g_ev_in, small_all) = rmsnorm_fwd(x, loc["ev_norm_w"], name="l0_norm",
                                           comm=([loc["ev_w_in"].astype(BF16), small_packed],
                                                 ["gather", "gather_direct"]))
    p = {n: loc[n] for n in SMALL}
    for n, a in zip(SMALL_SHARDED, _unpack_rows(small_all, small_spans)):
        p[n] = _col_unshards(a)
    p["ev_w_in"] = _col_unshards(g_ev_in)
    proj0, (g_od_in_t, g_od_out, g_ev_out) = mm_nn(
        n0, p["ev_w_in"], out_dtype=BF16, name="l0_in_proj",
        comm=([od_w_in_t.astype(BF16), loc["od_w_out"].astype(BF16), loc["ev_w_out"].astype(BF16)], ["gather"] * 3))
    p["ev_w_out"] = g_ev_out.reshape(-1, D)
    w1t = g_od_in_t[:, :w1_rows].reshape(in_odd, D)
    w1t = jnp.pad(w1t, ((0, -(-(in_odd + LANE) // 256) * 256 - in_odd), (0, 0)))
    od_w_out = g_od_out.reshape(-1, D)
    (u2,) = dwconv_fwd(proj0, (0, CW), p["ev_dw_w"], p["ev_dw_b"], C=CW, seq=seq, glu=True, silu_out=False,
                       name="l0_conv")
    o, ctot = sba_fwd(proj0, B=B, seq=seq, heads=heads, q_off=q_off, k_off=k_off, v_off=v_off, name="l0_attn")
    ycat = mix0_post_fwd(u2, proj0, o, p["ev_ln_w"], p["ev_ln_b"], CW=CW, gc_off=gc_off, ga_off=ga_off,
                         name="l0_post")
    h1 = mm_nn(ycat, p["ev_w_out"], add=x, out_dtype=F32, name="l0_out_proj")

    n1 = rmsnorm_fwd(h1, p["od_norm_w"], name="l1_norm")
    proj1 = mm_nt_terms([(n1, 0, D, 0)], w1t, out_dtype=BF16, name="l1_in_proj")
    u_pre, xbc = dwconv_fwd(proj1, (DI,), p["od_conv_w"], p["od_conv_b"], C=XW, seq=seq, glu=False, silu_out=True,
                            name="l1_conv")
    bias_p, alog_p = _pad_cols(p["od_dt_bias"], LANE), _pad_cols(p["od_a_log"], LANE)
    expand = _head_expand(H, DI)
    dt, cs, dt_x, cs_x = dt_fwd(proj1, bias_p, alog_p, expand, dt_off=dt_off, name="l1_dt")
    cs_row = _to_rows(cs, B, seq, H)
    d_full = jnp.repeat(p["od_d"], SSM_HEAD_DIM, axis=1)
    y_ssd, states = scan_fwd(xbc, dt_x, cs_x, cs_row, d_full, B=B, seq=seq, DI=DI, name="l1_ssd")
    yn = gnorm_fwd(y_ssd, proj1, p["od_gnorm_w"], DI=DI, name="l1_gnorm")
    h2 = mm_nn(yn, od_w_out, add=h1, out_dtype=F32, name="l1_out_proj")

    loss, dh2, dh2b, g_final = final_loss(h2, p["final_norm_w"], target, name="loss_head")

    g_od_w_out = mm_tn(yn, dh2b, out_dtype=BF16, name="l1_dw_out")
    dyn = mm_nt_terms([(dh2b, 0, D, 0)], od_w_out, out_dtype=BF16, name="l1_d_out_proj")
    dy_ssd, dz, g_gnorm = gnorm_bwd(dyn, y_ssd, proj1, p["od_gnorm_w"], DI=DI, name="l1_gnorm_bwd")
    dxbc_c, ddt_x, dcs_x, dcs_row, dd_part = scan_bwd(xbc, dt_x, cs_x, cs_row, d_full, states, dy_ssd, B=B, seq=seq,
                                                      DI=DI, name="l1_ssd_bwd")
    g_d = dd_part.sum(axis=(0, 1)).reshape(H, SSM_HEAD_DIM).sum(axis=1)[None, :]
    draw, g_bias, g_alog = dt_bwd(ddt_x, dcs_x, _from_rows(dcs_row, B, seq, H), proj1, dt, bias_p, alog_p, expand.T,
                                  dt_off=dt_off, n_heads=H, name="l1_dt_bwd")
    dxbc, g_conv_w, g_conv_b = dwconv_bwd(dxbc_c, u_pre, proj1, (DI,), p["od_conv_w"], C=XW, seq=seq, glu=False,
                                          silu_out=True, name="l1_conv_bwd")
    tw = 512 if DI % 512 == 0 else LANE
    terms = [(dz, j, tw, j * tw) for j in range(DI // tw)]
    terms += [(dxbc, j, tw, DI + j * tw) for j in range(XW // tw)]
    terms += [(draw, 0, LANE, dt_off)]
    dn1 = mm_nn_terms(terms, w1t, out_dtype=BF16, name="l1_d_in_proj")
    g_od_w_in_t = jnp.concatenate([mm_tn(dz, n1, out_dtype=BF16, name="l1_dw_in_z"),
                                   mm_tn(dxbc, n1, out_dtype=BF16, name="l1_dw_in_xbc"),
                                   mm_tn(draw, n1, out_dtype=BF16, name="l1_dw_in_dt")], axis=0)[:in_odd]
    w1_pad = (-w1_rows) % SHARD_ROW_ALIGN
    keep, give = _chip_split([jnp.pad(g_od_w_in_t.reshape(N_DEV, w1_rows, D), ((0, 0), (0, w1_pad), (0, 0))),
                              g_od_w_out.reshape(N_DEV, -1, D)])
    dh1, dh1b, g_od_norm, swapped = rmsnorm_bwd(h1, p["od_norm_w"], dn1, dh2, name="l1_norm_bwd",
                                                comm=(give, ["sibling"] * 2))
    l1_chip = [pair_sum(k, s, name=f"chip_sum_l1_{i}") for i, (k, s) in enumerate(zip(keep, swapped))]

    g_ev_w_out = mm_tn(ycat, dh1b, out_dtype=BF16, name="l0_dw_out")
    dycat = mm_nt_terms([(dh1b, 0, D, 0)], p["ev_w_out"], out_dtype=BF16, name="l0_d_out_proj")
    du2, dgc, dga, do, g_ln_w, g_ln_b = mix0_post_bwd(dycat, u2, proj0, o, p["ev_ln_w"], p["ev_ln_b"], CW=CW,
                                                      gc_off=gc_off, ga_off=ga_off, name="l0_post_bwd")
    dq, dk, dv, (r_od_in_t, r_od_out) = sba_bwd(proj0, ctot, do, B=B, seq=seq, heads=heads, q_off=q_off, k_off=k_off,
                                                v_off=v_off, name="l0_attn_bwd", comm=(l1_chip, ["chips", "chips"]))
    dga_a, dga_b, g_dw_w, g_dw_b = dwconv_bwd(du2, None, proj0, (0, CW), p["ev_dw_w"], C=CW, seq=seq, glu=True,
                                              silu_out=False, name="l0_conv_bwd")
    pieces = [dga_a, dga_b, dgc, dq, dk, dv, dga]
    g_ev_w_in = jnp.concatenate([mm_tn(n0, pc, out_dtype=BF16, name=f"l0_dw_in_{j}") for j, pc in enumerate(pieces)],
                                axis=1)
    l0_chip = _chip_sums([_col_shards(g_ev_w_in), g_ev_w_out.reshape(N_DEV, -1, D)], "l0")
    dn0, (r_ev_in, r_ev_out) = mm_nt_terms([(pc, 0, CW, j * CW) for j, pc in enumerate(pieces)], p["ev_w_in"],
                                           out_dtype=BF16, name="l0_d_in_proj", comm=(l0_chip, ["chips", "chips"]))
    dx, _, g_ev_norm = rmsnorm_bwd(x, p["ev_norm_w"], dn0, dh1, name="l0_norm_bwd")

    small = dict(ev_norm_w=g_ev_norm, ev_dw_w=g_dw_w, ev_dw_b=g_dw_b, ev_ln_w=g_ln_w, ev_ln_b=g_ln_b,
                 od_norm_w=g_od_norm, od_conv_w=g_conv_w, od_conv_b=g_conv_b, od_dt_bias=g_bias[:, :H],
                 od_a_log=g_alog[:, :H], od_d=g_d, od_gnorm_w=g_gnorm, final_norm_w=g_final)
    received = dict(ev_w_in=r_ev_in, ev_w_out=r_ev_out, od_w_in=r_od_in_t, od_w_out=r_od_out)
    return loss, dx, small, received


BIG = ("ev_w_in", "ev_w_out", "od_w_in", "od_w_out")
SMALL = ("ev_norm_w", "ev_dw_w", "ev_dw_b", "ev_ln_w", "ev_ln_b", "od_norm_w", "od_conv_w", "od_conv_b",
         "od_dt_bias", "od_a_log", "od_d", "od_gnorm_w", "final_norm_w")
SMALL_SHARDED = ("ev_dw_w", "od_norm_w", "od_conv_w", "od_conv_b", "od_gnorm_w")
ORDER = ("ev_norm_w", "ev_w_in", "ev_dw_w", "ev_dw_b", "ev_ln_w", "ev_ln_b", "ev_w_out", "od_norm_w", "od_w_in",
         "od_conv_w", "od_conv_b", "od_dt_bias", "od_a_log", "od_d", "od_gnorm_w", "od_w_out", "final_norm_w")


def _pack_rows(arrs, width, row_align):
    parts, spans, r0 = [], [], 0
    for a in arrs:
        flat = a.reshape(-1)
        rows = -(-flat.shape[0] // (width * row_align)) * row_align
        parts.append(jnp.pad(flat, (0, rows * width - flat.shape[0])).reshape(rows, width))
        spans.append((r0, a.size, a.shape))
        r0 += rows
    return jnp.concatenate(parts, axis=0), spans


def _unpack_rows(packed, spans):
    lead = packed.shape[:-2]
    width = packed.shape[-1]
    out = []
    for r0, size, shape in spans:
        rows = -(-size // width)
        blk = packed[..., r0:r0 + rows, :].reshape(lead + (rows * width,))[..., :size]
        out.append(blk.reshape(lead + tuple(shape)))
    return out


def _col_shards(a):
    R, C8 = a.shape
    return a.reshape(R, N_DEV, C8 // N_DEV).transpose(1, 0, 2)


def _col_unshards(a):
    n, R, C = a.shape
    return a.transpose(1, 0, 2).reshape(R, n * C)


def kernel(x, ev_norm_w, ev_w_in, ev_dw_w, ev_dw_b, ev_ln_w, ev_ln_b, ev_w_out, od_norm_w, od_w_in, od_conv_w, od_conv_b, od_dt_bias, od_a_log, od_d, od_gnorm_w, od_w_out, final_norm_w, loss_target, m_ev_norm_w, m_ev_w_in, m_ev_dw_w, m_ev_dw_b, m_ev_ln_w, m_ev_ln_b, m_ev_w_out, m_od_norm_w, m_od_w_in, m_od_conv_w, m_od_conv_b, m_od_dt_bias, m_od_a_log, m_od_d, m_od_gnorm_w, m_od_w_out, m_final_norm_w, v_ev_norm_w, v_ev_w_in, v_ev_dw_w, v_ev_dw_b, v_ev_ln_w, v_ev_ln_b, v_ev_w_out, v_od_norm_w, v_od_w_in, v_od_conv_w, v_od_conv_b, v_od_dt_bias, v_od_a_log, v_od_d, v_od_gnorm_w, v_od_w_out, v_final_norm_w):
    loc = dict(ev_norm_w=ev_norm_w, ev_w_in=ev_w_in, ev_dw_w=ev_dw_w, ev_dw_b=ev_dw_b, ev_ln_w=ev_ln_w,
               ev_ln_b=ev_ln_b, ev_w_out=ev_w_out, od_norm_w=od_norm_w, od_w_in=od_w_in, od_conv_w=od_conv_w,
               od_conv_b=od_conv_b, od_dt_bias=od_dt_bias, od_a_log=od_a_log, od_d=od_d, od_gnorm_w=od_gnorm_w,
               od_w_out=od_w_out, final_norm_w=final_norm_w)
    mom = dict(ev_norm_w=m_ev_norm_w, ev_w_in=m_ev_w_in, ev_dw_w=m_ev_dw_w, ev_dw_b=m_ev_dw_b, ev_ln_w=m_ev_ln_w,
               ev_ln_b=m_ev_ln_b, ev_w_out=m_ev_w_out, od_norm_w=m_od_norm_w, od_w_in=m_od_w_in,
               od_conv_w=m_od_conv_w, od_conv_b=m_od_conv_b, od_dt_bias=m_od_dt_bias, od_a_log=m_od_a_log,
               od_d=m_od_d, od_gnorm_w=m_od_gnorm_w, od_w_out=m_od_w_out, final_norm_w=m_final_norm_w)
    var = dict(ev_norm_w=v_ev_norm_w, ev_w_in=v_ev_w_in, ev_dw_w=v_ev_dw_w, ev_dw_b=v_ev_dw_b, ev_ln_w=v_ev_ln_w,
               ev_ln_b=v_ev_ln_b, ev_w_out=v_ev_w_out, od_norm_w=v_od_norm_w, od_w_in=v_od_w_in,
               od_conv_w=v_od_conv_w, od_conv_b=v_od_conv_b, od_dt_bias=v_od_dt_bias, od_a_log=v_od_a_log,
               od_d=v_od_d, od_gnorm_w=v_od_gnorm_w, od_w_out=v_od_w_out, final_norm_w=v_final_norm_w)
    shapes = {n: loc[n].shape for n in ORDER}
    loc = {n: (a.reshape(1, -1) if a.ndim == 1 else a.reshape(a.shape[-2:]) if a.ndim == 3 else a)
           for n, a in loc.items()}
    mom = {n: a.reshape(loc[n].shape) for n, a in mom.items()}
    var = {n: a.reshape(loc[n].shape) for n, a in var.items()}

    B, seq, D = x.shape
    me = 4 * lax.axis_index("x") + 2 * lax.axis_index("y") + lax.axis_index("c")

    w1_rows = loc["od_w_in"].shape[1]
    w1_pad = (-w1_rows) % SHARD_ROW_ALIGN

    def to_t(a):
        return jnp.pad(a.T, ((0, w1_pad), (0, 0)))

    loss, dx, grads, received = local_step(x.reshape(B * seq, D), loss_target.reshape(B * seq, D), loc,
                                           to_t(loc["od_w_in"]), B=B, seq=seq)

    gsmall_packed, gsmall_spans = _pack_rows([grads[n] for n in SMALL] + [loss], LANE, 8)
    (gsmall_recv,) = exchange([gsmall_packed], ["gather_direct"], name="gather_small_grads")

    big_out = [{} for _ in range(4)]
    for n in ("ev_w_in", "ev_w_out", "od_w_out"):
        for kind, a in enumerate(adamw(received[n], loc[n], mom[n], var[n], name="adamw_" + n)):
            big_out[kind][n] = a
    for kind, a in enumerate(adamw(received["od_w_in"], to_t(loc["od_w_in"]), to_t(mom["od_w_in"]),
                                   to_t(var["od_w_in"]), name="adamw_od_w_in")):
        big_out[kind]["od_w_in"] = a[:w1_rows].T

    summed = _unpack_rows(sum_slots(gsmall_recv, name="sum_small_grads"), gsmall_spans)
    loss_total = summed[-1][0, 0]
    gsmall = dict(zip(SMALL, summed[:-1]))
    for n in SMALL_SHARDED:
        width = loc[n].shape[1]
        gsmall[n] = lax.dynamic_slice_in_dim(gsmall[n], me * width, width, axis=1)
    gs, sspans = _pack_rows([gsmall[n] for n in SMALL], LANE, 8)
    ws, _ = _pack_rows([loc[n] for n in SMALL], LANE, 8)
    ms, _ = _pack_rows([mom[n] for n in SMALL], LANE, 8)
    vs, _ = _pack_rows([var[n] for n in SMALL], LANE, 8)
    small_out = [dict(zip(SMALL, _unpack_rows(a, sspans))) for a in adamw(gs[None], ws, ms, vs, name="adamw_small")]

    outs = [loss_total, dx.reshape(B, seq, D)]
    for kind in range(4):
        for n in ORDER:
            src = big_out[kind] if n in BIG else small_out[kind]
            outs.append(src[n].reshape(shapes[n]))
    return tuple(outs)
```

```python
import jax
import jax.numpy as jnp
from jax import lax
from jax.experimental import pallas as pl
from jax.experimental.pallas import tpu as pltpu

F32 = jnp.float32
BF16 = jnp.bfloat16

EPS = 1e-6
N_DEV = 8
LANE = 128
VMEM_LIMIT_BYTES = 48 * 1024 * 1024

SB_HEAD_DIM = 128
SSM_HEAD_DIM = 64
SSM_GROUPS = 4
SSM_STATE = 128
SSM_CHUNK = 128
HALO = 32
HEAD_ROWS = 8
NEG_BIG = -1e30

ADAM_LR = 0.001
ADAM_B1 = 0.9
ADAM_B2 = 0.999
ADAM_EPS = 1e-08
ADAM_WD = 0.01
ADAM_STEP = 10

NT = (((1,), (1,)), ((), ()))
TN = (((0,), (0,)), ((), ()))


def _cp(*sem):
    return pltpu.CompilerParams(dimension_semantics=sem, vmem_limit_bytes=VMEM_LIMIT_BYTES)


def _pick(n, cap, align):
    if n <= cap:
        return n
    t = (cap // align) * align
    while t >= align:
        if n % t == 0:
            return t
        t -= align
    raise ValueError(f"no tile for {n} (cap {cap}, align {align})")


def _sigmoid(x):
    return 0.5 * jnp.tanh(0.5 * x) + 0.5


def _silu(x):
    return x * _sigmoid(x)


def _silu_grad(x):
    s = _sigmoid(x)
    return s * (1.0 + x * (1.0 - s))


def _dot(a, b, dims=None):
    if dims is None:
        return jnp.dot(a, b, preferred_element_type=F32)
    return lax.dot_general(a, b, dims, preferred_element_type=F32)


def _tri_dot3(tri, x):
    hi = x.astype(BF16)
    r1 = x - hi.astype(F32)
    mid = r1.astype(BF16)
    lo = (r1 - mid.astype(F32)).astype(BF16)
    return _dot(tri, hi) + _dot(tri, mid) + _dot(tri, lo)


def _grid_step(n_inner):
    return lambda: pl.program_id(0) * n_inner + pl.program_id(1)


def mm_nn(a, b, *, add=None, out_dtype, name, comm=None):
    M, K = a.shape
    N = b.shape[1]
    tm = _pick(M, 2048 if K <= 1024 and add is None else 1024, 16)
    tn = _pick(N, 1024, LANE)

    def body(*refs):
        if add is None:
            a_ref, b_ref, o_ref = refs
        else:
            a_ref, b_ref, add_ref, o_ref = refs
        acc = _dot(a_ref[...], b_ref[...])
        if add is not None:
            acc = acc + add_ref[...]
        o_ref[...] = acc.astype(out_dtype)

    in_specs = [pl.BlockSpec((tm, K), lambda i, j: (i, 0)), pl.BlockSpec((K, tn), lambda i, j: (0, j))]
    args = [a, b]
    if add is not None:
        in_specs.append(pl.BlockSpec((tm, tn), lambda i, j: (i, j)))
        args.append(add)
    grid = (M // tm, N // tn)
    body, c_in, c_args, c_out, c_shape, c_scratch = _hosted(body, len(args), 1, comm, _grid_step(grid[1]),
                                                            grid[0] * grid[1])
    out = pl.pallas_call(
        body, name=name, grid=grid, in_specs=in_specs + c_in,
        out_specs=[pl.BlockSpec((tm, tn), lambda i, j: (i, j))] + c_out,
        out_shape=[jax.ShapeDtypeStruct((M, N), out_dtype)] + c_shape, scratch_shapes=c_scratch,
        compiler_params=_cp(*(("arbitrary",) * 2 if comm else ("parallel",) * 2)))(*args, *c_args)
    return (out[0], out[1:]) if comm else out[0]


def mm_nt_terms(terms, b, *, out_dtype, name, comm=None):
    M = terms[0][0].shape[0]
    N = b.shape[0]
    n_terms = len(terms)
    if n_terms == 1:
        tm, tn = _pick(M, 2048, 16), _pick(N, 1024, LANE)
    else:
        tm, tn = _pick(M, 256, 16), _pick(N, 1024, LANE)

    def body(*refs):
        o_ref = refs[-1]
        acc = None
        for t in range(n_terms):
            part = _dot(refs[2 * t][...], refs[2 * t + 1][...], NT)
            acc = part if acc is None else acc + part
        o_ref[...] = acc.astype(out_dtype)

    in_specs, args = [], []
    for arr, cb, w, off in terms:
        assert off % w == 0
        in_specs.append(pl.BlockSpec((tm, w), lambda i, j, cb=cb: (i, cb)))
        in_specs.append(pl.BlockSpec((tn, w), lambda i, j, ob=off // w: (j, ob)))
        args += [arr, b]
    grid = (M // tm, N // tn)
    body, c_in, c_args, c_out, c_shape, c_scratch = _hosted(body, len(args), 1, comm, _grid_step(grid[1]),
                                                            grid[0] * grid[1])
    out = pl.pallas_call(
        body, name=name, grid=grid, in_specs=in_specs + c_in,
        out_specs=[pl.BlockSpec((tm, tn), lambda i, j: (i, j))] + c_out,
        out_shape=[jax.ShapeDtypeStruct((M, N), out_dtype)] + c_shape, scratch_shapes=c_scratch,
        compiler_params=_cp(*(("arbitrary",) * 2 if comm else ("parallel",) * 2)))(*args, *c_args)
    return (out[0], out[1:]) if comm else out[0]


def mm_nn_terms(terms, b, *, out_dtype, name):
    M = terms[0][0].shape[0]
    N = b.shape[1]
    tm = _pick(M, 256, 16)
    tn = _pick(N, 1024, LANE)
    n_terms = len(terms)

    def body(*refs):
        o_ref = refs[-1]
        acc = None
        for t in range(n_terms):
            part = _dot(refs[2 * t][...], refs[2 * t + 1][...])
            acc = part if acc is None else acc + part
        o_ref[...] = acc.astype(out_dtype)

    in_specs, args = [], []
    for arr, cb, w, off in terms:
        assert off % w == 0
        in_specs.append(pl.BlockSpec((tm, w), lambda i, j, cb=cb: (i, cb)))
        in_specs.append(pl.BlockSpec((w, tn), lambda i, j, ob=off // w: (ob, j)))
        args += [arr, b]
    return pl.pallas_call(
        body, name=name, grid=(M // tm, N // tn), in_specs=in_specs,
        out_specs=pl.BlockSpec((tm, tn), lambda i, j: (i, j)),
        out_shape=jax.ShapeDtypeStruct((M, N), out_dtype),
        compiler_params=_cp("parallel", "parallel"))(*args)


def mm_tn(a, b, *, out_dtype, name):
    T, M = a.shape
    N = b.shape[1]
    tm = _pick(M, 1024, LANE)
    tn = _pick(N, 1024, LANE)
    tk = _pick(T, 2048, 16)
    nk = T // tk

    def body(a_ref, b_ref, o_ref, acc_ref):
        k = pl.program_id(2)

        @pl.when(k == 0)
        def _():
            acc_ref[...] = jnp.zeros_like(acc_ref)

        acc_ref[...] += _dot(a_ref[...], b_ref[...], TN)

        @pl.when(k == nk - 1)
        def _():
            o_ref[...] = acc_ref[...].astype(out_dtype)

    return pl.pallas_call(
        body, name=name, grid=(M // tm, N // tn, nk),
        in_specs=[pl.BlockSpec((tk, tm), lambda i, j, k: (k, i)), pl.BlockSpec((tk, tn), lambda i, j, k: (k, j))],
        out_specs=pl.BlockSpec((tm, tn), lambda i, j, k: (i, j)),
        out_shape=jax.ShapeDtypeStruct((M, N), out_dtype),
        scratch_shapes=[pltpu.VMEM((tm, tn), F32)],
        compiler_params=_cp("parallel", "parallel", "arbitrary"))(a, b)


def rmsnorm_fwd(h, w, *, name, comm=None):
    T, D = h.shape
    tt = _pick(T, 512, 16)

    def body(h_ref, w_ref, n_ref):
        x = h_ref[...]
        r = lax.rsqrt(jnp.mean(x * x, axis=-1, keepdims=True) + EPS)
        n_ref[...] = (x * r * w_ref[...]).astype(BF16)

    body, c_in, c_args, c_out, c_shape, c_scratch = _hosted(body, 2, 1, comm, lambda: pl.program_id(0), T // tt)
    out = pl.pallas_call(
        body, name=name, grid=(T // tt,),
        in_specs=[pl.BlockSpec((tt, D), lambda i: (i, 0)), pl.BlockSpec((1, D), lambda i: (0, 0))] + c_in,
        out_specs=[pl.BlockSpec((tt, D), lambda i: (i, 0))] + c_out,
        out_shape=[jax.ShapeDtypeStruct((T, D), BF16)] + c_shape, scratch_shapes=c_scratch,
        compiler_params=_cp("arbitrary" if comm else "parallel"))(h, w, *c_args)
    return (out[0], out[1:]) if comm else out[0]


def rmsnorm_bwd(h, w, dn, dres, *, name, comm=None):
    T, D = h.shape
    tt = _pick(T, 512, 16)

    def body(h_ref, w_ref, dn_ref, dres_ref, dh_ref, dhb_ref, gw_ref):
        @pl.when(pl.program_id(0) == 0)
        def _():
            gw_ref[...] = jnp.zeros_like(gw_ref)

        x = h_ref[...]
        r = lax.rsqrt(jnp.mean(x * x, axis=-1, keepdims=True) + EPS)
        xhat = x * r
        g = dn_ref[...].astype(F32)
        gw_ref[...] += jnp.sum(g * xhat, axis=0, keepdims=True)
        dxh = g * w_ref[...]
        dx = r * (dxh - xhat * jnp.mean(dxh * xhat, axis=-1, keepdims=True))
        dh = dres_ref[...] + dx
        dh_ref[...] = dh
        dhb_ref[...] = dh.astype(BF16)

    row = pl.BlockSpec((tt, D), lambda i: (i, 0))
    vec = pl.BlockSpec((1, D), lambda i: (0, 0))
    body, c_in, c_args, c_out, c_shape, c_scratch = _hosted(body, 4, 3, comm, lambda: pl.program_id(0), T // tt)
    out = pl.pallas_call(
        body, name=name, grid=(T // tt,), in_specs=[row, vec, row, row] + c_in, out_specs=[row, row, vec] + c_out,
        out_shape=[jax.ShapeDtypeStruct((T, D), F32), jax.ShapeDtypeStruct((T, D), BF16),
                   jax.ShapeDtypeStruct((1, D), F32)] + c_shape,
        scratch_shapes=c_scratch, compiler_params=_cp("arbitrary"))(h, w, dn, dres, *c_args)
    return (out[0], out[1], out[2], out[3:]) if comm else out


def final_loss(h, w, target, *, name):
    T, D = h.shape
    tt = _pick(T, 512, 16)

    def body(h_ref, w_ref, t_ref, loss_ref, dh_ref, dhb_ref, gw_ref):
        @pl.when(pl.program_id(0) == 0)
        def _():
            gw_ref[...] = jnp.zeros_like(gw_ref)
            loss_ref[...] = jnp.zeros_like(loss_ref)

        x = h_ref[...]
        r = lax.rsqrt(jnp.mean(x * x, axis=-1, keepdims=True) + EPS)
        xhat = x * r
        e = xhat * w_ref[...] - t_ref[...]
        loss_ref[...] += jnp.sum(e * e) * (0.5 / D)
        g = e * (1.0 / D)
        gw_ref[...] += jnp.sum(g * xhat, axis=0, keepdims=True)
        dxh = g * w_ref[...]
        dh = r * (dxh - xhat * jnp.mean(dxh * xhat, axis=-1, keepdims=True))
        dh_ref[...] = dh
        dhb_ref[...] = dh.astype(BF16)

    row = pl.BlockSpec((tt, D), lambda i: (i, 0))
    vec = pl.BlockSpec((1, D), lambda i: (0, 0))
    one = pl.BlockSpec((1, LANE), lambda i: (0, 0))
    return pl.pallas_call(
        body, name=name, grid=(T // tt,), in_specs=[row, vec, row], out_specs=[one, row, row, vec],
        out_shape=[jax.ShapeDtypeStruct((1, LANE), F32), jax.ShapeDtypeStruct((T, D), F32),
                   jax.ShapeDtypeStruct((T, D), BF16), jax.ShapeDtypeStruct((1, D), F32)],
        compiler_params=_cp("arbitrary"))(h, w, target)


CONV_CHUNK = 32
ROW_CHUNK = 16
SUBLANES = 8


def _conv_tiles(seq, C, K):
    return _pick(seq, 1024 if K <= SUBLANES else 512, HALO), _pick(C, 512, LANE)


def _residues(offsets):
    return sorted({s % SUBLANES for s in offsets} - {0})


def _fill_shifted(buf, shifted, residues):
    n = buf.shape[0] - SUBLANES
    for i, r in enumerate(residues):
        shifted[i, 0:n, :] = buf[r:r + n, :]


def _tap(buf, shifted, residues, offset, start, rows):
    r = offset % SUBLANES
    base = offset - r
    ref = buf if r == 0 else shifted.at[residues.index(r)]
    return ref[pl.ds(start + base, rows), :]


def dwconv_fwd(src, offs, w, b, *, C, seq, glu, silu_out, name, comm=None):
    T = src.shape[0]
    K = w.shape[0]
    assert K - 1 <= HALO
    tt, tc = _conv_tiles(seq, C, K)
    n_in = 2 if glu else 1
    per = tt // HALO
    offsets = [HALO - (K - 1) + k for k in range(K)]
    residues = _residues(offsets)

    def body(*refs):
        cur = refs[0:2 * n_in:2]
        halo = refs[1:2 * n_in:2]
        w_ref, b_ref = refs[2 * n_in], refs[2 * n_in + 1]
        outs = refs[2 * n_in + 2:-2]
        buf, shifted = refs[-2], refs[-1]
        i = pl.program_id(1)
        first = (i * tt) % seq == 0

        def pre(rs, rows):
            v = rs[0][rows, :].astype(F32)
            return v * _sigmoid(rs[1][rows, :].astype(F32)) if glu else v

        def build(ci, carry):
            start = pl.multiple_of(ci * CONV_CHUNK, CONV_CHUNK)
            buf[pl.ds(HALO + start, CONV_CHUNK), :] = pre(cur, pl.ds(start, CONV_CHUNK))
            return carry

        buf[0:HALO, :] = jnp.where(first, 0.0, pre(halo, slice(None)))
        lax.fori_loop(0, tt // CONV_CHUNK, build, 0, unroll=2)
        _fill_shifted(buf, shifted, residues)

        def chunk(ci, carry):
            start = pl.multiple_of(ci * CONV_CHUNK, CONV_CHUNK)
            acc = jnp.broadcast_to(b_ref[...], (CONV_CHUNK, tc))
            for k in range(K):
                acc = acc + w_ref[k:k + 1, :] * _tap(buf, shifted, residues, offsets[k], start, CONV_CHUNK)
            outs[0][pl.ds(start, CONV_CHUNK), :] = acc.astype(BF16)
            if silu_out:
                outs[1][pl.ds(start, CONV_CHUNK), :] = _silu(acc).astype(BF16)
            return carry

        lax.fori_loop(0, tt // CONV_CHUNK, chunk, 0)

    in_specs, args = [], []
    for off in offs:
        assert off % tc == 0
        in_specs.append(pl.BlockSpec((tt, tc), lambda j, i, ob=off // tc: (i, ob + j)))
        in_specs.append(pl.BlockSpec((HALO, tc), lambda j, i, ob=off // tc: (jnp.maximum(i * per - 1, 0), ob + j)))
        args += [src, src]
    in_specs += [pl.BlockSpec((K, tc), lambda j, i: (0, j)), pl.BlockSpec((1, tc), lambda j, i: (0, j))]
    args += [w, b]
    n_out = 2 if silu_out else 1
    grid = (C // tc, T // tt)
    body, c_in, c_args, c_out, c_shape, c_scratch = _hosted(body, len(args), n_out, comm, _grid_step(grid[1]),
                                                            grid[0] * grid[1])
    out = pl.pallas_call(
        body, name=name, grid=grid, in_specs=in_specs + c_in,
        out_specs=[pl.BlockSpec((tt, tc), lambda j, i: (i, j))] * n_out + c_out,
        out_shape=[jax.ShapeDtypeStruct((T, C), BF16)] * n_out + c_shape,
        scratch_shapes=[pltpu.VMEM((HALO + tt, tc), F32), pltpu.VMEM((max(len(residues), 1), HALO + tt, tc), F32)]
        + c_scratch,
        compiler_params=_cp("arbitrary" if comm else "parallel", "arbitrary"))(*args, *c_args)
    return (out[:n_out], out[n_out:]) if comm else out


def dwconv_bwd(du, u, src, offs, w, *, C, seq, glu, silu_out, name):
    T = src.shape[0]
    K = w.shape[0]
    assert K - 1 <= HALO
    tt, tc = _conv_tiles(seq, C, K)
    n_in = 2 if glu else 1
    per = tt // HALO
    last_blk = T // HALO - 1
    g_offsets = [K - 1 - k for k in range(K)]
    g_res = _residues(g_offsets)

    def body(*refs):
        pos = 0
        du_cur, du_nxt = refs[0], refs[1]
        pos = 2
        if silu_out:
            u_cur, u_nxt = refs[2], refs[3]
            pos = 4
        cur = refs[pos:pos + n_in]
        pos += n_in
        w_ref = refs[pos]
        outs = refs[pos + 1:pos + 1 + n_in]
        dw_ref, db_ref = refs[pos + 1 + n_in], refs[pos + 2 + n_in]
        gbuf, gshift, dw_acc, db_acc = refs[-4:]
        i = pl.program_id(1)
        last = ((i + 1) * tt) % seq == 0

        @pl.when(i == 0)
        def _():
            dw_acc[...] = jnp.zeros_like(dw_acc)
            db_acc[...] = jnp.zeros_like(db_acc)

        def build(ci, carry):
            rows = pl.ds(pl.multiple_of(ci * CONV_CHUNK, CONV_CHUNK), CONV_CHUNK)
            g = du_cur[rows, :].astype(F32)
            if silu_out:
                g = g * _silu_grad(u_cur[rows, :].astype(F32))
            gbuf[rows, :] = g
            return carry

        lax.fori_loop(0, tt // CONV_CHUNK, build, 0, unroll=2)
        g_nxt = du_nxt[...].astype(F32)
        if silu_out:
            g_nxt = g_nxt * _silu_grad(u_nxt[...].astype(F32))
        gbuf[tt:tt + HALO, :] = jnp.where(last, 0.0, g_nxt)
        _fill_shifted(gbuf, gshift, g_res)

        def fold(v):
            out = v[0:SUBLANES]
            for s in range(SUBLANES, CONV_CHUNK, SUBLANES):
                out = out + v[s:s + SUBLANES]
            return out

        def chunk(ci, carry):
            start = pl.multiple_of(ci * CONV_CHUNK, CONV_CHUNK)
            rows = pl.ds(start, CONV_CHUNK)
            a = cur[0][rows, :].astype(F32)
            if glu:
                s = _sigmoid(cur[1][rows, :].astype(F32))
                x_in = a * s
            else:
                x_in = a
            dx = jnp.zeros((CONV_CHUNK, tc), F32)
            for k in range(K):
                g_k = _tap(gbuf, gshift, g_res, g_offsets[k], start, CONV_CHUNK)
                dx = dx + w_ref[k:k + 1, :] * g_k
                dw_acc[k * SUBLANES:(k + 1) * SUBLANES, :] += fold(g_k * x_in)
            db_acc[...] += fold(gbuf[rows, :])
            if glu:
                outs[0][rows, :] = (dx * s).astype(BF16)
                outs[1][rows, :] = (dx * a * s * (1.0 - s)).astype(BF16)
            else:
                outs[0][rows, :] = dx.astype(BF16)
            return carry

        lax.fori_loop(0, tt // CONV_CHUNK, chunk, 0)

        @pl.when(i == T // tt - 1)
        def _():
            for k in range(K):
                dw_ref[k:k + 1, :] = jnp.sum(dw_acc[k * SUBLANES:(k + 1) * SUBLANES, :], axis=0, keepdims=True)
            db_ref[...] = jnp.sum(db_acc[...], axis=0, keepdims=True)

    def cur_spec(ob):
        return pl.BlockSpec((tt, tc), lambda j, i: (i, ob + j))

    def nxt_spec(ob):
        return pl.BlockSpec((HALO, tc), lambda j, i: (jnp.minimum((i + 1) * per, last_blk), ob + j))

    in_specs = [cur_spec(0), nxt_spec(0)]
    args = [du, du]
    if silu_out:
        in_specs += [cur_spec(0), nxt_spec(0)]
        args += [u, u]
    for off in offs:
        assert off % tc == 0
        in_specs.append(cur_spec(off // tc))
        args.append(src)
    in_specs.append(pl.BlockSpec((K, tc), lambda j, i: (0, j)))
    args.append(w)
    out_specs = [pl.BlockSpec((tt, tc), lambda j, i: (i, j))] * n_in
    out_specs += [pl.BlockSpec((K, tc), lambda j, i: (0, j)), pl.BlockSpec((1, tc), lambda j, i: (0, j))]
    out_shape = [jax.ShapeDtypeStruct((T, C), BF16)] * n_in
    out_shape += [jax.ShapeDtypeStruct((K, C), F32), jax.ShapeDtypeStruct((1, C), F32)]
    return pl.pallas_call(
        body, name=name, grid=(C // tc, T // tt), in_specs=in_specs, out_specs=out_specs, out_shape=out_shape,
        scratch_shapes=[pltpu.VMEM((tt + HALO, tc), F32), pltpu.VMEM((max(len(g_res), 1), tt + HALO, tc), F32),
                        pltpu.VMEM((K * SUBLANES, tc), F32), pltpu.VMEM((SUBLANES, tc), F32)],
        compiler_params=_cp("parallel", "arbitrary"))(*args)


def mix0_post_fwd(u2, proj, o, ln_w, ln_b, *, CW, gc_off, ga_off, name):
    T = u2.shape[0]
    tt = _pick(T, 256, 16)

    def body(u_ref, gc_ref, ga_ref, o_ref, lw_ref, lb_ref, y_ref):
        def chunk(ci, carry):
            rows = pl.ds(pl.multiple_of(ci * ROW_CHUNK, ROW_CHUNK), ROW_CHUNK)
            u = u_ref[rows, :].astype(F32)
            mu = jnp.mean(u, axis=-1, keepdims=True)
            xc = u - mu
            r = lax.rsqrt(jnp.mean(xc * xc, axis=-1, keepdims=True) + EPS)
            u3 = xc * r * lw_ref[...] + lb_ref[...]
            y_ref[rows, 0:CW] = (_silu(u3) * _silu(gc_ref[rows, :].astype(F32))).astype(BF16)
            y_ref[rows, CW:2 * CW] = (o_ref[rows, :].astype(F32) * _silu(ga_ref[rows, :].astype(F32))).astype(BF16)
            return carry

        lax.fori_loop(0, tt // ROW_CHUNK, chunk, 0, unroll=4)

    row = pl.BlockSpec((tt, CW), lambda i: (i, 0))
    vec = pl.BlockSpec((1, CW), lambda i: (0, 0))
    return pl.pallas_call(
        body, name=name, grid=(T // tt,),
        in_specs=[row, pl.BlockSpec((tt, CW), lambda i: (i, gc_off // CW)),
                  pl.BlockSpec((tt, CW), lambda i: (i, ga_off // CW)), row, vec, vec],
        out_specs=pl.BlockSpec((tt, 2 * CW), lambda i: (i, 0)),
        out_shape=jax.ShapeDtypeStruct((T, 2 * CW), BF16),
        compiler_params=_cp("parallel"))(u2, proj, proj, o, ln_w, ln_b)


def mix0_post_bwd(dy, u2, proj, o, ln_w, ln_b, *, CW, gc_off, ga_off, name):
    T = u2.shape[0]
    tt = _pick(T, 256, 16)

    def body(dy_ref, u_ref, gc_ref, ga_ref, o_ref, lw_ref, lb_ref, du_ref, dgc_ref, dga_ref, do_ref, dlw_ref, dlb_ref,
             lw_acc, lb_acc):
        i = pl.program_id(0)

        @pl.when(i == 0)
        def _():
            lw_acc[...] = jnp.zeros_like(lw_acc)
            lb_acc[...] = jnp.zeros_like(lb_acc)

        def fold(v):
            out = v[0:SUBLANES]
            for s in range(SUBLANES, ROW_CHUNK, SUBLANES):
                out = out + v[s:s + SUBLANES]
            return out

        def chunk(ci, carry):
            rows = pl.ds(pl.multiple_of(ci * ROW_CHUNK, ROW_CHUNK), ROW_CHUNK)
            dyc = dy_ref[rows, 0:CW].astype(F32)
            dya = dy_ref[rows, CW:2 * CW].astype(F32)
            u = u_ref[rows, :].astype(F32)
            mu = jnp.mean(u, axis=-1, keepdims=True)
            xc = u - mu
            r = lax.rsqrt(jnp.mean(xc * xc, axis=-1, keepdims=True) + EPS)
            xhat = xc * r
            u3 = xhat * lw_ref[...] + lb_ref[...]
            gc = gc_ref[rows, :].astype(F32)
            dgc_ref[rows, :] = (dyc * _silu(u3) * _silu_grad(gc)).astype(BF16)
            du3 = dyc * _silu(gc) * _silu_grad(u3)
            lw_acc[...] += fold(du3 * xhat)
            lb_acc[...] += fold(du3)
            dxh = du3 * lw_ref[...]
            du = r * (dxh - jnp.mean(dxh, axis=-1, keepdims=True)
                      - xhat * jnp.mean(dxh * xhat, axis=-1, keepdims=True))
            du_ref[rows, :] = du.astype(BF16)
            ga = ga_ref[rows, :].astype(F32)
            ov = o_ref[rows, :].astype(F32)
            do_ref[rows, :] = (dya * _silu(ga)).astype(BF16)
            dga_ref[rows, :] = (dya * ov * _silu_grad(ga)).astype(BF16)
            return carry

        lax.fori_loop(0, tt // ROW_CHUNK, chunk, 0, unroll=4)

        @pl.when(i == T // tt - 1)
        def _():
            dlw_ref[...] = jnp.sum(lw_acc[...], axis=0, keepdims=True)
            dlb_ref[...] = jnp.sum(lb_acc[...], axis=0, keepdims=True)

    row = pl.BlockSpec((tt, CW), lambda i: (i, 0))
    vec = pl.BlockSpec((1, CW), lambda i: (0, 0))
    big = jax.ShapeDtypeStruct((T, CW), BF16)
    small = jax.ShapeDtypeStruct((1, CW), F32)
    return pl.pallas_call(
        body, name=name, grid=(T // tt,),
        in_specs=[pl.BlockSpec((tt, 2 * CW), lambda i: (i, 0)), row,
                  pl.BlockSpec((tt, CW), lambda i: (i, gc_off // CW)),
                  pl.BlockSpec((tt, CW), lambda i: (i, ga_off // CW)), row, vec, vec],
        out_specs=[row, row, row, row, vec, vec],
        out_shape=[big, big, big, big, small, small],
        scratch_shapes=[pltpu.VMEM((SUBLANES, CW), F32), pltpu.VMEM((SUBLANES, CW), F32)],
        compiler_params=_cp("arbitrary"))(dy, u2, proj, proj, o, ln_w, ln_b)


SB_UNDERFLOW = 110.0
SB_BOUND_MARGIN = 1.02


def _sb_tile(seq):
    return _pick(seq, 256, LANE)


def _softplus(z):
    return jnp.maximum(z, 0.0) + jnp.log(1.0 + jnp.exp(-jnp.abs(z)))


def _tri01(n, lower):
    i = lax.broadcasted_iota(jnp.int32, (n, n), 0)
    j = lax.broadcasted_iota(jnp.int32, (n, n), 1)
    return ((i >= j) if lower else (i <= j)).astype(BF16)


SB_HEADS_FWD = 4
SB_HEADS_BWD = 2


def _sb_heads_per_step(heads, want):
    while heads % want:
        want //= 2
    return want


def sba_fwd(proj, *, B, seq, heads, q_off, k_off, v_off, name):
    dh = SB_HEAD_DIM
    tq = _sb_tile(seq)
    assert tq % (2 * LANE) == 0
    nq = seq // tq
    hps = _sb_heads_per_step(heads, SB_HEADS_FWD)
    hw = hps * dh
    scale = dh ** -0.5

    def body(q_ref, k_ref, v_ref, tri_ref, o_ref, ct_ref, acc_ref, kmax_ref):
        qi = pl.program_id(1)
        tri = tri_ref[...]
        qs = [(q_ref[:, h * dh:(h + 1) * dh].astype(F32) * scale).astype(BF16) for h in range(hps)]

        @pl.when(qi == 0)
        def _():
            def chunk(i, best):
                rows = k_ref[pl.ds(pl.multiple_of(i * tq, tq), tq), :].astype(F32)
                sq = rows * rows
                return tuple(jnp.maximum(best[h], jnp.max(jnp.sum(sq[:, h * dh:(h + 1) * dh], axis=1, keepdims=True),
                                                          axis=0, keepdims=True)) for h in range(hps))

            best = lax.fori_loop(0, nq, chunk, (jnp.zeros((1, 1), F32),) * hps)
            for h in range(hps):
                kmax_ref[h] = jnp.broadcast_to(jnp.sqrt(best[h]), (8, LANE))

        z_bound = [jnp.sqrt(jnp.sum(qs[h].astype(F32) ** 2, axis=1, keepdims=True))
                   * (SB_BOUND_MARGIN * jnp.max(kmax_ref[h], keepdims=True)) for h in range(hps)]

        def part(h, q_rows, start, n_keys, r, mask):
            k_blk = k_ref[pl.ds(start, n_keys), h * dh:(h + 1) * dh]
            v_blk = v_ref[pl.ds(start, n_keys), h * dh:(h + 1) * dh]
            z = _dot(q_rows, k_blk, NT)
            sp = _softplus(z)
            if mask is not None:
                sp = jnp.where(mask, sp, 0.0)
            wts = jnp.exp(z - (_dot(sp.astype(BF16), tri[0:n_keys, 0:n_keys]) + r))
            if mask is not None:
                wts = jnp.where(mask, wts, 0.0)
            return _dot(wts.astype(BF16), v_blk), r + jnp.sum(sp, axis=-1, keepdims=True)

        below = lax.broadcasted_iota(jnp.int32, (tq, tq), 1) < lax.broadcasted_iota(jnp.int32, (tq, tq), 0)
        has_left = qi > 0
        left = pl.multiple_of(jnp.maximum(qi - 1, 0) * tq, tq)
        rs = []
        for h in range(hps):
            pv_d, r = part(h, qs[h], pl.multiple_of(qi * tq, tq), tq, jnp.zeros((tq, 1), F32), below)
            pv_l, r = part(h, qs[h], left, tq, r, has_left)
            acc_ref[:, h * dh:(h + 1) * dh] = pv_d + pv_l
            rs.append(r)
        rs = tuple(rs)

        def block(start, rs):
            pvs, out = [], []
            for h in range(hps):
                pv, r = part(h, qs[h], start, tq, rs[h], None)
                pvs.append(pv)
                out.append(r)
            return pvs, tuple(out)

        def more(c):
            j, rs = c
            slack = rs[0] - z_bound[0]
            for h in range(1, hps):
                slack = jnp.minimum(slack, rs[h] - z_bound[h])
            return jnp.logical_and(j < qi, jnp.min(slack) <= SB_UNDERFLOW)

        def step(c):
            j, rs = c
            pvs, rs = block(pl.multiple_of((qi - 1 - j) * tq, tq), rs)
            for h in range(hps):
                acc_ref[:, h * dh:(h + 1) * dh] += pvs[h]
            return j + 1, rs

        n_left, totals = lax.while_loop(more, step, (has_left.astype(jnp.int32), rs))
        o_ref[...] = acc_ref[...].astype(BF16)
        for h in range(hps):
            ct_ref[0, 0, h, 0:8, :] = jnp.broadcast_to(totals[h], (tq, LANE)).T[0:8, :]
            ct_ref[0, 0, h, 8:16, :] = jnp.full((8, tq), n_left, F32)

    qb, kb, vb = q_off // hw, k_off // hw, v_off // hw
    G = heads // hps
    return pl.pallas_call(
        body, name=name, grid=(B * G, nq),
        in_specs=[pl.BlockSpec((tq, hw), lambda g, i: ((g // G) * nq + i, qb + g % G)),
                  pl.BlockSpec((seq, hw), lambda g, i: (g // G, kb + g % G)),
                  pl.BlockSpec((seq, hw), lambda g, i: (g // G, vb + g % G)),
                  pl.BlockSpec((tq, tq), lambda g, i: (0, 0))],
        out_specs=[pl.BlockSpec((tq, hw), lambda g, i: ((g // G) * nq + i, g % G)),
                   pl.BlockSpec((1, 1, hps, 16, tq), lambda g, i: (g // G, i, g % G, 0, 0))],
        out_shape=[jax.ShapeDtypeStruct((B * seq, heads * dh), BF16),
                   jax.ShapeDtypeStruct((B, nq, heads, 16, tq), F32)],
        scratch_shapes=[pltpu.VMEM((tq, hw), F32), pltpu.VMEM((hps, 8, LANE), F32)],
        compiler_params=_cp("parallel", "arbitrary"))(proj, proj, proj, jnp.tril(jnp.ones((tq, tq), BF16)))


def sba_bwd(proj, ctot, do, *, B, seq, heads, q_off, k_off, v_off, name, comm=None):
    dh = SB_HEAD_DIM
    tq = _sb_tile(seq)
    nq = seq // tq
    hps = _sb_heads_per_step(heads, SB_HEADS_BWD)
    hw = hps * dh
    scale = dh ** -0.5

    def body(q_ref, k_ref, v_ref, ct_ref, do_ref, sfx_ref, pre_ref, dq_ref, dk_ref, dv_ref, dq_acc, dk_acc, dv_acc):
        qi = pl.program_id(1)

        @pl.when(qi == 0)
        def _():
            dk_acc[...] = jnp.zeros_like(dk_acc)
            dv_acc[...] = jnp.zeros_like(dv_acc)

        tri_sfx = sfx_ref[...]
        tri_pre = pre_ref[...]
        qs = [(q_ref[:, h * dh:(h + 1) * dh].astype(F32) * scale).astype(BF16) for h in range(hps)]
        dos = [do_ref[:, h * dh:(h + 1) * dh] for h in range(hps)]
        totals = [jnp.max(jnp.broadcast_to(ct_ref[0, 0, h, 0:1, :], (LANE, tq)).T, axis=1, keepdims=True)
                  for h in range(hps)]
        dq_acc[...] = jnp.zeros_like(dq_acc)

        def part(h, rows, start, n_keys, pc, pg, mask):
            cols = slice(h * dh, (h + 1) * dh)
            q_rows, do_rows = qs[h][rows], dos[h][rows]
            k_blk = k_ref[pl.ds(start, n_keys), cols]
            v_blk = v_ref[pl.ds(start, n_keys), cols]
            z = _dot(q_rows, k_blk, NT)
            sp = _softplus(z)
            sig = jnp.exp(z - sp)
            if mask is not None:
                sp = jnp.where(mask, sp, 0.0)
            pc_next = pc + jnp.sum(sp, axis=-1, keepdims=True)
            wts = jnp.exp(z - (_dot(sp.astype(BF16), tri_sfx[0:n_keys, 0:n_keys]) + (totals[h][rows] - pc_next)))
            if mask is not None:
                wts = jnp.where(mask, wts, 0.0)
            g = _dot(do_rows, v_blk, NT) * wts
            dz = g - sig * (_dot(g.astype(BF16), tri_pre[0:n_keys, 0:n_keys]) + pg)
            if mask is not None:
                dz = jnp.where(mask, dz, 0.0)
            dz = dz.astype(BF16)
            dq_acc[rows, cols] += _dot(dz, k_blk)
            dk_acc[pl.ds(start, n_keys), cols] += _dot(dz, q_rows, TN)
            dv_acc[pl.ds(start, n_keys), cols] += _dot(wts.astype(BF16), do_rows, TN)
            return pc_next, pg + jnp.sum(g, axis=-1, keepdims=True)

        def block(start, carry):
            return tuple(part(h, slice(0, tq), start, tq, carry[h][0], carry[h][1], None) for h in range(hps))

        zero = jnp.zeros((tq, 1), F32)
        n_left = jnp.max(ct_ref[0, 0, 0, 8:16, :]).astype(jnp.int32)
        carry = lax.fori_loop(qi - n_left, qi - 1, lambda j, c: block(pl.multiple_of(j * tq, tq), c),
                              ((zero, zero),) * hps)
        below = lax.broadcasted_iota(jnp.int32, (tq, tq), 1) < lax.broadcasted_iota(jnp.int32, (tq, tq), 0)
        has_left = n_left > 0
        left = pl.multiple_of(jnp.maximum(qi - 1, 0) * tq, tq)
        for h in range(hps):
            pc, pg = part(h, slice(0, tq), left, tq, carry[h][0], carry[h][1], has_left)
            part(h, slice(0, tq), pl.multiple_of(qi * tq, tq), tq, pc, pg, below)
        dq_ref[...] = (dq_acc[...] * scale).astype(BF16)

        @pl.when(qi == nq - 1)
        def _():
            dk_ref[...] = dk_acc[...].astype(BF16)
            dv_ref[...] = dv_acc[...].astype(BF16)

    qb, kb, vb = q_off // hw, k_off // hw, v_off // hw
    G = heads // hps
    q_spec = pl.BlockSpec((tq, hw), lambda g, i: ((g // G) * nq + i, qb + g % G))
    o_spec = pl.BlockSpec((tq, hw), lambda g, i: ((g // G) * nq + i, g % G))
    kv_out = pl.BlockSpec((seq, hw), lambda g, i: (g // G, g % G))
    shp = jax.ShapeDtypeStruct((B * seq, heads * dh), BF16)
    body, c_in, c_args, c_out, c_shape, c_scratch = _hosted(body, 7, 3, comm, _grid_step(nq), B * G * nq)
    tri_spec = pl.BlockSpec((tq, tq), lambda g, i: (0, 0))
    ones = jnp.ones((tq, tq), BF16)
    out = pl.pallas_call(
        body, name=name, grid=(B * G, nq),
        in_specs=[q_spec,
                  pl.BlockSpec((seq, hw), lambda g, i: (g // G, kb + g % G)),
                  pl.BlockSpec((seq, hw), lambda g, i: (g // G, vb + g % G)),
                  pl.BlockSpec((1, 1, hps, 16, tq), lambda g, i: (g // G, i, g % G, 0, 0)), o_spec,
                  tri_spec, tri_spec] + c_in,
        out_specs=[o_spec, kv_out, kv_out] + c_out, out_shape=[shp, shp, shp] + c_shape,
        scratch_shapes=[pltpu.VMEM((tq, hw), F32), pltpu.VMEM((seq, hw), F32), pltpu.VMEM((seq, hw), F32)]
        + c_scratch,
        compiler_params=_cp("arbitrary" if comm else "parallel", "arbitrary"))(
            proj, proj, proj, ctot, do, jnp.tril(ones), jnp.triu(ones), *c_args)
    return (out[0], out[1], out[2], out[3:]) if comm else out


def _head_expand(n_heads, DI):
    j = jnp.arange(LANE, dtype=jnp.int32)[:, None]
    c = jnp.arange(DI, dtype=jnp.int32)[None, :] // SSM_HEAD_DIM
    return ((j == c) & (j < n_heads)).astype(BF16)


def _split3(x):
    hi = x.astype(BF16)
    r1 = x - hi.astype(F32)
    mid = r1.astype(BF16)
    return hi, mid, (r1 - mid.astype(F32)).astype(BF16)


def dt_fwd(proj, bias, a_log, expand, *, dt_off, name):
    T = proj.shape[0]
    DI = expand.shape[1]
    L = SSM_CHUNK
    tt = _pick(T, 512, L)

    def body(raw_ref, bias_ref, al_ref, e_ref, dt_ref, cs_ref, dtx_ref, csx_ref):
        x = raw_ref[...].astype(F32) + bias_ref[...]
        dt = _softplus(x)
        dt_ref[...] = dt
        la = dt * (-jnp.exp(al_ref[...]))
        tri = _tri01(L, True)
        for c in range(tt // L):
            cs_ref[c * L:(c + 1) * L, :] = _tri_dot3(tri, la[c * L:(c + 1) * L, :])
        e = e_ref[...]
        dtx_ref[...] = _dot(dt.astype(BF16), e).astype(BF16)
        hi, mid, lo = _split3(cs_ref[...])
        csx_ref[...] = _dot(hi, e) + _dot(mid, e) + _dot(lo, e)

    row = pl.BlockSpec((tt, LANE), lambda i: (i, 0))
    wide = pl.BlockSpec((tt, DI), lambda i: (i, 0))
    vec = pl.BlockSpec((1, LANE), lambda i: (0, 0))
    return pl.pallas_call(
        body, name=name, grid=(T // tt,),
        in_specs=[pl.BlockSpec((tt, LANE), lambda i: (i, dt_off // LANE)), vec, vec,
                  pl.BlockSpec((LANE, DI), lambda i: (0, 0))],
        out_specs=[row, row, wide, wide],
        out_shape=[jax.ShapeDtypeStruct((T, LANE), F32), jax.ShapeDtypeStruct((T, LANE), F32),
                   jax.ShapeDtypeStruct((T, DI), BF16), jax.ShapeDtypeStruct((T, DI), F32)],
        compiler_params=_cp("parallel"))(proj, bias, a_log, expand)


def dt_bwd(ddt_x, dcs_x, dcs_cols, proj, dt, bias, a_log, reduce_t, *, dt_off, n_heads, name):
    T = proj.shape[0]
    DI = reduce_t.shape[0]
    L = SSM_CHUNK
    tt = _pick(T, 512, L)

    def body(ddtx_ref, dcsx_ref, dcsc_ref, raw_ref, dt_ref, bias_ref, al_ref, r_ref, draw_ref, dbias_ref, dal_ref,
             dla_buf):
        @pl.when(pl.program_id(0) == 0)
        def _():
            dbias_ref[...] = jnp.zeros_like(dbias_ref)
            dal_ref[...] = jnp.zeros_like(dal_ref)

        r = r_ref[...]
        ddt = _dot(ddtx_ref[...], r)
        dx = dcsx_ref[...]
        hi = dx.astype(BF16)
        dcs = _dot(hi, r) + _dot((dx - hi.astype(F32)).astype(BF16), r) + dcsc_ref[...]
        triu = _tri01(L, False)
        for c in range(tt // L):
            dla_buf[c * L:(c + 1) * L, :] = _tri_dot3(triu, dcs[c * L:(c + 1) * L, :])
        dla = dla_buf[...]
        a = -jnp.exp(al_ref[...])
        valid = lax.broadcasted_iota(jnp.int32, (tt, LANE), 1) < n_heads
        dal_ref[...] += jnp.sum(jnp.where(valid, dla * dt_ref[...], 0.0), axis=0, keepdims=True) * a
        x = raw_ref[...].astype(F32) + bias_ref[...]
        draw = jnp.where(valid, (ddt + dla * a) * _sigmoid(x), 0.0)
        dbias_ref[...] += jnp.sum(draw, axis=0, keepdims=True)
        draw_ref[...] = draw.astype(BF16)

    row = pl.BlockSpec((tt, LANE), lambda i: (i, 0))
    wide = pl.BlockSpec((tt, DI), lambda i: (i, 0))
    vec = pl.BlockSpec((1, LANE), lambda i: (0, 0))
    return pl.pallas_call(
        body, name=name, grid=(T // tt,),
        in_specs=[wide, wide, row, pl.BlockSpec((tt, LANE), lambda i: (i, dt_off // LANE)), row, vec, vec,
                  pl.BlockSpec((DI, LANE), lambda i: (0, 0))],
        out_specs=[row, vec, vec],
        out_shape=[jax.ShapeDtypeStruct((T, LANE), BF16), jax.ShapeDtypeStruct((1, LANE), F32),
                   jax.ShapeDtypeStruct((1, LANE), F32)],
        scratch_shapes=[pltpu.VMEM((tt, LANE), F32)],
        compiler_params=_cp("arbitrary"))(ddt_x, dcs_x, dcs_cols, proj, dt, bias, a_log, reduce_t)


def _pair_terms(x_ref, dtx_ref, csx_ref, csr_ref, pair, ppg, lo_half, causal):
    L = SSM_CHUNK
    g, pp = divmod(pair, ppg)
    ra = g * HEAD_ROWS + 2 * pp
    cols = slice(pair * LANE, (pair + 1) * LANE)
    X = x_ref[:, cols].astype(F32)
    dt_p = dtx_ref[:, cols].astype(F32)
    own = csx_ref[:, cols]
    other = pltpu.roll(own, SSM_HEAD_DIM, 1)
    csa_c = jnp.where(lo_half, own, other)
    csb_c = jnp.where(lo_half, other, own)
    La = jnp.exp(jnp.where(causal, csa_c - csr_ref[0, ra:ra + 1, :], NEG_BIG))
    Lb = jnp.exp(jnp.where(causal, csb_c - csr_ref[0, ra + 1:ra + 2, :], NEG_BIG))
    last = csx_ref[L - 1:L, cols]
    return g, ra, cols, X, dt_p, La, Lb, jnp.exp(own), jnp.exp(last - own), jnp.exp(last)


def scan_fwd(xbc, dt_x, cs_x, cs_row, d_full, *, B, seq, DI, name):
    L, N, G = SSM_CHUNK, SSM_STATE, SSM_GROUPS
    nc = seq // L
    XW = xbc.shape[1]
    n_pairs = DI // LANE
    ppg = n_pairs // G

    def body(x_ref, dtx_ref, csx_ref, csr_ref, d_ref, y_ref, st_ref, state):
        c = pl.program_id(1)

        @pl.when(c == 0)
        def _():
            state[...] = jnp.zeros_like(state)

        causal = lax.broadcasted_iota(jnp.int32, (L, L), 0) >= lax.broadcasted_iota(jnp.int32, (L, L), 1)
        lo_half = lax.broadcasted_iota(jnp.int32, (L, LANE), 1) < SSM_HEAD_DIM
        cbs = []
        for g in range(G):
            Bc = x_ref[:, DI + g * N:DI + (g + 1) * N]
            Cc = x_ref[:, DI + G * N + g * N:DI + G * N + (g + 1) * N]
            cbs.append((Bc, Cc, _dot(Cc, Bc, NT)))
        for pair in range(n_pairs):
            g, _, cols, X, dt_p, La, Lb, ecs, tail, e_last = _pair_terms(
                x_ref, dtx_ref, csx_ref, csr_ref, pair, ppg, lo_half, causal)
            Bc, Cc, CB = cbs[g]
            xs = X * dt_p
            xsb = xs.astype(BF16)
            y = jnp.where(lo_half, _dot((CB * La).astype(BF16), xsb), _dot((CB * Lb).astype(BF16), xsb))
            ST = state[pair]
            st_ref[0, 0, pair] = ST
            y = y + ecs * _dot(Cc, ST.astype(BF16)) + d_ref[:, cols] * X
            y_ref[:, cols] = y.astype(BF16)
            state[pair] = e_last * ST + _dot(Bc, (xs * tail).astype(BF16), TN)

    wide = pl.BlockSpec((L, DI), lambda b, c: (b * nc + c, 0))
    return pl.pallas_call(
        body, name=name, grid=(B, nc),
        in_specs=[pl.BlockSpec((L, XW), lambda b, c: (b * nc + c, 0)), wide, wide,
                  pl.BlockSpec((1, G * HEAD_ROWS, L), lambda b, c: (b, 0, c)),
                  pl.BlockSpec((1, DI), lambda b, c: (0, 0))],
        out_specs=[wide, pl.BlockSpec((1, 1, n_pairs, N, LANE), lambda b, c: (b, c, 0, 0, 0))],
        out_shape=[jax.ShapeDtypeStruct((B * seq, DI), BF16),
                   jax.ShapeDtypeStruct((B, nc, n_pairs, N, LANE), F32)],
        scratch_shapes=[pltpu.VMEM((n_pairs, N, LANE), F32)],
        compiler_params=_cp("parallel", "arbitrary"))(xbc, dt_x, cs_x, cs_row, d_full)


def scan_bwd(xbc, dt_x, cs_x, cs_row, d_full, states, dy, *, B, seq, DI, name):
    L, N, G = SSM_CHUNK, SSM_STATE, SSM_GROUPS
    nc = seq // L
    XW = xbc.shape[1]
    n_pairs = DI // LANE
    ppg = n_pairs // G
    HR = G * HEAD_ROWS
    inv_p = 1.0 / SSM_HEAD_DIM

    def body(x_ref, dtx_ref, csx_ref, csr_ref, d_ref, st_ref, dy_ref, dx_ref, ddtx_ref, dcsx_ref, dcsr_ref, dd_ref,
             dH):
        c = pl.program_id(1)

        @pl.when(c == 0)
        def _():
            dH[...] = jnp.zeros_like(dH)
            dd_ref[...] = jnp.zeros_like(dd_ref)

        causal = lax.broadcasted_iota(jnp.int32, (L, L), 0) >= lax.broadcasted_iota(jnp.int32, (L, L), 1)
        lo_half = lax.broadcasted_iota(jnp.int32, (L, LANE), 1) < SSM_HEAD_DIM
        last_row = lax.broadcasted_iota(jnp.int32, (L, LANE), 0) == L - 1
        head_row = lax.broadcasted_iota(jnp.int32, (HR, 1), 0)
        dcs_rows = jnp.zeros((HR, L), F32)

        for g in range(G):
            Bc = x_ref[:, DI + g * N:DI + (g + 1) * N]
            Cc = x_ref[:, DI + G * N + g * N:DI + G * N + (g + 1) * N]
            CB = _dot(Cc, Bc, NT)
            dCB = jnp.zeros((L, L), F32)
            dC = jnp.zeros((L, N), F32)
            dB = jnp.zeros((L, N), F32)
            for pp in range(ppg):
                pair = g * ppg + pp
                _, ra, cols, X, dt_p, La, Lb, ecs, tail, e_last = _pair_terms(
                    x_ref, dtx_ref, csx_ref, csr_ref, pair, ppg, lo_half, causal)
                xs = X * dt_p
                xsb = xs.astype(BF16)
                Ma, Mb = CB * La, CB * Lb
                dY = dy_ref[:, cols].astype(F32)
                dYb = dY.astype(BF16)
                dMa = _dot(jnp.where(lo_half, dY, 0.0).astype(BF16), xsb, NT)
                dMb = _dot(jnp.where(lo_half, 0.0, dY).astype(BF16), xsb, NT)
                dSa, dSb = dMa * Ma, dMb * Mb
                dCB = dCB + dMa * La + dMb * Lb
                dcs = jnp.where(lo_half, jnp.sum(dSa, axis=1, keepdims=True), jnp.sum(dSb, axis=1, keepdims=True)) * inv_p
                dcs_rows = dcs_rows - jnp.where(head_row == ra, jnp.sum(dSa, axis=0, keepdims=True), 0.0)
                dcs_rows = dcs_rows - jnp.where(head_row == ra + 1, jnp.sum(dSb, axis=0, keepdims=True), 0.0)
                dxs = jnp.where(lo_half, _dot(Ma.astype(BF16), dYb, TN), _dot(Mb.astype(BF16), dYb, TN))
                ST = st_ref[0, 0, pair]
                STb = ST.astype(BF16)
                dYe = (dY * ecs).astype(BF16)
                dC = dC + _dot(dYe, STb, NT)
                dSTp = _dot(Cc, dYe, TN)
                dcs = dcs + dY * (ecs * _dot(Cc, STb))
                dSTn = dH[pair]
                dSTnb = dSTn.astype(BF16)
                dSTp = dSTp + e_last * dSTn
                XBt = _dot(Bc, dSTnb)
                dxs = dxs + tail * XBt
                t2 = xs * XBt * tail
                at_end = e_last * jnp.sum(dSTn * ST, axis=0, keepdims=True) + jnp.sum(t2, axis=0, keepdims=True)
                dcs = dcs - t2 + jnp.where(last_row, at_end, 0.0)
                dB = dB + _dot((xs * tail).astype(BF16), dSTnb, NT)
                dx_ref[:, cols] = (dxs * dt_p + d_ref[:, cols] * dY).astype(BF16)
                ddtx_ref[:, cols] = (dxs * X).astype(BF16)
                dcsx_ref[:, cols] = dcs
                dd_ref[0, :, cols] += jnp.sum(dY * X, axis=0, keepdims=True)
                dH[pair] = dSTp
            dCBb = dCB.astype(BF16)
            dx_ref[:, DI + g * N:DI + (g + 1) * N] = (dB + _dot(dCBb, Cc, TN)).astype(BF16)
            dx_ref[:, DI + G * N + g * N:DI + G * N + (g + 1) * N] = (dC + _dot(dCBb, Bc)).astype(BF16)
        dcsr_ref[0] = dcs_rows

    rev = lambda b, c: (b * nc + (nc - 1 - c), 0)
    wide = pl.BlockSpec((L, DI), rev)
    hrow = pl.BlockSpec((1, HR, L), lambda b, c: (b, 0, nc - 1 - c))
    return pl.pallas_call(
        body, name=name, grid=(B, nc),
        in_specs=[pl.BlockSpec((L, XW), rev), wide, wide, hrow,
                  pl.BlockSpec((1, DI), lambda b, c: (0, 0)),
                  pl.BlockSpec((1, 1, n_pairs, N, LANE), lambda b, c: (b, nc - 1 - c, 0, 0, 0)), wide],
        out_specs=[pl.BlockSpec((L, XW), rev), wide, wide, hrow, pl.BlockSpec((1, 1, DI), lambda b, c: (b, 0, 0))],
        out_shape=[jax.ShapeDtypeStruct((B * seq, XW), BF16), jax.ShapeDtypeStruct((B * seq, DI), BF16),
                   jax.ShapeDtypeStruct((B * seq, DI), F32), jax.ShapeDtypeStruct((B, HR, seq), F32),
                   jax.ShapeDtypeStruct((B, 1, DI), F32)],
        scratch_shapes=[pltpu.VMEM((n_pairs, N, LANE), F32)],
        compiler_params=_cp("parallel", "arbitrary"))(xbc, dt_x, cs_x, cs_row, d_full, states, dy)


def gnorm_fwd(y, proj, w, *, DI, name):
    T = y.shape[0]
    tt = _pick(T, 256, 16)
    gw = DI // SSM_GROUPS

    def body(y_ref, z_ref, w_ref, o_ref):
        for g in range(SSM_GROUPS):
            sl = slice(g * gw, (g + 1) * gw)
            y2 = y_ref[:, sl].astype(F32) * _silu(z_ref[:, sl].astype(F32))
            r = lax.rsqrt(jnp.mean(y2 * y2, axis=-1, keepdims=True) + EPS)
            o_ref[:, sl] = (y2 * r * w_ref[:, sl]).astype(BF16)

    row = pl.BlockSpec((tt, DI), lambda i: (i, 0))
    return pl.pallas_call(
        body, name=name, grid=(T // tt,),
        in_specs=[row, row, pl.BlockSpec((1, DI), lambda i: (0, 0))], out_specs=row,
        out_shape=jax.ShapeDtypeStruct((T, DI), BF16), compiler_params=_cp("parallel"))(y, proj, w)


def gnorm_bwd(dyn, y, proj, w, *, DI, name):
    T = y.shape[0]
    tt = _pick(T, 256, 16)
    gw = DI // SSM_GROUPS

    def body(dyn_ref, y_ref, z_ref, w_ref, dy_ref, dz_ref, dw_ref):
        @pl.when(pl.program_id(0) == 0)
        def _():
            dw_ref[...] = jnp.zeros_like(dw_ref)

        for g in range(SSM_GROUPS):
            sl = slice(g * gw, (g + 1) * gw)
            yv = y_ref[:, sl].astype(F32)
            z = z_ref[:, sl].astype(F32)
            sz = _silu(z)
            y2 = yv * sz
            r = lax.rsqrt(jnp.mean(y2 * y2, axis=-1, keepdims=True) + EPS)
            xhat = y2 * r
            d = dyn_ref[:, sl].astype(F32)
            dw_ref[:, sl] += jnp.sum(d * xhat, axis=0, keepdims=True)
            dxh = d * w_ref[:, sl]
            dy2 = r * (dxh - xhat * jnp.mean(dxh * xhat, axis=-1, keepdims=True))
            dy_ref[:, sl] = (dy2 * sz).astype(BF16)
            dz_ref[:, sl] = (dy2 * yv * _silu_grad(z)).astype(BF16)

    row = pl.BlockSpec((tt, DI), lambda i: (i, 0))
    vec = pl.BlockSpec((1, DI), lambda i: (0, 0))
    shp = jax.ShapeDtypeStruct((T, DI), BF16)
    return pl.pallas_call(
        body, name=name, grid=(T // tt,), in_specs=[row, row, row, vec], out_specs=[row, row, vec],
        out_shape=[shp, shp, jax.ShapeDtypeStruct((1, DI), F32)],
        compiler_params=_cp("arbitrary"))(dyn, y, proj, w)


N_CHIP = 4
SHARD_ROW_ALIGN = 128


def _comm_out_shapes(srcs, modes):
    return [jax.ShapeDtypeStruct(((N_DEV,) if mode in ("gather", "gather_direct") else ()) + s.shape, s.dtype)
            for s, mode in zip(srcs, modes)]


def _comm_scratch(n):
    return [pltpu.SemaphoreType.DMA((n, N_DEV - 1)), pltpu.SemaphoreType.DMA((n, N_DEV - 1)),
            pltpu.SemaphoreType.DMA((n,))]


def _comm_phases(modes, src_refs, out_refs, send_sems, recv_sems, local_sems):
    x, y, c = lax.axis_index("x"), lax.axis_index("y"), lax.axis_index("c")
    me, sibling = (x, y, c), (x, y, 1 - c)
    chips = [(1 - x, y), (x, 1 - y), (1 - x, 1 - y)]
    relays = [a for a, mode in enumerate(modes) if mode == "gather"]

    def slot(p):
        return 4 * p[0] + 2 * p[1] + p[2]

    def remote(a, k, src, dst, to):
        return pltpu.make_async_remote_copy(src_ref=src, dst_ref=dst, send_sem=send_sems.at[a, k],
                                            recv_sem=recv_sems.at[a, k], device_id=to,
                                            device_id_type=pl.DeviceIdType.MESH)

    def first_copies():
        local, two_way, send_only = [], [], []
        for a, mode in enumerate(modes):
            src, out = src_refs[a], out_refs[a]
            if mode == "sibling":
                two_way.append(remote(a, 0, src, out, sibling))
            elif mode == "chips":
                mine = 2 * x + y
                local.append(pltpu.make_async_copy(src.at[mine], out.at[mine], local_sems.at[a]))
                for j, chip in enumerate(chips):
                    two_way.append(remote(a, 1 + j, src.at[2 * chip[0] + chip[1]], out.at[mine], (*chip, c)))
            elif mode == "gather_direct":
                local.append(pltpu.make_async_copy(src, out.at[slot(me)], local_sems.at[a]))
                for k in range(1, N_DEV):
                    peer = (1 - x if k & 4 else x, 1 - y if k & 2 else y, 1 - c if k & 1 else c)
                    two_way.append(remote(a, k - 1, src, out.at[slot(me)], peer))
            else:
                assert mode == "gather"
                local.append(pltpu.make_async_copy(src, out.at[slot(me)], local_sems.at[a]))
                send_only.append(remote(a, 0, src, out.at[slot(me)], sibling))
                for j, chip in enumerate(chips):
                    send_only.append(remote(a, 1 + j, src, out.at[slot(me)], (*chip, c)))
        return local, two_way, send_only

    def forwards():
        out = []
        for a in relays:
            for j, chip in enumerate(chips):
                landed = out_refs[a].at[slot((*chip, c))]
                out.append((remote(a, 1 + j, landed, landed, me), remote(a, 4 + j, landed, landed, sibling)))
        return out

    def start():
        local, two_way, send_only = first_copies()
        for cp in local + two_way + send_only:
            cp.start()

    def relay():
        for arrival, fwd in forwards():
            arrival.wait_recv()
            fwd.start()

    def finish():
        local, two_way, send_only = first_copies()
        for a in relays:
            blk = out_refs[a].at[slot(sibling)]
            remote(a, 0, blk, blk, me).wait_recv()
            for j, chip in enumerate(chips):
                blk = out_refs[a].at[slot((*chip, 1 - c))]
                remote(a, 4 + j, blk, blk, me).wait_recv()
        for cp in send_only + [fwd for _, fwd in forwards()]:
            cp.wait_send()
        for cp in two_way + local:
            cp.wait()

    return start, relay, finish, bool(relays)


def _hosted(body, n_in, n_out, comm, step, n_steps):
    if comm is None:
        return body, [], [], [], [], []
    srcs, modes = comm
    nc = len(srcs)

    def wrapped(*refs):
        ins, csrc = refs[:n_in], refs[n_in:n_in + nc]
        outs = refs[n_in + nc:n_in + nc + n_out]
        cout = refs[n_in + nc + n_out:n_in + 2 * nc + n_out]
        scratch = refs[n_in + 2 * nc + n_out:len(refs) - 3]
        start, relay, finish, has_relay = _comm_phases(modes, csrc, cout, *refs[len(refs) - 3:])
        s = step()
        pl.when(s == 0)(start)
        body(*ins, *outs, *scratch)
        if has_relay:
            pl.when(s == (2 * n_steps) // 3)(relay)
        pl.when(s == n_steps - 1)(finish)

    any_spec = pl.BlockSpec(memory_space=pl.ANY)
    return wrapped, [any_spec] * nc, list(srcs), [any_spec] * nc, _comm_out_shapes(srcs, modes), _comm_scratch(nc)


def exchange(srcs, modes, *, name):
    n = len(srcs)

    def body(*refs):
        start, relay, finish, has_relay = _comm_phases(modes, refs[:n], refs[n:2 * n], *refs[2 * n:])
        start()
        if has_relay:
            relay()
        finish()

    any_spec = pl.BlockSpec(memory_space=pl.ANY)
    return pl.pallas_call(
        body, name=name, in_specs=[any_spec] * n, out_specs=[any_spec] * n, out_shape=_comm_out_shapes(srcs, modes),
        scratch_shapes=_comm_scratch(n), compiler_params=pltpu.CompilerParams(has_side_effects=True))(*srcs)


def pair_sum(a, b, *, name):
    n, R, C = a.shape
    tr = _pick(n * R, 1024, 16)

    def body(a_ref, b_ref, o_ref):
        o_ref[...] = (a_ref[...].astype(F32) + b_ref[...].astype(F32)).astype(BF16)

    blk = pl.BlockSpec((tr, C), lambda i: (i, 0))
    out = pl.pallas_call(
        body, name=name, grid=(n * R // tr,), in_specs=[blk, blk], out_specs=blk,
        out_shape=jax.ShapeDtypeStruct((n * R, C), BF16),
        compiler_params=_cp("parallel"))(a.reshape(n * R, C), b.reshape(n * R, C))
    return out.reshape(n, R, C)


def sum_slots(recv, *, name):
    _, R, C = recv.shape
    tr = _pick(R, 512, 8)

    def body(r_ref, o_ref):
        acc = r_ref[0].astype(F32)
        for p in range(1, N_DEV):
            acc = acc + r_ref[p].astype(F32)
        o_ref[...] = acc

    return pl.pallas_call(
        body, name=name, grid=(R // tr,),
        in_specs=[pl.BlockSpec((N_DEV, tr, C), lambda i: (0, i, 0))],
        out_specs=pl.BlockSpec((tr, C), lambda i: (i, 0)),
        out_shape=jax.ShapeDtypeStruct((R, C), F32), compiler_params=_cp("parallel"))(recv)


def adamw(gsrc, w, m, v, *, name):
    slots, R, C = gsrc.shape
    tr = _pick(R, 256, 16 if gsrc.dtype == BF16 else 8)
    c1 = 1.0 / (1.0 - ADAM_B1 ** ADAM_STEP)
    c2 = 1.0 / (1.0 - ADAM_B2 ** ADAM_STEP)

    def body(g_ref, w_ref, m_ref, v_ref, go_ref, d_ref, mo_ref, vo_ref):
        g = g_ref[0].astype(F32)
        for p in range(1, slots):
            g = g + g_ref[p].astype(F32)
        m2 = ADAM_B1 * m_ref[...] + (1.0 - ADAM_B1) * g
        v2 = ADAM_B2 * v_ref[...] + (1.0 - ADAM_B2) * (g * g)
        go_ref[...] = g
        mo_ref[...] = m2
        vo_ref[...] = v2
        d_ref[...] = -ADAM_LR * ((m2 * c1) / (jnp.sqrt(v2 * c2) + ADAM_EPS) + ADAM_WD * w_ref[...])

    blk = pl.BlockSpec((tr, C), lambda i: (i, 0))
    shp = jax.ShapeDtypeStruct((R, C), F32)
    return pl.pallas_call(
        body, name=name, grid=(R // tr,),
        in_specs=[pl.BlockSpec((slots, tr, C), lambda i: (0, i, 0)), blk, blk, blk],
        out_specs=[blk] * 4, out_shape=[shp] * 4, compiler_params=_cp("parallel"))(gsrc, w, m, v)


def _pad_cols(a, n):
    return jnp.pad(a, ((0, 0), (0, n - a.shape[1])))


def _to_rows(a, B, seq, H):
    G = SSM_GROUPS
    R = H // G
    t = a[:, :H].reshape(B, seq, G, R).transpose(0, 2, 3, 1)
    t = jnp.pad(t, ((0, 0), (0, 0), (0, HEAD_ROWS - R), (0, 0)))
    return t.reshape(B, G * HEAD_ROWS, seq)


def _from_rows(a, B, seq, H):
    G = SSM_GROUPS
    R = H // G
    t = a.reshape(B, G, HEAD_ROWS, seq)[:, :, :R].transpose(0, 3, 1, 2).reshape(B * seq, H)
    return _pad_cols(t, LANE)


def _chip_split(grads):
    c_idx = lax.axis_index("c")
    keep, give = [], []
    for g in grads:
        by_chip = g.reshape((N_CHIP, 2) + g.shape[1:])
        keep.append(lax.dynamic_index_in_dim(by_chip, c_idx, axis=1, keepdims=False))
        give.append(lax.dynamic_index_in_dim(by_chip, 1 - c_idx, axis=1, keepdims=False))
    return keep, give


def _chip_sums(grads, name):
    keep, give = _chip_split(grads)
    swapped = exchange(give, ["sibling"] * len(give), name="swap_" + name)
    return [pair_sum(k, s, name=f"chip_sum_{name}_{i}") for i, (k, s) in enumerate(zip(keep, swapped))]


def local_step(x, target, loc, od_w_in_t, *, B, seq):
    T, D = x.shape
    CW = D
    heads = CW // SB_HEAD_DIM
    DI = 2 * D
    H = DI // SSM_HEAD_DIM
    XW = DI + 2 * SSM_GROUPS * SSM_STATE
    in_odd = DI + XW + H
    w1_rows = in_odd // N_DEV
    q_off, k_off, v_off, gc_off, ga_off = 3 * CW, 4 * CW, 5 * CW, 2 * CW, 6 * CW
    dt_off = DI + XW

    small_packed, small_spans = _pack_rows([loc[n] for n in SMALL_SHARDED], LANE, 8)
    n0, (g_ev_in, small_all) = rmsnorm_fwd(x, loc["ev_norm_w"], name="l0_norm",
                                           comm=([loc["ev_w_in"].astype(BF16), small_packed],
                                                 ["gather", "gather_direct"]))
    p = {n: loc[n] for n in SMALL}
    for n, a in zip(SMALL_SHARDED, _unpack_rows(small_all, small_spans)):
        p[n] = _col_unshards(a)
    p["ev_w_in"] = _col_unshards(g_ev_in)
    proj0, (g_od_in_t,) = mm_nn(n0, p["ev_w_in"], out_dtype=BF16, name="l0_in_proj",
                                comm=([od_w_in_t.astype(BF16)], ["gather"]))
    w1t = g_od_in_t[:, :w1_rows].reshape(in_odd, D)
    w1t = jnp.pad(w1t, ((0, -(-(in_odd + LANE) // 256) * 256 - in_odd), (0, 0)))
    (u2,), (g_ev_out, g_od_out) = dwconv_fwd(
        proj0, (0, CW), p["ev_dw_w"], p["ev_dw_b"], C=CW, seq=seq, glu=True, silu_out=False, name="l0_conv",
        comm=([loc["ev_w_out"].astype(BF16), loc["od_w_out"].astype(BF16)], ["gather"] * 2))
    p["ev_w_out"] = g_ev_out.reshape(-1, D)
    od_w_out = g_od_out.reshape(-1, D)
    o, ctot = sba_fwd(proj0, B=B, seq=seq, heads=heads, q_off=q_off, k_off=k_off, v_off=v_off, name="l0_attn")
    ycat = mix0_post_fwd(u2, proj0, o, p["ev_ln_w"], p["ev_ln_b"], CW=CW, gc_off=gc_off, ga_off=ga_off,
                         name="l0_post")
    h1 = mm_nn(ycat, p["ev_w_out"], add=x, out_dtype=F32, name="l0_out_proj")

    n1 = rmsnorm_fwd(h1, p["od_norm_w"], name="l1_norm")
    proj1 = mm_nt_terms([(n1, 0, D, 0)], w1t, out_dtype=BF16, name="l1_in_proj")
    u_pre, xbc = dwconv_fwd(proj1, (DI,), p["od_conv_w"], p["od_conv_b"], C=XW, seq=seq, glu=False, silu_out=True,
                            name="l1_conv")
    bias_p, alog_p = _pad_cols(p["od_dt_bias"], LANE), _pad_cols(p["od_a_log"], LANE)
    expand = _head_expand(H, DI)
    dt, cs, dt_x, cs_x = dt_fwd(proj1, bias_p, alog_p, expand, dt_off=dt_off, name="l1_dt")
    cs_row = _to_rows(cs, B, seq, H)
    d_full = jnp.repeat(p["od_d"], SSM_HEAD_DIM, axis=1)
    y_ssd, states = scan_fwd(xbc, dt_x, cs_x, cs_row, d_full, B=B, seq=seq, DI=DI, name="l1_ssd")
    yn = gnorm_fwd(y_ssd, proj1, p["od_gnorm_w"], DI=DI, name="l1_gnorm")
    h2 = mm_nn(yn, od_w_out, add=h1, out_dtype=F32, name="l1_out_proj")

    loss, dh2, dh2b, g_final = final_loss(h2, p["final_norm_w"], target, name="loss_head")

    g_od_w_out = mm_tn(yn, dh2b, out_dtype=BF16, name="l1_dw_out")
    dyn = mm_nt_terms([(dh2b, 0, D, 0)], od_w_out, out_dtype=BF16, name="l1_d_out_proj")
    dy_ssd, dz, g_gnorm = gnorm_bwd(dyn, y_ssd, proj1, p["od_gnorm_w"], DI=DI, name="l1_gnorm_bwd")
    dxbc_c, ddt_x, dcs_x, dcs_row, dd_part = scan_bwd(xbc, dt_x, cs_x, cs_row, d_full, states, dy_ssd, B=B, seq=seq,
                                                      DI=DI, name="l1_ssd_bwd")
    g_d = dd_part.sum(axis=(0, 1)).reshape(H, SSM_HEAD_DIM).sum(axis=1)[None, :]
    draw, g_bias, g_alog = dt_bwd(ddt_x, dcs_x, _from_rows(dcs_row, B, seq, H), proj1, dt, bias_p, alog_p, expand.T,
                                  dt_off=dt_off, n_heads=H, name="l1_dt_bwd")
    dxbc, g_conv_w, g_conv_b = dwconv_bwd(dxbc_c, u_pre, proj1, (DI,), p["od_conv_w"], C=XW, seq=seq, glu=False,
                                          silu_out=True, name="l1_conv_bwd")
    tw = 512 if DI % 512 == 0 else LANE
    terms = [(dz, j, tw, j * tw) for j in range(DI // tw)]
    terms += [(dxbc, j, tw, DI + j * tw) for j in range(XW // tw)]
    terms += [(draw, 0, LANE, dt_off)]
    dn1 = mm_nn_terms(terms, w1t, out_dtype=BF16, name="l1_d_in_proj")
    g_od_w_in_t = jnp.concatenate([mm_tn(dz, n1, out_dtype=BF16, name="l1_dw_in_z"),
                                   mm_tn(dxbc, n1, out_dtype=BF16, name="l1_dw_in_xbc"),
                                   mm_tn(draw, n1, out_dtype=BF16, name="l1_dw_in_dt")], axis=0)[:in_odd]
    w1_pad = (-w1_rows) % SHARD_ROW_ALIGN
    keep, give = _chip_split([jnp.pad(g_od_w_in_t.reshape(N_DEV, w1_rows, D), ((0, 0), (0, w1_pad), (0, 0))),
                              g_od_w_out.reshape(N_DEV, -1, D)])
    dh1, dh1b, g_od_norm, swapped = rmsnorm_bwd(h1, p["od_norm_w"], dn1, dh2, name="l1_norm_bwd",
                                                comm=(give, ["sibling"] * 2))
    l1_chip = [pair_sum(k, s, name=f"chip_sum_l1_{i}") for i, (k, s) in enumerate(zip(keep, swapped))]

    g_ev_w_out = mm_tn(ycat, dh1b, out_dtype=BF16, name="l0_dw_out")
    dycat = mm_nt_terms([(dh1b, 0, D, 0)], p["ev_w_out"], out_dtype=BF16, name="l0_d_out_proj")
    du2, dgc, dga, do, g_ln_w, g_ln_b = mix0_post_bwd(dycat, u2, proj0, o, p["ev_ln_w"], p["ev_ln_b"], CW=CW,
                                                      gc_off=gc_off, ga_off=ga_off, name="l0_post_bwd")
    dq, dk, dv, (r_od_in_t, r_od_out) = sba_bwd(proj0, ctot, do, B=B, seq=seq, heads=heads, q_off=q_off, k_off=k_off,
                                                v_off=v_off, name="l0_attn_bwd", comm=(l1_chip, ["chips", "chips"]))
    dga_a, dga_b, g_dw_w, g_dw_b = dwconv_bwd(du2, None, proj0, (0, CW), p["ev_dw_w"], C=CW, seq=seq, glu=True,
                                              silu_out=False, name="l0_conv_bwd")
    pieces = [dga_a, dga_b, dgc, dq, dk, dv, dga]
    g_ev_w_in = jnp.concatenate([mm_tn(n0, pc, out_dtype=BF16, name=f"l0_dw_in_{j}") for j, pc in enumerate(pieces)],
                                axis=1)
    l0_chip = _chip_sums([_col_shards(g_ev_w_in), g_ev_w_out.reshape(N_DEV, -1, D)], "l0")
    dn0, (r_ev_in, r_ev_out) = mm_nt_terms([(pc, 0, CW, j * CW) for j, pc in enumerate(pieces)], p["ev_w_in"],
                                           out_dtype=BF16, name="l0_d_in_proj", comm=(l0_chip, ["chips", "chips"]))
    dx, _, g_ev_norm = rmsnorm_bwd(x, p["ev_norm_w"], dn0, dh1, name="l0_norm_bwd")

    small = dict(ev_norm_w=g_ev_norm, ev_dw_w=g_dw_w, ev_dw_b=g_dw_b, ev_ln_w=g_ln_w, ev_ln_b=g_ln_b,
                 od_norm_w=g_od_norm, od_conv_w=g_conv_w, od_conv_b=g_conv_b, od_dt_bias=g_bias[:, :H],
                 od_a_log=g_alog[:, :H], od_d=g_d, od_gnorm_w=g_gnorm, final_norm_w=g_final)
    received = dict(ev_w_in=r_ev_in, ev_w_out=r_ev_out, od_w_in=r_od_in_t, od_w_out=r_od_out)
    return loss, dx, small, received


BIG = ("ev_w_in", "ev_w_out", "od_w_in", "od_w_out")
SMALL = ("ev_norm_w", "ev_dw_w", "ev_dw_b", "ev_ln_w", "ev_ln_b", "od_norm_w", "od_conv_w", "od_conv_b",
         "od_dt_bias", "od_a_log", "od_d", "od_gnorm_w", "final_norm_w")
SMALL_SHARDED = ("ev_dw_w", "od_norm_w", "od_conv_w", "od_conv_b", "od_gnorm_w")
ORDER = ("ev_norm_w", "ev_w_in", "ev_dw_w", "ev_dw_b", "ev_ln_w", "ev_ln_b", "ev_w_out", "od_norm_w", "od_w_in",
         "od_conv_w", "od_conv_b", "od_dt_bias", "od_a_log", "od_d", "od_gnorm_w", "od_w_out", "final_norm_w")


def _pack_rows(arrs, width, row_align):
    parts, spans, r0 = [], [], 0
    for a in arrs:
        flat = a.reshape(-1)
        rows = -(-flat.shape[0] // (width * row_align)) * row_align
        parts.append(jnp.pad(flat, (0, rows * width - flat.shape[0])).reshape(rows, width))
        spans.append((r0, a.size, a.shape))
        r0 += rows
    return jnp.concatenate(parts, axis=0), spans


def _unpack_rows(packed, spans):
    lead = packed.shape[:-2]
    width = packed.shape[-1]
    out = []
    for r0, size, shape in spans:
        rows = -(-size // width)
        blk = packed[..., r0:r0 + rows, :].reshape(lead + (rows * width,))[..., :size]
        out.append(blk.reshape(lead + tuple(shape)))
    return out


def _col_shards(a):
    R, C8 = a.shape
    return a.reshape(R, N_DEV, C8 // N_DEV).transpose(1, 0, 2)


def _col_unshards(a):
    n, R, C = a.shape
    return a.transpose(1, 0, 2).reshape(R, n * C)


def kernel(x, ev_norm_w, ev_w_in, ev_dw_w, ev_dw_b, ev_ln_w, ev_ln_b, ev_w_out, od_norm_w, od_w_in, od_conv_w, od_conv_b, od_dt_bias, od_a_log, od_d, od_gnorm_w, od_w_out, final_norm_w, loss_target, m_ev_norm_w, m_ev_w_in, m_ev_dw_w, m_ev_dw_b, m_ev_ln_w, m_ev_ln_b, m_ev_w_out, m_od_norm_w, m_od_w_in, m_od_conv_w, m_od_conv_b, m_od_dt_bias, m_od_a_log, m_od_d, m_od_gnorm_w, m_od_w_out, m_final_norm_w, v_ev_norm_w, v_ev_w_in, v_ev_dw_w, v_ev_dw_b, v_ev_ln_w, v_ev_ln_b, v_ev_w_out, v_od_norm_w, v_od_w_in, v_od_conv_w, v_od_conv_b, v_od_dt_bias, v_od_a_log, v_od_d, v_od_gnorm_w, v_od_w_out, v_final_norm_w):
    loc = dict(ev_norm_w=ev_norm_w, ev_w_in=ev_w_in, ev_dw_w=ev_dw_w, ev_dw_b=ev_dw_b, ev_ln_w=ev_ln_w,
               ev_ln_b=ev_ln_b, ev_w_out=ev_w_out, od_norm_w=od_norm_w, od_w_in=od_w_in, od_conv_w=od_conv_w,
               od_conv_b=od_conv_b, od_dt_bias=od_dt_bias, od_a_log=od_a_log, od_d=od_d, od_gnorm_w=od_gnorm_w,
               od_w_out=od_w_out, final_norm_w=final_norm_w)
    mom = dict(ev_norm_w=m_ev_norm_w, ev_w_in=m_ev_w_in, ev_dw_w=m_ev_dw_w, ev_dw_b=m_ev_dw_b, ev_ln_w=m_ev_ln_w,
               ev_ln_b=m_ev_ln_b, ev_w_out=m_ev_w_out, od_norm_w=m_od_norm_w, od_w_in=m_od_w_in,
               od_conv_w=m_od_conv_w, od_conv_b=m_od_conv_b, od_dt_bias=m_od_dt_bias, od_a_log=m_od_a_log,
               od_d=m_od_d, od_gnorm_w=m_od_gnorm_w, od_w_out=m_od_w_out, final_norm_w=m_final_norm_w)
    var = dict(ev_norm_w=v_ev_norm_w, ev_w_in=v_ev_w_in, ev_dw_w=v_ev_dw_w, ev_dw_b=v_ev_dw_b, ev_ln_w=v_ev_ln_w,
               ev_ln_b=v_ev_ln_b, ev_w_out=v_ev_w_out, od_norm_w=v_od_norm_w, od_w_in=v_od_w_in,
               od_conv_w=v_od_conv_w, od_conv_b=v_od_conv_b, od_dt_bias=v_od_dt_bias, od_a_log=v_od_a_log,
               od_d=v_od_d, od_gnorm_w=v_od_gnorm_w, od_w_out=v_od_w_out, final_norm_w=v_final_norm_w)
    shapes = {n: loc[n].shape for n in ORDER}
    loc = {n: (a.reshape(1, -1) if a.ndim == 1 else a.reshape(a.shape[-2:]) if a.ndim == 3 else a)
           for n, a in loc.items()}
    mom = {n: a.reshape(loc[n].shape) for n, a in mom.items()}
    var = {n: a.reshape(loc[n].shape) for n, a in var.items()}

    B, seq, D = x.shape
    me = 4 * lax.axis_index("x") + 2 * lax.axis_index("y") + lax.axis_index("c")

    w1_rows = loc["od_w_in"].shape[1]
    w1_pad = (-w1_rows) % SHARD_ROW_ALIGN

    def to_t(a):
        return jnp.pad(a.T, ((0, w1_pad), (0, 0)))

    loss, dx, grads, received = local_step(x.reshape(B * seq, D), loss_target.reshape(B * seq, D), loc,
                                           to_t(loc["od_w_in"]), B=B, seq=seq)

    gsmall_packed, gsmall_spans = _pack_rows([grads[n] for n in SMALL] + [loss], LANE, 8)
    (gsmall_recv,) = exchange([gsmall_packed], ["gather_direct"], name="gather_small_grads")

    big_out = [{} for _ in range(4)]
    for n in ("ev_w_in", "ev_w_out", "od_w_out"):
        for kind, a in enumerate(adamw(received[n], loc[n], mom[n], var[n], name="adamw_" + n)):
            big_out[kind][n] = a
    for kind, a in enumerate(adamw(received["od_w_in"], to_t(loc["od_w_in"]), to_t(mom["od_w_in"]),
                                   to_t(var["od_w_in"]), name="adamw_od_w_in")):
        big_out[kind]["od_w_in"] = a[:w1_rows].T

    summed = _unpack_rows(sum_slots(gsmall_recv, name="sum_small_grads"), gsmall_spans)
    loss_total = summed[-1][0, 0]
    gsmall = dict(zip(SMALL, summed[:-1]))
    for n in SMALL_SHARDED:
        width = loc[n].shape[1]
        gsmall[n] = lax.dynamic_slice_in_dim(gsmall[n], me * width, width, axis=1)
    gs, sspans = _pack_rows([gsmall[n] for n in SMALL], LANE, 8)
    ws, _ = _pack_rows([loc[n] for n in SMALL], LANE, 8)
    ms, _ = _pack_rows([mom[n] for n in SMALL], LANE, 8)
    vs, _ = _pack_rows([var[n] for n in SMALL], LANE, 8)
    small_out = [dict(zip(SMALL, _unpack_rows(a, sspans))) for a in adamw(gs[None], ws, ms, vs, name="adamw_small")]

    outs = [loss_total, dx.reshape(B, seq, D)]
    for kind in range(4):
        for n in ORDER:
            src = big_out[kind] if n in BIG else small_out[kind]
            outs.append(src[n].reshape(shapes[n]))
    return tuple(outs)
```

```python
import jax
import jax.numpy as jnp
from jax import lax
from jax.experimental import pallas as pl
from jax.experimental.pallas import tpu as pltpu

F32 = jnp.float32
BF16 = jnp.bfloat16

EPS = 1e-6
N_DEV = 8
LANE = 128
VMEM_LIMIT_BYTES = 48 * 1024 * 1024

SB_HEAD_DIM = 128
SSM_HEAD_DIM = 64
SSM_GROUPS = 4
SSM_STATE = 128
SSM_CHUNK = 128
HALO = 32
HEAD_ROWS = 8
NEG_BIG = -1e30

ADAM_LR = 0.001
ADAM_B1 = 0.9
ADAM_B2 = 0.999
ADAM_EPS = 1e-08
ADAM_WD = 0.01
ADAM_STEP = 10

NT = (((1,), (1,)), ((), ()))
TN = (((0,), (0,)), ((), ()))


def _cp(*sem):
    return pltpu.CompilerParams(dimension_semantics=sem, vmem_limit_bytes=VMEM_LIMIT_BYTES)


def _pick(n, cap, align):
    if n <= cap:
        return n
    t = (cap // align) * align
    while t >= align:
        if n % t == 0:
            return t
        t -= align
    raise ValueError(f"no tile for {n} (cap {cap}, align {align})")


def _sigmoid(x):
    return 0.5 * jnp.tanh(0.5 * x) + 0.5


def _silu(x):
    return x * _sigmoid(x)


def _silu_grad(x):
    s = _sigmoid(x)
    return s * (1.0 + x * (1.0 - s))


def _dot(a, b, dims=None):
    if dims is None:
        return jnp.dot(a, b, preferred_element_type=F32)
    return lax.dot_general(a, b, dims, preferred_element_type=F32)


def _tri_dot3(tri, x):
    hi = x.astype(BF16)
    r1 = x - hi.astype(F32)
    mid = r1.astype(BF16)
    lo = (r1 - mid.astype(F32)).astype(BF16)
    return _dot(tri, hi) + _dot(tri, mid) + _dot(tri, lo)


def _grid_step(n_inner):
    return lambda: pl.program_id(0) * n_inner + pl.program_id(1)


def mm_nn(a, b, *, add=None, out_dtype, name, comm=None):
    M, K = a.shape
    N = b.shape[1]
    tm = _pick(M, 2048 if K <= 1024 and add is None else 1024, 16)
    tn = _pick(N, 1024, LANE)

    def body(*refs):
        if add is None:
            a_ref, b_ref, o_ref = refs
        else:
            a_ref, b_ref, add_ref, o_ref = refs
        acc = _dot(a_ref[...], b_ref[...])
        if add is not None:
            acc = acc + add_ref[...]
        o_ref[...] = acc.astype(out_dtype)

    in_specs = [pl.BlockSpec((tm, K), lambda i, j: (i, 0)), pl.BlockSpec((K, tn), lambda i, j: (0, j))]
    args = [a, b]
    if add is not None:
        in_specs.append(pl.BlockSpec((tm, tn), lambda i, j: (i, j)))
        args.append(add)
    grid = (M // tm, N // tn)
    body, c_in, c_args, c_out, c_shape, c_scratch = _hosted(body, len(args), 1, comm, _grid_step(grid[1]),
                                                            grid[0] * grid[1])
    out = pl.pallas_call(
        body, name=name, grid=grid, in_specs=in_specs + c_in,
        out_specs=[pl.BlockSpec((tm, tn), lambda i, j: (i, j))] + c_out,
        out_shape=[jax.ShapeDtypeStruct((M, N), out_dtype)] + c_shape, scratch_shapes=c_scratch,
        compiler_params=_cp(*(("arbitrary",) * 2 if comm else ("parallel",) * 2)))(*args, *c_args)
    return (out[0], out[1:]) if comm else out[0]


def mm_nt_terms(terms, b, *, out_dtype, name, comm=None):
    M = terms[0][0].shape[0]
    N = b.shape[0]
    n_terms = len(terms)
    if n_terms == 1:
        tm, tn = _pick(M, 2048, 16), _pick(N, 1024, LANE)
    else:
        tm, tn = _pick(M, 256, 16), _pick(N, 1024, LANE)

    def body(*refs):
        o_ref = refs[-1]
        acc = None
        for t in range(n_terms):
            part = _dot(refs[2 * t][...], refs[2 * t + 1][...], NT)
            acc = part if acc is None else acc + part
        o_ref[...] = acc.astype(out_dtype)

    in_specs, args = [], []
    for arr, cb, w, off in terms:
        assert off % w == 0
        in_specs.append(pl.BlockSpec((tm, w), lambda i, j, cb=cb: (i, cb)))
        in_specs.append(pl.BlockSpec((tn, w), lambda i, j, ob=off // w: (j, ob)))
        args += [arr, b]
    grid = (M // tm, N // tn)
    body, c_in, c_args, c_out, c_shape, c_scratch = _hosted(body, len(args), 1, comm, _grid_step(grid[1]),
                                                            grid[0] * grid[1])
    out = pl.pallas_call(
        body, name=name, grid=grid, in_specs=in_specs + c_in,
        out_specs=[pl.BlockSpec((tm, tn), lambda i, j: (i, j))] + c_out,
        out_shape=[jax.ShapeDtypeStruct((M, N), out_dtype)] + c_shape, scratch_shapes=c_scratch,
        compiler_params=_cp(*(("arbitrary",) * 2 if comm else ("parallel",) * 2)))(*args, *c_args)
    return (out[0], out[1:]) if comm else out[0]


def mm_nn_terms(terms, b, *, out_dtype, name):
    M = terms[0][0].shape[0]
    N = b.shape[1]
    tm = _pick(M, 256, 16)
    tn = _pick(N, 1024, LANE)
    n_terms = len(terms)

    def body(*refs):
        o_ref = refs[-1]
        acc = None
        for t in range(n_terms):
            part = _dot(refs[2 * t][...], refs[2 * t + 1][...])
            acc = part if acc is None else acc + part
        o_ref[...] = acc.astype(out_dtype)

    in_specs, args = [], []
    for arr, cb, w, off in terms:
        assert off % w == 0
        in_specs.append(pl.BlockSpec((tm, w), lambda i, j, cb=cb: (i, cb)))
        in_specs.append(pl.BlockSpec((w, tn), lambda i, j, ob=off // w: (ob, j)))
        args += [arr, b]
    return pl.pallas_call(
        body, name=name, grid=(M // tm, N // tn), in_specs=in_specs,
        out_specs=pl.BlockSpec((tm, tn), lambda i, j: (i, j)),
        out_shape=jax.ShapeDtypeStruct((M, N), out_dtype),
        compiler_params=_cp("parallel", "parallel"))(*args)


def mm_tn(a, b, *, out_dtype, name):
    T, M = a.shape
    N = b.shape[1]
    tm = _pick(M, 1024, LANE)
    tn = _pick(N, 1024, LANE)
    tk = _pick(T, 2048, 16)
    nk = T // tk

    def body(a_ref, b_ref, o_ref, acc_ref):
        k = pl.program_id(2)

        @pl.when(k == 0)
        def _():
            acc_ref[...] = jnp.zeros_like(acc_ref)

        acc_ref[...] += _dot(a_ref[...], b_ref[...], TN)

        @pl.when(k == nk - 1)
        def _():
            o_ref[...] = acc_ref[...].astype(out_dtype)

    return pl.pallas_call(
        body, name=name, grid=(M // tm, N // tn, nk),
        in_specs=[pl.BlockSpec((tk, tm), lambda i, j, k: (k, i)), pl.BlockSpec((tk, tn), lambda i, j, k: (k, j))],
        out_specs=pl.BlockSpec((tm, tn), lambda i, j, k: (i, j)),
        out_shape=jax.ShapeDtypeStruct((M, N), out_dtype),
        scratch_shapes=[pltpu.VMEM((tm, tn), F32)],
        compiler_params=_cp("parallel", "parallel", "arbitrary"))(a, b)


def rmsnorm_fwd(h, w, *, name, comm=None):
    T, D = h.shape
    tt = _pick(T, 512, 16)

    def body(h_ref, w_ref, n_ref):
        x = h_ref[...]
        r = lax.rsqrt(jnp.mean(x * x, axis=-1, keepdims=True) + EPS)
        n_ref[...] = (x * r * w_ref[...]).astype(BF16)

    body, c_in, c_args, c_out, c_shape, c_scratch = _hosted(body, 2, 1, comm, lambda: pl.program_id(0), T // tt)
    out = pl.pallas_call(
        body, name=name, grid=(T // tt,),
        in_specs=[pl.BlockSpec((tt, D), lambda i: (i, 0)), pl.BlockSpec((1, D), lambda i: (0, 0))] + c_in,
        out_specs=[pl.BlockSpec((tt, D), lambda i: (i, 0))] + c_out,
        out_shape=[jax.ShapeDtypeStruct((T, D), BF16)] + c_shape, scratch_shapes=c_scratch,
        compiler_params=_cp("arbitrary" if comm else "parallel"))(h, w, *c_args)
    return (out[0], out[1:]) if comm else out[0]


def rmsnorm_bwd(h, w, dn, dres, *, name, comm=None):
    T, D = h.shape
    tt = _pick(T, 512, 16)

    def body(h_ref, w_ref, dn_ref, dres_ref, dh_ref, dhb_ref, gw_ref):
        @pl.when(pl.program_id(0) == 0)
        def _():
            gw_ref[...] = jnp.zeros_like(gw_ref)

        x = h_ref[...]
        r = lax.rsqrt(jnp.mean(x * x, axis=-1, keepdims=True) + EPS)
        xhat = x * r
        g = dn_ref[...].astype(F32)
        gw_ref[...] += jnp.sum(g * xhat, axis=0, keepdims=True)
        dxh = g * w_ref[...]
        dx = r * (dxh - xhat * jnp.mean(dxh * xhat, axis=-1, keepdims=True))
        dh = dres_ref[...] + dx
        dh_ref[...] = dh
        dhb_ref[...] = dh.astype(BF16)

    row = pl.BlockSpec((tt, D), lambda i: (i, 0))
    vec = pl.BlockSpec((1, D), lambda i: (0, 0))
    body, c_in, c_args, c_out, c_shape, c_scratch = _hosted(body, 4, 3, comm, lambda: pl.program_id(0), T // tt)
    out = pl.pallas_call(
        body, name=name, grid=(T // tt,), in_specs=[row, vec, row, row] + c_in, out_specs=[row, row, vec] + c_out,
        out_shape=[jax.ShapeDtypeStruct((T, D), F32), jax.ShapeDtypeStruct((T, D), BF16),
                   jax.ShapeDtypeStruct((1, D), F32)] + c_shape,
        scratch_shapes=c_scratch, compiler_params=_cp("arbitrary"))(h, w, dn, dres, *c_args)
    return (out[0], out[1], out[2], out[3:]) if comm else out


def final_loss(h, w, target, *, name):
    T, D = h.shape
    tt = _pick(T, 512, 16)

    def body(h_ref, w_ref, t_ref, loss_ref, dh_ref, dhb_ref, gw_ref):
        @pl.when(pl.program_id(0) == 0)
        def _():
            gw_ref[...] = jnp.zeros_like(gw_ref)
            loss_ref[...] = jnp.zeros_like(loss_ref)

        x = h_ref[...]
        r = lax.rsqrt(jnp.mean(x * x, axis=-1, keepdims=True) + EPS)
        xhat = x * r
        e = xhat * w_ref[...] - t_ref[...]
        loss_ref[...] += jnp.sum(e * e) * (0.5 / D)
        g = e * (1.0 / D)
        gw_ref[...] += jnp.sum(g * xhat, axis=0, keepdims=True)
        dxh = g * w_ref[...]
        dh = r * (dxh - xhat * jnp.mean(dxh * xhat, axis=-1, keepdims=True))
        dh_ref[...] = dh
        dhb_ref[...] = dh.astype(BF16)

    row = pl.BlockSpec((tt, D), lambda i: (i, 0))
    vec = pl.BlockSpec((1, D), lambda i: (0, 0))
    one = pl.BlockSpec((1, LANE), lambda i: (0, 0))
    return pl.pallas_call(
        body, name=name, grid=(T // tt,), in_specs=[row, vec, row], out_specs=[one, row, row, vec],
        out_shape=[jax.ShapeDtypeStruct((1, LANE), F32), jax.ShapeDtypeStruct((T, D), F32),
                   jax.ShapeDtypeStruct((T, D), BF16), jax.ShapeDtypeStruct((1, D), F32)],
        compiler_params=_cp("arbitrary"))(h, w, target)


CONV_CHUNK = 32
ROW_CHUNK = 16
SUBLANES = 8


def _conv_tiles(seq, C, K):
    return _pick(seq, 1024 if K <= SUBLANES else 512, HALO), _pick(C, 512, LANE)


def _residues(offsets):
    return sorted({s % SUBLANES for s in offsets} - {0})


def _fill_shifted(buf, shifted, residues):
    n = buf.shape[0] - SUBLANES
    for i, r in enumerate(residues):
        shifted[i, 0:n, :] = buf[r:r + n, :]


def _tap(buf, shifted, residues, offset, start, rows):
    r = offset % SUBLANES
    base = offset - r
    ref = buf if r == 0 else shifted.at[residues.index(r)]
    return ref[pl.ds(start + base, rows), :]


def dwconv_fwd(src, offs, w, b, *, C, seq, glu, silu_out, name, comm=None):
    T = src.shape[0]
    K = w.shape[0]
    assert K - 1 <= HALO
    tt, tc = _conv_tiles(seq, C, K)
    n_in = 2 if glu else 1
    per = tt // HALO
    offsets = [HALO - (K - 1) + k for k in range(K)]
    residues = _residues(offsets)

    def body(*refs):
        cur = refs[0:2 * n_in:2]
        halo = refs[1:2 * n_in:2]
        w_ref, b_ref = refs[2 * n_in], refs[2 * n_in + 1]
        outs = refs[2 * n_in + 2:-2]
        buf, shifted = refs[-2], refs[-1]
        i = pl.program_id(1)
        first = (i * tt) % seq == 0

        def pre(rs, rows):
            v = rs[0][rows, :].astype(F32)
            return v * _sigmoid(rs[1][rows, :].astype(F32)) if glu else v

        def build(ci, carry):
            start = pl.multiple_of(ci * CONV_CHUNK, CONV_CHUNK)
            buf[pl.ds(HALO + start, CONV_CHUNK), :] = pre(cur, pl.ds(start, CONV_CHUNK))
            return carry

        buf[0:HALO, :] = jnp.where(first, 0.0, pre(halo, slice(None)))
        lax.fori_loop(0, tt // CONV_CHUNK, build, 0, unroll=2)
        _fill_shifted(buf, shifted, residues)

        def chunk(ci, carry):
            start = pl.multiple_of(ci * CONV_CHUNK, CONV_CHUNK)
            acc = jnp.broadcast_to(b_ref[...], (CONV_CHUNK, tc))
            for k in range(K):
                acc = acc + w_ref[k:k + 1, :] * _tap(buf, shifted, residues, offsets[k], start, CONV_CHUNK)
            outs[0][pl.ds(start, CONV_CHUNK), :] = acc.astype(BF16)
            if silu_out:
                outs[1][pl.ds(start, CONV_CHUNK), :] = _silu(acc).astype(BF16)
            return carry

        lax.fori_loop(0, tt // CONV_CHUNK, chunk, 0)

    in_specs, args = [], []
    for off in offs:
        assert off % tc == 0
        in_specs.append(pl.BlockSpec((tt, tc), lambda j, i, ob=off // tc: (i, ob + j)))
        in_specs.append(pl.BlockSpec((HALO, tc), lambda j, i, ob=off // tc: (jnp.maximum(i * per - 1, 0), ob + j)))
        args += [src, src]
    in_specs += [pl.BlockSpec((K, tc), lambda j, i: (0, j)), pl.BlockSpec((1, tc), lambda j, i: (0, j))]
    args += [w, b]
    n_out = 2 if silu_out else 1
    grid = (C // tc, T // tt)
    body, c_in, c_args, c_out, c_shape, c_scratch = _hosted(body, len(args), n_out, comm, _grid_step(grid[1]),
                                                            grid[0] * grid[1])
    out = pl.pallas_call(
        body, name=name, grid=grid, in_specs=in_specs + c_in,
        out_specs=[pl.BlockSpec((tt, tc), lambda j, i: (i, j))] * n_out + c_out,
        out_shape=[jax.ShapeDtypeStruct((T, C), BF16)] * n_out + c_shape,
        scratch_shapes=[pltpu.VMEM((HALO + tt, tc), F32), pltpu.VMEM((max(len(residues), 1), HALO + tt, tc), F32)]
        + c_scratch,
        compiler_params=_cp("arbitrary" if comm else "parallel", "arbitrary"))(*args, *c_args)
    return (out[:n_out], out[n_out:]) if comm else out


def dwconv_bwd(du, u, src, offs, w, *, C, seq, glu, silu_out, name):
    T = src.shape[0]
    K = w.shape[0]
    assert K - 1 <= HALO
    tt, tc = _conv_tiles(seq, C, K)
    n_in = 2 if glu else 1
    per = tt // HALO
    last_blk = T // HALO - 1
    g_offsets = [K - 1 - k for k in range(K)]
    g_res = _residues(g_offsets)

    def body(*refs):
        pos = 0
        du_cur, du_nxt = refs[0], refs[1]
        pos = 2
        if silu_out:
            u_cur, u_nxt = refs[2], refs[3]
            pos = 4
        cur = refs[pos:pos + n_in]
        pos += n_in
        w_ref = refs[pos]
        outs = refs[pos + 1:pos + 1 + n_in]
        dw_ref, db_ref = refs[pos + 1 + n_in], refs[pos + 2 + n_in]
        gbuf, gshift, dw_acc, db_acc = refs[-4:]
        i = pl.program_id(1)
        last = ((i + 1) * tt) % seq == 0

        @pl.when(i == 0)
        def _():
            dw_acc[...] = jnp.zeros_like(dw_acc)
            db_acc[...] = jnp.zeros_like(db_acc)

        def build(ci, carry):
            rows = pl.ds(pl.multiple_of(ci * CONV_CHUNK, CONV_CHUNK), CONV_CHUNK)
            g = du_cur[rows, :].astype(F32)
            if silu_out:
                g = g * _silu_grad(u_cur[rows, :].astype(F32))
            gbuf[rows, :] = g
            return carry

        lax.fori_loop(0, tt // CONV_CHUNK, build, 0, unroll=2)
        g_nxt = du_nxt[...].astype(F32)
        if silu_out:
            g_nxt = g_nxt * _silu_grad(u_nxt[...].astype(F32))
        gbuf[tt:tt + HALO, :] = jnp.where(last, 0.0, g_nxt)
        _fill_shifted(gbuf, gshift, g_res)

        def fold(v):
            out = v[0:SUBLANES]
            for s in range(SUBLANES, CONV_CHUNK, SUBLANES):
                out = out + v[s:s + SUBLANES]
            return out

        def chunk(ci, carry):
            start = pl.multiple_of(ci * CONV_CHUNK, CONV_CHUNK)
            rows = pl.ds(start, CONV_CHUNK)
            a = cur[0][rows, :].astype(F32)
            if glu:
                s = _sigmoid(cur[1][rows, :].astype(F32))
                x_in = a * s
            else:
                x_in = a
            dx = jnp.zeros((CONV_CHUNK, tc), F32)
            for k in range(K):
                g_k = _tap(gbuf, gshift, g_res, g_offsets[k], start, CONV_CHUNK)
                dx = dx + w_ref[k:k + 1, :] * g_k
                dw_acc[k * SUBLANES:(k + 1) * SUBLANES, :] += fold(g_k * x_in)
            db_acc[...] += fold(gbuf[rows, :])
            if glu:
                outs[0][rows, :] = (dx * s).astype(BF16)
                outs[1][rows, :] = (dx * a * s * (1.0 - s)).astype(BF16)
            else:
                outs[0][rows, :] = dx.astype(BF16)
            return carry

        lax.fori_loop(0, tt // CONV_CHUNK, chunk, 0)

        @pl.when(i == T // tt - 1)
        def _():
            for k in range(K):
                dw_ref[k:k + 1, :] = jnp.sum(dw_acc[k * SUBLANES:(k + 1) * SUBLANES, :], axis=0, keepdims=True)
            db_ref[...] = jnp.sum(db_acc[...], axis=0, keepdims=True)

    def cur_spec(ob):
        return pl.BlockSpec((tt, tc), lambda j, i: (i, ob + j))

    def nxt_spec(ob):
        return pl.BlockSpec((HALO, tc), lambda j, i: (jnp.minimum((i + 1) * per, last_blk), ob + j))

    in_specs = [cur_spec(0), nxt_spec(0)]
    args = [du, du]
    if silu_out:
        in_specs += [cur_spec(0), nxt_spec(0)]
        args += [u, u]
    for off in offs:
        assert off % tc == 0
        in_specs.append(cur_spec(off // tc))
        args.append(src)
    in_specs.append(pl.BlockSpec((K, tc), lambda j, i: (0, j)))
    args.append(w)
    out_specs = [pl.BlockSpec((tt, tc), lambda j, i: (i, j))] * n_in
    out_specs += [pl.BlockSpec((K, tc), lambda j, i: (0, j)), pl.BlockSpec((1, tc), lambda j, i: (0, j))]
    out_shape = [jax.ShapeDtypeStruct((T, C), BF16)] * n_in
    out_shape += [jax.ShapeDtypeStruct((K, C), F32), jax.ShapeDtypeStruct((1, C), F32)]
    return pl.pallas_call(
        body, name=name, grid=(C // tc, T // tt), in_specs=in_specs, out_specs=out_specs, out_shape=out_shape,
        scratch_shapes=[pltpu.VMEM((tt + HALO, tc), F32), pltpu.VMEM((max(len(g_res), 1), tt + HALO, tc), F32),
                        pltpu.VMEM((K * SUBLANES, tc), F32), pltpu.VMEM((SUBLANES, tc), F32)],
        compiler_params=_cp("parallel", "arbitrary"))(*args)


def mix0_post_fwd(u2, proj, o, ln_w, ln_b, *, CW, gc_off, ga_off, name):
    T = u2.shape[0]
    tt = _pick(T, 256, 16)

    def body(u_ref, gc_ref, ga_ref, o_ref, lw_ref, lb_ref, y_ref):
        def chunk(ci, carry):
            rows = pl.ds(pl.multiple_of(ci * ROW_CHUNK, ROW_CHUNK), ROW_CHUNK)
            u = u_ref[rows, :].astype(F32)
            mu = jnp.mean(u, axis=-1, keepdims=True)
            xc = u - mu
            r = lax.rsqrt(jnp.mean(xc * xc, axis=-1, keepdims=True) + EPS)
            u3 = xc * r * lw_ref[...] + lb_ref[...]
            y_ref[rows, 0:CW] = (_silu(u3) * _silu(gc_ref[rows, :].astype(F32))).astype(BF16)
            y_ref[rows, CW:2 * CW] = (o_ref[rows, :].astype(F32) * _silu(ga_ref[rows, :].astype(F32))).astype(BF16)
            return carry

        lax.fori_loop(0, tt // ROW_CHUNK, chunk, 0, unroll=4)

    row = pl.BlockSpec((tt, CW), lambda i: (i, 0))
    vec = pl.BlockSpec((1, CW), lambda i: (0, 0))
    return pl.pallas_call(
        body, name=name, grid=(T // tt,),
        in_specs=[row, pl.BlockSpec((tt, CW), lambda i: (i, gc_off // CW)),
                  pl.BlockSpec((tt, CW), lambda i: (i, ga_off // CW)), row, vec, vec],
        out_specs=pl.BlockSpec((tt, 2 * CW), lambda i: (i, 0)),
        out_shape=jax.ShapeDtypeStruct((T, 2 * CW), BF16),
        compiler_params=_cp("parallel"))(u2, proj, proj, o, ln_w, ln_b)


def mix0_post_bwd(dy, u2, proj, o, ln_w, ln_b, *, CW, gc_off, ga_off, name):
    T = u2.shape[0]
    tt = _pick(T, 256, 16)

    def body(dy_ref, u_ref, gc_ref, ga_ref, o_ref, lw_ref, lb_ref, du_ref, dgc_ref, dga_ref, do_ref, dlw_ref, dlb_ref,
             lw_acc, lb_acc):
        i = pl.program_id(0)

        @pl.when(i == 0)
        def _():
            lw_acc[...] = jnp.zeros_like(lw_acc)
            lb_acc[...] = jnp.zeros_like(lb_acc)

        def fold(v):
            out = v[0:SUBLANES]
            for s in range(SUBLANES, ROW_CHUNK, SUBLANES):
                out = out + v[s:s + SUBLANES]
            return out

        def chunk(ci, carry):
            rows = pl.ds(pl.multiple_of(ci * ROW_CHUNK, ROW_CHUNK), ROW_CHUNK)
            dyc = dy_ref[rows, 0:CW].astype(F32)
            dya = dy_ref[rows, CW:2 * CW].astype(F32)
            u = u_ref[rows, :].astype(F32)
            mu = jnp.mean(u, axis=-1, keepdims=True)
            xc = u - mu
            r = lax.rsqrt(jnp.mean(xc * xc, axis=-1, keepdims=True) + EPS)
            xhat = xc * r
            u3 = xhat * lw_ref[...] + lb_ref[...]
            gc = gc_ref[rows, :].astype(F32)
            dgc_ref[rows, :] = (dyc * _silu(u3) * _silu_grad(gc)).astype(BF16)
            du3 = dyc * _silu(gc) * _silu_grad(u3)
            lw_acc[...] += fold(du3 * xhat)
            lb_acc[...] += fold(du3)
            dxh = du3 * lw_ref[...]
            du = r * (dxh - jnp.mean(dxh, axis=-1, keepdims=True)
                      - xhat * jnp.mean(dxh * xhat, axis=-1, keepdims=True))
            du_ref[rows, :] = du.astype(BF16)
            ga = ga_ref[rows, :].astype(F32)
            ov = o_ref[rows, :].astype(F32)
            do_ref[rows, :] = (dya * _silu(ga)).astype(BF16)
            dga_ref[rows, :] = (dya * ov * _silu_grad(ga)).astype(BF16)
            return carry

        lax.fori_loop(0, tt // ROW_CHUNK, chunk, 0, unroll=4)

        @pl.when(i == T // tt - 1)
        def _():
            dlw_ref[...] = jnp.sum(lw_acc[...], axis=0, keepdims=True)
            dlb_ref[...] = jnp.sum(lb_acc[...], axis=0, keepdims=True)

    row = pl.BlockSpec((tt, CW), lambda i: (i, 0))
    vec = pl.BlockSpec((1, CW), lambda i: (0, 0))
    big = jax.ShapeDtypeStruct((T, CW), BF16)
    small = jax.ShapeDtypeStruct((1, CW), F32)
    return pl.pallas_call(
        body, name=name, grid=(T // tt,),
        in_specs=[pl.BlockSpec((tt, 2 * CW), lambda i: (i, 0)), row,
                  pl.BlockSpec((tt, CW), lambda i: (i, gc_off // CW)),
                  pl.BlockSpec((tt, CW), lambda i: (i, ga_off // CW)), row, vec, vec],
        out_specs=[row, row, row, row, vec, vec],
        out_shape=[big, big, big, big, small, small],
        scratch_shapes=[pltpu.VMEM((SUBLANES, CW), F32), pltpu.VMEM((SUBLANES, CW), F32)],
        compiler_params=_cp("arbitrary"))(dy, u2, proj, proj, o, ln_w, ln_b)


SB_UNDERFLOW = 110.0
SB_BOUND_MARGIN = 1.02


def _sb_tile(seq):
    return _pick(seq, 256, LANE)


def _softplus(z):
    return jnp.maximum(z, 0.0) + jnp.log(1.0 + jnp.exp(-jnp.abs(z)))


def _tri01(n, lower):
    i = lax.broadcasted_iota(jnp.int32, (n, n), 0)
    j = lax.broadcasted_iota(jnp.int32, (n, n), 1)
    return ((i >= j) if lower else (i <= j)).astype(BF16)


SB_HEADS_FWD = 4
SB_HEADS_BWD = 2


def _sb_heads_per_step(heads, want):
    while heads % want:
        want //= 2
    return want


def sba_fwd(proj, *, B, seq, heads, q_off, k_off, v_off, name):
    dh = SB_HEAD_DIM
    tq = _sb_tile(seq)
    assert tq % (2 * LANE) == 0
    nq = seq // tq
    hps = _sb_heads_per_step(heads, SB_HEADS_FWD)
    hw = hps * dh
    scale = dh ** -0.5

    def body(q_ref, k_ref, v_ref, tri_ref, o_ref, ct_ref, acc_ref, kmax_ref):
        qi = pl.program_id(1)
        tri = tri_ref[...]
        qs = [(q_ref[:, h * dh:(h + 1) * dh].astype(F32) * scale).astype(BF16) for h in range(hps)]

        @pl.when(qi == 0)
        def _():
            def chunk(i, best):
                rows = k_ref[pl.ds(pl.multiple_of(i * tq, tq), tq), :].astype(F32)
                sq = rows * rows
                return tuple(jnp.maximum(best[h], jnp.max(jnp.sum(sq[:, h * dh:(h + 1) * dh], axis=1, keepdims=True),
                                                          axis=0, keepdims=True)) for h in range(hps))

            best = lax.fori_loop(0, nq, chunk, (jnp.zeros((1, 1), F32),) * hps)
            for h in range(hps):
                kmax_ref[h] = jnp.broadcast_to(jnp.sqrt(best[h]), (8, LANE))

        z_bound = [jnp.sqrt(jnp.sum(qs[h].astype(F32) ** 2, axis=1, keepdims=True))
                   * (SB_BOUND_MARGIN * jnp.max(kmax_ref[h], keepdims=True)) for h in range(hps)]

        def part(h, q_rows, start, n_keys, r, mask):
            k_blk = k_ref[pl.ds(start, n_keys), h * dh:(h + 1) * dh]
            v_blk = v_ref[pl.ds(start, n_keys), h * dh:(h + 1) * dh]
            z = _dot(q_rows, k_blk, NT)
            sp = _softplus(z)
            if mask is not None:
                sp = jnp.where(mask, sp, 0.0)
            wts = jnp.exp(z - (_dot(sp.astype(BF16), tri[0:n_keys, 0:n_keys]) + r))
            if mask is not None:
                wts = jnp.where(mask, wts, 0.0)
            return _dot(wts.astype(BF16), v_blk), r + jnp.sum(sp, axis=-1, keepdims=True)

        below = lax.broadcasted_iota(jnp.int32, (tq, tq), 1) < lax.broadcasted_iota(jnp.int32, (tq, tq), 0)
        has_left = qi > 0
        left = pl.multiple_of(jnp.maximum(qi - 1, 0) * tq, tq)
        rs = []
        for h in range(hps):
            pv_d, r = part(h, qs[h], pl.multiple_of(qi * tq, tq), tq, jnp.zeros((tq, 1), F32), below)
            pv_l, r = part(h, qs[h], left, tq, r, has_left)
            acc_ref[:, h * dh:(h + 1) * dh] = pv_d + pv_l
            rs.append(r)
        rs = tuple(rs)

        def block(start, rs):
            pvs, out = [], []
            for h in range(hps):
                pv, r = part(h, qs[h], start, tq, rs[h], None)
                pvs.append(pv)
                out.append(r)
            return pvs, tuple(out)

        def more(c):
            j, rs = c
            slack = rs[0] - z_bound[0]
            for h in range(1, hps):
                slack = jnp.minimum(slack, rs[h] - z_bound[h])
            return jnp.logical_and(j < qi, jnp.min(slack) <= SB_UNDERFLOW)

        def step(c):
            j, rs = c
            pvs, rs = block(pl.multiple_of((qi - 1 - j) * tq, tq), rs)
            for h in range(hps):
                acc_ref[:, h * dh:(h + 1) * dh] += pvs[h]
            return j + 1, rs

        n_left, totals = lax.while_loop(more, step, (has_left.astype(jnp.int32), rs))
        o_ref[...] = acc_ref[...].astype(BF16)
        for h in range(hps):
            ct_ref[0, 0, h, 0:8, :] = jnp.broadcast_to(totals[h], (tq, LANE)).T[0:8, :]
            ct_ref[0, 0, h, 8:16, :] = jnp.full((8, tq), n_left, F32)

    qb, kb, vb = q_off // hw, k_off // hw, v_off // hw
    G = heads // hps
    return pl.pallas_call(
        body, name=name, grid=(B * G, nq),
        in_specs=[pl.BlockSpec((tq, hw), lambda g, i: ((g // G) * nq + i, qb + g % G)),
                  pl.BlockSpec((seq, hw), lambda g, i: (g // G, kb + g % G)),
                  pl.BlockSpec((seq, hw), lambda g, i: (g // G, vb + g % G)),
                  pl.BlockSpec((tq, tq), lambda g, i: (0, 0))],
        out_specs=[pl.BlockSpec((tq, hw), lambda g, i: ((g // G) * nq + i, g % G)),
                   pl.BlockSpec((1, 1, hps, 16, tq), lambda g, i: (g // G, i, g % G, 0, 0))],
        out_shape=[jax.ShapeDtypeStruct((B * seq, heads * dh), BF16),
                   jax.ShapeDtypeStruct((B, nq, heads, 16, tq), F32)],
        scratch_shapes=[pltpu.VMEM((tq, hw), F32), pltpu.VMEM((hps, 8, LANE), F32)],
        compiler_params=_cp("parallel", "arbitrary"))(proj, proj, proj, jnp.tril(jnp.ones((tq, tq), BF16)))


def sba_bwd(proj, ctot, do, *, B, seq, heads, q_off, k_off, v_off, name, comm=None):
    dh = SB_HEAD_DIM
    tq = _sb_tile(seq)
    nq = seq // tq
    hps = _sb_heads_per_step(heads, SB_HEADS_BWD)
    hw = hps * dh
    scale = dh ** -0.5

    def body(q_ref, k_ref, v_ref, ct_ref, do_ref, sfx_ref, pre_ref, dq_ref, dk_ref, dv_ref, dq_acc, dk_acc, dv_acc):
        qi = pl.program_id(1)

        @pl.when(qi == 0)
        def _():
            dk_acc[...] = jnp.zeros_like(dk_acc)
            dv_acc[...] = jnp.zeros_like(dv_acc)

        tri_sfx = sfx_ref[...]
        tri_pre = pre_ref[...]
        qs = [(q_ref[:, h * dh:(h + 1) * dh].astype(F32) * scale).astype(BF16) for h in range(hps)]
        dos = [do_ref[:, h * dh:(h + 1) * dh] for h in range(hps)]
        totals = [jnp.max(jnp.broadcast_to(ct_ref[0, 0, h, 0:1, :], (LANE, tq)).T, axis=1, keepdims=True)
                  for h in range(hps)]
        dq_acc[...] = jnp.zeros_like(dq_acc)

        def part(h, rows, start, n_keys, pc, pg, mask):
            cols = slice(h * dh, (h + 1) * dh)
            q_rows, do_rows = qs[h][rows], dos[h][rows]
            k_blk = k_ref[pl.ds(start, n_keys), cols]
            v_blk = v_ref[pl.ds(start, n_keys), cols]
            z = _dot(q_rows, k_blk, NT)
            sp = _softplus(z)
            sig = jnp.exp(z - sp)
            if mask is not None:
                sp = jnp.where(mask, sp, 0.0)
            pc_next = pc + jnp.sum(sp, axis=-1, keepdims=True)
            wts = jnp.exp(z - (_dot(sp.astype(BF16), tri_sfx[0:n_keys, 0:n_keys]) + (totals[h][rows] - pc_next)))
            if mask is not None:
                wts = jnp.where(mask, wts, 0.0)
            g = _dot(do_rows, v_blk, NT) * wts
            dz = g - sig * (_dot(g.astype(BF16), tri_pre[0:n_keys, 0:n_keys]) + pg)
            if mask is not None:
                dz = jnp.where(mask, dz, 0.0)
            dz = dz.astype(BF16)
            dk_acc[pl.ds(start, n_keys), cols] += _dot(dz, q_rows, TN)
            dv_acc[pl.ds(start, n_keys), cols] += _dot(wts.astype(BF16), do_rows, TN)
            return pc_next, pg + jnp.sum(g, axis=-1, keepdims=True), _dot(dz, k_blk)

        def block(start, carry):
            out = []
            for h in range(hps):
                pc, pg, dq = part(h, slice(0, tq), start, tq, carry[h][0], carry[h][1], None)
                dq_acc[:, h * dh:(h + 1) * dh] += dq
                out.append((pc, pg))
            return tuple(out)

        zero = jnp.zeros((tq, 1), F32)
        n_left = jnp.max(ct_ref[0, 0, 0, 8:16, :]).astype(jnp.int32)
        carry = lax.fori_loop(qi - n_left, qi - 1, lambda j, c: block(pl.multiple_of(j * tq, tq), c),
                              ((zero, zero),) * hps)
        below = lax.broadcasted_iota(jnp.int32, (tq, tq), 1) < lax.broadcasted_iota(jnp.int32, (tq, tq), 0)
        has_left = n_left > 0
        left = pl.multiple_of(jnp.maximum(qi - 1, 0) * tq, tq)
        for h in range(hps):
            cols = slice(h * dh, (h + 1) * dh)
            pc, pg, dq_l = part(h, slice(0, tq), left, tq, carry[h][0], carry[h][1], has_left)
            _, _, dq_d = part(h, slice(0, tq), pl.multiple_of(qi * tq, tq), tq, pc, pg, below)
            dq_ref[:, cols] = ((dq_acc[:, cols] + dq_l + dq_d) * scale).astype(BF16)

        @pl.when(qi == nq - 1)
        def _():
            dk_ref[...] = dk_acc[...].astype(BF16)
            dv_ref[...] = dv_acc[...].astype(BF16)

    qb, kb, vb = q_off // hw, k_off // hw, v_off // hw
    G = heads // hps
    q_spec = pl.BlockSpec((tq, hw), lambda g, i: ((g // G) * nq + i, qb + g % G))
    o_spec = pl.BlockSpec((tq, hw), lambda g, i: ((g // G) * nq + i, g % G))
    kv_out = pl.BlockSpec((seq, hw), lambda g, i: (g // G, g % G))
    shp = jax.ShapeDtypeStruct((B * seq, heads * dh), BF16)
    body, c_in, c_args, c_out, c_shape, c_scratch = _hosted(body, 7, 3, comm, _grid_step(nq), B * G * nq)
    tri_spec = pl.BlockSpec((tq, tq), lambda g, i: (0, 0))
    ones = jnp.ones((tq, tq), BF16)
    out = pl.pallas_call(
        body, name=name, grid=(B * G, nq),
        in_specs=[q_spec,
                  pl.BlockSpec((seq, hw), lambda g, i: (g // G, kb + g % G)),
                  pl.BlockSpec((seq, hw), lambda g, i: (g // G, vb + g % G)),
                  pl.BlockSpec((1, 1, hps, 16, tq), lambda g, i: (g // G, i, g % G, 0, 0)), o_spec,
                  tri_spec, tri_spec] + c_in,
        out_specs=[o_spec, kv_out, kv_out] + c_out, out_shape=[shp, shp, shp] + c_shape,
        scratch_shapes=[pltpu.VMEM((tq, hw), F32), pltpu.VMEM((seq, hw), F32), pltpu.VMEM((seq, hw), F32)]
        + c_scratch,
        compiler_params=_cp("arbitrary" if comm else "parallel", "arbitrary"))(
            proj, proj, proj, ctot, do, jnp.tril(ones), jnp.triu(ones), *c_args)
    return (out[0], out[1], out[2], out[3:]) if comm else out


def _head_expand(n_heads, DI):
    j = jnp.arange(LANE, dtype=jnp.int32)[:, None]
    c = jnp.arange(DI, dtype=jnp.int32)[None, :] // SSM_HEAD_DIM
    return ((j == c) & (j < n_heads)).astype(BF16)


def _split3(x):
    hi = x.astype(BF16)
    r1 = x - hi.astype(F32)
    mid = r1.astype(BF16)
    return hi, mid, (r1 - mid.astype(F32)).astype(BF16)


def dt_fwd(proj, bias, a_log, expand, *, dt_off, name):
    T = proj.shape[0]
    DI = expand.shape[1]
    L = SSM_CHUNK
    tt = _pick(T, 512, L)

    def body(raw_ref, bias_ref, al_ref, e_ref, dt_ref, cs_ref, dtx_ref, csx_ref):
        x = raw_ref[...].astype(F32) + bias_ref[...]
        dt = _softplus(x)
        dt_ref[...] = dt
        la = dt * (-jnp.exp(al_ref[...]))
        tri = _tri01(L, True)
        for c in range(tt // L):
            cs_ref[c * L:(c + 1) * L, :] = _tri_dot3(tri, la[c * L:(c + 1) * L, :])
        e = e_ref[...]
        dtx_ref[...] = _dot(dt.astype(BF16), e).astype(BF16)
        hi, mid, lo = _split3(cs_ref[...])
        csx_ref[...] = _dot(hi, e) + _dot(mid, e) + _dot(lo, e)

    row = pl.BlockSpec((tt, LANE), lambda i: (i, 0))
    wide = pl.BlockSpec((tt, DI), lambda i: (i, 0))
    vec = pl.BlockSpec((1, LANE), lambda i: (0, 0))
    return pl.pallas_call(
        body, name=name, grid=(T // tt,),
        in_specs=[pl.BlockSpec((tt, LANE), lambda i: (i, dt_off // LANE)), vec, vec,
                  pl.BlockSpec((LANE, DI), lambda i: (0, 0))],
        out_specs=[row, row, wide, wide],
        out_shape=[jax.ShapeDtypeStruct((T, LANE), F32), jax.ShapeDtypeStruct((T, LANE), F32),
                   jax.ShapeDtypeStruct((T, DI), BF16), jax.ShapeDtypeStruct((T, DI), F32)],
        compiler_params=_cp("parallel"))(proj, bias, a_log, expand)


def dt_bwd(ddt_x, dcs_x, dcs_cols, proj, dt, bias, a_log, reduce_t, *, dt_off, n_heads, name):
    T = proj.shape[0]
    DI = reduce_t.shape[0]
    L = SSM_CHUNK
    tt = _pick(T, 512, L)

    def body(ddtx_ref, dcsx_ref, dcsc_ref, raw_ref, dt_ref, bias_ref, al_ref, r_ref, draw_ref, dbias_ref, dal_ref,
             dla_buf):
        @pl.when(pl.program_id(0) == 0)
        def _():
            dbias_ref[...] = jnp.zeros_like(dbias_ref)
            dal_ref[...] = jnp.zeros_like(dal_ref)

        r = r_ref[...]
        ddt = _dot(ddtx_ref[...], r)
        dx = dcsx_ref[...]
        hi = dx.astype(BF16)
        dcs = _dot(hi, r) + _dot((dx - hi.astype(F32)).astype(BF16), r) + dcsc_ref[...]
        triu = _tri01(L, False)
        for c in range(tt // L):
            dla_buf[c * L:(c + 1) * L, :] = _tri_dot3(triu, dcs[c * L:(c + 1) * L, :])
        dla = dla_buf[...]
        a = -jnp.exp(al_ref[...])
        valid = lax.broadcasted_iota(jnp.int32, (tt, LANE), 1) < n_heads
        dal_ref[...] += jnp.sum(jnp.where(valid, dla * dt_ref[...], 0.0), axis=0, keepdims=True) * a
        x = raw_ref[...].astype(F32) + bias_ref[...]
        draw = jnp.where(valid, (ddt + dla * a) * _sigmoid(x), 0.0)
        dbias_ref[...] += jnp.sum(draw, axis=0, keepdims=True)
        draw_ref[...] = draw.astype(BF16)

    row = pl.BlockSpec((tt, LANE), lambda i: (i, 0))
    wide = pl.BlockSpec((tt, DI), lambda i: (i, 0))
    vec = pl.BlockSpec((1, LANE), lambda i: (0, 0))
    return pl.pallas_call(
        body, name=name, grid=(T // tt,),
        in_specs=[wide, wide, row, pl.BlockSpec((tt, LANE), lambda i: (i, dt_off // LANE)), row, vec, vec,
                  pl.BlockSpec((DI, LANE), lambda i: (0, 0))],
        out_specs=[row, vec, vec],
        out_shape=[jax.ShapeDtypeStruct((T, LANE), BF16), jax.ShapeDtypeStruct((1, LANE), F32),
                   jax.ShapeDtypeStruct((1, LANE), F32)],
        scratch_shapes=[pltpu.VMEM((tt, LANE), F32)],
        compiler_params=_cp("arbitrary"))(ddt_x, dcs_x, dcs_cols, proj, dt, bias, a_log, reduce_t)


def _pair_terms(x_ref, dtx_ref, csx_ref, csr_ref, pair, ppg, lo_half, causal):
    L = SSM_CHUNK
    g, pp = divmod(pair, ppg)
    ra = g * HEAD_ROWS + 2 * pp
    cols = slice(pair * LANE, (pair + 1) * LANE)
    X = x_ref[:, cols].astype(F32)
    dt_p = dtx_ref[:, cols].astype(F32)
    own = csx_ref[:, cols]
    other = pltpu.roll(own, SSM_HEAD_DIM, 1)
    csa_c = jnp.where(lo_half, own, other)
    csb_c = jnp.where(lo_half, other, own)
    La = jnp.exp(jnp.where(causal, csa_c - csr_ref[0, ra:ra + 1, :], NEG_BIG))
    Lb = jnp.exp(jnp.where(causal, csb_c - csr_ref[0, ra + 1:ra + 2, :], NEG_BIG))
    last = csx_ref[L - 1:L, cols]
    return g, ra, cols, X, dt_p, La, Lb, jnp.exp(own), jnp.exp(last - own), jnp.exp(last)


def scan_fwd(xbc, dt_x, cs_x, cs_row, d_full, *, B, seq, DI, name):
    L, N, G = SSM_CHUNK, SSM_STATE, SSM_GROUPS
    nc = seq // L
    XW = xbc.shape[1]
    n_pairs = DI // LANE
    ppg = n_pairs // G

    def body(x_ref, dtx_ref, csx_ref, csr_ref, d_ref, y_ref, st_ref, state):
        c = pl.program_id(1)

        @pl.when(c == 0)
        def _():
            state[...] = jnp.zeros_like(state)

        causal = lax.broadcasted_iota(jnp.int32, (L, L), 0) >= lax.broadcasted_iota(jnp.int32, (L, L), 1)
        lo_half = lax.broadcasted_iota(jnp.int32, (L, LANE), 1) < SSM_HEAD_DIM
        cbs = []
        for g in range(G):
            Bc = x_ref[:, DI + g * N:DI + (g + 1) * N]
            Cc = x_ref[:, DI + G * N + g * N:DI + G * N + (g + 1) * N]
            cbs.append((Bc, Cc, _dot(Cc, Bc, NT)))
        for pair in range(n_pairs):
            g, _, cols, X, dt_p, La, Lb, ecs, tail, e_last = _pair_terms(
                x_ref, dtx_ref, csx_ref, csr_ref, pair, ppg, lo_half, causal)
            Bc, Cc, CB = cbs[g]
            xs = X * dt_p
            xsb = xs.astype(BF16)
            y = jnp.where(lo_half, _dot((CB * La).astype(BF16), xsb), _dot((CB * Lb).astype(BF16), xsb))
            ST = state[pair]
            st_ref[0, 0, pair] = ST
            y = y + ecs * _dot(Cc, ST.astype(BF16)) + d_ref[:, cols] * X
            y_ref[:, cols] = y.astype(BF16)
            state[pair] = e_last * ST + _dot(Bc, (xs * tail).astype(BF16), TN)

    wide = pl.BlockSpec((L, DI), lambda b, c: (b * nc + c, 0))
    return pl.pallas_call(
        body, name=name, grid=(B, nc),
        in_specs=[pl.BlockSpec((L, XW), lambda b, c: (b * nc + c, 0)), wide, wide,
                  pl.BlockSpec((1, G * HEAD_ROWS, L), lambda b, c: (b, 0, c)),
                  pl.BlockSpec((1, DI), lambda b, c: (0, 0))],
        out_specs=[wide, pl.BlockSpec((1, 1, n_pairs, N, LANE), lambda b, c: (b, c, 0, 0, 0))],
        out_shape=[jax.ShapeDtypeStruct((B * seq, DI), BF16),
                   jax.ShapeDtypeStruct((B, nc, n_pairs, N, LANE), F32)],
        scratch_shapes=[pltpu.VMEM((n_pairs, N, LANE), F32)],
        compiler_params=_cp("parallel", "arbitrary"))(xbc, dt_x, cs_x, cs_row, d_full)


def scan_bwd(xbc, dt_x, cs_x, cs_row, d_full, states, dy, *, B, seq, DI, name):
    L, N, G = SSM_CHUNK, SSM_STATE, SSM_GROUPS
    nc = seq // L
    XW = xbc.shape[1]
    n_pairs = DI // LANE
    ppg = n_pairs // G
    HR = G * HEAD_ROWS
    inv_p = 1.0 / SSM_HEAD_DIM

    def body(x_ref, dtx_ref, csx_ref, csr_ref, d_ref, st_ref, dy_ref, dx_ref, ddtx_ref, dcsx_ref, dcsr_ref, dd_ref,
             dH):
        c = pl.program_id(1)

        @pl.when(c == 0)
        def _():
            dH[...] = jnp.zeros_like(dH)
            dd_ref[...] = jnp.zeros_like(dd_ref)

        causal = lax.broadcasted_iota(jnp.int32, (L, L), 0) >= lax.broadcasted_iota(jnp.int32, (L, L), 1)
        lo_half = lax.broadcasted_iota(jnp.int32, (L, LANE), 1) < SSM_HEAD_DIM
        last_row = lax.broadcasted_iota(jnp.int32, (L, LANE), 0) == L - 1
        head_row = lax.broadcasted_iota(jnp.int32, (HR, 1), 0)
        dcs_rows = jnp.zeros((HR, L), F32)

        for g in range(G):
            Bc = x_ref[:, DI + g * N:DI + (g + 1) * N]
            Cc = x_ref[:, DI + G * N + g * N:DI + G * N + (g + 1) * N]
            CB = _dot(Cc, Bc, NT)
            dCB = jnp.zeros((L, L), F32)
            dC = jnp.zeros((L, N), F32)
            dB = jnp.zeros((L, N), F32)
            for pp in range(ppg):
                pair = g * ppg + pp
                _, ra, cols, X, dt_p, La, Lb, ecs, tail, e_last = _pair_terms(
                    x_ref, dtx_ref, csx_ref, csr_ref, pair, ppg, lo_half, causal)
                xs = X * dt_p
                xsb = xs.astype(BF16)
                Ma, Mb = CB * La, CB * Lb
                dY = dy_ref[:, cols].astype(F32)
                dYb = dY.astype(BF16)
                dMa = _dot(jnp.where(lo_half, dY, 0.0).astype(BF16), xsb, NT)
                dMb = _dot(jnp.where(lo_half, 0.0, dY).astype(BF16), xsb, NT)
                dSa, dSb = dMa * Ma, dMb * Mb
                dCB = dCB + dMa * La + dMb * Lb
                dcs = jnp.where(lo_half, jnp.sum(dSa, axis=1, keepdims=True), jnp.sum(dSb, axis=1, keepdims=True)) * inv_p
                dcs_rows = dcs_rows - jnp.where(head_row == ra, jnp.sum(dSa, axis=0, keepdims=True), 0.0)
                dcs_rows = dcs_rows - jnp.where(head_row == ra + 1, jnp.sum(dSb, axis=0, keepdims=True), 0.0)
                dxs = jnp.where(lo_half, _dot(Ma.astype(BF16), dYb, TN), _dot(Mb.astype(BF16), dYb, TN))
                ST = st_ref[0, 0, pair]
                STb = ST.astype(BF16)
                dYe = (dY * ecs).astype(BF16)
                dC = dC + _dot(dYe, STb, NT)
                dSTp = _dot(Cc, dYe, TN)
                dcs = dcs + dY * (ecs * _dot(Cc, STb))
                dSTn = dH[pair]
                dSTnb = dSTn.astype(BF16)
                dSTp = dSTp + e_last * dSTn
                XBt = _dot(Bc, dSTnb)
                dxs = dxs + tail * XBt
                t2 = xs * XBt * tail
                at_end = e_last * jnp.sum(dSTn * ST, axis=0, keepdims=True) + jnp.sum(t2, axis=0, keepdims=True)
                dcs = dcs - t2 + jnp.where(last_row, at_end, 0.0)
                dB = dB + _dot((xs * tail).astype(BF16), dSTnb, NT)
                dx_ref[:, cols] = (dxs * dt_p + d_ref[:, cols] * dY).astype(BF16)
                ddtx_ref[:, cols] = (dxs * X).astype(BF16)
                dcsx_ref[:, cols] = dcs
                dd_ref[0, :, cols] += jnp.sum(dY * X, axis=0, keepdims=True)
                dH[pair] = dSTp
            dCBb = dCB.astype(BF16)
            dx_ref[:, DI + g * N:DI + (g + 1) * N] = (dB + _dot(dCBb, Cc, TN)).astype(BF16)
            dx_ref[:, DI + G * N + g * N:DI + G * N + (g + 1) * N] = (dC + _dot(dCBb, Bc)).astype(BF16)
        dcsr_ref[0] = dcs_rows

    rev = lambda b, c: (b * nc + (nc - 1 - c), 0)
    wide = pl.BlockSpec((L, DI), rev)
    hrow = pl.BlockSpec((1, HR, L), lambda b, c: (b, 0, nc - 1 - c))
    return pl.pallas_call(
        body, name=name, grid=(B, nc),
        in_specs=[pl.BlockSpec((L, XW), rev), wide, wide, hrow,
                  pl.BlockSpec((1, DI), lambda b, c: (0, 0)),
                  pl.BlockSpec((1, 1, n_pairs, N, LANE), lambda b, c: (b, nc - 1 - c, 0, 0, 0)), wide],
        out_specs=[pl.BlockSpec((L, XW), rev), wide, wide, hrow, pl.BlockSpec((1, 1, DI), lambda b, c: (b, 0, 0))],
        out_shape=[jax.ShapeDtypeStruct((B * seq, XW), BF16), jax.ShapeDtypeStruct((B * seq, DI), BF16),
                   jax.ShapeDtypeStruct((B * seq, DI), F32), jax.ShapeDtypeStruct((B, HR, seq), F32),
                   jax.ShapeDtypeStruct((B, 1, DI), F32)],
        scratch_shapes=[pltpu.VMEM((n_pairs, N, LANE), F32)],
        compiler_params=_cp("parallel", "arbitrary"))(xbc, dt_x, cs_x, cs_row, d_full, states, dy)


def gnorm_fwd(y, proj, w, *, DI, name):
    T = y.shape[0]
    tt = _pick(T, 256, 16)
    gw = DI // SSM_GROUPS

    def body(y_ref, z_ref, w_ref, o_ref):
        for g in range(SSM_GROUPS):
            sl = slice(g * gw, (g + 1) * gw)
            y2 = y_ref[:, sl].astype(F32) * _silu(z_ref[:, sl].astype(F32))
            r = lax.rsqrt(jnp.mean(y2 * y2, axis=-1, keepdims=True) + EPS)
            o_ref[:, sl] = (y2 * r * w_ref[:, sl]).astype(BF16)

    row = pl.BlockSpec((tt, DI), lambda i: (i, 0))
    return pl.pallas_call(
        body, name=name, grid=(T // tt,),
        in_specs=[row, row, pl.BlockSpec((1, DI), lambda i: (0, 0))], out_specs=row,
        out_shape=jax.ShapeDtypeStruct((T, DI), BF16), compiler_params=_cp("parallel"))(y, proj, w)


def gnorm_bwd(dyn, y, proj, w, *, DI, name):
    T = y.shape[0]
    tt = _pick(T, 256, 16)
    gw = DI // SSM_GROUPS

    def body(dyn_ref, y_ref, z_ref, w_ref, dy_ref, dz_ref, dw_ref):
        @pl.when(pl.program_id(0) == 0)
        def _():
            dw_ref[...] = jnp.zeros_like(dw_ref)

        for g in range(SSM_GROUPS):
            sl = slice(g * gw, (g + 1) * gw)
            yv = y_ref[:, sl].astype(F32)
            z = z_ref[:, sl].astype(F32)
            sz = _silu(z)
            y2 = yv * sz
            r = lax.rsqrt(jnp.mean(y2 * y2, axis=-1, keepdims=True) + EPS)
            xhat = y2 * r
            d = dyn_ref[:, sl].astype(F32)
            dw_ref[:, sl] += jnp.sum(d * xhat, axis=0, keepdims=True)
            dxh = d * w_ref[:, sl]
            dy2 = r * (dxh - xhat * jnp.mean(dxh * xhat, axis=-1, keepdims=True))
            dy_ref[:, sl] = (dy2 * sz).astype(BF16)
            dz_ref[:, sl] = (dy2 * yv * _silu_grad(z)).astype(BF16)

    row = pl.BlockSpec((tt, DI), lambda i: (i, 0))
    vec = pl.BlockSpec((1, DI), lambda i: (0, 0))
    shp = jax.ShapeDtypeStruct((T, DI), BF16)
    return pl.pallas_call(
        body, name=name, grid=(T // tt,), in_specs=[row, row, row, vec], out_specs=[row, row, vec],
        out_shape=[shp, shp, jax.ShapeDtypeStruct((1, DI), F32)],
        compiler_params=_cp("arbitrary"))(dyn, y, proj, w)


N_CHIP = 4
SHARD_ROW_ALIGN = 128


def _comm_out_shapes(srcs, modes):
    return [jax.ShapeDtypeStruct(((N_DEV,) if mode in ("gather", "gather_direct") else ()) + s.shape, s.dtype)
            for s, mode in zip(srcs, modes)]


def _comm_scratch(n):
    return [pltpu.SemaphoreType.DMA((n, N_DEV - 1)), pltpu.SemaphoreType.DMA((n, N_DEV - 1)),
            pltpu.SemaphoreType.DMA((n,))]


def _comm_phases(modes, src_refs, out_refs, send_sems, recv_sems, local_sems):
    x, y, c = lax.axis_index("x"), lax.axis_index("y"), lax.axis_index("c")
    me, sibling = (x, y, c), (x, y, 1 - c)
    chips = [(1 - x, y), (x, 1 - y), (1 - x, 1 - y)]
    relays = [a for a, mode in enumerate(modes) if mode == "gather"]

    def slot(p):
        return 4 * p[0] + 2 * p[1] + p[2]

    def remote(a, k, src, dst, to):
        return pltpu.make_async_remote_copy(src_ref=src, dst_ref=dst, send_sem=send_sems.at[a, k],
                                            recv_sem=recv_sems.at[a, k], device_id=to,
                                            device_id_type=pl.DeviceIdType.MESH)

    def first_copies():
        local, two_way, send_only = [], [], []
        for a, mode in enumerate(modes):
            src, out = src_refs[a], out_refs[a]
            if mode == "sibling":
                two_way.append(remote(a, 0, src, out, sibling))
            elif mode == "chips":
                mine = 2 * x + y
                local.append(pltpu.make_async_copy(src.at[mine], out.at[mine], local_sems.at[a]))
                for j, chip in enumerate(chips):
                    two_way.append(remote(a, 1 + j, src.at[2 * chip[0] + chip[1]], out.at[mine], (*chip, c)))
            elif mode == "gather_direct":
                local.append(pltpu.make_async_copy(src, out.at[slot(me)], local_sems.at[a]))
                for k in range(1, N_DEV):
                    peer = (1 - x if k & 4 else x, 1 - y if k & 2 else y, 1 - c if k & 1 else c)
                    two_way.append(remote(a, k - 1, src, out.at[slot(me)], peer))
            else:
                assert mode == "gather"
                local.append(pltpu.make_async_copy(src, out.at[slot(me)], local_sems.at[a]))
                send_only.append(remote(a, 0, src, out.at[slot(me)], sibling))
                for j, chip in enumerate(chips):
                    send_only.append(remote(a, 1 + j, src, out.at[slot(me)], (*chip, c)))
        return local, two_way, send_only

    def forwards():
        out = []
        for a in relays:
            for j, chip in enumerate(chips):
                landed = out_refs[a].at[slot((*chip, c))]
                out.append((remote(a, 1 + j, landed, landed, me), remote(a, 4 + j, landed, landed, sibling)))
        return out

    def start():
        local, two_way, send_only = first_copies()
        for cp in local + two_way + send_only:
            cp.start()

    def relay():
        for arrival, fwd in forwards():
            arrival.wait_recv()
            fwd.start()

    def finish():
        local, two_way, send_only = first_copies()
        for a in relays:
            blk = out_refs[a].at[slot(sibling)]
            remote(a, 0, blk, blk, me).wait_recv()
            for j, chip in enumerate(chips):
                blk = out_refs[a].at[slot((*chip, 1 - c))]
                remote(a, 4 + j, blk, blk, me).wait_recv()
        for cp in send_only + [fwd for _, fwd in forwards()]:
            cp.wait_send()
        for cp in two_way + local:
            cp.wait()

    return start, relay, finish, bool(relays)


def _hosted(body, n_in, n_out, comm, step, n_steps):
    if comm is None:
        return body, [], [], [], [], []
    srcs, modes = comm
    nc = len(srcs)

    def wrapped(*refs):
        ins, csrc = refs[:n_in], refs[n_in:n_in + nc]
        outs = refs[n_in + nc:n_in + nc + n_out]
        cout = refs[n_in + nc + n_out:n_in + 2 * nc + n_out]
        scratch = refs[n_in + 2 * nc + n_out:len(refs) - 3]
        start, relay, finish, has_relay = _comm_phases(modes, csrc, cout, *refs[len(refs) - 3:])
        s = step()
        pl.when(s == 0)(start)
        body(*ins, *outs, *scratch)
        if has_relay:
            pl.when(s == (2 * n_steps) // 3)(relay)
        pl.when(s == n_steps - 1)(finish)

    any_spec = pl.BlockSpec(memory_space=pl.ANY)
    return wrapped, [any_spec] * nc, list(srcs), [any_spec] * nc, _comm_out_shapes(srcs, modes), _comm_scratch(nc)


def exchange(srcs, modes, *, name):
    n = len(srcs)

    def body(*refs):
        start, relay, finish, has_relay = _comm_phases(modes, refs[:n], refs[n:2 * n], *refs[2 * n:])
        start()
        if has_relay:
            relay()
        finish()

    any_spec = pl.BlockSpec(memory_space=pl.ANY)
    return pl.pallas_call(
        body, name=name, in_specs=[any_spec] * n, out_specs=[any_spec] * n, out_shape=_comm_out_shapes(srcs, modes),
        scratch_shapes=_comm_scratch(n), compiler_params=pltpu.CompilerParams(has_side_effects=True))(*srcs)


def pair_sum(a, b, *, name):
    n, R, C = a.shape
    tr = _pick(n * R, 1024, 16)

    def body(a_ref, b_ref, o_ref):
        o_ref[...] = (a_ref[...].astype(F32) + b_ref[...].astype(F32)).astype(BF16)

    blk = pl.BlockSpec((tr, C), lambda i: (i, 0))
    out = pl.pallas_call(
        body, name=name, grid=(n * R // tr,), in_specs=[blk, blk], out_specs=blk,
        out_shape=jax.ShapeDtypeStruct((n * R, C), BF16),
        compiler_params=_cp("parallel"))(a.reshape(n * R, C), b.reshape(n * R, C))
    return out.reshape(n, R, C)


def sum_slots(recv, *, name):
    _, R, C = recv.shape
    tr = _pick(R, 512, 8)

    def body(r_ref, o_ref):
        acc = r_ref[0].astype(F32)
        for p in range(1, N_DEV):
            acc = acc + r_ref[p].astype(F32)
        o_ref[...] = acc

    return pl.pallas_call(
        body, name=name, grid=(R // tr,),
        in_specs=[pl.BlockSpec((N_DEV, tr, C), lambda i: (0, i, 0))],
        out_specs=pl.BlockSpec((tr, C), lambda i: (i, 0)),
        out_shape=jax.ShapeDtypeStruct((R, C), F32), compiler_params=_cp("parallel"))(recv)


def adamw(gsrc, w, m, v, *, name):
    slots, R, C = gsrc.shape
    tr = _pick(R, 256, 16 if gsrc.dtype == BF16 else 8)
    c1 = 1.0 / (1.0 - ADAM_B1 ** ADAM_STEP)
    c2 = 1.0 / (1.0 - ADAM_B2 ** ADAM_STEP)

    def body(g_ref, w_ref, m_ref, v_ref, go_ref, d_ref, mo_ref, vo_ref):
        g = g_ref[0].astype(F32)
        for p in range(1, slots):
            g = g + g_ref[p].astype(F32)
        m2 = ADAM_B1 * m_ref[...] + (1.0 - ADAM_B1) * g
        v2 = ADAM_B2 * v_ref[...] + (1.0 - ADAM_B2) * (g * g)
        go_ref[...] = g
        mo_ref[...] = m2
        vo_ref[...] = v2
        d_ref[...] = -ADAM_LR * ((m2 * c1) / (jnp.sqrt(v2 * c2) + ADAM_EPS) + ADAM_WD * w_ref[...])

    blk = pl.BlockSpec((tr, C), lambda i: (i, 0))
    shp = jax.ShapeDtypeStruct((R, C), F32)
    return pl.pallas_call(
        body, name=name, grid=(R // tr,),
        in_specs=[pl.BlockSpec((slots, tr, C), lambda i: (0, i, 0)), blk, blk, blk],
        out_specs=[blk] * 4, out_shape=[shp] * 4, compiler_params=_cp("parallel"))(gsrc, w, m, v)


def _pad_cols(a, n):
    return jnp.pad(a, ((0, 0), (0, n - a.shape[1])))


def _to_rows(a, B, seq, H):
    G = SSM_GROUPS
    R = H // G
    t = a[:, :H].reshape(B, seq, G, R).transpose(0, 2, 3, 1)
    t = jnp.pad(t, ((0, 0), (0, 0), (0, HEAD_ROWS - R), (0, 0)))
    return t.reshape(B, G * HEAD_ROWS, seq)


def _from_rows(a, B, seq, H):
    G = SSM_GROUPS
    R = H // G
    t = a.reshape(B, G, HEAD_ROWS, seq)[:, :, :R].transpose(0, 3, 1, 2).reshape(B * seq, H)
    return _pad_cols(t, LANE)


def _chip_split(grads):
    c_idx = lax.axis_index("c")
    keep, give = [], []
    for g in grads:
        by_chip = g.reshape((N_CHIP, 2) + g.shape[1:])
        keep.append(lax.dynamic_index_in_dim(by_chip, c_idx, axis=1, keepdims=False))
        give.append(lax.dynamic_index_in_dim(by_chip, 1 - c_idx, axis=1, keepdims=False))
    return keep, give


def _chip_sums(grads, name):
    keep, give = _chip_split(grads)
    swapped = exchange(give, ["sibling"] * len(give), name="swap_" + name)
    return [pair_sum(k, s, name=f"chip_sum_{name}_{i}") for i, (k, s) in enumerate(zip(keep, swapped))]


def local_step(x, target, loc, od_w_in_t, *, B, seq):
    T, D = x.shape
    CW = D
    heads = CW // SB_HEAD_DIM
    DI = 2 * D
    H = DI // SSM_HEAD_DIM
    XW = DI + 2 * SSM_GROUPS * SSM_STATE
    in_odd = DI + XW + H
    w1_rows = in_odd // N_DEV
    q_off, k_off, v_off, gc_off, ga_off = 3 * CW, 4 * CW, 5 * CW, 2 * CW, 6 * CW
    dt_off = DI + XW

    small_packed, small_spans = _pack_rows([loc[n] for n in SMALL_SHARDED], LANE, 8)
    n0, (g_ev_in, small_all) = rmsnorm_fwd(x, loc["ev_norm_w"], name="l0_norm",
                                           comm=([loc["ev_w_in"].astype(BF16), small_packed],
                                                 ["gather", "gather_direct"]))
    p = {n: loc[n] for n in SMALL}
    for n, a in zip(SMALL_SHARDED, _unpack_rows(small_all, small_spans)):
        p[n] = _col_unshards(a)
    p["ev_w_in"] = _col_unshards(g_ev_in)
    proj0, (g_od_in_t,) = mm_nn(n0, p["ev_w_in"], out_dtype=BF16, name="l0_in_proj",
                                comm=([od_w_in_t.astype(BF16)], ["gather"]))
    w1t = g_od_in_t[:, :w1_rows].reshape(in_odd, D)
    w1t = jnp.pad(w1t, ((0, -(-(in_odd + LANE) // 256) * 256 - in_odd), (0, 0)))
    (u2,), (g_ev_out, g_od_out) = dwconv_fwd(
        proj0, (0, CW), p["ev_dw_w"], p["ev_dw_b"], C=CW, seq=seq, glu=True, silu_out=False, name="l0_conv",
        comm=([loc["ev_w_out"].astype(BF16), loc["od_w_out"].astype(BF16)], ["gather"] * 2))
    p["ev_w_out"] = g_ev_out.reshape(-1, D)
    od_w_out = g_od_out.reshape(-1, D)
    o, ctot = sba_fwd(proj0, B=B, seq=seq, heads=heads, q_off=q_off, k_off=k_off, v_off=v_off, name="l0_attn")
    ycat = mix0_post_fwd(u2, proj0, o, p["ev_ln_w"], p["ev_ln_b"], CW=CW, gc_off=gc_off, ga_off=ga_off,
                         name="l0_post")
    h1 = mm_nn(ycat, p["ev_w_out"], add=x, out_dtype=F32, name="l0_out_proj")

    n1 = rmsnorm_fwd(h1, p["od_norm_w"], name="l1_norm")
    proj1 = mm_nt_terms([(n1, 0, D, 0)], w1t, out_dtype=BF16, name="l1_in_proj")
    u_pre, xbc = dwconv_fwd(proj1, (DI,), p["od_conv_w"], p["od_conv_b"], C=XW, seq=seq, glu=False, silu_out=True,
                            name="l1_conv")
    bias_p, alog_p = _pad_cols(p["od_dt_bias"], LANE), _pad_cols(p["od_a_log"], LANE)
    expand = _head_expand(H, DI)
    dt, cs, dt_x, cs_x = dt_fwd(proj1, bias_p, alog_p, expand, dt_off=dt_off, name="l1_dt")
    cs_row = _to_rows(cs, B, seq, H)
    d_full = jnp.repeat(p["od_d"], SSM_HEAD_DIM, axis=1)
    y_ssd, states = scan_fwd(xbc, dt_x, cs_x, cs_row, d_full, B=B, seq=seq, DI=DI, name="l1_ssd")
    yn = gnorm_fwd(y_ssd, proj1, p["od_gnorm_w"], DI=DI, name="l1_gnorm")
    h2 = mm_nn(yn, od_w_out, add=h1, out_dtype=F32, name="l1_out_proj")

    loss, dh2, dh2b, g_final = final_loss(h2, p["final_norm_w"], target, name="loss_head")

    g_od_w_out = mm_tn(yn, dh2b, out_dtype=BF16, name="l1_dw_out")
    dyn = mm_nt_terms([(dh2b, 0, D, 0)], od_w_out, out_dtype=BF16, name="l1_d_out_proj")
    dy_ssd, dz, g_gnorm = gnorm_bwd(dyn, y_ssd, proj1, p["od_gnorm_w"], DI=DI, name="l1_gnorm_bwd")
    dxbc_c, ddt_x, dcs_x, dcs_row, dd_part = scan_bwd(xbc, dt_x, cs_x, cs_row, d_full, states, dy_ssd, B=B, seq=seq,
                                                      DI=DI, name="l1_ssd_bwd")
    g_d = dd_part.sum(axis=(0, 1)).reshape(H, SSM_HEAD_DIM).sum(axis=1)[None, :]
    draw, g_bias, g_alog = dt_bwd(ddt_x, dcs_x, _from_rows(dcs_row, B, seq, H), proj1, dt, bias_p, alog_p, expand.T,
                                  dt_off=dt_off, n_heads=H, name="l1_dt_bwd")
    dxbc, g_conv_w, g_conv_b = dwconv_bwd(dxbc_c, u_pre, proj1, (DI,), p["od_conv_w"], C=XW, seq=seq, glu=False,
                                          silu_out=True, name="l1_conv_bwd")
    tw = 512 if DI % 512 == 0 else LANE
    terms = [(dz, j, tw, j * tw) for j in range(DI // tw)]
    terms += [(dxbc, j, tw, DI + j * tw) for j in range(XW // tw)]
    terms += [(draw, 0, LANE, dt_off)]
    dn1 = mm_nn_terms(terms, w1t, out_dtype=BF16, name="l1_d_in_proj")
    g_od_w_in_t = jnp.concatenate([mm_tn(dz, n1, out_dtype=BF16, name="l1_dw_in_z"),
                                   mm_tn(dxbc, n1, out_dtype=BF16, name="l1_dw_in_xbc"),
                                   mm_tn(draw, n1, out_dtype=BF16, name="l1_dw_in_dt")], axis=0)[:in_odd]
    w1_pad = (-w1_rows) % SHARD_ROW_ALIGN
    keep, give = _chip_split([jnp.pad(g_od_w_in_t.reshape(N_DEV, w1_rows, D), ((0, 0), (0, w1_pad), (0, 0))),
                              g_od_w_out.reshape(N_DEV, -1, D)])
    dh1, dh1b, g_od_norm, swapped = rmsnorm_bwd(h1, p["od_norm_w"], dn1, dh2, name="l1_norm_bwd",
                                                comm=(give, ["sibling"] * 2))
    l1_chip = [pair_sum(k, s, name=f"chip_sum_l1_{i}") for i, (k, s) in enumerate(zip(keep, swapped))]

    g_ev_w_out = mm_tn(ycat, dh1b, out_dtype=BF16, name="l0_dw_out")
    dycat = mm_nt_terms([(dh1b, 0, D, 0)], p["ev_w_out"], out_dtype=BF16, name="l0_d_out_proj")
    du2, dgc, dga, do, g_ln_w, g_ln_b = mix0_post_bwd(dycat, u2, proj0, o, p["ev_ln_w"], p["ev_ln_b"], CW=CW,
                                                      gc_off=gc_off, ga_off=ga_off, name="l0_post_bwd")
    dq, dk, dv, (r_od_in_t, r_od_out) = sba_bwd(proj0, ctot, do, B=B, seq=seq, heads=heads, q_off=q_off, k_off=k_off,
                                                v_off=v_off, name="l0_attn_bwd", comm=(l1_chip, ["chips", "chips"]))
    dga_a, dga_b, g_dw_w, g_dw_b = dwconv_bwd(du2, None, proj0, (0, CW), p["ev_dw_w"], C=CW, seq=seq, glu=True,
                                              silu_out=False, name="l0_conv_bwd")
    pieces = [dga_a, dga_b, dgc, dq, dk, dv, dga]
    g_ev_w_in = jnp.concatenate([mm_tn(n0, pc, out_dtype=BF16, name=f"l0_dw_in_{j}") for j, pc in enumerate(pieces)],
                                axis=1)
    l0_chip = _chip_sums([_col_shards(g_ev_w_in), g_ev_w_out.reshape(N_DEV, -1, D)], "l0")
    dn0, (r_ev_in, r_ev_out) = mm_nt_terms([(pc, 0, CW, j * CW) for j, pc in enumerate(pieces)], p["ev_w_in"],
                                           out_dtype=BF16, name="l0_d_in_proj", comm=(l0_chip, ["chips", "chips"]))
    dx, _, g_ev_norm = rmsnorm_bwd(x, p["ev_norm_w"], dn0, dh1, name="l0_norm_bwd")

    small = dict(ev_norm_w=g_ev_norm, ev_dw_w=g_dw_w, ev_dw_b=g_dw_b, ev_ln_w=g_ln_w, ev_ln_b=g_ln_b,
                 od_norm_w=g_od_norm, od_conv_w=g_conv_w, od_conv_b=g_conv_b, od_dt_bias=g_bias[:, :H],
                 od_a_log=g_alog[:, :H], od_d=g_d, od_gnorm_w=g_gnorm, final_norm_w=g_final)
    received = dict(ev_w_in=r_ev_in, ev_w_out=r_ev_out, od_w_in=r_od_in_t, od_w_out=r_od_out)
    return loss, dx, small, received


BIG = ("ev_w_in", "ev_w_out", "od_w_in", "od_w_out")
SMALL = ("ev_norm_w", "ev_dw_w", "ev_dw_b", "ev_ln_w", "ev_ln_b", "od_norm_w", "od_conv_w", "od_conv_b",
         "od_dt_bias", "od_a_log", "od_d", "od_gnorm_w", "final_norm_w")
SMALL_SHARDED = ("ev_dw_w", "od_norm_w", "od_conv_w", "od_conv_b", "od_gnorm_w")
ORDER = ("ev_norm_w", "ev_w_in", "ev_dw_w", "ev_dw_b", "ev_ln_w", "ev_ln_b", "ev_w_out", "od_norm_w", "od_w_in",
         "od_conv_w", "od_conv_b", "od_dt_bias", "od_a_log", "od_d", "od_gnorm_w", "od_w_out", "final_norm_w")


def _pack_rows(arrs, width, row_align):
    parts, spans, r0 = [], [], 0
    for a in arrs:
        flat = a.reshape(-1)
        rows = -(-flat.shape[0] // (width * row_align)) * row_align
        parts.append(jnp.pad(flat, (0, rows * width - flat.shape[0])).reshape(rows, width))
        spans.append((r0, a.size, a.shape))
        r0 += rows
    return jnp.concatenate(parts, axis=0), spans


def _unpack_rows(packed, spans):
    lead = packed.shape[:-2]
    width = packed.shape[-1]
    out = []
    for r0, size, shape in spans:
        rows = -(-size // width)
        blk = packed[..., r0:r0 + rows, :].reshape(lead + (rows * width,))[..., :size]
        out.append(blk.reshape(lead + tuple(shape)))
    return out


def _col_shards(a):
    R, C8 = a.shape
    return a.reshape(R, N_DEV, C8 // N_DEV).transpose(1, 0, 2)


def _col_unshards(a):
    n, R, C = a.shape
    return a.transpose(1, 0, 2).reshape(R, n * C)


def kernel(x, ev_norm_w, ev_w_in, ev_dw_w, ev_dw_b, ev_ln_w, ev_ln_b, ev_w_out, od_norm_w, od_w_in, od_conv_w, od_conv_b, od_dt_bias, od_a_log, od_d, od_gnorm_w, od_w_out, final_norm_w, loss_target, m_ev_norm_w, m_ev_w_in, m_ev_dw_w, m_ev_dw_b, m_ev_ln_w, m_ev_ln_b, m_ev_w_out, m_od_norm_w, m_od_w_in, m_od_conv_w, m_od_conv_b, m_od_dt_bias, m_od_a_log, m_od_d, m_od_gnorm_w, m_od_w_out, m_final_norm_w, v_ev_norm_w, v_ev_w_in, v_ev_dw_w, v_ev_dw_b, v_ev_ln_w, v_ev_ln_b, v_ev_w_out, v_od_norm_w, v_od_w_in, v_od_conv_w, v_od_conv_b, v_od_dt_bias, v_od_a_log, v_od_d, v_od_gnorm_w, v_od_w_out, v_final_norm_w):
    loc = dict(ev_norm_w=ev_norm_w, ev_w_in=ev_w_in, ev_dw_w=ev_dw_w, ev_dw_b=ev_dw_b, ev_ln_w=ev_ln_w,
               ev_ln_b=ev_ln_b, ev_w_out=ev_w_out, od_norm_w=od_norm_w, od_w_in=od_w_in, od_conv_w=od_conv_w,
               od_conv_b=od_conv_b, od_dt_bias=od_dt_bias, od_a_log=od_a_log, od_d=od_d, od_gnorm_w=od_gnorm_w,
               od_w_out=od_w_out, final_norm_w=final_norm_w)
    mom = dict(ev_norm_w=m_ev_norm_w, ev_w_in=m_ev_w_in, ev_dw_w=m_ev_dw_w, ev_dw_b=m_ev_dw_b, ev_ln_w=m_ev_ln_w,
               ev_ln_b=m_ev_ln_b, ev_w_out=m_ev_w_out, od_norm_w=m_od_norm_w, od_w_in=m_od_w_in,
               od_conv_w=m_od_conv_w, od_conv_b=m_od_conv_b, od_dt_bias=m_od_dt_bias, od_a_log=m_od_a_log,
               od_d=m_od_d, od_gnorm_w=m_od_gnorm_w, od_w_out=m_od_w_out, final_norm_w=m_final_norm_w)
    var = dict(ev_norm_w=v_ev_norm_w, ev_w_in=v_ev_w_in, ev_dw_w=v_ev_dw_w, ev_dw_b=v_ev_dw_b, ev_ln_w=v_ev_ln_w,
               ev_ln_b=v_ev_ln_b, ev_w_out=v_ev_w_out, od_norm_w=v_od_norm_w, od_w_in=v_od_w_in,
               od_conv_w=v_od_conv_w, od_conv_b=v_od_conv_b, od_dt_bias=v_od_dt_bias, od_a_log=v_od_a_log,
               od_d=v_od_d, od_gnorm_w=v_od_gnorm_w, od_w_out=v_od_w_out, final_norm_w=v_final_norm_w)
    shapes = {n: loc[n].shape for n in ORDER}
    loc = {n: (a.reshape(1, -1) if a.ndim == 1 else a.reshape(a.shape[-2:]) if a.ndim == 3 else a)
           for n, a in loc.items()}
    mom = {n: a.reshape(loc[n].shape) for n, a in mom.items()}
    var = {n: a.reshape(loc[n].shape) for n, a in var.items()}

    B, seq, D = x.shape
    me = 4 * lax.axis_index("x") + 2 * lax.axis_index("y") + lax.axis_index("c")

    w1_rows = loc["od_w_in"].shape[1]
    w1_pad = (-w1_rows) % SHARD_ROW_ALIGN

    def to_t(a):
        return jnp.pad(a.T, ((0, w1_pad), (0, 0)))

    loss, dx, grads, received = local_step(x.reshape(B * seq, D), loss_target.reshape(B * seq, D), loc,
                                           to_t(loc["od_w_in"]), B=B, seq=seq)

    gsmall_packed, gsmall_spans = _pack_rows([grads[n] for n in SMALL] + [loss], LANE, 8)
    (gsmall_recv,) = exchange([gsmall_packed], ["gather_direct"], name="gather_small_grads")

    big_out = [{} for _ in range(4)]
    for n in ("ev_w_in", "ev_w_out", "od_w_out"):
        for kind, a in enumerate(adamw(received[n], loc[n], mom[n], var[n], name="adamw_" + n)):
            big_out[kind][n] = a
    for kind, a in enumerate(adamw(received["od_w_in"], to_t(loc["od_w_in"]), to_t(mom["od_w_in"]),
                                   to_t(var["od_w_in"]), name="adamw_od_w_in")):
        big_out[kind]["od_w_in"] = a[:w1_rows].T

    summed = _unpack_rows(sum_slots(gsmall_recv, name="sum_small_grads"), gsmall_spans)
    loss_total = summed[-1][0, 0]
    gsmall = dict(zip(SMALL, summed[:-1]))
    for n in SMALL_SHARDED:
        width = loc[n].shape[1]
        gsmall[n] = lax.dynamic_slice_in_dim(gsmall[n], me * width, width, axis=1)
    gs, sspans = _pack_rows([gsmall[n] for n in SMALL], LANE, 8)
    ws, _ = _pack_rows([loc[n] for n in SMALL], LANE, 8)
    ms, _ = _pack_rows([mom[n] for n in SMALL], LANE, 8)
    vs, _ = _pack_rows([var[n] for n in SMALL], LANE, 8)
    small_out = [dict(zip(SMALL, _unpack_rows(a, sspans))) for a in adamw(gs[None], ws, ms, vs, name="adamw_small")]

    outs = [loss_total, dx.reshape(B, seq, D)]
    for kind in range(4):
        for n in ORDER:
            src = big_out[kind] if n in BIG else small_out[kind]
            outs.append(src[n].reshape(shapes[n]))
    return tuple(outs)
```

```python
import jax
import jax.numpy as jnp
from jax import lax
from jax.experimental import pallas as pl
from jax.experimental.pallas import tpu as pltpu

F32 = jnp.float32
BF16 = jnp.bfloat16

EPS = 1e-6
N_DEV = 8
LANE = 128
VMEM_LIMIT_BYTES = 48 * 1024 * 1024

SB_HEAD_DIM = 128
SSM_HEAD_DIM = 64
SSM_GROUPS = 4
SSM_STATE = 128
SSM_CHUNK = 128
HALO = 32
HEAD_ROWS = 8
NEG_BIG = -1e30

ADAM_LR = 0.001
ADAM_B1 = 0.9
ADAM_B2 = 0.999
ADAM_EPS = 1e-08
ADAM_WD = 0.01
ADAM_STEP = 10

NT = (((1,), (1,)), ((), ()))
TN = (((0,), (0,)), ((), ()))


def _cp(*sem):
    return pltpu.CompilerParams(dimension_semantics=sem, vmem_limit_bytes=VMEM_LIMIT_BYTES)


def _pick(n, cap, align):
    if n <= cap:
        return n
    t = (cap // align) * align
    while t >= align:
        if n % t == 0:
            return t
        t -= align
    raise ValueError(f"no tile for {n} (cap {cap}, align {align})")


def _sigmoid(x):
    return 0.5 * jnp.tanh(0.5 * x) + 0.5


def _silu(x):
    return x * _sigmoid(x)


def _silu_grad(x):
    s = _sigmoid(x)
    return s * (1.0 + x * (1.0 - s))


def _dot(a, b, dims=None):
    if dims is None:
        return jnp.dot(a, b, preferred_element_type=F32)
    return lax.dot_general(a, b, dims, preferred_element_type=F32)


def _tri_dot3(tri, x):
    hi = x.astype(BF16)
    r1 = x - hi.astype(F32)
    mid = r1.astype(BF16)
    lo = (r1 - mid.astype(F32)).astype(BF16)
    return _dot(tri, hi) + _dot(tri, mid) + _dot(tri, lo)


def _grid_step(n_inner):
    return lambda: pl.program_id(0) * n_inner + pl.program_id(1)


def mm_nn(a, b, *, add=None, out_dtype, name, comm=None):
    M, K = a.shape
    N = b.shape[1]
    tm = _pick(M, 2048 if K <= 1024 and add is None else 1024, 16)
    tn = _pick(N, 1024, LANE)

    def body(*refs):
        if add is None:
            a_ref, b_ref, o_ref = refs
        else:
            a_ref, b_ref, add_ref, o_ref = refs
        acc = _dot(a_ref[...], b_ref[...])
        if add is not None:
            acc = acc + add_ref[...]
        o_ref[...] = acc.astype(out_dtype)

    in_specs = [pl.BlockSpec((tm, K), lambda i, j: (i, 0)), pl.BlockSpec((K, tn), lambda i, j: (0, j))]
    args = [a, b]
    if add is not None:
        in_specs.append(pl.BlockSpec((tm, tn), lambda i, j: (i, j)))
        args.append(add)
    grid = (M // tm, N // tn)
    body, c_in, c_args, c_out, c_shape, c_scratch = _hosted(body, len(args), 1, comm, _grid_step(grid[1]),
                                                            grid[0] * grid[1])
    out = pl.pallas_call(
        body, name=name, grid=grid, in_specs=in_specs + c_in,
        out_specs=[pl.BlockSpec((tm, tn), lambda i, j: (i, j))] + c_out,
        out_shape=[jax.ShapeDtypeStruct((M, N), out_dtype)] + c_shape, scratch_shapes=c_scratch,
        compiler_params=_cp(*(("arbitrary",) * 2 if comm else ("parallel",) * 2)))(*args, *c_args)
    return (out[0], out[1:]) if comm else out[0]


def mm_nt_terms(terms, b, *, out_dtype, name, comm=None):
    M = terms[0][0].shape[0]
    N = b.shape[0]
    n_terms = len(terms)
    if n_terms == 1:
        tm, tn = _pick(M, 2048, 16), _pick(N, 1024, LANE)
    else:
        tm, tn = _pick(M, 256, 16), _pick(N, 1024, LANE)

    def body(*refs):
        o_ref = refs[-1]
        acc = None
        for t in range(n_terms):
            part = _dot(refs[2 * t][...], refs[2 * t + 1][...], NT)
            acc = part if acc is None else acc + part
        o_ref[...] = acc.astype(out_dtype)

    in_specs, args = [], []
    for arr, cb, w, off in terms:
        assert off % w == 0
        in_specs.append(pl.BlockSpec((tm, w), lambda i, j, cb=cb: (i, cb)))
        in_specs.append(pl.BlockSpec((tn, w), lambda i, j, ob=off // w: (j, ob)))
        args += [arr, b]
    grid = (M // tm, N // tn)
    body, c_in, c_args, c_out, c_shape, c_scratch = _hosted(body, len(args), 1, comm, _grid_step(grid[1]),
                                                            grid[0] * grid[1])
    out = pl.pallas_call(
        body, name=name, grid=grid, in_specs=in_specs + c_in,
        out_specs=[pl.BlockSpec((tm, tn), lambda i, j: (i, j))] + c_out,
        out_shape=[jax.ShapeDtypeStruct((M, N), out_dtype)] + c_shape, scratch_shapes=c_scratch,
        compiler_params=_cp(*(("arbitrary",) * 2 if comm else ("parallel",) * 2)))(*args, *c_args)
    return (out[0], out[1:]) if comm else out[0]


def mm_nn_terms(terms, b, *, out_dtype, name):
    M = terms[0][0].shape[0]
    N = b.shape[1]
    tm = _pick(M, 256, 16)
    tn = _pick(N, 1024, LANE)
    n_terms = len(terms)

    def body(*refs):
        o_ref = refs[-1]
        acc = None
        for t in range(n_terms):
            part = _dot(refs[2 * t][...], refs[2 * t + 1][...])
            acc = part if acc is None else acc + part
        o_ref[...] = acc.astype(out_dtype)

    in_specs, args = [], []
    for arr, cb, w, off in terms:
        assert off % w == 0
        in_specs.append(pl.BlockSpec((tm, w), lambda i, j, cb=cb: (i, cb)))
        in_specs.append(pl.BlockSpec((w, tn), lambda i, j, ob=off // w: (ob, j)))
        args += [arr, b]
    return pl.pallas_call(
        body, name=name, grid=(M // tm, N // tn), in_specs=in_specs,
        out_specs=pl.BlockSpec((tm, tn), lambda i, j: (i, j)),
        out_shape=jax.ShapeDtypeStruct((M, N), out_dtype),
        compiler_params=_cp("parallel", "parallel"))(*args)


def mm_tn(a, b, *, out_dtype, name):
    T, M = a.shape
    N = b.shape[1]
    tm = _pick(M, 1024, LANE)
    tn = _pick(N, 1024, LANE)
    tk = _pick(T, 2048, 16)
    nk = T // tk

    def body(a_ref, b_ref, o_ref, acc_ref):
        k = pl.program_id(2)

        @pl.when(k == 0)
        def _():
            acc_ref[...] = jnp.zeros_like(acc_ref)

        acc_ref[...] += _dot(a_ref[...], b_ref[...], TN)

        @pl.when(k == nk - 1)
        def _():
            o_ref[...] = acc_ref[...].astype(out_dtype)

    return pl.pallas_call(
        body, name=name, grid=(M // tm, N // tn, nk),
        in_specs=[pl.BlockSpec((tk, tm), lambda i, j, k: (k, i)), pl.BlockSpec((tk, tn), lambda i, j, k: (k, j))],
        out_specs=pl.BlockSpec((tm, tn), lambda i, j, k: (i, j)),
        out_shape=jax.ShapeDtypeStruct((M, N), out_dtype),
        scratch_shapes=[pltpu.VMEM((tm, tn), F32)],
        compiler_params=_cp("parallel", "parallel", "arbitrary"))(a, b)


def rmsnorm_fwd(h, w, *, name, comm=None):
    T, D = h.shape
    tt = _pick(T, 512, 16)

    def body(h_ref, w_ref, n_ref):
        x = h_ref[...]
        r = lax.rsqrt(jnp.mean(x * x, axis=-1, keepdims=True) + EPS)
        n_ref[...] = (x * r * w_ref[...]).astype(BF16)

    body, c_in, c_args, c_out, c_shape, c_scratch = _hosted(body, 2, 1, comm, lambda: pl.program_id(0), T // tt)
    out = pl.pallas_call(
        body, name=name, grid=(T // tt,),
        in_specs=[pl.BlockSpec((tt, D), lambda i: (i, 0)), pl.BlockSpec((1, D), lambda i: (0, 0))] + c_in,
        out_specs=[pl.BlockSpec((tt, D), lambda i: (i, 0))] + c_out,
        out_shape=[jax.ShapeDtypeStruct((T, D), BF16)] + c_shape, scratch_shapes=c_scratch,
        compiler_params=_cp("arbitrary" if comm else "parallel"))(h, w, *c_args)
    return (out[0], out[1:]) if comm else out[0]


def rmsnorm_bwd(h, w, dn, dres, *, name, comm=None):
    T, D = h.shape
    tt = _pick(T, 512, 16)

    def body(h_ref, w_ref, dn_ref, dres_ref, dh_ref, dhb_ref, gw_ref):
        @pl.when(pl.program_id(0) == 0)
        def _():
            gw_ref[...] = jnp.zeros_like(gw_ref)

        x = h_ref[...]
        r = lax.rsqrt(jnp.mean(x * x, axis=-1, keepdims=True) + EPS)
        xhat = x * r
        g = dn_ref[...].astype(F32)
        gw_ref[...] += jnp.sum(g * xhat, axis=0, keepdims=True)
        dxh = g * w_ref[...]
        dx = r * (dxh - xhat * jnp.mean(dxh * xhat, axis=-1, keepdims=True))
        dh = dres_ref[...] + dx
        dh_ref[...] = dh
        dhb_ref[...] = dh.astype(BF16)

    row = pl.BlockSpec((tt, D), lambda i: (i, 0))
    vec = pl.BlockSpec((1, D), lambda i: (0, 0))
    body, c_in, c_args, c_out, c_shape, c_scratch = _hosted(body, 4, 3, comm, lambda: pl.program_id(0), T // tt)
    out = pl.pallas_call(
        body, name=name, grid=(T // tt,), in_specs=[row, vec, row, row] + c_in, out_specs=[row, row, vec] + c_out,
        out_shape=[jax.ShapeDtypeStruct((T, D), F32), jax.ShapeDtypeStruct((T, D), BF16),
                   jax.ShapeDtypeStruct((1, D), F32)] + c_shape,
        scratch_shapes=c_scratch, compiler_params=_cp("arbitrary"))(h, w, dn, dres, *c_args)
    return (out[0], out[1], out[2], out[3:]) if comm else out


def final_loss(h, w, target, *, name):
    T, D = h.shape
    tt = _pick(T, 512, 16)

    def body(h_ref, w_ref, t_ref, loss_ref, dh_ref, dhb_ref, gw_ref):
        @pl.when(pl.program_id(0) == 0)
        def _():
            gw_ref[...] = jnp.zeros_like(gw_ref)
            loss_ref[...] = jnp.zeros_like(loss_ref)

        x = h_ref[...]
        r = lax.rsqrt(jnp.mean(x * x, axis=-1, keepdims=True) + EPS)
        xhat = x * r
        e = xhat * w_ref[...] - t_ref[...]
        loss_ref[...] += jnp.sum(e * e) * (0.5 / D)
        g = e * (1.0 / D)
        gw_ref[...] += jnp.sum(g * xhat, axis=0, keepdims=True)
        dxh = g * w_ref[...]
        dh = r * (dxh - xhat * jnp.mean(dxh * xhat, axis=-1, keepdims=True))
        dh_ref[...] = dh
        dhb_ref[...] = dh.astype(BF16)

    row = pl.BlockSpec((tt, D), lambda i: (i, 0))
    vec = pl.BlockSpec((1, D), lambda i: (0, 0))
    one = pl.BlockSpec((1, LANE), lambda i: (0, 0))
    return pl.pallas_call(
        body, name=name, grid=(T // tt,), in_specs=[row, vec, row], out_specs=[one, row, row, vec],
        out_shape=[jax.ShapeDtypeStruct((1, LANE), F32), jax.ShapeDtypeStruct((T, D), F32),
                   jax.ShapeDtypeStruct((T, D), BF16), jax.ShapeDtypeStruct((1, D), F32)],
        compiler_params=_cp("arbitrary"))(h, w, target)


CONV_CHUNK = 32
ROW_CHUNK = 16
SUBLANES = 8


def _conv_tiles(seq, C, K):
    return _pick(seq, 1024 if K <= SUBLANES else 512, HALO), _pick(C, 512, LANE)


def _residues(offsets):
    return sorted({s % SUBLANES for s in offsets} - {0})


def _fill_shifted(buf, shifted, residues):
    n = buf.shape[0] - SUBLANES
    for i, r in enumerate(residues):
        shifted[i, 0:n, :] = buf[r:r + n, :]


def _tap(buf, shifted, residues, offset, start, rows):
    r = offset % SUBLANES
    base = offset - r
    ref = buf if r == 0 else shifted.at[residues.index(r)]
    return ref[pl.ds(start + base, rows), :]


def dwconv_fwd(src, offs, w, b, *, C, seq, glu, silu_out, name, comm=None):
    T = src.shape[0]
    K = w.shape[0]
    assert K - 1 <= HALO
    tt, tc = _conv_tiles(seq, C, K)
    n_in = 2 if glu else 1
    per = tt // HALO
    offsets = [HALO - (K - 1) + k for k in range(K)]
    residues = _residues(offsets)

    def body(*refs):
        cur = refs[0:2 * n_in:2]
        halo = refs[1:2 * n_in:2]
        w_ref, b_ref = refs[2 * n_in], refs[2 * n_in + 1]
        outs = refs[2 * n_in + 2:-2]
        buf, shifted = refs[-2], refs[-1]
        i = pl.program_id(1)
        first = (i * tt) % seq == 0

        def pre(rs, rows):
            v = rs[0][rows, :].astype(F32)
            return v * _sigmoid(rs[1][rows, :].astype(F32)) if glu else v

        def build(ci, carry):
            start = pl.multiple_of(ci * CONV_CHUNK, CONV_CHUNK)
            buf[pl.ds(HALO + start, CONV_CHUNK), :] = pre(cur, pl.ds(start, CONV_CHUNK))
            return carry

        buf[0:HALO, :] = jnp.where(first, 0.0, pre(halo, slice(None)))
        lax.fori_loop(0, tt // CONV_CHUNK, build, 0, unroll=2)
        _fill_shifted(buf, shifted, residues)

        def chunk(ci, carry):
            start = pl.multiple_of(ci * CONV_CHUNK, CONV_CHUNK)
            acc = jnp.broadcast_to(b_ref[...], (CONV_CHUNK, tc))
            for k in range(K):
                acc = acc + w_ref[k:k + 1, :] * _tap(buf, shifted, residues, offsets[k], start, CONV_CHUNK)
            outs[0][pl.ds(start, CONV_CHUNK), :] = acc.astype(BF16)
            if silu_out:
                outs[1][pl.ds(start, CONV_CHUNK), :] = _silu(acc).astype(BF16)
            return carry

        lax.fori_loop(0, tt // CONV_CHUNK, chunk, 0)

    in_specs, args = [], []
    for off in offs:
        assert off % tc == 0
        in_specs.append(pl.BlockSpec((tt, tc), lambda j, i, ob=off // tc: (i, ob + j)))
        in_specs.append(pl.BlockSpec((HALO, tc), lambda j, i, ob=off // tc: (jnp.maximum(i * per - 1, 0), ob + j)))
        args += [src, src]
    in_specs += [pl.BlockSpec((K, tc), lambda j, i: (0, j)), pl.BlockSpec((1, tc), lambda j, i: (0, j))]
    args += [w, b]
    n_out = 2 if silu_out else 1
    grid = (C // tc, T // tt)
    body, c_in, c_args, c_out, c_shape, c_scratch = _hosted(body, len(args), n_out, comm, _grid_step(grid[1]),
                                                            grid[0] * grid[1])
    out = pl.pallas_call(
        body, name=name, grid=grid, in_specs=in_specs + c_in,
        out_specs=[pl.BlockSpec((tt, tc), lambda j, i: (i, j))] * n_out + c_out,
        out_shape=[jax.ShapeDtypeStruct((T, C), BF16)] * n_out + c_shape,
        scratch_shapes=[pltpu.VMEM((HALO + tt, tc), F32), pltpu.VMEM((max(len(residues), 1), HALO + tt, tc), F32)]
        + c_scratch,
        compiler_params=_cp("arbitrary" if comm else "parallel", "arbitrary"))(*args, *c_args)
    return (out[:n_out], out[n_out:]) if comm else out


def dwconv_bwd(du, u, src, offs, w, *, C, seq, glu, silu_out, name):
    T = src.shape[0]
    K = w.shape[0]
    assert K - 1 <= HALO
    tt, tc = _conv_tiles(seq, C, K)
    n_in = 2 if glu else 1
    per = tt // HALO
    last_blk = T // HALO - 1
    g_offsets = [K - 1 - k for k in range(K)]
    g_res = _residues(g_offsets)

    def body(*refs):
        pos = 0
        du_cur, du_nxt = refs[0], refs[1]
        pos = 2
        if silu_out:
            u_cur, u_nxt = refs[2], refs[3]
            pos = 4
        cur = refs[pos:pos + n_in]
        pos += n_in
        w_ref = refs[pos]
        outs = refs[pos + 1:pos + 1 + n_in]
        dw_ref, db_ref = refs[pos + 1 + n_in], refs[pos + 2 + n_in]
        gbuf, gshift, dw_acc, db_acc = refs[-4:]
        i = pl.program_id(1)
        last = ((i + 1) * tt) % seq == 0

        @pl.when(i == 0)
        def _():
            dw_acc[...] = jnp.zeros_like(dw_acc)
            db_acc[...] = jnp.zeros_like(db_acc)

        def build(ci, carry):
            rows = pl.ds(pl.multiple_of(ci * CONV_CHUNK, CONV_CHUNK), CONV_CHUNK)
            g = du_cur[rows, :].astype(F32)
            if silu_out:
                g = g * _silu_grad(u_cur[rows, :].astype(F32))
            gbuf[rows, :] = g
            return carry

        lax.fori_loop(0, tt // CONV_CHUNK, build, 0, unroll=2)
        g_nxt = du_nxt[...].astype(F32)
        if silu_out:
            g_nxt = g_nxt * _silu_grad(u_nxt[...].astype(F32))
        gbuf[tt:tt + HALO, :] = jnp.where(last, 0.0, g_nxt)
        _fill_shifted(gbuf, gshift, g_res)

        def fold(v):
            out = v[0:SUBLANES]
            for s in range(SUBLANES, CONV_CHUNK, SUBLANES):
                out = out + v[s:s + SUBLANES]
            return out

        def chunk(ci, carry):
            start = pl.multiple_of(ci * CONV_CHUNK, CONV_CHUNK)
            rows = pl.ds(start, CONV_CHUNK)
            a = cur[0][rows, :].astype(F32)
            if glu:
                s = _sigmoid(cur[1][rows, :].astype(F32))
                x_in = a * s
            else:
                x_in = a
            dx = jnp.zeros((CONV_CHUNK, tc), F32)
            for k in range(K):
                g_k = _tap(gbuf, gshift, g_res, g_offsets[k], start, CONV_CHUNK)
                dx = dx + w_ref[k:k + 1, :] * g_k
                dw_acc[k * SUBLANES:(k + 1) * SUBLANES, :] += fold(g_k * x_in)
            db_acc[...] += fold(gbuf[rows, :])
            if glu:
                outs[0][rows, :] = (dx * s).astype(BF16)
                outs[1][rows, :] = (dx * a * s * (1.0 - s)).astype(BF16)
            else:
                outs[0][rows, :] = dx.astype(BF16)
            return carry

        lax.fori_loop(0, tt // CONV_CHUNK, chunk, 0)

        @pl.when(i == T // tt - 1)
        def _():
            for k in range(K):
                dw_ref[k:k + 1, :] = jnp.sum(dw_acc[k * SUBLANES:(k + 1) * SUBLANES, :], axis=0, keepdims=True)
            db_ref[...] = jnp.sum(db_acc[...], axis=0, keepdims=True)

    def cur_spec(ob):
        return pl.BlockSpec((tt, tc), lambda j, i: (i, ob + j))

    def nxt_spec(ob):
        return pl.BlockSpec((HALO, tc), lambda j, i: (jnp.minimum((i + 1) * per, last_blk), ob + j))

    in_specs = [cur_spec(0), nxt_spec(0)]
    args = [du, du]
    if silu_out:
        in_specs += [cur_spec(0), nxt_spec(0)]
        args += [u, u]
    for off in offs:
        assert off % tc == 0
        in_specs.append(cur_spec(off // tc))
        args.append(src)
    in_specs.append(pl.BlockSpec((K, tc), lambda j, i: (0, j)))
    args.append(w)
    out_specs = [pl.BlockSpec((tt, tc), lambda j, i: (i, j))] * n_in
    out_specs += [pl.BlockSpec((K, tc), lambda j, i: (0, j)), pl.BlockSpec((1, tc), lambda j, i: (0, j))]
    out_shape = [jax.ShapeDtypeStruct((T, C), BF16)] * n_in
    out_shape += [jax.ShapeDtypeStruct((K, C), F32), jax.ShapeDtypeStruct((1, C), F32)]
    return pl.pallas_call(
        body, name=name, grid=(C // tc, T // tt), in_specs=in_specs, out_specs=out_specs, out_shape=out_shape,
        scratch_shapes=[pltpu.VMEM((tt + HALO, tc), F32), pltpu.VMEM((max(len(g_res), 1), tt + HALO, tc), F32),
                        pltpu.VMEM((K * SUBLANES, tc), F32), pltpu.VMEM((SUBLANES, tc), F32)],
        compiler_params=_cp("parallel", "arbitrary"))(*args)


def mix0_post_fwd(u2, proj, o, ln_w, ln_b, *, CW, gc_off, ga_off, name):
    T = u2.shape[0]
    tt = _pick(T, 256, 16)

    def body(u_ref, gc_ref, ga_ref, o_ref, lw_ref, lb_ref, y_ref):
        def chunk(ci, carry):
            rows = pl.ds(pl.multiple_of(ci * ROW_CHUNK, ROW_CHUNK), ROW_CHUNK)
            u = u_ref[rows, :].astype(F32)
            mu = jnp.mean(u, axis=-1, keepdims=True)
            xc = u - mu
            r = lax.rsqrt(jnp.mean(xc * xc, axis=-1, keepdims=True) + EPS)
            u3 = xc * r * lw_ref[...] + lb_ref[...]
            y_ref[rows, 0:CW] = (_silu(u3) * _silu(gc_ref[rows, :].astype(F32))).astype(BF16)
            y_ref[rows, CW:2 * CW] = (o_ref[rows, :].astype(F32) * _silu(ga_ref[rows, :].astype(F32))).astype(BF16)
            return carry

        lax.fori_loop(0, tt // ROW_CHUNK, chunk, 0, unroll=4)

    row = pl.BlockSpec((tt, CW), lambda i: (i, 0))
    vec = pl.BlockSpec((1, CW), lambda i: (0, 0))
    return pl.pallas_call(
        body, name=name, grid=(T // tt,),
        in_specs=[row, pl.BlockSpec((tt, CW), lambda i: (i, gc_off // CW)),
                  pl.BlockSpec((tt, CW), lambda i: (i, ga_off // CW)), row, vec, vec],
        out_specs=pl.BlockSpec((tt, 2 * CW), lambda i: (i, 0)),
        out_shape=jax.ShapeDtypeStruct((T, 2 * CW), BF16),
        compiler_params=_cp("parallel"))(u2, proj, proj, o, ln_w, ln_b)


def mix0_post_bwd(dy, u2, proj, o, ln_w, ln_b, *, CW, gc_off, ga_off, name):
    T = u2.shape[0]
    tt = _pick(T, 256, 16)

    def body(dy_ref, u_ref, gc_ref, ga_ref, o_ref, lw_ref, lb_ref, du_ref, dgc_ref, dga_ref, do_ref, dlw_ref, dlb_ref,
             lw_acc, lb_acc):
        i = pl.program_id(0)

        @pl.when(i == 0)
        def _():
            lw_acc[...] = jnp.zeros_like(lw_acc)
            lb_acc[...] = jnp.zeros_like(lb_acc)

        def fold(v):
            out = v[0:SUBLANES]
            for s in range(SUBLANES, ROW_CHUNK, SUBLANES):
                out = out + v[s:s + SUBLANES]
            return out

        def chunk(ci, carry):
            rows = pl.ds(pl.multiple_of(ci * ROW_CHUNK, ROW_CHUNK), ROW_CHUNK)
            dyc = dy_ref[rows, 0:CW].astype(F32)
            dya = dy_ref[rows, CW:2 * CW].astype(F32)
            u = u_ref[rows, :].astype(F32)
            mu = jnp.mean(u, axis=-1, keepdims=True)
            xc = u - mu
            r = lax.rsqrt(jnp.mean(xc * xc, axis=-1, keepdims=True) + EPS)
            xhat = xc * r
            u3 = xhat * lw_ref[...] + lb_ref[...]
            gc = gc_ref[rows, :].astype(F32)
            dgc_ref[rows, :] = (dyc * _silu(u3) * _silu_grad(gc)).astype(BF16)
            du3 = dyc * _silu(gc) * _silu_grad(u3)
            lw_acc[...] += fold(du3 * xhat)
            lb_acc[...] += fold(du3)
            dxh = du3 * lw_ref[...]
            du = r * (dxh - jnp.mean(dxh, axis=-1, keepdims=True)
                      - xhat * jnp.mean(dxh * xhat, axis=-1, keepdims=True))
            du_ref[rows, :] = du.astype(BF16)
            ga = ga_ref[rows, :].astype(F32)
            ov = o_ref[rows, :].astype(F32)
            do_ref[rows, :] = (dya * _silu(ga)).astype(BF16)
            dga_ref[rows, :] = (dya * ov * _silu_grad(ga)).astype(BF16)
            return carry

        lax.fori_loop(0, tt // ROW_CHUNK, chunk, 0, unroll=4)

        @pl.when(i == T // tt - 1)
        def _():
            dlw_ref[...] = jnp.sum(lw_acc[...], axis=0, keepdims=True)
            dlb_ref[...] = jnp.sum(lb_acc[...], axis=0, keepdims=True)

    row = pl.BlockSpec((tt, CW), lambda i: (i, 0))
    vec = pl.BlockSpec((1, CW), lambda i: (0, 0))
    big = jax.ShapeDtypeStruct((T, CW), BF16)
    small = jax.ShapeDtypeStruct((1, CW), F32)
    return pl.pallas_call(
        body, name=name, grid=(T // tt,),
        in_specs=[pl.BlockSpec((tt, 2 * CW), lambda i: (i, 0)), row,
                  pl.BlockSpec((tt, CW), lambda i: (i, gc_off // CW)),
                  pl.BlockSpec((tt, CW), lambda i: (i, ga_off // CW)), row, vec, vec],
        out_specs=[row, row, row, row, vec, vec],
        out_shape=[big, big, big, big, small, small],
        scratch_shapes=[pltpu.VMEM((SUBLANES, CW), F32), pltpu.VMEM((SUBLANES, CW), F32)],
        compiler_params=_cp("arbitrary"))(dy, u2, proj, proj, o, ln_w, ln_b)


SB_UNDERFLOW = 110.0
SB_BOUND_MARGIN = 1.02


def _sb_tile(seq):
    return _pick(seq, 256, LANE)


def _softplus(z):
    return jnp.maximum(z, 0.0) + jnp.log(1.0 + jnp.exp(-jnp.abs(z)))


def _tri01(n, lower):
    i = lax.broadcasted_iota(jnp.int32, (n, n), 0)
    j = lax.broadcasted_iota(jnp.int32, (n, n), 1)
    return ((i >= j) if lower else (i <= j)).astype(BF16)


SB_HEADS_FWD = 4
SB_HEADS_BWD = 2


def _sb_heads_per_step(heads, want):
    while heads % want:
        want //= 2
    return want


def sba_fwd(proj, *, B, seq, heads, q_off, k_off, v_off, name):
    dh = SB_HEAD_DIM
    tq = _sb_tile(seq)
    assert tq % (2 * LANE) == 0
    nq = seq // tq
    hps = _sb_heads_per_step(heads, SB_HEADS_FWD)
    hw = hps * dh
    scale = dh ** -0.5

    def body(q_ref, k_ref, v_ref, tri_ref, o_ref, ct_ref, acc_ref, kmax_ref):
        qi = pl.program_id(1)
        tri = tri_ref[...]
        qs = [(q_ref[:, h * dh:(h + 1) * dh].astype(F32) * scale).astype(BF16) for h in range(hps)]

        @pl.when(qi == 0)
        def _():
            def chunk(i, best):
                rows = k_ref[pl.ds(pl.multiple_of(i * tq, tq), tq), :].astype(F32)
                sq = rows * rows
                return tuple(jnp.maximum(best[h], jnp.max(jnp.sum(sq[:, h * dh:(h + 1) * dh], axis=1, keepdims=True),
                                                          axis=0, keepdims=True)) for h in range(hps))

            best = lax.fori_loop(0, nq, chunk, (jnp.zeros((1, 1), F32),) * hps)
            for h in range(hps):
                kmax_ref[h] = jnp.broadcast_to(jnp.sqrt(best[h]), (8, LANE))

        z_bound = [jnp.sqrt(jnp.sum(qs[h].astype(F32) ** 2, axis=1, keepdims=True))
                   * (SB_BOUND_MARGIN * jnp.max(kmax_ref[h], keepdims=True)) for h in range(hps)]

        def part(h, q_rows, start, n_keys, r, mask):
            k_blk = k_ref[pl.ds(start, n_keys), h * dh:(h + 1) * dh]
            v_blk = v_ref[pl.ds(start, n_keys), h * dh:(h + 1) * dh]
            z = _dot(q_rows, k_blk, NT)
            sp = _softplus(z)
            if mask is not None:
                sp = jnp.where(mask, sp, 0.0)
            wts = jnp.exp(z - (_dot(sp.astype(BF16), tri[0:n_keys, 0:n_keys]) + r))
            if mask is not None:
                wts = jnp.where(mask, wts, 0.0)
            return _dot(wts.astype(BF16), v_blk), r + jnp.sum(sp, axis=-1, keepdims=True)

        below = lax.broadcasted_iota(jnp.int32, (tq, tq), 1) < lax.broadcasted_iota(jnp.int32, (tq, tq), 0)
        has_left = qi > 0
        left = pl.multiple_of(jnp.maximum(qi - 1, 0) * tq, tq)
        rs = []
        for h in range(hps):
            pv_d, r = part(h, qs[h], pl.multiple_of(qi * tq, tq), tq, jnp.zeros((tq, 1), F32), below)
            pv_l, r = part(h, qs[h], left, tq, r, has_left)
            acc_ref[:, h * dh:(h + 1) * dh] = pv_d + pv_l
            rs.append(r)
        rs = tuple(rs)

        def block(start, rs):
            pvs, out = [], []
            for h in range(hps):
                pv, r = part(h, qs[h], start, tq, rs[h], None)
                pvs.append(pv)
                out.append(r)
            return pvs, tuple(out)

        def more(c):
            j, rs = c
            slack = rs[0] - z_bound[0]
            for h in range(1, hps):
                slack = jnp.minimum(slack, rs[h] - z_bound[h])
            return jnp.logical_and(j < qi, jnp.min(slack) <= SB_UNDERFLOW)

        def step(c):
            j, rs = c
            pvs, rs = block(pl.multiple_of((qi - 1 - j) * tq, tq), rs)
            for h in range(hps):
                acc_ref[:, h * dh:(h + 1) * dh] += pvs[h]
            return j + 1, rs

        n_left, totals = lax.while_loop(more, step, (has_left.astype(jnp.int32), rs))
        o_ref[...] = acc_ref[...].astype(BF16)
        for h in range(hps):
            ct_ref[0, 0, h, 0:8, :] = jnp.broadcast_to(totals[h], (tq, LANE)).T[0:8, :]
            ct_ref[0, 0, h, 8:16, :] = jnp.full((8, tq), n_left, F32)

    qb, kb, vb = q_off // hw, k_off // hw, v_off // hw
    G = heads // hps
    return pl.pallas_call(
        body, name=name, grid=(B * G, nq),
        in_specs=[pl.BlockSpec((tq, hw), lambda g, i: ((g // G) * nq + i, qb + g % G)),
                  pl.BlockSpec((seq, hw), lambda g, i: (g // G, kb + g % G)),
                  pl.BlockSpec((seq, hw), lambda g, i: (g // G, vb + g % G)),
                  pl.BlockSpec((tq, tq), lambda g, i: (0, 0))],
        out_specs=[pl.BlockSpec((tq, hw), lambda g, i: ((g // G) * nq + i, g % G)),
                   pl.BlockSpec((1, 1, hps, 16, tq), lambda g, i: (g // G, i, g % G, 0, 0))],
        out_shape=[jax.ShapeDtypeStruct((B * seq, heads * dh), BF16),
                   jax.ShapeDtypeStruct((B, nq, heads, 16, tq), F32)],
        scratch_shapes=[pltpu.VMEM((tq, hw), F32), pltpu.VMEM((hps, 8, LANE), F32)],
        compiler_params=_cp("parallel", "arbitrary"))(proj, proj, proj, jnp.tril(jnp.ones((tq, tq), BF16)))


def sba_bwd(proj, ctot, do, *, B, seq, heads, q_off, k_off, v_off, name, comm=None):
    dh = SB_HEAD_DIM
    tq = _sb_tile(seq)
    nq = seq // tq
    hps = _sb_heads_per_step(heads, SB_HEADS_BWD)
    hw = hps * dh
    scale = dh ** -0.5

    def body(q_ref, k_ref, v_ref, ct_ref, do_ref, sfx_ref, pre_ref, dq_ref, dk_ref, dv_ref, dq_acc, dk_acc, dv_acc):
        qi = pl.program_id(1)

        @pl.when(qi == 0)
        def _():
            dk_acc[...] = jnp.zeros_like(dk_acc)
            dv_acc[...] = jnp.zeros_like(dv_acc)

        tri_sfx = sfx_ref[...]
        tri_pre = pre_ref[...]
        qs = [(q_ref[:, h * dh:(h + 1) * dh].astype(F32) * scale).astype(BF16) for h in range(hps)]
        dos = [do_ref[:, h * dh:(h + 1) * dh] for h in range(hps)]
        totals = [jnp.max(jnp.broadcast_to(ct_ref[0, 0, h, 0:1, :], (LANE, tq)).T, axis=1, keepdims=True)
                  for h in range(hps)]
        dq_acc[...] = jnp.zeros_like(dq_acc)

        def part(h, rows, start, n_keys, pc, pg, mask):
            cols = slice(h * dh, (h + 1) * dh)
            q_rows, do_rows = qs[h][rows], dos[h][rows]
            k_blk = k_ref[pl.ds(start, n_keys), cols]
            v_blk = v_ref[pl.ds(start, n_keys), cols]
            z = _dot(q_rows, k_blk, NT)
            sp = _softplus(z)
            sig = jnp.exp(z - sp)
            if mask is not None:
                sp = jnp.where(mask, sp, 0.0)
            pc_next = pc + jnp.sum(sp, axis=-1, keepdims=True)
            wts = jnp.exp(z - (_dot(sp.astype(BF16), tri_sfx[0:n_keys, 0:n_keys]) + (totals[h][rows] - pc_next)))
            if mask is not None:
                wts = jnp.where(mask, wts, 0.0)
            g = _dot(do_rows, v_blk, NT) * wts
            dz = g - sig * (_dot(g.astype(BF16), tri_pre[0:n_keys, 0:n_keys]) + pg)
            if mask is not None:
                dz = jnp.where(mask, dz, 0.0)
            dz = dz.astype(BF16)
            dk_acc[pl.ds(start, n_keys), cols] += _dot(dz, q_rows, TN)
            dv_acc[pl.ds(start, n_keys), cols] += _dot(wts.astype(BF16), do_rows, TN)
            return pc_next, pg + jnp.sum(g, axis=-1, keepdims=True), _dot(dz, k_blk)

        def block(start, carry):
            out = []
            for h in range(hps):
                pc, pg, dq = part(h, slice(0, tq), start, tq, carry[h][0], carry[h][1], None)
                dq_acc[:, h * dh:(h + 1) * dh] += dq
                out.append((pc, pg))
            return tuple(out)

        zero = jnp.zeros((tq, 1), F32)
        n_left = jnp.max(ct_ref[0, 0, 0, 8:16, :]).astype(jnp.int32)
        carry = lax.fori_loop(qi - n_left, qi - 1, lambda j, c: block(pl.multiple_of(j * tq, tq), c),
                              ((zero, zero),) * hps)
        below = lax.broadcasted_iota(jnp.int32, (tq, tq), 1) < lax.broadcasted_iota(jnp.int32, (tq, tq), 0)
        has_left = n_left > 0
        left = pl.multiple_of(jnp.maximum(qi - 1, 0) * tq, tq)
        for h in range(hps):
            cols = slice(h * dh, (h + 1) * dh)
            pc, pg, dq_l = part(h, slice(0, tq), left, tq, carry[h][0], carry[h][1], has_left)
            _, _, dq_d = part(h, slice(0, tq), pl.multiple_of(qi * tq, tq), tq, pc, pg, below)
            dq_ref[:, cols] = ((dq_acc[:, cols] + dq_l + dq_d) * scale).astype(BF16)

        @pl.when(qi == nq - 1)
        def _():
            dk_ref[...] = dk_acc[...].astype(BF16)
            dv_ref[...] = dv_acc[...].astype(BF16)

    qb, kb, vb = q_off // hw, k_off // hw, v_off // hw
    G = heads // hps
    q_spec = pl.BlockSpec((tq, hw), lambda g, i: ((g // G) * nq + i, qb + g % G))
    o_spec = pl.BlockSpec((tq, hw), lambda g, i: ((g // G) * nq + i, g % G))
    kv_out = pl.BlockSpec((seq, hw), lambda g, i: (g // G, g % G))
    shp = jax.ShapeDtypeStruct((B * seq, heads * dh), BF16)
    body, c_in, c_args, c_out, c_shape, c_scratch = _hosted(body, 7, 3, comm, _grid_step(nq), B * G * nq)
    tri_spec = pl.BlockSpec((tq, tq), lambda g, i: (0, 0))
    ones = jnp.ones((tq, tq), BF16)
    out = pl.pallas_call(
        body, name=name, grid=(B * G, nq),
        in_specs=[q_spec,
                  pl.BlockSpec((seq, hw), lambda g, i: (g // G, kb + g % G)),
                  pl.BlockSpec((seq, hw), lambda g, i: (g // G, vb + g % G)),
                  pl.BlockSpec((1, 1, hps, 16, tq), lambda g, i: (g // G, i, g % G, 0, 0)), o_spec,
                  tri_spec, tri_spec] + c_in,
        out_specs=[o_spec, kv_out, kv_out] + c_out, out_shape=[shp, shp, shp] + c_shape,
        scratch_shapes=[pltpu.VMEM((tq, hw), F32), pltpu.VMEM((seq, hw), F32), pltpu.VMEM((seq, hw), F32)]
        + c_scratch,
        compiler_params=_cp("arbitrary" if comm else "parallel", "arbitrary"))(
            proj, proj, proj, ctot, do, jnp.tril(ones), jnp.triu(ones), *c_args)
    return (out[0], out[1], out[2], out[3:]) if comm else out


def _head_expand(n_heads, DI):
    j = jnp.arange(LANE, dtype=jnp.int32)[:, None]
    c = jnp.arange(DI, dtype=jnp.int32)[None, :] // SSM_HEAD_DIM
    return ((j == c) & (j < n_heads)).astype(BF16)


def _split3(x):
    hi = x.astype(BF16)
    r1 = x - hi.astype(F32)
    mid = r1.astype(BF16)
    return hi, mid, (r1 - mid.astype(F32)).astype(BF16)


def dt_fwd(proj, bias, a_log, expand, *, dt_off, name):
    T = proj.shape[0]
    DI = expand.shape[1]
    L = SSM_CHUNK
    tt = _pick(T, 512, L)

    def body(raw_ref, bias_ref, al_ref, e_ref, dt_ref, cs_ref, dtx_ref, csx_ref):
        x = raw_ref[...].astype(F32) + bias_ref[...]
        dt = _softplus(x)
        dt_ref[...] = dt
        la = dt * (-jnp.exp(al_ref[...]))
        tri = _tri01(L, True)
        for c in range(tt // L):
            cs_ref[c * L:(c + 1) * L, :] = _tri_dot3(tri, la[c * L:(c + 1) * L, :])
        e = e_ref[...]
        dtx_ref[...] = _dot(dt.astype(BF16), e).astype(BF16)
        hi, mid, lo = _split3(cs_ref[...])
        csx_ref[...] = _dot(hi, e) + _dot(mid, e) + _dot(lo, e)

    row = pl.BlockSpec((tt, LANE), lambda i: (i, 0))
    wide = pl.BlockSpec((tt, DI), lambda i: (i, 0))
    vec = pl.BlockSpec((1, LANE), lambda i: (0, 0))
    return pl.pallas_call(
        body, name=name, grid=(T // tt,),
        in_specs=[pl.BlockSpec((tt, LANE), lambda i: (i, dt_off // LANE)), vec, vec,
                  pl.BlockSpec((LANE, DI), lambda i: (0, 0))],
        out_specs=[row, row, wide, wide],
        out_shape=[jax.ShapeDtypeStruct((T, LANE), F32), jax.ShapeDtypeStruct((T, LANE), F32),
                   jax.ShapeDtypeStruct((T, DI), BF16), jax.ShapeDtypeStruct((T, DI), F32)],
        compiler_params=_cp("parallel"))(proj, bias, a_log, expand)


def dt_bwd(ddt_x, dcs_x, dcs_cols, proj, dt, bias, a_log, reduce_t, *, dt_off, n_heads, name):
    T = proj.shape[0]
    DI = reduce_t.shape[0]
    L = SSM_CHUNK
    tt = _pick(T, 512, L)

    def body(ddtx_ref, dcsx_ref, dcsc_ref, raw_ref, dt_ref, bias_ref, al_ref, r_ref, draw_ref, dbias_ref, dal_ref,
             dla_buf):
        @pl.when(pl.program_id(0) == 0)
        def _():
            dbias_ref[...] = jnp.zeros_like(dbias_ref)
            dal_ref[...] = jnp.zeros_like(dal_ref)

        r = r_ref[...]
        ddt = _dot(ddtx_ref[...], r)
        dx = dcsx_ref[...]
        hi = dx.astype(BF16)
        dcs = _dot(hi, r) + _dot((dx - hi.astype(F32)).astype(BF16), r) + dcsc_ref[...]
        triu = _tri01(L, False)
        for c in range(tt // L):
            dla_buf[c * L:(c + 1) * L, :] = _tri_dot3(triu, dcs[c * L:(c + 1) * L, :])
        dla = dla_buf[...]
        a = -jnp.exp(al_ref[...])
        valid = lax.broadcasted_iota(jnp.int32, (tt, LANE), 1) < n_heads
        dal_ref[...] += jnp.sum(jnp.where(valid, dla * dt_ref[...], 0.0), axis=0, keepdims=True) * a
        x = raw_ref[...].astype(F32) + bias_ref[...]
        draw = jnp.where(valid, (ddt + dla * a) * _sigmoid(x), 0.0)
        dbias_ref[...] += jnp.sum(draw, axis=0, keepdims=True)
        draw_ref[...] = draw.astype(BF16)

    row = pl.BlockSpec((tt, LANE), lambda i: (i, 0))
    wide = pl.BlockSpec((tt, DI), lambda i: (i, 0))
    vec = pl.BlockSpec((1, LANE), lambda i: (0, 0))
    return pl.pallas_call(
        body, name=name, grid=(T // tt,),
        in_specs=[wide, wide, row, pl.BlockSpec((tt, LANE), lambda i: (i, dt_off // LANE)), row, vec, vec,
                  pl.BlockSpec((DI, LANE), lambda i: (0, 0))],
        out_specs=[row, vec, vec],
        out_shape=[jax.ShapeDtypeStruct((T, LANE), BF16), jax.ShapeDtypeStruct((1, LANE), F32),
                   jax.ShapeDtypeStruct((1, LANE), F32)],
        scratch_shapes=[pltpu.VMEM((tt, LANE), F32)],
        compiler_params=_cp("arbitrary"))(ddt_x, dcs_x, dcs_cols, proj, dt, bias, a_log, reduce_t)


def _pair_terms(x_ref, dtx_ref, csx_ref, csr_ref, pair, ppg, lo_half, causal):
    L = SSM_CHUNK
    g, pp = divmod(pair, ppg)
    ra = g * HEAD_ROWS + 2 * pp
    cols = slice(pair * LANE, (pair + 1) * LANE)
    X = x_ref[:, cols].astype(F32)
    dt_p = dtx_ref[:, cols].astype(F32)
    own = csx_ref[:, cols]
    other = pltpu.roll(own, SSM_HEAD_DIM, 1)
    csa_c = jnp.where(lo_half, own, other)
    csb_c = jnp.where(lo_half, other, own)
    La = jnp.exp(jnp.where(causal, csa_c - csr_ref[ra:ra + 1, :], NEG_BIG))
    Lb = jnp.exp(jnp.where(causal, csb_c - csr_ref[ra + 1:ra + 2, :], NEG_BIG))
    last = csx_ref[L - 1:L, cols]
    return g, ra, cols, X, dt_p, La, Lb, jnp.exp(own), jnp.exp(last - own), jnp.exp(last)


def scan_fwd(xbc, dt_x, cs_x, cs_row, d_full, *, B, seq, DI, name):
    L, N, G = SSM_CHUNK, SSM_STATE, SSM_GROUPS
    nc = seq // L
    XW = xbc.shape[1]
    n_pairs = DI // LANE
    ppg = n_pairs // G

    def body(x_ref, dtx_ref, csx_ref, csr_ref, d_ref, y_ref, st_ref, state):
        @pl.when(pl.program_id(0) == 0)
        def _():
            state[...] = jnp.zeros_like(state)

        causal = lax.broadcasted_iota(jnp.int32, (L, L), 0) >= lax.broadcasted_iota(jnp.int32, (L, L), 1)
        lo_half = lax.broadcasted_iota(jnp.int32, (L, LANE), 1) < SSM_HEAD_DIM
        for b in range(B):
            xb, yb = x_ref.at[b], y_ref.at[b]
            cbs = []
            for g in range(G):
                Bc = xb[:, DI + g * N:DI + (g + 1) * N]
                Cc = xb[:, DI + G * N + g * N:DI + G * N + (g + 1) * N]
                cbs.append((Bc, Cc, _dot(Cc, Bc, NT)))
            for pair in range(n_pairs):
                g, _, cols, X, dt_p, La, Lb, ecs, tail, e_last = _pair_terms(
                    xb, dtx_ref.at[b], csx_ref.at[b], csr_ref.at[b], pair, ppg, lo_half, causal)
                Bc, Cc, CB = cbs[g]
                xs = X * dt_p
                xsb = xs.astype(BF16)
                y = jnp.where(lo_half, _dot((CB * La).astype(BF16), xsb), _dot((CB * Lb).astype(BF16), xsb))
                ST = state[b, pair]
                st_ref[b, 0, pair] = ST
                y = y + ecs * _dot(Cc, ST.astype(BF16)) + d_ref[:, cols] * X
                yb[:, cols] = y.astype(BF16)
                state[b, pair] = e_last * ST + _dot(Bc, (xs * tail).astype(BF16), TN)

    wide = pl.BlockSpec((B, L, DI), lambda c: (0, c, 0))
    y, states = pl.pallas_call(
        body, name=name, grid=(nc,),
        in_specs=[pl.BlockSpec((B, L, XW), lambda c: (0, c, 0)), wide, wide,
                  pl.BlockSpec((B, G * HEAD_ROWS, L), lambda c: (0, 0, c)),
                  pl.BlockSpec((1, DI), lambda c: (0, 0))],
        out_specs=[wide, pl.BlockSpec((B, 1, n_pairs, N, LANE), lambda c: (0, c, 0, 0, 0))],
        out_shape=[jax.ShapeDtypeStruct((B, seq, DI), BF16),
                   jax.ShapeDtypeStruct((B, nc, n_pairs, N, LANE), F32)],
        scratch_shapes=[pltpu.VMEM((B, n_pairs, N, LANE), F32)],
        compiler_params=_cp("arbitrary"))(xbc.reshape(B, seq, XW), dt_x.reshape(B, seq, DI),
                                          cs_x.reshape(B, seq, DI), cs_row, d_full)
    return y.reshape(B * seq, DI), states


def scan_bwd(xbc, dt_x, cs_x, cs_row, d_full, states, dy, *, B, seq, DI, name):
    L, N, G = SSM_CHUNK, SSM_STATE, SSM_GROUPS
    nc = seq // L
    XW = xbc.shape[1]
    n_pairs = DI // LANE
    ppg = n_pairs // G
    HR = G * HEAD_ROWS
    inv_p = 1.0 / SSM_HEAD_DIM

    def body(x_ref, dtx_ref, csx_ref, csr_ref, d_ref, st_ref, dy_ref, dx_ref, ddtx_ref, dcsx_ref, dcsr_ref, dd_ref,
             dH):
        @pl.when(pl.program_id(0) == 0)
        def _():
            dH[...] = jnp.zeros_like(dH)
            dd_ref[...] = jnp.zeros_like(dd_ref)

        causal = lax.broadcasted_iota(jnp.int32, (L, L), 0) >= lax.broadcasted_iota(jnp.int32, (L, L), 1)
        lo_half = lax.broadcasted_iota(jnp.int32, (L, LANE), 1) < SSM_HEAD_DIM
        last_row = lax.broadcasted_iota(jnp.int32, (L, LANE), 0) == L - 1
        head_row = lax.broadcasted_iota(jnp.int32, (HR, 1), 0)
        for b in range(B):
            xb, dxb = x_ref.at[b], dx_ref.at[b]
            dcs_rows = jnp.zeros((HR, L), F32)

            for g in range(G):
                Bc = xb[:, DI + g * N:DI + (g + 1) * N]
                Cc = xb[:, DI + G * N + g * N:DI + G * N + (g + 1) * N]
                CB = _dot(Cc, Bc, NT)
                dCB = jnp.zeros((L, L), F32)
                dC = jnp.zeros((L, N), F32)
                dB = jnp.zeros((L, N), F32)
                for pp in range(ppg):
                    pair = g * ppg + pp
                    _, ra, cols, X, dt_p, La, Lb, ecs, tail, e_last = _pair_terms(
                        xb, dtx_ref.at[b], csx_ref.at[b], csr_ref.at[b], pair, ppg, lo_half, causal)
                    xs = X * dt_p
                    xsb = xs.astype(BF16)
                    Ma, Mb = CB * La, CB * Lb
                    dY = dy_ref[b, :, cols].astype(F32)
                    dYb = dY.astype(BF16)
                    dMa = _dot(jnp.where(lo_half, dY, 0.0).astype(BF16), xsb, NT)
                    dMb = _dot(jnp.where(lo_half, 0.0, dY).astype(BF16), xsb, NT)
                    dSa, dSb = dMa * Ma, dMb * Mb
                    dCB = dCB + dMa * La + dMb * Lb
                    dcs = jnp.where(lo_half, jnp.sum(dSa, axis=1, keepdims=True),
                                    jnp.sum(dSb, axis=1, keepdims=True)) * inv_p
                    dcs_rows = dcs_rows - jnp.where(head_row == ra, jnp.sum(dSa, axis=0, keepdims=True), 0.0)
                    dcs_rows = dcs_rows - jnp.where(head_row == ra + 1, jnp.sum(dSb, axis=0, keepdims=True), 0.0)
                    dxs = jnp.where(lo_half, _dot(Ma.astype(BF16), dYb, TN), _dot(Mb.astype(BF16), dYb, TN))
                    ST = st_ref[b, 0, pair]
                    STb = ST.astype(BF16)
                    dYe = (dY * ecs).astype(BF16)
                    dC = dC + _dot(dYe, STb, NT)
                    dSTp = _dot(Cc, dYe, TN)
                    dcs = dcs + dY * (ecs * _dot(Cc, STb))
                    dSTn = dH[b, pair]
                    dSTnb = dSTn.astype(BF16)
                    dSTp = dSTp + e_last * dSTn
                    XBt = _dot(Bc, dSTnb)
                    dxs = dxs + tail * XBt
                    t2 = xs * XBt * tail
                    at_end = (e_last * jnp.sum(dSTn * ST, axis=0, keepdims=True)
                              + jnp.sum(t2, axis=0, keepdims=True))
                    dcs = dcs - t2 + jnp.where(last_row, at_end, 0.0)
                    dB = dB + _dot((xs * tail).astype(BF16), dSTnb, NT)
                    dxb[:, cols] = (dxs * dt_p + d_ref[:, cols] * dY).astype(BF16)
                    ddtx_ref[b, :, cols] = (dxs * X).astype(BF16)
                    dcsx_ref[b, :, cols] = dcs
                    dd_ref[b, :, cols] += jnp.sum(dY * X, axis=0, keepdims=True)
                    dH[b, pair] = dSTp
                dCBb = dCB.astype(BF16)
                dxb[:, DI + g * N:DI + (g + 1) * N] = (dB + _dot(dCBb, Cc, TN)).astype(BF16)
                dxb[:, DI + G * N + g * N:DI + G * N + (g + 1) * N] = (dC + _dot(dCBb, Bc)).astype(BF16)
            dcsr_ref[b] = dcs_rows

    rev = lambda c: (0, nc - 1 - c, 0)
    wide = pl.BlockSpec((B, L, DI), rev)
    xspec = pl.BlockSpec((B, L, XW), rev)
    hrow = pl.BlockSpec((B, HR, L), lambda c: (0, 0, nc - 1 - c))
    dx, ddt_x, dcs_x, dcs_row, dd = pl.pallas_call(
        body, name=name, grid=(nc,),
        in_specs=[xspec, wide, wide, hrow, pl.BlockSpec((1, DI), lambda c: (0, 0)),
                  pl.BlockSpec((B, 1, n_pairs, N, LANE), lambda c: (0, nc - 1 - c, 0, 0, 0)), wide],
        out_specs=[xspec, wide, wide, hrow, pl.BlockSpec((B, 1, DI), lambda c: (0, 0, 0))],
        out_shape=[jax.ShapeDtypeStruct((B, seq, XW), BF16), jax.ShapeDtypeStruct((B, seq, DI), BF16),
                   jax.ShapeDtypeStruct((B, seq, DI), F32), jax.ShapeDtypeStruct((B, HR, seq), F32),
                   jax.ShapeDtypeStruct((B, 1, DI), F32)],
        scratch_shapes=[pltpu.VMEM((B, n_pairs, N, LANE), F32)],
        compiler_params=_cp("arbitrary"))(xbc.reshape(B, seq, XW), dt_x.reshape(B, seq, DI), cs_x.reshape(B, seq, DI),
                                          cs_row, d_full, states, dy.reshape(B, seq, DI))
    return dx.reshape(B * seq, XW), ddt_x.reshape(B * seq, DI), dcs_x.reshape(B * seq, DI), dcs_row, dd


def gnorm_fwd(y, proj, w, *, DI, name):
    T = y.shape[0]
    tt = _pick(T, 256, 16)
    gw = DI // SSM_GROUPS

    def body(y_ref, z_ref, w_ref, o_ref):
        for g in range(SSM_GROUPS):
            sl = slice(g * gw, (g + 1) * gw)
            y2 = y_ref[:, sl].astype(F32) * _silu(z_ref[:, sl].astype(F32))
            r = lax.rsqrt(jnp.mean(y2 * y2, axis=-1, keepdims=True) + EPS)
            o_ref[:, sl] = (y2 * r * w_ref[:, sl]).astype(BF16)

    row = pl.BlockSpec((tt, DI), lambda i: (i, 0))
    return pl.pallas_call(
        body, name=name, grid=(T // tt,),
        in_specs=[row, row, pl.BlockSpec((1, DI), lambda i: (0, 0))], out_specs=row,
        out_shape=jax.ShapeDtypeStruct((T, DI), BF16), compiler_params=_cp("parallel"))(y, proj, w)


def gnorm_bwd(dyn, y, proj, w, *, DI, name):
    T = y.shape[0]
    tt = _pick(T, 256, 16)
    gw = DI // SSM_GROUPS

    def body(dyn_ref, y_ref, z_ref, w_ref, dy_ref, dz_ref, dw_ref):
        @pl.when(pl.program_id(0) == 0)
        def _():
            dw_ref[...] = jnp.zeros_like(dw_ref)

        for g in range(SSM_GROUPS):
            sl = slice(g * gw, (g + 1) * gw)
            yv = y_ref[:, sl].astype(F32)
            z = z_ref[:, sl].astype(F32)
            sz = _silu(z)
            y2 = yv * sz
            r = lax.rsqrt(jnp.mean(y2 * y2, axis=-1, keepdims=True) + EPS)
            xhat = y2 * r
            d = dyn_ref[:, sl].astype(F32)
            dw_ref[:, sl] += jnp.sum(d * xhat, axis=0, keepdims=True)
            dxh = d * w_ref[:, sl]
            dy2 = r * (dxh - xhat * jnp.mean(dxh * xhat, axis=-1, keepdims=True))
            dy_ref[:, sl] = (dy2 * sz).astype(BF16)
            dz_ref[:, sl] = (dy2 * yv * _silu_grad(z)).astype(BF16)

    row = pl.BlockSpec((tt, DI), lambda i: (i, 0))
    vec = pl.BlockSpec((1, DI), lambda i: (0, 0))
    shp = jax.ShapeDtypeStruct((T, DI), BF16)
    return pl.pallas_call(
        body, name=name, grid=(T // tt,), in_specs=[row, row, row, vec], out_specs=[row, row, vec],
        out_shape=[shp, shp, jax.ShapeDtypeStruct((1, DI), F32)],
        compiler_params=_cp("arbitrary"))(dyn, y, proj, w)


N_CHIP = 4
SHARD_ROW_ALIGN = 128


def _comm_out_shapes(srcs, modes):
    return [jax.ShapeDtypeStruct(((N_DEV,) if mode in ("gather", "gather_direct") else ()) + s.shape, s.dtype)
            for s, mode in zip(srcs, modes)]


def _comm_scratch(n):
    return [pltpu.SemaphoreType.DMA((n, N_DEV - 1)), pltpu.SemaphoreType.DMA((n, N_DEV - 1)),
            pltpu.SemaphoreType.DMA((n,))]


def _comm_phases(modes, src_refs, out_refs, send_sems, recv_sems, local_sems):
    x, y, c = lax.axis_index("x"), lax.axis_index("y"), lax.axis_index("c")
    me, sibling = (x, y, c), (x, y, 1 - c)
    chips = [(1 - x, y), (x, 1 - y), (1 - x, 1 - y)]
    relays = [a for a, mode in enumerate(modes) if mode == "gather"]

    def slot(p):
        return 4 * p[0] + 2 * p[1] + p[2]

    def remote(a, k, src, dst, to):
        return pltpu.make_async_remote_copy(src_ref=src, dst_ref=dst, send_sem=send_sems.at[a, k],
                                            recv_sem=recv_sems.at[a, k], device_id=to,
                                            device_id_type=pl.DeviceIdType.MESH)

    def first_copies():
        local, two_way, send_only = [], [], []
        for a, mode in enumerate(modes):
            src, out = src_refs[a], out_refs[a]
            if mode == "sibling":
                two_way.append(remote(a, 0, src, out, sibling))
            elif mode == "chips":
                mine = 2 * x + y
                local.append(pltpu.make_async_copy(src.at[mine], out.at[mine], local_sems.at[a]))
                for j, chip in enumerate(chips):
                    two_way.append(remote(a, 1 + j, src.at[2 * chip[0] + chip[1]], out.at[mine], (*chip, c)))
            elif mode == "gather_direct":
                local.append(pltpu.make_async_copy(src, out.at[slot(me)], local_sems.at[a]))
                for k in range(1, N_DEV):
                    peer = (1 - x if k & 4 else x, 1 - y if k & 2 else y, 1 - c if k & 1 else c)
                    two_way.append(remote(a, k - 1, src, out.at[slot(me)], peer))
            else:
                assert mode == "gather"
                local.append(pltpu.make_async_copy(src, out.at[slot(me)], local_sems.at[a]))
                send_only.append(remote(a, 0, src, out.at[slot(me)], sibling))
                for j, chip in enumerate(chips):
                    send_only.append(remote(a, 1 + j, src, out.at[slot(me)], (*chip, c)))
        return local, two_way, send_only

    def forwards():
        out = []
        for a in relays:
            for j, chip in enumerate(chips):
                landed = out_refs[a].at[slot((*chip, c))]
                out.append((remote(a, 1 + j, landed, landed, me), remote(a, 4 + j, landed, landed, sibling)))
        return out

    def start():
        local, two_way, send_only = first_copies()
        for cp in local + two_way + send_only:
            cp.start()

    def relay():
        for arrival, fwd in forwards():
            arrival.wait_recv()
            fwd.start()

    def finish():
        local, two_way, send_only = first_copies()
        for a in relays:
            blk = out_refs[a].at[slot(sibling)]
            remote(a, 0, blk, blk, me).wait_recv()
            for j, chip in enumerate(chips):
                blk = out_refs[a].at[slot((*chip, 1 - c))]
                remote(a, 4 + j, blk, blk, me).wait_recv()
        for cp in send_only + [fwd for _, fwd in forwards()]:
            cp.wait_send()
        for cp in two_way + local:
            cp.wait()

    return start, relay, finish, bool(relays)


def _hosted(body, n_in, n_out, comm, step, n_steps):
    if comm is None:
        return body, [], [], [], [], []
    srcs, modes = comm
    nc = len(srcs)

    def wrapped(*refs):
        ins, csrc = refs[:n_in], refs[n_in:n_in + nc]
        outs = refs[n_in + nc:n_in + nc + n_out]
        cout = refs[n_in + nc + n_out:n_in + 2 * nc + n_out]
        scratch = refs[n_in + 2 * nc + n_out:len(refs) - 3]
        start, relay, finish, has_relay = _comm_phases(modes, csrc, cout, *refs[len(refs) - 3:])
        s = step()
        pl.when(s == 0)(start)
        body(*ins, *outs, *scratch)
        if has_relay:
            pl.when(s == (2 * n_steps) // 3)(relay)
        pl.when(s == n_steps - 1)(finish)

    any_spec = pl.BlockSpec(memory_space=pl.ANY)
    return wrapped, [any_spec] * nc, list(srcs), [any_spec] * nc, _comm_out_shapes(srcs, modes), _comm_scratch(nc)


def exchange(srcs, modes, *, name):
    n = len(srcs)

    def body(*refs):
        start, relay, finish, has_relay = _comm_phases(modes, refs[:n], refs[n:2 * n], *refs[2 * n:])
        start()
        if has_relay:
            relay()
        finish()

    any_spec = pl.BlockSpec(memory_space=pl.ANY)
    return pl.pallas_call(
        body, name=name, in_specs=[any_spec] * n, out_specs=[any_spec] * n, out_shape=_comm_out_shapes(srcs, modes),
        scratch_shapes=_comm_scratch(n), compiler_params=pltpu.CompilerParams(has_side_effects=True))(*srcs)


def pair_sum(a, b, *, name):
    n, R, C = a.shape
    tr = _pick(n * R, 1024, 16)

    def body(a_ref, b_ref, o_ref):
        o_ref[...] = (a_ref[...].astype(F32) + b_ref[...].astype(F32)).astype(BF16)

    blk = pl.BlockSpec((tr, C), lambda i: (i, 0))
    out = pl.pallas_call(
        body, name=name, grid=(n * R // tr,), in_specs=[blk, blk], out_specs=blk,
        out_shape=jax.ShapeDtypeStruct((n * R, C), BF16),
        compiler_params=_cp("parallel"))(a.reshape(n * R, C), b.reshape(n * R, C))
    return out.reshape(n, R, C)


def sum_slots(recv, *, name):
    _, R, C = recv.shape
    tr = _pick(R, 512, 8)

    def body(r_ref, o_ref):
        acc = r_ref[0].astype(F32)
        for p in range(1, N_DEV):
            acc = acc + r_ref[p].astype(F32)
        o_ref[...] = acc

    return pl.pallas_call(
        body, name=name, grid=(R // tr,),
        in_specs=[pl.BlockSpec((N_DEV, tr, C), lambda i: (0, i, 0))],
        out_specs=pl.BlockSpec((tr, C), lambda i: (i, 0)),
        out_shape=jax.ShapeDtypeStruct((R, C), F32), compiler_params=_cp("parallel"))(recv)


def adamw(gsrc, w, m, v, *, name):
    slots, R, C = gsrc.shape
    tr = _pick(R, 256, 16 if gsrc.dtype == BF16 else 8)
    c1 = 1.0 / (1.0 - ADAM_B1 ** ADAM_STEP)
    c2 = 1.0 / (1.0 - ADAM_B2 ** ADAM_STEP)

    def body(g_ref, w_ref, m_ref, v_ref, go_ref, d_ref, mo_ref, vo_ref):
        g = g_ref[0].astype(F32)
        for p in range(1, slots):
            g = g + g_ref[p].astype(F32)
        m2 = ADAM_B1 * m_ref[...] + (1.0 - ADAM_B1) * g
        v2 = ADAM_B2 * v_ref[...] + (1.0 - ADAM_B2) * (g * g)
        go_ref[...] = g
        mo_ref[...] = m2
        vo_ref[...] = v2
        d_ref[...] = -ADAM_LR * ((m2 * c1) / (jnp.sqrt(v2 * c2) + ADAM_EPS) + ADAM_WD * w_ref[...])

    blk = pl.BlockSpec((tr, C), lambda i: (i, 0))
    shp = jax.ShapeDtypeStruct((R, C), F32)
    return pl.pallas_call(
        body, name=name, grid=(R // tr,),
        in_specs=[pl.BlockSpec((slots, tr, C), lambda i: (0, i, 0)), blk, blk, blk],
        out_specs=[blk] * 4, out_shape=[shp] * 4, compiler_params=_cp("parallel"))(gsrc, w, m, v)


def _pad_cols(a, n):
    return jnp.pad(a, ((0, 0), (0, n - a.shape[1])))


def _to_rows(a, B, seq, H):
    G = SSM_GROUPS
    R = H // G
    t = a[:, :H].reshape(B, seq, G, R).transpose(0, 2, 3, 1)
    t = jnp.pad(t, ((0, 0), (0, 0), (0, HEAD_ROWS - R), (0, 0)))
    return t.reshape(B, G * HEAD_ROWS, seq)


def _from_rows(a, B, seq, H):
    G = SSM_GROUPS
    R = H // G
    t = a.reshape(B, G, HEAD_ROWS, seq)[:, :, :R].transpose(0, 3, 1, 2).reshape(B * seq, H)
    return _pad_cols(t, LANE)


def _chip_split(grads):
    c_idx = lax.axis_index("c")
    keep, give = [], []
    for g in grads:
        by_chip = g.reshape((N_CHIP, 2) + g.shape[1:])
        keep.append(lax.dynamic_index_in_dim(by_chip, c_idx, axis=1, keepdims=False))
        give.append(lax.dynamic_index_in_dim(by_chip, 1 - c_idx, axis=1, keepdims=False))
    return keep, give


def _chip_sums(grads, name):
    keep, give = _chip_split(grads)
    swapped = exchange(give, ["sibling"] * len(give), name="swap_" + name)
    return [pair_sum(k, s, name=f"chip_sum_{name}_{i}") for i, (k, s) in enumerate(zip(keep, swapped))]


def local_step(x, target, loc, od_w_in_t, *, B, seq):
    T, D = x.shape
    CW = D
    heads = CW // SB_HEAD_DIM
    DI = 2 * D
    H = DI // SSM_HEAD_DIM
    XW = DI + 2 * SSM_GROUPS * SSM_STATE
    in_odd = DI + XW + H
    w1_rows = in_odd // N_DEV
    q_off, k_off, v_off, gc_off, ga_off = 3 * CW, 4 * CW, 5 * CW, 2 * CW, 6 * CW
    dt_off = DI + XW

    small_packed, small_spans = _pack_rows([loc[n] for n in SMALL_SHARDED], LANE, 8)
    n0, (g_ev_in, small_all) = rmsnorm_fwd(x, loc["ev_norm_w"], name="l0_norm",
                                           comm=([loc["ev_w_in"].astype(BF16), small_packed],
                                                 ["gather", "gather_direct"]))
    p = {n: loc[n] for n in SMALL}
    for n, a in zip(SMALL_SHARDED, _unpack_rows(small_all, small_spans)):
        p[n] = _col_unshards(a)
    p["ev_w_in"] = _col_unshards(g_ev_in)
    proj0, (g_od_in_t,) = mm_nn(n0, p["ev_w_in"], out_dtype=BF16, name="l0_in_proj",
                                comm=([od_w_in_t.astype(BF16)], ["gather"]))
    w1t = g_od_in_t[:, :w1_rows].reshape(in_odd, D)
    w1t = jnp.pad(w1t, ((0, -(-(in_odd + LANE) // 256) * 256 - in_odd), (0, 0)))
    (u2,), (g_ev_out, g_od_out) = dwconv_fwd(
        proj0, (0, CW), p["ev_dw_w"], p["ev_dw_b"], C=CW, seq=seq, glu=True, silu_out=False, name="l0_conv",
        comm=([loc["ev_w_out"].astype(BF16), loc["od_w_out"].astype(BF16)], ["gather"] * 2))
    p["ev_w_out"] = g_ev_out.reshape(-1, D)
    od_w_out = g_od_out.reshape(-1, D)
    o, ctot = sba_fwd(proj0, B=B, seq=seq, heads=heads, q_off=q_off, k_off=k_off, v_off=v_off, name="l0_attn")
    ycat = mix0_post_fwd(u2, proj0, o, p["ev_ln_w"], p["ev_ln_b"], CW=CW, gc_off=gc_off, ga_off=ga_off,
                         name="l0_post")
    h1 = mm_nn(ycat, p["ev_w_out"], add=x, out_dtype=F32, name="l0_out_proj")

    n1 = rmsnorm_fwd(h1, p["od_norm_w"], name="l1_norm")
    proj1 = mm_nt_terms([(n1, 0, D, 0)], w1t, out_dtype=BF16, name="l1_in_proj")
    u_pre, xbc = dwconv_fwd(proj1, (DI,), p["od_conv_w"], p["od_conv_b"], C=XW, seq=seq, glu=False, silu_out=True,
                            name="l1_conv")
    bias_p, alog_p = _pad_cols(p["od_dt_bias"], LANE), _pad_cols(p["od_a_log"], LANE)
    expand = _head_expand(H, DI)
    dt, cs, dt_x, cs_x = dt_fwd(proj1, bias_p, alog_p, expand, dt_off=dt_off, name="l1_dt")
    cs_row = _to_rows(cs, B, seq, H)
    d_full = jnp.repeat(p["od_d"], SSM_HEAD_DIM, axis=1)
    y_ssd, states = scan_fwd(xbc, dt_x, cs_x, cs_row, d_full, B=B, seq=seq, DI=DI, name="l1_ssd")
    yn = gnorm_fwd(y_ssd, proj1, p["od_gnorm_w"], DI=DI, name="l1_gnorm")
    h2 = mm_nn(yn, od_w_out, add=h1, out_dtype=F32, name="l1_out_proj")

    loss, dh2, dh2b, g_final = final_loss(h2, p["final_norm_w"], target, name="loss_head")

    g_od_w_out = mm_tn(yn, dh2b, out_dtype=BF16, name="l1_dw_out")
    dyn = mm_nt_terms([(dh2b, 0, D, 0)], od_w_out, out_dtype=BF16, name="l1_d_out_proj")
    dy_ssd, dz, g_gnorm = gnorm_bwd(dyn, y_ssd, proj1, p["od_gnorm_w"], DI=DI, name="l1_gnorm_bwd")
    dxbc_c, ddt_x, dcs_x, dcs_row, dd_part = scan_bwd(xbc, dt_x, cs_x, cs_row, d_full, states, dy_ssd, B=B, seq=seq,
                                                      DI=DI, name="l1_ssd_bwd")
    g_d = dd_part.sum(axis=(0, 1)).reshape(H, SSM_HEAD_DIM).sum(axis=1)[None, :]
    draw, g_bias, g_alog = dt_bwd(ddt_x, dcs_x, _from_rows(dcs_row, B, seq, H), proj1, dt, bias_p, alog_p, expand.T,
                                  dt_off=dt_off, n_heads=H, name="l1_dt_bwd")
    dxbc, g_conv_w, g_conv_b = dwconv_bwd(dxbc_c, u_pre, proj1, (DI,), p["od_conv_w"], C=XW, seq=seq, glu=False,
                                          silu_out=True, name="l1_conv_bwd")
    tw = 512 if DI % 512 == 0 else LANE
    terms = [(dz, j, tw, j * tw) for j in range(DI // tw)]
    terms += [(dxbc, j, tw, DI + j * tw) for j in range(XW // tw)]
    terms += [(draw, 0, LANE, dt_off)]
    dn1 = mm_nn_terms(terms, w1t, out_dtype=BF16, name="l1_d_in_proj")
    g_od_w_in_t = jnp.concatenate([mm_tn(dz, n1, out_dtype=BF16, name="l1_dw_in_z"),
                                   mm_tn(dxbc, n1, out_dtype=BF16, name="l1_dw_in_xbc"),
                                   mm_tn(draw, n1, out_dtype=BF16, name="l1_dw_in_dt")], axis=0)[:in_odd]
    w1_pad = (-w1_rows) % SHARD_ROW_ALIGN
    keep, give = _chip_split([jnp.pad(g_od_w_in_t.reshape(N_DEV, w1_rows, D), ((0, 0), (0, w1_pad), (0, 0))),
                              g_od_w_out.reshape(N_DEV, -1, D)])
    dh1, dh1b, g_od_norm, swapped = rmsnorm_bwd(h1, p["od_norm_w"], dn1, dh2, name="l1_norm_bwd",
                                                comm=(give, ["sibling"] * 2))
    l1_chip = [pair_sum(k, s, name=f"chip_sum_l1_{i}") for i, (k, s) in enumerate(zip(keep, swapped))]

    g_ev_w_out = mm_tn(ycat, dh1b, out_dtype=BF16, name="l0_dw_out")
    dycat = mm_nt_terms([(dh1b, 0, D, 0)], p["ev_w_out"], out_dtype=BF16, name="l0_d_out_proj")
    du2, dgc, dga, do, g_ln_w, g_ln_b = mix0_post_bwd(dycat, u2, proj0, o, p["ev_ln_w"], p["ev_ln_b"], CW=CW,
                                                      gc_off=gc_off, ga_off=ga_off, name="l0_post_bwd")
    dq, dk, dv, (r_od_in_t, r_od_out) = sba_bwd(proj0, ctot, do, B=B, seq=seq, heads=heads, q_off=q_off, k_off=k_off,
                                                v_off=v_off, name="l0_attn_bwd", comm=(l1_chip, ["chips", "chips"]))
    dga_a, dga_b, g_dw_w, g_dw_b = dwconv_bwd(du2, None, proj0, (0, CW), p["ev_dw_w"], C=CW, seq=seq, glu=True,
                                              silu_out=False, name="l0_conv_bwd")
    pieces = [dga_a, dga_b, dgc, dq, dk, dv, dga]
    g_ev_w_in = jnp.concatenate([mm_tn(n0, pc, out_dtype=BF16, name=f"l0_dw_in_{j}") for j, pc in enumerate(pieces)],
                                axis=1)
    l0_chip = _chip_sums([_col_shards(g_ev_w_in), g_ev_w_out.reshape(N_DEV, -1, D)], "l0")
    dn0, (r_ev_in, r_ev_out) = mm_nt_terms([(pc, 0, CW, j * CW) for j, pc in enumerate(pieces)], p["ev_w_in"],
                                           out_dtype=BF16, name="l0_d_in_proj", comm=(l0_chip, ["chips", "chips"]))
    dx, _, g_ev_norm = rmsnorm_bwd(x, p["ev_norm_w"], dn0, dh1, name="l0_norm_bwd")

    small = dict(ev_norm_w=g_ev_norm, ev_dw_w=g_dw_w, ev_dw_b=g_dw_b, ev_ln_w=g_ln_w, ev_ln_b=g_ln_b,
                 od_norm_w=g_od_norm, od_conv_w=g_conv_w, od_conv_b=g_conv_b, od_dt_bias=g_bias[:, :H],
                 od_a_log=g_alog[:, :H], od_d=g_d, od_gnorm_w=g_gnorm, final_norm_w=g_final)
    received = dict(ev_w_in=r_ev_in, ev_w_out=r_ev_out, od_w_in=r_od_in_t, od_w_out=r_od_out)
    return loss, dx, small, received


BIG = ("ev_w_in", "ev_w_out", "od_w_in", "od_w_out")
SMALL = ("ev_norm_w", "ev_dw_w", "ev_dw_b", "ev_ln_w", "ev_ln_b", "od_norm_w", "od_conv_w", "od_conv_b",
         "od_dt_bias", "od_a_log", "od_d", "od_gnorm_w", "final_norm_w")
SMALL_SHARDED = ("ev_dw_w", "od_norm_w", "od_conv_w", "od_conv_b", "od_gnorm_w")
ORDER = ("ev_norm_w", "ev_w_in", "ev_dw_w", "ev_dw_b", "ev_ln_w", "ev_ln_b", "ev_w_out", "od_norm_w", "od_w_in",
         "od_conv_w", "od_conv_b", "od_dt_bias", "od_a_log", "od_d", "od_gnorm_w", "od_w_out", "final_norm_w")


def _pack_rows(arrs, width, row_align):
    parts, spans, r0 = [], [], 0
    for a in arrs:
        flat = a.reshape(-1)
        rows = -(-flat.shape[0] // (width * row_align)) * row_align
        parts.append(jnp.pad(flat, (0, rows * width - flat.shape[0])).reshape(rows, width))
        spans.append((r0, a.size, a.shape))
        r0 += rows
    return jnp.concatenate(parts, axis=0), spans


def _unpack_rows(packed, spans):
    lead = packed.shape[:-2]
    width = packed.shape[-1]
    out = []
    for r0, size, shape in spans:
        rows = -(-size // width)
        blk = packed[..., r0:r0 + rows, :].reshape(lead + (rows * width,))[..., :size]
        out.append(blk.reshape(lead + tuple(shape)))
    return out


def _col_shards(a):
    R, C8 = a.shape
    return a.reshape(R, N_DEV, C8 // N_DEV).transpose(1, 0, 2)


def _col_unshards(a):
    n, R, C = a.shape
    return a.transpose(1, 0, 2).reshape(R, n * C)


def kernel(x, ev_norm_w, ev_w_in, ev_dw_w, ev_dw_b, ev_ln_w, ev_ln_b, ev_w_out, od_norm_w, od_w_in, od_conv_w, od_conv_b, od_dt_bias, od_a_log, od_d, od_gnorm_w, od_w_out, final_norm_w, loss_target, m_ev_norm_w, m_ev_w_in, m_ev_dw_w, m_ev_dw_b, m_ev_ln_w, m_ev_ln_b, m_ev_w_out, m_od_norm_w, m_od_w_in, m_od_conv_w, m_od_conv_b, m_od_dt_bias, m_od_a_log, m_od_d, m_od_gnorm_w, m_od_w_out, m_final_norm_w, v_ev_norm_w, v_ev_w_in, v_ev_dw_w, v_ev_dw_b, v_ev_ln_w, v_ev_ln_b, v_ev_w_out, v_od_norm_w, v_od_w_in, v_od_conv_w, v_od_conv_b, v_od_dt_bias, v_od_a_log, v_od_d, v_od_gnorm_w, v_od_w_out, v_final_norm_w):
    loc = dict(ev_norm_w=ev_norm_w, ev_w_in=ev_w_in, ev_dw_w=ev_dw_w, ev_dw_b=ev_dw_b, ev_ln_w=ev_ln_w,
               ev_ln_b=ev_ln_b, ev_w_out=ev_w_out, od_norm_w=od_norm_w, od_w_in=od_w_in, od_conv_w=od_conv_w,
               od_conv_b=od_conv_b, od_dt_bias=od_dt_bias, od_a_log=od_a_log, od_d=od_d, od_gnorm_w=od_gnorm_w,
               od_w_out=od_w_out, final_norm_w=final_norm_w)
    mom = dict(ev_norm_w=m_ev_norm_w, ev_w_in=m_ev_w_in, ev_dw_w=m_ev_dw_w, ev_dw_b=m_ev_dw_b, ev_ln_w=m_ev_ln_w,
               ev_ln_b=m_ev_ln_b, ev_w_out=m_ev_w_out, od_norm_w=m_od_norm_w, od_w_in=m_od_w_in,
               od_conv_w=m_od_conv_w, od_conv_b=m_od_conv_b, od_dt_bias=m_od_dt_bias, od_a_log=m_od_a_log,
               od_d=m_od_d, od_gnorm_w=m_od_gnorm_w, od_w_out=m_od_w_out, final_norm_w=m_final_norm_w)
    var = dict(ev_norm_w=v_ev_norm_w, ev_w_in=v_ev_w_in, ev_dw_w=v_ev_dw_w, ev_dw_b=v_ev_dw_b, ev_ln_w=v_ev_ln_w,
               ev_ln_b=v_ev_ln_b, ev_w_out=v_ev_w_out, od_norm_w=v_od_norm_w, od_w_in=v_od_w_in,
               od_conv_w=v_od_conv_w, od_conv_b=v_od_conv_b, od_dt_bias=v_od_dt_bias, od_a_log=v_od_a_log,
               od_d=v_od_d, od_gnorm_w=v_od_gnorm_w, od_w_out=v_od_w_out, final_norm_w=v_final_norm_w)
    shapes = {n: loc[n].shape for n in ORDER}
    loc = {n: (a.reshape(1, -1) if a.ndim == 1 else a.reshape(a.shape[-2:]) if a.ndim == 3 else a)
           for n, a in loc.items()}
    mom = {n: a.reshape(loc[n].shape) for n, a in mom.items()}
    var = {n: a.reshape(loc[n].shape) for n, a in var.items()}

    B, seq, D = x.shape
    me = 4 * lax.axis_index("x") + 2 * lax.axis_index("y") + lax.axis_index("c")

    w1_rows = loc["od_w_in"].shape[1]
    w1_pad = (-w1_rows) % SHARD_ROW_ALIGN

    def to_t(a):
        return jnp.pad(a.T, ((0, w1_pad), (0, 0)))

    loss, dx, grads, received = local_step(x.reshape(B * seq, D), loss_target.reshape(B * seq, D), loc,
                                           to_t(loc["od_w_in"]), B=B, seq=seq)

    gsmall_packed, gsmall_spans = _pack_rows([grads[n] for n in SMALL] + [loss], LANE, 8)
    (gsmall_recv,) = exchange([gsmall_packed], ["gather_direct"], name="gather_small_grads")

    big_out = [{} for _ in range(4)]
    for n in ("ev_w_in", "ev_w_out", "od_w_out"):
        for kind, a in enumerate(adamw(received[n], loc[n], mom[n], var[n], name="adamw_" + n)):
            big_out[kind][n] = a
    for kind, a in enumerate(adamw(received["od_w_in"], to_t(loc["od_w_in"]), to_t(mom["od_w_in"]),
                                   to_t(var["od_w_in"]), name="adamw_od_w_in")):
        big_out[kind]["od_w_in"] = a[:w1_rows].T

    summed = _unpack_rows(sum_slots(gsmall_recv, name="sum_small_grads"), gsmall_spans)
    loss_total = summed[-1][0, 0]
    gsmall = dict(zip(SMALL, summed[:-1]))
    for n in SMALL_SHARDED:
        width = loc[n].shape[1]
        gsmall[n] = lax.dynamic_slice_in_dim(gsmall[n], me * width, width, axis=1)
    gs, sspans = _pack_rows([gsmall[n] for n in SMALL], LANE, 8)
    ws, _ = _pack_rows([loc[n] for n in SMALL], LANE, 8)
    ms, _ = _pack_rows([mom[n] for n in SMALL], LANE, 8)
    vs, _ = _pack_rows([var[n] for n in SMALL], LANE, 8)
    small_out = [dict(zip(SMALL, _unpack_rows(a, sspans))) for a in adamw(gs[None], ws, ms, vs, name="adamw_small")]

    outs = [loss_total, dx.reshape(B, seq, D)]
    for kind in range(4):
        for n in ORDER:
            src = big_out[kind] if n in BIG else small_out[kind]
            outs.append(src[n].reshape(shapes[n]))
    return tuple(outs)
```

```python
import jax
import jax.numpy as jnp
from jax import lax
from jax.experimental import pallas as pl
from jax.experimental.pallas import tpu as pltpu

F32 = jnp.float32
BF16 = jnp.bfloat16

EPS = 1e-6
N_DEV = 8
LANE = 128
VMEM_LIMIT_BYTES = 48 * 1024 * 1024

SB_HEAD_DIM = 128
SSM_HEAD_DIM = 64
SSM_GROUPS = 4
SSM_STATE = 128
SSM_CHUNK = 128
HALO = 32
HEAD_ROWS = 8
NEG_BIG = -1e30

ADAM_LR = 0.001
ADAM_B1 = 0.9
ADAM_B2 = 0.999
ADAM_EPS = 1e-08
ADAM_WD = 0.01
ADAM_STEP = 10

NT = (((1,), (1,)), ((), ()))
TN = (((0,), (0,)), ((), ()))


def _cp(*sem):
    return pltpu.CompilerParams(dimension_semantics=sem, vmem_limit_bytes=VMEM_LIMIT_BYTES)


def _pick(n, cap, align):
    if n <= cap:
        return n
    t = (cap // align) * align
    while t >= align:
        if n % t == 0:
            return t
        t -= align
    raise ValueError(f"no tile for {n} (cap {cap}, align {align})")


def _sigmoid(x):
    return 0.5 * jnp.tanh(0.5 * x) + 0.5


def _silu(x):
    return x * _sigmoid(x)


def _silu_grad(x):
    s = _sigmoid(x)
    return s * (1.0 + x * (1.0 - s))


def _dot(a, b, dims=None):
    if dims is None:
        return jnp.dot(a, b, preferred_element_type=F32)
    return lax.dot_general(a, b, dims, preferred_element_type=F32)


def _tri_dot3(tri, x):
    hi = x.astype(BF16)
    r1 = x - hi.astype(F32)
    mid = r1.astype(BF16)
    lo = (r1 - mid.astype(F32)).astype(BF16)
    return _dot(tri, hi) + _dot(tri, mid) + _dot(tri, lo)


def _grid_step(n_inner):
    return lambda: pl.program_id(0) * n_inner + pl.program_id(1)


def mm_nn(a, b, *, add=None, out_dtype, name, comm=None):
    M, K = a.shape
    N = b.shape[1]
    tm = _pick(M, 2048 if K <= 1024 and add is None else 1024, 16)
    tn = _pick(N, 1024, LANE)

    def body(*refs):
        if add is None:
            a_ref, b_ref, o_ref = refs
        else:
            a_ref, b_ref, add_ref, o_ref = refs
        acc = _dot(a_ref[...], b_ref[...])
        if add is not None:
            acc = acc + add_ref[...]
        o_ref[...] = acc.astype(out_dtype)

    in_specs = [pl.BlockSpec((tm, K), lambda i, j: (i, 0)), pl.BlockSpec((K, tn), lambda i, j: (0, j))]
    args = [a, b]
    if add is not None:
        in_specs.append(pl.BlockSpec((tm, tn), lambda i, j: (i, j)))
        args.append(add)
    grid = (M // tm, N // tn)
    body, c_in, c_args, c_out, c_shape, c_scratch = _hosted(body, len(args), 1, comm, _grid_step(grid[1]),
                                                            grid[0] * grid[1])
    out = pl.pallas_call(
        body, name=name, grid=grid, in_specs=in_specs + c_in,
        out_specs=[pl.BlockSpec((tm, tn), lambda i, j: (i, j))] + c_out,
        out_shape=[jax.ShapeDtypeStruct((M, N), out_dtype)] + c_shape, scratch_shapes=c_scratch,
        compiler_params=_cp(*(("arbitrary",) * 2 if comm else ("parallel",) * 2)))(*args, *c_args)
    return (out[0], out[1:]) if comm else out[0]


def mm_nt_terms(terms, b, *, out_dtype, name, comm=None):
    M = terms[0][0].shape[0]
    N = b.shape[0]
    n_terms = len(terms)
    if n_terms == 1:
        tm, tn = _pick(M, 2048, 16), _pick(N, 1024, LANE)
    else:
        tm, tn = _pick(M, 256, 16), _pick(N, 1024, LANE)

    def body(*refs):
        o_ref = refs[-1]
        acc = None
        for t in range(n_terms):
            part = _dot(refs[2 * t][...], refs[2 * t + 1][...], NT)
            acc = part if acc is None else acc + part
        o_ref[...] = acc.astype(out_dtype)

    in_specs, args = [], []
    for arr, cb, w, off in terms:
        assert off % w == 0
        in_specs.append(pl.BlockSpec((tm, w), lambda i, j, cb=cb: (i, cb)))
        in_specs.append(pl.BlockSpec((tn, w), lambda i, j, ob=off // w: (j, ob)))
        args += [arr, b]
    grid = (M // tm, N // tn)
    body, c_in, c_args, c_out, c_shape, c_scratch = _hosted(body, len(args), 1, comm, _grid_step(grid[1]),
                                                            grid[0] * grid[1])
    out = pl.pallas_call(
        body, name=name, grid=grid, in_specs=in_specs + c_in,
        out_specs=[pl.BlockSpec((tm, tn), lambda i, j: (i, j))] + c_out,
        out_shape=[jax.ShapeDtypeStruct((M, N), out_dtype)] + c_shape, scratch_shapes=c_scratch,
        compiler_params=_cp(*(("arbitrary",) * 2 if comm else ("parallel",) * 2)))(*args, *c_args)
    return (out[0], out[1:]) if comm else out[0]


def mm_nn_terms(terms, b, *, out_dtype, name):
    M = terms[0][0].shape[0]
    N = b.shape[1]
    tm = _pick(M, 256, 16)
    tn = _pick(N, 1024, LANE)
    n_terms = len(terms)

    def body(*refs):
        o_ref = refs[-1]
        acc = None
        for t in range(n_terms):
            part = _dot(refs[2 * t][...], refs[2 * t + 1][...])
            acc = part if acc is None else acc + part
        o_ref[...] = acc.astype(out_dtype)

    in_specs, args = [], []
    for arr, cb, w, off in terms:
        assert off % w == 0
        in_specs.append(pl.BlockSpec((tm, w), lambda i, j, cb=cb: (i, cb)))
        in_specs.append(pl.BlockSpec((w, tn), lambda i, j, ob=off // w: (ob, j)))
        args += [arr, b]
    return pl.pallas_call(
        body, name=name, grid=(M // tm, N // tn), in_specs=in_specs,
        out_specs=pl.BlockSpec((tm, tn), lambda i, j: (i, j)),
        out_shape=jax.ShapeDtypeStruct((M, N), out_dtype),
        compiler_params=_cp("parallel", "parallel"))(*args)


def mm_tn(a, b, *, out_dtype, name):
    T, M = a.shape
    N = b.shape[1]
    tm = _pick(M, 1024, LANE)
    tn = _pick(N, 1024, LANE)
    tk = _pick(T, 2048, 16)
    nk = T // tk

    def body(a_ref, b_ref, o_ref, acc_ref):
        k = pl.program_id(2)

        @pl.when(k == 0)
        def _():
            acc_ref[...] = jnp.zeros_like(acc_ref)

        acc_ref[...] += _dot(a_ref[...], b_ref[...], TN)

        @pl.when(k == nk - 1)
        def _():
            o_ref[...] = acc_ref[...].astype(out_dtype)

    return pl.pallas_call(
        body, name=name, grid=(M // tm, N // tn, nk),
        in_specs=[pl.BlockSpec((tk, tm), lambda i, j, k: (k, i)), pl.BlockSpec((tk, tn), lambda i, j, k: (k, j))],
        out_specs=pl.BlockSpec((tm, tn), lambda i, j, k: (i, j)),
        out_shape=jax.ShapeDtypeStruct((M, N), out_dtype),
        scratch_shapes=[pltpu.VMEM((tm, tn), F32)],
        compiler_params=_cp("parallel", "parallel", "arbitrary"))(a, b)


def rmsnorm_fwd(h, w, *, name, comm=None):
    T, D = h.shape
    tt = _pick(T, 512, 16)

    def body(h_ref, w_ref, n_ref):
        x = h_ref[...]
        r = lax.rsqrt(jnp.mean(x * x, axis=-1, keepdims=True) + EPS)
        n_ref[...] = (x * r * w_ref[...]).astype(BF16)

    body, c_in, c_args, c_out, c_shape, c_scratch = _hosted(body, 2, 1, comm, lambda: pl.program_id(0), T // tt)
    out = pl.pallas_call(
        body, name=name, grid=(T // tt,),
        in_specs=[pl.BlockSpec((tt, D), lambda i: (i, 0)), pl.BlockSpec((1, D), lambda i: (0, 0))] + c_in,
        out_specs=[pl.BlockSpec((tt, D), lambda i: (i, 0))] + c_out,
        out_shape=[jax.ShapeDtypeStruct((T, D), BF16)] + c_shape, scratch_shapes=c_scratch,
        compiler_params=_cp("arbitrary" if comm else "parallel"))(h, w, *c_args)
    return (out[0], out[1:]) if comm else out[0]


def rmsnorm_bwd(h, w, dn, dres, *, name, comm=None):
    T, D = h.shape
    tt = _pick(T, 512, 16)

    def body(h_ref, w_ref, dn_ref, dres_ref, dh_ref, dhb_ref, gw_ref):
        @pl.when(pl.program_id(0) == 0)
        def _():
            gw_ref[...] = jnp.zeros_like(gw_ref)

        x = h_ref[...]
        r = lax.rsqrt(jnp.mean(x * x, axis=-1, keepdims=True) + EPS)
        xhat = x * r
        g = dn_ref[...].astype(F32)
        gw_ref[...] += jnp.sum(g * xhat, axis=0, keepdims=True)
        dxh = g * w_ref[...]
        dx = r * (dxh - xhat * jnp.mean(dxh * xhat, axis=-1, keepdims=True))
        dh = dres_ref[...] + dx
        dh_ref[...] = dh
        dhb_ref[...] = dh.astype(BF16)

    row = pl.BlockSpec((tt, D), lambda i: (i, 0))
    vec = pl.BlockSpec((1, D), lambda i: (0, 0))
    body, c_in, c_args, c_out, c_shape, c_scratch = _hosted(body, 4, 3, comm, lambda: pl.program_id(0), T // tt)
    out = pl.pallas_call(
        body, name=name, grid=(T // tt,), in_specs=[row, vec, row, row] + c_in, out_specs=[row, row, vec] + c_out,
        out_shape=[jax.ShapeDtypeStruct((T, D), F32), jax.ShapeDtypeStruct((T, D), BF16),
                   jax.ShapeDtypeStruct((1, D), F32)] + c_shape,
        scratch_shapes=c_scratch, compiler_params=_cp("arbitrary"))(h, w, dn, dres, *c_args)
    return (out[0], out[1], out[2], out[3:]) if comm else out


def final_loss(h, w, target, *, name):
    T, D = h.shape
    tt = _pick(T, 512, 16)

    def body(h_ref, w_ref, t_ref, loss_ref, dh_ref, dhb_ref, gw_ref):
        @pl.when(pl.program_id(0) == 0)
        def _():
            gw_ref[...] = jnp.zeros_like(gw_ref)
            loss_ref[...] = jnp.zeros_like(loss_ref)

        x = h_ref[...]
        r = lax.rsqrt(jnp.mean(x * x, axis=-1, keepdims=True) + EPS)
        xhat = x * r
        e = xhat * w_ref[...] - t_ref[...]
        loss_ref[...] += jnp.sum(e * e) * (0.5 / D)
        g = e * (1.0 / D)
        gw_ref[...] += jnp.sum(g * xhat, axis=0, keepdims=True)
        dxh = g * w_ref[...]
        dh = r * (dxh - xhat * jnp.mean(dxh * xhat, axis=-1, keepdims=True))
        dh_ref[...] = dh
        dhb_ref[...] = dh.astype(BF16)

    row = pl.BlockSpec((tt, D), lambda i: (i, 0))
    vec = pl.BlockSpec((1, D), lambda i: (0, 0))
    one = pl.BlockSpec((1, LANE), lambda i: (0, 0))
    return pl.pallas_call(
        body, name=name, grid=(T // tt,), in_specs=[row, vec, row], out_specs=[one, row, row, vec],
        out_shape=[jax.ShapeDtypeStruct((1, LANE), F32), jax.ShapeDtypeStruct((T, D), F32),
                   jax.ShapeDtypeStruct((T, D), BF16), jax.ShapeDtypeStruct((1, D), F32)],
        compiler_params=_cp("arbitrary"))(h, w, target)


CONV_CHUNK = 32
ROW_CHUNK = 16
SUBLANES = 8


def _conv_tiles(seq, C, K):
    return _pick(seq, 1024 if K <= SUBLANES else 512, HALO), _pick(C, 512, LANE)


def _residues(offsets):
    return sorted({s % SUBLANES for s in offsets} - {0})


def _fill_shifted(buf, shifted, residues):
    n = buf.shape[0] - SUBLANES
    for i, r in enumerate(residues):
        shifted[i, 0:n, :] = buf[r:r + n, :]


def _tap(buf, shifted, residues, offset, start, rows):
    r = offset % SUBLANES
    base = offset - r
    ref = buf if r == 0 else shifted.at[residues.index(r)]
    return ref[pl.ds(start + base, rows), :]


def dwconv_fwd(src, offs, w, b, *, C, seq, glu, silu_out, name, comm=None):
    T = src.shape[0]
    K = w.shape[0]
    assert K - 1 <= HALO
    tt, tc = _conv_tiles(seq, C, K)
    n_in = 2 if glu else 1
    per = tt // HALO
    offsets = [HALO - (K - 1) + k for k in range(K)]
    residues = _residues(offsets)

    def body(*refs):
        cur = refs[0:2 * n_in:2]
        halo = refs[1:2 * n_in:2]
        w_ref, b_ref = refs[2 * n_in], refs[2 * n_in + 1]
        outs = refs[2 * n_in + 2:-2]
        buf, shifted = refs[-2], refs[-1]
        i = pl.program_id(1)
        first = (i * tt) % seq == 0

        def pre(rs, rows):
            v = rs[0][rows, :].astype(F32)
            return v * _sigmoid(rs[1][rows, :].astype(F32)) if glu else v

        def build(ci, carry):
            start = pl.multiple_of(ci * CONV_CHUNK, CONV_CHUNK)
            buf[pl.ds(HALO + start, CONV_CHUNK), :] = pre(cur, pl.ds(start, CONV_CHUNK))
            return carry

        buf[0:HALO, :] = jnp.where(first, 0.0, pre(halo, slice(None)))
        lax.fori_loop(0, tt // CONV_CHUNK, build, 0, unroll=2)
        _fill_shifted(buf, shifted, residues)

        def chunk(ci, carry):
            start = pl.multiple_of(ci * CONV_CHUNK, CONV_CHUNK)
            acc = jnp.broadcast_to(b_ref[...], (CONV_CHUNK, tc))
            for k in range(K):
                acc = acc + w_ref[k:k + 1, :] * _tap(buf, shifted, residues, offsets[k], start, CONV_CHUNK)
            outs[0][pl.ds(start, CONV_CHUNK), :] = acc.astype(BF16)
            if silu_out:
                outs[1][pl.ds(start, CONV_CHUNK), :] = _silu(acc).astype(BF16)
            return carry

        lax.fori_loop(0, tt // CONV_CHUNK, chunk, 0)

    in_specs, args = [], []
    for off in offs:
        assert off % tc == 0
        in_specs.append(pl.BlockSpec((tt, tc), lambda j, i, ob=off // tc: (i, ob + j)))
        in_specs.append(pl.BlockSpec((HALO, tc), lambda j, i, ob=off // tc: (jnp.maximum(i * per - 1, 0), ob + j)))
        args += [src, src]
    in_specs += [pl.BlockSpec((K, tc), lambda j, i: (0, j)), pl.BlockSpec((1, tc), lambda j, i: (0, j))]
    args += [w, b]
    n_out = 2 if silu_out else 1
    grid = (C // tc, T // tt)
    body, c_in, c_args, c_out, c_shape, c_scratch = _hosted(body, len(args), n_out, comm, _grid_step(grid[1]),
                                                            grid[0] * grid[1])
    out = pl.pallas_call(
        body, name=name, grid=grid, in_specs=in_specs + c_in,
        out_specs=[pl.BlockSpec((tt, tc), lambda j, i: (i, j))] * n_out + c_out,
        out_shape=[jax.ShapeDtypeStruct((T, C), BF16)] * n_out + c_shape,
        scratch_shapes=[pltpu.VMEM((HALO + tt, tc), F32), pltpu.VMEM((max(len(residues), 1), HALO + tt, tc), F32)]
        + c_scratch,
        compiler_params=_cp("arbitrary" if comm else "parallel", "arbitrary"))(*args, *c_args)
    return (out[:n_out], out[n_out:]) if comm else out


def dwconv_bwd(du, u, src, offs, w, *, C, seq, glu, silu_out, name):
    T = src.shape[0]
    K = w.shape[0]
    assert K - 1 <= HALO
    tt, tc = _conv_tiles(seq, C, K)
    n_in = 2 if glu else 1
    per = tt // HALO
    last_blk = T // HALO - 1
    g_offsets = [K - 1 - k for k in range(K)]
    g_res = _residues(g_offsets)

    def body(*refs):
        pos = 0
        du_cur, du_nxt = refs[0], refs[1]
        pos = 2
        if silu_out:
            u_cur, u_nxt = refs[2], refs[3]
            pos = 4
        cur = refs[pos:pos + n_in]
        pos += n_in
        w_ref = refs[pos]
        outs = refs[pos + 1:pos + 1 + n_in]
        dw_ref, db_ref = refs[pos + 1 + n_in], refs[pos + 2 + n_in]
        gbuf, gshift, dw_acc, db_acc = refs[-4:]
        i = pl.program_id(1)
        last = ((i + 1) * tt) % seq == 0

        @pl.when(i == 0)
        def _():
            dw_acc[...] = jnp.zeros_like(dw_acc)
            db_acc[...] = jnp.zeros_like(db_acc)

        def build(ci, carry):
            rows = pl.ds(pl.multiple_of(ci * CONV_CHUNK, CONV_CHUNK), CONV_CHUNK)
            g = du_cur[rows, :].astype(F32)
            if silu_out:
                g = g * _silu_grad(u_cur[rows, :].astype(F32))
            gbuf[rows, :] = g
            return carry

        lax.fori_loop(0, tt // CONV_CHUNK, build, 0, unroll=2)
        g_nxt = du_nxt[...].astype(F32)
        if silu_out:
            g_nxt = g_nxt * _silu_grad(u_nxt[...].astype(F32))
        gbuf[tt:tt + HALO, :] = jnp.where(last, 0.0, g_nxt)
        _fill_shifted(gbuf, gshift, g_res)

        def fold(v):
            out = v[0:SUBLANES]
            for s in range(SUBLANES, CONV_CHUNK, SUBLANES):
                out = out + v[s:s + SUBLANES]
            return out

        def chunk(ci, carry):
            start = pl.multiple_of(ci * CONV_CHUNK, CONV_CHUNK)
            rows = pl.ds(start, CONV_CHUNK)
            a = cur[0][rows, :].astype(F32)
            if glu:
                s = _sigmoid(cur[1][rows, :].astype(F32))
                x_in = a * s
            else:
                x_in = a
            dx = jnp.zeros((CONV_CHUNK, tc), F32)
            for k in range(K):
                g_k = _tap(gbuf, gshift, g_res, g_offsets[k], start, CONV_CHUNK)
                dx = dx + w_ref[k:k + 1, :] * g_k
                dw_acc[k * SUBLANES:(k + 1) * SUBLANES, :] += fold(g_k * x_in)
            db_acc[...] += fold(gbuf[rows, :])
            if glu:
                outs[0][rows, :] = (dx * s).astype(BF16)
                outs[1][rows, :] = (dx * a * s * (1.0 - s)).astype(BF16)
            else:
                outs[0][rows, :] = dx.astype(BF16)
            return carry

        lax.fori_loop(0, tt // CONV_CHUNK, chunk, 0)

        @pl.when(i == T // tt - 1)
        def _():
            for k in range(K):
                dw_ref[k:k + 1, :] = jnp.sum(dw_acc[k * SUBLANES:(k + 1) * SUBLANES, :], axis=0, keepdims=True)
            db_ref[...] = jnp.sum(db_acc[...], axis=0, keepdims=True)

    def cur_spec(ob):
        return pl.BlockSpec((tt, tc), lambda j, i: (i, ob + j))

    def nxt_spec(ob):
        return pl.BlockSpec((HALO, tc), lambda j, i: (jnp.minimum((i + 1) * per, last_blk), ob + j))

    in_specs = [cur_spec(0), nxt_spec(0)]
    args = [du, du]
    if silu_out:
        in_specs += [cur_spec(0), nxt_spec(0)]
        args += [u, u]
    for off in offs:
        assert off % tc == 0
        in_specs.append(cur_spec(off // tc))
        args.append(src)
    in_specs.append(pl.BlockSpec((K, tc), lambda j, i: (0, j)))
    args.append(w)
    out_specs = [pl.BlockSpec((tt, tc), lambda j, i: (i, j))] * n_in
    out_specs += [pl.BlockSpec((K, tc), lambda j, i: (0, j)), pl.BlockSpec((1, tc), lambda j, i: (0, j))]
    out_shape = [jax.ShapeDtypeStruct((T, C), BF16)] * n_in
    out_shape += [jax.ShapeDtypeStruct((K, C), F32), jax.ShapeDtypeStruct((1, C), F32)]
    return pl.pallas_call(
        body, name=name, grid=(C // tc, T // tt), in_specs=in_specs, out_specs=out_specs, out_shape=out_shape,
        scratch_shapes=[pltpu.VMEM((tt + HALO, tc), F32), pltpu.VMEM((max(len(g_res), 1), tt + HALO, tc), F32),
                        pltpu.VMEM((K * SUBLANES, tc), F32), pltpu.VMEM((SUBLANES, tc), F32)],
        compiler_params=_cp("parallel", "arbitrary"))(*args)


def mix0_post_fwd(u2, proj, o, ln_w, ln_b, *, CW, gc_off, ga_off, name):
    T = u2.shape[0]
    tt = _pick(T, 256, 16)

    def body(u_ref, gc_ref, ga_ref, o_ref, lw_ref, lb_ref, y_ref):
        def chunk(ci, carry):
            rows = pl.ds(pl.multiple_of(ci * ROW_CHUNK, ROW_CHUNK), ROW_CHUNK)
            u = u_ref[rows, :].astype(F32)
            mu = jnp.mean(u, axis=-1, keepdims=True)
            xc = u - mu
            r = lax.rsqrt(jnp.mean(xc * xc, axis=-1, keepdims=True) + EPS)
            u3 = xc * r * lw_ref[...] + lb_ref[...]
            y_ref[rows, 0:CW] = (_silu(u3) * _silu(gc_ref[rows, :].astype(F32))).astype(BF16)
            y_ref[rows, CW:2 * CW] = (o_ref[rows, :].astype(F32) * _silu(ga_ref[rows, :].astype(F32))).astype(BF16)
            return carry

        lax.fori_loop(0, tt // ROW_CHUNK, chunk, 0, unroll=4)

    row = pl.BlockSpec((tt, CW), lambda i: (i, 0))
    vec = pl.BlockSpec((1, CW), lambda i: (0, 0))
    return pl.pallas_call(
        body, name=name, grid=(T // tt,),
        in_specs=[row, pl.BlockSpec((tt, CW), lambda i: (i, gc_off // CW)),
                  pl.BlockSpec((tt, CW), lambda i: (i, ga_off // CW)), row, vec, vec],
        out_specs=pl.BlockSpec((tt, 2 * CW), lambda i: (i, 0)),
        out_shape=jax.ShapeDtypeStruct((T, 2 * CW), BF16),
        compiler_params=_cp("parallel"))(u2, proj, proj, o, ln_w, ln_b)


def mix0_post_bwd(dy, u2, proj, o, ln_w, ln_b, *, CW, gc_off, ga_off, name):
    T = u2.shape[0]
    tt = _pick(T, 256, 16)

    def body(dy_ref, u_ref, gc_ref, ga_ref, o_ref, lw_ref, lb_ref, du_ref, dgc_ref, dga_ref, do_ref, dlw_ref, dlb_ref,
             lw_acc, lb_acc):
        i = pl.program_id(0)

        @pl.when(i == 0)
        def _():
            lw_acc[...] = jnp.zeros_like(lw_acc)
            lb_acc[...] = jnp.zeros_like(lb_acc)

        def fold(v):
            out = v[0:SUBLANES]
            for s in range(SUBLANES, ROW_CHUNK, SUBLANES):
                out = out + v[s:s + SUBLANES]
            return out

        def chunk(ci, carry):
            rows = pl.ds(pl.multiple_of(ci * ROW_CHUNK, ROW_CHUNK), ROW_CHUNK)
            dyc = dy_ref[rows, 0:CW].astype(F32)
            dya = dy_ref[rows, CW:2 * CW].astype(F32)
            u = u_ref[rows, :].astype(F32)
            mu = jnp.mean(u, axis=-1, keepdims=True)
            xc = u - mu
            r = lax.rsqrt(jnp.mean(xc * xc, axis=-1, keepdims=True) + EPS)
            xhat = xc * r
            u3 = xhat * lw_ref[...] + lb_ref[...]
            gc = gc_ref[rows, :].astype(F32)
            dgc_ref[rows, :] = (dyc * _silu(u3) * _silu_grad(gc)).astype(BF16)
            du3 = dyc * _silu(gc) * _silu_grad(u3)
            lw_acc[...] += fold(du3 * xhat)
            lb_acc[...] += fold(du3)
            dxh = du3 * lw_ref[...]
            du = r * (dxh - jnp.mean(dxh, axis=-1, keepdims=True)
                      - xhat * jnp.mean(dxh * xhat, axis=-1, keepdims=True))
            du_ref[rows, :] = du.astype(BF16)
            ga = ga_ref[rows, :].astype(F32)
            ov = o_ref[rows, :].astype(F32)
            do_ref[rows, :] = (dya * _silu(ga)).astype(BF16)
            dga_ref[rows, :] = (dya * ov * _silu_grad(ga)).astype(BF16)
            return carry

        lax.fori_loop(0, tt // ROW_CHUNK, chunk, 0, unroll=4)

        @pl.when(i == T // tt - 1)
        def _():
            dlw_ref[...] = jnp.sum(lw_acc[...], axis=0, keepdims=True)
            dlb_ref[...] = jnp.sum(lb_acc[...], axis=0, keepdims=True)

    row = pl.BlockSpec((tt, CW), lambda i: (i, 0))
    vec = pl.BlockSpec((1, CW), lambda i: (0, 0))
    big = jax.ShapeDtypeStruct((T, CW), BF16)
    small = jax.ShapeDtypeStruct((1, CW), F32)
    return pl.pallas_call(
        body, name=name, grid=(T // tt,),
        in_specs=[pl.BlockSpec((tt, 2 * CW), lambda i: (i, 0)), row,
                  pl.BlockSpec((tt, CW), lambda i: (i, gc_off // CW)),
                  pl.BlockSpec((tt, CW), lambda i: (i, ga_off // CW)), row, vec, vec],
        out_specs=[row, row, row, row, vec, vec],
        out_shape=[big, big, big, big, small, small],
        scratch_shapes=[pltpu.VMEM((SUBLANES, CW), F32), pltpu.VMEM((SUBLANES, CW), F32)],
        compiler_params=_cp("arbitrary"))(dy, u2, proj, proj, o, ln_w, ln_b)


SB_UNDERFLOW = 110.0
SB_BOUND_MARGIN = 1.02


def _sb_tile(seq):
    return _pick(seq, 256, LANE)


def _softplus(z):
    return jnp.maximum(z, 0.0) + jnp.log(1.0 + jnp.exp(-jnp.abs(z)))


def _tri01(n, lower):
    i = lax.broadcasted_iota(jnp.int32, (n, n), 0)
    j = lax.broadcasted_iota(jnp.int32, (n, n), 1)
    return ((i >= j) if lower else (i <= j)).astype(BF16)


SB_HEADS_FWD = 8
SB_HEADS_BWD = 2


def _sb_heads_per_step(heads, want):
    while heads % want:
        want //= 2
    return want


def sba_fwd(proj, *, B, seq, heads, q_off, k_off, v_off, name):
    dh = SB_HEAD_DIM
    tq = _sb_tile(seq)
    assert tq % (2 * LANE) == 0
    nq = seq // tq
    hps = _sb_heads_per_step(heads, SB_HEADS_FWD)
    hw = hps * dh
    scale = dh ** -0.5

    def body(q_ref, k_ref, v_ref, tri_ref, o_ref, ct_ref, acc_ref, kmax_ref):
        qi = pl.program_id(1)
        tri = tri_ref[...]
        qs = [(q_ref[:, h * dh:(h + 1) * dh].astype(F32) * scale).astype(BF16) for h in range(hps)]

        @pl.when(qi == 0)
        def _():
            def chunk(i, best):
                rows = k_ref[pl.ds(pl.multiple_of(i * tq, tq), tq), :].astype(F32)
                sq = rows * rows
                return tuple(jnp.maximum(best[h], jnp.max(jnp.sum(sq[:, h * dh:(h + 1) * dh], axis=1, keepdims=True),
                                                          axis=0, keepdims=True)) for h in range(hps))

            best = lax.fori_loop(0, nq, chunk, (jnp.zeros((1, 1), F32),) * hps)
            for h in range(hps):
                kmax_ref[h] = jnp.broadcast_to(jnp.sqrt(best[h]), (8, LANE))

        z_bound = [jnp.sqrt(jnp.sum(qs[h].astype(F32) ** 2, axis=1, keepdims=True))
                   * (SB_BOUND_MARGIN * jnp.max(kmax_ref[h], keepdims=True)) for h in range(hps)]

        def part(h, q_rows, start, n_keys, r, mask):
            k_blk = k_ref[pl.ds(start, n_keys), h * dh:(h + 1) * dh]
            v_blk = v_ref[pl.ds(start, n_keys), h * dh:(h + 1) * dh]
            z = _dot(q_rows, k_blk, NT)
            sp = _softplus(z)
            if mask is not None:
                sp = jnp.where(mask, sp, 0.0)
            wts = jnp.exp(z - (_dot(sp.astype(BF16), tri[0:n_keys, 0:n_keys]) + r))
            if mask is not None:
                wts = jnp.where(mask, wts, 0.0)
            return _dot(wts.astype(BF16), v_blk), r + jnp.sum(sp, axis=-1, keepdims=True)

        below = lax.broadcasted_iota(jnp.int32, (tq, tq), 1) < lax.broadcasted_iota(jnp.int32, (tq, tq), 0)
        has_left = qi > 0
        left = pl.multiple_of(jnp.maximum(qi - 1, 0) * tq, tq)
        rs = []
        for h in range(hps):
            pv_d, r = part(h, qs[h], pl.multiple_of(qi * tq, tq), tq, jnp.zeros((tq, 1), F32), below)
            pv_l, r = part(h, qs[h], left, tq, r, has_left)
            acc_ref[:, h * dh:(h + 1) * dh] = pv_d + pv_l
            rs.append(r)
        rs = tuple(rs)

        def block(start, rs):
            pvs, out = [], []
            for h in range(hps):
                pv, r = part(h, qs[h], start, tq, rs[h], None)
                pvs.append(pv)
                out.append(r)
            return pvs, tuple(out)

        def more(c):
            j, rs = c
            slack = rs[0] - z_bound[0]
            for h in range(1, hps):
                slack = jnp.minimum(slack, rs[h] - z_bound[h])
            return jnp.logical_and(j < qi, jnp.min(slack) <= SB_UNDERFLOW)

        def step(c):
            j, rs = c
            pvs, rs = block(pl.multiple_of((qi - 1 - j) * tq, tq), rs)
            for h in range(hps):
                acc_ref[:, h * dh:(h + 1) * dh] += pvs[h]
            return j + 1, rs

        n_left, totals = lax.while_loop(more, step, (has_left.astype(jnp.int32), rs))
        o_ref[...] = acc_ref[...].astype(BF16)
        for h in range(hps):
            ct_ref[0, 0, h, 0:8, :] = jnp.broadcast_to(totals[h], (tq, LANE)).T[0:8, :]
            ct_ref[0, 0, h, 8:16, :] = jnp.full((8, tq), n_left, F32)

    qb, kb, vb = q_off // hw, k_off // hw, v_off // hw
    G = heads // hps
    return pl.pallas_call(
        body, name=name, grid=(B * G, nq),
        in_specs=[pl.BlockSpec((tq, hw), lambda g, i: ((g // G) * nq + i, qb + g % G)),
                  pl.BlockSpec((seq, hw), lambda g, i: (g // G, kb + g % G)),
                  pl.BlockSpec((seq, hw), lambda g, i: (g // G, vb + g % G)),
                  pl.BlockSpec((tq, tq), lambda g, i: (0, 0))],
        out_specs=[pl.BlockSpec((tq, hw), lambda g, i: ((g // G) * nq + i, g % G)),
                   pl.BlockSpec((1, 1, hps, 16, tq), lambda g, i: (g // G, i, g % G, 0, 0))],
        out_shape=[jax.ShapeDtypeStruct((B * seq, heads * dh), BF16),
                   jax.ShapeDtypeStruct((B, nq, heads, 16, tq), F32)],
        scratch_shapes=[pltpu.VMEM((tq, hw), F32), pltpu.VMEM((hps, 8, LANE), F32)],
        compiler_params=_cp("parallel", "arbitrary"))(proj, proj, proj, jnp.tril(jnp.ones((tq, tq), BF16)))


def sba_bwd(proj, ctot, do, *, B, seq, heads, q_off, k_off, v_off, name, comm=None):
    dh = SB_HEAD_DIM
    tq = _sb_tile(seq)
    nq = seq // tq
    hps = _sb_heads_per_step(heads, SB_HEADS_BWD)
    hw = hps * dh
    scale = dh ** -0.5

    def body(q_ref, k_ref, v_ref, ct_ref, do_ref, sfx_ref, pre_ref, dq_ref, dk_ref, dv_ref, dq_acc, dk_acc, dv_acc):
        qi = pl.program_id(1)
        tri_sfx = sfx_ref[...]
        tri_pre = pre_ref[...]
        qs = [(q_ref[:, h * dh:(h + 1) * dh].astype(F32) * scale).astype(BF16) for h in range(hps)]
        dos = [do_ref[:, h * dh:(h + 1) * dh] for h in range(hps)]
        totals = [jnp.max(jnp.broadcast_to(ct_ref[0, 0, h, 0:1, :], (LANE, tq)).T, axis=1, keepdims=True)
                  for h in range(hps)]
        dq_acc[...] = jnp.zeros_like(dq_acc)

        def part(h, rows, start, n_keys, pc, pg, mask):
            cols = slice(h * dh, (h + 1) * dh)
            q_rows, do_rows = qs[h][rows], dos[h][rows]
            k_blk = k_ref[pl.ds(start, n_keys), cols]
            v_blk = v_ref[pl.ds(start, n_keys), cols]
            z = _dot(q_rows, k_blk, NT)
            sp = _softplus(z)
            sig = jnp.exp(z - sp)
            if mask is not None:
                sp = jnp.where(mask, sp, 0.0)
            pc_next = pc + jnp.sum(sp, axis=-1, keepdims=True)
            wts = jnp.exp(z - (_dot(sp.astype(BF16), tri_sfx[0:n_keys, 0:n_keys]) + (totals[h][rows] - pc_next)))
            if mask is not None:
                wts = jnp.where(mask, wts, 0.0)
            g = _dot(do_rows, v_blk, NT) * wts
            dz = g - sig * (_dot(g.astype(BF16), tri_pre[0:n_keys, 0:n_keys]) + pg)
            if mask is not None:
                dz = jnp.where(mask, dz, 0.0)
            dz = dz.astype(BF16)
            grads = (_dot(dz, k_blk), _dot(dz, q_rows, TN), _dot(wts.astype(BF16), do_rows, TN))
            return pc_next, pg + jnp.sum(g, axis=-1, keepdims=True), grads

        def block(start, carry):
            out = []
            for h in range(hps):
                cols = slice(h * dh, (h + 1) * dh)
                pc, pg, (dq, dk, dv) = part(h, slice(0, tq), start, tq, carry[h][0], carry[h][1], None)
                dq_acc[:, cols] += dq
                dk_acc[pl.ds(start, tq), cols] += dk
                dv_acc[pl.ds(start, tq), cols] += dv
                out.append((pc, pg))
            return tuple(out)

        zero = jnp.zeros((tq, 1), F32)
        n_left = jnp.max(ct_ref[0, 0, 0, 8:16, :]).astype(jnp.int32)
        carry = lax.fori_loop(qi - n_left, qi - 1, lambda j, c: block(pl.multiple_of(j * tq, tq), c),
                              ((zero, zero),) * hps)
        below = lax.broadcasted_iota(jnp.int32, (tq, tq), 1) < lax.broadcasted_iota(jnp.int32, (tq, tq), 0)
        has_left = n_left > 0
        left = pl.multiple_of(jnp.maximum(qi - 1, 0) * tq, tq)
        diag = pl.multiple_of(qi * tq, tq)
        for h in range(hps):
            cols = slice(h * dh, (h + 1) * dh)
            pc, pg, (dq_l, dk_l, dv_l) = part(h, slice(0, tq), left, tq, carry[h][0], carry[h][1], has_left)
            _, _, (dq_d, dk_d, dv_d) = part(h, slice(0, tq), diag, tq, pc, pg, below)
            dq_ref[:, cols] = ((dq_acc[:, cols] + dq_l + dq_d) * scale).astype(BF16)
            dk_acc[pl.ds(diag, tq), cols] = dk_d
            dv_acc[pl.ds(diag, tq), cols] = dv_d
            dk_acc[pl.ds(left, tq), cols] += dk_l
            dv_acc[pl.ds(left, tq), cols] += dv_l

        @pl.when(qi == nq - 1)
        def _():
            dk_ref[...] = dk_acc[...].astype(BF16)
            dv_ref[...] = dv_acc[...].astype(BF16)

    qb, kb, vb = q_off // hw, k_off // hw, v_off // hw
    G = heads // hps
    q_spec = pl.BlockSpec((tq, hw), lambda g, i: ((g // G) * nq + i, qb + g % G))
    o_spec = pl.BlockSpec((tq, hw), lambda g, i: ((g // G) * nq + i, g % G))
    kv_out = pl.BlockSpec((seq, hw), lambda g, i: (g // G, g % G))
    shp = jax.ShapeDtypeStruct((B * seq, heads * dh), BF16)
    body, c_in, c_args, c_out, c_shape, c_scratch = _hosted(body, 7, 3, comm, _grid_step(nq), B * G * nq)
    tri_spec = pl.BlockSpec((tq, tq), lambda g, i: (0, 0))
    ones = jnp.ones((tq, tq), BF16)
    out = pl.pallas_call(
        body, name=name, grid=(B * G, nq),
        in_specs=[q_spec,
                  pl.BlockSpec((seq, hw), lambda g, i: (g // G, kb + g % G)),
                  pl.BlockSpec((seq, hw), lambda g, i: (g // G, vb + g % G)),
                  pl.BlockSpec((1, 1, hps, 16, tq), lambda g, i: (g // G, i, g % G, 0, 0)), o_spec,
                  tri_spec, tri_spec] + c_in,
        out_specs=[o_spec, kv_out, kv_out] + c_out, out_shape=[shp, shp, shp] + c_shape,
        scratch_shapes=[pltpu.VMEM((tq, hw), F32), pltpu.VMEM((seq, hw), F32), pltpu.VMEM((seq, hw), F32)]
        + c_scratch,
        compiler_params=_cp("arbitrary" if comm else "parallel", "arbitrary"))(
            proj, proj, proj, ctot, do, jnp.tril(ones), jnp.triu(ones), *c_args)
    return (out[0], out[1], out[2], out[3:]) if comm else out


def _head_expand(n_heads, DI):
    j = jnp.arange(LANE, dtype=jnp.int32)[:, None]
    c = jnp.arange(DI, dtype=jnp.int32)[None, :] // SSM_HEAD_DIM
    return ((j == c) & (j < n_heads)).astype(BF16)


def _split3(x):
    hi = x.astype(BF16)
    r1 = x - hi.astype(F32)
    mid = r1.astype(BF16)
    return hi, mid, (r1 - mid.astype(F32)).astype(BF16)


def dt_fwd(proj, bias, a_log, expand, *, dt_off, name):
    T = proj.shape[0]
    DI = expand.shape[1]
    L = SSM_CHUNK
    tt = _pick(T, 512, L)

    def body(raw_ref, bias_ref, al_ref, e_ref, dt_ref, cs_ref, dtx_ref, csx_ref):
        x = raw_ref[...].astype(F32) + bias_ref[...]
        dt = _softplus(x)
        dt_ref[...] = dt
        la = dt * (-jnp.exp(al_ref[...]))
        tri = _tri01(L, True)
        for c in range(tt // L):
            cs_ref[c * L:(c + 1) * L, :] = _tri_dot3(tri, la[c * L:(c + 1) * L, :])
        e = e_ref[...]
        dtx_ref[...] = _dot(dt.astype(BF16), e).astype(BF16)
        hi, mid, lo = _split3(cs_ref[...])
        csx_ref[...] = _dot(hi, e) + _dot(mid, e) + _dot(lo, e)

    row = pl.BlockSpec((tt, LANE), lambda i: (i, 0))
    wide = pl.BlockSpec((tt, DI), lambda i: (i, 0))
    vec = pl.BlockSpec((1, LANE), lambda i: (0, 0))
    return pl.pallas_call(
        body, name=name, grid=(T // tt,),
        in_specs=[pl.BlockSpec((tt, LANE), lambda i: (i, dt_off // LANE)), vec, vec,
                  pl.BlockSpec((LANE, DI), lambda i: (0, 0))],
        out_specs=[row, row, wide, wide],
        out_shape=[jax.ShapeDtypeStruct((T, LANE), F32), jax.ShapeDtypeStruct((T, LANE), F32),
                   jax.ShapeDtypeStruct((T, DI), BF16), jax.ShapeDtypeStruct((T, DI), F32)],
        compiler_params=_cp("parallel"))(proj, bias, a_log, expand)


def dt_bwd(ddt_x, dcs_x, dcs_cols, proj, dt, bias, a_log, reduce_t, *, dt_off, n_heads, name):
    T = proj.shape[0]
    DI = reduce_t.shape[0]
    L = SSM_CHUNK
    tt = _pick(T, 512, L)

    def body(ddtx_ref, dcsx_ref, dcsc_ref, raw_ref, dt_ref, bias_ref, al_ref, r_ref, draw_ref, dbias_ref, dal_ref,
             dla_buf):
        @pl.when(pl.program_id(0) == 0)
        def _():
            dbias_ref[...] = jnp.zeros_like(dbias_ref)
            dal_ref[...] = jnp.zeros_like(dal_ref)

        r = r_ref[...]
        ddt = _dot(ddtx_ref[...], r)
        dx = dcsx_ref[...]
        hi = dx.astype(BF16)
        dcs = _dot(hi, r) + _dot((dx - hi.astype(F32)).astype(BF16), r) + dcsc_ref[...]
        triu = _tri01(L, False)
        for c in range(tt // L):
            dla_buf[c * L:(c + 1) * L, :] = _tri_dot3(triu, dcs[c * L:(c + 1) * L, :])
        dla = dla_buf[...]
        a = -jnp.exp(al_ref[...])
        valid = lax.broadcasted_iota(jnp.int32, (tt, LANE), 1) < n_heads
        dal_ref[...] += jnp.sum(jnp.where(valid, dla * dt_ref[...], 0.0), axis=0, keepdims=True) * a
        x = raw_ref[...].astype(F32) + bias_ref[...]
        draw = jnp.where(valid, (ddt + dla * a) * _sigmoid(x), 0.0)
        dbias_ref[...] += jnp.sum(draw, axis=0, keepdims=True)
        draw_ref[...] = draw.astype(BF16)

    row = pl.BlockSpec((tt, LANE), lambda i: (i, 0))
    wide = pl.BlockSpec((tt, DI), lambda i: (i, 0))
    vec = pl.BlockSpec((1, LANE), lambda i: (0, 0))
    return pl.pallas_call(
        body, name=name, grid=(T // tt,),
        in_specs=[wide, wide, row, pl.BlockSpec((tt, LANE), lambda i: (i, dt_off // LANE)), row, vec, vec,
                  pl.BlockSpec((DI, LANE), lambda i: (0, 0))],
        out_specs=[row, vec, vec],
        out_shape=[jax.ShapeDtypeStruct((T, LANE), BF16), jax.ShapeDtypeStruct((1, LANE), F32),
                   jax.ShapeDtypeStruct((1, LANE), F32)],
        scratch_shapes=[pltpu.VMEM((tt, LANE), F32)],
        compiler_params=_cp("arbitrary"))(ddt_x, dcs_x, dcs_cols, proj, dt, bias, a_log, reduce_t)


def _pair_terms(x_ref, dtx_ref, csx_ref, csr_ref, pair, ppg, lo_half, causal):
    L = SSM_CHUNK
    g, pp = divmod(pair, ppg)
    ra = g * HEAD_ROWS + 2 * pp
    cols = slice(pair * LANE, (pair + 1) * LANE)
    X = x_ref[:, cols].astype(F32)
    dt_p = dtx_ref[:, cols].astype(F32)
    own = csx_ref[:, cols]
    other = pltpu.roll(own, SSM_HEAD_DIM, 1)
    csa_c = jnp.where(lo_half, own, other)
    csb_c = jnp.where(lo_half, other, own)
    La = jnp.exp(jnp.where(causal, csa_c - csr_ref[ra:ra + 1, :], NEG_BIG))
    Lb = jnp.exp(jnp.where(causal, csb_c - csr_ref[ra + 1:ra + 2, :], NEG_BIG))
    last = csx_ref[L - 1:L, cols]
    return g, ra, cols, X, dt_p, La, Lb, jnp.exp(own), jnp.exp(last - own), jnp.exp(last)


def scan_fwd(xbc, dt_x, cs_x, cs_row, d_full, *, B, seq, DI, name):
    L, N, G = SSM_CHUNK, SSM_STATE, SSM_GROUPS
    nc = seq // L
    XW = xbc.shape[1]
    n_pairs = DI // LANE
    ppg = n_pairs // G

    def body(x_ref, dtx_ref, csx_ref, csr_ref, d_ref, y_ref, st_ref, state):
        @pl.when(pl.program_id(0) == 0)
        def _():
            state[...] = jnp.zeros_like(state)

        causal = lax.broadcasted_iota(jnp.int32, (L, L), 0) >= lax.broadcasted_iota(jnp.int32, (L, L), 1)
        lo_half = lax.broadcasted_iota(jnp.int32, (L, LANE), 1) < SSM_HEAD_DIM
        for b in range(B):
            xb, yb = x_ref.at[b], y_ref.at[b]
            cbs = []
            for g in range(G):
                Bc = xb[:, DI + g * N:DI + (g + 1) * N]
                Cc = xb[:, DI + G * N + g * N:DI + G * N + (g + 1) * N]
                cbs.append((Bc, Cc, _dot(Cc, Bc, NT)))
            for pair in range(n_pairs):
                g, _, cols, X, dt_p, La, Lb, ecs, tail, e_last = _pair_terms(
                    xb, dtx_ref.at[b], csx_ref.at[b], csr_ref.at[b], pair, ppg, lo_half, causal)
                Bc, Cc, CB = cbs[g]
                xs = X * dt_p
                xsb = xs.astype(BF16)
                y = jnp.where(lo_half, _dot((CB * La).astype(BF16), xsb), _dot((CB * Lb).astype(BF16), xsb))
                ST = state[b, pair]
                st_ref[b, 0, pair] = ST
                y = y + ecs * _dot(Cc, ST.astype(BF16)) + d_ref[:, cols] * X
                yb[:, cols] = y.astype(BF16)
                state[b, pair] = e_last * ST + _dot(Bc, (xs * tail).astype(BF16), TN)

    wide = pl.BlockSpec((B, L, DI), lambda c: (0, c, 0))
    y, states = pl.pallas_call(
        body, name=name, grid=(nc,),
        in_specs=[pl.BlockSpec((B, L, XW), lambda c: (0, c, 0)), wide, wide,
                  pl.BlockSpec((B, G * HEAD_ROWS, L), lambda c: (0, 0, c)),
                  pl.BlockSpec((1, DI), lambda c: (0, 0))],
        out_specs=[wide, pl.BlockSpec((B, 1, n_pairs, N, LANE), lambda c: (0, c, 0, 0, 0))],
        out_shape=[jax.ShapeDtypeStruct((B, seq, DI), BF16),
                   jax.ShapeDtypeStruct((B, nc, n_pairs, N, LANE), F32)],
        scratch_shapes=[pltpu.VMEM((B, n_pairs, N, LANE), F32)],
        compiler_params=_cp("arbitrary"))(xbc.reshape(B, seq, XW), dt_x.reshape(B, seq, DI),
                                          cs_x.reshape(B, seq, DI), cs_row, d_full)
    return y.reshape(B * seq, DI), states


def scan_bwd(xbc, dt_x, cs_x, cs_row, d_full, states, dy, *, B, seq, DI, name):
    L, N, G = SSM_CHUNK, SSM_STATE, SSM_GROUPS
    nc = seq // L
    XW = xbc.shape[1]
    n_pairs = DI // LANE
    ppg = n_pairs // G
    HR = G * HEAD_ROWS
    inv_p = 1.0 / SSM_HEAD_DIM

    def body(x_ref, dtx_ref, csx_ref, csr_ref, d_ref, st_ref, dy_ref, dx_ref, ddtx_ref, dcsx_ref, dcsr_ref, dd_ref,
             dH):
        @pl.when(pl.program_id(0) == 0)
        def _():
            dH[...] = jnp.zeros_like(dH)
            dd_ref[...] = jnp.zeros_like(dd_ref)

        causal = lax.broadcasted_iota(jnp.int32, (L, L), 0) >= lax.broadcasted_iota(jnp.int32, (L, L), 1)
        lo_half = lax.broadcasted_iota(jnp.int32, (L, LANE), 1) < SSM_HEAD_DIM
        last_row = lax.broadcasted_iota(jnp.int32, (L, LANE), 0) == L - 1
        head_row = lax.broadcasted_iota(jnp.int32, (HR, 1), 0)
        for b in range(B):
            xb, dxb = x_ref.at[b], dx_ref.at[b]
            dcs_rows = jnp.zeros((HR, L), F32)

            for g in range(G):
                Bc = xb[:, DI + g * N:DI + (g + 1) * N]
                Cc = xb[:, DI + G * N + g * N:DI + G * N + (g + 1) * N]
                CB = _dot(Cc, Bc, NT)
                dCB = jnp.zeros((L, L), F32)
                dC = jnp.zeros((L, N), F32)
                dB = jnp.zeros((L, N), F32)
                for pp in range(ppg):
                    pair = g * ppg + pp
                    _, ra, cols, X, dt_p, La, Lb, ecs, tail, e_last = _pair_terms(
                        xb, dtx_ref.at[b], csx_ref.at[b], csr_ref.at[b], pair, ppg, lo_half, causal)
                    xs = X * dt_p
                    xsb = xs.astype(BF16)
                    Ma, Mb = CB * La, CB * Lb
                    dY = dy_ref[b, :, cols].astype(F32)
                    dYb = dY.astype(BF16)
                    dMa = _dot(jnp.where(lo_half, dY, 0.0).astype(BF16), xsb, NT)
                    dMb = _dot(jnp.where(lo_half, 0.0, dY).astype(BF16), xsb, NT)
                    dSa, dSb = dMa * Ma, dMb * Mb
                    dCB = dCB + dMa * La + dMb * Lb
                    dcs = jnp.where(lo_half, jnp.sum(dSa, axis=1, keepdims=True),
                                    jnp.sum(dSb, axis=1, keepdims=True)) * inv_p
                    dcs_rows = dcs_rows - jnp.where(head_row == ra, jnp.sum(dSa, axis=0, keepdims=True), 0.0)
                    dcs_rows = dcs_rows - jnp.where(head_row == ra + 1, jnp.sum(dSb, axis=0, keepdims=True), 0.0)
                    dxs = jnp.where(lo_half, _dot(Ma.astype(BF16), dYb, TN), _dot(Mb.astype(BF16), dYb, TN))
                    ST = st_ref[b, 0, pair]
                    STb = ST.astype(BF16)
                    dYe = (dY * ecs).astype(BF16)
                    dC = dC + _dot(dYe, STb, NT)
                    dSTp = _dot(Cc, dYe, TN)
                    dcs = dcs + dY * (ecs * _dot(Cc, STb))
                    dSTn = dH[b, pair]
                    dSTnb = dSTn.astype(BF16)
                    dSTp = dSTp + e_last * dSTn
                    XBt = _dot(Bc, dSTnb)
                    dxs = dxs + tail * XBt
                    t2 = xs * XBt * tail
                    at_end = (e_last * jnp.sum(dSTn * ST, axis=0, keepdims=True)
                              + jnp.sum(t2, axis=0, keepdims=True))
                    dcs = dcs - t2 + jnp.where(last_row, at_end, 0.0)
                    dB = dB + _dot((xs * tail).astype(BF16), dSTnb, NT)
                    dxb[:, cols] = (dxs * dt_p + d_ref[:, cols] * dY).astype(BF16)
                    ddtx_ref[b, :, cols] = (dxs * X).astype(BF16)
                    dcsx_ref[b, :, cols] = dcs
                    dd_ref[b, :, cols] += jnp.sum(dY * X, axis=0, keepdims=True)
                    dH[b, pair] = dSTp
                dCBb = dCB.astype(BF16)
                dxb[:, DI + g * N:DI + (g + 1) * N] = (dB + _dot(dCBb, Cc, TN)).astype(BF16)
                dxb[:, DI + G * N + g * N:DI + G * N + (g + 1) * N] = (dC + _dot(dCBb, Bc)).astype(BF16)
            dcsr_ref[b] = dcs_rows

    rev = lambda c: (0, nc - 1 - c, 0)
    wide = pl.BlockSpec((B, L, DI), rev)
    xspec = pl.BlockSpec((B, L, XW), rev)
    hrow = pl.BlockSpec((B, HR, L), lambda c: (0, 0, nc - 1 - c))
    dx, ddt_x, dcs_x, dcs_row, dd = pl.pallas_call(
        body, name=name, grid=(nc,),
        in_specs=[xspec, wide, wide, hrow, pl.BlockSpec((1, DI), lambda c: (0, 0)),
                  pl.BlockSpec((B, 1, n_pairs, N, LANE), lambda c: (0, nc - 1 - c, 0, 0, 0)), wide],
        out_specs=[xspec, wide, wide, hrow, pl.BlockSpec((B, 1, DI), lambda c: (0, 0, 0))],
        out_shape=[jax.ShapeDtypeStruct((B, seq, XW), BF16), jax.ShapeDtypeStruct((B, seq, DI), BF16),
                   jax.ShapeDtypeStruct((B, seq, DI), F32), jax.ShapeDtypeStruct((B, HR, seq), F32),
                   jax.ShapeDtypeStruct((B, 1, DI), F32)],
        scratch_shapes=[pltpu.VMEM((B, n_pairs, N, LANE), F32)],
        compiler_params=_cp("arbitrary"))(xbc.reshape(B, seq, XW), dt_x.reshape(B, seq, DI), cs_x.reshape(B, seq, DI),
                                          cs_row, d_full, states, dy.reshape(B, seq, DI))
    return dx.reshape(B * seq, XW), ddt_x.reshape(B * seq, DI), dcs_x.reshape(B * seq, DI), dcs_row, dd


def gnorm_fwd(y, proj, w, *, DI, name):
    T = y.shape[0]
    tt = _pick(T, 256, 16)
    gw = DI // SSM_GROUPS

    def body(y_ref, z_ref, w_ref, o_ref):
        for g in range(SSM_GROUPS):
            sl = slice(g * gw, (g + 1) * gw)
            y2 = y_ref[:, sl].astype(F32) * _silu(z_ref[:, sl].astype(F32))
            r = lax.rsqrt(jnp.mean(y2 * y2, axis=-1, keepdims=True) + EPS)
            o_ref[:, sl] = (y2 * r * w_ref[:, sl]).astype(BF16)

    row = pl.BlockSpec((tt, DI), lambda i: (i, 0))
    return pl.pallas_call(
        body, name=name, grid=(T // tt,),
        in_specs=[row, row, pl.BlockSpec((1, DI), lambda i: (0, 0))], out_specs=row,
        out_shape=jax.ShapeDtypeStruct((T, DI), BF16), compiler_params=_cp("parallel"))(y, proj, w)


def gnorm_bwd(dyn, y, proj, w, *, DI, name):
    T = y.shape[0]
    tt = _pick(T, 256, 16)
    gw = DI // SSM_GROUPS

    def body(dyn_ref, y_ref, z_ref, w_ref, dy_ref, dz_ref, dw_ref):
        @pl.when(pl.program_id(0) == 0)
        def _():
            dw_ref[...] = jnp.zeros_like(dw_ref)

        for g in range(SSM_GROUPS):
            sl = slice(g * gw, (g + 1) * gw)
            yv = y_ref[:, sl].astype(F32)
            z = z_ref[:, sl].astype(F32)
            sz = _silu(z)
            y2 = yv * sz
            r = lax.rsqrt(jnp.mean(y2 * y2, axis=-1, keepdims=True) + EPS)
            xhat = y2 * r
            d = dyn_ref[:, sl].astype(F32)
            dw_ref[:, sl] += jnp.sum(d * xhat, axis=0, keepdims=True)
            dxh = d * w_ref[:, sl]
            dy2 = r * (dxh - xhat * jnp.mean(dxh * xhat, axis=-1, keepdims=True))
            dy_ref[:, sl] = (dy2 * sz).astype(BF16)
            dz_ref[:, sl] = (dy2 * yv * _silu_grad(z)).astype(BF16)

    row = pl.BlockSpec((tt, DI), lambda i: (i, 0))
    vec = pl.BlockSpec((1, DI), lambda i: (0, 0))
    shp = jax.ShapeDtypeStruct((T, DI), BF16)
    return pl.pallas_call(
        body, name=name, grid=(T // tt,), in_specs=[row, row, row, vec], out_specs=[row, row, vec],
        out_shape=[shp, shp, jax.ShapeDtypeStruct((1, DI), F32)],
        compiler_params=_cp("arbitrary"))(dyn, y, proj, w)


N_CHIP = 4
SHARD_ROW_ALIGN = 128


def _comm_out_shapes(srcs, modes):
    return [jax.ShapeDtypeStruct(((N_DEV,) if mode in ("gather", "gather_direct") else ()) + s.shape, s.dtype)
            for s, mode in zip(srcs, modes)]


def _comm_scratch(n):
    return [pltpu.SemaphoreType.DMA((n, N_DEV - 1)), pltpu.SemaphoreType.DMA((n, N_DEV - 1)),
            pltpu.SemaphoreType.DMA((n,))]


def _comm_phases(modes, src_refs, out_refs, send_sems, recv_sems, local_sems):
    x, y, c = lax.axis_index("x"), lax.axis_index("y"), lax.axis_index("c")
    me, sibling = (x, y, c), (x, y, 1 - c)
    chips = [(1 - x, y), (x, 1 - y), (1 - x, 1 - y)]
    relays = [a for a, mode in enumerate(modes) if mode == "gather"]

    def slot(p):
        return 4 * p[0] + 2 * p[1] + p[2]

    def remote(a, k, src, dst, to):
        return pltpu.make_async_remote_copy(src_ref=src, dst_ref=dst, send_sem=send_sems.at[a, k],
                                            recv_sem=recv_sems.at[a, k], device_id=to,
                                            device_id_type=pl.DeviceIdType.MESH)

    def first_copies():
        local, two_way, send_only = [], [], []
        for a, mode in enumerate(modes):
            src, out = src_refs[a], out_refs[a]
            if mode == "sibling":
                two_way.append(remote(a, 0, src, out, sibling))
            elif mode == "chips":
                mine = 2 * x + y
                local.append(pltpu.make_async_copy(src.at[mine], out.at[mine], local_sems.at[a]))
                for j, chip in enumerate(chips):
                    two_way.append(remote(a, 1 + j, src.at[2 * chip[0] + chip[1]], out.at[mine], (*chip, c)))
            elif mode == "gather_direct":
                local.append(pltpu.make_async_copy(src, out.at[slot(me)], local_sems.at[a]))
                for k in range(1, N_DEV):
                    peer = (1 - x if k & 4 else x, 1 - y if k & 2 else y, 1 - c if k & 1 else c)
                    two_way.append(remote(a, k - 1, src, out.at[slot(me)], peer))
            else:
                assert mode == "gather"
                local.append(pltpu.make_async_copy(src, out.at[slot(me)], local_sems.at[a]))
                send_only.append(remote(a, 0, src, out.at[slot(me)], sibling))
                for j, chip in enumerate(chips):
                    send_only.append(remote(a, 1 + j, src, out.at[slot(me)], (*chip, c)))
        return local, two_way, send_only

    def forwards():
        out = []
        for a in relays:
            for j, chip in enumerate(chips):
                landed = out_refs[a].at[slot((*chip, c))]
                out.append((remote(a, 1 + j, landed, landed, me), remote(a, 4 + j, landed, landed, sibling)))
        return out

    def start():
        local, two_way, send_only = first_copies()
        for cp in local + two_way + send_only:
            cp.start()

    def relay():
        for arrival, fwd in forwards():
            arrival.wait_recv()
            fwd.start()

    def finish():
        local, two_way, send_only = first_copies()
        for a in relays:
            blk = out_refs[a].at[slot(sibling)]
            remote(a, 0, blk, blk, me).wait_recv()
            for j, chip in enumerate(chips):
                blk = out_refs[a].at[slot((*chip, 1 - c))]
                remote(a, 4 + j, blk, blk, me).wait_recv()
        for cp in send_only + [fwd for _, fwd in forwards()]:
            cp.wait_send()
        for cp in two_way + local:
            cp.wait()

    return start, relay, finish, bool(relays)


def _hosted(body, n_in, n_out, comm, step, n_steps):
    if comm is None:
        return body, [], [], [], [], []
    srcs, modes = comm
    nc = len(srcs)

    def wrapped(*refs):
        ins, csrc = refs[:n_in], refs[n_in:n_in + nc]
        outs = refs[n_in + nc:n_in + nc + n_out]
        cout = refs[n_in + nc + n_out:n_in + 2 * nc + n_out]
        scratch = refs[n_in + 2 * nc + n_out:len(refs) - 3]
        start, relay, finish, has_relay = _comm_phases(modes, csrc, cout, *refs[len(refs) - 3:])
        s = step()
        pl.when(s == 0)(start)
        body(*ins, *outs, *scratch)
        if has_relay:
            pl.when(s == (2 * n_steps) // 3)(relay)
        pl.when(s == n_steps - 1)(finish)

    any_spec = pl.BlockSpec(memory_space=pl.ANY)
    return wrapped, [any_spec] * nc, list(srcs), [any_spec] * nc, _comm_out_shapes(srcs, modes), _comm_scratch(nc)


def exchange(srcs, modes, *, name):
    n = len(srcs)

    def body(*refs):
        start, relay, finish, has_relay = _comm_phases(modes, refs[:n], refs[n:2 * n], *refs[2 * n:])
        start()
        if has_relay:
            relay()
        finish()

    any_spec = pl.BlockSpec(memory_space=pl.ANY)
    return pl.pallas_call(
        body, name=name, in_specs=[any_spec] * n, out_specs=[any_spec] * n, out_shape=_comm_out_shapes(srcs, modes),
        scratch_shapes=_comm_scratch(n), compiler_params=pltpu.CompilerParams(has_side_effects=True))(*srcs)


def pair_sum(a, b, *, name):
    n, R, C = a.shape
    tr = _pick(n * R, 1024, 16)

    def body(a_ref, b_ref, o_ref):
        o_ref[...] = (a_ref[...].astype(F32) + b_ref[...].astype(F32)).astype(BF16)

    blk = pl.BlockSpec((tr, C), lambda i: (i, 0))
    out = pl.pallas_call(
        body, name=name, grid=(n * R // tr,), in_specs=[blk, blk], out_specs=blk,
        out_shape=jax.ShapeDtypeStruct((n * R, C), BF16),
        compiler_params=_cp("parallel"))(a.reshape(n * R, C), b.reshape(n * R, C))
    return out.reshape(n, R, C)


def sum_slots(recv, *, name):
    _, R, C = recv.shape
    tr = _pick(R, 512, 8)

    def body(r_ref, o_ref):
        acc = r_ref[0].astype(F32)
        for p in range(1, N_DEV):
            acc = acc + r_ref[p].astype(F32)
        o_ref[...] = acc

    return pl.pallas_call(
        body, name=name, grid=(R // tr,),
        in_specs=[pl.BlockSpec((N_DEV, tr, C), lambda i: (0, i, 0))],
        out_specs=pl.BlockSpec((tr, C), lambda i: (i, 0)),
        out_shape=jax.ShapeDtypeStruct((R, C), F32), compiler_params=_cp("parallel"))(recv)


def adamw(gsrc, w, m, v, *, name):
    slots, R, C = gsrc.shape
    tr = _pick(R, 256, 16 if gsrc.dtype == BF16 else 8)
    c1 = 1.0 / (1.0 - ADAM_B1 ** ADAM_STEP)
    c2 = 1.0 / (1.0 - ADAM_B2 ** ADAM_STEP)

    def body(g_ref, w_ref, m_ref, v_ref, go_ref, d_ref, mo_ref, vo_ref):
        g = g_ref[0].astype(F32)
        for p in range(1, slots):
            g = g + g_ref[p].astype(F32)
        m2 = ADAM_B1 * m_ref[...] + (1.0 - ADAM_B1) * g
        v2 = ADAM_B2 * v_ref[...] + (1.0 - ADAM_B2) * (g * g)
        go_ref[...] = g
        mo_ref[...] = m2
        vo_ref[...] = v2
        d_ref[...] = -ADAM_LR * ((m2 * c1) / (jnp.sqrt(v2 * c2) + ADAM_EPS) + ADAM_WD * w_ref[...])

    blk = pl.BlockSpec((tr, C), lambda i: (i, 0))
    shp = jax.ShapeDtypeStruct((R, C), F32)
    return pl.pallas_call(
        body, name=name, grid=(R // tr,),
        in_specs=[pl.BlockSpec((slots, tr, C), lambda i: (0, i, 0)), blk, blk, blk],
        out_specs=[blk] * 4, out_shape=[shp] * 4, compiler_params=_cp("parallel"))(gsrc, w, m, v)


def _pad_cols(a, n):
    return jnp.pad(a, ((0, 0), (0, n - a.shape[1])))


def _to_rows(a, B, seq, H):
    G = SSM_GROUPS
    R = H // G
    t = a[:, :H].reshape(B, seq, G, R).transpose(0, 2, 3, 1)
    t = jnp.pad(t, ((0, 0), (0, 0), (0, HEAD_ROWS - R), (0, 0)))
    return t.reshape(B, G * HEAD_ROWS, seq)


def _from_rows(a, B, seq, H):
    G = SSM_GROUPS
    R = H // G
    t = a.reshape(B, G, HEAD_ROWS, seq)[:, :, :R].transpose(0, 3, 1, 2).reshape(B * seq, H)
    return _pad_cols(t, LANE)


def _chip_split(grads):
    c_idx = lax.axis_index("c")
    keep, give = [], []
    for g in grads:
        by_chip = g.reshape((N_CHIP, 2) + g.shape[1:])
        keep.append(lax.dynamic_index_in_dim(by_chip, c_idx, axis=1, keepdims=False))
        give.append(lax.dynamic_index_in_dim(by_chip, 1 - c_idx, axis=1, keepdims=False))
    return keep, give


def _chip_sums(grads, name):
    keep, give = _chip_split(grads)
    swapped = exchange(give, ["sibling"] * len(give), name="swap_" + name)
    return [pair_sum(k, s, name=f"chip_sum_{name}_{i}") for i, (k, s) in enumerate(zip(keep, swapped))]


def local_step(x, target, loc, od_w_in_t, *, B, seq):
    T, D = x.shape
    CW = D
    heads = CW // SB_HEAD_DIM
    DI = 2 * D
    H = DI // SSM_HEAD_DIM
    XW = DI + 2 * SSM_GROUPS * SSM_STATE
    in_odd = DI + XW + H
    w1_rows = in_odd // N_DEV
    q_off, k_off, v_off, gc_off, ga_off = 3 * CW, 4 * CW, 5 * CW, 2 * CW, 6 * CW
    dt_off = DI + XW

    small_packed, small_spans = _pack_rows([loc[n] for n in SMALL_SHARDED], LANE, 8)
    n0, (g_ev_in, small_all) = rmsnorm_fwd(x, loc["ev_norm_w"], name="l0_norm",
                                           comm=([loc["ev_w_in"].astype(BF16), small_packed],
                                                 ["gather", "gather_direct"]))
    p = {n: loc[n] for n in SMALL}
    for n, a in zip(SMALL_SHARDED, _unpack_rows(small_all, small_spans)):
        p[n] = _col_unshards(a)
    p["ev_w_in"] = _col_unshards(g_ev_in)
    proj0, (g_od_in_t,) = mm_nn(n0, p["ev_w_in"], out_dtype=BF16, name="l0_in_proj",
                                comm=([od_w_in_t.astype(BF16)], ["gather"]))
    w1t = g_od_in_t[:, :w1_rows].reshape(in_odd, D)
    w1t = jnp.pad(w1t, ((0, -(-(in_odd + LANE) // 256) * 256 - in_odd), (0, 0)))
    (u2,), (g_ev_out, g_od_out) = dwconv_fwd(
        proj0, (0, CW), p["ev_dw_w"], p["ev_dw_b"], C=CW, seq=seq, glu=True, silu_out=False, name="l0_conv",
        comm=([loc["ev_w_out"].astype(BF16), loc["od_w_out"].astype(BF16)], ["gather"] * 2))
    p["ev_w_out"] = g_ev_out.reshape(-1, D)
    od_w_out = g_od_out.reshape(-1, D)
    o, ctot = sba_fwd(proj0, B=B, seq=seq, heads=heads, q_off=q_off, k_off=k_off, v_off=v_off, name="l0_attn")
    ycat = mix0_post_fwd(u2, proj0, o, p["ev_ln_w"], p["ev_ln_b"], CW=CW, gc_off=gc_off, ga_off=ga_off,
                         name="l0_post")
    h1 = mm_nn(ycat, p["ev_w_out"], add=x, out_dtype=F32, name="l0_out_proj")

    n1 = rmsnorm_fwd(h1, p["od_norm_w"], name="l1_norm")
    proj1 = mm_nt_terms([(n1, 0, D, 0)], w1t, out_dtype=BF16, name="l1_in_proj")
    u_pre, xbc = dwconv_fwd(proj1, (DI,), p["od_conv_w"], p["od_conv_b"], C=XW, seq=seq, glu=False, silu_out=True,
                            name="l1_conv")
    bias_p, alog_p = _pad_cols(p["od_dt_bias"], LANE), _pad_cols(p["od_a_log"], LANE)
    expand = _head_expand(H, DI)
    dt, cs, dt_x, cs_x = dt_fwd(proj1, bias_p, alog_p, expand, dt_off=dt_off, name="l1_dt")
    cs_row = _to_rows(cs, B, seq, H)
    d_full = jnp.repeat(p["od_d"], SSM_HEAD_DIM, axis=1)
    y_ssd, states = scan_fwd(xbc, dt_x, cs_x, cs_row, d_full, B=B, seq=seq, DI=DI, name="l1_ssd")
    yn = gnorm_fwd(y_ssd, proj1, p["od_gnorm_w"], DI=DI, name="l1_gnorm")
    h2 = mm_nn(yn, od_w_out, add=h1, out_dtype=F32, name="l1_out_proj")

    loss, dh2, dh2b, g_final = final_loss(h2, p["final_norm_w"], target, name="loss_head")

    g_od_w_out = mm_tn(yn, dh2b, out_dtype=BF16, name="l1_dw_out")
    dyn = mm_nt_terms([(dh2b, 0, D, 0)], od_w_out, out_dtype=BF16, name="l1_d_out_proj")
    dy_ssd, dz, g_gnorm = gnorm_bwd(dyn, y_ssd, proj1, p["od_gnorm_w"], DI=DI, name="l1_gnorm_bwd")
    dxbc_c, ddt_x, dcs_x, dcs_row, dd_part = scan_bwd(xbc, dt_x, cs_x, cs_row, d_full, states, dy_ssd, B=B, seq=seq,
                                                      DI=DI, name="l1_ssd_bwd")
    g_d = dd_part.sum(axis=(0, 1)).reshape(H, SSM_HEAD_DIM).sum(axis=1)[None, :]
    draw, g_bias, g_alog = dt_bwd(ddt_x, dcs_x, _from_rows(dcs_row, B, seq, H), proj1, dt, bias_p, alog_p, expand.T,
                                  dt_off=dt_off, n_heads=H, name="l1_dt_bwd")
    dxbc, g_conv_w, g_conv_b = dwconv_bwd(dxbc_c, u_pre, proj1, (DI,), p["od_conv_w"], C=XW, seq=seq, glu=False,
                                          silu_out=True, name="l1_conv_bwd")
    tw = 512 if DI % 512 == 0 else LANE
    terms = [(dz, j, tw, j * tw) for j in range(DI // tw)]
    terms += [(dxbc, j, tw, DI + j * tw) for j in range(XW // tw)]
    terms += [(draw, 0, LANE, dt_off)]
    dn1 = mm_nn_terms(terms, w1t, out_dtype=BF16, name="l1_d_in_proj")
    g_od_w_in_t = jnp.concatenate([mm_tn(dz, n1, out_dtype=BF16, name="l1_dw_in_z"),
                                   mm_tn(dxbc, n1, out_dtype=BF16, name="l1_dw_in_xbc"),
                                   mm_tn(draw, n1, out_dtype=BF16, name="l1_dw_in_dt")], axis=0)[:in_odd]
    w1_pad = (-w1_rows) % SHARD_ROW_ALIGN
    keep, give = _chip_split([jnp.pad(g_od_w_in_t.reshape(N_DEV, w1_rows, D), ((0, 0), (0, w1_pad), (0, 0))),
                              g_od_w_out.reshape(N_DEV, -1, D)])
    dh1, dh1b, g_od_norm, swapped = rmsnorm_bwd(h1, p["od_norm_w"], dn1, dh2, name="l1_norm_bwd",
                                                comm=(give, ["sibling"] * 2))
    l1_chip = [pair_sum(k, s, name=f"chip_sum_l1_{i}") for i, (k, s) in enumerate(zip(keep, swapped))]

    g_ev_w_out = mm_tn(ycat, dh1b, out_dtype=BF16, name="l0_dw_out")
    dycat = mm_nt_terms([(dh1b, 0, D, 0)], p["ev_w_out"], out_dtype=BF16, name="l0_d_out_proj")
    du2, dgc, dga, do, g_ln_w, g_ln_b = mix0_post_bwd(dycat, u2, proj0, o, p["ev_ln_w"], p["ev_ln_b"], CW=CW,
                                                      gc_off=gc_off, ga_off=ga_off, name="l0_post_bwd")
    dq, dk, dv, (r_od_in_t, r_od_out) = sba_bwd(proj0, ctot, do, B=B, seq=seq, heads=heads, q_off=q_off, k_off=k_off,
                                                v_off=v_off, name="l0_attn_bwd", comm=(l1_chip, ["chips", "chips"]))
    dga_a, dga_b, g_dw_w, g_dw_b = dwconv_bwd(du2, None, proj0, (0, CW), p["ev_dw_w"], C=CW, seq=seq, glu=True,
                                              silu_out=False, name="l0_conv_bwd")
    pieces = [dga_a, dga_b, dgc, dq, dk, dv, dga]
    g_ev_w_in = jnp.concatenate([mm_tn(n0, pc, out_dtype=BF16, name=f"l0_dw_in_{j}") for j, pc in enumerate(pieces)],
                                axis=1)
    l0_chip = _chip_sums([_col_shards(g_ev_w_in), g_ev_w_out.reshape(N_DEV, -1, D)], "l0")
    dn0, (r_ev_in, r_ev_out) = mm_nt_terms([(pc, 0, CW, j * CW) for j, pc in enumerate(pieces)], p["ev_w_in"],
                                           out_dtype=BF16, name="l0_d_in_proj", comm=(l0_chip, ["chips", "chips"]))
    dx, _, g_ev_norm = rmsnorm_bwd(x, p["ev_norm_w"], dn0, dh1, name="l0_norm_bwd")

    small = dict(ev_norm_w=g_ev_norm, ev_dw_w=g_dw_w, ev_dw_b=g_dw_b, ev_ln_w=g_ln_w, ev_ln_b=g_ln_b,
                 od_norm_w=g_od_norm, od_conv_w=g_conv_w, od_conv_b=g_conv_b, od_dt_bias=g_bias[:, :H],
                 od_a_log=g_alog[:, :H], od_d=g_d, od_gnorm_w=g_gnorm, final_norm_w=g_final)
    received = dict(ev_w_in=r_ev_in, ev_w_out=r_ev_out, od_w_in=r_od_in_t, od_w_out=r_od_out)
    return loss, dx, small, received


BIG = ("ev_w_in", "ev_w_out", "od_w_in", "od_w_out")
SMALL = ("ev_norm_w", "ev_dw_w", "ev_dw_b", "ev_ln_w", "ev_ln_b", "od_norm_w", "od_conv_w", "od_conv_b",
         "od_dt_bias", "od_a_log", "od_d", "od_gnorm_w", "final_norm_w")
SMALL_SHARDED = ("ev_dw_w", "od_norm_w", "od_conv_w", "od_conv_b", "od_gnorm_w")
ORDER = ("ev_norm_w", "ev_w_in", "ev_dw_w", "ev_dw_b", "ev_ln_w", "ev_ln_b", "ev_w_out", "od_norm_w", "od_w_in",
         "od_conv_w", "od_conv_b", "od_dt_bias", "od_a_log", "od_d", "od_gnorm_w", "od_w_out", "final_norm_w")


def _pack_rows(arrs, width, row_align):
    parts, spans, r0 = [], [], 0
    for a in arrs:
        flat = a.reshape(-1)
        rows = -(-flat.shape[0] // (width * row_align)) * row_align
        parts.append(jnp.pad(flat, (0, rows * width - flat.shape[0])).reshape(rows, width))
        spans.append((r0, a.size, a.shape))
        r0 += rows
    return jnp.concatenate(parts, axis=0), spans


def _unpack_rows(packed, spans):
    lead = packed.shape[:-2]
    width = packed.shape[-1]
    out = []
    for r0, size, shape in spans:
        rows = -(-size // width)
        blk = packed[..., r0:r0 + rows, :].reshape(lead + (rows * width,))[..., :size]
        out.append(blk.reshape(lead + tuple(shape)))
    return out


def _col_shards(a):
    R, C8 = a.shape
    return a.reshape(R, N_DEV, C8 // N_DEV).transpose(1, 0, 2)


def _col_unshards(a):
    n, R, C = a.shape
    return a.transpose(1, 0, 2).reshape(R, n * C)


def kernel(x, ev_norm_w, ev_w_in, ev_dw_w, ev_dw_b, ev_ln_w, ev_ln_b, ev_w_out, od_norm_w, od_w_in, od_conv_w, od_conv_b, od_dt_bias, od_a_log, od_d, od_gnorm_w, od_w_out, final_norm_w, loss_target, m_ev_norm_w, m_ev_w_in, m_ev_dw_w, m_ev_dw_b, m_ev_ln_w, m_ev_ln_b, m_ev_w_out, m_od_norm_w, m_od_w_in, m_od_conv_w, m_od_conv_b, m_od_dt_bias, m_od_a_log, m_od_d, m_od_gnorm_w, m_od_w_out, m_final_norm_w, v_ev_norm_w, v_ev_w_in, v_ev_dw_w, v_ev_dw_b, v_ev_ln_w, v_ev_ln_b, v_ev_w_out, v_od_norm_w, v_od_w_in, v_od_conv_w, v_od_conv_b, v_od_dt_bias, v_od_a_log, v_od_d, v_od_gnorm_w, v_od_w_out, v_final_norm_w):
    loc = dict(ev_norm_w=ev_norm_w, ev_w_in=ev_w_in, ev_dw_w=ev_dw_w, ev_dw_b=ev_dw_b, ev_ln_w=ev_ln_w,
               ev_ln_b=ev_ln_b, ev_w_out=ev_w_out, od_norm_w=od_norm_w, od_w_in=od_w_in, od_conv_w=od_conv_w,
               od_conv_b=od_conv_b, od_dt_bias=od_dt_bias, od_a_log=od_a_log, od_d=od_d, od_gnorm_w=od_gnorm_w,
               od_w_out=od_w_out, final_norm_w=final_norm_w)
    mom = dict(ev_norm_w=m_ev_norm_w, ev_w_in=m_ev_w_in, ev_dw_w=m_ev_dw_w, ev_dw_b=m_ev_dw_b, ev_ln_w=m_ev_ln_w,
               ev_ln_b=m_ev_ln_b, ev_w_out=m_ev_w_out, od_norm_w=m_od_norm_w, od_w_in=m_od_w_in,
               od_conv_w=m_od_conv_w, od_conv_b=m_od_conv_b, od_dt_bias=m_od_dt_bias, od_a_log=m_od_a_log,
               od_d=m_od_d, od_gnorm_w=m_od_gnorm_w, od_w_out=m_od_w_out, final_norm_w=m_final_norm_w)
    var = dict(ev_norm_w=v_ev_norm_w, ev_w_in=v_ev_w_in, ev_dw_w=v_ev_dw_w, ev_dw_b=v_ev_dw_b, ev_ln_w=v_ev_ln_w,
               ev_ln_b=v_ev_ln_b, ev_w_out=v_ev_w_out, od_norm_w=v_od_norm_w, od_w_in=v_od_w_in,
               od_conv_w=v_od_conv_w, od_conv_b=v_od_conv_b, od_dt_bias=v_od_dt_bias, od_a_log=v_od_a_log,
               od_d=v_od_d, od_gnorm_w=v_od_gnorm_w, od_w_out=v_od_w_out, final_norm_w=v_final_norm_w)
    shapes = {n: loc[n].shape for n in ORDER}
    loc = {n: (a.reshape(1, -1) if a.ndim == 1 else a.reshape(a.shape[-2:]) if a.ndim == 3 else a)
           for n, a in loc.items()}
    mom = {n: a.reshape(loc[n].shape) for n, a in mom.items()}
    var = {n: a.reshape(loc[n].shape) for n, a in var.items()}

    B, seq, D = x.shape
    me = 4 * lax.axis_index("x") + 2 * lax.axis_index("y") + lax.axis_index("c")

    w1_rows = loc["od_w_in"].shape[1]
    w1_pad = (-w1_rows) % SHARD_ROW_ALIGN

    def to_t(a):
        return jnp.pad(a.T, ((0, w1_pad), (0, 0)))

    loss, dx, grads, received = local_step(x.reshape(B * seq, D), loss_target.reshape(B * seq, D), loc,
                                           to_t(loc["od_w_in"]), B=B, seq=seq)

    gsmall_packed, gsmall_spans = _pack_rows([grads[n] for n in SMALL] + [loss], LANE, 8)
    (gsmall_recv,) = exchange([gsmall_packed], ["gather_direct"], name="gather_small_grads")

    big_out = [{} for _ in range(4)]
    for n in ("ev_w_in", "ev_w_out", "od_w_out"):
        for kind, a in enumerate(adamw(received[n], loc[n], mom[n], var[n], name="adamw_" + n)):
            big_out[kind][n] = a
    for kind, a in enumerate(adamw(received["od_w_in"], to_t(loc["od_w_in"]), to_t(mom["od_w_in"]),
                                   to_t(var["od_w_in"]), name="adamw_od_w_in")):
        big_out[kind]["od_w_in"] = a[:w1_rows].T

    summed = _unpack_rows(sum_slots(gsmall_recv, name="sum_small_grads"), gsmall_spans)
    loss_total = summed[-1][0, 0]
    gsmall = dict(zip(SMALL, summed[:-1]))
    for n in SMALL_SHARDED:
        width = loc[n].shape[1]
        gsmall[n] = lax.dynamic_slice_in_dim(gsmall[n], me * width, width, axis=1)
    gs, sspans = _pack_rows([gsmall[n] for n in SMALL], LANE, 8)
    ws, _ = _pack_rows([loc[n] for n in SMALL], LANE, 8)
    ms, _ = _pack_rows([mom[n] for n in SMALL], LANE, 8)
    vs, _ = _pack_rows([var[n] for n in SMALL], LANE, 8)
    small_out = [dict(zip(SMALL, _unpack_rows(a, sspans))) for a in adamw(gs[None], ws, ms, vs, name="adamw_small")]

    outs = [loss_total, dx.reshape(B, seq, D)]
    for kind in range(4):
        for n in ORDER:
            src = big_out[kind] if n in BIG else small_out[kind]
            outs.append(src[n].reshape(shapes[n]))
    return tuple(outs)
```

```python
import jax
import jax.numpy as jnp
from jax import lax
from jax.experimental import pallas as pl
from jax.experimental.pallas import tpu as pltpu

F32 = jnp.float32
BF16 = jnp.bfloat16

EPS = 1e-6
N_DEV = 8
LANE = 128
VMEM_LIMIT_BYTES = 48 * 1024 * 1024

SB_HEAD_DIM = 128
SSM_HEAD_DIM = 64
SSM_GROUPS = 4
SSM_STATE = 128
SSM_CHUNK = 128
HALO = 32
HEAD_ROWS = 8
NEG_BIG = -1e30

ADAM_LR = 0.001
ADAM_B1 = 0.9
ADAM_B2 = 0.999
ADAM_EPS = 1e-08
ADAM_WD = 0.01
ADAM_STEP = 10

NT = (((1,), (1,)), ((), ()))
TN = (((0,), (0,)), ((), ()))


def _cp(*sem):
    return pltpu.CompilerParams(dimension_semantics=sem, vmem_limit_bytes=VMEM_LIMIT_BYTES)


def _pick(n, cap, align):
    if n <= cap:
        return n
    t = (cap // align) * align
    while t >= align:
        if n % t == 0:
            return t
        t -= align
    raise ValueError(f"no tile for {n} (cap {cap}, align {align})")


def _sigmoid(x):
    return 0.5 * jnp.tanh(0.5 * x) + 0.5


def _silu(x):
    return x * _sigmoid(x)


def _silu_grad(x):
    s = _sigmoid(x)
    return s * (1.0 + x * (1.0 - s))


def _dot(a, b, dims=None):
    if dims is None:
        return jnp.dot(a, b, preferred_element_type=F32)
    return lax.dot_general(a, b, dims, preferred_element_type=F32)


def _tri_dot3(tri, x):
    hi = x.astype(BF16)
    r1 = x - hi.astype(F32)
    mid = r1.astype(BF16)
    lo = (r1 - mid.astype(F32)).astype(BF16)
    return _dot(tri, hi) + _dot(tri, mid) + _dot(tri, lo)


def _grid_step(n_inner):
    return lambda: pl.program_id(0) * n_inner + pl.program_id(1)


def mm_nn(a, b, *, add=None, out_dtype, name, comm=None):
    M, K = a.shape
    N = b.shape[1]
    tm = _pick(M, 2048 if K <= 1024 and add is None else 1024, 16)
    tn = _pick(N, 1024, LANE)

    def body(*refs):
        if add is None:
            a_ref, b_ref, o_ref = refs
        else:
            a_ref, b_ref, add_ref, o_ref = refs
        acc = _dot(a_ref[...], b_ref[...])
        if add is not None:
            acc = acc + add_ref[...]
        o_ref[...] = acc.astype(out_dtype)

    in_specs = [pl.BlockSpec((tm, K), lambda i, j: (i, 0)), pl.BlockSpec((K, tn), lambda i, j: (0, j))]
    args = [a, b]
    if add is not None:
        in_specs.append(pl.BlockSpec((tm, tn), lambda i, j: (i, j)))
        args.append(add)
    grid = (M // tm, N // tn)
    body, c_in, c_args, c_out, c_shape, c_scratch = _hosted(body, len(args), 1, comm, _grid_step(grid[1]),
                                                            grid[0] * grid[1])
    out = pl.pallas_call(
        body, name=name, grid=grid, in_specs=in_specs + c_in,
        out_specs=[pl.BlockSpec((tm, tn), lambda i, j: (i, j))] + c_out,
        out_shape=[jax.ShapeDtypeStruct((M, N), out_dtype)] + c_shape, scratch_shapes=c_scratch,
        compiler_params=_cp(*(("arbitrary",) * 2 if comm else ("parallel",) * 2)))(*args, *c_args)
    return (out[0], out[1:]) if comm else out[0]


def mm_nt_terms(terms, b, *, out_dtype, name, comm=None):
    M = terms[0][0].shape[0]
    N = b.shape[0]
    n_terms = len(terms)
    if n_terms == 1:
        tm, tn = _pick(M, 2048, 16), _pick(N, 1024, LANE)
    else:
        tm, tn = _pick(M, 256, 16), _pick(N, 1024, LANE)

    def body(*refs):
        o_ref = refs[-1]
        acc = None
        for t in range(n_terms):
            part = _dot(refs[2 * t][...], refs[2 * t + 1][...], NT)
            acc = part if acc is None else acc + part
        o_ref[...] = acc.astype(out_dtype)

    in_specs, args = [], []
    for arr, cb, w, off in terms:
        assert off % w == 0
        in_specs.append(pl.BlockSpec((tm, w), lambda i, j, cb=cb: (i, cb)))
        in_specs.append(pl.BlockSpec((tn, w), lambda i, j, ob=off // w: (j, ob)))
        args += [arr, b]
    grid = (M // tm, N // tn)
    body, c_in, c_args, c_out, c_shape, c_scratch = _hosted(body, len(args), 1, comm, _grid_step(grid[1]),
                                                            grid[0] * grid[1])
    out = pl.pallas_call(
        body, name=name, grid=grid, in_specs=in_specs + c_in,
        out_specs=[pl.BlockSpec((tm, tn), lambda i, j: (i, j))] + c_out,
        out_shape=[jax.ShapeDtypeStruct((M, N), out_dtype)] + c_shape, scratch_shapes=c_scratch,
        compiler_params=_cp(*(("arbitrary",) * 2 if comm else ("parallel",) * 2)))(*args, *c_args)
    return (out[0], out[1:]) if comm else out[0]


def mm_nn_terms(terms, b, *, out_dtype, name):
    M = terms[0][0].shape[0]
    N = b.shape[1]
    tm = _pick(M, 256, 16)
    tn = _pick(N, 1024, LANE)
    n_terms = len(terms)

    def body(*refs):
        o_ref = refs[-1]
        acc = None
        for t in range(n_terms):
            part = _dot(refs[2 * t][...], refs[2 * t + 1][...])
            acc = part if acc is None else acc + part
        o_ref[...] = acc.astype(out_dtype)

    in_specs, args = [], []
    for arr, cb, w, off in terms:
        assert off % w == 0
        in_specs.append(pl.BlockSpec((tm, w), lambda i, j, cb=cb: (i, cb)))
        in_specs.append(pl.BlockSpec((w, tn), lambda i, j, ob=off // w: (ob, j)))
        args += [arr, b]
    return pl.pallas_call(
        body, name=name, grid=(M // tm, N // tn), in_specs=in_specs,
        out_specs=pl.BlockSpec((tm, tn), lambda i, j: (i, j)),
        out_shape=jax.ShapeDtypeStruct((M, N), out_dtype),
        compiler_params=_cp("parallel", "parallel"))(*args)


def mm_tn(a, b, *, out_dtype, name):
    T, M = a.shape
    N = b.shape[1]
    tm = _pick(M, 1024, LANE)
    tn = _pick(N, 1024, LANE)
    tk = _pick(T, 2048, 16)
    nk = T // tk

    def body(a_ref, b_ref, o_ref, acc_ref):
        k = pl.program_id(2)

        @pl.when(k == 0)
        def _():
            acc_ref[...] = jnp.zeros_like(acc_ref)

        acc_ref[...] += _dot(a_ref[...], b_ref[...], TN)

        @pl.when(k == nk - 1)
        def _():
            o_ref[...] = acc_ref[...].astype(out_dtype)

    return pl.pallas_call(
        body, name=name, grid=(M // tm, N // tn, nk),
        in_specs=[pl.BlockSpec((tk, tm), lambda i, j, k: (k, i)), pl.BlockSpec((tk, tn), lambda i, j, k: (k, j))],
        out_specs=pl.BlockSpec((tm, tn), lambda i, j, k: (i, j)),
        out_shape=jax.ShapeDtypeStruct((M, N), out_dtype),
        scratch_shapes=[pltpu.VMEM((tm, tn), F32)],
        compiler_params=_cp("parallel", "parallel", "arbitrary"))(a, b)


def rmsnorm_fwd(h, w, *, name, comm=None):
    T, D = h.shape
    tt = _pick(T, 512, 16)

    def body(h_ref, w_ref, n_ref):
        x = h_ref[...]
        r = lax.rsqrt(jnp.mean(x * x, axis=-1, keepdims=True) + EPS)
        n_ref[...] = (x * r * w_ref[...]).astype(BF16)

    body, c_in, c_args, c_out, c_shape, c_scratch = _hosted(body, 2, 1, comm, lambda: pl.program_id(0), T // tt)
    out = pl.pallas_call(
        body, name=name, grid=(T // tt,),
        in_specs=[pl.BlockSpec((tt, D), lambda i: (i, 0)), pl.BlockSpec((1, D), lambda i: (0, 0))] + c_in,
        out_specs=[pl.BlockSpec((tt, D), lambda i: (i, 0))] + c_out,
        out_shape=[jax.ShapeDtypeStruct((T, D), BF16)] + c_shape, scratch_shapes=c_scratch,
        compiler_params=_cp("arbitrary" if comm else "parallel"))(h, w, *c_args)
    return (out[0], out[1:]) if comm else out[0]


def rmsnorm_bwd(h, w, dn, dres, *, name, comm=None):
    T, D = h.shape
    tt = _pick(T, 512, 16)

    def body(h_ref, w_ref, dn_ref, dres_ref, dh_ref, dhb_ref, gw_ref):
        @pl.when(pl.program_id(0) == 0)
        def _():
            gw_ref[...] = jnp.zeros_like(gw_ref)

        x = h_ref[...]
        r = lax.rsqrt(jnp.mean(x * x, axis=-1, keepdims=True) + EPS)
        xhat = x * r
        g = dn_ref[...].astype(F32)
        gw_ref[...] += jnp.sum(g * xhat, axis=0, keepdims=True)
        dxh = g * w_ref[...]
        dx = r * (dxh - xhat * jnp.mean(dxh * xhat, axis=-1, keepdims=True))
        dh = dres_ref[...] + dx
        dh_ref[...] = dh
        dhb_ref[...] = dh.astype(BF16)

    row = pl.BlockSpec((tt, D), lambda i: (i, 0))
    vec = pl.BlockSpec((1, D), lambda i: (0, 0))
    body, c_in, c_args, c_out, c_shape, c_scratch = _hosted(body, 4, 3, comm, lambda: pl.program_id(0), T // tt)
    out = pl.pallas_call(
        body, name=name, grid=(T // tt,), in_specs=[row, vec, row, row] + c_in, out_specs=[row, row, vec] + c_out,
        out_shape=[jax.ShapeDtypeStruct((T, D), F32), jax.ShapeDtypeStruct((T, D), BF16),
                   jax.ShapeDtypeStruct((1, D), F32)] + c_shape,
        scratch_shapes=c_scratch, compiler_params=_cp("arbitrary"))(h, w, dn, dres, *c_args)
    return (out[0], out[1], out[2], out[3:]) if comm else out


def final_loss(h, w, target, *, name):
    T, D = h.shape
    tt = _pick(T, 512, 16)

    def body(h_ref, w_ref, t_ref, loss_ref, dh_ref, dhb_ref, gw_ref):
        @pl.when(pl.program_id(0) == 0)
        def _():
            gw_ref[...] = jnp.zeros_like(gw_ref)
            loss_ref[...] = jnp.zeros_like(loss_ref)

        x = h_ref[...]
        r = lax.rsqrt(jnp.mean(x * x, axis=-1, keepdims=True) + EPS)
        xhat = x * r
        e = xhat * w_ref[...] - t_ref[...]
        loss_ref[...] += jnp.sum(e * e) * (0.5 / D)
        g = e * (1.0 / D)
        gw_ref[...] += jnp.sum(g * xhat, axis=0, keepdims=True)
        dxh = g * w_ref[...]
        dh = r * (dxh - xhat * jnp.mean(dxh * xhat, axis=-1, keepdims=True))
        dh_ref[...] = dh
        dhb_ref[...] = dh.astype(BF16)

    row = pl.BlockSpec((tt, D), lambda i: (i, 0))
    vec = pl.BlockSpec((1, D), lambda i: (0, 0))
    one = pl.BlockSpec((1, LANE), lambda i: (0, 0))
    return pl.pallas_call(
        body, name=name, grid=(T // tt,), in_specs=[row, vec, row], out_specs=[one, row, row, vec],
        out_shape=[jax.ShapeDtypeStruct((1, LANE), F32), jax.ShapeDtypeStruct((T, D), F32),
                   jax.ShapeDtypeStruct((T, D), BF16), jax.ShapeDtypeStruct((1, D), F32)],
        compiler_params=_cp("arbitrary"))(h, w, target)


CONV_CHUNK = 32
ROW_CHUNK = 16
SUBLANES = 8


def _conv_tiles(seq, C, K):
    return _pick(seq, 1024 if K <= SUBLANES else 512, HALO), _pick(C, 512, LANE)


def _residues(offsets):
    return sorted({s % SUBLANES for s in offsets} - {0})


def _fill_shifted(buf, shifted, residues):
    n = buf.shape[0] - SUBLANES
    for i, r in enumerate(residues):
        shifted[i, 0:n, :] = buf[r:r + n, :]


def _tap(buf, shifted, residues, offset, start, rows):
    r = offset % SUBLANES
    base = offset - r
    ref = buf if r == 0 else shifted.at[residues.index(r)]
    return ref[pl.ds(start + base, rows), :]


def dwconv_fwd(src, offs, w, b, *, C, seq, glu, silu_out, name, comm=None):
    T = src.shape[0]
    K = w.shape[0]
    assert K - 1 <= HALO
    tt, tc = _conv_tiles(seq, C, K)
    n_in = 2 if glu else 1
    per = tt // HALO
    offsets = [HALO - (K - 1) + k for k in range(K)]
    residues = _residues(offsets)

    def body(*refs):
        cur = refs[0:2 * n_in:2]
        halo = refs[1:2 * n_in:2]
        w_ref, b_ref = refs[2 * n_in], refs[2 * n_in + 1]
        outs = refs[2 * n_in + 2:-2]
        buf, shifted = refs[-2], refs[-1]
        i = pl.program_id(1)
        first = (i * tt) % seq == 0

        def pre(rs, rows):
            v = rs[0][rows, :].astype(F32)
            return v * _sigmoid(rs[1][rows, :].astype(F32)) if glu else v

        def build(ci, carry):
            start = pl.multiple_of(ci * CONV_CHUNK, CONV_CHUNK)
            buf[pl.ds(HALO + start, CONV_CHUNK), :] = pre(cur, pl.ds(start, CONV_CHUNK))
            return carry

        buf[0:HALO, :] = jnp.where(first, 0.0, pre(halo, slice(None)))
        lax.fori_loop(0, tt // CONV_CHUNK, build, 0, unroll=2)
        _fill_shifted(buf, shifted, residues)

        def chunk(ci, carry):
            start = pl.multiple_of(ci * CONV_CHUNK, CONV_CHUNK)
            acc = jnp.broadcast_to(b_ref[...], (CONV_CHUNK, tc))
            for k in range(K):
                acc = acc + w_ref[k:k + 1, :] * _tap(buf, shifted, residues, offsets[k], start, CONV_CHUNK)
            outs[0][pl.ds(start, CONV_CHUNK), :] = acc.astype(BF16)
            if silu_out:
                outs[1][pl.ds(start, CONV_CHUNK), :] = _silu(acc).astype(BF16)
            return carry

        lax.fori_loop(0, tt // CONV_CHUNK, chunk, 0)

    in_specs, args = [], []
    for off in offs:
        assert off % tc == 0
        in_specs.append(pl.BlockSpec((tt, tc), lambda j, i, ob=off // tc: (i, ob + j)))
        in_specs.append(pl.BlockSpec((HALO, tc), lambda j, i, ob=off // tc: (jnp.maximum(i * per - 1, 0), ob + j)))
        args += [src, src]
    in_specs += [pl.BlockSpec((K, tc), lambda j, i: (0, j)), pl.BlockSpec((1, tc), lambda j, i: (0, j))]
    args += [w, b]
    n_out = 2 if silu_out else 1
    grid = (C // tc, T // tt)
    body, c_in, c_args, c_out, c_shape, c_scratch = _hosted(body, len(args), n_out, comm, _grid_step(grid[1]),
                                                            grid[0] * grid[1])
    out = pl.pallas_call(
        body, name=name, grid=grid, in_specs=in_specs + c_in,
        out_specs=[pl.BlockSpec((tt, tc), lambda j, i: (i, j))] * n_out + c_out,
        out_shape=[jax.ShapeDtypeStruct((T, C), BF16)] * n_out + c_shape,
        scratch_shapes=[pltpu.VMEM((HALO + tt, tc), F32), pltpu.VMEM((max(len(residues), 1), HALO + tt, tc), F32)]
        + c_scratch,
        compiler_params=_cp("arbitrary" if comm else "parallel", "arbitrary"))(*args, *c_args)
    return (out[:n_out], out[n_out:]) if comm else out


def dwconv_bwd(du, u, src, offs, w, *, C, seq, glu, silu_out, name):
    T = src.shape[0]
    K = w.shape[0]
    assert K - 1 <= HALO
    tt, tc = _conv_tiles(seq, C, K)
    n_in = 2 if glu else 1
    per = tt // HALO
    last_blk = T // HALO - 1
    g_offsets = [K - 1 - k for k in range(K)]
    g_res = _residues(g_offsets)

    def body(*refs):
        pos = 0
        du_cur, du_nxt = refs[0], refs[1]
        pos = 2
        if silu_out:
            u_cur, u_nxt = refs[2], refs[3]
            pos = 4
        cur = refs[pos:pos + n_in]
        pos += n_in
        w_ref = refs[pos]
        outs = refs[pos + 1:pos + 1 + n_in]
        dw_ref, db_ref = refs[pos + 1 + n_in], refs[pos + 2 + n_in]
        gbuf, gshift, dw_acc, db_acc = refs[-4:]
        i = pl.program_id(1)
        last = ((i + 1) * tt) % seq == 0

        @pl.when(i == 0)
        def _():
            dw_acc[...] = jnp.zeros_like(dw_acc)
            db_acc[...] = jnp.zeros_like(db_acc)

        def build(ci, carry):
            rows = pl.ds(pl.multiple_of(ci * CONV_CHUNK, CONV_CHUNK), CONV_CHUNK)
            g = du_cur[rows, :].astype(F32)
            if silu_out:
                g = g * _silu_grad(u_cur[rows, :].astype(F32))
            gbuf[rows, :] = g
            return carry

        lax.fori_loop(0, tt // CONV_CHUNK, build, 0, unroll=2)
        g_nxt = du_nxt[...].astype(F32)
        if silu_out:
            g_nxt = g_nxt * _silu_grad(u_nxt[...].astype(F32))
        gbuf[tt:tt + HALO, :] = jnp.where(last, 0.0, g_nxt)
        _fill_shifted(gbuf, gshift, g_res)

        def fold(v):
            out = v[0:SUBLANES]
            for s in range(SUBLANES, CONV_CHUNK, SUBLANES):
                out = out + v[s:s + SUBLANES]
            return out

        def chunk(ci, carry):
            start = pl.multiple_of(ci * CONV_CHUNK, CONV_CHUNK)
            rows = pl.ds(start, CONV_CHUNK)
            a = cur[0][rows, :].astype(F32)
            if glu:
                s = _sigmoid(cur[1][rows, :].astype(F32))
                x_in = a * s
            else:
                x_in = a
            dx = jnp.zeros((CONV_CHUNK, tc), F32)
            for k in range(K):
                g_k = _tap(gbuf, gshift, g_res, g_offsets[k], start, CONV_CHUNK)
                dx = dx + w_ref[k:k + 1, :] * g_k
                dw_acc[k * SUBLANES:(k + 1) * SUBLANES, :] += fold(g_k * x_in)
            db_acc[...] += fold(gbuf[rows, :])
            if glu:
                outs[0][rows, :] = (dx * s).astype(BF16)
                outs[1][rows, :] = (dx * a * s * (1.0 - s)).astype(BF16)
            else:
                outs[0][rows, :] = dx.astype(BF16)
            return carry

        lax.fori_loop(0, tt // CONV_CHUNK, chunk, 0)

        @pl.when(i == T // tt - 1)
        def _():
            for k in range(K):
                dw_ref[k:k + 1, :] = jnp.sum(dw_acc[k * SUBLANES:(k + 1) * SUBLANES, :], axis=0, keepdims=True)
            db_ref[...] = jnp.sum(db_acc[...], axis=0, keepdims=True)

    def cur_spec(ob):
        return pl.BlockSpec((tt, tc), lambda j, i: (i, ob + j))

    def nxt_spec(ob):
        return pl.BlockSpec((HALO, tc), lambda j, i: (jnp.minimum((i + 1) * per, last_blk), ob + j))

    in_specs = [cur_spec(0), nxt_spec(0)]
    args = [du, du]
    if silu_out:
        in_specs += [cur_spec(0), nxt_spec(0)]
        args += [u, u]
    for off in offs:
        assert off % tc == 0
        in_specs.append(cur_spec(off // tc))
        args.append(src)
    in_specs.append(pl.BlockSpec((K, tc), lambda j, i: (0, j)))
    args.append(w)
    out_specs = [pl.BlockSpec((tt, tc), lambda j, i: (i, j))] * n_in
    out_specs += [pl.BlockSpec((K, tc), lambda j, i: (0, j)), pl.BlockSpec((1, tc), lambda j, i: (0, j))]
    out_shape = [jax.ShapeDtypeStruct((T, C), BF16)] * n_in
    out_shape += [jax.ShapeDtypeStruct((K, C), F32), jax.ShapeDtypeStruct((1, C), F32)]
    return pl.pallas_call(
        body, name=name, grid=(C // tc, T // tt), in_specs=in_specs, out_specs=out_specs, out_shape=out_shape,
        scratch_shapes=[pltpu.VMEM((tt + HALO, tc), F32), pltpu.VMEM((max(len(g_res), 1), tt + HALO, tc), F32),
                        pltpu.VMEM((K * SUBLANES, tc), F32), pltpu.VMEM((SUBLANES, tc), F32)],
        compiler_params=_cp("parallel", "arbitrary"))(*args)


def mix0_post_fwd(u2, proj, o, ln_w, ln_b, *, CW, gc_off, ga_off, name):
    T = u2.shape[0]
    tt = _pick(T, 256, 16)

    def body(u_ref, gc_ref, ga_ref, o_ref, lw_ref, lb_ref, y_ref):
        def chunk(ci, carry):
            rows = pl.ds(pl.multiple_of(ci * ROW_CHUNK, ROW_CHUNK), ROW_CHUNK)
            u = u_ref[rows, :].astype(F32)
            mu = jnp.mean(u, axis=-1, keepdims=True)
            xc = u - mu
            r = lax.rsqrt(jnp.mean(xc * xc, axis=-1, keepdims=True) + EPS)
            u3 = xc * r * lw_ref[...] + lb_ref[...]
            y_ref[rows, 0:CW] = (_silu(u3) * _silu(gc_ref[rows, :].astype(F32))).astype(BF16)
            y_ref[rows, CW:2 * CW] = (o_ref[rows, :].astype(F32) * _silu(ga_ref[rows, :].astype(F32))).astype(BF16)
            return carry

        lax.fori_loop(0, tt // ROW_CHUNK, chunk, 0, unroll=4)

    row = pl.BlockSpec((tt, CW), lambda i: (i, 0))
    vec = pl.BlockSpec((1, CW), lambda i: (0, 0))
    return pl.pallas_call(
        body, name=name, grid=(T // tt,),
        in_specs=[row, pl.BlockSpec((tt, CW), lambda i: (i, gc_off // CW)),
                  pl.BlockSpec((tt, CW), lambda i: (i, ga_off // CW)), row, vec, vec],
        out_specs=pl.BlockSpec((tt, 2 * CW), lambda i: (i, 0)),
        out_shape=jax.ShapeDtypeStruct((T, 2 * CW), BF16),
        compiler_params=_cp("parallel"))(u2, proj, proj, o, ln_w, ln_b)


def mix0_post_bwd(dy, u2, proj, o, ln_w, ln_b, *, CW, gc_off, ga_off, name):
    T = u2.shape[0]
    tt = _pick(T, 256, 16)

    def body(dy_ref, u_ref, gc_ref, ga_ref, o_ref, lw_ref, lb_ref, du_ref, dgc_ref, dga_ref, do_ref, dlw_ref, dlb_ref,
             lw_acc, lb_acc):
        i = pl.program_id(0)

        @pl.when(i == 0)
        def _():
            lw_acc[...] = jnp.zeros_like(lw_acc)
            lb_acc[...] = jnp.zeros_like(lb_acc)

        def fold(v):
            out = v[0:SUBLANES]
            for s in range(SUBLANES, ROW_CHUNK, SUBLANES):
                out = out + v[s:s + SUBLANES]
            return out

        def chunk(ci, carry):
            rows = pl.ds(pl.multiple_of(ci * ROW_CHUNK, ROW_CHUNK), ROW_CHUNK)
            dyc = dy_ref[rows, 0:CW].astype(F32)
            dya = dy_ref[rows, CW:2 * CW].astype(F32)
            u = u_ref[rows, :].astype(F32)
            mu = jnp.mean(u, axis=-1, keepdims=True)
            xc = u - mu
            r = lax.rsqrt(jnp.mean(xc * xc, axis=-1, keepdims=True) + EPS)
            xhat = xc * r
            u3 = xhat * lw_ref[...] + lb_ref[...]
            gc = gc_ref[rows, :].astype(F32)
            dgc_ref[rows, :] = (dyc * _silu(u3) * _silu_grad(gc)).astype(BF16)
            du3 = dyc * _silu(gc) * _silu_grad(u3)
            lw_acc[...] += fold(du3 * xhat)
            lb_acc[...] += fold(du3)
            dxh = du3 * lw_ref[...]
            du = r * (dxh - jnp.mean(dxh, axis=-1, keepdims=True)
                      - xhat * jnp.mean(dxh * xhat, axis=-1, keepdims=True))
            du_ref[rows, :] = du.astype(BF16)
            ga = ga_ref[rows, :].astype(F32)
            ov = o_ref[rows, :].astype(F32)
            do_ref[rows, :] = (dya * _silu(ga)).astype(BF16)
            dga_ref[rows, :] = (dya * ov * _silu_grad(ga)).astype(BF16)
            return carry

        lax.fori_loop(0, tt // ROW_CHUNK, chunk, 0, unroll=4)

        @pl.when(i == T // tt - 1)
        def _():
            dlw_ref[...] = jnp.sum(lw_acc[...], axis=0, keepdims=True)
            dlb_ref[...] = jnp.sum(lb_acc[...], axis=0, keepdims=True)

    row = pl.BlockSpec((tt, CW), lambda i: (i, 0))
    vec = pl.BlockSpec((1, CW), lambda i: (0, 0))
    big = jax.ShapeDtypeStruct((T, CW), BF16)
    small = jax.ShapeDtypeStruct((1, CW), F32)
    return pl.pallas_call(
        body, name=name, grid=(T // tt,),
        in_specs=[pl.BlockSpec((tt, 2 * CW), lambda i: (i, 0)), row,
                  pl.BlockSpec((tt, CW), lambda i: (i, gc_off // CW)),
                  pl.BlockSpec((tt, CW), lambda i: (i, ga_off // CW)), row, vec, vec],
        out_specs=[row, row, row, row, vec, vec],
        out_shape=[big, big, big, big, small, small],
        scratch_shapes=[pltpu.VMEM((SUBLANES, CW), F32), pltpu.VMEM((SUBLANES, CW), F32)],
        compiler_params=_cp("arbitrary"))(dy, u2, proj, proj, o, ln_w, ln_b)


SB_UNDERFLOW = 110.0
SB_BOUND_MARGIN = 1.02


def _sb_tile(seq):
    return _pick(seq, 256, LANE)


def _softplus(z):
    return jnp.maximum(z, 0.0) + jnp.log(1.0 + jnp.exp(-jnp.abs(z)))


def _tri01(n, lower):
    i = lax.broadcasted_iota(jnp.int32, (n, n), 0)
    j = lax.broadcasted_iota(jnp.int32, (n, n), 1)
    return ((i >= j) if lower else (i <= j)).astype(BF16)


SB_HEADS_FWD = 8
SB_HEADS_BWD = 2


def _sb_heads_per_step(heads, want):
    while heads % want:
        want //= 2
    return want


def sba_fwd(proj, *, B, seq, heads, q_off, k_off, v_off, name):
    dh = SB_HEAD_DIM
    tq = _sb_tile(seq)
    assert tq % (2 * LANE) == 0
    nq = seq // tq
    hps = _sb_heads_per_step(heads, SB_HEADS_FWD)
    hw = hps * dh
    scale = dh ** -0.5

    def body(q_ref, k_ref, v_ref, tri_ref, o_ref, ct_ref, acc_ref, kmax_ref):
        qi = pl.program_id(1)
        tri = tri_ref[...]
        qs = [(q_ref[:, h * dh:(h + 1) * dh].astype(F32) * scale).astype(BF16) for h in range(hps)]

        @pl.when(qi == 0)
        def _():
            def chunk(i, best):
                rows = k_ref[pl.ds(pl.multiple_of(i * tq, tq), tq), :].astype(F32)
                sq = rows * rows
                return tuple(jnp.maximum(best[h], jnp.max(jnp.sum(sq[:, h * dh:(h + 1) * dh], axis=1, keepdims=True),
                                                          axis=0, keepdims=True)) for h in range(hps))

            best = lax.fori_loop(0, nq, chunk, (jnp.zeros((1, 1), F32),) * hps)
            for h in range(hps):
                kmax_ref[h] = jnp.broadcast_to(jnp.sqrt(best[h]), (8, LANE))

        z_bound = [jnp.sqrt(jnp.sum(qs[h].astype(F32) ** 2, axis=1, keepdims=True))
                   * (SB_BOUND_MARGIN * jnp.max(kmax_ref[h], keepdims=True)) for h in range(hps)]

        def part(h, q_rows, start, n_keys, r, mask):
            k_blk = k_ref[pl.ds(start, n_keys), h * dh:(h + 1) * dh]
            v_blk = v_ref[pl.ds(start, n_keys), h * dh:(h + 1) * dh]
            z = _dot(q_rows, k_blk, NT)
            sp = _softplus(z)
            if mask is not None:
                sp = jnp.where(mask, sp, 0.0)
            wts = jnp.exp(z - (_dot(sp.astype(BF16), tri[0:n_keys, 0:n_keys]) + r))
            if mask is not None:
                wts = jnp.where(mask, wts, 0.0)
            return _dot(wts.astype(BF16), v_blk), r + jnp.sum(sp, axis=-1, keepdims=True)

        below = lax.broadcasted_iota(jnp.int32, (tq, tq), 1) < lax.broadcasted_iota(jnp.int32, (tq, tq), 0)
        has_left = qi > 0
        left = pl.multiple_of(jnp.maximum(qi - 1, 0) * tq, tq)
        rs = []
        for h in range(hps):
            pv_d, r = part(h, qs[h], pl.multiple_of(qi * tq, tq), tq, jnp.zeros((tq, 1), F32), below)
            pv_l, r = part(h, qs[h], left, tq, r, has_left)
            acc_ref[:, h * dh:(h + 1) * dh] = pv_d + pv_l
            rs.append(r)
        rs = tuple(rs)

        def block(start, rs):
            pvs, out = [], []
            for h in range(hps):
                pv, r = part(h, qs[h], start, tq, rs[h], None)
                pvs.append(pv)
                out.append(r)
            return pvs, tuple(out)

        def more(c):
            j, rs = c
            slack = rs[0] - z_bound[0]
            for h in range(1, hps):
                slack = jnp.minimum(slack, rs[h] - z_bound[h])
            return jnp.logical_and(j < qi, jnp.min(slack) <= SB_UNDERFLOW)

        def step(c):
            j, rs = c
            pvs, rs = block(pl.multiple_of((qi - 1 - j) * tq, tq), rs)
            for h in range(hps):
                acc_ref[:, h * dh:(h + 1) * dh] += pvs[h]
            return j + 1, rs

        n_left, totals = lax.while_loop(more, step, (has_left.astype(jnp.int32), rs))
        o_ref[...] = acc_ref[...].astype(BF16)
        for h in range(hps):
            ct_ref[0, 0, h, 0:8, :] = jnp.broadcast_to(totals[h], (tq, LANE)).T[0:8, :]
            ct_ref[0, 0, h, 8:16, :] = jnp.full((8, tq), n_left, F32)

    qb, kb, vb = q_off // hw, k_off // hw, v_off // hw
    G = heads // hps
    return pl.pallas_call(
        body, name=name, grid=(B * G, nq),
        in_specs=[pl.BlockSpec((tq, hw), lambda g, i: ((g // G) * nq + i, qb + g % G)),
                  pl.BlockSpec((seq, hw), lambda g, i: (g // G, kb + g % G)),
                  pl.BlockSpec((seq, hw), lambda g, i: (g // G, vb + g % G)),
                  pl.BlockSpec((tq, tq), lambda g, i: (0, 0))],
        out_specs=[pl.BlockSpec((tq, hw), lambda g, i: ((g // G) * nq + i, g % G)),
                   pl.BlockSpec((1, 1, hps, 16, tq), lambda g, i: (g // G, i, g % G, 0, 0))],
        out_shape=[jax.ShapeDtypeStruct((B * seq, heads * dh), BF16),
                   jax.ShapeDtypeStruct((B, nq, heads, 16, tq), F32)],
        scratch_shapes=[pltpu.VMEM((tq, hw), F32), pltpu.VMEM((hps, 8, LANE), F32)],
        compiler_params=_cp("parallel", "arbitrary"))(proj, proj, proj, jnp.tril(jnp.ones((tq, tq), BF16)))


def sba_bwd(proj, ctot, do, *, B, seq, heads, q_off, k_off, v_off, name, comm=None):
    dh = SB_HEAD_DIM
    tq = _sb_tile(seq)
    nq = seq // tq
    hps = _sb_heads_per_step(heads, SB_HEADS_BWD)
    hw = hps * dh
    scale = dh ** -0.5

    def body(q_ref, k_ref, v_ref, ct_ref, do_ref, sfx_ref, pre_ref, dq_ref, dk_ref, dv_ref, dq_acc, dk_acc, dv_acc):
        qi = pl.program_id(1)

        @pl.when(qi == 0)
        def _():
            dk_acc[...] = jnp.zeros_like(dk_acc)
            dv_acc[...] = jnp.zeros_like(dv_acc)

        tri_sfx = sfx_ref[...]
        tri_pre = pre_ref[...]
        qs = [(q_ref[:, h * dh:(h + 1) * dh].astype(F32) * scale).astype(BF16) for h in range(hps)]
        dos = [do_ref[:, h * dh:(h + 1) * dh] for h in range(hps)]
        totals = [jnp.max(jnp.broadcast_to(ct_ref[0, 0, h, 0:1, :], (LANE, tq)).T, axis=1, keepdims=True)
                  for h in range(hps)]
        dq_acc[...] = jnp.zeros_like(dq_acc)

        def part(h, rows, start, n_keys, pc, pg, mask):
            cols = slice(h * dh, (h + 1) * dh)
            q_rows, do_rows = qs[h][rows], dos[h][rows]
            k_blk = k_ref[pl.ds(start, n_keys), cols]
            v_blk = v_ref[pl.ds(start, n_keys), cols]
            z = _dot(q_rows, k_blk, NT)
            sp = _softplus(z)
            sig = jnp.exp(z - sp)
            if mask is not None:
                sp = jnp.where(mask, sp, 0.0)
            pc_next = pc + jnp.sum(sp, axis=-1, keepdims=True)
            wts = jnp.exp(z - (_dot(sp.astype(BF16), tri_sfx[0:n_keys, 0:n_keys]) + (totals[h][rows] - pc_next)))
            if mask is not None:
                wts = jnp.where(mask, wts, 0.0)
            g = _dot(do_rows, v_blk, NT) * wts
            dz = g - sig * (_dot(g.astype(BF16), tri_pre[0:n_keys, 0:n_keys]) + pg)
            if mask is not None:
                dz = jnp.where(mask, dz, 0.0)
            dz = dz.astype(BF16)
            dk_acc[pl.ds(start, n_keys), cols] += _dot(dz, q_rows, TN)
            dv_acc[pl.ds(start, n_keys), cols] += _dot(wts.astype(BF16), do_rows, TN)
            return pc_next, pg + jnp.sum(g, axis=-1, keepdims=True), _dot(dz, k_blk)

        def block(start, carry):
            out = []
            for h in range(hps):
                pc, pg, dq = part(h, slice(0, tq), start, tq, carry[h][0], carry[h][1], None)
                dq_acc[:, h * dh:(h + 1) * dh] += dq
                out.append((pc, pg))
            return tuple(out)

        zero = jnp.zeros((tq, 1), F32)
        n_left = jnp.max(ct_ref[0, 0, 0, 8:16, :]).astype(jnp.int32)
        carry = lax.fori_loop(qi - n_left, qi - 1, lambda j, c: block(pl.multiple_of(j * tq, tq), c),
                              ((zero, zero),) * hps)
        below = lax.broadcasted_iota(jnp.int32, (tq, tq), 1) < lax.broadcasted_iota(jnp.int32, (tq, tq), 0)
        has_left = n_left > 0
        left = pl.multiple_of(jnp.maximum(qi - 1, 0) * tq, tq)
        for h in range(hps):
            cols = slice(h * dh, (h + 1) * dh)
            pc, pg, dq_l = part(h, slice(0, tq), left, tq, carry[h][0], carry[h][1], has_left)
            _, _, dq_d = part(h, slice(0, tq), pl.multiple_of(qi * tq, tq), tq, pc, pg, below)
            dq_ref[:, cols] = ((dq_acc[:, cols] + dq_l + dq_d) * scale).astype(BF16)

        @pl.when(qi == nq - 1)
        def _():
            dk_ref[...] = dk_acc[...].astype(BF16)
            dv_ref[...] = dv_acc[...].astype(BF16)

    qb, kb, vb = q_off // hw, k_off // hw, v_off // hw
    G = heads // hps
    q_spec = pl.BlockSpec((tq, hw), lambda g, i: ((g // G) * nq + i, qb + g % G))
    o_spec = pl.BlockSpec((tq, hw), lambda g, i: ((g // G) * nq + i, g % G))
    kv_out = pl.BlockSpec((seq, hw), lambda g, i: (g // G, g % G))
    shp = jax.ShapeDtypeStruct((B * seq, heads * dh), BF16)
    body, c_in, c_args, c_out, c_shape, c_scratch = _hosted(body, 7, 3, comm, _grid_step(nq), B * G * nq)
    tri_spec = pl.BlockSpec((tq, tq), lambda g, i: (0, 0))
    ones = jnp.ones((tq, tq), BF16)
    out = pl.pallas_call(
        body, name=name, grid=(B * G, nq),
        in_specs=[q_spec,
                  pl.BlockSpec((seq, hw), lambda g, i: (g // G, kb + g % G)),
                  pl.BlockSpec((seq, hw), lambda g, i: (g // G, vb + g % G)),
                  pl.BlockSpec((1, 1, hps, 16, tq), lambda g, i: (g // G, i, g % G, 0, 0)), o_spec,
                  tri_spec, tri_spec] + c_in,
        out_specs=[o_spec, kv_out, kv_out] + c_out, out_shape=[shp, shp, shp] + c_shape,
        scratch_shapes=[pltpu.VMEM((tq, hw), F32), pltpu.VMEM((seq, hw), F32), pltpu.VMEM((seq, hw), F32)]
        + c_scratch,
        compiler_params=_cp("arbitrary" if comm else "parallel", "arbitrary"))(
            proj, proj, proj, ctot, do, jnp.tril(ones), jnp.triu(ones), *c_args)
    return (out[0], out[1], out[2], out[3:]) if comm else out


def _head_expand(n_heads, DI):
    j = jnp.arange(LANE, dtype=jnp.int32)[:, None]
    c = jnp.arange(DI, dtype=jnp.int32)[None, :] // SSM_HEAD_DIM
    return ((j == c) & (j < n_heads)).astype(BF16)


def _split3(x):
    hi = x.astype(BF16)
    r1 = x - hi.astype(F32)
    mid = r1.astype(BF16)
    return hi, mid, (r1 - mid.astype(F32)).astype(BF16)


def dt_fwd(proj, bias, a_log, expand, *, dt_off, name):
    T = proj.shape[0]
    DI = expand.shape[1]
    L = SSM_CHUNK
    tt = _pick(T, 512, L)

    def body(raw_ref, bias_ref, al_ref, e_ref, dt_ref, cs_ref, dtx_ref, csx_ref):
        x = raw_ref[...].astype(F32) + bias_ref[...]
        dt = _softplus(x)
        dt_ref[...] = dt
        la = dt * (-jnp.exp(al_ref[...]))
        tri = _tri01(L, True)
        for c in range(tt // L):
            cs_ref[c * L:(c + 1) * L, :] = _tri_dot3(tri, la[c * L:(c + 1) * L, :])
        e = e_ref[...]
        dtx_ref[...] = _dot(dt.astype(BF16), e).astype(BF16)
        hi, mid, lo = _split3(cs_ref[...])
        csx_ref[...] = _dot(hi, e) + _dot(mid, e) + _dot(lo, e)

    row = pl.BlockSpec((tt, LANE), lambda i: (i, 0))
    wide = pl.BlockSpec((tt, DI), lambda i: (i, 0))
    vec = pl.BlockSpec((1, LANE), lambda i: (0, 0))
    return pl.pallas_call(
        body, name=name, grid=(T // tt,),
        in_specs=[pl.BlockSpec((tt, LANE), lambda i: (i, dt_off // LANE)), vec, vec,
                  pl.BlockSpec((LANE, DI), lambda i: (0, 0))],
        out_specs=[row, row, wide, wide],
        out_shape=[jax.ShapeDtypeStruct((T, LANE), F32), jax.ShapeDtypeStruct((T, LANE), F32),
                   jax.ShapeDtypeStruct((T, DI), BF16), jax.ShapeDtypeStruct((T, DI), F32)],
        compiler_params=_cp("parallel"))(proj, bias, a_log, expand)


def dt_bwd(ddt_x, dcs_x, dcs_cols, proj, dt, bias, a_log, reduce_t, *, dt_off, n_heads, name):
    T = proj.shape[0]
    DI = reduce_t.shape[0]
    L = SSM_CHUNK
    tt = _pick(T, 512, L)

    def body(ddtx_ref, dcsx_ref, dcsc_ref, raw_ref, dt_ref, bias_ref, al_ref, r_ref, draw_ref, dbias_ref, dal_ref,
             dla_buf):
        @pl.when(pl.program_id(0) == 0)
        def _():
            dbias_ref[...] = jnp.zeros_like(dbias_ref)
            dal_ref[...] = jnp.zeros_like(dal_ref)

        r = r_ref[...]
        ddt = _dot(ddtx_ref[...], r)
        dx = dcsx_ref[...]
        hi = dx.astype(BF16)
        dcs = _dot(hi, r) + _dot((dx - hi.astype(F32)).astype(BF16), r) + dcsc_ref[...]
        triu = _tri01(L, False)
        for c in range(tt // L):
            dla_buf[c * L:(c + 1) * L, :] = _tri_dot3(triu, dcs[c * L:(c + 1) * L, :])
        dla = dla_buf[...]
        a = -jnp.exp(al_ref[...])
        valid = lax.broadcasted_iota(jnp.int32, (tt, LANE), 1) < n_heads
        dal_ref[...] += jnp.sum(jnp.where(valid, dla * dt_ref[...], 0.0), axis=0, keepdims=True) * a
        x = raw_ref[...].astype(F32) + bias_ref[...]
        draw = jnp.where(valid, (ddt + dla * a) * _sigmoid(x), 0.0)
        dbias_ref[...] += jnp.sum(draw, axis=0, keepdims=True)
        draw_ref[...] = draw.astype(BF16)

    row = pl.BlockSpec((tt, LANE), lambda i: (i, 0))
    wide = pl.BlockSpec((tt, DI), lambda i: (i, 0))
    vec = pl.BlockSpec((1, LANE), lambda i: (0, 0))
    return pl.pallas_call(
        body, name=name, grid=(T // tt,),
        in_specs=[wide, wide, row, pl.BlockSpec((tt, LANE), lambda i: (i, dt_off // LANE)), row, vec, vec,
                  pl.BlockSpec((DI, LANE), lambda i: (0, 0))],
        out_specs=[row, vec, vec],
        out_shape=[jax.ShapeDtypeStruct((T, LANE), BF16), jax.ShapeDtypeStruct((1, LANE), F32),
                   jax.ShapeDtypeStruct((1, LANE), F32)],
        scratch_shapes=[pltpu.VMEM((tt, LANE), F32)],
        compiler_params=_cp("arbitrary"))(ddt_x, dcs_x, dcs_cols, proj, dt, bias, a_log, reduce_t)


def _pair_terms(x_ref, dtx_ref, csx_ref, csr_ref, pair, ppg, lo_half, causal):
    L = SSM_CHUNK
    g, pp = divmod(pair, ppg)
    ra = g * HEAD_ROWS + 2 * pp
    cols = slice(pair * LANE, (pair + 1) * LANE)
    X = x_ref[:, cols].astype(F32)
    dt_p = dtx_ref[:, cols].astype(F32)
    own = csx_ref[:, cols]
    other = pltpu.roll(own, SSM_HEAD_DIM, 1)
    csa_c = jnp.where(lo_half, own, other)
    csb_c = jnp.where(lo_half, other, own)
    La = jnp.exp(jnp.where(causal, csa_c - csr_ref[ra:ra + 1, :], NEG_BIG))
    Lb = jnp.exp(jnp.where(causal, csb_c - csr_ref[ra + 1:ra + 2, :], NEG_BIG))
    last = csx_ref[L - 1:L, cols]
    return g, ra, cols, X, dt_p, La, Lb, jnp.exp(own), jnp.exp(last - own), jnp.exp(last)


def scan_fwd(xbc, dt_x, cs_x, cs_row, d_full, *, B, seq, DI, name):
    L, N, G = SSM_CHUNK, SSM_STATE, SSM_GROUPS
    nc = seq // L
    XW = xbc.shape[1]
    n_pairs = DI // LANE
    ppg = n_pairs // G

    def body(x_ref, dtx_ref, csx_ref, csr_ref, d_ref, y_ref, st_ref, state):
        @pl.when(pl.program_id(0) == 0)
        def _():
            state[...] = jnp.zeros_like(state)

        causal = lax.broadcasted_iota(jnp.int32, (L, L), 0) >= lax.broadcasted_iota(jnp.int32, (L, L), 1)
        lo_half = lax.broadcasted_iota(jnp.int32, (L, LANE), 1) < SSM_HEAD_DIM
        for b in range(B):
            xb, yb = x_ref.at[b], y_ref.at[b]
            cbs = []
            for g in range(G):
                Bc = xb[:, DI + g * N:DI + (g + 1) * N]
                Cc = xb[:, DI + G * N + g * N:DI + G * N + (g + 1) * N]
                cbs.append((Bc, Cc, _dot(Cc, Bc, NT)))
            for pair in range(n_pairs):
                g, _, cols, X, dt_p, La, Lb, ecs, tail, e_last = _pair_terms(
                    xb, dtx_ref.at[b], csx_ref.at[b], csr_ref.at[b], pair, ppg, lo_half, causal)
                Bc, Cc, CB = cbs[g]
                xs = X * dt_p
                xsb = xs.astype(BF16)
                y = jnp.where(lo_half, _dot((CB * La).astype(BF16), xsb), _dot((CB * Lb).astype(BF16), xsb))
                ST = state[b, pair]
                st_ref[b, 0, pair] = ST
                y = y + ecs * _dot(Cc, ST.astype(BF16)) + d_ref[:, cols] * X
                yb[:, cols] = y.astype(BF16)
                state[b, pair] = e_last * ST + _dot(Bc, (xs * tail).astype(BF16), TN)

    wide = pl.BlockSpec((B, L, DI), lambda c: (0, c, 0))
    y, states = pl.pallas_call(
        body, name=name, grid=(nc,),
        in_specs=[pl.BlockSpec((B, L, XW), lambda c: (0, c, 0)), wide, wide,
                  pl.BlockSpec((B, G * HEAD_ROWS, L), lambda c: (0, 0, c)),
                  pl.BlockSpec((1, DI), lambda c: (0, 0))],
        out_specs=[wide, pl.BlockSpec((B, 1, n_pairs, N, LANE), lambda c: (0, c, 0, 0, 0))],
        out_shape=[jax.ShapeDtypeStruct((B, seq, DI), BF16),
                   jax.ShapeDtypeStruct((B, nc, n_pairs, N, LANE), F32)],
        scratch_shapes=[pltpu.VMEM((B, n_pairs, N, LANE), F32)],
        compiler_params=_cp("arbitrary"))(xbc.reshape(B, seq, XW), dt_x.reshape(B, seq, DI),
                                          cs_x.reshape(B, seq, DI), cs_row, d_full)
    return y.reshape(B * seq, DI), states


def scan_bwd(xbc, dt_x, cs_x, cs_row, d_full, states, dy, *, B, seq, DI, name):
    L, N, G = SSM_CHUNK, SSM_STATE, SSM_GROUPS
    nc = seq // L
    XW = xbc.shape[1]
    n_pairs = DI // LANE
    ppg = n_pairs // G
    HR = G * HEAD_ROWS
    inv_p = 1.0 / SSM_HEAD_DIM

    def body(x_ref, dtx_ref, csx_ref, csr_ref, d_ref, st_ref, dy_ref, dx_ref, ddtx_ref, dcsx_ref, dcsr_ref, dd_ref,
             dH):
        @pl.when(pl.program_id(0) == 0)
        def _():
            dH[...] = jnp.zeros_like(dH)
            dd_ref[...] = jnp.zeros_like(dd_ref)

        causal = lax.broadcasted_iota(jnp.int32, (L, L), 0) >= lax.broadcasted_iota(jnp.int32, (L, L), 1)
        lo_half = lax.broadcasted_iota(jnp.int32, (L, LANE), 1) < SSM_HEAD_DIM
        last_row = lax.broadcasted_iota(jnp.int32, (L, LANE), 0) == L - 1
        head_row = lax.broadcasted_iota(jnp.int32, (HR, 1), 0)
        for b in range(B):
            xb, dxb = x_ref.at[b], dx_ref.at[b]
            dcs_rows = jnp.zeros((HR, L), F32)

            for g in range(G):
                Bc = xb[:, DI + g * N:DI + (g + 1) * N]
                Cc = xb[:, DI + G * N + g * N:DI + G * N + (g + 1) * N]
                CB = _dot(Cc, Bc, NT)
                dCB = jnp.zeros((L, L), F32)
                dC = jnp.zeros((L, N), F32)
                dB = jnp.zeros((L, N), F32)
                for pp in range(ppg):
                    pair = g * ppg + pp
                    _, ra, cols, X, dt_p, La, Lb, ecs, tail, e_last = _pair_terms(
                        xb, dtx_ref.at[b], csx_ref.at[b], csr_ref.at[b], pair, ppg, lo_half, causal)
                    xs = X * dt_p
                    xsb = xs.astype(BF16)
                    Ma, Mb = CB * La, CB * Lb
                    dY = dy_ref[b, :, cols].astype(F32)
                    dYb = dY.astype(BF16)
                    dMa = _dot(jnp.where(lo_half, dY, 0.0).astype(BF16), xsb, NT)
                    dMb = _dot(jnp.where(lo_half, 0.0, dY).astype(BF16), xsb, NT)
                    dSa, dSb = dMa * Ma, dMb * Mb
                    dCB = dCB + dMa * La + dMb * Lb
                    dcs = jnp.where(lo_half, jnp.sum(dSa, axis=1, keepdims=True),
                                    jnp.sum(dSb, axis=1, keepdims=True)) * inv_p
                    dcs_rows = dcs_rows - jnp.where(head_row == ra, jnp.sum(dSa, axis=0, keepdims=True), 0.0)
                    dcs_rows = dcs_rows - jnp.where(head_row == ra + 1, jnp.sum(dSb, axis=0, keepdims=True), 0.0)
                    dxs = jnp.where(lo_half, _dot(Ma.astype(BF16), dYb, TN), _dot(Mb.astype(BF16), dYb, TN))
                    ST = st_ref[b, 0, pair]
                    STb = ST.astype(BF16)
                    dYe = (dY * ecs).astype(BF16)
                    dC = dC + _dot(dYe, STb, NT)
                    dSTp = _dot(Cc, dYe, TN)
                    dcs = dcs + dY * (ecs * _dot(Cc, STb))
                    dSTn = dH[b, pair]
                    dSTnb = dSTn.astype(BF16)
                    dSTp = dSTp + e_last * dSTn
                    XBt = _dot(Bc, dSTnb)
                    dxs = dxs + tail * XBt
                    t2 = xs * XBt * tail
                    at_end = (e_last * jnp.sum(dSTn * ST, axis=0, keepdims=True)
                              + jnp.sum(t2, axis=0, keepdims=True))
                    dcs = dcs - t2 + jnp.where(last_row, at_end, 0.0)
                    dB = dB + _dot((xs * tail).astype(BF16), dSTnb, NT)
                    dxb[:, cols] = (dxs * dt_p + d_ref[:, cols] * dY).astype(BF16)
                    ddtx_ref[b, :, cols] = (dxs * X).astype(BF16)
                    dcsx_ref[b, :, cols] = dcs
                    dd_ref[b, :, cols] += jnp.sum(dY * X, axis=0, keepdims=True)
                    dH[b, pair] = dSTp
                dCBb = dCB.astype(BF16)
                dxb[:, DI + g * N:DI + (g + 1) * N] = (dB + _dot(dCBb, Cc, TN)).astype(BF16)
                dxb[:, DI + G * N + g * N:DI + G * N + (g + 1) * N] = (dC + _dot(dCBb, Bc)).astype(BF16)
            dcsr_ref[b] = dcs_rows

    rev = lambda c: (0, nc - 1 - c, 0)
    wide = pl.BlockSpec((B, L, DI), rev)
    xspec = pl.BlockSpec((B, L, XW), rev)
    hrow = pl.BlockSpec((B, HR, L), lambda c: (0, 0, nc - 1 - c))
    dx, ddt_x, dcs_x, dcs_row, dd = pl.pallas_call(
        body, name=name, grid=(nc,),
        in_specs=[xspec, wide, wide, hrow, pl.BlockSpec((1, DI), lambda c: (0, 0)),
                  pl.BlockSpec((B, 1, n_pairs, N, LANE), lambda c: (0, nc - 1 - c, 0, 0, 0)), wide],
        out_specs=[xspec, wide, wide, hrow, pl.BlockSpec((B, 1, DI), lambda c: (0, 0, 0))],
        out_shape=[jax.ShapeDtypeStruct((B, seq, XW), BF16), jax.ShapeDtypeStruct((B, seq, DI), BF16),
                   jax.ShapeDtypeStruct((B, seq, DI), F32), jax.ShapeDtypeStruct((B, HR, seq), F32),
                   jax.ShapeDtypeStruct((B, 1, DI), F32)],
        scratch_shapes=[pltpu.VMEM((B, n_pairs, N, LANE), F32)],
        compiler_params=_cp("arbitrary"))(xbc.reshape(B, seq, XW), dt_x.reshape(B, seq, DI), cs_x.reshape(B, seq, DI),
                                          cs_row, d_full, states, dy.reshape(B, seq, DI))
    return dx.reshape(B * seq, XW), ddt_x.reshape(B * seq, DI), dcs_x.reshape(B * seq, DI), dcs_row, dd


def gnorm_fwd(y, proj, w, *, DI, name):
    T = y.shape[0]
    tt = _pick(T, 256, 16)
    gw = DI // SSM_GROUPS

    def body(y_ref, z_ref, w_ref, o_ref):
        for g in range(SSM_GROUPS):
            sl = slice(g * gw, (g + 1) * gw)
            y2 = y_ref[:, sl].astype(F32) * _silu(z_ref[:, sl].astype(F32))
            r = lax.rsqrt(jnp.mean(y2 * y2, axis=-1, keepdims=True) + EPS)
            o_ref[:, sl] = (y2 * r * w_ref[:, sl]).astype(BF16)

    row = pl.BlockSpec((tt, DI), lambda i: (i, 0))
    return pl.pallas_call(
        body, name=name, grid=(T // tt,),
        in_specs=[row, row, pl.BlockSpec((1, DI), lambda i: (0, 0))], out_specs=row,
        out_shape=jax.ShapeDtypeStruct((T, DI), BF16), compiler_params=_cp("parallel"))(y, proj, w)


def gnorm_bwd(dyn, y, proj, w, *, DI, name):
    T = y.shape[0]
    tt = _pick(T, 256, 16)
    gw = DI // SSM_GROUPS

    def body(dyn_ref, y_ref, z_ref, w_ref, dy_ref, dz_ref, dw_ref):
        @pl.when(pl.program_id(0) == 0)
        def _():
            dw_ref[...] = jnp.zeros_like(dw_ref)

        for g in range(SSM_GROUPS):
            sl = slice(g * gw, (g + 1) * gw)
            yv = y_ref[:, sl].astype(F32)
            z = z_ref[:, sl].astype(F32)
            sz = _silu(z)
            y2 = yv * sz
            r = lax.rsqrt(jnp.mean(y2 * y2, axis=-1, keepdims=True) + EPS)
            xhat = y2 * r
            d = dyn_ref[:, sl].astype(F32)
            dw_ref[:, sl] += jnp.sum(d * xhat, axis=0, keepdims=True)
            dxh = d * w_ref[:, sl]
            dy2 = r * (dxh - xhat * jnp.mean(dxh * xhat, axis=-1, keepdims=True))
            dy_ref[:, sl] = (dy2 * sz).astype(BF16)
            dz_ref[:, sl] = (dy2 * yv * _silu_grad(z)).astype(BF16)

    row = pl.BlockSpec((tt, DI), lambda i: (i, 0))
    vec = pl.BlockSpec((1, DI), lambda i: (0, 0))
    shp = jax.ShapeDtypeStruct((T, DI), BF16)
    return pl.pallas_call(
        body, name=name, grid=(T // tt,), in_specs=[row, row, row, vec], out_specs=[row, row, vec],
        out_shape=[shp, shp, jax.ShapeDtypeStruct((1, DI), F32)],
        compiler_params=_cp("arbitrary"))(dyn, y, proj, w)


N_CHIP = 4
SHARD_ROW_ALIGN = 128


def _comm_out_shapes(srcs, modes):
    return [jax.ShapeDtypeStruct(((N_DEV,) if mode in ("gather", "gather_direct") else ()) + s.shape, s.dtype)
            for s, mode in zip(srcs, modes)]


def _comm_scratch(n):
    return [pltpu.SemaphoreType.DMA((n, N_DEV - 1)), pltpu.SemaphoreType.DMA((n, N_DEV - 1)),
            pltpu.SemaphoreType.DMA((n,))]


def _comm_phases(modes, src_refs, out_refs, send_sems, recv_sems, local_sems):
    x, y, c = lax.axis_index("x"), lax.axis_index("y"), lax.axis_index("c")
    me, sibling = (x, y, c), (x, y, 1 - c)
    chips = [(1 - x, y), (x, 1 - y), (1 - x, 1 - y)]
    relays = [a for a, mode in enumerate(modes) if mode == "gather"]

    def slot(p):
        return 4 * p[0] + 2 * p[1] + p[2]

    def remote(a, k, src, dst, to):
        return pltpu.make_async_remote_copy(src_ref=src, dst_ref=dst, send_sem=send_sems.at[a, k],
                                            recv_sem=recv_sems.at[a, k], device_id=to,
                                            device_id_type=pl.DeviceIdType.MESH)

    def first_copies():
        local, two_way, send_only = [], [], []
        for a, mode in enumerate(modes):
            src, out = src_refs[a], out_refs[a]
            if mode == "sibling":
                two_way.append(remote(a, 0, src, out, sibling))
            elif mode == "chips":
                mine = 2 * x + y
                local.append(pltpu.make_async_copy(src.at[mine], out.at[mine], local_sems.at[a]))
                for j, chip in enumerate(chips):
                    two_way.append(remote(a, 1 + j, src.at[2 * chip[0] + chip[1]], out.at[mine], (*chip, c)))
            elif mode == "gather_direct":
                local.append(pltpu.make_async_copy(src, out.at[slot(me)], local_sems.at[a]))
                for k in range(1, N_DEV):
                    peer = (1 - x if k & 4 else x, 1 - y if k & 2 else y, 1 - c if k & 1 else c)
                    two_way.append(remote(a, k - 1, src, out.at[slot(me)], peer))
            else:
                assert mode == "gather"
                local.append(pltpu.make_async_copy(src, out.at[slot(me)], local_sems.at[a]))
                send_only.append(remote(a, 0, src, out.at[slot(me)], sibling))
                for j, chip in enumerate(chips):
                    send_only.append(remote(a, 1 + j, src, out.at[slot(me)], (*chip, c)))
        return local, two_way, send_only

    def forwards():
        out = []
        for a in relays:
            for j, chip in enumerate(chips):
                landed = out_refs[a].at[slot((*chip, c))]
                out.append((remote(a, 1 + j, landed, landed, me), remote(a, 4 + j, landed, landed, sibling)))
        return out

    def start():
        local, two_way, send_only = first_copies()
        for cp in local + two_way + send_only:
            cp.start()

    def relay():
        for arrival, fwd in forwards():
            arrival.wait_recv()
            fwd.start()

    def finish():
        local, two_way, send_only = first_copies()
        for a in relays:
            blk = out_refs[a].at[slot(sibling)]
            remote(a, 0, blk, blk, me).wait_recv()
            for j, chip in enumerate(chips):
                blk = out_refs[a].at[slot((*chip, 1 - c))]
                remote(a, 4 + j, blk, blk, me).wait_recv()
        for cp in send_only + [fwd for _, fwd in forwards()]:
            cp.wait_send()
        for cp in two_way + local:
            cp.wait()

    return start, relay, finish, bool(relays)


def _hosted(body, n_in, n_out, comm, step, n_steps):
    if comm is None:
        return body, [], [], [], [], []
    srcs, modes = comm
    nc = len(srcs)

    def wrapped(*refs):
        ins, csrc = refs[:n_in], refs[n_in:n_in + nc]
        outs = refs[n_in + nc:n_in + nc + n_out]
        cout = refs[n_in + nc + n_out:n_in + 2 * nc + n_out]
        scratch = refs[n_in + 2 * nc + n_out:len(refs) - 3]
        start, relay, finish, has_relay = _comm_phases(modes, csrc, cout, *refs[len(refs) - 3:])
        s = step()
        pl.when(s == 0)(start)
        body(*ins, *outs, *scratch)
        if has_relay:
            pl.when(s == (2 * n_steps) // 3)(relay)
        pl.when(s == n_steps - 1)(finish)

    any_spec = pl.BlockSpec(memory_space=pl.ANY)
    return wrapped, [any_spec] * nc, list(srcs), [any_spec] * nc, _comm_out_shapes(srcs, modes), _comm_scratch(nc)


def exchange(srcs, modes, *, name):
    n = len(srcs)

    def body(*refs):
        start, relay, finish, has_relay = _comm_phases(modes, refs[:n], refs[n:2 * n], *refs[2 * n:])
        start()
        if has_relay:
            relay()
        finish()

    any_spec = pl.BlockSpec(memory_space=pl.ANY)
    return pl.pallas_call(
        body, name=name, in_specs=[any_spec] * n, out_specs=[any_spec] * n, out_shape=_comm_out_shapes(srcs, modes),
        scratch_shapes=_comm_scratch(n), compiler_params=pltpu.CompilerParams(has_side_effects=True))(*srcs)


def pair_sum(a, b, *, name):
    n, R, C = a.shape
    tr = _pick(n * R, 1024, 16)

    def body(a_ref, b_ref, o_ref):
        o_ref[...] = (a_ref[...].astype(F32) + b_ref[...].astype(F32)).astype(BF16)

    blk = pl.BlockSpec((tr, C), lambda i: (i, 0))
    out = pl.pallas_call(
        body, name=name, grid=(n * R // tr,), in_specs=[blk, blk], out_specs=blk,
        out_shape=jax.ShapeDtypeStruct((n * R, C), BF16),
        compiler_params=_cp("parallel"))(a.reshape(n * R, C), b.reshape(n * R, C))
    return out.reshape(n, R, C)


def sum_slots(recv, *, name):
    _, R, C = recv.shape
    tr = _pick(R, 512, 8)

    def body(r_ref, o_ref):
        acc = r_ref[0].astype(F32)
        for p in range(1, N_DEV):
            acc = acc + r_ref[p].astype(F32)
        o_ref[...] = acc

    return pl.pallas_call(
        body, name=name, grid=(R // tr,),
        in_specs=[pl.BlockSpec((N_DEV, tr, C), lambda i: (0, i, 0))],
        out_specs=pl.BlockSpec((tr, C), lambda i: (i, 0)),
        out_shape=jax.ShapeDtypeStruct((R, C), F32), compiler_params=_cp("parallel"))(recv)


def adamw(gsrc, w, m, v, *, name):
    slots, R, C = gsrc.shape
    tr = _pick(R, 256, 16 if gsrc.dtype == BF16 else 8)
    c1 = 1.0 / (1.0 - ADAM_B1 ** ADAM_STEP)
    c2 = 1.0 / (1.0 - ADAM_B2 ** ADAM_STEP)

    def body(g_ref, w_ref, m_ref, v_ref, go_ref, d_ref, mo_ref, vo_ref):
        g = g_ref[0].astype(F32)
        for p in range(1, slots):
            g = g + g_ref[p].astype(F32)
        m2 = ADAM_B1 * m_ref[...] + (1.0 - ADAM_B1) * g
        v2 = ADAM_B2 * v_ref[...] + (1.0 - ADAM_B2) * (g * g)
        go_ref[...] = g
        mo_ref[...] = m2
        vo_ref[...] = v2
        d_ref[...] = -ADAM_LR * ((m2 * c1) / (jnp.sqrt(v2 * c2) + ADAM_EPS) + ADAM_WD * w_ref[...])

    blk = pl.BlockSpec((tr, C), lambda i: (i, 0))
    shp = jax.ShapeDtypeStruct((R, C), F32)
    return pl.pallas_call(
        body, name=name, grid=(R // tr,),
        in_specs=[pl.BlockSpec((slots, tr, C), lambda i: (0, i, 0)), blk, blk, blk],
        out_specs=[blk] * 4, out_shape=[shp] * 4, compiler_params=_cp("parallel"))(gsrc, w, m, v)


def _pad_cols(a, n):
    return jnp.pad(a, ((0, 0), (0, n - a.shape[1])))


def _to_rows(a, B, seq, H):
    G = SSM_GROUPS
    R = H // G
    t = a[:, :H].reshape(B, seq, G, R).transpose(0, 2, 3, 1)
    t = jnp.pad(t, ((0, 0), (0, 0), (0, HEAD_ROWS - R), (0, 0)))
    return t.reshape(B, G * HEAD_ROWS, seq)


def _from_rows(a, B, seq, H):
    G = SSM_GROUPS
    R = H // G
    t = a.reshape(B, G, HEAD_ROWS, seq)[:, :, :R].transpose(0, 3, 1, 2).reshape(B * seq, H)
    return _pad_cols(t, LANE)


def _chip_split(grads):
    c_idx = lax.axis_index("c")
    keep, give = [], []
    for g in grads:
        by_chip = g.reshape((N_CHIP, 2) + g.shape[1:])
        keep.append(lax.dynamic_index_in_dim(by_chip, c_idx, axis=1, keepdims=False))
        give.append(lax.dynamic_index_in_dim(by_chip, 1 - c_idx, axis=1, keepdims=False))
    return keep, give


def _chip_sums(grads, name):
    keep, give = _chip_split(grads)
    swapped = exchange(give, ["sibling"] * len(give), name="swap_" + name)
    return [pair_sum(k, s, name=f"chip_sum_{name}_{i}") for i, (k, s) in enumerate(zip(keep, swapped))]


def local_step(x, target, loc, od_w_in_t, *, B, seq):
    T, D = x.shape
    CW = D
    heads = CW // SB_HEAD_DIM
    DI = 2 * D
    H = DI // SSM_HEAD_DIM
    XW = DI + 2 * SSM_GROUPS * SSM_STATE
    in_odd = DI + XW + H
    w1_rows = in_odd // N_DEV
    q_off, k_off, v_off, gc_off, ga_off = 3 * CW, 4 * CW, 5 * CW, 2 * CW, 6 * CW
    dt_off = DI + XW

    small_packed, small_spans = _pack_rows([loc[n] for n in SMALL_SHARDED], LANE, 8)
    n0, (g_ev_in, small_all) = rmsnorm_fwd(x, loc["ev_norm_w"], name="l0_norm",
                                           comm=([loc["ev_w_in"].astype(BF16), small_packed],
                                                 ["gather", "gather_direct"]))
    p = {n: loc[n] for n in SMALL}
    for n, a in zip(SMALL_SHARDED, _unpack_rows(small_all, small_spans)):
        p[n] = _col_unshards(a)
    p["ev_w_in"] = _col_unshards(g_ev_in)
    proj0, (g_od_in_t,) = mm_nn(n0, p["ev_w_in"], out_dtype=BF16, name="l0_in_proj",
                                comm=([od_w_in_t.astype(BF16)], ["gather"]))
    w1t = g_od_in_t[:, :w1_rows].reshape(in_odd, D)
    w1t = jnp.pad(w1t, ((0, -(-(in_odd + LANE) // 256) * 256 - in_odd), (0, 0)))
    (u2,), (g_ev_out, g_od_out) = dwconv_fwd(
        proj0, (0, CW), p["ev_dw_w"], p["ev_dw_b"], C=CW, seq=seq, glu=True, silu_out=False, name="l0_conv",
        comm=([loc["ev_w_out"].astype(BF16), loc["od_w_out"].astype(BF16)], ["gather"] * 2))
    p["ev_w_out"] = g_ev_out.reshape(-1, D)
    od_w_out = g_od_out.reshape(-1, D)
    o, ctot = sba_fwd(proj0, B=B, seq=seq, heads=heads, q_off=q_off, k_off=k_off, v_off=v_off, name="l0_attn")
    ycat = mix0_post_fwd(u2, proj0, o, p["ev_ln_w"], p["ev_ln_b"], CW=CW, gc_off=gc_off, ga_off=ga_off,
                         name="l0_post")
    h1 = mm_nn(ycat, p["ev_w_out"], add=x, out_dtype=F32, name="l0_out_proj")

    n1 = rmsnorm_fwd(h1, p["od_norm_w"], name="l1_norm")
    proj1 = mm_nt_terms([(n1, 0, D, 0)], w1t, out_dtype=BF16, name="l1_in_proj")
    u_pre, xbc = dwconv_fwd(proj1, (DI,), p["od_conv_w"], p["od_conv_b"], C=XW, seq=seq, glu=False, silu_out=True,
                            name="l1_conv")
    bias_p, alog_p = _pad_cols(p["od_dt_bias"], LANE), _pad_cols(p["od_a_log"], LANE)
    expand = _head_expand(H, DI)
    dt, cs, dt_x, cs_x = dt_fwd(proj1, bias_p, alog_p, expand, dt_off=dt_off, name="l1_dt")
    cs_row = _to_rows(cs, B, seq, H)
    d_full = jnp.repeat(p["od_d"], SSM_HEAD_DIM, axis=1)
    y_ssd, states = scan_fwd(xbc, dt_x, cs_x, cs_row, d_full, B=B, seq=seq, DI=DI, name="l1_ssd")
    yn = gnorm_fwd(y_ssd, proj1, p["od_gnorm_w"], DI=DI, name="l1_gnorm")
    h2 = mm_nn(yn, od_w_out, add=h1, out_dtype=F32, name="l1_out_proj")

    loss, dh2, dh2b, g_final = final_loss(h2, p["final_norm_w"], target, name="loss_head")

    g_od_w_out = mm_tn(yn, dh2b, out_dtype=BF16, name="l1_dw_out")
    dyn = mm_nt_terms([(dh2b, 0, D, 0)], od_w_out, out_dtype=BF16, name="l1_d_out_proj")
    dy_ssd, dz, g_gnorm = gnorm_bwd(dyn, y_ssd, proj1, p["od_gnorm_w"], DI=DI, name="l1_gnorm_bwd")
    dxbc_c, ddt_x, dcs_x, dcs_row, dd_part = scan_bwd(xbc, dt_x, cs_x, cs_row, d_full, states, dy_ssd, B=B, seq=seq,
                                                      DI=DI, name="l1_ssd_bwd")
    g_d = dd_part.sum(axis=(0, 1)).reshape(H, SSM_HEAD_DIM).sum(axis=1)[None, :]
    draw, g_bias, g_alog = dt_bwd(ddt_x, dcs_x, _from_rows(dcs_row, B, seq, H), proj1, dt, bias_p, alog_p, expand.T,
                                  dt_off=dt_off, n_heads=H, name="l1_dt_bwd")
    dxbc, g_conv_w, g_conv_b = dwconv_bwd(dxbc_c, u_pre, proj1, (DI,), p["od_conv_w"], C=XW, seq=seq, glu=False,
                                          silu_out=True, name="l1_conv_bwd")
    tw = 512 if DI % 512 == 0 else LANE
    terms = [(dz, j, tw, j * tw) for j in range(DI // tw)]
    terms += [(dxbc, j, tw, DI + j * tw) for j in range(XW // tw)]
    terms += [(draw, 0, LANE, dt_off)]
    dn1 = mm_nn_terms(terms, w1t, out_dtype=BF16, name="l1_d_in_proj")
    g_od_w_in_t = jnp.concatenate([mm_tn(dz, n1, out_dtype=BF16, name="l1_dw_in_z"),
                                   mm_tn(dxbc, n1, out_dtype=BF16, name="l1_dw_in_xbc"),
                                   mm_tn(draw, n1, out_dtype=BF16, name="l1_dw_in_dt")], axis=0)[:in_odd]
    w1_pad = (-w1_rows) % SHARD_ROW_ALIGN
    keep, give = _chip_split([jnp.pad(g_od_w_in_t.reshape(N_DEV, w1_rows, D), ((0, 0), (0, w1_pad), (0, 0))),
                              g_od_w_out.reshape(N_DEV, -1, D)])
    dh1, dh1b, g_od_norm, swapped = rmsnorm_bwd(h1, p["od_norm_w"], dn1, dh2, name="l1_norm_bwd",
                                                comm=(give, ["sibling"] * 2))
    l1_chip = [pair_sum(k, s, name=f"chip_sum_l1_{i}") for i, (k, s) in enumerate(zip(keep, swapped))]

    g_ev_w_out = mm_tn(ycat, dh1b, out_dtype=BF16, name="l0_dw_out")
    dycat = mm_nt_terms([(dh1b, 0, D, 0)], p["ev_w_out"], out_dtype=BF16, name="l0_d_out_proj")
    du2, dgc, dga, do, g_ln_w, g_ln_b = mix0_post_bwd(dycat, u2, proj0, o, p["ev_ln_w"], p["ev_ln_b"], CW=CW,
                                                      gc_off=gc_off, ga_off=ga_off, name="l0_post_bwd")
    dq, dk, dv, (r_od_in_t, r_od_out) = sba_bwd(proj0, ctot, do, B=B, seq=seq, heads=heads, q_off=q_off, k_off=k_off,
                                                v_off=v_off, name="l0_attn_bwd", comm=(l1_chip, ["chips", "chips"]))
    dga_a, dga_b, g_dw_w, g_dw_b = dwconv_bwd(du2, None, proj0, (0, CW), p["ev_dw_w"], C=CW, seq=seq, glu=True,
                                              silu_out=False, name="l0_conv_bwd")
    pieces = [dga_a, dga_b, dgc, dq, dk, dv, dga]
    g_ev_w_in = jnp.concatenate([mm_tn(n0, pc, out_dtype=BF16, name=f"l0_dw_in_{j}") for j, pc in enumerate(pieces)],
                                axis=1)
    l0_chip = _chip_sums([_col_shards(g_ev_w_in), g_ev_w_out.reshape(N_DEV, -1, D)], "l0")
    dn0, (r_ev_in, r_ev_out) = mm_nt_terms([(pc, 0, CW, j * CW) for j, pc in enumerate(pieces)], p["ev_w_in"],
                                           out_dtype=BF16, name="l0_d_in_proj", comm=(l0_chip, ["chips", "chips"]))
    dx, _, g_ev_norm = rmsnorm_bwd(x, p["ev_norm_w"], dn0, dh1, name="l0_norm_bwd")

    small = dict(ev_norm_w=g_ev_norm, ev_dw_w=g_dw_w, ev_dw_b=g_dw_b, ev_ln_w=g_ln_w, ev_ln_b=g_ln_b,
                 od_norm_w=g_od_norm, od_conv_w=g_conv_w, od_conv_b=g_conv_b, od_dt_bias=g_bias[:, :H],
                 od_a_log=g_alog[:, :H], od_d=g_d, od_gnorm_w=g_gnorm, final_norm_w=g_final)
    received = dict(ev_w_in=r_ev_in, ev_w_out=r_ev_out, od_w_in=r_od_in_t, od_w_out=r_od_out)
    return loss, dx, small, received


BIG = ("ev_w_in", "ev_w_out", "od_w_in", "od_w_out")
SMALL = ("ev_norm_w", "ev_dw_w", "ev_dw_b", "ev_ln_w", "ev_ln_b", "od_norm_w", "od_conv_w", "od_conv_b",
         "od_dt_bias", "od_a_log", "od_d", "od_gnorm_w", "final_norm_w")
SMALL_SHARDED = ("ev_dw_w", "od_norm_w", "od_conv_w", "od_conv_b", "od_gnorm_w")
ORDER = ("ev_norm_w", "ev_w_in", "ev_dw_w", "ev_dw_b", "ev_ln_w", "ev_ln_b", "ev_w_out", "od_norm_w", "od_w_in",
         "od_conv_w", "od_conv_b", "od_dt_bias", "od_a_log", "od_d", "od_gnorm_w", "od_w_out", "final_norm_w")


def _pack_rows(arrs, width, row_align):
    parts, spans, r0 = [], [], 0
    for a in arrs:
        flat = a.reshape(-1)
        rows = -(-flat.shape[0] // (width * row_align)) * row_align
        parts.append(jnp.pad(flat, (0, rows * width - flat.shape[0])).reshape(rows, width))
        spans.append((r0, a.size, a.shape))
        r0 += rows
    return jnp.concatenate(parts, axis=0), spans


def _unpack_rows(packed, spans):
    lead = packed.shape[:-2]
    width = packed.shape[-1]
    out = []
    for r0, size, shape in spans:
        rows = -(-size // width)
        blk = packed[..., r0:r0 + rows, :].reshape(lead + (rows * width,))[..., :size]
        out.append(blk.reshape(lead + tuple(shape)))
    return out


def _col_shards(a):
    R, C8 = a.shape
    return a.reshape(R, N_DEV, C8 // N_DEV).transpose(1, 0, 2)


def _col_unshards(a):
    n, R, C = a.shape
    return a.transpose(1, 0, 2).reshape(R, n * C)


def kernel(x, ev_norm_w, ev_w_in, ev_dw_w, ev_dw_b, ev_ln_w, ev_ln_b, ev_w_out, od_norm_w, od_w_in, od_conv_w, od_conv_b, od_dt_bias, od_a_log, od_d, od_gnorm_w, od_w_out, final_norm_w, loss_target, m_ev_norm_w, m_ev_w_in, m_ev_dw_w, m_ev_dw_b, m_ev_ln_w, m_ev_ln_b, m_ev_w_out, m_od_norm_w, m_od_w_in, m_od_conv_w, m_od_conv_b, m_od_dt_bias, m_od_a_log, m_od_d, m_od_gnorm_w, m_od_w_out, m_final_norm_w, v_ev_norm_w, v_ev_w_in, v_ev_dw_w, v_ev_dw_b, v_ev_ln_w, v_ev_ln_b, v_ev_w_out, v_od_norm_w, v_od_w_in, v_od_conv_w, v_od_conv_b, v_od_dt_bias, v_od_a_log, v_od_d, v_od_gnorm_w, v_od_w_out, v_final_norm_w):
    loc = dict(ev_norm_w=ev_norm_w, ev_w_in=ev_w_in, ev_dw_w=ev_dw_w, ev_dw_b=ev_dw_b, ev_ln_w=ev_ln_w,
               ev_ln_b=ev_ln_b, ev_w_out=ev_w_out, od_norm_w=od_norm_w, od_w_in=od_w_in, od_conv_w=od_conv_w,
               od_conv_b=od_conv_b, od_dt_bias=od_dt_bias, od_a_log=od_a_log, od_d=od_d, od_gnorm_w=od_gnorm_w,
               od_w_out=od_w_out, final_norm_w=final_norm_w)
    mom = dict(ev_norm_w=m_ev_norm_w, ev_w_in=m_ev_w_in, ev_dw_w=m_ev_dw_w, ev_dw_b=m_ev_dw_b, ev_ln_w=m_ev_ln_w,
               ev_ln_b=m_ev_ln_b, ev_w_out=m_ev_w_out, od_norm_w=m_od_norm_w, od_w_in=m_od_w_in,
               od_conv_w=m_od_conv_w, od_conv_b=m_od_conv_b, od_dt_bias=m_od_dt_bias, od_a_log=m_od_a_log,
               od_d=m_od_d, od_gnorm_w=m_od_gnorm_w, od_w_out=m_od_w_out, final_norm_w=m_final_norm_w)
    var = dict(ev_norm_w=v_ev_norm_w, ev_w_in=v_ev_w_in, ev_dw_w=v_ev_dw_w, ev_dw_b=v_ev_dw_b, ev_ln_w=v_ev_ln_w,
               ev_ln_b=v_ev_ln_b, ev_w_out=v_ev_w_out, od_norm_w=v_od_norm_w, od_w_in=v_od_w_in,
               od_conv_w=v_od_conv_w, od_conv_b=v_od_conv_b, od_dt_bias=v_od_dt_bias, od_a_log=v_od_a_log,
               od_d=v_od_d, od_gnorm_w=v_od_gnorm_w, od_w_out=v_od_w_out, final_norm_w=v_final_norm_w)
    shapes = {n: loc[n].shape for n in ORDER}
    loc = {n: (a.reshape(1, -1) if a.ndim == 1 else a.reshape(a.shape[-2:]) if a.ndim == 3 else a)
           for n, a in loc.items()}
    mom = {n: a.reshape(loc[n].shape) for n, a in mom.items()}
    var = {n: a.reshape(loc[n].shape) for n, a in var.items()}

    B, seq, D = x.shape
    me = 4 * lax.axis_index("x") + 2 * lax.axis_index("y") + lax.axis_index("c")

    w1_rows = loc["od_w_in"].shape[1]
    w1_pad = (-w1_rows) % SHARD_ROW_ALIGN

    def to_t(a):
        return jnp.pad(a.T, ((0, w1_pad), (0, 0)))

    loss, dx, grads, received = local_step(x.reshape(B * seq, D), loss_target.reshape(B * seq, D), loc,
                                           to_t(loc["od_w_in"]), B=B, seq=seq)

    gsmall_packed, gsmall_spans = _pack_rows([grads[n] for n in SMALL] + [loss], LANE, 8)
    (gsmall_recv,) = exchange([gsmall_packed], ["gather_direct"], name="gather_small_grads")

    big_out = [{} for _ in range(4)]
    for n in ("ev_w_in", "ev_w_out", "od_w_out"):
        for kind, a in enumerate(adamw(received[n], loc[n], mom[n], var[n], name="adamw_" + n)):
            big_out[kind][n] = a
    for kind, a in enumerate(adamw(received["od_w_in"], to_t(loc["od_w_in"]), to_t(mom["od_w_in"]),
                                   to_t(var["od_w_in"]), name="adamw_od_w_in")):
        big_out[kind]["od_w_in"] = a[:w1_rows].T

    summed = _unpack_rows(sum_slots(gsmall_recv, name="sum_small_grads"), gsmall_spans)
    loss_total = summed[-1][0, 0]
    gsmall = dict(zip(SMALL, summed[:-1]))
    for n in SMALL_SHARDED:
        width = loc[n].shape[1]
        gsmall[n] = lax.dynamic_slice_in_dim(gsmall[n], me * width, width, axis=1)
    gs, sspans = _pack_rows([gsmall[n] for n in SMALL], LANE, 8)
    ws, _ = _pack_rows([loc[n] for n in SMALL], LANE, 8)
    ms, _ = _pack_rows([mom[n] for n in SMALL], LANE, 8)
    vs, _ = _pack_rows([var[n] for n in SMALL], LANE, 8)
    small_out = [dict(zip(SMALL, _unpack_rows(a, sspans))) for a in adamw(gs[None], ws, ms, vs, name="adamw_small")]

    outs = [loss_total, dx.reshape(B, seq, D)]
    for kind in range(4):
        for n in ORDER:
            src = big_out[kind] if n in BIG else small_out[kind]
            outs.append(src[n].reshape(shapes[n]))
    return tuple(outs)
```

```python
import jax
import jax.numpy as jnp
from jax import lax
from jax.experimental import pallas as pl
from jax.experimental.pallas import tpu as pltpu

F32 = jnp.float32
BF16 = jnp.bfloat16

EPS = 1e-6
N_DEV = 8
LANE = 128
VMEM_LIMIT_BYTES = 48 * 1024 * 1024

SB_HEAD_DIM = 128
SSM_HEAD_DIM = 64
SSM_GROUPS = 4
SSM_STATE = 128
SSM_CHUNK = 128
HALO = 32
HEAD_ROWS = 8
NEG_BIG = -1e30

ADAM_LR = 0.001
ADAM_B1 = 0.9
ADAM_B2 = 0.999
ADAM_EPS = 1e-08
ADAM_WD = 0.01
ADAM_STEP = 10

NT = (((1,), (1,)), ((), ()))
TN = (((0,), (0,)), ((), ()))


def _cp(*sem):
    return pltpu.CompilerParams(dimension_semantics=sem, vmem_limit_bytes=VMEM_LIMIT_BYTES)


def _pick(n, cap, align):
    if n <= cap:
        return n
    t = (cap // align) * align
    while t >= align:
        if n % t == 0:
            return t
        t -= align
    raise ValueError(f"no tile for {n} (cap {cap}, align {align})")


def _sigmoid(x):
    return 0.5 * jnp.tanh(0.5 * x) + 0.5


def _silu(x):
    return x * _sigmoid(x)


def _silu_grad(x):
    s = _sigmoid(x)
    return s * (1.0 + x * (1.0 - s))


def _dot(a, b, dims=None):
    if dims is None:
        return jnp.dot(a, b, preferred_element_type=F32)
    return lax.dot_general(a, b, dims, preferred_element_type=F32)


def _tri_dot3(tri, x):
    hi = x.astype(BF16)
    r1 = x - hi.astype(F32)
    mid = r1.astype(BF16)
    lo = (r1 - mid.astype(F32)).astype(BF16)
    return _dot(tri, hi) + _dot(tri, mid) + _dot(tri, lo)


def _grid_step(n_inner):
    return lambda: pl.program_id(0) * n_inner + pl.program_id(1)


def mm_nn(a, b, *, add=None, out_dtype, name, comm=None):
    M, K = a.shape
    N = b.shape[1]
    tm = _pick(M, 2048 if K <= 1024 and add is None else 1024, 16)
    tn = _pick(N, 1024, LANE)

    def body(*refs):
        if add is None:
            a_ref, b_ref, o_ref = refs
        else:
            a_ref, b_ref, add_ref, o_ref = refs
        acc = _dot(a_ref[...], b_ref[...])
        if add is not None:
            acc = acc + add_ref[...]
        o_ref[...] = acc.astype(out_dtype)

    in_specs = [pl.BlockSpec((tm, K), lambda i, j: (i, 0)), pl.BlockSpec((K, tn), lambda i, j: (0, j))]
    args = [a, b]
    if add is not None:
        in_specs.append(pl.BlockSpec((tm, tn), lambda i, j: (i, j)))
        args.append(add)
    grid = (M // tm, N // tn)
    body, c_in, c_args, c_out, c_shape, c_scratch = _hosted(body, len(args), 1, comm, _grid_step(grid[1]),
                                                            grid[0] * grid[1])
    out = pl.pallas_call(
        body, name=name, grid=grid, in_specs=in_specs + c_in,
        out_specs=[pl.BlockSpec((tm, tn), lambda i, j: (i, j))] + c_out,
        out_shape=[jax.ShapeDtypeStruct((M, N), out_dtype)] + c_shape, scratch_shapes=c_scratch,
        compiler_params=_cp(*(("arbitrary",) * 2 if comm else ("parallel",) * 2)))(*args, *c_args)
    return (out[0], out[1:]) if comm else out[0]


def mm_nt_terms(terms, b, *, out_dtype, name, comm=None):
    M = terms[0][0].shape[0]
    N = b.shape[0]
    n_terms = len(terms)
    if n_terms == 1:
        tm, tn = _pick(M, 2048, 16), _pick(N, 1024, LANE)
    else:
        tm, tn = _pick(M, 256, 16), _pick(N, 1024, LANE)

    def body(*refs):
        o_ref = refs[-1]
        acc = None
        for t in range(n_terms):
            part = _dot(refs[2 * t][...], refs[2 * t + 1][...], NT)
            acc = part if acc is None else acc + part
        o_ref[...] = acc.astype(out_dtype)

    in_specs, args = [], []
    for arr, cb, w, off in terms:
        assert off % w == 0
        in_specs.append(pl.BlockSpec((tm, w), lambda i, j, cb=cb: (i, cb)))
        in_specs.append(pl.BlockSpec((tn, w), lambda i, j, ob=off // w: (j, ob)))
        args += [arr, b]
    grid = (M // tm, N // tn)
    body, c_in, c_args, c_out, c_shape, c_scratch = _hosted(body, len(args), 1, comm, _grid_step(grid[1]),
                                                            grid[0] * grid[1])
    out = pl.pallas_call(
        body, name=name, grid=grid, in_specs=in_specs + c_in,
        out_specs=[pl.BlockSpec((tm, tn), lambda i, j: (i, j))] + c_out,
        out_shape=[jax.ShapeDtypeStruct((M, N), out_dtype)] + c_shape, scratch_shapes=c_scratch,
        compiler_params=_cp(*(("arbitrary",) * 2 if comm else ("parallel",) * 2)))(*args, *c_args)
    return (out[0], out[1:]) if comm else out[0]


def mm_nn_terms(terms, b, *, out_dtype, name):
    M = terms[0][0].shape[0]
    N = b.shape[1]
    tm = _pick(M, 256, 16)
    tn = _pick(N, 1024, LANE)
    n_terms = len(terms)

    def body(*refs):
        o_ref = refs[-1]
        acc = None
        for t in range(n_terms):
            part = _dot(refs[2 * t][...], refs[2 * t + 1][...])
            acc = part if acc is None else acc + part
        o_ref[...] = acc.astype(out_dtype)

    in_specs, args = [], []
    for arr, cb, w, off in terms:
        assert off % w == 0
        in_specs.append(pl.BlockSpec((tm, w), lambda i, j, cb=cb: (i, cb)))
        in_specs.append(pl.BlockSpec((w, tn), lambda i, j, ob=off // w: (ob, j)))
        args += [arr, b]
    return pl.pallas_call(
        body, name=name, grid=(M // tm, N // tn), in_specs=in_specs,
        out_specs=pl.BlockSpec((tm, tn), lambda i, j: (i, j)),
        out_shape=jax.ShapeDtypeStruct((M, N), out_dtype),
        compiler_params=_cp("parallel", "parallel"))(*args)


def mm_tn(a, b, *, out_dtype, name):
    T, M = a.shape
    N = b.shape[1]
    tm = _pick(M, 1024, LANE)
    tn = _pick(N, 1024, LANE)
    tk = _pick(T, 2048, 16)
    nk = T // tk

    def body(a_ref, b_ref, o_ref, acc_ref):
        k = pl.program_id(2)

        @pl.when(k == 0)
        def _():
            acc_ref[...] = jnp.zeros_like(acc_ref)

        acc_ref[...] += _dot(a_ref[...], b_ref[...], TN)

        @pl.when(k == nk - 1)
        def _():
            o_ref[...] = acc_ref[...].astype(out_dtype)

    return pl.pallas_call(
        body, name=name, grid=(M // tm, N // tn, nk),
        in_specs=[pl.BlockSpec((tk, tm), lambda i, j, k: (k, i)), pl.BlockSpec((tk, tn), lambda i, j, k: (k, j))],
        out_specs=pl.BlockSpec((tm, tn), lambda i, j, k: (i, j)),
        out_shape=jax.ShapeDtypeStruct((M, N), out_dtype),
        scratch_shapes=[pltpu.VMEM((tm, tn), F32)],
        compiler_params=_cp("parallel", "parallel", "arbitrary"))(a, b)


def rmsnorm_fwd(h, w, *, name, comm=None):
    T, D = h.shape
    tt = _pick(T, 512, 16)

    def body(h_ref, w_ref, n_ref):
        x = h_ref[...]
        r = lax.rsqrt(jnp.mean(x * x, axis=-1, keepdims=True) + EPS)
        n_ref[...] = (x * r * w_ref[...]).astype(BF16)

    body, c_in, c_args, c_out, c_shape, c_scratch = _hosted(body, 2, 1, comm, lambda: pl.program_id(0), T // tt)
    out = pl.pallas_call(
        body, name=name, grid=(T // tt,),
        in_specs=[pl.BlockSpec((tt, D), lambda i: (i, 0)), pl.BlockSpec((1, D), lambda i: (0, 0))] + c_in,
        out_specs=[pl.BlockSpec((tt, D), lambda i: (i, 0))] + c_out,
        out_shape=[jax.ShapeDtypeStruct((T, D), BF16)] + c_shape, scratch_shapes=c_scratch,
        compiler_params=_cp("arbitrary" if comm else "parallel"))(h, w, *c_args)
    return (out[0], out[1:]) if comm else out[0]


def rmsnorm_bwd(h, w, dn, dres, *, name, comm=None):
    T, D = h.shape
    tt = _pick(T, 512, 16)

    def body(h_ref, w_ref, dn_ref, dres_ref, dh_ref, dhb_ref, gw_ref):
        @pl.when(pl.program_id(0) == 0)
        def _():
            gw_ref[...] = jnp.zeros_like(gw_ref)

        x = h_ref[...]
        r = lax.rsqrt(jnp.mean(x * x, axis=-1, keepdims=True) + EPS)
        xhat = x * r
        g = dn_ref[...].astype(F32)
        gw_ref[...] += jnp.sum(g * xhat, axis=0, keepdims=True)
        dxh = g * w_ref[...]
        dx = r * (dxh - xhat * jnp.mean(dxh * xhat, axis=-1, keepdims=True))
        dh = dres_ref[...] + dx
        dh_ref[...] = dh
        dhb_ref[...] = dh.astype(BF16)

    row = pl.BlockSpec((tt, D), lambda i: (i, 0))
    vec = pl.BlockSpec((1, D), lambda i: (0, 0))
    body, c_in, c_args, c_out, c_shape, c_scratch = _hosted(body, 4, 3, comm, lambda: pl.program_id(0), T // tt)
    out = pl.pallas_call(
        body, name=name, grid=(T // tt,), in_specs=[row, vec, row, row] + c_in, out_specs=[row, row, vec] + c_out,
        out_shape=[jax.ShapeDtypeStruct((T, D), F32), jax.ShapeDtypeStruct((T, D), BF16),
                   jax.ShapeDtypeStruct((1, D), F32)] + c_shape,
        scratch_shapes=c_scratch, compiler_params=_cp("arbitrary"))(h, w, dn, dres, *c_args)
    return (out[0], out[1], out[2], out[3:]) if comm else out


def final_loss(h, w, target, *, name):
    T, D = h.shape
    tt = _pick(T, 512, 16)

    def body(h_ref, w_ref, t_ref, loss_ref, dh_ref, dhb_ref, gw_ref):
        @pl.when(pl.program_id(0) == 0)
        def _():
            gw_ref[...] = jnp.zeros_like(gw_ref)
            loss_ref[...] = jnp.zeros_like(loss_ref)

        x = h_ref[...]
        r = lax.rsqrt(jnp.mean(x * x, axis=-1, keepdims=True) + EPS)
        xhat = x * r
        e = xhat * w_ref[...] - t_ref[...]
        loss_ref[...] += jnp.sum(e * e) * (0.5 / D)
        g = e * (1.0 / D)
        gw_ref[...] += jnp.sum(g * xhat, axis=0, keepdims=True)
        dxh = g * w_ref[...]
        dh = r * (dxh - xhat * jnp.mean(dxh * xhat, axis=-1, keepdims=True))
        dh_ref[...] = dh
        dhb_ref[...] = dh.astype(BF16)

    row = pl.BlockSpec((tt, D), lambda i: (i, 0))
    vec = pl.BlockSpec((1, D), lambda i: (0, 0))
    one = pl.BlockSpec((1, LANE), lambda i: (0, 0))
    return pl.pallas_call(
        body, name=name, grid=(T // tt,), in_specs=[row, vec, row], out_specs=[one, row, row, vec],
        out_shape=[jax.ShapeDtypeStruct((1, LANE), F32), jax.ShapeDtypeStruct((T, D), F32),
                   jax.ShapeDtypeStruct((T, D), BF16), jax.ShapeDtypeStruct((1, D), F32)],
        compiler_params=_cp("arbitrary"))(h, w, target)


CONV_CHUNK = 32
ROW_CHUNK = 16
SUBLANES = 8


def _conv_tiles(seq, C, K):
    return _pick(seq, 1024 if K <= SUBLANES else 512, HALO), _pick(C, 512, LANE)


def _residues(offsets):
    return sorted({s % SUBLANES for s in offsets} - {0})


def _fill_shifted(buf, shifted, residues):
    n = buf.shape[0] - SUBLANES
    for i, r in enumerate(residues):
        shifted[i, 0:n, :] = buf[r:r + n, :]


def _tap(buf, shifted, residues, offset, start, rows):
    r = offset % SUBLANES
    base = offset - r
    ref = buf if r == 0 else shifted.at[residues.index(r)]
    return ref[pl.ds(start + base, rows), :]


def dwconv_fwd(src, offs, w, b, *, C, seq, glu, silu_out, name, comm=None):
    T = src.shape[0]
    K = w.shape[0]
    assert K - 1 <= HALO
    tt, tc = _conv_tiles(seq, C, K)
    n_in = 2 if glu else 1
    per = tt // HALO
    offsets = [HALO - (K - 1) + k for k in range(K)]
    residues = _residues(offsets)

    def body(*refs):
        cur = refs[0:2 * n_in:2]
        halo = refs[1:2 * n_in:2]
        w_ref, b_ref = refs[2 * n_in], refs[2 * n_in + 1]
        outs = refs[2 * n_in + 2:-2]
        buf, shifted = refs[-2], refs[-1]
        i = pl.program_id(1)
        first = (i * tt) % seq == 0

        def pre(rs, rows):
            v = rs[0][rows, :].astype(F32)
            return v * _sigmoid(rs[1][rows, :].astype(F32)) if glu else v

        def build(ci, carry):
            start = pl.multiple_of(ci * CONV_CHUNK, CONV_CHUNK)
            buf[pl.ds(HALO + start, CONV_CHUNK), :] = pre(cur, pl.ds(start, CONV_CHUNK))
            return carry

        buf[0:HALO, :] = jnp.where(first, 0.0, pre(halo, slice(None)))
        lax.fori_loop(0, tt // CONV_CHUNK, build, 0, unroll=2)
        _fill_shifted(buf, shifted, residues)

        def chunk(ci, carry):
            start = pl.multiple_of(ci * CONV_CHUNK, CONV_CHUNK)
            acc = jnp.broadcast_to(b_ref[...], (CONV_CHUNK, tc))
            for k in range(K):
                acc = acc + w_ref[k:k + 1, :] * _tap(buf, shifted, residues, offsets[k], start, CONV_CHUNK)
            outs[0][pl.ds(start, CONV_CHUNK), :] = acc.astype(BF16)
            if silu_out:
                outs[1][pl.ds(start, CONV_CHUNK), :] = _silu(acc).astype(BF16)
            return carry

        lax.fori_loop(0, tt // CONV_CHUNK, chunk, 0)

    in_specs, args = [], []
    for off in offs:
        assert off % tc == 0
        in_specs.append(pl.BlockSpec((tt, tc), lambda j, i, ob=off // tc: (i, ob + j)))
        in_specs.append(pl.BlockSpec((HALO, tc), lambda j, i, ob=off // tc: (jnp.maximum(i * per - 1, 0), ob + j)))
        args += [src, src]
    in_specs += [pl.BlockSpec((K, tc), lambda j, i: (0, j)), pl.BlockSpec((1, tc), lambda j, i: (0, j))]
    args += [w, b]
    n_out = 2 if silu_out else 1
    grid = (C // tc, T // tt)
    body, c_in, c_args, c_out, c_shape, c_scratch = _hosted(body, len(args), n_out, comm, _grid_step(grid[1]),
                                                            grid[0] * grid[1])
    out = pl.pallas_call(
        body, name=name, grid=grid, in_specs=in_specs + c_in,
        out_specs=[pl.BlockSpec((tt, tc), lambda j, i: (i, j))] * n_out + c_out,
        out_shape=[jax.ShapeDtypeStruct((T, C), BF16)] * n_out + c_shape,
        scratch_shapes=[pltpu.VMEM((HALO + tt, tc), F32), pltpu.VMEM((max(len(residues), 1), HALO + tt, tc), F32)]
        + c_scratch,
        compiler_params=_cp("arbitrary" if comm else "parallel", "arbitrary"))(*args, *c_args)
    return (out[:n_out], out[n_out:]) if comm else out


def dwconv_bwd(du, u, src, offs, w, *, C, seq, glu, silu_out, name):
    T = src.shape[0]
    K = w.shape[0]
    assert K - 1 <= HALO
    tt, tc = _conv_tiles(seq, C, K)
    n_in = 2 if glu else 1
    per = tt // HALO
    last_blk = T // HALO - 1
    g_offsets = [K - 1 - k for k in range(K)]
    g_res = _residues(g_offsets)

    def body(*refs):
        pos = 0
        du_cur, du_nxt = refs[0], refs[1]
        pos = 2
        if silu_out:
            u_cur, u_nxt = refs[2], refs[3]
            pos = 4
        cur = refs[pos:pos + n_in]
        pos += n_in
        w_ref = refs[pos]
        outs = refs[pos + 1:pos + 1 + n_in]
        dw_ref, db_ref = refs[pos + 1 + n_in], refs[pos + 2 + n_in]
        gbuf, gshift, dw_acc, db_acc = refs[-4:]
        i = pl.program_id(1)
        last = ((i + 1) * tt) % seq == 0

        @pl.when(i == 0)
        def _():
            dw_acc[...] = jnp.zeros_like(dw_acc)
            db_acc[...] = jnp.zeros_like(db_acc)

        def build(ci, carry):
            rows = pl.ds(pl.multiple_of(ci * CONV_CHUNK, CONV_CHUNK), CONV_CHUNK)
            g = du_cur[rows, :].astype(F32)
            if silu_out:
                g = g * _silu_grad(u_cur[rows, :].astype(F32))
            gbuf[rows, :] = g
            return carry

        lax.fori_loop(0, tt // CONV_CHUNK, build, 0, unroll=2)
        g_nxt = du_nxt[...].astype(F32)
        if silu_out:
            g_nxt = g_nxt * _silu_grad(u_nxt[...].astype(F32))
        gbuf[tt:tt + HALO, :] = jnp.where(last, 0.0, g_nxt)
        _fill_shifted(gbuf, gshift, g_res)

        def fold(v):
            out = v[0:SUBLANES]
            for s in range(SUBLANES, CONV_CHUNK, SUBLANES):
                out = out + v[s:s + SUBLANES]
            return out

        def chunk(ci, carry):
            start = pl.multiple_of(ci * CONV_CHUNK, CONV_CHUNK)
            rows = pl.ds(start, CONV_CHUNK)
            a = cur[0][rows, :].astype(F32)
            if glu:
                s = _sigmoid(cur[1][rows, :].astype(F32))
                x_in = a * s
            else:
                x_in = a
            dx = jnp.zeros((CONV_CHUNK, tc), F32)
            for k in range(K):
                g_k = _tap(gbuf, gshift, g_res, g_offsets[k], start, CONV_CHUNK)
                dx = dx + w_ref[k:k + 1, :] * g_k
                dw_acc[k * SUBLANES:(k + 1) * SUBLANES, :] += fold(g_k * x_in)
            db_acc[...] += fold(gbuf[rows, :])
            if glu:
                outs[0][rows, :] = (dx * s).astype(BF16)
                outs[1][rows, :] = (dx * a * s * (1.0 - s)).astype(BF16)
            else:
                outs[0][rows, :] = dx.astype(BF16)
            return carry

        lax.fori_loop(0, tt // CONV_CHUNK, chunk, 0)

        @pl.when(i == T // tt - 1)
        def _():
            for k in range(K):
                dw_ref[k:k + 1, :] = jnp.sum(dw_acc[k * SUBLANES:(k + 1) * SUBLANES, :], axis=0, keepdims=True)
            db_ref[...] = jnp.sum(db_acc[...], axis=0, keepdims=True)

    def cur_spec(ob):
        return pl.BlockSpec((tt, tc), lambda j, i: (i, ob + j))

    def nxt_spec(ob):
        return pl.BlockSpec((HALO, tc), lambda j, i: (jnp.minimum((i + 1) * per, last_blk), ob + j))

    in_specs = [cur_spec(0), nxt_spec(0)]
    args = [du, du]
    if silu_out:
        in_specs += [cur_spec(0), nxt_spec(0)]
        args += [u, u]
    for off in offs:
        assert off % tc == 0
        in_specs.append(cur_spec(off // tc))
        args.append(src)
    in_specs.append(pl.BlockSpec((K, tc), lambda j, i: (0, j)))
    args.append(w)
    out_specs = [pl.BlockSpec((tt, tc), lambda j, i: (i, j))] * n_in
    out_specs += [pl.BlockSpec((K, tc), lambda j, i: (0, j)), pl.BlockSpec((1, tc), lambda j, i: (0, j))]
    out_shape = [jax.ShapeDtypeStruct((T, C), BF16)] * n_in
    out_shape += [jax.ShapeDtypeStruct((K, C), F32), jax.ShapeDtypeStruct((1, C), F32)]
    return pl.pallas_call(
        body, name=name, grid=(C // tc, T // tt), in_specs=in_specs, out_specs=out_specs, out_shape=out_shape,
        scratch_shapes=[pltpu.VMEM((tt + HALO, tc), F32), pltpu.VMEM((max(len(g_res), 1), tt + HALO, tc), F32),
                        pltpu.VMEM((K * SUBLANES, tc), F32), pltpu.VMEM((SUBLANES, tc), F32)],
        compiler_params=_cp("parallel", "arbitrary"))(*args)


def mix0_post_fwd(u2, proj, o, ln_w, ln_b, *, CW, gc_off, ga_off, name):
    T = u2.shape[0]
    tt = _pick(T, 512, 16)

    def body(u_ref, gc_ref, ga_ref, o_ref, lw_ref, lb_ref, y_ref):
        def chunk(ci, carry):
            rows = pl.ds(pl.multiple_of(ci * ROW_CHUNK, ROW_CHUNK), ROW_CHUNK)
            u = u_ref[rows, :].astype(F32)
            mu = jnp.mean(u, axis=-1, keepdims=True)
            xc = u - mu
            r = lax.rsqrt(jnp.mean(xc * xc, axis=-1, keepdims=True) + EPS)
            u3 = xc * r * lw_ref[...] + lb_ref[...]
            y_ref[rows, 0:CW] = (_silu(u3) * _silu(gc_ref[rows, :].astype(F32))).astype(BF16)
            y_ref[rows, CW:2 * CW] = (o_ref[rows, :].astype(F32) * _silu(ga_ref[rows, :].astype(F32))).astype(BF16)
            return carry

        lax.fori_loop(0, tt // ROW_CHUNK, chunk, 0, unroll=4)

    row = pl.BlockSpec((tt, CW), lambda i: (i, 0))
    vec = pl.BlockSpec((1, CW), lambda i: (0, 0))
    return pl.pallas_call(
        body, name=name, grid=(T // tt,),
        in_specs=[row, pl.BlockSpec((tt, CW), lambda i: (i, gc_off // CW)),
                  pl.BlockSpec((tt, CW), lambda i: (i, ga_off // CW)), row, vec, vec],
        out_specs=pl.BlockSpec((tt, 2 * CW), lambda i: (i, 0)),
        out_shape=jax.ShapeDtypeStruct((T, 2 * CW), BF16),
        compiler_params=_cp("parallel"))(u2, proj, proj, o, ln_w, ln_b)


def mix0_post_bwd(dy, u2, proj, o, ln_w, ln_b, *, CW, gc_off, ga_off, name):
    T = u2.shape[0]
    tt = _pick(T, 512, 16)

    def body(dy_ref, u_ref, gc_ref, ga_ref, o_ref, lw_ref, lb_ref, du_ref, dgc_ref, dga_ref, do_ref, dlw_ref, dlb_ref,
             lw_acc, lb_acc):
        i = pl.program_id(0)

        @pl.when(i == 0)
        def _():
            lw_acc[...] = jnp.zeros_like(lw_acc)
            lb_acc[...] = jnp.zeros_like(lb_acc)

        def fold(v):
            out = v[0:SUBLANES]
            for s in range(SUBLANES, ROW_CHUNK, SUBLANES):
                out = out + v[s:s + SUBLANES]
            return out

        def chunk(ci, carry):
            rows = pl.ds(pl.multiple_of(ci * ROW_CHUNK, ROW_CHUNK), ROW_CHUNK)
            dyc = dy_ref[rows, 0:CW].astype(F32)
            dya = dy_ref[rows, CW:2 * CW].astype(F32)
            u = u_ref[rows, :].astype(F32)
            mu = jnp.mean(u, axis=-1, keepdims=True)
            xc = u - mu
            r = lax.rsqrt(jnp.mean(xc * xc, axis=-1, keepdims=True) + EPS)
            xhat = xc * r
            u3 = xhat * lw_ref[...] + lb_ref[...]
            gc = gc_ref[rows, :].astype(F32)
            dgc_ref[rows, :] = (dyc * _silu(u3) * _silu_grad(gc)).astype(BF16)
            du3 = dyc * _silu(gc) * _silu_grad(u3)
            lw_acc[...] += fold(du3 * xhat)
            lb_acc[...] += fold(du3)
            dxh = du3 * lw_ref[...]
            du = r * (dxh - jnp.mean(dxh, axis=-1, keepdims=True)
                      - xhat * jnp.mean(dxh * xhat, axis=-1, keepdims=True))
            du_ref[rows, :] = du.astype(BF16)
            ga = ga_ref[rows, :].astype(F32)
            ov = o_ref[rows, :].astype(F32)
            do_ref[rows, :] = (dya * _silu(ga)).astype(BF16)
            dga_ref[rows, :] = (dya * ov * _silu_grad(ga)).astype(BF16)
            return carry

        lax.fori_loop(0, tt // ROW_CHUNK, chunk, 0, unroll=4)

        @pl.when(i == T // tt - 1)
        def _():
            dlw_ref[...] = jnp.sum(lw_acc[...], axis=0, keepdims=True)
            dlb_ref[...] = jnp.sum(lb_acc[...], axis=0, keepdims=True)

    row = pl.BlockSpec((tt, CW), lambda i: (i, 0))
    vec = pl.BlockSpec((1, CW), lambda i: (0, 0))
    big = jax.ShapeDtypeStruct((T, CW), BF16)
    small = jax.ShapeDtypeStruct((1, CW), F32)
    return pl.pallas_call(
        body, name=name, grid=(T // tt,),
        in_specs=[pl.BlockSpec((tt, 2 * CW), lambda i: (i, 0)), row,
                  pl.BlockSpec((tt, CW), lambda i: (i, gc_off // CW)),
                  pl.BlockSpec((tt, CW), lambda i: (i, ga_off // CW)), row, vec, vec],
        out_specs=[row, row, row, row, vec, vec],
        out_shape=[big, big, big, big, small, small],
        scratch_shapes=[pltpu.VMEM((SUBLANES, CW), F32), pltpu.VMEM((SUBLANES, CW), F32)],
        compiler_params=_cp("arbitrary"))(dy, u2, proj, proj, o, ln_w, ln_b)


SB_UNDERFLOW = 110.0
SB_BOUND_MARGIN = 1.02


def _sb_tile(seq):
    return _pick(seq, 256, LANE)


def _softplus(z):
    return jnp.maximum(z, 0.0) + jnp.log(1.0 + jnp.exp(-jnp.abs(z)))


def _tri01(n, lower):
    i = lax.broadcasted_iota(jnp.int32, (n, n), 0)
    j = lax.broadcasted_iota(jnp.int32, (n, n), 1)
    return ((i >= j) if lower else (i <= j)).astype(BF16)


SB_HEADS_FWD = 8
SB_HEADS_BWD = 2


def _sb_heads_per_step(heads, want):
    while heads % want:
        want //= 2
    return want


def sba_fwd(proj, *, B, seq, heads, q_off, k_off, v_off, name):
    dh = SB_HEAD_DIM
    tq = _sb_tile(seq)
    assert tq % (2 * LANE) == 0
    nq = seq // tq
    hps = _sb_heads_per_step(heads, SB_HEADS_FWD)
    hw = hps * dh
    scale = dh ** -0.5

    def body(q_ref, k_ref, v_ref, tri_ref, o_ref, ct_ref, acc_ref, kmax_ref):
        qi = pl.program_id(1)
        tri = tri_ref[...]
        qs = [(q_ref[:, h * dh:(h + 1) * dh].astype(F32) * scale).astype(BF16) for h in range(hps)]

        @pl.when(qi == 0)
        def _():
            def chunk(i, best):
                rows = k_ref[pl.ds(pl.multiple_of(i * tq, tq), tq), :].astype(F32)
                sq = rows * rows
                return tuple(jnp.maximum(best[h], jnp.max(jnp.sum(sq[:, h * dh:(h + 1) * dh], axis=1, keepdims=True),
                                                          axis=0, keepdims=True)) for h in range(hps))

            best = lax.fori_loop(0, nq, chunk, (jnp.zeros((1, 1), F32),) * hps)
            for h in range(hps):
                kmax_ref[h] = jnp.broadcast_to(jnp.sqrt(best[h]), (8, LANE))

        z_bound = [jnp.sqrt(jnp.sum(qs[h].astype(F32) ** 2, axis=1, keepdims=True))
                   * (SB_BOUND_MARGIN * jnp.max(kmax_ref[h], keepdims=True)) for h in range(hps)]

        def part(h, q_rows, start, n_keys, r, mask):
            k_blk = k_ref[pl.ds(start, n_keys), h * dh:(h + 1) * dh]
            v_blk = v_ref[pl.ds(start, n_keys), h * dh:(h + 1) * dh]
            z = _dot(q_rows, k_blk, NT)
            sp = _softplus(z)
            if mask is not None:
                sp = jnp.where(mask, sp, 0.0)
            wts = jnp.exp(z - (_dot(sp.astype(BF16), tri[0:n_keys, 0:n_keys]) + r))
            if mask is not None:
                wts = jnp.where(mask, wts, 0.0)
            return _dot(wts.astype(BF16), v_blk), r + jnp.sum(sp, axis=-1, keepdims=True)

        below = lax.broadcasted_iota(jnp.int32, (tq, tq), 1) < lax.broadcasted_iota(jnp.int32, (tq, tq), 0)
        has_left = qi > 0
        left = pl.multiple_of(jnp.maximum(qi - 1, 0) * tq, tq)
        rs = []
        for h in range(hps):
            pv_d, r = part(h, qs[h], pl.multiple_of(qi * tq, tq), tq, jnp.zeros((tq, 1), F32), below)
            pv_l, r = part(h, qs[h], left, tq, r, has_left)
            acc_ref[:, h * dh:(h + 1) * dh] = pv_d + pv_l
            rs.append(r)
        rs = tuple(rs)

        def block(start, rs):
            pvs, out = [], []
            for h in range(hps):
                pv, r = part(h, qs[h], start, tq, rs[h], None)
                pvs.append(pv)
                out.append(r)
            return pvs, tuple(out)

        def more(c):
            j, rs = c
            slack = rs[0] - z_bound[0]
            for h in range(1, hps):
                slack = jnp.minimum(slack, rs[h] - z_bound[h])
            return jnp.logical_and(j < qi, jnp.min(slack) <= SB_UNDERFLOW)

        def step(c):
            j, rs = c
            pvs, rs = block(pl.multiple_of((qi - 1 - j) * tq, tq), rs)
            for h in range(hps):
                acc_ref[:, h * dh:(h + 1) * dh] += pvs[h]
            return j + 1, rs

        n_left, totals = lax.while_loop(more, step, (has_left.astype(jnp.int32), rs))
        o_ref[...] = acc_ref[...].astype(BF16)
        for h in range(hps):
            ct_ref[0, 0, h, 0:8, :] = jnp.broadcast_to(totals[h], (tq, LANE)).T[0:8, :]
            ct_ref[0, 0, h, 8:16, :] = jnp.full((8, tq), n_left, F32)

    qb, kb, vb = q_off // hw, k_off // hw, v_off // hw
    G = heads // hps
    return pl.pallas_call(
        body, name=name, grid=(B * G, nq),
        in_specs=[pl.BlockSpec((tq, hw), lambda g, i: ((g // G) * nq + i, qb + g % G)),
                  pl.BlockSpec((seq, hw), lambda g, i: (g // G, kb + g % G)),
                  pl.BlockSpec((seq, hw), lambda g, i: (g // G, vb + g % G)),
                  pl.BlockSpec((tq, tq), lambda g, i: (0, 0))],
        out_specs=[pl.BlockSpec((tq, hw), lambda g, i: ((g // G) * nq + i, g % G)),
                   pl.BlockSpec((1, 1, hps, 16, tq), lambda g, i: (g // G, i, g % G, 0, 0))],
        out_shape=[jax.ShapeDtypeStruct((B * seq, heads * dh), BF16),
                   jax.ShapeDtypeStruct((B, nq, heads, 16, tq), F32)],
        scratch_shapes=[pltpu.VMEM((tq, hw), F32), pltpu.VMEM((hps, 8, LANE), F32)],
        compiler_params=_cp("parallel", "arbitrary"))(proj, proj, proj, jnp.tril(jnp.ones((tq, tq), BF16)))


def sba_bwd(proj, ctot, do, *, B, seq, heads, q_off, k_off, v_off, name, comm=None):
    dh = SB_HEAD_DIM
    tq = _sb_tile(seq)
    nq = seq // tq
    hps = _sb_heads_per_step(heads, SB_HEADS_BWD)
    hw = hps * dh
    scale = dh ** -0.5

    def body(q_ref, k_ref, v_ref, ct_ref, do_ref, sfx_ref, pre_ref, dq_ref, dk_ref, dv_ref, dq_acc, dk_acc, dv_acc):
        qi = pl.program_id(1)

        @pl.when(qi == 0)
        def _():
            dk_acc[...] = jnp.zeros_like(dk_acc)
            dv_acc[...] = jnp.zeros_like(dv_acc)

        tri_sfx = sfx_ref[...]
        tri_pre = pre_ref[...]
        qs = [(q_ref[:, h * dh:(h + 1) * dh].astype(F32) * scale).astype(BF16) for h in range(hps)]
        dos = [do_ref[:, h * dh:(h + 1) * dh] for h in range(hps)]
        totals = [jnp.max(jnp.broadcast_to(ct_ref[0, 0, h, 0:1, :], (LANE, tq)).T, axis=1, keepdims=True)
                  for h in range(hps)]
        dq_acc[...] = jnp.zeros_like(dq_acc)

        def part(h, rows, start, n_keys, pc, pg, mask):
            cols = slice(h * dh, (h + 1) * dh)
            q_rows, do_rows = qs[h][rows], dos[h][rows]
            k_blk = k_ref[pl.ds(start, n_keys), cols]
            v_blk = v_ref[pl.ds(start, n_keys), cols]
            z = _dot(q_rows, k_blk, NT)
            sp = _softplus(z)
            sig = jnp.exp(z - sp)
            if mask is not None:
                sp = jnp.where(mask, sp, 0.0)
            pc_next = pc + jnp.sum(sp, axis=-1, keepdims=True)
            wts = jnp.exp(z - (_dot(sp.astype(BF16), tri_sfx[0:n_keys, 0:n_keys]) + (totals[h][rows] - pc_next)))
            if mask is not None:
                wts = jnp.where(mask, wts, 0.0)
            g = _dot(do_rows, v_blk, NT) * wts
            dz = g - sig * (_dot(g.astype(BF16), tri_pre[0:n_keys, 0:n_keys]) + pg)
            if mask is not None:
                dz = jnp.where(mask, dz, 0.0)
            dz = dz.astype(BF16)
            dk_acc[pl.ds(start, n_keys), cols] += _dot(dz, q_rows, TN)
            dv_acc[pl.ds(start, n_keys), cols] += _dot(wts.astype(BF16), do_rows, TN)
            return pc_next, pg + jnp.sum(g, axis=-1, keepdims=True), _dot(dz, k_blk)

        def block(start, carry):
            out = []
            for h in range(hps):
                pc, pg, dq = part(h, slice(0, tq), start, tq, carry[h][0], carry[h][1], None)
                dq_acc[:, h * dh:(h + 1) * dh] += dq
                out.append((pc, pg))
            return tuple(out)

        zero = jnp.zeros((tq, 1), F32)
        n_left = jnp.max(ct_ref[0, 0, 0, 8:16, :]).astype(jnp.int32)
        carry = lax.fori_loop(qi - n_left, qi - 1, lambda j, c: block(pl.multiple_of(j * tq, tq), c),
                              ((zero, zero),) * hps)
        below = lax.broadcasted_iota(jnp.int32, (tq, tq), 1) < lax.broadcasted_iota(jnp.int32, (tq, tq), 0)
        has_left = n_left > 0
        left = pl.multiple_of(jnp.maximum(qi - 1, 0) * tq, tq)
        for h in range(hps):
            cols = slice(h * dh, (h + 1) * dh)
            pc, pg, dq_l = part(h, slice(0, tq), left, tq, carry[h][0], carry[h][1], has_left)
            _, _, dq_d = part(h, slice(0, tq), pl.multiple_of(qi * tq, tq), tq, pc, pg, below)
            dq_ref[:, cols] = ((dq_acc[:, cols] + dq_l + dq_d) * scale).astype(BF16)

        @pl.when(qi == nq - 1)
        def _():
            dk_ref[...] = dk_acc[...].astype(BF16)
            dv_ref[...] = dv_acc[...].astype(BF16)

    qb, kb, vb = q_off // hw, k_off // hw, v_off // hw
    G = heads // hps
    q_spec = pl.BlockSpec((tq, hw), lambda g, i: ((g // G) * nq + i, qb + g % G))
    o_spec = pl.BlockSpec((tq, hw), lambda g, i: ((g // G) * nq + i, g % G))
    kv_out = pl.BlockSpec((seq, hw), lambda g, i: (g // G, g % G))
    shp = jax.ShapeDtypeStruct((B * seq, heads * dh), BF16)
    body, c_in, c_args, c_out, c_shape, c_scratch = _hosted(body, 7, 3, comm, _grid_step(nq), B * G * nq)
    tri_spec = pl.BlockSpec((tq, tq), lambda g, i: (0, 0))
    ones = jnp.ones((tq, tq), BF16)
    out = pl.pallas_call(
        body, name=name, grid=(B * G, nq),
        in_specs=[q_spec,
                  pl.BlockSpec((seq, hw), lambda g, i: (g // G, kb + g % G)),
                  pl.BlockSpec((seq, hw), lambda g, i: (g // G, vb + g % G)),
                  pl.BlockSpec((1, 1, hps, 16, tq), lambda g, i: (g // G, i, g % G, 0, 0)), o_spec,
                  tri_spec, tri_spec] + c_in,
        out_specs=[o_spec, kv_out, kv_out] + c_out, out_shape=[shp, shp, shp] + c_shape,
        scratch_shapes=[pltpu.VMEM((tq, hw), F32), pltpu.VMEM((seq, hw), F32), pltpu.VMEM((seq, hw), F32)]
        + c_scratch,
        compiler_params=_cp("arbitrary" if comm else "parallel", "arbitrary"))(
            proj, proj, proj, ctot, do, jnp.tril(ones), jnp.triu(ones), *c_args)
    return (out[0], out[1], out[2], out[3:]) if comm else out


def _head_expand(n_heads, DI):
    j = jnp.arange(LANE, dtype=jnp.int32)[:, None]
    c = jnp.arange(DI, dtype=jnp.int32)[None, :] // SSM_HEAD_DIM
    return ((j == c) & (j < n_heads)).astype(BF16)


def _split3(x):
    hi = x.astype(BF16)
    r1 = x - hi.astype(F32)
    mid = r1.astype(BF16)
    return hi, mid, (r1 - mid.astype(F32)).astype(BF16)


def dt_fwd(proj, bias, a_log, expand, *, dt_off, name):
    T = proj.shape[0]
    DI = expand.shape[1]
    L = SSM_CHUNK
    tt = _pick(T, 512, L)

    def body(raw_ref, bias_ref, al_ref, e_ref, dt_ref, cs_ref, dtx_ref, csx_ref):
        x = raw_ref[...].astype(F32) + bias_ref[...]
        dt = _softplus(x)
        dt_ref[...] = dt
        la = dt * (-jnp.exp(al_ref[...]))
        tri = _tri01(L, True)
        for c in range(tt // L):
            cs_ref[c * L:(c + 1) * L, :] = _tri_dot3(tri, la[c * L:(c + 1) * L, :])
        e = e_ref[...]
        dtx_ref[...] = _dot(dt.astype(BF16), e).astype(BF16)
        hi, mid, lo = _split3(cs_ref[...])
        csx_ref[...] = _dot(hi, e) + _dot(mid, e) + _dot(lo, e)

    row = pl.BlockSpec((tt, LANE), lambda i: (i, 0))
    wide = pl.BlockSpec((tt, DI), lambda i: (i, 0))
    vec = pl.BlockSpec((1, LANE), lambda i: (0, 0))
    return pl.pallas_call(
        body, name=name, grid=(T // tt,),
        in_specs=[pl.BlockSpec((tt, LANE), lambda i: (i, dt_off // LANE)), vec, vec,
                  pl.BlockSpec((LANE, DI), lambda i: (0, 0))],
        out_specs=[row, row, wide, wide],
        out_shape=[jax.ShapeDtypeStruct((T, LANE), F32), jax.ShapeDtypeStruct((T, LANE), F32),
                   jax.ShapeDtypeStruct((T, DI), BF16), jax.ShapeDtypeStruct((T, DI), F32)],
        compiler_params=_cp("parallel"))(proj, bias, a_log, expand)


def dt_bwd(ddt_x, dcs_x, dcs_cols, proj, dt, bias, a_log, reduce_t, *, dt_off, n_heads, name):
    T = proj.shape[0]
    DI = reduce_t.shape[0]
    L = SSM_CHUNK
    tt = _pick(T, 512, L)

    def body(ddtx_ref, dcsx_ref, dcsc_ref, raw_ref, dt_ref, bias_ref, al_ref, r_ref, draw_ref, dbias_ref, dal_ref,
             dla_buf):
        @pl.when(pl.program_id(0) == 0)
        def _():
            dbias_ref[...] = jnp.zeros_like(dbias_ref)
            dal_ref[...] = jnp.zeros_like(dal_ref)

        r = r_ref[...]
        ddt = _dot(ddtx_ref[...], r)
        dx = dcsx_ref[...]
        hi = dx.astype(BF16)
        dcs = _dot(hi, r) + _dot((dx - hi.astype(F32)).astype(BF16), r) + dcsc_ref[...]
        triu = _tri01(L, False)
        for c in range(tt // L):
            dla_buf[c * L:(c + 1) * L, :] = _tri_dot3(triu, dcs[c * L:(c + 1) * L, :])
        dla = dla_buf[...]
        a = -jnp.exp(al_ref[...])
        valid = lax.broadcasted_iota(jnp.int32, (tt, LANE), 1) < n_heads
        dal_ref[...] += jnp.sum(jnp.where(valid, dla * dt_ref[...], 0.0), axis=0, keepdims=True) * a
        x = raw_ref[...].astype(F32) + bias_ref[...]
        draw = jnp.where(valid, (ddt + dla * a) * _sigmoid(x), 0.0)
        dbias_ref[...] += jnp.sum(draw, axis=0, keepdims=True)
        draw_ref[...] = draw.astype(BF16)

    row = pl.BlockSpec((tt, LANE), lambda i: (i, 0))
    wide = pl.BlockSpec((tt, DI), lambda i: (i, 0))
    vec = pl.BlockSpec((1, LANE), lambda i: (0, 0))
    return pl.pallas_call(
        body, name=name, grid=(T // tt,),
        in_specs=[wide, wide, row, pl.BlockSpec((tt, LANE), lambda i: (i, dt_off // LANE)), row, vec, vec,
                  pl.BlockSpec((DI, LANE), lambda i: (0, 0))],
        out_specs=[row, vec, vec],
        out_shape=[jax.ShapeDtypeStruct((T, LANE), BF16), jax.ShapeDtypeStruct((1, LANE), F32),
                   jax.ShapeDtypeStruct((1, LANE), F32)],
        scratch_shapes=[pltpu.VMEM((tt, LANE), F32)],
        compiler_params=_cp("arbitrary"))(ddt_x, dcs_x, dcs_cols, proj, dt, bias, a_log, reduce_t)


def _pair_terms(x_ref, dtx_ref, csx_ref, csr_ref, pair, ppg, lo_half, causal):
    L = SSM_CHUNK
    g, pp = divmod(pair, ppg)
    ra = g * HEAD_ROWS + 2 * pp
    cols = slice(pair * LANE, (pair + 1) * LANE)
    X = x_ref[:, cols].astype(F32)
    dt_p = dtx_ref[:, cols].astype(F32)
    own = csx_ref[:, cols]
    other = pltpu.roll(own, SSM_HEAD_DIM, 1)
    csa_c = jnp.where(lo_half, own, other)
    csb_c = jnp.where(lo_half, other, own)
    La = jnp.exp(jnp.where(causal, csa_c - csr_ref[ra:ra + 1, :], NEG_BIG))
    Lb = jnp.exp(jnp.where(causal, csb_c - csr_ref[ra + 1:ra + 2, :], NEG_BIG))
    last = csx_ref[L - 1:L, cols]
    return g, ra, cols, X, dt_p, La, Lb, jnp.exp(own), jnp.exp(last - own), jnp.exp(last)


def scan_fwd(xbc, dt_x, cs_x, cs_row, d_full, *, B, seq, DI, name):
    L, N, G = SSM_CHUNK, SSM_STATE, SSM_GROUPS
    nc = seq // L
    XW = xbc.shape[1]
    n_pairs = DI // LANE
    ppg = n_pairs // G

    def body(x_ref, dtx_ref, csx_ref, csr_ref, d_ref, y_ref, st_ref, state):
        @pl.when(pl.program_id(0) == 0)
        def _():
            state[...] = jnp.zeros_like(state)

        causal = lax.broadcasted_iota(jnp.int32, (L, L), 0) >= lax.broadcasted_iota(jnp.int32, (L, L), 1)
        lo_half = lax.broadcasted_iota(jnp.int32, (L, LANE), 1) < SSM_HEAD_DIM
        for b in range(B):
            xb, yb = x_ref.at[b], y_ref.at[b]
            cbs = []
            for g in range(G):
                Bc = xb[:, DI + g * N:DI + (g + 1) * N]
                Cc = xb[:, DI + G * N + g * N:DI + G * N + (g + 1) * N]
                cbs.append((Bc, Cc, _dot(Cc, Bc, NT)))
            for pair in range(n_pairs):
                g, _, cols, X, dt_p, La, Lb, ecs, tail, e_last = _pair_terms(
                    xb, dtx_ref.at[b], csx_ref.at[b], csr_ref.at[b], pair, ppg, lo_half, causal)
                Bc, Cc, CB = cbs[g]
                xs = X * dt_p
                xsb = xs.astype(BF16)
                y = jnp.where(lo_half, _dot((CB * La).astype(BF16), xsb), _dot((CB * Lb).astype(BF16), xsb))
                ST = state[b, pair]
                st_ref[b, 0, pair] = ST
                y = y + ecs * _dot(Cc, ST.astype(BF16)) + d_ref[:, cols] * X
                yb[:, cols] = y.astype(BF16)
                state[b, pair] = e_last * ST + _dot(Bc, (xs * tail).astype(BF16), TN)

    wide = pl.BlockSpec((B, L, DI), lambda c: (0, c, 0))
    y, states = pl.pallas_call(
        body, name=name, grid=(nc,),
        in_specs=[pl.BlockSpec((B, L, XW), lambda c: (0, c, 0)), wide, wide,
                  pl.BlockSpec((B, G * HEAD_ROWS, L), lambda c: (0, 0, c)),
                  pl.BlockSpec((1, DI), lambda c: (0, 0))],
        out_specs=[wide, pl.BlockSpec((B, 1, n_pairs, N, LANE), lambda c: (0, c, 0, 0, 0))],
        out_shape=[jax.ShapeDtypeStruct((B, seq, DI), BF16),
                   jax.ShapeDtypeStruct((B, nc, n_pairs, N, LANE), F32)],
        scratch_shapes=[pltpu.VMEM((B, n_pairs, N, LANE), F32)],
        compiler_params=_cp("arbitrary"))(xbc.reshape(B, seq, XW), dt_x.reshape(B, seq, DI),
                                          cs_x.reshape(B, seq, DI), cs_row, d_full)
    return y.reshape(B * seq, DI), states


def scan_bwd(xbc, dt_x, cs_x, cs_row, d_full, states, dy, *, B, seq, DI, name):
    L, N, G = SSM_CHUNK, SSM_STATE, SSM_GROUPS
    nc = seq // L
    XW = xbc.shape[1]
    n_pairs = DI // LANE
    ppg = n_pairs // G
    HR = G * HEAD_ROWS
    inv_p = 1.0 / SSM_HEAD_DIM

    def body(x_ref, dtx_ref, csx_ref, csr_ref, d_ref, st_ref, dy_ref, dx_ref, ddtx_ref, dcsx_ref, dcsr_ref, dd_ref,
             dH):
        @pl.when(pl.program_id(0) == 0)
        def _():
            dH[...] = jnp.zeros_like(dH)
            dd_ref[...] = jnp.zeros_like(dd_ref)

        causal = lax.broadcasted_iota(jnp.int32, (L, L), 0) >= lax.broadcasted_iota(jnp.int32, (L, L), 1)
        lo_half = lax.broadcasted_iota(jnp.int32, (L, LANE), 1) < SSM_HEAD_DIM
        last_row = lax.broadcasted_iota(jnp.int32, (L, LANE), 0) == L - 1
        head_row = lax.broadcasted_iota(jnp.int32, (HR, 1), 0)
        for b in range(B):
            xb, dxb = x_ref.at[b], dx_ref.at[b]
            dcs_rows = jnp.zeros((HR, L), F32)

            for g in range(G):
                Bc = xb[:, DI + g * N:DI + (g + 1) * N]
                Cc = xb[:, DI + G * N + g * N:DI + G * N + (g + 1) * N]
                CB = _dot(Cc, Bc, NT)
                dCB = jnp.zeros((L, L), F32)
                dC = jnp.zeros((L, N), F32)
                dB = jnp.zeros((L, N), F32)
                for pp in range(ppg):
                    pair = g * ppg + pp
                    _, ra, cols, X, dt_p, La, Lb, ecs, tail, e_last = _pair_terms(
                        xb, dtx_ref.at[b], csx_ref.at[b], csr_ref.at[b], pair, ppg, lo_half, causal)
                    xs = X * dt_p
                    xsb = xs.astype(BF16)
                    Ma, Mb = CB * La, CB * Lb
                    dY = dy_ref[b, :, cols].astype(F32)
                    dYb = dY.astype(BF16)
                    dMa = _dot(jnp.where(lo_half, dY, 0.0).astype(BF16), xsb, NT)
                    dMb = _dot(jnp.where(lo_half, 0.0, dY).astype(BF16), xsb, NT)
                    dSa, dSb = dMa * Ma, dMb * Mb
                    dCB = dCB + dMa * La + dMb * Lb
                    dcs = jnp.where(lo_half, jnp.sum(dSa, axis=1, keepdims=True),
                                    jnp.sum(dSb, axis=1, keepdims=True)) * inv_p
                    dcs_rows = dcs_rows - jnp.where(head_row == ra, jnp.sum(dSa, axis=0, keepdims=True), 0.0)
                    dcs_rows = dcs_rows - jnp.where(head_row == ra + 1, jnp.sum(dSb, axis=0, keepdims=True), 0.0)
                    dxs = jnp.where(lo_half, _dot(Ma.astype(BF16), dYb, TN), _dot(Mb.astype(BF16), dYb, TN))
                    ST = st_ref[b, 0, pair]
                    STb = ST.astype(BF16)
                    dYe = (dY * ecs).astype(BF16)
                    dC = dC + _dot(dYe, STb, NT)
                    dSTp = _dot(Cc, dYe, TN)
                    dcs = dcs + dY * (ecs * _dot(Cc, STb))
                    dSTn = dH[b, pair]
                    dSTnb = dSTn.astype(BF16)
                    dSTp = dSTp + e_last * dSTn
                    XBt = _dot(Bc, dSTnb)
                    dxs = dxs + tail * XBt
                    t2 = xs * XBt * tail
                    at_end = (e_last * jnp.sum(dSTn * ST, axis=0, keepdims=True)
                              + jnp.sum(t2, axis=0, keepdims=True))
                    dcs = dcs - t2 + jnp.where(last_row, at_end, 0.0)
                    dB = dB + _dot((xs * tail).astype(BF16), dSTnb, NT)
                    dxb[:, cols] = (dxs * dt_p + d_ref[:, cols] * dY).astype(BF16)
                    ddtx_ref[b, :, cols] = (dxs * X).astype(BF16)
                    dcsx_ref[b, :, cols] = dcs
                    dd_ref[b, :, cols] += jnp.sum(dY * X, axis=0, keepdims=True)
                    dH[b, pair] = dSTp
                dCBb = dCB.astype(BF16)
                dxb[:, DI + g * N:DI + (g + 1) * N] = (dB + _dot(dCBb, Cc, TN)).astype(BF16)
                dxb[:, DI + G * N + g * N:DI + G * N + (g + 1) * N] = (dC + _dot(dCBb, Bc)).astype(BF16)
            dcsr_ref[b] = dcs_rows

    rev = lambda c: (0, nc - 1 - c, 0)
    wide = pl.BlockSpec((B, L, DI), rev)
    xspec = pl.BlockSpec((B, L, XW), rev)
    hrow = pl.BlockSpec((B, HR, L), lambda c: (0, 0, nc - 1 - c))
    dx, ddt_x, dcs_x, dcs_row, dd = pl.pallas_call(
        body, name=name, grid=(nc,),
        in_specs=[xspec, wide, wide, hrow, pl.BlockSpec((1, DI), lambda c: (0, 0)),
                  pl.BlockSpec((B, 1, n_pairs, N, LANE), lambda c: (0, nc - 1 - c, 0, 0, 0)), wide],
        out_specs=[xspec, wide, wide, hrow, pl.BlockSpec((B, 1, DI), lambda c: (0, 0, 0))],
        out_shape=[jax.ShapeDtypeStruct((B, seq, XW), BF16), jax.ShapeDtypeStruct((B, seq, DI), BF16),
                   jax.ShapeDtypeStruct((B, seq, DI), F32), jax.ShapeDtypeStruct((B, HR, seq), F32),
                   jax.ShapeDtypeStruct((B, 1, DI), F32)],
        scratch_shapes=[pltpu.VMEM((B, n_pairs, N, LANE), F32)],
        compiler_params=_cp("arbitrary"))(xbc.reshape(B, seq, XW), dt_x.reshape(B, seq, DI), cs_x.reshape(B, seq, DI),
                                          cs_row, d_full, states, dy.reshape(B, seq, DI))
    return dx.reshape(B * seq, XW), ddt_x.reshape(B * seq, DI), dcs_x.reshape(B * seq, DI), dcs_row, dd


def gnorm_fwd(y, proj, w, *, DI, name):
    T = y.shape[0]
    tt = _pick(T, 512, 16)
    gw = DI // SSM_GROUPS

    def body(y_ref, z_ref, w_ref, o_ref):
        for g in range(SSM_GROUPS):
            sl = slice(g * gw, (g + 1) * gw)
            y2 = y_ref[:, sl].astype(F32) * _silu(z_ref[:, sl].astype(F32))
            r = lax.rsqrt(jnp.mean(y2 * y2, axis=-1, keepdims=True) + EPS)
            o_ref[:, sl] = (y2 * r * w_ref[:, sl]).astype(BF16)

    row = pl.BlockSpec((tt, DI), lambda i: (i, 0))
    return pl.pallas_call(
        body, name=name, grid=(T // tt,),
        in_specs=[row, row, pl.BlockSpec((1, DI), lambda i: (0, 0))], out_specs=row,
        out_shape=jax.ShapeDtypeStruct((T, DI), BF16), compiler_params=_cp("parallel"))(y, proj, w)


def gnorm_bwd(dyn, y, proj, w, *, DI, name):
    T = y.shape[0]
    tt = _pick(T, 512, 16)
    gw = DI // SSM_GROUPS

    def body(dyn_ref, y_ref, z_ref, w_ref, dy_ref, dz_ref, dw_ref):
        @pl.when(pl.program_id(0) == 0)
        def _():
            dw_ref[...] = jnp.zeros_like(dw_ref)

        for g in range(SSM_GROUPS):
            sl = slice(g * gw, (g + 1) * gw)
            yv = y_ref[:, sl].astype(F32)
            z = z_ref[:, sl].astype(F32)
            sz = _silu(z)
            y2 = yv * sz
            r = lax.rsqrt(jnp.mean(y2 * y2, axis=-1, keepdims=True) + EPS)
            xhat = y2 * r
            d = dyn_ref[:, sl].astype(F32)
            dw_ref[:, sl] += jnp.sum(d * xhat, axis=0, keepdims=True)
            dxh = d * w_ref[:, sl]
            dy2 = r * (dxh - xhat * jnp.mean(dxh * xhat, axis=-1, keepdims=True))
            dy_ref[:, sl] = (dy2 * sz).astype(BF16)
            dz_ref[:, sl] = (dy2 * yv * _silu_grad(z)).astype(BF16)

    row = pl.BlockSpec((tt, DI), lambda i: (i, 0))
    vec = pl.BlockSpec((1, DI), lambda i: (0, 0))
    shp = jax.ShapeDtypeStruct((T, DI), BF16)
    return pl.pallas_call(
        body, name=name, grid=(T // tt,), in_specs=[row, row, row, vec], out_specs=[row, row, vec],
        out_shape=[shp, shp, jax.ShapeDtypeStruct((1, DI), F32)],
        compiler_params=_cp("arbitrary"))(dyn, y, proj, w)


N_CHIP = 4
SHARD_ROW_ALIGN = 128


def _comm_out_shapes(srcs, modes):
    return [jax.ShapeDtypeStruct(((N_DEV,) if mode in ("gather", "gather_direct") else ()) + s.shape, s.dtype)
            for s, mode in zip(srcs, modes)]


def _comm_scratch(n):
    return [pltpu.SemaphoreType.DMA((n, N_DEV - 1)), pltpu.SemaphoreType.DMA((n, N_DEV - 1)),
            pltpu.SemaphoreType.DMA((n,))]


def _comm_phases(modes, src_refs, out_refs, send_sems, recv_sems, local_sems):
    x, y, c = lax.axis_index("x"), lax.axis_index("y"), lax.axis_index("c")
    me, sibling = (x, y, c), (x, y, 1 - c)
    chips = [(1 - x, y), (x, 1 - y), (1 - x, 1 - y)]
    relays = [a for a, mode in enumerate(modes) if mode == "gather"]

    def slot(p):
        return 4 * p[0] + 2 * p[1] + p[2]

    def remote(a, k, src, dst, to):
        return pltpu.make_async_remote_copy(src_ref=src, dst_ref=dst, send_sem=send_sems.at[a, k],
                                            recv_sem=recv_sems.at[a, k], device_id=to,
                                            device_id_type=pl.DeviceIdType.MESH)

    def first_copies():
        local, two_way, send_only = [], [], []
        for a, mode in enumerate(modes):
            src, out = src_refs[a], out_refs[a]
            if mode == "sibling":
                two_way.append(remote(a, 0, src, out, sibling))
            elif mode == "chips":
                mine = 2 * x + y
                local.append(pltpu.make_async_copy(src.at[mine], out.at[mine], local_sems.at[a]))
                for j, chip in enumerate(chips):
                    two_way.append(remote(a, 1 + j, src.at[2 * chip[0] + chip[1]], out.at[mine], (*chip, c)))
            elif mode == "gather_direct":
                local.append(pltpu.make_async_copy(src, out.at[slot(me)], local_sems.at[a]))
                for k in range(1, N_DEV):
                    peer = (1 - x if k & 4 else x, 1 - y if k & 2 else y, 1 - c if k & 1 else c)
                    two_way.append(remote(a, k - 1, src, out.at[slot(me)], peer))
            else:
                assert mode == "gather"
                local.append(pltpu.make_async_copy(src, out.at[slot(me)], local_sems.at[a]))
                send_only.append(remote(a, 0, src, out.at[slot(me)], sibling))
                for j, chip in enumerate(chips):
                    send_only.append(remote(a, 1 + j, src, out.at[slot(me)], (*chip, c)))
        return local, two_way, send_only

    def forwards():
        out = []
        for a in relays:
            for j, chip in enumerate(chips):
                landed = out_refs[a].at[slot((*chip, c))]
                out.append((remote(a, 1 + j, landed, landed, me), remote(a, 4 + j, landed, landed, sibling)))
        return out

    def start():
        local, two_way, send_only = first_copies()
        for cp in local + two_way + send_only:
            cp.start()

    def relay():
        for arrival, fwd in forwards():
            arrival.wait_recv()
            fwd.start()

    def finish():
        local, two_way, send_only = first_copies()
        for a in relays:
            blk = out_refs[a].at[slot(sibling)]
            remote(a, 0, blk, blk, me).wait_recv()
            for j, chip in enumerate(chips):
                blk = out_refs[a].at[slot((*chip, 1 - c))]
                remote(a, 4 + j, blk, blk, me).wait_recv()
        for cp in send_only + [fwd for _, fwd in forwards()]:
            cp.wait_send()
        for cp in two_way + local:
            cp.wait()

    return start, relay, finish, bool(relays)


def _hosted(body, n_in, n_out, comm, step, n_steps):
    if comm is None:
        return body, [], [], [], [], []
    srcs, modes = comm
    nc = len(srcs)

    def wrapped(*refs):
        ins, csrc = refs[:n_in], refs[n_in:n_in + nc]
        outs = refs[n_in + nc:n_in + nc + n_out]
        cout = refs[n_in + nc + n_out:n_in + 2 * nc + n_out]
        scratch = refs[n_in + 2 * nc + n_out:len(refs) - 3]
        start, relay, finish, has_relay = _comm_phases(modes, csrc, cout, *refs[len(refs) - 3:])
        s = step()
        pl.when(s == 0)(start)
        body(*ins, *outs, *scratch)
        if has_relay:
            pl.when(s == (2 * n_steps) // 3)(relay)
        pl.when(s == n_steps - 1)(finish)

    any_spec = pl.BlockSpec(memory_space=pl.ANY)
    return wrapped, [any_spec] * nc, list(srcs), [any_spec] * nc, _comm_out_shapes(srcs, modes), _comm_scratch(nc)


def exchange(srcs, modes, *, name):
    n = len(srcs)

    def body(*refs):
        start, relay, finish, has_relay = _comm_phases(modes, refs[:n], refs[n:2 * n], *refs[2 * n:])
        start()
        if has_relay:
            relay()
        finish()

    any_spec = pl.BlockSpec(memory_space=pl.ANY)
    return pl.pallas_call(
        body, name=name, in_specs=[any_spec] * n, out_specs=[any_spec] * n, out_shape=_comm_out_shapes(srcs, modes),
        scratch_shapes=_comm_scratch(n), compiler_params=pltpu.CompilerParams(has_side_effects=True))(*srcs)


def pair_sum(a, b, *, name):
    n, R, C = a.shape
    tr = _pick(n * R, 1024, 16)

    def body(a_ref, b_ref, o_ref):
        o_ref[...] = (a_ref[...].astype(F32) + b_ref[...].astype(F32)).astype(BF16)

    blk = pl.BlockSpec((tr, C), lambda i: (i, 0))
    out = pl.pallas_call(
        body, name=name, grid=(n * R // tr,), in_specs=[blk, blk], out_specs=blk,
        out_shape=jax.ShapeDtypeStruct((n * R, C), BF16),
        compiler_params=_cp("parallel"))(a.reshape(n * R, C), b.reshape(n * R, C))
    return out.reshape(n, R, C)


def sum_slots(recv, *, name):
    _, R, C = recv.shape
    tr = _pick(R, 512, 8)

    def body(r_ref, o_ref):
        acc = r_ref[0].astype(F32)
        for p in range(1, N_DEV):
            acc = acc + r_ref[p].astype(F32)
        o_ref[...] = acc

    return pl.pallas_call(
        body, name=name, grid=(R // tr,),
        in_specs=[pl.BlockSpec((N_DEV, tr, C), lambda i: (0, i, 0))],
        out_specs=pl.BlockSpec((tr, C), lambda i: (i, 0)),
        out_shape=jax.ShapeDtypeStruct((R, C), F32), compiler_params=_cp("parallel"))(recv)


def adamw(gsrc, w, m, v, *, name):
    slots, R, C = gsrc.shape
    tr = _pick(R, 256, 16 if gsrc.dtype == BF16 else 8)
    c1 = 1.0 / (1.0 - ADAM_B1 ** ADAM_STEP)
    c2 = 1.0 / (1.0 - ADAM_B2 ** ADAM_STEP)

    def body(g_ref, w_ref, m_ref, v_ref, go_ref, d_ref, mo_ref, vo_ref):
        g = g_ref[0].astype(F32)
        for p in range(1, slots):
            g = g + g_ref[p].astype(F32)
        m2 = ADAM_B1 * m_ref[...] + (1.0 - ADAM_B1) * g
        v2 = ADAM_B2 * v_ref[...] + (1.0 - ADAM_B2) * (g * g)
        go_ref[...] = g
        mo_ref[...] = m2
        vo_ref[...] = v2
        d_ref[...] = -ADAM_LR * ((m2 * c1) / (jnp.sqrt(v2 * c2) + ADAM_EPS) + ADAM_WD * w_ref[...])

    blk = pl.BlockSpec((tr, C), lambda i: (i, 0))
    shp = jax.ShapeDtypeStruct((R, C), F32)
    return pl.pallas_call(
        body, name=name, grid=(R // tr,),
        in_specs=[pl.BlockSpec((slots, tr, C), lambda i: (0, i, 0)), blk, blk, blk],
        out_specs=[blk] * 4, out_shape=[shp] * 4, compiler_params=_cp("parallel"))(gsrc, w, m, v)


def _pad_cols(a, n):
    return jnp.pad(a, ((0, 0), (0, n - a.shape[1])))


def _to_rows(a, B, seq, H):
    G = SSM_GROUPS
    R = H // G
    t = a[:, :H].reshape(B, seq, G, R).transpose(0, 2, 3, 1)
    t = jnp.pad(t, ((0, 0), (0, 0), (0, HEAD_ROWS - R), (0, 0)))
    return t.reshape(B, G * HEAD_ROWS, seq)


def _from_rows(a, B, seq, H):
    G = SSM_GROUPS
    R = H // G
    t = a.reshape(B, G, HEAD_ROWS, seq)[:, :, :R].transpose(0, 3, 1, 2).reshape(B * seq, H)
    return _pad_cols(t, LANE)


def _chip_split(grads):
    c_idx = lax.axis_index("c")
    keep, give = [], []
    for g in grads:
        by_chip = g.reshape((N_CHIP, 2) + g.shape[1:])
        keep.append(lax.dynamic_index_in_dim(by_chip, c_idx, axis=1, keepdims=False))
        give.append(lax.dynamic_index_in_dim(by_chip, 1 - c_idx, axis=1, keepdims=False))
    return keep, give


def _chip_sums(grads, name):
    keep, give = _chip_split(grads)
    swapped = exchange(give, ["sibling"] * len(give), name="swap_" + name)
    return [pair_sum(k, s, name=f"chip_sum_{name}_{i}") for i, (k, s) in enumerate(zip(keep, swapped))]


def local_step(x, target, loc, od_w_in_t, *, B, seq):
    T, D = x.shape
    CW = D
    heads = CW // SB_HEAD_DIM
    DI = 2 * D
    H = DI // SSM_HEAD_DIM
    XW = DI + 2 * SSM_GROUPS * SSM_STATE
    in_odd = DI + XW + H
    w1_rows = in_odd // N_DEV
    q_off, k_off, v_off, gc_off, ga_off = 3 * CW, 4 * CW, 5 * CW, 2 * CW, 6 * CW
    dt_off = DI + XW

    small_packed, small_spans = _pack_rows([loc[n] for n in SMALL_SHARDED], LANE, 8)
    n0, (g_ev_in, small_all) = rmsnorm_fwd(x, loc["ev_norm_w"], name="l0_norm",
                                           comm=([loc["ev_w_in"].astype(BF16), small_packed],
                                                 ["gather", "gather_direct"]))
    p = {n: loc[n] for n in SMALL}
    for n, a in zip(SMALL_SHARDED, _unpack_rows(small_all, small_spans)):
        p[n] = _col_unshards(a)
    p["ev_w_in"] = _col_unshards(g_ev_in)
    proj0, (g_od_in_t,) = mm_nn(n0, p["ev_w_in"], out_dtype=BF16, name="l0_in_proj",
                                comm=([od_w_in_t.astype(BF16)], ["gather"]))
    w1t = g_od_in_t[:, :w1_rows].reshape(in_odd, D)
    w1t = jnp.pad(w1t, ((0, -(-(in_odd + LANE) // 256) * 256 - in_odd), (0, 0)))
    (u2,), (g_ev_out, g_od_out) = dwconv_fwd(
        proj0, (0, CW), p["ev_dw_w"], p["ev_dw_b"], C=CW, seq=seq, glu=True, silu_out=False, name="l0_conv",
        comm=([loc["ev_w_out"].astype(BF16), loc["od_w_out"].astype(BF16)], ["gather"] * 2))
    p["ev_w_out"] = g_ev_out.reshape(-1, D)
    od_w_out = g_od_out.reshape(-1, D)
    o, ctot = sba_fwd(proj0, B=B, seq=seq, heads=heads, q_off=q_off, k_off=k_off, v_off=v_off, name="l0_attn")
    ycat = mix0_post_fwd(u2, proj0, o, p["ev_ln_w"], p["ev_ln_b"], CW=CW, gc_off=gc_off, ga_off=ga_off,
                         name="l0_post")
    h1 = mm_nn(ycat, p["ev_w_out"], add=x, out_dtype=F32, name="l0_out_proj")

    n1 = rmsnorm_fwd(h1, p["od_norm_w"], name="l1_norm")
    proj1 = mm_nt_terms([(n1, 0, D, 0)], w1t, out_dtype=BF16, name="l1_in_proj")
    u_pre, xbc = dwconv_fwd(proj1, (DI,), p["od_conv_w"], p["od_conv_b"], C=XW, seq=seq, glu=False, silu_out=True,
                            name="l1_conv")
    bias_p, alog_p = _pad_cols(p["od_dt_bias"], LANE), _pad_cols(p["od_a_log"], LANE)
    expand = _head_expand(H, DI)
    dt, cs, dt_x, cs_x = dt_fwd(proj1, bias_p, alog_p, expand, dt_off=dt_off, name="l1_dt")
    cs_row = _to_rows(cs, B, seq, H)
    d_full = jnp.repeat(p["od_d"], SSM_HEAD_DIM, axis=1)
    y_ssd, states = scan_fwd(xbc, dt_x, cs_x, cs_row, d_full, B=B, seq=seq, DI=DI, name="l1_ssd")
    yn = gnorm_fwd(y_ssd, proj1, p["od_gnorm_w"], DI=DI, name="l1_gnorm")
    h2 = mm_nn(yn, od_w_out, add=h1, out_dtype=F32, name="l1_out_proj")

    loss, dh2, dh2b, g_final = final_loss(h2, p["final_norm_w"], target, name="loss_head")

    g_od_w_out = mm_tn(yn, dh2b, out_dtype=BF16, name="l1_dw_out")
    dyn = mm_nt_terms([(dh2b, 0, D, 0)], od_w_out, out_dtype=BF16, name="l1_d_out_proj")
    dy_ssd, dz, g_gnorm = gnorm_bwd(dyn, y_ssd, proj1, p["od_gnorm_w"], DI=DI, name="l1_gnorm_bwd")
    dxbc_c, ddt_x, dcs_x, dcs_row, dd_part = scan_bwd(xbc, dt_x, cs_x, cs_row, d_full, states, dy_ssd, B=B, seq=seq,
                                                      DI=DI, name="l1_ssd_bwd")
    g_d = dd_part.sum(axis=(0, 1)).reshape(H, SSM_HEAD_DIM).sum(axis=1)[None, :]
    draw, g_bias, g_alog = dt_bwd(ddt_x, dcs_x, _from_rows(dcs_row, B, seq, H), proj1, dt, bias_p, alog_p, expand.T,
                                  dt_off=dt_off, n_heads=H, name="l1_dt_bwd")
    dxbc, g_conv_w, g_conv_b = dwconv_bwd(dxbc_c, u_pre, proj1, (DI,), p["od_conv_w"], C=XW, seq=seq, glu=False,
                                          silu_out=True, name="l1_conv_bwd")
    tw = 512 if DI % 512 == 0 else LANE
    terms = [(dz, j, tw, j * tw) for j in range(DI // tw)]
    terms += [(dxbc, j, tw, DI + j * tw) for j in range(XW // tw)]
    terms += [(draw, 0, LANE, dt_off)]
    dn1 = mm_nn_terms(terms, w1t, out_dtype=BF16, name="l1_d_in_proj")
    g_od_w_in_t = jnp.concatenate([mm_tn(dz, n1, out_dtype=BF16, name="l1_dw_in_z"),
                                   mm_tn(dxbc, n1, out_dtype=BF16, name="l1_dw_in_xbc"),
                                   mm_tn(draw, n1, out_dtype=BF16, name="l1_dw_in_dt")], axis=0)[:in_odd]
    w1_pad = (-w1_rows) % SHARD_ROW_ALIGN
    keep, give = _chip_split([jnp.pad(g_od_w_in_t.reshape(N_DEV, w1_rows, D), ((0, 0), (0, w1_pad), (0, 0))),
                              g_od_w_out.reshape(N_DEV, -1, D)])
    dh1, dh1b, g_od_norm, swapped = rmsnorm_bwd(h1, p["od_norm_w"], dn1, dh2, name="l1_norm_bwd",
                                                comm=(give, ["sibling"] * 2))
    l1_chip = [pair_sum(k, s, name=f"chip_sum_l1_{i}") for i, (k, s) in enumerate(zip(keep, swapped))]

    g_ev_w_out = mm_tn(ycat, dh1b, out_dtype=BF16, name="l0_dw_out")
    dycat = mm_nt_terms([(dh1b, 0, D, 0)], p["ev_w_out"], out_dtype=BF16, name="l0_d_out_proj")
    du2, dgc, dga, do, g_ln_w, g_ln_b = mix0_post_bwd(dycat, u2, proj0, o, p["ev_ln_w"], p["ev_ln_b"], CW=CW,
                                                      gc_off=gc_off, ga_off=ga_off, name="l0_post_bwd")
    dq, dk, dv, (r_od_in_t, r_od_out) = sba_bwd(proj0, ctot, do, B=B, seq=seq, heads=heads, q_off=q_off, k_off=k_off,
                                                v_off=v_off, name="l0_attn_bwd", comm=(l1_chip, ["chips", "chips"]))
    dga_a, dga_b, g_dw_w, g_dw_b = dwconv_bwd(du2, None, proj0, (0, CW), p["ev_dw_w"], C=CW, seq=seq, glu=True,
                                              silu_out=False, name="l0_conv_bwd")
    pieces = [dga_a, dga_b, dgc, dq, dk, dv, dga]
    g_ev_w_in = jnp.concatenate([mm_tn(n0, pc, out_dtype=BF16, name=f"l0_dw_in_{j}") for j, pc in enumerate(pieces)],
                                axis=1)
    l0_chip = _chip_sums([_col_shards(g_ev_w_in), g_ev_w_out.reshape(N_DEV, -1, D)], "l0")
    dn0, (r_ev_in, r_ev_out) = mm_nt_terms([(pc, 0, CW, j * CW) for j, pc in enumerate(pieces)], p["ev_w_in"],
                                           out_dtype=BF16, name="l0_d_in_proj", comm=(l0_chip, ["chips", "chips"]))
    dx, _, g_ev_norm = rmsnorm_bwd(x, p["ev_norm_w"], dn0, dh1, name="l0_norm_bwd")

    small = dict(ev_norm_w=g_ev_norm, ev_dw_w=g_dw_w, ev_dw_b=g_dw_b, ev_ln_w=g_ln_w, ev_ln_b=g_ln_b,
                 od_norm_w=g_od_norm, od_conv_w=g_conv_w, od_conv_b=g_conv_b, od_dt_bias=g_bias[:, :H],
                 od_a_log=g_alog[:, :H], od_d=g_d, od_gnorm_w=g_gnorm, final_norm_w=g_final)
    received = dict(ev_w_in=r_ev_in, ev_w_out=r_ev_out, od_w_in=r_od_in_t, od_w_out=r_od_out)
    return loss, dx, small, received


BIG = ("ev_w_in", "ev_w_out", "od_w_in", "od_w_out")
SMALL = ("ev_norm_w", "ev_dw_w", "ev_dw_b", "ev_ln_w", "ev_ln_b", "od_norm_w", "od_conv_w", "od_conv_b",
         "od_dt_bias", "od_a_log", "od_d", "od_gnorm_w", "final_norm_w")
SMALL_SHARDED = ("ev_dw_w", "od_norm_w", "od_conv_w", "od_conv_b", "od_gnorm_w")
ORDER = ("ev_norm_w", "ev_w_in", "ev_dw_w", "ev_dw_b", "ev_ln_w", "ev_ln_b", "ev_w_out", "od_norm_w", "od_w_in",
         "od_conv_w", "od_conv_b", "od_dt_bias", "od_a_log", "od_d", "od_gnorm_w", "od_w_out", "final_norm_w")


def _pack_rows(arrs, width, row_align):
    parts, spans, r0 = [], [], 0
    for a in arrs:
        flat = a.reshape(-1)
        rows = -(-flat.shape[0] // (width * row_align)) * row_align
        parts.append(jnp.pad(flat, (0, rows * width - flat.shape[0])).reshape(rows, width))
        spans.append((r0, a.size, a.shape))
        r0 += rows
    return jnp.concatenate(parts, axis=0), spans


def _unpack_rows(packed, spans):
    lead = packed.shape[:-2]
    width = packed.shape[-1]
    out = []
    for r0, size, shape in spans:
        rows = -(-size // width)
        blk = packed[..., r0:r0 + rows, :].reshape(lead + (rows * width,))[..., :size]
        out.append(blk.reshape(lead + tuple(shape)))
    return out


def _col_shards(a):
    R, C8 = a.shape
    return a.reshape(R, N_DEV, C8 // N_DEV).transpose(1, 0, 2)


def _col_unshards(a):
    n, R, C = a.shape
    return a.transpose(1, 0, 2).reshape(R, n * C)


def kernel(x, ev_norm_w, ev_w_in, ev_dw_w, ev_dw_b, ev_ln_w, ev_ln_b, ev_w_out, od_norm_w, od_w_in, od_conv_w, od_conv_b, od_dt_bias, od_a_log, od_d, od_gnorm_w, od_w_out, final_norm_w, loss_target, m_ev_norm_w, m_ev_w_in, m_ev_dw_w, m_ev_dw_b, m_ev_ln_w, m_ev_ln_b, m_ev_w_out, m_od_norm_w, m_od_w_in, m_od_conv_w, m_od_conv_b, m_od_dt_bias, m_od_a_log, m_od_d, m_od_gnorm_w, m_od_w_out, m_final_norm_w, v_ev_norm_w, v_ev_w_in, v_ev_dw_w, v_ev_dw_b, v_ev_ln_w, v_ev_ln_b, v_ev_w_out, v_od_norm_w, v_od_w_in, v_od_conv_w, v_od_conv_b, v_od_dt_bias, v_od_a_log, v_od_d, v_od_gnorm_w, v_od_w_out, v_final_norm_w):
    loc = dict(ev_norm_w=ev_norm_w, ev_w_in=ev_w_in, ev_dw_w=ev_dw_w, ev_dw_b=ev_dw_b, ev_ln_w=ev_ln_w,
               ev_ln_b=ev_ln_b, ev_w_out=ev_w_out, od_norm_w=od_norm_w, od_w_in=od_w_in, od_conv_w=od_conv_w,
               od_conv_b=od_conv_b, od_dt_bias=od_dt_bias, od_a_log=od_a_log, od_d=od_d, od_gnorm_w=od_gnorm_w,
               od_w_out=od_w_out, final_norm_w=final_norm_w)
    mom = dict(ev_norm_w=m_ev_norm_w, ev_w_in=m_ev_w_in, ev_dw_w=m_ev_dw_w, ev_dw_b=m_ev_dw_b, ev_ln_w=m_ev_ln_w,
               ev_ln_b=m_ev_ln_b, ev_w_out=m_ev_w_out, od_norm_w=m_od_norm_w, od_w_in=m_od_w_in,
               od_conv_w=m_od_conv_w, od_conv_b=m_od_conv_b, od_dt_bias=m_od_dt_bias, od_a_log=m_od_a_log,
               od_d=m_od_d, od_gnorm_w=m_od_gnorm_w, od_w_out=m_od_w_out, final_norm_w=m_final_norm_w)
    var = dict(ev_norm_w=v_ev_norm_w, ev_w_in=v_ev_w_in, ev_dw_w=v_ev_dw_w, ev_dw_b=v_ev_dw_b, ev_ln_w=v_ev_ln_w,
               ev_ln_b=v_ev_ln_b, ev_w_out=v_ev_w_out, od_norm_w=v_od_norm_w, od_w_in=v_od_w_in,
               od_conv_w=v_od_conv_w, od_conv_b=v_od_conv_b, od_dt_bias=v_od_dt_bias, od_a_log=v_od_a_log,
               od_d=v_od_d, od_gnorm_w=v_od_gnorm_w, od_w_out=v_od_w_out, final_norm_w=v_final_norm_w)
    shapes = {n: loc[n].shape for n in ORDER}
    loc = {n: (a.reshape(1, -1) if a.ndim == 1 else a.reshape(a.shape[-2:]) if a.ndim == 3 else a)
           for n, a in loc.items()}
    mom = {n: a.reshape(loc[n].shape) for n, a in mom.items()}
    var = {n: a.reshape(loc[n].shape) for n, a in var.items()}

    B, seq, D = x.shape
    me = 4 * lax.axis_index("x") + 2 * lax.axis_index("y") + lax.axis_index("c")

    w1_rows = loc["od_w_in"].shape[1]
    w1_pad = (-w1_rows) % SHARD_ROW_ALIGN

    def to_t(a):
        return jnp.pad(a.T, ((0, w1_pad), (0, 0)))

    loss, dx, grads, received = local_step(x.reshape(B * seq, D), loss_target.reshape(B * seq, D), loc,
                                           to_t(loc["od_w_in"]), B=B, seq=seq)

    gsmall_packed, gsmall_spans = _pack_rows([grads[n] for n in SMALL] + [loss], LANE, 8)
    (gsmall_recv,) = exchange([gsmall_packed], ["gather_direct"], name="gather_small_grads")

    big_out = [{} for _ in range(4)]
    for n in ("ev_w_in", "ev_w_out", "od_w_out"):
        for kind, a in enumerate(adamw(received[n], loc[n], mom[n], var[n], name="adamw_" + n)):
            big_out[kind][n] = a
    for kind, a in enumerate(adamw(received["od_w_in"], to_t(loc["od_w_in"]), to_t(mom["od_w_in"]),
                                   to_t(var["od_w_in"]), name="adamw_od_w_in")):
        big_out[kind]["od_w_in"] = a[:w1_rows].T

    summed = _unpack_rows(sum_slots(gsmall_recv, name="sum_small_grads"), gsmall_spans)
    loss_total = summed[-1][0, 0]
    gsmall = dict(zip(SMALL, summed[:-1]))
    for n in SMALL_SHARDED:
        width = loc[n].shape[1]
        gsmall[n] = lax.dynamic_slice_in_dim(gsmall[n], me * width, width, axis=1)
    gs, sspans = _pack_rows([gsmall[n] for n in SMALL], LANE, 8)
    ws, _ = _pack_rows([loc[n] for n in SMALL], LANE, 8)
    ms, _ = _pack_rows([mom[n] for n in SMALL], LANE, 8)
    vs, _ = _pack_rows([var[n] for n in SMALL], LANE, 8)
    small_out = [dict(zip(SMALL, _unpack_rows(a, sspans))) for a in adamw(gs[None], ws, ms, vs, name="adamw_small")]

    outs = [loss_total, dx.reshape(B, seq, D)]
    for kind in range(4):
        for n in ORDER:
            src = big_out[kind] if n in BIG else small_out[kind]
            outs.append(src[n].reshape(shapes[n]))
    return tuple(outs)
```

```python
import jax
import jax.numpy as jnp
from jax import lax
from jax.experimental import pallas as pl
from jax.experimental.pallas import tpu as pltpu

F32 = jnp.float32
BF16 = jnp.bfloat16

EPS = 1e-6
N_DEV = 8
LANE = 128
VMEM_LIMIT_BYTES = 48 * 1024 * 1024

SB_HEAD_DIM = 128
SSM_HEAD_DIM = 64
SSM_GROUPS = 4
SSM_STATE = 128
SSM_CHUNK = 128
HALO = 32
HEAD_ROWS = 8
NEG_BIG = -1e30

ADAM_LR = 0.001
ADAM_B1 = 0.9
ADAM_B2 = 0.999
ADAM_EPS = 1e-08
ADAM_WD = 0.01
ADAM_STEP = 10

NT = (((1,), (1,)), ((), ()))
TN = (((0,), (0,)), ((), ()))


def _cp(*sem):
    return pltpu.CompilerParams(dimension_semantics=sem, vmem_limit_bytes=VMEM_LIMIT_BYTES)


def _pick(n, cap, align):
    if n <= cap:
        return n
    t = (cap // align) * align
    while t >= align:
        if n % t == 0:
            return t
        t -= align
    raise ValueError(f"no tile for {n} (cap {cap}, align {align})")


def _sigmoid(x):
    return 0.5 * jnp.tanh(0.5 * x) + 0.5


def _silu(x):
    return x * _sigmoid(x)


def _silu_grad(x):
    s = _sigmoid(x)
    return s * (1.0 + x * (1.0 - s))


def _dot(a, b, dims=None):
    if dims is None:
        return jnp.dot(a, b, preferred_element_type=F32)
    return lax.dot_general(a, b, dims, preferred_element_type=F32)


def _tri_dot3(tri, x):
    hi = x.astype(BF16)
    r1 = x - hi.astype(F32)
    mid = r1.astype(BF16)
    lo = (r1 - mid.astype(F32)).astype(BF16)
    return _dot(tri, hi) + _dot(tri, mid) + _dot(tri, lo)


def _grid_step(n_inner):
    return lambda: pl.program_id(0) * n_inner + pl.program_id(1)


def mm_nn(a, b, *, add=None, out_dtype, name, comm=None):
    M, K = a.shape
    N = b.shape[1]
    tm = _pick(M, 2048 if K <= 1024 and add is None else 1024, 16)
    tn = _pick(N, 1024, LANE)

    def body(*refs):
        if add is None:
            a_ref, b_ref, o_ref = refs
        else:
            a_ref, b_ref, add_ref, o_ref = refs
        acc = _dot(a_ref[...], b_ref[...])
        if add is not None:
            acc = acc + add_ref[...]
        o_ref[...] = acc.astype(out_dtype)

    in_specs = [pl.BlockSpec((tm, K), lambda i, j: (i, 0)), pl.BlockSpec((K, tn), lambda i, j: (0, j))]
    args = [a, b]
    if add is not None:
        in_specs.append(pl.BlockSpec((tm, tn), lambda i, j: (i, j)))
        args.append(add)
    grid = (M // tm, N // tn)
    body, c_in, c_args, c_out, c_shape, c_scratch = _hosted(body, len(args), 1, comm, _grid_step(grid[1]),
                                                            grid[0] * grid[1])
    out = pl.pallas_call(
        body, name=name, grid=grid, in_specs=in_specs + c_in,
        out_specs=[pl.BlockSpec((tm, tn), lambda i, j: (i, j))] + c_out,
        out_shape=[jax.ShapeDtypeStruct((M, N), out_dtype)] + c_shape, scratch_shapes=c_scratch,
        compiler_params=_cp(*(("arbitrary",) * 2 if comm else ("parallel",) * 2)))(*args, *c_args)
    return (out[0], out[1:]) if comm else out[0]


def mm_nt_terms(terms, b, *, out_dtype, name, comm=None):
    M = terms[0][0].shape[0]
    N = b.shape[0]
    n_terms = len(terms)
    if n_terms == 1:
        tm, tn = _pick(M, 2048, 16), _pick(N, 1024, LANE)
    else:
        tm, tn = _pick(M, 256, 16), _pick(N, 1024, LANE)

    def body(*refs):
        o_ref = refs[-1]
        acc = None
        for t in range(n_terms):
            part = _dot(refs[2 * t][...], refs[2 * t + 1][...], NT)
            acc = part if acc is None else acc + part
        o_ref[...] = acc.astype(out_dtype)

    in_specs, args = [], []
    for arr, cb, w, off in terms:
        assert off % w == 0
        in_specs.append(pl.BlockSpec((tm, w), lambda i, j, cb=cb: (i, cb)))
        in_specs.append(pl.BlockSpec((tn, w), lambda i, j, ob=off // w: (j, ob)))
        args += [arr, b]
    grid = (M // tm, N // tn)
    body, c_in, c_args, c_out, c_shape, c_scratch = _hosted(body, len(args), 1, comm, _grid_step(grid[1]),
                                                            grid[0] * grid[1])
    out = pl.pallas_call(
        body, name=name, grid=grid, in_specs=in_specs + c_in,
        out_specs=[pl.BlockSpec((tm, tn), lambda i, j: (i, j))] + c_out,
        out_shape=[jax.ShapeDtypeStruct((M, N), out_dtype)] + c_shape, scratch_shapes=c_scratch,
        compiler_params=_cp(*(("arbitrary",) * 2 if comm else ("parallel",) * 2)))(*args, *c_args)
    return (out[0], out[1:]) if comm else out[0]


def mm_nn_terms(terms, b, *, out_dtype, name):
    M = terms[0][0].shape[0]
    N = b.shape[1]
    tm = _pick(M, 256, 16)
    tn = _pick(N, 1024, LANE)
    n_terms = len(terms)

    def body(*refs):
        o_ref = refs[-1]
        acc = None
        for t in range(n_terms):
            part = _dot(refs[2 * t][...], refs[2 * t + 1][...])
            acc = part if acc is None else acc + part
        o_ref[...] = acc.astype(out_dtype)

    in_specs, args = [], []
    for arr, cb, w, off in terms:
        assert off % w == 0
        in_specs.append(pl.BlockSpec((tm, w), lambda i, j, cb=cb: (i, cb)))
        in_specs.append(pl.BlockSpec((w, tn), lambda i, j, ob=off // w: (ob, j)))
        args += [arr, b]
    return pl.pallas_call(
        body, name=name, grid=(M // tm, N // tn), in_specs=in_specs,
        out_specs=pl.BlockSpec((tm, tn), lambda i, j: (i, j)),
        out_shape=jax.ShapeDtypeStruct((M, N), out_dtype),
        compiler_params=_cp("parallel", "parallel"))(*args)


def mm_tn(a, b, *, out_dtype, name):
    T, M = a.shape
    N = b.shape[1]
    tm = _pick(M, 1024, LANE)
    tn = _pick(N, 1024, LANE)
    tk = _pick(T, 2048, 16)
    nk = T // tk

    def body(a_ref, b_ref, o_ref, acc_ref):
        k = pl.program_id(2)

        @pl.when(k == 0)
        def _():
            acc_ref[...] = jnp.zeros_like(acc_ref)

        acc_ref[...] += _dot(a_ref[...], b_ref[...], TN)

        @pl.when(k == nk - 1)
        def _():
            o_ref[...] = acc_ref[...].astype(out_dtype)

    return pl.pallas_call(
        body, name=name, grid=(M // tm, N // tn, nk),
        in_specs=[pl.BlockSpec((tk, tm), lambda i, j, k: (k, i)), pl.BlockSpec((tk, tn), lambda i, j, k: (k, j))],
        out_specs=pl.BlockSpec((tm, tn), lambda i, j, k: (i, j)),
        out_shape=jax.ShapeDtypeStruct((M, N), out_dtype),
        scratch_shapes=[pltpu.VMEM((tm, tn), F32)],
        compiler_params=_cp("parallel", "parallel", "arbitrary"))(a, b)


def rmsnorm_fwd(h, w, *, name, comm=None):
    T, D = h.shape
    tt = _pick(T, 512, 16)

    def body(h_ref, w_ref, n_ref):
        x = h_ref[...]
        r = lax.rsqrt(jnp.mean(x * x, axis=-1, keepdims=True) + EPS)
        n_ref[...] = (x * r * w_ref[...]).astype(BF16)

    body, c_in, c_args, c_out, c_shape, c_scratch = _hosted(body, 2, 1, comm, lambda: pl.program_id(0), T // tt)
    out = pl.pallas_call(
        body, name=name, grid=(T // tt,),
        in_specs=[pl.BlockSpec((tt, D), lambda i: (i, 0)), pl.BlockSpec((1, D), lambda i: (0, 0))] + c_in,
        out_specs=[pl.BlockSpec((tt, D), lambda i: (i, 0))] + c_out,
        out_shape=[jax.ShapeDtypeStruct((T, D), BF16)] + c_shape, scratch_shapes=c_scratch,
        compiler_params=_cp("arbitrary" if comm else "parallel"))(h, w, *c_args)
    return (out[0], out[1:]) if comm else out[0]


def rmsnorm_bwd(h, w, dn, dres, *, name, comm=None):
    T, D = h.shape
    tt = _pick(T, 512, 16)

    def body(h_ref, w_ref, dn_ref, dres_ref, dh_ref, dhb_ref, gw_ref):
        @pl.when(pl.program_id(0) == 0)
        def _():
            gw_ref[...] = jnp.zeros_like(gw_ref)

        x = h_ref[...]
        r = lax.rsqrt(jnp.mean(x * x, axis=-1, keepdims=True) + EPS)
        xhat = x * r
        g = dn_ref[...].astype(F32)
        gw_ref[...] += jnp.sum(g * xhat, axis=0, keepdims=True)
        dxh = g * w_ref[...]
        dx = r * (dxh - xhat * jnp.mean(dxh * xhat, axis=-1, keepdims=True))
        dh = dres_ref[...] + dx
        dh_ref[...] = dh
        dhb_ref[...] = dh.astype(BF16)

    row = pl.BlockSpec((tt, D), lambda i: (i, 0))
    vec = pl.BlockSpec((1, D), lambda i: (0, 0))
    body, c_in, c_args, c_out, c_shape, c_scratch = _hosted(body, 4, 3, comm, lambda: pl.program_id(0), T // tt)
    out = pl.pallas_call(
        body, name=name, grid=(T // tt,), in_specs=[row, vec, row, row] + c_in, out_specs=[row, row, vec] + c_out,
        out_shape=[jax.ShapeDtypeStruct((T, D), F32), jax.ShapeDtypeStruct((T, D), BF16),
                   jax.ShapeDtypeStruct((1, D), F32)] + c_shape,
        scratch_shapes=c_scratch, compiler_params=_cp("arbitrary"))(h, w, dn, dres, *c_args)
    return (out[0], out[1], out[2], out[3:]) if comm else out


def final_loss(h, w, target, *, name):
    T, D = h.shape
    tt = _pick(T, 512, 16)

    def body(h_ref, w_ref, t_ref, loss_ref, dh_ref, dhb_ref, gw_ref):
        @pl.when(pl.program_id(0) == 0)
        def _():
            gw_ref[...] = jnp.zeros_like(gw_ref)
            loss_ref[...] = jnp.zeros_like(loss_ref)

        x = h_ref[...]
        r = lax.rsqrt(jnp.mean(x * x, axis=-1, keepdims=True) + EPS)
        xhat = x * r
        e = xhat * w_ref[...] - t_ref[...]
        loss_ref[...] += jnp.sum(e * e) * (0.5 / D)
        g = e * (1.0 / D)
        gw_ref[...] += jnp.sum(g * xhat, axis=0, keepdims=True)
        dxh = g * w_ref[...]
        dh = r * (dxh - xhat * jnp.mean(dxh * xhat, axis=-1, keepdims=True))
        dh_ref[...] = dh
        dhb_ref[...] = dh.astype(BF16)

    row = pl.BlockSpec((tt, D), lambda i: (i, 0))
    vec = pl.BlockSpec((1, D), lambda i: (0, 0))
    one = pl.BlockSpec((1, LANE), lambda i: (0, 0))
    return pl.pallas_call(
        body, name=name, grid=(T // tt,), in_specs=[row, vec, row], out_specs=[one, row, row, vec],
        out_shape=[jax.ShapeDtypeStruct((1, LANE), F32), jax.ShapeDtypeStruct((T, D), F32),
                   jax.ShapeDtypeStruct((T, D), BF16), jax.ShapeDtypeStruct((1, D), F32)],
        compiler_params=_cp("arbitrary"))(h, w, target)


CONV_CHUNK = 32
ROW_CHUNK = 16
SUBLANES = 8


def _conv_tiles(seq, C, K):
    return _pick(seq, 1024, HALO), _pick(C, 512, LANE)


def _residues(offsets):
    return sorted({s % SUBLANES for s in offsets} - {0})


def _fill_shifted(buf, shifted, residues):
    n = buf.shape[0] - SUBLANES
    for i, r in enumerate(residues):
        shifted[i, 0:n, :] = buf[r:r + n, :]


def _tap(buf, shifted, residues, offset, start, rows):
    r = offset % SUBLANES
    base = offset - r
    ref = buf if r == 0 else shifted.at[residues.index(r)]
    return ref[pl.ds(start + base, rows), :]


def dwconv_fwd(src, offs, w, b, *, C, seq, glu, silu_out, name, comm=None):
    T = src.shape[0]
    K = w.shape[0]
    assert K - 1 <= HALO
    tt, tc = _conv_tiles(seq, C, K)
    n_in = 2 if glu else 1
    per = tt // HALO
    offsets = [HALO - (K - 1) + k for k in range(K)]
    residues = _residues(offsets)

    def body(*refs):
        cur = refs[0:2 * n_in:2]
        halo = refs[1:2 * n_in:2]
        w_ref, b_ref = refs[2 * n_in], refs[2 * n_in + 1]
        outs = refs[2 * n_in + 2:-2]
        buf, shifted = refs[-2], refs[-1]
        i = pl.program_id(1)
        first = (i * tt) % seq == 0

        def pre(rs, rows):
            v = rs[0][rows, :].astype(F32)
            return v * _sigmoid(rs[1][rows, :].astype(F32)) if glu else v

        def build(ci, carry):
            start = pl.multiple_of(ci * CONV_CHUNK, CONV_CHUNK)
            buf[pl.ds(HALO + start, CONV_CHUNK), :] = pre(cur, pl.ds(start, CONV_CHUNK))
            return carry

        buf[0:HALO, :] = jnp.where(first, 0.0, pre(halo, slice(None)))
        lax.fori_loop(0, tt // CONV_CHUNK, build, 0, unroll=2)
        _fill_shifted(buf, shifted, residues)

        def chunk(ci, carry):
            start = pl.multiple_of(ci * CONV_CHUNK, CONV_CHUNK)
            acc = jnp.broadcast_to(b_ref[...], (CONV_CHUNK, tc))
            for k in range(K):
                acc = acc + w_ref[k:k + 1, :] * _tap(buf, shifted, residues, offsets[k], start, CONV_CHUNK)
            outs[0][pl.ds(start, CONV_CHUNK), :] = acc.astype(BF16)
            if silu_out:
                outs[1][pl.ds(start, CONV_CHUNK), :] = _silu(acc).astype(BF16)
            return carry

        lax.fori_loop(0, tt // CONV_CHUNK, chunk, 0)

    in_specs, args = [], []
    for off in offs:
        assert off % tc == 0
        in_specs.append(pl.BlockSpec((tt, tc), lambda j, i, ob=off // tc: (i, ob + j)))
        in_specs.append(pl.BlockSpec((HALO, tc), lambda j, i, ob=off // tc: (jnp.maximum(i * per - 1, 0), ob + j)))
        args += [src, src]
    in_specs += [pl.BlockSpec((K, tc), lambda j, i: (0, j)), pl.BlockSpec((1, tc), lambda j, i: (0, j))]
    args += [w, b]
    n_out = 2 if silu_out else 1
    grid = (C // tc, T // tt)
    body, c_in, c_args, c_out, c_shape, c_scratch = _hosted(body, len(args), n_out, comm, _grid_step(grid[1]),
                                                            grid[0] * grid[1])
    out = pl.pallas_call(
        body, name=name, grid=grid, in_specs=in_specs + c_in,
        out_specs=[pl.BlockSpec((tt, tc), lambda j, i: (i, j))] * n_out + c_out,
        out_shape=[jax.ShapeDtypeStruct((T, C), BF16)] * n_out + c_shape,
        scratch_shapes=[pltpu.VMEM((HALO + tt, tc), F32), pltpu.VMEM((max(len(residues), 1), HALO + tt, tc), F32)]
        + c_scratch,
        compiler_params=_cp("arbitrary" if comm else "parallel", "arbitrary"))(*args, *c_args)
    return (out[:n_out], out[n_out:]) if comm else out


def dwconv_bwd(du, u, src, offs, w, *, C, seq, glu, silu_out, name):
    T = src.shape[0]
    K = w.shape[0]
    assert K - 1 <= HALO
    tt, tc = _conv_tiles(seq, C, K)
    n_in = 2 if glu else 1
    per = tt // HALO
    last_blk = T // HALO - 1
    g_offsets = [K - 1 - k for k in range(K)]
    g_res = _residues(g_offsets)

    def body(*refs):
        pos = 0
        du_cur, du_nxt = refs[0], refs[1]
        pos = 2
        if silu_out:
            u_cur, u_nxt = refs[2], refs[3]
            pos = 4
        cur = refs[pos:pos + n_in]
        pos += n_in
        w_ref = refs[pos]
        outs = refs[pos + 1:pos + 1 + n_in]
        dw_ref, db_ref = refs[pos + 1 + n_in], refs[pos + 2 + n_in]
        gbuf, gshift, dw_acc, db_acc = refs[-4:]
        i = pl.program_id(1)
        last = ((i + 1) * tt) % seq == 0

        @pl.when(i == 0)
        def _():
            dw_acc[...] = jnp.zeros_like(dw_acc)
            db_acc[...] = jnp.zeros_like(db_acc)

        def build(ci, carry):
            rows = pl.ds(pl.multiple_of(ci * CONV_CHUNK, CONV_CHUNK), CONV_CHUNK)
            g = du_cur[rows, :].astype(F32)
            if silu_out:
                g = g * _silu_grad(u_cur[rows, :].astype(F32))
            gbuf[rows, :] = g
            return carry

        lax.fori_loop(0, tt // CONV_CHUNK, build, 0, unroll=2)
        g_nxt = du_nxt[...].astype(F32)
        if silu_out:
            g_nxt = g_nxt * _silu_grad(u_nxt[...].astype(F32))
        gbuf[tt:tt + HALO, :] = jnp.where(last, 0.0, g_nxt)
        _fill_shifted(gbuf, gshift, g_res)

        def fold(v):
            out = v[0:SUBLANES]
            for s in range(SUBLANES, CONV_CHUNK, SUBLANES):
                out = out + v[s:s + SUBLANES]
            return out

        def chunk(ci, carry):
            start = pl.multiple_of(ci * CONV_CHUNK, CONV_CHUNK)
            rows = pl.ds(start, CONV_CHUNK)
            a = cur[0][rows, :].astype(F32)
            if glu:
                s = _sigmoid(cur[1][rows, :].astype(F32))
                x_in = a * s
            else:
                x_in = a
            dx = jnp.zeros((CONV_CHUNK, tc), F32)
            for k in range(K):
                g_k = _tap(gbuf, gshift, g_res, g_offsets[k], start, CONV_CHUNK)
                dx = dx + w_ref[k:k + 1, :] * g_k
                dw_acc[k * SUBLANES:(k + 1) * SUBLANES, :] += fold(g_k * x_in)
            db_acc[...] += fold(gbuf[rows, :])
            if glu:
                outs[0][rows, :] = (dx * s).astype(BF16)
                outs[1][rows, :] = (dx * a * s * (1.0 - s)).astype(BF16)
            else:
                outs[0][rows, :] = dx.astype(BF16)
            return carry

        lax.fori_loop(0, tt // CONV_CHUNK, chunk, 0)

        @pl.when(i == T // tt - 1)
        def _():
            for k in range(K):
                dw_ref[k:k + 1, :] = jnp.sum(dw_acc[k * SUBLANES:(k + 1) * SUBLANES, :], axis=0, keepdims=True)
            db_ref[...] = jnp.sum(db_acc[...], axis=0, keepdims=True)

    def cur_spec(ob):
        return pl.BlockSpec((tt, tc), lambda j, i: (i, ob + j))

    def nxt_spec(ob):
        return pl.BlockSpec((HALO, tc), lambda j, i: (jnp.minimum((i + 1) * per, last_blk), ob + j))

    in_specs = [cur_spec(0), nxt_spec(0)]
    args = [du, du]
    if silu_out:
        in_specs += [cur_spec(0), nxt_spec(0)]
        args += [u, u]
    for off in offs:
        assert off % tc == 0
        in_specs.append(cur_spec(off // tc))
        args.append(src)
    in_specs.append(pl.BlockSpec((K, tc), lambda j, i: (0, j)))
    args.append(w)
    out_specs = [pl.BlockSpec((tt, tc), lambda j, i: (i, j))] * n_in
    out_specs += [pl.BlockSpec((K, tc), lambda j, i: (0, j)), pl.BlockSpec((1, tc), lambda j, i: (0, j))]
    out_shape = [jax.ShapeDtypeStruct((T, C), BF16)] * n_in
    out_shape += [jax.ShapeDtypeStruct((K, C), F32), jax.ShapeDtypeStruct((1, C), F32)]
    return pl.pallas_call(
        body, name=name, grid=(C // tc, T // tt), in_specs=in_specs, out_specs=out_specs, out_shape=out_shape,
        scratch_shapes=[pltpu.VMEM((tt + HALO, tc), F32), pltpu.VMEM((max(len(g_res), 1), tt + HALO, tc), F32),
                        pltpu.VMEM((K * SUBLANES, tc), F32), pltpu.VMEM((SUBLANES, tc), F32)],
        compiler_params=_cp("parallel", "arbitrary"))(*args)


def mix0_post_fwd(u2, proj, o, ln_w, ln_b, *, CW, gc_off, ga_off, name):
    T = u2.shape[0]
    tt = _pick(T, 512, 16)

    def body(u_ref, gc_ref, ga_ref, o_ref, lw_ref, lb_ref, y_ref):
        def chunk(ci, carry):
            rows = pl.ds(pl.multiple_of(ci * ROW_CHUNK, ROW_CHUNK), ROW_CHUNK)
            u = u_ref[rows, :].astype(F32)
            mu = jnp.mean(u, axis=-1, keepdims=True)
            xc = u - mu
            r = lax.rsqrt(jnp.mean(xc * xc, axis=-1, keepdims=True) + EPS)
            u3 = xc * r * lw_ref[...] + lb_ref[...]
            y_ref[rows, 0:CW] = (_silu(u3) * _silu(gc_ref[rows, :].astype(F32))).astype(BF16)
            y_ref[rows, CW:2 * CW] = (o_ref[rows, :].astype(F32) * _silu(ga_ref[rows, :].astype(F32))).astype(BF16)
            return carry

        lax.fori_loop(0, tt // ROW_CHUNK, chunk, 0, unroll=4)

    row = pl.BlockSpec((tt, CW), lambda i: (i, 0))
    vec = pl.BlockSpec((1, CW), lambda i: (0, 0))
    return pl.pallas_call(
        body, name=name, grid=(T // tt,),
        in_specs=[row, pl.BlockSpec((tt, CW), lambda i: (i, gc_off // CW)),
                  pl.BlockSpec((tt, CW), lambda i: (i, ga_off // CW)), row, vec, vec],
        out_specs=pl.BlockSpec((tt, 2 * CW), lambda i: (i, 0)),
        out_shape=jax.ShapeDtypeStruct((T, 2 * CW), BF16),
        compiler_params=_cp("parallel"))(u2, proj, proj, o, ln_w, ln_b)


def mix0_post_bwd(dy, u2, proj, o, ln_w, ln_b, *, CW, gc_off, ga_off, name):
    T = u2.shape[0]
    tt = _pick(T, 512, 16)

    def body(dy_ref, u_ref, gc_ref, ga_ref, o_ref, lw_ref, lb_ref, du_ref, dgc_ref, dga_ref, do_ref, dlw_ref, dlb_ref,
             lw_acc, lb_acc):
        i = pl.program_id(0)

        @pl.when(i == 0)
        def _():
            lw_acc[...] = jnp.zeros_like(lw_acc)
            lb_acc[...] = jnp.zeros_like(lb_acc)

        def fold(v):
            out = v[0:SUBLANES]
            for s in range(SUBLANES, ROW_CHUNK, SUBLANES):
                out = out + v[s:s + SUBLANES]
            return out

        def chunk(ci, carry):
            rows = pl.ds(pl.multiple_of(ci * ROW_CHUNK, ROW_CHUNK), ROW_CHUNK)
            dyc = dy_ref[rows, 0:CW].astype(F32)
            dya = dy_ref[rows, CW:2 * CW].astype(F32)
            u = u_ref[rows, :].astype(F32)
            mu = jnp.mean(u, axis=-1, keepdims=True)
            xc = u - mu
            r = lax.rsqrt(jnp.mean(xc * xc, axis=-1, keepdims=True) + EPS)
            xhat = xc * r
            u3 = xhat * lw_ref[...] + lb_ref[...]
            gc = gc_ref[rows, :].astype(F32)
            dgc_ref[rows, :] = (dyc * _silu(u3) * _silu_grad(gc)).astype(BF16)
            du3 = dyc * _silu(gc) * _silu_grad(u3)
            lw_acc[...] += fold(du3 * xhat)
            lb_acc[...] += fold(du3)
            dxh = du3 * lw_ref[...]
            du = r * (dxh - jnp.mean(dxh, axis=-1, keepdims=True)
                      - xhat * jnp.mean(dxh * xhat, axis=-1, keepdims=True))
            du_ref[rows, :] = du.astype(BF16)
            ga = ga_ref[rows, :].astype(F32)
            ov = o_ref[rows, :].astype(F32)
            do_ref[rows, :] = (dya * _silu(ga)).astype(BF16)
            dga_ref[rows, :] = (dya * ov * _silu_grad(ga)).astype(BF16)
            return carry

        lax.fori_loop(0, tt // ROW_CHUNK, chunk, 0, unroll=4)

        @pl.when(i == T // tt - 1)
        def _():
            dlw_ref[...] = jnp.sum(lw_acc[...], axis=0, keepdims=True)
            dlb_ref[...] = jnp.sum(lb_acc[...], axis=0, keepdims=True)

    row = pl.BlockSpec((tt, CW), lambda i: (i, 0))
    vec = pl.BlockSpec((1, CW), lambda i: (0, 0))
    big = jax.ShapeDtypeStruct((T, CW), BF16)
    small = jax.ShapeDtypeStruct((1, CW), F32)
    return pl.pallas_call(
        body, name=name, grid=(T // tt,),
        in_specs=[pl.BlockSpec((tt, 2 * CW), lambda i: (i, 0)), row,
                  pl.BlockSpec((tt, CW), lambda i: (i, gc_off // CW)),
                  pl.BlockSpec((tt, CW), lambda i: (i, ga_off // CW)), row, vec, vec],
        out_specs=[row, row, row, row, vec, vec],
        out_shape=[big, big, big, big, small, small],
        scratch_shapes=[pltpu.VMEM((SUBLANES, CW), F32), pltpu.VMEM((SUBLANES, CW), F32)],
        compiler_params=_cp("arbitrary"))(dy, u2, proj, proj, o, ln_w, ln_b)


SB_UNDERFLOW = 110.0
SB_BOUND_MARGIN = 1.02


def _sb_tile(seq):
    return _pick(seq, 256, LANE)


def _softplus(z):
    return jnp.maximum(z, 0.0) + jnp.log(1.0 + jnp.exp(-jnp.abs(z)))


def _tri01(n, lower):
    i = lax.broadcasted_iota(jnp.int32, (n, n), 0)
    j = lax.broadcasted_iota(jnp.int32, (n, n), 1)
    return ((i >= j) if lower else (i <= j)).astype(BF16)


SB_HEADS_FWD = 8
SB_HEADS_BWD = 2


def _sb_heads_per_step(heads, want):
    while heads % want:
        want //= 2
    return want


def sba_fwd(proj, *, B, seq, heads, q_off, k_off, v_off, name):
    dh = SB_HEAD_DIM
    tq = _sb_tile(seq)
    assert tq % (2 * LANE) == 0
    nq = seq // tq
    hps = _sb_heads_per_step(heads, SB_HEADS_FWD)
    hw = hps * dh
    scale = dh ** -0.5

    def body(q_ref, k_ref, v_ref, tri_ref, o_ref, ct_ref, acc_ref, kmax_ref):
        qi = pl.program_id(1)
        tri = tri_ref[...]
        qs = [(q_ref[:, h * dh:(h + 1) * dh].astype(F32) * scale).astype(BF16) for h in range(hps)]

        @pl.when(qi == 0)
        def _():
            def chunk(i, best):
                rows = k_ref[pl.ds(pl.multiple_of(i * tq, tq), tq), :].astype(F32)
                sq = rows * rows
                return tuple(jnp.maximum(best[h], jnp.max(jnp.sum(sq[:, h * dh:(h + 1) * dh], axis=1, keepdims=True),
                                                          axis=0, keepdims=True)) for h in range(hps))

            best = lax.fori_loop(0, nq, chunk, (jnp.zeros((1, 1), F32),) * hps)
            for h in range(hps):
                kmax_ref[h] = jnp.broadcast_to(jnp.sqrt(best[h]), (8, LANE))

        z_bound = [jnp.sqrt(jnp.sum(qs[h].astype(F32) ** 2, axis=1, keepdims=True))
                   * (SB_BOUND_MARGIN * jnp.max(kmax_ref[h], keepdims=True)) for h in range(hps)]

        def part(h, q_rows, start, n_keys, r, mask):
            k_blk = k_ref[pl.ds(start, n_keys), h * dh:(h + 1) * dh]
            v_blk = v_ref[pl.ds(start, n_keys), h * dh:(h + 1) * dh]
            z = _dot(q_rows, k_blk, NT)
            sp = _softplus(z)
            if mask is not None:
                sp = jnp.where(mask, sp, 0.0)
            wts = jnp.exp(z - (_dot(sp.astype(BF16), tri[0:n_keys, 0:n_keys]) + r))
            if mask is not None:
                wts = jnp.where(mask, wts, 0.0)
            return _dot(wts.astype(BF16), v_blk), r + jnp.sum(sp, axis=-1, keepdims=True)

        below = lax.broadcasted_iota(jnp.int32, (tq, tq), 1) < lax.broadcasted_iota(jnp.int32, (tq, tq), 0)
        has_left = qi > 0
        left = pl.multiple_of(jnp.maximum(qi - 1, 0) * tq, tq)
        rs = []
        for h in range(hps):
            pv_d, r = part(h, qs[h], pl.multiple_of(qi * tq, tq), tq, jnp.zeros((tq, 1), F32), below)
            pv_l, r = part(h, qs[h], left, tq, r, has_left)
            acc_ref[:, h * dh:(h + 1) * dh] = pv_d + pv_l
            rs.append(r)
        rs = tuple(rs)

        def block(start, rs):
            pvs, out = [], []
            for h in range(hps):
                pv, r = part(h, qs[h], start, tq, rs[h], None)
                pvs.append(pv)
                out.append(r)
            return pvs, tuple(out)

        def more(c):
            j, rs = c
            slack = rs[0] - z_bound[0]
            for h in range(1, hps):
                slack = jnp.minimum(slack, rs[h] - z_bound[h])
            return jnp.logical_and(j < qi, jnp.min(slack) <= SB_UNDERFLOW)

        def step(c):
            j, rs = c
            pvs, rs = block(pl.multiple_of((qi - 1 - j) * tq, tq), rs)
            for h in range(hps):
                acc_ref[:, h * dh:(h + 1) * dh] += pvs[h]
            return j + 1, rs

        n_left, totals = lax.while_loop(more, step, (has_left.astype(jnp.int32), rs))
        o_ref[...] = acc_ref[...].astype(BF16)
        for h in range(hps):
            ct_ref[0, 0, h, 0:8, :] = jnp.broadcast_to(totals[h], (tq, LANE)).T[0:8, :]
            ct_ref[0, 0, h, 8:16, :] = jnp.full((8, tq), n_left, F32)

    qb, kb, vb = q_off // hw, k_off // hw, v_off // hw
    G = heads // hps
    return pl.pallas_call(
        body, name=name, grid=(B * G, nq),
        in_specs=[pl.BlockSpec((tq, hw), lambda g, i: ((g // G) * nq + i, qb + g % G)),
                  pl.BlockSpec((seq, hw), lambda g, i: (g // G, kb + g % G)),
                  pl.BlockSpec((seq, hw), lambda g, i: (g // G, vb + g % G)),
                  pl.BlockSpec((tq, tq), lambda g, i: (0, 0))],
        out_specs=[pl.BlockSpec((tq, hw), lambda g, i: ((g // G) * nq + i, g % G)),
                   pl.BlockSpec((1, 1, hps, 16, tq), lambda g, i: (g // G, i, g % G, 0, 0))],
        out_shape=[jax.ShapeDtypeStruct((B * seq, heads * dh), BF16),
                   jax.ShapeDtypeStruct((B, nq, heads, 16, tq), F32)],
        scratch_shapes=[pltpu.VMEM((tq, hw), F32), pltpu.VMEM((hps, 8, LANE), F32)],
        compiler_params=_cp("parallel", "arbitrary"))(proj, proj, proj, jnp.tril(jnp.ones((tq, tq), BF16)))


def sba_bwd(proj, ctot, do, *, B, seq, heads, q_off, k_off, v_off, name, comm=None):
    dh = SB_HEAD_DIM
    tq = _sb_tile(seq)
    nq = seq // tq
    hps = _sb_heads_per_step(heads, SB_HEADS_BWD)
    hw = hps * dh
    scale = dh ** -0.5

    def body(q_ref, k_ref, v_ref, ct_ref, do_ref, sfx_ref, pre_ref, dq_ref, dk_ref, dv_ref, dq_acc, dk_acc, dv_acc):
        qi = pl.program_id(1)

        @pl.when(qi == 0)
        def _():
            dk_acc[...] = jnp.zeros_like(dk_acc)
            dv_acc[...] = jnp.zeros_like(dv_acc)

        tri_sfx = sfx_ref[...]
        tri_pre = pre_ref[...]
        qs = [(q_ref[:, h * dh:(h + 1) * dh].astype(F32) * scale).astype(BF16) for h in range(hps)]
        dos = [do_ref[:, h * dh:(h + 1) * dh] for h in range(hps)]
        totals = [jnp.max(jnp.broadcast_to(ct_ref[0, 0, h, 0:1, :], (LANE, tq)).T, axis=1, keepdims=True)
                  for h in range(hps)]
        dq_acc[...] = jnp.zeros_like(dq_acc)

        def part(h, rows, start, n_keys, pc, pg, mask):
            cols = slice(h * dh, (h + 1) * dh)
            q_rows, do_rows = qs[h][rows], dos[h][rows]
            k_blk = k_ref[pl.ds(start, n_keys), cols]
            v_blk = v_ref[pl.ds(start, n_keys), cols]
            z = _dot(q_rows, k_blk, NT)
            sp = _softplus(z)
            sig = jnp.exp(z - sp)
            if mask is not None:
                sp = jnp.where(mask, sp, 0.0)
            pc_next = pc + jnp.sum(sp, axis=-1, keepdims=True)
            wts = jnp.exp(z - (_dot(sp.astype(BF16), tri_sfx[0:n_keys, 0:n_keys]) + (totals[h][rows] - pc_next)))
            if mask is not None:
                wts = jnp.where(mask, wts, 0.0)
            g = _dot(do_rows, v_blk, NT) * wts
            dz = g - sig * (_dot(g.astype(BF16), tri_pre[0:n_keys, 0:n_keys]) + pg)
            if mask is not None:
                dz = jnp.where(mask, dz, 0.0)
            dz = dz.astype(BF16)
            dk_acc[pl.ds(start, n_keys), cols] += _dot(dz, q_rows, TN)
            dv_acc[pl.ds(start, n_keys), cols] += _dot(wts.astype(BF16), do_rows, TN)
            return pc_next, pg + jnp.sum(g, axis=-1, keepdims=True), _dot(dz, k_blk)

        def block(start, carry):
            out = []
            for h in range(hps):
                pc, pg, dq = part(h, slice(0, tq), start, tq, carry[h][0], carry[h][1], None)
                dq_acc[:, h * dh:(h + 1) * dh] += dq
                out.append((pc, pg))
            return tuple(out)

        zero = jnp.zeros((tq, 1), F32)
        n_left = jnp.max(ct_ref[0, 0, 0, 8:16, :]).astype(jnp.int32)
        carry = lax.fori_loop(qi - n_left, qi - 1, lambda j, c: block(pl.multiple_of(j * tq, tq), c),
                              ((zero, zero),) * hps)
        below = lax.broadcasted_iota(jnp.int32, (tq, tq), 1) < lax.broadcasted_iota(jnp.int32, (tq, tq), 0)
        has_left = n_left > 0
        left = pl.multiple_of(jnp.maximum(qi - 1, 0) * tq, tq)
        for h in range(hps):
            cols = slice(h * dh, (h + 1) * dh)
            pc, pg, dq_l = part(h, slice(0, tq), left, tq, carry[h][0], carry[h][1], has_left)
            _, _, dq_d = part(h, slice(0, tq), pl.multiple_of(qi * tq, tq), tq, pc, pg, below)
            dq_ref[:, cols] = ((dq_acc[:, cols] + dq_l + dq_d) * scale).astype(BF16)

        @pl.when(qi == nq - 1)
        def _():
            dk_ref[...] = dk_acc[...].astype(BF16)
            dv_ref[...] = dv_acc[...].astype(BF16)

    qb, kb, vb = q_off // hw, k_off // hw, v_off // hw
    G = heads // hps
    q_spec = pl.BlockSpec((tq, hw), lambda g, i: ((g // G) * nq + i, qb + g % G))
    o_spec = pl.BlockSpec((tq, hw), lambda g, i: ((g // G) * nq + i, g % G))
    kv_out = pl.BlockSpec((seq, hw), lambda g, i: (g // G, g % G))
    shp = jax.ShapeDtypeStruct((B * seq, heads * dh), BF16)
    body, c_in, c_args, c_out, c_shape, c_scratch = _hosted(body, 7, 3, comm, _grid_step(nq), B * G * nq)
    tri_spec = pl.BlockSpec((tq, tq), lambda g, i: (0, 0))
    ones = jnp.ones((tq, tq), BF16)
    out = pl.pallas_call(
        body, name=name, grid=(B * G, nq),
        in_specs=[q_spec,
                  pl.BlockSpec((seq, hw), lambda g, i: (g // G, kb + g % G)),
                  pl.BlockSpec((seq, hw), lambda g, i: (g // G, vb + g % G)),
                  pl.BlockSpec((1, 1, hps, 16, tq), lambda g, i: (g // G, i, g % G, 0, 0)), o_spec,
                  tri_spec, tri_spec] + c_in,
        out_specs=[o_spec, kv_out, kv_out] + c_out, out_shape=[shp, shp, shp] + c_shape,
        scratch_shapes=[pltpu.VMEM((tq, hw), F32), pltpu.VMEM((seq, hw), F32), pltpu.VMEM((seq, hw), F32)]
        + c_scratch,
        compiler_params=_cp("arbitrary" if comm else "parallel", "arbitrary"))(
            proj, proj, proj, ctot, do, jnp.tril(ones), jnp.triu(ones), *c_args)
    return (out[0], out[1], out[2], out[3:]) if comm else out


def _head_expand(n_heads, DI):
    j = jnp.arange(LANE, dtype=jnp.int32)[:, None]
    c = jnp.arange(DI, dtype=jnp.int32)[None, :] // SSM_HEAD_DIM
    return ((j == c) & (j < n_heads)).astype(BF16)


def _split3(x):
    hi = x.astype(BF16)
    r1 = x - hi.astype(F32)
    mid = r1.astype(BF16)
    return hi, mid, (r1 - mid.astype(F32)).astype(BF16)


def dt_fwd(proj, bias, a_log, expand, *, dt_off, name):
    T = proj.shape[0]
    DI = expand.shape[1]
    L = SSM_CHUNK
    tt = _pick(T, 512, L)

    def body(raw_ref, bias_ref, al_ref, e_ref, dt_ref, cs_ref, dtx_ref, csx_ref):
        x = raw_ref[...].astype(F32) + bias_ref[...]
        dt = _softplus(x)
        dt_ref[...] = dt
        la = dt * (-jnp.exp(al_ref[...]))
        tri = _tri01(L, True)
        for c in range(tt // L):
            cs_ref[c * L:(c + 1) * L, :] = _tri_dot3(tri, la[c * L:(c + 1) * L, :])
        e = e_ref[...]
        dtx_ref[...] = _dot(dt.astype(BF16), e).astype(BF16)
        hi, mid, lo = _split3(cs_ref[...])
        csx_ref[...] = _dot(hi, e) + _dot(mid, e) + _dot(lo, e)

    row = pl.BlockSpec((tt, LANE), lambda i: (i, 0))
    wide = pl.BlockSpec((tt, DI), lambda i: (i, 0))
    vec = pl.BlockSpec((1, LANE), lambda i: (0, 0))
    return pl.pallas_call(
        body, name=name, grid=(T // tt,),
        in_specs=[pl.BlockSpec((tt, LANE), lambda i: (i, dt_off // LANE)), vec, vec,
                  pl.BlockSpec((LANE, DI), lambda i: (0, 0))],
        out_specs=[row, row, wide, wide],
        out_shape=[jax.ShapeDtypeStruct((T, LANE), F32), jax.ShapeDtypeStruct((T, LANE), F32),
                   jax.ShapeDtypeStruct((T, DI), BF16), jax.ShapeDtypeStruct((T, DI), F32)],
        compiler_params=_cp("parallel"))(proj, bias, a_log, expand)


def dt_bwd(ddt_x, dcs_x, dcs_cols, proj, dt, bias, a_log, reduce_t, *, dt_off, n_heads, name):
    T = proj.shape[0]
    DI = reduce_t.shape[0]
    L = SSM_CHUNK
    tt = _pick(T, 512, L)

    def body(ddtx_ref, dcsx_ref, dcsc_ref, raw_ref, dt_ref, bias_ref, al_ref, r_ref, draw_ref, dbias_ref, dal_ref,
             dla_buf):
        @pl.when(pl.program_id(0) == 0)
        def _():
            dbias_ref[...] = jnp.zeros_like(dbias_ref)
            dal_ref[...] = jnp.zeros_like(dal_ref)

        r = r_ref[...]
        ddt = _dot(ddtx_ref[...], r)
        dx = dcsx_ref[...]
        hi = dx.astype(BF16)
        dcs = _dot(hi, r) + _dot((dx - hi.astype(F32)).astype(BF16), r) + dcsc_ref[...]
        triu = _tri01(L, False)
        for c in range(tt // L):
            dla_buf[c * L:(c + 1) * L, :] = _tri_dot3(triu, dcs[c * L:(c + 1) * L, :])
        dla = dla_buf[...]
        a = -jnp.exp(al_ref[...])
        valid = lax.broadcasted_iota(jnp.int32, (tt, LANE), 1) < n_heads
        dal_ref[...] += jnp.sum(jnp.where(valid, dla * dt_ref[...], 0.0), axis=0, keepdims=True) * a
        x = raw_ref[...].astype(F32) + bias_ref[...]
        draw = jnp.where(valid, (ddt + dla * a) * _sigmoid(x), 0.0)
        dbias_ref[...] += jnp.sum(draw, axis=0, keepdims=True)
        draw_ref[...] = draw.astype(BF16)

    row = pl.BlockSpec((tt, LANE), lambda i: (i, 0))
    wide = pl.BlockSpec((tt, DI), lambda i: (i, 0))
    vec = pl.BlockSpec((1, LANE), lambda i: (0, 0))
    return pl.pallas_call(
        body, name=name, grid=(T // tt,),
        in_specs=[wide, wide, row, pl.BlockSpec((tt, LANE), lambda i: (i, dt_off // LANE)), row, vec, vec,
                  pl.BlockSpec((DI, LANE), lambda i: (0, 0))],
        out_specs=[row, vec, vec],
        out_shape=[jax.ShapeDtypeStruct((T, LANE), BF16), jax.ShapeDtypeStruct((1, LANE), F32),
                   jax.ShapeDtypeStruct((1, LANE), F32)],
        scratch_shapes=[pltpu.VMEM((tt, LANE), F32)],
        compiler_params=_cp("arbitrary"))(ddt_x, dcs_x, dcs_cols, proj, dt, bias, a_log, reduce_t)


def _pair_terms(x_ref, dtx_ref, csx_ref, csr_ref, pair, ppg, lo_half, causal):
    L = SSM_CHUNK
    g, pp = divmod(pair, ppg)
    ra = g * HEAD_ROWS + 2 * pp
    cols = slice(pair * LANE, (pair + 1) * LANE)
    X = x_ref[:, cols].astype(F32)
    dt_p = dtx_ref[:, cols].astype(F32)
    own = csx_ref[:, cols]
    other = pltpu.roll(own, SSM_HEAD_DIM, 1)
    csa_c = jnp.where(lo_half, own, other)
    csb_c = jnp.where(lo_half, other, own)
    La = jnp.exp(jnp.where(causal, csa_c - csr_ref[ra:ra + 1, :], NEG_BIG))
    Lb = jnp.exp(jnp.where(causal, csb_c - csr_ref[ra + 1:ra + 2, :], NEG_BIG))
    last = csx_ref[L - 1:L, cols]
    return g, ra, cols, X, dt_p, La, Lb, jnp.exp(own), jnp.exp(last - own), jnp.exp(last)


def scan_fwd(xbc, dt_x, cs_x, cs_row, d_full, *, B, seq, DI, name):
    L, N, G = SSM_CHUNK, SSM_STATE, SSM_GROUPS
    nc = seq // L
    XW = xbc.shape[1]
    n_pairs = DI // LANE
    ppg = n_pairs // G

    def body(x_ref, dtx_ref, csx_ref, csr_ref, d_ref, y_ref, st_ref, state):
        @pl.when(pl.program_id(0) == 0)
        def _():
            state[...] = jnp.zeros_like(state)

        causal = lax.broadcasted_iota(jnp.int32, (L, L), 0) >= lax.broadcasted_iota(jnp.int32, (L, L), 1)
        lo_half = lax.broadcasted_iota(jnp.int32, (L, LANE), 1) < SSM_HEAD_DIM
        for b in range(B):
            xb, yb = x_ref.at[b], y_ref.at[b]
            cbs = []
            for g in range(G):
                Bc = xb[:, DI + g * N:DI + (g + 1) * N]
                Cc = xb[:, DI + G * N + g * N:DI + G * N + (g + 1) * N]
                cbs.append((Bc, Cc, _dot(Cc, Bc, NT)))
            for pair in range(n_pairs):
                g, _, cols, X, dt_p, La, Lb, ecs, tail, e_last = _pair_terms(
                    xb, dtx_ref.at[b], csx_ref.at[b], csr_ref.at[b], pair, ppg, lo_half, causal)
                Bc, Cc, CB = cbs[g]
                xs = X * dt_p
                xsb = xs.astype(BF16)
                y = jnp.where(lo_half, _dot((CB * La).astype(BF16), xsb), _dot((CB * Lb).astype(BF16), xsb))
                ST = state[b, pair]
                st_ref[b, 0, pair] = ST
                y = y + ecs * _dot(Cc, ST.astype(BF16)) + d_ref[:, cols] * X
                yb[:, cols] = y.astype(BF16)
                state[b, pair] = e_last * ST + _dot(Bc, (xs * tail).astype(BF16), TN)

    wide = pl.BlockSpec((B, L, DI), lambda c: (0, c, 0))
    y, states = pl.pallas_call(
        body, name=name, grid=(nc,),
        in_specs=[pl.BlockSpec((B, L, XW), lambda c: (0, c, 0)), wide, wide,
                  pl.BlockSpec((B, G * HEAD_ROWS, L), lambda c: (0, 0, c)),
                  pl.BlockSpec((1, DI), lambda c: (0, 0))],
        out_specs=[wide, pl.BlockSpec((B, 1, n_pairs, N, LANE), lambda c: (0, c, 0, 0, 0))],
        out_shape=[jax.ShapeDtypeStruct((B, seq, DI), BF16),
                   jax.ShapeDtypeStruct((B, nc, n_pairs, N, LANE), F32)],
        scratch_shapes=[pltpu.VMEM((B, n_pairs, N, LANE), F32)],
        compiler_params=_cp("arbitrary"))(xbc.reshape(B, seq, XW), dt_x.reshape(B, seq, DI),
                                          cs_x.reshape(B, seq, DI), cs_row, d_full)
    return y.reshape(B * seq, DI), states


def scan_bwd(xbc, dt_x, cs_x, cs_row, d_full, states, dy, *, B, seq, DI, name):
    L, N, G = SSM_CHUNK, SSM_STATE, SSM_GROUPS
    nc = seq // L
    XW = xbc.shape[1]
    n_pairs = DI // LANE
    ppg = n_pairs // G
    HR = G * HEAD_ROWS
    inv_p = 1.0 / SSM_HEAD_DIM

    def body(x_ref, dtx_ref, csx_ref, csr_ref, d_ref, st_ref, dy_ref, dx_ref, ddtx_ref, dcsx_ref, dcsr_ref, dd_ref,
             dH):
        @pl.when(pl.program_id(0) == 0)
        def _():
            dH[...] = jnp.zeros_like(dH)
            dd_ref[...] = jnp.zeros_like(dd_ref)

        causal = lax.broadcasted_iota(jnp.int32, (L, L), 0) >= lax.broadcasted_iota(jnp.int32, (L, L), 1)
        lo_half = lax.broadcasted_iota(jnp.int32, (L, LANE), 1) < SSM_HEAD_DIM
        last_row = lax.broadcasted_iota(jnp.int32, (L, LANE), 0) == L - 1
        head_row = lax.broadcasted_iota(jnp.int32, (HR, 1), 0)
        for b in range(B):
            xb, dxb = x_ref.at[b], dx_ref.at[b]
            dcs_rows = jnp.zeros((HR, L), F32)

            for g in range(G):
                Bc = xb[:, DI + g * N:DI + (g + 1) * N]
                Cc = xb[:, DI + G * N + g * N:DI + G * N + (g + 1) * N]
                CB = _dot(Cc, Bc, NT)
                dCB = jnp.zeros((L, L), F32)
                dC = jnp.zeros((L, N), F32)
                dB = jnp.zeros((L, N), F32)
                for pp in range(ppg):
                    pair = g * ppg + pp
                    _, ra, cols, X, dt_p, La, Lb, ecs, tail, e_last = _pair_terms(
                        xb, dtx_ref.at[b], csx_ref.at[b], csr_ref.at[b], pair, ppg, lo_half, causal)
                    xs = X * dt_p
                    xsb = xs.astype(BF16)
                    Ma, Mb = CB * La, CB * Lb
                    dY = dy_ref[b, :, cols].astype(F32)
                    dYb = dY.astype(BF16)
                    dMa = _dot(jnp.where(lo_half, dY, 0.0).astype(BF16), xsb, NT)
                    dMb = _dot(jnp.where(lo_half, 0.0, dY).astype(BF16), xsb, NT)
                    dSa, dSb = dMa * Ma, dMb * Mb
                    dCB = dCB + dMa * La + dMb * Lb
                    dcs = jnp.where(lo_half, jnp.sum(dSa, axis=1, keepdims=True),
                                    jnp.sum(dSb, axis=1, keepdims=True)) * inv_p
                    dcs_rows = dcs_rows - jnp.where(head_row == ra, jnp.sum(dSa, axis=0, keepdims=True), 0.0)
                    dcs_rows = dcs_rows - jnp.where(head_row == ra + 1, jnp.sum(dSb, axis=0, keepdims=True), 0.0)
                    dxs = jnp.where(lo_half, _dot(Ma.astype(BF16), dYb, TN), _dot(Mb.astype(BF16), dYb, TN))
                    ST = st_ref[b, 0, pair]
                    STb = ST.astype(BF16)
                    dYe = (dY * ecs).astype(BF16)
                    dC = dC + _dot(dYe, STb, NT)
                    dSTp = _dot(Cc, dYe, TN)
                    dcs = dcs + dY * (ecs * _dot(Cc, STb))
                    dSTn = dH[b, pair]
                    dSTnb = dSTn.astype(BF16)
                    dSTp = dSTp + e_last * dSTn
                    XBt = _dot(Bc, dSTnb)
                    dxs = dxs + tail * XBt
                    t2 = xs * XBt * tail
                    at_end = (e_last * jnp.sum(dSTn * ST, axis=0, keepdims=True)
                              + jnp.sum(t2, axis=0, keepdims=True))
                    dcs = dcs - t2 + jnp.where(last_row, at_end, 0.0)
                    dB = dB + _dot((xs * tail).astype(BF16), dSTnb, NT)
                    dxb[:, cols] = (dxs * dt_p + d_ref[:, cols] * dY).astype(BF16)
                    ddtx_ref[b, :, cols] = (dxs * X).astype(BF16)
                    dcsx_ref[b, :, cols] = dcs
                    dd_ref[b, :, cols] += jnp.sum(dY * X, axis=0, keepdims=True)
                    dH[b, pair] = dSTp
                dCBb = dCB.astype(BF16)
                dxb[:, DI + g * N:DI + (g + 1) * N] = (dB + _dot(dCBb, Cc, TN)).astype(BF16)
                dxb[:, DI + G * N + g * N:DI + G * N + (g + 1) * N] = (dC + _dot(dCBb, Bc)).astype(BF16)
            dcsr_ref[b] = dcs_rows

    rev = lambda c: (0, nc - 1 - c, 0)
    wide = pl.BlockSpec((B, L, DI), rev)
    xspec = pl.BlockSpec((B, L, XW), rev)
    hrow = pl.BlockSpec((B, HR, L), lambda c: (0, 0, nc - 1 - c))
    dx, ddt_x, dcs_x, dcs_row, dd = pl.pallas_call(
        body, name=name, grid=(nc,),
        in_specs=[xspec, wide, wide, hrow, pl.BlockSpec((1, DI), lambda c: (0, 0)),
                  pl.BlockSpec((B, 1, n_pairs, N, LANE), lambda c: (0, nc - 1 - c, 0, 0, 0)), wide],
        out_specs=[xspec, wide, wide, hrow, pl.BlockSpec((B, 1, DI), lambda c: (0, 0, 0))],
        out_shape=[jax.ShapeDtypeStruct((B, seq, XW), BF16), jax.ShapeDtypeStruct((B, seq, DI), BF16),
                   jax.ShapeDtypeStruct((B, seq, DI), F32), jax.ShapeDtypeStruct((B, HR, seq), F32),
                   jax.ShapeDtypeStruct((B, 1, DI), F32)],
        scratch_shapes=[pltpu.VMEM((B, n_pairs, N, LANE), F32)],
        compiler_params=_cp("arbitrary"))(xbc.reshape(B, seq, XW), dt_x.reshape(B, seq, DI), cs_x.reshape(B, seq, DI),
                                          cs_row, d_full, states, dy.reshape(B, seq, DI))
    return dx.reshape(B * seq, XW), ddt_x.reshape(B * seq, DI), dcs_x.reshape(B * seq, DI), dcs_row, dd


def gnorm_fwd(y, proj, w, *, DI, name):
    T = y.shape[0]
    tt = _pick(T, 512, 16)
    gw = DI // SSM_GROUPS

    def body(y_ref, z_ref, w_ref, o_ref):
        for g in range(SSM_GROUPS):
            sl = slice(g * gw, (g + 1) * gw)
            y2 = y_ref[:, sl].astype(F32) * _silu(z_ref[:, sl].astype(F32))
            r = lax.rsqrt(jnp.mean(y2 * y2, axis=-1, keepdims=True) + EPS)
            o_ref[:, sl] = (y2 * r * w_ref[:, sl]).astype(BF16)

    row = pl.BlockSpec((tt, DI), lambda i: (i, 0))
    return pl.pallas_call(
        body, name=name, grid=(T // tt,),
        in_specs=[row, row, pl.BlockSpec((1, DI), lambda i: (0, 0))], out_specs=row,
        out_shape=jax.ShapeDtypeStruct((T, DI), BF16), compiler_params=_cp("parallel"))(y, proj, w)


def gnorm_bwd(dyn, y, proj, w, *, DI, name):
    T = y.shape[0]
    tt = _pick(T, 512, 16)
    gw = DI // SSM_GROUPS

    def body(dyn_ref, y_ref, z_ref, w_ref, dy_ref, dz_ref, dw_ref):
        @pl.when(pl.program_id(0) == 0)
        def _():
            dw_ref[...] = jnp.zeros_like(dw_ref)

        for g in range(SSM_GROUPS):
            sl = slice(g * gw, (g + 1) * gw)
            yv = y_ref[:, sl].astype(F32)
            z = z_ref[:, sl].astype(F32)
            sz = _silu(z)
            y2 = yv * sz
            r = lax.rsqrt(jnp.mean(y2 * y2, axis=-1, keepdims=True) + EPS)
            xhat = y2 * r
            d = dyn_ref[:, sl].astype(F32)
            dw_ref[:, sl] += jnp.sum(d * xhat, axis=0, keepdims=True)
            dxh = d * w_ref[:, sl]
            dy2 = r * (dxh - xhat * jnp.mean(dxh * xhat, axis=-1, keepdims=True))
            dy_ref[:, sl] = (dy2 * sz).astype(BF16)
            dz_ref[:, sl] = (dy2 * yv * _silu_grad(z)).astype(BF16)

    row = pl.BlockSpec((tt, DI), lambda i: (i, 0))
    vec = pl.BlockSpec((1, DI), lambda i: (0, 0))
    shp = jax.ShapeDtypeStruct((T, DI), BF16)
    return pl.pallas_call(
        body, name=name, grid=(T // tt,), in_specs=[row, row, row, vec], out_specs=[row, row, vec],
        out_shape=[shp, shp, jax.ShapeDtypeStruct((1, DI), F32)],
        compiler_params=_cp("arbitrary"))(dyn, y, proj, w)


N_CHIP = 4
SHARD_ROW_ALIGN = 128


def _comm_out_shapes(srcs, modes):
    return [jax.ShapeDtypeStruct(((N_DEV,) if mode in ("gather", "gather_direct") else ()) + s.shape, s.dtype)
            for s, mode in zip(srcs, modes)]


def _comm_scratch(n):
    return [pltpu.SemaphoreType.DMA((n, N_DEV - 1)), pltpu.SemaphoreType.DMA((n, N_DEV - 1)),
            pltpu.SemaphoreType.DMA((n,))]


def _comm_phases(modes, src_refs, out_refs, send_sems, recv_sems, local_sems):
    x, y, c = lax.axis_index("x"), lax.axis_index("y"), lax.axis_index("c")
    me, sibling = (x, y, c), (x, y, 1 - c)
    chips = [(1 - x, y), (x, 1 - y), (1 - x, 1 - y)]
    relays = [a for a, mode in enumerate(modes) if mode == "gather"]

    def slot(p):
        return 4 * p[0] + 2 * p[1] + p[2]

    def remote(a, k, src, dst, to):
        return pltpu.make_async_remote_copy(src_ref=src, dst_ref=dst, send_sem=send_sems.at[a, k],
                                            recv_sem=recv_sems.at[a, k], device_id=to,
                                            device_id_type=pl.DeviceIdType.MESH)

    def first_copies():
        local, two_way, send_only = [], [], []
        for a, mode in enumerate(modes):
            src, out = src_refs[a], out_refs[a]
            if mode == "sibling":
                two_way.append(remote(a, 0, src, out, sibling))
            elif mode == "chips":
                mine = 2 * x + y
                local.append(pltpu.make_async_copy(src.at[mine], out.at[mine], local_sems.at[a]))
                for j, chip in enumerate(chips):
                    two_way.append(remote(a, 1 + j, src.at[2 * chip[0] + chip[1]], out.at[mine], (*chip, c)))
            elif mode == "gather_direct":
                local.append(pltpu.make_async_copy(src, out.at[slot(me)], local_sems.at[a]))
                for k in range(1, N_DEV):
                    peer = (1 - x if k & 4 else x, 1 - y if k & 2 else y, 1 - c if k & 1 else c)
                    two_way.append(remote(a, k - 1, src, out.at[slot(me)], peer))
            else:
                assert mode == "gather"
                local.append(pltpu.make_async_copy(src, out.at[slot(me)], local_sems.at[a]))
                send_only.append(remote(a, 0, src, out.at[slot(me)], sibling))
                for j, chip in enumerate(chips):
                    send_only.append(remote(a, 1 + j, src, out.at[slot(me)], (*chip, c)))
        return local, two_way, send_only

    def forwards():
        out = []
        for a in relays:
            for j, chip in enumerate(chips):
                landed = out_refs[a].at[slot((*chip, c))]
                out.append((remote(a, 1 + j, landed, landed, me), remote(a, 4 + j, landed, landed, sibling)))
        return out

    def start():
        local, two_way, send_only = first_copies()
        for cp in local + two_way + send_only:
            cp.start()

    def relay():
        for arrival, fwd in forwards():
            arrival.wait_recv()
            fwd.start()

    def finish():
        local, two_way, send_only = first_copies()
        for a in relays:
            blk = out_refs[a].at[slot(sibling)]
            remote(a, 0, blk, blk, me).wait_recv()
            for j, chip in enumerate(chips):
                blk = out_refs[a].at[slot((*chip, 1 - c))]
                remote(a, 4 + j, blk, blk, me).wait_recv()
        for cp in send_only + [fwd for _, fwd in forwards()]:
            cp.wait_send()
        for cp in two_way + local:
            cp.wait()

    return start, relay, finish, bool(relays)


def _hosted(body, n_in, n_out, comm, step, n_steps):
    if comm is None:
        return body, [], [], [], [], []
    srcs, modes = comm
    nc = len(srcs)

    def wrapped(*refs):
        ins, csrc = refs[:n_in], refs[n_in:n_in + nc]
        outs = refs[n_in + nc:n_in + nc + n_out]
        cout = refs[n_in + nc + n_out:n_in + 2 * nc + n_out]
        scratch = refs[n_in + 2 * nc + n_out:len(refs) - 3]
        start, relay, finish, has_relay = _comm_phases(modes, csrc, cout, *refs[len(refs) - 3:])
        s = step()
        pl.when(s == 0)(start)
        body(*ins, *outs, *scratch)
        if has_relay:
            pl.when(s == (2 * n_steps) // 3)(relay)
        pl.when(s == n_steps - 1)(finish)

    any_spec = pl.BlockSpec(memory_space=pl.ANY)
    return wrapped, [any_spec] * nc, list(srcs), [any_spec] * nc, _comm_out_shapes(srcs, modes), _comm_scratch(nc)


def exchange(srcs, modes, *, name):
    n = len(srcs)

    def body(*refs):
        start, relay, finish, has_relay = _comm_phases(modes, refs[:n], refs[n:2 * n], *refs[2 * n:])
        start()
        if has_relay:
            relay()
        finish()

    any_spec = pl.BlockSpec(memory_space=pl.ANY)
    return pl.pallas_call(
        body, name=name, in_specs=[any_spec] * n, out_specs=[any_spec] * n, out_shape=_comm_out_shapes(srcs, modes),
        scratch_shapes=_comm_scratch(n), compiler_params=pltpu.CompilerParams(has_side_effects=True))(*srcs)


def pair_sum(a, b, *, name):
    n, R, C = a.shape
    tr = _pick(n * R, 1024, 16)

    def body(a_ref, b_ref, o_ref):
        o_ref[...] = (a_ref[...].astype(F32) + b_ref[...].astype(F32)).astype(BF16)

    blk = pl.BlockSpec((tr, C), lambda i: (i, 0))
    out = pl.pallas_call(
        body, name=name, grid=(n * R // tr,), in_specs=[blk, blk], out_specs=blk,
        out_shape=jax.ShapeDtypeStruct((n * R, C), BF16),
        compiler_params=_cp("parallel"))(a.reshape(n * R, C), b.reshape(n * R, C))
    return out.reshape(n, R, C)


def sum_slots(recv, *, name):
    _, R, C = recv.shape
    tr = _pick(R, 512, 8)

    def body(r_ref, o_ref):
        acc = r_ref[0].astype(F32)
        for p in range(1, N_DEV):
            acc = acc + r_ref[p].astype(F32)
        o_ref[...] = acc

    return pl.pallas_call(
        body, name=name, grid=(R // tr,),
        in_specs=[pl.BlockSpec((N_DEV, tr, C), lambda i: (0, i, 0))],
        out_specs=pl.BlockSpec((tr, C), lambda i: (i, 0)),
        out_shape=jax.ShapeDtypeStruct((R, C), F32), compiler_params=_cp("parallel"))(recv)


def adamw(gsrc, w, m, v, *, name):
    slots, R, C = gsrc.shape
    tr = _pick(R, 256, 16 if gsrc.dtype == BF16 else 8)
    c1 = 1.0 / (1.0 - ADAM_B1 ** ADAM_STEP)
    c2 = 1.0 / (1.0 - ADAM_B2 ** ADAM_STEP)

    def body(g_ref, w_ref, m_ref, v_ref, go_ref, d_ref, mo_ref, vo_ref):
        g = g_ref[0].astype(F32)
        for p in range(1, slots):
            g = g + g_ref[p].astype(F32)
        m2 = ADAM_B1 * m_ref[...] + (1.0 - ADAM_B1) * g
        v2 = ADAM_B2 * v_ref[...] + (1.0 - ADAM_B2) * (g * g)
        go_ref[...] = g
        mo_ref[...] = m2
        vo_ref[...] = v2
        d_ref[...] = -ADAM_LR * ((m2 * c1) / (jnp.sqrt(v2 * c2) + ADAM_EPS) + ADAM_WD * w_ref[...])

    blk = pl.BlockSpec((tr, C), lambda i: (i, 0))
    shp = jax.ShapeDtypeStruct((R, C), F32)
    return pl.pallas_call(
        body, name=name, grid=(R // tr,),
        in_specs=[pl.BlockSpec((slots, tr, C), lambda i: (0, i, 0)), blk, blk, blk],
        out_specs=[blk] * 4, out_shape=[shp] * 4, compiler_params=_cp("parallel"))(gsrc, w, m, v)


def _pad_cols(a, n):
    return jnp.pad(a, ((0, 0), (0, n - a.shape[1])))


def _to_rows(a, B, seq, H):
    G = SSM_GROUPS
    R = H // G
    t = a[:, :H].reshape(B, seq, G, R).transpose(0, 2, 3, 1)
    t = jnp.pad(t, ((0, 0), (0, 0), (0, HEAD_ROWS - R), (0, 0)))
    return t.reshape(B, G * HEAD_ROWS, seq)


def _from_rows(a, B, seq, H):
    G = SSM_GROUPS
    R = H // G
    t = a.reshape(B, G, HEAD_ROWS, seq)[:, :, :R].transpose(0, 3, 1, 2).reshape(B * seq, H)
    return _pad_cols(t, LANE)


def _chip_split(grads):
    c_idx = lax.axis_index("c")
    keep, give = [], []
    for g in grads:
        by_chip = g.reshape((N_CHIP, 2) + g.shape[1:])
        keep.append(lax.dynamic_index_in_dim(by_chip, c_idx, axis=1, keepdims=False))
        give.append(lax.dynamic_index_in_dim(by_chip, 1 - c_idx, axis=1, keepdims=False))
    return keep, give


def _chip_sums(grads, name):
    keep, give = _chip_split(grads)
    swapped = exchange(give, ["sibling"] * len(give), name="swap_" + name)
    return [pair_sum(k, s, name=f"chip_sum_{name}_{i}") for i, (k, s) in enumerate(zip(keep, swapped))]


def local_step(x, target, loc, od_w_in_t, *, B, seq):
    T, D = x.shape
    CW = D
    heads = CW // SB_HEAD_DIM
    DI = 2 * D
    H = DI // SSM_HEAD_DIM
    XW = DI + 2 * SSM_GROUPS * SSM_STATE
    in_odd = DI + XW + H
    w1_rows = in_odd // N_DEV
    q_off, k_off, v_off, gc_off, ga_off = 3 * CW, 4 * CW, 5 * CW, 2 * CW, 6 * CW
    dt_off = DI + XW

    small_packed, small_spans = _pack_rows([loc[n] for n in SMALL_SHARDED], LANE, 8)
    n0, (g_ev_in, small_all) = rmsnorm_fwd(x, loc["ev_norm_w"], name="l0_norm",
                                           comm=([loc["ev_w_in"].astype(BF16), small_packed],
                                                 ["gather", "gather_direct"]))
    p = {n: loc[n] for n in SMALL}
    for n, a in zip(SMALL_SHARDED, _unpack_rows(small_all, small_spans)):
        p[n] = _col_unshards(a)
    p["ev_w_in"] = _col_unshards(g_ev_in)
    proj0, (g_od_in_t,) = mm_nn(n0, p["ev_w_in"], out_dtype=BF16, name="l0_in_proj",
                                comm=([od_w_in_t.astype(BF16)], ["gather"]))
    w1t = g_od_in_t[:, :w1_rows].reshape(in_odd, D)
    w1t = jnp.pad(w1t, ((0, -(-(in_odd + LANE) // 256) * 256 - in_odd), (0, 0)))
    (u2,), (g_ev_out, g_od_out) = dwconv_fwd(
        proj0, (0, CW), p["ev_dw_w"], p["ev_dw_b"], C=CW, seq=seq, glu=True, silu_out=False, name="l0_conv",
        comm=([loc["ev_w_out"].astype(BF16), loc["od_w_out"].astype(BF16)], ["gather"] * 2))
    p["ev_w_out"] = g_ev_out.reshape(-1, D)
    od_w_out = g_od_out.reshape(-1, D)
    o, ctot = sba_fwd(proj0, B=B, seq=seq, heads=heads, q_off=q_off, k_off=k_off, v_off=v_off, name="l0_attn")
    ycat = mix0_post_fwd(u2, proj0, o, p["ev_ln_w"], p["ev_ln_b"], CW=CW, gc_off=gc_off, ga_off=ga_off,
                         name="l0_post")
    h1 = mm_nn(ycat, p["ev_w_out"], add=x, out_dtype=F32, name="l0_out_proj")

    n1 = rmsnorm_fwd(h1, p["od_norm_w"], name="l1_norm")
    proj1 = mm_nt_terms([(n1, 0, D, 0)], w1t, out_dtype=BF16, name="l1_in_proj")
    u_pre, xbc = dwconv_fwd(proj1, (DI,), p["od_conv_w"], p["od_conv_b"], C=XW, seq=seq, glu=False, silu_out=True,
                            name="l1_conv")
    bias_p, alog_p = _pad_cols(p["od_dt_bias"], LANE), _pad_cols(p["od_a_log"], LANE)
    expand = _head_expand(H, DI)
    dt, cs, dt_x, cs_x = dt_fwd(proj1, bias_p, alog_p, expand, dt_off=dt_off, name="l1_dt")
    cs_row = _to_rows(cs, B, seq, H)
    d_full = jnp.repeat(p["od_d"], SSM_HEAD_DIM, axis=1)
    y_ssd, states = scan_fwd(xbc, dt_x, cs_x, cs_row, d_full, B=B, seq=seq, DI=DI, name="l1_ssd")
    yn = gnorm_fwd(y_ssd, proj1, p["od_gnorm_w"], DI=DI, name="l1_gnorm")
    h2 = mm_nn(yn, od_w_out, add=h1, out_dtype=F32, name="l1_out_proj")

    loss, dh2, dh2b, g_final = final_loss(h2, p["final_norm_w"], target, name="loss_head")

    g_od_w_out = mm_tn(yn, dh2b, out_dtype=BF16, name="l1_dw_out")
    dyn = mm_nt_terms([(dh2b, 0, D, 0)], od_w_out, out_dtype=BF16, name="l1_d_out_proj")
    dy_ssd, dz, g_gnorm = gnorm_bwd(dyn, y_ssd, proj1, p["od_gnorm_w"], DI=DI, name="l1_gnorm_bwd")
    dxbc_c, ddt_x, dcs_x, dcs_row, dd_part = scan_bwd(xbc, dt_x, cs_x, cs_row, d_full, states, dy_ssd, B=B, seq=seq,
                                                      DI=DI, name="l1_ssd_bwd")
    g_d = dd_part.sum(axis=(0, 1)).reshape(H, SSM_HEAD_DIM).sum(axis=1)[None, :]
    draw, g_bias, g_alog = dt_bwd(ddt_x, dcs_x, _from_rows(dcs_row, B, seq, H), proj1, dt, bias_p, alog_p, expand.T,
                                  dt_off=dt_off, n_heads=H, name="l1_dt_bwd")
    dxbc, g_conv_w, g_conv_b = dwconv_bwd(dxbc_c, u_pre, proj1, (DI,), p["od_conv_w"], C=XW, seq=seq, glu=False,
                                          silu_out=True, name="l1_conv_bwd")
    tw = 512 if DI % 512 == 0 else LANE
    terms = [(dz, j, tw, j * tw) for j in range(DI // tw)]
    terms += [(dxbc, j, tw, DI + j * tw) for j in range(XW // tw)]
    terms += [(draw, 0, LANE, dt_off)]
    dn1 = mm_nn_terms(terms, w1t, out_dtype=BF16, name="l1_d_in_proj")
    g_od_w_in_t = jnp.concatenate([mm_tn(dz, n1, out_dtype=BF16, name="l1_dw_in_z"),
                                   mm_tn(dxbc, n1, out_dtype=BF16, name="l1_dw_in_xbc"),
                                   mm_tn(draw, n1, out_dtype=BF16, name="l1_dw_in_dt")], axis=0)[:in_odd]
    w1_pad = (-w1_rows) % SHARD_ROW_ALIGN
    keep, give = _chip_split([jnp.pad(g_od_w_in_t.reshape(N_DEV, w1_rows, D), ((0, 0), (0, w1_pad), (0, 0))),
                              g_od_w_out.reshape(N_DEV, -1, D)])
    dh1, dh1b, g_od_norm, swapped = rmsnorm_bwd(h1, p["od_norm_w"], dn1, dh2, name="l1_norm_bwd",
                                                comm=(give, ["sibling"] * 2))
    l1_chip = [pair_sum(k, s, name=f"chip_sum_l1_{i}") for i, (k, s) in enumerate(zip(keep, swapped))]

    g_ev_w_out = mm_tn(ycat, dh1b, out_dtype=BF16, name="l0_dw_out")
    dycat = mm_nt_terms([(dh1b, 0, D, 0)], p["ev_w_out"], out_dtype=BF16, name="l0_d_out_proj")
    du2, dgc, dga, do, g_ln_w, g_ln_b = mix0_post_bwd(dycat, u2, proj0, o, p["ev_ln_w"], p["ev_ln_b"], CW=CW,
                                                      gc_off=gc_off, ga_off=ga_off, name="l0_post_bwd")
    dq, dk, dv, (r_od_in_t, r_od_out) = sba_bwd(proj0, ctot, do, B=B, seq=seq, heads=heads, q_off=q_off, k_off=k_off,
                                                v_off=v_off, name="l0_attn_bwd", comm=(l1_chip, ["chips", "chips"]))
    dga_a, dga_b, g_dw_w, g_dw_b = dwconv_bwd(du2, None, proj0, (0, CW), p["ev_dw_w"], C=CW, seq=seq, glu=True,
                                              silu_out=False, name="l0_conv_bwd")
    pieces = [dga_a, dga_b, dgc, dq, dk, dv, dga]
    g_ev_w_in = jnp.concatenate([mm_tn(n0, pc, out_dtype=BF16, name=f"l0_dw_in_{j}") for j, pc in enumerate(pieces)],
                                axis=1)
    l0_chip = _chip_sums([_col_shards(g_ev_w_in), g_ev_w_out.reshape(N_DEV, -1, D)], "l0")
    dn0, (r_ev_in, r_ev_out) = mm_nt_terms([(pc, 0, CW, j * CW) for j, pc in enumerate(pieces)], p["ev_w_in"],
                                           out_dtype=BF16, name="l0_d_in_proj", comm=(l0_chip, ["chips", "chips"]))
    dx, _, g_ev_norm = rmsnorm_bwd(x, p["ev_norm_w"], dn0, dh1, name="l0_norm_bwd")

    small = dict(ev_norm_w=g_ev_norm, ev_dw_w=g_dw_w, ev_dw_b=g_dw_b, ev_ln_w=g_ln_w, ev_ln_b=g_ln_b,
                 od_norm_w=g_od_norm, od_conv_w=g_conv_w, od_conv_b=g_conv_b, od_dt_bias=g_bias[:, :H],
                 od_a_log=g_alog[:, :H], od_d=g_d, od_gnorm_w=g_gnorm, final_norm_w=g_final)
    received = dict(ev_w_in=r_ev_in, ev_w_out=r_ev_out, od_w_in=r_od_in_t, od_w_out=r_od_out)
    return loss, dx, small, received


BIG = ("ev_w_in", "ev_w_out", "od_w_in", "od_w_out")
SMALL = ("ev_norm_w", "ev_dw_w", "ev_dw_b", "ev_ln_w", "ev_ln_b", "od_norm_w", "od_conv_w", "od_conv_b",
         "od_dt_bias", "od_a_log", "od_d", "od_gnorm_w", "final_norm_w")
SMALL_SHARDED = ("ev_dw_w", "od_norm_w", "od_conv_w", "od_conv_b", "od_gnorm_w")
ORDER = ("ev_norm_w", "ev_w_in", "ev_dw_w", "ev_dw_b", "ev_ln_w", "ev_ln_b", "ev_w_out", "od_norm_w", "od_w_in",
         "od_conv_w", "od_conv_b", "od_dt_bias", "od_a_log", "od_d", "od_gnorm_w", "od_w_out", "final_norm_w")


def _pack_rows(arrs, width, row_align):
    parts, spans, r0 = [], [], 0
    for a in arrs:
        flat = a.reshape(-1)
        rows = -(-flat.shape[0] // (width * row_align)) * row_align
        parts.append(jnp.pad(flat, (0, rows * width - flat.shape[0])).reshape(rows, width))
        spans.append((r0, a.size, a.shape))
        r0 += rows
    return jnp.concatenate(parts, axis=0), spans


def _unpack_rows(packed, spans):
    lead = packed.shape[:-2]
    width = packed.shape[-1]
    out = []
    for r0, size, shape in spans:
        rows = -(-size // width)
        blk = packed[..., r0:r0 + rows, :].reshape(lead + (rows * width,))[..., :size]
        out.append(blk.reshape(lead + tuple(shape)))
    return out


def _col_shards(a):
    R, C8 = a.shape
    return a.reshape(R, N_DEV, C8 // N_DEV).transpose(1, 0, 2)


def _col_unshards(a):
    n, R, C = a.shape
    return a.transpose(1, 0, 2).reshape(R, n * C)


def kernel(x, ev_norm_w, ev_w_in, ev_dw_w, ev_dw_b, ev_ln_w, ev_ln_b, ev_w_out, od_norm_w, od_w_in, od_conv_w, od_conv_b, od_dt_bias, od_a_log, od_d, od_gnorm_w, od_w_out, final_norm_w, loss_target, m_ev_norm_w, m_ev_w_in, m_ev_dw_w, m_ev_dw_b, m_ev_ln_w, m_ev_ln_b, m_ev_w_out, m_od_norm_w, m_od_w_in, m_od_conv_w, m_od_conv_b, m_od_dt_bias, m_od_a_log, m_od_d, m_od_gnorm_w, m_od_w_out, m_final_norm_w, v_ev_norm_w, v_ev_w_in, v_ev_dw_w, v_ev_dw_b, v_ev_ln_w, v_ev_ln_b, v_ev_w_out, v_od_norm_w, v_od_w_in, v_od_conv_w, v_od_conv_b, v_od_dt_bias, v_od_a_log, v_od_d, v_od_gnorm_w, v_od_w_out, v_final_norm_w):
    loc = dict(ev_norm_w=ev_norm_w, ev_w_in=ev_w_in, ev_dw_w=ev_dw_w, ev_dw_b=ev_dw_b, ev_ln_w=ev_ln_w,
               ev_ln_b=ev_ln_b, ev_w_out=ev_w_out, od_norm_w=od_norm_w, od_w_in=od_w_in, od_conv_w=od_conv_w,
               od_conv_b=od_conv_b, od_dt_bias=od_dt_bias, od_a_log=od_a_log, od_d=od_d, od_gnorm_w=od_gnorm_w,
               od_w_out=od_w_out, final_norm_w=final_norm_w)
    mom = dict(ev_norm_w=m_ev_norm_w, ev_w_in=m_ev_w_in, ev_dw_w=m_ev_dw_w, ev_dw_b=m_ev_dw_b, ev_ln_w=m_ev_ln_w,
               ev_ln_b=m_ev_ln_b, ev_w_out=m_ev_w_out, od_norm_w=m_od_norm_w, od_w_in=m_od_w_in,
               od_conv_w=m_od_conv_w, od_conv_b=m_od_conv_b, od_dt_bias=m_od_dt_bias, od_a_log=m_od_a_log,
               od_d=m_od_d, od_gnorm_w=m_od_gnorm_w, od_w_out=m_od_w_out, final_norm_w=m_final_norm_w)
    var = dict(ev_norm_w=v_ev_norm_w, ev_w_in=v_ev_w_in, ev_dw_w=v_ev_dw_w, ev_dw_b=v_ev_dw_b, ev_ln_w=v_ev_ln_w,
               ev_ln_b=v_ev_ln_b, ev_w_out=v_ev_w_out, od_norm_w=v_od_norm_w, od_w_in=v_od_w_in,
               od_conv_w=v_od_conv_w, od_conv_b=v_od_conv_b, od_dt_bias=v_od_dt_bias, od_a_log=v_od_a_log,
               od_d=v_od_d, od_gnorm_w=v_od_gnorm_w, od_w_out=v_od_w_out, final_norm_w=v_final_norm_w)
    shapes = {n: loc[n].shape for n in ORDER}
    loc = {n: (a.reshape(1, -1) if a.ndim == 1 else a.reshape(a.shape[-2:]) if a.ndim == 3 else a)
           for n, a in loc.items()}
    mom = {n: a.reshape(loc[n].shape) for n, a in mom.items()}
    var = {n: a.reshape(loc[n].shape) for n, a in var.items()}

    B, seq, D = x.shape
    me = 4 * lax.axis_index("x") + 2 * lax.axis_index("y") + lax.axis_index("c")

    w1_rows = loc["od_w_in"].shape[1]
    w1_pad = (-w1_rows) % SHARD_ROW_ALIGN

    def to_t(a):
        return jnp.pad(a.T, ((0, w1_pad), (0, 0)))

    loss, dx, grads, received = local_step(x.reshape(B * seq, D), loss_target.reshape(B * seq, D), loc,
                                           to_t(loc["od_w_in"]), B=B, seq=seq)

    gsmall_packed, gsmall_spans = _pack_rows([grads[n] for n in SMALL] + [loss], LANE, 8)
    (gsmall_recv,) = exchange([gsmall_packed], ["gather_direct"], name="gather_small_grads")

    big_out = [{} for _ in range(4)]
    for n in ("ev_w_in", "ev_w_out", "od_w_out"):
        for kind, a in enumerate(adamw(received[n], loc[n], mom[n], var[n], name="adamw_" + n)):
            big_out[kind][n] = a
    for kind, a in enumerate(adamw(received["od_w_in"], to_t(loc["od_w_in"]), to_t(mom["od_w_in"]),
                                   to_t(var["od_w_in"]), name="adamw_od_w_in")):
        big_out[kind]["od_w_in"] = a[:w1_rows].T

    summed = _unpack_rows(sum_slots(gsmall_recv, name="sum_small_grads"), gsmall_spans)
    loss_total = summed[-1][0, 0]
    gsmall = dict(zip(SMALL, summed[:-1]))
    for n in SMALL_SHARDED:
        width = loc[n].shape[1]
        gsmall[n] = lax.dynamic_slice_in_dim(gsmall[n], me * width, width, axis=1)
    gs, sspans = _pack_rows([gsmall[n] for n in SMALL], LANE, 8)
    ws, _ = _pack_rows([loc[n] for n in SMALL], LANE, 8)
    ms, _ = _pack_rows([mom[n] for n in SMALL], LANE, 8)
    vs, _ = _pack_rows([var[n] for n in SMALL], LANE, 8)
    small_out = [dict(zip(SMALL, _unpack_rows(a, sspans))) for a in adamw(gs[None], ws, ms, vs, name="adamw_small")]

    outs = [loss_total, dx.reshape(B, seq, D)]
    for kind in range(4):
        for n in ORDER:
            src = big_out[kind] if n in BIG else small_out[kind]
            outs.append(src[n].reshape(shapes[n]))
    return tuple(outs)
```

```python
import jax
import jax.numpy as jnp
from jax import lax
from jax.experimental import pallas as pl
from jax.experimental.pallas import tpu as pltpu

F32 = jnp.float32
BF16 = jnp.bfloat16

EPS = 1e-6
N_DEV = 8
LANE = 128
VMEM_LIMIT_BYTES = 48 * 1024 * 1024

SB_HEAD_DIM = 128
SSM_HEAD_DIM = 64
SSM_GROUPS = 4
SSM_STATE = 128
SSM_CHUNK = 128
HALO = 32
HEAD_ROWS = 8
NEG_BIG = -1e30

ADAM_LR = 0.001
ADAM_B1 = 0.9
ADAM_B2 = 0.999
ADAM_EPS = 1e-08
ADAM_WD = 0.01
ADAM_STEP = 10

NT = (((1,), (1,)), ((), ()))
TN = (((0,), (0,)), ((), ()))


def _cp(*sem):
    return pltpu.CompilerParams(dimension_semantics=sem, vmem_limit_bytes=VMEM_LIMIT_BYTES)


def _pick(n, cap, align):
    if n <= cap:
        return n
    t = (cap // align) * align
    while t >= align:
        if n % t == 0:
            return t
        t -= align
    raise ValueError(f"no tile for {n} (cap {cap}, align {align})")


def _sigmoid(x):
    return 0.5 * jnp.tanh(0.5 * x) + 0.5


def _silu(x):
    return x * _sigmoid(x)


def _silu_grad(x):
    s = _sigmoid(x)
    return s * (1.0 + x * (1.0 - s))


def _dot(a, b, dims=None):
    if dims is None:
        return jnp.dot(a, b, preferred_element_type=F32)
    return lax.dot_general(a, b, dims, preferred_element_type=F32)


def _tri_dot3(tri, x):
    hi = x.astype(BF16)
    r1 = x - hi.astype(F32)
    mid = r1.astype(BF16)
    lo = (r1 - mid.astype(F32)).astype(BF16)
    return _dot(tri, hi) + _dot(tri, mid) + _dot(tri, lo)


def _grid_step(n_inner):
    return lambda: pl.program_id(0) * n_inner + pl.program_id(1)


def mm_nn(a, b, *, add=None, out_dtype, name, comm=None):
    M, K = a.shape
    N = b.shape[1]
    tm = _pick(M, 2048 if K <= 1024 and add is None else 1024, 16)
    tn = _pick(N, 1024, LANE)

    def body(*refs):
        if add is None:
            a_ref, b_ref, o_ref = refs
        else:
            a_ref, b_ref, add_ref, o_ref = refs
        acc = _dot(a_ref[...], b_ref[...])
        if add is not None:
            acc = acc + add_ref[...]
        o_ref[...] = acc.astype(out_dtype)

    in_specs = [pl.BlockSpec((tm, K), lambda i, j: (i, 0)), pl.BlockSpec((K, tn), lambda i, j: (0, j))]
    args = [a, b]
    if add is not None:
        in_specs.append(pl.BlockSpec((tm, tn), lambda i, j: (i, j)))
        args.append(add)
    grid = (M // tm, N // tn)
    body, c_in, c_args, c_out, c_shape, c_scratch = _hosted(body, len(args), 1, comm, _grid_step(grid[1]),
                                                            grid[0] * grid[1])
    out = pl.pallas_call(
        body, name=name, grid=grid, in_specs=in_specs + c_in,
        out_specs=[pl.BlockSpec((tm, tn), lambda i, j: (i, j))] + c_out,
        out_shape=[jax.ShapeDtypeStruct((M, N), out_dtype)] + c_shape, scratch_shapes=c_scratch,
        compiler_params=_cp(*(("arbitrary",) * 2 if comm else ("parallel",) * 2)))(*args, *c_args)
    return (out[0], out[1:]) if comm else out[0]


def mm_nt_terms(terms, b, *, out_dtype, name, comm=None):
    M = terms[0][0].shape[0]
    N = b.shape[0]
    n_terms = len(terms)
    if n_terms == 1:
        tm, tn = _pick(M, 2048, 16), _pick(N, 1024, LANE)
    else:
        tm, tn = _pick(M, 256, 16), _pick(N, 1024, LANE)

    def body(*refs):
        o_ref = refs[-1]
        acc = None
        for t in range(n_terms):
            part = _dot(refs[2 * t][...], refs[2 * t + 1][...], NT)
            acc = part if acc is None else acc + part
        o_ref[...] = acc.astype(out_dtype)

    in_specs, args = [], []
    for arr, cb, w, off in terms:
        assert off % w == 0
        in_specs.append(pl.BlockSpec((tm, w), lambda i, j, cb=cb: (i, cb)))
        in_specs.append(pl.BlockSpec((tn, w), lambda i, j, ob=off // w: (j, ob)))
        args += [arr, b]
    grid = (M // tm, N // tn)
    body, c_in, c_args, c_out, c_shape, c_scratch = _hosted(body, len(args), 1, comm, _grid_step(grid[1]),
                                                            grid[0] * grid[1])
    out = pl.pallas_call(
        body, name=name, grid=grid, in_specs=in_specs + c_in,
        out_specs=[pl.BlockSpec((tm, tn), lambda i, j: (i, j))] + c_out,
        out_shape=[jax.ShapeDtypeStruct((M, N), out_dtype)] + c_shape, scratch_shapes=c_scratch,
        compiler_params=_cp(*(("arbitrary",) * 2 if comm else ("parallel",) * 2)))(*args, *c_args)
    return (out[0], out[1:]) if comm else out[0]


def mm_nn_terms(terms, b, *, out_dtype, name):
    M = terms[0][0].shape[0]
    N = b.shape[1]
    tm = _pick(M, 256, 16)
    tn = _pick(N, 1024, LANE)
    n_terms = len(terms)

    def body(*refs):
        o_ref = refs[-1]
        acc = None
        for t in range(n_terms):
            part = _dot(refs[2 * t][...], refs[2 * t + 1][...])
            acc = part if acc is None else acc + part
        o_ref[...] = acc.astype(out_dtype)

    in_specs, args = [], []
    for arr, cb, w, off in terms:
        assert off % w == 0
        in_specs.append(pl.BlockSpec((tm, w), lambda i, j, cb=cb: (i, cb)))
        in_specs.append(pl.BlockSpec((w, tn), lambda i, j, ob=off // w: (ob, j)))
        args += [arr, b]
    return pl.pallas_call(
        body, name=name, grid=(M // tm, N // tn), in_specs=in_specs,
        out_specs=pl.BlockSpec((tm, tn), lambda i, j: (i, j)),
        out_shape=jax.ShapeDtypeStruct((M, N), out_dtype),
        compiler_params=_cp("parallel", "parallel"))(*args)


def mm_tn(a, b, *, out_dtype, name):
    T, M = a.shape
    N = b.shape[1]
    tm = _pick(M, 1024, LANE)
    tn = _pick(N, 1024, LANE)
    tk = _pick(T, 2048, 16)
    nk = T // tk

    def body(a_ref, b_ref, o_ref, acc_ref):
        k = pl.program_id(2)

        @pl.when(k == 0)
        def _():
            acc_ref[...] = jnp.zeros_like(acc_ref)

        acc_ref[...] += _dot(a_ref[...], b_ref[...], TN)

        @pl.when(k == nk - 1)
        def _():
            o_ref[...] = acc_ref[...].astype(out_dtype)

    return pl.pallas_call(
        body, name=name, grid=(M // tm, N // tn, nk),
        in_specs=[pl.BlockSpec((tk, tm), lambda i, j, k: (k, i)), pl.BlockSpec((tk, tn), lambda i, j, k: (k, j))],
        out_specs=pl.BlockSpec((tm, tn), lambda i, j, k: (i, j)),
        out_shape=jax.ShapeDtypeStruct((M, N), out_dtype),
        scratch_shapes=[pltpu.VMEM((tm, tn), F32)],
        compiler_params=_cp("parallel", "parallel", "arbitrary"))(a, b)


def rmsnorm_fwd(h, w, *, name, comm=None):
    T, D = h.shape
    tt = _pick(T, 512, 16)

    def body(h_ref, w_ref, n_ref):
        x = h_ref[...]
        r = lax.rsqrt(jnp.mean(x * x, axis=-1, keepdims=True) + EPS)
        n_ref[...] = (x * r * w_ref[...]).astype(BF16)

    body, c_in, c_args, c_out, c_shape, c_scratch = _hosted(body, 2, 1, comm, lambda: pl.program_id(0), T // tt)
    out = pl.pallas_call(
        body, name=name, grid=(T // tt,),
        in_specs=[pl.BlockSpec((tt, D), lambda i: (i, 0)), pl.BlockSpec((1, D), lambda i: (0, 0))] + c_in,
        out_specs=[pl.BlockSpec((tt, D), lambda i: (i, 0))] + c_out,
        out_shape=[jax.ShapeDtypeStruct((T, D), BF16)] + c_shape, scratch_shapes=c_scratch,
        compiler_params=_cp("arbitrary" if comm else "parallel"))(h, w, *c_args)
    return (out[0], out[1:]) if comm else out[0]


def rmsnorm_bwd(h, w, dn, dres, *, name, comm=None):
    T, D = h.shape
    tt = _pick(T, 512, 16)

    def body(h_ref, w_ref, dn_ref, dres_ref, dh_ref, dhb_ref, gw_ref):
        @pl.when(pl.program_id(0) == 0)
        def _():
            gw_ref[...] = jnp.zeros_like(gw_ref)

        x = h_ref[...]
        r = lax.rsqrt(jnp.mean(x * x, axis=-1, keepdims=True) + EPS)
        xhat = x * r
        g = dn_ref[...].astype(F32)
        gw_ref[...] += jnp.sum(g * xhat, axis=0, keepdims=True)
        dxh = g * w_ref[...]
        dx = r * (dxh - xhat * jnp.mean(dxh * xhat, axis=-1, keepdims=True))
        dh = dres_ref[...] + dx
        dh_ref[...] = dh
        dhb_ref[...] = dh.astype(BF16)

    row = pl.BlockSpec((tt, D), lambda i: (i, 0))
    vec = pl.BlockSpec((1, D), lambda i: (0, 0))
    body, c_in, c_args, c_out, c_shape, c_scratch = _hosted(body, 4, 3, comm, lambda: pl.program_id(0), T // tt)
    out = pl.pallas_call(
        body, name=name, grid=(T // tt,), in_specs=[row, vec, row, row] + c_in, out_specs=[row, row, vec] + c_out,
        out_shape=[jax.ShapeDtypeStruct((T, D), F32), jax.ShapeDtypeStruct((T, D), BF16),
                   jax.ShapeDtypeStruct((1, D), F32)] + c_shape,
        scratch_shapes=c_scratch, compiler_params=_cp("arbitrary"))(h, w, dn, dres, *c_args)
    return (out[0], out[1], out[2], out[3:]) if comm else out


def final_loss(h, w, target, *, name):
    T, D = h.shape
    tt = _pick(T, 512, 16)

    def body(h_ref, w_ref, t_ref, loss_ref, dh_ref, dhb_ref, gw_ref):
        @pl.when(pl.program_id(0) == 0)
        def _():
            gw_ref[...] = jnp.zeros_like(gw_ref)
            loss_ref[...] = jnp.zeros_like(loss_ref)

        x = h_ref[...]
        r = lax.rsqrt(jnp.mean(x * x, axis=-1, keepdims=True) + EPS)
        xhat = x * r
        e = xhat * w_ref[...] - t_ref[...]
        loss_ref[...] += jnp.sum(e * e) * (0.5 / D)
        g = e * (1.0 / D)
        gw_ref[...] += jnp.sum(g * xhat, axis=0, keepdims=True)
        dxh = g * w_ref[...]
        dh = r * (dxh - xhat * jnp.mean(dxh * xhat, axis=-1, keepdims=True))
        dh_ref[...] = dh
        dhb_ref[...] = dh.astype(BF16)

    row = pl.BlockSpec((tt, D), lambda i: (i, 0))
    vec = pl.BlockSpec((1, D), lambda i: (0, 0))
    one = pl.BlockSpec((1, LANE), lambda i: (0, 0))
    return pl.pallas_call(
        body, name=name, grid=(T // tt,), in_specs=[row, vec, row], out_specs=[one, row, row, vec],
        out_shape=[jax.ShapeDtypeStruct((1, LANE), F32), jax.ShapeDtypeStruct((T, D), F32),
                   jax.ShapeDtypeStruct((T, D), BF16), jax.ShapeDtypeStruct((1, D), F32)],
        compiler_params=_cp("arbitrary"))(h, w, target)


CONV_CHUNK = 32
ROW_CHUNK = 16
SUBLANES = 8


def _conv_tiles(seq, C, K):
    return _pick(seq, 1024 if K <= SUBLANES else 512, HALO), _pick(C, 512, LANE)


def _residues(offsets):
    return sorted({s % SUBLANES for s in offsets} - {0})


def _fill_shifted(buf, shifted, residues):
    n = buf.shape[0] - SUBLANES
    for i, r in enumerate(residues):
        shifted[i, 0:n, :] = buf[r:r + n, :]


def _tap(buf, shifted, residues, offset, start, rows):
    r = offset % SUBLANES
    base = offset - r
    ref = buf if r == 0 else shifted.at[residues.index(r)]
    return ref[pl.ds(start + base, rows), :]


def dwconv_fwd(src, offs, w, b, *, C, seq, glu, silu_out, name, comm=None):
    T = src.shape[0]
    K = w.shape[0]
    assert K - 1 <= HALO
    tt, tc = _conv_tiles(seq, C, K)
    n_in = 2 if glu else 1
    per = tt // HALO
    offsets = [HALO - (K - 1) + k for k in range(K)]
    residues = _residues(offsets)

    def body(*refs):
        cur = refs[0:2 * n_in:2]
        halo = refs[1:2 * n_in:2]
        w_ref, b_ref = refs[2 * n_in], refs[2 * n_in + 1]
        outs = refs[2 * n_in + 2:-2]
        buf, shifted = refs[-2], refs[-1]
        i = pl.program_id(1)
        first = (i * tt) % seq == 0

        def pre(rs, rows):
            v = rs[0][rows, :].astype(F32)
            return v * _sigmoid(rs[1][rows, :].astype(F32)) if glu else v

        def build(ci, carry):
            start = pl.multiple_of(ci * CONV_CHUNK, CONV_CHUNK)
            buf[pl.ds(HALO + start, CONV_CHUNK), :] = pre(cur, pl.ds(start, CONV_CHUNK))
            return carry

        buf[0:HALO, :] = jnp.where(first, 0.0, pre(halo, slice(None)))
        lax.fori_loop(0, tt // CONV_CHUNK, build, 0, unroll=2)
        _fill_shifted(buf, shifted, residues)

        def chunk(ci, carry):
            start = pl.multiple_of(ci * CONV_CHUNK, CONV_CHUNK)
            acc = jnp.broadcast_to(b_ref[...], (CONV_CHUNK, tc))
            for k in range(K):
                acc = acc + w_ref[k:k + 1, :] * _tap(buf, shifted, residues, offsets[k], start, CONV_CHUNK)
            outs[0][pl.ds(start, CONV_CHUNK), :] = acc.astype(BF16)
            if silu_out:
                outs[1][pl.ds(start, CONV_CHUNK), :] = _silu(acc).astype(BF16)
            return carry

        lax.fori_loop(0, tt // CONV_CHUNK, chunk, 0)

    in_specs, args = [], []
    for off in offs:
        assert off % tc == 0
        in_specs.append(pl.BlockSpec((tt, tc), lambda j, i, ob=off // tc: (i, ob + j)))
        in_specs.append(pl.BlockSpec((HALO, tc), lambda j, i, ob=off // tc: (jnp.maximum(i * per - 1, 0), ob + j)))
        args += [src, src]
    in_specs += [pl.BlockSpec((K, tc), lambda j, i: (0, j)), pl.BlockSpec((1, tc), lambda j, i: (0, j))]
    args += [w, b]
    n_out = 2 if silu_out else 1
    grid = (C // tc, T // tt)
    body, c_in, c_args, c_out, c_shape, c_scratch = _hosted(body, len(args), n_out, comm, _grid_step(grid[1]),
                                                            grid[0] * grid[1])
    out = pl.pallas_call(
        body, name=name, grid=grid, in_specs=in_specs + c_in,
        out_specs=[pl.BlockSpec((tt, tc), lambda j, i: (i, j))] * n_out + c_out,
        out_shape=[jax.ShapeDtypeStruct((T, C), BF16)] * n_out + c_shape,
        scratch_shapes=[pltpu.VMEM((HALO + tt, tc), F32), pltpu.VMEM((max(len(residues), 1), HALO + tt, tc), F32)]
        + c_scratch,
        compiler_params=_cp("arbitrary" if comm else "parallel", "arbitrary"))(*args, *c_args)
    return (out[:n_out], out[n_out:]) if comm else out


def dwconv_bwd(du, u, src, offs, w, *, C, seq, glu, silu_out, name):
    T = src.shape[0]
    K = w.shape[0]
    assert K - 1 <= HALO
    tt, tc = _conv_tiles(seq, C, K)
    n_in = 2 if glu else 1
    per = tt // HALO
    last_blk = T // HALO - 1
    g_offsets = [K - 1 - k for k in range(K)]
    g_res = _residues(g_offsets)

    def body(*refs):
        pos = 0
        du_cur, du_nxt = refs[0], refs[1]
        pos = 2
        if silu_out:
            u_cur, u_nxt = refs[2], refs[3]
            pos = 4
        cur = refs[pos:pos + n_in]
        pos += n_in
        w_ref = refs[pos]
        outs = refs[pos + 1:pos + 1 + n_in]
        dw_ref, db_ref = refs[pos + 1 + n_in], refs[pos + 2 + n_in]
        gbuf, gshift, dw_acc, db_acc = refs[-4:]
        i = pl.program_id(1)
        last = ((i + 1) * tt) % seq == 0

        @pl.when(i == 0)
        def _():
            dw_acc[...] = jnp.zeros_like(dw_acc)
            db_acc[...] = jnp.zeros_like(db_acc)

        def build(ci, carry):
            rows = pl.ds(pl.multiple_of(ci * CONV_CHUNK, CONV_CHUNK), CONV_CHUNK)
            g = du_cur[rows, :].astype(F32)
            if silu_out:
                g = g * _silu_grad(u_cur[rows, :].astype(F32))
            gbuf[rows, :] = g
            return carry

        lax.fori_loop(0, tt // CONV_CHUNK, build, 0, unroll=2)
        g_nxt = du_nxt[...].astype(F32)
        if silu_out:
            g_nxt = g_nxt * _silu_grad(u_nxt[...].astype(F32))
        gbuf[tt:tt + HALO, :] = jnp.where(last, 0.0, g_nxt)
        _fill_shifted(gbuf, gshift, g_res)

        def fold(v):
            out = v[0:SUBLANES]
            for s in range(SUBLANES, CONV_CHUNK, SUBLANES):
                out = out + v[s:s + SUBLANES]
            return out

        def chunk(ci, carry):
            start = pl.multiple_of(ci * CONV_CHUNK, CONV_CHUNK)
            rows = pl.ds(start, CONV_CHUNK)
            a = cur[0][rows, :].astype(F32)
            if glu:
                s = _sigmoid(cur[1][rows, :].astype(F32))
                x_in = a * s
            else:
                x_in = a
            dx = jnp.zeros((CONV_CHUNK, tc), F32)
            for k in range(K):
                g_k = _tap(gbuf, gshift, g_res, g_offsets[k], start, CONV_CHUNK)
                dx = dx + w_ref[k:k + 1, :] * g_k
                dw_acc[k * SUBLANES:(k + 1) * SUBLANES, :] += fold(g_k * x_in)
            db_acc[...] += fold(gbuf[rows, :])
            if glu:
                outs[0][rows, :] = (dx * s).astype(BF16)
                outs[1][rows, :] = (dx * a * s * (1.0 - s)).astype(BF16)
            else:
                outs[0][rows, :] = dx.astype(BF16)
            return carry

        lax.fori_loop(0, tt // CONV_CHUNK, chunk, 0)

        @pl.when(i == T // tt - 1)
        def _():
            for k in range(K):
                dw_ref[k:k + 1, :] = jnp.sum(dw_acc[k * SUBLANES:(k + 1) * SUBLANES, :], axis=0, keepdims=True)
            db_ref[...] = jnp.sum(db_acc[...], axis=0, keepdims=True)

    def cur_spec(ob):
        return pl.BlockSpec((tt, tc), lambda j, i: (i, ob + j))

    def nxt_spec(ob):
        return pl.BlockSpec((HALO, tc), lambda j, i: (jnp.minimum((i + 1) * per, last_blk), ob + j))

    in_specs = [cur_spec(0), nxt_spec(0)]
    args = [du, du]
    if silu_out:
        in_specs += [cur_spec(0), nxt_spec(0)]
        args += [u, u]
    for off in offs:
        assert off % tc == 0
        in_specs.append(cur_spec(off // tc))
        args.append(src)
    in_specs.append(pl.BlockSpec((K, tc), lambda j, i: (0, j)))
    args.append(w)
    out_specs = [pl.BlockSpec((tt, tc), lambda j, i: (i, j))] * n_in
    out_specs += [pl.BlockSpec((K, tc), lambda j, i: (0, j)), pl.BlockSpec((1, tc), lambda j, i: (0, j))]
    out_shape = [jax.ShapeDtypeStruct((T, C), BF16)] * n_in
    out_shape += [jax.ShapeDtypeStruct((K, C), F32), jax.ShapeDtypeStruct((1, C), F32)]
    return pl.pallas_call(
        body, name=name, grid=(C // tc, T // tt), in_specs=in_specs, out_specs=out_specs, out_shape=out_shape,
        scratch_shapes=[pltpu.VMEM((tt + HALO, tc), F32), pltpu.VMEM((max(len(g_res), 1), tt + HALO, tc), F32),
                        pltpu.VMEM((K * SUBLANES, tc), F32), pltpu.VMEM((SUBLANES, tc), F32)],
        compiler_params=_cp("parallel", "arbitrary"))(*args)


def mix0_post_fwd(u2, proj, o, ln_w, ln_b, *, CW, gc_off, ga_off, name):
    T = u2.shape[0]
    tt = _pick(T, 512, 16)

    def body(u_ref, gc_ref, ga_ref, o_ref, lw_ref, lb_ref, y_ref):
        def chunk(ci, carry):
            rows = pl.ds(pl.multiple_of(ci * ROW_CHUNK, ROW_CHUNK), ROW_CHUNK)
            u = u_ref[rows, :].astype(F32)
            mu = jnp.mean(u, axis=-1, keepdims=True)
            xc = u - mu
            r = lax.rsqrt(jnp.mean(xc * xc, axis=-1, keepdims=True) + EPS)
            u3 = xc * r * lw_ref[...] + lb_ref[...]
            y_ref[rows, 0:CW] = (_silu(u3) * _silu(gc_ref[rows, :].astype(F32))).astype(BF16)
            y_ref[rows, CW:2 * CW] = (o_ref[rows, :].astype(F32) * _silu(ga_ref[rows, :].astype(F32))).astype(BF16)
            return carry

        lax.fori_loop(0, tt // ROW_CHUNK, chunk, 0, unroll=4)

    row = pl.BlockSpec((tt, CW), lambda i: (i, 0))
    vec = pl.BlockSpec((1, CW), lambda i: (0, 0))
    return pl.pallas_call(
        body, name=name, grid=(T // tt,),
        in_specs=[row, pl.BlockSpec((tt, CW), lambda i: (i, gc_off // CW)),
                  pl.BlockSpec((tt, CW), lambda i: (i, ga_off // CW)), row, vec, vec],
        out_specs=pl.BlockSpec((tt, 2 * CW), lambda i: (i, 0)),
        out_shape=jax.ShapeDtypeStruct((T, 2 * CW), BF16),
        compiler_params=_cp("parallel"))(u2, proj, proj, o, ln_w, ln_b)


def mix0_post_bwd(dy, u2, proj, o, ln_w, ln_b, *, CW, gc_off, ga_off, name):
    T = u2.shape[0]
    tt = _pick(T, 512, 16)

    def body(dy_ref, u_ref, gc_ref, ga_ref, o_ref, lw_ref, lb_ref, du_ref, dgc_ref, dga_ref, do_ref, dlw_ref, dlb_ref,
             lw_acc, lb_acc):
        i = pl.program_id(0)

        @pl.when(i == 0)
        def _():
            lw_acc[...] = jnp.zeros_like(lw_acc)
            lb_acc[...] = jnp.zeros_like(lb_acc)

        def fold(v):
            out = v[0:SUBLANES]
            for s in range(SUBLANES, ROW_CHUNK, SUBLANES):
                out = out + v[s:s + SUBLANES]
            return out

        def chunk(ci, carry):
            rows = pl.ds(pl.multiple_of(ci * ROW_CHUNK, ROW_CHUNK), ROW_CHUNK)
            dyc = dy_ref[rows, 0:CW].astype(F32)
            dya = dy_ref[rows, CW:2 * CW].astype(F32)
            u = u_ref[rows, :].astype(F32)
            mu = jnp.mean(u, axis=-1, keepdims=True)
            xc = u - mu
            r = lax.rsqrt(jnp.mean(xc * xc, axis=-1, keepdims=True) + EPS)
            xhat = xc * r
            u3 = xhat * lw_ref[...] + lb_ref[...]
            gc = gc_ref[rows, :].astype(F32)
            dgc_ref[rows, :] = (dyc * _silu(u3) * _silu_grad(gc)).astype(BF16)
            du3 = dyc * _silu(gc) * _silu_grad(u3)
            lw_acc[...] += fold(du3 * xhat)
            lb_acc[...] += fold(du3)
            dxh = du3 * lw_ref[...]
            du = r * (dxh - jnp.mean(dxh, axis=-1, keepdims=True)
                      - xhat * jnp.mean(dxh * xhat, axis=-1, keepdims=True))
            du_ref[rows, :] = du.astype(BF16)
            ga = ga_ref[rows, :].astype(F32)
            ov = o_ref[rows, :].astype(F32)
            do_ref[rows, :] = (dya * _silu(ga)).astype(BF16)
            dga_ref[rows, :] = (dya * ov * _silu_grad(ga)).astype(BF16)
            return carry

        lax.fori_loop(0, tt // ROW_CHUNK, chunk, 0, unroll=4)

        @pl.when(i == T // tt - 1)
        def _():
            dlw_ref[...] = jnp.sum(lw_acc[...], axis=0, keepdims=True)
            dlb_ref[...] = jnp.sum(lb_acc[...], axis=0, keepdims=True)

    row = pl.BlockSpec((tt, CW), lambda i: (i, 0))
    vec = pl.BlockSpec((1, CW), lambda i: (0, 0))
    big = jax.ShapeDtypeStruct((T, CW), BF16)
    small = jax.ShapeDtypeStruct((1, CW), F32)
    return pl.pallas_call(
        body, name=name, grid=(T // tt,),
        in_specs=[pl.BlockSpec((tt, 2 * CW), lambda i: (i, 0)), row,
                  pl.BlockSpec((tt, CW), lambda i: (i, gc_off // CW)),
                  pl.BlockSpec((tt, CW), lambda i: (i, ga_off // CW)), row, vec, vec],
        out_specs=[row, row, row, row, vec, vec],
        out_shape=[big, big, big, big, small, small],
        scratch_shapes=[pltpu.VMEM((SUBLANES, CW), F32), pltpu.VMEM((SUBLANES, CW), F32)],
        compiler_params=_cp("arbitrary"))(dy, u2, proj, proj, o, ln_w, ln_b)


SB_UNDERFLOW = 110.0
SB_BOUND_MARGIN = 1.02


def _sb_tile(seq):
    return _pick(seq, 256, LANE)


def _softplus(z):
    return jnp.maximum(z, 0.0) + jnp.log(1.0 + jnp.exp(-jnp.abs(z)))


def _tri01(n, lower):
    i = lax.broadcasted_iota(jnp.int32, (n, n), 0)
    j = lax.broadcasted_iota(jnp.int32, (n, n), 1)
    return ((i >= j) if lower else (i <= j)).astype(BF16)


SB_HEADS_FWD = 8
SB_HEADS_BWD = 2


def _sb_heads_per_step(heads, want):
    while heads % want:
        want //= 2
    return want


def sba_fwd(proj, *, B, seq, heads, q_off, k_off, v_off, name):
    dh = SB_HEAD_DIM
    tq = _sb_tile(seq)
    assert tq % (2 * LANE) == 0
    nq = seq // tq
    hps = _sb_heads_per_step(heads, SB_HEADS_FWD)
    hw = hps * dh
    scale = dh ** -0.5

    def body(q_ref, k_ref, v_ref, tri_ref, o_ref, ct_ref, acc_ref, kmax_ref):
        qi = pl.program_id(1)
        tri = tri_ref[...]
        qs = [(q_ref[:, h * dh:(h + 1) * dh].astype(F32) * scale).astype(BF16) for h in range(hps)]

        @pl.when(qi == 0)
        def _():
            def chunk(i, best):
                rows = k_ref[pl.ds(pl.multiple_of(i * tq, tq), tq), :].astype(F32)
                sq = rows * rows
                return tuple(jnp.maximum(best[h], jnp.max(jnp.sum(sq[:, h * dh:(h + 1) * dh], axis=1, keepdims=True),
                                                          axis=0, keepdims=True)) for h in range(hps))

            best = lax.fori_loop(0, nq, chunk, (jnp.zeros((1, 1), F32),) * hps)
            for h in range(hps):
                kmax_ref[h] = jnp.broadcast_to(jnp.sqrt(best[h]), (8, LANE))

        z_bound = [jnp.sqrt(jnp.sum(qs[h].astype(F32) ** 2, axis=1, keepdims=True))
                   * (SB_BOUND_MARGIN * jnp.max(kmax_ref[h], keepdims=True)) for h in range(hps)]

        def part(h, q_rows, start, n_keys, r, mask):
            k_blk = k_ref[pl.ds(start, n_keys), h * dh:(h + 1) * dh]
            v_blk = v_ref[pl.ds(start, n_keys), h * dh:(h + 1) * dh]
            z = _dot(q_rows, k_blk, NT)
            sp = _softplus(z)
            if mask is not None:
                sp = jnp.where(mask, sp, 0.0)
            wts = jnp.exp(z - (_dot(sp.astype(BF16), tri[0:n_keys, 0:n_keys]) + r))
            if mask is not None:
                wts = jnp.where(mask, wts, 0.0)
            return _dot(wts.astype(BF16), v_blk), r + jnp.sum(sp, axis=-1, keepdims=True)

        below = lax.broadcasted_iota(jnp.int32, (tq, tq), 1) < lax.broadcasted_iota(jnp.int32, (tq, tq), 0)
        has_left = qi > 0
        left = pl.multiple_of(jnp.maximum(qi - 1, 0) * tq, tq)
        rs = []
        for h in range(hps):
            pv_d, r = part(h, qs[h], pl.multiple_of(qi * tq, tq), tq, jnp.zeros((tq, 1), F32), below)
            pv_l, r = part(h, qs[h], left, tq, r, has_left)
            acc_ref[:, h * dh:(h + 1) * dh] = pv_d + pv_l
            rs.append(r)
        rs = tuple(rs)

        def block(start, rs):
            pvs, out = [], []
            for h in range(hps):
                pv, r = part(h, qs[h], start, tq, rs[h], None)
                pvs.append(pv)
                out.append(r)
            return pvs, tuple(out)

        def more(c):
            j, rs = c
            slack = rs[0] - z_bound[0]
            for h in range(1, hps):
                slack = jnp.minimum(slack, rs[h] - z_bound[h])
            return jnp.logical_and(j < qi, jnp.min(slack) <= SB_UNDERFLOW)

        def step(c):
            j, rs = c
            pvs, rs = block(pl.multiple_of((qi - 1 - j) * tq, tq), rs)
            for h in range(hps):
                acc_ref[:, h * dh:(h + 1) * dh] += pvs[h]
            return j + 1, rs

        n_left, totals = lax.while_loop(more, step, (has_left.astype(jnp.int32), rs))
        o_ref[...] = acc_ref[...].astype(BF16)
        for h in range(hps):
            ct_ref[0, 0, h, 0:8, :] = jnp.broadcast_to(totals[h], (tq, LANE)).T[0:8, :]
            ct_ref[0, 0, h, 8:16, :] = jnp.full((8, tq), n_left, F32)

    qb, kb, vb = q_off // hw, k_off // hw, v_off // hw
    G = heads // hps
    return pl.pallas_call(
        body, name=name, grid=(B * G, nq),
        in_specs=[pl.BlockSpec((tq, hw), lambda g, i: ((g // G) * nq + i, qb + g % G)),
                  pl.BlockSpec((seq, hw), lambda g, i: (g // G, kb + g % G)),
                  pl.BlockSpec((seq, hw), lambda g, i: (g // G, vb + g % G)),
                  pl.BlockSpec((tq, tq), lambda g, i: (0, 0))],
        out_specs=[pl.BlockSpec((tq, hw), lambda g, i: ((g // G) * nq + i, g % G)),
                   pl.BlockSpec((1, 1, hps, 16, tq), lambda g, i: (g // G, i, g % G, 0, 0))],
        out_shape=[jax.ShapeDtypeStruct((B * seq, heads * dh), BF16),
                   jax.ShapeDtypeStruct((B, nq, heads, 16, tq), F32)],
        scratch_shapes=[pltpu.VMEM((tq, hw), F32), pltpu.VMEM((hps, 8, LANE), F32)],
        compiler_params=_cp("parallel", "arbitrary"))(proj, proj, proj, jnp.tril(jnp.ones((tq, tq), BF16)))


def sba_bwd(proj, ctot, do, *, B, seq, heads, q_off, k_off, v_off, name, comm=None):
    dh = SB_HEAD_DIM
    tq = _sb_tile(seq)
    nq = seq // tq
    hps = _sb_heads_per_step(heads, SB_HEADS_BWD)
    hw = hps * dh
    scale = dh ** -0.5

    def body(q_ref, k_ref, v_ref, ct_ref, do_ref, sfx_ref, pre_ref, dq_ref, dk_ref, dv_ref, dq_acc, dk_acc, dv_acc):
        qi = pl.program_id(1)

        @pl.when(qi == 0)
        def _():
            dk_acc[...] = jnp.zeros_like(dk_acc)
            dv_acc[...] = jnp.zeros_like(dv_acc)

        tri_sfx = sfx_ref[...]
        tri_pre = pre_ref[...]
        qs = [(q_ref[:, h * dh:(h + 1) * dh].astype(F32) * scale).astype(BF16) for h in range(hps)]
        dos = [do_ref[:, h * dh:(h + 1) * dh] for h in range(hps)]
        totals = [jnp.max(jnp.broadcast_to(ct_ref[0, 0, h, 0:1, :], (LANE, tq)).T, axis=1, keepdims=True)
                  for h in range(hps)]
        dq_acc[...] = jnp.zeros_like(dq_acc)

        def part(h, rows, start, n_keys, pc, pg, mask):
            cols = slice(h * dh, (h + 1) * dh)
            q_rows, do_rows = qs[h][rows], dos[h][rows]
            k_blk = k_ref[pl.ds(start, n_keys), cols]
            v_blk = v_ref[pl.ds(start, n_keys), cols]
            z = _dot(q_rows, k_blk, NT)
            sp = _softplus(z)
            sig = jnp.exp(z - sp)
            if mask is not None:
                sp = jnp.where(mask, sp, 0.0)
            pc_next = pc + jnp.sum(sp, axis=-1, keepdims=True)
            wts = jnp.exp(z - (_dot(sp.astype(BF16), tri_sfx[0:n_keys, 0:n_keys]) + (totals[h][rows] - pc_next)))
            if mask is not None:
                wts = jnp.where(mask, wts, 0.0)
            g = _dot(do_rows, v_blk, NT) * wts
            dz = g - sig * (_dot(g.astype(BF16), tri_pre[0:n_keys, 0:n_keys]) + pg)
            if mask is not None:
                dz = jnp.where(mask, dz, 0.0)
            dz = dz.astype(BF16)
            dk_acc[pl.ds(start, n_keys), cols] += _dot(dz, q_rows, TN)
            dv_acc[pl.ds(start, n_keys), cols] += _dot(wts.astype(BF16), do_rows, TN)
            return pc_next, pg + jnp.sum(g, axis=-1, keepdims=True), _dot(dz, k_blk)

        def block(start, carry):
            out = []
            for h in range(hps):
                pc, pg, dq = part(h, slice(0, tq), start, tq, carry[h][0], carry[h][1], None)
                dq_acc[:, h * dh:(h + 1) * dh] += dq
                out.append((pc, pg))
            return tuple(out)

        zero = jnp.zeros((tq, 1), F32)
        n_left = jnp.max(ct_ref[0, 0, 0, 8:16, :]).astype(jnp.int32)
        carry = lax.fori_loop(qi - n_left, qi - 1, lambda j, c: block(pl.multiple_of(j * tq, tq), c),
                              ((zero, zero),) * hps)
        below = lax.broadcasted_iota(jnp.int32, (tq, tq), 1) < lax.broadcasted_iota(jnp.int32, (tq, tq), 0)
        has_left = n_left > 0
        left = pl.multiple_of(jnp.maximum(qi - 1, 0) * tq, tq)
        for h in range(hps):
            cols = slice(h * dh, (h + 1) * dh)
            pc, pg, dq_l = part(h, slice(0, tq), left, tq, carry[h][0], carry[h][1], has_left)
            _, _, dq_d = part(h, slice(0, tq), pl.multiple_of(qi * tq, tq), tq, pc, pg, below)
            dq_ref[:, cols] = ((dq_acc[:, cols] + dq_l + dq_d) * scale).astype(BF16)

        @pl.when(qi == nq - 1)
        def _():
            dk_ref[...] = dk_acc[...].astype(BF16)
            dv_ref[...] = dv_acc[...].astype(BF16)

    qb, kb, vb = q_off // hw, k_off // hw, v_off // hw
    G = heads // hps
    q_spec = pl.BlockSpec((tq, hw), lambda g, i: ((g // G) * nq + i, qb + g % G))
    o_spec = pl.BlockSpec((tq, hw), lambda g, i: ((g // G) * nq + i, g % G))
    kv_out = pl.BlockSpec((seq, hw), lambda g, i: (g // G, g % G))
    shp = jax.ShapeDtypeStruct((B * seq, heads * dh), BF16)
    body, c_in, c_args, c_out, c_shape, c_scratch = _hosted(body, 7, 3, comm, _grid_step(nq), B * G * nq)
    tri_spec = pl.BlockSpec((tq, tq), lambda g, i: (0, 0))
    ones = jnp.ones((tq, tq), BF16)
    out = pl.pallas_call(
        body, name=name, grid=(B * G, nq),
        in_specs=[q_spec,
                  pl.BlockSpec((seq, hw), lambda g, i: (g // G, kb + g % G)),
                  pl.BlockSpec((seq, hw), lambda g, i: (g // G, vb + g % G)),
                  pl.BlockSpec((1, 1, hps, 16, tq), lambda g, i: (g // G, i, g % G, 0, 0)), o_spec,
                  tri_spec, tri_spec] + c_in,
        out_specs=[o_spec, kv_out, kv_out] + c_out, out_shape=[shp, shp, shp] + c_shape,
        scratch_shapes=[pltpu.VMEM((tq, hw), F32), pltpu.VMEM((seq, hw), F32), pltpu.VMEM((seq, hw), F32)]
        + c_scratch,
        compiler_params=_cp("arbitrary" if comm else "parallel", "arbitrary"))(
            proj, proj, proj, ctot, do, jnp.tril(ones), jnp.triu(ones), *c_args)
    return (out[0], out[1], out[2], out[3:]) if comm else out


def _head_expand(n_heads, DI):
    j = jnp.arange(LANE, dtype=jnp.int32)[:, None]
    c = jnp.arange(DI, dtype=jnp.int32)[None, :] // SSM_HEAD_DIM
    return ((j == c) & (j < n_heads)).astype(BF16)


def _split3(x):
    hi = x.astype(BF16)
    r1 = x - hi.astype(F32)
    mid = r1.astype(BF16)
    return hi, mid, (r1 - mid.astype(F32)).astype(BF16)


def dt_fwd(proj, bias, a_log, expand, *, dt_off, name):
    T = proj.shape[0]
    DI = expand.shape[1]
    L = SSM_CHUNK
    tt = _pick(T, 1024, L)

    def body(raw_ref, bias_ref, al_ref, e_ref, dt_ref, cs_ref, dtx_ref, csx_ref):
        x = raw_ref[...].astype(F32) + bias_ref[...]
        dt = _softplus(x)
        dt_ref[...] = dt
        la = dt * (-jnp.exp(al_ref[...]))
        tri = _tri01(L, True)
        for c in range(tt // L):
            cs_ref[c * L:(c + 1) * L, :] = _tri_dot3(tri, la[c * L:(c + 1) * L, :])
        e = e_ref[...]
        dtx_ref[...] = _dot(dt.astype(BF16), e).astype(BF16)
        hi, mid, lo = _split3(cs_ref[...])
        csx_ref[...] = _dot(hi, e) + _dot(mid, e) + _dot(lo, e)

    row = pl.BlockSpec((tt, LANE), lambda i: (i, 0))
    wide = pl.BlockSpec((tt, DI), lambda i: (i, 0))
    vec = pl.BlockSpec((1, LANE), lambda i: (0, 0))
    return pl.pallas_call(
        body, name=name, grid=(T // tt,),
        in_specs=[pl.BlockSpec((tt, LANE), lambda i: (i, dt_off // LANE)), vec, vec,
                  pl.BlockSpec((LANE, DI), lambda i: (0, 0))],
        out_specs=[row, row, wide, wide],
        out_shape=[jax.ShapeDtypeStruct((T, LANE), F32), jax.ShapeDtypeStruct((T, LANE), F32),
                   jax.ShapeDtypeStruct((T, DI), BF16), jax.ShapeDtypeStruct((T, DI), F32)],
        compiler_params=_cp("parallel"))(proj, bias, a_log, expand)


def dt_bwd(ddt_x, dcs_x, dcs_cols, proj, dt, bias, a_log, reduce_t, *, dt_off, n_heads, name):
    T = proj.shape[0]
    DI = reduce_t.shape[0]
    L = SSM_CHUNK
    tt = _pick(T, 1024, L)

    def body(ddtx_ref, dcsx_ref, dcsc_ref, raw_ref, dt_ref, bias_ref, al_ref, r_ref, draw_ref, dbias_ref, dal_ref,
             dla_buf):
        @pl.when(pl.program_id(0) == 0)
        def _():
            dbias_ref[...] = jnp.zeros_like(dbias_ref)
            dal_ref[...] = jnp.zeros_like(dal_ref)

        r = r_ref[...]
        ddt = _dot(ddtx_ref[...], r)
        dx = dcsx_ref[...]
        hi = dx.astype(BF16)
        dcs = _dot(hi, r) + _dot((dx - hi.astype(F32)).astype(BF16), r) + dcsc_ref[...]
        triu = _tri01(L, False)
        for c in range(tt // L):
            dla_buf[c * L:(c + 1) * L, :] = _tri_dot3(triu, dcs[c * L:(c + 1) * L, :])
        dla = dla_buf[...]
        a = -jnp.exp(al_ref[...])
        valid = lax.broadcasted_iota(jnp.int32, (tt, LANE), 1) < n_heads
        dal_ref[...] += jnp.sum(jnp.where(valid, dla * dt_ref[...], 0.0), axis=0, keepdims=True) * a
        x = raw_ref[...].astype(F32) + bias_ref[...]
        draw = jnp.where(valid, (ddt + dla * a) * _sigmoid(x), 0.0)
        dbias_ref[...] += jnp.sum(draw, axis=0, keepdims=True)
        draw_ref[...] = draw.astype(BF16)

    row = pl.BlockSpec((tt, LANE), lambda i: (i, 0))
    wide = pl.BlockSpec((tt, DI), lambda i: (i, 0))
    vec = pl.BlockSpec((1, LANE), lambda i: (0, 0))
    return pl.pallas_call(
        body, name=name, grid=(T // tt,),
        in_specs=[wide, wide, row, pl.BlockSpec((tt, LANE), lambda i: (i, dt_off // LANE)), row, vec, vec,
                  pl.BlockSpec((DI, LANE), lambda i: (0, 0))],
        out_specs=[row, vec, vec],
        out_shape=[jax.ShapeDtypeStruct((T, LANE), BF16), jax.ShapeDtypeStruct((1, LANE), F32),
                   jax.ShapeDtypeStruct((1, LANE), F32)],
        scratch_shapes=[pltpu.VMEM((tt, LANE), F32)],
        compiler_params=_cp("arbitrary"))(ddt_x, dcs_x, dcs_cols, proj, dt, bias, a_log, reduce_t)


def _pair_terms(x_ref, dtx_ref, csx_ref, csr_ref, pair, ppg, lo_half, causal):
    L = SSM_CHUNK
    g, pp = divmod(pair, ppg)
    ra = g * HEAD_ROWS + 2 * pp
    cols = slice(pair * LANE, (pair + 1) * LANE)
    X = x_ref[:, cols].astype(F32)
    dt_p = dtx_ref[:, cols].astype(F32)
    own = csx_ref[:, cols]
    other = pltpu.roll(own, SSM_HEAD_DIM, 1)
    csa_c = jnp.where(lo_half, own, other)
    csb_c = jnp.where(lo_half, other, own)
    La = jnp.exp(jnp.where(causal, csa_c - csr_ref[ra:ra + 1, :], NEG_BIG))
    Lb = jnp.exp(jnp.where(causal, csb_c - csr_ref[ra + 1:ra + 2, :], NEG_BIG))
    last = csx_ref[L - 1:L, cols]
    return g, ra, cols, X, dt_p, La, Lb, jnp.exp(own), jnp.exp(last - own), jnp.exp(last)


def scan_fwd(xbc, dt_x, cs_x, cs_row, d_full, *, B, seq, DI, name):
    L, N, G = SSM_CHUNK, SSM_STATE, SSM_GROUPS
    nc = seq // L
    XW = xbc.shape[1]
    n_pairs = DI // LANE
    ppg = n_pairs // G

    def body(x_ref, dtx_ref, csx_ref, csr_ref, d_ref, y_ref, st_ref, state):
        @pl.when(pl.program_id(0) == 0)
        def _():
            state[...] = jnp.zeros_like(state)

        causal = lax.broadcasted_iota(jnp.int32, (L, L), 0) >= lax.broadcasted_iota(jnp.int32, (L, L), 1)
        lo_half = lax.broadcasted_iota(jnp.int32, (L, LANE), 1) < SSM_HEAD_DIM
        for b in range(B):
            xb, yb = x_ref.at[b], y_ref.at[b]
            cbs = []
            for g in range(G):
                Bc = xb[:, DI + g * N:DI + (g + 1) * N]
                Cc = xb[:, DI + G * N + g * N:DI + G * N + (g + 1) * N]
                cbs.append((Bc, Cc, _dot(Cc, Bc, NT)))
            for pair in range(n_pairs):
                g, _, cols, X, dt_p, La, Lb, ecs, tail, e_last = _pair_terms(
                    xb, dtx_ref.at[b], csx_ref.at[b], csr_ref.at[b], pair, ppg, lo_half, causal)
                Bc, Cc, CB = cbs[g]
                xs = X * dt_p
                xsb = xs.astype(BF16)
                y = jnp.where(lo_half, _dot((CB * La).astype(BF16), xsb), _dot((CB * Lb).astype(BF16), xsb))
                ST = state[b, pair]
                st_ref[b, 0, pair] = ST
                y = y + ecs * _dot(Cc, ST.astype(BF16)) + d_ref[:, cols] * X
                yb[:, cols] = y.astype(BF16)
                state[b, pair] = e_last * ST + _dot(Bc, (xs * tail).astype(BF16), TN)

    wide = pl.BlockSpec((B, L, DI), lambda c: (0, c, 0))
    y, states = pl.pallas_call(
        body, name=name, grid=(nc,),
        in_specs=[pl.BlockSpec((B, L, XW), lambda c: (0, c, 0)), wide, wide,
                  pl.BlockSpec((B, G * HEAD_ROWS, L), lambda c: (0, 0, c)),
                  pl.BlockSpec((1, DI), lambda c: (0, 0))],
        out_specs=[wide, pl.BlockSpec((B, 1, n_pairs, N, LANE), lambda c: (0, c, 0, 0, 0))],
        out_shape=[jax.ShapeDtypeStruct((B, seq, DI), BF16),
                   jax.ShapeDtypeStruct((B, nc, n_pairs, N, LANE), F32)],
        scratch_shapes=[pltpu.VMEM((B, n_pairs, N, LANE), F32)],
        compiler_params=_cp("arbitrary"))(xbc.reshape(B, seq, XW), dt_x.reshape(B, seq, DI),
                                          cs_x.reshape(B, seq, DI), cs_row, d_full)
    return y.reshape(B * seq, DI), states


def scan_bwd(xbc, dt_x, cs_x, cs_row, d_full, states, dy, *, B, seq, DI, name):
    L, N, G = SSM_CHUNK, SSM_STATE, SSM_GROUPS
    nc = seq // L
    XW = xbc.shape[1]
    n_pairs = DI // LANE
    ppg = n_pairs // G
    HR = G * HEAD_ROWS
    inv_p = 1.0 / SSM_HEAD_DIM

    def body(x_ref, dtx_ref, csx_ref, csr_ref, d_ref, st_ref, dy_ref, dx_ref, ddtx_ref, dcsx_ref, dcsr_ref, dd_ref,
             dH):
        @pl.when(pl.program_id(0) == 0)
        def _():
            dH[...] = jnp.zeros_like(dH)
            dd_ref[...] = jnp.zeros_like(dd_ref)

        causal = lax.broadcasted_iota(jnp.int32, (L, L), 0) >= lax.broadcasted_iota(jnp.int32, (L, L), 1)
        lo_half = lax.broadcasted_iota(jnp.int32, (L, LANE), 1) < SSM_HEAD_DIM
        last_row = lax.broadcasted_iota(jnp.int32, (L, LANE), 0) == L - 1
        head_row = lax.broadcasted_iota(jnp.int32, (HR, 1), 0)
        for b in range(B):
            xb, dxb = x_ref.at[b], dx_ref.at[b]
            dcs_rows = jnp.zeros((HR, L), F32)

            for g in range(G):
                Bc = xb[:, DI + g * N:DI + (g + 1) * N]
                Cc = xb[:, DI + G * N + g * N:DI + G * N + (g + 1) * N]
                CB = _dot(Cc, Bc, NT)
                dCB = jnp.zeros((L, L), F32)
                dC = jnp.zeros((L, N), F32)
                dB = jnp.zeros((L, N), F32)
                for pp in range(ppg):
                    pair = g * ppg + pp
                    _, ra, cols, X, dt_p, La, Lb, ecs, tail, e_last = _pair_terms(
                        xb, dtx_ref.at[b], csx_ref.at[b], csr_ref.at[b], pair, ppg, lo_half, causal)
                    xs = X * dt_p
                    xsb = xs.astype(BF16)
                    Ma, Mb = CB * La, CB * Lb
                    dY = dy_ref[b, :, cols].astype(F32)
                    dYb = dY.astype(BF16)
                    dMa = _dot(jnp.where(lo_half, dY, 0.0).astype(BF16), xsb, NT)
                    dMb = _dot(jnp.where(lo_half, 0.0, dY).astype(BF16), xsb, NT)
                    dSa, dSb = dMa * Ma, dMb * Mb
                    dCB = dCB + dMa * La + dMb * Lb
                    dcs = jnp.where(lo_half, jnp.sum(dSa, axis=1, keepdims=True),
                                    jnp.sum(dSb, axis=1, keepdims=True)) * inv_p
                    dcs_rows = dcs_rows - jnp.where(head_row == ra, jnp.sum(dSa, axis=0, keepdims=True), 0.0)
                    dcs_rows = dcs_rows - jnp.where(head_row == ra + 1, jnp.sum(dSb, axis=0, keepdims=True), 0.0)
                    dxs = jnp.where(lo_half, _dot(Ma.astype(BF16), dYb, TN), _dot(Mb.astype(BF16), dYb, TN))
                    ST = st_ref[b, 0, pair]
                    STb = ST.astype(BF16)
                    dYe = (dY * ecs).astype(BF16)
                    dC = dC + _dot(dYe, STb, NT)
                    dSTp = _dot(Cc, dYe, TN)
                    dcs = dcs + dY * (ecs * _dot(Cc, STb))
                    dSTn = dH[b, pair]
                    dSTnb = dSTn.astype(BF16)
                    dSTp = dSTp + e_last * dSTn
                    XBt = _dot(Bc, dSTnb)
                    dxs = dxs + tail * XBt
                    t2 = xs * XBt * tail
                    at_end = (e_last * jnp.sum(dSTn * ST, axis=0, keepdims=True)
                              + jnp.sum(t2, axis=0, keepdims=True))
                    dcs = dcs - t2 + jnp.where(last_row, at_end, 0.0)
                    dB = dB + _dot((xs * tail).astype(BF16), dSTnb, NT)
                    dxb[:, cols] = (dxs * dt_p + d_ref[:, cols] * dY).astype(BF16)
                    ddtx_ref[b, :, cols] = (dxs * X).astype(BF16)
                    dcsx_ref[b, :, cols] = dcs
                    dd_ref[b, :, cols] += jnp.sum(dY * X, axis=0, keepdims=True)
                    dH[b, pair] = dSTp
                dCBb = dCB.astype(BF16)
                dxb[:, DI + g * N:DI + (g + 1) * N] = (dB + _dot(dCBb, Cc, TN)).astype(BF16)
                dxb[:, DI + G * N + g * N:DI + G * N + (g + 1) * N] = (dC + _dot(dCBb, Bc)).astype(BF16)
            dcsr_ref[b] = dcs_rows

    rev = lambda c: (0, nc - 1 - c, 0)
    wide = pl.BlockSpec((B, L, DI), rev)
    xspec = pl.BlockSpec((B, L, XW), rev)
    hrow = pl.BlockSpec((B, HR, L), lambda c: (0, 0, nc - 1 - c))
    dx, ddt_x, dcs_x, dcs_row, dd = pl.pallas_call(
        body, name=name, grid=(nc,),
        in_specs=[xspec, wide, wide, hrow, pl.BlockSpec((1, DI), lambda c: (0, 0)),
                  pl.BlockSpec((B, 1, n_pairs, N, LANE), lambda c: (0, nc - 1 - c, 0, 0, 0)), wide],
        out_specs=[xspec, wide, wide, hrow, pl.BlockSpec((B, 1, DI), lambda c: (0, 0, 0))],
        out_shape=[jax.ShapeDtypeStruct((B, seq, XW), BF16), jax.ShapeDtypeStruct((B, seq, DI), BF16),
                   jax.ShapeDtypeStruct((B, seq, DI), F32), jax.ShapeDtypeStruct((B, HR, seq), F32),
                   jax.ShapeDtypeStruct((B, 1, DI), F32)],
        scratch_shapes=[pltpu.VMEM((B, n_pairs, N, LANE), F32)],
        compiler_params=_cp("arbitrary"))(xbc.reshape(B, seq, XW), dt_x.reshape(B, seq, DI), cs_x.reshape(B, seq, DI),
                                          cs_row, d_full, states, dy.reshape(B, seq, DI))
    return dx.reshape(B * seq, XW), ddt_x.reshape(B * seq, DI), dcs_x.reshape(B * seq, DI), dcs_row, dd


def gnorm_fwd(y, proj, w, *, DI, name):
    T = y.shape[0]
    tt = _pick(T, 512, 16)
    gw = DI // SSM_GROUPS

    def body(y_ref, z_ref, w_ref, o_ref):
        for g in range(SSM_GROUPS):
            sl = slice(g * gw, (g + 1) * gw)
            y2 = y_ref[:, sl].astype(F32) * _silu(z_ref[:, sl].astype(F32))
            r = lax.rsqrt(jnp.mean(y2 * y2, axis=-1, keepdims=True) + EPS)
            o_ref[:, sl] = (y2 * r * w_ref[:, sl]).astype(BF16)

    row = pl.BlockSpec((tt, DI), lambda i: (i, 0))
    return pl.pallas_call(
        body, name=name, grid=(T // tt,),
        in_specs=[row, row, pl.BlockSpec((1, DI), lambda i: (0, 0))], out_specs=row,
        out_shape=jax.ShapeDtypeStruct((T, DI), BF16), compiler_params=_cp("parallel"))(y, proj, w)


def gnorm_bwd(dyn, y, proj, w, *, DI, name):
    T = y.shape[0]
    tt = _pick(T, 512, 16)
    gw = DI // SSM_GROUPS

    def body(dyn_ref, y_ref, z_ref, w_ref, dy_ref, dz_ref, dw_ref):
        @pl.when(pl.program_id(0) == 0)
        def _():
            dw_ref[...] = jnp.zeros_like(dw_ref)

        for g in range(SSM_GROUPS):
            sl = slice(g * gw, (g + 1) * gw)
            yv = y_ref[:, sl].astype(F32)
            z = z_ref[:, sl].astype(F32)
            sz = _silu(z)
            y2 = yv * sz
            r = lax.rsqrt(jnp.mean(y2 * y2, axis=-1, keepdims=True) + EPS)
            xhat = y2 * r
            d = dyn_ref[:, sl].astype(F32)
            dw_ref[:, sl] += jnp.sum(d * xhat, axis=0, keepdims=True)
            dxh = d * w_ref[:, sl]
            dy2 = r * (dxh - xhat * jnp.mean(dxh * xhat, axis=-1, keepdims=True))
            dy_ref[:, sl] = (dy2 * sz).astype(BF16)
            dz_ref[:, sl] = (dy2 * yv * _silu_grad(z)).astype(BF16)

    row = pl.BlockSpec((tt, DI), lambda i: (i, 0))
    vec = pl.BlockSpec((1, DI), lambda i: (0, 0))
    shp = jax.ShapeDtypeStruct((T, DI), BF16)
    return pl.pallas_call(
        body, name=name, grid=(T // tt,), in_specs=[row, row, row, vec], out_specs=[row, row, vec],
        out_shape=[shp, shp, jax.ShapeDtypeStruct((1, DI), F32)],
        compiler_params=_cp("arbitrary"))(dyn, y, proj, w)


N_CHIP = 4
SHARD_ROW_ALIGN = 128


def _comm_out_shapes(srcs, modes):
    return [jax.ShapeDtypeStruct(((N_DEV,) if mode in ("gather", "gather_direct") else ()) + s.shape, s.dtype)
            for s, mode in zip(srcs, modes)]


def _comm_scratch(n):
    return [pltpu.SemaphoreType.DMA((n, N_DEV - 1)), pltpu.SemaphoreType.DMA((n, N_DEV - 1)),
            pltpu.SemaphoreType.DMA((n,))]


def _comm_phases(modes, src_refs, out_refs, send_sems, recv_sems, local_sems):
    x, y, c = lax.axis_index("x"), lax.axis_index("y"), lax.axis_index("c")
    me, sibling = (x, y, c), (x, y, 1 - c)
    chips = [(1 - x, y), (x, 1 - y), (1 - x, 1 - y)]
    relays = [a for a, mode in enumerate(modes) if mode == "gather"]

    def slot(p):
        return 4 * p[0] + 2 * p[1] + p[2]

    def remote(a, k, src, dst, to):
        return pltpu.make_async_remote_copy(src_ref=src, dst_ref=dst, send_sem=send_sems.at[a, k],
                                            recv_sem=recv_sems.at[a, k], device_id=to,
                                            device_id_type=pl.DeviceIdType.MESH)

    def first_copies():
        local, two_way, send_only = [], [], []
        for a, mode in enumerate(modes):
            src, out = src_refs[a], out_refs[a]
            if mode == "sibling":
                two_way.append(remote(a, 0, src, out, sibling))
            elif mode == "chips":
                mine = 2 * x + y
                local.append(pltpu.make_async_copy(src.at[mine], out.at[mine], local_sems.at[a]))
                for j, chip in enumerate(chips):
                    two_way.append(remote(a, 1 + j, src.at[2 * chip[0] + chip[1]], out.at[mine], (*chip, c)))
            elif mode == "gather_direct":
                local.append(pltpu.make_async_copy(src, out.at[slot(me)], local_sems.at[a]))
                for k in range(1, N_DEV):
                    peer = (1 - x if k & 4 else x, 1 - y if k & 2 else y, 1 - c if k & 1 else c)
                    two_way.append(remote(a, k - 1, src, out.at[slot(me)], peer))
            else:
                assert mode == "gather"
                local.append(pltpu.make_async_copy(src, out.at[slot(me)], local_sems.at[a]))
                send_only.append(remote(a, 0, src, out.at[slot(me)], sibling))
                for j, chip in enumerate(chips):
                    send_only.append(remote(a, 1 + j, src, out.at[slot(me)], (*chip, c)))
        return local, two_way, send_only

    def forwards():
        out = []
        for a in relays:
            for j, chip in enumerate(chips):
                landed = out_refs[a].at[slot((*chip, c))]
                out.append((remote(a, 1 + j, landed, landed, me), remote(a, 4 + j, landed, landed, sibling)))
        return out

    def start():
        local, two_way, send_only = first_copies()
        for cp in local + two_way + send_only:
            cp.start()

    def relay():
        for arrival, fwd in forwards():
            arrival.wait_recv()
            fwd.start()

    def finish():
        local, two_way, send_only = first_copies()
        for a in relays:
            blk = out_refs[a].at[slot(sibling)]
            remote(a, 0, blk, blk, me).wait_recv()
            for j, chip in enumerate(chips):
                blk = out_refs[a].at[slot((*chip, 1 - c))]
                remote(a, 4 + j, blk, blk, me).wait_recv()
        for cp in send_only + [fwd for _, fwd in forwards()]:
            cp.wait_send()
        for cp in two_way + local:
            cp.wait()

    return start, relay, finish, bool(relays)


def _hosted(body, n_in, n_out, comm, step, n_steps):
    if comm is None:
        return body, [], [], [], [], []
    srcs, modes = comm
    nc = len(srcs)

    def wrapped(*refs):
        ins, csrc = refs[:n_in], refs[n_in:n_in + nc]
        outs = refs[n_in + nc:n_in + nc + n_out]
        cout = refs[n_in + nc + n_out:n_in + 2 * nc + n_out]
        scratch = refs[n_in + 2 * nc + n_out:len(refs) - 3]
        start, relay, finish, has_relay = _comm_phases(modes, csrc, cout, *refs[len(refs) - 3:])
        s = step()
        pl.when(s == 0)(start)
        body(*ins, *outs, *scratch)
        if has_relay:
            pl.when(s == (2 * n_steps) // 3)(relay)
        pl.when(s == n_steps - 1)(finish)

    any_spec = pl.BlockSpec(memory_space=pl.ANY)
    return wrapped, [any_spec] * nc, list(srcs), [any_spec] * nc, _comm_out_shapes(srcs, modes), _comm_scratch(nc)


def exchange(srcs, modes, *, name):
    n = len(srcs)

    def body(*refs):
        start, relay, finish, has_relay = _comm_phases(modes, refs[:n], refs[n:2 * n], *refs[2 * n:])
        start()
        if has_relay:
            relay()
        finish()

    any_spec = pl.BlockSpec(memory_space=pl.ANY)
    return pl.pallas_call(
        body, name=name, in_specs=[any_spec] * n, out_specs=[any_spec] * n, out_shape=_comm_out_shapes(srcs, modes),
        scratch_shapes=_comm_scratch(n), compiler_params=pltpu.CompilerParams(has_side_effects=True))(*srcs)


def pair_sum(a, b, *, name):
    n, R, C = a.shape
    tr = _pick(n * R, 1024, 16)

    def body(a_ref, b_ref, o_ref):
        o_ref[...] = (a_ref[...].astype(F32) + b_ref[...].astype(F32)).astype(BF16)

    blk = pl.BlockSpec((tr, C), lambda i: (i, 0))
    out = pl.pallas_call(
        body, name=name, grid=(n * R // tr,), in_specs=[blk, blk], out_specs=blk,
        out_shape=jax.ShapeDtypeStruct((n * R, C), BF16),
        compiler_params=_cp("parallel"))(a.reshape(n * R, C), b.reshape(n * R, C))
    return out.reshape(n, R, C)


def sum_slots(recv, *, name):
    _, R, C = recv.shape
    tr = _pick(R, 512, 8)

    def body(r_ref, o_ref):
        acc = r_ref[0].astype(F32)
        for p in range(1, N_DEV):
            acc = acc + r_ref[p].astype(F32)
        o_ref[...] = acc

    return pl.pallas_call(
        body, name=name, grid=(R // tr,),
        in_specs=[pl.BlockSpec((N_DEV, tr, C), lambda i: (0, i, 0))],
        out_specs=pl.BlockSpec((tr, C), lambda i: (i, 0)),
        out_shape=jax.ShapeDtypeStruct((R, C), F32), compiler_params=_cp("parallel"))(recv)


def adamw(gsrc, w, m, v, *, name):
    slots, R, C = gsrc.shape
    tr = _pick(R, 256, 16 if gsrc.dtype == BF16 else 8)
    c1 = 1.0 / (1.0 - ADAM_B1 ** ADAM_STEP)
    c2 = 1.0 / (1.0 - ADAM_B2 ** ADAM_STEP)

    def body(g_ref, w_ref, m_ref, v_ref, go_ref, d_ref, mo_ref, vo_ref):
        g = g_ref[0].astype(F32)
        for p in range(1, slots):
            g = g + g_ref[p].astype(F32)
        m2 = ADAM_B1 * m_ref[...] + (1.0 - ADAM_B1) * g
        v2 = ADAM_B2 * v_ref[...] + (1.0 - ADAM_B2) * (g * g)
        go_ref[...] = g
        mo_ref[...] = m2
        vo_ref[...] = v2
        d_ref[...] = -ADAM_LR * ((m2 * c1) / (jnp.sqrt(v2 * c2) + ADAM_EPS) + ADAM_WD * w_ref[...])

    blk = pl.BlockSpec((tr, C), lambda i: (i, 0))
    shp = jax.ShapeDtypeStruct((R, C), F32)
    return pl.pallas_call(
        body, name=name, grid=(R // tr,),
        in_specs=[pl.BlockSpec((slots, tr, C), lambda i: (0, i, 0)), blk, blk, blk],
        out_specs=[blk] * 4, out_shape=[shp] * 4, compiler_params=_cp("parallel"))(gsrc, w, m, v)


def _pad_cols(a, n):
    return jnp.pad(a, ((0, 0), (0, n - a.shape[1])))


def _to_rows(a, B, seq, H):
    G = SSM_GROUPS
    R = H // G
    t = a[:, :H].reshape(B, seq, G, R).transpose(0, 2, 3, 1)
    t = jnp.pad(t, ((0, 0), (0, 0), (0, HEAD_ROWS - R), (0, 0)))
    return t.reshape(B, G * HEAD_ROWS, seq)


def _from_rows(a, B, seq, H):
    G = SSM_GROUPS
    R = H // G
    t = a.reshape(B, G, HEAD_ROWS, seq)[:, :, :R].transpose(0, 3, 1, 2).reshape(B * seq, H)
    return _pad_cols(t, LANE)


def _chip_split(grads):
    c_idx = lax.axis_index("c")
    keep, give = [], []
    for g in grads:
        by_chip = g.reshape((N_CHIP, 2) + g.shape[1:])
        keep.append(lax.dynamic_index_in_dim(by_chip, c_idx, axis=1, keepdims=False))
        give.append(lax.dynamic_index_in_dim(by_chip, 1 - c_idx, axis=1, keepdims=False))
    return keep, give


def _chip_sums(grads, name):
    keep, give = _chip_split(grads)
    swapped = exchange(give, ["sibling"] * len(give), name="swap_" + name)
    return [pair_sum(k, s, name=f"chip_sum_{name}_{i}") for i, (k, s) in enumerate(zip(keep, swapped))]


def local_step(x, target, loc, od_w_in_t, *, B, seq):
    T, D = x.shape
    CW = D
    heads = CW // SB_HEAD_DIM
    DI = 2 * D
    H = DI // SSM_HEAD_DIM
    XW = DI + 2 * SSM_GROUPS * SSM_STATE
    in_odd = DI + XW + H
    w1_rows = in_odd // N_DEV
    q_off, k_off, v_off, gc_off, ga_off = 3 * CW, 4 * CW, 5 * CW, 2 * CW, 6 * CW
    dt_off = DI + XW

    small_packed, small_spans = _pack_rows([loc[n] for n in SMALL_SHARDED], LANE, 8)
    n0, (g_ev_in, small_all) = rmsnorm_fwd(x, loc["ev_norm_w"], name="l0_norm",
                                           comm=([loc["ev_w_in"].astype(BF16), small_packed],
                                                 ["gather", "gather_direct"]))
    p = {n: loc[n] for n in SMALL}
    for n, a in zip(SMALL_SHARDED, _unpack_rows(small_all, small_spans)):
        p[n] = _col_unshards(a)
    p["ev_w_in"] = _col_unshards(g_ev_in)
    proj0, (g_od_in_t,) = mm_nn(n0, p["ev_w_in"], out_dtype=BF16, name="l0_in_proj",
                                comm=([od_w_in_t.astype(BF16)], ["gather"]))
    w1t = g_od_in_t[:, :w1_rows].reshape(in_odd, D)
    w1t = jnp.pad(w1t, ((0, -(-(in_odd + LANE) // 256) * 256 - in_odd), (0, 0)))
    (u2,), (g_ev_out, g_od_out) = dwconv_fwd(
        proj0, (0, CW), p["ev_dw_w"], p["ev_dw_b"], C=CW, seq=seq, glu=True, silu_out=False, name="l0_conv",
        comm=([loc["ev_w_out"].astype(BF16), loc["od_w_out"].astype(BF16)], ["gather"] * 2))
    p["ev_w_out"] = g_ev_out.reshape(-1, D)
    od_w_out = g_od_out.reshape(-1, D)
    o, ctot = sba_fwd(proj0, B=B, seq=seq, heads=heads, q_off=q_off, k_off=k_off, v_off=v_off, name="l0_attn")
    ycat = mix0_post_fwd(u2, proj0, o, p["ev_ln_w"], p["ev_ln_b"], CW=CW, gc_off=gc_off, ga_off=ga_off,
                         name="l0_post")
    h1 = mm_nn(ycat, p["ev_w_out"], add=x, out_dtype=F32, name="l0_out_proj")

    n1 = rmsnorm_fwd(h1, p["od_norm_w"], name="l1_norm")
    proj1 = mm_nt_terms([(n1, 0, D, 0)], w1t, out_dtype=BF16, name="l1_in_proj")
    u_pre, xbc = dwconv_fwd(proj1, (DI,), p["od_conv_w"], p["od_conv_b"], C=XW, seq=seq, glu=False, silu_out=True,
                            name="l1_conv")
    bias_p, alog_p = _pad_cols(p["od_dt_bias"], LANE), _pad_cols(p["od_a_log"], LANE)
    expand = _head_expand(H, DI)
    dt, cs, dt_x, cs_x = dt_fwd(proj1, bias_p, alog_p, expand, dt_off=dt_off, name="l1_dt")
    cs_row = _to_rows(cs, B, seq, H)
    d_full = jnp.repeat(p["od_d"], SSM_HEAD_DIM, axis=1)
    y_ssd, states = scan_fwd(xbc, dt_x, cs_x, cs_row, d_full, B=B, seq=seq, DI=DI, name="l1_ssd")
    yn = gnorm_fwd(y_ssd, proj1, p["od_gnorm_w"], DI=DI, name="l1_gnorm")
    h2 = mm_nn(yn, od_w_out, add=h1, out_dtype=F32, name="l1_out_proj")

    loss, dh2, dh2b, g_final = final_loss(h2, p["final_norm_w"], target, name="loss_head")

    g_od_w_out = mm_tn(yn, dh2b, out_dtype=BF16, name="l1_dw_out")
    dyn = mm_nt_terms([(dh2b, 0, D, 0)], od_w_out, out_dtype=BF16, name="l1_d_out_proj")
    dy_ssd, dz, g_gnorm = gnorm_bwd(dyn, y_ssd, proj1, p["od_gnorm_w"], DI=DI, name="l1_gnorm_bwd")
    dxbc_c, ddt_x, dcs_x, dcs_row, dd_part = scan_bwd(xbc, dt_x, cs_x, cs_row, d_full, states, dy_ssd, B=B, seq=seq,
                                                      DI=DI, name="l1_ssd_bwd")
    g_d = dd_part.sum(axis=(0, 1)).reshape(H, SSM_HEAD_DIM).sum(axis=1)[None, :]
    draw, g_bias, g_alog = dt_bwd(ddt_x, dcs_x, _from_rows(dcs_row, B, seq, H), proj1, dt, bias_p, alog_p, expand.T,
                                  dt_off=dt_off, n_heads=H, name="l1_dt_bwd")
    dxbc, g_conv_w, g_conv_b = dwconv_bwd(dxbc_c, u_pre, proj1, (DI,), p["od_conv_w"], C=XW, seq=seq, glu=False,
                                          silu_out=True, name="l1_conv_bwd")
    tw = 512 if DI % 512 == 0 else LANE
    terms = [(dz, j, tw, j * tw) for j in range(DI // tw)]
    terms += [(dxbc, j, tw, DI + j * tw) for j in range(XW // tw)]
    terms += [(draw, 0, LANE, dt_off)]
    dn1 = mm_nn_terms(terms, w1t, out_dtype=BF16, name="l1_d_in_proj")
    g_od_w_in_t = jnp.concatenate([mm_tn(dz, n1, out_dtype=BF16, name="l1_dw_in_z"),
                                   mm_tn(dxbc, n1, out_dtype=BF16, name="l1_dw_in_xbc"),
                                   mm_tn(draw, n1, out_dtype=BF16, name="l1_dw_in_dt")], axis=0)[:in_odd]
    w1_pad = (-w1_rows) % SHARD_ROW_ALIGN
    keep, give = _chip_split([jnp.pad(g_od_w_in_t.reshape(N_DEV, w1_rows, D), ((0, 0), (0, w1_pad), (0, 0))),
                              g_od_w_out.reshape(N_DEV, -1, D)])
    dh1, dh1b, g_od_norm, swapped = rmsnorm_bwd(h1, p["od_norm_w"], dn1, dh2, name="l1_norm_bwd",
                                                comm=(give, ["sibling"] * 2))
    l1_chip = [pair_sum(k, s, name=f"chip_sum_l1_{i}") for i, (k, s) in enumerate(zip(keep, swapped))]

    g_ev_w_out = mm_tn(ycat, dh1b, out_dtype=BF16, name="l0_dw_out")
    dycat = mm_nt_terms([(dh1b, 0, D, 0)], p["ev_w_out"], out_dtype=BF16, name="l0_d_out_proj")
    du2, dgc, dga, do, g_ln_w, g_ln_b = mix0_post_bwd(dycat, u2, proj0, o, p["ev_ln_w"], p["ev_ln_b"], CW=CW,
                                                      gc_off=gc_off, ga_off=ga_off, name="l0_post_bwd")
    dq, dk, dv, (r_od_in_t, r_od_out) = sba_bwd(proj0, ctot, do, B=B, seq=seq, heads=heads, q_off=q_off, k_off=k_off,
                                                v_off=v_off, name="l0_attn_bwd", comm=(l1_chip, ["chips", "chips"]))
    dga_a, dga_b, g_dw_w, g_dw_b = dwconv_bwd(du2, None, proj0, (0, CW), p["ev_dw_w"], C=CW, seq=seq, glu=True,
                                              silu_out=False, name="l0_conv_bwd")
    pieces = [dga_a, dga_b, dgc, dq, dk, dv, dga]
    g_ev_w_in = jnp.concatenate([mm_tn(n0, pc, out_dtype=BF16, name=f"l0_dw_in_{j}") for j, pc in enumerate(pieces)],
                                axis=1)
    l0_chip = _chip_sums([_col_shards(g_ev_w_in), g_ev_w_out.reshape(N_DEV, -1, D)], "l0")
    dn0, (r_ev_in, r_ev_out) = mm_nt_terms([(pc, 0, CW, j * CW) for j, pc in enumerate(pieces)], p["ev_w_in"],
                                           out_dtype=BF16, name="l0_d_in_proj", comm=(l0_chip, ["chips", "chips"]))
    dx, _, g_ev_norm = rmsnorm_bwd(x, p["ev_norm_w"], dn0, dh1, name="l0_norm_bwd")

    small = dict(ev_norm_w=g_ev_norm, ev_dw_w=g_dw_w, ev_dw_b=g_dw_b, ev_ln_w=g_ln_w, ev_ln_b=g_ln_b,
                 od_norm_w=g_od_norm, od_conv_w=g_conv_w, od_conv_b=g_conv_b, od_dt_bias=g_bias[:, :H],
                 od_a_log=g_alog[:, :H], od_d=g_d, od_gnorm_w=g_gnorm, final_norm_w=g_final)
    received = dict(ev_w_in=r_ev_in, ev_w_out=r_ev_out, od_w_in=r_od_in_t, od_w_out=r_od_out)
    return loss, dx, small, received


BIG = ("ev_w_in", "ev_w_out", "od_w_in", "od_w_out")
SMALL = ("ev_norm_w", "ev_dw_w", "ev_dw_b", "ev_ln_w", "ev_ln_b", "od_norm_w", "od_conv_w", "od_conv_b",
         "od_dt_bias", "od_a_log", "od_d", "od_gnorm_w", "final_norm_w")
SMALL_SHARDED = ("ev_dw_w", "od_norm_w", "od_conv_w", "od_conv_b", "od_gnorm_w")
ORDER = ("ev_norm_w", "ev_w_in", "ev_dw_w", "ev_dw_b", "ev_ln_w", "ev_ln_b", "ev_w_out", "od_norm_w", "od_w_in",
         "od_conv_w", "od_conv_b", "od_dt_bias", "od_a_log", "od_d", "od_gnorm_w", "od_w_out", "final_norm_w")


def _pack_rows(arrs, width, row_align):
    parts, spans, r0 = [], [], 0
    for a in arrs:
        flat = a.reshape(-1)
        rows = -(-flat.shape[0] // (width * row_align)) * row_align
        parts.append(jnp.pad(flat, (0, rows * width - flat.shape[0])).reshape(rows, width))
        spans.append((r0, a.size, a.shape))
        r0 += rows
    return jnp.concatenate(parts, axis=0), spans


def _unpack_rows(packed, spans):
    lead = packed.shape[:-2]
    width = packed.shape[-1]
    out = []
    for r0, size, shape in spans:
        rows = -(-size // width)
        blk = packed[..., r0:r0 + rows, :].reshape(lead + (rows * width,))[..., :size]
        out.append(blk.reshape(lead + tuple(shape)))
    return out


def _col_shards(a):
    R, C8 = a.shape
    return a.reshape(R, N_DEV, C8 // N_DEV).transpose(1, 0, 2)


def _col_unshards(a):
    n, R, C = a.shape
    return a.transpose(1, 0, 2).reshape(R, n * C)


def kernel(x, ev_norm_w, ev_w_in, ev_dw_w, ev_dw_b, ev_ln_w, ev_ln_b, ev_w_out, od_norm_w, od_w_in, od_conv_w, od_conv_b, od_dt_bias, od_a_log, od_d, od_gnorm_w, od_w_out, final_norm_w, loss_target, m_ev_norm_w, m_ev_w_in, m_ev_dw_w, m_ev_dw_b, m_ev_ln_w, m_ev_ln_b, m_ev_w_out, m_od_norm_w, m_od_w_in, m_od_conv_w, m_od_conv_b, m_od_dt_bias, m_od_a_log, m_od_d, m_od_gnorm_w, m_od_w_out, m_final_norm_w, v_ev_norm_w, v_ev_w_in, v_ev_dw_w, v_ev_dw_b, v_ev_ln_w, v_ev_ln_b, v_ev_w_out, v_od_norm_w, v_od_w_in, v_od_conv_w, v_od_conv_b, v_od_dt_bias, v_od_a_log, v_od_d, v_od_gnorm_w, v_od_w_out, v_final_norm_w):
    loc = dict(ev_norm_w=ev_norm_w, ev_w_in=ev_w_in, ev_dw_w=ev_dw_w, ev_dw_b=ev_dw_b, ev_ln_w=ev_ln_w,
               ev_ln_b=ev_ln_b, ev_w_out=ev_w_out, od_norm_w=od_norm_w, od_w_in=od_w_in, od_conv_w=od_conv_w,
               od_conv_b=od_conv_b, od_dt_bias=od_dt_bias, od_a_log=od_a_log, od_d=od_d, od_gnorm_w=od_gnorm_w,
               od_w_out=od_w_out, final_norm_w=final_norm_w)
    mom = dict(ev_norm_w=m_ev_norm_w, ev_w_in=m_ev_w_in, ev_dw_w=m_ev_dw_w, ev_dw_b=m_ev_dw_b, ev_ln_w=m_ev_ln_w,
               ev_ln_b=m_ev_ln_b, ev_w_out=m_ev_w_out, od_norm_w=m_od_norm_w, od_w_in=m_od_w_in,
               od_conv_w=m_od_conv_w, od_conv_b=m_od_conv_b, od_dt_bias=m_od_dt_bias, od_a_log=m_od_a_log,
               od_d=m_od_d, od_gnorm_w=m_od_gnorm_w, od_w_out=m_od_w_out, final_norm_w=m_final_norm_w)
    var = dict(ev_norm_w=v_ev_norm_w, ev_w_in=v_ev_w_in, ev_dw_w=v_ev_dw_w, ev_dw_b=v_ev_dw_b, ev_ln_w=v_ev_ln_w,
               ev_ln_b=v_ev_ln_b, ev_w_out=v_ev_w_out, od_norm_w=v_od_norm_w, od_w_in=v_od_w_in,
               od_conv_w=v_od_conv_w, od_conv_b=v_od_conv_b, od_dt_bias=v_od_dt_bias, od_a_log=v_od_a_log,
               od_d=v_od_d, od_gnorm_w=v_od_gnorm_w, od_w_out=v_od_w_out, final_norm_w=v_final_norm_w)
    shapes = {n: loc[n].shape for n in ORDER}
    loc = {n: (a.reshape(1, -1) if a.ndim == 1 else a.reshape(a.shape[-2:]) if a.ndim == 3 else a)
           for n, a in loc.items()}
    mom = {n: a.reshape(loc[n].shape) for n, a in mom.items()}
    var = {n: a.reshape(loc[n].shape) for n, a in var.items()}

    B, seq, D = x.shape
    me = 4 * lax.axis_index("x") + 2 * lax.axis_index("y") + lax.axis_index("c")

    w1_rows = loc["od_w_in"].shape[1]
    w1_pad = (-w1_rows) % SHARD_ROW_ALIGN

    def to_t(a):
        return jnp.pad(a.T, ((0, w1_pad), (0, 0)))

    loss, dx, grads, received = local_step(x.reshape(B * seq, D), loss_target.reshape(B * seq, D), loc,
                                           to_t(loc["od_w_in"]), B=B, seq=seq)

    gsmall_packed, gsmall_spans = _pack_rows([grads[n] for n in SMALL] + [loss], LANE, 8)
    (gsmall_recv,) = exchange([gsmall_packed], ["gather_direct"], name="gather_small_grads")

    big_out = [{} for _ in range(4)]
    for n in ("ev_w_in", "ev_w_out", "od_w_out"):
        for kind, a in enumerate(adamw(received[n], loc[n], mom[n], var[n], name="adamw_" + n)):
            big_out[kind][n] = a
    for kind, a in enumerate(adamw(received["od_w_in"], to_t(loc["od_w_in"]), to_t(mom["od_w_in"]),
                                   to_t(var["od_w_in"]), name="adamw_od_w_in")):
        big_out[kind]["od_w_in"] = a[:w1_rows].T

    summed = _unpack_rows(sum_slots(gsmall_recv, name="sum_small_grads"), gsmall_spans)
    loss_total = summed[-1][0, 0]
    gsmall = dict(zip(SMALL, summed[:-1]))
    for n in SMALL_SHARDED:
        width = loc[n].shape[1]
        gsmall[n] = lax.dynamic_slice_in_dim(gsmall[n], me * width, width, axis=1)
    gs, sspans = _pack_rows([gsmall[n] for n in SMALL], LANE, 8)
    ws, _ = _pack_rows([loc[n] for n in SMALL], LANE, 8)
    ms, _ = _pack_rows([mom[n] for n in SMALL], LANE, 8)
    vs, _ = _pack_rows([var[n] for n in SMALL], LANE, 8)
    small_out = [dict(zip(SMALL, _unpack_rows(a, sspans))) for a in adamw(gs[None], ws, ms, vs, name="adamw_small")]

    outs = [loss_total, dx.reshape(B, seq, D)]
    for kind in range(4):
        for n in ORDER:
            src = big_out[kind] if n in BIG else small_out[kind]
            outs.append(src[n].reshape(shapes[n]))
    return tuple(outs)
```

```python
import jax
import jax.numpy as jnp
from jax import lax
from jax.experimental import pallas as pl
from jax.experimental.pallas import tpu as pltpu

F32 = jnp.float32
BF16 = jnp.bfloat16

EPS = 1e-6
N_DEV = 8
LANE = 128
VMEM_LIMIT_BYTES = 48 * 1024 * 1024

SB_HEAD_DIM = 128
SSM_HEAD_DIM = 64
SSM_GROUPS = 4
SSM_STATE = 128
SSM_CHUNK = 128
HALO = 32
HEAD_ROWS = 8
NEG_BIG = -1e30

ADAM_LR = 0.001
ADAM_B1 = 0.9
ADAM_B2 = 0.999
ADAM_EPS = 1e-08
ADAM_WD = 0.01
ADAM_STEP = 10

NT = (((1,), (1,)), ((), ()))
TN = (((0,), (0,)), ((), ()))


def _cp(*sem):
    return pltpu.CompilerParams(dimension_semantics=sem, vmem_limit_bytes=VMEM_LIMIT_BYTES)


def _pick(n, cap, align):
    if n <= cap:
        return n
    t = (cap // align) * align
    while t >= align:
        if n % t == 0:
            return t
        t -= align
    raise ValueError(f"no tile for {n} (cap {cap}, align {align})")


def _sigmoid(x):
    return 0.5 * jnp.tanh(0.5 * x) + 0.5


def _silu(x):
    return x * _sigmoid(x)


def _silu_grad(x):
    s = _sigmoid(x)
    return s * (1.0 + x * (1.0 - s))


def _dot(a, b, dims=None):
    if dims is None:
        return jnp.dot(a, b, preferred_element_type=F32)
    return lax.dot_general(a, b, dims, preferred_element_type=F32)


def _tri_dot3(tri, x):
    hi = x.astype(BF16)
    r1 = x - hi.astype(F32)
    mid = r1.astype(BF16)
    lo = (r1 - mid.astype(F32)).astype(BF16)
    return _dot(tri, hi) + _dot(tri, mid) + _dot(tri, lo)


def _grid_step(n_inner):
    return lambda: pl.program_id(0) * n_inner + pl.program_id(1)


def mm_nn(a, b, *, add=None, out_dtype, name, comm=None):
    M, K = a.shape
    N = b.shape[1]
    tm = _pick(M, 2048 if K <= 1024 and add is None else 1024, 16)
    tn = _pick(N, 1024, LANE)

    def body(*refs):
        if add is None:
            a_ref, b_ref, o_ref = refs
        else:
            a_ref, b_ref, add_ref, o_ref = refs
        acc = _dot(a_ref[...], b_ref[...])
        if add is not None:
            acc = acc + add_ref[...]
        o_ref[...] = acc.astype(out_dtype)

    in_specs = [pl.BlockSpec((tm, K), lambda i, j: (i, 0)), pl.BlockSpec((K, tn), lambda i, j: (0, j))]
    args = [a, b]
    if add is not None:
        in_specs.append(pl.BlockSpec((tm, tn), lambda i, j: (i, j)))
        args.append(add)
    grid = (M // tm, N // tn)
    body, c_in, c_args, c_out, c_shape, c_scratch = _hosted(body, len(args), 1, comm, _grid_step(grid[1]),
                                                            grid[0] * grid[1])
    out = pl.pallas_call(
        body, name=name, grid=grid, in_specs=in_specs + c_in,
        out_specs=[pl.BlockSpec((tm, tn), lambda i, j: (i, j))] + c_out,
        out_shape=[jax.ShapeDtypeStruct((M, N), out_dtype)] + c_shape, scratch_shapes=c_scratch,
        compiler_params=_cp(*(("arbitrary",) * 2 if comm else ("parallel",) * 2)))(*args, *c_args)
    return (out[0], out[1:]) if comm else out[0]


def mm_nt_terms(terms, b, *, out_dtype, name, comm=None):
    M = terms[0][0].shape[0]
    N = b.shape[0]
    n_terms = len(terms)
    if n_terms == 1:
        tm, tn = _pick(M, 2048, 16), _pick(N, 1024, LANE)
    else:
        tm, tn = _pick(M, 256, 16), _pick(N, 1024, LANE)

    def body(*refs):
        o_ref = refs[-1]
        acc = None
        for t in range(n_terms):
            part = _dot(refs[2 * t][...], refs[2 * t + 1][...], NT)
            acc = part if acc is None else acc + part
        o_ref[...] = acc.astype(out_dtype)

    in_specs, args = [], []
    for arr, cb, w, off in terms:
        assert off % w == 0
        in_specs.append(pl.BlockSpec((tm, w), lambda i, j, cb=cb: (i, cb)))
        in_specs.append(pl.BlockSpec((tn, w), lambda i, j, ob=off // w: (j, ob)))
        args += [arr, b]
    grid = (M // tm, N // tn)
    body, c_in, c_args, c_out, c_shape, c_scratch = _hosted(body, len(args), 1, comm, _grid_step(grid[1]),
                                                            grid[0] * grid[1])
    out = pl.pallas_call(
        body, name=name, grid=grid, in_specs=in_specs + c_in,
        out_specs=[pl.BlockSpec((tm, tn), lambda i, j: (i, j))] + c_out,
        out_shape=[jax.ShapeDtypeStruct((M, N), out_dtype)] + c_shape, scratch_shapes=c_scratch,
        compiler_params=_cp(*(("arbitrary",) * 2 if comm else ("parallel",) * 2)))(*args, *c_args)
    return (out[0], out[1:]) if comm else out[0]


def mm_nn_terms(terms, b, *, out_dtype, name):
    M = terms[0][0].shape[0]
    N = b.shape[1]
    tm = _pick(M, 256, 16)
    tn = _pick(N, 1024, LANE)
    n_terms = len(terms)

    def body(*refs):
        o_ref = refs[-1]
        acc = None
        for t in range(n_terms):
            part = _dot(refs[2 * t][...], refs[2 * t + 1][...])
            acc = part if acc is None else acc + part
        o_ref[...] = acc.astype(out_dtype)

    in_specs, args = [], []
    for arr, cb, w, off in terms:
        assert off % w == 0
        in_specs.append(pl.BlockSpec((tm, w), lambda i, j, cb=cb: (i, cb)))
        in_specs.append(pl.BlockSpec((w, tn), lambda i, j, ob=off // w: (ob, j)))
        args += [arr, b]
    return pl.pallas_call(
        body, name=name, grid=(M // tm, N // tn), in_specs=in_specs,
        out_specs=pl.BlockSpec((tm, tn), lambda i, j: (i, j)),
        out_shape=jax.ShapeDtypeStruct((M, N), out_dtype),
        compiler_params=_cp("parallel", "parallel"))(*args)


def mm_tn(a, b, *, out_dtype, name):
    T, M = a.shape
    N = b.shape[1]
    tm = _pick(M, 1024, LANE)
    tn = _pick(N, 1024, LANE)
    tk = _pick(T, 2048, 16)
    nk = T // tk

    def body(a_ref, b_ref, o_ref, acc_ref):
        k = pl.program_id(2)

        @pl.when(k == 0)
        def _():
            acc_ref[...] = jnp.zeros_like(acc_ref)

        acc_ref[...] += _dot(a_ref[...], b_ref[...], TN)

        @pl.when(k == nk - 1)
        def _():
            o_ref[...] = acc_ref[...].astype(out_dtype)

    return pl.pallas_call(
        body, name=name, grid=(M // tm, N // tn, nk),
        in_specs=[pl.BlockSpec((tk, tm), lambda i, j, k: (k, i)), pl.BlockSpec((tk, tn), lambda i, j, k: (k, j))],
        out_specs=pl.BlockSpec((tm, tn), lambda i, j, k: (i, j)),
        out_shape=jax.ShapeDtypeStruct((M, N), out_dtype),
        scratch_shapes=[pltpu.VMEM((tm, tn), F32)],
        compiler_params=_cp("parallel", "parallel", "arbitrary"))(a, b)


def rmsnorm_fwd(h, w, *, name, comm=None):
    T, D = h.shape
    tt = _pick(T, 512, 16)

    def body(h_ref, w_ref, n_ref):
        x = h_ref[...]
        r = lax.rsqrt(jnp.mean(x * x, axis=-1, keepdims=True) + EPS)
        n_ref[...] = (x * r * w_ref[...]).astype(BF16)

    body, c_in, c_args, c_out, c_shape, c_scratch = _hosted(body, 2, 1, comm, lambda: pl.program_id(0), T // tt)
    out = pl.pallas_call(
        body, name=name, grid=(T // tt,),
        in_specs=[pl.BlockSpec((tt, D), lambda i: (i, 0)), pl.BlockSpec((1, D), lambda i: (0, 0))] + c_in,
        out_specs=[pl.BlockSpec((tt, D), lambda i: (i, 0))] + c_out,
        out_shape=[jax.ShapeDtypeStruct((T, D), BF16)] + c_shape, scratch_shapes=c_scratch,
        compiler_params=_cp("arbitrary" if comm else "parallel"))(h, w, *c_args)
    return (out[0], out[1:]) if comm else out[0]


def rmsnorm_bwd(h, w, dn, dres, *, name, comm=None):
    T, D = h.shape
    tt = _pick(T, 512, 16)

    def body(h_ref, w_ref, dn_ref, dres_ref, dh_ref, dhb_ref, gw_ref):
        @pl.when(pl.program_id(0) == 0)
        def _():
            gw_ref[...] = jnp.zeros_like(gw_ref)

        x = h_ref[...]
        r = lax.rsqrt(jnp.mean(x * x, axis=-1, keepdims=True) + EPS)
        xhat = x * r
        g = dn_ref[...].astype(F32)
        gw_ref[...] += jnp.sum(g * xhat, axis=0, keepdims=True)
        dxh = g * w_ref[...]
        dx = r * (dxh - xhat * jnp.mean(dxh * xhat, axis=-1, keepdims=True))
        dh = dres_ref[...] + dx
        dh_ref[...] = dh
        dhb_ref[...] = dh.astype(BF16)

    row = pl.BlockSpec((tt, D), lambda i: (i, 0))
    vec = pl.BlockSpec((1, D), lambda i: (0, 0))
    body, c_in, c_args, c_out, c_shape, c_scratch = _hosted(body, 4, 3, comm, lambda: pl.program_id(0), T // tt)
    out = pl.pallas_call(
        body, name=name, grid=(T // tt,), in_specs=[row, vec, row, row] + c_in, out_specs=[row, row, vec] + c_out,
        out_shape=[jax.ShapeDtypeStruct((T, D), F32), jax.ShapeDtypeStruct((T, D), BF16),
                   jax.ShapeDtypeStruct((1, D), F32)] + c_shape,
        scratch_shapes=c_scratch, compiler_params=_cp("arbitrary"))(h, w, dn, dres, *c_args)
    return (out[0], out[1], out[2], out[3:]) if comm else out


def final_loss(h, w, target, *, name):
    T, D = h.shape
    tt = _pick(T, 512, 16)

    def body(h_ref, w_ref, t_ref, loss_ref, dh_ref, dhb_ref, gw_ref):
        @pl.when(pl.program_id(0) == 0)
        def _():
            gw_ref[...] = jnp.zeros_like(gw_ref)
            loss_ref[...] = jnp.zeros_like(loss_ref)

        x = h_ref[...]
        r = lax.rsqrt(jnp.mean(x * x, axis=-1, keepdims=True) + EPS)
        xhat = x * r
        e = xhat * w_ref[...] - t_ref[...]
        loss_ref[...] += jnp.sum(e * e) * (0.5 / D)
        g = e * (1.0 / D)
        gw_ref[...] += jnp.sum(g * xhat, axis=0, keepdims=True)
        dxh = g * w_ref[...]
        dh = r * (dxh - xhat * jnp.mean(dxh * xhat, axis=-1, keepdims=True))
        dh_ref[...] = dh
        dhb_ref[...] = dh.astype(BF16)

    row = pl.BlockSpec((tt, D), lambda i: (i, 0))
    vec = pl.BlockSpec((1, D), lambda i: (0, 0))
    one = pl.BlockSpec((1, LANE), lambda i: (0, 0))
    return pl.pallas_call(
        body, name=name, grid=(T // tt,), in_specs=[row, vec, row], out_specs=[one, row, row, vec],
        out_shape=[jax.ShapeDtypeStruct((1, LANE), F32), jax.ShapeDtypeStruct((T, D), F32),
                   jax.ShapeDtypeStruct((T, D), BF16), jax.ShapeDtypeStruct((1, D), F32)],
        compiler_params=_cp("arbitrary"))(h, w, target)


CONV_CHUNK = 32
ROW_CHUNK = 16
SUBLANES = 8


def _conv_tiles(seq, C, K):
    return _pick(seq, 1024 if K <= SUBLANES else 512, HALO), _pick(C, 512, LANE)


def _residues(offsets):
    return sorted({s % SUBLANES for s in offsets} - {0})


def _fill_shifted(buf, shifted, residues):
    n = buf.shape[0] - SUBLANES
    for i, r in enumerate(residues):
        shifted[i, 0:n, :] = buf[r:r + n, :]


def _tap(buf, shifted, residues, offset, start, rows):
    r = offset % SUBLANES
    base = offset - r
    ref = buf if r == 0 else shifted.at[residues.index(r)]
    return ref[pl.ds(start + base, rows), :]


def dwconv_fwd(src, offs, w, b, *, C, seq, glu, silu_out, name, comm=None):
    T = src.shape[0]
    K = w.shape[0]
    assert K - 1 <= HALO
    tt, tc = _conv_tiles(seq, C, K)
    n_in = 2 if glu else 1
    per = tt // HALO
    offsets = [HALO - (K - 1) + k for k in range(K)]
    residues = _residues(offsets)

    def body(*refs):
        cur = refs[0:2 * n_in:2]
        halo = refs[1:2 * n_in:2]
        w_ref, b_ref = refs[2 * n_in], refs[2 * n_in + 1]
        outs = refs[2 * n_in + 2:-2]
        buf, shifted = refs[-2], refs[-1]
        i = pl.program_id(1)
        first = (i * tt) % seq == 0

        def pre(rs, rows):
            v = rs[0][rows, :].astype(F32)
            return v * _sigmoid(rs[1][rows, :].astype(F32)) if glu else v

        def build(ci, carry):
            start = pl.multiple_of(ci * CONV_CHUNK, CONV_CHUNK)
            buf[pl.ds(HALO + start, CONV_CHUNK), :] = pre(cur, pl.ds(start, CONV_CHUNK))
            return carry

        buf[0:HALO, :] = jnp.where(first, 0.0, pre(halo, slice(None)))
        lax.fori_loop(0, tt // CONV_CHUNK, build, 0, unroll=2)
        _fill_shifted(buf, shifted, residues)

        def chunk(ci, carry):
            start = pl.multiple_of(ci * CONV_CHUNK, CONV_CHUNK)
            acc = jnp.broadcast_to(b_ref[...], (CONV_CHUNK, tc))
            for k in range(K):
                acc = acc + w_ref[k:k + 1, :] * _tap(buf, shifted, residues, offsets[k], start, CONV_CHUNK)
            outs[0][pl.ds(start, CONV_CHUNK), :] = acc.astype(BF16)
            if silu_out:
                outs[1][pl.ds(start, CONV_CHUNK), :] = _silu(acc).astype(BF16)
            return carry

        lax.fori_loop(0, tt // CONV_CHUNK, chunk, 0)

    in_specs, args = [], []
    for off in offs:
        assert off % tc == 0
        in_specs.append(pl.BlockSpec((tt, tc), lambda j, i, ob=off // tc: (i, ob + j)))
        in_specs.append(pl.BlockSpec((HALO, tc), lambda j, i, ob=off // tc: (jnp.maximum(i * per - 1, 0), ob + j)))
        args += [src, src]
    in_specs += [pl.BlockSpec((K, tc), lambda j, i: (0, j)), pl.BlockSpec((1, tc), lambda j, i: (0, j))]
    args += [w, b]
    n_out = 2 if silu_out else 1
    grid = (C // tc, T // tt)
    body, c_in, c_args, c_out, c_shape, c_scratch = _hosted(body, len(args), n_out, comm, _grid_step(grid[1]),
                                                            grid[0] * grid[1])
    out = pl.pallas_call(
        body, name=name, grid=grid, in_specs=in_specs + c_in,
        out_specs=[pl.BlockSpec((tt, tc), lambda j, i: (i, j))] * n_out + c_out,
        out_shape=[jax.ShapeDtypeStruct((T, C), BF16)] * n_out + c_shape,
        scratch_shapes=[pltpu.VMEM((HALO + tt, tc), F32), pltpu.VMEM((max(len(residues), 1), HALO + tt, tc), F32)]
        + c_scratch,
        compiler_params=_cp("arbitrary" if comm else "parallel", "arbitrary"))(*args, *c_args)
    return (out[:n_out], out[n_out:]) if comm else out


def dwconv_bwd(du, u, src, offs, w, *, C, seq, glu, silu_out, name):
    T = src.shape[0]
    K = w.shape[0]
    assert K - 1 <= HALO
    tt, tc = _conv_tiles(seq, C, K)
    n_in = 2 if glu else 1
    per = tt // HALO
    last_blk = T // HALO - 1
    g_offsets = [K - 1 - k for k in range(K)]
    g_res = _residues(g_offsets)

    def body(*refs):
        pos = 0
        du_cur, du_nxt = refs[0], refs[1]
        pos = 2
        if silu_out:
            u_cur, u_nxt = refs[2], refs[3]
            pos = 4
        cur = refs[pos:pos + n_in]
        pos += n_in
        w_ref = refs[pos]
        outs = refs[pos + 1:pos + 1 + n_in]
        dw_ref, db_ref = refs[pos + 1 + n_in], refs[pos + 2 + n_in]
        gbuf, gshift, dw_acc, db_acc = refs[-4:]
        i = pl.program_id(1)
        last = ((i + 1) * tt) % seq == 0

        @pl.when(i == 0)
        def _():
            dw_acc[...] = jnp.zeros_like(dw_acc)
            db_acc[...] = jnp.zeros_like(db_acc)

        def build(ci, carry):
            rows = pl.ds(pl.multiple_of(ci * CONV_CHUNK, CONV_CHUNK), CONV_CHUNK)
            g = du_cur[rows, :].astype(F32)
            if silu_out:
                g = g * _silu_grad(u_cur[rows, :].astype(F32))
            gbuf[rows, :] = g
            return carry

        lax.fori_loop(0, tt // CONV_CHUNK, build, 0, unroll=2)
        g_nxt = du_nxt[...].astype(F32)
        if silu_out:
            g_nxt = g_nxt * _silu_grad(u_nxt[...].astype(F32))
        gbuf[tt:tt + HALO, :] = jnp.where(last, 0.0, g_nxt)
        _fill_shifted(gbuf, gshift, g_res)

        def fold(v):
            out = v[0:SUBLANES]
            for s in range(SUBLANES, CONV_CHUNK, SUBLANES):
                out = out + v[s:s + SUBLANES]
            return out

        def chunk(ci, carry):
            start = pl.multiple_of(ci * CONV_CHUNK, CONV_CHUNK)
            rows = pl.ds(start, CONV_CHUNK)
            a = cur[0][rows, :].astype(F32)
            if glu:
                s = _sigmoid(cur[1][rows, :].astype(F32))
                x_in = a * s
            else:
                x_in = a
            dx = jnp.zeros((CONV_CHUNK, tc), F32)
            for k in range(K):
                g_k = _tap(gbuf, gshift, g_res, g_offsets[k], start, CONV_CHUNK)
                dx = dx + w_ref[k:k + 1, :] * g_k
                dw_acc[k * SUBLANES:(k + 1) * SUBLANES, :] += fold(g_k * x_in)
            db_acc[...] += fold(gbuf[rows, :])
            if glu:
                outs[0][rows, :] = (dx * s).astype(BF16)
                outs[1][rows, :] = (dx * a * s * (1.0 - s)).astype(BF16)
            else:
                outs[0][rows, :] = dx.astype(BF16)
            return carry

        lax.fori_loop(0, tt // CONV_CHUNK, chunk, 0)

        @pl.when(i == T // tt - 1)
        def _():
            for k in range(K):
                dw_ref[k:k + 1, :] = jnp.sum(dw_acc[k * SUBLANES:(k + 1) * SUBLANES, :], axis=0, keepdims=True)
            db_ref[...] = jnp.sum(db_acc[...], axis=0, keepdims=True)

    def cur_spec(ob):
        return pl.BlockSpec((tt, tc), lambda j, i: (i, ob + j))

    def nxt_spec(ob):
        return pl.BlockSpec((HALO, tc), lambda j, i: (jnp.minimum((i + 1) * per, last_blk), ob + j))

    in_specs = [cur_spec(0), nxt_spec(0)]
    args = [du, du]
    if silu_out:
        in_specs += [cur_spec(0), nxt_spec(0)]
        args += [u, u]
    for off in offs:
        assert off % tc == 0
        in_specs.append(cur_spec(off // tc))
        args.append(src)
    in_specs.append(pl.BlockSpec((K, tc), lambda j, i: (0, j)))
    args.append(w)
    out_specs = [pl.BlockSpec((tt, tc), lambda j, i: (i, j))] * n_in
    out_specs += [pl.BlockSpec((K, tc), lambda j, i: (0, j)), pl.BlockSpec((1, tc), lambda j, i: (0, j))]
    out_shape = [jax.ShapeDtypeStruct((T, C), BF16)] * n_in
    out_shape += [jax.ShapeDtypeStruct((K, C), F32), jax.ShapeDtypeStruct((1, C), F32)]
    return pl.pallas_call(
        body, name=name, grid=(C // tc, T // tt), in_specs=in_specs, out_specs=out_specs, out_shape=out_shape,
        scratch_shapes=[pltpu.VMEM((tt + HALO, tc), F32), pltpu.VMEM((max(len(g_res), 1), tt + HALO, tc), F32),
                        pltpu.VMEM((K * SUBLANES, tc), F32), pltpu.VMEM((SUBLANES, tc), F32)],
        compiler_params=_cp("parallel", "arbitrary"))(*args)


def mix0_post_fwd(u2, proj, o, ln_w, ln_b, *, CW, gc_off, ga_off, name):
    T = u2.shape[0]
    tt = _pick(T, 512, 16)

    def body(u_ref, gc_ref, ga_ref, o_ref, lw_ref, lb_ref, y_ref):
        def chunk(ci, carry):
            rows = pl.ds(pl.multiple_of(ci * ROW_CHUNK, ROW_CHUNK), ROW_CHUNK)
            u = u_ref[rows, :].astype(F32)
            mu = jnp.mean(u, axis=-1, keepdims=True)
            xc = u - mu
            r = lax.rsqrt(jnp.mean(xc * xc, axis=-1, keepdims=True) + EPS)
            u3 = xc * r * lw_ref[...] + lb_ref[...]
            y_ref[rows, 0:CW] = (_silu(u3) * _silu(gc_ref[rows, :].astype(F32))).astype(BF16)
            y_ref[rows, CW:2 * CW] = (o_ref[rows, :].astype(F32) * _silu(ga_ref[rows, :].astype(F32))).astype(BF16)
            return carry

        lax.fori_loop(0, tt // ROW_CHUNK, chunk, 0, unroll=8)

    row = pl.BlockSpec((tt, CW), lambda i: (i, 0))
    vec = pl.BlockSpec((1, CW), lambda i: (0, 0))
    return pl.pallas_call(
        body, name=name, grid=(T // tt,),
        in_specs=[row, pl.BlockSpec((tt, CW), lambda i: (i, gc_off // CW)),
                  pl.BlockSpec((tt, CW), lambda i: (i, ga_off // CW)), row, vec, vec],
        out_specs=pl.BlockSpec((tt, 2 * CW), lambda i: (i, 0)),
        out_shape=jax.ShapeDtypeStruct((T, 2 * CW), BF16),
        compiler_params=_cp("parallel"))(u2, proj, proj, o, ln_w, ln_b)


def mix0_post_bwd(dy, u2, proj, o, ln_w, ln_b, *, CW, gc_off, ga_off, name):
    T = u2.shape[0]
    tt = _pick(T, 512, 16)

    def body(dy_ref, u_ref, gc_ref, ga_ref, o_ref, lw_ref, lb_ref, du_ref, dgc_ref, dga_ref, do_ref, dlw_ref, dlb_ref,
             lw_acc, lb_acc):
        i = pl.program_id(0)

        @pl.when(i == 0)
        def _():
            lw_acc[...] = jnp.zeros_like(lw_acc)
            lb_acc[...] = jnp.zeros_like(lb_acc)

        def fold(v):
            out = v[0:SUBLANES]
            for s in range(SUBLANES, ROW_CHUNK, SUBLANES):
                out = out + v[s:s + SUBLANES]
            return out

        def chunk(ci, carry):
            rows = pl.ds(pl.multiple_of(ci * ROW_CHUNK, ROW_CHUNK), ROW_CHUNK)
            dyc = dy_ref[rows, 0:CW].astype(F32)
            dya = dy_ref[rows, CW:2 * CW].astype(F32)
            u = u_ref[rows, :].astype(F32)
            mu = jnp.mean(u, axis=-1, keepdims=True)
            xc = u - mu
            r = lax.rsqrt(jnp.mean(xc * xc, axis=-1, keepdims=True) + EPS)
            xhat = xc * r
            u3 = xhat * lw_ref[...] + lb_ref[...]
            gc = gc_ref[rows, :].astype(F32)
            dgc_ref[rows, :] = (dyc * _silu(u3) * _silu_grad(gc)).astype(BF16)
            du3 = dyc * _silu(gc) * _silu_grad(u3)
            lw_acc[...] += fold(du3 * xhat)
            lb_acc[...] += fold(du3)
            dxh = du3 * lw_ref[...]
            du = r * (dxh - jnp.mean(dxh, axis=-1, keepdims=True)
                      - xhat * jnp.mean(dxh * xhat, axis=-1, keepdims=True))
            du_ref[rows, :] = du.astype(BF16)
            ga = ga_ref[rows, :].astype(F32)
            ov = o_ref[rows, :].astype(F32)
            do_ref[rows, :] = (dya * _silu(ga)).astype(BF16)
            dga_ref[rows, :] = (dya * ov * _silu_grad(ga)).astype(BF16)
            return carry

        lax.fori_loop(0, tt // ROW_CHUNK, chunk, 0, unroll=8)

        @pl.when(i == T // tt - 1)
        def _():
            dlw_ref[...] = jnp.sum(lw_acc[...], axis=0, keepdims=True)
            dlb_ref[...] = jnp.sum(lb_acc[...], axis=0, keepdims=True)

    row = pl.BlockSpec((tt, CW), lambda i: (i, 0))
    vec = pl.BlockSpec((1, CW), lambda i: (0, 0))
    big = jax.ShapeDtypeStruct((T, CW), BF16)
    small = jax.ShapeDtypeStruct((1, CW), F32)
    return pl.pallas_call(
        body, name=name, grid=(T // tt,),
        in_specs=[pl.BlockSpec((tt, 2 * CW), lambda i: (i, 0)), row,
                  pl.BlockSpec((tt, CW), lambda i: (i, gc_off // CW)),
                  pl.BlockSpec((tt, CW), lambda i: (i, ga_off // CW)), row, vec, vec],
        out_specs=[row, row, row, row, vec, vec],
        out_shape=[big, big, big, big, small, small],
        scratch_shapes=[pltpu.VMEM((SUBLANES, CW), F32), pltpu.VMEM((SUBLANES, CW), F32)],
        compiler_params=_cp("arbitrary"))(dy, u2, proj, proj, o, ln_w, ln_b)


SB_UNDERFLOW = 110.0
SB_BOUND_MARGIN = 1.02


def _sb_tile(seq):
    return _pick(seq, 256, LANE)


def _softplus(z):
    return jnp.maximum(z, 0.0) + jnp.log(1.0 + jnp.exp(-jnp.abs(z)))


def _tri01(n, lower):
    i = lax.broadcasted_iota(jnp.int32, (n, n), 0)
    j = lax.broadcasted_iota(jnp.int32, (n, n), 1)
    return ((i >= j) if lower else (i <= j)).astype(BF16)


SB_HEADS_FWD = 8
SB_HEADS_BWD = 2


def _sb_heads_per_step(heads, want):
    while heads % want:
        want //= 2
    return want


def sba_fwd(proj, *, B, seq, heads, q_off, k_off, v_off, name):
    dh = SB_HEAD_DIM
    tq = _sb_tile(seq)
    assert tq % (2 * LANE) == 0
    nq = seq // tq
    hps = _sb_heads_per_step(heads, SB_HEADS_FWD)
    hw = hps * dh
    scale = dh ** -0.5

    def body(q_ref, k_ref, v_ref, tri_ref, o_ref, ct_ref, acc_ref, kmax_ref):
        qi = pl.program_id(1)
        tri = tri_ref[...]
        qs = [(q_ref[:, h * dh:(h + 1) * dh].astype(F32) * scale).astype(BF16) for h in range(hps)]

        @pl.when(qi == 0)
        def _():
            def chunk(i, best):
                rows = k_ref[pl.ds(pl.multiple_of(i * tq, tq), tq), :].astype(F32)
                sq = rows * rows
                return tuple(jnp.maximum(best[h], jnp.max(jnp.sum(sq[:, h * dh:(h + 1) * dh], axis=1, keepdims=True),
                                                          axis=0, keepdims=True)) for h in range(hps))

            best = lax.fori_loop(0, nq, chunk, (jnp.zeros((1, 1), F32),) * hps)
            for h in range(hps):
                kmax_ref[h] = jnp.broadcast_to(jnp.sqrt(best[h]), (8, LANE))

        z_bound = [jnp.sqrt(jnp.sum(qs[h].astype(F32) ** 2, axis=1, keepdims=True))
                   * (SB_BOUND_MARGIN * jnp.max(kmax_ref[h], keepdims=True)) for h in range(hps)]

        def part(h, q_rows, start, n_keys, r, mask):
            k_blk = k_ref[pl.ds(start, n_keys), h * dh:(h + 1) * dh]
            v_blk = v_ref[pl.ds(start, n_keys), h * dh:(h + 1) * dh]
            z = _dot(q_rows, k_blk, NT)
            sp = _softplus(z)
            if mask is not None:
                sp = jnp.where(mask, sp, 0.0)
            wts = jnp.exp(z - (_dot(sp.astype(BF16), tri[0:n_keys, 0:n_keys]) + r))
            if mask is not None:
                wts = jnp.where(mask, wts, 0.0)
            return _dot(wts.astype(BF16), v_blk), r + jnp.sum(sp, axis=-1, keepdims=True)

        below = lax.broadcasted_iota(jnp.int32, (tq, tq), 1) < lax.broadcasted_iota(jnp.int32, (tq, tq), 0)
        has_left = qi > 0
        left = pl.multiple_of(jnp.maximum(qi - 1, 0) * tq, tq)
        rs = []
        for h in range(hps):
            pv_d, r = part(h, qs[h], pl.multiple_of(qi * tq, tq), tq, jnp.zeros((tq, 1), F32), below)
            pv_l, r = part(h, qs[h], left, tq, r, has_left)
            acc_ref[:, h * dh:(h + 1) * dh] = pv_d + pv_l
            rs.append(r)
        rs = tuple(rs)

        def block(start, rs):
            pvs, out = [], []
            for h in range(hps):
                pv, r = part(h, qs[h], start, tq, rs[h], None)
                pvs.append(pv)
                out.append(r)
            return pvs, tuple(out)

        def more(c):
            j, rs = c
            slack = rs[0] - z_bound[0]
            for h in range(1, hps):
                slack = jnp.minimum(slack, rs[h] - z_bound[h])
            return jnp.logical_and(j < qi, jnp.min(slack) <= SB_UNDERFLOW)

        def step(c):
            j, rs = c
            pvs, rs = block(pl.multiple_of((qi - 1 - j) * tq, tq), rs)
            for h in range(hps):
                acc_ref[:, h * dh:(h + 1) * dh] += pvs[h]
            return j + 1, rs

        n_left, totals = lax.while_loop(more, step, (has_left.astype(jnp.int32), rs))
        o_ref[...] = acc_ref[...].astype(BF16)
        for h in range(hps):
            ct_ref[0, 0, h, 0:8, :] = jnp.broadcast_to(totals[h], (tq, LANE)).T[0:8, :]
            ct_ref[0, 0, h, 8:16, :] = jnp.full((8, tq), n_left, F32)

    qb, kb, vb = q_off // hw, k_off // hw, v_off // hw
    G = heads // hps
    return pl.pallas_call(
        body, name=name, grid=(B * G, nq),
        in_specs=[pl.BlockSpec((tq, hw), lambda g, i: ((g // G) * nq + i, qb + g % G)),
                  pl.BlockSpec((seq, hw), lambda g, i: (g // G, kb + g % G)),
                  pl.BlockSpec((seq, hw), lambda g, i: (g // G, vb + g % G)),
                  pl.BlockSpec((tq, tq), lambda g, i: (0, 0))],
        out_specs=[pl.BlockSpec((tq, hw), lambda g, i: ((g // G) * nq + i, g % G)),
                   pl.BlockSpec((1, 1, hps, 16, tq), lambda g, i: (g // G, i, g % G, 0, 0))],
        out_shape=[jax.ShapeDtypeStruct((B * seq, heads * dh), BF16),
                   jax.ShapeDtypeStruct((B, nq, heads, 16, tq), F32)],
        scratch_shapes=[pltpu.VMEM((tq, hw), F32), pltpu.VMEM((hps, 8, LANE), F32)],
        compiler_params=_cp("parallel", "arbitrary"))(proj, proj, proj, jnp.tril(jnp.ones((tq, tq), BF16)))


def sba_bwd(proj, ctot, do, *, B, seq, heads, q_off, k_off, v_off, name, comm=None):
    dh = SB_HEAD_DIM
    tq = _sb_tile(seq)
    nq = seq // tq
    hps = _sb_heads_per_step(heads, SB_HEADS_BWD)
    hw = hps * dh
    scale = dh ** -0.5

    def body(q_ref, k_ref, v_ref, ct_ref, do_ref, sfx_ref, pre_ref, dq_ref, dk_ref, dv_ref, dq_acc, dk_acc, dv_acc):
        qi = pl.program_id(1)

        @pl.when(qi == 0)
        def _():
            dk_acc[...] = jnp.zeros_like(dk_acc)
            dv_acc[...] = jnp.zeros_like(dv_acc)

        tri_sfx = sfx_ref[...]
        tri_pre = pre_ref[...]
        qs = [(q_ref[:, h * dh:(h + 1) * dh].astype(F32) * scale).astype(BF16) for h in range(hps)]
        dos = [do_ref[:, h * dh:(h + 1) * dh] for h in range(hps)]
        totals = [jnp.max(jnp.broadcast_to(ct_ref[0, 0, h, 0:1, :], (LANE, tq)).T, axis=1, keepdims=True)
                  for h in range(hps)]
        dq_acc[...] = jnp.zeros_like(dq_acc)

        def part(h, rows, start, n_keys, pc, pg, mask):
            cols = slice(h * dh, (h + 1) * dh)
            q_rows, do_rows = qs[h][rows], dos[h][rows]
            k_blk = k_ref[pl.ds(start, n_keys), cols]
            v_blk = v_ref[pl.ds(start, n_keys), cols]
            z = _dot(q_rows, k_blk, NT)
            sp = _softplus(z)
            sig = jnp.exp(z - sp)
            if mask is not None:
                sp = jnp.where(mask, sp, 0.0)
            pc_next = pc + jnp.sum(sp, axis=-1, keepdims=True)
            wts = jnp.exp(z - (_dot(sp.astype(BF16), tri_sfx[0:n_keys, 0:n_keys]) + (totals[h][rows] - pc_next)))
            if mask is not None:
                wts = jnp.where(mask, wts, 0.0)
            g = _dot(do_rows, v_blk, NT) * wts
            dz = g - sig * (_dot(g.astype(BF16), tri_pre[0:n_keys, 0:n_keys]) + pg)
            if mask is not None:
                dz = jnp.where(mask, dz, 0.0)
            dz = dz.astype(BF16)
            dk_acc[pl.ds(start, n_keys), cols] += _dot(dz, q_rows, TN)
            dv_acc[pl.ds(start, n_keys), cols] += _dot(wts.astype(BF16), do_rows, TN)
            return pc_next, pg + jnp.sum(g, axis=-1, keepdims=True), _dot(dz, k_blk)

        def block(start, carry):
            out = []
            for h in range(hps):
                pc, pg, dq = part(h, slice(0, tq), start, tq, carry[h][0], carry[h][1], None)
                dq_acc[:, h * dh:(h + 1) * dh] += dq
                out.append((pc, pg))
            return tuple(out)

        zero = jnp.zeros((tq, 1), F32)
        n_left = jnp.max(ct_ref[0, 0, 0, 8:16, :]).astype(jnp.int32)
        carry = lax.fori_loop(qi - n_left, qi - 1, lambda j, c: block(pl.multiple_of(j * tq, tq), c),
                              ((zero, zero),) * hps)
        below = lax.broadcasted_iota(jnp.int32, (tq, tq), 1) < lax.broadcasted_iota(jnp.int32, (tq, tq), 0)
        has_left = n_left > 0
        left = pl.multiple_of(jnp.maximum(qi - 1, 0) * tq, tq)
        for h in range(hps):
            cols = slice(h * dh, (h + 1) * dh)
            pc, pg, dq_l = part(h, slice(0, tq), left, tq, carry[h][0], carry[h][1], has_left)
            _, _, dq_d = part(h, slice(0, tq), pl.multiple_of(qi * tq, tq), tq, pc, pg, below)
            dq_ref[:, cols] = ((dq_acc[:, cols] + dq_l + dq_d) * scale).astype(BF16)

        @pl.when(qi == nq - 1)
        def _():
            dk_ref[...] = dk_acc[...].astype(BF16)
            dv_ref[...] = dv_acc[...].astype(BF16)

    qb, kb, vb = q_off // hw, k_off // hw, v_off // hw
    G = heads // hps
    q_spec = pl.BlockSpec((tq, hw), lambda g, i: ((g // G) * nq + i, qb + g % G))
    o_spec = pl.BlockSpec((tq, hw), lambda g, i: ((g // G) * nq + i, g % G))
    kv_out = pl.BlockSpec((seq, hw), lambda g, i: (g // G, g % G))
    shp = jax.ShapeDtypeStruct((B * seq, heads * dh), BF16)
    body, c_in, c_args, c_out, c_shape, c_scratch = _hosted(body, 7, 3, comm, _grid_step(nq), B * G * nq)
    tri_spec = pl.BlockSpec((tq, tq), lambda g, i: (0, 0))
    ones = jnp.ones((tq, tq), BF16)
    out = pl.pallas_call(
        body, name=name, grid=(B * G, nq),
        in_specs=[q_spec,
                  pl.BlockSpec((seq, hw), lambda g, i: (g // G, kb + g % G)),
                  pl.BlockSpec((seq, hw), lambda g, i: (g // G, vb + g % G)),
                  pl.BlockSpec((1, 1, hps, 16, tq), lambda g, i: (g // G, i, g % G, 0, 0)), o_spec,
                  tri_spec, tri_spec] + c_in,
        out_specs=[o_spec, kv_out, kv_out] + c_out, out_shape=[shp, shp, shp] + c_shape,
        scratch_shapes=[pltpu.VMEM((tq, hw), F32), pltpu.VMEM((seq, hw), F32), pltpu.VMEM((seq, hw), F32)]
        + c_scratch,
        compiler_params=_cp("arbitrary" if comm else "parallel", "arbitrary"))(
            proj, proj, proj, ctot, do, jnp.tril(ones), jnp.triu(ones), *c_args)
    return (out[0], out[1], out[2], out[3:]) if comm else out


def _head_expand(n_heads, DI):
    j = jnp.arange(LANE, dtype=jnp.int32)[:, None]
    c = jnp.arange(DI, dtype=jnp.int32)[None, :] // SSM_HEAD_DIM
    return ((j == c) & (j < n_heads)).astype(BF16)


def _split3(x):
    hi = x.astype(BF16)
    r1 = x - hi.astype(F32)
    mid = r1.astype(BF16)
    return hi, mid, (r1 - mid.astype(F32)).astype(BF16)


def dt_fwd(proj, bias, a_log, expand, *, dt_off, name):
    T = proj.shape[0]
    DI = expand.shape[1]
    L = SSM_CHUNK
    tt = _pick(T, 1024, L)

    def body(raw_ref, bias_ref, al_ref, e_ref, dt_ref, cs_ref, dtx_ref, csx_ref):
        x = raw_ref[...].astype(F32) + bias_ref[...]
        dt = _softplus(x)
        dt_ref[...] = dt
        la = dt * (-jnp.exp(al_ref[...]))
        tri = _tri01(L, True)
        for c in range(tt // L):
            cs_ref[c * L:(c + 1) * L, :] = _tri_dot3(tri, la[c * L:(c + 1) * L, :])
        e = e_ref[...]
        dtx_ref[...] = _dot(dt.astype(BF16), e).astype(BF16)
        hi, mid, lo = _split3(cs_ref[...])
        csx_ref[...] = _dot(hi, e) + _dot(mid, e) + _dot(lo, e)

    row = pl.BlockSpec((tt, LANE), lambda i: (i, 0))
    wide = pl.BlockSpec((tt, DI), lambda i: (i, 0))
    vec = pl.BlockSpec((1, LANE), lambda i: (0, 0))
    return pl.pallas_call(
        body, name=name, grid=(T // tt,),
        in_specs=[pl.BlockSpec((tt, LANE), lambda i: (i, dt_off // LANE)), vec, vec,
                  pl.BlockSpec((LANE, DI), lambda i: (0, 0))],
        out_specs=[row, row, wide, wide],
        out_shape=[jax.ShapeDtypeStruct((T, LANE), F32), jax.ShapeDtypeStruct((T, LANE), F32),
                   jax.ShapeDtypeStruct((T, DI), BF16), jax.ShapeDtypeStruct((T, DI), F32)],
        compiler_params=_cp("parallel"))(proj, bias, a_log, expand)


def dt_bwd(ddt_x, dcs_x, dcs_cols, proj, dt, bias, a_log, reduce_t, *, dt_off, n_heads, name):
    T = proj.shape[0]
    DI = reduce_t.shape[0]
    L = SSM_CHUNK
    tt = _pick(T, 1024, L)

    def body(ddtx_ref, dcsx_ref, dcsc_ref, raw_ref, dt_ref, bias_ref, al_ref, r_ref, draw_ref, dbias_ref, dal_ref,
             dla_buf):
        @pl.when(pl.program_id(0) == 0)
        def _():
            dbias_ref[...] = jnp.zeros_like(dbias_ref)
            dal_ref[...] = jnp.zeros_like(dal_ref)

        r = r_ref[...]
        ddt = _dot(ddtx_ref[...], r)
        dx = dcsx_ref[...]
        hi = dx.astype(BF16)
        dcs = _dot(hi, r) + _dot((dx - hi.astype(F32)).astype(BF16), r) + dcsc_ref[...]
        triu = _tri01(L, False)
        for c in range(tt // L):
            dla_buf[c * L:(c + 1) * L, :] = _tri_dot3(triu, dcs[c * L:(c + 1) * L, :])
        dla = dla_buf[...]
        a = -jnp.exp(al_ref[...])
        valid = lax.broadcasted_iota(jnp.int32, (tt, LANE), 1) < n_heads
        dal_ref[...] += jnp.sum(jnp.where(valid, dla * dt_ref[...], 0.0), axis=0, keepdims=True) * a
        x = raw_ref[...].astype(F32) + bias_ref[...]
        draw = jnp.where(valid, (ddt + dla * a) * _sigmoid(x), 0.0)
        dbias_ref[...] += jnp.sum(draw, axis=0, keepdims=True)
        draw_ref[...] = draw.astype(BF16)

    row = pl.BlockSpec((tt, LANE), lambda i: (i, 0))
    wide = pl.BlockSpec((tt, DI), lambda i: (i, 0))
    vec = pl.BlockSpec((1, LANE), lambda i: (0, 0))
    return pl.pallas_call(
        body, name=name, grid=(T // tt,),
        in_specs=[wide, wide, row, pl.BlockSpec((tt, LANE), lambda i: (i, dt_off // LANE)), row, vec, vec,
                  pl.BlockSpec((DI, LANE), lambda i: (0, 0))],
        out_specs=[row, vec, vec],
        out_shape=[jax.ShapeDtypeStruct((T, LANE), BF16), jax.ShapeDtypeStruct((1, LANE), F32),
                   jax.ShapeDtypeStruct((1, LANE), F32)],
        scratch_shapes=[pltpu.VMEM((tt, LANE), F32)],
        compiler_params=_cp("arbitrary"))(ddt_x, dcs_x, dcs_cols, proj, dt, bias, a_log, reduce_t)


def _pair_terms(x_ref, dtx_ref, csx_ref, csr_ref, pair, ppg, lo_half, causal):
    L = SSM_CHUNK
    g, pp = divmod(pair, ppg)
    ra = g * HEAD_ROWS + 2 * pp
    cols = slice(pair * LANE, (pair + 1) * LANE)
    X = x_ref[:, cols].astype(F32)
    dt_p = dtx_ref[:, cols].astype(F32)
    own = csx_ref[:, cols]
    other = pltpu.roll(own, SSM_HEAD_DIM, 1)
    csa_c = jnp.where(lo_half, own, other)
    csb_c = jnp.where(lo_half, other, own)
    La = jnp.exp(jnp.where(causal, csa_c - csr_ref[ra:ra + 1, :], NEG_BIG))
    Lb = jnp.exp(jnp.where(causal, csb_c - csr_ref[ra + 1:ra + 2, :], NEG_BIG))
    last = csx_ref[L - 1:L, cols]
    return g, ra, cols, X, dt_p, La, Lb, jnp.exp(own), jnp.exp(last - own), jnp.exp(last)


def scan_fwd(xbc, dt_x, cs_x, cs_row, d_full, *, B, seq, DI, name):
    L, N, G = SSM_CHUNK, SSM_STATE, SSM_GROUPS
    nc = seq // L
    XW = xbc.shape[1]
    n_pairs = DI // LANE
    ppg = n_pairs // G

    def body(x_ref, dtx_ref, csx_ref, csr_ref, d_ref, y_ref, st_ref, state):
        @pl.when(pl.program_id(0) == 0)
        def _():
            state[...] = jnp.zeros_like(state)

        causal = lax.broadcasted_iota(jnp.int32, (L, L), 0) >= lax.broadcasted_iota(jnp.int32, (L, L), 1)
        lo_half = lax.broadcasted_iota(jnp.int32, (L, LANE), 1) < SSM_HEAD_DIM
        for b in range(B):
            xb, yb = x_ref.at[b], y_ref.at[b]
            cbs = []
            for g in range(G):
                Bc = xb[:, DI + g * N:DI + (g + 1) * N]
                Cc = xb[:, DI + G * N + g * N:DI + G * N + (g + 1) * N]
                cbs.append((Bc, Cc, _dot(Cc, Bc, NT)))
            for pair in range(n_pairs):
                g, _, cols, X, dt_p, La, Lb, ecs, tail, e_last = _pair_terms(
                    xb, dtx_ref.at[b], csx_ref.at[b], csr_ref.at[b], pair, ppg, lo_half, causal)
                Bc, Cc, CB = cbs[g]
                xs = X * dt_p
                xsb = xs.astype(BF16)
                y = jnp.where(lo_half, _dot((CB * La).astype(BF16), xsb), _dot((CB * Lb).astype(BF16), xsb))
                ST = state[b, pair]
                st_ref[b, 0, pair] = ST
                y = y + ecs * _dot(Cc, ST.astype(BF16)) + d_ref[:, cols] * X
                yb[:, cols] = y.astype(BF16)
                state[b, pair] = e_last * ST + _dot(Bc, (xs * tail).astype(BF16), TN)

    wide = pl.BlockSpec((B, L, DI), lambda c: (0, c, 0))
    y, states = pl.pallas_call(
        body, name=name, grid=(nc,),
        in_specs=[pl.BlockSpec((B, L, XW), lambda c: (0, c, 0)), wide, wide,
                  pl.BlockSpec((B, G * HEAD_ROWS, L), lambda c: (0, 0, c)),
                  pl.BlockSpec((1, DI), lambda c: (0, 0))],
        out_specs=[wide, pl.BlockSpec((B, 1, n_pairs, N, LANE), lambda c: (0, c, 0, 0, 0))],
        out_shape=[jax.ShapeDtypeStruct((B, seq, DI), BF16),
                   jax.ShapeDtypeStruct((B, nc, n_pairs, N, LANE), F32)],
        scratch_shapes=[pltpu.VMEM((B, n_pairs, N, LANE), F32)],
        compiler_params=_cp("arbitrary"))(xbc.reshape(B, seq, XW), dt_x.reshape(B, seq, DI),
                                          cs_x.reshape(B, seq, DI), cs_row, d_full)
    return y.reshape(B * seq, DI), states


def scan_bwd(xbc, dt_x, cs_x, cs_row, d_full, states, dy, *, B, seq, DI, name):
    L, N, G = SSM_CHUNK, SSM_STATE, SSM_GROUPS
    nc = seq // L
    XW = xbc.shape[1]
    n_pairs = DI // LANE
    ppg = n_pairs // G
    HR = G * HEAD_ROWS
    inv_p = 1.0 / SSM_HEAD_DIM

    def body(x_ref, dtx_ref, csx_ref, csr_ref, d_ref, st_ref, dy_ref, dx_ref, ddtx_ref, dcsx_ref, dcsr_ref, dd_ref,
             dH):
        @pl.when(pl.program_id(0) == 0)
        def _():
            dH[...] = jnp.zeros_like(dH)
            dd_ref[...] = jnp.zeros_like(dd_ref)

        causal = lax.broadcasted_iota(jnp.int32, (L, L), 0) >= lax.broadcasted_iota(jnp.int32, (L, L), 1)
        lo_half = lax.broadcasted_iota(jnp.int32, (L, LANE), 1) < SSM_HEAD_DIM
        last_row = lax.broadcasted_iota(jnp.int32, (L, LANE), 0) == L - 1
        head_row = lax.broadcasted_iota(jnp.int32, (HR, 1), 0)
        for b in range(B):
            xb, dxb = x_ref.at[b], dx_ref.at[b]
            dcs_rows = jnp.zeros((HR, L), F32)

            for g in range(G):
                Bc = xb[:, DI + g * N:DI + (g + 1) * N]
                Cc = xb[:, DI + G * N + g * N:DI + G * N + (g + 1) * N]
                CB = _dot(Cc, Bc, NT)
                dCB = jnp.zeros((L, L), F32)
                dC = jnp.zeros((L, N), F32)
                dB = jnp.zeros((L, N), F32)
                for pp in range(ppg):
                    pair = g * ppg + pp
                    _, ra, cols, X, dt_p, La, Lb, ecs, tail, e_last = _pair_terms(
                        xb, dtx_ref.at[b], csx_ref.at[b], csr_ref.at[b], pair, ppg, lo_half, causal)
                    xs = X * dt_p
                    xsb = xs.astype(BF16)
                    Ma, Mb = CB * La, CB * Lb
                    dY = dy_ref[b, :, cols].astype(F32)
                    dYb = dY.astype(BF16)
                    dMa = _dot(jnp.where(lo_half, dY, 0.0).astype(BF16), xsb, NT)
                    dMb = _dot(jnp.where(lo_half, 0.0, dY).astype(BF16), xsb, NT)
                    dSa, dSb = dMa * Ma, dMb * Mb
                    dCB = dCB + dMa * La + dMb * Lb
                    dcs = jnp.where(lo_half, jnp.sum(dSa, axis=1, keepdims=True),
                                    jnp.sum(dSb, axis=1, keepdims=True)) * inv_p
                    dcs_rows = dcs_rows - jnp.where(head_row == ra, jnp.sum(dSa, axis=0, keepdims=True), 0.0)
                    dcs_rows = dcs_rows - jnp.where(head_row == ra + 1, jnp.sum(dSb, axis=0, keepdims=True), 0.0)
                    dxs = jnp.where(lo_half, _dot(Ma.astype(BF16), dYb, TN), _dot(Mb.astype(BF16), dYb, TN))
                    ST = st_ref[b, 0, pair]
                    STb = ST.astype(BF16)
                    dYe = (dY * ecs).astype(BF16)
                    dC = dC + _dot(dYe, STb, NT)
                    dSTp = _dot(Cc, dYe, TN)
                    dcs = dcs + dY * (ecs * _dot(Cc, STb))
                    dSTn = dH[b, pair]
                    dSTnb = dSTn.astype(BF16)
                    dSTp = dSTp + e_last * dSTn
                    XBt = _dot(Bc, dSTnb)
                    dxs = dxs + tail * XBt
                    t2 = xs * XBt * tail
                    at_end = (e_last * jnp.sum(dSTn * ST, axis=0, keepdims=True)
                              + jnp.sum(t2, axis=0, keepdims=True))
                    dcs = dcs - t2 + jnp.where(last_row, at_end, 0.0)
                    dB = dB + _dot((xs * tail).astype(BF16), dSTnb, NT)
                    dxb[:, cols] = (dxs * dt_p + d_ref[:, cols] * dY).astype(BF16)
                    ddtx_ref[b, :, cols] = (dxs * X).astype(BF16)
                    dcsx_ref[b, :, cols] = dcs
                    dd_ref[b, :, cols] += jnp.sum(dY * X, axis=0, keepdims=True)
                    dH[b, pair] = dSTp
                dCBb = dCB.astype(BF16)
                dxb[:, DI + g * N:DI + (g + 1) * N] = (dB + _dot(dCBb, Cc, TN)).astype(BF16)
                dxb[:, DI + G * N + g * N:DI + G * N + (g + 1) * N] = (dC + _dot(dCBb, Bc)).astype(BF16)
            dcsr_ref[b] = dcs_rows

    rev = lambda c: (0, nc - 1 - c, 0)
    wide = pl.BlockSpec((B, L, DI), rev)
    xspec = pl.BlockSpec((B, L, XW), rev)
    hrow = pl.BlockSpec((B, HR, L), lambda c: (0, 0, nc - 1 - c))
    dx, ddt_x, dcs_x, dcs_row, dd = pl.pallas_call(
        body, name=name, grid=(nc,),
        in_specs=[xspec, wide, wide, hrow, pl.BlockSpec((1, DI), lambda c: (0, 0)),
                  pl.BlockSpec((B, 1, n_pairs, N, LANE), lambda c: (0, nc - 1 - c, 0, 0, 0)), wide],
        out_specs=[xspec, wide, wide, hrow, pl.BlockSpec((B, 1, DI), lambda c: (0, 0, 0))],
        out_shape=[jax.ShapeDtypeStruct((B, seq, XW), BF16), jax.ShapeDtypeStruct((B, seq, DI), BF16),
                   jax.ShapeDtypeStruct((B, seq, DI), F32), jax.ShapeDtypeStruct((B, HR, seq), F32),
                   jax.ShapeDtypeStruct((B, 1, DI), F32)],
        scratch_shapes=[pltpu.VMEM((B, n_pairs, N, LANE), F32)],
        compiler_params=_cp("arbitrary"))(xbc.reshape(B, seq, XW), dt_x.reshape(B, seq, DI), cs_x.reshape(B, seq, DI),
                                          cs_row, d_full, states, dy.reshape(B, seq, DI))
    return dx.reshape(B * seq, XW), ddt_x.reshape(B * seq, DI), dcs_x.reshape(B * seq, DI), dcs_row, dd


def gnorm_fwd(y, proj, w, *, DI, name):
    T = y.shape[0]
    tt = _pick(T, 512, 16)
    gw = DI // SSM_GROUPS

    def body(y_ref, z_ref, w_ref, o_ref):
        for g in range(SSM_GROUPS):
            sl = slice(g * gw, (g + 1) * gw)
            y2 = y_ref[:, sl].astype(F32) * _silu(z_ref[:, sl].astype(F32))
            r = lax.rsqrt(jnp.mean(y2 * y2, axis=-1, keepdims=True) + EPS)
            o_ref[:, sl] = (y2 * r * w_ref[:, sl]).astype(BF16)

    row = pl.BlockSpec((tt, DI), lambda i: (i, 0))
    return pl.pallas_call(
        body, name=name, grid=(T // tt,),
        in_specs=[row, row, pl.BlockSpec((1, DI), lambda i: (0, 0))], out_specs=row,
        out_shape=jax.ShapeDtypeStruct((T, DI), BF16), compiler_params=_cp("parallel"))(y, proj, w)


def gnorm_bwd(dyn, y, proj, w, *, DI, name):
    T = y.shape[0]
    tt = _pick(T, 512, 16)
    gw = DI // SSM_GROUPS

    def body(dyn_ref, y_ref, z_ref, w_ref, dy_ref, dz_ref, dw_ref):
        @pl.when(pl.program_id(0) == 0)
        def _():
            dw_ref[...] = jnp.zeros_like(dw_ref)

        for g in range(SSM_GROUPS):
            sl = slice(g * gw, (g + 1) * gw)
            yv = y_ref[:, sl].astype(F32)
            z = z_ref[:, sl].astype(F32)
            sz = _silu(z)
            y2 = yv * sz
            r = lax.rsqrt(jnp.mean(y2 * y2, axis=-1, keepdims=True) + EPS)
            xhat = y2 * r
            d = dyn_ref[:, sl].astype(F32)
            dw_ref[:, sl] += jnp.sum(d * xhat, axis=0, keepdims=True)
            dxh = d * w_ref[:, sl]
            dy2 = r * (dxh - xhat * jnp.mean(dxh * xhat, axis=-1, keepdims=True))
            dy_ref[:, sl] = (dy2 * sz).astype(BF16)
            dz_ref[:, sl] = (dy2 * yv * _silu_grad(z)).astype(BF16)

    row = pl.BlockSpec((tt, DI), lambda i: (i, 0))
    vec = pl.BlockSpec((1, DI), lambda i: (0, 0))
    shp = jax.ShapeDtypeStruct((T, DI), BF16)
    return pl.pallas_call(
        body, name=name, grid=(T // tt,), in_specs=[row, row, row, vec], out_specs=[row, row, vec],
        out_shape=[shp, shp, jax.ShapeDtypeStruct((1, DI), F32)],
        compiler_params=_cp("arbitrary"))(dyn, y, proj, w)


N_CHIP = 4
SHARD_ROW_ALIGN = 128


def _comm_out_shapes(srcs, modes):
    return [jax.ShapeDtypeStruct(((N_DEV,) if mode in ("gather", "gather_direct") else ()) + s.shape, s.dtype)
            for s, mode in zip(srcs, modes)]


def _comm_scratch(n):
    return [pltpu.SemaphoreType.DMA((n, N_DEV - 1)), pltpu.SemaphoreType.DMA((n, N_DEV - 1)),
            pltpu.SemaphoreType.DMA((n,))]


def _comm_phases(modes, src_refs, out_refs, send_sems, recv_sems, local_sems):
    x, y, c = lax.axis_index("x"), lax.axis_index("y"), lax.axis_index("c")
    me, sibling = (x, y, c), (x, y, 1 - c)
    chips = [(1 - x, y), (x, 1 - y), (1 - x, 1 - y)]
    relays = [a for a, mode in enumerate(modes) if mode == "gather"]

    def slot(p):
        return 4 * p[0] + 2 * p[1] + p[2]

    def remote(a, k, src, dst, to):
        return pltpu.make_async_remote_copy(src_ref=src, dst_ref=dst, send_sem=send_sems.at[a, k],
                                            recv_sem=recv_sems.at[a, k], device_id=to,
                                            device_id_type=pl.DeviceIdType.MESH)

    def first_copies():
        local, two_way, send_only = [], [], []
        for a, mode in enumerate(modes):
            src, out = src_refs[a], out_refs[a]
            if mode == "sibling":
                two_way.append(remote(a, 0, src, out, sibling))
            elif mode == "chips":
                mine = 2 * x + y
                local.append(pltpu.make_async_copy(src.at[mine], out.at[mine], local_sems.at[a]))
                for j, chip in enumerate(chips):
                    two_way.append(remote(a, 1 + j, src.at[2 * chip[0] + chip[1]], out.at[mine], (*chip, c)))
            elif mode == "gather_direct":
                local.append(pltpu.make_async_copy(src, out.at[slot(me)], local_sems.at[a]))
                for k in range(1, N_DEV):
                    peer = (1 - x if k & 4 else x, 1 - y if k & 2 else y, 1 - c if k & 1 else c)
                    two_way.append(remote(a, k - 1, src, out.at[slot(me)], peer))
            else:
                assert mode == "gather"
                local.append(pltpu.make_async_copy(src, out.at[slot(me)], local_sems.at[a]))
                send_only.append(remote(a, 0, src, out.at[slot(me)], sibling))
                for j, chip in enumerate(chips):
                    send_only.append(remote(a, 1 + j, src, out.at[slot(me)], (*chip, c)))
        return local, two_way, send_only

    def forwards():
        out = []
        for a in relays:
            for j, chip in enumerate(chips):
                landed = out_refs[a].at[slot((*chip, c))]
                out.append((remote(a, 1 + j, landed, landed, me), remote(a, 4 + j, landed, landed, sibling)))
        return out

    def start():
        local, two_way, send_only = first_copies()
        for cp in local + two_way + send_only:
            cp.start()

    def relay():
        for arrival, fwd in forwards():
            arrival.wait_recv()
            fwd.start()

    def finish():
        local, two_way, send_only = first_copies()
        for a in relays:
            blk = out_refs[a].at[slot(sibling)]
            remote(a, 0, blk, blk, me).wait_recv()
            for j, chip in enumerate(chips):
                blk = out_refs[a].at[slot((*chip, 1 - c))]
                remote(a, 4 + j, blk, blk, me).wait_recv()
        for cp in send_only + [fwd for _, fwd in forwards()]:
            cp.wait_send()
        for cp in two_way + local:
            cp.wait()

    return start, relay, finish, bool(relays)


def _hosted(body, n_in, n_out, comm, step, n_steps):
    if comm is None:
        return body, [], [], [], [], []
    srcs, modes = comm
    nc = len(srcs)

    def wrapped(*refs):
        ins, csrc = refs[:n_in], refs[n_in:n_in + nc]
        outs = refs[n_in + nc:n_in + nc + n_out]
        cout = refs[n_in + nc + n_out:n_in + 2 * nc + n_out]
        scratch = refs[n_in + 2 * nc + n_out:len(refs) - 3]
        start, relay, finish, has_relay = _comm_phases(modes, csrc, cout, *refs[len(refs) - 3:])
        s = step()
        pl.when(s == 0)(start)
        body(*ins, *outs, *scratch)
        if has_relay:
            pl.when(s == (2 * n_steps) // 3)(relay)
        pl.when(s == n_steps - 1)(finish)

    any_spec = pl.BlockSpec(memory_space=pl.ANY)
    return wrapped, [any_spec] * nc, list(srcs), [any_spec] * nc, _comm_out_shapes(srcs, modes), _comm_scratch(nc)


def exchange(srcs, modes, *, name):
    n = len(srcs)

    def body(*refs):
        start, relay, finish, has_relay = _comm_phases(modes, refs[:n], refs[n:2 * n], *refs[2 * n:])
        start()
        if has_relay:
            relay()
        finish()

    any_spec = pl.BlockSpec(memory_space=pl.ANY)
    return pl.pallas_call(
        body, name=name, in_specs=[any_spec] * n, out_specs=[any_spec] * n, out_shape=_comm_out_shapes(srcs, modes),
        scratch_shapes=_comm_scratch(n), compiler_params=pltpu.CompilerParams(has_side_effects=True))(*srcs)


def pair_sum(a, b, *, name):
    n, R, C = a.shape
    tr = _pick(n * R, 1024, 16)

    def body(a_ref, b_ref, o_ref):
        o_ref[...] = (a_ref[...].astype(F32) + b_ref[...].astype(F32)).astype(BF16)

    blk = pl.BlockSpec((tr, C), lambda i: (i, 0))
    out = pl.pallas_call(
        body, name=name, grid=(n * R // tr,), in_specs=[blk, blk], out_specs=blk,
        out_shape=jax.ShapeDtypeStruct((n * R, C), BF16),
        compiler_params=_cp("parallel"))(a.reshape(n * R, C), b.reshape(n * R, C))
    return out.reshape(n, R, C)


def sum_slots(recv, *, name):
    _, R, C = recv.shape
    tr = _pick(R, 512, 8)

    def body(r_ref, o_ref):
        acc = r_ref[0].astype(F32)
        for p in range(1, N_DEV):
            acc = acc + r_ref[p].astype(F32)
        o_ref[...] = acc

    return pl.pallas_call(
        body, name=name, grid=(R // tr,),
        in_specs=[pl.BlockSpec((N_DEV, tr, C), lambda i: (0, i, 0))],
        out_specs=pl.BlockSpec((tr, C), lambda i: (i, 0)),
        out_shape=jax.ShapeDtypeStruct((R, C), F32), compiler_params=_cp("parallel"))(recv)


def adamw(gsrc, w, m, v, *, name):
    slots, R, C = gsrc.shape
    tr = _pick(R, 256, 16 if gsrc.dtype == BF16 else 8)
    c1 = 1.0 / (1.0 - ADAM_B1 ** ADAM_STEP)
    c2 = 1.0 / (1.0 - ADAM_B2 ** ADAM_STEP)

    def body(g_ref, w_ref, m_ref, v_ref, go_ref, d_ref, mo_ref, vo_ref):
        g = g_ref[0].astype(F32)
        for p in range(1, slots):
            g = g + g_ref[p].astype(F32)
        m2 = ADAM_B1 * m_ref[...] + (1.0 - ADAM_B1) * g
        v2 = ADAM_B2 * v_ref[...] + (1.0 - ADAM_B2) * (g * g)
        go_ref[...] = g
        mo_ref[...] = m2
        vo_ref[...] = v2
        d_ref[...] = -ADAM_LR * ((m2 * c1) / (jnp.sqrt(v2 * c2) + ADAM_EPS) + ADAM_WD * w_ref[...])

    blk = pl.BlockSpec((tr, C), lambda i: (i, 0))
    shp = jax.ShapeDtypeStruct((R, C), F32)
    return pl.pallas_call(
        body, name=name, grid=(R // tr,),
        in_specs=[pl.BlockSpec((slots, tr, C), lambda i: (0, i, 0)), blk, blk, blk],
        out_specs=[blk] * 4, out_shape=[shp] * 4, compiler_params=_cp("parallel"))(gsrc, w, m, v)


def _pad_cols(a, n):
    return jnp.pad(a, ((0, 0), (0, n - a.shape[1])))


def _to_rows(a, B, seq, H):
    G = SSM_GROUPS
    R = H // G
    t = a[:, :H].reshape(B, seq, G, R).transpose(0, 2, 3, 1)
    t = jnp.pad(t, ((0, 0), (0, 0), (0, HEAD_ROWS - R), (0, 0)))
    return t.reshape(B, G * HEAD_ROWS, seq)


def _from_rows(a, B, seq, H):
    G = SSM_GROUPS
    R = H // G
    t = a.reshape(B, G, HEAD_ROWS, seq)[:, :, :R].transpose(0, 3, 1, 2).reshape(B * seq, H)
    return _pad_cols(t, LANE)


def _chip_split(grads):
    c_idx = lax.axis_index("c")
    keep, give = [], []
    for g in grads:
        by_chip = g.reshape((N_CHIP, 2) + g.shape[1:])
        keep.append(lax.dynamic_index_in_dim(by_chip, c_idx, axis=1, keepdims=False))
        give.append(lax.dynamic_index_in_dim(by_chip, 1 - c_idx, axis=1, keepdims=False))
    return keep, give


def _chip_sums(grads, name):
    keep, give = _chip_split(grads)
    swapped = exchange(give, ["sibling"] * len(give), name="swap_" + name)
    return [pair_sum(k, s, name=f"chip_sum_{name}_{i}") for i, (k, s) in enumerate(zip(keep, swapped))]


def local_step(x, target, loc, od_w_in_t, *, B, seq):
    T, D = x.shape
    CW = D
    heads = CW // SB_HEAD_DIM
    DI = 2 * D
    H = DI // SSM_HEAD_DIM
    XW = DI + 2 * SSM_GROUPS * SSM_STATE
    in_odd = DI + XW + H
    w1_rows = in_odd // N_DEV
    q_off, k_off, v_off, gc_off, ga_off = 3 * CW, 4 * CW, 5 * CW, 2 * CW, 6 * CW
    dt_off = DI + XW

    small_packed, small_spans = _pack_rows([loc[n] for n in SMALL_SHARDED], LANE, 8)
    n0, (g_ev_in, small_all) = rmsnorm_fwd(x, loc["ev_norm_w"], name="l0_norm",
                                           comm=([loc["ev_w_in"].astype(BF16), small_packed],
                                                 ["gather", "gather_direct"]))
    p = {n: loc[n] for n in SMALL}
    for n, a in zip(SMALL_SHARDED, _unpack_rows(small_all, small_spans)):
        p[n] = _col_unshards(a)
    p["ev_w_in"] = _col_unshards(g_ev_in)
    proj0, (g_od_in_t,) = mm_nn(n0, p["ev_w_in"], out_dtype=BF16, name="l0_in_proj",
                                comm=([od_w_in_t.astype(BF16)], ["gather"]))
    w1t = g_od_in_t[:, :w1_rows].reshape(in_odd, D)
    w1t = jnp.pad(w1t, ((0, -(-(in_odd + LANE) // 256) * 256 - in_odd), (0, 0)))
    (u2,), (g_ev_out, g_od_out) = dwconv_fwd(
        proj0, (0, CW), p["ev_dw_w"], p["ev_dw_b"], C=CW, seq=seq, glu=True, silu_out=False, name="l0_conv",
        comm=([loc["ev_w_out"].astype(BF16), loc["od_w_out"].astype(BF16)], ["gather"] * 2))
    p["ev_w_out"] = g_ev_out.reshape(-1, D)
    od_w_out = g_od_out.reshape(-1, D)
    o, ctot = sba_fwd(proj0, B=B, seq=seq, heads=heads, q_off=q_off, k_off=k_off, v_off=v_off, name="l0_attn")
    ycat = mix0_post_fwd(u2, proj0, o, p["ev_ln_w"], p["ev_ln_b"], CW=CW, gc_off=gc_off, ga_off=ga_off,
                         name="l0_post")
    h1 = mm_nn(ycat, p["ev_w_out"], add=x, out_dtype=F32, name="l0_out_proj")

    n1 = rmsnorm_fwd(h1, p["od_norm_w"], name="l1_norm")
    proj1 = mm_nt_terms([(n1, 0, D, 0)], w1t, out_dtype=BF16, name="l1_in_proj")
    u_pre, xbc = dwconv_fwd(proj1, (DI,), p["od_conv_w"], p["od_conv_b"], C=XW, seq=seq, glu=False, silu_out=True,
                            name="l1_conv")
    bias_p, alog_p = _pad_cols(p["od_dt_bias"], LANE), _pad_cols(p["od_a_log"], LANE)
    expand = _head_expand(H, DI)
    dt, cs, dt_x, cs_x = dt_fwd(proj1, bias_p, alog_p, expand, dt_off=dt_off, name="l1_dt")
    cs_row = _to_rows(cs, B, seq, H)
    d_full = jnp.repeat(p["od_d"], SSM_HEAD_DIM, axis=1)
    y_ssd, states = scan_fwd(xbc, dt_x, cs_x, cs_row, d_full, B=B, seq=seq, DI=DI, name="l1_ssd")
    yn = gnorm_fwd(y_ssd, proj1, p["od_gnorm_w"], DI=DI, name="l1_gnorm")
    h2 = mm_nn(yn, od_w_out, add=h1, out_dtype=F32, name="l1_out_proj")

    loss, dh2, dh2b, g_final = final_loss(h2, p["final_norm_w"], target, name="loss_head")

    g_od_w_out = mm_tn(yn, dh2b, out_dtype=BF16, name="l1_dw_out")
    dyn = mm_nt_terms([(dh2b, 0, D, 0)], od_w_out, out_dtype=BF16, name="l1_d_out_proj")
    dy_ssd, dz, g_gnorm = gnorm_bwd(dyn, y_ssd, proj1, p["od_gnorm_w"], DI=DI, name="l1_gnorm_bwd")
    dxbc_c, ddt_x, dcs_x, dcs_row, dd_part = scan_bwd(xbc, dt_x, cs_x, cs_row, d_full, states, dy_ssd, B=B, seq=seq,
                                                      DI=DI, name="l1_ssd_bwd")
    g_d = dd_part.sum(axis=(0, 1)).reshape(H, SSM_HEAD_DIM).sum(axis=1)[None, :]
    draw, g_bias, g_alog = dt_bwd(ddt_x, dcs_x, _from_rows(dcs_row, B, seq, H), proj1, dt, bias_p, alog_p, expand.T,
                                  dt_off=dt_off, n_heads=H, name="l1_dt_bwd")
    dxbc, g_conv_w, g_conv_b = dwconv_bwd(dxbc_c, u_pre, proj1, (DI,), p["od_conv_w"], C=XW, seq=seq, glu=False,
                                          silu_out=True, name="l1_conv_bwd")
    tw = 512 if DI % 512 == 0 else LANE
    terms = [(dz, j, tw, j * tw) for j in range(DI // tw)]
    terms += [(dxbc, j, tw, DI + j * tw) for j in range(XW // tw)]
    terms += [(draw, 0, LANE, dt_off)]
    dn1 = mm_nn_terms(terms, w1t, out_dtype=BF16, name="l1_d_in_proj")
    g_od_w_in_t = jnp.concatenate([mm_tn(dz, n1, out_dtype=BF16, name="l1_dw_in_z"),
                                   mm_tn(dxbc, n1, out_dtype=BF16, name="l1_dw_in_xbc"),
                                   mm_tn(draw, n1, out_dtype=BF16, name="l1_dw_in_dt")], axis=0)[:in_odd]
    w1_pad = (-w1_rows) % SHARD_ROW_ALIGN
    keep, give = _chip_split([jnp.pad(g_od_w_in_t.reshape(N_DEV, w1_rows, D), ((0, 0), (0, w1_pad), (0, 0))),
                              g_od_w_out.reshape(N_DEV, -1, D)])
    dh1, dh1b, g_od_norm, swapped = rmsnorm_bwd(h1, p["od_norm_w"], dn1, dh2, name="l1_norm_bwd",
                                                comm=(give, ["sibling"] * 2))
    l1_chip = [pair_sum(k, s, name=f"chip_sum_l1_{i}") for i, (k, s) in enumerate(zip(keep, swapped))]

    g_ev_w_out = mm_tn(ycat, dh1b, out_dtype=BF16, name="l0_dw_out")
    dycat = mm_nt_terms([(dh1b, 0, D, 0)], p["ev_w_out"], out_dtype=BF16, name="l0_d_out_proj")
    du2, dgc, dga, do, g_ln_w, g_ln_b = mix0_post_bwd(dycat, u2, proj0, o, p["ev_ln_w"], p["ev_ln_b"], CW=CW,
                                                      gc_off=gc_off, ga_off=ga_off, name="l0_post_bwd")
    dq, dk, dv, (r_od_in_t, r_od_out) = sba_bwd(proj0, ctot, do, B=B, seq=seq, heads=heads, q_off=q_off, k_off=k_off,
                                                v_off=v_off, name="l0_attn_bwd", comm=(l1_chip, ["chips", "chips"]))
    dga_a, dga_b, g_dw_w, g_dw_b = dwconv_bwd(du2, None, proj0, (0, CW), p["ev_dw_w"], C=CW, seq=seq, glu=True,
                                              silu_out=False, name="l0_conv_bwd")
    pieces = [dga_a, dga_b, dgc, dq, dk, dv, dga]
    g_ev_w_in = jnp.concatenate([mm_tn(n0, pc, out_dtype=BF16, name=f"l0_dw_in_{j}") for j, pc in enumerate(pieces)],
                                axis=1)
    l0_chip = _chip_sums([_col_shards(g_ev_w_in), g_ev_w_out.reshape(N_DEV, -1, D)], "l0")
    dn0, (r_ev_in, r_ev_out) = mm_nt_terms([(pc, 0, CW, j * CW) for j, pc in enumerate(pieces)], p["ev_w_in"],
                                           out_dtype=BF16, name="l0_d_in_proj", comm=(l0_chip, ["chips", "chips"]))
    dx, _, g_ev_norm = rmsnorm_bwd(x, p["ev_norm_w"], dn0, dh1, name="l0_norm_bwd")

    small = dict(ev_norm_w=g_ev_norm, ev_dw_w=g_dw_w, ev_dw_b=g_dw_b, ev_ln_w=g_ln_w, ev_ln_b=g_ln_b,
                 od_norm_w=g_od_norm, od_conv_w=g_conv_w, od_conv_b=g_conv_b, od_dt_bias=g_bias[:, :H],
                 od_a_log=g_alog[:, :H], od_d=g_d, od_gnorm_w=g_gnorm, final_norm_w=g_final)
    received = dict(ev_w_in=r_ev_in, ev_w_out=r_ev_out, od_w_in=r_od_in_t, od_w_out=r_od_out)
    return loss, dx, small, received


BIG = ("ev_w_in", "ev_w_out", "od_w_in", "od_w_out")
SMALL = ("ev_norm_w", "ev_dw_w", "ev_dw_b", "ev_ln_w", "ev_ln_b", "od_norm_w", "od_conv_w", "od_conv_b",
         "od_dt_bias", "od_a_log", "od_d", "od_gnorm_w", "final_norm_w")
SMALL_SHARDED = ("ev_dw_w", "od_norm_w", "od_conv_w", "od_conv_b", "od_gnorm_w")
ORDER = ("ev_norm_w", "ev_w_in", "ev_dw_w", "ev_dw_b", "ev_ln_w", "ev_ln_b", "ev_w_out", "od_norm_w", "od_w_in",
         "od_conv_w", "od_conv_b", "od_dt_bias", "od_a_log", "od_d", "od_gnorm_w", "od_w_out", "final_norm_w")


def _pack_rows(arrs, width, row_align):
    parts, spans, r0 = [], [], 0
    for a in arrs:
        flat = a.reshape(-1)
        rows = -(-flat.shape[0] // (width * row_align)) * row_align
        parts.append(jnp.pad(flat, (0, rows * width - flat.shape[0])).reshape(rows, width))
        spans.append((r0, a.size, a.shape))
        r0 += rows
    return jnp.concatenate(parts, axis=0), spans


def _unpack_rows(packed, spans):
    lead = packed.shape[:-2]
    width = packed.shape[-1]
    out = []
    for r0, size, shape in spans:
        rows = -(-size // width)
        blk = packed[..., r0:r0 + rows, :].reshape(lead + (rows * width,))[..., :size]
        out.append(blk.reshape(lead + tuple(shape)))
    return out


def _col_shards(a):
    R, C8 = a.shape
    return a.reshape(R, N_DEV, C8 // N_DEV).transpose(1, 0, 2)


def _col_unshards(a):
    n, R, C = a.shape
    return a.transpose(1, 0, 2).reshape(R, n * C)


def kernel(x, ev_norm_w, ev_w_in, ev_dw_w, ev_dw_b, ev_ln_w, ev_ln_b, ev_w_out, od_norm_w, od_w_in, od_conv_w, od_conv_b, od_dt_bias, od_a_log, od_d, od_gnorm_w, od_w_out, final_norm_w, loss_target, m_ev_norm_w, m_ev_w_in, m_ev_dw_w, m_ev_dw_b, m_ev_ln_w, m_ev_ln_b, m_ev_w_out, m_od_norm_w, m_od_w_in, m_od_conv_w, m_od_conv_b, m_od_dt_bias, m_od_a_log, m_od_d, m_od_gnorm_w, m_od_w_out, m_final_norm_w, v_ev_norm_w, v_ev_w_in, v_ev_dw_w, v_ev_dw_b, v_ev_ln_w, v_ev_ln_b, v_ev_w_out, v_od_norm_w, v_od_w_in, v_od_conv_w, v_od_conv_b, v_od_dt_bias, v_od_a_log, v_od_d, v_od_gnorm_w, v_od_w_out, v_final_norm_w):
    loc = dict(ev_norm_w=ev_norm_w, ev_w_in=ev_w_in, ev_dw_w=ev_dw_w, ev_dw_b=ev_dw_b, ev_ln_w=ev_ln_w,
               ev_ln_b=ev_ln_b, ev_w_out=ev_w_out, od_norm_w=od_norm_w, od_w_in=od_w_in, od_conv_w=od_conv_w,
               od_conv_b=od_conv_b, od_dt_bias=od_dt_bias, od_a_log=od_a_log, od_d=od_d, od_gnorm_w=od_gnorm_w,
               od_w_out=od_w_out, final_norm_w=final_norm_w)
    mom = dict(ev_norm_w=m_ev_norm_w, ev_w_in=m_ev_w_in, ev_dw_w=m_ev_dw_w, ev_dw_b=m_ev_dw_b, ev_ln_w=m_ev_ln_w,
               ev_ln_b=m_ev_ln_b, ev_w_out=m_ev_w_out, od_norm_w=m_od_norm_w, od_w_in=m_od_w_in,
               od_conv_w=m_od_conv_w, od_conv_b=m_od_conv_b, od_dt_bias=m_od_dt_bias, od_a_log=m_od_a_log,
               od_d=m_od_d, od_gnorm_w=m_od_gnorm_w, od_w_out=m_od_w_out, final_norm_w=m_final_norm_w)
    var = dict(ev_norm_w=v_ev_norm_w, ev_w_in=v_ev_w_in, ev_dw_w=v_ev_dw_w, ev_dw_b=v_ev_dw_b, ev_ln_w=v_ev_ln_w,
               ev_ln_b=v_ev_ln_b, ev_w_out=v_ev_w_out, od_norm_w=v_od_norm_w, od_w_in=v_od_w_in,
               od_conv_w=v_od_conv_w, od_conv_b=v_od_conv_b, od_dt_bias=v_od_dt_bias, od_a_log=v_od_a_log,
               od_d=v_od_d, od_gnorm_w=v_od_gnorm_w, od_w_out=v_od_w_out, final_norm_w=v_final_norm_w)
    shapes = {n: loc[n].shape for n in ORDER}
    loc = {n: (a.reshape(1, -1) if a.ndim == 1 else a.reshape(a.shape[-2:]) if a.ndim == 3 else a)
           for n, a in loc.items()}
    mom = {n: a.reshape(loc[n].shape) for n, a in mom.items()}
    var = {n: a.reshape(loc[n].shape) for n, a in var.items()}

    B, seq, D = x.shape
    me = 4 * lax.axis_index("x") + 2 * lax.axis_index("y") + lax.axis_index("c")

    w1_rows = loc["od_w_in"].shape[1]
    w1_pad = (-w1_rows) % SHARD_ROW_ALIGN

    def to_t(a):
        return jnp.pad(a.T, ((0, w1_pad), (0, 0)))

    loss, dx, grads, received = local_step(x.reshape(B * seq, D), loss_target.reshape(B * seq, D), loc,
                                           to_t(loc["od_w_in"]), B=B, seq=seq)

    gsmall_packed, gsmall_spans = _pack_rows([grads[n] for n in SMALL] + [loss], LANE, 8)
    (gsmall_recv,) = exchange([gsmall_packed], ["gather_direct"], name="gather_small_grads")

    big_out = [{} for _ in range(4)]
    for n in ("ev_w_in", "ev_w_out", "od_w_out"):
        for kind, a in enumerate(adamw(received[n], loc[n], mom[n], var[n], name="adamw_" + n)):
            big_out[kind][n] = a
    for kind, a in enumerate(adamw(received["od_w_in"], to_t(loc["od_w_in"]), to_t(mom["od_w_in"]),
                                   to_t(var["od_w_in"]), name="adamw_od_w_in")):
        big_out[kind]["od_w_in"] = a[:w1_rows].T

    summed = _unpack_rows(sum_slots(gsmall_recv, name="sum_small_grads"), gsmall_spans)
    loss_total = summed[-1][0, 0]
    gsmall = dict(zip(SMALL, summed[:-1]))
    for n in SMALL_SHARDED:
        width = loc[n].shape[1]
        gsmall[n] = lax.dynamic_slice_in_dim(gsmall[n], me * width, width, axis=1)
    gs, sspans = _pack_rows([gsmall[n] for n in SMALL], LANE, 8)
    ws, _ = _pack_rows([loc[n] for n in SMALL], LANE, 8)
    ms, _ = _pack_rows([mom[n] for n in SMALL], LANE, 8)
    vs, _ = _pack_rows([var[n] for n in SMALL], LANE, 8)
    small_out = [dict(zip(SMALL, _unpack_rows(a, sspans))) for a in adamw(gs[None], ws, ms, vs, name="adamw_small")]

    outs = [loss_total, dx.reshape(B, seq, D)]
    for kind in range(4):
        for n in ORDER:
            src = big_out[kind] if n in BIG else small_out[kind]
            outs.append(src[n].reshape(shapes[n]))
    return tuple(outs)
```
